```python
import jax, jax.numpy as jnp
from jax import lax
import numpy as np

D_MODEL = 1024
BATCH = 16
SEQ = 2048
DEPTH = 1

D_RNN = D_MODEL
RNN_BLOCKS = 16
RNN_BLOCK_W = D_RNN // RNN_BLOCKS
CONV_W = 4
LRU_C = 8.0
HEAD_DIM = 64
N_Q_HEADS = D_MODEL // HEAD_DIM
N_KV_HEADS = 4
GQA_GROUP = N_Q_HEADS // N_KV_HEADS
WINDOW = 128
ATTN_BLOCK = WINDOW
ROPE_THETA = 10000.0
Q_W = N_Q_HEADS * HEAD_DIM
KV_W = N_KV_HEADS * HEAD_DIM
D_FF = 4 * D_MODEL
PLE_DIM = 256
NORM_EPS = 1e-6
IN_WIDTHS = [D_RNN, D_RNN, Q_W, KV_W, KV_W, D_MODEL, D_MODEL]
IN_TOTAL = int(sum(IN_WIDTHS))
SPLIT_IDX = [int(v) for v in np.cumsum(IN_WIDTHS)[:-1]]

kernel_name = 'hybrid_rglru_swa_sink_gated_block'


def _rmsnorm(t, g):
    tf = t.astype(jnp.float32)
    y = tf * lax.rsqrt(jnp.mean(tf * tf, axis=-1, keepdims=True) + NORM_EPS)
    return (y * g.astype(jnp.float32)).astype(t.dtype)


def _rope_tables(S):
    inv = ROPE_THETA ** (-jnp.arange(0, HEAD_DIM, 2, dtype=jnp.float32) / HEAD_DIM)
    ang = jnp.arange(S, dtype=jnp.float32)[:, None] * inv[None, :]
    return jnp.cos(ang), jnp.sin(ang)


def _rope(t, cos, sin):
    tf = t.astype(jnp.float32)
    t1, t2 = jnp.split(tf, 2, axis=-1)
    c = cos[None, :, None, :]
    s = sin[None, :, None, :]
    return jnp.concatenate([t1 * c - t2 * s, t2 * c + t1 * s], axis=-1).astype(t.dtype)


def _causal_conv(t, w, b):
    S = t.shape[1]
    tp = jnp.pad(t, ((0, 0), (CONV_W - 1, 0), (0, 0)))
    out = b + tp[:, 0:S] * w[0]
    for j in range(1, CONV_W):
        out = out + tp[:, j:j + S] * w[j]
    return out


def _rg_lru(xc, w_rg, b_rg, w_ig, b_ig, lam):
    B, S, _ = xc.shape
    xb = xc.reshape(B, S, RNN_BLOCKS, RNN_BLOCK_W)
    r = jax.nn.sigmoid(jnp.einsum('bshi,hij->bshj', xb, w_rg).reshape(B, S, D_RNN) + b_rg)
    i = jax.nn.sigmoid(jnp.einsum('bshi,hij->bshj', xb, w_ig).reshape(B, S, D_RNN) + b_ig)
    log_a = -LRU_C * r.astype(jnp.float32) * jax.nn.softplus(-lam.astype(jnp.float32))
    a = jnp.exp(log_a)
    mult = jnp.sqrt(-jnp.expm1(2.0 * log_a))
    bterm = mult * (i * xc).astype(jnp.float32)

    def combine(left, right):
        a1, b1 = left
        a2, b2 = right
        return a1 * a2, a2 * b1 + b2

    _, h = lax.associative_scan(combine, (a, bterm), axis=1)
    return h.astype(xc.dtype)


def _sliding_window_attention(q, k, v, q_gain, k_gain, sinks, cos, sin):
    B, S, _ = q.shape
    NB = S // ATTN_BLOCK
    q = _rope(_rmsnorm(q.reshape(B, S, N_Q_HEADS, HEAD_DIM), q_gain), cos, sin)
    k = _rope(_rmsnorm(k.reshape(B, S, N_KV_HEADS, HEAD_DIM), k_gain), cos, sin)
    v = v.reshape(B, S, N_KV_HEADS, HEAD_DIM)
    qb = q.reshape(B, NB, ATTN_BLOCK, N_KV_HEADS, GQA_GROUP, HEAD_DIM)

    def band(t):
        tb = t.reshape(B, NB, ATTN_BLOCK, N_KV_HEADS, HEAD_DIM)
        prev = jnp.pad(tb[:, :-1], ((0, 0), (1, 0), (0, 0), (0, 0), (0, 0)))
        return jnp.concatenate([prev, tb], axis=2)

    kb = band(k)
    vb = band(v)
    s = jnp.einsum('bnqkgd,bnckd->bnkgqc', qb, kb).astype(jnp.float32) * (HEAD_DIM ** -0.5)
    qi = jnp.arange(ATTN_BLOCK)[:, None]
    ci = jnp.arange(2 * ATTN_BLOCK)[None, :]
    diff = ATTN_BLOCK + qi - ci
    blk = jnp.arange(NB)[:, None, None]
    valid = (diff >= 0) & (diff < WINDOW) & ((blk - 1) * ATTN_BLOCK + ci >= 0)
    s = jnp.where(valid[None, :, None, None, :, :], s, -jnp.inf)
    sink = sinks.astype(jnp.float32).reshape(N_KV_HEADS, GQA_GROUP)[None, None, :, :, None, None]
    m = jnp.maximum(jnp.max(s, axis=-1, keepdims=True), sink)
    e = jnp.exp(s - m)
    probs = e / (jnp.sum(e, axis=-1, keepdims=True) + jnp.exp(sink - m))
    o = jnp.einsum('bnkgqc,bnckd->bnqkgd', probs.astype(v.dtype), vb)
    return o.reshape(B, S, Q_W)


def _fwd_setup_inputs(seed: int = 0) -> dict:
    key = jax.random.key(seed)
    ks = jax.random.split(key, 24)
    f32 = jnp.float32
    L = DEPTH

    def nrm(k, shape, scale):
        return jax.random.normal(k, shape, f32) * scale

    u = jax.random.uniform(ks[10], (L, D_RNN), f32, minval=0.9, maxval=0.999)
    s_a = u ** (1.0 / LRU_C)
    lru_lambda = jnp.log(s_a) - jnp.log1p(-s_a)
    return {
        'x': nrm(ks[0], (BATCH, SEQ, D_MODEL), 1.0),
        'p': nrm(ks[1], (DEPTH, BATCH, SEQ, PLE_DIM), 1.0),
        'g_mix': 1.0 + nrm(ks[2], (L, D_MODEL), 0.02),
        'w_in': nrm(ks[3], (L, D_MODEL, IN_TOTAL), D_MODEL ** -0.5),
        'conv_w': nrm(ks[4], (L, CONV_W, D_RNN), CONV_W ** -0.5),
        'conv_b': nrm(ks[5], (L, D_RNN), 0.01),
        'w_rg': nrm(ks[6], (L, RNN_BLOCKS, RNN_BLOCK_W, RNN_BLOCK_W), RNN_BLOCK_W ** -0.5),
        'b_rg': nrm(ks[7], (L, D_RNN), 0.01),
        'w_ig': nrm(ks[8], (L, RNN_BLOCKS, RNN_BLOCK_W, RNN_BLOCK_W), RNN_BLOCK_W ** -0.5),
        'b_ig': nrm(ks[9], (L, D_RNN), 0.01),
        'lru_lambda': lru_lambda,
        'w_rnn_proj': nrm(ks[11], (L, D_RNN, D_MODEL), D_RNN ** -0.5),
        'q_gain': 1.0 + nrm(ks[12], (L, HEAD_DIM), 0.02),
        'k_gain': 1.0 + nrm(ks[13], (L, HEAD_DIM), 0.02),
        'sinks': nrm(ks[14], (L, N_Q_HEADS), 0.5),
        'w_attn_proj': nrm(ks[15], (L, Q_W, D_MODEL), Q_W ** -0.5),
        'w_out': nrm(ks[16], (L, D_MODEL, D_MODEL), D_MODEL ** -0.5),
        'g_mlp': 1.0 + nrm(ks[17], (L, D_MODEL), 0.02),
        'w_up': nrm(ks[18], (L, D_MODEL, D_FF), D_MODEL ** -0.5),
        'w_down': nrm(ks[19], (L, D_FF, D_MODEL), D_FF ** -0.5),
        'g_ple': 1.0 + nrm(ks[20], (L, D_MODEL), 0.02),
        'w_ple_gate': nrm(ks[21], (L, D_MODEL, D_MODEL), D_MODEL ** -0.5),
        'w_ple_proj': nrm(ks[22], (L, PLE_DIM, D_MODEL), PLE_DIM ** -0.5),
    }


def _fwd_reference(x, p, g_mix, w_in, conv_w, conv_b, w_rg, b_rg, w_ig, b_ig, lru_lambda,
              w_rnn_proj, q_gain, k_gain, sinks, w_attn_proj, w_out, g_mlp, w_up, w_down,
              g_ple, w_ple_gate, w_ple_proj):
    S = x.shape[1]
    cos, sin = _rope_tables(S)
    for l in range(DEPTH):
        h = _rmsnorm(x, g_mix[l])
        z = h @ w_in[l]
        x_rnn, g_rnn, q, k, v, gate_a, gate_b = jnp.split(z, SPLIT_IDX, axis=-1)
        xc = _causal_conv(x_rnn, conv_w[l], conv_b[l])
        hr = _rg_lru(xc, w_rg[l], b_rg[l], w_ig[l], b_ig[l], lru_lambda[l])
        y_a = (hr * jax.nn.gelu(g_rnn)) @ w_rnn_proj[l]
        y_b = _sliding_window_attention(q, k, v, q_gain[l], k_gain[l], sinks[l], cos, sin) @ w_attn_proj[l]
        merged = jax.nn.sigmoid(gate_a) * y_a + jax.nn.sigmoid(gate_b) * y_b
        x = x + merged @ w_out[l]
        hm = _rmsnorm(x, g_mlp[l])
        x = x + jnp.square(jax.nn.relu(hm @ w_up[l])) @ w_down[l]
        e = p[l] @ w_ple_proj[l]
        x = x + e * jax.nn.sigmoid(_rmsnorm(x, g_ple[l]) @ w_ple_gate[l])
    return x


import jax as _jax
import jax.numpy as _jnp

TWIN_FORMAT = 'train_step'
FWD_PARAMS = ['x', 'p', 'g_mix', 'w_in', 'conv_w', 'conv_b', 'w_rg', 'b_rg', 'w_ig', 'b_ig', 'lru_lambda', 'w_rnn_proj', 'q_gain', 'k_gain', 'sinks', 'w_attn_proj', 'w_out', 'g_mlp', 'w_up', 'w_down', 'g_ple', 'w_ple_gate', 'w_ple_proj']
TWIN_WEIGHTS = ['g_mix', 'w_in', 'conv_w', 'conv_b', 'w_rg', 'b_rg', 'w_ig', 'b_ig', 'lru_lambda', 'w_rnn_proj', 'q_gain', 'k_gain', 'sinks', 'w_attn_proj', 'w_out', 'g_mlp', 'w_up', 'w_down', 'g_ple', 'w_ple_gate', 'w_ple_proj']
TWIN_DIFF_INPUT = 'x'
TWIN_INPUTS = ['x', 'p', 'g_mix', 'w_in', 'conv_w', 'conv_b', 'w_rg', 'b_rg', 'w_ig', 'b_ig', 'lru_lambda', 'w_rnn_proj', 'q_gain', 'k_gain', 'sinks', 'w_attn_proj', 'w_out', 'g_mlp', 'w_up', 'w_down', 'g_ple', 'w_ple_gate', 'w_ple_proj', 'loss_target', 'm_g_mix', 'm_w_in', 'm_conv_w', 'm_conv_b', 'm_w_rg', 'm_b_rg', 'm_w_ig', 'm_b_ig', 'm_lru_lambda', 'm_w_rnn_proj', 'm_q_gain', 'm_k_gain', 'm_sinks', 'm_w_attn_proj', 'm_w_out', 'm_g_mlp', 'm_w_up', 'm_w_down', 'm_g_ple', 'm_w_ple_gate', 'm_w_ple_proj', 'v_g_mix', 'v_w_in', 'v_conv_w', 'v_conv_b', 'v_w_rg', 'v_b_rg', 'v_w_ig', 'v_b_ig', 'v_lru_lambda', 'v_w_rnn_proj', 'v_q_gain', 'v_k_gain', 'v_sinks', 'v_w_attn_proj', 'v_w_out', 'v_g_mlp', 'v_w_up', 'v_w_down', 'v_g_ple', 'v_w_ple_gate', 'v_w_ple_proj']
TWIN_OUTPUTS = ['loss', 'grad_x', 'grad_g_mix', 'grad_w_in', 'grad_conv_w', 'grad_conv_b', 'grad_w_rg', 'grad_b_rg', 'grad_w_ig', 'grad_b_ig', 'grad_lru_lambda', 'grad_w_rnn_proj', 'grad_q_gain', 'grad_k_gain', 'grad_sinks', 'grad_w_attn_proj', 'grad_w_out', 'grad_g_mlp', 'grad_w_up', 'grad_w_down', 'grad_g_ple', 'grad_w_ple_gate', 'grad_w_ple_proj', 'delta_g_mix', 'delta_w_in', 'delta_conv_w', 'delta_conv_b', 'delta_w_rg', 'delta_b_rg', 'delta_w_ig', 'delta_b_ig', 'delta_lru_lambda', 'delta_w_rnn_proj', 'delta_q_gain', 'delta_k_gain', 'delta_sinks', 'delta_w_attn_proj', 'delta_w_out', 'delta_g_mlp', 'delta_w_up', 'delta_w_down', 'delta_g_ple', 'delta_w_ple_gate', 'delta_w_ple_proj', 'new_m_g_mix', 'new_m_w_in', 'new_m_conv_w', 'new_m_conv_b', 'new_m_w_rg', 'new_m_b_rg', 'new_m_w_ig', 'new_m_b_ig', 'new_m_lru_lambda', 'new_m_w_rnn_proj', 'new_m_q_gain', 'new_m_k_gain', 'new_m_sinks', 'new_m_w_attn_proj', 'new_m_w_out', 'new_m_g_mlp', 'new_m_w_up', 'new_m_w_down', 'new_m_g_ple', 'new_m_w_ple_gate', 'new_m_w_ple_proj', 'new_v_g_mix', 'new_v_w_in', 'new_v_conv_w', 'new_v_conv_b', 'new_v_w_rg', 'new_v_b_rg', 'new_v_w_ig', 'new_v_b_ig', 'new_v_lru_lambda', 'new_v_w_rnn_proj', 'new_v_q_gain', 'new_v_k_gain', 'new_v_sinks', 'new_v_w_attn_proj', 'new_v_w_out', 'new_v_g_mlp', 'new_v_w_up', 'new_v_w_down', 'new_v_g_ple', 'new_v_w_ple_gate', 'new_v_w_ple_proj']
TWIN_LEAF_KINDS = {'loss': 'loss', 'grad_x': 'grad_x', 'grad_g_mix': 'grad_w', 'grad_w_in': 'grad_w', 'grad_conv_w': 'grad_w', 'grad_conv_b': 'grad_w', 'grad_w_rg': 'grad_w', 'grad_b_rg': 'grad_w', 'grad_w_ig': 'grad_w', 'grad_b_ig': 'grad_w', 'grad_lru_lambda': 'grad_w', 'grad_w_rnn_proj': 'grad_w', 'grad_q_gain': 'grad_w', 'grad_k_gain': 'grad_w', 'grad_sinks': 'grad_w', 'grad_w_attn_proj': 'grad_w', 'grad_w_out': 'grad_w', 'grad_g_mlp': 'grad_w', 'grad_w_up': 'grad_w', 'grad_w_down': 'grad_w', 'grad_g_ple': 'grad_w', 'grad_w_ple_gate': 'grad_w', 'grad_w_ple_proj': 'grad_w', 'delta_g_mix': 'delta_w', 'delta_w_in': 'delta_w', 'delta_conv_w': 'delta_w', 'delta_conv_b': 'delta_w', 'delta_w_rg': 'delta_w', 'delta_b_rg': 'delta_w', 'delta_w_ig': 'delta_w', 'delta_b_ig': 'delta_w', 'delta_lru_lambda': 'delta_w', 'delta_w_rnn_proj': 'delta_w', 'delta_q_gain': 'delta_w', 'delta_k_gain': 'delta_w', 'delta_sinks': 'delta_w', 'delta_w_attn_proj': 'delta_w', 'delta_w_out': 'delta_w', 'delta_g_mlp': 'delta_w', 'delta_w_up': 'delta_w', 'delta_w_down': 'delta_w', 'delta_g_ple': 'delta_w', 'delta_w_ple_gate': 'delta_w', 'delta_w_ple_proj': 'delta_w', 'new_m_g_mix': 'new_m', 'new_m_w_in': 'new_m', 'new_m_conv_w': 'new_m', 'new_m_conv_b': 'new_m', 'new_m_w_rg': 'new_m', 'new_m_b_rg': 'new_m', 'new_m_w_ig': 'new_m', 'new_m_b_ig': 'new_m', 'new_m_lru_lambda': 'new_m', 'new_m_w_rnn_proj': 'new_m', 'new_m_q_gain': 'new_m', 'new_m_k_gain': 'new_m', 'new_m_sinks': 'new_m', 'new_m_w_attn_proj': 'new_m', 'new_m_w_out': 'new_m', 'new_m_g_mlp': 'new_m', 'new_m_w_up': 'new_m', 'new_m_w_down': 'new_m', 'new_m_g_ple': 'new_m', 'new_m_w_ple_gate': 'new_m', 'new_m_w_ple_proj': 'new_m', 'new_v_g_mix': 'new_v', 'new_v_w_in': 'new_v', 'new_v_conv_w': 'new_v', 'new_v_conv_b': 'new_v', 'new_v_w_rg': 'new_v', 'new_v_b_rg': 'new_v', 'new_v_w_ig': 'new_v', 'new_v_b_ig': 'new_v', 'new_v_lru_lambda': 'new_v', 'new_v_w_rnn_proj': 'new_v', 'new_v_q_gain': 'new_v', 'new_v_k_gain': 'new_v', 'new_v_sinks': 'new_v', 'new_v_w_attn_proj': 'new_v', 'new_v_w_out': 'new_v', 'new_v_g_mlp': 'new_v', 'new_v_w_up': 'new_v', 'new_v_w_down': 'new_v', 'new_v_g_ple': 'new_v', 'new_v_w_ple_gate': 'new_v', 'new_v_w_ple_proj': 'new_v'}


def _forward(args):
    return _fwd_reference(*[args[k] for k in FWD_PARAMS])


def _output_shape():
    out = _jax.eval_shape(lambda: _forward(_fwd_setup_inputs(0)))
    return out.shape, out.dtype

N_MICROBATCH = 1
ADAM_LR = 0.001
ADAM_B1 = 0.9
ADAM_B2 = 0.999
ADAM_EPS = 1e-08
ADAM_WD = 0.01
ADAM_STEP = 10
PER_EXAMPLE_BATCH_AXIS = {'x': 0, 'p': 1, 'loss_target': 0}
SHARED_INPUTS = []
_WEIGHT_DTYPES = {'g_mix': _jnp.float32, 'w_in': _jnp.float32, 'conv_w': _jnp.float32, 'conv_b': _jnp.float32, 'w_rg': _jnp.float32, 'b_rg': _jnp.float32, 'w_ig': _jnp.float32, 'b_ig': _jnp.float32, 'lru_lambda': _jnp.float32, 'w_rnn_proj': _jnp.float32, 'q_gain': _jnp.float32, 'k_gain': _jnp.float32, 'sinks': _jnp.float32, 'w_attn_proj': _jnp.float32, 'w_out': _jnp.float32, 'g_mlp': _jnp.float32, 'w_up': _jnp.float32, 'w_down': _jnp.float32, 'g_ple': _jnp.float32, 'w_ple_gate': _jnp.float32, 'w_ple_proj': _jnp.float32}
MOMENT_SCALE = {'g_mix': 4.406309e+00, 'w_in': 2.858177e-01, 'conv_w': 2.003677e+00, 'conv_b': 1.880063e+01, 'w_rg': 7.705667e-01, 'b_rg': 4.732135e-01, 'w_ig': 1.415270e+00, 'b_ig': 8.643682e-01, 'lru_lambda': 7.585694e-01, 'w_rnn_proj': 1.634780e+00, 'q_gain': 1.653931e+00, 'k_gain': 1.605075e+00, 'sinks': 2.451534e-01, 'w_attn_proj': 1.467824e-01, 'w_out': 1.841962e+00, 'g_mlp': 9.611288e+01, 'w_up': 9.299996e-01, 'w_down': 8.160516e+00, 'g_ple': 1.424420e+00, 'w_ple_gate': 9.934459e-01, 'w_ple_proj': 4.217248e-01}


def _to_microbatches(a, axis):
    t = _jnp.moveaxis(a, axis, 0)
    t = t.reshape((N_MICROBATCH, t.shape[0] // N_MICROBATCH) + t.shape[1:])
    return _jnp.moveaxis(t, 1, axis + 1)


def setup_inputs(seed: int = 0) -> dict:
    inp = _fwd_setup_inputs(seed)
    key = _jax.random.fold_in(_jax.random.key(seed), 7919)
    shape, _ = _output_shape()
    out = dict(inp)
    out["loss_target"] = _jax.random.normal(_jax.random.fold_in(key, 0), shape, _jnp.float32)
    for i, name in enumerate(TWIN_WEIGHTS):
        w = inp[name].astype(_jnp.float32)
        if MOMENT_SCALE is None:
            s = _jnp.sqrt(_jnp.mean(_jnp.square(w)) + 1e-30)
        else:
            s = MOMENT_SCALE[name]
        km, kv = _jax.random.split(_jax.random.fold_in(key, i + 1))
        out[name] = w
        out["m_" + name] = s * _jax.random.normal(km, w.shape, _jnp.float32)
        out["v_" + name] = (s * s) * _jax.random.uniform(kv, w.shape, _jnp.float32, 0.5, 1.5)
    if N_MICROBATCH > 1:
        for name, axis in PER_EXAMPLE_BATCH_AXIS.items():
            out[name] = _to_microbatches(out[name], axis)
    return {'x': out['x'], 'p': out['p'], 'g_mix': out['g_mix'], 'w_in': out['w_in'], 'conv_w': out['conv_w'], 'conv_b': out['conv_b'], 'w_rg': out['w_rg'], 'b_rg': out['b_rg'], 'w_ig': out['w_ig'], 'b_ig': out['b_ig'], 'lru_lambda': out['lru_lambda'], 'w_rnn_proj': out['w_rnn_proj'], 'q_gain': out['q_gain'], 'k_gain': out['k_gain'], 'sinks': out['sinks'], 'w_attn_proj': out['w_attn_proj'], 'w_out': out['w_out'], 'g_mlp': out['g_mlp'], 'w_up': out['w_up'], 'w_down': out['w_down'], 'g_ple': out['g_ple'], 'w_ple_gate': out['w_ple_gate'], 'w_ple_proj': out['w_ple_proj'], 'loss_target': out['loss_target'], 'm_g_mix': out['m_g_mix'], 'm_w_in': out['m_w_in'], 'm_conv_w': out['m_conv_w'], 'm_conv_b': out['m_conv_b'], 'm_w_rg': out['m_w_rg'], 'm_b_rg': out['m_b_rg'], 'm_w_ig': out['m_w_ig'], 'm_b_ig': out['m_b_ig'], 'm_lru_lambda': out['m_lru_lambda'], 'm_w_rnn_proj': out['m_w_rnn_proj'], 'm_q_gain': out['m_q_gain'], 'm_k_gain': out['m_k_gain'], 'm_sinks': out['m_sinks'], 'm_w_attn_proj': out['m_w_attn_proj'], 'm_w_out': out['m_w_out'], 'm_g_mlp': out['m_g_mlp'], 'm_w_up': out['m_w_up'], 'm_w_down': out['m_w_down'], 'm_g_ple': out['m_g_ple'], 'm_w_ple_gate': out['m_w_ple_gate'], 'm_w_ple_proj': out['m_w_ple_proj'], 'v_g_mix': out['v_g_mix'], 'v_w_in': out['v_w_in'], 'v_conv_w': out['v_conv_w'], 'v_conv_b': out['v_conv_b'], 'v_w_rg': out['v_w_rg'], 'v_b_rg': out['v_b_rg'], 'v_w_ig': out['v_w_ig'], 'v_b_ig': out['v_b_ig'], 'v_lru_lambda': out['v_lru_lambda'], 'v_w_rnn_proj': out['v_w_rnn_proj'], 'v_q_gain': out['v_q_gain'], 'v_k_gain': out['v_k_gain'], 'v_sinks': out['v_sinks'], 'v_w_attn_proj': out['v_w_attn_proj'], 'v_w_out': out['v_w_out'], 'v_g_mlp': out['v_g_mlp'], 'v_w_up': out['v_w_up'], 'v_w_down': out['v_w_down'], 'v_g_ple': out['v_g_ple'], 'v_w_ple_gate': out['v_w_ple_gate'], 'v_w_ple_proj': out['v_w_ple_proj']}


def _loss(weights, diff, rest, loss_target):
    with _jax.named_scope("forward"):
        args = {**rest, TWIN_DIFF_INPUT: diff, **{k: w.astype(_WEIGHT_DTYPES[k]) for k, w in weights.items()}}
        y = _forward(args)
    with _jax.named_scope("loss_head"):
        err = _jnp.square(y.astype(_jnp.float32) - loss_target)
        return 0.5 * _jnp.sum(_jnp.mean(err, axis=-1)) if err.ndim else 0.5 * err


def _adamw(w, g, m, v):
    m = ADAM_B1 * m + (1.0 - ADAM_B1) * g
    v = ADAM_B2 * v + (1.0 - ADAM_B2) * _jnp.square(g)
    m_hat = m / (1.0 - ADAM_B1 ** ADAM_STEP)
    v_hat = v / (1.0 - ADAM_B2 ** ADAM_STEP)
    delta = -ADAM_LR * (m_hat / (_jnp.sqrt(v_hat) + ADAM_EPS) + ADAM_WD * w)
    return delta, m, v


def reference(x, p, g_mix, w_in, conv_w, conv_b, w_rg, b_rg, w_ig, b_ig, lru_lambda, w_rnn_proj, q_gain, k_gain, sinks, w_attn_proj, w_out, g_mlp, w_up, w_down, g_ple, w_ple_gate, w_ple_proj, loss_target, m_g_mix, m_w_in, m_conv_w, m_conv_b, m_w_rg, m_b_rg, m_w_ig, m_b_ig, m_lru_lambda, m_w_rnn_proj, m_q_gain, m_k_gain, m_sinks, m_w_attn_proj, m_w_out, m_g_mlp, m_w_up, m_w_down, m_g_ple, m_w_ple_gate, m_w_ple_proj, v_g_mix, v_w_in, v_conv_w, v_conv_b, v_w_rg, v_b_rg, v_w_ig, v_b_ig, v_lru_lambda, v_w_rnn_proj, v_q_gain, v_k_gain, v_sinks, v_w_attn_proj, v_w_out, v_g_mlp, v_w_up, v_w_down, v_g_ple, v_w_ple_gate, v_w_ple_proj):
    given = dict(x=x, p=p, g_mix=g_mix, w_in=w_in, conv_w=conv_w, conv_b=conv_b, w_rg=w_rg, b_rg=b_rg, w_ig=w_ig, b_ig=b_ig, lru_lambda=lru_lambda, w_rnn_proj=w_rnn_proj, q_gain=q_gain, k_gain=k_gain, sinks=sinks, w_attn_proj=w_attn_proj, w_out=w_out, g_mlp=g_mlp, w_up=w_up, w_down=w_down, g_ple=g_ple, w_ple_gate=w_ple_gate, w_ple_proj=w_ple_proj, loss_target=loss_target, m_g_mix=m_g_mix, m_w_in=m_w_in, m_conv_w=m_conv_w, m_conv_b=m_conv_b, m_w_rg=m_w_rg, m_b_rg=m_b_rg, m_w_ig=m_w_ig, m_b_ig=m_b_ig, m_lru_lambda=m_lru_lambda, m_w_rnn_proj=m_w_rnn_proj, m_q_gain=m_q_gain, m_k_gain=m_k_gain, m_sinks=m_sinks, m_w_attn_proj=m_w_attn_proj, m_w_out=m_w_out, m_g_mlp=m_g_mlp, m_w_up=m_w_up, m_w_down=m_w_down, m_g_ple=m_g_ple, m_w_ple_gate=m_w_ple_gate, m_w_ple_proj=m_w_ple_proj, v_g_mix=v_g_mix, v_w_in=v_w_in, v_conv_w=v_conv_w, v_conv_b=v_conv_b, v_w_rg=v_w_rg, v_b_rg=v_b_rg, v_w_ig=v_w_ig, v_b_ig=v_b_ig, v_lru_lambda=v_lru_lambda, v_w_rnn_proj=v_w_rnn_proj, v_q_gain=v_q_gain, v_k_gain=v_k_gain, v_sinks=v_sinks, v_w_attn_proj=v_w_attn_proj, v_w_out=v_w_out, v_g_mlp=v_g_mlp, v_w_up=v_w_up, v_w_down=v_w_down, v_g_ple=v_g_ple, v_w_ple_gate=v_w_ple_gate, v_w_ple_proj=v_w_ple_proj)
    weights = {n: given[n] for n in TWIN_WEIGHTS}
    shared = {n: given[n] for n in SHARED_INPUTS}
    per_example = {n: given[n] for n in ['x', 'p']}
    grad_fn = _jax.value_and_grad(_loss, argnums=(0, 1))

    def one_microbatch(ex, loss_target):
        ex = dict(ex)
        diff = ex.pop(TWIN_DIFF_INPUT)
        return grad_fn(weights, diff, {**shared, **ex}, loss_target)

    if N_MICROBATCH == 1:
        loss, (grad_w, grad_x) = one_microbatch(per_example, given["loss_target"])
    else:
        def body(carry, xs):
            loss_sum, grad_sum = carry
            l_k, (gw_k, gx_k) = one_microbatch(xs[0], xs[1])
            with _jax.named_scope("update"):
                return (loss_sum + l_k, _jax.tree.map(_jnp.add, grad_sum, gw_k)), gx_k

        init = (_jnp.zeros((), _jnp.float32), _jax.tree.map(_jnp.zeros_like, weights))
        (loss, grad_w), grad_x = _jax.lax.scan(body, init, (per_example, given["loss_target"]))
    with _jax.named_scope("update"):
        delta_w, new_m, new_v = {}, {}, {}
        for n in TWIN_WEIGHTS:
            delta_w[n], new_m[n], new_v[n] = _adamw(weights[n], grad_w[n], given["m_" + n], given["v_" + n])
    return (loss, grad_x, *[grad_w[n] for n in TWIN_WEIGHTS], *[delta_w[n] for n in TWIN_WEIGHTS],
            *[new_m[n] for n in TWIN_WEIGHTS], *[new_v[n] for n in TWIN_WEIGHTS])
```

```python
import functools

import jax
import jax.numpy as jnp
from jax import lax
from jax.experimental import pallas as pl
from jax.experimental.pallas import tpu as pltpu

F32 = jnp.float32
BF16 = jnp.bfloat16

N_DEV = 8
D_MODEL = 1024
RNN_BLOCK_W = 64
CONV_W = 4
LRU_C = 8.0
HEAD_DIM = 64
N_Q_HEADS = 16
N_KV_HEADS = 4
KV_W = N_KV_HEADS * HEAD_DIM
WINDOW = 128
ROPE_THETA = 10000.0
D_FF = 4096
PLE_DIM = 256
NORM_EPS = 1e-6
IN_TOTAL = 5632
COL_XRNN, COL_GRNN, COL_Q, COL_K, COL_V, COL_GA, COL_GB = 0, 1024, 2048, 3072, 3328, 3584, 4608

ADAM_LR = 0.001
ADAM_B1 = 0.9
ADAM_B2 = 0.999
ADAM_EPS = 1e-08
ADAM_WD = 0.01
ADAM_STEP = 10

LANES = 128
RNN_TILE = 256
VMEM_LIMIT = 48 * 1024 * 1024
NEG_BIG = -1e30


def _params(*sem):
    return pltpu.CompilerParams(dimension_semantics=sem if sem else None, vmem_limit_bytes=VMEM_LIMIT)


def _sig(x):
    return 1.0 / (1.0 + jnp.exp(-x))


def _dot_nt(a, b):
    return lax.dot_general(a, b, (((1,), (1,)), ((), ())), preferred_element_type=F32)


def _dot_tn(a, b):
    return lax.dot_general(a, b, (((0,), (0,)), ((), ())), preferred_element_type=F32)


def _matmul(a, b, *, mode, tm, tn, out_dtypes, name, epilogue=None, extras=()):
    m, k = a.shape
    n = b.shape[1] if mode == "nn" else b.shape[0]
    tm, tn = min(tm, m), min(tn, n)
    n_extra = len(extras)

    def body(a_ref, b_ref, *rest):
        extra_refs, out_refs = rest[:n_extra], rest[n_extra:]
        if mode == "nn":
            acc = jnp.dot(a_ref[...], b_ref[...], preferred_element_type=F32)
        else:
            acc = _dot_nt(a_ref[...], b_ref[...])
        res = epilogue(acc, *[e[...] for e in extra_refs]) if epilogue is not None else (acc,)
        for o_ref, r in zip(out_refs, res):
            o_ref[...] = r.astype(o_ref.dtype)

    b_spec = pl.BlockSpec((k, tn), lambda i, j: (0, j)) if mode == "nn" else pl.BlockSpec((tn, k), lambda i, j: (j, 0))
    tile = pl.BlockSpec((tm, tn), lambda i, j: (i, j))
    outs = pl.pallas_call(
        body,
        grid=(m // tm, n // tn),
        in_specs=[pl.BlockSpec((tm, k), lambda i, j: (i, 0)), b_spec] + [tile] * n_extra,
        out_specs=[tile] * len(out_dtypes),
        out_shape=[jax.ShapeDtypeStruct((m, n), dt) for dt in out_dtypes],
        compiler_params=_params("parallel", "arbitrary"),
        name=name,
    )(a, b, *extras)
    return outs[0] if len(outs) == 1 else outs


def _matmul_tn(a, b, *, tk, tn, tt, name, stacked=False):
    t, k = a.shape
    n = b.shape[1]
    tk, tn, tt = min(tk, k), min(tn, n), min(tt, t)

    def body(a_ref, b_ref, o_ref):
        @pl.when(pl.program_id(2) == 0)
        def _():
            o_ref[...] = jnp.zeros_like(o_ref)

        o_ref[...] += _dot_tn(a_ref[...], b_ref[...])

    if stacked:
        assert k == tk
        out_spec = pl.BlockSpec((None, tk, tn), lambda i, j, s: (j, i, 0))
        out_shape = jax.ShapeDtypeStruct((n // tn, k, tn), F32)
    else:
        out_spec = pl.BlockSpec((tk, tn), lambda i, j, s: (i, j))
        out_shape = jax.ShapeDtypeStruct((k, n), F32)
    return pl.pallas_call(
        body,
        grid=(k // tk, n // tn, t // tt),
        in_specs=[pl.BlockSpec((tt, tk), lambda i, j, s: (s, i)), pl.BlockSpec((tt, tn), lambda i, j, s: (s, j))],
        out_specs=out_spec,
        out_shape=out_shape,
        compiler_params=_params("parallel", "parallel", "arbitrary"),
        name=name,
    )(a, b)


def _rmsnorm_fwd(x, g, *, name):
    t, d = x.shape
    tm = min(512, t)

    def body(x_ref, g_ref, o_ref):
        xv = x_ref[...]
        r = lax.rsqrt(jnp.mean(xv * xv, axis=-1, keepdims=True) + NORM_EPS)
        o_ref[...] = (xv * r * g_ref[...]).astype(BF16)

    return pl.pallas_call(
        body,
        grid=(t // tm,),
        in_specs=[pl.BlockSpec((tm, d), lambda i: (i, 0)), pl.BlockSpec((1, d), lambda i: (0, 0))],
        out_specs=pl.BlockSpec((tm, d), lambda i: (i, 0)),
        out_shape=jax.ShapeDtypeStruct((t, d), BF16),
        compiler_params=_params("parallel"),
        name=name,
    )(x, g)


def _rmsnorm_bwd(x, g, dy, dres, *, name, want_bf16):
    t, d = x.shape
    tm = min(256, t)

    def body(x_ref, g_ref, dy_ref, dres_ref, *out_refs):
        dx_ref, dg_ref = out_refs[0], out_refs[-1]
        xv, dyv = x_ref[...], dy_ref[...]
        r = lax.rsqrt(jnp.mean(xv * xv, axis=-1, keepdims=True) + NORM_EPS)
        xr = xv * r
        gy = dyv * g_ref[...]
        dx = dres_ref[...] + r * (gy - xr * jnp.mean(gy * xr, axis=-1, keepdims=True))
        dx_ref[...] = dx
        if want_bf16:
            out_refs[1][...] = dx.astype(BF16)

        @pl.when(pl.program_id(0) == 0)
        def _():
            dg_ref[...] = jnp.zeros_like(dg_ref)

        dg_ref[...] += jnp.sum(dyv * xr, axis=0, keepdims=True)

    tile = pl.BlockSpec((tm, d), lambda i: (i, 0))
    vec = pl.BlockSpec((1, d), lambda i: (0, 0))
    out_specs = [tile] + ([tile] if want_bf16 else []) + [vec]
    out_shape = [jax.ShapeDtypeStruct((t, d), F32)] + ([jax.ShapeDtypeStruct((t, d), BF16)] if want_bf16 else [])
    out_shape.append(jax.ShapeDtypeStruct((1, d), F32))
    return pl.pallas_call(
        body,
        grid=(t // tm,),
        in_specs=[tile, vec, tile, tile],
        out_specs=out_specs,
        out_shape=out_shape,
        compiler_params=_params("arbitrary"),
        name=name,
    )(x, g, dy, dres)


def _softplus_neg(lam):
    z = -lam
    return jnp.maximum(z, 0.0) + jnp.log1p(jnp.exp(-jnp.abs(z)))


def _neg_expm1(y):
    series = -y * (1.0 + y * 0.5 * (1.0 + y * (1.0 / 3.0) * (1.0 + y * 0.25 * (1.0 + y * 0.2 * (
        1.0 + y * (1.0 / 6.0) * (1.0 + y * (1.0 / 7.0)))))))
    return jnp.where(y > -0.25, series, 1.0 - jnp.exp(y))


def _gelu_parts(x):
    c = 0.7978845608028654
    u = c * (x + 0.044715 * x * x * x)
    th = jnp.tanh(u)
    gel = 0.5 * x * (1.0 + th)
    dgel = 0.5 * (1.0 + th) + 0.5 * x * (1.0 - th * th) * c * (1.0 + 3.0 * 0.044715 * x * x)
    return gel, dgel


def _shift_down(v, k, rows):
    return jnp.where(rows < k, 0.0, pltpu.roll(v, k, 0))


def _shift_up(v, k, rows, n):
    return jnp.where(rows >= n - k, 0.0, pltpu.roll(v, n - k, 0))


def _rnn_gates(xc, wrg, brg, wig, big, lam):
    xcb = xc.astype(BF16)
    r = _sig(jnp.dot(xcb, wrg, preferred_element_type=F32) + brg)
    i = _sig(jnp.dot(xcb, wig, preferred_element_type=F32) + big)
    sp = _softplus_neg(lam)
    log_a = -LRU_C * r * sp
    a = jnp.exp(log_a)
    mult = jnp.sqrt(_neg_expm1(2.0 * log_a))
    return xcb, r, i, sp, a, mult


def _conv_fwd(xv, cw, cb, rows):
    return (cb + _shift_down(xv, 3, rows) * cw[0:1, :] + _shift_down(xv, 2, rows) * cw[1:2, :]
            + _shift_down(xv, 1, rows) * cw[2:3, :] + xv * cw[3:4, :])


def _rnn_fwd(z, conv_w, conv_b, wrg_bd, b_rg, wig_bd, b_ig, lam, *, n_seq, seq):
    t = n_seq * seq
    ct = RNN_TILE
    n_ct = D_MODEL // ct

    def body(x_ref, g_ref, cw_ref, cb_ref, wrg_ref, brg_ref, wig_ref, big_ref, lam_ref,
             xc_ref, hr_ref, ya_ref, a_s, b_s):
        rows = lax.broadcasted_iota(jnp.int32, (seq, ct), 0)
        xc = _conv_fwd(x_ref[...], cw_ref[...], cb_ref[...], rows)
        _, r, i, sp, a, mult = _rnn_gates(xc, wrg_ref[...], brg_ref[...], wig_ref[...], big_ref[...], lam_ref[...])
        a_s[...] = a
        b_s[...] = mult * (i * xc)

        def step(s, h):
            h = a_s[pl.ds(s, 1), :] * h + b_s[pl.ds(s, 1), :]
            hr_ref[pl.ds(s, 1), :] = h
            return h

        lax.fori_loop(0, seq, step, jnp.zeros((1, ct), F32), unroll=8)
        gel, _ = _gelu_parts(g_ref[...])
        xc_ref[...] = xc
        ya_ref[...] = (hr_ref[...] * gel).astype(BF16)

    vec = pl.BlockSpec((1, ct), lambda b, c: (0, c))
    gate_w = pl.BlockSpec((None, ct, ct), lambda b, c: (c, 0, 0))
    tile = pl.BlockSpec((seq, ct), lambda b, c: (b, c))
    return pl.pallas_call(
        body,
        grid=(n_seq, n_ct),
        in_specs=[
            pl.BlockSpec((seq, ct), lambda b, c: (b, COL_XRNN // ct + c)),
            pl.BlockSpec((seq, ct), lambda b, c: (b, COL_GRNN // ct + c)),
            pl.BlockSpec((CONV_W, ct), lambda b, c: (0, c)), vec, gate_w, vec, gate_w, vec, vec,
        ],
        out_specs=[tile, tile, tile],
        out_shape=[jax.ShapeDtypeStruct((t, D_MODEL), F32), jax.ShapeDtypeStruct((t, D_MODEL), F32),
                   jax.ShapeDtypeStruct((t, D_MODEL), BF16)],
        scratch_shapes=[pltpu.VMEM((seq, ct), F32), pltpu.VMEM((seq, ct), F32)],
        compiler_params=_params("parallel", "parallel"),
        name="rnn_fwd",
    )(z, z, conv_w, conv_b, wrg_bd, b_rg, wig_bd, b_ig, lam)


def _rnn_bwd(dya, z, xc, hr, conv_w, wrg_bd, b_rg, wig_bd, b_ig, lam, *, n_seq, seq):
    t = n_seq * seq
    ct = RNN_TILE
    n_ct = D_MODEL // ct

    def body(dya_ref, x_ref, g_ref, xc_ref, hr_ref, cw_ref, wrg_ref, brg_ref, wig_ref, big_ref, lam_ref,
             dx_ref, dg_ref, dwrg_ref, dwig_ref, vec_ref, a_s, d_s, g_s):
        rows = lax.broadcasted_iota(jnp.int32, (seq, ct), 0)
        xv, xc, hr, dyv = x_ref[...], xc_ref[...], hr_ref[...], dya_ref[...]
        lamv = lam_ref[...]
        gel, dgel = _gelu_parts(g_ref[...])
        dg_ref[...] = (dyv * hr * dgel).astype(BF16)
        xcb, r, i, sp, a, mult = _rnn_gates(xc, wrg_ref[...], brg_ref[...], wig_ref[...], big_ref[...], lamv)
        a_s[...] = a
        d_s[...] = dyv * gel

        def step(k, c):
            s = seq - 1 - k
            gs = d_s[pl.ds(s, 1), :] + c
            g_s[pl.ds(s, 1), :] = gs
            return a_s[pl.ds(s, 1), :] * gs

        lax.fori_loop(0, seq, step, jnp.zeros((1, ct), F32), unroll=8)
        gsum = g_s[...]
        gated = i * xc
        d_log_a = gsum * _shift_down(hr, 1, rows) * a - gsum * gated * (a * a / mult)
        d_gated = gsum * mult
        d_pre_r = (d_log_a * (-LRU_C) * sp) * r * (1.0 - r)
        d_pre_i = (d_gated * xc) * i * (1.0 - i)
        dprb, dpib = d_pre_r.astype(BF16), d_pre_i.astype(BF16)
        dxc = d_gated * i + _dot_nt(dprb, wrg_ref[...]) + _dot_nt(dpib, wig_ref[...])
        cw = cw_ref[...]
        dx = (dxc * cw[3:4, :] + _shift_up(dxc, 1, rows, seq) * cw[2:3, :]
              + _shift_up(dxc, 2, rows, seq) * cw[1:2, :] + _shift_up(dxc, 3, rows, seq) * cw[0:1, :])
        dx_ref[...] = dx.astype(BF16)

        @pl.when(pl.program_id(1) == 0)
        def _():
            dwrg_ref[...] = jnp.zeros_like(dwrg_ref)
            dwig_ref[...] = jnp.zeros_like(dwig_ref)
            vec_ref[...] = jnp.zeros_like(vec_ref)

        dwrg_ref[...] += _dot_tn(xcb, dprb)
        dwig_ref[...] += _dot_tn(xcb, dpib)

        def colsum(v):
            return jnp.sum(v, axis=0, keepdims=True)

        d_sp = colsum(d_log_a * (-LRU_C) * r)
        vec_ref[0:1, :] += colsum(d_pre_r)
        vec_ref[1:2, :] += colsum(d_pre_i)
        vec_ref[2:3, :] += d_sp * (-_sig(-lamv))
        vec_ref[3:4, :] += colsum(dxc)
        vec_ref[4:5, :] += colsum(dxc * _shift_down(xv, 3, rows))
        vec_ref[5:6, :] += colsum(dxc * _shift_down(xv, 2, rows))
        vec_ref[6:7, :] += colsum(dxc * _shift_down(xv, 1, rows))
        vec_ref[7:8, :] += colsum(dxc * xv)

    vec = pl.BlockSpec((1, ct), lambda c, b: (0, c))
    gate_w = pl.BlockSpec((None, ct, ct), lambda c, b: (c, 0, 0))
    tile = pl.BlockSpec((seq, ct), lambda c, b: (b, c))
    return pl.pallas_call(
        body,
        grid=(n_ct, n_seq),
        in_specs=[
            tile,
            pl.BlockSpec((seq, ct), lambda c, b: (b, COL_XRNN // ct + c)),
            pl.BlockSpec((seq, ct), lambda c, b: (b, COL_GRNN // ct + c)),
            tile, tile,
            pl.BlockSpec((CONV_W, ct), lambda c, b: (0, c)), gate_w, vec, gate_w, vec, vec,
        ],
        out_specs=[tile, tile, gate_w, gate_w, pl.BlockSpec((8, ct), lambda c, b: (0, c))],
        out_shape=[jax.ShapeDtypeStruct((t, D_MODEL), BF16), jax.ShapeDtypeStruct((t, D_MODEL), BF16),
                   jax.ShapeDtypeStruct((n_ct, ct, ct), F32), jax.ShapeDtypeStruct((n_ct, ct, ct), F32),
                   jax.ShapeDtypeStruct((8, D_MODEL), F32)],
        scratch_shapes=[pltpu.VMEM((seq, ct), F32)] * 3,
        compiler_params=_params("parallel", "arbitrary"),
        name="rnn_bwd",
    )(dya, z, z, xc, hr, conv_w, wrg_bd, b_rg, wig_bd, b_ig, lam)


def _split_hi_lo(x):
    hi = x.astype(BF16)
    return hi, (x - hi.astype(F32)).astype(BF16)


def _dot_split(x, m):
    hi, lo = _split_hi_lo(x)
    return jnp.dot(hi, m, preferred_element_type=F32) + jnp.dot(lo, m, preferred_element_type=F32)


def _head_matrices(width):
    ec = (lax.broadcasted_iota(jnp.int32, (width, LANES), 0) // HEAD_DIM
          == lax.broadcasted_iota(jnp.int32, (width, LANES), 1))
    ee = (lax.broadcasted_iota(jnp.int32, (LANES, width), 1) // HEAD_DIM
          == lax.broadcasted_iota(jnp.int32, (LANES, width), 0))
    return jnp.where(ec, 1.0, 0.0).astype(BF16), jnp.where(ee, 1.0, 0.0).astype(BF16)


def _swap_halves(y):
    w = y.shape[1]
    first = (lax.broadcasted_iota(jnp.int32, y.shape, 1) % HEAD_DIM) < HEAD_DIM // 2
    return jnp.where(first, pltpu.roll(y, w - HEAD_DIM // 2, 1), pltpu.roll(y, HEAD_DIM // 2, 1))


def _normrope_fwd(x, gain, cos_t, sin_t, ec, ee):
    w = x.shape[1]
    rs = _dot_split(lax.rsqrt(_dot_split(x * x, ec) * (1.0 / HEAD_DIM) + NORM_EPS), ee)
    nx = x * rs
    y = nx * gain
    reps = w // LANES
    out = y * jnp.tile(cos_t, (1, reps)) + _swap_halves(y) * jnp.tile(sin_t, (1, reps))
    return out, nx, rs


def _normrope_bwd(dout, nx, rs, gain, cos_t, sin_t, ec, ee):
    w = dout.shape[1]
    reps = w // LANES
    dy = dout * jnp.tile(cos_t, (1, reps)) + _swap_halves(dout * jnp.tile(sin_t, (1, reps)))
    dgain = jnp.sum(dy * nx, axis=0, keepdims=True)
    dn = dy * gain
    seg = _dot_split(_dot_split(dn * nx, ec) * (1.0 / HEAD_DIM), ee)
    return rs * (dn - nx * seg), dgain


def _pair_operand(t, group):
    chunk = t[:, (group // 2) * LANES:(group // 2 + 1) * LANES]
    low = lax.broadcasted_iota(jnp.int32, chunk.shape, 1) < HEAD_DIM
    rolled = pltpu.roll(chunk, HEAD_DIM, 1)
    return jnp.where(low, chunk, rolled) if group % 2 == 0 else jnp.where(low, rolled, chunk)


def _window_mask(nq, nk, is_prev):
    qi = lax.broadcasted_iota(jnp.int32, (nq, nk), 0)
    ci = lax.broadcasted_iota(jnp.int32, (nq, nk), 1)
    return ci > qi if is_prev else ci <= qi


def _attn_fwd(z, cos_t, sin_t, q_gain_t, k_gain_t, sinks_t, *, n_seq, seq):
    t = n_seq * seq
    blk = WINDOW
    nb = seq // blk

    def body(q_ref, kp_ref, kc_ref, vp_ref, vc_ref, cosc_ref, sinc_ref, cosp_ref, sinp_ref, qg_ref, kg_ref, sk_ref,
             o_ref, l_ref):
        n = pl.program_id(1)
        ecq, eeq = _head_matrices(D_MODEL)
        eck, eek = _head_matrices(KV_W)
        cosc, sinc = cosc_ref[...], sinc_ref[...]
        qh, _, _ = _normrope_fwd(q_ref[...], qg_ref[...], cosc, sinc, ecq, eeq)
        kc, _, _ = _normrope_fwd(kc_ref[...], kg_ref[...], cosc, sinc, eck, eek)
        kp, _, _ = _normrope_fwd(kp_ref[...], kg_ref[...], cosp_ref[...], sinp_ref[...], eck, eek)
        kcat = jnp.concatenate([kp, kc], axis=0)
        vcat = jnp.concatenate([vp_ref[...], vc_ref[...]], axis=0)
        valid = jnp.concatenate([_window_mask(blk, blk, True) & (n > 0), _window_mask(blk, blk, False)], axis=1)
        low = lax.broadcasted_iota(jnp.int32, (blk, LANES), 1) < HEAD_DIM
        lane = lax.broadcasted_iota(jnp.int32, (blk, LANES), 1)
        sk = sk_ref[...]
        lmat = jnp.zeros((blk, LANES), F32)
        for group in range(N_KV_HEADS):
            k2 = _pair_operand(kcat, group).astype(BF16)
            v2 = _pair_operand(vcat, group).astype(BF16)
            for pp in range(2):
                pair = 2 * group + pp
                qp = qh[:, pair * LANES:(pair + 1) * LANES]
                outs = []
                for half in range(2):
                    head = 2 * pair + half
                    qm = jnp.where(low if half == 0 else ~low, qp, 0.0).astype(BF16)
                    s = jnp.where(valid, _dot_nt(qm, k2) * (HEAD_DIM ** -0.5), NEG_BIG)
                    sink = sk[:, head:head + 1]
                    m = jnp.maximum(jnp.max(s, axis=-1, keepdims=True), sink)
                    e = jnp.exp(s - m)
                    den = jnp.sum(e, axis=-1, keepdims=True) + jnp.exp(sink - m)
                    outs.append(jnp.dot((e / den).astype(BF16), v2, preferred_element_type=F32))
                    lmat = lmat + jnp.where(lane == head, m + jnp.log(den), 0.0)
                o_ref[:, pair * LANES:(pair + 1) * LANES] = jnp.where(low, outs[0], outs[1]).astype(BF16)
        l_ref[...] = lmat

    def row(b, n):
        return b * nb + n

    def prev(b, n):
        return b * nb + jnp.maximum(n - 1, 0)

    kw = KV_W
    tab_c = pl.BlockSpec((blk, LANES), lambda b, n: (n, 0))
    tab_p = pl.BlockSpec((blk, LANES), lambda b, n: (jnp.maximum(n - 1, 0), 0))
    return pl.pallas_call(
        body,
        grid=(n_seq, nb),
        in_specs=[
            pl.BlockSpec((blk, D_MODEL), lambda b, n: (row(b, n), COL_Q // D_MODEL)),
            pl.BlockSpec((blk, kw), lambda b, n: (prev(b, n), COL_K // kw)),
            pl.BlockSpec((blk, kw), lambda b, n: (row(b, n), COL_K // kw)),
            pl.BlockSpec((blk, kw), lambda b, n: (prev(b, n), COL_V // kw)),
            pl.BlockSpec((blk, kw), lambda b, n: (row(b, n), COL_V // kw)),
            tab_c, tab_c, tab_p, tab_p,
            pl.BlockSpec((1, D_MODEL), lambda b, n: (0, 0)),
            pl.BlockSpec((1, kw), lambda b, n: (0, 0)),
            pl.BlockSpec((1, LANES), lambda b, n: (0, 0)),
        ],
        out_specs=[pl.BlockSpec((blk, D_MODEL), lambda b, n: (row(b, n), 0)),
                   pl.BlockSpec((blk, LANES), lambda b, n: (row(b, n), 0))],
        out_shape=[jax.ShapeDtypeStruct((t, D_MODEL), BF16), jax.ShapeDtypeStruct((t, LANES), F32)],
        compiler_params=_params("parallel", "parallel"),
        name="attn_fwd",
    )(z, z, z, z, z, cos_t, sin_t, cos_t, sin_t, q_gain_t, k_gain_t, sinks_t)


def _attn_bwd(z, o, lse, do, cos_t, sin_t, q_gain_t, k_gain_t, sinks_t, *, n_seq, seq):
    t = n_seq * seq
    blk = WINDOW
    nb = seq // blk
    kw = KV_W
    scale = HEAD_DIM ** -0.5

    def body(qc_ref, qn_ref, kp_ref, kc_ref, vp_ref, vc_ref, oc_ref, on_ref, doc_ref, don_ref, lc_ref, ln_ref,
             cosc_ref, sinc_ref, cosp_ref, sinp_ref, cosn_ref, sinn_ref, qg_ref, kg_ref, sk_ref,
             dq_ref, dk_ref, dv_ref, vec_ref):
        n = pl.program_id(1)
        ecq, eeq = _head_matrices(D_MODEL)
        eck, eek = _head_matrices(KV_W)
        cosc, sinc = cosc_ref[...], sinc_ref[...]
        qg, kg = qg_ref[...], kg_ref[...]
        qhc, nqc, rsqc = _normrope_fwd(qc_ref[...], qg, cosc, sinc, ecq, eeq)
        qhn, _, _ = _normrope_fwd(qn_ref[...], qg, cosn_ref[...], sinn_ref[...], ecq, eeq)
        khc, nkc, rskc = _normrope_fwd(kc_ref[...], kg, cosc, sinc, eck, eek)
        khp, _, _ = _normrope_fwd(kp_ref[...], kg, cosp_ref[...], sinp_ref[...], eck, eek)
        doc = doc_ref[...].astype(F32)
        don = don_ref[...].astype(F32)
        delc = _dot_split(doc * oc_ref[...].astype(F32), ecq)
        deln = _dot_split(don * on_ref[...].astype(F32), ecq)
        lc, ln = lc_ref[...], ln_ref[...]
        vis_prev = _window_mask(blk, blk, True)
        vis_same = _window_mask(blk, blk, False)
        mask_a = vis_prev & (n > 0)
        mask_c = vis_prev & (n < nb - 1)
        low = lax.broadcasted_iota(jnp.int32, (blk, LANES), 1) < HEAD_DIM
        lane = lax.broadcasted_iota(jnp.int32, (1, LANES), 1)
        sk = sk_ref[...]
        dsink = jnp.zeros((1, LANES), F32)
        dk_chunks, dv_chunks = [], []
        dk_pair, dv_pair = [], []
        for group in range(N_KV_HEADS):
            k2c = _pair_operand(khc, group).astype(BF16)
            k2p = _pair_operand(khp, group).astype(BF16)
            v2c = _pair_operand(vc_ref[...], group).astype(BF16)
            v2p = _pair_operand(vp_ref[...], group).astype(BF16)
            dk2 = jnp.zeros((blk, LANES), F32)
            dv2 = jnp.zeros((blk, LANES), F32)
            for pp in range(2):
                pair = 2 * group + pp
                lanes = slice(pair * LANES, (pair + 1) * LANES)
                dq_halves = []
                for half in range(2):
                    head = 2 * pair + half
                    sel = low if half == 0 else ~low
                    qcm = jnp.where(sel, qhc[:, lanes], 0.0).astype(BF16)
                    qnm = jnp.where(sel, qhn[:, lanes], 0.0).astype(BF16)
                    docm = jnp.where(sel, doc[:, lanes], 0.0).astype(BF16)
                    donm = jnp.where(sel, don[:, lanes], 0.0).astype(BF16)
                    l_c, l_n = lc[:, head:head + 1], ln[:, head:head + 1]
                    d_c, d_n = delc[:, head:head + 1], deln[:, head:head + 1]

                    def probs(qm, k2, lrow, mask):
                        return jnp.where(mask, jnp.exp(_dot_nt(qm, k2) * scale - lrow), 0.0)

                    p_a = probs(qcm, k2p, l_c, mask_a)
                    p_b = probs(qcm, k2c, l_c, vis_same)
                    p_c = probs(qnm, k2c, l_n, mask_c)
                    ds_a = (p_a * (_dot_nt(docm, v2p) - d_c)).astype(BF16)
                    ds_b = (p_b * (_dot_nt(docm, v2c) - d_c)).astype(BF16)
                    ds_c = (p_c * (_dot_nt(donm, v2c) - d_n)).astype(BF16)
                    dq_halves.append((jnp.dot(ds_a, k2p, preferred_element_type=F32)
                                      + jnp.dot(ds_b, k2c, preferred_element_type=F32)) * scale)
                    dk2 = dk2 + (_dot_tn(ds_b, qcm) + _dot_tn(ds_c, qnm)) * scale
                    dv2 = dv2 + _dot_tn(p_b.astype(BF16), docm) + _dot_tn(p_c.astype(BF16), donm)
                    p_sink = jnp.exp(sk[:, head:head + 1] - l_c)
                    dsink = dsink + jnp.where(lane == head, -jnp.sum(p_sink * d_c, axis=0, keepdims=True), 0.0)
                dq_ref[:, lanes] = jnp.where(low, dq_halves[0], dq_halves[1])
            dk_pair.append(dk2 + pltpu.roll(dk2, HEAD_DIM, 1))
            dv_pair.append(dv2 + pltpu.roll(dv2, HEAD_DIM, 1))
            if group % 2 == 1:
                dk_chunks.append(jnp.where(low, dk_pair[-2], dk_pair[-1]))
                dv_chunks.append(jnp.where(low, dv_pair[-2], dv_pair[-1]))
        dkh = jnp.concatenate(dk_chunks, axis=1)
        dv_ref[...] = jnp.concatenate(dv_chunks, axis=1).astype(BF16)
        dq, dqg = _normrope_bwd(dq_ref[...], nqc, rsqc, qg, cosc, sinc, ecq, eeq)
        dk, dkg = _normrope_bwd(dkh, nkc, rskc, kg, cosc, sinc, eck, eek)
        dq_ref[...] = dq
        dk_ref[...] = dk.astype(BF16)

        @pl.when(n == 0)
        def _():
            vec_ref[...] = jnp.zeros_like(vec_ref)

        vec_ref[0:1, :] += dqg
        vec_ref[1:2, 0:kw] += dkg
        vec_ref[2:3, 0:LANES] += dsink

    def row(b, n):
        return b * nb + n

    def prev(b, n):
        return b * nb + jnp.maximum(n - 1, 0)

    def nxt(b, n):
        return b * nb + jnp.minimum(n + 1, nb - 1)

    def tiles(width, col, which):
        return pl.BlockSpec((blk, width), lambda b, n: (which(b, n), col))

    def table(which):
        return pl.BlockSpec((blk, LANES), lambda b, n: (which(0, n), 0))

    dq, dk, dv, vec = pl.pallas_call(
        body,
        grid=(n_seq, nb),
        in_specs=[
            tiles(D_MODEL, COL_Q // D_MODEL, row), tiles(D_MODEL, COL_Q // D_MODEL, nxt),
            tiles(kw, COL_K // kw, prev), tiles(kw, COL_K // kw, row),
            tiles(kw, COL_V // kw, prev), tiles(kw, COL_V // kw, row),
            tiles(D_MODEL, 0, row), tiles(D_MODEL, 0, nxt),
            tiles(D_MODEL, 0, row), tiles(D_MODEL, 0, nxt),
            tiles(LANES, 0, row), tiles(LANES, 0, nxt),
            table(row), table(row), table(prev), table(prev), table(nxt), table(nxt),
            pl.BlockSpec((1, D_MODEL), lambda b, n: (0, 0)),
            pl.BlockSpec((1, kw), lambda b, n: (0, 0)),
            pl.BlockSpec((1, LANES), lambda b, n: (0, 0)),
        ],
        out_specs=[tiles(D_MODEL, 0, row), tiles(kw, 0, row), tiles(kw, 0, row),
                   pl.BlockSpec((None, 8, D_MODEL), lambda b, n: (b, 0, 0))],
        out_shape=[jax.ShapeDtypeStruct((t, D_MODEL), F32), jax.ShapeDtypeStruct((t, kw), BF16),
                   jax.ShapeDtypeStruct((t, kw), BF16), jax.ShapeDtypeStruct((n_seq, 8, D_MODEL), F32)],
        compiler_params=_params("parallel", "arbitrary"),
        name="attn_bwd",
    )(z, z, z, z, z, z, o, o, do, do, lse, lse, cos_t, sin_t, cos_t, sin_t, cos_t, sin_t,
      q_gain_t, k_gain_t, sinks_t)
    return dq, dk, dv, vec


MERGE_COLS = 512


def _merge_fwd(z, ya, yb):
    t = ya.shape[0]
    tm, tc = min(512, t), MERGE_COLS

    def body(ga_ref, gb_ref, ya_ref, yb_ref, o_ref):
        o_ref[...] = (_sig(ga_ref[...]) * ya_ref[...] + _sig(gb_ref[...]) * yb_ref[...]).astype(BF16)

    tile = pl.BlockSpec((tm, tc), lambda i, j: (i, j))
    return pl.pallas_call(
        body,
        grid=(t // tm, D_MODEL // tc),
        in_specs=[pl.BlockSpec((tm, tc), lambda i, j: (i, COL_GA // tc + j)),
                  pl.BlockSpec((tm, tc), lambda i, j: (i, COL_GB // tc + j)), tile, tile],
        out_specs=tile,
        out_shape=jax.ShapeDtypeStruct((t, D_MODEL), BF16),
        compiler_params=_params("parallel", "parallel"),
        name="merge_fwd",
    )(z, z, ya, yb)


def _merge_bwd(z, ya, yb, dmerged):
    t = ya.shape[0]
    tm, tc = min(512, t), MERGE_COLS

    def body(ga_ref, gb_ref, ya_ref, yb_ref, dm_ref, dya_ref, dyb_ref, dga_ref, dgb_ref):
        dm = dm_ref[...]
        sa, sb = _sig(ga_ref[...]), _sig(gb_ref[...])
        dya_ref[...] = (dm * sa).astype(BF16)
        dyb_ref[...] = (dm * sb).astype(BF16)
        dga_ref[...] = (dm * ya_ref[...] * sa * (1.0 - sa)).astype(BF16)
        dgb_ref[...] = (dm * yb_ref[...] * sb * (1.0 - sb)).astype(BF16)

    tile = pl.BlockSpec((tm, tc), lambda i, j: (i, j))
    return pl.pallas_call(
        body,
        grid=(t // tm, D_MODEL // tc),
        in_specs=[pl.BlockSpec((tm, tc), lambda i, j: (i, COL_GA // tc + j)),
                  pl.BlockSpec((tm, tc), lambda i, j: (i, COL_GB // tc + j)), tile, tile, tile],
        out_specs=[tile] * 4,
        out_shape=[jax.ShapeDtypeStruct((t, D_MODEL), BF16)] * 4,
        compiler_params=_params("parallel", "parallel"),
        name="merge_bwd",
    )(z, z, ya, yb, dmerged)


def _loss_head(x2, e, gt, target):
    t, d = x2.shape
    tm = min(256, t)

    def body(x_ref, e_ref, gt_ref, tg_ref, loss_ref, dx_ref, dgt_ref, de_ref):
        ev = e_ref[...]
        sg = _sig(gt_ref[...])
        diff = x_ref[...] + ev * sg - tg_ref[...]
        dx = diff * (1.0 / d)
        dx_ref[...] = dx
        dgt_ref[...] = (dx * ev * sg * (1.0 - sg)).astype(BF16)
        de_ref[...] = (dx * sg).astype(BF16)

        @pl.when(pl.program_id(0) == 0)
        def _():
            loss_ref[...] = jnp.zeros_like(loss_ref)

        loss_ref[...] += jnp.sum(jnp.sum(diff * diff, axis=1, keepdims=True), axis=0, keepdims=True)

    tile = pl.BlockSpec((tm, d), lambda i: (i, 0))
    return pl.pallas_call(
        body,
        grid=(t // tm,),
        in_specs=[tile] * 4,
        out_specs=[pl.BlockSpec((1, LANES), lambda i: (0, 0)), tile, tile, tile],
        out_shape=[jax.ShapeDtypeStruct((1, LANES), F32), jax.ShapeDtypeStruct((t, d), F32),
                   jax.ShapeDtypeStruct((t, d), BF16), jax.ShapeDtypeStruct((t, d), BF16)],
        compiler_params=_params("arbitrary"),
        name="loss_head",
    )(x2, e, gt, target)


def _rope_tables(seq):
    inv = ROPE_THETA ** (-jnp.arange(0, HEAD_DIM, 2, dtype=F32) / HEAD_DIM)
    ang = jnp.arange(seq, dtype=F32)[:, None] * inv[None, :]
    cos, sin = jnp.cos(ang), jnp.sin(ang)
    return jnp.tile(jnp.concatenate([cos, cos], axis=1), (1, 2)), jnp.tile(jnp.concatenate([-sin, sin], axis=1), (1, 2))


def _block_diag_tiles(w):
    per = RNN_TILE // RNN_BLOCK_W
    w4 = w.reshape(D_MODEL // RNN_TILE, per, RNN_BLOCK_W, RNN_BLOCK_W)
    eye = jnp.eye(per, dtype=w.dtype)
    dense = jnp.einsum("tpij,pq->tpiqj", w4, eye)
    return dense.reshape(D_MODEL // RNN_TILE, RNN_TILE, RNN_TILE).astype(BF16)


def _block_diag_extract(dense):
    per = RNN_TILE // RNN_BLOCK_W
    d5 = dense.reshape(D_MODEL // RNN_TILE, per, RNN_BLOCK_W, per, RNN_BLOCK_W)
    blocks = jnp.stack([d5[:, p, :, p, :] for p in range(per)], axis=1)
    return blocks.reshape(D_MODEL // RNN_BLOCK_W, RNN_BLOCK_W, RNN_BLOCK_W)


def _local_step(x, p, target, w, *, n_seq, seq):
    cos_t, sin_t = _rope_tables(seq)
    q_gain_t = jnp.tile(w["q_gain"], (1, N_Q_HEADS))
    k_gain_t = jnp.tile(w["k_gain"], (1, N_KV_HEADS))
    sinks_t = jnp.pad(w["sinks"], ((0, 0), (0, LANES - N_Q_HEADS)))
    wrg_bd, wig_bd = _block_diag_tiles(w["w_rg"]), _block_diag_tiles(w["w_ig"])
    dims = dict(n_seq=n_seq, seq=seq)

    h = _rmsnorm_fwd(x, w["g_mix"], name="norm_mix")
    z = _matmul(h, w["w_in"], mode="nn", tm=1024, tn=512, out_dtypes=[F32], name="mm_in")
    xc, hr, ya_in = _rnn_fwd(z, w["conv_w"], w["conv_b"], wrg_bd, w["b_rg"], wig_bd, w["b_ig"], w["lru_lambda"], **dims)
    o, lse = _attn_fwd(z, cos_t, sin_t, q_gain_t, k_gain_t, sinks_t, **dims)
    ya = _matmul(ya_in, w["w_rnn_proj"], mode="nn", tm=1024, tn=512, out_dtypes=[F32], name="mm_rnn_proj")
    yb = _matmul(o, w["w_attn_proj"], mode="nn", tm=1024, tn=512, out_dtypes=[F32], name="mm_attn_proj")
    merged = _merge_fwd(z, ya, yb)
    x1 = _matmul(merged, w["w_out"], mode="nn", tm=1024, tn=512, out_dtypes=[F32], name="mm_out",
                 epilogue=lambda acc, res: (res + acc,), extras=(x,))
    hm = _rmsnorm_fwd(x1, w["g_mlp"], name="norm_mlp")
    u, act = _matmul(hm, w["w_up"], mode="nn", tm=1024, tn=512, out_dtypes=[F32, BF16], name="mm_up",
                     epilogue=lambda acc: (acc, jnp.square(jnp.maximum(acc, 0.0))))
    x2 = _matmul(act, w["w_down"], mode="nn", tm=512, tn=512, out_dtypes=[F32], name="mm_down",
                 epilogue=lambda acc, res: (res + acc,), extras=(x1,))
    hp = _rmsnorm_fwd(x2, w["g_ple"], name="norm_ple")
    gt = _matmul(hp, w["w_ple_gate"], mode="nn", tm=1024, tn=512, out_dtypes=[F32], name="mm_ple_gate")
    p_bf = p.astype(BF16)
    e = _matmul(p_bf, w["w_ple_proj"], mode="nn", tm=1024, tn=512, out_dtypes=[F32], name="mm_ple_proj")
    loss_row, dx3, dgt, de = _loss_head(x2, e, gt, target)

    g = {}
    g["w_ple_proj"] = _matmul_tn(p_bf, de, tk=PLE_DIM, tn=LANES, tt=512, name="mm_d_ple_proj", stacked=True)
    g["w_ple_gate"] = _matmul_tn(hp, dgt, tk=1024, tn=1024, tt=512, name="mm_d_ple_gate")
    dhp = _matmul(dgt, w["w_ple_gate"], mode="nt", tm=1024, tn=512, out_dtypes=[F32], name="mm_dhp")
    dx2, dx2_bf, g["g_ple"] = _rmsnorm_bwd(x2, w["g_ple"], dhp, dx3, name="norm_ple_bwd", want_bf16=True)
    g["w_down"] = _matmul_tn(act, dx2_bf, tk=1024, tn=1024, tt=512, name="mm_d_down")
    du = _matmul(dx2_bf, w["w_down"], mode="nt", tm=1024, tn=512, out_dtypes=[BF16], name="mm_dact",
                 epilogue=lambda acc, uu: (acc * (2.0 * jnp.maximum(uu, 0.0)),), extras=(u,))
    g["w_up"] = _matmul_tn(hm, du, tk=1024, tn=D_FF // N_DEV, tt=512, name="mm_d_up", stacked=True)
    dhm = _matmul(du, w["w_up"], mode="nt", tm=512, tn=512, out_dtypes=[F32], name="mm_dhm")
    dx1, dx1_bf, g["g_mlp"] = _rmsnorm_bwd(x1, w["g_mlp"], dhm, dx2, name="norm_mlp_bwd", want_bf16=True)
    g["w_out"] = _matmul_tn(merged, dx1_bf, tk=1024, tn=1024, tt=512, name="mm_d_out")
    dmerged = _matmul(dx1_bf, w["w_out"], mode="nt", tm=1024, tn=512, out_dtypes=[F32], name="mm_dmerged")
    dya, dyb, dga, dgb = _merge_bwd(z, ya, yb, dmerged)
    g["w_rnn_proj"] = _matmul_tn(ya_in, dya, tk=1024, tn=1024, tt=512, name="mm_d_rnn_proj")
    g["w_attn_proj"] = _matmul_tn(o, dyb, tk=1024, tn=1024, tt=512, name="mm_d_attn_proj")
    dya_in = _matmul(dya, w["w_rnn_proj"], mode="nt", tm=1024, tn=512, out_dtypes=[F32], name="mm_dya_in")
    do = _matmul(dyb, w["w_attn_proj"], mode="nt", tm=1024, tn=512, out_dtypes=[BF16], name="mm_do")
    dx_rnn, dg_rnn, dwrg_dense, dwig_dense, rnn_vec = _rnn_bwd(
        dya_in, z, xc, hr, w["conv_w"], wrg_bd, w["b_rg"], wig_bd, w["b_ig"], w["lru_lambda"], **dims)
    dq, dk, dv, attn_vec = _attn_bwd(z, o, lse, do, cos_t, sin_t, q_gain_t, k_gain_t, sinks_t, **dims)
    dz = jnp.concatenate([dx_rnn, dg_rnn, dq.astype(BF16), dk, dv, dga, dgb], axis=1)
    g["w_in"] = _matmul_tn(h, dz, tk=1024, tn=IN_TOTAL // 4, tt=512, name="mm_d_in")
    dh = _matmul(dz, w["w_in"], mode="nt", tm=512, tn=512, out_dtypes=[F32], name="mm_dh")
    grad_x, g["g_mix"] = _rmsnorm_bwd(x, w["g_mix"], dh, dx1, name="norm_mix_bwd", want_bf16=False)

    g["w_rg"] = _block_diag_extract(dwrg_dense)
    g["w_ig"] = _block_diag_extract(dwig_dense)
    g["b_rg"], g["b_ig"], g["lru_lambda"], g["conv_b"] = (rnn_vec[i:i + 1] for i in range(4))
    g["conv_w"] = rnn_vec[4:8]
    attn_vec = attn_vec[0] if n_seq == 1 else functools.reduce(jnp.add, [attn_vec[b] for b in range(n_seq)])
    g["q_gain"] = attn_vec[0].reshape(N_Q_HEADS, HEAD_DIM).sum(axis=0)[None, :]
    g["k_gain"] = attn_vec[1, :KV_W].reshape(N_KV_HEADS, HEAD_DIM).sum(axis=0)[None, :]
    g["sinks"] = attn_vec[2:3, :N_Q_HEADS]
    return loss_row[0, 0], grad_x, g


MESH_ID = pl.DeviceIdType.MESH


def _coords(index):
    return (index >> 2) & 1, (index >> 1) & 1, index & 1


def _exchange(srcs, kinds, *, name):
    n = len(srcs)
    n_peer = N_DEV - 1

    def body(*refs):
        src, dst = refs[:n], refs[n:2 * n]
        send_sems, recv_sems, local_sems = refs[2 * n:]
        me = 4 * lax.axis_index("x") + 2 * lax.axis_index("y") + lax.axis_index("c")

        def remote(i, d):
            peer = (me + d) & (N_DEV - 1)
            piece = src[i] if kinds[i] == "gather" else src[i].at[peer]
            return pltpu.make_async_remote_copy(
                src_ref=piece, dst_ref=dst[i].at[me], send_sem=send_sems.at[i * n_peer + d - 1],
                recv_sem=recv_sems.at[i * n_peer + d - 1], device_id=_coords(peer), device_id_type=MESH_ID)

        def arrival(i, d):
            sender = (me - d) & (N_DEV - 1)
            piece = src[i] if kinds[i] == "gather" else src[i].at[sender]
            return pltpu.make_async_remote_copy(
                src_ref=piece, dst_ref=dst[i].at[sender], send_sem=send_sems.at[i * n_peer + d - 1],
                recv_sem=recv_sems.at[i * n_peer + d - 1], device_id=_coords(sender), device_id_type=MESH_ID)

        own = []
        for i in range(n):
            piece = src[i] if kinds[i] == "gather" else src[i].at[me]
            own.append(pltpu.make_async_copy(piece, dst[i].at[me], local_sems.at[i]))
            own[-1].start()
        sent = [remote(i, d) for d in range(1, N_DEV) for i in range(n)]
        for cp in sent:
            cp.start()
        for d in range(1, N_DEV):
            for i in range(n):
                arrival(i, d).wait_recv()
        for cp in sent:
            cp.wait_send()
        for cp in own:
            cp.wait()

    def out_of(s, kind):
        shape = s.shape if kind == "scatter" else (N_DEV,) + s.shape
        return jax.ShapeDtypeStruct(shape, s.dtype)

    any_spec = pl.BlockSpec(memory_space=pl.ANY)
    return pl.pallas_call(
        body,
        in_specs=[any_spec] * n,
        out_specs=[any_spec] * n,
        out_shape=[out_of(s, k) for s, k in zip(srcs, kinds)],
        scratch_shapes=[pltpu.SemaphoreType.DMA((n * n_peer,)), pltpu.SemaphoreType.DMA((n * n_peer,)),
                        pltpu.SemaphoreType.DMA((n,))],
        compiler_params=pltpu.CompilerParams(has_side_effects=True),
        name=name,
    )(*srcs)


def _adamw(parts, w, m, v, *, name):
    r, c = w.shape
    tr = min(256, r)
    c1 = 1.0 - ADAM_B1 ** ADAM_STEP
    c2 = 1.0 - ADAM_B2 ** ADAM_STEP

    def body(p_ref, w_ref, m_ref, v_ref, g_ref, d_ref, nm_ref, nv_ref):
        g = p_ref[0]
        for s in range(1, N_DEV):
            g = g + p_ref[s]
        nm = ADAM_B1 * m_ref[...] + (1.0 - ADAM_B1) * g
        nv = ADAM_B2 * v_ref[...] + (1.0 - ADAM_B2) * (g * g)
        g_ref[...] = g
        nm_ref[...] = nm
        nv_ref[...] = nv
        d_ref[...] = -ADAM_LR * ((nm / c1) / (jnp.sqrt(nv / c2) + ADAM_EPS) + ADAM_WD * w_ref[...])

    tile = pl.BlockSpec((tr, c), lambda i: (i, 0))
    return pl.pallas_call(
        body,
        grid=(r // tr,),
        in_specs=[pl.BlockSpec((N_DEV, tr, c), lambda i: (0, i, 0)), tile, tile, tile],
        out_specs=[tile] * 4,
        out_shape=[jax.ShapeDtypeStruct((r, c), F32)] * 4,
        compiler_params=_params("parallel"),
        name=name,
    )(parts, w, m, v)


BIG = ("w_in", "w_rnn_proj", "w_attn_proj", "w_out", "w_up", "w_down", "w_ple_gate", "w_ple_proj")
SMALL = (("g_mix", 1), ("conv_b", 1), ("b_rg", 1), ("b_ig", 1), ("lru_lambda", 1), ("g_mlp", 1), ("g_ple", 1),
         ("conv_w", 4), ("q_gain", 1), ("k_gain", 1), ("sinks", 1), ("w_rg", 64), ("w_ig", 64))
SMALL_ROWS = 144


def _pack_small(vals):
    rows = []
    for nm, nrow in SMALL:
        flat = vals[nm].reshape(-1).astype(F32)
        rows.append(jnp.pad(flat, (0, nrow * D_MODEL - flat.shape[0])).reshape(nrow, D_MODEL))
    used = sum(nrow for _, nrow in SMALL)
    rows.append(jnp.zeros((SMALL_ROWS - used, D_MODEL), F32))
    return jnp.concatenate(rows, axis=0)


def _unpack_small(packed, shapes):
    out, at = {}, 0
    for nm, nrow in SMALL:
        size = 1
        for s in shapes[nm]:
            size *= s
        out[nm] = packed[at:at + nrow].reshape(-1)[:size].reshape(shapes[nm])
        at += nrow
    return out


def kernel(x, p, g_mix, w_in, conv_w, conv_b, w_rg, b_rg, w_ig, b_ig, lru_lambda, w_rnn_proj, q_gain, k_gain, sinks, w_attn_proj, w_out, g_mlp, w_up, w_down, g_ple, w_ple_gate, w_ple_proj, loss_target, m_g_mix, m_w_in, m_conv_w, m_conv_b, m_w_rg, m_b_rg, m_w_ig, m_b_ig, m_lru_lambda, m_w_rnn_proj, m_q_gain, m_k_gain, m_sinks, m_w_attn_proj, m_w_out, m_g_mlp, m_w_up, m_w_down, m_g_ple, m_w_ple_gate, m_w_ple_proj, v_g_mix, v_w_in, v_conv_w, v_conv_b, v_w_rg, v_b_rg, v_w_ig, v_b_ig, v_lru_lambda, v_w_rnn_proj, v_q_gain, v_k_gain, v_sinks, v_w_attn_proj, v_w_out, v_g_mlp, v_w_up, v_w_down, v_g_ple, v_w_ple_gate, v_w_ple_proj):
    names = ("g_mix", "w_in", "conv_w", "conv_b", "w_rg", "b_rg", "w_ig", "b_ig", "lru_lambda", "w_rnn_proj",
             "q_gain", "k_gain", "sinks", "w_attn_proj", "w_out", "g_mlp", "w_up", "w_down", "g_ple",
             "w_ple_gate", "w_ple_proj")
    wts = dict(zip(names, (g_mix, w_in, conv_w, conv_b, w_rg, b_rg, w_ig, b_ig, lru_lambda, w_rnn_proj, q_gain,
                           k_gain, sinks, w_attn_proj, w_out, g_mlp, w_up, w_down, g_ple, w_ple_gate, w_ple_proj)))
    mom1 = dict(zip(names, (m_g_mix, m_w_in, m_conv_w, m_conv_b, m_w_rg, m_b_rg, m_w_ig, m_b_ig, m_lru_lambda,
                            m_w_rnn_proj, m_q_gain, m_k_gain, m_sinks, m_w_attn_proj, m_w_out, m_g_mlp, m_w_up,
                            m_w_down, m_g_ple, m_w_ple_gate, m_w_ple_proj)))
    mom2 = dict(zip(names, (v_g_mix, v_w_in, v_conv_w, v_conv_b, v_w_rg, v_b_rg, v_w_ig, v_b_ig, v_lru_lambda,
                            v_w_rnn_proj, v_q_gain, v_k_gain, v_sinks, v_w_attn_proj, v_w_out, v_g_mlp, v_w_up,
                            v_w_down, v_g_ple, v_w_ple_gate, v_w_ple_proj)))
    n_seq, seq, _ = x.shape
    me = 4 * lax.axis_index("x") + 2 * lax.axis_index("y") + lax.axis_index("c")

    shards = [wts[nm][0].astype(BF16) for nm in BIG] + [conv_w[0]]
    gathered = _exchange(shards, ["gather"] * len(shards), name="gather_weights")
    full = dict(zip(BIG, gathered[:-1]))
    w = {nm: wts[nm] for nm in names if nm not in BIG}
    w["w_rg"], w["w_ig"] = w_rg[0], w_ig[0]
    w["conv_w"] = gathered[-1].transpose(1, 0, 2).reshape(CONV_W, D_MODEL)
    for nm in ("w_in", "w_up", "w_ple_proj"):
        a = full[nm]
        w[nm] = a.transpose(1, 0, 2).reshape(a.shape[1], N_DEV * a.shape[2])
    for nm in ("w_rnn_proj", "w_attn_proj", "w_out", "w_down", "w_ple_gate"):
        a = full[nm]
        w[nm] = a.reshape(N_DEV * a.shape[1], a.shape[2])

    loss_sum, grad_x, g = _local_step(
        x.reshape(n_seq * seq, D_MODEL), p.reshape(n_seq * seq, PLE_DIM), loss_target.reshape(n_seq * seq, D_MODEL),
        w, n_seq=n_seq, seq=seq)
    loss = lax.psum(loss_sum, ("x", "y", "c")) * (0.5 / D_MODEL)

    shard_w = IN_TOTAL // N_DEV
    parts = {
        "w_in": g["w_in"].reshape(D_MODEL, N_DEV, shard_w).transpose(1, 0, 2),
        "w_up": g["w_up"], "w_ple_proj": g["w_ple_proj"],
    }
    for nm in ("w_rnn_proj", "w_attn_proj", "w_out", "w_down", "w_ple_gate"):
        parts[nm] = g[nm].reshape(N_DEV, g[nm].shape[0] // N_DEV, g[nm].shape[1])
    small_names = [nm for nm, _ in SMALL]
    small_part = _pack_small({nm: g[nm] for nm in small_names})
    received = _exchange([parts[nm] for nm in BIG] + [small_part], ["scatter"] * len(BIG) + ["gather"],
                         name="exchange_grads")

    res = {}
    for nm, parts_nm in zip(BIG, received[:-1]):
        res[nm] = _adamw(parts_nm, wts[nm][0], mom1[nm][0], mom2[nm][0], name="adamw_" + nm)
    conv_lanes = D_MODEL // N_DEV
    full_small = {}
    for src, key in ((wts, "w"), (mom1, "m"), (mom2, "v")):
        vals = {nm: src[nm][0] for nm in small_names if nm != "conv_w"}
        vals["conv_w"] = lax.dynamic_update_slice(jnp.zeros((CONV_W, D_MODEL), F32), src["conv_w"][0], (0, me * conv_lanes))
        full_small[key] = _pack_small(vals)
    small_res = _adamw(received[-1], full_small["w"], full_small["m"], full_small["v"], name="adamw_small")
    shapes = {nm: wts[nm].shape[1:] for nm in small_names}
    shapes["conv_w"] = (CONV_W, D_MODEL)
    small_out = [_unpack_small(r, shapes) for r in small_res]
    for nm in small_names:
        vals = [so[nm] for so in small_out]
        if nm == "conv_w":
            vals = [lax.dynamic_slice(a, (0, me * conv_lanes), (CONV_W, conv_lanes)) for a in vals]
        res[nm] = vals

    outs = [loss, grad_x.reshape(n_seq, seq, D_MODEL)]
    for k in range(4):
        outs.extend(res[nm][k][None] for nm in names)
    return tuple(outs)
```

```python
import functools
from typing import Callable, NamedTuple

import jax
import jax.numpy as jnp
from jax import lax
from jax.experimental import pallas as pl
from jax.experimental.pallas import tpu as pltpu

F32 = jnp.float32
BF16 = jnp.bfloat16

N_DEV = 8
D_MODEL = 1024
RNN_BLOCK_W = 64
CONV_W = 4
LRU_C = 8.0
HEAD_DIM = 64
N_Q_HEADS = 16
N_KV_HEADS = 4
KV_W = N_KV_HEADS * HEAD_DIM
WINDOW = 128
ROPE_THETA = 10000.0
D_FF = 4096
PLE_DIM = 256
NORM_EPS = 1e-6
IN_TOTAL = 5632
COL_XRNN, COL_GRNN, COL_Q, COL_K, COL_V, COL_GA, COL_GB = 0, 1024, 2048, 3072, 3328, 3584, 4608

ADAM_LR = 0.001
ADAM_B1 = 0.9
ADAM_B2 = 0.999
ADAM_EPS = 1e-08
ADAM_WD = 0.01
ADAM_STEP = 10

LANES = 128
RNN_TILE = 256
VMEM_LIMIT = 48 * 1024 * 1024
NEG_BIG = -1e30


def _params(*sem):
    return pltpu.CompilerParams(dimension_semantics=sem if sem else None, vmem_limit_bytes=VMEM_LIMIT)


def _sig(x):
    return 1.0 / (1.0 + jnp.exp(-x))


def _dot_nt(a, b):
    return lax.dot_general(a, b, (((1,), (1,)), ((), ())), preferred_element_type=F32)


def _dot_tn(a, b):
    return lax.dot_general(a, b, (((0,), (0,)), ((), ())), preferred_element_type=F32)


class _Xfer:
    def __init__(self, start, wait):
        self.start, self.wait = start, wait


class _Hosted(NamedTuple):
    srcs: tuple
    out_shape: tuple
    n_sems: int
    plan: Callable


def _call(body, *, grid, in_specs, out_specs, out_shape, args, name, semantics, scratch_shapes=(), hosted=None):
    if hosted is None:
        outs = pl.pallas_call(body, grid=grid, in_specs=list(in_specs), out_specs=list(out_specs),
                              out_shape=list(out_shape), scratch_shapes=list(scratch_shapes),
                              compiler_params=_params(*semantics), name=name)(*args)
        return list(outs), []
    counts = (len(in_specs), len(hosted.srcs), len(out_specs), len(hosted.out_shape), len(scratch_shapes), 3)

    def wrapped(*refs):
        at, groups = 0, []
        for count in counts:
            groups.append(refs[at:at + count])
            at += count
        ins, srcs, outs, dsts, scratch, sems = groups
        copies = hosted.plan(srcs, dsts, *sems)
        ids = [pl.program_id(axis) for axis in range(len(grid))]
        first = functools.reduce(jnp.logical_and, [i == 0 for i in ids])
        last = functools.reduce(jnp.logical_and, [i == g - 1 for i, g in zip(ids, grid)])

        @pl.when(first)
        def _():
            for cp in copies:
                cp.start()

        body(*ins, *outs, *scratch)

        @pl.when(last)
        def _():
            for cp in copies:
                cp.wait()

    any_spec = pl.BlockSpec(memory_space=pl.ANY)
    sems = [pltpu.SemaphoreType.DMA((hosted.n_sems,))] * 3
    outs = pl.pallas_call(
        wrapped, grid=grid, in_specs=list(in_specs) + [any_spec] * counts[1],
        out_specs=list(out_specs) + [any_spec] * counts[3], out_shape=list(out_shape) + list(hosted.out_shape),
        scratch_shapes=list(scratch_shapes) + sems, compiler_params=_params(*["arbitrary"] * len(grid)),
        name=name)(*args, *hosted.srcs)
    return list(outs[:counts[2]]), list(outs[counts[2]:])


def _matmul(a, b, *, mode, tm, tn, out_dtypes, name, epilogue=None, extras=(), hosted=None):
    m, k = a.shape
    n = b.shape[1] if mode == "nn" else b.shape[0]
    tm, tn = min(tm, m), min(tn, n)
    n_extra = len(extras)

    def body(a_ref, b_ref, *rest):
        extra_refs, out_refs = rest[:n_extra], rest[n_extra:]
        if mode == "nn":
            acc = jnp.dot(a_ref[...], b_ref[...], preferred_element_type=F32)
        else:
            acc = _dot_nt(a_ref[...], b_ref[...])
        res = epilogue(acc, *[e[...] for e in extra_refs]) if epilogue is not None else (acc,)
        for o_ref, r in zip(out_refs, res):
            o_ref[...] = r.astype(o_ref.dtype)

    b_spec = pl.BlockSpec((k, tn), lambda i, j: (0, j)) if mode == "nn" else pl.BlockSpec((tn, k), lambda i, j: (j, 0))
    tile = pl.BlockSpec((tm, tn), lambda i, j: (i, j))
    outs, landed = _call(
        body,
        grid=(m // tm, n // tn),
        in_specs=[pl.BlockSpec((tm, k), lambda i, j: (i, 0)), b_spec] + [tile] * n_extra,
        out_specs=[tile] * len(out_dtypes),
        out_shape=[jax.ShapeDtypeStruct((m, n), dt) for dt in out_dtypes],
        args=(a, b, *extras), name=name, semantics=("parallel", "arbitrary"), hosted=hosted)
    if hosted is not None:
        return (*outs, landed)
    return outs[0] if len(outs) == 1 else outs


def _matmul_tn(a, b, *, tk, tn, tt, name, stacked=False):
    t, k = a.shape
    n = b.shape[1]
    tk, tn, tt = min(tk, k), min(tn, n), min(tt, t)

    def body(a_ref, b_ref, o_ref):
        @pl.when(pl.program_id(2) == 0)
        def _():
            o_ref[...] = jnp.zeros_like(o_ref)

        o_ref[...] += _dot_tn(a_ref[...], b_ref[...])

    if stacked:
        assert k == tk
        out_spec = pl.BlockSpec((None, tk, tn), lambda i, j, s: (j, i, 0))
        out_shape = jax.ShapeDtypeStruct((n // tn, k, tn), F32)
    else:
        out_spec = pl.BlockSpec((tk, tn), lambda i, j, s: (i, j))
        out_shape = jax.ShapeDtypeStruct((k, n), F32)
    return pl.pallas_call(
        body,
        grid=(k // tk, n // tn, t // tt),
        in_specs=[pl.BlockSpec((tt, tk), lambda i, j, s: (s, i)), pl.BlockSpec((tt, tn), lambda i, j, s: (s, j))],
        out_specs=out_spec,
        out_shape=out_shape,
        compiler_params=_params("parallel", "parallel", "arbitrary"),
        name=name,
    )(a, b)


def _rmsnorm_fwd(x, g, *, name):
    t, d = x.shape
    tm = min(512, t)

    def body(x_ref, g_ref, o_ref):
        xv = x_ref[...]
        r = lax.rsqrt(jnp.mean(xv * xv, axis=-1, keepdims=True) + NORM_EPS)
        o_ref[...] = (xv * r * g_ref[...]).astype(BF16)

    return pl.pallas_call(
        body,
        grid=(t // tm,),
        in_specs=[pl.BlockSpec((tm, d), lambda i: (i, 0)), pl.BlockSpec((1, d), lambda i: (0, 0))],
        out_specs=pl.BlockSpec((tm, d), lambda i: (i, 0)),
        out_shape=jax.ShapeDtypeStruct((t, d), BF16),
        compiler_params=_params("parallel"),
        name=name,
    )(x, g)


def _rmsnorm_bwd(x, g, dy, dres, *, name, want_bf16, hosted=None):
    t, d = x.shape
    tm = min(256, t)

    def body(x_ref, g_ref, dy_ref, dres_ref, *out_refs):
        dx_ref, dg_ref = out_refs[0], out_refs[-1]
        xv, dyv = x_ref[...], dy_ref[...]
        r = lax.rsqrt(jnp.mean(xv * xv, axis=-1, keepdims=True) + NORM_EPS)
        xr = xv * r
        gy = dyv * g_ref[...]
        dx = dres_ref[...] + r * (gy - xr * jnp.mean(gy * xr, axis=-1, keepdims=True))
        dx_ref[...] = dx
        if want_bf16:
            out_refs[1][...] = dx.astype(BF16)

        @pl.when(pl.program_id(0) == 0)
        def _():
            dg_ref[...] = jnp.zeros_like(dg_ref)

        dg_ref[...] += jnp.sum(dyv * xr, axis=0, keepdims=True)

    tile = pl.BlockSpec((tm, d), lambda i: (i, 0))
    vec = pl.BlockSpec((1, d), lambda i: (0, 0))
    out_specs = [tile] + ([tile] if want_bf16 else []) + [vec]
    out_shape = [jax.ShapeDtypeStruct((t, d), F32)] + ([jax.ShapeDtypeStruct((t, d), BF16)] if want_bf16 else [])
    out_shape.append(jax.ShapeDtypeStruct((1, d), F32))
    outs, landed = _call(body, grid=(t // tm,), in_specs=[tile, vec, tile, tile], out_specs=out_specs,
                         out_shape=out_shape, args=(x, g, dy, dres), name=name, semantics=("arbitrary",), hosted=hosted)
    return (*outs, landed) if hosted is not None else outs


def _softplus_neg(lam):
    z = -lam
    return jnp.maximum(z, 0.0) + jnp.log1p(jnp.exp(-jnp.abs(z)))


def _neg_expm1(y):
    series = -y * (1.0 + y * 0.5 * (1.0 + y * (1.0 / 3.0) * (1.0 + y * 0.25 * (1.0 + y * 0.2 * (
        1.0 + y * (1.0 / 6.0) * (1.0 + y * (1.0 / 7.0)))))))
    return jnp.where(y > -0.25, series, 1.0 - jnp.exp(y))


def _gelu_parts(x):
    c = 0.7978845608028654
    u = c * (x + 0.044715 * x * x * x)
    th = jnp.tanh(u)
    gel = 0.5 * x * (1.0 + th)
    dgel = 0.5 * (1.0 + th) + 0.5 * x * (1.0 - th * th) * c * (1.0 + 3.0 * 0.044715 * x * x)
    return gel, dgel


def _shift_down(v, k, rows):
    return jnp.where(rows < k, 0.0, pltpu.roll(v, k, 0))


def _shift_up(v, k, rows, n):
    return jnp.where(rows >= n - k, 0.0, pltpu.roll(v, n - k, 0))


def _rnn_gates(xc, wrg, brg, wig, big, lam):
    xcb = xc.astype(BF16)
    r = _sig(jnp.dot(xcb, wrg, preferred_element_type=F32) + brg)
    i = _sig(jnp.dot(xcb, wig, preferred_element_type=F32) + big)
    sp = _softplus_neg(lam)
    log_a = -LRU_C * r * sp
    a = jnp.exp(log_a)
    mult = jnp.sqrt(_neg_expm1(2.0 * log_a))
    return xcb, r, i, sp, a, mult


def _conv_fwd(xv, cw, cb, rows):
    return (cb + _shift_down(xv, 3, rows) * cw[0:1, :] + _shift_down(xv, 2, rows) * cw[1:2, :]
            + _shift_down(xv, 1, rows) * cw[2:3, :] + xv * cw[3:4, :])


def _rnn_fwd(z, conv_w, conv_b, wrg_bd, b_rg, wig_bd, b_ig, lam, *, n_seq, seq, hosted=None):
    t = n_seq * seq
    ct = RNN_TILE
    n_ct = D_MODEL // ct

    def body(x_ref, g_ref, cw_ref, cb_ref, wrg_ref, brg_ref, wig_ref, big_ref, lam_ref,
             xc_ref, hr_ref, ya_ref, a_s, b_s):
        rows = lax.broadcasted_iota(jnp.int32, (seq, ct), 0)
        xc = _conv_fwd(x_ref[...], cw_ref[...], cb_ref[...], rows)
        _, r, i, sp, a, mult = _rnn_gates(xc, wrg_ref[...], brg_ref[...], wig_ref[...], big_ref[...], lam_ref[...])
        a_s[...] = a
        b_s[...] = mult * (i * xc)

        def step(s, h):
            h = a_s[pl.ds(s, 1), :] * h + b_s[pl.ds(s, 1), :]
            hr_ref[pl.ds(s, 1), :] = h
            return h

        lax.fori_loop(0, seq, step, jnp.zeros((1, ct), F32), unroll=8)
        gel, _ = _gelu_parts(g_ref[...])
        xc_ref[...] = xc
        ya_ref[...] = (hr_ref[...] * gel).astype(BF16)

    vec = pl.BlockSpec((1, ct), lambda b, c: (0, c))
    gate_w = pl.BlockSpec((None, ct, ct), lambda b, c: (c, 0, 0))
    tile = pl.BlockSpec((seq, ct), lambda b, c: (b, c))
    outs, landed = _call(
        body,
        grid=(n_seq, n_ct),
        in_specs=[
            pl.BlockSpec((seq, ct), lambda b, c: (b, COL_XRNN // ct + c)),
            pl.BlockSpec((seq, ct), lambda b, c: (b, COL_GRNN // ct + c)),
            pl.BlockSpec((CONV_W, ct), lambda b, c: (0, c)), vec, gate_w, vec, gate_w, vec, vec,
        ],
        out_specs=[tile, tile, tile],
        out_shape=[jax.ShapeDtypeStruct((t, D_MODEL), F32), jax.ShapeDtypeStruct((t, D_MODEL), F32),
                   jax.ShapeDtypeStruct((t, D_MODEL), BF16)],
        scratch_shapes=[pltpu.VMEM((seq, ct), F32), pltpu.VMEM((seq, ct), F32)],
        args=(z, z, conv_w, conv_b, wrg_bd, b_rg, wig_bd, b_ig, lam), name="rnn_fwd",
        semantics=("parallel", "parallel"), hosted=hosted)
    return (*outs, landed) if hosted is not None else outs


def _rnn_bwd(dya, z, xc, hr, conv_w, wrg_bd, b_rg, wig_bd, b_ig, lam, *, n_seq, seq, hosted=None):
    t = n_seq * seq
    ct = RNN_TILE
    n_ct = D_MODEL // ct

    def body(dya_ref, x_ref, g_ref, xc_ref, hr_ref, cw_ref, wrg_ref, brg_ref, wig_ref, big_ref, lam_ref,
             dx_ref, dg_ref, dwrg_ref, dwig_ref, vec_ref, a_s, d_s, g_s):
        rows = lax.broadcasted_iota(jnp.int32, (seq, ct), 0)
        xv, xc, hr, dyv = x_ref[...], xc_ref[...], hr_ref[...], dya_ref[...]
        lamv = lam_ref[...]
        gel, dgel = _gelu_parts(g_ref[...])
        dg_ref[...] = (dyv * hr * dgel).astype(BF16)
        xcb, r, i, sp, a, mult = _rnn_gates(xc, wrg_ref[...], brg_ref[...], wig_ref[...], big_ref[...], lamv)
        a_s[...] = a
        d_s[...] = dyv * gel

        def step(k, c):
            s = seq - 1 - k
            gs = d_s[pl.ds(s, 1), :] + c
            g_s[pl.ds(s, 1), :] = gs
            return a_s[pl.ds(s, 1), :] * gs

        lax.fori_loop(0, seq, step, jnp.zeros((1, ct), F32), unroll=8)
        gsum = g_s[...]
        gated = i * xc
        d_log_a = gsum * _shift_down(hr, 1, rows) * a - gsum * gated * (a * a / mult)
        d_gated = gsum * mult
        d_pre_r = (d_log_a * (-LRU_C) * sp) * r * (1.0 - r)
        d_pre_i = (d_gated * xc) * i * (1.0 - i)
        dprb, dpib = d_pre_r.astype(BF16), d_pre_i.astype(BF16)
        dxc = d_gated * i + _dot_nt(dprb, wrg_ref[...]) + _dot_nt(dpib, wig_ref[...])
        cw = cw_ref[...]
        dx = (dxc * cw[3:4, :] + _shift_up(dxc, 1, rows, seq) * cw[2:3, :]
              + _shift_up(dxc, 2, rows, seq) * cw[1:2, :] + _shift_up(dxc, 3, rows, seq) * cw[0:1, :])
        dx_ref[...] = dx.astype(BF16)

        @pl.when(pl.program_id(1) == 0)
        def _():
            dwrg_ref[...] = jnp.zeros_like(dwrg_ref)
            dwig_ref[...] = jnp.zeros_like(dwig_ref)
            vec_ref[...] = jnp.zeros_like(vec_ref)

        dwrg_ref[...] += _dot_tn(xcb, dprb)
        dwig_ref[...] += _dot_tn(xcb, dpib)

        def colsum(v):
            return jnp.sum(v, axis=0, keepdims=True)

        d_sp = colsum(d_log_a * (-LRU_C) * r)
        vec_ref[0:1, :] += colsum(d_pre_r)
        vec_ref[1:2, :] += colsum(d_pre_i)
        vec_ref[2:3, :] += d_sp * (-_sig(-lamv))
        vec_ref[3:4, :] += colsum(dxc)
        vec_ref[4:5, :] += colsum(dxc * _shift_down(xv, 3, rows))
        vec_ref[5:6, :] += colsum(dxc * _shift_down(xv, 2, rows))
        vec_ref[6:7, :] += colsum(dxc * _shift_down(xv, 1, rows))
        vec_ref[7:8, :] += colsum(dxc * xv)

    vec = pl.BlockSpec((1, ct), lambda c, b: (0, c))
    gate_w = pl.BlockSpec((None, ct, ct), lambda c, b: (c, 0, 0))
    tile = pl.BlockSpec((seq, ct), lambda c, b: (b, c))
    outs, landed = _call(
        body,
        grid=(n_ct, n_seq),
        in_specs=[
            tile,
            pl.BlockSpec((seq, ct), lambda c, b: (b, COL_XRNN // ct + c)),
            pl.BlockSpec((seq, ct), lambda c, b: (b, COL_GRNN // ct + c)),
            tile, tile,
            pl.BlockSpec((CONV_W, ct), lambda c, b: (0, c)), gate_w, vec, gate_w, vec, vec,
        ],
        out_specs=[tile, tile, gate_w, gate_w, pl.BlockSpec((8, ct), lambda c, b: (0, c))],
        out_shape=[jax.ShapeDtypeStruct((t, D_MODEL), BF16), jax.ShapeDtypeStruct((t, D_MODEL), BF16),
                   jax.ShapeDtypeStruct((n_ct, ct, ct), F32), jax.ShapeDtypeStruct((n_ct, ct, ct), F32),
                   jax.ShapeDtypeStruct((8, D_MODEL), F32)],
        scratch_shapes=[pltpu.VMEM((seq, ct), F32)] * 3,
        args=(dya, z, z, xc, hr, conv_w, wrg_bd, b_rg, wig_bd, b_ig, lam), name="rnn_bwd",
        semantics=("parallel", "arbitrary"), hosted=hosted)
    return (*outs, landed) if hosted is not None else outs


def _split_hi_lo(x):
    hi = x.astype(BF16)
    return hi, (x - hi.astype(F32)).astype(BF16)


def _dot_split(x, m):
    hi, lo = _split_hi_lo(x)
    return jnp.dot(hi, m, preferred_element_type=F32) + jnp.dot(lo, m, preferred_element_type=F32)


def _head_matrices(width):
    ec = (lax.broadcasted_iota(jnp.int32, (width, LANES), 0) // HEAD_DIM
          == lax.broadcasted_iota(jnp.int32, (width, LANES), 1))
    ee = (lax.broadcasted_iota(jnp.int32, (LANES, width), 1) // HEAD_DIM
          == lax.broadcasted_iota(jnp.int32, (LANES, width), 0))
    return jnp.where(ec, 1.0, 0.0).astype(BF16), jnp.where(ee, 1.0, 0.0).astype(BF16)


def _swap_halves(y):
    w = y.shape[1]
    first = (lax.broadcasted_iota(jnp.int32, y.shape, 1) % HEAD_DIM) < HEAD_DIM // 2
    return jnp.where(first, pltpu.roll(y, w - HEAD_DIM // 2, 1), pltpu.roll(y, HEAD_DIM // 2, 1))


def _normrope_fwd(x, gain, cos_t, sin_t, ec, ee):
    w = x.shape[1]
    rs = _dot_split(lax.rsqrt(_dot_split(x * x, ec) * (1.0 / HEAD_DIM) + NORM_EPS), ee)
    nx = x * rs
    y = nx * gain
    reps = w // LANES
    out = y * jnp.tile(cos_t, (1, reps)) + _swap_halves(y) * jnp.tile(sin_t, (1, reps))
    return out, nx, rs


def _normrope_bwd(dout, nx, rs, gain, cos_t, sin_t, ec, ee):
    w = dout.shape[1]
    reps = w // LANES
    dy = dout * jnp.tile(cos_t, (1, reps)) + _swap_halves(dout * jnp.tile(sin_t, (1, reps)))
    dgain = jnp.sum(dy * nx, axis=0, keepdims=True)
    dn = dy * gain
    seg = _dot_split(_dot_split(dn * nx, ec) * (1.0 / HEAD_DIM), ee)
    return rs * (dn - nx * seg), dgain


def _pair_operand(t, group):
    chunk = t[:, (group // 2) * LANES:(group // 2 + 1) * LANES]
    low = lax.broadcasted_iota(jnp.int32, chunk.shape, 1) < HEAD_DIM
    rolled = pltpu.roll(chunk, HEAD_DIM, 1)
    return jnp.where(low, chunk, rolled) if group % 2 == 0 else jnp.where(low, rolled, chunk)


def _window_mask(nq, nk, is_prev):
    qi = lax.broadcasted_iota(jnp.int32, (nq, nk), 0)
    ci = lax.broadcasted_iota(jnp.int32, (nq, nk), 1)
    return ci > qi if is_prev else ci <= qi


def _attn_fwd(z, cos_t, sin_t, q_gain_t, k_gain_t, sinks_t, *, n_seq, seq, hosted=None):
    t = n_seq * seq
    blk = WINDOW
    nb = seq // blk

    def body(q_ref, kp_ref, kc_ref, vp_ref, vc_ref, cosc_ref, sinc_ref, cosp_ref, sinp_ref, qg_ref, kg_ref, sk_ref,
             o_ref, l_ref):
        n = pl.program_id(1)
        ecq, eeq = _head_matrices(D_MODEL)
        eck, eek = _head_matrices(KV_W)
        cosc, sinc = cosc_ref[...], sinc_ref[...]
        qh, _, _ = _normrope_fwd(q_ref[...], qg_ref[...], cosc, sinc, ecq, eeq)
        kc, _, _ = _normrope_fwd(kc_ref[...], kg_ref[...], cosc, sinc, eck, eek)
        kp, _, _ = _normrope_fwd(kp_ref[...], kg_ref[...], cosp_ref[...], sinp_ref[...], eck, eek)
        kcat = jnp.concatenate([kp, kc], axis=0)
        vcat = jnp.concatenate([vp_ref[...], vc_ref[...]], axis=0)
        valid = jnp.concatenate([_window_mask(blk, blk, True) & (n > 0), _window_mask(blk, blk, False)], axis=1)
        low = lax.broadcasted_iota(jnp.int32, (blk, LANES), 1) < HEAD_DIM
        lane = lax.broadcasted_iota(jnp.int32, (blk, LANES), 1)
        sk = sk_ref[...]
        lmat = jnp.zeros((blk, LANES), F32)
        for group in range(N_KV_HEADS):
            k2 = _pair_operand(kcat, group).astype(BF16)
            v2 = _pair_operand(vcat, group).astype(BF16)
            for pp in range(2):
                pair = 2 * group + pp
                qp = qh[:, pair * LANES:(pair + 1) * LANES]
                outs = []
                for half in range(2):
                    head = 2 * pair + half
                    qm = jnp.where(low if half == 0 else ~low, qp, 0.0).astype(BF16)
                    s = jnp.where(valid, _dot_nt(qm, k2) * (HEAD_DIM ** -0.5), NEG_BIG)
                    sink = sk[:, head:head + 1]
                    m = jnp.maximum(jnp.max(s, axis=-1, keepdims=True), sink)
                    e = jnp.exp(s - m)
                    den = jnp.sum(e, axis=-1, keepdims=True) + jnp.exp(sink - m)
                    outs.append(jnp.dot((e / den).astype(BF16), v2, preferred_element_type=F32))
                    lmat = lmat + jnp.where(lane == head, m + jnp.log(den), 0.0)
                o_ref[:, pair * LANES:(pair + 1) * LANES] = jnp.where(low, outs[0], outs[1]).astype(BF16)
        l_ref[...] = lmat

    def row(b, n):
        return b * nb + n

    def prev(b, n):
        return b * nb + jnp.maximum(n - 1, 0)

    kw = KV_W
    tab_c = pl.BlockSpec((blk, LANES), lambda b, n: (n, 0))
    tab_p = pl.BlockSpec((blk, LANES), lambda b, n: (jnp.maximum(n - 1, 0), 0))
    outs, landed = _call(
        body,
        grid=(n_seq, nb),
        in_specs=[
            pl.BlockSpec((blk, D_MODEL), lambda b, n: (row(b, n), COL_Q // D_MODEL)),
            pl.BlockSpec((blk, kw), lambda b, n: (prev(b, n), COL_K // kw)),
            pl.BlockSpec((blk, kw), lambda b, n: (row(b, n), COL_K // kw)),
            pl.BlockSpec((blk, kw), lambda b, n: (prev(b, n), COL_V // kw)),
            pl.BlockSpec((blk, kw), lambda b, n: (row(b, n), COL_V // kw)),
            tab_c, tab_c, tab_p, tab_p,
            pl.BlockSpec((1, D_MODEL), lambda b, n: (0, 0)),
            pl.BlockSpec((1, kw), lambda b, n: (0, 0)),
            pl.BlockSpec((1, LANES), lambda b, n: (0, 0)),
        ],
        out_specs=[pl.BlockSpec((blk, D_MODEL), lambda b, n: (row(b, n), 0)),
                   pl.BlockSpec((blk, LANES), lambda b, n: (row(b, n), 0))],
        out_shape=[jax.ShapeDtypeStruct((t, D_MODEL), BF16), jax.ShapeDtypeStruct((t, LANES), F32)],
        args=(z, z, z, z, z, cos_t, sin_t, cos_t, sin_t, q_gain_t, k_gain_t, sinks_t), name="attn_fwd",
        semantics=("parallel", "parallel"), hosted=hosted)
    return (*outs, landed) if hosted is not None else outs


def _attn_bwd(z, o, lse, do, cos_t, sin_t, q_gain_t, k_gain_t, sinks_t, *, n_seq, seq, hosted=None):
    t = n_seq * seq
    blk = WINDOW
    nb = seq // blk
    kw = KV_W
    scale = HEAD_DIM ** -0.5

    def body(qc_ref, qn_ref, kp_ref, kc_ref, vp_ref, vc_ref, oc_ref, on_ref, doc_ref, don_ref, lc_ref, ln_ref,
             cosc_ref, sinc_ref, cosp_ref, sinp_ref, cosn_ref, sinn_ref, qg_ref, kg_ref, sk_ref,
             dq_ref, dk_ref, dv_ref, vec_ref):
        n = pl.program_id(1)
        ecq, eeq = _head_matrices(D_MODEL)
        eck, eek = _head_matrices(KV_W)
        cosc, sinc = cosc_ref[...], sinc_ref[...]
        qg, kg = qg_ref[...], kg_ref[...]
        qhc, nqc, rsqc = _normrope_fwd(qc_ref[...], qg, cosc, sinc, ecq, eeq)
        qhn, _, _ = _normrope_fwd(qn_ref[...], qg, cosn_ref[...], sinn_ref[...], ecq, eeq)
        khc, nkc, rskc = _normrope_fwd(kc_ref[...], kg, cosc, sinc, eck, eek)
        khp, _, _ = _normrope_fwd(kp_ref[...], kg, cosp_ref[...], sinp_ref[...], eck, eek)
        doc = doc_ref[...].astype(F32)
        don = don_ref[...].astype(F32)
        delc = _dot_split(doc * oc_ref[...].astype(F32), ecq)
        deln = _dot_split(don * on_ref[...].astype(F32), ecq)
        lc, ln = lc_ref[...], ln_ref[...]
        vis_prev = _window_mask(blk, blk, True)
        vis_same = _window_mask(blk, blk, False)
        mask_a = vis_prev & (n > 0)
        mask_c = vis_prev & (n < nb - 1)
        low = lax.broadcasted_iota(jnp.int32, (blk, LANES), 1) < HEAD_DIM
        lane = lax.broadcasted_iota(jnp.int32, (1, LANES), 1)
        sk = sk_ref[...]
        dsink = jnp.zeros((1, LANES), F32)
        dk_chunks, dv_chunks = [], []
        dk_pair, dv_pair = [], []
        for group in range(N_KV_HEADS):
            k2c = _pair_operand(khc, group).astype(BF16)
            k2p = _pair_operand(khp, group).astype(BF16)
            v2c = _pair_operand(vc_ref[...], group).astype(BF16)
            v2p = _pair_operand(vp_ref[...], group).astype(BF16)
            dk2 = jnp.zeros((blk, LANES), F32)
            dv2 = jnp.zeros((blk, LANES), F32)
            for pp in range(2):
                pair = 2 * group + pp
                lanes = slice(pair * LANES, (pair + 1) * LANES)
                dq_halves = []
                for half in range(2):
                    head = 2 * pair + half
                    sel = low if half == 0 else ~low
                    qcm = jnp.where(sel, qhc[:, lanes], 0.0).astype(BF16)
                    qnm = jnp.where(sel, qhn[:, lanes], 0.0).astype(BF16)
                    docm = jnp.where(sel, doc[:, lanes], 0.0).astype(BF16)
                    donm = jnp.where(sel, don[:, lanes], 0.0).astype(BF16)
                    l_c, l_n = lc[:, head:head + 1], ln[:, head:head + 1]
                    d_c, d_n = delc[:, head:head + 1], deln[:, head:head + 1]

                    def probs(qm, k2, lrow, mask):
                        return jnp.where(mask, jnp.exp(_dot_nt(qm, k2) * scale - lrow), 0.0)

                    p_a = probs(qcm, k2p, l_c, mask_a)
                    p_b = probs(qcm, k2c, l_c, vis_same)
                    p_c = probs(qnm, k2c, l_n, mask_c)
                    ds_a = (p_a * (_dot_nt(docm, v2p) - d_c)).astype(BF16)
                    ds_b = (p_b * (_dot_nt(docm, v2c) - d_c)).astype(BF16)
                    ds_c = (p_c * (_dot_nt(donm, v2c) - d_n)).astype(BF16)
                    dq_halves.append((jnp.dot(ds_a, k2p, preferred_element_type=F32)
                                      + jnp.dot(ds_b, k2c, preferred_element_type=F32)) * scale)
                    dk2 = dk2 + (_dot_tn(ds_b, qcm) + _dot_tn(ds_c, qnm)) * scale
                    dv2 = dv2 + _dot_tn(p_b.astype(BF16), docm) + _dot_tn(p_c.astype(BF16), donm)
                    p_sink = jnp.exp(sk[:, head:head + 1] - l_c)
                    dsink = dsink + jnp.where(lane == head, -jnp.sum(p_sink * d_c, axis=0, keepdims=True), 0.0)
                dq_ref[:, lanes] = jnp.where(low, dq_halves[0], dq_halves[1])
            dk_pair.append(dk2 + pltpu.roll(dk2, HEAD_DIM, 1))
            dv_pair.append(dv2 + pltpu.roll(dv2, HEAD_DIM, 1))
            if group % 2 == 1:
                dk_chunks.append(jnp.where(low, dk_pair[-2], dk_pair[-1]))
                dv_chunks.append(jnp.where(low, dv_pair[-2], dv_pair[-1]))
        dkh = jnp.concatenate(dk_chunks, axis=1)
        dv_ref[...] = jnp.concatenate(dv_chunks, axis=1).astype(BF16)
        dq, dqg = _normrope_bwd(dq_ref[...], nqc, rsqc, qg, cosc, sinc, ecq, eeq)
        dk, dkg = _normrope_bwd(dkh, nkc, rskc, kg, cosc, sinc, eck, eek)
        dq_ref[...] = dq
        dk_ref[...] = dk.astype(BF16)

        @pl.when(n == 0)
        def _():
            vec_ref[...] = jnp.zeros_like(vec_ref)

        vec_ref[0:1, :] += dqg
        vec_ref[1:2, 0:kw] += dkg
        vec_ref[2:3, 0:LANES] += dsink

    def row(b, n):
        return b * nb + n

    def prev(b, n):
        return b * nb + jnp.maximum(n - 1, 0)

    def nxt(b, n):
        return b * nb + jnp.minimum(n + 1, nb - 1)

    def tiles(width, col, which):
        return pl.BlockSpec((blk, width), lambda b, n: (which(b, n), col))

    def table(which):
        return pl.BlockSpec((blk, LANES), lambda b, n: (which(0, n), 0))

    outs, landed = _call(
        body,
        grid=(n_seq, nb),
        in_specs=[
            tiles(D_MODEL, COL_Q // D_MODEL, row), tiles(D_MODEL, COL_Q // D_MODEL, nxt),
            tiles(kw, COL_K // kw, prev), tiles(kw, COL_K // kw, row),
            tiles(kw, COL_V // kw, prev), tiles(kw, COL_V // kw, row),
            tiles(D_MODEL, 0, row), tiles(D_MODEL, 0, nxt),
            tiles(D_MODEL, 0, row), tiles(D_MODEL, 0, nxt),
            tiles(LANES, 0, row), tiles(LANES, 0, nxt),
            table(row), table(row), table(prev), table(prev), table(nxt), table(nxt),
            pl.BlockSpec((1, D_MODEL), lambda b, n: (0, 0)),
            pl.BlockSpec((1, kw), lambda b, n: (0, 0)),
            pl.BlockSpec((1, LANES), lambda b, n: (0, 0)),
        ],
        out_specs=[tiles(D_MODEL, 0, row), tiles(kw, 0, row), tiles(kw, 0, row),
                   pl.BlockSpec((None, 8, D_MODEL), lambda b, n: (b, 0, 0))],
        out_shape=[jax.ShapeDtypeStruct((t, D_MODEL), F32), jax.ShapeDtypeStruct((t, kw), BF16),
                   jax.ShapeDtypeStruct((t, kw), BF16), jax.ShapeDtypeStruct((n_seq, 8, D_MODEL), F32)],
        args=(z, z, z, z, z, z, o, o, do, do, lse, lse, cos_t, sin_t, cos_t, sin_t, cos_t, sin_t,
              q_gain_t, k_gain_t, sinks_t), name="attn_bwd", semantics=("parallel", "arbitrary"), hosted=hosted)
    return (*outs, landed) if hosted is not None else outs


MERGE_COLS = 512


def _merge_fwd(z, ya, yb):
    t = ya.shape[0]
    tm, tc = min(512, t), MERGE_COLS

    def body(ga_ref, gb_ref, ya_ref, yb_ref, o_ref):
        o_ref[...] = (_sig(ga_ref[...]) * ya_ref[...] + _sig(gb_ref[...]) * yb_ref[...]).astype(BF16)

    tile = pl.BlockSpec((tm, tc), lambda i, j: (i, j))
    return pl.pallas_call(
        body,
        grid=(t // tm, D_MODEL // tc),
        in_specs=[pl.BlockSpec((tm, tc), lambda i, j: (i, COL_GA // tc + j)),
                  pl.BlockSpec((tm, tc), lambda i, j: (i, COL_GB // tc + j)), tile, tile],
        out_specs=tile,
        out_shape=jax.ShapeDtypeStruct((t, D_MODEL), BF16),
        compiler_params=_params("parallel", "parallel"),
        name="merge_fwd",
    )(z, z, ya, yb)


def _merge_bwd(z, ya, yb, dmerged):
    t = ya.shape[0]
    tm, tc = min(512, t), MERGE_COLS

    def body(ga_ref, gb_ref, ya_ref, yb_ref, dm_ref, dya_ref, dyb_ref, dga_ref, dgb_ref):
        dm = dm_ref[...]
        sa, sb = _sig(ga_ref[...]), _sig(gb_ref[...])
        dya_ref[...] = (dm * sa).astype(BF16)
        dyb_ref[...] = (dm * sb).astype(BF16)
        dga_ref[...] = (dm * ya_ref[...] * sa * (1.0 - sa)).astype(BF16)
        dgb_ref[...] = (dm * yb_ref[...] * sb * (1.0 - sb)).astype(BF16)

    tile = pl.BlockSpec((tm, tc), lambda i, j: (i, j))
    return pl.pallas_call(
        body,
        grid=(t // tm, D_MODEL // tc),
        in_specs=[pl.BlockSpec((tm, tc), lambda i, j: (i, COL_GA // tc + j)),
                  pl.BlockSpec((tm, tc), lambda i, j: (i, COL_GB // tc + j)), tile, tile, tile],
        out_specs=[tile] * 4,
        out_shape=[jax.ShapeDtypeStruct((t, D_MODEL), BF16)] * 4,
        compiler_params=_params("parallel", "parallel"),
        name="merge_bwd",
    )(z, z, ya, yb, dmerged)


def _loss_head(x2, e, gt, target):
    t, d = x2.shape
    tm = min(256, t)

    def body(x_ref, e_ref, gt_ref, tg_ref, loss_ref, dx_ref, dgt_ref, de_ref):
        ev = e_ref[...]
        sg = _sig(gt_ref[...])
        diff = x_ref[...] + ev * sg - tg_ref[...]
        dx = diff * (1.0 / d)
        dx_ref[...] = dx
        dgt_ref[...] = (dx * ev * sg * (1.0 - sg)).astype(BF16)
        de_ref[...] = (dx * sg).astype(BF16)

        @pl.when(pl.program_id(0) == 0)
        def _():
            loss_ref[...] = jnp.zeros_like(loss_ref)

        loss_ref[...] += jnp.sum(jnp.sum(diff * diff, axis=1, keepdims=True), axis=0, keepdims=True)

    tile = pl.BlockSpec((tm, d), lambda i: (i, 0))
    return pl.pallas_call(
        body,
        grid=(t // tm,),
        in_specs=[tile] * 4,
        out_specs=[pl.BlockSpec((1, LANES), lambda i: (0, 0)), tile, tile, tile],
        out_shape=[jax.ShapeDtypeStruct((1, LANES), F32), jax.ShapeDtypeStruct((t, d), F32),
                   jax.ShapeDtypeStruct((t, d), BF16), jax.ShapeDtypeStruct((t, d), BF16)],
        compiler_params=_params("arbitrary"),
        name="loss_head",
    )(x2, e, gt, target)


def _rope_tables(seq):
    inv = ROPE_THETA ** (-jnp.arange(0, HEAD_DIM, 2, dtype=F32) / HEAD_DIM)
    ang = jnp.arange(seq, dtype=F32)[:, None] * inv[None, :]
    cos, sin = jnp.cos(ang), jnp.sin(ang)
    return jnp.tile(jnp.concatenate([cos, cos], axis=1), (1, 2)), jnp.tile(jnp.concatenate([-sin, sin], axis=1), (1, 2))


def _block_diag_tiles(w):
    per = RNN_TILE // RNN_BLOCK_W
    w4 = w.reshape(D_MODEL // RNN_TILE, per, RNN_BLOCK_W, RNN_BLOCK_W)
    eye = jnp.eye(per, dtype=w.dtype)
    dense = jnp.einsum("tpij,pq->tpiqj", w4, eye)
    return dense.reshape(D_MODEL // RNN_TILE, RNN_TILE, RNN_TILE).astype(BF16)


def _block_diag_extract(dense):
    per = RNN_TILE // RNN_BLOCK_W
    d5 = dense.reshape(D_MODEL // RNN_TILE, per, RNN_BLOCK_W, per, RNN_BLOCK_W)
    blocks = jnp.stack([d5[:, p, :, p, :] for p in range(per)], axis=1)
    return blocks.reshape(D_MODEL // RNN_BLOCK_W, RNN_BLOCK_W, RNN_BLOCK_W)


def _local_step(x, p, target, w, *, n_seq, seq, comm=None):
    w = dict(w)

    def run(tag, fn, *args, **kwargs):
        hosted = comm.host(tag) if comm is not None else None
        if hosted is None:
            return fn(*args, **kwargs)
        *outs, landed = fn(*args, hosted=hosted, **kwargs)
        comm.landed(tag, landed, w)
        return outs[0] if len(outs) == 1 else outs

    def ready(batch, grads):
        if comm is not None:
            comm.ready(batch, grads)

    cos_t, sin_t = _rope_tables(seq)
    q_gain_t = jnp.tile(w["q_gain"], (1, N_Q_HEADS))
    k_gain_t = jnp.tile(w["k_gain"], (1, N_KV_HEADS))
    sinks_t = jnp.pad(w["sinks"], ((0, 0), (0, LANES - N_Q_HEADS)))
    wrg_bd, wig_bd = _block_diag_tiles(w["w_rg"]), _block_diag_tiles(w["w_ig"])
    dims = dict(n_seq=n_seq, seq=seq)

    h = _rmsnorm_fwd(x, w["g_mix"], name="norm_mix")
    z = run("mm_in", _matmul, h, w["w_in"], mode="nn", tm=1024, tn=512, out_dtypes=[F32], name="mm_in")
    xc, hr, ya_in = run("rnn_fwd", _rnn_fwd, z, w["conv_w"], w["conv_b"], wrg_bd, w["b_rg"], wig_bd, w["b_ig"],
                        w["lru_lambda"], **dims)
    o, lse = run("attn_fwd", _attn_fwd, z, cos_t, sin_t, q_gain_t, k_gain_t, sinks_t, **dims)
    ya = _matmul(ya_in, w["w_rnn_proj"], mode="nn", tm=1024, tn=512, out_dtypes=[F32], name="mm_rnn_proj")
    yb = _matmul(o, w["w_attn_proj"], mode="nn", tm=1024, tn=512, out_dtypes=[F32], name="mm_attn_proj")
    merged = _merge_fwd(z, ya, yb)
    x1 = _matmul(merged, w["w_out"], mode="nn", tm=1024, tn=512, out_dtypes=[F32], name="mm_out",
                 epilogue=lambda acc, res: (res + acc,), extras=(x,))
    hm = _rmsnorm_fwd(x1, w["g_mlp"], name="norm_mlp")
    u, act = _matmul(hm, w["w_up"], mode="nn", tm=1024, tn=512, out_dtypes=[F32, BF16], name="mm_up",
                     epilogue=lambda acc: (acc, jnp.square(jnp.maximum(acc, 0.0))))
    x2 = _matmul(act, w["w_down"], mode="nn", tm=512, tn=512, out_dtypes=[F32], name="mm_down",
                 epilogue=lambda acc, res: (res + acc,), extras=(x1,))
    hp = _rmsnorm_fwd(x2, w["g_ple"], name="norm_ple")
    gt = _matmul(hp, w["w_ple_gate"], mode="nn", tm=1024, tn=512, out_dtypes=[F32], name="mm_ple_gate")
    p_bf = p.astype(BF16)
    e = _matmul(p_bf, w["w_ple_proj"], mode="nn", tm=1024, tn=512, out_dtypes=[F32], name="mm_ple_proj")
    loss_row, dx3, dgt, de = _loss_head(x2, e, gt, target)

    g = {}
    g["w_ple_proj"] = _matmul_tn(p_bf, de, tk=PLE_DIM, tn=LANES, tt=512, name="mm_d_ple_proj", stacked=True)
    g["w_ple_gate"] = _matmul_tn(hp, dgt, tk=1024, tn=1024, tt=512, name="mm_d_ple_gate")
    dhp = _matmul(dgt, w["w_ple_gate"], mode="nt", tm=1024, tn=512, out_dtypes=[F32], name="mm_dhp")
    dx2, dx2_bf, g["g_ple"] = _rmsnorm_bwd(x2, w["g_ple"], dhp, dx3, name="norm_ple_bwd", want_bf16=True)
    g["w_down"] = _matmul_tn(act, dx2_bf, tk=1024, tn=1024, tt=512, name="mm_d_down")
    du = _matmul(dx2_bf, w["w_down"], mode="nt", tm=1024, tn=512, out_dtypes=[BF16], name="mm_dact",
                 epilogue=lambda acc, uu: (acc * (2.0 * jnp.maximum(uu, 0.0)),), extras=(u,))
    g["w_up"] = _matmul_tn(hm, du, tk=1024, tn=D_FF // N_DEV, tt=512, name="mm_d_up", stacked=True)
    ready(1, g)
    dhm = run("mm_dhm", _matmul, du, w["w_up"], mode="nt", tm=512, tn=512, out_dtypes=[F32], name="mm_dhm")
    dx1, dx1_bf, g["g_mlp"] = _rmsnorm_bwd(x1, w["g_mlp"], dhm, dx2, name="norm_mlp_bwd", want_bf16=True)
    g["w_out"] = _matmul_tn(merged, dx1_bf, tk=1024, tn=1024, tt=512, name="mm_d_out")
    dmerged = _matmul(dx1_bf, w["w_out"], mode="nt", tm=1024, tn=512, out_dtypes=[F32], name="mm_dmerged")
    dya, dyb, dga, dgb = _merge_bwd(z, ya, yb, dmerged)
    g["w_rnn_proj"] = _matmul_tn(ya_in, dya, tk=1024, tn=1024, tt=512, name="mm_d_rnn_proj")
    g["w_attn_proj"] = _matmul_tn(o, dyb, tk=1024, tn=1024, tt=512, name="mm_d_attn_proj")
    ready(2, g)
    dya_in = run("mm_dya_in", _matmul, dya, w["w_rnn_proj"], mode="nt", tm=1024, tn=512, out_dtypes=[F32],
                 name="mm_dya_in")
    do = _matmul(dyb, w["w_attn_proj"], mode="nt", tm=1024, tn=512, out_dtypes=[BF16], name="mm_do")
    dx_rnn, dg_rnn, dwrg_dense, dwig_dense, rnn_vec = run(
        "rnn_bwd", _rnn_bwd, dya_in, z, xc, hr, w["conv_w"], wrg_bd, w["b_rg"], wig_bd, w["b_ig"], w["lru_lambda"],
        **dims)
    dq, dk, dv, attn_vec = run("attn_bwd", _attn_bwd, z, o, lse, do, cos_t, sin_t, q_gain_t, k_gain_t, sinks_t,
                               **dims)
    dz = jnp.concatenate([dx_rnn, dg_rnn, dq.astype(BF16), dk, dv, dga, dgb], axis=1)
    g["w_in"] = _matmul_tn(h, dz, tk=1024, tn=IN_TOTAL // 4, tt=512, name="mm_d_in")
    g["w_rg"] = _block_diag_extract(dwrg_dense)
    g["w_ig"] = _block_diag_extract(dwig_dense)
    g["b_rg"], g["b_ig"], g["lru_lambda"], g["conv_b"] = (rnn_vec[i:i + 1] for i in range(4))
    g["conv_w"] = rnn_vec[4:8]
    attn_vec = attn_vec[0] if n_seq == 1 else functools.reduce(jnp.add, [attn_vec[b] for b in range(n_seq)])
    g["q_gain"] = attn_vec[0].reshape(N_Q_HEADS, HEAD_DIM).sum(axis=0)[None, :]
    g["k_gain"] = attn_vec[1, :KV_W].reshape(N_KV_HEADS, HEAD_DIM).sum(axis=0)[None, :]
    g["sinks"] = attn_vec[2:3, :N_Q_HEADS]
    ready(3, g)
    dh = run("mm_dh", _matmul, dz, w["w_in"], mode="nt", tm=512, tn=512, out_dtypes=[F32], name="mm_dh")
    grad_x, g["g_mix"] = run("norm_mix_bwd", _rmsnorm_bwd, x, w["g_mix"], dh, dx1, name="norm_mix_bwd",
                             want_bf16=False)
    return loss_row[0, 0], grad_x, g


MESH_ID = pl.DeviceIdType.MESH


def _coords(index):
    return (index >> 2) & 1, (index >> 1) & 1, index & 1


def _exchange(srcs, kinds, *, name):
    n = len(srcs)
    n_peer = N_DEV - 1

    def body(*refs):
        src, dst = refs[:n], refs[n:2 * n]
        send_sems, recv_sems, local_sems = refs[2 * n:]
        me = 4 * lax.axis_index("x") + 2 * lax.axis_index("y") + lax.axis_index("c")

        def remote(i, d):
            peer = (me + d) & (N_DEV - 1)
            piece = src[i] if kinds[i] == "gather" else src[i].at[peer]
            return pltpu.make_async_remote_copy(
                src_ref=piece, dst_ref=dst[i].at[me], send_sem=send_sems.at[i * n_peer + d - 1],
                recv_sem=recv_sems.at[i * n_peer + d - 1], device_id=_coords(peer), device_id_type=MESH_ID)

        def arrival(i, d):
            sender = (me - d) & (N_DEV - 1)
            piece = src[i] if kinds[i] == "gather" else src[i].at[sender]
            return pltpu.make_async_remote_copy(
                src_ref=piece, dst_ref=dst[i].at[sender], send_sem=send_sems.at[i * n_peer + d - 1],
                recv_sem=recv_sems.at[i * n_peer + d - 1], device_id=_coords(sender), device_id_type=MESH_ID)

        own = []
        for i in range(n):
            piece = src[i] if kinds[i] == "gather" else src[i].at[me]
            own.append(pltpu.make_async_copy(piece, dst[i].at[me], local_sems.at[i]))
            own[-1].start()
        sent = [remote(i, d) for d in range(1, N_DEV) for i in range(n)]
        for cp in sent:
            cp.start()
        for d in range(1, N_DEV):
            for i in range(n):
                arrival(i, d).wait_recv()
        for cp in sent:
            cp.wait_send()
        for cp in own:
            cp.wait()

    def out_of(s, kind):
        shape = s.shape if kind == "scatter" else (N_DEV,) + s.shape
        return jax.ShapeDtypeStruct(shape, s.dtype)

    any_spec = pl.BlockSpec(memory_space=pl.ANY)
    return pl.pallas_call(
        body,
        in_specs=[any_spec] * n,
        out_specs=[any_spec] * n,
        out_shape=[out_of(s, k) for s, k in zip(srcs, kinds)],
        scratch_shapes=[pltpu.SemaphoreType.DMA((n * n_peer,)), pltpu.SemaphoreType.DMA((n * n_peer,)),
                        pltpu.SemaphoreType.DMA((n,))],
        compiler_params=pltpu.CompilerParams(has_side_effects=True),
        name=name,
    )(*srcs)


def _remote(src, dst, send_sem, recv_sem, to):
    return pltpu.make_async_remote_copy(src_ref=src, dst_ref=dst, send_sem=send_sem, recv_sem=recv_sem,
                                        device_id=to, device_id_type=MESH_ID)


def _gather_two_level(shards, *, name):
    n = len(shards)
    per = N_DEV - 1

    def body(*refs):
        src, dst = refs[:n], refs[n:2 * n]
        send_sems, recv_sems, local_sems = refs[2 * n:]
        x, y, c = lax.axis_index("x"), lax.axis_index("y"), lax.axis_index("c")
        me, sibling = (x, y, c), (x, y, 1 - c)
        chips = [(1 - x, y), (x, 1 - y), (1 - x, 1 - y)]

        def slot(pos):
            return 4 * pos[0] + 2 * pos[1] + pos[2]

        def copy(i, k, block, to, from_shard=False):
            source = src[i] if from_shard else dst[i].at[slot(block)]
            return _remote(source, dst[i].at[slot(block)], send_sems.at[i * per + k], recv_sems.at[i * per + k], to)

        mine = [pltpu.make_async_copy(src[i], dst[i].at[slot(me)], local_sems.at[i]) for i in range(n)]
        for cp in mine:
            cp.start()
        first = []
        for i in range(n):
            first.append(copy(i, 0, me, sibling, from_shard=True))
            first += [copy(i, 1 + j, me, (*chip, c), from_shard=True) for j, chip in enumerate(chips)]
        for cp in first:
            cp.start()
        passed = []
        for i in range(n):
            for j, chip in enumerate(chips):
                copy(i, 1 + j, (*chip, c), me).wait_recv()
                passed.append(copy(i, 4 + j, (*chip, c), sibling))
                passed[-1].start()
        for i in range(n):
            copy(i, 0, sibling, me).wait_recv()
            for j, chip in enumerate(chips):
                copy(i, 4 + j, (*chip, 1 - c), me).wait_recv()
        for cp in first + passed:
            cp.wait_send()
        for cp in mine:
            cp.wait()

    any_spec = pl.BlockSpec(memory_space=pl.ANY)
    return pl.pallas_call(
        body,
        in_specs=[any_spec] * n,
        out_specs=[any_spec] * n,
        out_shape=[jax.ShapeDtypeStruct((N_DEV,) + s.shape, s.dtype) for s in shards],
        scratch_shapes=[pltpu.SemaphoreType.DMA((n * per,)), pltpu.SemaphoreType.DMA((n * per,)),
                        pltpu.SemaphoreType.DMA((n,))],
        name=name,
    )(*shards)


def _hosted_gather(shards):
    n = len(shards)
    per = N_DEV - 1

    def plan(src, dst, send_sems, recv_sems, local_sems):
        me = 4 * lax.axis_index("x") + 2 * lax.axis_index("y") + lax.axis_index("c")
        copies = []
        for i in range(n):
            own = pltpu.make_async_copy(src[i], dst[i].at[me], local_sems.at[i])
            copies.append(_Xfer(own.start, own.wait))
        for d in range(1, N_DEV):
            peer, sender = (me + d) & (N_DEV - 1), (me - d) & (N_DEV - 1)
            for i in range(n):
                k = i * per + d - 1
                out = _remote(src[i], dst[i].at[me], send_sems.at[k], recv_sems.at[k], _coords(peer))
                arrival = _remote(src[i], dst[i].at[sender], send_sems.at[k], recv_sems.at[k], _coords(sender))

                def wait(out=out, arrival=arrival):
                    arrival.wait_recv()
                    out.wait_send()

                copies.append(_Xfer(out.start, wait))
        return copies

    out_shape = tuple(jax.ShapeDtypeStruct((N_DEV,) + s.shape, s.dtype) for s in shards)
    return _Hosted(tuple(shards), out_shape, n * per, plan)


CHIPS = N_DEV // 2


def _hosted_sibling_swap(arrays, sliced):
    n_sems = sum(CHIPS if s else 1 for s in sliced)

    def plan(src, dst, send_sems, recv_sems, local_sems):
        x, y, c = lax.axis_index("x"), lax.axis_index("y"), lax.axis_index("c")
        sibling = (x, y, 1 - c)
        copies, k = [], 0
        for i, is_sliced in enumerate(sliced):
            pieces = [(src[i].at[2 * s + 1 - c], dst[i].at[s]) for s in range(CHIPS)] if is_sliced else [(src[i], dst[i])]
            for source, target in pieces:
                cp = _remote(source, target, send_sems.at[k], recv_sems.at[k], sibling)
                copies.append(_Xfer(cp.start, cp.wait))
                k += 1
        return copies

    out_shape = tuple(jax.ShapeDtypeStruct((CHIPS,) + a.shape[1:] if s else a.shape, a.dtype)
                      for a, s in zip(arrays, sliced))
    return _Hosted(tuple(arrays), out_shape, n_sems, plan)


def _hosted_chip_exchange(arrays, sliced):
    n = len(arrays)
    per = CHIPS - 1

    def plan(src, dst, send_sems, recv_sems, local_sems):
        x, y, c = lax.axis_index("x"), lax.axis_index("y"), lax.axis_index("c")
        chip = 2 * x + y
        copies = []
        for i in range(n):
            own = pltpu.make_async_copy(src[i].at[chip] if sliced[i] else src[i], dst[i].at[chip], local_sems.at[i])
            copies.append(_Xfer(own.start, own.wait))
        for d in range(1, CHIPS):
            other = chip ^ d
            to = ((other >> 1) & 1, other & 1, c)
            for i in range(n):
                k = i * per + d - 1
                source = src[i].at[other] if sliced[i] else src[i]
                out = _remote(source, dst[i].at[chip], send_sems.at[k], recv_sems.at[k], to)
                arrival = _remote(source, dst[i].at[other], send_sems.at[k], recv_sems.at[k], to)

                def wait(out=out, arrival=arrival):
                    arrival.wait_recv()
                    out.wait_send()

                copies.append(_Xfer(out.start, wait))
        return copies

    out_shape = tuple(jax.ShapeDtypeStruct(a.shape if s else (CHIPS,) + a.shape, a.dtype)
                      for a, s in zip(arrays, sliced))
    return _Hosted(tuple(arrays), out_shape, n * per, plan)


def _add_sibling(parts, received, core, *, name):
    _, r, cols = parts.shape
    tr = min(256, r)

    def body(core_ref, a_ref, b_ref, o_ref):
        o_ref[...] = (a_ref[...] + b_ref[...]).astype(BF16)

    grid_spec = pltpu.PrefetchScalarGridSpec(
        num_scalar_prefetch=1,
        grid=(CHIPS, r // tr),
        in_specs=[pl.BlockSpec((None, tr, cols), lambda k, i, core_ref: (2 * k + core_ref[0], i, 0)),
                  pl.BlockSpec((None, tr, cols), lambda k, i, core_ref: (k, i, 0))],
        out_specs=pl.BlockSpec((None, tr, cols), lambda k, i, core_ref: (k, i, 0)),
    )
    return pl.pallas_call(body, grid_spec=grid_spec, out_shape=jax.ShapeDtypeStruct((CHIPS, r, cols), BF16),
                          compiler_params=_params("parallel", "parallel"), name=name)(core, parts, received)


def _add_whole(a, b, *, name):
    def body(a_ref, b_ref, o_ref):
        o_ref[...] = a_ref[...] + b_ref[...]

    return pl.pallas_call(body, out_shape=jax.ShapeDtypeStruct(a.shape, F32), name=name)(a, b)


def _adamw(parts, w, m, v, *, name):
    r, c = w.shape
    n_parts = parts.shape[0]
    tr = min(256, r)
    c1 = 1.0 - ADAM_B1 ** ADAM_STEP
    c2 = 1.0 - ADAM_B2 ** ADAM_STEP

    def body(p_ref, w_ref, m_ref, v_ref, g_ref, d_ref, nm_ref, nv_ref):
        g = p_ref[0].astype(F32)
        for s in range(1, n_parts):
            g = g + p_ref[s].astype(F32)
        nm = ADAM_B1 * m_ref[...] + (1.0 - ADAM_B1) * g
        nv = ADAM_B2 * v_ref[...] + (1.0 - ADAM_B2) * (g * g)
        g_ref[...] = g
        nm_ref[...] = nm
        nv_ref[...] = nv
        d_ref[...] = -ADAM_LR * ((nm / c1) / (jnp.sqrt(nv / c2) + ADAM_EPS) + ADAM_WD * w_ref[...])

    tile = pl.BlockSpec((tr, c), lambda i: (i, 0))
    return pl.pallas_call(
        body,
        grid=(r // tr,),
        in_specs=[pl.BlockSpec((n_parts, tr, c), lambda i: (0, i, 0)), tile, tile, tile],
        out_specs=[tile] * 4,
        out_shape=[jax.ShapeDtypeStruct((r, c), F32)] * 4,
        compiler_params=_params("parallel"),
        name=name,
    )(parts, w, m, v)


BIG = ("w_in", "w_rnn_proj", "w_attn_proj", "w_out", "w_up", "w_down", "w_ple_gate", "w_ple_proj")
SMALL = (("conv_b", 1), ("b_rg", 1), ("b_ig", 1), ("lru_lambda", 1), ("g_mlp", 1), ("g_ple", 1),
         ("conv_w", 4), ("q_gain", 1), ("k_gain", 1), ("sinks", 1), ("w_rg", 64), ("w_ig", 64))
SMALL_ROWS = 144
ROW_SHARDED = ("w_rnn_proj", "w_attn_proj", "w_out", "w_down", "w_ple_gate")
COL_SHARDED = ("w_in", "w_up", "w_ple_proj")
BATCHES = {1: ("w_ple_proj", "w_ple_gate", "w_down", "w_up"), 2: ("w_out", "w_rnn_proj", "w_attn_proj"), 3: ("w_in",)}


def _pack_small(vals):
    rows = []
    for nm, nrow in SMALL:
        flat = vals[nm].reshape(-1).astype(F32)
        rows.append(jnp.pad(flat, (0, nrow * D_MODEL - flat.shape[0])).reshape(nrow, D_MODEL))
    used = sum(nrow for _, nrow in SMALL)
    rows.append(jnp.zeros((SMALL_ROWS - used, D_MODEL), F32))
    return jnp.concatenate(rows, axis=0)


def _unpack_small(packed, shapes):
    out, at = {}, 0
    for nm, nrow in SMALL:
        size = 1
        for s in shapes[nm]:
            size *= s
        out[nm] = packed[at:at + nrow].reshape(-1)[:size].reshape(shapes[nm])
        at += nrow
    return out


def _full_weight(name, landed):
    if name in COL_SHARDED:
        return landed.transpose(1, 0, 2).reshape(landed.shape[1], N_DEV * landed.shape[2])
    return landed.reshape(N_DEV * landed.shape[1], landed.shape[2])


def _owner_slots(name, grad):
    if name == "w_in":
        return grad.reshape(D_MODEL, N_DEV, IN_TOTAL // N_DEV).transpose(1, 0, 2)
    if name in COL_SHARDED:
        return grad
    return grad.reshape(N_DEV, grad.shape[0] // N_DEV, grad.shape[1])


class _StepExchanges:
    GATHERS = {"mm_in": ("w_rnn_proj", "w_attn_proj", "w_out"), "rnn_fwd": ("w_up",),
               "attn_fwd": ("w_down", "w_ple_gate", "w_ple_proj")}
    SWAPS = {"mm_dhm": 1, "mm_dya_in": 2, "mm_dh": 3}
    CHIP_EXCHANGES = {"rnn_bwd": 1, "attn_bwd": 2, "norm_mix_bwd": 3}

    def __init__(self, shards, core):
        self.shards = shards
        self.core = core
        self.parts, self.swapped, self.summed = {}, {}, {}

    def ready(self, batch, grads):
        arrays = [_owner_slots(nm, grads[nm]) for nm in BATCHES[batch]]
        sliced = [True] * len(arrays)
        if batch == 3:
            arrays.append(_pack_small(grads))
            sliced.append(False)
        self.parts[batch] = (arrays, sliced)

    def host(self, tag):
        if tag in self.GATHERS:
            return _hosted_gather([self.shards[nm] for nm in self.GATHERS[tag]])
        if tag in self.SWAPS:
            return _hosted_sibling_swap(*self.parts[self.SWAPS[tag]])
        if tag in self.CHIP_EXCHANGES:
            batch = self.CHIP_EXCHANGES[tag]
            arrays, sliced = self.parts[batch]
            labels = list(BATCHES[batch]) + ["small"]
            sums = [_add_sibling(a, r, self.core, name="add_" + lb) if s else _add_whole(a, r, name="add_" + lb)
                    for a, r, s, lb in zip(arrays, self.swapped[batch], sliced, labels)]
            return _hosted_chip_exchange(sums, sliced)
        return None

    def landed(self, tag, landed, weights):
        if tag in self.GATHERS:
            for nm, buf in zip(self.GATHERS[tag], landed):
                weights[nm] = _full_weight(nm, buf)
        elif tag in self.SWAPS:
            self.swapped[self.SWAPS[tag]] = landed
        else:
            self.summed[self.CHIP_EXCHANGES[tag]] = landed


def kernel(x, p, g_mix, w_in, conv_w, conv_b, w_rg, b_rg, w_ig, b_ig, lru_lambda, w_rnn_proj, q_gain, k_gain, sinks, w_attn_proj, w_out, g_mlp, w_up, w_down, g_ple, w_ple_gate, w_ple_proj, loss_target, m_g_mix, m_w_in, m_conv_w, m_conv_b, m_w_rg, m_b_rg, m_w_ig, m_b_ig, m_lru_lambda, m_w_rnn_proj, m_q_gain, m_k_gain, m_sinks, m_w_attn_proj, m_w_out, m_g_mlp, m_w_up, m_w_down, m_g_ple, m_w_ple_gate, m_w_ple_proj, v_g_mix, v_w_in, v_conv_w, v_conv_b, v_w_rg, v_b_rg, v_w_ig, v_b_ig, v_lru_lambda, v_w_rnn_proj, v_q_gain, v_k_gain, v_sinks, v_w_attn_proj, v_w_out, v_g_mlp, v_w_up, v_w_down, v_g_ple, v_w_ple_gate, v_w_ple_proj):
    names = ("g_mix", "w_in", "conv_w", "conv_b", "w_rg", "b_rg", "w_ig", "b_ig", "lru_lambda", "w_rnn_proj",
             "q_gain", "k_gain", "sinks", "w_attn_proj", "w_out", "g_mlp", "w_up", "w_down", "g_ple",
             "w_ple_gate", "w_ple_proj")
    wts = dict(zip(names, (g_mix, w_in, conv_w, conv_b, w_rg, b_rg, w_ig, b_ig, lru_lambda, w_rnn_proj, q_gain,
                           k_gain, sinks, w_attn_proj, w_out, g_mlp, w_up, w_down, g_ple, w_ple_gate, w_ple_proj)))
    mom1 = dict(zip(names, (m_g_mix, m_w_in, m_conv_w, m_conv_b, m_w_rg, m_b_rg, m_w_ig, m_b_ig, m_lru_lambda,
                            m_w_rnn_proj, m_q_gain, m_k_gain, m_sinks, m_w_attn_proj, m_w_out, m_g_mlp, m_w_up,
                            m_w_down, m_g_ple, m_w_ple_gate, m_w_ple_proj)))
    mom2 = dict(zip(names, (v_g_mix, v_w_in, v_conv_w, v_conv_b, v_w_rg, v_b_rg, v_w_ig, v_b_ig, v_lru_lambda,
                            v_w_rnn_proj, v_q_gain, v_k_gain, v_sinks, v_w_attn_proj, v_w_out, v_g_mlp, v_w_up,
                            v_w_down, v_g_ple, v_w_ple_gate, v_w_ple_proj)))
    n_seq, seq, _ = x.shape
    me = 4 * lax.axis_index("x") + 2 * lax.axis_index("y") + lax.axis_index("c")
    core = lax.axis_index("c").astype(jnp.int32).reshape(1)

    shards = {nm: wts[nm][0].astype(BF16) for nm in BIG}
    w_in_all, conv_all = _gather_two_level([shards["w_in"], conv_w[0]], name="gather_w_in")
    w = {nm: wts[nm] for nm in names if nm not in BIG}
    w["w_rg"], w["w_ig"] = w_rg[0], w_ig[0]
    w["conv_w"] = conv_all.transpose(1, 0, 2).reshape(CONV_W, D_MODEL)
    w["w_in"] = _full_weight("w_in", w_in_all)
    comm = _StepExchanges(shards, core)
    loss_sum, grad_x, g = _local_step(
        x.reshape(n_seq * seq, D_MODEL), p.reshape(n_seq * seq, PLE_DIM), loss_target.reshape(n_seq * seq, D_MODEL),
        w, n_seq=n_seq, seq=seq, comm=comm)
    loss = lax.psum(loss_sum, ("x", "y", "c")) * (0.5 / D_MODEL)

    res = {}
    for batch, batch_names in BATCHES.items():
        for nm, summed in zip(batch_names, comm.summed[batch]):
            res[nm] = _adamw(summed, wts[nm][0], mom1[nm][0], mom2[nm][0], name="adamw_" + nm)
    g_mix_parts, = _exchange([g["g_mix"]], ["gather"], name="gather_g_mix")
    res["g_mix"] = [r[0] for r in _adamw(g_mix_parts, g_mix, m_g_mix, v_g_mix, name="adamw_g_mix")]
    small_names = [nm for nm, _ in SMALL]
    conv_lanes = D_MODEL // N_DEV
    full_small = {}
    for src, key in ((wts, "w"), (mom1, "m"), (mom2, "v")):
        vals = {nm: src[nm][0] for nm in small_names if nm != "conv_w"}
        vals["conv_w"] = lax.dynamic_update_slice(jnp.zeros((CONV_W, D_MODEL), F32), src["conv_w"][0], (0, me * conv_lanes))
        full_small[key] = _pack_small(vals)
    small_res = _adamw(comm.summed[3][-1], full_small["w"], full_small["m"], full_small["v"], name="adamw_small")
    shapes = {nm: wts[nm].shape[1:] for nm in small_names}
    shapes["conv_w"] = (CONV_W, D_MODEL)
    small_out = [_unpack_small(r, shapes) for r in small_res]
    for nm in small_names:
        vals = [so[nm] for so in small_out]
        if nm == "conv_w":
            vals = [lax.dynamic_slice(a, (0, me * conv_lanes), (CONV_W, conv_lanes)) for a in vals]
        res[nm] = vals

    outs = [loss, grad_x.reshape(n_seq, seq, D_MODEL)]
    for k in range(4):
        outs.extend(res[nm][k][None] for nm in names)
    return tuple(outs)
```

```python
import functools
from typing import Callable, NamedTuple

import jax
import jax.numpy as jnp
from jax import lax
from jax.experimental import pallas as pl
from jax.experimental.pallas import tpu as pltpu

F32 = jnp.float32
BF16 = jnp.bfloat16

N_DEV = 8
D_MODEL = 1024
RNN_BLOCK_W = 64
CONV_W = 4
LRU_C = 8.0
HEAD_DIM = 64
N_Q_HEADS = 16
N_KV_HEADS = 4
KV_W = N_KV_HEADS * HEAD_DIM
WINDOW = 128
ROPE_THETA = 10000.0
D_FF = 4096
PLE_DIM = 256
NORM_EPS = 1e-6
IN_TOTAL = 5632
COL_XRNN, COL_GRNN, COL_Q, COL_K, COL_V, COL_GA, COL_GB = 0, 1024, 2048, 3072, 3328, 3584, 4608

ADAM_LR = 0.001
ADAM_B1 = 0.9
ADAM_B2 = 0.999
ADAM_EPS = 1e-08
ADAM_WD = 0.01
ADAM_STEP = 10

LANES = 128
SUBLANES = 8
RNN_TILE = 256
VMEM_LIMIT = 48 * 1024 * 1024
NEG_BIG = -1e30


def _params(*sem):
    return pltpu.CompilerParams(dimension_semantics=sem if sem else None, vmem_limit_bytes=VMEM_LIMIT)


def _sig(x):
    return 0.5 * jnp.tanh(0.5 * x) + 0.5


def _dot_nt(a, b):
    return lax.dot_general(a, b, (((1,), (1,)), ((), ())), preferred_element_type=F32)


def _dot_tn(a, b):
    return lax.dot_general(a, b, (((0,), (0,)), ((), ())), preferred_element_type=F32)


class _Xfer:
    def __init__(self, start, wait):
        self.start, self.wait = start, wait


class _Hosted(NamedTuple):
    srcs: tuple
    out_shape: tuple
    n_sems: int
    plan: Callable


def _call(body, *, grid, in_specs, out_specs, out_shape, args, name, semantics, scratch_shapes=(), hosted=None):
    if hosted is None:
        outs = pl.pallas_call(body, grid=grid, in_specs=list(in_specs), out_specs=list(out_specs),
                              out_shape=list(out_shape), scratch_shapes=list(scratch_shapes),
                              compiler_params=_params(*semantics), name=name)(*args)
        return list(outs), []
    counts = (len(in_specs), len(hosted.srcs), len(out_specs), len(hosted.out_shape), len(scratch_shapes), 3)

    def wrapped(*refs):
        at, groups = 0, []
        for count in counts:
            groups.append(refs[at:at + count])
            at += count
        ins, srcs, outs, dsts, scratch, sems = groups
        copies = hosted.plan(srcs, dsts, *sems)
        ids = [pl.program_id(axis) for axis in range(len(grid))]
        first = functools.reduce(jnp.logical_and, [i == 0 for i in ids])
        last = functools.reduce(jnp.logical_and, [i == g - 1 for i, g in zip(ids, grid)])

        @pl.when(first)
        def _():
            for cp in copies:
                cp.start()

        body(*ins, *outs, *scratch)

        @pl.when(last)
        def _():
            for cp in copies:
                cp.wait()

    any_spec = pl.BlockSpec(memory_space=pl.ANY)
    sems = [pltpu.SemaphoreType.DMA((hosted.n_sems,))] * 3
    outs = pl.pallas_call(
        wrapped, grid=grid, in_specs=list(in_specs) + [any_spec] * counts[1],
        out_specs=list(out_specs) + [any_spec] * counts[3], out_shape=list(out_shape) + list(hosted.out_shape),
        scratch_shapes=list(scratch_shapes) + sems, compiler_params=_params(*["arbitrary"] * len(grid)),
        name=name)(*args, *hosted.srcs)
    return list(outs[:counts[2]]), list(outs[counts[2]:])


def _matmul(a, b, *, mode, tm, tn, out_dtypes, name, epilogue=None, extras=(), hosted=None):
    m, k = a.shape
    n = b.shape[1] if mode == "nn" else b.shape[0]
    tm, tn = min(tm, m), min(tn, n)
    n_extra = len(extras)

    def body(a_ref, b_ref, *rest):
        extra_refs, out_refs = rest[:n_extra], rest[n_extra:]
        if mode == "nn":
            acc = jnp.dot(a_ref[...], b_ref[...], preferred_element_type=F32)
        else:
            acc = _dot_nt(a_ref[...], b_ref[...])
        res = epilogue(acc, *[e[...] for e in extra_refs]) if epilogue is not None else (acc,)
        for o_ref, r in zip(out_refs, res):
            o_ref[...] = r.astype(o_ref.dtype)

    b_spec = pl.BlockSpec((k, tn), lambda i, j: (0, j)) if mode == "nn" else pl.BlockSpec((tn, k), lambda i, j: (j, 0))
    tile = pl.BlockSpec((tm, tn), lambda i, j: (i, j))
    outs, landed = _call(
        body,
        grid=(m // tm, n // tn),
        in_specs=[pl.BlockSpec((tm, k), lambda i, j: (i, 0)), b_spec] + [tile] * n_extra,
        out_specs=[tile] * len(out_dtypes),
        out_shape=[jax.ShapeDtypeStruct((m, n), dt) for dt in out_dtypes],
        args=(a, b, *extras), name=name, semantics=("parallel", "arbitrary"), hosted=hosted)
    if hosted is not None:
        return (*outs, landed)
    return outs[0] if len(outs) == 1 else outs


def _matmul_tn(a, b, *, tk, tn, tt, name, slot_cols=None):
    t, k = a.shape
    n = b.shape[1]
    tk, tn, tt = min(tk, k), min(tn, n), min(tt, t)

    def body(a_ref, b_ref, o_ref):
        @pl.when(pl.program_id(2) == 0)
        def _():
            o_ref[...] = jnp.zeros_like(o_ref)

        if slot_cols is None:
            o_ref[...] += _dot_tn(a_ref[...], b_ref[...])
        else:
            av = a_ref[...]
            for s in range(tn // slot_cols):
                o_ref[s] += _dot_tn(av, b_ref[:, s * slot_cols:(s + 1) * slot_cols])

    if slot_cols is not None:
        out_spec = pl.BlockSpec((tn // slot_cols, tk, slot_cols), lambda i, j, s: (j, i, 0))
        out_shape = jax.ShapeDtypeStruct((n // slot_cols, k, slot_cols), F32)
    else:
        out_spec = pl.BlockSpec((tk, tn), lambda i, j, s: (i, j))
        out_shape = jax.ShapeDtypeStruct((k, n), F32)
    return pl.pallas_call(
        body,
        grid=(k // tk, n // tn, t // tt),
        in_specs=[pl.BlockSpec((tt, tk), lambda i, j, s: (s, i)), pl.BlockSpec((tt, tn), lambda i, j, s: (s, j))],
        out_specs=out_spec,
        out_shape=out_shape,
        compiler_params=_params("parallel", "parallel", "arbitrary"),
        name=name,
    )(a, b)


def _rmsnorm_fwd(x, g, *, name):
    t, d = x.shape
    tm = min(512, t)

    def body(x_ref, g_ref, o_ref):
        xv = x_ref[...]
        r = lax.rsqrt(jnp.mean(xv * xv, axis=-1, keepdims=True) + NORM_EPS)
        o_ref[...] = (xv * r * g_ref[...]).astype(BF16)

    return pl.pallas_call(
        body,
        grid=(t // tm,),
        in_specs=[pl.BlockSpec((tm, d), lambda i: (i, 0)), pl.BlockSpec((1, d), lambda i: (0, 0))],
        out_specs=pl.BlockSpec((tm, d), lambda i: (i, 0)),
        out_shape=jax.ShapeDtypeStruct((t, d), BF16),
        compiler_params=_params("parallel"),
        name=name,
    )(x, g)


def _rmsnorm_bwd(x, g, dy, dres, *, name, want_bf16, hosted=None):
    t, d = x.shape
    tm = min(256, t)

    def body(x_ref, g_ref, dy_ref, dres_ref, *out_refs):
        dx_ref, dg_ref = out_refs[0], out_refs[-1]
        xv, dyv = x_ref[...], dy_ref[...]
        r = lax.rsqrt(jnp.mean(xv * xv, axis=-1, keepdims=True) + NORM_EPS)
        xr = xv * r
        gy = dyv * g_ref[...]
        dx = dres_ref[...] + r * (gy - xr * jnp.mean(gy * xr, axis=-1, keepdims=True))
        dx_ref[...] = dx
        if want_bf16:
            out_refs[1][...] = dx.astype(BF16)

        @pl.when(pl.program_id(0) == 0)
        def _():
            dg_ref[...] = jnp.zeros_like(dg_ref)

        dg_ref[...] += jnp.sum(dyv * xr, axis=0, keepdims=True)

    tile = pl.BlockSpec((tm, d), lambda i: (i, 0))
    vec = pl.BlockSpec((1, d), lambda i: (0, 0))
    out_specs = [tile] + ([tile] if want_bf16 else []) + [vec]
    out_shape = [jax.ShapeDtypeStruct((t, d), F32)] + ([jax.ShapeDtypeStruct((t, d), BF16)] if want_bf16 else [])
    out_shape.append(jax.ShapeDtypeStruct((1, d), F32))
    outs, landed = _call(body, grid=(t // tm,), in_specs=[tile, vec, tile, tile], out_specs=out_specs,
                         out_shape=out_shape, args=(x, g, dy, dres), name=name, semantics=("arbitrary",), hosted=hosted)
    return (*outs, landed) if hosted is not None else outs


def _softplus_neg(lam):
    z = -lam
    return jnp.maximum(z, 0.0) + jnp.log1p(jnp.exp(-jnp.abs(z)))


def _neg_expm1(y, exp_half_y):
    series = -y * (1.0 + y * 0.5 * (1.0 + y * (1.0 / 3.0) * (1.0 + y * 0.25 * (1.0 + y * 0.2))))
    return jnp.where(y > -0.0625, series, 1.0 - exp_half_y * exp_half_y)


def _gelu_parts(x):
    c = 0.7978845608028654
    u = c * (x + 0.044715 * x * x * x)
    th = jnp.tanh(u)
    gel = 0.5 * x * (1.0 + th)
    dgel = 0.5 * (1.0 + th) + 0.5 * x * (1.0 - th * th) * c * (1.0 + 3.0 * 0.044715 * x * x)
    return gel, dgel


def _shift_down(v, k, rows):
    return jnp.where(rows < k, 0.0, pltpu.roll(v, k, 0))


def _shift_up(v, k, rows, n):
    return jnp.where(rows >= n - k, 0.0, pltpu.roll(v, n - k, 0))


def _scan_within_groups(a, b, rows, *, reverse):
    n = a.shape[0]
    in_group = rows & (SUBLANES - 1)
    for s in (1, 2, 4):
        if reverse:
            inside, shift = in_group < SUBLANES - s, n - s
        else:
            inside, shift = in_group >= s, s
        b = b + a * jnp.where(inside, pltpu.roll(b, shift, 0), 0.0)
        a = a * jnp.where(inside, pltpu.roll(a, shift, 0), 1.0)
    return a, b


def _rnn_gates(xc, wrg, brg, wig, big, lam):
    xcb = xc.astype(BF16)
    r = _sig(jnp.dot(xcb, wrg, preferred_element_type=F32) + brg)
    i = _sig(jnp.dot(xcb, wig, preferred_element_type=F32) + big)
    sp = _softplus_neg(lam)
    log_a = -LRU_C * r * sp
    a = jnp.exp(log_a)
    mult = jnp.sqrt(_neg_expm1(2.0 * log_a, a))
    return xcb, r, i, sp, a, mult


def _conv_fwd(xv, cw, cb, rows):
    return (cb + _shift_down(xv, 3, rows) * cw[0:1, :] + _shift_down(xv, 2, rows) * cw[1:2, :]
            + _shift_down(xv, 1, rows) * cw[2:3, :] + xv * cw[3:4, :])


def _rnn_fwd(z, conv_w, conv_b, wrg_bd, b_rg, wig_bd, b_ig, lam, *, n_seq, seq, hosted=None):
    t = n_seq * seq
    ct = RNN_TILE
    n_ct = D_MODEL // ct

    def body(x_ref, g_ref, cw_ref, cb_ref, wrg_ref, brg_ref, wig_ref, big_ref, lam_ref,
             xc_ref, hr_ref, ya_ref, a_s, b_s):
        rows = lax.broadcasted_iota(jnp.int32, (seq, ct), 0)
        xc = _conv_fwd(x_ref[...], cw_ref[...], cb_ref[...], rows)
        _, r, i, sp, a, mult = _rnn_gates(xc, wrg_ref[...], brg_ref[...], wig_ref[...], big_ref[...], lam_ref[...])
        a_s[...], b_s[...] = _scan_within_groups(a, mult * (i * xc), rows, reverse=False)

        def step(j, carry):
            r0 = pl.multiple_of(j * SUBLANES, SUBLANES)
            h = b_s[pl.ds(r0, SUBLANES), :] + a_s[pl.ds(r0, SUBLANES), :] * carry
            hr_ref[pl.ds(r0, SUBLANES), :] = h
            return h[SUBLANES - 1:SUBLANES, :]

        lax.fori_loop(0, seq // SUBLANES, step, jnp.zeros((1, ct), F32), unroll=4)
        gel, _ = _gelu_parts(g_ref[...])
        xc_ref[...] = xc
        ya_ref[...] = (hr_ref[...] * gel).astype(BF16)

    vec = pl.BlockSpec((1, ct), lambda b, c: (0, c))
    gate_w = pl.BlockSpec((None, ct, ct), lambda b, c: (c, 0, 0))
    tile = pl.BlockSpec((seq, ct), lambda b, c: (b, c))
    outs, landed = _call(
        body,
        grid=(n_seq, n_ct),
        in_specs=[
            pl.BlockSpec((seq, ct), lambda b, c: (b, COL_XRNN // ct + c)),
            pl.BlockSpec((seq, ct), lambda b, c: (b, COL_GRNN // ct + c)),
            pl.BlockSpec((CONV_W, ct), lambda b, c: (0, c)), vec, gate_w, vec, gate_w, vec, vec,
        ],
        out_specs=[tile, tile, tile],
        out_shape=[jax.ShapeDtypeStruct((t, D_MODEL), F32), jax.ShapeDtypeStruct((t, D_MODEL), F32),
                   jax.ShapeDtypeStruct((t, D_MODEL), BF16)],
        scratch_shapes=[pltpu.VMEM((seq, ct), F32), pltpu.VMEM((seq, ct), F32)],
        args=(z, z, conv_w, conv_b, wrg_bd, b_rg, wig_bd, b_ig, lam), name="rnn_fwd",
        semantics=("parallel", "parallel"), hosted=hosted)
    return (*outs, landed) if hosted is not None else outs


def _rnn_bwd(dya, z, xc, hr, conv_w, wrg_bd, b_rg, wig_bd, b_ig, lam, *, n_seq, seq, hosted=None):
    t = n_seq * seq
    ct = RNN_TILE
    n_ct = D_MODEL // ct

    def body(dya_ref, x_ref, g_ref, xc_ref, hr_ref, cw_ref, wrg_ref, brg_ref, wig_ref, big_ref, lam_ref,
             dx_ref, dg_ref, dwrg_ref, dwig_ref, vec_ref, a_s, d_s, g_s):
        rows = lax.broadcasted_iota(jnp.int32, (seq, ct), 0)
        xv, xc, hr, dyv = x_ref[...], xc_ref[...], hr_ref[...], dya_ref[...]
        lamv = lam_ref[...]
        gel, dgel = _gelu_parts(g_ref[...])
        dg_ref[...] = (dyv * hr * dgel).astype(BF16)
        xcb, r, i, sp, a, mult = _rnn_gates(xc, wrg_ref[...], brg_ref[...], wig_ref[...], big_ref[...], lamv)
        a_s[...], d_s[...] = _scan_within_groups(_shift_up(a, 1, rows, seq), dyv * gel, rows, reverse=True)

        def step(k, carry):
            r0 = pl.multiple_of((seq // SUBLANES - 1 - k) * SUBLANES, SUBLANES)
            gs = d_s[pl.ds(r0, SUBLANES), :] + a_s[pl.ds(r0, SUBLANES), :] * carry
            g_s[pl.ds(r0, SUBLANES), :] = gs
            return gs[0:1, :]

        lax.fori_loop(0, seq // SUBLANES, step, jnp.zeros((1, ct), F32), unroll=4)
        gsum = g_s[...]
        gated = i * xc
        d_log_a = gsum * _shift_down(hr, 1, rows) * a - gsum * gated * (a * a / mult)
        d_gated = gsum * mult
        d_pre_r = (d_log_a * (-LRU_C) * sp) * r * (1.0 - r)
        d_pre_i = (d_gated * xc) * i * (1.0 - i)
        dprb, dpib = d_pre_r.astype(BF16), d_pre_i.astype(BF16)
        dxc = d_gated * i + _dot_nt(dprb, wrg_ref[...]) + _dot_nt(dpib, wig_ref[...])
        cw = cw_ref[...]
        dx = (dxc * cw[3:4, :] + _shift_up(dxc, 1, rows, seq) * cw[2:3, :]
              + _shift_up(dxc, 2, rows, seq) * cw[1:2, :] + _shift_up(dxc, 3, rows, seq) * cw[0:1, :])
        dx_ref[...] = dx.astype(BF16)

        @pl.when(pl.program_id(1) == 0)
        def _():
            dwrg_ref[...] = jnp.zeros_like(dwrg_ref)
            dwig_ref[...] = jnp.zeros_like(dwig_ref)
            vec_ref[...] = jnp.zeros_like(vec_ref)

        dwrg_ref[...] += _dot_tn(xcb, dprb)
        dwig_ref[...] += _dot_tn(xcb, dpib)

        def colsum(v):
            return jnp.sum(v, axis=0, keepdims=True)

        d_sp = colsum(d_log_a * (-LRU_C) * r)
        vec_ref[0:1, :] += colsum(d_pre_r)
        vec_ref[1:2, :] += colsum(d_pre_i)
        vec_ref[2:3, :] += d_sp * (-_sig(-lamv))
        vec_ref[3:4, :] += colsum(dxc)
        vec_ref[4:5, :] += colsum(dxc * _shift_down(xv, 3, rows))
        vec_ref[5:6, :] += colsum(dxc * _shift_down(xv, 2, rows))
        vec_ref[6:7, :] += colsum(dxc * _shift_down(xv, 1, rows))
        vec_ref[7:8, :] += colsum(dxc * xv)

    vec = pl.BlockSpec((1, ct), lambda c, b: (0, c))
    gate_w = pl.BlockSpec((None, ct, ct), lambda c, b: (c, 0, 0))
    tile = pl.BlockSpec((seq, ct), lambda c, b: (b, c))
    outs, landed = _call(
        body,
        grid=(n_ct, n_seq),
        in_specs=[
            tile,
            pl.BlockSpec((seq, ct), lambda c, b: (b, COL_XRNN // ct + c)),
            pl.BlockSpec((seq, ct), lambda c, b: (b, COL_GRNN // ct + c)),
            tile, tile,
            pl.BlockSpec((CONV_W, ct), lambda c, b: (0, c)), gate_w, vec, gate_w, vec, vec,
        ],
        out_specs=[tile, tile, gate_w, gate_w, pl.BlockSpec((8, ct), lambda c, b: (0, c))],
        out_shape=[jax.ShapeDtypeStruct((t, D_MODEL), BF16), jax.ShapeDtypeStruct((t, D_MODEL), BF16),
                   jax.ShapeDtypeStruct((n_ct, ct, ct), F32), jax.ShapeDtypeStruct((n_ct, ct, ct), F32),
                   jax.ShapeDtypeStruct((8, D_MODEL), F32)],
        scratch_shapes=[pltpu.VMEM((seq, ct), F32)] * 3,
        args=(dya, z, z, xc, hr, conv_w, wrg_bd, b_rg, wig_bd, b_ig, lam), name="rnn_bwd",
        semantics=("parallel", "arbitrary"), hosted=hosted)
    return (*outs, landed) if hosted is not None else outs


def _split_hi_lo(x):
    hi = x.astype(BF16)
    return hi, (x - hi.astype(F32)).astype(BF16)


def _dot_split(x, m):
    hi, lo = _split_hi_lo(x)
    return jnp.dot(hi, m, preferred_element_type=F32) + jnp.dot(lo, m, preferred_element_type=F32)


def _head_matrices(width):
    ec = (lax.broadcasted_iota(jnp.int32, (width, LANES), 0) // HEAD_DIM
          == lax.broadcasted_iota(jnp.int32, (width, LANES), 1))
    ee = (lax.broadcasted_iota(jnp.int32, (LANES, width), 1) // HEAD_DIM
          == lax.broadcasted_iota(jnp.int32, (LANES, width), 0))
    return jnp.where(ec, 1.0, 0.0).astype(BF16), jnp.where(ee, 1.0, 0.0).astype(BF16)


def _swap_halves(y):
    w = y.shape[1]
    first = (lax.broadcasted_iota(jnp.int32, y.shape, 1) % HEAD_DIM) < HEAD_DIM // 2
    return jnp.where(first, pltpu.roll(y, w - HEAD_DIM // 2, 1), pltpu.roll(y, HEAD_DIM // 2, 1))


def _normrope_fwd(x, gain, cos_t, sin_t, ec, ee):
    w = x.shape[1]
    rs = _dot_split(lax.rsqrt(_dot_split(x * x, ec) * (1.0 / HEAD_DIM) + NORM_EPS), ee)
    nx = x * rs
    y = nx * gain
    reps = w // LANES
    out = y * jnp.tile(cos_t, (1, reps)) + _swap_halves(y) * jnp.tile(sin_t, (1, reps))
    return out, nx, rs


def _normrope_bwd(dout, nx, rs, gain, cos_t, sin_t, ec, ee):
    w = dout.shape[1]
    reps = w // LANES
    dy = dout * jnp.tile(cos_t, (1, reps)) + _swap_halves(dout * jnp.tile(sin_t, (1, reps)))
    dgain = jnp.sum(dy * nx, axis=0, keepdims=True)
    dn = dy * gain
    seg = _dot_split(_dot_split(dn * nx, ec) * (1.0 / HEAD_DIM), ee)
    return rs * (dn - nx * seg), dgain


def _pair_operand(t, group):
    chunk = t[:, (group // 2) * LANES:(group // 2 + 1) * LANES]
    low = lax.broadcasted_iota(jnp.int32, chunk.shape, 1) < HEAD_DIM
    rolled = pltpu.roll(chunk, HEAD_DIM, 1)
    return jnp.where(low, chunk, rolled) if group % 2 == 0 else jnp.where(low, rolled, chunk)


GROUP = N_Q_HEADS // N_KV_HEADS
GROUP_W = GROUP * HEAD_DIM


def _replicate_head(t, group):
    return jnp.tile(_pair_operand(t, group), (1, 2))


def _head_blocks(t):
    seg = lax.broadcasted_iota(jnp.int32, t.shape, 1) // HEAD_DIM
    return jnp.concatenate([jnp.where(seg == h, t, 0.0) for h in range(GROUP)], axis=0)


def _fold_head_blocks(t):
    r = t.shape[0] // GROUP
    seg = lax.broadcasted_iota(jnp.int32, (r, GROUP_W), 1) // HEAD_DIM
    out = jnp.where(seg == 0, t[0:r], 0.0)
    for h in range(1, GROUP):
        out = out + jnp.where(seg == h, t[h * r:(h + 1) * r], 0.0)
    return out


def _head_columns(mat, group, rows):
    return jnp.concatenate([jnp.broadcast_to(mat[:, GROUP * group + h:GROUP * group + h + 1], (rows, 1))
                            for h in range(GROUP)], axis=0)


def _head_rows(mat_t, group):
    return jnp.concatenate([mat_t[GROUP * group + h:GROUP * group + h + 1, :] for h in range(GROUP)], axis=1)


def _attn_fwd(z, cos_t, sin_t, q_gain_t, k_gain_t, sinks_t, *, n_seq, seq, hosted=None):
    t = n_seq * seq
    blk = WINDOW
    nb = seq // blk

    def body(q_ref, kp_ref, kc_ref, vp_ref, vc_ref, cosc_ref, sinc_ref, cosp_ref, sinp_ref, qg_ref, kg_ref, sk_ref,
             o_ref, l_ref):
        n = pl.program_id(1)
        ecq, eeq = _head_matrices(D_MODEL)
        eck, eek = _head_matrices(KV_W)
        cosc, sinc = cosc_ref[...], sinc_ref[...]
        qh, _, _ = _normrope_fwd(q_ref[...], qg_ref[...], cosc, sinc, ecq, eeq)
        kc, _, _ = _normrope_fwd(kc_ref[...], kg_ref[...], cosc, sinc, eck, eek)
        kp, _, _ = _normrope_fwd(kp_ref[...], kg_ref[...], cosp_ref[...], sinp_ref[...], eck, eek)
        kcat = jnp.concatenate([kp, kc], axis=0)
        vcat = jnp.concatenate([vp_ref[...], vc_ref[...]], axis=0)
        qi = lax.broadcasted_iota(jnp.int32, (GROUP * blk, 2 * blk), 0) & (blk - 1)
        ci = lax.broadcasted_iota(jnp.int32, (GROUP * blk, 2 * blk), 1)
        valid = ((ci < blk) & (ci > qi) & (n > 0)) | ((ci >= blk) & (ci - blk <= qi))
        lane = lax.broadcasted_iota(jnp.int32, (blk, LANES), 1)
        sk = sk_ref[...]
        lmat = jnp.zeros((blk, LANES), F32)
        for group in range(N_KV_HEADS):
            cols = slice(group * GROUP_W, (group + 1) * GROUP_W)
            qb = _head_blocks(qh[:, cols]).astype(BF16)
            krep = _replicate_head(kcat, group).astype(BF16)
            vrep = _replicate_head(vcat, group).astype(BF16)
            s = jnp.where(valid, _dot_nt(qb, krep) * (HEAD_DIM ** -0.5), NEG_BIG)
            sink = _head_columns(jnp.broadcast_to(sk, (blk, LANES)), group, blk)
            m = jnp.maximum(jnp.max(s, axis=-1, keepdims=True), sink)
            e = jnp.exp(s - m)
            den = jnp.sum(e, axis=-1, keepdims=True) + jnp.exp(sink - m)
            probs = (e * (1.0 / den)).astype(BF16)
            o_ref[:, cols] = _fold_head_blocks(jnp.dot(probs, vrep, preferred_element_type=F32)).astype(BF16)
            lse = m + jnp.log(den)
            for h in range(GROUP):
                lmat = lmat + jnp.where(lane == GROUP * group + h, lse[h * blk:(h + 1) * blk], 0.0)
        l_ref[...] = lmat

    def row(b, n):
        return b * nb + n

    def prev(b, n):
        return b * nb + jnp.maximum(n - 1, 0)

    kw = KV_W
    tab_c = pl.BlockSpec((blk, LANES), lambda b, n: (n, 0))
    tab_p = pl.BlockSpec((blk, LANES), lambda b, n: (jnp.maximum(n - 1, 0), 0))
    outs, landed = _call(
        body,
        grid=(n_seq, nb),
        in_specs=[
            pl.BlockSpec((blk, D_MODEL), lambda b, n: (row(b, n), COL_Q // D_MODEL)),
            pl.BlockSpec((blk, kw), lambda b, n: (prev(b, n), COL_K // kw)),
            pl.BlockSpec((blk, kw), lambda b, n: (row(b, n), COL_K // kw)),
            pl.BlockSpec((blk, kw), lambda b, n: (prev(b, n), COL_V // kw)),
            pl.BlockSpec((blk, kw), lambda b, n: (row(b, n), COL_V // kw)),
            tab_c, tab_c, tab_p, tab_p,
            pl.BlockSpec((1, D_MODEL), lambda b, n: (0, 0)),
            pl.BlockSpec((1, kw), lambda b, n: (0, 0)),
            pl.BlockSpec((1, LANES), lambda b, n: (0, 0)),
        ],
        out_specs=[pl.BlockSpec((blk, D_MODEL), lambda b, n: (row(b, n), 0)),
                   pl.BlockSpec((blk, LANES), lambda b, n: (row(b, n), 0))],
        out_shape=[jax.ShapeDtypeStruct((t, D_MODEL), BF16), jax.ShapeDtypeStruct((t, LANES), F32)],
        args=(z, z, z, z, z, cos_t, sin_t, cos_t, sin_t, q_gain_t, k_gain_t, sinks_t), name="attn_fwd",
        semantics=("parallel", "parallel"), hosted=hosted)
    return (*outs, landed) if hosted is not None else outs


def _attn_bwd(z, o, lse, do, cos_t, sin_t, q_gain_t, k_gain_t, sinks_t, *, n_seq, seq, hosted=None):
    t = n_seq * seq
    blk = WINDOW
    nb = seq // blk
    kw = KV_W
    scale = HEAD_DIM ** -0.5

    def body(qc_ref, qn_ref, kp_ref, kc_ref, vp_ref, vc_ref, oc_ref, on_ref, doc_ref, don_ref, lc_ref, ln_ref,
             cosc_ref, sinc_ref, cosp_ref, sinp_ref, cosn_ref, sinn_ref, qg_ref, kg_ref, sk_ref,
             dq_ref, dk_ref, dv_ref, vec_ref):
        n = pl.program_id(1)
        ecq, eeq = _head_matrices(D_MODEL)
        eck, eek = _head_matrices(KV_W)
        cosc, sinc = cosc_ref[...], sinc_ref[...]
        qg, kg = qg_ref[...], kg_ref[...]
        qhc, nqc, rsqc = _normrope_fwd(qc_ref[...], qg, cosc, sinc, ecq, eeq)
        qhn, _, _ = _normrope_fwd(qn_ref[...], qg, cosn_ref[...], sinn_ref[...], ecq, eeq)
        khc, nkc, rskc = _normrope_fwd(kc_ref[...], kg, cosc, sinc, eck, eek)
        khp, _, _ = _normrope_fwd(kp_ref[...], kg, cosp_ref[...], sinp_ref[...], eck, eek)
        doc = doc_ref[...].astype(F32)
        don = don_ref[...].astype(F32)
        delc = _dot_split(doc * oc_ref[...].astype(F32), ecq)
        deln = _dot_split(don * on_ref[...].astype(F32), ecq)
        lc, ln = lc_ref[...], ln_ref[...]
        lc_t, ln_t, delc_t, deln_t = lc.T, ln.T, delc.T, deln.T
        qi = lax.broadcasted_iota(jnp.int32, (GROUP * blk, 2 * blk), 0) & (blk - 1)
        ci = lax.broadcasted_iota(jnp.int32, (GROUP * blk, 2 * blk), 1)
        mask_q = ((ci < blk) & (ci > qi) & (n > 0)) | ((ci >= blk) & (ci - blk <= qi))
        kj = lax.broadcasted_iota(jnp.int32, (blk, GROUP * blk), 0)
        qj = lax.broadcasted_iota(jnp.int32, (blk, GROUP * blk), 1) & (blk - 1)
        mask_same, mask_next = kj <= qj, (kj > qj) & (n < nb - 1)
        seg = lax.broadcasted_iota(jnp.int32, (blk, GROUP_W), 1) // HEAD_DIM
        lane = lax.broadcasted_iota(jnp.int32, (1, LANES), 1)
        sk = sk_ref[...]
        dsink = jnp.zeros((1, LANES), F32)
        kcat = jnp.concatenate([khp, khc], axis=0)
        vcat = jnp.concatenate([vp_ref[...], vc_ref[...]], axis=0)
        dkh = jnp.zeros((blk, GROUP_W), F32)
        dvh = jnp.zeros((blk, GROUP_W), F32)

        def fold_to(group, t):
            total = t + pltpu.roll(t, HEAD_DIM, 1)
            total = total + pltpu.roll(total, 2 * HEAD_DIM, 1)
            return jnp.where(seg == group, total, 0.0)

        for group in range(N_KV_HEADS):
            cols = slice(group * GROUP_W, (group + 1) * GROUP_W)
            krep = _replicate_head(kcat, group).astype(BF16)
            vrep = _replicate_head(vcat, group).astype(BF16)
            qb_c = _head_blocks(qhc[:, cols]).astype(BF16)
            dob_c = _head_blocks(doc[:, cols]).astype(BF16)
            l_col, d_col = _head_columns(lc, group, blk), _head_columns(delc, group, blk)
            p = jnp.where(mask_q, jnp.exp(_dot_nt(qb_c, krep) * scale - l_col), 0.0)
            ds = (p * (_dot_nt(dob_c, vrep) - d_col)).astype(BF16)
            dq_ref[:, cols] = _fold_head_blocks(jnp.dot(ds, krep, preferred_element_type=F32)) * scale
            p_sink = jnp.exp(_head_columns(jnp.broadcast_to(sk, (blk, LANES)), group, blk) - l_col) * d_col
            for h in range(GROUP):
                dsink = dsink + jnp.where(lane == GROUP * group + h,
                                          -jnp.sum(p_sink[h * blk:(h + 1) * blk], axis=0, keepdims=True), 0.0)
            krep_c, vrep_c = krep[blk:], vrep[blk:]
            qb_n = _head_blocks(qhn[:, cols]).astype(BF16)
            dob_n = _head_blocks(don[:, cols]).astype(BF16)
            dk_rep = jnp.zeros((blk, GROUP_W), F32)
            dv_rep = jnp.zeros((blk, GROUP_W), F32)
            for qb, dob, l_t, d_t, mask in ((qb_c, dob_c, lc_t, delc_t, mask_same), (qb_n, dob_n, ln_t, deln_t, mask_next)):
                p_t = jnp.where(mask, jnp.exp(_dot_nt(krep_c, qb) * scale - _head_rows(l_t, group)), 0.0)
                ds_t = (p_t * (_dot_nt(vrep_c, dob) - _head_rows(d_t, group))).astype(BF16)
                dk_rep = dk_rep + jnp.dot(ds_t, qb, preferred_element_type=F32)
                dv_rep = dv_rep + jnp.dot(p_t.astype(BF16), dob, preferred_element_type=F32)
            dkh = dkh + fold_to(group, dk_rep * scale)
            dvh = dvh + fold_to(group, dv_rep)
        dv_ref[...] = dvh.astype(BF16)
        dq, dqg = _normrope_bwd(dq_ref[...], nqc, rsqc, qg, cosc, sinc, ecq, eeq)
        dk, dkg = _normrope_bwd(dkh, nkc, rskc, kg, cosc, sinc, eck, eek)
        dq_ref[...] = dq
        dk_ref[...] = dk.astype(BF16)

        @pl.when(n == 0)
        def _():
            vec_ref[...] = jnp.zeros_like(vec_ref)

        vec_ref[0:1, :] += dqg
        vec_ref[1:2, 0:kw] += dkg
        vec_ref[2:3, 0:LANES] += dsink

    def row(b, n):
        return b * nb + n

    def prev(b, n):
        return b * nb + jnp.maximum(n - 1, 0)

    def nxt(b, n):
        return b * nb + jnp.minimum(n + 1, nb - 1)

    def tiles(width, col, which):
        return pl.BlockSpec((blk, width), lambda b, n: (which(b, n), col))

    def table(which):
        return pl.BlockSpec((blk, LANES), lambda b, n: (which(0, n), 0))

    outs, landed = _call(
        body,
        grid=(n_seq, nb),
        in_specs=[
            tiles(D_MODEL, COL_Q // D_MODEL, row), tiles(D_MODEL, COL_Q // D_MODEL, nxt),
            tiles(kw, COL_K // kw, prev), tiles(kw, COL_K // kw, row),
            tiles(kw, COL_V // kw, prev), tiles(kw, COL_V // kw, row),
            tiles(D_MODEL, 0, row), tiles(D_MODEL, 0, nxt),
            tiles(D_MODEL, 0, row), tiles(D_MODEL, 0, nxt),
            tiles(LANES, 0, row), tiles(LANES, 0, nxt),
            table(row), table(row), table(prev), table(prev), table(nxt), table(nxt),
            pl.BlockSpec((1, D_MODEL), lambda b, n: (0, 0)),
            pl.BlockSpec((1, kw), lambda b, n: (0, 0)),
            pl.BlockSpec((1, LANES), lambda b, n: (0, 0)),
        ],
        out_specs=[tiles(D_MODEL, 0, row), tiles(kw, 0, row), tiles(kw, 0, row),
                   pl.BlockSpec((None, 8, D_MODEL), lambda b, n: (b, 0, 0))],
        out_shape=[jax.ShapeDtypeStruct((t, D_MODEL), F32), jax.ShapeDtypeStruct((t, kw), BF16),
                   jax.ShapeDtypeStruct((t, kw), BF16), jax.ShapeDtypeStruct((n_seq, 8, D_MODEL), F32)],
        args=(z, z, z, z, z, z, o, o, do, do, lse, lse, cos_t, sin_t, cos_t, sin_t, cos_t, sin_t,
              q_gain_t, k_gain_t, sinks_t), name="attn_bwd", semantics=("parallel", "arbitrary"), hosted=hosted)
    return (*outs, landed) if hosted is not None else outs


MERGE_COLS = 512


def _merge_fwd(z, ya, yb):
    t = ya.shape[0]
    tm, tc = min(512, t), MERGE_COLS

    def body(ga_ref, gb_ref, ya_ref, yb_ref, o_ref):
        o_ref[...] = (_sig(ga_ref[...]) * ya_ref[...] + _sig(gb_ref[...]) * yb_ref[...]).astype(BF16)

    tile = pl.BlockSpec((tm, tc), lambda i, j: (i, j))
    return pl.pallas_call(
        body,
        grid=(t // tm, D_MODEL // tc),
        in_specs=[pl.BlockSpec((tm, tc), lambda i, j: (i, COL_GA // tc + j)),
                  pl.BlockSpec((tm, tc), lambda i, j: (i, COL_GB // tc + j)), tile, tile],
        out_specs=tile,
        out_shape=jax.ShapeDtypeStruct((t, D_MODEL), BF16),
        compiler_params=_params("parallel", "parallel"),
        name="merge_fwd",
    )(z, z, ya, yb)


def _merge_bwd(z, ya, yb, dmerged):
    t = ya.shape[0]
    tm, tc = min(512, t), MERGE_COLS

    def body(ga_ref, gb_ref, ya_ref, yb_ref, dm_ref, dya_ref, dyb_ref, dga_ref, dgb_ref):
        dm = dm_ref[...]
        sa, sb = _sig(ga_ref[...]), _sig(gb_ref[...])
        dya_ref[...] = (dm * sa).astype(BF16)
        dyb_ref[...] = (dm * sb).astype(BF16)
        dga_ref[...] = (dm * ya_ref[...] * sa * (1.0 - sa)).astype(BF16)
        dgb_ref[...] = (dm * yb_ref[...] * sb * (1.0 - sb)).astype(BF16)

    tile = pl.BlockSpec((tm, tc), lambda i, j: (i, j))
    return pl.pallas_call(
        body,
        grid=(t // tm, D_MODEL // tc),
        in_specs=[pl.BlockSpec((tm, tc), lambda i, j: (i, COL_GA // tc + j)),
                  pl.BlockSpec((tm, tc), lambda i, j: (i, COL_GB // tc + j)), tile, tile, tile],
        out_specs=[tile] * 4,
        out_shape=[jax.ShapeDtypeStruct((t, D_MODEL), BF16)] * 4,
        compiler_params=_params("parallel", "parallel"),
        name="merge_bwd",
    )(z, z, ya, yb, dmerged)


def _loss_head(x2, e, gt, target):
    t, d = x2.shape
    tm = min(256, t)

    def body(x_ref, e_ref, gt_ref, tg_ref, loss_ref, dx_ref, dgt_ref, de_ref):
        ev = e_ref[...]
        sg = _sig(gt_ref[...])
        diff = x_ref[...] + ev * sg - tg_ref[...]
        dx = diff * (1.0 / d)
        dx_ref[...] = dx
        dgt_ref[...] = (dx * ev * sg * (1.0 - sg)).astype(BF16)
        de_ref[...] = (dx * sg).astype(BF16)

        @pl.when(pl.program_id(0) == 0)
        def _():
            loss_ref[...] = jnp.zeros_like(loss_ref)

        loss_ref[...] += jnp.sum(jnp.sum(diff * diff, axis=1, keepdims=True), axis=0, keepdims=True)

    tile = pl.BlockSpec((tm, d), lambda i: (i, 0))
    return pl.pallas_call(
        body,
        grid=(t // tm,),
        in_specs=[tile] * 4,
        out_specs=[pl.BlockSpec((1, LANES), lambda i: (0, 0)), tile, tile, tile],
        out_shape=[jax.ShapeDtypeStruct((1, LANES), F32), jax.ShapeDtypeStruct((t, d), F32),
                   jax.ShapeDtypeStruct((t, d), BF16), jax.ShapeDtypeStruct((t, d), BF16)],
        compiler_params=_params("arbitrary"),
        name="loss_head",
    )(x2, e, gt, target)


def _rope_tables(seq):
    inv = ROPE_THETA ** (-jnp.arange(0, HEAD_DIM, 2, dtype=F32) / HEAD_DIM)
    ang = jnp.arange(seq, dtype=F32)[:, None] * inv[None, :]
    cos, sin = jnp.cos(ang), jnp.sin(ang)
    return jnp.tile(jnp.concatenate([cos, cos], axis=1), (1, 2)), jnp.tile(jnp.concatenate([-sin, sin], axis=1), (1, 2))


def _block_diag_tiles(w):
    per = RNN_TILE // RNN_BLOCK_W
    w4 = w.reshape(D_MODEL // RNN_TILE, per, RNN_BLOCK_W, RNN_BLOCK_W)
    eye = jnp.eye(per, dtype=w.dtype)
    dense = jnp.einsum("tpij,pq->tpiqj", w4, eye)
    return dense.reshape(D_MODEL // RNN_TILE, RNN_TILE, RNN_TILE).astype(BF16)


def _block_diag_extract(dense):
    per = RNN_TILE // RNN_BLOCK_W
    d5 = dense.reshape(D_MODEL // RNN_TILE, per, RNN_BLOCK_W, per, RNN_BLOCK_W)
    blocks = jnp.stack([d5[:, p, :, p, :] for p in range(per)], axis=1)
    return blocks.reshape(D_MODEL // RNN_BLOCK_W, RNN_BLOCK_W, RNN_BLOCK_W)


def _local_step(x, p, target, w, *, n_seq, seq, comm=None):
    w = dict(w)

    def run(tag, fn, *args, **kwargs):
        hosted = comm.host(tag) if comm is not None else None
        if hosted is None:
            return fn(*args, **kwargs)
        *outs, landed = fn(*args, hosted=hosted, **kwargs)
        comm.landed(tag, landed, w)
        return outs[0] if len(outs) == 1 else outs

    def ready(batch, grads):
        if comm is not None:
            comm.ready(batch, grads)

    cos_t, sin_t = _rope_tables(seq)
    q_gain_t = jnp.tile(w["q_gain"], (1, N_Q_HEADS))
    k_gain_t = jnp.tile(w["k_gain"], (1, N_KV_HEADS))
    sinks_t = jnp.pad(w["sinks"], ((0, 0), (0, LANES - N_Q_HEADS)))
    wrg_bd, wig_bd = _block_diag_tiles(w["w_rg"]), _block_diag_tiles(w["w_ig"])
    dims = dict(n_seq=n_seq, seq=seq)

    h = _rmsnorm_fwd(x, w["g_mix"], name="norm_mix")
    z = run("mm_in", _matmul, h, w["w_in"], mode="nn", tm=1024, tn=512, out_dtypes=[F32], name="mm_in")
    xc, hr, ya_in = run("rnn_fwd", _rnn_fwd, z, w["conv_w"], w["conv_b"], wrg_bd, w["b_rg"], wig_bd, w["b_ig"],
                        w["lru_lambda"], **dims)
    o, lse = run("attn_fwd", _attn_fwd, z, cos_t, sin_t, q_gain_t, k_gain_t, sinks_t, **dims)
    ya = _matmul(ya_in, w["w_rnn_proj"], mode="nn", tm=1024, tn=512, out_dtypes=[F32], name="mm_rnn_proj")
    yb = _matmul(o, w["w_attn_proj"], mode="nn", tm=1024, tn=512, out_dtypes=[F32], name="mm_attn_proj")
    merged = _merge_fwd(z, ya, yb)
    x1 = _matmul(merged, w["w_out"], mode="nn", tm=1024, tn=512, out_dtypes=[F32], name="mm_out",
                 epilogue=lambda acc, res: (res + acc,), extras=(x,))
    hm = _rmsnorm_fwd(x1, w["g_mlp"], name="norm_mlp")
    u, act = _matmul(hm, w["w_up"], mode="nn", tm=1024, tn=512, out_dtypes=[F32, BF16], name="mm_up",
                     epilogue=lambda acc: (acc, jnp.square(jnp.maximum(acc, 0.0))))
    x2 = _matmul(act, w["w_down"], mode="nn", tm=512, tn=512, out_dtypes=[F32], name="mm_down",
                 epilogue=lambda acc, res: (res + acc,), extras=(x1,))
    hp = _rmsnorm_fwd(x2, w["g_ple"], name="norm_ple")
    gt = _matmul(hp, w["w_ple_gate"], mode="nn", tm=1024, tn=512, out_dtypes=[F32], name="mm_ple_gate")
    p_bf = p.astype(BF16)
    e = _matmul(p_bf, w["w_ple_proj"], mode="nn", tm=1024, tn=512, out_dtypes=[F32], name="mm_ple_proj")
    loss_row, dx3, dgt, de = _loss_head(x2, e, gt, target)

    g = {}
    g["w_ple_proj"] = _matmul_tn(p_bf, de, tk=PLE_DIM, tn=1024, tt=1024, name="mm_d_ple_proj",
                                 slot_cols=D_MODEL // N_DEV)
    g["w_ple_gate"] = _matmul_tn(hp, dgt, tk=1024, tn=1024, tt=512, name="mm_d_ple_gate")
    dhp = _matmul(dgt, w["w_ple_gate"], mode="nt", tm=1024, tn=512, out_dtypes=[F32], name="mm_dhp")
    dx2, dx2_bf, g["g_ple"] = _rmsnorm_bwd(x2, w["g_ple"], dhp, dx3, name="norm_ple_bwd", want_bf16=True)
    g["w_down"] = _matmul_tn(act, dx2_bf, tk=1024, tn=1024, tt=512, name="mm_d_down")
    du = _matmul(dx2_bf, w["w_down"], mode="nt", tm=1024, tn=512, out_dtypes=[BF16], name="mm_dact",
                 epilogue=lambda acc, uu: (acc * (2.0 * jnp.maximum(uu, 0.0)),), extras=(u,))
    g["w_up"] = _matmul_tn(hm, du, tk=1024, tn=1024, tt=512, name="mm_d_up", slot_cols=D_FF // N_DEV)
    ready(1, g)
    dhm = run("mm_dhm", _matmul, du, w["w_up"], mode="nt", tm=512, tn=512, out_dtypes=[F32], name="mm_dhm")
    dx1, dx1_bf, g["g_mlp"] = _rmsnorm_bwd(x1, w["g_mlp"], dhm, dx2, name="norm_mlp_bwd", want_bf16=True)
    g["w_out"] = _matmul_tn(merged, dx1_bf, tk=1024, tn=1024, tt=512, name="mm_d_out")
    dmerged = _matmul(dx1_bf, w["w_out"], mode="nt", tm=1024, tn=512, out_dtypes=[F32], name="mm_dmerged")
    dya, dyb, dga, dgb = _merge_bwd(z, ya, yb, dmerged)
    g["w_rnn_proj"] = _matmul_tn(ya_in, dya, tk=1024, tn=1024, tt=512, name="mm_d_rnn_proj")
    g["w_attn_proj"] = _matmul_tn(o, dyb, tk=1024, tn=1024, tt=512, name="mm_d_attn_proj")
    ready(2, g)
    dya_in = run("mm_dya_in", _matmul, dya, w["w_rnn_proj"], mode="nt", tm=1024, tn=512, out_dtypes=[F32],
                 name="mm_dya_in")
    do = _matmul(dyb, w["w_attn_proj"], mode="nt", tm=1024, tn=512, out_dtypes=[BF16], name="mm_do")
    dx_rnn, dg_rnn, dwrg_dense, dwig_dense, rnn_vec = run(
        "rnn_bwd", _rnn_bwd, dya_in, z, xc, hr, w["conv_w"], wrg_bd, w["b_rg"], wig_bd, w["b_ig"], w["lru_lambda"],
        **dims)
    dq, dk, dv, attn_vec = run("attn_bwd", _attn_bwd, z, o, lse, do, cos_t, sin_t, q_gain_t, k_gain_t, sinks_t,
                               **dims)
    dz = jnp.concatenate([dx_rnn, dg_rnn, dq.astype(BF16), dk, dv, dga, dgb], axis=1)
    g["w_in"] = _matmul_tn(h, dz, tk=1024, tn=IN_TOTAL // 4, tt=512, name="mm_d_in")
    g["w_rg"] = _block_diag_extract(dwrg_dense)
    g["w_ig"] = _block_diag_extract(dwig_dense)
    g["b_rg"], g["b_ig"], g["lru_lambda"], g["conv_b"] = (rnn_vec[i:i + 1] for i in range(4))
    g["conv_w"] = rnn_vec[4:8]
    attn_vec = attn_vec[0] if n_seq == 1 else functools.reduce(jnp.add, [attn_vec[b] for b in range(n_seq)])
    g["q_gain"] = attn_vec[0].reshape(N_Q_HEADS, HEAD_DIM).sum(axis=0)[None, :]
    g["k_gain"] = attn_vec[1, :KV_W].reshape(N_KV_HEADS, HEAD_DIM).sum(axis=0)[None, :]
    g["sinks"] = attn_vec[2:3, :N_Q_HEADS]
    ready(3, g)
    dh = run("mm_dh", _matmul, dz, w["w_in"], mode="nt", tm=512, tn=512, out_dtypes=[F32], name="mm_dh")
    grad_x, g["g_mix"] = run("norm_mix_bwd", _rmsnorm_bwd, x, w["g_mix"], dh, dx1, name="norm_mix_bwd",
                             want_bf16=False)
    return loss_row[0, 0], grad_x, g


MESH_ID = pl.DeviceIdType.MESH


def _coords(index):
    return (index >> 2) & 1, (index >> 1) & 1, index & 1


def _exchange(srcs, kinds, *, name):
    n = len(srcs)
    n_peer = N_DEV - 1

    def body(*refs):
        src, dst = refs[:n], refs[n:2 * n]
        send_sems, recv_sems, local_sems = refs[2 * n:]
        me = 4 * lax.axis_index("x") + 2 * lax.axis_index("y") + lax.axis_index("c")

        def remote(i, d):
            peer = (me + d) & (N_DEV - 1)
            piece = src[i] if kinds[i] == "gather" else src[i].at[peer]
            return pltpu.make_async_remote_copy(
                src_ref=piece, dst_ref=dst[i].at[me], send_sem=send_sems.at[i * n_peer + d - 1],
                recv_sem=recv_sems.at[i * n_peer + d - 1], device_id=_coords(peer), device_id_type=MESH_ID)

        def arrival(i, d):
            sender = (me - d) & (N_DEV - 1)
            piece = src[i] if kinds[i] == "gather" else src[i].at[sender]
            return pltpu.make_async_remote_copy(
                src_ref=piece, dst_ref=dst[i].at[sender], send_sem=send_sems.at[i * n_peer + d - 1],
                recv_sem=recv_sems.at[i * n_peer + d - 1], device_id=_coords(sender), device_id_type=MESH_ID)

        own = []
        for i in range(n):
            piece = src[i] if kinds[i] == "gather" else src[i].at[me]
            own.append(pltpu.make_async_copy(piece, dst[i].at[me], local_sems.at[i]))
            own[-1].start()
        sent = [remote(i, d) for d in range(1, N_DEV) for i in range(n)]
        for cp in sent:
            cp.start()
        for d in range(1, N_DEV):
            for i in range(n):
                arrival(i, d).wait_recv()
        for cp in sent:
            cp.wait_send()
        for cp in own:
            cp.wait()

    def out_of(s, kind):
        shape = s.shape if kind == "scatter" else (N_DEV,) + s.shape
        return jax.ShapeDtypeStruct(shape, s.dtype)

    any_spec = pl.BlockSpec(memory_space=pl.ANY)
    return pl.pallas_call(
        body,
        in_specs=[any_spec] * n,
        out_specs=[any_spec] * n,
        out_shape=[out_of(s, k) for s, k in zip(srcs, kinds)],
        scratch_shapes=[pltpu.SemaphoreType.DMA((n * n_peer,)), pltpu.SemaphoreType.DMA((n * n_peer,)),
                        pltpu.SemaphoreType.DMA((n,))],
        compiler_params=pltpu.CompilerParams(has_side_effects=True),
        name=name,
    )(*srcs)


def _remote(src, dst, send_sem, recv_sem, to):
    return pltpu.make_async_remote_copy(src_ref=src, dst_ref=dst, send_sem=send_sem, recv_sem=recv_sem,
                                        device_id=to, device_id_type=MESH_ID)


def _gather_two_level(shards, *, name):
    n = len(shards)
    per = N_DEV - 1

    def body(*refs):
        src, dst = refs[:n], refs[n:2 * n]
        send_sems, recv_sems, local_sems = refs[2 * n:]
        x, y, c = lax.axis_index("x"), lax.axis_index("y"), lax.axis_index("c")
        me, sibling = (x, y, c), (x, y, 1 - c)
        chips = [(1 - x, y), (x, 1 - y), (1 - x, 1 - y)]

        def slot(pos):
            return 4 * pos[0] + 2 * pos[1] + pos[2]

        def copy(i, k, block, to, from_shard=False):
            source = src[i] if from_shard else dst[i].at[slot(block)]
            return _remote(source, dst[i].at[slot(block)], send_sems.at[i * per + k], recv_sems.at[i * per + k], to)

        mine = [pltpu.make_async_copy(src[i], dst[i].at[slot(me)], local_sems.at[i]) for i in range(n)]
        for cp in mine:
            cp.start()
        first = []
        for i in range(n):
            first.append(copy(i, 0, me, sibling, from_shard=True))
            first += [copy(i, 1 + j, me, (*chip, c), from_shard=True) for j, chip in enumerate(chips)]
        for cp in first:
            cp.start()
        passed = []
        for i in range(n):
            for j, chip in enumerate(chips):
                copy(i, 1 + j, (*chip, c), me).wait_recv()
                passed.append(copy(i, 4 + j, (*chip, c), sibling))
                passed[-1].start()
        for i in range(n):
            copy(i, 0, sibling, me).wait_recv()
            for j, chip in enumerate(chips):
                copy(i, 4 + j, (*chip, 1 - c), me).wait_recv()
        for cp in first + passed:
            cp.wait_send()
        for cp in mine:
            cp.wait()

    any_spec = pl.BlockSpec(memory_space=pl.ANY)
    return pl.pallas_call(
        body,
        in_specs=[any_spec] * n,
        out_specs=[any_spec] * n,
        out_shape=[jax.ShapeDtypeStruct((N_DEV,) + s.shape, s.dtype) for s in shards],
        scratch_shapes=[pltpu.SemaphoreType.DMA((n * per,)), pltpu.SemaphoreType.DMA((n * per,)),
                        pltpu.SemaphoreType.DMA((n,))],
        name=name,
    )(*shards)


def _hosted_gather(shards):
    n = len(shards)
    per = N_DEV - 1

    def plan(src, dst, send_sems, recv_sems, local_sems):
        me = 4 * lax.axis_index("x") + 2 * lax.axis_index("y") + lax.axis_index("c")
        copies = []
        for i in range(n):
            own = pltpu.make_async_copy(src[i], dst[i].at[me], local_sems.at[i])
            copies.append(_Xfer(own.start, own.wait))
        for d in range(1, N_DEV):
            peer, sender = (me + d) & (N_DEV - 1), (me - d) & (N_DEV - 1)
            for i in range(n):
                k = i * per + d - 1
                out = _remote(src[i], dst[i].at[me], send_sems.at[k], recv_sems.at[k], _coords(peer))
                arrival = _remote(src[i], dst[i].at[sender], send_sems.at[k], recv_sems.at[k], _coords(sender))

                def wait(out=out, arrival=arrival):
                    arrival.wait_recv()
                    out.wait_send()

                copies.append(_Xfer(out.start, wait))
        return copies

    out_shape = tuple(jax.ShapeDtypeStruct((N_DEV,) + s.shape, s.dtype) for s in shards)
    return _Hosted(tuple(shards), out_shape, n * per, plan)


CHIPS = N_DEV // 2


def _hosted_sibling_swap(arrays, sliced):
    n_sems = sum(CHIPS if s else 1 for s in sliced)

    def plan(src, dst, send_sems, recv_sems, local_sems):
        x, y, c = lax.axis_index("x"), lax.axis_index("y"), lax.axis_index("c")
        sibling = (x, y, 1 - c)
        copies, k = [], 0
        for i, is_sliced in enumerate(sliced):
            pieces = [(src[i].at[2 * s + 1 - c], dst[i].at[s]) for s in range(CHIPS)] if is_sliced else [(src[i], dst[i])]
            for source, target in pieces:
                cp = _remote(source, target, send_sems.at[k], recv_sems.at[k], sibling)
                copies.append(_Xfer(cp.start, cp.wait))
                k += 1
        return copies

    out_shape = tuple(jax.ShapeDtypeStruct((CHIPS,) + a.shape[1:] if s else a.shape, a.dtype)
                      for a, s in zip(arrays, sliced))
    return _Hosted(tuple(arrays), out_shape, n_sems, plan)


def _hosted_chip_exchange(arrays, sliced):
    n = len(arrays)
    per = CHIPS - 1

    def plan(src, dst, send_sems, recv_sems, local_sems):
        x, y, c = lax.axis_index("x"), lax.axis_index("y"), lax.axis_index("c")
        chip = 2 * x + y
        copies = []
        for i in range(n):
            own = pltpu.make_async_copy(src[i].at[chip] if sliced[i] else src[i], dst[i].at[chip], local_sems.at[i])
            copies.append(_Xfer(own.start, own.wait))
        for d in range(1, CHIPS):
            other = chip ^ d
            to = ((other >> 1) & 1, other & 1, c)
            for i in range(n):
                k = i * per + d - 1
                source = src[i].at[other] if sliced[i] else src[i]
                out = _remote(source, dst[i].at[chip], send_sems.at[k], recv_sems.at[k], to)
                arrival = _remote(source, dst[i].at[other], send_sems.at[k], recv_sems.at[k], to)

                def wait(out=out, arrival=arrival):
                    arrival.wait_recv()
                    out.wait_send()

                copies.append(_Xfer(out.start, wait))
        return copies

    out_shape = tuple(jax.ShapeDtypeStruct(a.shape if s else (CHIPS,) + a.shape, a.dtype)
                      for a, s in zip(arrays, sliced))
    return _Hosted(tuple(arrays), out_shape, n * per, plan)


def _add_sibling(parts, received, core, *, name):
    _, r, cols = parts.shape
    tr = min(256, r)

    def body(core_ref, a_ref, b_ref, o_ref):
        o_ref[...] = (a_ref[...] + b_ref[...]).astype(BF16)

    grid_spec = pltpu.PrefetchScalarGridSpec(
        num_scalar_prefetch=1,
        grid=(CHIPS, r // tr),
        in_specs=[pl.BlockSpec((None, tr, cols), lambda k, i, core_ref: (2 * k + core_ref[0], i, 0)),
                  pl.BlockSpec((None, tr, cols), lambda k, i, core_ref: (k, i, 0))],
        out_specs=pl.BlockSpec((None, tr, cols), lambda k, i, core_ref: (k, i, 0)),
    )
    return pl.pallas_call(body, grid_spec=grid_spec, out_shape=jax.ShapeDtypeStruct((CHIPS, r, cols), BF16),
                          compiler_params=_params("parallel", "parallel"), name=name)(core, parts, received)


def _add_whole(a, b, *, name):
    def body(a_ref, b_ref, o_ref):
        o_ref[...] = a_ref[...] + b_ref[...]

    return pl.pallas_call(body, out_shape=jax.ShapeDtypeStruct(a.shape, F32), name=name)(a, b)


def _adamw(parts, w, m, v, *, name):
    r, c = w.shape
    n_parts = parts.shape[0]
    tr = min(256, r)
    c1 = 1.0 - ADAM_B1 ** ADAM_STEP
    c2 = 1.0 - ADAM_B2 ** ADAM_STEP

    def body(p_ref, w_ref, m_ref, v_ref, g_ref, d_ref, nm_ref, nv_ref):
        g = p_ref[0].astype(F32)
        for s in range(1, n_parts):
            g = g + p_ref[s].astype(F32)
        nm = ADAM_B1 * m_ref[...] + (1.0 - ADAM_B1) * g
        nv = ADAM_B2 * v_ref[...] + (1.0 - ADAM_B2) * (g * g)
        g_ref[...] = g
        nm_ref[...] = nm
        nv_ref[...] = nv
        d_ref[...] = -ADAM_LR * ((nm / c1) / (jnp.sqrt(nv / c2) + ADAM_EPS) + ADAM_WD * w_ref[...])

    tile = pl.BlockSpec((tr, c), lambda i: (i, 0))
    return pl.pallas_call(
        body,
        grid=(r // tr,),
        in_specs=[pl.BlockSpec((n_parts, tr, c), lambda i: (0, i, 0)), tile, tile, tile],
        out_specs=[tile] * 4,
        out_shape=[jax.ShapeDtypeStruct((r, c), F32)] * 4,
        compiler_params=_params("parallel"),
        name=name,
    )(parts, w, m, v)


BIG = ("w_in", "w_rnn_proj", "w_attn_proj", "w_out", "w_up", "w_down", "w_ple_gate", "w_ple_proj")
SMALL = (("conv_b", 1), ("b_rg", 1), ("b_ig", 1), ("lru_lambda", 1), ("g_mlp", 1), ("g_ple", 1),
         ("conv_w", 4), ("q_gain", 1), ("k_gain", 1), ("sinks", 1), ("w_rg", 64), ("w_ig", 64))
SMALL_ROWS = 144
ROW_SHARDED = ("w_rnn_proj", "w_attn_proj", "w_out", "w_down", "w_ple_gate")
COL_SHARDED = ("w_in", "w_up", "w_ple_proj")
BATCHES = {1: ("w_ple_proj", "w_ple_gate", "w_down", "w_up"), 2: ("w_out", "w_rnn_proj", "w_attn_proj"), 3: ("w_in",)}


def _pack_small(vals):
    rows = []
    for nm, nrow in SMALL:
        flat = vals[nm].reshape(-1).astype(F32)
        rows.append(jnp.pad(flat, (0, nrow * D_MODEL - flat.shape[0])).reshape(nrow, D_MODEL))
    used = sum(nrow for _, nrow in SMALL)
    rows.append(jnp.zeros((SMALL_ROWS - used, D_MODEL), F32))
    return jnp.concatenate(rows, axis=0)


def _unpack_small(packed, shapes):
    out, at = {}, 0
    for nm, nrow in SMALL:
        size = 1
        for s in shapes[nm]:
            size *= s
        out[nm] = packed[at:at + nrow].reshape(-1)[:size].reshape(shapes[nm])
        at += nrow
    return out


def _full_weight(name, landed):
    if name in COL_SHARDED:
        return landed.transpose(1, 0, 2).reshape(landed.shape[1], N_DEV * landed.shape[2])
    return landed.reshape(N_DEV * landed.shape[1], landed.shape[2])


def _owner_slots(name, grad):
    if name == "w_in":
        return grad.reshape(D_MODEL, N_DEV, IN_TOTAL // N_DEV).transpose(1, 0, 2)
    if name in COL_SHARDED:
        return grad
    return grad.reshape(N_DEV, grad.shape[0] // N_DEV, grad.shape[1])


class _StepExchanges:
    GATHERS = {"mm_in": ("w_rnn_proj", "w_attn_proj", "w_out"), "rnn_fwd": ("w_up",),
               "attn_fwd": ("w_down", "w_ple_gate", "w_ple_proj")}
    SWAPS = {"mm_dhm": 1, "mm_dya_in": 2}
    CHIP_EXCHANGES = {"rnn_bwd": 1, "attn_bwd": 2, "mm_dh": 3}

    def __init__(self, shards, core):
        self.shards = shards
        self.core = core
        self.parts, self.swapped, self.summed = {}, {}, {}

    def ready(self, batch, grads):
        arrays = [_owner_slots(nm, grads[nm]) for nm in BATCHES[batch]]
        sliced = [True] * len(arrays)
        if batch == 3:
            arrays.append(_pack_small(grads))
            sliced.append(False)
        self.parts[batch] = (arrays, sliced)
        if batch not in self.SWAPS.values():
            _, self.swapped[batch] = _call(
                lambda: None, grid=(1,), in_specs=[], out_specs=[], out_shape=[], args=(), name="swap_last",
                semantics=("arbitrary",), hosted=_hosted_sibling_swap(arrays, sliced))

    def host(self, tag):
        if tag in self.GATHERS:
            return _hosted_gather([self.shards[nm] for nm in self.GATHERS[tag]])
        if tag in self.SWAPS:
            return _hosted_sibling_swap(*self.parts[self.SWAPS[tag]])
        if tag in self.CHIP_EXCHANGES:
            batch = self.CHIP_EXCHANGES[tag]
            arrays, sliced = self.parts[batch]
            labels = list(BATCHES[batch]) + ["small"]
            sums = [_add_sibling(a, r, self.core, name="add_" + lb) if s else _add_whole(a, r, name="add_" + lb)
                    for a, r, s, lb in zip(arrays, self.swapped[batch], sliced, labels)]
            return _hosted_chip_exchange(sums, sliced)
        return None

    def landed(self, tag, landed, weights):
        if tag in self.GATHERS:
            for nm, buf in zip(self.GATHERS[tag], landed):
                weights[nm] = _full_weight(nm, buf)
        elif tag in self.SWAPS:
            self.swapped[self.SWAPS[tag]] = landed
        else:
            self.summed[self.CHIP_EXCHANGES[tag]] = landed


def kernel(x, p, g_mix, w_in, conv_w, conv_b, w_rg, b_rg, w_ig, b_ig, lru_lambda, w_rnn_proj, q_gain, k_gain, sinks, w_attn_proj, w_out, g_mlp, w_up, w_down, g_ple, w_ple_gate, w_ple_proj, loss_target, m_g_mix, m_w_in, m_conv_w, m_conv_b, m_w_rg, m_b_rg, m_w_ig, m_b_ig, m_lru_lambda, m_w_rnn_proj, m_q_gain, m_k_gain, m_sinks, m_w_attn_proj, m_w_out, m_g_mlp, m_w_up, m_w_down, m_g_ple, m_w_ple_gate, m_w_ple_proj, v_g_mix, v_w_in, v_conv_w, v_conv_b, v_w_rg, v_b_rg, v_w_ig, v_b_ig, v_lru_lambda, v_w_rnn_proj, v_q_gain, v_k_gain, v_sinks, v_w_attn_proj, v_w_out, v_g_mlp, v_w_up, v_w_down, v_g_ple, v_w_ple_gate, v_w_ple_proj):
    names = ("g_mix", "w_in", "conv_w", "conv_b", "w_rg", "b_rg", "w_ig", "b_ig", "lru_lambda", "w_rnn_proj",
             "q_gain", "k_gain", "sinks", "w_attn_proj", "w_out", "g_mlp", "w_up", "w_down", "g_ple",
             "w_ple_gate", "w_ple_proj")
    wts = dict(zip(names, (g_mix, w_in, conv_w, conv_b, w_rg, b_rg, w_ig, b_ig, lru_lambda, w_rnn_proj, q_gain,
                           k_gain, sinks, w_attn_proj, w_out, g_mlp, w_up, w_down, g_ple, w_ple_gate, w_ple_proj)))
    mom1 = dict(zip(names, (m_g_mix, m_w_in, m_conv_w, m_conv_b, m_w_rg, m_b_rg, m_w_ig, m_b_ig, m_lru_lambda,
                            m_w_rnn_proj, m_q_gain, m_k_gain, m_sinks, m_w_attn_proj, m_w_out, m_g_mlp, m_w_up,
                            m_w_down, m_g_ple, m_w_ple_gate, m_w_ple_proj)))
    mom2 = dict(zip(names, (v_g_mix, v_w_in, v_conv_w, v_conv_b, v_w_rg, v_b_rg, v_w_ig, v_b_ig, v_lru_lambda,
                            v_w_rnn_proj, v_q_gain, v_k_gain, v_sinks, v_w_attn_proj, v_w_out, v_g_mlp, v_w_up,
                            v_w_down, v_g_ple, v_w_ple_gate, v_w_ple_proj)))
    n_seq, seq, _ = x.shape
    me = 4 * lax.axis_index("x") + 2 * lax.axis_index("y") + lax.axis_index("c")
    core = lax.axis_index("c").astype(jnp.int32).reshape(1)

    shards = {nm: wts[nm][0].astype(BF16) for nm in BIG}
    w_in_all, conv_all = _gather_two_level([shards["w_in"], conv_w[0]], name="gather_w_in")
    w = {nm: wts[nm] for nm in names if nm not in BIG}
    w["w_rg"], w["w_ig"] = w_rg[0], w_ig[0]
    w["conv_w"] = conv_all.transpose(1, 0, 2).reshape(CONV_W, D_MODEL)
    w["w_in"] = _full_weight("w_in", w_in_all)
    comm = _StepExchanges(shards, core)
    loss_sum, grad_x, g = _local_step(
        x.reshape(n_seq * seq, D_MODEL), p.reshape(n_seq * seq, PLE_DIM), loss_target.reshape(n_seq * seq, D_MODEL),
        w, n_seq=n_seq, seq=seq, comm=comm)
    loss = lax.psum(loss_sum, ("x", "y", "c")) * (0.5 / D_MODEL)

    res = {}
    for batch, batch_names in BATCHES.items():
        for nm, summed in zip(batch_names, comm.summed[batch]):
            res[nm] = _adamw(summed, wts[nm][0], mom1[nm][0], mom2[nm][0], name="adamw_" + nm)
    g_mix_parts, = _exchange([g["g_mix"]], ["gather"], name="gather_g_mix")
    res["g_mix"] = [r[0] for r in _adamw(g_mix_parts, g_mix, m_g_mix, v_g_mix, name="adamw_g_mix")]
    small_names = [nm for nm, _ in SMALL]
    conv_lanes = D_MODEL // N_DEV
    full_small = {}
    for src, key in ((wts, "w"), (mom1, "m"), (mom2, "v")):
        vals = {nm: src[nm][0] for nm in small_names if nm != "conv_w"}
        vals["conv_w"] = lax.dynamic_update_slice(jnp.zeros((CONV_W, D_MODEL), F32), src["conv_w"][0], (0, me * conv_lanes))
        full_small[key] = _pack_small(vals)
    small_res = _adamw(comm.summed[3][-1], full_small["w"], full_small["m"], full_small["v"], name="adamw_small")
    shapes = {nm: wts[nm].shape[1:] for nm in small_names}
    shapes["conv_w"] = (CONV_W, D_MODEL)
    small_out = [_unpack_small(r, shapes) for r in small_res]
    for nm in small_names:
        vals = [so[nm] for so in small_out]
        if nm == "conv_w":
            vals = [lax.dynamic_slice(a, (0, me * conv_lanes), (CONV_W, conv_lanes)) for a in vals]
        res[nm] = vals

    outs = [loss, grad_x.reshape(n_seq, seq, D_MODEL)]
    for k in range(4):
        outs.extend(res[nm][k][None] for nm in names)
    return tuple(outs)
```

```python
import functools
from typing import Callable, NamedTuple

import jax
import jax.numpy as jnp
from jax import lax
from jax.experimental import pallas as pl
from jax.experimental.pallas import tpu as pltpu

F32 = jnp.float32
BF16 = jnp.bfloat16

N_DEV = 8
D_MODEL = 1024
RNN_BLOCK_W = 64
CONV_W = 4
LRU_C = 8.0
HEAD_DIM = 64
N_Q_HEADS = 16
N_KV_HEADS = 4
KV_W = N_KV_HEADS * HEAD_DIM
WINDOW = 128
ROPE_THETA = 10000.0
D_FF = 4096
PLE_DIM = 256
NORM_EPS = 1e-6
IN_TOTAL = 5632
COL_XRNN, COL_GRNN, COL_Q, COL_K, COL_V, COL_GA, COL_GB = 0, 1024, 2048, 3072, 3328, 3584, 4608

ADAM_LR = 0.001
ADAM_B1 = 0.9
ADAM_B2 = 0.999
ADAM_EPS = 1e-08
ADAM_WD = 0.01
ADAM_STEP = 10

LANES = 128
SUBLANES = 8
RNN_TILE = 256
VMEM_LIMIT = 48 * 1024 * 1024
NEG_BIG = -1e30


def _params(*sem):
    return pltpu.CompilerParams(dimension_semantics=sem if sem else None, vmem_limit_bytes=VMEM_LIMIT)


def _sig(x):
    return 0.5 * jnp.tanh(0.5 * x) + 0.5


def _dot_nt(a, b):
    return lax.dot_general(a, b, (((1,), (1,)), ((), ())), preferred_element_type=F32)


def _dot_tn(a, b):
    return lax.dot_general(a, b, (((0,), (0,)), ((), ())), preferred_element_type=F32)


class _Xfer:
    def __init__(self, start, wait):
        self.start, self.wait = start, wait


class _Hosted(NamedTuple):
    srcs: tuple
    out_shape: tuple
    n_sems: int
    plan: Callable


def _call(body, *, grid, in_specs, out_specs, out_shape, args, name, semantics, scratch_shapes=(), hosted=None):
    if hosted is None:
        outs = pl.pallas_call(body, grid=grid, in_specs=list(in_specs), out_specs=list(out_specs),
                              out_shape=list(out_shape), scratch_shapes=list(scratch_shapes),
                              compiler_params=_params(*semantics), name=name)(*args)
        return list(outs), []
    counts = (len(in_specs), len(hosted.srcs), len(out_specs), len(hosted.out_shape), len(scratch_shapes), 3)

    def wrapped(*refs):
        at, groups = 0, []
        for count in counts:
            groups.append(refs[at:at + count])
            at += count
        ins, srcs, outs, dsts, scratch, sems = groups
        copies = hosted.plan(srcs, dsts, *sems)
        ids = [pl.program_id(axis) for axis in range(len(grid))]
        first = functools.reduce(jnp.logical_and, [i == 0 for i in ids])
        last = functools.reduce(jnp.logical_and, [i == g - 1 for i, g in zip(ids, grid)])

        @pl.when(first)
        def _():
            for cp in copies:
                cp.start()

        body(*ins, *outs, *scratch)

        @pl.when(last)
        def _():
            for cp in copies:
                cp.wait()

    any_spec = pl.BlockSpec(memory_space=pl.ANY)
    sems = [pltpu.SemaphoreType.DMA((hosted.n_sems,))] * 3
    outs = pl.pallas_call(
        wrapped, grid=grid, in_specs=list(in_specs) + [any_spec] * counts[1],
        out_specs=list(out_specs) + [any_spec] * counts[3], out_shape=list(out_shape) + list(hosted.out_shape),
        scratch_shapes=list(scratch_shapes) + sems, compiler_params=_params(*["arbitrary"] * len(grid)),
        name=name)(*args, *hosted.srcs)
    return list(outs[:counts[2]]), list(outs[counts[2]:])


def _matmul(a, b, *, mode, tm, tn, out_dtypes, name, epilogue=None, extras=(), hosted=None):
    m, k = a.shape
    n = b.shape[1] if mode == "nn" else b.shape[0]
    tm, tn = min(tm, m), min(tn, n)
    n_extra = len(extras)

    def body(a_ref, b_ref, *rest):
        extra_refs, out_refs = rest[:n_extra], rest[n_extra:]
        if mode == "nn":
            acc = jnp.dot(a_ref[...], b_ref[...], preferred_element_type=F32)
        else:
            acc = _dot_nt(a_ref[...], b_ref[...])
        res = epilogue(acc, *[e[...] for e in extra_refs]) if epilogue is not None else (acc,)
        for o_ref, r in zip(out_refs, res):
            o_ref[...] = r.astype(o_ref.dtype)

    b_spec = pl.BlockSpec((k, tn), lambda i, j: (0, j)) if mode == "nn" else pl.BlockSpec((tn, k), lambda i, j: (j, 0))
    tile = pl.BlockSpec((tm, tn), lambda i, j: (i, j))
    outs, landed = _call(
        body,
        grid=(m // tm, n // tn),
        in_specs=[pl.BlockSpec((tm, k), lambda i, j: (i, 0)), b_spec] + [tile] * n_extra,
        out_specs=[tile] * len(out_dtypes),
        out_shape=[jax.ShapeDtypeStruct((m, n), dt) for dt in out_dtypes],
        args=(a, b, *extras), name=name, semantics=("parallel", "arbitrary"), hosted=hosted)
    if hosted is not None:
        return (*outs, landed)
    return outs[0] if len(outs) == 1 else outs


def _matmul_tn(a, b, *, tk, tn, tt, name, slot_cols=None):
    t, k = a.shape
    n = b.shape[1]
    tk, tn, tt = min(tk, k), min(tn, n), min(tt, t)

    def body(a_ref, b_ref, o_ref):
        @pl.when(pl.program_id(2) == 0)
        def _():
            o_ref[...] = jnp.zeros_like(o_ref)

        if slot_cols is None:
            o_ref[...] += _dot_tn(a_ref[...], b_ref[...])
        else:
            av = a_ref[...]
            for s in range(tn // slot_cols):
                o_ref[s] += _dot_tn(av, b_ref[:, s * slot_cols:(s + 1) * slot_cols])

    if slot_cols is not None:
        out_spec = pl.BlockSpec((tn // slot_cols, tk, slot_cols), lambda i, j, s: (j, i, 0))
        out_shape = jax.ShapeDtypeStruct((n // slot_cols, k, slot_cols), F32)
    else:
        out_spec = pl.BlockSpec((tk, tn), lambda i, j, s: (i, j))
        out_shape = jax.ShapeDtypeStruct((k, n), F32)
    return pl.pallas_call(
        body,
        grid=(k // tk, n // tn, t // tt),
        in_specs=[pl.BlockSpec((tt, tk), lambda i, j, s: (s, i)), pl.BlockSpec((tt, tn), lambda i, j, s: (s, j))],
        out_specs=out_spec,
        out_shape=out_shape,
        compiler_params=_params("parallel", "parallel", "arbitrary"),
        name=name,
    )(a, b)


def _rmsnorm_fwd(x, g, *, name):
    t, d = x.shape
    tm = min(512, t)

    def body(x_ref, g_ref, o_ref):
        xv = x_ref[...]
        r = lax.rsqrt(jnp.mean(xv * xv, axis=-1, keepdims=True) + NORM_EPS)
        o_ref[...] = (xv * r * g_ref[...]).astype(BF16)

    return pl.pallas_call(
        body,
        grid=(t // tm,),
        in_specs=[pl.BlockSpec((tm, d), lambda i: (i, 0)), pl.BlockSpec((1, d), lambda i: (0, 0))],
        out_specs=pl.BlockSpec((tm, d), lambda i: (i, 0)),
        out_shape=jax.ShapeDtypeStruct((t, d), BF16),
        compiler_params=_params("parallel"),
        name=name,
    )(x, g)


def _rmsnorm_bwd(x, g, dy, dres, *, name, want_bf16, hosted=None):
    t, d = x.shape
    tm = min(256, t)

    def body(x_ref, g_ref, dy_ref, dres_ref, *out_refs):
        dx_ref, dg_ref = out_refs[0], out_refs[-1]
        xv, dyv = x_ref[...], dy_ref[...]
        r = lax.rsqrt(jnp.mean(xv * xv, axis=-1, keepdims=True) + NORM_EPS)
        xr = xv * r
        gy = dyv * g_ref[...]
        dx = dres_ref[...] + r * (gy - xr * jnp.mean(gy * xr, axis=-1, keepdims=True))
        dx_ref[...] = dx
        if want_bf16:
            out_refs[1][...] = dx.astype(BF16)

        @pl.when(pl.program_id(0) == 0)
        def _():
            dg_ref[...] = jnp.zeros_like(dg_ref)

        dg_ref[...] += jnp.sum(dyv * xr, axis=0, keepdims=True)

    tile = pl.BlockSpec((tm, d), lambda i: (i, 0))
    vec = pl.BlockSpec((1, d), lambda i: (0, 0))
    out_specs = [tile] + ([tile] if want_bf16 else []) + [vec]
    out_shape = [jax.ShapeDtypeStruct((t, d), F32)] + ([jax.ShapeDtypeStruct((t, d), BF16)] if want_bf16 else [])
    out_shape.append(jax.ShapeDtypeStruct((1, d), F32))
    outs, landed = _call(body, grid=(t // tm,), in_specs=[tile, vec, tile, tile], out_specs=out_specs,
                         out_shape=out_shape, args=(x, g, dy, dres), name=name, semantics=("arbitrary",), hosted=hosted)
    return (*outs, landed) if hosted is not None else outs


def _softplus_neg(lam):
    z = -lam
    return jnp.maximum(z, 0.0) + jnp.log1p(jnp.exp(-jnp.abs(z)))


def _neg_expm1(y, exp_half_y):
    series = -y * (1.0 + y * 0.5 * (1.0 + y * (1.0 / 3.0) * (1.0 + y * 0.25 * (1.0 + y * 0.2))))
    return jnp.where(y > -0.0625, series, 1.0 - exp_half_y * exp_half_y)


def _gelu_parts(x):
    c = 0.7978845608028654
    u = c * (x + 0.044715 * x * x * x)
    th = jnp.tanh(u)
    gel = 0.5 * x * (1.0 + th)
    dgel = 0.5 * (1.0 + th) + 0.5 * x * (1.0 - th * th) * c * (1.0 + 3.0 * 0.044715 * x * x)
    return gel, dgel


def _shift_down(v, k, rows):
    return jnp.where(rows < k, 0.0, pltpu.roll(v, k, 0))


def _shift_up(v, k, rows, n):
    return jnp.where(rows >= n - k, 0.0, pltpu.roll(v, n - k, 0))


def _scan_within_groups(a, b, rows, *, reverse):
    n = a.shape[0]
    in_group = rows & (SUBLANES - 1)
    for s in (1, 2, 4):
        if reverse:
            inside, shift = in_group < SUBLANES - s, n - s
        else:
            inside, shift = in_group >= s, s
        b = b + a * jnp.where(inside, pltpu.roll(b, shift, 0), 0.0)
        a = a * jnp.where(inside, pltpu.roll(a, shift, 0), 1.0)
    return a, b


def _rnn_gates(xc, wrg, brg, wig, big, lam):
    xcb = xc.astype(BF16)
    r = _sig(jnp.dot(xcb, wrg, preferred_element_type=F32) + brg)
    i = _sig(jnp.dot(xcb, wig, preferred_element_type=F32) + big)
    sp = _softplus_neg(lam)
    log_a = -LRU_C * r * sp
    a = jnp.exp(log_a)
    mult = jnp.sqrt(_neg_expm1(2.0 * log_a, a))
    return xcb, r, i, sp, a, mult


def _conv_fwd(xv, cw, cb, rows):
    return (cb + _shift_down(xv, 3, rows) * cw[0:1, :] + _shift_down(xv, 2, rows) * cw[1:2, :]
            + _shift_down(xv, 1, rows) * cw[2:3, :] + xv * cw[3:4, :])


def _rnn_fwd(z, conv_w, conv_b, wrg_bd, b_rg, wig_bd, b_ig, lam, *, n_seq, seq, hosted=None):
    t = n_seq * seq
    ct = RNN_TILE
    n_ct = D_MODEL // ct

    def body(x_ref, g_ref, cw_ref, cb_ref, wrg_ref, brg_ref, wig_ref, big_ref, lam_ref,
             xc_ref, hr_ref, ya_ref, a_s, b_s):
        rows = lax.broadcasted_iota(jnp.int32, (seq, ct), 0)
        xc = _conv_fwd(x_ref[...], cw_ref[...], cb_ref[...], rows)
        _, r, i, sp, a, mult = _rnn_gates(xc, wrg_ref[...], brg_ref[...], wig_ref[...], big_ref[...], lam_ref[...])
        a_s[...], b_s[...] = _scan_within_groups(a, mult * (i * xc), rows, reverse=False)

        def step(j, carry):
            r0 = pl.multiple_of(j * SUBLANES, SUBLANES)
            h = b_s[pl.ds(r0, SUBLANES), :] + a_s[pl.ds(r0, SUBLANES), :] * carry
            hr_ref[pl.ds(r0, SUBLANES), :] = h
            return h[SUBLANES - 1:SUBLANES, :]

        lax.fori_loop(0, seq // SUBLANES, step, jnp.zeros((1, ct), F32), unroll=4)
        gel, _ = _gelu_parts(g_ref[...])
        xc_ref[...] = xc
        ya_ref[...] = (hr_ref[...] * gel).astype(BF16)

    vec = pl.BlockSpec((1, ct), lambda b, c: (0, c))
    gate_w = pl.BlockSpec((None, ct, ct), lambda b, c: (c, 0, 0))
    tile = pl.BlockSpec((seq, ct), lambda b, c: (b, c))
    outs, landed = _call(
        body,
        grid=(n_seq, n_ct),
        in_specs=[
            pl.BlockSpec((seq, ct), lambda b, c: (b, COL_XRNN // ct + c)),
            pl.BlockSpec((seq, ct), lambda b, c: (b, COL_GRNN // ct + c)),
            pl.BlockSpec((CONV_W, ct), lambda b, c: (0, c)), vec, gate_w, vec, gate_w, vec, vec,
        ],
        out_specs=[tile, tile, tile],
        out_shape=[jax.ShapeDtypeStruct((t, D_MODEL), F32), jax.ShapeDtypeStruct((t, D_MODEL), F32),
                   jax.ShapeDtypeStruct((t, D_MODEL), BF16)],
        scratch_shapes=[pltpu.VMEM((seq, ct), F32), pltpu.VMEM((seq, ct), F32)],
        args=(z, z, conv_w, conv_b, wrg_bd, b_rg, wig_bd, b_ig, lam), name="rnn_fwd",
        semantics=("parallel", "parallel"), hosted=hosted)
    return (*outs, landed) if hosted is not None else outs


def _rnn_bwd(dya, z, xc, hr, conv_w, wrg_bd, b_rg, wig_bd, b_ig, lam, *, n_seq, seq, hosted=None):
    t = n_seq * seq
    ct = RNN_TILE
    n_ct = D_MODEL // ct

    def body(dya_ref, x_ref, g_ref, xc_ref, hr_ref, cw_ref, wrg_ref, brg_ref, wig_ref, big_ref, lam_ref,
             dx_ref, dg_ref, dwrg_ref, dwig_ref, vec_ref, a_s, d_s, g_s):
        rows = lax.broadcasted_iota(jnp.int32, (seq, ct), 0)
        xv, xc, hr, dyv = x_ref[...], xc_ref[...], hr_ref[...], dya_ref[...]
        lamv = lam_ref[...]
        gel, dgel = _gelu_parts(g_ref[...])
        dg_ref[...] = (dyv * hr * dgel).astype(BF16)
        xcb, r, i, sp, a, mult = _rnn_gates(xc, wrg_ref[...], brg_ref[...], wig_ref[...], big_ref[...], lamv)
        a_s[...], d_s[...] = _scan_within_groups(_shift_up(a, 1, rows, seq), dyv * gel, rows, reverse=True)

        def step(k, carry):
            r0 = pl.multiple_of((seq // SUBLANES - 1 - k) * SUBLANES, SUBLANES)
            gs = d_s[pl.ds(r0, SUBLANES), :] + a_s[pl.ds(r0, SUBLANES), :] * carry
            g_s[pl.ds(r0, SUBLANES), :] = gs
            return gs[0:1, :]

        lax.fori_loop(0, seq // SUBLANES, step, jnp.zeros((1, ct), F32), unroll=4)
        gsum = g_s[...]
        gated = i * xc
        d_log_a = gsum * _shift_down(hr, 1, rows) * a - gsum * gated * (a * a / mult)
        d_gated = gsum * mult
        d_pre_r = (d_log_a * (-LRU_C) * sp) * r * (1.0 - r)
        d_pre_i = (d_gated * xc) * i * (1.0 - i)
        dprb, dpib = d_pre_r.astype(BF16), d_pre_i.astype(BF16)
        dxc = d_gated * i + _dot_nt(dprb, wrg_ref[...]) + _dot_nt(dpib, wig_ref[...])
        cw = cw_ref[...]
        dx = (dxc * cw[3:4, :] + _shift_up(dxc, 1, rows, seq) * cw[2:3, :]
              + _shift_up(dxc, 2, rows, seq) * cw[1:2, :] + _shift_up(dxc, 3, rows, seq) * cw[0:1, :])
        dx_ref[...] = dx.astype(BF16)

        @pl.when(pl.program_id(1) == 0)
        def _():
            dwrg_ref[...] = jnp.zeros_like(dwrg_ref)
            dwig_ref[...] = jnp.zeros_like(dwig_ref)
            vec_ref[...] = jnp.zeros_like(vec_ref)

        dwrg_ref[...] += _dot_tn(xcb, dprb)
        dwig_ref[...] += _dot_tn(xcb, dpib)

        def colsum(v):
            return jnp.sum(v, axis=0, keepdims=True)

        d_sp = colsum(d_log_a * (-LRU_C) * r)
        vec_ref[0:1, :] += colsum(d_pre_r)
        vec_ref[1:2, :] += colsum(d_pre_i)
        vec_ref[2:3, :] += d_sp * (-_sig(-lamv))
        vec_ref[3:4, :] += colsum(dxc)
        vec_ref[4:5, :] += colsum(dxc * _shift_down(xv, 3, rows))
        vec_ref[5:6, :] += colsum(dxc * _shift_down(xv, 2, rows))
        vec_ref[6:7, :] += colsum(dxc * _shift_down(xv, 1, rows))
        vec_ref[7:8, :] += colsum(dxc * xv)

    vec = pl.BlockSpec((1, ct), lambda c, b: (0, c))
    gate_w = pl.BlockSpec((None, ct, ct), lambda c, b: (c, 0, 0))
    tile = pl.BlockSpec((seq, ct), lambda c, b: (b, c))
    outs, landed = _call(
        body,
        grid=(n_ct, n_seq),
        in_specs=[
            tile,
            pl.BlockSpec((seq, ct), lambda c, b: (b, COL_XRNN // ct + c)),
            pl.BlockSpec((seq, ct), lambda c, b: (b, COL_GRNN // ct + c)),
            tile, tile,
            pl.BlockSpec((CONV_W, ct), lambda c, b: (0, c)), gate_w, vec, gate_w, vec, vec,
        ],
        out_specs=[tile, tile, gate_w, gate_w, pl.BlockSpec((8, ct), lambda c, b: (0, c))],
        out_shape=[jax.ShapeDtypeStruct((t, D_MODEL), BF16), jax.ShapeDtypeStruct((t, D_MODEL), BF16),
                   jax.ShapeDtypeStruct((n_ct, ct, ct), F32), jax.ShapeDtypeStruct((n_ct, ct, ct), F32),
                   jax.ShapeDtypeStruct((8, D_MODEL), F32)],
        scratch_shapes=[pltpu.VMEM((seq, ct), F32)] * 3,
        args=(dya, z, z, xc, hr, conv_w, wrg_bd, b_rg, wig_bd, b_ig, lam), name="rnn_bwd",
        semantics=("parallel", "arbitrary"), hosted=hosted)
    return (*outs, landed) if hosted is not None else outs


def _split_hi_lo(x):
    hi = x.astype(BF16)
    return hi, (x - hi.astype(F32)).astype(BF16)


def _dot_split(x, m_twice):
    hi, lo = _split_hi_lo(x)
    return jnp.dot(jnp.concatenate([hi, lo], axis=1), m_twice, preferred_element_type=F32)


def _head_matrices(width):
    ec = ((lax.broadcasted_iota(jnp.int32, (2 * width, LANES), 0) & (width - 1)) // HEAD_DIM
          == lax.broadcasted_iota(jnp.int32, (2 * width, LANES), 1))
    ee = (lax.broadcasted_iota(jnp.int32, (2 * LANES, width), 1) // HEAD_DIM
          == (lax.broadcasted_iota(jnp.int32, (2 * LANES, width), 0) & (LANES - 1)))
    return jnp.where(ec, 1.0, 0.0).astype(BF16), jnp.where(ee, 1.0, 0.0).astype(BF16)


def _swap_halves(y):
    w = y.shape[1]
    first = (lax.broadcasted_iota(jnp.int32, y.shape, 1) % HEAD_DIM) < HEAD_DIM // 2
    return jnp.where(first, pltpu.roll(y, w - HEAD_DIM // 2, 1), pltpu.roll(y, HEAD_DIM // 2, 1))


def _normrope_fwd(x, gain, cos_t, sin_t, ec, ee):
    w = x.shape[1]
    rs = _dot_split(lax.rsqrt(_dot_split(x * x, ec) * (1.0 / HEAD_DIM) + NORM_EPS), ee)
    nx = x * rs
    y = nx * gain
    reps = w // LANES
    out = y * jnp.tile(cos_t, (1, reps)) + _swap_halves(y) * jnp.tile(sin_t, (1, reps))
    return out, nx, rs


def _normrope_bwd(dout, nx, rs, gain, cos_t, sin_t, ec, ee):
    w = dout.shape[1]
    reps = w // LANES
    dy = dout * jnp.tile(cos_t, (1, reps)) + _swap_halves(dout * jnp.tile(sin_t, (1, reps)))
    dgain = jnp.sum(dy * nx, axis=0, keepdims=True)
    dn = dy * gain
    seg = _dot_split(_dot_split(dn * nx, ec) * (1.0 / HEAD_DIM), ee)
    return rs * (dn - nx * seg), dgain


def _pair_operand(t, group):
    chunk = t[:, (group // 2) * LANES:(group // 2 + 1) * LANES]
    low = lax.broadcasted_iota(jnp.int32, chunk.shape, 1) < HEAD_DIM
    rolled = pltpu.roll(chunk, HEAD_DIM, 1)
    return jnp.where(low, chunk, rolled) if group % 2 == 0 else jnp.where(low, rolled, chunk)


GROUP = N_Q_HEADS // N_KV_HEADS
GROUP_W = GROUP * HEAD_DIM


def _replicate_head(t, group):
    return jnp.tile(_pair_operand(t, group), (1, 2))


def _head_blocks(t):
    seg = lax.broadcasted_iota(jnp.int32, t.shape, 1) // HEAD_DIM
    return jnp.concatenate([jnp.where(seg == h, t, 0.0) for h in range(GROUP)], axis=0)


def _stack_heads(t_t, rows):
    return jnp.concatenate([t_t[:, h * rows:(h + 1) * rows] for h in range(GROUP)], axis=0)


def _head_rows(mat_t, group):
    return jnp.concatenate([mat_t[GROUP * group + h:GROUP * group + h + 1, :] for h in range(GROUP)], axis=1)


def _window_masks(blk):
    key = lax.broadcasted_iota(jnp.int32, (blk, GROUP * blk), 0)
    query = lax.broadcasted_iota(jnp.int32, (blk, GROUP * blk), 1) & (blk - 1)
    return key > query, key <= query


def _mask_window(t, before_ok, own_ok, fill):
    blk = t.shape[0] // 2
    return jnp.concatenate([jnp.where(before_ok, t[:blk], fill), jnp.where(own_ok, t[blk:], fill)], axis=0)


def _attn_fwd(z, cos_t, sin_t, q_gain_t, k_gain_t, sinks_t, *, n_seq, seq, hosted=None):
    t = n_seq * seq
    blk = WINDOW
    nb = seq // blk

    def body(q_ref, kp_ref, kc_ref, vp_ref, vc_ref, cosc_ref, sinc_ref, cosp_ref, sinp_ref, qg_ref, kg_ref, sk_ref,
             o_ref, l_ref):
        n = pl.program_id(1)
        ecq, eeq = _head_matrices(D_MODEL)
        eck, eek = _head_matrices(KV_W)
        cosc, sinc = cosc_ref[...], sinc_ref[...]
        qh, _, _ = _normrope_fwd(q_ref[...], qg_ref[...], cosc, sinc, ecq, eeq)
        qh = qh * (HEAD_DIM ** -0.5)
        kc, _, _ = _normrope_fwd(kc_ref[...], kg_ref[...], cosc, sinc, eck, eek)
        kp, _, _ = _normrope_fwd(kp_ref[...], kg_ref[...], cosp_ref[...], sinp_ref[...], eck, eek)
        kcat = jnp.concatenate([kp, kc], axis=0)
        vcat = jnp.concatenate([vp_ref[...], vc_ref[...]], axis=0)
        above, causal = _window_masks(blk)
        above = above & (n > 0)
        head_row = lax.broadcasted_iota(jnp.int32, (blk, blk), 0)
        sk_t = jnp.broadcast_to(sk_ref[...], (blk, LANES)).T
        vcat_t = vcat.T.astype(BF16)
        lmat = jnp.zeros((blk, blk), F32)
        groups = range(N_KV_HEADS)
        cols = [slice(g * GROUP_W, (g + 1) * GROUP_W) for g in groups]
        scores = [_dot_nt(_replicate_head(kcat, g).astype(BF16), _head_blocks(qh[:, cols[g]]).astype(BF16))
                  for g in groups]
        probs = []
        for g in groups:
            s = _mask_window(scores[g], above, causal, NEG_BIG)
            sink = _head_rows(sk_t, g)
            m = jnp.maximum(jnp.max(s, axis=0, keepdims=True), sink)
            e = jnp.exp(s - m)
            den = jnp.sum(e, axis=0, keepdims=True) + jnp.exp(sink - m)
            probs.append((e * (1.0 / den)).astype(BF16))
            lse = m + jnp.log(den)
            for h in range(GROUP):
                lmat = lmat + jnp.where(head_row == GROUP * g + h, lse[:, h * blk:(h + 1) * blk], 0.0)
        for g in groups:
            out_t = jnp.dot(vcat_t[g * HEAD_DIM:(g + 1) * HEAD_DIM], probs[g], preferred_element_type=F32)
            o_ref[:, cols[g]] = _stack_heads(out_t, blk).T.astype(BF16)
        l_ref[...] = lmat

    def row(b, n):
        return b * nb + n

    def prev(b, n):
        return b * nb + jnp.maximum(n - 1, 0)

    kw = KV_W
    tab_c = pl.BlockSpec((blk, LANES), lambda b, n: (n, 0))
    tab_p = pl.BlockSpec((blk, LANES), lambda b, n: (jnp.maximum(n - 1, 0), 0))
    outs, landed = _call(
        body,
        grid=(n_seq, nb),
        in_specs=[
            pl.BlockSpec((blk, D_MODEL), lambda b, n: (row(b, n), COL_Q // D_MODEL)),
            pl.BlockSpec((blk, kw), lambda b, n: (prev(b, n), COL_K // kw)),
            pl.BlockSpec((blk, kw), lambda b, n: (row(b, n), COL_K // kw)),
            pl.BlockSpec((blk, kw), lambda b, n: (prev(b, n), COL_V // kw)),
            pl.BlockSpec((blk, kw), lambda b, n: (row(b, n), COL_V // kw)),
            tab_c, tab_c, tab_p, tab_p,
            pl.BlockSpec((1, D_MODEL), lambda b, n: (0, 0)),
            pl.BlockSpec((1, kw), lambda b, n: (0, 0)),
            pl.BlockSpec((1, LANES), lambda b, n: (0, 0)),
        ],
        out_specs=[pl.BlockSpec((blk, D_MODEL), lambda b, n: (row(b, n), 0)),
                   pl.BlockSpec((blk, LANES), lambda b, n: (row(b, n), 0))],
        out_shape=[jax.ShapeDtypeStruct((t, D_MODEL), BF16), jax.ShapeDtypeStruct((t, LANES), F32)],
        args=(z, z, z, z, z, cos_t, sin_t, cos_t, sin_t, q_gain_t, k_gain_t, sinks_t), name="attn_fwd",
        semantics=("parallel", "parallel"), hosted=hosted)
    return (*outs, landed) if hosted is not None else outs


def _attn_bwd(z, o, lse, do, cos_t, sin_t, q_gain_t, k_gain_t, sinks_t, *, n_seq, seq, hosted=None):
    t = n_seq * seq
    blk = WINDOW
    nb = seq // blk
    kw = KV_W
    scale = HEAD_DIM ** -0.5

    def body(qc_ref, qn_ref, kp_ref, kc_ref, vp_ref, vc_ref, oc_ref, on_ref, doc_ref, don_ref, lc_ref, ln_ref,
             cosc_ref, sinc_ref, cosp_ref, sinp_ref, cosn_ref, sinn_ref, qg_ref, kg_ref, sk_ref,
             dq_ref, dk_ref, dv_ref, vec_ref):
        n = pl.program_id(1)
        ecq, eeq = _head_matrices(D_MODEL)
        eck, eek = _head_matrices(KV_W)
        cosc, sinc = cosc_ref[...], sinc_ref[...]
        qg, kg = qg_ref[...], kg_ref[...]
        qhc, nqc, rsqc = _normrope_fwd(qc_ref[...], qg, cosc, sinc, ecq, eeq)
        qhn, _, _ = _normrope_fwd(qn_ref[...], qg, cosn_ref[...], sinn_ref[...], ecq, eeq)
        khc, nkc, rskc = _normrope_fwd(kc_ref[...], kg, cosc, sinc, eck, eek)
        khp, _, _ = _normrope_fwd(kp_ref[...], kg, cosp_ref[...], sinp_ref[...], eck, eek)
        doc = doc_ref[...].astype(F32)
        don = don_ref[...].astype(F32)
        delc = _dot_split(doc * oc_ref[...].astype(F32), ecq)
        deln = _dot_split(don * on_ref[...].astype(F32), ecq)
        lc_t, ln_t, delc_t, deln_t = lc_ref[...], ln_ref[...], delc.T, deln.T
        above, causal = _window_masks(blk)
        above_c, above_n = above & (n > 0), above & (n < nb - 1)
        seg = lax.broadcasted_iota(jnp.int32, (blk, GROUP_W), 1) // HEAD_DIM
        lane = lax.broadcasted_iota(jnp.int32, (1, LANES), 1)
        sk_t = jnp.broadcast_to(sk_ref[...], (blk, LANES)).T
        dsink = jnp.zeros((1, LANES), F32)
        kcat = jnp.concatenate([khp, khc], axis=0)
        vcat = jnp.concatenate([vp_ref[...], vc_ref[...]], axis=0)
        kcat_t = kcat.T.astype(BF16)
        dkh = jnp.zeros((blk, GROUP_W), F32)
        dvh = jnp.zeros((blk, GROUP_W), F32)

        def fold_to(group, t):
            total = t + pltpu.roll(t, HEAD_DIM, 1)
            total = total + pltpu.roll(total, 2 * HEAD_DIM, 1)
            return jnp.where(seg == group, total, 0.0)

        groups = range(N_KV_HEADS)
        cols = [slice(g * GROUP_W, (g + 1) * GROUP_W) for g in groups]
        qsc, qsn = qhc * scale, qhn * scale
        qb_c = [_head_blocks(qsc[:, cols[g]]).astype(BF16) for g in groups]
        qb_n = [_head_blocks(qsn[:, cols[g]]).astype(BF16) for g in groups]
        dob_c = [_head_blocks(doc[:, cols[g]]).astype(BF16) for g in groups]
        dob_n = [_head_blocks(don[:, cols[g]]).astype(BF16) for g in groups]
        raw = []
        for g in groups:
            krep = _replicate_head(kcat, g).astype(BF16)
            vrep = _replicate_head(vcat, g).astype(BF16)
            raw.append((_dot_nt(krep, qb_c[g]), _dot_nt(vrep, dob_c[g]),
                        _dot_nt(krep[blk:], qb_n[g]), _dot_nt(vrep[blk:], dob_n[g])))
        cooked = []
        for g in groups:
            s_c, dp_c, s_n, dp_n = raw[g]
            l_row, d_row = _head_rows(lc_t, g), _head_rows(delc_t, g)
            p_c = _mask_window(jnp.exp(s_c - l_row), above_c, causal, 0.0)
            ds_c = (p_c * (dp_c - d_row)).astype(BF16)
            p_n = jnp.where(above_n, jnp.exp(s_n - _head_rows(ln_t, g)), 0.0)
            ds_n = (p_n * (dp_n - _head_rows(deln_t, g))).astype(BF16)
            cooked.append((p_c[blk:].astype(BF16), ds_c, p_n.astype(BF16), ds_n))
            p_sink = jnp.exp(_head_rows(sk_t, g) - l_row) * d_row
            for h in range(GROUP):
                dsink = dsink + jnp.where(lane == GROUP * g + h,
                                          -jnp.sum(p_sink[:, h * blk:(h + 1) * blk], axis=1, keepdims=True), 0.0)
        for g in groups:
            p_cb, ds_c, p_nb, ds_n = cooked[g]
            dq_t = jnp.dot(kcat_t[g * HEAD_DIM:(g + 1) * HEAD_DIM], ds_c, preferred_element_type=F32)
            dq_ref[:, cols[g]] = _stack_heads(dq_t, blk).T * scale
            dk_rep = (jnp.dot(ds_c[blk:], qb_c[g], preferred_element_type=F32)
                      + jnp.dot(ds_n, qb_n[g], preferred_element_type=F32))
            dv_rep = (jnp.dot(p_cb, dob_c[g], preferred_element_type=F32)
                      + jnp.dot(p_nb, dob_n[g], preferred_element_type=F32))
            dkh = dkh + fold_to(g, dk_rep)
            dvh = dvh + fold_to(g, dv_rep)
        dv_ref[...] = dvh.astype(BF16)
        dq, dqg = _normrope_bwd(dq_ref[...], nqc, rsqc, qg, cosc, sinc, ecq, eeq)
        dk, dkg = _normrope_bwd(dkh, nkc, rskc, kg, cosc, sinc, eck, eek)
        dq_ref[...] = dq
        dk_ref[...] = dk.astype(BF16)

        @pl.when(n == 0)
        def _():
            vec_ref[...] = jnp.zeros_like(vec_ref)

        vec_ref[0:1, :] += dqg
        vec_ref[1:2, 0:kw] += dkg
        vec_ref[2:3, 0:LANES] += dsink

    def row(b, n):
        return b * nb + n

    def prev(b, n):
        return b * nb + jnp.maximum(n - 1, 0)

    def nxt(b, n):
        return b * nb + jnp.minimum(n + 1, nb - 1)

    def tiles(width, col, which):
        return pl.BlockSpec((blk, width), lambda b, n: (which(b, n), col))

    def table(which):
        return pl.BlockSpec((blk, LANES), lambda b, n: (which(0, n), 0))

    outs, landed = _call(
        body,
        grid=(n_seq, nb),
        in_specs=[
            tiles(D_MODEL, COL_Q // D_MODEL, row), tiles(D_MODEL, COL_Q // D_MODEL, nxt),
            tiles(kw, COL_K // kw, prev), tiles(kw, COL_K // kw, row),
            tiles(kw, COL_V // kw, prev), tiles(kw, COL_V // kw, row),
            tiles(D_MODEL, 0, row), tiles(D_MODEL, 0, nxt),
            tiles(D_MODEL, 0, row), tiles(D_MODEL, 0, nxt),
            tiles(LANES, 0, row), tiles(LANES, 0, nxt),
            table(row), table(row), table(prev), table(prev), table(nxt), table(nxt),
            pl.BlockSpec((1, D_MODEL), lambda b, n: (0, 0)),
            pl.BlockSpec((1, kw), lambda b, n: (0, 0)),
            pl.BlockSpec((1, LANES), lambda b, n: (0, 0)),
        ],
        out_specs=[tiles(D_MODEL, 0, row), tiles(kw, 0, row), tiles(kw, 0, row),
                   pl.BlockSpec((None, 8, D_MODEL), lambda b, n: (b, 0, 0))],
        out_shape=[jax.ShapeDtypeStruct((t, D_MODEL), F32), jax.ShapeDtypeStruct((t, kw), BF16),
                   jax.ShapeDtypeStruct((t, kw), BF16), jax.ShapeDtypeStruct((n_seq, 8, D_MODEL), F32)],
        args=(z, z, z, z, z, z, o, o, do, do, lse, lse, cos_t, sin_t, cos_t, sin_t, cos_t, sin_t,
              q_gain_t, k_gain_t, sinks_t), name="attn_bwd", semantics=("parallel", "arbitrary"), hosted=hosted)
    return (*outs, landed) if hosted is not None else outs


MERGE_COLS = 512


def _merge_fwd(z, ya, yb):
    t = ya.shape[0]
    tm, tc = min(512, t), MERGE_COLS

    def body(ga_ref, gb_ref, ya_ref, yb_ref, o_ref):
        o_ref[...] = (_sig(ga_ref[...]) * ya_ref[...] + _sig(gb_ref[...]) * yb_ref[...]).astype(BF16)

    tile = pl.BlockSpec((tm, tc), lambda i, j: (i, j))
    return pl.pallas_call(
        body,
        grid=(t // tm, D_MODEL // tc),
        in_specs=[pl.BlockSpec((tm, tc), lambda i, j: (i, COL_GA // tc + j)),
                  pl.BlockSpec((tm, tc), lambda i, j: (i, COL_GB // tc + j)), tile, tile],
        out_specs=tile,
        out_shape=jax.ShapeDtypeStruct((t, D_MODEL), BF16),
        compiler_params=_params("parallel", "parallel"),
        name="merge_fwd",
    )(z, z, ya, yb)


def _merge_bwd(z, ya, yb, dmerged):
    t = ya.shape[0]
    tm, tc = min(512, t), MERGE_COLS

    def body(ga_ref, gb_ref, ya_ref, yb_ref, dm_ref, dya_ref, dyb_ref, dga_ref, dgb_ref):
        dm = dm_ref[...]
        sa, sb = _sig(ga_ref[...]), _sig(gb_ref[...])
        dya_ref[...] = (dm * sa).astype(BF16)
        dyb_ref[...] = (dm * sb).astype(BF16)
        dga_ref[...] = (dm * ya_ref[...] * sa * (1.0 - sa)).astype(BF16)
        dgb_ref[...] = (dm * yb_ref[...] * sb * (1.0 - sb)).astype(BF16)

    tile = pl.BlockSpec((tm, tc), lambda i, j: (i, j))
    return pl.pallas_call(
        body,
        grid=(t // tm, D_MODEL // tc),
        in_specs=[pl.BlockSpec((tm, tc), lambda i, j: (i, COL_GA // tc + j)),
                  pl.BlockSpec((tm, tc), lambda i, j: (i, COL_GB // tc + j)), tile, tile, tile],
        out_specs=[tile] * 4,
        out_shape=[jax.ShapeDtypeStruct((t, D_MODEL), BF16)] * 4,
        compiler_params=_params("parallel", "parallel"),
        name="merge_bwd",
    )(z, z, ya, yb, dmerged)


def _loss_head(x2, e, gt, target):
    t, d = x2.shape
    tm = min(256, t)

    def body(x_ref, e_ref, gt_ref, tg_ref, loss_ref, dx_ref, dgt_ref, de_ref):
        ev = e_ref[...]
        sg = _sig(gt_ref[...])
        diff = x_ref[...] + ev * sg - tg_ref[...]
        dx = diff * (1.0 / d)
        dx_ref[...] = dx
        dgt_ref[...] = (dx * ev * sg * (1.0 - sg)).astype(BF16)
        de_ref[...] = (dx * sg).astype(BF16)

        @pl.when(pl.program_id(0) == 0)
        def _():
            loss_ref[...] = jnp.zeros_like(loss_ref)

        loss_ref[...] += jnp.sum(jnp.sum(diff * diff, axis=1, keepdims=True), axis=0, keepdims=True)

    tile = pl.BlockSpec((tm, d), lambda i: (i, 0))
    return pl.pallas_call(
        body,
        grid=(t // tm,),
        in_specs=[tile] * 4,
        out_specs=[pl.BlockSpec((1, LANES), lambda i: (0, 0)), tile, tile, tile],
        out_shape=[jax.ShapeDtypeStruct((1, LANES), F32), jax.ShapeDtypeStruct((t, d), F32),
                   jax.ShapeDtypeStruct((t, d), BF16), jax.ShapeDtypeStruct((t, d), BF16)],
        compiler_params=_params("arbitrary"),
        name="loss_head",
    )(x2, e, gt, target)


def _rope_tables(seq):
    inv = ROPE_THETA ** (-jnp.arange(0, HEAD_DIM, 2, dtype=F32) / HEAD_DIM)
    ang = jnp.arange(seq, dtype=F32)[:, None] * inv[None, :]
    cos, sin = jnp.cos(ang), jnp.sin(ang)
    return jnp.tile(jnp.concatenate([cos, cos], axis=1), (1, 2)), jnp.tile(jnp.concatenate([-sin, sin], axis=1), (1, 2))


def _block_diag_tiles(w):
    per = RNN_TILE // RNN_BLOCK_W
    w4 = w.reshape(D_MODEL // RNN_TILE, per, RNN_BLOCK_W, RNN_BLOCK_W)
    eye = jnp.eye(per, dtype=w.dtype)
    dense = jnp.einsum("tpij,pq->tpiqj", w4, eye)
    return dense.reshape(D_MODEL // RNN_TILE, RNN_TILE, RNN_TILE).astype(BF16)


def _block_diag_extract(dense):
    per = RNN_TILE // RNN_BLOCK_W
    d5 = dense.reshape(D_MODEL // RNN_TILE, per, RNN_BLOCK_W, per, RNN_BLOCK_W)
    blocks = jnp.stack([d5[:, p, :, p, :] for p in range(per)], axis=1)
    return blocks.reshape(D_MODEL // RNN_BLOCK_W, RNN_BLOCK_W, RNN_BLOCK_W)


def _local_step(x, p, target, w, *, n_seq, seq, comm=None):
    w = dict(w)

    def run(tag, fn, *args, **kwargs):
        hosted = comm.host(tag) if comm is not None else None
        if hosted is None:
            return fn(*args, **kwargs)
        *outs, landed = fn(*args, hosted=hosted, **kwargs)
        comm.landed(tag, landed, w)
        return outs[0] if len(outs) == 1 else outs

    def ready(batch, grads):
        if comm is not None:
            comm.ready(batch, grads)

    cos_t, sin_t = _rope_tables(seq)
    q_gain_t = jnp.tile(w["q_gain"], (1, N_Q_HEADS))
    k_gain_t = jnp.tile(w["k_gain"], (1, N_KV_HEADS))
    sinks_t = jnp.pad(w["sinks"], ((0, 0), (0, LANES - N_Q_HEADS)))
    wrg_bd, wig_bd = _block_diag_tiles(w["w_rg"]), _block_diag_tiles(w["w_ig"])
    dims = dict(n_seq=n_seq, seq=seq)

    h = _rmsnorm_fwd(x, w["g_mix"], name="norm_mix")
    z = run("mm_in", _matmul, h, w["w_in"], mode="nn", tm=1024, tn=512, out_dtypes=[F32], name="mm_in")
    xc, hr, ya_in = run("rnn_fwd", _rnn_fwd, z, w["conv_w"], w["conv_b"], wrg_bd, w["b_rg"], wig_bd, w["b_ig"],
                        w["lru_lambda"], **dims)
    o, lse = run("attn_fwd", _attn_fwd, z, cos_t, sin_t, q_gain_t, k_gain_t, sinks_t, **dims)
    ya = _matmul(ya_in, w["w_rnn_proj"], mode="nn", tm=1024, tn=512, out_dtypes=[F32], name="mm_rnn_proj")
    yb = _matmul(o, w["w_attn_proj"], mode="nn", tm=1024, tn=512, out_dtypes=[F32], name="mm_attn_proj")
    merged = _merge_fwd(z, ya, yb)
    x1 = _matmul(merged, w["w_out"], mode="nn", tm=1024, tn=512, out_dtypes=[F32], name="mm_out",
                 epilogue=lambda acc, res: (res + acc,), extras=(x,))
    hm = _rmsnorm_fwd(x1, w["g_mlp"], name="norm_mlp")
    act = _matmul(hm, w["w_up"], mode="nn", tm=1024, tn=512, out_dtypes=[BF16], name="mm_up",
                  epilogue=lambda acc: (jnp.square(jnp.maximum(acc, 0.0)),))
    x2 = _matmul(act, w["w_down"], mode="nn", tm=512, tn=512, out_dtypes=[F32], name="mm_down",
                 epilogue=lambda acc, res: (res + acc,), extras=(x1,))
    hp = _rmsnorm_fwd(x2, w["g_ple"], name="norm_ple")
    gt = _matmul(hp, w["w_ple_gate"], mode="nn", tm=1024, tn=512, out_dtypes=[F32], name="mm_ple_gate")
    p_bf = p.astype(BF16)
    e = _matmul(p_bf, w["w_ple_proj"], mode="nn", tm=1024, tn=512, out_dtypes=[F32], name="mm_ple_proj")
    loss_row, dx3, dgt, de = _loss_head(x2, e, gt, target)

    g = {}
    g["w_ple_proj"] = _matmul_tn(p_bf, de, tk=PLE_DIM, tn=1024, tt=1024, name="mm_d_ple_proj",
                                 slot_cols=D_MODEL // N_DEV)
    g["w_ple_gate"] = _matmul_tn(hp, dgt, tk=1024, tn=1024, tt=512, name="mm_d_ple_gate")
    dhp = _matmul(dgt, w["w_ple_gate"], mode="nt", tm=1024, tn=512, out_dtypes=[F32], name="mm_dhp")
    dx2, dx2_bf, g["g_ple"] = _rmsnorm_bwd(x2, w["g_ple"], dhp, dx3, name="norm_ple_bwd", want_bf16=True)
    g["w_down"] = _matmul_tn(act, dx2_bf, tk=1024, tn=1024, tt=512, name="mm_d_down")
    du = _matmul(dx2_bf, w["w_down"], mode="nt", tm=1024, tn=512, out_dtypes=[BF16], name="mm_dact",
                 epilogue=lambda acc, a: (acc * (2.0 * jnp.sqrt(a.astype(F32))),), extras=(act,))
    g["w_up"] = _matmul_tn(hm, du, tk=1024, tn=1024, tt=512, name="mm_d_up", slot_cols=D_FF // N_DEV)
    ready(1, g)
    dhm = run("mm_dhm", _matmul, du, w["w_up"], mode="nt", tm=512, tn=512, out_dtypes=[F32], name="mm_dhm")
    dx1, dx1_bf, g["g_mlp"] = _rmsnorm_bwd(x1, w["g_mlp"], dhm, dx2, name="norm_mlp_bwd", want_bf16=True)
    g["w_out"] = _matmul_tn(merged, dx1_bf, tk=1024, tn=1024, tt=512, name="mm_d_out")
    dmerged = _matmul(dx1_bf, w["w_out"], mode="nt", tm=1024, tn=512, out_dtypes=[F32], name="mm_dmerged")
    dya, dyb, dga, dgb = _merge_bwd(z, ya, yb, dmerged)
    g["w_rnn_proj"] = _matmul_tn(ya_in, dya, tk=1024, tn=1024, tt=512, name="mm_d_rnn_proj")
    g["w_attn_proj"] = _matmul_tn(o, dyb, tk=1024, tn=1024, tt=512, name="mm_d_attn_proj")
    ready(2, g)
    dya_in = run("mm_dya_in", _matmul, dya, w["w_rnn_proj"], mode="nt", tm=1024, tn=512, out_dtypes=[F32],
                 name="mm_dya_in")
    do = _matmul(dyb, w["w_attn_proj"], mode="nt", tm=1024, tn=512, out_dtypes=[BF16], name="mm_do")
    dx_rnn, dg_rnn, dwrg_dense, dwig_dense, rnn_vec = run(
        "rnn_bwd", _rnn_bwd, dya_in, z, xc, hr, w["conv_w"], wrg_bd, w["b_rg"], wig_bd, w["b_ig"], w["lru_lambda"],
        **dims)
    dq, dk, dv, attn_vec = run("attn_bwd", _attn_bwd, z, o, lse, do, cos_t, sin_t, q_gain_t, k_gain_t, sinks_t,
                               **dims)
    dz = jnp.concatenate([dx_rnn, dg_rnn, dq.astype(BF16), dk, dv, dga, dgb], axis=1)
    g["w_in"] = _matmul_tn(h, dz, tk=1024, tn=IN_TOTAL // 4, tt=512, name="mm_d_in")
    g["w_rg"] = _block_diag_extract(dwrg_dense)
    g["w_ig"] = _block_diag_extract(dwig_dense)
    g["b_rg"], g["b_ig"], g["lru_lambda"], g["conv_b"] = (rnn_vec[i:i + 1] for i in range(4))
    g["conv_w"] = rnn_vec[4:8]
    attn_vec = attn_vec[0] if n_seq == 1 else functools.reduce(jnp.add, [attn_vec[b] for b in range(n_seq)])
    g["q_gain"] = attn_vec[0].reshape(N_Q_HEADS, HEAD_DIM).sum(axis=0)[None, :]
    g["k_gain"] = attn_vec[1, :KV_W].reshape(N_KV_HEADS, HEAD_DIM).sum(axis=0)[None, :]
    g["sinks"] = attn_vec[2:3, :N_Q_HEADS]
    ready(3, g)
    dh = run("mm_dh", _matmul, dz, w["w_in"], mode="nt", tm=512, tn=512, out_dtypes=[F32], name="mm_dh")
    grad_x, g["g_mix"] = run("norm_mix_bwd", _rmsnorm_bwd, x, w["g_mix"], dh, dx1, name="norm_mix_bwd",
                             want_bf16=False)
    return loss_row[0, 0], grad_x, g


MESH_ID = pl.DeviceIdType.MESH


def _coords(index):
    return (index >> 2) & 1, (index >> 1) & 1, index & 1


def _exchange(srcs, kinds, *, name):
    n = len(srcs)
    n_peer = N_DEV - 1

    def body(*refs):
        src, dst = refs[:n], refs[n:2 * n]
        send_sems, recv_sems, local_sems = refs[2 * n:]
        me = 4 * lax.axis_index("x") + 2 * lax.axis_index("y") + lax.axis_index("c")

        def remote(i, d):
            peer = (me + d) & (N_DEV - 1)
            piece = src[i] if kinds[i] == "gather" else src[i].at[peer]
            return pltpu.make_async_remote_copy(
                src_ref=piece, dst_ref=dst[i].at[me], send_sem=send_sems.at[i * n_peer + d - 1],
                recv_sem=recv_sems.at[i * n_peer + d - 1], device_id=_coords(peer), device_id_type=MESH_ID)

        def arrival(i, d):
            sender = (me - d) & (N_DEV - 1)
            piece = src[i] if kinds[i] == "gather" else src[i].at[sender]
            return pltpu.make_async_remote_copy(
                src_ref=piece, dst_ref=dst[i].at[sender], send_sem=send_sems.at[i * n_peer + d - 1],
                recv_sem=recv_sems.at[i * n_peer + d - 1], device_id=_coords(sender), device_id_type=MESH_ID)

        own = []
        for i in range(n):
            piece = src[i] if kinds[i] == "gather" else src[i].at[me]
            own.append(pltpu.make_async_copy(piece, dst[i].at[me], local_sems.at[i]))
            own[-1].start()
        sent = [remote(i, d) for d in range(1, N_DEV) for i in range(n)]
        for cp in sent:
            cp.start()
        for d in range(1, N_DEV):
            for i in range(n):
                arrival(i, d).wait_recv()
        for cp in sent:
            cp.wait_send()
        for cp in own:
            cp.wait()

    def out_of(s, kind):
        shape = s.shape if kind == "scatter" else (N_DEV,) + s.shape
        return jax.ShapeDtypeStruct(shape, s.dtype)

    any_spec = pl.BlockSpec(memory_space=pl.ANY)
    return pl.pallas_call(
        body,
        in_specs=[any_spec] * n,
        out_specs=[any_spec] * n,
        out_shape=[out_of(s, k) for s, k in zip(srcs, kinds)],
        scratch_shapes=[pltpu.SemaphoreType.DMA((n * n_peer,)), pltpu.SemaphoreType.DMA((n * n_peer,)),
                        pltpu.SemaphoreType.DMA((n,))],
        compiler_params=pltpu.CompilerParams(has_side_effects=True),
        name=name,
    )(*srcs)


def _remote(src, dst, send_sem, recv_sem, to):
    return pltpu.make_async_remote_copy(src_ref=src, dst_ref=dst, send_sem=send_sem, recv_sem=recv_sem,
                                        device_id=to, device_id_type=MESH_ID)


def _gather_two_level(shards, *, name):
    n = len(shards)
    per = N_DEV - 1

    def body(*refs):
        src, dst = refs[:n], refs[n:2 * n]
        send_sems, recv_sems, local_sems = refs[2 * n:]
        x, y, c = lax.axis_index("x"), lax.axis_index("y"), lax.axis_index("c")
        me, sibling = (x, y, c), (x, y, 1 - c)
        chips = [(1 - x, y), (x, 1 - y), (1 - x, 1 - y)]

        def slot(pos):
            return 4 * pos[0] + 2 * pos[1] + pos[2]

        def copy(i, k, block, to, from_shard=False):
            source = src[i] if from_shard else dst[i].at[slot(block)]
            return _remote(source, dst[i].at[slot(block)], send_sems.at[i * per + k], recv_sems.at[i * per + k], to)

        mine = [pltpu.make_async_copy(src[i], dst[i].at[slot(me)], local_sems.at[i]) for i in range(n)]
        for cp in mine:
            cp.start()
        first = []
        for i in range(n):
            first.append(copy(i, 0, me, sibling, from_shard=True))
            first += [copy(i, 1 + j, me, (*chip, c), from_shard=True) for j, chip in enumerate(chips)]
        for cp in first:
            cp.start()
        passed = []
        for i in range(n):
            for j, chip in enumerate(chips):
                copy(i, 1 + j, (*chip, c), me).wait_recv()
                passed.append(copy(i, 4 + j, (*chip, c), sibling))
                passed[-1].start()
        for i in range(n):
            copy(i, 0, sibling, me).wait_recv()
            for j, chip in enumerate(chips):
                copy(i, 4 + j, (*chip, 1 - c), me).wait_recv()
        for cp in first + passed:
            cp.wait_send()
        for cp in mine:
            cp.wait()

    any_spec = pl.BlockSpec(memory_space=pl.ANY)
    return pl.pallas_call(
        body,
        in_specs=[any_spec] * n,
        out_specs=[any_spec] * n,
        out_shape=[jax.ShapeDtypeStruct((N_DEV,) + s.shape, s.dtype) for s in shards],
        scratch_shapes=[pltpu.SemaphoreType.DMA((n * per,)), pltpu.SemaphoreType.DMA((n * per,)),
                        pltpu.SemaphoreType.DMA((n,))],
        name=name,
    )(*shards)


def _hosted_gather(shards):
    n = len(shards)
    per = N_DEV - 1

    def plan(src, dst, send_sems, recv_sems, local_sems):
        me = 4 * lax.axis_index("x") + 2 * lax.axis_index("y") + lax.axis_index("c")
        copies = []
        for i in range(n):
            own = pltpu.make_async_copy(src[i], dst[i].at[me], local_sems.at[i])
            copies.append(_Xfer(own.start, own.wait))
        for d in range(1, N_DEV):
            peer, sender = (me + d) & (N_DEV - 1), (me - d) & (N_DEV - 1)
            for i in range(n):
                k = i * per + d - 1
                out = _remote(src[i], dst[i].at[me], send_sems.at[k], recv_sems.at[k], _coords(peer))
                arrival = _remote(src[i], dst[i].at[sender], send_sems.at[k], recv_sems.at[k], _coords(sender))

                def wait(out=out, arrival=arrival):
                    arrival.wait_recv()
                    out.wait_send()

                copies.append(_Xfer(out.start, wait))
        return copies

    out_shape = tuple(jax.ShapeDtypeStruct((N_DEV,) + s.shape, s.dtype) for s in shards)
    return _Hosted(tuple(shards), out_shape, n * per, plan)


CHIPS = N_DEV // 2


def _hosted_sibling_swap(arrays, sliced):
    n_sems = sum(CHIPS if s else 1 for s in sliced)

    def plan(src, dst, send_sems, recv_sems, local_sems):
        x, y, c = lax.axis_index("x"), lax.axis_index("y"), lax.axis_index("c")
        sibling = (x, y, 1 - c)
        copies, k = [], 0
        for i, is_sliced in enumerate(sliced):
            pieces = [(src[i].at[2 * s + 1 - c], dst[i].at[s]) for s in range(CHIPS)] if is_sliced else [(src[i], dst[i])]
            for source, target in pieces:
                cp = _remote(source, target, send_sems.at[k], recv_sems.at[k], sibling)
                copies.append(_Xfer(cp.start, cp.wait))
                k += 1
        return copies

    out_shape = tuple(jax.ShapeDtypeStruct((CHIPS,) + a.shape[1:] if s else a.shape, a.dtype)
                      for a, s in zip(arrays, sliced))
    return _Hosted(tuple(arrays), out_shape, n_sems, plan)


def _hosted_chip_exchange(arrays, sliced):
    n = len(arrays)
    per = CHIPS - 1

    def plan(src, dst, send_sems, recv_sems, local_sems):
        x, y, c = lax.axis_index("x"), lax.axis_index("y"), lax.axis_index("c")
        chip = 2 * x + y
        copies = []
        for i in range(n):
            own = pltpu.make_async_copy(src[i].at[chip] if sliced[i] else src[i], dst[i].at[chip], local_sems.at[i])
            copies.append(_Xfer(own.start, own.wait))
        for d in range(1, CHIPS):
            other = chip ^ d
            to = ((other >> 1) & 1, other & 1, c)
            for i in range(n):
                k = i * per + d - 1
                source = src[i].at[other] if sliced[i] else src[i]
                out = _remote(source, dst[i].at[chip], send_sems.at[k], recv_sems.at[k], to)
                arrival = _remote(source, dst[i].at[other], send_sems.at[k], recv_sems.at[k], to)

                def wait(out=out, arrival=arrival):
                    arrival.wait_recv()
                    out.wait_send()

                copies.append(_Xfer(out.start, wait))
        return copies

    out_shape = tuple(jax.ShapeDtypeStruct(a.shape if s else (CHIPS,) + a.shape, a.dtype)
                      for a, s in zip(arrays, sliced))
    return _Hosted(tuple(arrays), out_shape, n * per, plan)


def _add_sibling(parts, received, core, *, name):
    _, r, cols = parts.shape
    tr = min(256, r)

    def body(core_ref, a_ref, b_ref, o_ref):
        o_ref[...] = (a_ref[...] + b_ref[...]).astype(BF16)

    grid_spec = pltpu.PrefetchScalarGridSpec(
        num_scalar_prefetch=1,
        grid=(CHIPS, r // tr),
        in_specs=[pl.BlockSpec((None, tr, cols), lambda k, i, core_ref: (2 * k + core_ref[0], i, 0)),
                  pl.BlockSpec((None, tr, cols), lambda k, i, core_ref: (k, i, 0))],
        out_specs=pl.BlockSpec((None, tr, cols), lambda k, i, core_ref: (k, i, 0)),
    )
    return pl.pallas_call(body, grid_spec=grid_spec, out_shape=jax.ShapeDtypeStruct((CHIPS, r, cols), BF16),
                          compiler_params=_params("parallel", "parallel"), name=name)(core, parts, received)


def _add_whole(a, b, *, name):
    def body(a_ref, b_ref, o_ref):
        o_ref[...] = a_ref[...] + b_ref[...]

    return pl.pallas_call(body, out_shape=jax.ShapeDtypeStruct(a.shape, F32), name=name)(a, b)


def _adamw(parts, w, m, v, *, name):
    r, c = w.shape
    n_parts = parts.shape[0]
    tr = min(256, r)
    c1 = 1.0 - ADAM_B1 ** ADAM_STEP
    c2 = 1.0 - ADAM_B2 ** ADAM_STEP

    def body(p_ref, w_ref, m_ref, v_ref, g_ref, d_ref, nm_ref, nv_ref):
        g = p_ref[0].astype(F32)
        for s in range(1, n_parts):
            g = g + p_ref[s].astype(F32)
        nm = ADAM_B1 * m_ref[...] + (1.0 - ADAM_B1) * g
        nv = ADAM_B2 * v_ref[...] + (1.0 - ADAM_B2) * (g * g)
        g_ref[...] = g
        nm_ref[...] = nm
        nv_ref[...] = nv
        d_ref[...] = -ADAM_LR * ((nm / c1) / (jnp.sqrt(nv / c2) + ADAM_EPS) + ADAM_WD * w_ref[...])

    tile = pl.BlockSpec((tr, c), lambda i: (i, 0))
    return pl.pallas_call(
        body,
        grid=(r // tr,),
        in_specs=[pl.BlockSpec((n_parts, tr, c), lambda i: (0, i, 0)), tile, tile, tile],
        out_specs=[tile] * 4,
        out_shape=[jax.ShapeDtypeStruct((r, c), F32)] * 4,
        compiler_params=_params("parallel"),
        name=name,
    )(parts, w, m, v)


BIG = ("w_in", "w_rnn_proj", "w_attn_proj", "w_out", "w_up", "w_down", "w_ple_gate", "w_ple_proj")
SMALL = (("conv_b", 1), ("b_rg", 1), ("b_ig", 1), ("lru_lambda", 1), ("g_mlp", 1), ("g_ple", 1),
         ("conv_w", 4), ("q_gain", 1), ("k_gain", 1), ("sinks", 1), ("w_rg", 64), ("w_ig", 64))
SMALL_ROWS = 144
ROW_SHARDED = ("w_rnn_proj", "w_attn_proj", "w_out", "w_down", "w_ple_gate")
COL_SHARDED = ("w_in", "w_up", "w_ple_proj")
BATCHES = {1: ("w_ple_proj", "w_ple_gate", "w_down", "w_up"), 2: ("w_out", "w_rnn_proj", "w_attn_proj"), 3: ("w_in",)}


def _pack_small(vals):
    rows = []
    for nm, nrow in SMALL:
        flat = vals[nm].reshape(-1).astype(F32)
        rows.append(jnp.pad(flat, (0, nrow * D_MODEL - flat.shape[0])).reshape(nrow, D_MODEL))
    used = sum(nrow for _, nrow in SMALL)
    rows.append(jnp.zeros((SMALL_ROWS - used, D_MODEL), F32))
    return jnp.concatenate(rows, axis=0)


def _unpack_small(packed, shapes):
    out, at = {}, 0
    for nm, nrow in SMALL:
        size = 1
        for s in shapes[nm]:
            size *= s
        out[nm] = packed[at:at + nrow].reshape(-1)[:size].reshape(shapes[nm])
        at += nrow
    return out


def _full_weight(name, landed):
    if name in COL_SHARDED:
        return landed.transpose(1, 0, 2).reshape(landed.shape[1], N_DEV * landed.shape[2])
    return landed.reshape(N_DEV * landed.shape[1], landed.shape[2])


def _owner_slots(name, grad):
    if name == "w_in":
        return grad.reshape(D_MODEL, N_DEV, IN_TOTAL // N_DEV).transpose(1, 0, 2)
    if name in COL_SHARDED:
        return grad
    return grad.reshape(N_DEV, grad.shape[0] // N_DEV, grad.shape[1])


class _StepExchanges:
    GATHERS = {"mm_in": ("w_rnn_proj", "w_attn_proj", "w_out"), "rnn_fwd": ("w_up",),
               "attn_fwd": ("w_down", "w_ple_gate", "w_ple_proj")}
    SWAPS = {"mm_dhm": 1, "mm_dya_in": 2}
    CHIP_EXCHANGES = {"rnn_bwd": 1, "attn_bwd": 2, "mm_dh": 3}

    def __init__(self, shards, core):
        self.shards = shards
        self.core = core
        self.parts, self.swapped, self.summed = {}, {}, {}

    def ready(self, batch, grads):
        arrays = [_owner_slots(nm, grads[nm]) for nm in BATCHES[batch]]
        sliced = [True] * len(arrays)
        if batch == 3:
            arrays.append(_pack_small(grads))
            sliced.append(False)
        self.parts[batch] = (arrays, sliced)
        if batch not in self.SWAPS.values():
            _, self.swapped[batch] = _call(
                lambda: None, grid=(1,), in_specs=[], out_specs=[], out_shape=[], args=(), name="swap_last",
                semantics=("arbitrary",), hosted=_hosted_sibling_swap(arrays, sliced))

    def host(self, tag):
        if tag in self.GATHERS:
            return _hosted_gather([self.shards[nm] for nm in self.GATHERS[tag]])
        if tag in self.SWAPS:
            return _hosted_sibling_swap(*self.parts[self.SWAPS[tag]])
        if tag in self.CHIP_EXCHANGES:
            batch = self.CHIP_EXCHANGES[tag]
            arrays, sliced = self.parts[batch]
            labels = list(BATCHES[batch]) + ["small"]
            sums = [_add_sibling(a, r, self.core, name="add_" + lb) if s else _add_whole(a, r, name="add_" + lb)
                    for a, r, s, lb in zip(arrays, self.swapped[batch], sliced, labels)]
            return _hosted_chip_exchange(sums, sliced)
        return None

    def landed(self, tag, landed, weights):
        if tag in self.GATHERS:
            for nm, buf in zip(self.GATHERS[tag], landed):
                weights[nm] = _full_weight(nm, buf)
        elif tag in self.SWAPS:
            self.swapped[self.SWAPS[tag]] = landed
        else:
            self.summed[self.CHIP_EXCHANGES[tag]] = landed


def kernel(x, p, g_mix, w_in, conv_w, conv_b, w_rg, b_rg, w_ig, b_ig, lru_lambda, w_rnn_proj, q_gain, k_gain, sinks, w_attn_proj, w_out, g_mlp, w_up, w_down, g_ple, w_ple_gate, w_ple_proj, loss_target, m_g_mix, m_w_in, m_conv_w, m_conv_b, m_w_rg, m_b_rg, m_w_ig, m_b_ig, m_lru_lambda, m_w_rnn_proj, m_q_gain, m_k_gain, m_sinks, m_w_attn_proj, m_w_out, m_g_mlp, m_w_up, m_w_down, m_g_ple, m_w_ple_gate, m_w_ple_proj, v_g_mix, v_w_in, v_conv_w, v_conv_b, v_w_rg, v_b_rg, v_w_ig, v_b_ig, v_lru_lambda, v_w_rnn_proj, v_q_gain, v_k_gain, v_sinks, v_w_attn_proj, v_w_out, v_g_mlp, v_w_up, v_w_down, v_g_ple, v_w_ple_gate, v_w_ple_proj):
    names = ("g_mix", "w_in", "conv_w", "conv_b", "w_rg", "b_rg", "w_ig", "b_ig", "lru_lambda", "w_rnn_proj",
             "q_gain", "k_gain", "sinks", "w_attn_proj", "w_out", "g_mlp", "w_up", "w_down", "g_ple",
             "w_ple_gate", "w_ple_proj")
    wts = dict(zip(names, (g_mix, w_in, conv_w, conv_b, w_rg, b_rg, w_ig, b_ig, lru_lambda, w_rnn_proj, q_gain,
                           k_gain, sinks, w_attn_proj, w_out, g_mlp, w_up, w_down, g_ple, w_ple_gate, w_ple_proj)))
    mom1 = dict(zip(names, (m_g_mix, m_w_in, m_conv_w, m_conv_b, m_w_rg, m_b_rg, m_w_ig, m_b_ig, m_lru_lambda,
                            m_w_rnn_proj, m_q_gain, m_k_gain, m_sinks, m_w_attn_proj, m_w_out, m_g_mlp, m_w_up,
                            m_w_down, m_g_ple, m_w_ple_gate, m_w_ple_proj)))
    mom2 = dict(zip(names, (v_g_mix, v_w_in, v_conv_w, v_conv_b, v_w_rg, v_b_rg, v_w_ig, v_b_ig, v_lru_lambda,
                            v_w_rnn_proj, v_q_gain, v_k_gain, v_sinks, v_w_attn_proj, v_w_out, v_g_mlp, v_w_up,
                            v_w_down, v_g_ple, v_w_ple_gate, v_w_ple_proj)))
    n_seq, seq, _ = x.shape
    me = 4 * lax.axis_index("x") + 2 * lax.axis_index("y") + lax.axis_index("c")
    core = lax.axis_index("c").astype(jnp.int32).reshape(1)

    shards = {nm: wts[nm][0].astype(BF16) for nm in BIG}
    w_in_all, conv_all = _gather_two_level([shards["w_in"], conv_w[0]], name="gather_w_in")
    w = {nm: wts[nm] for nm in names if nm not in BIG}
    w["w_rg"], w["w_ig"] = w_rg[0], w_ig[0]
    w["conv_w"] = conv_all.transpose(1, 0, 2).reshape(CONV_W, D_MODEL)
    w["w_in"] = _full_weight("w_in", w_in_all)
    comm = _StepExchanges(shards, core)
    loss_sum, grad_x, g = _local_step(
        x.reshape(n_seq * seq, D_MODEL), p.reshape(n_seq * seq, PLE_DIM), loss_target.reshape(n_seq * seq, D_MODEL),
        w, n_seq=n_seq, seq=seq, comm=comm)
    loss = lax.psum(loss_sum, ("x", "y", "c")) * (0.5 / D_MODEL)

    res = {}
    for batch, batch_names in BATCHES.items():
        for nm, summed in zip(batch_names, comm.summed[batch]):
            res[nm] = _adamw(summed, wts[nm][0], mom1[nm][0], mom2[nm][0], name="adamw_" + nm)
    g_mix_parts, = _exchange([g["g_mix"]], ["gather"], name="gather_g_mix")
    res["g_mix"] = [r[0] for r in _adamw(g_mix_parts, g_mix, m_g_mix, v_g_mix, name="adamw_g_mix")]
    small_names = [nm for nm, _ in SMALL]
    conv_lanes = D_MODEL // N_DEV
    full_small = {}
    for src, key in ((wts, "w"), (mom1, "m"), (mom2, "v")):
        vals = {nm: src[nm][0] for nm in small_names if nm != "conv_w"}
        vals["conv_w"] = lax.dynamic_update_slice(jnp.zeros((CONV_W, D_MODEL), F32), src["conv_w"][0], (0, me * conv_lanes))
        full_small[key] = _pack_small(vals)
    small_res = _adamw(comm.summed[3][-1], full_small["w"], full_small["m"], full_small["v"], name="adamw_small")
    shapes = {nm: wts[nm].shape[1:] for nm in small_names}
    shapes["conv_w"] = (CONV_W, D_MODEL)
    small_out = [_unpack_small(r, shapes) for r in small_res]
    for nm in small_names:
        vals = [so[nm] for so in small_out]
        if nm == "conv_w":
            vals = [lax.dynamic_slice(a, (0, me * conv_lanes), (CONV_W, conv_lanes)) for a in vals]
        res[nm] = vals

    outs = [loss, grad_x.reshape(n_seq, seq, D_MODEL)]
    for k in range(4):
        outs.extend(res[nm][k][None] for nm in names)
    return tuple(outs)
```

```python
import functools
from typing import Callable, NamedTuple

import jax
import jax.numpy as jnp
from jax import lax
from jax.experimental import pallas as pl
from jax.experimental.pallas import tpu as pltpu

F32 = jnp.float32
BF16 = jnp.bfloat16

N_DEV = 8
D_MODEL = 1024
RNN_BLOCK_W = 64
CONV_W = 4
LRU_C = 8.0
HEAD_DIM = 64
N_Q_HEADS = 16
N_KV_HEADS = 4
KV_W = N_KV_HEADS * HEAD_DIM
WINDOW = 128
ROPE_THETA = 10000.0
D_FF = 4096
PLE_DIM = 256
NORM_EPS = 1e-6
IN_TOTAL = 5632
COL_XRNN, COL_GRNN, COL_Q, COL_K, COL_V, COL_GA, COL_GB = 0, 1024, 2048, 3072, 3328, 3584, 4608

ADAM_LR = 0.001
ADAM_B1 = 0.9
ADAM_B2 = 0.999
ADAM_EPS = 1e-08
ADAM_WD = 0.01
ADAM_STEP = 10

LANES = 128
SUBLANES = 8
RNN_TILE = 256
VMEM_LIMIT = 48 * 1024 * 1024
NEG_BIG = -1e30


def _params(*sem):
    return pltpu.CompilerParams(dimension_semantics=sem if sem else None, vmem_limit_bytes=VMEM_LIMIT)


def _sig(x):
    return 0.5 * jnp.tanh(0.5 * x) + 0.5


def _dot_nt(a, b):
    return lax.dot_general(a, b, (((1,), (1,)), ((), ())), preferred_element_type=F32)


def _dot_tn(a, b):
    return lax.dot_general(a, b, (((0,), (0,)), ((), ())), preferred_element_type=F32)


class _Xfer:
    def __init__(self, start, wait):
        self.start, self.wait = start, wait


class _Hosted(NamedTuple):
    srcs: tuple
    out_shape: tuple
    n_sems: int
    plan: Callable
    aliases: tuple = ()


def _merge_hosted(parts):
    parts = [p for p in parts if p is not None]
    if len(parts) <= 1:
        return parts[0] if parts else None
    src_at, dst_at, sem_at, aliases = [0], [0], [0], []
    for p in parts:
        aliases += [(i + src_at[-1], j + dst_at[-1]) for i, j in p.aliases]
        src_at.append(src_at[-1] + len(p.srcs))
        dst_at.append(dst_at[-1] + len(p.out_shape))
        sem_at.append(sem_at[-1] + p.n_sems)

    def plan(src, dst, send_sems, recv_sems, local_sems, first_sem):
        copies = []
        for k, p in enumerate(parts):
            copies += p.plan(src[src_at[k]:src_at[k + 1]], dst[dst_at[k]:dst_at[k + 1]], send_sems, recv_sems,
                             local_sems, first_sem + sem_at[k])
        return copies

    return _Hosted(tuple(a for p in parts for a in p.srcs), tuple(s for p in parts for s in p.out_shape),
                   sem_at[-1], plan, tuple(aliases))


def _call(body, *, grid, in_specs, out_specs, out_shape, args, name, semantics, scratch_shapes=(), hosted=None):
    if hosted is None:
        outs = pl.pallas_call(body, grid=grid, in_specs=list(in_specs), out_specs=list(out_specs),
                              out_shape=list(out_shape), scratch_shapes=list(scratch_shapes),
                              compiler_params=_params(*semantics), name=name)(*args)
        return list(outs), []
    counts = (len(in_specs), len(hosted.srcs), len(out_specs), len(hosted.out_shape), len(scratch_shapes), 3)

    def wrapped(*refs):
        at, groups = 0, []
        for count in counts:
            groups.append(refs[at:at + count])
            at += count
        ins, srcs, outs, dsts, scratch, sems = groups
        copies = hosted.plan(srcs, dsts, *sems, 0)
        ids = [pl.program_id(axis) for axis in range(len(grid))]
        first = functools.reduce(jnp.logical_and, [i == 0 for i in ids])
        last = functools.reduce(jnp.logical_and, [i == g - 1 for i, g in zip(ids, grid)])

        @pl.when(first)
        def _():
            for cp in copies:
                cp.start()

        body(*ins, *outs, *scratch)

        @pl.when(last)
        def _():
            for cp in copies:
                cp.wait()

    any_spec = pl.BlockSpec(memory_space=pl.ANY)
    sems = [pltpu.SemaphoreType.DMA((hosted.n_sems,))] * 3
    outs = pl.pallas_call(
        wrapped, grid=grid, in_specs=list(in_specs) + [any_spec] * counts[1],
        out_specs=list(out_specs) + [any_spec] * counts[3], out_shape=list(out_shape) + list(hosted.out_shape),
        scratch_shapes=list(scratch_shapes) + sems, compiler_params=_params(*["arbitrary"] * len(grid)),
        input_output_aliases={counts[0] + i: counts[2] + j for i, j in hosted.aliases},
        name=name)(*args, *hosted.srcs)
    return list(outs[:counts[2]]), list(outs[counts[2]:])


def _matmul(a, b, *, mode, tm, tn, out_dtypes, name, epilogue=None, extras=(), hosted=None):
    m, k = a.shape
    n = b.shape[1] if mode == "nn" else b.shape[0]
    tm, tn = min(tm, m), min(tn, n)
    n_extra = len(extras)

    def body(a_ref, b_ref, *rest):
        extra_refs, out_refs = rest[:n_extra], rest[n_extra:]
        if mode == "nn":
            acc = jnp.dot(a_ref[...], b_ref[...], preferred_element_type=F32)
        else:
            acc = _dot_nt(a_ref[...], b_ref[...])
        res = epilogue(acc, *[e[...] for e in extra_refs]) if epilogue is not None else (acc,)
        for o_ref, r in zip(out_refs, res):
            o_ref[...] = r.astype(o_ref.dtype)

    b_spec = pl.BlockSpec((k, tn), lambda i, j: (0, j)) if mode == "nn" else pl.BlockSpec((tn, k), lambda i, j: (j, 0))
    tile = pl.BlockSpec((tm, tn), lambda i, j: (i, j))
    outs, landed = _call(
        body,
        grid=(m // tm, n // tn),
        in_specs=[pl.BlockSpec((tm, k), lambda i, j: (i, 0)), b_spec] + [tile] * n_extra,
        out_specs=[tile] * len(out_dtypes),
        out_shape=[jax.ShapeDtypeStruct((m, n), dt) for dt in out_dtypes],
        args=(a, b, *extras), name=name, semantics=("parallel", "arbitrary"), hosted=hosted)
    if hosted is not None:
        return (*outs, landed)
    return outs[0] if len(outs) == 1 else outs


def _matmul_tn(a, b, *, tk, tn, tt, name, slot_cols=None):
    t, k = a.shape
    n = b.shape[1]
    tk, tn, tt = min(tk, k), min(tn, n), min(tt, t)

    def body(a_ref, b_ref, o_ref):
        @pl.when(pl.program_id(2) == 0)
        def _():
            o_ref[...] = jnp.zeros_like(o_ref)

        if slot_cols is None:
            o_ref[...] += _dot_tn(a_ref[...], b_ref[...])
        else:
            av = a_ref[...]
            for s in range(tn // slot_cols):
                o_ref[s] += _dot_tn(av, b_ref[:, s * slot_cols:(s + 1) * slot_cols])

    if slot_cols is not None:
        out_spec = pl.BlockSpec((tn // slot_cols, tk, slot_cols), lambda i, j, s: (j, i, 0))
        out_shape = jax.ShapeDtypeStruct((n // slot_cols, k, slot_cols), F32)
    else:
        out_spec = pl.BlockSpec((tk, tn), lambda i, j, s: (i, j))
        out_shape = jax.ShapeDtypeStruct((k, n), F32)
    return pl.pallas_call(
        body,
        grid=(k // tk, n // tn, t // tt),
        in_specs=[pl.BlockSpec((tt, tk), lambda i, j, s: (s, i)), pl.BlockSpec((tt, tn), lambda i, j, s: (s, j))],
        out_specs=out_spec,
        out_shape=out_shape,
        compiler_params=_params("parallel", "parallel", "arbitrary"),
        name=name,
    )(a, b)


def _rmsnorm_fwd(x, g, *, name):
    t, d = x.shape
    tm = min(512, t)

    def body(x_ref, g_ref, o_ref):
        xv = x_ref[...]
        r = lax.rsqrt(jnp.mean(xv * xv, axis=-1, keepdims=True) + NORM_EPS)
        o_ref[...] = (xv * r * g_ref[...]).astype(BF16)

    return pl.pallas_call(
        body,
        grid=(t // tm,),
        in_specs=[pl.BlockSpec((tm, d), lambda i: (i, 0)), pl.BlockSpec((1, d), lambda i: (0, 0))],
        out_specs=pl.BlockSpec((tm, d), lambda i: (i, 0)),
        out_shape=jax.ShapeDtypeStruct((t, d), BF16),
        compiler_params=_params("parallel"),
        name=name,
    )(x, g)


def _rmsnorm_bwd(x, g, dy, dres, *, name, want_bf16, hosted=None):
    t, d = x.shape
    tm = min(256, t)

    def body(x_ref, g_ref, dy_ref, dres_ref, *out_refs):
        dx_ref, dg_ref = out_refs[0], out_refs[-1]
        xv, dyv = x_ref[...], dy_ref[...]
        r = lax.rsqrt(jnp.mean(xv * xv, axis=-1, keepdims=True) + NORM_EPS)
        xr = xv * r
        gy = dyv * g_ref[...]
        dx = dres_ref[...] + r * (gy - xr * jnp.mean(gy * xr, axis=-1, keepdims=True))
        dx_ref[...] = dx
        if want_bf16:
            out_refs[1][...] = dx.astype(BF16)

        @pl.when(pl.program_id(0) == 0)
        def _():
            dg_ref[...] = jnp.zeros_like(dg_ref)

        dg_ref[...] += jnp.sum(dyv * xr, axis=0, keepdims=True)

    tile = pl.BlockSpec((tm, d), lambda i: (i, 0))
    vec = pl.BlockSpec((1, d), lambda i: (0, 0))
    out_specs = [tile] + ([tile] if want_bf16 else []) + [vec]
    out_shape = [jax.ShapeDtypeStruct((t, d), F32)] + ([jax.ShapeDtypeStruct((t, d), BF16)] if want_bf16 else [])
    out_shape.append(jax.ShapeDtypeStruct((1, d), F32))
    outs, landed = _call(body, grid=(t // tm,), in_specs=[tile, vec, tile, tile], out_specs=out_specs,
                         out_shape=out_shape, args=(x, g, dy, dres), name=name, semantics=("arbitrary",), hosted=hosted)
    return (*outs, landed) if hosted is not None else outs


def _softplus_neg(lam):
    z = -lam
    return jnp.maximum(z, 0.0) + jnp.log1p(jnp.exp(-jnp.abs(z)))


def _neg_expm1(y, exp_half_y):
    series = -y * (1.0 + y * 0.5 * (1.0 + y * (1.0 / 3.0) * (1.0 + y * 0.25 * (1.0 + y * 0.2))))
    return jnp.where(y > -0.0625, series, 1.0 - exp_half_y * exp_half_y)


def _gelu_parts(x):
    c = 0.7978845608028654
    u = c * (x + 0.044715 * x * x * x)
    th = jnp.tanh(u)
    gel = 0.5 * x * (1.0 + th)
    dgel = 0.5 * (1.0 + th) + 0.5 * x * (1.0 - th * th) * c * (1.0 + 3.0 * 0.044715 * x * x)
    return gel, dgel


def _shift_down(v, k, rows):
    return jnp.where(rows < k, 0.0, pltpu.roll(v, k, 0))


def _shift_up(v, k, rows, n):
    return jnp.where(rows >= n - k, 0.0, pltpu.roll(v, n - k, 0))


def _scan_within_groups(a, b, rows, *, reverse):
    n = a.shape[0]
    in_group = rows & (SUBLANES - 1)
    for s in (1, 2, 4):
        if reverse:
            inside, shift = in_group < SUBLANES - s, n - s
        else:
            inside, shift = in_group >= s, s
        b = b + a * jnp.where(inside, pltpu.roll(b, shift, 0), 0.0)
        a = a * jnp.where(inside, pltpu.roll(a, shift, 0), 1.0)
    return a, b


def _rnn_gates(xc, wrg, brg, wig, big, lam):
    xcb = xc.astype(BF16)
    r = _sig(jnp.dot(xcb, wrg, preferred_element_type=F32) + brg)
    i = _sig(jnp.dot(xcb, wig, preferred_element_type=F32) + big)
    sp = _softplus_neg(lam)
    log_a = -LRU_C * r * sp
    a = jnp.exp(log_a)
    mult = jnp.sqrt(_neg_expm1(2.0 * log_a, a))
    return xcb, r, i, sp, a, mult


def _conv_fwd(xv, cw, cb, rows):
    return (cb + _shift_down(xv, 3, rows) * cw[0:1, :] + _shift_down(xv, 2, rows) * cw[1:2, :]
            + _shift_down(xv, 1, rows) * cw[2:3, :] + xv * cw[3:4, :])


def _rnn_fwd(z, conv_w, conv_b, wrg_bd, b_rg, wig_bd, b_ig, lam, *, n_seq, seq, hosted=None):
    t = n_seq * seq
    ct = RNN_TILE
    n_ct = D_MODEL // ct

    def body(x_ref, g_ref, cw_ref, cb_ref, wrg_ref, brg_ref, wig_ref, big_ref, lam_ref,
             xc_ref, hr_ref, ya_ref, a_s, b_s):
        rows = lax.broadcasted_iota(jnp.int32, (seq, ct), 0)
        xc = _conv_fwd(x_ref[...], cw_ref[...], cb_ref[...], rows)
        _, r, i, sp, a, mult = _rnn_gates(xc, wrg_ref[...], brg_ref[...], wig_ref[...], big_ref[...], lam_ref[...])
        a_s[...], b_s[...] = _scan_within_groups(a, mult * (i * xc), rows, reverse=False)

        def step(j, carry):
            r0 = pl.multiple_of(j * SUBLANES, SUBLANES)
            h = b_s[pl.ds(r0, SUBLANES), :] + a_s[pl.ds(r0, SUBLANES), :] * carry
            hr_ref[pl.ds(r0, SUBLANES), :] = h
            return h[SUBLANES - 1:SUBLANES, :]

        lax.fori_loop(0, seq // SUBLANES, step, jnp.zeros((1, ct), F32), unroll=4)
        gel, _ = _gelu_parts(g_ref[...])
        xc_ref[...] = xc
        ya_ref[...] = (hr_ref[...] * gel).astype(BF16)

    vec = pl.BlockSpec((1, ct), lambda b, c: (0, c))
    gate_w = pl.BlockSpec((None, ct, ct), lambda b, c: (c, 0, 0))
    tile = pl.BlockSpec((seq, ct), lambda b, c: (b, c))
    outs, landed = _call(
        body,
        grid=(n_seq, n_ct),
        in_specs=[
            pl.BlockSpec((seq, ct), lambda b, c: (b, COL_XRNN // ct + c)),
            pl.BlockSpec((seq, ct), lambda b, c: (b, COL_GRNN // ct + c)),
            pl.BlockSpec((CONV_W, ct), lambda b, c: (0, c)), vec, gate_w, vec, gate_w, vec, vec,
        ],
        out_specs=[tile, tile, tile],
        out_shape=[jax.ShapeDtypeStruct((t, D_MODEL), F32), jax.ShapeDtypeStruct((t, D_MODEL), F32),
                   jax.ShapeDtypeStruct((t, D_MODEL), BF16)],
        scratch_shapes=[pltpu.VMEM((seq, ct), F32), pltpu.VMEM((seq, ct), F32)],
        args=(z, z, conv_w, conv_b, wrg_bd, b_rg, wig_bd, b_ig, lam), name="rnn_fwd",
        semantics=("parallel", "parallel"), hosted=hosted)
    return (*outs, landed) if hosted is not None else outs


def _rnn_bwd(dya, z, xc, hr, conv_w, wrg_bd, b_rg, wig_bd, b_ig, lam, *, n_seq, seq, hosted=None):
    t = n_seq * seq
    ct = RNN_TILE
    n_ct = D_MODEL // ct

    def body(dya_ref, x_ref, g_ref, xc_ref, hr_ref, cw_ref, wrg_ref, brg_ref, wig_ref, big_ref, lam_ref,
             dx_ref, dg_ref, dwrg_ref, dwig_ref, vec_ref, a_s, d_s, g_s):
        rows = lax.broadcasted_iota(jnp.int32, (seq, ct), 0)
        xv, xc, hr, dyv = x_ref[...], xc_ref[...], hr_ref[...], dya_ref[...]
        lamv = lam_ref[...]
        gel, dgel = _gelu_parts(g_ref[...])
        dg_ref[...] = (dyv * hr * dgel).astype(BF16)
        xcb, r, i, sp, a, mult = _rnn_gates(xc, wrg_ref[...], brg_ref[...], wig_ref[...], big_ref[...], lamv)
        a_s[...], d_s[...] = _scan_within_groups(_shift_up(a, 1, rows, seq), dyv * gel, rows, reverse=True)

        def step(k, carry):
            r0 = pl.multiple_of((seq // SUBLANES - 1 - k) * SUBLANES, SUBLANES)
            gs = d_s[pl.ds(r0, SUBLANES), :] + a_s[pl.ds(r0, SUBLANES), :] * carry
            g_s[pl.ds(r0, SUBLANES), :] = gs
            return gs[0:1, :]

        lax.fori_loop(0, seq // SUBLANES, step, jnp.zeros((1, ct), F32), unroll=4)
        gsum = g_s[...]
        gated = i * xc
        d_log_a = gsum * _shift_down(hr, 1, rows) * a - gsum * gated * (a * a / mult)
        d_gated = gsum * mult
        d_pre_r = (d_log_a * (-LRU_C) * sp) * r * (1.0 - r)
        d_pre_i = (d_gated * xc) * i * (1.0 - i)
        dprb, dpib = d_pre_r.astype(BF16), d_pre_i.astype(BF16)
        dxc = d_gated * i + _dot_nt(dprb, wrg_ref[...]) + _dot_nt(dpib, wig_ref[...])
        cw = cw_ref[...]
        dx = (dxc * cw[3:4, :] + _shift_up(dxc, 1, rows, seq) * cw[2:3, :]
              + _shift_up(dxc, 2, rows, seq) * cw[1:2, :] + _shift_up(dxc, 3, rows, seq) * cw[0:1, :])
        dx_ref[...] = dx.astype(BF16)

        @pl.when(pl.program_id(1) == 0)
        def _():
            dwrg_ref[...] = jnp.zeros_like(dwrg_ref)
            dwig_ref[...] = jnp.zeros_like(dwig_ref)
            vec_ref[...] = jnp.zeros_like(vec_ref)

        dwrg_ref[...] += _dot_tn(xcb, dprb)
        dwig_ref[...] += _dot_tn(xcb, dpib)

        def colsum(v):
            return jnp.sum(v, axis=0, keepdims=True)

        d_sp = colsum(d_log_a * (-LRU_C) * r)
        vec_ref[0:1, :] += colsum(d_pre_r)
        vec_ref[1:2, :] += colsum(d_pre_i)
        vec_ref[2:3, :] += d_sp * (-_sig(-lamv))
        vec_ref[3:4, :] += colsum(dxc)
        vec_ref[4:5, :] += colsum(dxc * _shift_down(xv, 3, rows))
        vec_ref[5:6, :] += colsum(dxc * _shift_down(xv, 2, rows))
        vec_ref[6:7, :] += colsum(dxc * _shift_down(xv, 1, rows))
        vec_ref[7:8, :] += colsum(dxc * xv)

    vec = pl.BlockSpec((1, ct), lambda c, b: (0, c))
    gate_w = pl.BlockSpec((None, ct, ct), lambda c, b: (c, 0, 0))
    tile = pl.BlockSpec((seq, ct), lambda c, b: (b, c))
    outs, landed = _call(
        body,
        grid=(n_ct, n_seq),
        in_specs=[
            tile,
            pl.BlockSpec((seq, ct), lambda c, b: (b, COL_XRNN // ct + c)),
            pl.BlockSpec((seq, ct), lambda c, b: (b, COL_GRNN // ct + c)),
            tile, tile,
            pl.BlockSpec((CONV_W, ct), lambda c, b: (0, c)), gate_w, vec, gate_w, vec, vec,
        ],
        out_specs=[tile, tile, gate_w, gate_w, pl.BlockSpec((8, ct), lambda c, b: (0, c))],
        out_shape=[jax.ShapeDtypeStruct((t, D_MODEL), BF16), jax.ShapeDtypeStruct((t, D_MODEL), BF16),
                   jax.ShapeDtypeStruct((n_ct, ct, ct), F32), jax.ShapeDtypeStruct((n_ct, ct, ct), F32),
                   jax.ShapeDtypeStruct((8, D_MODEL), F32)],
        scratch_shapes=[pltpu.VMEM((seq, ct), F32)] * 3,
        args=(dya, z, z, xc, hr, conv_w, wrg_bd, b_rg, wig_bd, b_ig, lam), name="rnn_bwd",
        semantics=("parallel", "arbitrary"), hosted=hosted)
    return (*outs, landed) if hosted is not None else outs


def _split_hi_lo(x):
    hi = x.astype(BF16)
    return hi, (x - hi.astype(F32)).astype(BF16)


def _dot_split(x, m_twice):
    hi, lo = _split_hi_lo(x)
    return jnp.dot(jnp.concatenate([hi, lo], axis=1), m_twice, preferred_element_type=F32)


def _head_matrices(width):
    ec = ((lax.broadcasted_iota(jnp.int32, (2 * width, LANES), 0) & (width - 1)) // HEAD_DIM
          == lax.broadcasted_iota(jnp.int32, (2 * width, LANES), 1))
    ee = (lax.broadcasted_iota(jnp.int32, (2 * LANES, width), 1) // HEAD_DIM
          == (lax.broadcasted_iota(jnp.int32, (2 * LANES, width), 0) & (LANES - 1)))
    return jnp.where(ec, 1.0, 0.0).astype(BF16), jnp.where(ee, 1.0, 0.0).astype(BF16)


def _swap_halves(y):
    w = y.shape[1]
    first = (lax.broadcasted_iota(jnp.int32, y.shape, 1) % HEAD_DIM) < HEAD_DIM // 2
    return jnp.where(first, pltpu.roll(y, w - HEAD_DIM // 2, 1), pltpu.roll(y, HEAD_DIM // 2, 1))


def _normrope_fwd(x, gain, cos_t, sin_t, ec, ee):
    w = x.shape[1]
    rs = _dot_split(lax.rsqrt(_dot_split(x * x, ec) * (1.0 / HEAD_DIM) + NORM_EPS), ee)
    nx = x * rs
    y = nx * gain
    reps = w // LANES
    out = y * jnp.tile(cos_t, (1, reps)) + _swap_halves(y) * jnp.tile(sin_t, (1, reps))
    return out, nx, rs


def _normrope_bwd(dout, nx, rs, gain, cos_t, sin_t, ec, ee):
    w = dout.shape[1]
    reps = w // LANES
    dy = dout * jnp.tile(cos_t, (1, reps)) + _swap_halves(dout * jnp.tile(sin_t, (1, reps)))
    dgain = jnp.sum(dy * nx, axis=0, keepdims=True)
    dn = dy * gain
    seg = _dot_split(_dot_split(dn * nx, ec) * (1.0 / HEAD_DIM), ee)
    return rs * (dn - nx * seg), dgain


def _pair_operand(t, group):
    chunk = t[:, (group // 2) * LANES:(group // 2 + 1) * LANES]
    low = lax.broadcasted_iota(jnp.int32, chunk.shape, 1) < HEAD_DIM
    rolled = pltpu.roll(chunk, HEAD_DIM, 1)
    return jnp.where(low, chunk, rolled) if group % 2 == 0 else jnp.where(low, rolled, chunk)


GROUP = N_Q_HEADS // N_KV_HEADS
GROUP_W = GROUP * HEAD_DIM


def _replicate_head(t, group):
    return jnp.tile(_pair_operand(t, group), (1, 2))


def _head_blocks(t):
    seg = lax.broadcasted_iota(jnp.int32, t.shape, 1) // HEAD_DIM
    return jnp.concatenate([jnp.where(seg == h, t, 0.0) for h in range(GROUP)], axis=0)


def _stack_heads(t_t, rows):
    return jnp.concatenate([t_t[:, h * rows:(h + 1) * rows] for h in range(GROUP)], axis=0)


def _head_rows(mat_t, group):
    return jnp.concatenate([mat_t[GROUP * group + h:GROUP * group + h + 1, :] for h in range(GROUP)], axis=1)


def _window_masks(blk):
    key = lax.broadcasted_iota(jnp.int32, (blk, GROUP * blk), 0)
    query = lax.broadcasted_iota(jnp.int32, (blk, GROUP * blk), 1) & (blk - 1)
    return key > query, key <= query


def _mask_window(t, before_ok, own_ok, fill):
    blk = t.shape[0] // 2
    return jnp.concatenate([jnp.where(before_ok, t[:blk], fill), jnp.where(own_ok, t[blk:], fill)], axis=0)


def _attn_fwd(z, cos_t, sin_t, q_gain_t, k_gain_t, sinks_t, *, n_seq, seq, hosted=None):
    t = n_seq * seq
    blk = WINDOW
    nb = seq // blk

    def body(q_ref, kp_ref, kc_ref, vp_ref, vc_ref, cosc_ref, sinc_ref, cosp_ref, sinp_ref, qg_ref, kg_ref, sk_ref,
             o_ref, l_ref):
        n = pl.program_id(1)
        ecq, eeq = _head_matrices(D_MODEL)
        eck, eek = _head_matrices(KV_W)
        cosc, sinc = cosc_ref[...], sinc_ref[...]
        qh, _, _ = _normrope_fwd(q_ref[...], qg_ref[...], cosc, sinc, ecq, eeq)
        qh = qh * (HEAD_DIM ** -0.5)
        kc, _, _ = _normrope_fwd(kc_ref[...], kg_ref[...], cosc, sinc, eck, eek)
        kp, _, _ = _normrope_fwd(kp_ref[...], kg_ref[...], cosp_ref[...], sinp_ref[...], eck, eek)
        kcat = jnp.concatenate([kp, kc], axis=0)
        vcat = jnp.concatenate([vp_ref[...], vc_ref[...]], axis=0)
        above, causal = _window_masks(blk)
        above = above & (n > 0)
        head_row = lax.broadcasted_iota(jnp.int32, (blk, blk), 0)
        sk_t = jnp.broadcast_to(sk_ref[...], (blk, LANES)).T
        vcat_t = vcat.T.astype(BF16)
        lmat = jnp.zeros((blk, blk), F32)
        groups = range(N_KV_HEADS)
        cols = [slice(g * GROUP_W, (g + 1) * GROUP_W) for g in groups]
        scores = [_dot_nt(_replicate_head(kcat, g).astype(BF16), _head_blocks(qh[:, cols[g]]).astype(BF16))
                  for g in groups]
        probs = []
        for g in groups:
            s = _mask_window(scores[g], above, causal, NEG_BIG)
            sink = _head_rows(sk_t, g)
            m = jnp.maximum(jnp.max(s, axis=0, keepdims=True), sink)
            e = jnp.exp(s - m)
            den = jnp.sum(e, axis=0, keepdims=True) + jnp.exp(sink - m)
            probs.append((e * (1.0 / den)).astype(BF16))
            lse = m + jnp.log(den)
            for h in range(GROUP):
                lmat = lmat + jnp.where(head_row == GROUP * g + h, lse[:, h * blk:(h + 1) * blk], 0.0)
        for g in groups:
            out_t = jnp.dot(vcat_t[g * HEAD_DIM:(g + 1) * HEAD_DIM], probs[g], preferred_element_type=F32)
            o_ref[:, cols[g]] = _stack_heads(out_t, blk).T.astype(BF16)
        l_ref[...] = lmat

    def row(b, n):
        return b * nb + n

    def prev(b, n):
        return b * nb + jnp.maximum(n - 1, 0)

    kw = KV_W
    tab_c = pl.BlockSpec((blk, LANES), lambda b, n: (n, 0))
    tab_p = pl.BlockSpec((blk, LANES), lambda b, n: (jnp.maximum(n - 1, 0), 0))
    outs, landed = _call(
        body,
        grid=(n_seq, nb),
        in_specs=[
            pl.BlockSpec((blk, D_MODEL), lambda b, n: (row(b, n), COL_Q // D_MODEL)),
            pl.BlockSpec((blk, kw), lambda b, n: (prev(b, n), COL_K // kw)),
            pl.BlockSpec((blk, kw), lambda b, n: (row(b, n), COL_K // kw)),
            pl.BlockSpec((blk, kw), lambda b, n: (prev(b, n), COL_V // kw)),
            pl.BlockSpec((blk, kw), lambda b, n: (row(b, n), COL_V // kw)),
            tab_c, tab_c, tab_p, tab_p,
            pl.BlockSpec((1, D_MODEL), lambda b, n: (0, 0)),
            pl.BlockSpec((1, kw), lambda b, n: (0, 0)),
            pl.BlockSpec((1, LANES), lambda b, n: (0, 0)),
        ],
        out_specs=[pl.BlockSpec((blk, D_MODEL), lambda b, n: (row(b, n), 0)),
                   pl.BlockSpec((blk, LANES), lambda b, n: (row(b, n), 0))],
        out_shape=[jax.ShapeDtypeStruct((t, D_MODEL), BF16), jax.ShapeDtypeStruct((t, LANES), F32)],
        args=(z, z, z, z, z, cos_t, sin_t, cos_t, sin_t, q_gain_t, k_gain_t, sinks_t), name="attn_fwd",
        semantics=("parallel", "parallel"), hosted=hosted)
    return (*outs, landed) if hosted is not None else outs


def _attn_bwd(z, o, lse, do, cos_t, sin_t, q_gain_t, k_gain_t, sinks_t, *, n_seq, seq, hosted=None):
    t = n_seq * seq
    blk = WINDOW
    nb = seq // blk
    kw = KV_W
    scale = HEAD_DIM ** -0.5

    def body(qc_ref, qn_ref, kp_ref, kc_ref, vp_ref, vc_ref, oc_ref, on_ref, doc_ref, don_ref, lc_ref, ln_ref,
             cosc_ref, sinc_ref, cosp_ref, sinp_ref, cosn_ref, sinn_ref, qg_ref, kg_ref, sk_ref,
             dq_ref, dk_ref, dv_ref, vec_ref):
        n = pl.program_id(1)
        ecq, eeq = _head_matrices(D_MODEL)
        eck, eek = _head_matrices(KV_W)
        cosc, sinc = cosc_ref[...], sinc_ref[...]
        qg, kg = qg_ref[...], kg_ref[...]
        qhc, nqc, rsqc = _normrope_fwd(qc_ref[...], qg, cosc, sinc, ecq, eeq)
        qhn, _, _ = _normrope_fwd(qn_ref[...], qg, cosn_ref[...], sinn_ref[...], ecq, eeq)
        khc, nkc, rskc = _normrope_fwd(kc_ref[...], kg, cosc, sinc, eck, eek)
        khp, _, _ = _normrope_fwd(kp_ref[...], kg, cosp_ref[...], sinp_ref[...], eck, eek)
        doc = doc_ref[...].astype(F32)
        don = don_ref[...].astype(F32)
        delc = _dot_split(doc * oc_ref[...].astype(F32), ecq)
        deln = _dot_split(don * on_ref[...].astype(F32), ecq)
        lc_t, ln_t, delc_t, deln_t = lc_ref[...], ln_ref[...], delc.T, deln.T
        above, causal = _window_masks(blk)
        above_c, above_n = above & (n > 0), above & (n < nb - 1)
        seg = lax.broadcasted_iota(jnp.int32, (blk, GROUP_W), 1) // HEAD_DIM
        lane = lax.broadcasted_iota(jnp.int32, (1, LANES), 1)
        sk_t = jnp.broadcast_to(sk_ref[...], (blk, LANES)).T
        dsink = jnp.zeros((1, LANES), F32)
        kcat = jnp.concatenate([khp, khc], axis=0)
        vcat = jnp.concatenate([vp_ref[...], vc_ref[...]], axis=0)
        kcat_t = kcat.T.astype(BF16)
        dkh = jnp.zeros((blk, GROUP_W), F32)
        dvh = jnp.zeros((blk, GROUP_W), F32)

        def fold_to(group, t):
            total = t + pltpu.roll(t, HEAD_DIM, 1)
            total = total + pltpu.roll(total, 2 * HEAD_DIM, 1)
            return jnp.where(seg == group, total, 0.0)

        groups = range(N_KV_HEADS)
        cols = [slice(g * GROUP_W, (g + 1) * GROUP_W) for g in groups]
        qsc, qsn = qhc * scale, qhn * scale
        qb_c = [_head_blocks(qsc[:, cols[g]]).astype(BF16) for g in groups]
        qb_n = [_head_blocks(qsn[:, cols[g]]).astype(BF16) for g in groups]
        dob_c = [_head_blocks(doc[:, cols[g]]).astype(BF16) for g in groups]
        dob_n = [_head_blocks(don[:, cols[g]]).astype(BF16) for g in groups]
        raw = []
        for g in groups:
            krep = _replicate_head(kcat, g).astype(BF16)
            vrep = _replicate_head(vcat, g).astype(BF16)
            raw.append((_dot_nt(krep, qb_c[g]), _dot_nt(vrep, dob_c[g]),
                        _dot_nt(krep[blk:], qb_n[g]), _dot_nt(vrep[blk:], dob_n[g])))
        cooked = []
        for g in groups:
            s_c, dp_c, s_n, dp_n = raw[g]
            l_row, d_row = _head_rows(lc_t, g), _head_rows(delc_t, g)
            p_c = _mask_window(jnp.exp(s_c - l_row), above_c, causal, 0.0)
            ds_c = (p_c * (dp_c - d_row)).astype(BF16)
            p_n = jnp.where(above_n, jnp.exp(s_n - _head_rows(ln_t, g)), 0.0)
            ds_n = (p_n * (dp_n - _head_rows(deln_t, g))).astype(BF16)
            cooked.append((p_c[blk:].astype(BF16), ds_c, p_n.astype(BF16), ds_n))
            p_sink = jnp.exp(_head_rows(sk_t, g) - l_row) * d_row
            for h in range(GROUP):
                dsink = dsink + jnp.where(lane == GROUP * g + h,
                                          -jnp.sum(p_sink[:, h * blk:(h + 1) * blk], axis=1, keepdims=True), 0.0)
        for g in groups:
            p_cb, ds_c, p_nb, ds_n = cooked[g]
            dq_t = jnp.dot(kcat_t[g * HEAD_DIM:(g + 1) * HEAD_DIM], ds_c, preferred_element_type=F32)
            dq_ref[:, cols[g]] = _stack_heads(dq_t, blk).T * scale
            dk_rep = (jnp.dot(ds_c[blk:], qb_c[g], preferred_element_type=F32)
                      + jnp.dot(ds_n, qb_n[g], preferred_element_type=F32))
            dv_rep = (jnp.dot(p_cb, dob_c[g], preferred_element_type=F32)
                      + jnp.dot(p_nb, dob_n[g], preferred_element_type=F32))
            dkh = dkh + fold_to(g, dk_rep)
            dvh = dvh + fold_to(g, dv_rep)
        dv_ref[...] = dvh.astype(BF16)
        dq, dqg = _normrope_bwd(dq_ref[...], nqc, rsqc, qg, cosc, sinc, ecq, eeq)
        dk, dkg = _normrope_bwd(dkh, nkc, rskc, kg, cosc, sinc, eck, eek)
        dq_ref[...] = dq
        dk_ref[...] = dk.astype(BF16)

        @pl.when(n == 0)
        def _():
            vec_ref[...] = jnp.zeros_like(vec_ref)

        vec_ref[0:1, :] += dqg
        vec_ref[1:2, 0:kw] += dkg
        vec_ref[2:3, 0:LANES] += dsink

    def row(b, n):
        return b * nb + n

    def prev(b, n):
        return b * nb + jnp.maximum(n - 1, 0)

    def nxt(b, n):
        return b * nb + jnp.minimum(n + 1, nb - 1)

    def tiles(width, col, which):
        return pl.BlockSpec((blk, width), lambda b, n: (which(b, n), col))

    def table(which):
        return pl.BlockSpec((blk, LANES), lambda b, n: (which(0, n), 0))

    outs, landed = _call(
        body,
        grid=(n_seq, nb),
        in_specs=[
            tiles(D_MODEL, COL_Q // D_MODEL, row), tiles(D_MODEL, COL_Q // D_MODEL, nxt),
            tiles(kw, COL_K // kw, prev), tiles(kw, COL_K // kw, row),
            tiles(kw, COL_V // kw, prev), tiles(kw, COL_V // kw, row),
            tiles(D_MODEL, 0, row), tiles(D_MODEL, 0, nxt),
            tiles(D_MODEL, 0, row), tiles(D_MODEL, 0, nxt),
            tiles(LANES, 0, row), tiles(LANES, 0, nxt),
            table(row), table(row), table(prev), table(prev), table(nxt), table(nxt),
            pl.BlockSpec((1, D_MODEL), lambda b, n: (0, 0)),
            pl.BlockSpec((1, kw), lambda b, n: (0, 0)),
            pl.BlockSpec((1, LANES), lambda b, n: (0, 0)),
        ],
        out_specs=[tiles(D_MODEL, 0, row), tiles(kw, 0, row), tiles(kw, 0, row),
                   pl.BlockSpec((None, 8, D_MODEL), lambda b, n: (b, 0, 0))],
        out_shape=[jax.ShapeDtypeStruct((t, D_MODEL), F32), jax.ShapeDtypeStruct((t, kw), BF16),
                   jax.ShapeDtypeStruct((t, kw), BF16), jax.ShapeDtypeStruct((n_seq, 8, D_MODEL), F32)],
        args=(z, z, z, z, z, z, o, o, do, do, lse, lse, cos_t, sin_t, cos_t, sin_t, cos_t, sin_t,
              q_gain_t, k_gain_t, sinks_t), name="attn_bwd", semantics=("parallel", "arbitrary"), hosted=hosted)
    return (*outs, landed) if hosted is not None else outs


MERGE_COLS = 512


def _merge_fwd(z, ya, yb):
    t = ya.shape[0]
    tm, tc = min(512, t), MERGE_COLS

    def body(ga_ref, gb_ref, ya_ref, yb_ref, o_ref):
        o_ref[...] = (_sig(ga_ref[...]) * ya_ref[...] + _sig(gb_ref[...]) * yb_ref[...]).astype(BF16)

    tile = pl.BlockSpec((tm, tc), lambda i, j: (i, j))
    return pl.pallas_call(
        body,
        grid=(t // tm, D_MODEL // tc),
        in_specs=[pl.BlockSpec((tm, tc), lambda i, j: (i, COL_GA // tc + j)),
                  pl.BlockSpec((tm, tc), lambda i, j: (i, COL_GB // tc + j)), tile, tile],
        out_specs=tile,
        out_shape=jax.ShapeDtypeStruct((t, D_MODEL), BF16),
        compiler_params=_params("parallel", "parallel"),
        name="merge_fwd",
    )(z, z, ya, yb)


def _merge_bwd(z, ya, yb, dmerged):
    t = ya.shape[0]
    tm, tc = min(512, t), MERGE_COLS

    def body(ga_ref, gb_ref, ya_ref, yb_ref, dm_ref, dya_ref, dyb_ref, dga_ref, dgb_ref):
        dm = dm_ref[...]
        sa, sb = _sig(ga_ref[...]), _sig(gb_ref[...])
        dya_ref[...] = (dm * sa).astype(BF16)
        dyb_ref[...] = (dm * sb).astype(BF16)
        dga_ref[...] = (dm * ya_ref[...] * sa * (1.0 - sa)).astype(BF16)
        dgb_ref[...] = (dm * yb_ref[...] * sb * (1.0 - sb)).astype(BF16)

    tile = pl.BlockSpec((tm, tc), lambda i, j: (i, j))
    return pl.pallas_call(
        body,
        grid=(t // tm, D_MODEL // tc),
        in_specs=[pl.BlockSpec((tm, tc), lambda i, j: (i, COL_GA // tc + j)),
                  pl.BlockSpec((tm, tc), lambda i, j: (i, COL_GB // tc + j)), tile, tile, tile],
        out_specs=[tile] * 4,
        out_shape=[jax.ShapeDtypeStruct((t, D_MODEL), BF16)] * 4,
        compiler_params=_params("parallel", "parallel"),
        name="merge_bwd",
    )(z, z, ya, yb, dmerged)


def _loss_head(x2, e, gt, target):
    t, d = x2.shape
    tm = min(256, t)

    def body(x_ref, e_ref, gt_ref, tg_ref, loss_ref, dx_ref, dgt_ref, de_ref):
        ev = e_ref[...]
        sg = _sig(gt_ref[...])
        diff = x_ref[...] + ev * sg - tg_ref[...]
        dx = diff * (1.0 / d)
        dx_ref[...] = dx
        dgt_ref[...] = (dx * ev * sg * (1.0 - sg)).astype(BF16)
        de_ref[...] = (dx * sg).astype(BF16)

        @pl.when(pl.program_id(0) == 0)
        def _():
            loss_ref[...] = jnp.zeros_like(loss_ref)

        loss_ref[...] += jnp.sum(jnp.sum(diff * diff, axis=1, keepdims=True), axis=0, keepdims=True)

    tile = pl.BlockSpec((tm, d), lambda i: (i, 0))
    return pl.pallas_call(
        body,
        grid=(t // tm,),
        in_specs=[tile] * 4,
        out_specs=[pl.BlockSpec((1, LANES), lambda i: (0, 0)), tile, tile, tile],
        out_shape=[jax.ShapeDtypeStruct((1, LANES), F32), jax.ShapeDtypeStruct((t, d), F32),
                   jax.ShapeDtypeStruct((t, d), BF16), jax.ShapeDtypeStruct((t, d), BF16)],
        compiler_params=_params("arbitrary"),
        name="loss_head",
    )(x2, e, gt, target)


def _rope_tables(seq):
    inv = ROPE_THETA ** (-jnp.arange(0, HEAD_DIM, 2, dtype=F32) / HEAD_DIM)
    ang = jnp.arange(seq, dtype=F32)[:, None] * inv[None, :]
    cos, sin = jnp.cos(ang), jnp.sin(ang)
    return jnp.tile(jnp.concatenate([cos, cos], axis=1), (1, 2)), jnp.tile(jnp.concatenate([-sin, sin], axis=1), (1, 2))


def _block_diag_tiles(w):
    per = RNN_TILE // RNN_BLOCK_W
    w4 = w.reshape(D_MODEL // RNN_TILE, per, RNN_BLOCK_W, RNN_BLOCK_W)
    eye = jnp.eye(per, dtype=w.dtype)
    dense = jnp.einsum("tpij,pq->tpiqj", w4, eye)
    return dense.reshape(D_MODEL // RNN_TILE, RNN_TILE, RNN_TILE).astype(BF16)


def _block_diag_extract(dense):
    per = RNN_TILE // RNN_BLOCK_W
    d5 = dense.reshape(D_MODEL // RNN_TILE, per, RNN_BLOCK_W, per, RNN_BLOCK_W)
    blocks = jnp.stack([d5[:, p, :, p, :] for p in range(per)], axis=1)
    return blocks.reshape(D_MODEL // RNN_BLOCK_W, RNN_BLOCK_W, RNN_BLOCK_W)


def _local_step(x, p, target, w, *, n_seq, seq, comm=None):
    w = dict(w)

    def run(tag, fn, *args, **kwargs):
        hosted = comm.host(tag) if comm is not None else None
        if hosted is None:
            return fn(*args, **kwargs)
        *outs, landed = fn(*args, hosted=hosted, **kwargs)
        comm.landed(tag, landed, w)
        return outs[0] if len(outs) == 1 else outs

    def ready(batch, grads, extra=None):
        if comm is not None:
            comm.ready(batch, grads, extra)

    cos_t, sin_t = _rope_tables(seq)
    q_gain_t = jnp.tile(w["q_gain"], (1, N_Q_HEADS))
    k_gain_t = jnp.tile(w["k_gain"], (1, N_KV_HEADS))
    sinks_t = jnp.pad(w["sinks"], ((0, 0), (0, LANES - N_Q_HEADS)))
    wrg_bd, wig_bd = _block_diag_tiles(w["w_rg"]), _block_diag_tiles(w["w_ig"])
    dims = dict(n_seq=n_seq, seq=seq)

    h = _rmsnorm_fwd(x, w["g_mix"], name="norm_mix")
    z = run("mm_in", _matmul, h, w["w_in"], mode="nn", tm=1024, tn=IN_TOTAL // 4, out_dtypes=[F32], name="mm_in")
    xc, hr, ya_in = run("rnn_fwd", _rnn_fwd, z, w["conv_w"], w["conv_b"], wrg_bd, w["b_rg"], wig_bd, w["b_ig"],
                        w["lru_lambda"], **dims)
    o, lse = run("attn_fwd", _attn_fwd, z, cos_t, sin_t, q_gain_t, k_gain_t, sinks_t, **dims)
    ya = run("mm_rnn_proj", _matmul, ya_in, w["w_rnn_proj"], mode="nn", tm=1024, tn=1024, out_dtypes=[F32],
             name="mm_rnn_proj")
    yb = _matmul(o, w["w_attn_proj"], mode="nn", tm=1024, tn=1024, out_dtypes=[F32], name="mm_attn_proj")
    merged = _merge_fwd(z, ya, yb)
    x1 = _matmul(merged, w["w_out"], mode="nn", tm=1024, tn=1024, out_dtypes=[F32], name="mm_out",
                 epilogue=lambda acc, res: (res + acc,), extras=(x,))
    hm = _rmsnorm_fwd(x1, w["g_mlp"], name="norm_mlp")
    act = _matmul(hm, w["w_up"], mode="nn", tm=1024, tn=1024, out_dtypes=[BF16], name="mm_up",
                  epilogue=lambda acc: (jnp.square(jnp.maximum(acc, 0.0)),))
    x2 = _matmul(act, w["w_down"], mode="nn", tm=512, tn=1024, out_dtypes=[F32], name="mm_down",
                 epilogue=lambda acc, res: (res + acc,), extras=(x1,))
    hp = _rmsnorm_fwd(x2, w["g_ple"], name="norm_ple")
    gt = _matmul(hp, w["w_ple_gate"], mode="nn", tm=1024, tn=1024, out_dtypes=[F32], name="mm_ple_gate")
    p_bf = p.astype(BF16)
    e = _matmul(p_bf, w["w_ple_proj"], mode="nn", tm=1024, tn=1024, out_dtypes=[F32], name="mm_ple_proj")
    loss_row, dx3, dgt, de = _loss_head(x2, e, gt, target)

    g = {}
    g["w_ple_proj"] = _matmul_tn(p_bf, de, tk=PLE_DIM, tn=1024, tt=1024, name="mm_d_ple_proj",
                                 slot_cols=D_MODEL // N_DEV)
    g["w_ple_gate"] = _matmul_tn(hp, dgt, tk=1024, tn=1024, tt=512, name="mm_d_ple_gate")
    dhp = _matmul(dgt, w["w_ple_gate"], mode="nt", tm=1024, tn=1024, out_dtypes=[F32], name="mm_dhp")
    dx2, dx2_bf, g["g_ple"] = _rmsnorm_bwd(x2, w["g_ple"], dhp, dx3, name="norm_ple_bwd", want_bf16=True)
    g["w_down"] = _matmul_tn(act, dx2_bf, tk=1024, tn=1024, tt=512, name="mm_d_down")
    du = _matmul(dx2_bf, w["w_down"], mode="nt", tm=1024, tn=1024, out_dtypes=[BF16], name="mm_dact",
                 epilogue=lambda acc, a: (acc * (2.0 * jnp.sqrt(a.astype(F32))),), extras=(act,))
    g["w_up"] = _matmul_tn(hm, du, tk=1024, tn=1024, tt=512, name="mm_d_up", slot_cols=D_FF // N_DEV)
    ready(1, g)
    dhm = run("mm_dhm", _matmul, du, w["w_up"], mode="nt", tm=512, tn=1024, out_dtypes=[F32], name="mm_dhm")
    dx1, dx1_bf, g["g_mlp"] = _rmsnorm_bwd(x1, w["g_mlp"], dhm, dx2, name="norm_mlp_bwd", want_bf16=True)
    g["w_out"] = _matmul_tn(merged, dx1_bf, tk=1024, tn=1024, tt=512, name="mm_d_out")
    dmerged = _matmul(dx1_bf, w["w_out"], mode="nt", tm=1024, tn=1024, out_dtypes=[F32], name="mm_dmerged")
    dya, dyb, dga, dgb = _merge_bwd(z, ya, yb, dmerged)
    g["w_rnn_proj"] = _matmul_tn(ya_in, dya, tk=1024, tn=1024, tt=512, name="mm_d_rnn_proj")
    g["w_attn_proj"] = _matmul_tn(o, dyb, tk=1024, tn=1024, tt=512, name="mm_d_attn_proj")
    ready(2, g)
    dya_in = run("mm_dya_in", _matmul, dya, w["w_rnn_proj"], mode="nt", tm=1024, tn=1024, out_dtypes=[F32],
                 name="mm_dya_in")
    do = _matmul(dyb, w["w_attn_proj"], mode="nt", tm=1024, tn=1024, out_dtypes=[BF16], name="mm_do")
    dx_rnn, dg_rnn, dwrg_dense, dwig_dense, rnn_vec = run(
        "rnn_bwd", _rnn_bwd, dya_in, z, xc, hr, w["conv_w"], wrg_bd, w["b_rg"], wig_bd, w["b_ig"], w["lru_lambda"],
        **dims)
    dq, dk, dv, attn_vec = run("attn_bwd", _attn_bwd, z, o, lse, do, cos_t, sin_t, q_gain_t, k_gain_t, sinks_t,
                               **dims)
    dz = jnp.concatenate([dx_rnn, dg_rnn, dq.astype(BF16), dk, dv, dga, dgb], axis=1)
    g["w_in"] = _matmul_tn(h, dz, tk=1024, tn=IN_TOTAL // 4, tt=512, name="mm_d_in")
    g["w_rg"] = _block_diag_extract(dwrg_dense)
    g["w_ig"] = _block_diag_extract(dwig_dense)
    g["b_rg"], g["b_ig"], g["lru_lambda"], g["conv_b"] = (rnn_vec[i:i + 1] for i in range(4))
    g["conv_w"] = rnn_vec[4:8]
    attn_vec = attn_vec[0] if n_seq == 1 else functools.reduce(jnp.add, [attn_vec[b] for b in range(n_seq)])
    g["q_gain"] = attn_vec[0].reshape(N_Q_HEADS, HEAD_DIM).sum(axis=0)[None, :]
    g["k_gain"] = attn_vec[1, :KV_W].reshape(N_KV_HEADS, HEAD_DIM).sum(axis=0)[None, :]
    g["sinks"] = attn_vec[2:3, :N_Q_HEADS]
    ready(3, g, {LOSS_ROW: loss_row})
    dh = run("mm_dh", _matmul, dz, w["w_in"], mode="nt", tm=512, tn=1024, out_dtypes=[F32], name="mm_dh")
    grad_x, g["g_mix"] = run("norm_mix_bwd", _rmsnorm_bwd, x, w["g_mix"], dh, dx1, name="norm_mix_bwd",
                             want_bf16=False)
    return loss_row[0, 0], grad_x, g


MESH_ID = pl.DeviceIdType.MESH


def _coords(index):
    return (index >> 2) & 1, (index >> 1) & 1, index & 1


def _exchange(srcs, kinds, *, name):
    n = len(srcs)
    n_peer = N_DEV - 1

    def body(*refs):
        src, dst = refs[:n], refs[n:2 * n]
        send_sems, recv_sems, local_sems = refs[2 * n:]
        me = 4 * lax.axis_index("x") + 2 * lax.axis_index("y") + lax.axis_index("c")

        def remote(i, d):
            peer = (me + d) & (N_DEV - 1)
            piece = src[i] if kinds[i] == "gather" else src[i].at[peer]
            return pltpu.make_async_remote_copy(
                src_ref=piece, dst_ref=dst[i].at[me], send_sem=send_sems.at[i * n_peer + d - 1],
                recv_sem=recv_sems.at[i * n_peer + d - 1], device_id=_coords(peer), device_id_type=MESH_ID)

        def arrival(i, d):
            sender = (me - d) & (N_DEV - 1)
            piece = src[i] if kinds[i] == "gather" else src[i].at[sender]
            return pltpu.make_async_remote_copy(
                src_ref=piece, dst_ref=dst[i].at[sender], send_sem=send_sems.at[i * n_peer + d - 1],
                recv_sem=recv_sems.at[i * n_peer + d - 1], device_id=_coords(sender), device_id_type=MESH_ID)

        own = []
        for i in range(n):
            piece = src[i] if kinds[i] == "gather" else src[i].at[me]
            own.append(pltpu.make_async_copy(piece, dst[i].at[me], local_sems.at[i]))
            own[-1].start()
        sent = [remote(i, d) for d in range(1, N_DEV) for i in range(n)]
        for cp in sent:
            cp.start()
        for d in range(1, N_DEV):
            for i in range(n):
                arrival(i, d).wait_recv()
        for cp in sent:
            cp.wait_send()
        for cp in own:
            cp.wait()

    def out_of(s, kind):
        shape = s.shape if kind == "scatter" else (N_DEV,) + s.shape
        return jax.ShapeDtypeStruct(shape, s.dtype)

    any_spec = pl.BlockSpec(memory_space=pl.ANY)
    return pl.pallas_call(
        body,
        in_specs=[any_spec] * n,
        out_specs=[any_spec] * n,
        out_shape=[out_of(s, k) for s, k in zip(srcs, kinds)],
        scratch_shapes=[pltpu.SemaphoreType.DMA((n * n_peer,)), pltpu.SemaphoreType.DMA((n * n_peer,)),
                        pltpu.SemaphoreType.DMA((n,))],
        compiler_params=pltpu.CompilerParams(has_side_effects=True),
        name=name,
    )(*srcs)


def _remote(src, dst, send_sem, recv_sem, to):
    return pltpu.make_async_remote_copy(src_ref=src, dst_ref=dst, send_sem=send_sem, recv_sem=recv_sem,
                                        device_id=to, device_id_type=MESH_ID)


def _gather_two_level(shards, *, name):
    n = len(shards)
    per = N_DEV - 1

    def body(*refs):
        src, dst = refs[:n], refs[n:2 * n]
        send_sems, recv_sems, local_sems = refs[2 * n:]
        x, y, c = lax.axis_index("x"), lax.axis_index("y"), lax.axis_index("c")
        me, sibling = (x, y, c), (x, y, 1 - c)
        chips = [(1 - x, y), (x, 1 - y), (1 - x, 1 - y)]

        def slot(pos):
            return 4 * pos[0] + 2 * pos[1] + pos[2]

        def copy(i, k, block, to, from_shard=False):
            source = src[i] if from_shard else dst[i].at[slot(block)]
            return _remote(source, dst[i].at[slot(block)], send_sems.at[i * per + k], recv_sems.at[i * per + k], to)

        mine = [pltpu.make_async_copy(src[i], dst[i].at[slot(me)], local_sems.at[i]) for i in range(n)]
        for cp in mine:
            cp.start()
        first = []
        for i in range(n):
            first.append(copy(i, 0, me, sibling, from_shard=True))
            first += [copy(i, 1 + j, me, (*chip, c), from_shard=True) for j, chip in enumerate(chips)]
        for cp in first:
            cp.start()
        passed = []
        for i in range(n):
            for j, chip in enumerate(chips):
                copy(i, 1 + j, (*chip, c), me).wait_recv()
                passed.append(copy(i, 4 + j, (*chip, c), sibling))
                passed[-1].start()
        for i in range(n):
            copy(i, 0, sibling, me).wait_recv()
            for j, chip in enumerate(chips):
                copy(i, 4 + j, (*chip, 1 - c), me).wait_recv()
        for cp in first + passed:
            cp.wait_send()
        for cp in mine:
            cp.wait()

    any_spec = pl.BlockSpec(memory_space=pl.ANY)
    return pl.pallas_call(
        body,
        in_specs=[any_spec] * n,
        out_specs=[any_spec] * n,
        out_shape=[jax.ShapeDtypeStruct((N_DEV,) + s.shape, s.dtype) for s in shards],
        scratch_shapes=[pltpu.SemaphoreType.DMA((n * per,)), pltpu.SemaphoreType.DMA((n * per,)),
                        pltpu.SemaphoreType.DMA((n,))],
        name=name,
    )(*shards)


CHIPS = N_DEV // 2


def _other_chips(x, y):
    return [(x, 1 - y), (1 - x, y), (1 - x, 1 - y)]


def _hosted_gather_first(shards):
    n = len(shards)
    per = CHIPS

    def plan(src, dst, send_sems, recv_sems, local_sems, first_sem):
        x, y, c = lax.axis_index("x"), lax.axis_index("y"), lax.axis_index("c")
        peers = [(x, y, 1 - c)] + [(*chip, c) for chip in _other_chips(x, y)]
        copies = []
        for i in range(n):
            own = pltpu.make_async_copy(src[i], dst[i].at[4 * x + 2 * y + c], local_sems.at[first_sem + i])
            copies.append(_Xfer(own.start, own.wait))
        for j, peer in enumerate(peers):
            for i in range(n):
                k = first_sem + i * per + j
                out = _remote(src[i], dst[i].at[4 * x + 2 * y + c], send_sems.at[k], recv_sems.at[k], peer)
                arrival = _remote(src[i], dst[i].at[4 * peer[0] + 2 * peer[1] + peer[2]], send_sems.at[k],
                                  recv_sems.at[k], peer)

                def wait(out=out, arrival=arrival):
                    arrival.wait_recv()
                    out.wait_send()

                copies.append(_Xfer(out.start, wait))
        return copies

    out_shape = tuple(jax.ShapeDtypeStruct((N_DEV,) + s.shape, s.dtype) for s in shards)
    return _Hosted(tuple(shards), out_shape, n * per, plan)


def _hosted_gather_second(landed):
    n = len(landed)
    per = CHIPS - 1

    def plan(src, dst, send_sems, recv_sems, local_sems, first_sem):
        x, y, c = lax.axis_index("x"), lax.axis_index("y"), lax.axis_index("c")
        copies = []
        for j, chip in enumerate(_other_chips(x, y)):
            mine, theirs = 4 * chip[0] + 2 * chip[1] + c, 4 * chip[0] + 2 * chip[1] + 1 - c
            for i in range(n):
                k = first_sem + i * per + j
                out = _remote(src[i].at[mine], dst[i].at[mine], send_sems.at[k], recv_sems.at[k], (x, y, 1 - c))
                arrival = _remote(src[i].at[theirs], dst[i].at[theirs], send_sems.at[k], recv_sems.at[k],
                                  (x, y, 1 - c))

                def wait(out=out, arrival=arrival):
                    arrival.wait_recv()
                    out.wait_send()

                copies.append(_Xfer(out.start, wait))
        return copies

    out_shape = tuple(jax.ShapeDtypeStruct(a.shape, a.dtype) for a in landed)
    return _Hosted(tuple(landed), out_shape, n * per, plan, tuple((i, i) for i in range(n)))


def _hosted_sibling_swap(arrays, sliced):
    n_sems = sum(CHIPS if s else 1 for s in sliced)

    def plan(src, dst, send_sems, recv_sems, local_sems, first_sem):
        x, y, c = lax.axis_index("x"), lax.axis_index("y"), lax.axis_index("c")
        sibling = (x, y, 1 - c)
        copies, k = [], first_sem
        for i, is_sliced in enumerate(sliced):
            pieces = [(src[i].at[2 * s + 1 - c], dst[i].at[s]) for s in range(CHIPS)] if is_sliced else [(src[i], dst[i])]
            for source, target in pieces:
                cp = _remote(source, target, send_sems.at[k], recv_sems.at[k], sibling)
                copies.append(_Xfer(cp.start, cp.wait))
                k += 1
        return copies

    out_shape = tuple(jax.ShapeDtypeStruct((CHIPS,) + a.shape[1:] if s else a.shape, a.dtype)
                      for a, s in zip(arrays, sliced))
    return _Hosted(tuple(arrays), out_shape, n_sems, plan)


def _hosted_chip_exchange(arrays, sliced):
    n = len(arrays)
    per = CHIPS - 1

    def plan(src, dst, send_sems, recv_sems, local_sems, first_sem):
        x, y, c = lax.axis_index("x"), lax.axis_index("y"), lax.axis_index("c")
        chip = 2 * x + y
        copies = []
        for i in range(n):
            own = pltpu.make_async_copy(src[i].at[chip] if sliced[i] else src[i], dst[i].at[chip],
                                        local_sems.at[first_sem + i])
            copies.append(_Xfer(own.start, own.wait))
        for d in range(1, CHIPS):
            other = chip ^ d
            to = ((other >> 1) & 1, other & 1, c)
            for i in range(n):
                k = first_sem + i * per + d - 1
                source = src[i].at[other] if sliced[i] else src[i]
                out = _remote(source, dst[i].at[chip], send_sems.at[k], recv_sems.at[k], to)
                arrival = _remote(source, dst[i].at[other], send_sems.at[k], recv_sems.at[k], to)

                def wait(out=out, arrival=arrival):
                    arrival.wait_recv()
                    out.wait_send()

                copies.append(_Xfer(out.start, wait))
        return copies

    out_shape = tuple(jax.ShapeDtypeStruct(a.shape if s else (CHIPS,) + a.shape, a.dtype)
                      for a, s in zip(arrays, sliced))
    return _Hosted(tuple(arrays), out_shape, n * per, plan)


def _add_sibling(parts, received, core, *, name):
    _, r, cols = parts.shape
    tr = min(256, r)

    def body(core_ref, a_ref, b_ref, o_ref):
        o_ref[...] = (a_ref[...] + b_ref[...]).astype(BF16)

    grid_spec = pltpu.PrefetchScalarGridSpec(
        num_scalar_prefetch=1,
        grid=(CHIPS, r // tr),
        in_specs=[pl.BlockSpec((None, tr, cols), lambda k, i, core_ref: (2 * k + core_ref[0], i, 0)),
                  pl.BlockSpec((None, tr, cols), lambda k, i, core_ref: (k, i, 0))],
        out_specs=pl.BlockSpec((None, tr, cols), lambda k, i, core_ref: (k, i, 0)),
    )
    return pl.pallas_call(body, grid_spec=grid_spec, out_shape=jax.ShapeDtypeStruct((CHIPS, r, cols), BF16),
                          compiler_params=_params("parallel", "parallel"), name=name)(core, parts, received)


def _add_whole(a, b, *, name):
    def body(a_ref, b_ref, o_ref):
        o_ref[...] = a_ref[...] + b_ref[...]

    return pl.pallas_call(body, out_shape=jax.ShapeDtypeStruct(a.shape, F32), name=name)(a, b)


def _adamw(parts, w, m, v, *, name):
    r, c = w.shape
    n_parts = parts.shape[0]
    tr = min(256, r)
    c1 = 1.0 - ADAM_B1 ** ADAM_STEP
    c2 = 1.0 - ADAM_B2 ** ADAM_STEP

    def body(p_ref, w_ref, m_ref, v_ref, g_ref, d_ref, nm_ref, nv_ref):
        g = p_ref[0].astype(F32)
        for s in range(1, n_parts):
            g = g + p_ref[s].astype(F32)
        nm = ADAM_B1 * m_ref[...] + (1.0 - ADAM_B1) * g
        nv = ADAM_B2 * v_ref[...] + (1.0 - ADAM_B2) * (g * g)
        g_ref[...] = g
        nm_ref[...] = nm
        nv_ref[...] = nv
        d_ref[...] = -ADAM_LR * ((nm / c1) / (jnp.sqrt(nv / c2) + ADAM_EPS) + ADAM_WD * w_ref[...])

    tile = pl.BlockSpec((tr, c), lambda i: (i, 0))
    return pl.pallas_call(
        body,
        grid=(r // tr,),
        in_specs=[pl.BlockSpec((n_parts, tr, c), lambda i: (0, i, 0)), tile, tile, tile],
        out_specs=[tile] * 4,
        out_shape=[jax.ShapeDtypeStruct((r, c), F32)] * 4,
        compiler_params=_params("parallel"),
        name=name,
    )(parts, w, m, v)


BIG = ("w_in", "w_rnn_proj", "w_attn_proj", "w_out", "w_up", "w_down", "w_ple_gate", "w_ple_proj")
LOSS_ROW = "loss"
SMALL = (("conv_b", 1), ("b_rg", 1), ("b_ig", 1), ("lru_lambda", 1), ("g_mlp", 1), ("g_ple", 1),
         ("q_gain", 1), ("k_gain", 1), ("sinks", 1), (LOSS_ROW, 1), ("w_rg", 64), ("w_ig", 64))
SMALL_ROWS = 144
ROW_SHARDED = ("w_rnn_proj", "w_attn_proj", "w_out", "w_down", "w_ple_gate")
COL_SHARDED = ("w_in", "w_up", "w_ple_proj")
BATCHES = {1: ("w_ple_proj", "w_ple_gate", "w_down", "w_up"), 2: ("w_out", "w_rnn_proj", "w_attn_proj"),
           3: ("w_in", "conv_w")}


def _pack_small(vals):
    rows = []
    for nm, nrow in SMALL:
        flat = vals[nm].reshape(-1).astype(F32)
        rows.append(jnp.pad(flat, (0, nrow * D_MODEL - flat.shape[0])).reshape(nrow, D_MODEL))
    used = sum(nrow for _, nrow in SMALL)
    rows.append(jnp.zeros((SMALL_ROWS - used, D_MODEL), F32))
    return jnp.concatenate(rows, axis=0)


def _unpack_small(packed, shapes):
    out, at = {}, 0
    for nm, nrow in SMALL:
        size = 1
        for s in shapes[nm]:
            size *= s
        out[nm] = packed[at:at + nrow].reshape(-1)[:size].reshape(shapes[nm])
        at += nrow
    return out


def _full_weight(name, landed):
    if name in COL_SHARDED:
        return landed.transpose(1, 0, 2).reshape(landed.shape[1], N_DEV * landed.shape[2])
    return landed.reshape(N_DEV * landed.shape[1], landed.shape[2])


def _owner_slots(name, grad):
    if name == "w_in":
        return grad.reshape(D_MODEL, N_DEV, IN_TOTAL // N_DEV).transpose(1, 0, 2)
    if name == "conv_w":
        return grad.reshape(CONV_W, N_DEV, D_MODEL // N_DEV).transpose(1, 0, 2)
    if name in COL_SHARDED:
        return grad
    return grad.reshape(N_DEV, grad.shape[0] // N_DEV, grad.shape[1])


class _StepExchanges:
    FIRST, SECOND = "first", "second"
    EARLY, MID, LATE = ("w_rnn_proj", "w_attn_proj", "w_out"), ("w_up",), ("w_down", "w_ple_gate", "w_ple_proj")
    GATHERS = {"mm_in": ((FIRST, EARLY),), "rnn_fwd": ((SECOND, EARLY), (FIRST, MID)),
               "attn_fwd": ((SECOND, MID), (FIRST, LATE)), "mm_rnn_proj": ((SECOND, LATE),)}
    SWAPS = {"mm_dhm": 1, "mm_dya_in": 2}
    CHIP_EXCHANGES = {"rnn_bwd": 1, "attn_bwd": 2, "mm_dh": 3}

    def __init__(self, shards, core):
        self.shards = shards
        self.core = core
        self.parts, self.swapped, self.summed, self.half_gathered = {}, {}, {}, {}

    def ready(self, batch, grads, extra=None):
        arrays = [_owner_slots(nm, grads[nm]) for nm in BATCHES[batch]]
        sliced = [True] * len(arrays)
        if batch == 3:
            arrays.append(_pack_small({**grads, **extra}))
            sliced.append(False)
        self.parts[batch] = (arrays, sliced)
        if batch not in self.SWAPS.values():
            _, self.swapped[batch] = _call(
                lambda: None, grid=(1,), in_specs=[], out_specs=[], out_shape=[], args=(), name="swap_last",
                semantics=("arbitrary",), hosted=_hosted_sibling_swap(arrays, sliced))

    def host(self, tag):
        if tag in self.GATHERS:
            return _merge_hosted([
                _hosted_gather_first([self.shards[nm] for nm in group]) if half == self.FIRST
                else _hosted_gather_second([self.half_gathered[nm] for nm in group])
                for half, group in self.GATHERS[tag]])
        if tag in self.SWAPS:
            return _hosted_sibling_swap(*self.parts[self.SWAPS[tag]])
        if tag in self.CHIP_EXCHANGES:
            batch = self.CHIP_EXCHANGES[tag]
            arrays, sliced = self.parts[batch]
            labels = list(BATCHES[batch]) + ["small"]
            sums = [_add_sibling(a, r, self.core, name="add_" + lb) if s else _add_whole(a, r, name="add_" + lb)
                    for a, r, s, lb in zip(arrays, self.swapped[batch], sliced, labels)]
            return _hosted_chip_exchange(sums, sliced)
        return None

    def landed(self, tag, landed, weights):
        if tag in self.GATHERS:
            names = [(half, nm) for half, group in self.GATHERS[tag] for nm in group]
            for (half, nm), buf in zip(names, landed):
                if half == self.FIRST:
                    self.half_gathered[nm] = buf
                else:
                    weights[nm] = _full_weight(nm, buf)
        elif tag in self.SWAPS:
            self.swapped[self.SWAPS[tag]] = landed
        else:
            self.summed[self.CHIP_EXCHANGES[tag]] = landed


def kernel(x, p, g_mix, w_in, conv_w, conv_b, w_rg, b_rg, w_ig, b_ig, lru_lambda, w_rnn_proj, q_gain, k_gain, sinks, w_attn_proj, w_out, g_mlp, w_up, w_down, g_ple, w_ple_gate, w_ple_proj, loss_target, m_g_mix, m_w_in, m_conv_w, m_conv_b, m_w_rg, m_b_rg, m_w_ig, m_b_ig, m_lru_lambda, m_w_rnn_proj, m_q_gain, m_k_gain, m_sinks, m_w_attn_proj, m_w_out, m_g_mlp, m_w_up, m_w_down, m_g_ple, m_w_ple_gate, m_w_ple_proj, v_g_mix, v_w_in, v_conv_w, v_conv_b, v_w_rg, v_b_rg, v_w_ig, v_b_ig, v_lru_lambda, v_w_rnn_proj, v_q_gain, v_k_gain, v_sinks, v_w_attn_proj, v_w_out, v_g_mlp, v_w_up, v_w_down, v_g_ple, v_w_ple_gate, v_w_ple_proj):
    names = ("g_mix", "w_in", "conv_w", "conv_b", "w_rg", "b_rg", "w_ig", "b_ig", "lru_lambda", "w_rnn_proj",
             "q_gain", "k_gain", "sinks", "w_attn_proj", "w_out", "g_mlp", "w_up", "w_down", "g_ple",
             "w_ple_gate", "w_ple_proj")
    wts = dict(zip(names, (g_mix, w_in, conv_w, conv_b, w_rg, b_rg, w_ig, b_ig, lru_lambda, w_rnn_proj, q_gain,
                           k_gain, sinks, w_attn_proj, w_out, g_mlp, w_up, w_down, g_ple, w_ple_gate, w_ple_proj)))
    mom1 = dict(zip(names, (m_g_mix, m_w_in, m_conv_w, m_conv_b, m_w_rg, m_b_rg, m_w_ig, m_b_ig, m_lru_lambda,
                            m_w_rnn_proj, m_q_gain, m_k_gain, m_sinks, m_w_attn_proj, m_w_out, m_g_mlp, m_w_up,
                            m_w_down, m_g_ple, m_w_ple_gate, m_w_ple_proj)))
    mom2 = dict(zip(names, (v_g_mix, v_w_in, v_conv_w, v_conv_b, v_w_rg, v_b_rg, v_w_ig, v_b_ig, v_lru_lambda,
                            v_w_rnn_proj, v_q_gain, v_k_gain, v_sinks, v_w_attn_proj, v_w_out, v_g_mlp, v_w_up,
                            v_w_down, v_g_ple, v_w_ple_gate, v_w_ple_proj)))
    n_seq, seq, _ = x.shape
    core = lax.axis_index("c").astype(jnp.int32).reshape(1)

    shards = {nm: wts[nm][0].astype(BF16) for nm in BIG}
    w_in_all, conv_all = _gather_two_level([shards["w_in"], conv_w[0]], name="gather_w_in")
    w = {nm: wts[nm] for nm in names if nm not in BIG}
    w["w_rg"], w["w_ig"] = w_rg[0], w_ig[0]
    w["conv_w"] = conv_all.transpose(1, 0, 2).reshape(CONV_W, D_MODEL)
    w["w_in"] = _full_weight("w_in", w_in_all)
    comm = _StepExchanges(shards, core)
    loss_sum, grad_x, g = _local_step(
        x.reshape(n_seq * seq, D_MODEL), p.reshape(n_seq * seq, PLE_DIM), loss_target.reshape(n_seq * seq, D_MODEL),
        w, n_seq=n_seq, seq=seq, comm=comm)
    del loss_sum

    res = {}
    for batch, batch_names in BATCHES.items():
        for nm, summed in zip(batch_names, comm.summed[batch]):
            res[nm] = _adamw(summed, wts[nm][0], mom1[nm][0], mom2[nm][0], name="adamw_" + nm)
    g_mix_parts, = _exchange([g["g_mix"]], ["gather"], name="gather_g_mix")
    res["g_mix"] = [r[0] for r in _adamw(g_mix_parts, g_mix, m_g_mix, v_g_mix, name="adamw_g_mix")]
    small_names = [nm for nm, _ in SMALL if nm != LOSS_ROW]
    full_small = {}
    for src, key in ((wts, "w"), (mom1, "m"), (mom2, "v")):
        vals = {nm: src[nm][0] for nm in small_names}
        vals[LOSS_ROW] = jnp.zeros((1,), F32)
        full_small[key] = _pack_small(vals)
    small_res = _adamw(comm.summed[3][-1], full_small["w"], full_small["m"], full_small["v"], name="adamw_small")
    shapes = {nm: wts[nm].shape[1:] for nm in small_names}
    shapes[LOSS_ROW] = (1,)
    small_out = [_unpack_small(r, shapes) for r in small_res]
    for nm in small_names:
        res[nm] = [so[nm] for so in small_out]
    loss = small_out[0][LOSS_ROW][0] * (0.5 / D_MODEL)

    outs = [loss, grad_x.reshape(n_seq, seq, D_MODEL)]
    for k in range(4):
        outs.extend(res[nm][k][None] for nm in names)
    return tuple(outs)
```

```python
import functools
from typing import Callable, NamedTuple

import jax
import jax.numpy as jnp
from jax import lax
from jax.experimental import pallas as pl
from jax.experimental.pallas import tpu as pltpu

F32 = jnp.float32
BF16 = jnp.bfloat16

N_DEV = 8
D_MODEL = 1024
RNN_BLOCK_W = 64
CONV_W = 4
LRU_C = 8.0
HEAD_DIM = 64
N_Q_HEADS = 16
N_KV_HEADS = 4
KV_W = N_KV_HEADS * HEAD_DIM
WINDOW = 128
ROPE_THETA = 10000.0
D_FF = 4096
PLE_DIM = 256
NORM_EPS = 1e-6
IN_TOTAL = 5632
COL_RNN_END, COL_ATTN_END = 2048, 3584
ATTN_W = COL_ATTN_END - COL_RNN_END
ATTN_K_AT, ATTN_V_AT = 1024, 1280

ADAM_LR = 0.001
ADAM_B1 = 0.9
ADAM_B2 = 0.999
ADAM_EPS = 1e-08
ADAM_WD = 0.01
ADAM_STEP = 10

LANES = 128
SUBLANES = 8
RNN_TILE = 256
VMEM_LIMIT = 48 * 1024 * 1024
NEG_BIG = -1e30


def _params(*sem):
    return pltpu.CompilerParams(dimension_semantics=sem if sem else None, vmem_limit_bytes=VMEM_LIMIT)


def _sig(x):
    return 0.5 * jnp.tanh(0.5 * x) + 0.5


def _dot_nt(a, b):
    return lax.dot_general(a, b, (((1,), (1,)), ((), ())), preferred_element_type=F32)


def _dot_tn(a, b):
    return lax.dot_general(a, b, (((0,), (0,)), ((), ())), preferred_element_type=F32)


class _Xfer:
    def __init__(self, start, wait):
        self.start, self.wait = start, wait


class _Hosted(NamedTuple):
    srcs: tuple
    out_shape: tuple
    n_sems: int
    plan: Callable
    aliases: tuple = ()


def _merge_hosted(parts):
    parts = [p for p in parts if p is not None]
    if len(parts) <= 1:
        return parts[0] if parts else None
    src_at, dst_at, sem_at, aliases = [0], [0], [0], []
    for p in parts:
        aliases += [(i + src_at[-1], j + dst_at[-1]) for i, j in p.aliases]
        src_at.append(src_at[-1] + len(p.srcs))
        dst_at.append(dst_at[-1] + len(p.out_shape))
        sem_at.append(sem_at[-1] + p.n_sems)

    def plan(src, dst, send_sems, recv_sems, local_sems, first_sem):
        copies = []
        for k, p in enumerate(parts):
            copies += p.plan(src[src_at[k]:src_at[k + 1]], dst[dst_at[k]:dst_at[k + 1]], send_sems, recv_sems,
                             local_sems, first_sem + sem_at[k])
        return copies

    return _Hosted(tuple(a for p in parts for a in p.srcs), tuple(s for p in parts for s in p.out_shape),
                   sem_at[-1], plan, tuple(aliases))


def _call(body, *, grid, in_specs, out_specs, out_shape, args, name, semantics, scratch_shapes=(), hosted=None):
    if hosted is None:
        outs = pl.pallas_call(body, grid=grid, in_specs=list(in_specs), out_specs=list(out_specs),
                              out_shape=list(out_shape), scratch_shapes=list(scratch_shapes),
                              compiler_params=_params(*semantics), name=name)(*args)
        return list(outs), []
    counts = (len(in_specs), len(hosted.srcs), len(out_specs), len(hosted.out_shape), len(scratch_shapes), 3)

    def wrapped(*refs):
        at, groups = 0, []
        for count in counts:
            groups.append(refs[at:at + count])
            at += count
        ins, srcs, outs, dsts, scratch, sems = groups
        copies = hosted.plan(srcs, dsts, *sems, 0)
        ids = [pl.program_id(axis) for axis in range(len(grid))]
        first = functools.reduce(jnp.logical_and, [i == 0 for i in ids])
        last = functools.reduce(jnp.logical_and, [i == g - 1 for i, g in zip(ids, grid)])

        @pl.when(first)
        def _():
            for cp in copies:
                cp.start()

        body(*ins, *outs, *scratch)

        @pl.when(last)
        def _():
            for cp in copies:
                cp.wait()

    any_spec = pl.BlockSpec(memory_space=pl.ANY)
    sems = [pltpu.SemaphoreType.DMA((hosted.n_sems,))] * 3
    outs = pl.pallas_call(
        wrapped, grid=grid, in_specs=list(in_specs) + [any_spec] * counts[1],
        out_specs=list(out_specs) + [any_spec] * counts[3], out_shape=list(out_shape) + list(hosted.out_shape),
        scratch_shapes=list(scratch_shapes) + sems, compiler_params=_params(*["arbitrary"] * len(grid)),
        input_output_aliases={counts[0] + i: counts[2] + j for i, j in hosted.aliases},
        name=name)(*args, *hosted.srcs)
    return list(outs[:counts[2]]), list(outs[counts[2]:])


def _dividing_tile(n, want):
    tile = min(want, n)
    while n % tile:
        tile -= LANES
    return tile


def _matmul(a, b, *, mode, tm, tn, out_dtypes, name, epilogue=None, extras=(), hosted=None):
    a_parts = tuple(a) if isinstance(a, (tuple, list)) else (a,)
    b_parts = tuple(b) if isinstance(b, (tuple, list)) else (b,)
    assert len(a_parts) == len(b_parts) and (mode == "nt" or len(a_parts) == 1)
    n_parts = len(a_parts)
    m = a_parts[0].shape[0]
    n = b_parts[0].shape[1] if mode == "nn" else b_parts[0].shape[0]
    tm, tn = min(tm, m), _dividing_tile(n, tn)
    n_extra = len(extras)

    def body(*refs):
        a_refs, b_refs = refs[:n_parts], refs[n_parts:2 * n_parts]
        rest = refs[2 * n_parts:]
        extra_refs, out_refs = rest[:n_extra], rest[n_extra:]
        if mode == "nn":
            acc = jnp.dot(a_refs[0][...], b_refs[0][...], preferred_element_type=F32)
        else:
            acc = _dot_nt(a_refs[0][...], b_refs[0][...])
            for a_ref, b_ref in zip(a_refs[1:], b_refs[1:]):
                acc = acc + _dot_nt(a_ref[...], b_ref[...])
        res = epilogue(acc, *[e[...] for e in extra_refs]) if epilogue is not None else (acc,)
        for o_ref, r in zip(out_refs, res):
            o_ref[...] = r.astype(o_ref.dtype)

    a_specs = [pl.BlockSpec((tm, ap.shape[1]), lambda i, j: (i, 0)) for ap in a_parts]
    if mode == "nn":
        b_specs = [pl.BlockSpec((b_parts[0].shape[0], tn), lambda i, j: (0, j))]
    else:
        b_specs = [pl.BlockSpec((tn, bp.shape[1]), lambda i, j: (j, 0)) for bp in b_parts]
    tile = pl.BlockSpec((tm, tn), lambda i, j: (i, j))
    outs, landed = _call(
        body,
        grid=(m // tm, n // tn),
        in_specs=a_specs + b_specs + [tile] * n_extra,
        out_specs=[tile] * len(out_dtypes),
        out_shape=[jax.ShapeDtypeStruct((m, n), dt) for dt in out_dtypes],
        args=(*a_parts, *b_parts, *extras), name=name, semantics=("parallel", "arbitrary"), hosted=hosted)
    if hosted is not None:
        return (*outs, landed)
    return outs[0] if len(outs) == 1 else outs


def _matmul_tn(a, b, *, tk, tn, tt, name, slot_cols=None):
    t, k = a.shape
    n = b.shape[1]
    tk, tn, tt = min(tk, k), _dividing_tile(n, tn), min(tt, t)

    def body(a_ref, b_ref, o_ref):
        @pl.when(pl.program_id(2) == 0)
        def _():
            o_ref[...] = jnp.zeros_like(o_ref)

        if slot_cols is None:
            o_ref[...] += _dot_tn(a_ref[...], b_ref[...])
        else:
            av = a_ref[...]
            for s in range(tn // slot_cols):
                o_ref[s] += _dot_tn(av, b_ref[:, s * slot_cols:(s + 1) * slot_cols])

    if slot_cols is not None:
        out_spec = pl.BlockSpec((tn // slot_cols, tk, slot_cols), lambda i, j, s: (j, i, 0))
        out_shape = jax.ShapeDtypeStruct((n // slot_cols, k, slot_cols), F32)
    else:
        out_spec = pl.BlockSpec((tk, tn), lambda i, j, s: (i, j))
        out_shape = jax.ShapeDtypeStruct((k, n), F32)
    return pl.pallas_call(
        body,
        grid=(k // tk, n // tn, t // tt),
        in_specs=[pl.BlockSpec((tt, tk), lambda i, j, s: (s, i)), pl.BlockSpec((tt, tn), lambda i, j, s: (s, j))],
        out_specs=out_spec,
        out_shape=out_shape,
        compiler_params=_params("parallel", "parallel", "arbitrary"),
        name=name,
    )(a, b)


def _rmsnorm_fwd(x, g, *, name):
    t, d = x.shape
    tm = min(512, t)

    def body(x_ref, g_ref, o_ref):
        xv = x_ref[...]
        r = lax.rsqrt(jnp.mean(xv * xv, axis=-1, keepdims=True) + NORM_EPS)
        o_ref[...] = (xv * r * g_ref[...]).astype(BF16)

    return pl.pallas_call(
        body,
        grid=(t // tm,),
        in_specs=[pl.BlockSpec((tm, d), lambda i: (i, 0)), pl.BlockSpec((1, d), lambda i: (0, 0))],
        out_specs=pl.BlockSpec((tm, d), lambda i: (i, 0)),
        out_shape=jax.ShapeDtypeStruct((t, d), BF16),
        compiler_params=_params("parallel"),
        name=name,
    )(x, g)


def _rmsnorm_bwd(x, g, dy, dres, *, name, want_bf16, hosted=None):
    t, d = x.shape
    tm = min(256, t)

    def body(x_ref, g_ref, dy_ref, dres_ref, *out_refs):
        dx_ref, dg_ref = out_refs[0], out_refs[-1]
        xv, dyv = x_ref[...], dy_ref[...]
        r = lax.rsqrt(jnp.mean(xv * xv, axis=-1, keepdims=True) + NORM_EPS)
        xr = xv * r
        gy = dyv * g_ref[...]
        dx = dres_ref[...] + r * (gy - xr * jnp.mean(gy * xr, axis=-1, keepdims=True))
        dx_ref[...] = dx
        if want_bf16:
            out_refs[1][...] = dx.astype(BF16)

        @pl.when(pl.program_id(0) == 0)
        def _():
            dg_ref[...] = jnp.zeros_like(dg_ref)

        dg_ref[...] += jnp.sum(dyv * xr, axis=0, keepdims=True)

    tile = pl.BlockSpec((tm, d), lambda i: (i, 0))
    vec = pl.BlockSpec((1, d), lambda i: (0, 0))
    out_specs = [tile] + ([tile] if want_bf16 else []) + [vec]
    out_shape = [jax.ShapeDtypeStruct((t, d), F32)] + ([jax.ShapeDtypeStruct((t, d), BF16)] if want_bf16 else [])
    out_shape.append(jax.ShapeDtypeStruct((1, d), F32))
    outs, landed = _call(body, grid=(t // tm,), in_specs=[tile, vec, tile, tile], out_specs=out_specs,
                         out_shape=out_shape, args=(x, g, dy, dres), name=name, semantics=("arbitrary",), hosted=hosted)
    return (*outs, landed) if hosted is not None else outs


def _softplus_neg(lam):
    z = -lam
    return jnp.maximum(z, 0.0) + jnp.log1p(jnp.exp(-jnp.abs(z)))


def _neg_expm1(y, exp_half_y):
    series = -y * (1.0 + y * 0.5 * (1.0 + y * (1.0 / 3.0) * (1.0 + y * 0.25 * (1.0 + y * 0.2))))
    return jnp.where(y > -0.0625, series, 1.0 - exp_half_y * exp_half_y)


def _gelu_parts(x):
    c = 0.7978845608028654
    u = c * (x + 0.044715 * x * x * x)
    th = jnp.tanh(u)
    gel = 0.5 * x * (1.0 + th)
    dgel = 0.5 * (1.0 + th) + 0.5 * x * (1.0 - th * th) * c * (1.0 + 3.0 * 0.044715 * x * x)
    return gel, dgel


def _shift_down(v, k, rows):
    return jnp.where(rows < k, 0.0, pltpu.roll(v, k, 0))


def _shift_up(v, k, rows, n):
    return jnp.where(rows >= n - k, 0.0, pltpu.roll(v, n - k, 0))


def _scan_within_groups(a, b, rows, *, reverse):
    n = a.shape[0]
    in_group = rows & (SUBLANES - 1)
    for s in (1, 2, 4):
        if reverse:
            inside, shift = in_group < SUBLANES - s, n - s
        else:
            inside, shift = in_group >= s, s
        b = b + a * jnp.where(inside, pltpu.roll(b, shift, 0), 0.0)
        a = a * jnp.where(inside, pltpu.roll(a, shift, 0), 1.0)
    return a, b


def _rnn_gates(xc, wrg, brg, wig, big, lam):
    xcb = xc.astype(BF16)
    r = _sig(jnp.dot(xcb, wrg, preferred_element_type=F32) + brg)
    i = _sig(jnp.dot(xcb, wig, preferred_element_type=F32) + big)
    sp = _softplus_neg(lam)
    log_a = -LRU_C * r * sp
    a = jnp.exp(log_a)
    mult = jnp.sqrt(_neg_expm1(2.0 * log_a, a))
    return xcb, r, i, sp, a, mult


def _conv_fwd(xv, cw, cb, rows):
    return (cb + _shift_down(xv, 3, rows) * cw[0:1, :] + _shift_down(xv, 2, rows) * cw[1:2, :]
            + _shift_down(xv, 1, rows) * cw[2:3, :] + xv * cw[3:4, :])


def _rnn_fwd(z, conv_w, conv_b, wrg_bd, b_rg, wig_bd, b_ig, lam, *, n_seq, seq, hosted=None):
    t = n_seq * seq
    ct = RNN_TILE
    n_ct = D_MODEL // ct

    def body(x_ref, g_ref, cw_ref, cb_ref, wrg_ref, brg_ref, wig_ref, big_ref, lam_ref,
             xc_ref, hr_ref, ya_ref, a_s, b_s):
        rows = lax.broadcasted_iota(jnp.int32, (seq, ct), 0)
        xc = _conv_fwd(x_ref[...], cw_ref[...], cb_ref[...], rows)
        _, r, i, sp, a, mult = _rnn_gates(xc, wrg_ref[...], brg_ref[...], wig_ref[...], big_ref[...], lam_ref[...])
        a_s[...], b_s[...] = _scan_within_groups(a, mult * (i * xc), rows, reverse=False)

        def step(j, carry):
            r0 = pl.multiple_of(j * SUBLANES, SUBLANES)
            h = b_s[pl.ds(r0, SUBLANES), :] + a_s[pl.ds(r0, SUBLANES), :] * carry
            hr_ref[pl.ds(r0, SUBLANES), :] = h
            return h[SUBLANES - 1:SUBLANES, :]

        lax.fori_loop(0, seq // SUBLANES, step, jnp.zeros((1, ct), F32), unroll=4)
        gel, _ = _gelu_parts(g_ref[...])
        xc_ref[...] = xc
        ya_ref[...] = (hr_ref[...] * gel).astype(BF16)

    vec = pl.BlockSpec((1, ct), lambda b, c: (0, c))
    gate_w = pl.BlockSpec((None, ct, ct), lambda b, c: (c, 0, 0))
    tile = pl.BlockSpec((seq, ct), lambda b, c: (b, c))
    outs, landed = _call(
        body,
        grid=(n_seq, n_ct),
        in_specs=[
            pl.BlockSpec((seq, ct), lambda b, c: (b, 2 * c)),
            pl.BlockSpec((seq, ct), lambda b, c: (b, 2 * c + 1)),
            pl.BlockSpec((CONV_W, ct), lambda b, c: (0, c)), vec, gate_w, vec, gate_w, vec, vec,
        ],
        out_specs=[tile, tile, tile],
        out_shape=[jax.ShapeDtypeStruct((t, D_MODEL), F32), jax.ShapeDtypeStruct((t, D_MODEL), F32),
                   jax.ShapeDtypeStruct((t, D_MODEL), BF16)],
        scratch_shapes=[pltpu.VMEM((seq, ct), F32), pltpu.VMEM((seq, ct), F32)],
        args=(z, z, conv_w, conv_b, wrg_bd, b_rg, wig_bd, b_ig, lam), name="rnn_fwd",
        semantics=("parallel", "parallel"), hosted=hosted)
    return (*outs, landed) if hosted is not None else outs


def _rnn_bwd(dya, z, xc, hr, conv_w, wrg_bd, b_rg, wig_bd, b_ig, lam, *, n_seq, seq, hosted=None):
    t = n_seq * seq
    ct = RNN_TILE
    n_ct = D_MODEL // ct

    def body(dya_ref, x_ref, g_ref, xc_ref, hr_ref, cw_ref, wrg_ref, brg_ref, wig_ref, big_ref, lam_ref,
             dz_ref, dwrg_ref, dwig_ref, vec_ref, a_s, d_s, g_s):
        rows = lax.broadcasted_iota(jnp.int32, (seq, ct), 0)
        xv, xc, hr, dyv = x_ref[...], xc_ref[...], hr_ref[...], dya_ref[...]
        lamv = lam_ref[...]
        gel, dgel = _gelu_parts(g_ref[...])
        dz_ref[:, ct:] = (dyv * hr * dgel).astype(BF16)
        xcb, r, i, sp, a, mult = _rnn_gates(xc, wrg_ref[...], brg_ref[...], wig_ref[...], big_ref[...], lamv)
        a_s[...], d_s[...] = _scan_within_groups(_shift_up(a, 1, rows, seq), dyv * gel, rows, reverse=True)

        def step(k, carry):
            r0 = pl.multiple_of((seq // SUBLANES - 1 - k) * SUBLANES, SUBLANES)
            gs = d_s[pl.ds(r0, SUBLANES), :] + a_s[pl.ds(r0, SUBLANES), :] * carry
            g_s[pl.ds(r0, SUBLANES), :] = gs
            return gs[0:1, :]

        lax.fori_loop(0, seq // SUBLANES, step, jnp.zeros((1, ct), F32), unroll=4)
        gsum = g_s[...]
        gated = i * xc
        d_log_a = gsum * _shift_down(hr, 1, rows) * a - gsum * gated * (a * a / mult)
        d_gated = gsum * mult
        d_pre_r = (d_log_a * (-LRU_C) * sp) * r * (1.0 - r)
        d_pre_i = (d_gated * xc) * i * (1.0 - i)
        dprb, dpib = d_pre_r.astype(BF16), d_pre_i.astype(BF16)
        dxc = d_gated * i + _dot_nt(dprb, wrg_ref[...]) + _dot_nt(dpib, wig_ref[...])
        cw = cw_ref[...]
        dx = (dxc * cw[3:4, :] + _shift_up(dxc, 1, rows, seq) * cw[2:3, :]
              + _shift_up(dxc, 2, rows, seq) * cw[1:2, :] + _shift_up(dxc, 3, rows, seq) * cw[0:1, :])
        dz_ref[:, :ct] = dx.astype(BF16)

        @pl.when(pl.program_id(1) == 0)
        def _():
            dwrg_ref[...] = jnp.zeros_like(dwrg_ref)
            dwig_ref[...] = jnp.zeros_like(dwig_ref)
            vec_ref[...] = jnp.zeros_like(vec_ref)

        dwrg_ref[...] += _dot_tn(xcb, dprb)
        dwig_ref[...] += _dot_tn(xcb, dpib)

        def colsum(v):
            return jnp.sum(v, axis=0, keepdims=True)

        d_sp = colsum(d_log_a * (-LRU_C) * r)
        vec_ref[0:1, :] += colsum(d_pre_r)
        vec_ref[1:2, :] += colsum(d_pre_i)
        vec_ref[2:3, :] += d_sp * (-_sig(-lamv))
        vec_ref[3:4, :] += colsum(dxc)
        vec_ref[4:5, :] += colsum(dxc * _shift_down(xv, 3, rows))
        vec_ref[5:6, :] += colsum(dxc * _shift_down(xv, 2, rows))
        vec_ref[6:7, :] += colsum(dxc * _shift_down(xv, 1, rows))
        vec_ref[7:8, :] += colsum(dxc * xv)

    vec = pl.BlockSpec((1, ct), lambda c, b: (0, c))
    gate_w = pl.BlockSpec((None, ct, ct), lambda c, b: (c, 0, 0))
    tile = pl.BlockSpec((seq, ct), lambda c, b: (b, c))
    outs, landed = _call(
        body,
        grid=(n_ct, n_seq),
        in_specs=[
            tile,
            pl.BlockSpec((seq, ct), lambda c, b: (b, 2 * c)),
            pl.BlockSpec((seq, ct), lambda c, b: (b, 2 * c + 1)),
            tile, tile,
            pl.BlockSpec((CONV_W, ct), lambda c, b: (0, c)), gate_w, vec, gate_w, vec, vec,
        ],
        out_specs=[pl.BlockSpec((seq, 2 * ct), lambda c, b: (b, c)), gate_w, gate_w,
                   pl.BlockSpec((8, ct), lambda c, b: (0, c))],
        out_shape=[jax.ShapeDtypeStruct((t, 2 * D_MODEL), BF16),
                   jax.ShapeDtypeStruct((n_ct, ct, ct), F32), jax.ShapeDtypeStruct((n_ct, ct, ct), F32),
                   jax.ShapeDtypeStruct((8, D_MODEL), F32)],
        scratch_shapes=[pltpu.VMEM((seq, ct), F32)] * 3,
        args=(dya, z, z, xc, hr, conv_w, wrg_bd, b_rg, wig_bd, b_ig, lam), name="rnn_bwd",
        semantics=("parallel", "arbitrary"), hosted=hosted)
    return (*outs, landed) if hosted is not None else outs


def _split_hi_lo(x):
    hi = x.astype(BF16)
    return hi, (x - hi.astype(F32)).astype(BF16)


def _dot_split(x, m_twice):
    hi, lo = _split_hi_lo(x)
    return jnp.dot(jnp.concatenate([hi, lo], axis=1), m_twice, preferred_element_type=F32)


def _head_matrices(width):
    ec = ((lax.broadcasted_iota(jnp.int32, (2 * width, LANES), 0) & (width - 1)) // HEAD_DIM
          == lax.broadcasted_iota(jnp.int32, (2 * width, LANES), 1))
    ee = (lax.broadcasted_iota(jnp.int32, (2 * LANES, width), 1) // HEAD_DIM
          == (lax.broadcasted_iota(jnp.int32, (2 * LANES, width), 0) & (LANES - 1)))
    return jnp.where(ec, 1.0, 0.0).astype(BF16), jnp.where(ee, 1.0, 0.0).astype(BF16)


def _swap_halves(y):
    w = y.shape[1]
    first = (lax.broadcasted_iota(jnp.int32, y.shape, 1) % HEAD_DIM) < HEAD_DIM // 2
    return jnp.where(first, pltpu.roll(y, w - HEAD_DIM // 2, 1), pltpu.roll(y, HEAD_DIM // 2, 1))


def _normrope_fwd(x, gain, cos_t, sin_t, ec, ee):
    w = x.shape[1]
    rs = _dot_split(lax.rsqrt(_dot_split(x * x, ec) * (1.0 / HEAD_DIM) + NORM_EPS), ee)
    nx = x * rs
    y = nx * gain
    reps = w // LANES
    out = y * jnp.tile(cos_t, (1, reps)) + _swap_halves(y) * jnp.tile(sin_t, (1, reps))
    return out, nx, rs


def _normrope_bwd(dout, nx, rs, gain, cos_t, sin_t, ec, ee):
    w = dout.shape[1]
    reps = w // LANES
    dy = dout * jnp.tile(cos_t, (1, reps)) + _swap_halves(dout * jnp.tile(sin_t, (1, reps)))
    dgain = jnp.sum(dy * nx, axis=0, keepdims=True)
    dn = dy * gain
    seg = _dot_split(_dot_split(dn * nx, ec) * (1.0 / HEAD_DIM), ee)
    return rs * (dn - nx * seg), dgain


def _pair_operand(t, group):
    chunk = t[:, (group // 2) * LANES:(group // 2 + 1) * LANES]
    low = lax.broadcasted_iota(jnp.int32, chunk.shape, 1) < HEAD_DIM
    rolled = pltpu.roll(chunk, HEAD_DIM, 1)
    return jnp.where(low, chunk, rolled) if group % 2 == 0 else jnp.where(low, rolled, chunk)


GROUP = N_Q_HEADS // N_KV_HEADS
GROUP_W = GROUP * HEAD_DIM


def _replicate_head(t, group):
    return jnp.tile(_pair_operand(t, group), (1, 2))


def _head_blocks(t):
    seg = lax.broadcasted_iota(jnp.int32, t.shape, 1) // HEAD_DIM
    return jnp.concatenate([jnp.where(seg == h, t, 0.0) for h in range(GROUP)], axis=0)


def _stack_heads(t_t, rows):
    return jnp.concatenate([t_t[:, h * rows:(h + 1) * rows] for h in range(GROUP)], axis=0)


def _head_rows(mat_t, group):
    return jnp.concatenate([mat_t[GROUP * group + h:GROUP * group + h + 1, :] for h in range(GROUP)], axis=1)


def _window_masks(blk):
    key = lax.broadcasted_iota(jnp.int32, (blk, GROUP * blk), 0)
    query = lax.broadcasted_iota(jnp.int32, (blk, GROUP * blk), 1) & (blk - 1)
    return key > query, key <= query


def _mask_window(t, before_ok, own_ok, fill):
    blk = t.shape[0] // 2
    return jnp.concatenate([jnp.where(before_ok, t[:blk], fill), jnp.where(own_ok, t[blk:], fill)], axis=0)


def _attn_fwd(z, cos_t, sin_t, q_gain_t, k_gain_t, sinks_t, *, n_seq, seq, hosted=None):
    t = n_seq * seq
    blk = WINDOW
    nb = seq // blk

    def body(q_ref, kp_ref, kc_ref, vp_ref, vc_ref, cosc_ref, sinc_ref, cosp_ref, sinp_ref, qg_ref, kg_ref, sk_ref,
             o_ref, l_ref):
        n = pl.program_id(1)
        ecq, eeq = _head_matrices(D_MODEL)
        eck, eek = _head_matrices(KV_W)
        cosc, sinc = cosc_ref[...], sinc_ref[...]
        qh, _, _ = _normrope_fwd(q_ref[...], qg_ref[...], cosc, sinc, ecq, eeq)
        qh = qh * (HEAD_DIM ** -0.5)
        kc, _, _ = _normrope_fwd(kc_ref[...], kg_ref[...], cosc, sinc, eck, eek)
        kp, _, _ = _normrope_fwd(kp_ref[...], kg_ref[...], cosp_ref[...], sinp_ref[...], eck, eek)
        kcat = jnp.concatenate([kp, kc], axis=0)
        vcat = jnp.concatenate([vp_ref[...], vc_ref[...]], axis=0)
        above, causal = _window_masks(blk)
        above = above & (n > 0)
        head_row = lax.broadcasted_iota(jnp.int32, (blk, blk), 0)
        sk_t = jnp.broadcast_to(sk_ref[...], (blk, LANES)).T
        vcat_t = vcat.T.astype(BF16)
        lmat = jnp.zeros((blk, blk), F32)
        groups = range(N_KV_HEADS)
        cols = [slice(g * GROUP_W, (g + 1) * GROUP_W) for g in groups]
        scores = [_dot_nt(_replicate_head(kcat, g).astype(BF16), _head_blocks(qh[:, cols[g]]).astype(BF16))
                  for g in groups]
        probs = []
        for g in groups:
            s = _mask_window(scores[g], above, causal, NEG_BIG)
            sink = _head_rows(sk_t, g)
            m = jnp.maximum(jnp.max(s, axis=0, keepdims=True), sink)
            e = jnp.exp(s - m)
            den = jnp.sum(e, axis=0, keepdims=True) + jnp.exp(sink - m)
            probs.append((e * (1.0 / den)).astype(BF16))
            lse = m + jnp.log(den)
            for h in range(GROUP):
                lmat = lmat + jnp.where(head_row == GROUP * g + h, lse[:, h * blk:(h + 1) * blk], 0.0)
        for g in groups:
            out_t = jnp.dot(vcat_t[g * HEAD_DIM:(g + 1) * HEAD_DIM], probs[g], preferred_element_type=F32)
            o_ref[:, cols[g]] = _stack_heads(out_t, blk).T.astype(BF16)
        l_ref[...] = lmat

    def row(b, n):
        return b * nb + n

    def prev(b, n):
        return b * nb + jnp.maximum(n - 1, 0)

    kw = KV_W
    tab_c = pl.BlockSpec((blk, LANES), lambda b, n: (n, 0))
    tab_p = pl.BlockSpec((blk, LANES), lambda b, n: (jnp.maximum(n - 1, 0), 0))
    outs, landed = _call(
        body,
        grid=(n_seq, nb),
        in_specs=[
            pl.BlockSpec((blk, D_MODEL), lambda b, n: (row(b, n), 0)),
            pl.BlockSpec((blk, kw), lambda b, n: (prev(b, n), ATTN_K_AT // kw)),
            pl.BlockSpec((blk, kw), lambda b, n: (row(b, n), ATTN_K_AT // kw)),
            pl.BlockSpec((blk, kw), lambda b, n: (prev(b, n), ATTN_V_AT // kw)),
            pl.BlockSpec((blk, kw), lambda b, n: (row(b, n), ATTN_V_AT // kw)),
            tab_c, tab_c, tab_p, tab_p,
            pl.BlockSpec((1, D_MODEL), lambda b, n: (0, 0)),
            pl.BlockSpec((1, kw), lambda b, n: (0, 0)),
            pl.BlockSpec((1, LANES), lambda b, n: (0, 0)),
        ],
        out_specs=[pl.BlockSpec((blk, D_MODEL), lambda b, n: (row(b, n), 0)),
                   pl.BlockSpec((blk, LANES), lambda b, n: (row(b, n), 0))],
        out_shape=[jax.ShapeDtypeStruct((t, D_MODEL), BF16), jax.ShapeDtypeStruct((t, LANES), F32)],
        args=(z, z, z, z, z, cos_t, sin_t, cos_t, sin_t, q_gain_t, k_gain_t, sinks_t), name="attn_fwd",
        semantics=("parallel", "parallel"), hosted=hosted)
    return (*outs, landed) if hosted is not None else outs


def _attn_bwd(z, o, lse, do, cos_t, sin_t, q_gain_t, k_gain_t, sinks_t, *, n_seq, seq, hosted=None):
    t = n_seq * seq
    blk = WINDOW
    nb = seq // blk
    kw = KV_W
    scale = HEAD_DIM ** -0.5

    def body(qc_ref, qn_ref, kp_ref, kc_ref, vp_ref, vc_ref, oc_ref, on_ref, doc_ref, don_ref, lc_ref, ln_ref,
             cosc_ref, sinc_ref, cosp_ref, sinp_ref, cosn_ref, sinn_ref, qg_ref, kg_ref, sk_ref,
             dz_ref, vec_ref, dq_s):
        n = pl.program_id(1)
        ecq, eeq = _head_matrices(D_MODEL)
        eck, eek = _head_matrices(KV_W)
        cosc, sinc = cosc_ref[...], sinc_ref[...]
        qg, kg = qg_ref[...], kg_ref[...]
        qhc, nqc, rsqc = _normrope_fwd(qc_ref[...], qg, cosc, sinc, ecq, eeq)
        qhn, _, _ = _normrope_fwd(qn_ref[...], qg, cosn_ref[...], sinn_ref[...], ecq, eeq)
        khc, nkc, rskc = _normrope_fwd(kc_ref[...], kg, cosc, sinc, eck, eek)
        khp, _, _ = _normrope_fwd(kp_ref[...], kg, cosp_ref[...], sinp_ref[...], eck, eek)
        doc = doc_ref[...].astype(F32)
        don = don_ref[...].astype(F32)
        delc = _dot_split(doc * oc_ref[...].astype(F32), ecq)
        deln = _dot_split(don * on_ref[...].astype(F32), ecq)
        lc_t, ln_t, delc_t, deln_t = lc_ref[...], ln_ref[...], delc.T, deln.T
        above, causal = _window_masks(blk)
        above_c, above_n = above & (n > 0), above & (n < nb - 1)
        seg = lax.broadcasted_iota(jnp.int32, (blk, GROUP_W), 1) // HEAD_DIM
        lane = lax.broadcasted_iota(jnp.int32, (1, LANES), 1)
        sk_t = jnp.broadcast_to(sk_ref[...], (blk, LANES)).T
        dsink = jnp.zeros((1, LANES), F32)
        kcat = jnp.concatenate([khp, khc], axis=0)
        vcat = jnp.concatenate([vp_ref[...], vc_ref[...]], axis=0)
        kcat_t = kcat.T.astype(BF16)
        dkh = jnp.zeros((blk, GROUP_W), F32)
        dvh = jnp.zeros((blk, GROUP_W), F32)

        def fold_to(group, t):
            total = t + pltpu.roll(t, HEAD_DIM, 1)
            total = total + pltpu.roll(total, 2 * HEAD_DIM, 1)
            return jnp.where(seg == group, total, 0.0)

        groups = range(N_KV_HEADS)
        cols = [slice(g * GROUP_W, (g + 1) * GROUP_W) for g in groups]
        qsc, qsn = qhc * scale, qhn * scale
        qb_c = [_head_blocks(qsc[:, cols[g]]).astype(BF16) for g in groups]
        qb_n = [_head_blocks(qsn[:, cols[g]]).astype(BF16) for g in groups]
        dob_c = [_head_blocks(doc[:, cols[g]]).astype(BF16) for g in groups]
        dob_n = [_head_blocks(don[:, cols[g]]).astype(BF16) for g in groups]
        raw = []
        for g in groups:
            krep = _replicate_head(kcat, g).astype(BF16)
            vrep = _replicate_head(vcat, g).astype(BF16)
            raw.append((_dot_nt(krep, qb_c[g]), _dot_nt(vrep, dob_c[g]),
                        _dot_nt(krep[blk:], qb_n[g]), _dot_nt(vrep[blk:], dob_n[g])))
        cooked = []
        for g in groups:
            s_c, dp_c, s_n, dp_n = raw[g]
            l_row, d_row = _head_rows(lc_t, g), _head_rows(delc_t, g)
            p_c = _mask_window(jnp.exp(s_c - l_row), above_c, causal, 0.0)
            ds_c = (p_c * (dp_c - d_row)).astype(BF16)
            p_n = jnp.where(above_n, jnp.exp(s_n - _head_rows(ln_t, g)), 0.0)
            ds_n = (p_n * (dp_n - _head_rows(deln_t, g))).astype(BF16)
            cooked.append((p_c[blk:].astype(BF16), ds_c, p_n.astype(BF16), ds_n))
            p_sink = jnp.exp(_head_rows(sk_t, g) - l_row) * d_row
            for h in range(GROUP):
                dsink = dsink + jnp.where(lane == GROUP * g + h,
                                          -jnp.sum(p_sink[:, h * blk:(h + 1) * blk], axis=1, keepdims=True), 0.0)
        for g in groups:
            p_cb, ds_c, p_nb, ds_n = cooked[g]
            dq_t = jnp.dot(kcat_t[g * HEAD_DIM:(g + 1) * HEAD_DIM], ds_c, preferred_element_type=F32)
            dq_s[:, cols[g]] = _stack_heads(dq_t, blk).T * scale
            dk_rep = (jnp.dot(ds_c[blk:], qb_c[g], preferred_element_type=F32)
                      + jnp.dot(ds_n, qb_n[g], preferred_element_type=F32))
            dv_rep = (jnp.dot(p_cb, dob_c[g], preferred_element_type=F32)
                      + jnp.dot(p_nb, dob_n[g], preferred_element_type=F32))
            dkh = dkh + fold_to(g, dk_rep)
            dvh = dvh + fold_to(g, dv_rep)
        dq, dqg = _normrope_bwd(dq_s[...], nqc, rsqc, qg, cosc, sinc, ecq, eeq)
        dk, dkg = _normrope_bwd(dkh, nkc, rskc, kg, cosc, sinc, eck, eek)
        dz_ref[:, :ATTN_K_AT] = dq.astype(BF16)
        dz_ref[:, ATTN_K_AT:ATTN_V_AT] = dk.astype(BF16)
        dz_ref[:, ATTN_V_AT:] = dvh.astype(BF16)

        @pl.when(n == 0)
        def _():
            vec_ref[...] = jnp.zeros_like(vec_ref)

        vec_ref[0:1, :] += dqg
        vec_ref[1:2, 0:kw] += dkg
        vec_ref[2:3, 0:LANES] += dsink

    def row(b, n):
        return b * nb + n

    def prev(b, n):
        return b * nb + jnp.maximum(n - 1, 0)

    def nxt(b, n):
        return b * nb + jnp.minimum(n + 1, nb - 1)

    def tiles(width, col, which):
        return pl.BlockSpec((blk, width), lambda b, n: (which(b, n), col))

    def table(which):
        return pl.BlockSpec((blk, LANES), lambda b, n: (which(0, n), 0))

    outs, landed = _call(
        body,
        grid=(n_seq, nb),
        in_specs=[
            tiles(D_MODEL, 0, row), tiles(D_MODEL, 0, nxt),
            tiles(kw, ATTN_K_AT // kw, prev), tiles(kw, ATTN_K_AT // kw, row),
            tiles(kw, ATTN_V_AT // kw, prev), tiles(kw, ATTN_V_AT // kw, row),
            tiles(D_MODEL, 0, row), tiles(D_MODEL, 0, nxt),
            tiles(D_MODEL, 0, row), tiles(D_MODEL, 0, nxt),
            tiles(LANES, 0, row), tiles(LANES, 0, nxt),
            table(row), table(row), table(prev), table(prev), table(nxt), table(nxt),
            pl.BlockSpec((1, D_MODEL), lambda b, n: (0, 0)),
            pl.BlockSpec((1, kw), lambda b, n: (0, 0)),
            pl.BlockSpec((1, LANES), lambda b, n: (0, 0)),
        ],
        out_specs=[tiles(ATTN_W, 0, row), pl.BlockSpec((None, 8, D_MODEL), lambda b, n: (b, 0, 0))],
        out_shape=[jax.ShapeDtypeStruct((t, ATTN_W), BF16), jax.ShapeDtypeStruct((n_seq, 8, D_MODEL), F32)],
        scratch_shapes=[pltpu.VMEM((blk, D_MODEL), F32)],
        args=(z, z, z, z, z, z, o, o, do, do, lse, lse, cos_t, sin_t, cos_t, sin_t, cos_t, sin_t,
              q_gain_t, k_gain_t, sinks_t), name="attn_bwd", semantics=("parallel", "arbitrary"), hosted=hosted)
    return (*outs, landed) if hosted is not None else outs


MERGE_COLS = 512


def _merge_fwd(z, ya, yb):
    t = ya.shape[0]
    tm, tc = min(512, t), MERGE_COLS

    def body(ga_ref, gb_ref, ya_ref, yb_ref, o_ref):
        o_ref[...] = (_sig(ga_ref[...]) * ya_ref[...] + _sig(gb_ref[...]) * yb_ref[...]).astype(BF16)

    tile = pl.BlockSpec((tm, tc), lambda i, j: (i, j))
    return pl.pallas_call(
        body,
        grid=(t // tm, D_MODEL // tc),
        in_specs=[pl.BlockSpec((tm, tc), lambda i, j: (i, 2 * j)),
                  pl.BlockSpec((tm, tc), lambda i, j: (i, 2 * j + 1)), tile, tile],
        out_specs=tile,
        out_shape=jax.ShapeDtypeStruct((t, D_MODEL), BF16),
        compiler_params=_params("parallel", "parallel"),
        name="merge_fwd",
    )(z, z, ya, yb)


def _merge_bwd(z, ya, yb, dmerged):
    t = ya.shape[0]
    tm, tc = min(512, t), MERGE_COLS

    def body(ga_ref, gb_ref, ya_ref, yb_ref, dm_ref, dya_ref, dyb_ref, dz_ref):
        dm = dm_ref[...]
        sa, sb = _sig(ga_ref[...]), _sig(gb_ref[...])
        dya_ref[...] = (dm * sa).astype(BF16)
        dyb_ref[...] = (dm * sb).astype(BF16)
        dz_ref[:, :tc] = (dm * ya_ref[...] * sa * (1.0 - sa)).astype(BF16)
        dz_ref[:, tc:] = (dm * yb_ref[...] * sb * (1.0 - sb)).astype(BF16)

    tile = pl.BlockSpec((tm, tc), lambda i, j: (i, j))
    return pl.pallas_call(
        body,
        grid=(t // tm, D_MODEL // tc),
        in_specs=[pl.BlockSpec((tm, tc), lambda i, j: (i, 2 * j)),
                  pl.BlockSpec((tm, tc), lambda i, j: (i, 2 * j + 1)), tile, tile, tile],
        out_specs=[tile, tile, pl.BlockSpec((tm, 2 * tc), lambda i, j: (i, j))],
        out_shape=[jax.ShapeDtypeStruct((t, D_MODEL), BF16)] * 2 + [jax.ShapeDtypeStruct((t, 2 * D_MODEL), BF16)],
        compiler_params=_params("parallel", "parallel"),
        name="merge_bwd",
    )(z, z, ya, yb, dmerged)


def _loss_head(x2, e, gt, target):
    t, d = x2.shape
    tm = min(256, t)

    def body(x_ref, e_ref, gt_ref, tg_ref, loss_ref, dx_ref, dgt_ref, de_ref):
        ev = e_ref[...]
        sg = _sig(gt_ref[...])
        diff = x_ref[...] + ev * sg - tg_ref[...]
        dx = diff * (1.0 / d)
        dx_ref[...] = dx
        dgt_ref[...] = (dx * ev * sg * (1.0 - sg)).astype(BF16)
        de_ref[...] = (dx * sg).astype(BF16)

        @pl.when(pl.program_id(0) == 0)
        def _():
            loss_ref[...] = jnp.zeros_like(loss_ref)

        loss_ref[...] += jnp.sum(jnp.sum(diff * diff, axis=1, keepdims=True), axis=0, keepdims=True)

    tile = pl.BlockSpec((tm, d), lambda i: (i, 0))
    return pl.pallas_call(
        body,
        grid=(t // tm,),
        in_specs=[tile] * 4,
        out_specs=[pl.BlockSpec((1, LANES), lambda i: (0, 0)), tile, tile, tile],
        out_shape=[jax.ShapeDtypeStruct((1, LANES), F32), jax.ShapeDtypeStruct((t, d), F32),
                   jax.ShapeDtypeStruct((t, d), BF16), jax.ShapeDtypeStruct((t, d), BF16)],
        compiler_params=_params("arbitrary"),
        name="loss_head",
    )(x2, e, gt, target)


def _rope_tables(seq):
    inv = ROPE_THETA ** (-jnp.arange(0, HEAD_DIM, 2, dtype=F32) / HEAD_DIM)
    ang = jnp.arange(seq, dtype=F32)[:, None] * inv[None, :]
    cos, sin = jnp.cos(ang), jnp.sin(ang)
    return jnp.tile(jnp.concatenate([cos, cos], axis=1), (1, 2)), jnp.tile(jnp.concatenate([-sin, sin], axis=1), (1, 2))


def _block_diag_tiles(w):
    per = RNN_TILE // RNN_BLOCK_W
    w4 = w.reshape(D_MODEL // RNN_TILE, per, RNN_BLOCK_W, RNN_BLOCK_W)
    eye = jnp.eye(per, dtype=w.dtype)
    dense = jnp.einsum("tpij,pq->tpiqj", w4, eye)
    return dense.reshape(D_MODEL // RNN_TILE, RNN_TILE, RNN_TILE).astype(BF16)


def _block_diag_extract(dense):
    per = RNN_TILE // RNN_BLOCK_W
    d5 = dense.reshape(D_MODEL // RNN_TILE, per, RNN_BLOCK_W, per, RNN_BLOCK_W)
    blocks = jnp.stack([d5[:, p, :, p, :] for p in range(per)], axis=1)
    return blocks.reshape(D_MODEL // RNN_BLOCK_W, RNN_BLOCK_W, RNN_BLOCK_W)


def _split_regions(w_in):
    d = w_in.shape[0]
    rnn = w_in[:, :COL_RNN_END].reshape(d, 2, D_MODEL // RNN_TILE, RNN_TILE).transpose(0, 2, 1, 3)
    gate = w_in[:, COL_ATTN_END:].reshape(d, 2, D_MODEL // MERGE_COLS, MERGE_COLS).transpose(0, 2, 1, 3)
    return rnn.reshape(d, 2 * D_MODEL), w_in[:, COL_RNN_END:COL_ATTN_END], gate.reshape(d, 2 * D_MODEL)


def _join_regions(rnn, attn, gate):
    d = rnn.shape[0]
    rnn = rnn.reshape(d, D_MODEL // RNN_TILE, 2, RNN_TILE).transpose(0, 2, 1, 3).reshape(d, 2 * D_MODEL)
    gate = gate.reshape(d, D_MODEL // MERGE_COLS, 2, MERGE_COLS).transpose(0, 2, 1, 3).reshape(d, 2 * D_MODEL)
    return jnp.concatenate([rnn, attn, gate], axis=1)


def _local_step(x, p, target, w, *, n_seq, seq, comm=None):
    w = dict(w)

    def run(tag, fn, *args, **kwargs):
        hosted = comm.host(tag) if comm is not None else None
        if hosted is None:
            return fn(*args, **kwargs)
        *outs, landed = fn(*args, hosted=hosted, **kwargs)
        comm.landed(tag, landed, w)
        return outs[0] if len(outs) == 1 else outs

    def ready(batch, grads, extra=None):
        if comm is not None:
            comm.ready(batch, grads, extra)

    cos_t, sin_t = _rope_tables(seq)
    q_gain_t = jnp.tile(w["q_gain"], (1, N_Q_HEADS))
    k_gain_t = jnp.tile(w["k_gain"], (1, N_KV_HEADS))
    sinks_t = jnp.pad(w["sinks"], ((0, 0), (0, LANES - N_Q_HEADS)))
    wrg_bd, wig_bd = _block_diag_tiles(w["w_rg"]), _block_diag_tiles(w["w_ig"])
    dims = dict(n_seq=n_seq, seq=seq)

    h = _rmsnorm_fwd(x, w["g_mix"], name="norm_mix")
    w_regions = _split_regions(w["w_in"])
    z_rnn, z_attn, z_gate = (
        _matmul(h, wr, mode="nn", tm=1024, tn=1024, out_dtypes=[F32], name="mm_in_" + nm)
        for nm, wr in zip(("rnn", "attn", "gate"), w_regions))
    xc, hr, ya_in = run("rnn_fwd", _rnn_fwd, z_rnn, w["conv_w"], w["conv_b"], wrg_bd, w["b_rg"], wig_bd, w["b_ig"],
                        w["lru_lambda"], **dims)
    o, lse = run("attn_fwd", _attn_fwd, z_attn, cos_t, sin_t, q_gain_t, k_gain_t, sinks_t, **dims)
    ya = run("mm_rnn_proj", _matmul, ya_in, w["w_rnn_proj"], mode="nn", tm=1024, tn=1024, out_dtypes=[F32],
             name="mm_rnn_proj")
    yb = _matmul(o, w["w_attn_proj"], mode="nn", tm=1024, tn=1024, out_dtypes=[F32], name="mm_attn_proj")
    merged = _merge_fwd(z_gate, ya, yb)
    x1 = _matmul(merged, w["w_out"], mode="nn", tm=1024, tn=1024, out_dtypes=[F32], name="mm_out",
                 epilogue=lambda acc, res: (res + acc,), extras=(x,))
    hm = _rmsnorm_fwd(x1, w["g_mlp"], name="norm_mlp")
    act = _matmul(hm, w["w_up"], mode="nn", tm=1024, tn=1024, out_dtypes=[BF16], name="mm_up",
                  epilogue=lambda acc: (jnp.square(jnp.maximum(acc, 0.0)),))
    x2 = _matmul(act, w["w_down"], mode="nn", tm=512, tn=1024, out_dtypes=[F32], name="mm_down",
                 epilogue=lambda acc, res: (res + acc,), extras=(x1,))
    hp = _rmsnorm_fwd(x2, w["g_ple"], name="norm_ple")
    gt = _matmul(hp, w["w_ple_gate"], mode="nn", tm=1024, tn=1024, out_dtypes=[F32], name="mm_ple_gate")
    p_bf = p.astype(BF16)
    e = _matmul(p_bf, w["w_ple_proj"], mode="nn", tm=1024, tn=1024, out_dtypes=[F32], name="mm_ple_proj")
    loss_row, dx3, dgt, de = _loss_head(x2, e, gt, target)

    g = {}
    g["w_ple_proj"] = _matmul_tn(p_bf, de, tk=PLE_DIM, tn=1024, tt=1024, name="mm_d_ple_proj",
                                 slot_cols=D_MODEL // N_DEV)
    g["w_ple_gate"] = _matmul_tn(hp, dgt, tk=1024, tn=1024, tt=512, name="mm_d_ple_gate")
    dhp = _matmul(dgt, w["w_ple_gate"], mode="nt", tm=1024, tn=1024, out_dtypes=[F32], name="mm_dhp")
    dx2, dx2_bf, g["g_ple"] = _rmsnorm_bwd(x2, w["g_ple"], dhp, dx3, name="norm_ple_bwd", want_bf16=True)
    g["w_down"] = _matmul_tn(act, dx2_bf, tk=1024, tn=1024, tt=512, name="mm_d_down")
    du = _matmul(dx2_bf, w["w_down"], mode="nt", tm=1024, tn=1024, out_dtypes=[BF16], name="mm_dact",
                 epilogue=lambda acc, a: (acc * (2.0 * jnp.sqrt(a.astype(F32))),), extras=(act,))
    g["w_up"] = _matmul_tn(hm, du, tk=1024, tn=1024, tt=512, name="mm_d_up", slot_cols=D_FF // N_DEV)
    ready(1, g)
    dhm = run("mm_dhm", _matmul, du, w["w_up"], mode="nt", tm=512, tn=1024, out_dtypes=[F32], name="mm_dhm")
    dx1, dx1_bf, g["g_mlp"] = _rmsnorm_bwd(x1, w["g_mlp"], dhm, dx2, name="norm_mlp_bwd", want_bf16=True)
    g["w_out"] = _matmul_tn(merged, dx1_bf, tk=1024, tn=1024, tt=512, name="mm_d_out")
    dmerged = _matmul(dx1_bf, w["w_out"], mode="nt", tm=1024, tn=1024, out_dtypes=[F32], name="mm_dmerged")
    dya, dyb, dz_gate = _merge_bwd(z_gate, ya, yb, dmerged)
    g["w_rnn_proj"] = _matmul_tn(ya_in, dya, tk=1024, tn=1024, tt=512, name="mm_d_rnn_proj")
    g["w_attn_proj"] = _matmul_tn(o, dyb, tk=1024, tn=1024, tt=512, name="mm_d_attn_proj")
    ready(2, g)
    dya_in = run("mm_dya_in", _matmul, dya, w["w_rnn_proj"], mode="nt", tm=1024, tn=1024, out_dtypes=[F32],
                 name="mm_dya_in")
    do = _matmul(dyb, w["w_attn_proj"], mode="nt", tm=1024, tn=1024, out_dtypes=[BF16], name="mm_do")
    dz_rnn, dwrg_dense, dwig_dense, rnn_vec = run(
        "rnn_bwd", _rnn_bwd, dya_in, z_rnn, xc, hr, w["conv_w"], wrg_bd, w["b_rg"], wig_bd, w["b_ig"],
        w["lru_lambda"], **dims)
    dz_attn, attn_vec = run("attn_bwd", _attn_bwd, z_attn, o, lse, do, cos_t, sin_t, q_gain_t, k_gain_t, sinks_t,
                            **dims)
    dz_regions = (dz_rnn, dz_attn, dz_gate)
    g["w_in"] = _join_regions(*[
        _matmul_tn(h, dzr, tk=1024, tn=1024, tt=512, name="mm_d_in_" + nm)
        for nm, dzr in zip(("rnn", "attn", "gate"), dz_regions)])
    g["w_rg"] = _block_diag_extract(dwrg_dense)
    g["w_ig"] = _block_diag_extract(dwig_dense)
    g["b_rg"], g["b_ig"], g["lru_lambda"], g["conv_b"] = (rnn_vec[i:i + 1] for i in range(4))
    g["conv_w"] = rnn_vec[4:8]
    attn_vec = attn_vec[0] if n_seq == 1 else functools.reduce(jnp.add, [attn_vec[b] for b in range(n_seq)])
    g["q_gain"] = attn_vec[0].reshape(N_Q_HEADS, HEAD_DIM).sum(axis=0)[None, :]
    g["k_gain"] = attn_vec[1, :KV_W].reshape(N_KV_HEADS, HEAD_DIM).sum(axis=0)[None, :]
    g["sinks"] = attn_vec[2:3, :N_Q_HEADS]
    ready(3, g, {LOSS_ROW: loss_row})
    dh = run("mm_dh", _matmul, dz_regions, w_regions, mode="nt", tm=512, tn=1024, out_dtypes=[F32], name="mm_dh")
    grad_x, g["g_mix"] = run("norm_mix_bwd", _rmsnorm_bwd, x, w["g_mix"], dh, dx1, name="norm_mix_bwd",
                             want_bf16=False)
    return loss_row[0, 0], grad_x, g


MESH_ID = pl.DeviceIdType.MESH


def _coords(index):
    return (index >> 2) & 1, (index >> 1) & 1, index & 1


def _exchange(srcs, kinds, *, name):
    n = len(srcs)
    n_peer = N_DEV - 1

    def body(*refs):
        src, dst = refs[:n], refs[n:2 * n]
        send_sems, recv_sems, local_sems = refs[2 * n:]
        me = 4 * lax.axis_index("x") + 2 * lax.axis_index("y") + lax.axis_index("c")

        def remote(i, d):
            peer = (me + d) & (N_DEV - 1)
            piece = src[i] if kinds[i] == "gather" else src[i].at[peer]
            return pltpu.make_async_remote_copy(
                src_ref=piece, dst_ref=dst[i].at[me], send_sem=send_sems.at[i * n_peer + d - 1],
                recv_sem=recv_sems.at[i * n_peer + d - 1], device_id=_coords(peer), device_id_type=MESH_ID)

        def arrival(i, d):
            sender = (me - d) & (N_DEV - 1)
            piece = src[i] if kinds[i] == "gather" else src[i].at[sender]
            return pltpu.make_async_remote_copy(
                src_ref=piece, dst_ref=dst[i].at[sender], send_sem=send_sems.at[i * n_peer + d - 1],
                recv_sem=recv_sems.at[i * n_peer + d - 1], device_id=_coords(sender), device_id_type=MESH_ID)

        own = []
        for i in range(n):
            piece = src[i] if kinds[i] == "gather" else src[i].at[me]
            own.append(pltpu.make_async_copy(piece, dst[i].at[me], local_sems.at[i]))
            own[-1].start()
        sent = [remote(i, d) for d in range(1, N_DEV) for i in range(n)]
        for cp in sent:
            cp.start()
        for d in range(1, N_DEV):
            for i in range(n):
                arrival(i, d).wait_recv()
        for cp in sent:
            cp.wait_send()
        for cp in own:
            cp.wait()

    def out_of(s, kind):
        shape = s.shape if kind == "scatter" else (N_DEV,) + s.shape
        return jax.ShapeDtypeStruct(shape, s.dtype)

    any_spec = pl.BlockSpec(memory_space=pl.ANY)
    return pl.pallas_call(
        body,
        in_specs=[any_spec] * n,
        out_specs=[any_spec] * n,
        out_shape=[out_of(s, k) for s, k in zip(srcs, kinds)],
        scratch_shapes=[pltpu.SemaphoreType.DMA((n * n_peer,)), pltpu.SemaphoreType.DMA((n * n_peer,)),
                        pltpu.SemaphoreType.DMA((n,))],
        compiler_params=pltpu.CompilerParams(has_side_effects=True),
        name=name,
    )(*srcs)


def _remote(src, dst, send_sem, recv_sem, to):
    return pltpu.make_async_remote_copy(src_ref=src, dst_ref=dst, send_sem=send_sem, recv_sem=recv_sem,
                                        device_id=to, device_id_type=MESH_ID)


def _gather_two_level(shards, *, name):
    n = len(shards)
    per = N_DEV - 1

    def body(*refs):
        src, dst = refs[:n], refs[n:2 * n]
        send_sems, recv_sems, local_sems = refs[2 * n:]
        x, y, c = lax.axis_index("x"), lax.axis_index("y"), lax.axis_index("c")
        me, sibling = (x, y, c), (x, y, 1 - c)
        chips = [(1 - x, y), (x, 1 - y), (1 - x, 1 - y)]

        def slot(pos):
            return 4 * pos[0] + 2 * pos[1] + pos[2]

        def copy(i, k, block, to, from_shard=False):
            source = src[i] if from_shard else dst[i].at[slot(block)]
            return _remote(source, dst[i].at[slot(block)], send_sems.at[i * per + k], recv_sems.at[i * per + k], to)

        mine = [pltpu.make_async_copy(src[i], dst[i].at[slot(me)], local_sems.at[i]) for i in range(n)]
        for cp in mine:
            cp.start()
        first = []
        for i in range(n):
            first.append(copy(i, 0, me, sibling, from_shard=True))
            first += [copy(i, 1 + j, me, (*chip, c), from_shard=True) for j, chip in enumerate(chips)]
        for cp in first:
            cp.start()
        passed = []
        for i in range(n):
            for j, chip in enumerate(chips):
                copy(i, 1 + j, (*chip, c), me).wait_recv()
                passed.append(copy(i, 4 + j, (*chip, c), sibling))
                passed[-1].start()
        for i in range(n):
            copy(i, 0, sibling, me).wait_recv()
            for j, chip in enumerate(chips):
                copy(i, 4 + j, (*chip, 1 - c), me).wait_recv()
        for cp in first + passed:
            cp.wait_send()
        for cp in mine:
            cp.wait()

    any_spec = pl.BlockSpec(memory_space=pl.ANY)
    return pl.pallas_call(
        body,
        in_specs=[any_spec] * n,
        out_specs=[any_spec] * n,
        out_shape=[jax.ShapeDtypeStruct((N_DEV,) + s.shape, s.dtype) for s in shards],
        scratch_shapes=[pltpu.SemaphoreType.DMA((n * per,)), pltpu.SemaphoreType.DMA((n * per,)),
                        pltpu.SemaphoreType.DMA((n,))],
        name=name,
    )(*shards)


CHIPS = N_DEV // 2


def _other_chips(x, y):
    return [(x, 1 - y), (1 - x, y), (1 - x, 1 - y)]


def _hosted_gather_first(shards):
    n = len(shards)
    per = CHIPS

    def plan(src, dst, send_sems, recv_sems, local_sems, first_sem):
        x, y, c = lax.axis_index("x"), lax.axis_index("y"), lax.axis_index("c")
        peers = [(x, y, 1 - c)] + [(*chip, c) for chip in _other_chips(x, y)]
        copies = []
        for i in range(n):
            own = pltpu.make_async_copy(src[i], dst[i].at[4 * x + 2 * y + c], local_sems.at[first_sem + i])
            copies.append(_Xfer(own.start, own.wait))
        for j, peer in enumerate(peers):
            for i in range(n):
                k = first_sem + i * per + j
                out = _remote(src[i], dst[i].at[4 * x + 2 * y + c], send_sems.at[k], recv_sems.at[k], peer)
                arrival = _remote(src[i], dst[i].at[4 * peer[0] + 2 * peer[1] + peer[2]], send_sems.at[k],
                                  recv_sems.at[k], peer)

                def wait(out=out, arrival=arrival):
                    arrival.wait_recv()
                    out.wait_send()

                copies.append(_Xfer(out.start, wait))
        return copies

    out_shape = tuple(jax.ShapeDtypeStruct((N_DEV,) + s.shape, s.dtype) for s in shards)
    return _Hosted(tuple(shards), out_shape, n * per, plan)


def _hosted_gather_second(landed):
    n = len(landed)
    per = CHIPS - 1

    def plan(src, dst, send_sems, recv_sems, local_sems, first_sem):
        x, y, c = lax.axis_index("x"), lax.axis_index("y"), lax.axis_index("c")
        copies = []
        for j, chip in enumerate(_other_chips(x, y)):
            mine, theirs = 4 * chip[0] + 2 * chip[1] + c, 4 * chip[0] + 2 * chip[1] + 1 - c
            for i in range(n):
                k = first_sem + i * per + j
                out = _remote(src[i].at[mine], dst[i].at[mine], send_sems.at[k], recv_sems.at[k], (x, y, 1 - c))
                arrival = _remote(src[i].at[theirs], dst[i].at[theirs], send_sems.at[k], recv_sems.at[k],
                                  (x, y, 1 - c))

                def wait(out=out, arrival=arrival):
                    arrival.wait_recv()
                    out.wait_send()

                copies.append(_Xfer(out.start, wait))
        return copies

    out_shape = tuple(jax.ShapeDtypeStruct(a.shape, a.dtype) for a in landed)
    return _Hosted(tuple(landed), out_shape, n * per, plan, tuple((i, i) for i in range(n)))


def _hosted_sibling_swap(arrays, sliced):
    n_sems = sum(CHIPS if s else 1 for s in sliced)

    def plan(src, dst, send_sems, recv_sems, local_sems, first_sem):
        x, y, c = lax.axis_index("x"), lax.axis_index("y"), lax.axis_index("c")
        sibling = (x, y, 1 - c)
        copies, k = [], first_sem
        for i, is_sliced in enumerate(sliced):
            pieces = [(src[i].at[2 * s + 1 - c], dst[i].at[s]) for s in range(CHIPS)] if is_sliced else [(src[i], dst[i])]
            for source, target in pieces:
                cp = _remote(source, target, send_sems.at[k], recv_sems.at[k], sibling)
                copies.append(_Xfer(cp.start, cp.wait))
                k += 1
        return copies

    out_shape = tuple(jax.ShapeDtypeStruct((CHIPS,) + a.shape[1:] if s else a.shape, a.dtype)
                      for a, s in zip(arrays, sliced))
    return _Hosted(tuple(arrays), out_shape, n_sems, plan)


def _hosted_chip_exchange(arrays, sliced):
    n = len(arrays)
    per = CHIPS - 1

    def plan(src, dst, send_sems, recv_sems, local_sems, first_sem):
        x, y, c = lax.axis_index("x"), lax.axis_index("y"), lax.axis_index("c")
        chip = 2 * x + y
        copies = []
        for i in range(n):
            own = pltpu.make_async_copy(src[i].at[chip] if sliced[i] else src[i], dst[i].at[chip],
                                        local_sems.at[first_sem + i])
            copies.append(_Xfer(own.start, own.wait))
        for d in range(1, CHIPS):
            other = chip ^ d
            to = ((other >> 1) & 1, other & 1, c)
            for i in range(n):
                k = first_sem + i * per + d - 1
                source = src[i].at[other] if sliced[i] else src[i]
                out = _remote(source, dst[i].at[chip], send_sems.at[k], recv_sems.at[k], to)
                arrival = _remote(source, dst[i].at[other], send_sems.at[k], recv_sems.at[k], to)

                def wait(out=out, arrival=arrival):
                    arrival.wait_recv()
                    out.wait_send()

                copies.append(_Xfer(out.start, wait))
        return copies

    out_shape = tuple(jax.ShapeDtypeStruct(a.shape if s else (CHIPS,) + a.shape, a.dtype)
                      for a, s in zip(arrays, sliced))
    return _Hosted(tuple(arrays), out_shape, n * per, plan)


def _add_sibling(parts, received, core, *, name):
    _, r, cols = parts.shape
    tr = min(256, r)

    def body(core_ref, a_ref, b_ref, o_ref):
        o_ref[...] = (a_ref[...] + b_ref[...]).astype(BF16)

    grid_spec = pltpu.PrefetchScalarGridSpec(
        num_scalar_prefetch=1,
        grid=(CHIPS, r // tr),
        in_specs=[pl.BlockSpec((None, tr, cols), lambda k, i, core_ref: (2 * k + core_ref[0], i, 0)),
                  pl.BlockSpec((None, tr, cols), lambda k, i, core_ref: (k, i, 0))],
        out_specs=pl.BlockSpec((None, tr, cols), lambda k, i, core_ref: (k, i, 0)),
    )
    return pl.pallas_call(body, grid_spec=grid_spec, out_shape=jax.ShapeDtypeStruct((CHIPS, r, cols), BF16),
                          compiler_params=_params("parallel", "parallel"), name=name)(core, parts, received)


def _add_whole(a, b, *, name):
    def body(a_ref, b_ref, o_ref):
        o_ref[...] = a_ref[...] + b_ref[...]

    return pl.pallas_call(body, out_shape=jax.ShapeDtypeStruct(a.shape, F32), name=name)(a, b)


def _adamw(parts, w, m, v, *, name):
    r, c = w.shape
    n_parts = parts.shape[0]
    tr = min(256, r)
    c1 = 1.0 - ADAM_B1 ** ADAM_STEP
    c2 = 1.0 - ADAM_B2 ** ADAM_STEP

    def body(p_ref, w_ref, m_ref, v_ref, g_ref, d_ref, nm_ref, nv_ref):
        g = p_ref[0].astype(F32)
        for s in range(1, n_parts):
            g = g + p_ref[s].astype(F32)
        nm = ADAM_B1 * m_ref[...] + (1.0 - ADAM_B1) * g
        nv = ADAM_B2 * v_ref[...] + (1.0 - ADAM_B2) * (g * g)
        g_ref[...] = g
        nm_ref[...] = nm
        nv_ref[...] = nv
        d_ref[...] = -ADAM_LR * ((nm / c1) / (jnp.sqrt(nv / c2) + ADAM_EPS) + ADAM_WD * w_ref[...])

    tile = pl.BlockSpec((tr, c), lambda i: (i, 0))
    return pl.pallas_call(
        body,
        grid=(r // tr,),
        in_specs=[pl.BlockSpec((n_parts, tr, c), lambda i: (0, i, 0)), tile, tile, tile],
        out_specs=[tile] * 4,
        out_shape=[jax.ShapeDtypeStruct((r, c), F32)] * 4,
        compiler_params=_params("parallel"),
        name=name,
    )(parts, w, m, v)


BIG = ("w_in", "w_rnn_proj", "w_attn_proj", "w_out", "w_up", "w_down", "w_ple_gate", "w_ple_proj")
LOSS_ROW = "loss"
SMALL = (("conv_b", 1), ("b_rg", 1), ("b_ig", 1), ("lru_lambda", 1), ("g_mlp", 1), ("g_ple", 1),
         ("q_gain", 1), ("k_gain", 1), ("sinks", 1), (LOSS_ROW, 1), ("w_rg", 64), ("w_ig", 64))
SMALL_ROWS = 144
ROW_SHARDED = ("w_rnn_proj", "w_attn_proj", "w_out", "w_down", "w_ple_gate")
COL_SHARDED = ("w_in", "w_up", "w_ple_proj")
BATCHES = {1: ("w_ple_proj", "w_ple_gate", "w_down", "w_up"), 2: ("w_out", "w_rnn_proj", "w_attn_proj"),
           3: ("w_in", "conv_w")}
SMALL_BATCH = 4


def _pack_small(vals):
    rows = []
    for nm, nrow in SMALL:
        flat = vals[nm].reshape(-1).astype(F32)
        rows.append(jnp.pad(flat, (0, nrow * D_MODEL - flat.shape[0])).reshape(nrow, D_MODEL))
    used = sum(nrow for _, nrow in SMALL)
    rows.append(jnp.zeros((SMALL_ROWS - used, D_MODEL), F32))
    return jnp.concatenate(rows, axis=0)


def _unpack_small(packed, shapes):
    out, at = {}, 0
    for nm, nrow in SMALL:
        size = 1
        for s in shapes[nm]:
            size *= s
        out[nm] = packed[at:at + nrow].reshape(-1)[:size].reshape(shapes[nm])
        at += nrow
    return out


def _full_weight(name, landed):
    if name in COL_SHARDED:
        return landed.transpose(1, 0, 2).reshape(landed.shape[1], N_DEV * landed.shape[2])
    return landed.reshape(N_DEV * landed.shape[1], landed.shape[2])


def _owner_slots(name, grad):
    if name == "w_in":
        return grad.reshape(D_MODEL, N_DEV, IN_TOTAL // N_DEV).transpose(1, 0, 2)
    if name == "conv_w":
        return grad.reshape(CONV_W, N_DEV, D_MODEL // N_DEV).transpose(1, 0, 2)
    if name in COL_SHARDED:
        return grad
    return grad.reshape(N_DEV, grad.shape[0] // N_DEV, grad.shape[1])


class _StepExchanges:
    FIRST, SECOND = "first", "second"
    EARLY, MID, LATE = ("w_rnn_proj", "w_attn_proj", "w_out"), ("w_up",), ("w_down", "w_ple_gate", "w_ple_proj")
    GATHERS = {"rnn_fwd": ((FIRST, EARLY), (FIRST, MID)),
               "attn_fwd": ((SECOND, EARLY), (SECOND, MID), (FIRST, LATE)), "mm_rnn_proj": ((SECOND, LATE),)}
    SWAPS = {"mm_dhm": 1, "mm_dya_in": 2}
    CHIP_EXCHANGES = {"rnn_bwd": 1, "attn_bwd": 2, "mm_dh": 3, "norm_mix_bwd": SMALL_BATCH}

    def __init__(self, shards, core):
        self.shards = shards
        self.core = core
        self.parts, self.swapped, self.summed, self.half_gathered = {}, {}, {}, {}

    def ready(self, batch, grads, extra=None):
        arrays = [_owner_slots(nm, grads[nm]) for nm in BATCHES[batch]]
        self.parts[batch] = (arrays, [True] * len(arrays))
        if batch not in self.SWAPS.values():
            self.parts[SMALL_BATCH] = ([_pack_small({**grads, **extra})], [False])
            both = [a + b for a, b in zip(self.parts[batch], self.parts[SMALL_BATCH])]
            _, swapped = _call(
                lambda: None, grid=(1,), in_specs=[], out_specs=[], out_shape=[], args=(), name="swap_last",
                semantics=("arbitrary",), hosted=_hosted_sibling_swap(*both))
            self.swapped[batch], self.swapped[SMALL_BATCH] = swapped[:-1], swapped[-1:]

    def host(self, tag):
        if tag in self.GATHERS:
            return _merge_hosted([
                _hosted_gather_first([self.shards[nm] for nm in group]) if half == self.FIRST
                else _hosted_gather_second([self.half_gathered[nm] for nm in group])
                for half, group in self.GATHERS[tag]])
        if tag in self.SWAPS:
            return _hosted_sibling_swap(*self.parts[self.SWAPS[tag]])
        if tag in self.CHIP_EXCHANGES:
            batch = self.CHIP_EXCHANGES[tag]
            arrays, sliced = self.parts[batch]
            labels = BATCHES.get(batch, ("small",))
            sums = [_add_sibling(a, r, self.core, name="add_" + lb) if s else _add_whole(a, r, name="add_" + lb)
                    for a, r, s, lb in zip(arrays, self.swapped[batch], sliced, labels)]
            return _hosted_chip_exchange(sums, sliced)
        return None

    def landed(self, tag, landed, weights):
        if tag in self.GATHERS:
            names = [(half, nm) for half, group in self.GATHERS[tag] for nm in group]
            for (half, nm), buf in zip(names, landed):
                if half == self.FIRST:
                    self.half_gathered[nm] = buf
                else:
                    weights[nm] = _full_weight(nm, buf)
        elif tag in self.SWAPS:
            self.swapped[self.SWAPS[tag]] = landed
        else:
            self.summed[self.CHIP_EXCHANGES[tag]] = landed


def kernel(x, p, g_mix, w_in, conv_w, conv_b, w_rg, b_rg, w_ig, b_ig, lru_lambda, w_rnn_proj, q_gain, k_gain, sinks, w_attn_proj, w_out, g_mlp, w_up, w_down, g_ple, w_ple_gate, w_ple_proj, loss_target, m_g_mix, m_w_in, m_conv_w, m_conv_b, m_w_rg, m_b_rg, m_w_ig, m_b_ig, m_lru_lambda, m_w_rnn_proj, m_q_gain, m_k_gain, m_sinks, m_w_attn_proj, m_w_out, m_g_mlp, m_w_up, m_w_down, m_g_ple, m_w_ple_gate, m_w_ple_proj, v_g_mix, v_w_in, v_conv_w, v_conv_b, v_w_rg, v_b_rg, v_w_ig, v_b_ig, v_lru_lambda, v_w_rnn_proj, v_q_gain, v_k_gain, v_sinks, v_w_attn_proj, v_w_out, v_g_mlp, v_w_up, v_w_down, v_g_ple, v_w_ple_gate, v_w_ple_proj):
    names = ("g_mix", "w_in", "conv_w", "conv_b", "w_rg", "b_rg", "w_ig", "b_ig", "lru_lambda", "w_rnn_proj",
             "q_gain", "k_gain", "sinks", "w_attn_proj", "w_out", "g_mlp", "w_up", "w_down", "g_ple",
             "w_ple_gate", "w_ple_proj")
    wts = dict(zip(names, (g_mix, w_in, conv_w, conv_b, w_rg, b_rg, w_ig, b_ig, lru_lambda, w_rnn_proj, q_gain,
                           k_gain, sinks, w_attn_proj, w_out, g_mlp, w_up, w_down, g_ple, w_ple_gate, w_ple_proj)))
    mom1 = dict(zip(names, (m_g_mix, m_w_in, m_conv_w, m_conv_b, m_w_rg, m_b_rg, m_w_ig, m_b_ig, m_lru_lambda,
                            m_w_rnn_proj, m_q_gain, m_k_gain, m_sinks, m_w_attn_proj, m_w_out, m_g_mlp, m_w_up,
                            m_w_down, m_g_ple, m_w_ple_gate, m_w_ple_proj)))
    mom2 = dict(zip(names, (v_g_mix, v_w_in, v_conv_w, v_conv_b, v_w_rg, v_b_rg, v_w_ig, v_b_ig, v_lru_lambda,
                            v_w_rnn_proj, v_q_gain, v_k_gain, v_sinks, v_w_attn_proj, v_w_out, v_g_mlp, v_w_up,
                            v_w_down, v_g_ple, v_w_ple_gate, v_w_ple_proj)))
    n_seq, seq, _ = x.shape
    core = lax.axis_index("c").astype(jnp.int32).reshape(1)

    shards = {nm: wts[nm][0].astype(BF16) for nm in BIG}
    w_in_all, conv_all = _gather_two_level([shards["w_in"], conv_w[0]], name="gather_w_in")
    w = {nm: wts[nm] for nm in names if nm not in BIG}
    w["w_rg"], w["w_ig"] = w_rg[0], w_ig[0]
    w["conv_w"] = conv_all.transpose(1, 0, 2).reshape(CONV_W, D_MODEL)
    w["w_in"] = _full_weight("w_in", w_in_all)
    comm = _StepExchanges(shards, core)
    loss_sum, grad_x, g = _local_step(
        x.reshape(n_seq * seq, D_MODEL), p.reshape(n_seq * seq, PLE_DIM), loss_target.reshape(n_seq * seq, D_MODEL),
        w, n_seq=n_seq, seq=seq, comm=comm)
    del loss_sum

    res = {}
    for batch, batch_names in BATCHES.items():
        for nm, summed in zip(batch_names, comm.summed[batch]):
            res[nm] = _adamw(summed, wts[nm][0], mom1[nm][0], mom2[nm][0], name="adamw_" + nm)
    g_mix_parts, = _exchange([g["g_mix"]], ["gather"], name="gather_g_mix")
    res["g_mix"] = [r[0] for r in _adamw(g_mix_parts, g_mix, m_g_mix, v_g_mix, name="adamw_g_mix")]
    small_names = [nm for nm, _ in SMALL if nm != LOSS_ROW]
    full_small = {}
    for src, key in ((wts, "w"), (mom1, "m"), (mom2, "v")):
        vals = {nm: src[nm][0] for nm in small_names}
        vals[LOSS_ROW] = jnp.zeros((1,), F32)
        full_small[key] = _pack_small(vals)
    small_res = _adamw(comm.summed[SMALL_BATCH][0],full_small["w"], full_small["m"], full_small["v"], name="adamw_small")
    shapes = {nm: wts[nm].shape[1:] for nm in small_names}
    shapes[LOSS_ROW] = (1,)
    small_out = [_unpack_small(r, shapes) for r in small_res]
    for nm in small_names:
        res[nm] = [so[nm] for so in small_out]
    loss = small_out[0][LOSS_ROW][0] * (0.5 / D_MODEL)

    outs = [loss, grad_x.reshape(n_seq, seq, D_MODEL)]
    for k in range(4):
        outs.extend(res[nm][k][None] for nm in names)
    return tuple(outs)
```

```python
import functools
from typing import Callable, NamedTuple

import jax
import jax.numpy as jnp
from jax import lax
from jax.experimental import pallas as pl
from jax.experimental.pallas import tpu as pltpu

F32 = jnp.float32
BF16 = jnp.bfloat16

N_DEV = 8
D_MODEL = 1024
RNN_BLOCK_W = 64
CONV_W = 4
LRU_C = 8.0
HEAD_DIM = 64
N_Q_HEADS = 16
N_KV_HEADS = 4
KV_W = N_KV_HEADS * HEAD_DIM
WINDOW = 128
ROPE_THETA = 10000.0
D_FF = 4096
PLE_DIM = 256
NORM_EPS = 1e-6
IN_TOTAL = 5632
COL_RNN_END, COL_ATTN_END = 2048, 3584
ATTN_W = COL_ATTN_END - COL_RNN_END
ATTN_K_AT, ATTN_V_AT = 1024, 1280

ADAM_LR = 0.001
ADAM_B1 = 0.9
ADAM_B2 = 0.999
ADAM_EPS = 1e-08
ADAM_WD = 0.01
ADAM_STEP = 10

LANES = 128
SUBLANES = 8
RNN_TILE = 256
VMEM_LIMIT = 48 * 1024 * 1024
NEG_BIG = -1e30


def _params(*sem):
    return pltpu.CompilerParams(dimension_semantics=sem if sem else None, vmem_limit_bytes=VMEM_LIMIT)


def _sig(x):
    return 0.5 * jnp.tanh(0.5 * x) + 0.5


def _dot_nt(a, b):
    return lax.dot_general(a, b, (((1,), (1,)), ((), ())), preferred_element_type=F32)


def _dot_tn(a, b):
    return lax.dot_general(a, b, (((0,), (0,)), ((), ())), preferred_element_type=F32)


class _Xfer:
    def __init__(self, start, wait):
        self.start, self.wait = start, wait


class _Hosted(NamedTuple):
    srcs: tuple
    out_shape: tuple
    n_sems: int
    plan: Callable
    aliases: tuple = ()


def _merge_hosted(parts):
    parts = [p for p in parts if p is not None]
    if len(parts) <= 1:
        return parts[0] if parts else None
    src_at, dst_at, sem_at, aliases = [0], [0], [0], []
    for p in parts:
        aliases += [(i + src_at[-1], j + dst_at[-1]) for i, j in p.aliases]
        src_at.append(src_at[-1] + len(p.srcs))
        dst_at.append(dst_at[-1] + len(p.out_shape))
        sem_at.append(sem_at[-1] + p.n_sems)

    def plan(src, dst, send_sems, recv_sems, local_sems, first_sem):
        copies = []
        for k, p in enumerate(parts):
            copies += p.plan(src[src_at[k]:src_at[k + 1]], dst[dst_at[k]:dst_at[k + 1]], send_sems, recv_sems,
                             local_sems, first_sem + sem_at[k])
        return copies

    return _Hosted(tuple(a for p in parts for a in p.srcs), tuple(s for p in parts for s in p.out_shape),
                   sem_at[-1], plan, tuple(aliases))


def _call(body, *, grid, in_specs, out_specs, out_shape, args, name, semantics, scratch_shapes=(), hosted=None):
    if hosted is None:
        outs = pl.pallas_call(body, grid=grid, in_specs=list(in_specs), out_specs=list(out_specs),
                              out_shape=list(out_shape), scratch_shapes=list(scratch_shapes),
                              compiler_params=_params(*semantics), name=name)(*args)
        return list(outs), []
    counts = (len(in_specs), len(hosted.srcs), len(out_specs), len(hosted.out_shape), len(scratch_shapes), 3)

    def wrapped(*refs):
        at, groups = 0, []
        for count in counts:
            groups.append(refs[at:at + count])
            at += count
        ins, srcs, outs, dsts, scratch, sems = groups
        copies = hosted.plan(srcs, dsts, *sems, 0)
        ids = [pl.program_id(axis) for axis in range(len(grid))]
        first = functools.reduce(jnp.logical_and, [i == 0 for i in ids])
        last = functools.reduce(jnp.logical_and, [i == g - 1 for i, g in zip(ids, grid)])

        @pl.when(first)
        def _():
            for cp in copies:
                cp.start()

        body(*ins, *outs, *scratch)

        @pl.when(last)
        def _():
            for cp in copies:
                cp.wait()

    any_spec = pl.BlockSpec(memory_space=pl.ANY)
    sems = [pltpu.SemaphoreType.DMA((hosted.n_sems,))] * 3
    outs = pl.pallas_call(
        wrapped, grid=grid, in_specs=list(in_specs) + [any_spec] * counts[1],
        out_specs=list(out_specs) + [any_spec] * counts[3], out_shape=list(out_shape) + list(hosted.out_shape),
        scratch_shapes=list(scratch_shapes) + sems, compiler_params=_params(*["arbitrary"] * len(grid)),
        input_output_aliases={counts[0] + i: counts[2] + j for i, j in hosted.aliases},
        name=name)(*args, *hosted.srcs)
    return list(outs[:counts[2]]), list(outs[counts[2]:])


def _dividing_tile(n, want):
    tile = min(want, n)
    while n % tile:
        tile -= LANES
    return tile


def _matmul(a, b, *, mode, tm, tn, out_dtypes, name, epilogue=None, extras=(), hosted=None, b_cols=None):
    a_parts = tuple(a) if isinstance(a, (tuple, list)) else (a,)
    b_parts = tuple(b) if isinstance(b, (tuple, list)) else (b,)
    assert len(a_parts) == len(b_parts) and (mode == "nt" or len(a_parts) == 1)
    n_parts = len(a_parts)
    m = a_parts[0].shape[0]
    if b_cols is None:
        b_cols = [(0, bp.shape[1]) for bp in b_parts]
    n = b_cols[0][1] if mode == "nn" else b_parts[0].shape[0]
    tm, tn = min(tm, m), _dividing_tile(n, tn)
    n_extra = len(extras)

    def body(*refs):
        a_refs, b_refs = refs[:n_parts], refs[n_parts:2 * n_parts]
        rest = refs[2 * n_parts:]
        extra_refs, out_refs = rest[:n_extra], rest[n_extra:]
        if mode == "nn":
            acc = jnp.dot(a_refs[0][...], b_refs[0][...], preferred_element_type=F32)
        else:
            acc = _dot_nt(a_refs[0][...], b_refs[0][...])
            for a_ref, b_ref in zip(a_refs[1:], b_refs[1:]):
                acc = acc + _dot_nt(a_ref[...], b_ref[...])
        res = epilogue(acc, *[e[...] for e in extra_refs]) if epilogue is not None else (acc,)
        for o_ref, r in zip(out_refs, res):
            o_ref[...] = r.astype(o_ref.dtype)

    a_specs = [pl.BlockSpec((tm, ap.shape[1]), lambda i, j: (i, 0)) for ap in a_parts]
    if mode == "nn":
        assert b_cols[0][0] % tn == 0
        first = b_cols[0][0] // tn
        b_specs = [pl.BlockSpec((b_parts[0].shape[0], tn), lambda i, j: (0, first + j))]
    else:
        assert all(at % width == 0 for at, width in b_cols)
        b_specs = [pl.BlockSpec((tn, width), functools.partial(lambda i, j, blk: (j, blk), blk=at // width))
                   for at, width in b_cols]
    tile = pl.BlockSpec((tm, tn), lambda i, j: (i, j))
    outs, landed = _call(
        body,
        grid=(m // tm, n // tn),
        in_specs=a_specs + b_specs + [tile] * n_extra,
        out_specs=[tile] * len(out_dtypes),
        out_shape=[jax.ShapeDtypeStruct((m, n), dt) for dt in out_dtypes],
        args=(*a_parts, *b_parts, *extras), name=name, semantics=("parallel", "arbitrary"), hosted=hosted)
    if hosted is not None:
        return (*outs, landed)
    return outs[0] if len(outs) == 1 else outs


def _matmul_tn(a, b, *, tk, tn, tt, name, slot_cols=None):
    t, k = a.shape
    n = b.shape[1]
    tk, tn, tt = min(tk, k), _dividing_tile(n, tn), min(tt, t)

    def body(a_ref, b_ref, o_ref):
        @pl.when(pl.program_id(2) == 0)
        def _():
            o_ref[...] = jnp.zeros_like(o_ref)

        if slot_cols is None:
            o_ref[...] += _dot_tn(a_ref[...], b_ref[...])
        else:
            av = a_ref[...]
            for s in range(tn // slot_cols):
                o_ref[s] += _dot_tn(av, b_ref[:, s * slot_cols:(s + 1) * slot_cols])

    if slot_cols is not None:
        out_spec = pl.BlockSpec((tn // slot_cols, tk, slot_cols), lambda i, j, s: (j, i, 0))
        out_shape = jax.ShapeDtypeStruct((n // slot_cols, k, slot_cols), F32)
    else:
        out_spec = pl.BlockSpec((tk, tn), lambda i, j, s: (i, j))
        out_shape = jax.ShapeDtypeStruct((k, n), F32)
    return pl.pallas_call(
        body,
        grid=(k // tk, n // tn, t // tt),
        in_specs=[pl.BlockSpec((tt, tk), lambda i, j, s: (s, i)), pl.BlockSpec((tt, tn), lambda i, j, s: (s, j))],
        out_specs=out_spec,
        out_shape=out_shape,
        compiler_params=_params("parallel", "parallel", "arbitrary"),
        name=name,
    )(a, b)


def _matmul_tn_multi(a, bs, *, tt, name):
    t, k = a.shape
    tt = min(tt, t)
    n_b = len(bs)

    def body(a_ref, *refs):
        b_refs, o_refs = refs[:n_b], refs[n_b:]

        @pl.when(pl.program_id(0) == 0)
        def _():
            for o_ref in o_refs:
                o_ref[...] = jnp.zeros_like(o_ref)

        a_t = a_ref[...].T
        for b_ref, o_ref in zip(b_refs, o_refs):
            o_ref[...] += jnp.dot(a_t, b_ref[...], preferred_element_type=F32)

    return pl.pallas_call(
        body,
        grid=(t // tt,),
        in_specs=[pl.BlockSpec((tt, k), lambda s: (s, 0))] + [pl.BlockSpec((tt, b.shape[1]), lambda s: (s, 0)) for b in bs],
        out_specs=[pl.BlockSpec((k, b.shape[1]), lambda s: (0, 0)) for b in bs],
        out_shape=[jax.ShapeDtypeStruct((k, b.shape[1]), F32) for b in bs],
        compiler_params=_params("arbitrary"),
        name=name,
    )(a, *bs)


def _rmsnorm_fwd(x, g, *, name):
    t, d = x.shape
    tm = min(512, t)

    def body(x_ref, g_ref, o_ref):
        xv = x_ref[...]
        r = lax.rsqrt(jnp.mean(xv * xv, axis=-1, keepdims=True) + NORM_EPS)
        o_ref[...] = (xv * r * g_ref[...]).astype(BF16)

    return pl.pallas_call(
        body,
        grid=(t // tm,),
        in_specs=[pl.BlockSpec((tm, d), lambda i: (i, 0)), pl.BlockSpec((1, d), lambda i: (0, 0))],
        out_specs=pl.BlockSpec((tm, d), lambda i: (i, 0)),
        out_shape=jax.ShapeDtypeStruct((t, d), BF16),
        compiler_params=_params("parallel"),
        name=name,
    )(x, g)


def _rmsnorm_bwd(x, g, dy, dres, *, name, want_bf16, hosted=None):
    t, d = x.shape
    tm = min(256, t)

    def body(x_ref, g_ref, dy_ref, dres_ref, *out_refs):
        dx_ref, dg_ref = out_refs[0], out_refs[-1]
        xv, dyv = x_ref[...], dy_ref[...]
        r = lax.rsqrt(jnp.mean(xv * xv, axis=-1, keepdims=True) + NORM_EPS)
        xr = xv * r
        gy = dyv * g_ref[...]
        dx = dres_ref[...] + r * (gy - xr * jnp.mean(gy * xr, axis=-1, keepdims=True))
        dx_ref[...] = dx
        if want_bf16:
            out_refs[1][...] = dx.astype(BF16)

        @pl.when(pl.program_id(0) == 0)
        def _():
            dg_ref[...] = jnp.zeros_like(dg_ref)

        dg_ref[...] += jnp.sum(dyv * xr, axis=0, keepdims=True)

    tile = pl.BlockSpec((tm, d), lambda i: (i, 0))
    vec = pl.BlockSpec((1, d), lambda i: (0, 0))
    out_specs = [tile] + ([tile] if want_bf16 else []) + [vec]
    out_shape = [jax.ShapeDtypeStruct((t, d), F32)] + ([jax.ShapeDtypeStruct((t, d), BF16)] if want_bf16 else [])
    out_shape.append(jax.ShapeDtypeStruct((1, d), F32))
    outs, landed = _call(body, grid=(t // tm,), in_specs=[tile, vec, tile, tile], out_specs=out_specs,
                         out_shape=out_shape, args=(x, g, dy, dres), name=name, semantics=("arbitrary",), hosted=hosted)
    return (*outs, landed) if hosted is not None else outs


def _softplus_neg(lam):
    z = -lam
    return jnp.maximum(z, 0.0) + jnp.log1p(jnp.exp(-jnp.abs(z)))


def _neg_expm1(y, exp_half_y):
    series = -y * (1.0 + y * 0.5 * (1.0 + y * (1.0 / 3.0) * (1.0 + y * 0.25 * (1.0 + y * 0.2))))
    return jnp.where(y > -0.0625, series, 1.0 - exp_half_y * exp_half_y)


def _gelu_parts(x):
    c = 0.7978845608028654
    u = c * (x + 0.044715 * x * x * x)
    th = jnp.tanh(u)
    gel = 0.5 * x * (1.0 + th)
    dgel = 0.5 * (1.0 + th) + 0.5 * x * (1.0 - th * th) * c * (1.0 + 3.0 * 0.044715 * x * x)
    return gel, dgel


def _shift_down(v, k, rows):
    return jnp.where(rows < k, 0.0, pltpu.roll(v, k, 0))


def _shift_up(v, k, rows, n):
    return jnp.where(rows >= n - k, 0.0, pltpu.roll(v, n - k, 0))


def _scan_within_groups(a, b, rows, *, reverse):
    n = a.shape[0]
    in_group = rows & (SUBLANES - 1)
    for s in (1, 2, 4):
        if reverse:
            inside, shift = in_group < SUBLANES - s, n - s
        else:
            inside, shift = in_group >= s, s
        b = b + a * jnp.where(inside, pltpu.roll(b, shift, 0), 0.0)
        a = a * jnp.where(inside, pltpu.roll(a, shift, 0), 1.0)
    return a, b


def _rnn_gates(xc, wrg, brg, wig, big, lam):
    xcb = xc.astype(BF16)
    r = _sig(jnp.dot(xcb, wrg, preferred_element_type=F32) + brg)
    i = _sig(jnp.dot(xcb, wig, preferred_element_type=F32) + big)
    sp = _softplus_neg(lam)
    log_a = -LRU_C * r * sp
    a = jnp.exp(log_a)
    mult = jnp.sqrt(_neg_expm1(2.0 * log_a, a))
    return xcb, r, i, sp, a, mult


def _conv_fwd(xv, cw, cb, rows):
    return (cb + _shift_down(xv, 3, rows) * cw[0:1, :] + _shift_down(xv, 2, rows) * cw[1:2, :]
            + _shift_down(xv, 1, rows) * cw[2:3, :] + xv * cw[3:4, :])


def _rnn_fwd(z, conv_w, conv_b, wrg_bd, b_rg, wig_bd, b_ig, lam, *, n_seq, seq, hosted=None):
    t = n_seq * seq
    ct = RNN_TILE
    n_ct = D_MODEL // ct

    def body(x_ref, g_ref, cw_ref, cb_ref, wrg_ref, brg_ref, wig_ref, big_ref, lam_ref,
             xc_ref, hr_ref, ya_ref, a_s, b_s):
        rows = lax.broadcasted_iota(jnp.int32, (seq, ct), 0)
        xc = _conv_fwd(x_ref[...], cw_ref[...], cb_ref[...], rows)
        _, r, i, sp, a, mult = _rnn_gates(xc, wrg_ref[...], brg_ref[...], wig_ref[...], big_ref[...], lam_ref[...])
        a_s[...], b_s[...] = _scan_within_groups(a, mult * (i * xc), rows, reverse=False)

        def step(j, carry):
            r0 = pl.multiple_of(j * SUBLANES, SUBLANES)
            h = b_s[pl.ds(r0, SUBLANES), :] + a_s[pl.ds(r0, SUBLANES), :] * carry
            hr_ref[pl.ds(r0, SUBLANES), :] = h
            return h[SUBLANES - 1:SUBLANES, :]

        lax.fori_loop(0, seq // SUBLANES, step, jnp.zeros((1, ct), F32), unroll=4)
        gel, _ = _gelu_parts(g_ref[...])
        xc_ref[...] = xc
        ya_ref[...] = (hr_ref[...] * gel).astype(BF16)

    vec = pl.BlockSpec((1, ct), lambda b, c: (0, c))
    gate_w = pl.BlockSpec((None, ct, ct), lambda b, c: (c, 0, 0))
    tile = pl.BlockSpec((seq, ct), lambda b, c: (b, c))
    outs, landed = _call(
        body,
        grid=(n_seq, n_ct),
        in_specs=[
            pl.BlockSpec((seq, ct), lambda b, c: (b, c)),
            pl.BlockSpec((seq, ct), lambda b, c: (b, n_ct + c)),
            pl.BlockSpec((CONV_W, ct), lambda b, c: (0, c)), vec, gate_w, vec, gate_w, vec, vec,
        ],
        out_specs=[tile, tile, tile],
        out_shape=[jax.ShapeDtypeStruct((t, D_MODEL), F32), jax.ShapeDtypeStruct((t, D_MODEL), F32),
                   jax.ShapeDtypeStruct((t, D_MODEL), BF16)],
        scratch_shapes=[pltpu.VMEM((seq, ct), F32), pltpu.VMEM((seq, ct), F32)],
        args=(z, z, conv_w, conv_b, wrg_bd, b_rg, wig_bd, b_ig, lam), name="rnn_fwd",
        semantics=("parallel", "parallel"), hosted=hosted)
    return (*outs, landed) if hosted is not None else outs


def _rnn_bwd(dya, z, xc, hr, conv_w, wrg_bd, b_rg, wig_bd, b_ig, lam, *, n_seq, seq, hosted=None):
    t = n_seq * seq
    ct = RNN_TILE
    n_ct = D_MODEL // ct

    def body(dya_ref, x_ref, g_ref, xc_ref, hr_ref, cw_ref, wrg_ref, brg_ref, wig_ref, big_ref, lam_ref,
             dx_ref, dg_ref, dwrg_ref, dwig_ref, vec_ref, a_s, d_s, g_s):
        rows = lax.broadcasted_iota(jnp.int32, (seq, ct), 0)
        xv, xc, hr, dyv = x_ref[...], xc_ref[...], hr_ref[...], dya_ref[...]
        lamv = lam_ref[...]
        gel, dgel = _gelu_parts(g_ref[...])
        dg_ref[...] = (dyv * hr * dgel).astype(BF16)
        xcb, r, i, sp, a, mult = _rnn_gates(xc, wrg_ref[...], brg_ref[...], wig_ref[...], big_ref[...], lamv)
        a_s[...], d_s[...] = _scan_within_groups(_shift_up(a, 1, rows, seq), dyv * gel, rows, reverse=True)

        def step(k, carry):
            r0 = pl.multiple_of((seq // SUBLANES - 1 - k) * SUBLANES, SUBLANES)
            gs = d_s[pl.ds(r0, SUBLANES), :] + a_s[pl.ds(r0, SUBLANES), :] * carry
            g_s[pl.ds(r0, SUBLANES), :] = gs
            return gs[0:1, :]

        lax.fori_loop(0, seq // SUBLANES, step, jnp.zeros((1, ct), F32), unroll=4)
        gsum = g_s[...]
        gated = i * xc
        d_log_a = gsum * _shift_down(hr, 1, rows) * a - gsum * gated * (a * a / mult)
        d_gated = gsum * mult
        d_pre_r = (d_log_a * (-LRU_C) * sp) * r * (1.0 - r)
        d_pre_i = (d_gated * xc) * i * (1.0 - i)
        dprb, dpib = d_pre_r.astype(BF16), d_pre_i.astype(BF16)
        dxc = d_gated * i + _dot_nt(dprb, wrg_ref[...]) + _dot_nt(dpib, wig_ref[...])
        cw = cw_ref[...]
        dx = (dxc * cw[3:4, :] + _shift_up(dxc, 1, rows, seq) * cw[2:3, :]
              + _shift_up(dxc, 2, rows, seq) * cw[1:2, :] + _shift_up(dxc, 3, rows, seq) * cw[0:1, :])
        dx_ref[...] = dx.astype(BF16)

        @pl.when(pl.program_id(1) == 0)
        def _():
            dwrg_ref[...] = jnp.zeros_like(dwrg_ref)
            dwig_ref[...] = jnp.zeros_like(dwig_ref)
            vec_ref[...] = jnp.zeros_like(vec_ref)

        dwrg_ref[...] += _dot_tn(xcb, dprb)
        dwig_ref[...] += _dot_tn(xcb, dpib)

        def colsum(v):
            return jnp.sum(v, axis=0, keepdims=True)

        d_sp = colsum(d_log_a * (-LRU_C) * r)
        vec_ref[0:1, :] += colsum(d_pre_r)
        vec_ref[1:2, :] += colsum(d_pre_i)
        vec_ref[2:3, :] += d_sp * (-_sig(-lamv))
        vec_ref[3:4, :] += colsum(dxc)
        vec_ref[4:5, :] += colsum(dxc * _shift_down(xv, 3, rows))
        vec_ref[5:6, :] += colsum(dxc * _shift_down(xv, 2, rows))
        vec_ref[6:7, :] += colsum(dxc * _shift_down(xv, 1, rows))
        vec_ref[7:8, :] += colsum(dxc * xv)

    vec = pl.BlockSpec((1, ct), lambda c, b: (0, c))
    gate_w = pl.BlockSpec((None, ct, ct), lambda c, b: (c, 0, 0))
    tile = pl.BlockSpec((seq, ct), lambda c, b: (b, c))
    outs, landed = _call(
        body,
        grid=(n_ct, n_seq),
        in_specs=[
            tile,
            pl.BlockSpec((seq, ct), lambda c, b: (b, c)),
            pl.BlockSpec((seq, ct), lambda c, b: (b, n_ct + c)),
            tile, tile,
            pl.BlockSpec((CONV_W, ct), lambda c, b: (0, c)), gate_w, vec, gate_w, vec, vec,
        ],
        out_specs=[tile, tile, gate_w, gate_w, pl.BlockSpec((8, ct), lambda c, b: (0, c))],
        out_shape=[jax.ShapeDtypeStruct((t, D_MODEL), BF16), jax.ShapeDtypeStruct((t, D_MODEL), BF16),
                   jax.ShapeDtypeStruct((n_ct, ct, ct), F32), jax.ShapeDtypeStruct((n_ct, ct, ct), F32),
                   jax.ShapeDtypeStruct((8, D_MODEL), F32)],
        scratch_shapes=[pltpu.VMEM((seq, ct), F32)] * 3,
        args=(dya, z, z, xc, hr, conv_w, wrg_bd, b_rg, wig_bd, b_ig, lam), name="rnn_bwd",
        semantics=("parallel", "arbitrary"), hosted=hosted)
    return (*outs, landed) if hosted is not None else outs


def _split_hi_lo(x):
    hi = x.astype(BF16)
    return hi, (x - hi.astype(F32)).astype(BF16)


def _dot_split(x, m_twice):
    hi, lo = _split_hi_lo(x)
    return jnp.dot(jnp.concatenate([hi, lo], axis=1), m_twice, preferred_element_type=F32)


def _head_matrices(width):
    ec = ((lax.broadcasted_iota(jnp.int32, (2 * width, LANES), 0) & (width - 1)) // HEAD_DIM
          == lax.broadcasted_iota(jnp.int32, (2 * width, LANES), 1))
    ee = (lax.broadcasted_iota(jnp.int32, (2 * LANES, width), 1) // HEAD_DIM
          == (lax.broadcasted_iota(jnp.int32, (2 * LANES, width), 0) & (LANES - 1)))
    return jnp.where(ec, 1.0, 0.0).astype(BF16), jnp.where(ee, 1.0, 0.0).astype(BF16)


def _swap_halves(y):
    w = y.shape[1]
    first = (lax.broadcasted_iota(jnp.int32, y.shape, 1) % HEAD_DIM) < HEAD_DIM // 2
    return jnp.where(first, pltpu.roll(y, w - HEAD_DIM // 2, 1), pltpu.roll(y, HEAD_DIM // 2, 1))


def _normrope_fwd(x, gain, cos_t, sin_t, ec, ee):
    w = x.shape[1]
    rs = _dot_split(lax.rsqrt(_dot_split(x * x, ec) * (1.0 / HEAD_DIM) + NORM_EPS), ee)
    nx = x * rs
    y = nx * gain
    reps = w // LANES
    out = y * jnp.tile(cos_t, (1, reps)) + _swap_halves(y) * jnp.tile(sin_t, (1, reps))
    return out, nx, rs


def _normrope_bwd(dout, nx, rs, gain, cos_t, sin_t, ec, ee):
    w = dout.shape[1]
    reps = w // LANES
    dy = dout * jnp.tile(cos_t, (1, reps)) + _swap_halves(dout * jnp.tile(sin_t, (1, reps)))
    dgain = jnp.sum(dy * nx, axis=0, keepdims=True)
    dn = dy * gain
    seg = _dot_split(_dot_split(dn * nx, ec) * (1.0 / HEAD_DIM), ee)
    return rs * (dn - nx * seg), dgain


def _pair_operand(t, group):
    chunk = t[:, (group // 2) * LANES:(group // 2 + 1) * LANES]
    low = lax.broadcasted_iota(jnp.int32, chunk.shape, 1) < HEAD_DIM
    rolled = pltpu.roll(chunk, HEAD_DIM, 1)
    return jnp.where(low, chunk, rolled) if group % 2 == 0 else jnp.where(low, rolled, chunk)


GROUP = N_Q_HEADS // N_KV_HEADS
GROUP_W = GROUP * HEAD_DIM


def _replicate_head(t, group):
    return jnp.tile(_pair_operand(t, group), (1, 2))


def _head_blocks(t):
    seg = lax.broadcasted_iota(jnp.int32, t.shape, 1) // HEAD_DIM
    return jnp.concatenate([jnp.where(seg == h, t, 0.0) for h in range(GROUP)], axis=0)


def _stack_heads(t_t, rows):
    return jnp.concatenate([t_t[:, h * rows:(h + 1) * rows] for h in range(GROUP)], axis=0)


def _head_rows(mat_t, group):
    return jnp.concatenate([mat_t[GROUP * group + h:GROUP * group + h + 1, :] for h in range(GROUP)], axis=1)


def _window_masks(blk):
    key = lax.broadcasted_iota(jnp.int32, (blk, GROUP * blk), 0)
    query = lax.broadcasted_iota(jnp.int32, (blk, GROUP * blk), 1) & (blk - 1)
    return key > query, key <= query


def _mask_window(t, before_ok, own_ok, fill):
    blk = t.shape[0] // 2
    return jnp.concatenate([jnp.where(before_ok, t[:blk], fill), jnp.where(own_ok, t[blk:], fill)], axis=0)


def _attn_fwd(z, cos_t, sin_t, q_gain_t, k_gain_t, sinks_t, *, n_seq, seq, hosted=None):
    t = n_seq * seq
    blk = WINDOW
    nb = seq // blk

    def body(q_ref, kp_ref, kc_ref, vp_ref, vc_ref, cosc_ref, sinc_ref, cosp_ref, sinp_ref, qg_ref, kg_ref, sk_ref,
             o_ref, l_ref):
        n = pl.program_id(1)
        ecq, eeq = _head_matrices(D_MODEL)
        eck, eek = _head_matrices(KV_W)
        cosc, sinc = cosc_ref[...], sinc_ref[...]
        qh, _, _ = _normrope_fwd(q_ref[...], qg_ref[...], cosc, sinc, ecq, eeq)
        qh = qh * (HEAD_DIM ** -0.5)
        kc, _, _ = _normrope_fwd(kc_ref[...], kg_ref[...], cosc, sinc, eck, eek)
        kp, _, _ = _normrope_fwd(kp_ref[...], kg_ref[...], cosp_ref[...], sinp_ref[...], eck, eek)
        kcat = jnp.concatenate([kp, kc], axis=0)
        vcat = jnp.concatenate([vp_ref[...], vc_ref[...]], axis=0)
        above, causal = _window_masks(blk)
        above = above & (n > 0)
        head_row = lax.broadcasted_iota(jnp.int32, (blk, blk), 0)
        sk_t = jnp.broadcast_to(sk_ref[...], (blk, LANES)).T
        vcat_t = vcat.T.astype(BF16)
        lmat = jnp.zeros((blk, blk), F32)
        groups = range(N_KV_HEADS)
        cols = [slice(g * GROUP_W, (g + 1) * GROUP_W) for g in groups]
        scores = [_dot_nt(_replicate_head(kcat, g).astype(BF16), _head_blocks(qh[:, cols[g]]).astype(BF16))
                  for g in groups]
        probs = []
        for g in groups:
            s = _mask_window(scores[g], above, causal, NEG_BIG)
            sink = _head_rows(sk_t, g)
            m = jnp.maximum(jnp.max(s, axis=0, keepdims=True), sink)
            e = jnp.exp(s - m)
            den = jnp.sum(e, axis=0, keepdims=True) + jnp.exp(sink - m)
            probs.append((e * (1.0 / den)).astype(BF16))
            lse = m + jnp.log(den)
            for h in range(GROUP):
                lmat = lmat + jnp.where(head_row == GROUP * g + h, lse[:, h * blk:(h + 1) * blk], 0.0)
        for g in groups:
            out_t = jnp.dot(vcat_t[g * HEAD_DIM:(g + 1) * HEAD_DIM], probs[g], preferred_element_type=F32)
            o_ref[:, cols[g]] = _stack_heads(out_t, blk).T.astype(BF16)
        l_ref[...] = lmat

    def row(b, n):
        return b * nb + n

    def prev(b, n):
        return b * nb + jnp.maximum(n - 1, 0)

    kw = KV_W
    tab_c = pl.BlockSpec((blk, LANES), lambda b, n: (n, 0))
    tab_p = pl.BlockSpec((blk, LANES), lambda b, n: (jnp.maximum(n - 1, 0), 0))
    outs, landed = _call(
        body,
        grid=(n_seq, nb),
        in_specs=[
            pl.BlockSpec((blk, D_MODEL), lambda b, n: (row(b, n), 0)),
            pl.BlockSpec((blk, kw), lambda b, n: (prev(b, n), ATTN_K_AT // kw)),
            pl.BlockSpec((blk, kw), lambda b, n: (row(b, n), ATTN_K_AT // kw)),
            pl.BlockSpec((blk, kw), lambda b, n: (prev(b, n), ATTN_V_AT // kw)),
            pl.BlockSpec((blk, kw), lambda b, n: (row(b, n), ATTN_V_AT // kw)),
            tab_c, tab_c, tab_p, tab_p,
            pl.BlockSpec((1, D_MODEL), lambda b, n: (0, 0)),
            pl.BlockSpec((1, kw), lambda b, n: (0, 0)),
            pl.BlockSpec((1, LANES), lambda b, n: (0, 0)),
        ],
        out_specs=[pl.BlockSpec((blk, D_MODEL), lambda b, n: (row(b, n), 0)),
                   pl.BlockSpec((blk, LANES), lambda b, n: (row(b, n), 0))],
        out_shape=[jax.ShapeDtypeStruct((t, D_MODEL), BF16), jax.ShapeDtypeStruct((t, LANES), F32)],
        args=(z, z, z, z, z, cos_t, sin_t, cos_t, sin_t, q_gain_t, k_gain_t, sinks_t), name="attn_fwd",
        semantics=("parallel", "parallel"), hosted=hosted)
    return (*outs, landed) if hosted is not None else outs


def _attn_bwd(z, o, lse, do, cos_t, sin_t, q_gain_t, k_gain_t, sinks_t, *, n_seq, seq, hosted=None):
    t = n_seq * seq
    blk = WINDOW
    nb = seq // blk
    kw = KV_W
    scale = HEAD_DIM ** -0.5

    def body(qc_ref, qn_ref, kp_ref, kc_ref, vp_ref, vc_ref, oc_ref, on_ref, doc_ref, don_ref, lc_ref, ln_ref,
             cosc_ref, sinc_ref, cosp_ref, sinp_ref, cosn_ref, sinn_ref, qg_ref, kg_ref, sk_ref,
             dz_ref, vec_ref, dq_s):
        n = pl.program_id(1)
        ecq, eeq = _head_matrices(D_MODEL)
        eck, eek = _head_matrices(KV_W)
        cosc, sinc = cosc_ref[...], sinc_ref[...]
        qg, kg = qg_ref[...], kg_ref[...]
        qhc, nqc, rsqc = _normrope_fwd(qc_ref[...], qg, cosc, sinc, ecq, eeq)
        qhn, _, _ = _normrope_fwd(qn_ref[...], qg, cosn_ref[...], sinn_ref[...], ecq, eeq)
        khc, nkc, rskc = _normrope_fwd(kc_ref[...], kg, cosc, sinc, eck, eek)
        khp, _, _ = _normrope_fwd(kp_ref[...], kg, cosp_ref[...], sinp_ref[...], eck, eek)
        doc = doc_ref[...].astype(F32)
        don = don_ref[...].astype(F32)
        delc = _dot_split(doc * oc_ref[...].astype(F32), ecq)
        deln = _dot_split(don * on_ref[...].astype(F32), ecq)
        lc_t, ln_t, delc_t, deln_t = lc_ref[...], ln_ref[...], delc.T, deln.T
        above, causal = _window_masks(blk)
        above_c, above_n = above & (n > 0), above & (n < nb - 1)
        seg = lax.broadcasted_iota(jnp.int32, (blk, GROUP_W), 1) // HEAD_DIM
        lane = lax.broadcasted_iota(jnp.int32, (1, LANES), 1)
        sk_t = jnp.broadcast_to(sk_ref[...], (blk, LANES)).T
        dsink = jnp.zeros((1, LANES), F32)
        kcat = jnp.concatenate([khp, khc], axis=0)
        vcat = jnp.concatenate([vp_ref[...], vc_ref[...]], axis=0)
        kcat_t = kcat.T.astype(BF16)
        dkh = jnp.zeros((blk, GROUP_W), F32)
        dvh = jnp.zeros((blk, GROUP_W), F32)

        def fold_to(group, t):
            total = t + pltpu.roll(t, HEAD_DIM, 1)
            total = total + pltpu.roll(total, 2 * HEAD_DIM, 1)
            return jnp.where(seg == group, total, 0.0)

        groups = range(N_KV_HEADS)
        cols = [slice(g * GROUP_W, (g + 1) * GROUP_W) for g in groups]
        qsc, qsn = qhc * scale, qhn * scale
        qb_c = [_head_blocks(qsc[:, cols[g]]).astype(BF16) for g in groups]
        qb_n = [_head_blocks(qsn[:, cols[g]]).astype(BF16) for g in groups]
        dob_c = [_head_blocks(doc[:, cols[g]]).astype(BF16) for g in groups]
        dob_n = [_head_blocks(don[:, cols[g]]).astype(BF16) for g in groups]
        raw = []
        for g in groups:
            krep = _replicate_head(kcat, g).astype(BF16)
            vrep = _replicate_head(vcat, g).astype(BF16)
            raw.append((_dot_nt(krep, qb_c[g]), _dot_nt(vrep, dob_c[g]),
                        _dot_nt(krep[blk:], qb_n[g]), _dot_nt(vrep[blk:], dob_n[g])))
        cooked = []
        for g in groups:
            s_c, dp_c, s_n, dp_n = raw[g]
            l_row, d_row = _head_rows(lc_t, g), _head_rows(delc_t, g)
            p_c = _mask_window(jnp.exp(s_c - l_row), above_c, causal, 0.0)
            ds_c = (p_c * (dp_c - d_row)).astype(BF16)
            p_n = jnp.where(above_n, jnp.exp(s_n - _head_rows(ln_t, g)), 0.0)
            ds_n = (p_n * (dp_n - _head_rows(deln_t, g))).astype(BF16)
            cooked.append((p_c[blk:].astype(BF16), ds_c, p_n.astype(BF16), ds_n))
            p_sink = jnp.exp(_head_rows(sk_t, g) - l_row) * d_row
            for h in range(GROUP):
                dsink = dsink + jnp.where(lane == GROUP * g + h,
                                          -jnp.sum(p_sink[:, h * blk:(h + 1) * blk], axis=1, keepdims=True), 0.0)
        for g in groups:
            p_cb, ds_c, p_nb, ds_n = cooked[g]
            dq_t = jnp.dot(kcat_t[g * HEAD_DIM:(g + 1) * HEAD_DIM], ds_c, preferred_element_type=F32)
            dq_s[:, cols[g]] = _stack_heads(dq_t, blk).T * scale
            dk_rep = (jnp.dot(ds_c[blk:], qb_c[g], preferred_element_type=F32)
                      + jnp.dot(ds_n, qb_n[g], preferred_element_type=F32))
            dv_rep = (jnp.dot(p_cb, dob_c[g], preferred_element_type=F32)
                      + jnp.dot(p_nb, dob_n[g], preferred_element_type=F32))
            dkh = dkh + fold_to(g, dk_rep)
            dvh = dvh + fold_to(g, dv_rep)
        dq, dqg = _normrope_bwd(dq_s[...], nqc, rsqc, qg, cosc, sinc, ecq, eeq)
        dk, dkg = _normrope_bwd(dkh, nkc, rskc, kg, cosc, sinc, eck, eek)
        dz_ref[:, :ATTN_K_AT] = dq.astype(BF16)
        dz_ref[:, ATTN_K_AT:ATTN_V_AT] = dk.astype(BF16)
        dz_ref[:, ATTN_V_AT:] = dvh.astype(BF16)

        @pl.when(n == 0)
        def _():
            vec_ref[...] = jnp.zeros_like(vec_ref)

        vec_ref[0:1, :] += dqg
        vec_ref[1:2, 0:kw] += dkg
        vec_ref[2:3, 0:LANES] += dsink

    def row(b, n):
        return b * nb + n

    def prev(b, n):
        return b * nb + jnp.maximum(n - 1, 0)

    def nxt(b, n):
        return b * nb + jnp.minimum(n + 1, nb - 1)

    def tiles(width, col, which):
        return pl.BlockSpec((blk, width), lambda b, n: (which(b, n), col))

    def table(which):
        return pl.BlockSpec((blk, LANES), lambda b, n: (which(0, n), 0))

    outs, landed = _call(
        body,
        grid=(n_seq, nb),
        in_specs=[
            tiles(D_MODEL, 0, row), tiles(D_MODEL, 0, nxt),
            tiles(kw, ATTN_K_AT // kw, prev), tiles(kw, ATTN_K_AT // kw, row),
            tiles(kw, ATTN_V_AT // kw, prev), tiles(kw, ATTN_V_AT // kw, row),
            tiles(D_MODEL, 0, row), tiles(D_MODEL, 0, nxt),
            tiles(D_MODEL, 0, row), tiles(D_MODEL, 0, nxt),
            tiles(LANES, 0, row), tiles(LANES, 0, nxt),
            table(row), table(row), table(prev), table(prev), table(nxt), table(nxt),
            pl.BlockSpec((1, D_MODEL), lambda b, n: (0, 0)),
            pl.BlockSpec((1, kw), lambda b, n: (0, 0)),
            pl.BlockSpec((1, LANES), lambda b, n: (0, 0)),
        ],
        out_specs=[tiles(ATTN_W, 0, row), pl.BlockSpec((None, 8, D_MODEL), lambda b, n: (b, 0, 0))],
        out_shape=[jax.ShapeDtypeStruct((t, ATTN_W), BF16), jax.ShapeDtypeStruct((n_seq, 8, D_MODEL), F32)],
        scratch_shapes=[pltpu.VMEM((blk, D_MODEL), F32)],
        args=(z, z, z, z, z, z, o, o, do, do, lse, lse, cos_t, sin_t, cos_t, sin_t, cos_t, sin_t,
              q_gain_t, k_gain_t, sinks_t), name="attn_bwd", semantics=("parallel", "arbitrary"), hosted=hosted)
    return (*outs, landed) if hosted is not None else outs


MERGE_COLS = 512


def _merge_fwd(z, ya, yb):
    t = ya.shape[0]
    tm, tc = min(512, t), MERGE_COLS

    def body(ga_ref, gb_ref, ya_ref, yb_ref, o_ref):
        o_ref[...] = (_sig(ga_ref[...]) * ya_ref[...] + _sig(gb_ref[...]) * yb_ref[...]).astype(BF16)

    tile = pl.BlockSpec((tm, tc), lambda i, j: (i, j))
    return pl.pallas_call(
        body,
        grid=(t // tm, D_MODEL // tc),
        in_specs=[pl.BlockSpec((tm, tc), lambda i, j: (i, j)),
                  pl.BlockSpec((tm, tc), lambda i, j: (i, D_MODEL // tc + j)), tile, tile],
        out_specs=tile,
        out_shape=jax.ShapeDtypeStruct((t, D_MODEL), BF16),
        compiler_params=_params("parallel", "parallel"),
        name="merge_fwd",
    )(z, z, ya, yb)


def _merge_bwd(z, ya, yb, dmerged):
    t = ya.shape[0]
    tm, tc = min(512, t), MERGE_COLS

    def body(ga_ref, gb_ref, ya_ref, yb_ref, dm_ref, dya_ref, dyb_ref, dga_ref, dgb_ref):
        dm = dm_ref[...]
        sa, sb = _sig(ga_ref[...]), _sig(gb_ref[...])
        dya_ref[...] = (dm * sa).astype(BF16)
        dyb_ref[...] = (dm * sb).astype(BF16)
        dga_ref[...] = (dm * ya_ref[...] * sa * (1.0 - sa)).astype(BF16)
        dgb_ref[...] = (dm * yb_ref[...] * sb * (1.0 - sb)).astype(BF16)

    tile = pl.BlockSpec((tm, tc), lambda i, j: (i, j))
    return pl.pallas_call(
        body,
        grid=(t // tm, D_MODEL // tc),
        in_specs=[pl.BlockSpec((tm, tc), lambda i, j: (i, j)),
                  pl.BlockSpec((tm, tc), lambda i, j: (i, D_MODEL // tc + j)), tile, tile, tile],
        out_specs=[tile] * 4,
        out_shape=[jax.ShapeDtypeStruct((t, D_MODEL), BF16)] * 4,
        compiler_params=_params("parallel", "parallel"),
        name="merge_bwd",
    )(z, z, ya, yb, dmerged)


def _loss_head(x2, e, gt, target):
    t, d = x2.shape
    tm = min(256, t)

    def body(x_ref, e_ref, gt_ref, tg_ref, loss_ref, dx_ref, dgt_ref, de_ref):
        ev = e_ref[...]
        sg = _sig(gt_ref[...])
        diff = x_ref[...] + ev * sg - tg_ref[...]
        dx = diff * (1.0 / d)
        dx_ref[...] = dx
        dgt_ref[...] = (dx * ev * sg * (1.0 - sg)).astype(BF16)
        de_ref[...] = (dx * sg).astype(BF16)

        @pl.when(pl.program_id(0) == 0)
        def _():
            loss_ref[...] = jnp.zeros_like(loss_ref)

        loss_ref[...] += jnp.sum(jnp.sum(diff * diff, axis=1, keepdims=True), axis=0, keepdims=True)

    tile = pl.BlockSpec((tm, d), lambda i: (i, 0))
    return pl.pallas_call(
        body,
        grid=(t // tm,),
        in_specs=[tile] * 4,
        out_specs=[pl.BlockSpec((1, LANES), lambda i: (0, 0)), tile, tile, tile],
        out_shape=[jax.ShapeDtypeStruct((1, LANES), F32), jax.ShapeDtypeStruct((t, d), F32),
                   jax.ShapeDtypeStruct((t, d), BF16), jax.ShapeDtypeStruct((t, d), BF16)],
        compiler_params=_params("arbitrary"),
        name="loss_head",
    )(x2, e, gt, target)


def _rope_tables(seq):
    inv = ROPE_THETA ** (-jnp.arange(0, HEAD_DIM, 2, dtype=F32) / HEAD_DIM)
    ang = jnp.arange(seq, dtype=F32)[:, None] * inv[None, :]
    cos, sin = jnp.cos(ang), jnp.sin(ang)
    return jnp.tile(jnp.concatenate([cos, cos], axis=1), (1, 2)), jnp.tile(jnp.concatenate([-sin, sin], axis=1), (1, 2))


def _block_diag_tiles(w):
    per = RNN_TILE // RNN_BLOCK_W
    w4 = w.reshape(D_MODEL // RNN_TILE, per, RNN_BLOCK_W, RNN_BLOCK_W)
    eye = jnp.eye(per, dtype=w.dtype)
    dense = jnp.einsum("tpij,pq->tpiqj", w4, eye)
    return dense.reshape(D_MODEL // RNN_TILE, RNN_TILE, RNN_TILE).astype(BF16)


def _block_diag_extract(dense):
    per = RNN_TILE // RNN_BLOCK_W
    d5 = dense.reshape(D_MODEL // RNN_TILE, per, RNN_BLOCK_W, per, RNN_BLOCK_W)
    blocks = jnp.stack([d5[:, p, :, p, :] for p in range(per)], axis=1)
    return blocks.reshape(D_MODEL // RNN_BLOCK_W, RNN_BLOCK_W, RNN_BLOCK_W)


def _local_step(x, p, target, w, *, n_seq, seq, comm=None):
    w = dict(w)

    def run(tag, fn, *args, **kwargs):
        hosted = comm.host(tag) if comm is not None else None
        if hosted is None:
            return fn(*args, **kwargs)
        *outs, landed = fn(*args, hosted=hosted, **kwargs)
        comm.landed(tag, landed, w)
        return outs[0] if len(outs) == 1 else outs

    def ready(batch, grads, extra=None):
        if comm is not None:
            comm.ready(batch, grads, extra)

    cos_t, sin_t = _rope_tables(seq)
    q_gain_t = jnp.tile(w["q_gain"], (1, N_Q_HEADS))
    k_gain_t = jnp.tile(w["k_gain"], (1, N_KV_HEADS))
    sinks_t = jnp.pad(w["sinks"], ((0, 0), (0, LANES - N_Q_HEADS)))
    wrg_bd, wig_bd = _block_diag_tiles(w["w_rg"]), _block_diag_tiles(w["w_ig"])
    dims = dict(n_seq=n_seq, seq=seq)

    h = _rmsnorm_fwd(x, w["g_mix"], name="norm_mix")
    z_rnn, z_attn, z_gate = (
        _matmul(h, w["w_in"], mode="nn", tm=1024, tn=tn, out_dtypes=[F32], name="mm_in_" + nm,
                b_cols=[(at, width)])
        for nm, at, width, tn in (("rnn", 0, COL_RNN_END, 1024), ("attn", COL_RNN_END, ATTN_W, 512),
                                  ("gate", COL_ATTN_END, IN_TOTAL - COL_ATTN_END, 512)))
    xc, hr, ya_in = run("rnn_fwd", _rnn_fwd, z_rnn, w["conv_w"], w["conv_b"], wrg_bd, w["b_rg"], wig_bd, w["b_ig"],
                        w["lru_lambda"], **dims)
    o, lse = run("attn_fwd", _attn_fwd, z_attn, cos_t, sin_t, q_gain_t, k_gain_t, sinks_t, **dims)
    ya = run("mm_rnn_proj", _matmul, ya_in, w["w_rnn_proj"], mode="nn", tm=1024, tn=1024, out_dtypes=[F32],
             name="mm_rnn_proj")
    yb = _matmul(o, w["w_attn_proj"], mode="nn", tm=1024, tn=1024, out_dtypes=[F32], name="mm_attn_proj")
    merged = _merge_fwd(z_gate, ya, yb)
    x1 = _matmul(merged, w["w_out"], mode="nn", tm=1024, tn=1024, out_dtypes=[F32], name="mm_out",
                 epilogue=lambda acc, res: (res + acc,), extras=(x,))
    hm = _rmsnorm_fwd(x1, w["g_mlp"], name="norm_mlp")
    act = _matmul(hm, w["w_up"], mode="nn", tm=1024, tn=1024, out_dtypes=[BF16], name="mm_up",
                  epilogue=lambda acc: (jnp.square(jnp.maximum(acc, 0.0)),))
    x2 = _matmul(act, w["w_down"], mode="nn", tm=512, tn=1024, out_dtypes=[F32], name="mm_down",
                 epilogue=lambda acc, res: (res + acc,), extras=(x1,))
    hp = _rmsnorm_fwd(x2, w["g_ple"], name="norm_ple")
    gt = _matmul(hp, w["w_ple_gate"], mode="nn", tm=1024, tn=1024, out_dtypes=[F32], name="mm_ple_gate")
    p_bf = p.astype(BF16)
    e = _matmul(p_bf, w["w_ple_proj"], mode="nn", tm=1024, tn=1024, out_dtypes=[F32], name="mm_ple_proj")
    loss_row, dx3, dgt, de = _loss_head(x2, e, gt, target)

    g = {}
    g["w_ple_proj"] = _matmul_tn(p_bf, de, tk=PLE_DIM, tn=1024, tt=1024, name="mm_d_ple_proj",
                                 slot_cols=D_MODEL // N_DEV)
    g["w_ple_gate"] = _matmul_tn(hp, dgt, tk=1024, tn=1024, tt=512, name="mm_d_ple_gate")
    dhp = _matmul(dgt, w["w_ple_gate"], mode="nt", tm=1024, tn=1024, out_dtypes=[F32], name="mm_dhp")
    dx2, dx2_bf, g["g_ple"] = _rmsnorm_bwd(x2, w["g_ple"], dhp, dx3, name="norm_ple_bwd", want_bf16=True)
    g["w_down"] = _matmul_tn(act, dx2_bf, tk=1024, tn=1024, tt=512, name="mm_d_down")
    du = _matmul(dx2_bf, w["w_down"], mode="nt", tm=1024, tn=1024, out_dtypes=[BF16], name="mm_dact",
                 epilogue=lambda acc, a: (acc * (2.0 * jnp.sqrt(a.astype(F32))),), extras=(act,))
    g["w_up"] = _matmul_tn(hm, du, tk=1024, tn=1024, tt=512, name="mm_d_up", slot_cols=D_FF // N_DEV)
    ready(1, g)
    dhm = run("mm_dhm", _matmul, du, w["w_up"], mode="nt", tm=512, tn=1024, out_dtypes=[F32], name="mm_dhm")
    dx1, dx1_bf, g["g_mlp"] = _rmsnorm_bwd(x1, w["g_mlp"], dhm, dx2, name="norm_mlp_bwd", want_bf16=True)
    g["w_out"] = _matmul_tn(merged, dx1_bf, tk=1024, tn=1024, tt=512, name="mm_d_out")
    dmerged = _matmul(dx1_bf, w["w_out"], mode="nt", tm=1024, tn=1024, out_dtypes=[F32], name="mm_dmerged")
    dya, dyb, dga, dgb = _merge_bwd(z_gate, ya, yb, dmerged)
    g["w_rnn_proj"] = _matmul_tn(ya_in, dya, tk=1024, tn=1024, tt=512, name="mm_d_rnn_proj")
    g["w_attn_proj"] = _matmul_tn(o, dyb, tk=1024, tn=1024, tt=512, name="mm_d_attn_proj")
    ready(2, g)
    dya_in = run("mm_dya_in", _matmul, dya, w["w_rnn_proj"], mode="nt", tm=1024, tn=1024, out_dtypes=[F32],
                 name="mm_dya_in")
    do = _matmul(dyb, w["w_attn_proj"], mode="nt", tm=1024, tn=1024, out_dtypes=[BF16], name="mm_do")
    dx_rnn, dg_rnn, dwrg_dense, dwig_dense, rnn_vec = run(
        "rnn_bwd", _rnn_bwd, dya_in, z_rnn, xc, hr, w["conv_w"], wrg_bd, w["b_rg"], wig_bd, w["b_ig"],
        w["lru_lambda"], **dims)
    dz_attn, attn_vec = run("attn_bwd", _attn_bwd, z_attn, o, lse, do, cos_t, sin_t, q_gain_t, k_gain_t, sinks_t,
                            **dims)
    dz_parts = (dx_rnn, dg_rnn, dz_attn, dga, dgb)
    g["w_in"] = jnp.concatenate(
        _matmul_tn_multi(h, dz_parts[:2], tt=512, name="mm_d_in_rnn")
        + _matmul_tn_multi(h, dz_parts[2:], tt=512, name="mm_d_in_rest"), axis=1)
    g["w_rg"] = _block_diag_extract(dwrg_dense)
    g["w_ig"] = _block_diag_extract(dwig_dense)
    g["b_rg"], g["b_ig"], g["lru_lambda"], g["conv_b"] = (rnn_vec[i:i + 1] for i in range(4))
    g["conv_w"] = rnn_vec[4:8]
    attn_vec = attn_vec[0] if n_seq == 1 else functools.reduce(jnp.add, [attn_vec[b] for b in range(n_seq)])
    g["q_gain"] = attn_vec[0].reshape(N_Q_HEADS, HEAD_DIM).sum(axis=0)[None, :]
    g["k_gain"] = attn_vec[1, :KV_W].reshape(N_KV_HEADS, HEAD_DIM).sum(axis=0)[None, :]
    g["sinks"] = attn_vec[2:3, :N_Q_HEADS]
    ready(3, g, {LOSS_ROW: loss_row})
    windows, at = [], 0
    for part in dz_parts:
        width = part.shape[1]
        windows.append((w["w_in"], (at, width)) if at % width == 0 else (w["w_in"][:, at:at + width], (0, width)))
        at += width
    dh = run("mm_dh", _matmul, dz_parts, [wd[0] for wd in windows], mode="nt", tm=512, tn=1024, out_dtypes=[F32],
             name="mm_dh", b_cols=[wd[1] for wd in windows])
    grad_x, g["g_mix"] = run("norm_mix_bwd", _rmsnorm_bwd, x, w["g_mix"], dh, dx1, name="norm_mix_bwd",
                             want_bf16=False)
    return loss_row[0, 0], grad_x, g


MESH_ID = pl.DeviceIdType.MESH


def _coords(index):
    return (index >> 2) & 1, (index >> 1) & 1, index & 1


def _exchange(srcs, kinds, *, name):
    n = len(srcs)
    n_peer = N_DEV - 1

    def body(*refs):
        src, dst = refs[:n], refs[n:2 * n]
        send_sems, recv_sems, local_sems = refs[2 * n:]
        me = 4 * lax.axis_index("x") + 2 * lax.axis_index("y") + lax.axis_index("c")

        def remote(i, d):
            peer = (me + d) & (N_DEV - 1)
            piece = src[i] if kinds[i] == "gather" else src[i].at[peer]
            return pltpu.make_async_remote_copy(
                src_ref=piece, dst_ref=dst[i].at[me], send_sem=send_sems.at[i * n_peer + d - 1],
                recv_sem=recv_sems.at[i * n_peer + d - 1], device_id=_coords(peer), device_id_type=MESH_ID)

        def arrival(i, d):
            sender = (me - d) & (N_DEV - 1)
            piece = src[i] if kinds[i] == "gather" else src[i].at[sender]
            return pltpu.make_async_remote_copy(
                src_ref=piece, dst_ref=dst[i].at[sender], send_sem=send_sems.at[i * n_peer + d - 1],
                recv_sem=recv_sems.at[i * n_peer + d - 1], device_id=_coords(sender), device_id_type=MESH_ID)

        own = []
        for i in range(n):
            piece = src[i] if kinds[i] == "gather" else src[i].at[me]
            own.append(pltpu.make_async_copy(piece, dst[i].at[me], local_sems.at[i]))
            own[-1].start()
        sent = [remote(i, d) for d in range(1, N_DEV) for i in range(n)]
        for cp in sent:
            cp.start()
        for d in range(1, N_DEV):
            for i in range(n):
                arrival(i, d).wait_recv()
        for cp in sent:
            cp.wait_send()
        for cp in own:
            cp.wait()

    def out_of(s, kind):
        shape = s.shape if kind == "scatter" else (N_DEV,) + s.shape
        return jax.ShapeDtypeStruct(shape, s.dtype)

    any_spec = pl.BlockSpec(memory_space=pl.ANY)
    return pl.pallas_call(
        body,
        in_specs=[any_spec] * n,
        out_specs=[any_spec] * n,
        out_shape=[out_of(s, k) for s, k in zip(srcs, kinds)],
        scratch_shapes=[pltpu.SemaphoreType.DMA((n * n_peer,)), pltpu.SemaphoreType.DMA((n * n_peer,)),
                        pltpu.SemaphoreType.DMA((n,))],
        compiler_params=pltpu.CompilerParams(has_side_effects=True),
        name=name,
    )(*srcs)


def _remote(src, dst, send_sem, recv_sem, to):
    return pltpu.make_async_remote_copy(src_ref=src, dst_ref=dst, send_sem=send_sem, recv_sem=recv_sem,
                                        device_id=to, device_id_type=MESH_ID)


def _gather_two_level(shards, *, name):
    n = len(shards)
    per = N_DEV - 1

    def body(*refs):
        src, dst = refs[:n], refs[n:2 * n]
        send_sems, recv_sems, local_sems = refs[2 * n:]
        x, y, c = lax.axis_index("x"), lax.axis_index("y"), lax.axis_index("c")
        me, sibling = (x, y, c), (x, y, 1 - c)
        chips = [(1 - x, y), (x, 1 - y), (1 - x, 1 - y)]

        def slot(pos):
            return 4 * pos[0] + 2 * pos[1] + pos[2]

        def copy(i, k, block, to, from_shard=False):
            source = src[i] if from_shard else dst[i].at[slot(block)]
            return _remote(source, dst[i].at[slot(block)], send_sems.at[i * per + k], recv_sems.at[i * per + k], to)

        mine = [pltpu.make_async_copy(src[i], dst[i].at[slot(me)], local_sems.at[i]) for i in range(n)]
        for cp in mine:
            cp.start()
        first = []
        for i in range(n):
            first.append(copy(i, 0, me, sibling, from_shard=True))
            first += [copy(i, 1 + j, me, (*chip, c), from_shard=True) for j, chip in enumerate(chips)]
        for cp in first:
            cp.start()
        passed = []
        for i in range(n):
            for j, chip in enumerate(chips):
                copy(i, 1 + j, (*chip, c), me).wait_recv()
                passed.append(copy(i, 4 + j, (*chip, c), sibling))
                passed[-1].start()
        for i in range(n):
            copy(i, 0, sibling, me).wait_recv()
            for j, chip in enumerate(chips):
                copy(i, 4 + j, (*chip, 1 - c), me).wait_recv()
        for cp in first + passed:
            cp.wait_send()
        for cp in mine:
            cp.wait()

    any_spec = pl.BlockSpec(memory_space=pl.ANY)
    return pl.pallas_call(
        body,
        in_specs=[any_spec] * n,
        out_specs=[any_spec] * n,
        out_shape=[jax.ShapeDtypeStruct((N_DEV,) + s.shape, s.dtype) for s in shards],
        scratch_shapes=[pltpu.SemaphoreType.DMA((n * per,)), pltpu.SemaphoreType.DMA((n * per,)),
                        pltpu.SemaphoreType.DMA((n,))],
        name=name,
    )(*shards)


CHIPS = N_DEV // 2


def _other_chips(x, y):
    return [(x, 1 - y), (1 - x, y), (1 - x, 1 - y)]


def _hosted_gather_first(shards):
    n = len(shards)
    per = CHIPS

    def plan(src, dst, send_sems, recv_sems, local_sems, first_sem):
        x, y, c = lax.axis_index("x"), lax.axis_index("y"), lax.axis_index("c")
        peers = [(x, y, 1 - c)] + [(*chip, c) for chip in _other_chips(x, y)]
        copies = []
        for i in range(n):
            own = pltpu.make_async_copy(src[i], dst[i].at[4 * x + 2 * y + c], local_sems.at[first_sem + i])
            copies.append(_Xfer(own.start, own.wait))
        for j, peer in enumerate(peers):
            for i in range(n):
                k = first_sem + i * per + j
                out = _remote(src[i], dst[i].at[4 * x + 2 * y + c], send_sems.at[k], recv_sems.at[k], peer)
                arrival = _remote(src[i], dst[i].at[4 * peer[0] + 2 * peer[1] + peer[2]], send_sems.at[k],
                                  recv_sems.at[k], peer)

                def wait(out=out, arrival=arrival):
                    arrival.wait_recv()
                    out.wait_send()

                copies.append(_Xfer(out.start, wait))
        return copies

    out_shape = tuple(jax.ShapeDtypeStruct((N_DEV,) + s.shape, s.dtype) for s in shards)
    return _Hosted(tuple(shards), out_shape, n * per, plan)


def _hosted_gather_second(landed):
    n = len(landed)
    per = CHIPS - 1

    def plan(src, dst, send_sems, recv_sems, local_sems, first_sem):
        x, y, c = lax.axis_index("x"), lax.axis_index("y"), lax.axis_index("c")
        copies = []
        for j, chip in enumerate(_other_chips(x, y)):
            mine, theirs = 4 * chip[0] + 2 * chip[1] + c, 4 * chip[0] + 2 * chip[1] + 1 - c
            for i in range(n):
                k = first_sem + i * per + j
                out = _remote(src[i].at[mine], dst[i].at[mine], send_sems.at[k], recv_sems.at[k], (x, y, 1 - c))
                arrival = _remote(src[i].at[theirs], dst[i].at[theirs], send_sems.at[k], recv_sems.at[k],
                                  (x, y, 1 - c))

                def wait(out=out, arrival=arrival):
                    arrival.wait_recv()
                    out.wait_send()

                copies.append(_Xfer(out.start, wait))
        return copies

    out_shape = tuple(jax.ShapeDtypeStruct(a.shape, a.dtype) for a in landed)
    return _Hosted(tuple(landed), out_shape, n * per, plan, tuple((i, i) for i in range(n)))


def _hosted_sibling_swap(arrays, sliced):
    n_sems = sum(CHIPS if s else 1 for s in sliced)

    def plan(src, dst, send_sems, recv_sems, local_sems, first_sem):
        x, y, c = lax.axis_index("x"), lax.axis_index("y"), lax.axis_index("c")
        sibling = (x, y, 1 - c)
        copies, k = [], first_sem
        for i, is_sliced in enumerate(sliced):
            pieces = [(src[i].at[2 * s + 1 - c], dst[i].at[s]) for s in range(CHIPS)] if is_sliced else [(src[i], dst[i])]
            for source, target in pieces:
                cp = _remote(source, target, send_sems.at[k], recv_sems.at[k], sibling)
                copies.append(_Xfer(cp.start, cp.wait))
                k += 1
        return copies

    out_shape = tuple(jax.ShapeDtypeStruct((CHIPS,) + a.shape[1:] if s else a.shape, a.dtype)
                      for a, s in zip(arrays, sliced))
    return _Hosted(tuple(arrays), out_shape, n_sems, plan)


def _hosted_chip_exchange(arrays, sliced):
    n = len(arrays)
    per = CHIPS - 1

    def plan(src, dst, send_sems, recv_sems, local_sems, first_sem):
        x, y, c = lax.axis_index("x"), lax.axis_index("y"), lax.axis_index("c")
        chip = 2 * x + y
        copies = []
        for i in range(n):
            own = pltpu.make_async_copy(src[i].at[chip] if sliced[i] else src[i], dst[i].at[chip],
                                        local_sems.at[first_sem + i])
            copies.append(_Xfer(own.start, own.wait))
        for d in range(1, CHIPS):
            other = chip ^ d
            to = ((other >> 1) & 1, other & 1, c)
            for i in range(n):
                k = first_sem + i * per + d - 1
                source = src[i].at[other] if sliced[i] else src[i]
                out = _remote(source, dst[i].at[chip], send_sems.at[k], recv_sems.at[k], to)
                arrival = _remote(source, dst[i].at[other], send_sems.at[k], recv_sems.at[k], to)

                def wait(out=out, arrival=arrival):
                    arrival.wait_recv()
                    out.wait_send()

                copies.append(_Xfer(out.start, wait))
        return copies

    out_shape = tuple(jax.ShapeDtypeStruct(a.shape if s else (CHIPS,) + a.shape, a.dtype)
                      for a, s in zip(arrays, sliced))
    return _Hosted(tuple(arrays), out_shape, n * per, plan)


def _add_sibling(parts, received, core, *, name):
    _, r, cols = parts.shape
    tr = min(256, r)

    def body(core_ref, a_ref, b_ref, o_ref):
        o_ref[...] = (a_ref[...] + b_ref[...]).astype(BF16)

    grid_spec = pltpu.PrefetchScalarGridSpec(
        num_scalar_prefetch=1,
        grid=(CHIPS, r // tr),
        in_specs=[pl.BlockSpec((None, tr, cols), lambda k, i, core_ref: (2 * k + core_ref[0], i, 0)),
                  pl.BlockSpec((None, tr, cols), lambda k, i, core_ref: (k, i, 0))],
        out_specs=pl.BlockSpec((None, tr, cols), lambda k, i, core_ref: (k, i, 0)),
    )
    return pl.pallas_call(body, grid_spec=grid_spec, out_shape=jax.ShapeDtypeStruct((CHIPS, r, cols), BF16),
                          compiler_params=_params("parallel", "parallel"), name=name)(core, parts, received)


def _add_whole(a, b, *, name):
    def body(a_ref, b_ref, o_ref):
        o_ref[...] = a_ref[...] + b_ref[...]

    return pl.pallas_call(body, out_shape=jax.ShapeDtypeStruct(a.shape, F32), name=name)(a, b)


def _adamw(parts, w, m, v, *, name):
    r, c = w.shape
    n_parts = parts.shape[0]
    tr = min(256, r)
    c1 = 1.0 - ADAM_B1 ** ADAM_STEP
    c2 = 1.0 - ADAM_B2 ** ADAM_STEP

    def body(p_ref, w_ref, m_ref, v_ref, g_ref, d_ref, nm_ref, nv_ref):
        g = p_ref[0].astype(F32)
        for s in range(1, n_parts):
            g = g + p_ref[s].astype(F32)
        nm = ADAM_B1 * m_ref[...] + (1.0 - ADAM_B1) * g
        nv = ADAM_B2 * v_ref[...] + (1.0 - ADAM_B2) * (g * g)
        g_ref[...] = g
        nm_ref[...] = nm
        nv_ref[...] = nv
        d_ref[...] = -ADAM_LR * ((nm / c1) / (jnp.sqrt(nv / c2) + ADAM_EPS) + ADAM_WD * w_ref[...])

    tile = pl.BlockSpec((tr, c), lambda i: (i, 0))
    return pl.pallas_call(
        body,
        grid=(r // tr,),
        in_specs=[pl.BlockSpec((n_parts, tr, c), lambda i: (0, i, 0)), tile, tile, tile],
        out_specs=[tile] * 4,
        out_shape=[jax.ShapeDtypeStruct((r, c), F32)] * 4,
        compiler_params=_params("parallel"),
        name=name,
    )(parts, w, m, v)


BIG = ("w_in", "w_rnn_proj", "w_attn_proj", "w_out", "w_up", "w_down", "w_ple_gate", "w_ple_proj")
LOSS_ROW = "loss"
SMALL = (("conv_b", 1), ("b_rg", 1), ("b_ig", 1), ("lru_lambda", 1), ("g_mlp", 1), ("g_ple", 1),
         ("q_gain", 1), ("k_gain", 1), ("sinks", 1), (LOSS_ROW, 1), ("w_rg", 64), ("w_ig", 64))
SMALL_ROWS = 144
ROW_SHARDED = ("w_rnn_proj", "w_attn_proj", "w_out", "w_down", "w_ple_gate")
COL_SHARDED = ("w_in", "w_up", "w_ple_proj")
BATCHES = {1: ("w_ple_proj", "w_ple_gate", "w_down", "w_up"), 2: ("w_out", "w_rnn_proj", "w_attn_proj"),
           3: ("w_in", "conv_w")}
SMALL_BATCH = 4


def _pack_small(vals):
    rows = []
    for nm, nrow in SMALL:
        flat = vals[nm].reshape(-1).astype(F32)
        rows.append(jnp.pad(flat, (0, nrow * D_MODEL - flat.shape[0])).reshape(nrow, D_MODEL))
    used = sum(nrow for _, nrow in SMALL)
    rows.append(jnp.zeros((SMALL_ROWS - used, D_MODEL), F32))
    return jnp.concatenate(rows, axis=0)


def _unpack_small(packed, shapes):
    out, at = {}, 0
    for nm, nrow in SMALL:
        size = 1
        for s in shapes[nm]:
            size *= s
        out[nm] = packed[at:at + nrow].reshape(-1)[:size].reshape(shapes[nm])
        at += nrow
    return out


def _full_weight(name, landed):
    if name in COL_SHARDED:
        return landed.transpose(1, 0, 2).reshape(landed.shape[1], N_DEV * landed.shape[2])
    return landed.reshape(N_DEV * landed.shape[1], landed.shape[2])


def _owner_slots(name, grad):
    if name == "w_in":
        return grad.reshape(D_MODEL, N_DEV, IN_TOTAL // N_DEV).transpose(1, 0, 2)
    if name == "conv_w":
        return grad.reshape(CONV_W, N_DEV, D_MODEL // N_DEV).transpose(1, 0, 2)
    if name in COL_SHARDED:
        return grad
    return grad.reshape(N_DEV, grad.shape[0] // N_DEV, grad.shape[1])


class _StepExchanges:
    FIRST, SECOND = "first", "second"
    EARLY, MID, LATE = ("w_rnn_proj", "w_attn_proj", "w_out"), ("w_up",), ("w_down", "w_ple_gate", "w_ple_proj")
    GATHERS = {"rnn_fwd": ((FIRST, EARLY), (FIRST, MID)),
               "attn_fwd": ((SECOND, EARLY), (SECOND, MID), (FIRST, LATE)), "mm_rnn_proj": ((SECOND, LATE),)}
    SWAPS = {"mm_dhm": 1, "mm_dya_in": 2}
    CHIP_EXCHANGES = {"rnn_bwd": 1, "attn_bwd": 2, "mm_dh": 3, "norm_mix_bwd": SMALL_BATCH}

    def __init__(self, shards, core):
        self.shards = shards
        self.core = core
        self.parts, self.swapped, self.summed, self.half_gathered = {}, {}, {}, {}

    def ready(self, batch, grads, extra=None):
        arrays = [_owner_slots(nm, grads[nm]) for nm in BATCHES[batch]]
        self.parts[batch] = (arrays, [True] * len(arrays))
        if batch not in self.SWAPS.values():
            self.parts[SMALL_BATCH] = ([_pack_small({**grads, **extra})], [False])
            both = [a + b for a, b in zip(self.parts[batch], self.parts[SMALL_BATCH])]
            _, swapped = _call(
                lambda: None, grid=(1,), in_specs=[], out_specs=[], out_shape=[], args=(), name="swap_last",
                semantics=("arbitrary",), hosted=_hosted_sibling_swap(*both))
            self.swapped[batch], self.swapped[SMALL_BATCH] = swapped[:-1], swapped[-1:]

    def host(self, tag):
        if tag in self.GATHERS:
            return _merge_hosted([
                _hosted_gather_first([self.shards[nm] for nm in group]) if half == self.FIRST
                else _hosted_gather_second([self.half_gathered[nm] for nm in group])
                for half, group in self.GATHERS[tag]])
        if tag in self.SWAPS:
            return _hosted_sibling_swap(*self.parts[self.SWAPS[tag]])
        if tag in self.CHIP_EXCHANGES:
            batch = self.CHIP_EXCHANGES[tag]
            arrays, sliced = self.parts[batch]
            labels = BATCHES.get(batch, ("small",))
            sums = [_add_sibling(a, r, self.core, name="add_" + lb) if s else _add_whole(a, r, name="add_" + lb)
                    for a, r, s, lb in zip(arrays, self.swapped[batch], sliced, labels)]
            return _hosted_chip_exchange(sums, sliced)
        return None

    def landed(self, tag, landed, weights):
        if tag in self.GATHERS:
            names = [(half, nm) for half, group in self.GATHERS[tag] for nm in group]
            for (half, nm), buf in zip(names, landed):
                if half == self.FIRST:
                    self.half_gathered[nm] = buf
                else:
                    weights[nm] = _full_weight(nm, buf)
        elif tag in self.SWAPS:
            self.swapped[self.SWAPS[tag]] = landed
        else:
            self.summed[self.CHIP_EXCHANGES[tag]] = landed


def kernel(x, p, g_mix, w_in, conv_w, conv_b, w_rg, b_rg, w_ig, b_ig, lru_lambda, w_rnn_proj, q_gain, k_gain, sinks, w_attn_proj, w_out, g_mlp, w_up, w_down, g_ple, w_ple_gate, w_ple_proj, loss_target, m_g_mix, m_w_in, m_conv_w, m_conv_b, m_w_rg, m_b_rg, m_w_ig, m_b_ig, m_lru_lambda, m_w_rnn_proj, m_q_gain, m_k_gain, m_sinks, m_w_attn_proj, m_w_out, m_g_mlp, m_w_up, m_w_down, m_g_ple, m_w_ple_gate, m_w_ple_proj, v_g_mix, v_w_in, v_conv_w, v_conv_b, v_w_rg, v_b_rg, v_w_ig, v_b_ig, v_lru_lambda, v_w_rnn_proj, v_q_gain, v_k_gain, v_sinks, v_w_attn_proj, v_w_out, v_g_mlp, v_w_up, v_w_down, v_g_ple, v_w_ple_gate, v_w_ple_proj):
    names = ("g_mix", "w_in", "conv_w", "conv_b", "w_rg", "b_rg", "w_ig", "b_ig", "lru_lambda", "w_rnn_proj",
             "q_gain", "k_gain", "sinks", "w_attn_proj", "w_out", "g_mlp", "w_up", "w_down", "g_ple",
             "w_ple_gate", "w_ple_proj")
    wts = dict(zip(names, (g_mix, w_in, conv_w, conv_b, w_rg, b_rg, w_ig, b_ig, lru_lambda, w_rnn_proj, q_gain,
                           k_gain, sinks, w_attn_proj, w_out, g_mlp, w_up, w_down, g_ple, w_ple_gate, w_ple_proj)))
    mom1 = dict(zip(names, (m_g_mix, m_w_in, m_conv_w, m_conv_b, m_w_rg, m_b_rg, m_w_ig, m_b_ig, m_lru_lambda,
                            m_w_rnn_proj, m_q_gain, m_k_gain, m_sinks, m_w_attn_proj, m_w_out, m_g_mlp, m_w_up,
                            m_w_down, m_g_ple, m_w_ple_gate, m_w_ple_proj)))
    mom2 = dict(zip(names, (v_g_mix, v_w_in, v_conv_w, v_conv_b, v_w_rg, v_b_rg, v_w_ig, v_b_ig, v_lru_lambda,
                            v_w_rnn_proj, v_q_gain, v_k_gain, v_sinks, v_w_attn_proj, v_w_out, v_g_mlp, v_w_up,
                            v_w_down, v_g_ple, v_w_ple_gate, v_w_ple_proj)))
    n_seq, seq, _ = x.shape
    core = lax.axis_index("c").astype(jnp.int32).reshape(1)

    shards = {nm: wts[nm][0].astype(BF16) for nm in BIG}
    w_in_all, conv_all = _gather_two_level([shards["w_in"], conv_w[0]], name="gather_w_in")
    w = {nm: wts[nm] for nm in names if nm not in BIG}
    w["w_rg"], w["w_ig"] = w_rg[0], w_ig[0]
    w["conv_w"] = conv_all.transpose(1, 0, 2).reshape(CONV_W, D_MODEL)
    w["w_in"] = _full_weight("w_in", w_in_all)
    comm = _StepExchanges(shards, core)
    loss_sum, grad_x, g = _local_step(
        x.reshape(n_seq * seq, D_MODEL), p.reshape(n_seq * seq, PLE_DIM), loss_target.reshape(n_seq * seq, D_MODEL),
        w, n_seq=n_seq, seq=seq, comm=comm)
    del loss_sum

    res = {}
    for batch, batch_names in BATCHES.items():
        for nm, summed in zip(batch_names, comm.summed[batch]):
            res[nm] = _adamw(summed, wts[nm][0], mom1[nm][0], mom2[nm][0], name="adamw_" + nm)
    g_mix_parts, = _exchange([g["g_mix"]], ["gather"], name="gather_g_mix")
    res["g_mix"] = [r[0] for r in _adamw(g_mix_parts, g_mix, m_g_mix, v_g_mix, name="adamw_g_mix")]
    small_names = [nm for nm, _ in SMALL if nm != LOSS_ROW]
    full_small = {}
    for src, key in ((wts, "w"), (mom1, "m"), (mom2, "v")):
        vals = {nm: src[nm][0] for nm in small_names}
        vals[LOSS_ROW] = jnp.zeros((1,), F32)
        full_small[key] = _pack_small(vals)
    small_res = _adamw(comm.summed[SMALL_BATCH][0],full_small["w"], full_small["m"], full_small["v"], name="adamw_small")
    shapes = {nm: wts[nm].shape[1:] for nm in small_names}
    shapes[LOSS_ROW] = (1,)
    small_out = [_unpack_small(r, shapes) for r in small_res]
    for nm in small_names:
        res[nm] = [so[nm] for so in small_out]
    loss = small_out[0][LOSS_ROW][0] * (0.5 / D_MODEL)

    outs = [loss, grad_x.reshape(n_seq, seq, D_MODEL)]
    for k in range(4):
        outs.extend(res[nm][k][None] for nm in names)
    return tuple(outs)
```

```python
import functools
from typing import Callable, NamedTuple

import jax
import jax.numpy as jnp
from jax import lax
from jax.experimental import pallas as pl
from jax.experimental.pallas import tpu as pltpu

F32 = jnp.float32
BF16 = jnp.bfloat16

N_DEV = 8
D_MODEL = 1024
RNN_BLOCK_W = 64
CONV_W = 4
LRU_C = 8.0
HEAD_DIM = 64
N_Q_HEADS = 16
N_KV_HEADS = 4
KV_W = N_KV_HEADS * HEAD_DIM
WINDOW = 128
ROPE_THETA = 10000.0
D_FF = 4096
PLE_DIM = 256
NORM_EPS = 1e-6
IN_TOTAL = 5632
COL_RNN_END, COL_ATTN_END = 2048, 3584
ATTN_W = COL_ATTN_END - COL_RNN_END
ATTN_K_AT, ATTN_V_AT = 1024, 1280

ADAM_LR = 0.001
ADAM_B1 = 0.9
ADAM_B2 = 0.999
ADAM_EPS = 1e-08
ADAM_WD = 0.01
ADAM_STEP = 10

LANES = 128
SUBLANES = 8
RNN_TILE = 256
VMEM_LIMIT = 48 * 1024 * 1024
NEG_BIG = -1e30


def _params(*sem):
    return pltpu.CompilerParams(dimension_semantics=sem if sem else None, vmem_limit_bytes=VMEM_LIMIT)


def _sig(x):
    return 0.5 * jnp.tanh(0.5 * x) + 0.5


def _dot_nt(a, b):
    return lax.dot_general(a, b, (((1,), (1,)), ((), ())), preferred_element_type=F32)


def _dot_tn(a, b):
    return lax.dot_general(a, b, (((0,), (0,)), ((), ())), preferred_element_type=F32)


class _Xfer:
    def __init__(self, start, wait):
        self.start, self.wait = start, wait


class _Hosted(NamedTuple):
    srcs: tuple
    out_shape: tuple
    n_sems: int
    plan: Callable
    aliases: tuple = ()


def _merge_hosted(parts):
    parts = [p for p in parts if p is not None]
    if len(parts) <= 1:
        return parts[0] if parts else None
    src_at, dst_at, sem_at, aliases = [0], [0], [0], []
    for p in parts:
        aliases += [(i + src_at[-1], j + dst_at[-1]) for i, j in p.aliases]
        src_at.append(src_at[-1] + len(p.srcs))
        dst_at.append(dst_at[-1] + len(p.out_shape))
        sem_at.append(sem_at[-1] + p.n_sems)

    def plan(src, dst, send_sems, recv_sems, local_sems, first_sem):
        copies = []
        for k, p in enumerate(parts):
            copies += p.plan(src[src_at[k]:src_at[k + 1]], dst[dst_at[k]:dst_at[k + 1]], send_sems, recv_sems,
                             local_sems, first_sem + sem_at[k])
        return copies

    return _Hosted(tuple(a for p in parts for a in p.srcs), tuple(s for p in parts for s in p.out_shape),
                   sem_at[-1], plan, tuple(aliases))


def _call(body, *, grid, in_specs, out_specs, out_shape, args, name, semantics, scratch_shapes=(), hosted=None):
    if hosted is None:
        outs = pl.pallas_call(body, grid=grid, in_specs=list(in_specs), out_specs=list(out_specs),
                              out_shape=list(out_shape), scratch_shapes=list(scratch_shapes),
                              compiler_params=_params(*semantics), name=name)(*args)
        return list(outs), []
    counts = (len(in_specs), len(hosted.srcs), len(out_specs), len(hosted.out_shape), len(scratch_shapes), 3)

    def wrapped(*refs):
        at, groups = 0, []
        for count in counts:
            groups.append(refs[at:at + count])
            at += count
        ins, srcs, outs, dsts, scratch, sems = groups
        copies = hosted.plan(srcs, dsts, *sems, 0)
        ids = [pl.program_id(axis) for axis in range(len(grid))]
        first = functools.reduce(jnp.logical_and, [i == 0 for i in ids])
        last = functools.reduce(jnp.logical_and, [i == g - 1 for i, g in zip(ids, grid)])

        @pl.when(first)
        def _():
            for cp in copies:
                cp.start()

        body(*ins, *outs, *scratch)

        @pl.when(last)
        def _():
            for cp in copies:
                cp.wait()

    any_spec = pl.BlockSpec(memory_space=pl.ANY)
    sems = [pltpu.SemaphoreType.DMA((hosted.n_sems,))] * 3
    outs = pl.pallas_call(
        wrapped, grid=grid, in_specs=list(in_specs) + [any_spec] * counts[1],
        out_specs=list(out_specs) + [any_spec] * counts[3], out_shape=list(out_shape) + list(hosted.out_shape),
        scratch_shapes=list(scratch_shapes) + sems, compiler_params=_params(*["arbitrary"] * len(grid)),
        input_output_aliases={counts[0] + i: counts[2] + j for i, j in hosted.aliases},
        name=name)(*args, *hosted.srcs)
    return list(outs[:counts[2]]), list(outs[counts[2]:])


def _dividing_tile(n, want):
    tile = min(want, n)
    while n % tile:
        tile -= LANES
    return tile


def _matmul(a, b, *, mode, tm, tn, out_dtypes, name, epilogue=None, extras=(), hosted=None, b_cols=None,
            row_vecs=(), n_row_sums=0):
    a_parts = tuple(a) if isinstance(a, (tuple, list)) else (a,)
    b_parts = tuple(b) if isinstance(b, (tuple, list)) else (b,)
    assert len(a_parts) == len(b_parts) and (mode == "nt" or len(a_parts) == 1)
    n_parts = len(a_parts)
    m = a_parts[0].shape[0]
    if b_cols is None:
        b_cols = [(0, bp.shape[1]) for bp in b_parts]
    n = b_cols[0][1] if mode == "nn" else b_parts[0].shape[0]
    tm, tn = min(tm, m), _dividing_tile(n, tn)
    n_extra = len(extras) + len(row_vecs)
    n_tiles_out = len(out_dtypes)
    assert n_row_sums == 0 or n == tn

    def body(*refs):
        a_refs, b_refs = refs[:n_parts], refs[n_parts:2 * n_parts]
        rest = refs[2 * n_parts:]
        extra_refs, out_refs = rest[:n_extra], rest[n_extra:]
        if mode == "nn":
            acc = jnp.dot(a_refs[0][...], b_refs[0][...], preferred_element_type=F32)
        else:
            acc = _dot_nt(a_refs[0][...], b_refs[0][...])
            for a_ref, b_ref in zip(a_refs[1:], b_refs[1:]):
                acc = acc + _dot_nt(a_ref[...], b_ref[...])
        res = epilogue(acc, *[e[...] for e in extra_refs]) if epilogue is not None else (acc,)
        for o_ref, r in zip(out_refs[:n_tiles_out], res):
            o_ref[...] = r.astype(o_ref.dtype)
        if n_row_sums:
            @pl.when(pl.program_id(0) == 0)
            def _():
                for o_ref in out_refs[n_tiles_out:]:
                    o_ref[...] = jnp.zeros_like(o_ref)

            for o_ref, r in zip(out_refs[n_tiles_out:], res[n_tiles_out:]):
                o_ref[...] += r

    a_specs = [pl.BlockSpec((tm, ap.shape[1]), lambda i, j: (i, 0)) for ap in a_parts]
    if mode == "nn":
        assert b_cols[0][0] % tn == 0
        first = b_cols[0][0] // tn
        b_specs = [pl.BlockSpec((b_parts[0].shape[0], tn), lambda i, j: (0, first + j))]
    else:
        assert all(at % width == 0 for at, width in b_cols)
        b_specs = [pl.BlockSpec((tn, width), functools.partial(lambda i, j, blk: (j, blk), blk=at // width))
                   for at, width in b_cols]
    tile = pl.BlockSpec((tm, tn), lambda i, j: (i, j))
    row = pl.BlockSpec((1, tn), lambda i, j: (0, j))
    outs, landed = _call(
        body,
        grid=(m // tm, n // tn),
        in_specs=a_specs + b_specs + [tile] * len(extras) + [row] * len(row_vecs),
        out_specs=[tile] * n_tiles_out + [row] * n_row_sums,
        out_shape=[jax.ShapeDtypeStruct((m, n), dt) for dt in out_dtypes]
        + [jax.ShapeDtypeStruct((1, n), F32)] * n_row_sums,
        args=(*a_parts, *b_parts, *extras, *row_vecs), name=name,
        semantics=("arbitrary" if n_row_sums else "parallel", "arbitrary"), hosted=hosted)
    if hosted is not None:
        return (*outs, landed)
    return outs[0] if len(outs) == 1 else outs


def _matmul_tn(a, b, *, tk, tn, tt, name, slot_cols=None):
    t, k = a.shape
    n = b.shape[1]
    tk, tn, tt = min(tk, k), _dividing_tile(n, tn), min(tt, t)

    def body(a_ref, b_ref, o_ref):
        @pl.when(pl.program_id(2) == 0)
        def _():
            o_ref[...] = jnp.zeros_like(o_ref)

        if slot_cols is None:
            o_ref[...] += _dot_tn(a_ref[...], b_ref[...])
        else:
            av = a_ref[...]
            for s in range(tn // slot_cols):
                o_ref[s] += _dot_tn(av, b_ref[:, s * slot_cols:(s + 1) * slot_cols])

    if slot_cols is not None:
        out_spec = pl.BlockSpec((tn // slot_cols, tk, slot_cols), lambda i, j, s: (j, i, 0))
        out_shape = jax.ShapeDtypeStruct((n // slot_cols, k, slot_cols), F32)
    else:
        out_spec = pl.BlockSpec((tk, tn), lambda i, j, s: (i, j))
        out_shape = jax.ShapeDtypeStruct((k, n), F32)
    return pl.pallas_call(
        body,
        grid=(k // tk, n // tn, t // tt),
        in_specs=[pl.BlockSpec((tt, tk), lambda i, j, s: (s, i)), pl.BlockSpec((tt, tn), lambda i, j, s: (s, j))],
        out_specs=out_spec,
        out_shape=out_shape,
        compiler_params=_params("parallel", "parallel", "arbitrary"),
        name=name,
    )(a, b)


def _matmul_tn_multi(a, bs, *, tt, name):
    t, k = a.shape
    tt = min(tt, t)
    n_b = len(bs)

    def body(a_ref, *refs):
        b_refs, o_refs = refs[:n_b], refs[n_b:]

        @pl.when(pl.program_id(0) == 0)
        def _():
            for o_ref in o_refs:
                o_ref[...] = jnp.zeros_like(o_ref)

        a_t = a_ref[...].T
        for b_ref, o_ref in zip(b_refs, o_refs):
            o_ref[...] += jnp.dot(a_t, b_ref[...], preferred_element_type=F32)

    return pl.pallas_call(
        body,
        grid=(t // tt,),
        in_specs=[pl.BlockSpec((tt, k), lambda s: (s, 0))] + [pl.BlockSpec((tt, b.shape[1]), lambda s: (s, 0)) for b in bs],
        out_specs=[pl.BlockSpec((k, b.shape[1]), lambda s: (0, 0)) for b in bs],
        out_shape=[jax.ShapeDtypeStruct((k, b.shape[1]), F32) for b in bs],
        compiler_params=_params("arbitrary"),
        name=name,
    )(a, *bs)


def _rmsnorm_fwd(x, g, *, name):
    t, d = x.shape
    tm = min(512, t)

    def body(x_ref, g_ref, o_ref):
        xv = x_ref[...]
        r = lax.rsqrt(jnp.mean(xv * xv, axis=-1, keepdims=True) + NORM_EPS)
        o_ref[...] = (xv * r * g_ref[...]).astype(BF16)

    return pl.pallas_call(
        body,
        grid=(t // tm,),
        in_specs=[pl.BlockSpec((tm, d), lambda i: (i, 0)), pl.BlockSpec((1, d), lambda i: (0, 0))],
        out_specs=pl.BlockSpec((tm, d), lambda i: (i, 0)),
        out_shape=jax.ShapeDtypeStruct((t, d), BF16),
        compiler_params=_params("parallel"),
        name=name,
    )(x, g)


def _rmsnorm_bwd_rows(dy, x, dres, g):
    r = lax.rsqrt(jnp.mean(x * x, axis=-1, keepdims=True) + NORM_EPS)
    xr = x * r
    gy = dy * g
    dx = dres + r * (gy - xr * jnp.mean(gy * xr, axis=-1, keepdims=True))
    return dx, jnp.sum(dy * xr, axis=0, keepdims=True)


def _rmsnorm_bwd(x, g, dy, dres, *, name, want_bf16, hosted=None):
    t, d = x.shape
    tm = min(256, t)

    def body(x_ref, g_ref, dy_ref, dres_ref, *out_refs):
        dx_ref, dg_ref = out_refs[0], out_refs[-1]
        dx, dgain = _rmsnorm_bwd_rows(dy_ref[...], x_ref[...], dres_ref[...], g_ref[...])
        dx_ref[...] = dx
        if want_bf16:
            out_refs[1][...] = dx.astype(BF16)

        @pl.when(pl.program_id(0) == 0)
        def _():
            dg_ref[...] = jnp.zeros_like(dg_ref)

        dg_ref[...] += dgain

    tile = pl.BlockSpec((tm, d), lambda i: (i, 0))
    vec = pl.BlockSpec((1, d), lambda i: (0, 0))
    out_specs = [tile] + ([tile] if want_bf16 else []) + [vec]
    out_shape = [jax.ShapeDtypeStruct((t, d), F32)] + ([jax.ShapeDtypeStruct((t, d), BF16)] if want_bf16 else [])
    out_shape.append(jax.ShapeDtypeStruct((1, d), F32))
    outs, landed = _call(body, grid=(t // tm,), in_specs=[tile, vec, tile, tile], out_specs=out_specs,
                         out_shape=out_shape, args=(x, g, dy, dres), name=name, semantics=("arbitrary",), hosted=hosted)
    return (*outs, landed) if hosted is not None else outs


def _softplus_neg(lam):
    z = -lam
    return jnp.maximum(z, 0.0) + jnp.log1p(jnp.exp(-jnp.abs(z)))


def _neg_expm1(y, exp_half_y):
    series = -y * (1.0 + y * 0.5 * (1.0 + y * (1.0 / 3.0) * (1.0 + y * 0.25 * (1.0 + y * 0.2))))
    return jnp.where(y > -0.0625, series, 1.0 - exp_half_y * exp_half_y)


def _gelu_parts(x):
    c = 0.7978845608028654
    u = c * (x + 0.044715 * x * x * x)
    th = jnp.tanh(u)
    gel = 0.5 * x * (1.0 + th)
    dgel = 0.5 * (1.0 + th) + 0.5 * x * (1.0 - th * th) * c * (1.0 + 3.0 * 0.044715 * x * x)
    return gel, dgel


def _shift_down(v, k, rows):
    return jnp.where(rows < k, 0.0, pltpu.roll(v, k, 0))


def _shift_up(v, k, rows, n):
    return jnp.where(rows >= n - k, 0.0, pltpu.roll(v, n - k, 0))


def _scan_within_groups(a, b, rows, *, reverse):
    n = a.shape[0]
    in_group = rows & (SUBLANES - 1)
    for s in (1, 2, 4):
        if reverse:
            inside, shift = in_group < SUBLANES - s, n - s
        else:
            inside, shift = in_group >= s, s
        b = b + a * jnp.where(inside, pltpu.roll(b, shift, 0), 0.0)
        a = a * jnp.where(inside, pltpu.roll(a, shift, 0), 1.0)
    return a, b


def _rnn_gates(xc, wrg, brg, wig, big, lam):
    xcb = xc.astype(BF16)
    r = _sig(jnp.dot(xcb, wrg, preferred_element_type=F32) + brg)
    i = _sig(jnp.dot(xcb, wig, preferred_element_type=F32) + big)
    sp = _softplus_neg(lam)
    log_a = -LRU_C * r * sp
    a = jnp.exp(log_a)
    mult = jnp.sqrt(_neg_expm1(2.0 * log_a, a))
    return xcb, r, i, sp, a, mult


def _conv_fwd(xv, cw, cb, rows):
    return (cb + _shift_down(xv, 3, rows) * cw[0:1, :] + _shift_down(xv, 2, rows) * cw[1:2, :]
            + _shift_down(xv, 1, rows) * cw[2:3, :] + xv * cw[3:4, :])


def _rnn_fwd(z, conv_w, conv_b, wrg_bd, b_rg, wig_bd, b_ig, lam, *, n_seq, seq, hosted=None):
    t = n_seq * seq
    ct = RNN_TILE
    n_ct = D_MODEL // ct

    def body(x_ref, g_ref, cw_ref, cb_ref, wrg_ref, brg_ref, wig_ref, big_ref, lam_ref,
             xc_ref, hr_ref, ya_ref, a_s, b_s):
        rows = lax.broadcasted_iota(jnp.int32, (seq, ct), 0)
        xc = _conv_fwd(x_ref[...], cw_ref[...], cb_ref[...], rows)
        _, r, i, sp, a, mult = _rnn_gates(xc, wrg_ref[...], brg_ref[...], wig_ref[...], big_ref[...], lam_ref[...])
        a_s[...], b_s[...] = _scan_within_groups(a, mult * (i * xc), rows, reverse=False)

        def step(j, carry):
            r0 = pl.multiple_of(j * SUBLANES, SUBLANES)
            h = b_s[pl.ds(r0, SUBLANES), :] + a_s[pl.ds(r0, SUBLANES), :] * carry
            hr_ref[pl.ds(r0, SUBLANES), :] = h
            return h[SUBLANES - 1:SUBLANES, :]

        lax.fori_loop(0, seq // SUBLANES, step, jnp.zeros((1, ct), F32), unroll=4)
        gel, _ = _gelu_parts(g_ref[...])
        xc_ref[...] = xc
        ya_ref[...] = (hr_ref[...] * gel).astype(BF16)

    vec = pl.BlockSpec((1, ct), lambda b, c: (0, c))
    gate_w = pl.BlockSpec((None, ct, ct), lambda b, c: (c, 0, 0))
    tile = pl.BlockSpec((seq, ct), lambda b, c: (b, c))
    outs, landed = _call(
        body,
        grid=(n_seq, n_ct),
        in_specs=[
            pl.BlockSpec((seq, ct), lambda b, c: (b, c)),
            pl.BlockSpec((seq, ct), lambda b, c: (b, n_ct + c)),
            pl.BlockSpec((CONV_W, ct), lambda b, c: (0, c)), vec, gate_w, vec, gate_w, vec, vec,
        ],
        out_specs=[tile, tile, tile],
        out_shape=[jax.ShapeDtypeStruct((t, D_MODEL), F32), jax.ShapeDtypeStruct((t, D_MODEL), F32),
                   jax.ShapeDtypeStruct((t, D_MODEL), BF16)],
        scratch_shapes=[pltpu.VMEM((seq, ct), F32), pltpu.VMEM((seq, ct), F32)],
        args=(z, z, conv_w, conv_b, wrg_bd, b_rg, wig_bd, b_ig, lam), name="rnn_fwd",
        semantics=("parallel", "parallel"), hosted=hosted)
    return (*outs, landed) if hosted is not None else outs


def _rnn_bwd(dya, z, xc, hr, conv_w, wrg_bd, b_rg, wig_bd, b_ig, lam, *, n_seq, seq, hosted=None):
    t = n_seq * seq
    ct = RNN_TILE
    n_ct = D_MODEL // ct

    def body(dya_ref, x_ref, g_ref, xc_ref, hr_ref, cw_ref, wrg_ref, brg_ref, wig_ref, big_ref, lam_ref,
             dx_ref, dg_ref, dwrg_ref, dwig_ref, vec_ref, a_s, d_s, g_s):
        rows = lax.broadcasted_iota(jnp.int32, (seq, ct), 0)
        xv, xc, hr, dyv = x_ref[...], xc_ref[...], hr_ref[...], dya_ref[...]
        lamv = lam_ref[...]
        gel, dgel = _gelu_parts(g_ref[...])
        dg_ref[...] = (dyv * hr * dgel).astype(BF16)
        xcb, r, i, sp, a, mult = _rnn_gates(xc, wrg_ref[...], brg_ref[...], wig_ref[...], big_ref[...], lamv)
        a_s[...], d_s[...] = _scan_within_groups(_shift_up(a, 1, rows, seq), dyv * gel, rows, reverse=True)

        def step(k, carry):
            r0 = pl.multiple_of((seq // SUBLANES - 1 - k) * SUBLANES, SUBLANES)
            gs = d_s[pl.ds(r0, SUBLANES), :] + a_s[pl.ds(r0, SUBLANES), :] * carry
            g_s[pl.ds(r0, SUBLANES), :] = gs
            return gs[0:1, :]

        lax.fori_loop(0, seq // SUBLANES, step, jnp.zeros((1, ct), F32), unroll=4)
        gsum = g_s[...]
        gated = i * xc
        d_log_a = gsum * _shift_down(hr, 1, rows) * a - gsum * gated * (a * a / mult)
        d_gated = gsum * mult
        d_pre_r = (d_log_a * (-LRU_C) * sp) * r * (1.0 - r)
        d_pre_i = (d_gated * xc) * i * (1.0 - i)
        dprb, dpib = d_pre_r.astype(BF16), d_pre_i.astype(BF16)
        dxc = d_gated * i + _dot_nt(dprb, wrg_ref[...]) + _dot_nt(dpib, wig_ref[...])
        cw = cw_ref[...]
        dx = (dxc * cw[3:4, :] + _shift_up(dxc, 1, rows, seq) * cw[2:3, :]
              + _shift_up(dxc, 2, rows, seq) * cw[1:2, :] + _shift_up(dxc, 3, rows, seq) * cw[0:1, :])
        dx_ref[...] = dx.astype(BF16)

        @pl.when(pl.program_id(1) == 0)
        def _():
            dwrg_ref[...] = jnp.zeros_like(dwrg_ref)
            dwig_ref[...] = jnp.zeros_like(dwig_ref)
            vec_ref[...] = jnp.zeros_like(vec_ref)

        dwrg_ref[...] += _dot_tn(xcb, dprb)
        dwig_ref[...] += _dot_tn(xcb, dpib)

        def colsum(v):
            return jnp.sum(v, axis=0, keepdims=True)

        d_sp = colsum(d_log_a * (-LRU_C) * r)
        vec_ref[0:1, :] += colsum(d_pre_r)
        vec_ref[1:2, :] += colsum(d_pre_i)
        vec_ref[2:3, :] += d_sp * (-_sig(-lamv))
        vec_ref[3:4, :] += colsum(dxc)
        vec_ref[4:5, :] += colsum(dxc * _shift_down(xv, 3, rows))
        vec_ref[5:6, :] += colsum(dxc * _shift_down(xv, 2, rows))
        vec_ref[6:7, :] += colsum(dxc * _shift_down(xv, 1, rows))
        vec_ref[7:8, :] += colsum(dxc * xv)

    vec = pl.BlockSpec((1, ct), lambda c, b: (0, c))
    gate_w = pl.BlockSpec((None, ct, ct), lambda c, b: (c, 0, 0))
    tile = pl.BlockSpec((seq, ct), lambda c, b: (b, c))
    outs, landed = _call(
        body,
        grid=(n_ct, n_seq),
        in_specs=[
            tile,
            pl.BlockSpec((seq, ct), lambda c, b: (b, c)),
            pl.BlockSpec((seq, ct), lambda c, b: (b, n_ct + c)),
            tile, tile,
            pl.BlockSpec((CONV_W, ct), lambda c, b: (0, c)), gate_w, vec, gate_w, vec, vec,
        ],
        out_specs=[tile, tile, gate_w, gate_w, pl.BlockSpec((8, ct), lambda c, b: (0, c))],
        out_shape=[jax.ShapeDtypeStruct((t, D_MODEL), BF16), jax.ShapeDtypeStruct((t, D_MODEL), BF16),
                   jax.ShapeDtypeStruct((n_ct, ct, ct), F32), jax.ShapeDtypeStruct((n_ct, ct, ct), F32),
                   jax.ShapeDtypeStruct((8, D_MODEL), F32)],
        scratch_shapes=[pltpu.VMEM((seq, ct), F32)] * 3,
        args=(dya, z, z, xc, hr, conv_w, wrg_bd, b_rg, wig_bd, b_ig, lam), name="rnn_bwd",
        semantics=("parallel", "arbitrary"), hosted=hosted)
    return (*outs, landed) if hosted is not None else outs


def _split_hi_lo(x):
    hi = x.astype(BF16)
    return hi, (x - hi.astype(F32)).astype(BF16)


def _dot_split(x, m_twice):
    hi, lo = _split_hi_lo(x)
    return jnp.dot(jnp.concatenate([hi, lo], axis=1), m_twice, preferred_element_type=F32)


def _head_matrices(width):
    ec = ((lax.broadcasted_iota(jnp.int32, (2 * width, LANES), 0) & (width - 1)) // HEAD_DIM
          == lax.broadcasted_iota(jnp.int32, (2 * width, LANES), 1))
    ee = (lax.broadcasted_iota(jnp.int32, (2 * LANES, width), 1) // HEAD_DIM
          == (lax.broadcasted_iota(jnp.int32, (2 * LANES, width), 0) & (LANES - 1)))
    return jnp.where(ec, 1.0, 0.0).astype(BF16), jnp.where(ee, 1.0, 0.0).astype(BF16)


def _swap_halves(y):
    w = y.shape[1]
    first = (lax.broadcasted_iota(jnp.int32, y.shape, 1) % HEAD_DIM) < HEAD_DIM // 2
    return jnp.where(first, pltpu.roll(y, w - HEAD_DIM // 2, 1), pltpu.roll(y, HEAD_DIM // 2, 1))


def _normrope_fwd(x, gain, cos_t, sin_t, ec, ee):
    w = x.shape[1]
    rs = _dot_split(lax.rsqrt(_dot_split(x * x, ec) * (1.0 / HEAD_DIM) + NORM_EPS), ee)
    nx = x * rs
    y = nx * gain
    reps = w // LANES
    out = y * jnp.tile(cos_t, (1, reps)) + _swap_halves(y) * jnp.tile(sin_t, (1, reps))
    return out, nx, rs


def _normrope_bwd(dout, nx, rs, gain, cos_t, sin_t, ec, ee):
    w = dout.shape[1]
    reps = w // LANES
    dy = dout * jnp.tile(cos_t, (1, reps)) + _swap_halves(dout * jnp.tile(sin_t, (1, reps)))
    dgain = jnp.sum(dy * nx, axis=0, keepdims=True)
    dn = dy * gain
    seg = _dot_split(_dot_split(dn * nx, ec) * (1.0 / HEAD_DIM), ee)
    return rs * (dn - nx * seg), dgain


def _pair_operand(t, group):
    chunk = t[:, (group // 2) * LANES:(group // 2 + 1) * LANES]
    low = lax.broadcasted_iota(jnp.int32, chunk.shape, 1) < HEAD_DIM
    rolled = pltpu.roll(chunk, HEAD_DIM, 1)
    return jnp.where(low, chunk, rolled) if group % 2 == 0 else jnp.where(low, rolled, chunk)


GROUP = N_Q_HEADS // N_KV_HEADS
GROUP_W = GROUP * HEAD_DIM


def _replicate_head(t, group):
    return jnp.tile(_pair_operand(t, group), (1, 2))


def _head_blocks(t):
    seg = lax.broadcasted_iota(jnp.int32, t.shape, 1) // HEAD_DIM
    return jnp.concatenate([jnp.where(seg == h, t, 0.0) for h in range(GROUP)], axis=0)


def _stack_heads(t_t, rows):
    return jnp.concatenate([t_t[:, h * rows:(h + 1) * rows] for h in range(GROUP)], axis=0)


def _head_rows(mat_t, group):
    return jnp.concatenate([mat_t[GROUP * group + h:GROUP * group + h + 1, :] for h in range(GROUP)], axis=1)


def _window_masks(blk):
    key = lax.broadcasted_iota(jnp.int32, (blk, GROUP * blk), 0)
    query = lax.broadcasted_iota(jnp.int32, (blk, GROUP * blk), 1) & (blk - 1)
    return key > query, key <= query


def _mask_window(t, before_ok, own_ok, fill):
    blk = t.shape[0] // 2
    return jnp.concatenate([jnp.where(before_ok, t[:blk], fill), jnp.where(own_ok, t[blk:], fill)], axis=0)


def _attn_fwd(z, cos_t, sin_t, q_gain_t, k_gain_t, sinks_t, *, n_seq, seq, hosted=None):
    t = n_seq * seq
    blk = WINDOW
    nb = seq // blk

    def body(q_ref, kp_ref, kc_ref, vp_ref, vc_ref, cosc_ref, sinc_ref, cosp_ref, sinp_ref, qg_ref, kg_ref, sk_ref,
             o_ref, l_ref):
        n = pl.program_id(1)
        ecq, eeq = _head_matrices(D_MODEL)
        eck, eek = _head_matrices(KV_W)
        cosc, sinc = cosc_ref[...], sinc_ref[...]
        qh, _, _ = _normrope_fwd(q_ref[...], qg_ref[...], cosc, sinc, ecq, eeq)
        qh = qh * (HEAD_DIM ** -0.5)
        kc, _, _ = _normrope_fwd(kc_ref[...], kg_ref[...], cosc, sinc, eck, eek)
        kp, _, _ = _normrope_fwd(kp_ref[...], kg_ref[...], cosp_ref[...], sinp_ref[...], eck, eek)
        kcat = jnp.concatenate([kp, kc], axis=0)
        vcat = jnp.concatenate([vp_ref[...], vc_ref[...]], axis=0)
        above, causal = _window_masks(blk)
        above = above & (n > 0)
        head_row = lax.broadcasted_iota(jnp.int32, (blk, blk), 0)
        sk_t = jnp.broadcast_to(sk_ref[...], (blk, LANES)).T
        vcat_t = vcat.T.astype(BF16)
        lmat = jnp.zeros((blk, blk), F32)
        groups = range(N_KV_HEADS)
        cols = [slice(g * GROUP_W, (g + 1) * GROUP_W) for g in groups]
        scores = [_dot_nt(_replicate_head(kcat, g).astype(BF16), _head_blocks(qh[:, cols[g]]).astype(BF16))
                  for g in groups]
        probs = []
        for g in groups:
            s = _mask_window(scores[g], above, causal, NEG_BIG)
            sink = _head_rows(sk_t, g)
            m = jnp.maximum(jnp.max(s, axis=0, keepdims=True), sink)
            e = jnp.exp(s - m)
            den = jnp.sum(e, axis=0, keepdims=True) + jnp.exp(sink - m)
            probs.append((e * (1.0 / den)).astype(BF16))
            lse = m + jnp.log(den)
            for h in range(GROUP):
                lmat = lmat + jnp.where(head_row == GROUP * g + h, lse[:, h * blk:(h + 1) * blk], 0.0)
        for g in groups:
            out_t = jnp.dot(vcat_t[g * HEAD_DIM:(g + 1) * HEAD_DIM], probs[g], preferred_element_type=F32)
            o_ref[:, cols[g]] = _stack_heads(out_t, blk).T.astype(BF16)
        l_ref[...] = lmat

    def row(b, n):
        return b * nb + n

    def prev(b, n):
        return b * nb + jnp.maximum(n - 1, 0)

    kw = KV_W
    tab_c = pl.BlockSpec((blk, LANES), lambda b, n: (n, 0))
    tab_p = pl.BlockSpec((blk, LANES), lambda b, n: (jnp.maximum(n - 1, 0), 0))
    outs, landed = _call(
        body,
        grid=(n_seq, nb),
        in_specs=[
            pl.BlockSpec((blk, D_MODEL), lambda b, n: (row(b, n), 0)),
            pl.BlockSpec((blk, kw), lambda b, n: (prev(b, n), ATTN_K_AT // kw)),
            pl.BlockSpec((blk, kw), lambda b, n: (row(b, n), ATTN_K_AT // kw)),
            pl.BlockSpec((blk, kw), lambda b, n: (prev(b, n), ATTN_V_AT // kw)),
            pl.BlockSpec((blk, kw), lambda b, n: (row(b, n), ATTN_V_AT // kw)),
            tab_c, tab_c, tab_p, tab_p,
            pl.BlockSpec((1, D_MODEL), lambda b, n: (0, 0)),
            pl.BlockSpec((1, kw), lambda b, n: (0, 0)),
            pl.BlockSpec((1, LANES), lambda b, n: (0, 0)),
        ],
        out_specs=[pl.BlockSpec((blk, D_MODEL), lambda b, n: (row(b, n), 0)),
                   pl.BlockSpec((blk, LANES), lambda b, n: (row(b, n), 0))],
        out_shape=[jax.ShapeDtypeStruct((t, D_MODEL), BF16), jax.ShapeDtypeStruct((t, LANES), F32)],
        args=(z, z, z, z, z, cos_t, sin_t, cos_t, sin_t, q_gain_t, k_gain_t, sinks_t), name="attn_fwd",
        semantics=("parallel", "parallel"), hosted=hosted)
    return (*outs, landed) if hosted is not None else outs


def _attn_bwd(z, o, lse, do, cos_t, sin_t, q_gain_t, k_gain_t, sinks_t, *, n_seq, seq, hosted=None):
    t = n_seq * seq
    blk = WINDOW
    nb = seq // blk
    kw = KV_W
    scale = HEAD_DIM ** -0.5

    def body(qc_ref, qn_ref, kp_ref, kc_ref, vp_ref, vc_ref, oc_ref, on_ref, doc_ref, don_ref, lc_ref, ln_ref,
             cosc_ref, sinc_ref, cosp_ref, sinp_ref, cosn_ref, sinn_ref, qg_ref, kg_ref, sk_ref,
             dz_ref, vec_ref, dq_s):
        n = pl.program_id(1)
        ecq, eeq = _head_matrices(D_MODEL)
        eck, eek = _head_matrices(KV_W)
        cosc, sinc = cosc_ref[...], sinc_ref[...]
        qg, kg = qg_ref[...], kg_ref[...]
        qhc, nqc, rsqc = _normrope_fwd(qc_ref[...], qg, cosc, sinc, ecq, eeq)
        qhn, _, _ = _normrope_fwd(qn_ref[...], qg, cosn_ref[...], sinn_ref[...], ecq, eeq)
        khc, nkc, rskc = _normrope_fwd(kc_ref[...], kg, cosc, sinc, eck, eek)
        khp, _, _ = _normrope_fwd(kp_ref[...], kg, cosp_ref[...], sinp_ref[...], eck, eek)
        doc = doc_ref[...].astype(F32)
        don = don_ref[...].astype(F32)
        delc = _dot_split(doc * oc_ref[...].astype(F32), ecq)
        deln = _dot_split(don * on_ref[...].astype(F32), ecq)
        lc_t, ln_t, delc_t, deln_t = lc_ref[...], ln_ref[...], delc.T, deln.T
        above, causal = _window_masks(blk)
        above_c, above_n = above & (n > 0), above & (n < nb - 1)
        seg = lax.broadcasted_iota(jnp.int32, (blk, GROUP_W), 1) // HEAD_DIM
        lane = lax.broadcasted_iota(jnp.int32, (1, LANES), 1)
        sk_t = jnp.broadcast_to(sk_ref[...], (blk, LANES)).T
        dsink = jnp.zeros((1, LANES), F32)
        kcat = jnp.concatenate([khp, khc], axis=0)
        vcat = jnp.concatenate([vp_ref[...], vc_ref[...]], axis=0)
        kcat_t = kcat.T.astype(BF16)
        dkh = jnp.zeros((blk, GROUP_W), F32)
        dvh = jnp.zeros((blk, GROUP_W), F32)

        def fold_to(group, t):
            total = t + pltpu.roll(t, HEAD_DIM, 1)
            total = total + pltpu.roll(total, 2 * HEAD_DIM, 1)
            return jnp.where(seg == group, total, 0.0)

        groups = range(N_KV_HEADS)
        cols = [slice(g * GROUP_W, (g + 1) * GROUP_W) for g in groups]
        qsc, qsn = qhc * scale, qhn * scale
        qb_c = [_head_blocks(qsc[:, cols[g]]).astype(BF16) for g in groups]
        qb_n = [_head_blocks(qsn[:, cols[g]]).astype(BF16) for g in groups]
        dob_c = [_head_blocks(doc[:, cols[g]]).astype(BF16) for g in groups]
        dob_n = [_head_blocks(don[:, cols[g]]).astype(BF16) for g in groups]
        raw = []
        for g in groups:
            krep = _replicate_head(kcat, g).astype(BF16)
            vrep = _replicate_head(vcat, g).astype(BF16)
            raw.append((_dot_nt(krep, qb_c[g]), _dot_nt(vrep, dob_c[g]),
                        _dot_nt(krep[blk:], qb_n[g]), _dot_nt(vrep[blk:], dob_n[g])))
        cooked = []
        for g in groups:
            s_c, dp_c, s_n, dp_n = raw[g]
            l_row, d_row = _head_rows(lc_t, g), _head_rows(delc_t, g)
            p_c = _mask_window(jnp.exp(s_c - l_row), above_c, causal, 0.0)
            ds_c = (p_c * (dp_c - d_row)).astype(BF16)
            p_n = jnp.where(above_n, jnp.exp(s_n - _head_rows(ln_t, g)), 0.0)
            ds_n = (p_n * (dp_n - _head_rows(deln_t, g))).astype(BF16)
            cooked.append((p_c[blk:].astype(BF16), ds_c, p_n.astype(BF16), ds_n))
            p_sink = jnp.exp(_head_rows(sk_t, g) - l_row) * d_row
            for h in range(GROUP):
                dsink = dsink + jnp.where(lane == GROUP * g + h,
                                          -jnp.sum(p_sink[:, h * blk:(h + 1) * blk], axis=1, keepdims=True), 0.0)
        for g in groups:
            p_cb, ds_c, p_nb, ds_n = cooked[g]
            dq_t = jnp.dot(kcat_t[g * HEAD_DIM:(g + 1) * HEAD_DIM], ds_c, preferred_element_type=F32)
            dq_s[:, cols[g]] = _stack_heads(dq_t, blk).T * scale
            dk_rep = (jnp.dot(ds_c[blk:], qb_c[g], preferred_element_type=F32)
                      + jnp.dot(ds_n, qb_n[g], preferred_element_type=F32))
            dv_rep = (jnp.dot(p_cb, dob_c[g], preferred_element_type=F32)
                      + jnp.dot(p_nb, dob_n[g], preferred_element_type=F32))
            dkh = dkh + fold_to(g, dk_rep)
            dvh = dvh + fold_to(g, dv_rep)
        dq, dqg = _normrope_bwd(dq_s[...], nqc, rsqc, qg, cosc, sinc, ecq, eeq)
        dk, dkg = _normrope_bwd(dkh, nkc, rskc, kg, cosc, sinc, eck, eek)
        dz_ref[:, :ATTN_K_AT] = dq.astype(BF16)
        dz_ref[:, ATTN_K_AT:ATTN_V_AT] = dk.astype(BF16)
        dz_ref[:, ATTN_V_AT:] = dvh.astype(BF16)

        @pl.when(n == 0)
        def _():
            vec_ref[...] = jnp.zeros_like(vec_ref)

        vec_ref[0:1, :] += dqg
        vec_ref[1:2, 0:kw] += dkg
        vec_ref[2:3, 0:LANES] += dsink

    def row(b, n):
        return b * nb + n

    def prev(b, n):
        return b * nb + jnp.maximum(n - 1, 0)

    def nxt(b, n):
        return b * nb + jnp.minimum(n + 1, nb - 1)

    def tiles(width, col, which):
        return pl.BlockSpec((blk, width), lambda b, n: (which(b, n), col))

    def table(which):
        return pl.BlockSpec((blk, LANES), lambda b, n: (which(0, n), 0))

    outs, landed = _call(
        body,
        grid=(n_seq, nb),
        in_specs=[
            tiles(D_MODEL, 0, row), tiles(D_MODEL, 0, nxt),
            tiles(kw, ATTN_K_AT // kw, prev), tiles(kw, ATTN_K_AT // kw, row),
            tiles(kw, ATTN_V_AT // kw, prev), tiles(kw, ATTN_V_AT // kw, row),
            tiles(D_MODEL, 0, row), tiles(D_MODEL, 0, nxt),
            tiles(D_MODEL, 0, row), tiles(D_MODEL, 0, nxt),
            tiles(LANES, 0, row), tiles(LANES, 0, nxt),
            table(row), table(row), table(prev), table(prev), table(nxt), table(nxt),
            pl.BlockSpec((1, D_MODEL), lambda b, n: (0, 0)),
            pl.BlockSpec((1, kw), lambda b, n: (0, 0)),
            pl.BlockSpec((1, LANES), lambda b, n: (0, 0)),
        ],
        out_specs=[tiles(ATTN_W, 0, row), pl.BlockSpec((None, 8, D_MODEL), lambda b, n: (b, 0, 0))],
        out_shape=[jax.ShapeDtypeStruct((t, ATTN_W), BF16), jax.ShapeDtypeStruct((n_seq, 8, D_MODEL), F32)],
        scratch_shapes=[pltpu.VMEM((blk, D_MODEL), F32)],
        args=(z, z, z, z, z, z, o, o, do, do, lse, lse, cos_t, sin_t, cos_t, sin_t, cos_t, sin_t,
              q_gain_t, k_gain_t, sinks_t), name="attn_bwd", semantics=("parallel", "arbitrary"), hosted=hosted)
    return (*outs, landed) if hosted is not None else outs


MERGE_COLS = 512


def _merge_fwd(z, ya, yb):
    t = ya.shape[0]
    tm, tc = min(512, t), MERGE_COLS

    def body(ga_ref, gb_ref, ya_ref, yb_ref, o_ref):
        o_ref[...] = (_sig(ga_ref[...]) * ya_ref[...] + _sig(gb_ref[...]) * yb_ref[...]).astype(BF16)

    tile = pl.BlockSpec((tm, tc), lambda i, j: (i, j))
    return pl.pallas_call(
        body,
        grid=(t // tm, D_MODEL // tc),
        in_specs=[pl.BlockSpec((tm, tc), lambda i, j: (i, j)),
                  pl.BlockSpec((tm, tc), lambda i, j: (i, D_MODEL // tc + j)), tile, tile],
        out_specs=tile,
        out_shape=jax.ShapeDtypeStruct((t, D_MODEL), BF16),
        compiler_params=_params("parallel", "parallel"),
        name="merge_fwd",
    )(z, z, ya, yb)


def _merge_bwd(z, ya, yb, dmerged):
    t = ya.shape[0]
    tm, tc = min(512, t), MERGE_COLS

    def body(ga_ref, gb_ref, ya_ref, yb_ref, dm_ref, dya_ref, dyb_ref, dga_ref, dgb_ref):
        dm = dm_ref[...]
        sa, sb = _sig(ga_ref[...]), _sig(gb_ref[...])
        dya_ref[...] = (dm * sa).astype(BF16)
        dyb_ref[...] = (dm * sb).astype(BF16)
        dga_ref[...] = (dm * ya_ref[...] * sa * (1.0 - sa)).astype(BF16)
        dgb_ref[...] = (dm * yb_ref[...] * sb * (1.0 - sb)).astype(BF16)

    tile = pl.BlockSpec((tm, tc), lambda i, j: (i, j))
    return pl.pallas_call(
        body,
        grid=(t // tm, D_MODEL // tc),
        in_specs=[pl.BlockSpec((tm, tc), lambda i, j: (i, j)),
                  pl.BlockSpec((tm, tc), lambda i, j: (i, D_MODEL // tc + j)), tile, tile, tile],
        out_specs=[tile] * 4,
        out_shape=[jax.ShapeDtypeStruct((t, D_MODEL), BF16)] * 4,
        compiler_params=_params("parallel", "parallel"),
        name="merge_bwd",
    )(z, z, ya, yb, dmerged)


def _loss_head(x2, e, gt, target):
    t, d = x2.shape
    tm = min(256, t)

    def body(x_ref, e_ref, gt_ref, tg_ref, loss_ref, dx_ref, dgt_ref, de_ref):
        ev = e_ref[...]
        sg = _sig(gt_ref[...])
        diff = x_ref[...] + ev * sg - tg_ref[...]
        dx = diff * (1.0 / d)
        dx_ref[...] = dx
        dgt_ref[...] = (dx * ev * sg * (1.0 - sg)).astype(BF16)
        de_ref[...] = (dx * sg).astype(BF16)

        @pl.when(pl.program_id(0) == 0)
        def _():
            loss_ref[...] = jnp.zeros_like(loss_ref)

        loss_ref[...] += jnp.sum(jnp.sum(diff * diff, axis=1, keepdims=True), axis=0, keepdims=True)

    tile = pl.BlockSpec((tm, d), lambda i: (i, 0))
    return pl.pallas_call(
        body,
        grid=(t // tm,),
        in_specs=[tile] * 4,
        out_specs=[pl.BlockSpec((1, LANES), lambda i: (0, 0)), tile, tile, tile],
        out_shape=[jax.ShapeDtypeStruct((1, LANES), F32), jax.ShapeDtypeStruct((t, d), F32),
                   jax.ShapeDtypeStruct((t, d), BF16), jax.ShapeDtypeStruct((t, d), BF16)],
        compiler_params=_params("arbitrary"),
        name="loss_head",
    )(x2, e, gt, target)


def _rope_tables(seq):
    inv = ROPE_THETA ** (-jnp.arange(0, HEAD_DIM, 2, dtype=F32) / HEAD_DIM)
    ang = jnp.arange(seq, dtype=F32)[:, None] * inv[None, :]
    cos, sin = jnp.cos(ang), jnp.sin(ang)
    return jnp.tile(jnp.concatenate([cos, cos], axis=1), (1, 2)), jnp.tile(jnp.concatenate([-sin, sin], axis=1), (1, 2))


def _block_diag_tiles(w):
    per = RNN_TILE // RNN_BLOCK_W
    w4 = w.reshape(D_MODEL // RNN_TILE, per, RNN_BLOCK_W, RNN_BLOCK_W)
    eye = jnp.eye(per, dtype=w.dtype)
    dense = jnp.einsum("tpij,pq->tpiqj", w4, eye)
    return dense.reshape(D_MODEL // RNN_TILE, RNN_TILE, RNN_TILE).astype(BF16)


def _block_diag_extract(dense):
    per = RNN_TILE // RNN_BLOCK_W
    d5 = dense.reshape(D_MODEL // RNN_TILE, per, RNN_BLOCK_W, per, RNN_BLOCK_W)
    blocks = jnp.stack([d5[:, p, :, p, :] for p in range(per)], axis=1)
    return blocks.reshape(D_MODEL // RNN_BLOCK_W, RNN_BLOCK_W, RNN_BLOCK_W)


def _local_step(x, p, target, w, *, n_seq, seq, comm=None):
    w = dict(w)

    def run(tag, fn, *args, **kwargs):
        hosted = comm.host(tag) if comm is not None else None
        if hosted is None:
            return fn(*args, **kwargs)
        *outs, landed = fn(*args, hosted=hosted, **kwargs)
        comm.landed(tag, landed, w)
        return outs[0] if len(outs) == 1 else outs

    def ready(batch, grads, extra=None):
        if comm is not None:
            comm.ready(batch, grads, extra)

    cos_t, sin_t = _rope_tables(seq)
    q_gain_t = jnp.tile(w["q_gain"], (1, N_Q_HEADS))
    k_gain_t = jnp.tile(w["k_gain"], (1, N_KV_HEADS))
    sinks_t = jnp.pad(w["sinks"], ((0, 0), (0, LANES - N_Q_HEADS)))
    wrg_bd, wig_bd = _block_diag_tiles(w["w_rg"]), _block_diag_tiles(w["w_ig"])
    dims = dict(n_seq=n_seq, seq=seq)

    h = _rmsnorm_fwd(x, w["g_mix"], name="norm_mix")
    z_rnn = _matmul(h, w["w_in"], mode="nn", tm=1024, tn=1024, out_dtypes=[F32], name="mm_in_rnn",
                    b_cols=[(0, COL_RNN_END)])
    w_in_attn, w_in_gate = w["w_in"][:, COL_RNN_END:COL_ATTN_END], w["w_in"][:, COL_ATTN_END:]
    z_attn = _matmul(h, w_in_attn, mode="nn", tm=1024, tn=1024, out_dtypes=[F32], name="mm_in_attn")
    z_gate = _matmul(h, w_in_gate, mode="nn", tm=1024, tn=1024, out_dtypes=[F32], name="mm_in_gate")
    xc, hr, ya_in = run("rnn_fwd", _rnn_fwd, z_rnn, w["conv_w"], w["conv_b"], wrg_bd, w["b_rg"], wig_bd, w["b_ig"],
                        w["lru_lambda"], **dims)
    o, lse = run("attn_fwd", _attn_fwd, z_attn, cos_t, sin_t, q_gain_t, k_gain_t, sinks_t, **dims)
    ya = run("mm_rnn_proj", _matmul, ya_in, w["w_rnn_proj"], mode="nn", tm=1024, tn=1024, out_dtypes=[F32],
             name="mm_rnn_proj")
    yb = _matmul(o, w["w_attn_proj"], mode="nn", tm=1024, tn=1024, out_dtypes=[F32], name="mm_attn_proj")
    merged = _merge_fwd(z_gate, ya, yb)
    x1 = _matmul(merged, w["w_out"], mode="nn", tm=1024, tn=1024, out_dtypes=[F32], name="mm_out",
                 epilogue=lambda acc, res: (res + acc,), extras=(x,))
    hm = _rmsnorm_fwd(x1, w["g_mlp"], name="norm_mlp")
    act = _matmul(hm, w["w_up"], mode="nn", tm=1024, tn=1024, out_dtypes=[BF16], name="mm_up",
                  epilogue=lambda acc: (jnp.square(jnp.maximum(acc, 0.0)),))
    x2 = _matmul(act, w["w_down"], mode="nn", tm=512, tn=1024, out_dtypes=[F32], name="mm_down",
                 epilogue=lambda acc, res: (res + acc,), extras=(x1,))
    hp = _rmsnorm_fwd(x2, w["g_ple"], name="norm_ple")
    gt = _matmul(hp, w["w_ple_gate"], mode="nn", tm=1024, tn=1024, out_dtypes=[F32], name="mm_ple_gate")
    p_bf = p.astype(BF16)
    e = _matmul(p_bf, w["w_ple_proj"], mode="nn", tm=1024, tn=1024, out_dtypes=[F32], name="mm_ple_proj")
    loss_row, dx3, dgt, de = _loss_head(x2, e, gt, target)

    g = {}
    g["w_ple_proj"] = _matmul_tn(p_bf, de, tk=PLE_DIM, tn=1024, tt=1024, name="mm_d_ple_proj",
                                 slot_cols=D_MODEL // N_DEV)
    g["w_ple_gate"] = _matmul_tn(hp, dgt, tk=1024, tn=1024, tt=512, name="mm_d_ple_gate")
    def through_norm(dy, xv, dres, gain):
        dx, dgain = _rmsnorm_bwd_rows(dy, xv, dres, gain)
        return dx, dx, dgain

    dx2, dx2_bf, g["g_ple"] = _matmul(
        dgt, w["w_ple_gate"], mode="nt", tm=512, tn=1024, out_dtypes=[F32, BF16], name="mm_dhp",
        epilogue=through_norm, extras=(x2, dx3), row_vecs=(w["g_ple"],), n_row_sums=1)
    g["w_down"] = _matmul_tn(act, dx2_bf, tk=1024, tn=1024, tt=512, name="mm_d_down")
    du = _matmul(dx2_bf, w["w_down"], mode="nt", tm=1024, tn=1024, out_dtypes=[BF16], name="mm_dact",
                 epilogue=lambda acc, a: (acc * (2.0 * jnp.sqrt(a.astype(F32))),), extras=(act,))
    g["w_up"] = _matmul_tn(hm, du, tk=1024, tn=1024, tt=512, name="mm_d_up", slot_cols=D_FF // N_DEV)
    ready(1, g)
    dx1, dx1_bf, g["g_mlp"] = run(
        "mm_dhm", _matmul, du, w["w_up"], mode="nt", tm=512, tn=1024, out_dtypes=[F32, BF16], name="mm_dhm",
        epilogue=through_norm, extras=(x1, dx2), row_vecs=(w["g_mlp"],), n_row_sums=1)
    g["w_out"] = _matmul_tn(merged, dx1_bf, tk=1024, tn=1024, tt=512, name="mm_d_out")
    dmerged = _matmul(dx1_bf, w["w_out"], mode="nt", tm=1024, tn=1024, out_dtypes=[F32], name="mm_dmerged")
    dya, dyb, dga, dgb = _merge_bwd(z_gate, ya, yb, dmerged)
    g["w_rnn_proj"] = _matmul_tn(ya_in, dya, tk=1024, tn=1024, tt=512, name="mm_d_rnn_proj")
    g["w_attn_proj"] = _matmul_tn(o, dyb, tk=1024, tn=1024, tt=512, name="mm_d_attn_proj")
    ready(2, g)
    dya_in = run("mm_dya_in", _matmul, dya, w["w_rnn_proj"], mode="nt", tm=1024, tn=1024, out_dtypes=[F32],
                 name="mm_dya_in")
    do = _matmul(dyb, w["w_attn_proj"], mode="nt", tm=1024, tn=1024, out_dtypes=[BF16], name="mm_do")
    dx_rnn, dg_rnn, dwrg_dense, dwig_dense, rnn_vec = run(
        "rnn_bwd", _rnn_bwd, dya_in, z_rnn, xc, hr, w["conv_w"], wrg_bd, w["b_rg"], wig_bd, w["b_ig"],
        w["lru_lambda"], **dims)
    dz_attn, attn_vec = run("attn_bwd", _attn_bwd, z_attn, o, lse, do, cos_t, sin_t, q_gain_t, k_gain_t, sinks_t,
                            **dims)
    dz_parts = (dx_rnn, dg_rnn, dz_attn, dga, dgb)
    g["w_in"] = jnp.concatenate(
        _matmul_tn_multi(h, dz_parts[:2], tt=512, name="mm_d_in_rnn")
        + _matmul_tn_multi(h, dz_parts[2:], tt=512, name="mm_d_in_rest"), axis=1)
    g["w_rg"] = _block_diag_extract(dwrg_dense)
    g["w_ig"] = _block_diag_extract(dwig_dense)
    g["b_rg"], g["b_ig"], g["lru_lambda"], g["conv_b"] = (rnn_vec[i:i + 1] for i in range(4))
    g["conv_w"] = rnn_vec[4:8]
    attn_vec = attn_vec[0] if n_seq == 1 else functools.reduce(jnp.add, [attn_vec[b] for b in range(n_seq)])
    g["q_gain"] = attn_vec[0].reshape(N_Q_HEADS, HEAD_DIM).sum(axis=0)[None, :]
    g["k_gain"] = attn_vec[1, :KV_W].reshape(N_KV_HEADS, HEAD_DIM).sum(axis=0)[None, :]
    g["sinks"] = attn_vec[2:3, :N_Q_HEADS]
    ready(3, g, {LOSS_ROW: loss_row})
    windows = ((w["w_in"], (0, D_MODEL)), (w["w_in"], (D_MODEL, D_MODEL)), (w_in_attn, (0, ATTN_W)),
               (w_in_gate, (0, D_MODEL)), (w_in_gate, (D_MODEL, D_MODEL)))
    dh = run("mm_dh", _matmul, dz_parts, [wd[0] for wd in windows], mode="nt", tm=512, tn=1024, out_dtypes=[F32],
             name="mm_dh", b_cols=[wd[1] for wd in windows])
    grad_x, g["g_mix"] = run("norm_mix_bwd", _rmsnorm_bwd, x, w["g_mix"], dh, dx1, name="norm_mix_bwd",
                             want_bf16=False)
    return loss_row[0, 0], grad_x, g


MESH_ID = pl.DeviceIdType.MESH


def _coords(index):
    return (index >> 2) & 1, (index >> 1) & 1, index & 1


def _exchange(srcs, kinds, *, name):
    n = len(srcs)
    n_peer = N_DEV - 1

    def body(*refs):
        src, dst = refs[:n], refs[n:2 * n]
        send_sems, recv_sems, local_sems = refs[2 * n:]
        me = 4 * lax.axis_index("x") + 2 * lax.axis_index("y") + lax.axis_index("c")

        def remote(i, d):
            peer = (me + d) & (N_DEV - 1)
            piece = src[i] if kinds[i] == "gather" else src[i].at[peer]
            return pltpu.make_async_remote_copy(
                src_ref=piece, dst_ref=dst[i].at[me], send_sem=send_sems.at[i * n_peer + d - 1],
                recv_sem=recv_sems.at[i * n_peer + d - 1], device_id=_coords(peer), device_id_type=MESH_ID)

        def arrival(i, d):
            sender = (me - d) & (N_DEV - 1)
            piece = src[i] if kinds[i] == "gather" else src[i].at[sender]
            return pltpu.make_async_remote_copy(
                src_ref=piece, dst_ref=dst[i].at[sender], send_sem=send_sems.at[i * n_peer + d - 1],
                recv_sem=recv_sems.at[i * n_peer + d - 1], device_id=_coords(sender), device_id_type=MESH_ID)

        own = []
        for i in range(n):
            piece = src[i] if kinds[i] == "gather" else src[i].at[me]
            own.append(pltpu.make_async_copy(piece, dst[i].at[me], local_sems.at[i]))
            own[-1].start()
        sent = [remote(i, d) for d in range(1, N_DEV) for i in range(n)]
        for cp in sent:
            cp.start()
        for d in range(1, N_DEV):
            for i in range(n):
                arrival(i, d).wait_recv()
        for cp in sent:
            cp.wait_send()
        for cp in own:
            cp.wait()

    def out_of(s, kind):
        shape = s.shape if kind == "scatter" else (N_DEV,) + s.shape
        return jax.ShapeDtypeStruct(shape, s.dtype)

    any_spec = pl.BlockSpec(memory_space=pl.ANY)
    return pl.pallas_call(
        body,
        in_specs=[any_spec] * n,
        out_specs=[any_spec] * n,
        out_shape=[out_of(s, k) for s, k in zip(srcs, kinds)],
        scratch_shapes=[pltpu.SemaphoreType.DMA((n * n_peer,)), pltpu.SemaphoreType.DMA((n * n_peer,)),
                        pltpu.SemaphoreType.DMA((n,))],
        compiler_params=pltpu.CompilerParams(has_side_effects=True),
        name=name,
    )(*srcs)


def _remote(src, dst, send_sem, recv_sem, to):
    return pltpu.make_async_remote_copy(src_ref=src, dst_ref=dst, send_sem=send_sem, recv_sem=recv_sem,
                                        device_id=to, device_id_type=MESH_ID)


def _gather_two_level(shards, *, name):
    n = len(shards)
    per = N_DEV - 1

    def body(*refs):
        src, dst = refs[:n], refs[n:2 * n]
        send_sems, recv_sems, local_sems = refs[2 * n:]
        x, y, c = lax.axis_index("x"), lax.axis_index("y"), lax.axis_index("c")
        me, sibling = (x, y, c), (x, y, 1 - c)
        chips = [(1 - x, y), (x, 1 - y), (1 - x, 1 - y)]

        def slot(pos):
            return 4 * pos[0] + 2 * pos[1] + pos[2]

        def copy(i, k, block, to, from_shard=False):
            source = src[i] if from_shard else dst[i].at[slot(block)]
            return _remote(source, dst[i].at[slot(block)], send_sems.at[i * per + k], recv_sems.at[i * per + k], to)

        mine = [pltpu.make_async_copy(src[i], dst[i].at[slot(me)], local_sems.at[i]) for i in range(n)]
        for cp in mine:
            cp.start()
        first = []
        for i in range(n):
            first.append(copy(i, 0, me, sibling, from_shard=True))
            first += [copy(i, 1 + j, me, (*chip, c), from_shard=True) for j, chip in enumerate(chips)]
        for cp in first:
            cp.start()
        passed = []
        for i in range(n):
            for j, chip in enumerate(chips):
                copy(i, 1 + j, (*chip, c), me).wait_recv()
                passed.append(copy(i, 4 + j, (*chip, c), sibling))
                passed[-1].start()
        for i in range(n):
            copy(i, 0, sibling, me).wait_recv()
            for j, chip in enumerate(chips):
                copy(i, 4 + j, (*chip, 1 - c), me).wait_recv()
        for cp in first + passed:
            cp.wait_send()
        for cp in mine:
            cp.wait()

    any_spec = pl.BlockSpec(memory_space=pl.ANY)
    return pl.pallas_call(
        body,
        in_specs=[any_spec] * n,
        out_specs=[any_spec] * n,
        out_shape=[jax.ShapeDtypeStruct((N_DEV,) + s.shape, s.dtype) for s in shards],
        scratch_shapes=[pltpu.SemaphoreType.DMA((n * per,)), pltpu.SemaphoreType.DMA((n * per,)),
                        pltpu.SemaphoreType.DMA((n,))],
        name=name,
    )(*shards)


CHIPS = N_DEV // 2


def _other_chips(x, y):
    return [(x, 1 - y), (1 - x, y), (1 - x, 1 - y)]


def _hosted_gather_first(shards):
    n = len(shards)
    per = CHIPS

    def plan(src, dst, send_sems, recv_sems, local_sems, first_sem):
        x, y, c = lax.axis_index("x"), lax.axis_index("y"), lax.axis_index("c")
        peers = [(x, y, 1 - c)] + [(*chip, c) for chip in _other_chips(x, y)]
        copies = []
        for i in range(n):
            own = pltpu.make_async_copy(src[i], dst[i].at[4 * x + 2 * y + c], local_sems.at[first_sem + i])
            copies.append(_Xfer(own.start, own.wait))
        for j, peer in enumerate(peers):
            for i in range(n):
                k = first_sem + i * per + j
                out = _remote(src[i], dst[i].at[4 * x + 2 * y + c], send_sems.at[k], recv_sems.at[k], peer)
                arrival = _remote(src[i], dst[i].at[4 * peer[0] + 2 * peer[1] + peer[2]], send_sems.at[k],
                                  recv_sems.at[k], peer)

                def wait(out=out, arrival=arrival):
                    arrival.wait_recv()
                    out.wait_send()

                copies.append(_Xfer(out.start, wait))
        return copies

    out_shape = tuple(jax.ShapeDtypeStruct((N_DEV,) + s.shape, s.dtype) for s in shards)
    return _Hosted(tuple(shards), out_shape, n * per, plan)


def _hosted_gather_second(landed):
    n = len(landed)
    per = CHIPS - 1

    def plan(src, dst, send_sems, recv_sems, local_sems, first_sem):
        x, y, c = lax.axis_index("x"), lax.axis_index("y"), lax.axis_index("c")
        copies = []
        for j, chip in enumerate(_other_chips(x, y)):
            mine, theirs = 4 * chip[0] + 2 * chip[1] + c, 4 * chip[0] + 2 * chip[1] + 1 - c
            for i in range(n):
                k = first_sem + i * per + j
                out = _remote(src[i].at[mine], dst[i].at[mine], send_sems.at[k], recv_sems.at[k], (x, y, 1 - c))
                arrival = _remote(src[i].at[theirs], dst[i].at[theirs], send_sems.at[k], recv_sems.at[k],
                                  (x, y, 1 - c))

                def wait(out=out, arrival=arrival):
                    arrival.wait_recv()
                    out.wait_send()

                copies.append(_Xfer(out.start, wait))
        return copies

    out_shape = tuple(jax.ShapeDtypeStruct(a.shape, a.dtype) for a in landed)
    return _Hosted(tuple(landed), out_shape, n * per, plan, tuple((i, i) for i in range(n)))


def _hosted_sibling_swap(arrays, sliced):
    n_sems = sum(CHIPS if s else 1 for s in sliced)

    def plan(src, dst, send_sems, recv_sems, local_sems, first_sem):
        x, y, c = lax.axis_index("x"), lax.axis_index("y"), lax.axis_index("c")
        sibling = (x, y, 1 - c)
        copies, k = [], first_sem
        for i, is_sliced in enumerate(sliced):
            pieces = [(src[i].at[2 * s + 1 - c], dst[i].at[s]) for s in range(CHIPS)] if is_sliced else [(src[i], dst[i])]
            for source, target in pieces:
                cp = _remote(source, target, send_sems.at[k], recv_sems.at[k], sibling)
                copies.append(_Xfer(cp.start, cp.wait))
                k += 1
        return copies

    out_shape = tuple(jax.ShapeDtypeStruct((CHIPS,) + a.shape[1:] if s else a.shape, a.dtype)
                      for a, s in zip(arrays, sliced))
    return _Hosted(tuple(arrays), out_shape, n_sems, plan)


def _hosted_chip_exchange(arrays, sliced):
    n = len(arrays)
    per = CHIPS - 1

    def plan(src, dst, send_sems, recv_sems, local_sems, first_sem):
        x, y, c = lax.axis_index("x"), lax.axis_index("y"), lax.axis_index("c")
        chip = 2 * x + y
        copies = []
        for i in range(n):
            own = pltpu.make_async_copy(src[i].at[chip] if sliced[i] else src[i], dst[i].at[chip],
                                        local_sems.at[first_sem + i])
            copies.append(_Xfer(own.start, own.wait))
        for d in range(1, CHIPS):
            other = chip ^ d
            to = ((other >> 1) & 1, other & 1, c)
            for i in range(n):
                k = first_sem + i * per + d - 1
                source = src[i].at[other] if sliced[i] else src[i]
                out = _remote(source, dst[i].at[chip], send_sems.at[k], recv_sems.at[k], to)
                arrival = _remote(source, dst[i].at[other], send_sems.at[k], recv_sems.at[k], to)

                def wait(out=out, arrival=arrival):
                    arrival.wait_recv()
                    out.wait_send()

                copies.append(_Xfer(out.start, wait))
        return copies

    out_shape = tuple(jax.ShapeDtypeStruct(a.shape if s else (CHIPS,) + a.shape, a.dtype)
                      for a, s in zip(arrays, sliced))
    return _Hosted(tuple(arrays), out_shape, n * per, plan)


def _add_sibling(parts, received, core, *, name):
    _, r, cols = parts.shape
    tr = min(256, r)

    def body(core_ref, a_ref, b_ref, o_ref):
        o_ref[...] = (a_ref[...] + b_ref[...]).astype(BF16)

    grid_spec = pltpu.PrefetchScalarGridSpec(
        num_scalar_prefetch=1,
        grid=(CHIPS, r // tr),
        in_specs=[pl.BlockSpec((None, tr, cols), lambda k, i, core_ref: (2 * k + core_ref[0], i, 0)),
                  pl.BlockSpec((None, tr, cols), lambda k, i, core_ref: (k, i, 0))],
        out_specs=pl.BlockSpec((None, tr, cols), lambda k, i, core_ref: (k, i, 0)),
    )
    return pl.pallas_call(body, grid_spec=grid_spec, out_shape=jax.ShapeDtypeStruct((CHIPS, r, cols), BF16),
                          compiler_params=_params("parallel", "parallel"), name=name)(core, parts, received)


def _add_whole(a, b, *, name):
    def body(a_ref, b_ref, o_ref):
        o_ref[...] = a_ref[...] + b_ref[...]

    return pl.pallas_call(body, out_shape=jax.ShapeDtypeStruct(a.shape, F32), name=name)(a, b)


def _adamw(parts, w, m, v, *, name):
    r, c = w.shape
    n_parts = parts.shape[0]
    tr = min(256, r)
    c1 = 1.0 - ADAM_B1 ** ADAM_STEP
    c2 = 1.0 - ADAM_B2 ** ADAM_STEP

    def body(p_ref, w_ref, m_ref, v_ref, g_ref, d_ref, nm_ref, nv_ref):
        g = p_ref[0].astype(F32)
        for s in range(1, n_parts):
            g = g + p_ref[s].astype(F32)
        nm = ADAM_B1 * m_ref[...] + (1.0 - ADAM_B1) * g
        nv = ADAM_B2 * v_ref[...] + (1.0 - ADAM_B2) * (g * g)
        g_ref[...] = g
        nm_ref[...] = nm
        nv_ref[...] = nv
        d_ref[...] = -ADAM_LR * ((nm / c1) / (jnp.sqrt(nv / c2) + ADAM_EPS) + ADAM_WD * w_ref[...])

    tile = pl.BlockSpec((tr, c), lambda i: (i, 0))
    return pl.pallas_call(
        body,
        grid=(r // tr,),
        in_specs=[pl.BlockSpec((n_parts, tr, c), lambda i: (0, i, 0)), tile, tile, tile],
        out_specs=[tile] * 4,
        out_shape=[jax.ShapeDtypeStruct((r, c), F32)] * 4,
        compiler_params=_params("parallel"),
        name=name,
    )(parts, w, m, v)


BIG = ("w_in", "w_rnn_proj", "w_attn_proj", "w_out", "w_up", "w_down", "w_ple_gate", "w_ple_proj")
LOSS_ROW = "loss"
SMALL = (("conv_b", 1), ("b_rg", 1), ("b_ig", 1), ("lru_lambda", 1), ("g_mlp", 1), ("g_ple", 1),
         ("q_gain", 1), ("k_gain", 1), ("sinks", 1), (LOSS_ROW, 1), ("w_rg", 64), ("w_ig", 64))
SMALL_ROWS = 144
ROW_SHARDED = ("w_rnn_proj", "w_attn_proj", "w_out", "w_down", "w_ple_gate")
COL_SHARDED = ("w_in", "w_up", "w_ple_proj")
BATCHES = {1: ("w_ple_proj", "w_ple_gate", "w_down", "w_up"), 2: ("w_out", "w_rnn_proj", "w_attn_proj"),
           3: ("w_in", "conv_w")}
SMALL_BATCH = 4


def _pack_small(vals):
    rows = []
    for nm, nrow in SMALL:
        flat = vals[nm].reshape(-1).astype(F32)
        rows.append(jnp.pad(flat, (0, nrow * D_MODEL - flat.shape[0])).reshape(nrow, D_MODEL))
    used = sum(nrow for _, nrow in SMALL)
    rows.append(jnp.zeros((SMALL_ROWS - used, D_MODEL), F32))
    return jnp.concatenate(rows, axis=0)


def _unpack_small(packed, shapes):
    out, at = {}, 0
    for nm, nrow in SMALL:
        size = 1
        for s in shapes[nm]:
            size *= s
        out[nm] = packed[at:at + nrow].reshape(-1)[:size].reshape(shapes[nm])
        at += nrow
    return out


def _full_weight(name, landed):
    if name in COL_SHARDED:
        return landed.transpose(1, 0, 2).reshape(landed.shape[1], N_DEV * landed.shape[2])
    return landed.reshape(N_DEV * landed.shape[1], landed.shape[2])


def _owner_slots(name, grad):
    if name == "w_in":
        return grad.reshape(D_MODEL, N_DEV, IN_TOTAL // N_DEV).transpose(1, 0, 2)
    if name == "conv_w":
        return grad.reshape(CONV_W, N_DEV, D_MODEL // N_DEV).transpose(1, 0, 2)
    if name in COL_SHARDED:
        return grad
    return grad.reshape(N_DEV, grad.shape[0] // N_DEV, grad.shape[1])


class _StepExchanges:
    FIRST, SECOND = "first", "second"
    EARLY, MID, LATE = ("w_rnn_proj", "w_attn_proj", "w_out"), ("w_up",), ("w_down", "w_ple_gate", "w_ple_proj")
    GATHERS = {"rnn_fwd": ((FIRST, EARLY), (FIRST, MID)),
               "attn_fwd": ((SECOND, EARLY), (SECOND, MID), (FIRST, LATE)), "mm_rnn_proj": ((SECOND, LATE),)}
    SWAPS = {"mm_dhm": 1, "mm_dya_in": 2}
    CHIP_EXCHANGES = {"rnn_bwd": 1, "attn_bwd": 2, "mm_dh": 3, "norm_mix_bwd": SMALL_BATCH}

    def __init__(self, shards, core):
        self.shards = shards
        self.core = core
        self.parts, self.swapped, self.summed, self.half_gathered = {}, {}, {}, {}

    def ready(self, batch, grads, extra=None):
        arrays = [_owner_slots(nm, grads[nm]) for nm in BATCHES[batch]]
        self.parts[batch] = (arrays, [True] * len(arrays))
        if batch not in self.SWAPS.values():
            self.parts[SMALL_BATCH] = ([_pack_small({**grads, **extra})], [False])
            both = [a + b for a, b in zip(self.parts[batch], self.parts[SMALL_BATCH])]
            _, swapped = _call(
                lambda: None, grid=(1,), in_specs=[], out_specs=[], out_shape=[], args=(), name="swap_last",
                semantics=("arbitrary",), hosted=_hosted_sibling_swap(*both))
            self.swapped[batch], self.swapped[SMALL_BATCH] = swapped[:-1], swapped[-1:]

    def host(self, tag):
        if tag in self.GATHERS:
            return _merge_hosted([
                _hosted_gather_first([self.shards[nm] for nm in group]) if half == self.FIRST
                else _hosted_gather_second([self.half_gathered[nm] for nm in group])
                for half, group in self.GATHERS[tag]])
        if tag in self.SWAPS:
            return _hosted_sibling_swap(*self.parts[self.SWAPS[tag]])
        if tag in self.CHIP_EXCHANGES:
            batch = self.CHIP_EXCHANGES[tag]
            arrays, sliced = self.parts[batch]
            labels = BATCHES.get(batch, ("small",))
            sums = [_add_sibling(a, r, self.core, name="add_" + lb) if s else _add_whole(a, r, name="add_" + lb)
                    for a, r, s, lb in zip(arrays, self.swapped[batch], sliced, labels)]
            return _hosted_chip_exchange(sums, sliced)
        return None

    def landed(self, tag, landed, weights):
        if tag in self.GATHERS:
            names = [(half, nm) for half, group in self.GATHERS[tag] for nm in group]
            for (half, nm), buf in zip(names, landed):
                if half == self.FIRST:
                    self.half_gathered[nm] = buf
                else:
                    weights[nm] = _full_weight(nm, buf)
        elif tag in self.SWAPS:
            self.swapped[self.SWAPS[tag]] = landed
        else:
            self.summed[self.CHIP_EXCHANGES[tag]] = landed


def kernel(x, p, g_mix, w_in, conv_w, conv_b, w_rg, b_rg, w_ig, b_ig, lru_lambda, w_rnn_proj, q_gain, k_gain, sinks, w_attn_proj, w_out, g_mlp, w_up, w_down, g_ple, w_ple_gate, w_ple_proj, loss_target, m_g_mix, m_w_in, m_conv_w, m_conv_b, m_w_rg, m_b_rg, m_w_ig, m_b_ig, m_lru_lambda, m_w_rnn_proj, m_q_gain, m_k_gain, m_sinks, m_w_attn_proj, m_w_out, m_g_mlp, m_w_up, m_w_down, m_g_ple, m_w_ple_gate, m_w_ple_proj, v_g_mix, v_w_in, v_conv_w, v_conv_b, v_w_rg, v_b_rg, v_w_ig, v_b_ig, v_lru_lambda, v_w_rnn_proj, v_q_gain, v_k_gain, v_sinks, v_w_attn_proj, v_w_out, v_g_mlp, v_w_up, v_w_down, v_g_ple, v_w_ple_gate, v_w_ple_proj):
    names = ("g_mix", "w_in", "conv_w", "conv_b", "w_rg", "b_rg", "w_ig", "b_ig", "lru_lambda", "w_rnn_proj",
             "q_gain", "k_gain", "sinks", "w_attn_proj", "w_out", "g_mlp", "w_up", "w_down", "g_ple",
             "w_ple_gate", "w_ple_proj")
    wts = dict(zip(names, (g_mix, w_in, conv_w, conv_b, w_rg, b_rg, w_ig, b_ig, lru_lambda, w_rnn_proj, q_gain,
                           k_gain, sinks, w_attn_proj, w_out, g_mlp, w_up, w_down, g_ple, w_ple_gate, w_ple_proj)))
    mom1 = dict(zip(names, (m_g_mix, m_w_in, m_conv_w, m_conv_b, m_w_rg, m_b_rg, m_w_ig, m_b_ig, m_lru_lambda,
                            m_w_rnn_proj, m_q_gain, m_k_gain, m_sinks, m_w_attn_proj, m_w_out, m_g_mlp, m_w_up,
                            m_w_down, m_g_ple, m_w_ple_gate, m_w_ple_proj)))
    mom2 = dict(zip(names, (v_g_mix, v_w_in, v_conv_w, v_conv_b, v_w_rg, v_b_rg, v_w_ig, v_b_ig, v_lru_lambda,
                            v_w_rnn_proj, v_q_gain, v_k_gain, v_sinks, v_w_attn_proj, v_w_out, v_g_mlp, v_w_up,
                            v_w_down, v_g_ple, v_w_ple_gate, v_w_ple_proj)))
    n_seq, seq, _ = x.shape
    core = lax.axis_index("c").astype(jnp.int32).reshape(1)

    shards = {nm: wts[nm][0].astype(BF16) for nm in BIG}
    w_in_all, conv_all = _gather_two_level([shards["w_in"], conv_w[0]], name="gather_w_in")
    w = {nm: wts[nm] for nm in names if nm not in BIG}
    w["w_rg"], w["w_ig"] = w_rg[0], w_ig[0]
    w["conv_w"] = conv_all.transpose(1, 0, 2).reshape(CONV_W, D_MODEL)
    w["w_in"] = _full_weight("w_in", w_in_all)
    comm = _StepExchanges(shards, core)
    loss_sum, grad_x, g = _local_step(
        x.reshape(n_seq * seq, D_MODEL), p.reshape(n_seq * seq, PLE_DIM), loss_target.reshape(n_seq * seq, D_MODEL),
        w, n_seq=n_seq, seq=seq, comm=comm)
    del loss_sum

    res = {}
    for batch, batch_names in BATCHES.items():
        for nm, summed in zip(batch_names, comm.summed[batch]):
            res[nm] = _adamw(summed, wts[nm][0], mom1[nm][0], mom2[nm][0], name="adamw_" + nm)
    g_mix_parts, = _exchange([g["g_mix"]], ["gather"], name="gather_g_mix")
    res["g_mix"] = [r[0] for r in _adamw(g_mix_parts, g_mix, m_g_mix, v_g_mix, name="adamw_g_mix")]
    small_names = [nm for nm, _ in SMALL if nm != LOSS_ROW]
    full_small = {}
    for src, key in ((wts, "w"), (mom1, "m"), (mom2, "v")):
        vals = {nm: src[nm][0] for nm in small_names}
        vals[LOSS_ROW] = jnp.zeros((1,), F32)
        full_small[key] = _pack_small(vals)
    small_res = _adamw(comm.summed[SMALL_BATCH][0],full_small["w"], full_small["m"], full_small["v"], name="adamw_small")
    shapes = {nm: wts[nm].shape[1:] for nm in small_names}
    shapes[LOSS_ROW] = (1,)
    small_out = [_unpack_small(r, shapes) for r in small_res]
    for nm in small_names:
        res[nm] = [so[nm] for so in small_out]
    loss = small_out[0][LOSS_ROW][0] * (0.5 / D_MODEL)

    outs = [loss, grad_x.reshape(n_seq, seq, D_MODEL)]
    for k in range(4):
        outs.extend(res[nm][k][None] for nm in names)
    return tuple(outs)
```

```python
import functools
from typing import Callable, NamedTuple

import jax
import jax.numpy as jnp
from jax import lax
from jax.experimental import pallas as pl
from jax.experimental.pallas import tpu as pltpu

F32 = jnp.float32
BF16 = jnp.bfloat16

N_DEV = 8
D_MODEL = 1024
RNN_BLOCK_W = 64
CONV_W = 4
LRU_C = 8.0
HEAD_DIM = 64
N_Q_HEADS = 16
N_KV_HEADS = 4
KV_W = N_KV_HEADS * HEAD_DIM
WINDOW = 128
ROPE_THETA = 10000.0
D_FF = 4096
PLE_DIM = 256
NORM_EPS = 1e-6
IN_TOTAL = 5632
COL_RNN_END, COL_ATTN_END = 2048, 3584
ATTN_W = COL_ATTN_END - COL_RNN_END
ATTN_K_AT, ATTN_V_AT = 1024, 1280

ADAM_LR = 0.001
ADAM_B1 = 0.9
ADAM_B2 = 0.999
ADAM_EPS = 1e-08
ADAM_WD = 0.01
ADAM_STEP = 10

LANES = 128
SUBLANES = 8
RNN_TILE = 256
VMEM_LIMIT = 48 * 1024 * 1024
NEG_BIG = -1e30


def _params(*sem):
    return pltpu.CompilerParams(dimension_semantics=sem if sem else None, vmem_limit_bytes=VMEM_LIMIT)


def _sig(x):
    return 0.5 * jnp.tanh(0.5 * x) + 0.5


def _dot_nt(a, b):
    return lax.dot_general(a, b, (((1,), (1,)), ((), ())), preferred_element_type=F32)


def _dot_tn(a, b):
    return lax.dot_general(a, b, (((0,), (0,)), ((), ())), preferred_element_type=F32)


class _Xfer:
    def __init__(self, start, wait):
        self.start, self.wait = start, wait


class _Hosted(NamedTuple):
    srcs: tuple
    out_shape: tuple
    n_sems: int
    plan: Callable
    aliases: tuple = ()


def _merge_hosted(parts):
    parts = [p for p in parts if p is not None]
    if len(parts) <= 1:
        return parts[0] if parts else None
    src_at, dst_at, sem_at, aliases = [0], [0], [0], []
    for p in parts:
        aliases += [(i + src_at[-1], j + dst_at[-1]) for i, j in p.aliases]
        src_at.append(src_at[-1] + len(p.srcs))
        dst_at.append(dst_at[-1] + len(p.out_shape))
        sem_at.append(sem_at[-1] + p.n_sems)

    def plan(src, dst, send_sems, recv_sems, local_sems, first_sem):
        copies = []
        for k, p in enumerate(parts):
            copies += p.plan(src[src_at[k]:src_at[k + 1]], dst[dst_at[k]:dst_at[k + 1]], send_sems, recv_sems,
                             local_sems, first_sem + sem_at[k])
        return copies

    return _Hosted(tuple(a for p in parts for a in p.srcs), tuple(s for p in parts for s in p.out_shape),
                   sem_at[-1], plan, tuple(aliases))


def _call(body, *, grid, in_specs, out_specs, out_shape, args, name, semantics, scratch_shapes=(), hosted=None):
    if hosted is None:
        outs = pl.pallas_call(body, grid=grid, in_specs=list(in_specs), out_specs=list(out_specs),
                              out_shape=list(out_shape), scratch_shapes=list(scratch_shapes),
                              compiler_params=_params(*semantics), name=name)(*args)
        return list(outs), []
    counts = (len(in_specs), len(hosted.srcs), len(out_specs), len(hosted.out_shape), len(scratch_shapes), 3)

    def wrapped(*refs):
        at, groups = 0, []
        for count in counts:
            groups.append(refs[at:at + count])
            at += count
        ins, srcs, outs, dsts, scratch, sems = groups
        copies = hosted.plan(srcs, dsts, *sems, 0)
        ids = [pl.program_id(axis) for axis in range(len(grid))]
        first = functools.reduce(jnp.logical_and, [i == 0 for i in ids])
        last = functools.reduce(jnp.logical_and, [i == g - 1 for i, g in zip(ids, grid)])

        @pl.when(first)
        def _():
            for cp in copies:
                cp.start()

        body(*ins, *outs, *scratch)

        @pl.when(last)
        def _():
            for cp in copies:
                cp.wait()

    any_spec = pl.BlockSpec(memory_space=pl.ANY)
    sems = [pltpu.SemaphoreType.DMA((hosted.n_sems,))] * 3
    outs = pl.pallas_call(
        wrapped, grid=grid, in_specs=list(in_specs) + [any_spec] * counts[1],
        out_specs=list(out_specs) + [any_spec] * counts[3], out_shape=list(out_shape) + list(hosted.out_shape),
        scratch_shapes=list(scratch_shapes) + sems, compiler_params=_params(*["arbitrary"] * len(grid)),
        input_output_aliases={counts[0] + i: counts[2] + j for i, j in hosted.aliases},
        name=name)(*args, *hosted.srcs)
    return list(outs[:counts[2]]), list(outs[counts[2]:])


def _dividing_tile(n, want):
    tile = min(want, n)
    while n % tile:
        tile -= LANES
    return tile


def _matmul(a, b, *, mode, tm, tn, out_dtypes, name, epilogue=None, extras=(), hosted=None, b_cols=None,
            row_vecs=(), n_row_sums=0, extra_col_blocks=None):
    a_parts = tuple(a) if isinstance(a, (tuple, list)) else (a,)
    b_parts = tuple(b) if isinstance(b, (tuple, list)) else (b,)
    assert len(a_parts) == len(b_parts) and (mode == "nt" or len(a_parts) == 1)
    n_parts = len(a_parts)
    m = a_parts[0].shape[0]
    if b_cols is None:
        b_cols = [(0, bp.shape[1]) for bp in b_parts]
    n = b_cols[0][1] if mode == "nn" else b_parts[0].shape[0]
    tm, tn = min(tm, m), _dividing_tile(n, tn)
    n_extra = len(extras) + len(row_vecs)
    n_tiles_out = len(out_dtypes)
    assert n_row_sums == 0 or n == tn

    def body(*refs):
        a_refs, b_refs = refs[:n_parts], refs[n_parts:2 * n_parts]
        rest = refs[2 * n_parts:]
        extra_refs, out_refs = rest[:n_extra], rest[n_extra:]
        if mode == "nn":
            acc = jnp.dot(a_refs[0][...], b_refs[0][...], preferred_element_type=F32)
        else:
            acc = _dot_nt(a_refs[0][...], b_refs[0][...])
            for a_ref, b_ref in zip(a_refs[1:], b_refs[1:]):
                acc = acc + _dot_nt(a_ref[...], b_ref[...])
        res = epilogue(acc, *[e[...] for e in extra_refs]) if epilogue is not None else (acc,)
        for o_ref, r in zip(out_refs[:n_tiles_out], res):
            o_ref[...] = r.astype(o_ref.dtype)
        if n_row_sums:
            @pl.when(pl.program_id(0) == 0)
            def _():
                for o_ref in out_refs[n_tiles_out:]:
                    o_ref[...] = jnp.zeros_like(o_ref)

            for o_ref, r in zip(out_refs[n_tiles_out:], res[n_tiles_out:]):
                o_ref[...] += r

    a_specs = [pl.BlockSpec((tm, ap.shape[1]), lambda i, j: (i, 0)) for ap in a_parts]
    if mode == "nn":
        assert b_cols[0][0] % tn == 0
        first = b_cols[0][0] // tn
        b_specs = [pl.BlockSpec((b_parts[0].shape[0], tn), lambda i, j: (0, first + j))]
    else:
        assert all(at % width == 0 for at, width in b_cols)
        b_specs = [pl.BlockSpec((tn, width), functools.partial(lambda i, j, blk: (j, blk), blk=at // width))
                   for at, width in b_cols]
    tile = pl.BlockSpec((tm, tn), lambda i, j: (i, j))
    row = pl.BlockSpec((1, tn), lambda i, j: (0, j))
    extra_specs = [pl.BlockSpec((tm, tn), functools.partial(lambda i, j, first: (i, first + j), first=first))
                   for first in (extra_col_blocks or [0] * len(extras))]
    outs, landed = _call(
        body,
        grid=(m // tm, n // tn),
        in_specs=a_specs + b_specs + extra_specs + [row] * len(row_vecs),
        out_specs=[tile] * n_tiles_out + [row] * n_row_sums,
        out_shape=[jax.ShapeDtypeStruct((m, n), dt) for dt in out_dtypes]
        + [jax.ShapeDtypeStruct((1, n), F32)] * n_row_sums,
        args=(*a_parts, *b_parts, *extras, *row_vecs), name=name,
        semantics=("arbitrary" if n_row_sums else "parallel", "arbitrary"), hosted=hosted)
    if hosted is not None:
        return (*outs, landed)
    return outs[0] if len(outs) == 1 else outs


def _matmul_tn(a, b, *, tk, tn, tt, name, slot_cols=None):
    t, k = a.shape
    n = b.shape[1]
    tk, tn, tt = min(tk, k), _dividing_tile(n, tn), min(tt, t)

    def body(a_ref, b_ref, o_ref):
        @pl.when(pl.program_id(2) == 0)
        def _():
            o_ref[...] = jnp.zeros_like(o_ref)

        if slot_cols is None:
            o_ref[...] += _dot_tn(a_ref[...], b_ref[...])
        else:
            av = a_ref[...]
            for s in range(tn // slot_cols):
                o_ref[s] += _dot_tn(av, b_ref[:, s * slot_cols:(s + 1) * slot_cols])

    if slot_cols is not None:
        out_spec = pl.BlockSpec((tn // slot_cols, tk, slot_cols), lambda i, j, s: (j, i, 0))
        out_shape = jax.ShapeDtypeStruct((n // slot_cols, k, slot_cols), F32)
    else:
        out_spec = pl.BlockSpec((tk, tn), lambda i, j, s: (i, j))
        out_shape = jax.ShapeDtypeStruct((k, n), F32)
    return pl.pallas_call(
        body,
        grid=(k // tk, n // tn, t // tt),
        in_specs=[pl.BlockSpec((tt, tk), lambda i, j, s: (s, i)), pl.BlockSpec((tt, tn), lambda i, j, s: (s, j))],
        out_specs=out_spec,
        out_shape=out_shape,
        compiler_params=_params("parallel", "parallel", "arbitrary"),
        name=name,
    )(a, b)


def _matmul_tn_multi(a, bs, *, tt, name):
    t, k = a.shape
    tt = min(tt, t)
    n_b = len(bs)

    def body(a_ref, *refs):
        b_refs, o_refs = refs[:n_b], refs[n_b:]

        @pl.when(pl.program_id(0) == 0)
        def _():
            for o_ref in o_refs:
                o_ref[...] = jnp.zeros_like(o_ref)

        a_t = a_ref[...].T
        for b_ref, o_ref in zip(b_refs, o_refs):
            o_ref[...] += jnp.dot(a_t, b_ref[...], preferred_element_type=F32)

    return pl.pallas_call(
        body,
        grid=(t // tt,),
        in_specs=[pl.BlockSpec((tt, k), lambda s: (s, 0))] + [pl.BlockSpec((tt, b.shape[1]), lambda s: (s, 0)) for b in bs],
        out_specs=[pl.BlockSpec((k, b.shape[1]), lambda s: (0, 0)) for b in bs],
        out_shape=[jax.ShapeDtypeStruct((k, b.shape[1]), F32) for b in bs],
        compiler_params=_params("arbitrary"),
        name=name,
    )(a, *bs)


def _rmsnorm_rows(x, g):
    return x * lax.rsqrt(jnp.mean(x * x, axis=-1, keepdims=True) + NORM_EPS) * g


def _rmsnorm_fwd(x, g, *, name):
    t, d = x.shape
    tm = min(512, t)

    def body(x_ref, g_ref, o_ref):
        o_ref[...] = _rmsnorm_rows(x_ref[...], g_ref[...]).astype(BF16)

    return pl.pallas_call(
        body,
        grid=(t // tm,),
        in_specs=[pl.BlockSpec((tm, d), lambda i: (i, 0)), pl.BlockSpec((1, d), lambda i: (0, 0))],
        out_specs=pl.BlockSpec((tm, d), lambda i: (i, 0)),
        out_shape=jax.ShapeDtypeStruct((t, d), BF16),
        compiler_params=_params("parallel"),
        name=name,
    )(x, g)


def _rmsnorm_bwd_rows(dy, x, dres, g):
    r = lax.rsqrt(jnp.mean(x * x, axis=-1, keepdims=True) + NORM_EPS)
    xr = x * r
    gy = dy * g
    dx = dres + r * (gy - xr * jnp.mean(gy * xr, axis=-1, keepdims=True))
    return dx, jnp.sum(dy * xr, axis=0, keepdims=True)


def _softplus_neg(lam):
    z = -lam
    return jnp.maximum(z, 0.0) + jnp.log1p(jnp.exp(-jnp.abs(z)))


def _neg_expm1(y, exp_half_y):
    series = -y * (1.0 + y * 0.5 * (1.0 + y * (1.0 / 3.0) * (1.0 + y * 0.25 * (1.0 + y * 0.2))))
    return jnp.where(y > -0.0625, series, 1.0 - exp_half_y * exp_half_y)


def _gelu_parts(x):
    c = 0.7978845608028654
    u = c * (x + 0.044715 * x * x * x)
    th = jnp.tanh(u)
    gel = 0.5 * x * (1.0 + th)
    dgel = 0.5 * (1.0 + th) + 0.5 * x * (1.0 - th * th) * c * (1.0 + 3.0 * 0.044715 * x * x)
    return gel, dgel


def _shift_down(v, k, rows):
    return jnp.where(rows < k, 0.0, pltpu.roll(v, k, 0))


def _shift_up(v, k, rows, n):
    return jnp.where(rows >= n - k, 0.0, pltpu.roll(v, n - k, 0))


def _scan_within_groups(a, b, rows, *, reverse):
    n = a.shape[0]
    in_group = rows & (SUBLANES - 1)
    for s in (1, 2, 4):
        if reverse:
            inside, shift = in_group < SUBLANES - s, n - s
        else:
            inside, shift = in_group >= s, s
        b = b + a * jnp.where(inside, pltpu.roll(b, shift, 0), 0.0)
        a = a * jnp.where(inside, pltpu.roll(a, shift, 0), 1.0)
    return a, b


def _rnn_gates(xc, wrg, brg, wig, big, lam):
    xcb = xc.astype(BF16)
    r = _sig(jnp.dot(xcb, wrg, preferred_element_type=F32) + brg)
    i = _sig(jnp.dot(xcb, wig, preferred_element_type=F32) + big)
    sp = _softplus_neg(lam)
    log_a = -LRU_C * r * sp
    a = jnp.exp(log_a)
    mult = jnp.sqrt(_neg_expm1(2.0 * log_a, a))
    return xcb, r, i, sp, a, mult


def _conv_fwd(xv, cw, cb, rows):
    return (cb + _shift_down(xv, 3, rows) * cw[0:1, :] + _shift_down(xv, 2, rows) * cw[1:2, :]
            + _shift_down(xv, 1, rows) * cw[2:3, :] + xv * cw[3:4, :])


def _rnn_fwd(z, conv_w, conv_b, wrg_bd, b_rg, wig_bd, b_ig, lam, *, n_seq, seq, hosted=None):
    t = n_seq * seq
    ct = RNN_TILE
    n_ct = D_MODEL // ct

    def body(x_ref, g_ref, cw_ref, cb_ref, wrg_ref, brg_ref, wig_ref, big_ref, lam_ref,
             xc_ref, hr_ref, ya_ref, a_s, b_s):
        rows = lax.broadcasted_iota(jnp.int32, (seq, ct), 0)
        xc = _conv_fwd(x_ref[...], cw_ref[...], cb_ref[...], rows)
        _, r, i, sp, a, mult = _rnn_gates(xc, wrg_ref[...], brg_ref[...], wig_ref[...], big_ref[...], lam_ref[...])
        a_s[...], b_s[...] = _scan_within_groups(a, mult * (i * xc), rows, reverse=False)

        def step(j, carry):
            r0 = pl.multiple_of(j * SUBLANES, SUBLANES)
            h = b_s[pl.ds(r0, SUBLANES), :] + a_s[pl.ds(r0, SUBLANES), :] * carry
            hr_ref[pl.ds(r0, SUBLANES), :] = h
            return h[SUBLANES - 1:SUBLANES, :]

        lax.fori_loop(0, seq // SUBLANES, step, jnp.zeros((1, ct), F32), unroll=4)
        gel, _ = _gelu_parts(g_ref[...])
        xc_ref[...] = xc
        ya_ref[...] = (hr_ref[...] * gel).astype(BF16)

    vec = pl.BlockSpec((1, ct), lambda b, c: (0, c))
    gate_w = pl.BlockSpec((None, ct, ct), lambda b, c: (c, 0, 0))
    tile = pl.BlockSpec((seq, ct), lambda b, c: (b, c))
    outs, landed = _call(
        body,
        grid=(n_seq, n_ct),
        in_specs=[
            pl.BlockSpec((seq, ct), lambda b, c: (b, c)),
            pl.BlockSpec((seq, ct), lambda b, c: (b, n_ct + c)),
            pl.BlockSpec((CONV_W, ct), lambda b, c: (0, c)), vec, gate_w, vec, gate_w, vec, vec,
        ],
        out_specs=[tile, tile, tile],
        out_shape=[jax.ShapeDtypeStruct((t, D_MODEL), F32), jax.ShapeDtypeStruct((t, D_MODEL), F32),
                   jax.ShapeDtypeStruct((t, D_MODEL), BF16)],
        scratch_shapes=[pltpu.VMEM((seq, ct), F32), pltpu.VMEM((seq, ct), F32)],
        args=(z, z, conv_w, conv_b, wrg_bd, b_rg, wig_bd, b_ig, lam), name="rnn_fwd",
        semantics=("parallel", "parallel"), hosted=hosted)
    return (*outs, landed) if hosted is not None else outs


def _rnn_bwd(dya, z, xc, hr, conv_w, wrg_bd, b_rg, wig_bd, b_ig, lam, *, n_seq, seq, hosted=None):
    t = n_seq * seq
    ct = RNN_TILE
    n_ct = D_MODEL // ct

    def body(dya_ref, x_ref, g_ref, xc_ref, hr_ref, cw_ref, wrg_ref, brg_ref, wig_ref, big_ref, lam_ref,
             dx_ref, dg_ref, dwrg_ref, dwig_ref, vec_ref, a_s, d_s, g_s):
        rows = lax.broadcasted_iota(jnp.int32, (seq, ct), 0)
        xv, xc, hr, dyv = x_ref[...], xc_ref[...], hr_ref[...], dya_ref[...]
        lamv = lam_ref[...]
        gel, dgel = _gelu_parts(g_ref[...])
        dg_ref[...] = (dyv * hr * dgel).astype(BF16)
        xcb, r, i, sp, a, mult = _rnn_gates(xc, wrg_ref[...], brg_ref[...], wig_ref[...], big_ref[...], lamv)
        a_s[...], d_s[...] = _scan_within_groups(_shift_up(a, 1, rows, seq), dyv * gel, rows, reverse=True)

        def step(k, carry):
            r0 = pl.multiple_of((seq // SUBLANES - 1 - k) * SUBLANES, SUBLANES)
            gs = d_s[pl.ds(r0, SUBLANES), :] + a_s[pl.ds(r0, SUBLANES), :] * carry
            g_s[pl.ds(r0, SUBLANES), :] = gs
            return gs[0:1, :]

        lax.fori_loop(0, seq // SUBLANES, step, jnp.zeros((1, ct), F32), unroll=4)
        gsum = g_s[...]
        gated = i * xc
        d_log_a = gsum * _shift_down(hr, 1, rows) * a - gsum * gated * (a * a / mult)
        d_gated = gsum * mult
        d_pre_r = (d_log_a * (-LRU_C) * sp) * r * (1.0 - r)
        d_pre_i = (d_gated * xc) * i * (1.0 - i)
        dprb, dpib = d_pre_r.astype(BF16), d_pre_i.astype(BF16)
        dxc = d_gated * i + _dot_nt(dprb, wrg_ref[...]) + _dot_nt(dpib, wig_ref[...])
        cw = cw_ref[...]
        dx = (dxc * cw[3:4, :] + _shift_up(dxc, 1, rows, seq) * cw[2:3, :]
              + _shift_up(dxc, 2, rows, seq) * cw[1:2, :] + _shift_up(dxc, 3, rows, seq) * cw[0:1, :])
        dx_ref[...] = dx.astype(BF16)

        @pl.when(pl.program_id(1) == 0)
        def _():
            dwrg_ref[...] = jnp.zeros_like(dwrg_ref)
            dwig_ref[...] = jnp.zeros_like(dwig_ref)
            vec_ref[...] = jnp.zeros_like(vec_ref)

        dwrg_ref[...] += _dot_tn(xcb, dprb)
        dwig_ref[...] += _dot_tn(xcb, dpib)

        def colsum(v):
            return jnp.sum(v, axis=0, keepdims=True)

        d_sp = colsum(d_log_a * (-LRU_C) * r)
        vec_ref[0:1, :] += colsum(d_pre_r)
        vec_ref[1:2, :] += colsum(d_pre_i)
        vec_ref[2:3, :] += d_sp * (-_sig(-lamv))
        vec_ref[3:4, :] += colsum(dxc)
        vec_ref[4:5, :] += colsum(dxc * _shift_down(xv, 3, rows))
        vec_ref[5:6, :] += colsum(dxc * _shift_down(xv, 2, rows))
        vec_ref[6:7, :] += colsum(dxc * _shift_down(xv, 1, rows))
        vec_ref[7:8, :] += colsum(dxc * xv)

    vec = pl.BlockSpec((1, ct), lambda c, b: (0, c))
    gate_w = pl.BlockSpec((None, ct, ct), lambda c, b: (c, 0, 0))
    tile = pl.BlockSpec((seq, ct), lambda c, b: (b, c))
    outs, landed = _call(
        body,
        grid=(n_ct, n_seq),
        in_specs=[
            tile,
            pl.BlockSpec((seq, ct), lambda c, b: (b, c)),
            pl.BlockSpec((seq, ct), lambda c, b: (b, n_ct + c)),
            tile, tile,
            pl.BlockSpec((CONV_W, ct), lambda c, b: (0, c)), gate_w, vec, gate_w, vec, vec,
        ],
        out_specs=[tile, tile, gate_w, gate_w, pl.BlockSpec((8, ct), lambda c, b: (0, c))],
        out_shape=[jax.ShapeDtypeStruct((t, D_MODEL), BF16), jax.ShapeDtypeStruct((t, D_MODEL), BF16),
                   jax.ShapeDtypeStruct((n_ct, ct, ct), F32), jax.ShapeDtypeStruct((n_ct, ct, ct), F32),
                   jax.ShapeDtypeStruct((8, D_MODEL), F32)],
        scratch_shapes=[pltpu.VMEM((seq, ct), F32)] * 3,
        args=(dya, z, z, xc, hr, conv_w, wrg_bd, b_rg, wig_bd, b_ig, lam), name="rnn_bwd",
        semantics=("parallel", "arbitrary"), hosted=hosted)
    return (*outs, landed) if hosted is not None else outs


def _split_hi_lo(x):
    hi = x.astype(BF16)
    return hi, (x - hi.astype(F32)).astype(BF16)


def _dot_split(x, m_twice):
    hi, lo = _split_hi_lo(x)
    return jnp.dot(jnp.concatenate([hi, lo], axis=1), m_twice, preferred_element_type=F32)


def _head_matrices(width):
    ec = ((lax.broadcasted_iota(jnp.int32, (2 * width, LANES), 0) & (width - 1)) // HEAD_DIM
          == lax.broadcasted_iota(jnp.int32, (2 * width, LANES), 1))
    ee = (lax.broadcasted_iota(jnp.int32, (2 * LANES, width), 1) // HEAD_DIM
          == (lax.broadcasted_iota(jnp.int32, (2 * LANES, width), 0) & (LANES - 1)))
    return jnp.where(ec, 1.0, 0.0).astype(BF16), jnp.where(ee, 1.0, 0.0).astype(BF16)


def _swap_halves(y):
    w = y.shape[1]
    first = (lax.broadcasted_iota(jnp.int32, y.shape, 1) % HEAD_DIM) < HEAD_DIM // 2
    return jnp.where(first, pltpu.roll(y, w - HEAD_DIM // 2, 1), pltpu.roll(y, HEAD_DIM // 2, 1))


def _normrope_fwd(x, gain, cos_t, sin_t, ec, ee):
    w = x.shape[1]
    rs = _dot_split(lax.rsqrt(_dot_split(x * x, ec) * (1.0 / HEAD_DIM) + NORM_EPS), ee)
    nx = x * rs
    y = nx * gain
    reps = w // LANES
    out = y * jnp.tile(cos_t, (1, reps)) + _swap_halves(y) * jnp.tile(sin_t, (1, reps))
    return out, nx, rs


def _normrope_bwd(dout, nx, rs, gain, cos_t, sin_t, ec, ee):
    w = dout.shape[1]
    reps = w // LANES
    dy = dout * jnp.tile(cos_t, (1, reps)) + _swap_halves(dout * jnp.tile(sin_t, (1, reps)))
    dgain = jnp.sum(dy * nx, axis=0, keepdims=True)
    dn = dy * gain
    seg = _dot_split(_dot_split(dn * nx, ec) * (1.0 / HEAD_DIM), ee)
    return rs * (dn - nx * seg), dgain


def _pair_operand(t, group):
    chunk = t[:, (group // 2) * LANES:(group // 2 + 1) * LANES]
    low = lax.broadcasted_iota(jnp.int32, chunk.shape, 1) < HEAD_DIM
    rolled = pltpu.roll(chunk, HEAD_DIM, 1)
    return jnp.where(low, chunk, rolled) if group % 2 == 0 else jnp.where(low, rolled, chunk)


GROUP = N_Q_HEADS // N_KV_HEADS
GROUP_W = GROUP * HEAD_DIM


def _replicate_head(t, group):
    return jnp.tile(_pair_operand(t, group), (1, 2))


def _head_blocks(t):
    seg = lax.broadcasted_iota(jnp.int32, t.shape, 1) // HEAD_DIM
    return jnp.concatenate([jnp.where(seg == h, t, 0.0) for h in range(GROUP)], axis=0)


def _stack_heads(t_t, rows):
    return jnp.concatenate([t_t[:, h * rows:(h + 1) * rows] for h in range(GROUP)], axis=0)


def _head_rows(mat_t, group):
    return jnp.concatenate([mat_t[GROUP * group + h:GROUP * group + h + 1, :] for h in range(GROUP)], axis=1)


def _window_masks(blk):
    key = lax.broadcasted_iota(jnp.int32, (blk, GROUP * blk), 0)
    query = lax.broadcasted_iota(jnp.int32, (blk, GROUP * blk), 1) & (blk - 1)
    return key > query, key <= query


def _mask_window(t, before_ok, own_ok, fill):
    blk = t.shape[0] // 2
    return jnp.concatenate([jnp.where(before_ok, t[:blk], fill), jnp.where(own_ok, t[blk:], fill)], axis=0)


def _attn_fwd(z, cos_t, sin_t, q_gain_t, k_gain_t, sinks_t, *, n_seq, seq, hosted=None):
    t = n_seq * seq
    blk = WINDOW
    nb = seq // blk

    def body(q_ref, kp_ref, kc_ref, vp_ref, vc_ref, cosc_ref, sinc_ref, cosp_ref, sinp_ref, qg_ref, kg_ref, sk_ref,
             o_ref, l_ref):
        n = pl.program_id(1)
        ecq, eeq = _head_matrices(D_MODEL)
        eck, eek = _head_matrices(KV_W)
        cosc, sinc = cosc_ref[...], sinc_ref[...]
        qh, _, _ = _normrope_fwd(q_ref[...], qg_ref[...], cosc, sinc, ecq, eeq)
        qh = qh * (HEAD_DIM ** -0.5)
        kc, _, _ = _normrope_fwd(kc_ref[...], kg_ref[...], cosc, sinc, eck, eek)
        kp, _, _ = _normrope_fwd(kp_ref[...], kg_ref[...], cosp_ref[...], sinp_ref[...], eck, eek)
        kcat = jnp.concatenate([kp, kc], axis=0)
        vcat = jnp.concatenate([vp_ref[...], vc_ref[...]], axis=0)
        above, causal = _window_masks(blk)
        above = above & (n > 0)
        head_row = lax.broadcasted_iota(jnp.int32, (blk, blk), 0)
        sk_t = jnp.broadcast_to(sk_ref[...], (blk, LANES)).T
        vcat_t = vcat.T.astype(BF16)
        lmat = jnp.zeros((blk, blk), F32)
        groups = range(N_KV_HEADS)
        cols = [slice(g * GROUP_W, (g + 1) * GROUP_W) for g in groups]
        scores = [_dot_nt(_replicate_head(kcat, g).astype(BF16), _head_blocks(qh[:, cols[g]]).astype(BF16))
                  for g in groups]
        probs = []
        for g in groups:
            s = _mask_window(scores[g], above, causal, NEG_BIG)
            sink = _head_rows(sk_t, g)
            m = jnp.maximum(jnp.max(s, axis=0, keepdims=True), sink)
            e = jnp.exp(s - m)
            den = jnp.sum(e, axis=0, keepdims=True) + jnp.exp(sink - m)
            probs.append((e * (1.0 / den)).astype(BF16))
            lse = m + jnp.log(den)
            for h in range(GROUP):
                lmat = lmat + jnp.where(head_row == GROUP * g + h, lse[:, h * blk:(h + 1) * blk], 0.0)
        for g in groups:
            out_t = jnp.dot(vcat_t[g * HEAD_DIM:(g + 1) * HEAD_DIM], probs[g], preferred_element_type=F32)
            o_ref[:, cols[g]] = _stack_heads(out_t, blk).T.astype(BF16)
        l_ref[...] = lmat

    def row(b, n):
        return b * nb + n

    def prev(b, n):
        return b * nb + jnp.maximum(n - 1, 0)

    kw = KV_W
    tab_c = pl.BlockSpec((blk, LANES), lambda b, n: (n, 0))
    tab_p = pl.BlockSpec((blk, LANES), lambda b, n: (jnp.maximum(n - 1, 0), 0))
    outs, landed = _call(
        body,
        grid=(n_seq, nb),
        in_specs=[
            pl.BlockSpec((blk, D_MODEL), lambda b, n: (row(b, n), 0)),
            pl.BlockSpec((blk, kw), lambda b, n: (prev(b, n), ATTN_K_AT // kw)),
            pl.BlockSpec((blk, kw), lambda b, n: (row(b, n), ATTN_K_AT // kw)),
            pl.BlockSpec((blk, kw), lambda b, n: (prev(b, n), ATTN_V_AT // kw)),
            pl.BlockSpec((blk, kw), lambda b, n: (row(b, n), ATTN_V_AT // kw)),
            tab_c, tab_c, tab_p, tab_p,
            pl.BlockSpec((1, D_MODEL), lambda b, n: (0, 0)),
            pl.BlockSpec((1, kw), lambda b, n: (0, 0)),
            pl.BlockSpec((1, LANES), lambda b, n: (0, 0)),
        ],
        out_specs=[pl.BlockSpec((blk, D_MODEL), lambda b, n: (row(b, n), 0)),
                   pl.BlockSpec((blk, LANES), lambda b, n: (row(b, n), 0))],
        out_shape=[jax.ShapeDtypeStruct((t, D_MODEL), BF16), jax.ShapeDtypeStruct((t, LANES), F32)],
        args=(z, z, z, z, z, cos_t, sin_t, cos_t, sin_t, q_gain_t, k_gain_t, sinks_t), name="attn_fwd",
        semantics=("parallel", "parallel"), hosted=hosted)
    return (*outs, landed) if hosted is not None else outs


def _attn_bwd(z, o, lse, do, cos_t, sin_t, q_gain_t, k_gain_t, sinks_t, *, n_seq, seq, hosted=None):
    t = n_seq * seq
    blk = WINDOW
    nb = seq // blk
    kw = KV_W
    scale = HEAD_DIM ** -0.5

    def body(qc_ref, qn_ref, kp_ref, kc_ref, vp_ref, vc_ref, oc_ref, on_ref, doc_ref, don_ref, lc_ref, ln_ref,
             cosc_ref, sinc_ref, cosp_ref, sinp_ref, cosn_ref, sinn_ref, qg_ref, kg_ref, sk_ref,
             dz_ref, vec_ref, dq_s):
        n = pl.program_id(1)
        ecq, eeq = _head_matrices(D_MODEL)
        eck, eek = _head_matrices(KV_W)
        cosc, sinc = cosc_ref[...], sinc_ref[...]
        qg, kg = qg_ref[...], kg_ref[...]
        qhc, nqc, rsqc = _normrope_fwd(qc_ref[...], qg, cosc, sinc, ecq, eeq)
        qhn, _, _ = _normrope_fwd(qn_ref[...], qg, cosn_ref[...], sinn_ref[...], ecq, eeq)
        khc, nkc, rskc = _normrope_fwd(kc_ref[...], kg, cosc, sinc, eck, eek)
        khp, _, _ = _normrope_fwd(kp_ref[...], kg, cosp_ref[...], sinp_ref[...], eck, eek)
        doc = doc_ref[...].astype(F32)
        don = don_ref[...].astype(F32)
        delc = _dot_split(doc * oc_ref[...].astype(F32), ecq)
        deln = _dot_split(don * on_ref[...].astype(F32), ecq)
        lc_t, ln_t, delc_t, deln_t = lc_ref[...], ln_ref[...], delc.T, deln.T
        above, causal = _window_masks(blk)
        above_c, above_n = above & (n > 0), above & (n < nb - 1)
        seg = lax.broadcasted_iota(jnp.int32, (blk, GROUP_W), 1) // HEAD_DIM
        lane = lax.broadcasted_iota(jnp.int32, (1, LANES), 1)
        sk_t = jnp.broadcast_to(sk_ref[...], (blk, LANES)).T
        dsink = jnp.zeros((1, LANES), F32)
        kcat = jnp.concatenate([khp, khc], axis=0)
        vcat = jnp.concatenate([vp_ref[...], vc_ref[...]], axis=0)
        kcat_t = kcat.T.astype(BF16)
        dkh = jnp.zeros((blk, GROUP_W), F32)
        dvh = jnp.zeros((blk, GROUP_W), F32)

        def fold_to(group, t):
            total = t + pltpu.roll(t, HEAD_DIM, 1)
            total = total + pltpu.roll(total, 2 * HEAD_DIM, 1)
            return jnp.where(seg == group, total, 0.0)

        groups = range(N_KV_HEADS)
        cols = [slice(g * GROUP_W, (g + 1) * GROUP_W) for g in groups]
        qsc, qsn = qhc * scale, qhn * scale
        qb_c = [_head_blocks(qsc[:, cols[g]]).astype(BF16) for g in groups]
        qb_n = [_head_blocks(qsn[:, cols[g]]).astype(BF16) for g in groups]
        dob_c = [_head_blocks(doc[:, cols[g]]).astype(BF16) for g in groups]
        dob_n = [_head_blocks(don[:, cols[g]]).astype(BF16) for g in groups]
        raw = []
        for g in groups:
            krep = _replicate_head(kcat, g).astype(BF16)
            vrep = _replicate_head(vcat, g).astype(BF16)
            raw.append((_dot_nt(krep, qb_c[g]), _dot_nt(vrep, dob_c[g]),
                        _dot_nt(krep[blk:], qb_n[g]), _dot_nt(vrep[blk:], dob_n[g])))
        cooked = []
        for g in groups:
            s_c, dp_c, s_n, dp_n = raw[g]
            l_row, d_row = _head_rows(lc_t, g), _head_rows(delc_t, g)
            p_c = _mask_window(jnp.exp(s_c - l_row), above_c, causal, 0.0)
            ds_c = (p_c * (dp_c - d_row)).astype(BF16)
            p_n = jnp.where(above_n, jnp.exp(s_n - _head_rows(ln_t, g)), 0.0)
            ds_n = (p_n * (dp_n - _head_rows(deln_t, g))).astype(BF16)
            cooked.append((p_c[blk:].astype(BF16), ds_c, p_n.astype(BF16), ds_n))
            p_sink = jnp.exp(_head_rows(sk_t, g) - l_row) * d_row
            for h in range(GROUP):
                dsink = dsink + jnp.where(lane == GROUP * g + h,
                                          -jnp.sum(p_sink[:, h * blk:(h + 1) * blk], axis=1, keepdims=True), 0.0)
        for g in groups:
            p_cb, ds_c, p_nb, ds_n = cooked[g]
            dq_t = jnp.dot(kcat_t[g * HEAD_DIM:(g + 1) * HEAD_DIM], ds_c, preferred_element_type=F32)
            dq_s[:, cols[g]] = _stack_heads(dq_t, blk).T * scale
            dk_rep = (jnp.dot(ds_c[blk:], qb_c[g], preferred_element_type=F32)
                      + jnp.dot(ds_n, qb_n[g], preferred_element_type=F32))
            dv_rep = (jnp.dot(p_cb, dob_c[g], preferred_element_type=F32)
                      + jnp.dot(p_nb, dob_n[g], preferred_element_type=F32))
            dkh = dkh + fold_to(g, dk_rep)
            dvh = dvh + fold_to(g, dv_rep)
        dq, dqg = _normrope_bwd(dq_s[...], nqc, rsqc, qg, cosc, sinc, ecq, eeq)
        dk, dkg = _normrope_bwd(dkh, nkc, rskc, kg, cosc, sinc, eck, eek)
        dz_ref[:, :ATTN_K_AT] = dq.astype(BF16)
        dz_ref[:, ATTN_K_AT:ATTN_V_AT] = dk.astype(BF16)
        dz_ref[:, ATTN_V_AT:] = dvh.astype(BF16)

        @pl.when(n == 0)
        def _():
            vec_ref[...] = jnp.zeros_like(vec_ref)

        vec_ref[0:1, :] += dqg
        vec_ref[1:2, 0:kw] += dkg
        vec_ref[2:3, 0:LANES] += dsink

    def row(b, n):
        return b * nb + n

    def prev(b, n):
        return b * nb + jnp.maximum(n - 1, 0)

    def nxt(b, n):
        return b * nb + jnp.minimum(n + 1, nb - 1)

    def tiles(width, col, which):
        return pl.BlockSpec((blk, width), lambda b, n: (which(b, n), col))

    def table(which):
        return pl.BlockSpec((blk, LANES), lambda b, n: (which(0, n), 0))

    outs, landed = _call(
        body,
        grid=(n_seq, nb),
        in_specs=[
            tiles(D_MODEL, 0, row), tiles(D_MODEL, 0, nxt),
            tiles(kw, ATTN_K_AT // kw, prev), tiles(kw, ATTN_K_AT // kw, row),
            tiles(kw, ATTN_V_AT // kw, prev), tiles(kw, ATTN_V_AT // kw, row),
            tiles(D_MODEL, 0, row), tiles(D_MODEL, 0, nxt),
            tiles(D_MODEL, 0, row), tiles(D_MODEL, 0, nxt),
            tiles(LANES, 0, row), tiles(LANES, 0, nxt),
            table(row), table(row), table(prev), table(prev), table(nxt), table(nxt),
            pl.BlockSpec((1, D_MODEL), lambda b, n: (0, 0)),
            pl.BlockSpec((1, kw), lambda b, n: (0, 0)),
            pl.BlockSpec((1, LANES), lambda b, n: (0, 0)),
        ],
        out_specs=[tiles(ATTN_W, 0, row), pl.BlockSpec((None, 8, D_MODEL), lambda b, n: (b, 0, 0))],
        out_shape=[jax.ShapeDtypeStruct((t, ATTN_W), BF16), jax.ShapeDtypeStruct((n_seq, 8, D_MODEL), F32)],
        scratch_shapes=[pltpu.VMEM((blk, D_MODEL), F32)],
        args=(z, z, z, z, z, z, o, o, do, do, lse, lse, cos_t, sin_t, cos_t, sin_t, cos_t, sin_t,
              q_gain_t, k_gain_t, sinks_t), name="attn_bwd", semantics=("parallel", "arbitrary"), hosted=hosted)
    return (*outs, landed) if hosted is not None else outs


MERGE_COLS = 512


def _merge_fwd(z, ya, yb):
    t = ya.shape[0]
    tm, tc = min(512, t), MERGE_COLS

    def body(ga_ref, gb_ref, ya_ref, yb_ref, o_ref):
        o_ref[...] = (_sig(ga_ref[...]) * ya_ref[...] + _sig(gb_ref[...]) * yb_ref[...]).astype(BF16)

    tile = pl.BlockSpec((tm, tc), lambda i, j: (i, j))
    return pl.pallas_call(
        body,
        grid=(t // tm, D_MODEL // tc),
        in_specs=[pl.BlockSpec((tm, tc), lambda i, j: (i, j)),
                  pl.BlockSpec((tm, tc), lambda i, j: (i, D_MODEL // tc + j)), tile, tile],
        out_specs=tile,
        out_shape=jax.ShapeDtypeStruct((t, D_MODEL), BF16),
        compiler_params=_params("parallel", "parallel"),
        name="merge_fwd",
    )(z, z, ya, yb)


def _rope_tables(seq):
    inv = ROPE_THETA ** (-jnp.arange(0, HEAD_DIM, 2, dtype=F32) / HEAD_DIM)
    ang = jnp.arange(seq, dtype=F32)[:, None] * inv[None, :]
    cos, sin = jnp.cos(ang), jnp.sin(ang)
    return jnp.tile(jnp.concatenate([cos, cos], axis=1), (1, 2)), jnp.tile(jnp.concatenate([-sin, sin], axis=1), (1, 2))


def _block_diag_tiles(w):
    per = RNN_TILE // RNN_BLOCK_W
    w4 = w.reshape(D_MODEL // RNN_TILE, per, RNN_BLOCK_W, RNN_BLOCK_W)
    eye = jnp.eye(per, dtype=w.dtype)
    dense = jnp.einsum("tpij,pq->tpiqj", w4, eye)
    return dense.reshape(D_MODEL // RNN_TILE, RNN_TILE, RNN_TILE).astype(BF16)


def _block_diag_extract(dense):
    per = RNN_TILE // RNN_BLOCK_W
    d5 = dense.reshape(D_MODEL // RNN_TILE, per, RNN_BLOCK_W, per, RNN_BLOCK_W)
    blocks = jnp.stack([d5[:, p, :, p, :] for p in range(per)], axis=1)
    return blocks.reshape(D_MODEL // RNN_BLOCK_W, RNN_BLOCK_W, RNN_BLOCK_W)


def _local_step(x, p, target, w, *, n_seq, seq, comm=None):
    w = dict(w)

    def run(tag, fn, *args, **kwargs):
        hosted = comm.host(tag) if comm is not None else None
        if hosted is None:
            return fn(*args, **kwargs)
        *outs, landed = fn(*args, hosted=hosted, **kwargs)
        comm.landed(tag, landed, w)
        return outs[0] if len(outs) == 1 else outs

    def ready(batch, grads, extra=None):
        if comm is not None:
            comm.ready(batch, grads, extra)

    cos_t, sin_t = _rope_tables(seq)
    q_gain_t = jnp.tile(w["q_gain"], (1, N_Q_HEADS))
    k_gain_t = jnp.tile(w["k_gain"], (1, N_KV_HEADS))
    sinks_t = jnp.pad(w["sinks"], ((0, 0), (0, LANES - N_Q_HEADS)))
    wrg_bd, wig_bd = _block_diag_tiles(w["w_rg"]), _block_diag_tiles(w["w_ig"])
    dims = dict(n_seq=n_seq, seq=seq)

    h = _rmsnorm_fwd(x, w["g_mix"], name="norm_mix")
    z_rnn = _matmul(h, w["w_in"], mode="nn", tm=1024, tn=1024, out_dtypes=[F32], name="mm_in_rnn",
                    b_cols=[(0, COL_RNN_END)])
    w_in_attn, w_in_gate = w["w_in"][:, COL_RNN_END:COL_ATTN_END], w["w_in"][:, COL_ATTN_END:]
    z_attn = _matmul(h, w_in_attn, mode="nn", tm=1024, tn=1024, out_dtypes=[F32], name="mm_in_attn")
    z_gate = _matmul(h, w_in_gate, mode="nn", tm=1024, tn=1024, out_dtypes=[F32], name="mm_in_gate")
    xc, hr, ya_in = run("rnn_fwd", _rnn_fwd, z_rnn, w["conv_w"], w["conv_b"], wrg_bd, w["b_rg"], wig_bd, w["b_ig"],
                        w["lru_lambda"], **dims)
    o, lse = run("attn_fwd", _attn_fwd, z_attn, cos_t, sin_t, q_gain_t, k_gain_t, sinks_t, **dims)
    ya = run("mm_rnn_proj", _matmul, ya_in, w["w_rnn_proj"], mode="nn", tm=1024, tn=1024, out_dtypes=[F32],
             name="mm_rnn_proj")
    yb = _matmul(o, w["w_attn_proj"], mode="nn", tm=1024, tn=1024, out_dtypes=[F32], name="mm_attn_proj")
    merged = _merge_fwd(z_gate, ya, yb)
    def residual_then_norm(acc, res, gain):
        new = res + acc
        return new, _rmsnorm_rows(new, gain)

    x1, hm = _matmul(merged, w["w_out"], mode="nn", tm=512, tn=1024, out_dtypes=[F32, BF16], name="mm_out",
                     epilogue=residual_then_norm, extras=(x,), row_vecs=(w["g_mlp"],))
    act = _matmul(hm, w["w_up"], mode="nn", tm=1024, tn=1024, out_dtypes=[BF16], name="mm_up",
                  epilogue=lambda acc: (jnp.square(jnp.maximum(acc, 0.0)),))
    x2, hp = _matmul(act, w["w_down"], mode="nn", tm=512, tn=1024, out_dtypes=[F32, BF16], name="mm_down",
                     epilogue=residual_then_norm, extras=(x1,), row_vecs=(w["g_ple"],))
    p_bf = p.astype(BF16)
    e = _matmul(p_bf, w["w_ple_proj"], mode="nn", tm=1024, tn=1024, out_dtypes=[F32], name="mm_ple_proj")

    def loss_head(gt, x2v, ev, tgt):
        sg = _sig(gt)
        diff = x2v + ev * sg - tgt
        dx = diff * (1.0 / D_MODEL)
        return dx, dx * ev * sg * (1.0 - sg), dx * sg, jnp.sum(diff * diff, axis=0, keepdims=True)

    dx3, dgt, de, loss_row = _matmul(hp, w["w_ple_gate"], mode="nn", tm=512, tn=1024, out_dtypes=[F32, BF16, BF16],
                                     name="mm_ple_gate", epilogue=loss_head, extras=(x2, e, target), n_row_sums=1)

    g = {}
    g["w_ple_proj"] = _matmul_tn(p_bf, de, tk=PLE_DIM, tn=1024, tt=1024, name="mm_d_ple_proj",
                                 slot_cols=D_MODEL // N_DEV)
    g["w_ple_gate"] = _matmul_tn(hp, dgt, tk=1024, tn=1024, tt=512, name="mm_d_ple_gate")
    def through_norm(dy, xv, dres, gain):
        dx, dgain = _rmsnorm_bwd_rows(dy, xv, dres, gain)
        return dx, dx, dgain

    dx2, dx2_bf, g["g_ple"] = _matmul(
        dgt, w["w_ple_gate"], mode="nt", tm=512, tn=1024, out_dtypes=[F32, BF16], name="mm_dhp",
        epilogue=through_norm, extras=(x2, dx3), row_vecs=(w["g_ple"],), n_row_sums=1)
    g["w_down"] = _matmul_tn(act, dx2_bf, tk=1024, tn=1024, tt=512, name="mm_d_down")
    du = _matmul(dx2_bf, w["w_down"], mode="nt", tm=1024, tn=1024, out_dtypes=[BF16], name="mm_dact",
                 epilogue=lambda acc, a: (acc * (2.0 * jnp.sqrt(a.astype(F32))),), extras=(act,))
    g["w_up"] = _matmul_tn(hm, du, tk=1024, tn=1024, tt=512, name="mm_d_up", slot_cols=D_FF // N_DEV)
    ready(1, g)
    dx1, dx1_bf, g["g_mlp"] = run(
        "mm_dhm", _matmul, du, w["w_up"], mode="nt", tm=512, tn=1024, out_dtypes=[F32, BF16], name="mm_dhm",
        epilogue=through_norm, extras=(x1, dx2), row_vecs=(w["g_mlp"],), n_row_sums=1)
    g["w_out"] = _matmul_tn(merged, dx1_bf, tk=1024, tn=1024, tt=512, name="mm_d_out")
    def merge_bwd(dm, ga, gb, yav, ybv):
        sa, sb = _sig(ga), _sig(gb)
        return dm * sa, dm * sb, dm * yav * sa * (1.0 - sa), dm * ybv * sb * (1.0 - sb)

    dya, dyb, dga, dgb = _matmul(dx1_bf, w["w_out"], mode="nt", tm=512, tn=1024, out_dtypes=[BF16] * 4,
                                 name="mm_dmerged", epilogue=merge_bwd, extras=(z_gate, z_gate, ya, yb),
                                 extra_col_blocks=(0, 1, 0, 0))
    g["w_rnn_proj"] = _matmul_tn(ya_in, dya, tk=1024, tn=1024, tt=512, name="mm_d_rnn_proj")
    g["w_attn_proj"] = _matmul_tn(o, dyb, tk=1024, tn=1024, tt=512, name="mm_d_attn_proj")
    ready(2, g)
    dya_in = run("mm_dya_in", _matmul, dya, w["w_rnn_proj"], mode="nt", tm=1024, tn=1024, out_dtypes=[F32],
                 name="mm_dya_in")
    do = _matmul(dyb, w["w_attn_proj"], mode="nt", tm=1024, tn=1024, out_dtypes=[BF16], name="mm_do")
    dx_rnn, dg_rnn, dwrg_dense, dwig_dense, rnn_vec = run(
        "rnn_bwd", _rnn_bwd, dya_in, z_rnn, xc, hr, w["conv_w"], wrg_bd, w["b_rg"], wig_bd, w["b_ig"],
        w["lru_lambda"], **dims)
    dz_attn, attn_vec = run("attn_bwd", _attn_bwd, z_attn, o, lse, do, cos_t, sin_t, q_gain_t, k_gain_t, sinks_t,
                            **dims)
    dz_parts = (dx_rnn, dg_rnn, dz_attn, dga, dgb)
    g["w_in"] = jnp.concatenate(
        _matmul_tn_multi(h, dz_parts[:2], tt=512, name="mm_d_in_rnn")
        + _matmul_tn_multi(h, dz_parts[2:], tt=512, name="mm_d_in_rest"), axis=1)
    g["w_rg"] = _block_diag_extract(dwrg_dense)
    g["w_ig"] = _block_diag_extract(dwig_dense)
    g["b_rg"], g["b_ig"], g["lru_lambda"], g["conv_b"] = (rnn_vec[i:i + 1] for i in range(4))
    g["conv_w"] = rnn_vec[4:8]
    attn_vec = attn_vec[0] if n_seq == 1 else functools.reduce(jnp.add, [attn_vec[b] for b in range(n_seq)])
    g["q_gain"] = attn_vec[0].reshape(N_Q_HEADS, HEAD_DIM).sum(axis=0)[None, :]
    g["k_gain"] = attn_vec[1, :KV_W].reshape(N_KV_HEADS, HEAD_DIM).sum(axis=0)[None, :]
    g["sinks"] = attn_vec[2:3, :N_Q_HEADS]
    ready(3, g, {LOSS_ROW: loss_row})
    windows = ((w["w_in"], (0, D_MODEL)), (w["w_in"], (D_MODEL, D_MODEL)), (w_in_attn, (0, ATTN_W)),
               (w_in_gate, (0, D_MODEL)), (w_in_gate, (D_MODEL, D_MODEL)))
    grad_x, g["g_mix"] = run(
        "mm_dh", _matmul, dz_parts, [wd[0] for wd in windows], mode="nt", tm=256, tn=1024, out_dtypes=[F32],
        name="mm_dh", b_cols=[wd[1] for wd in windows], epilogue=_rmsnorm_bwd_rows, extras=(x, dx1),
        row_vecs=(w["g_mix"],), n_row_sums=1)
    return jnp.sum(loss_row), grad_x, g


MESH_ID = pl.DeviceIdType.MESH


def _coords(index):
    return (index >> 2) & 1, (index >> 1) & 1, index & 1


def _exchange(srcs, kinds, *, name):
    n = len(srcs)
    n_peer = N_DEV - 1

    def body(*refs):
        src, dst = refs[:n], refs[n:2 * n]
        send_sems, recv_sems, local_sems = refs[2 * n:]
        me = 4 * lax.axis_index("x") + 2 * lax.axis_index("y") + lax.axis_index("c")

        def remote(i, d):
            peer = (me + d) & (N_DEV - 1)
            piece = src[i] if kinds[i] == "gather" else src[i].at[peer]
            return pltpu.make_async_remote_copy(
                src_ref=piece, dst_ref=dst[i].at[me], send_sem=send_sems.at[i * n_peer + d - 1],
                recv_sem=recv_sems.at[i * n_peer + d - 1], device_id=_coords(peer), device_id_type=MESH_ID)

        def arrival(i, d):
            sender = (me - d) & (N_DEV - 1)
            piece = src[i] if kinds[i] == "gather" else src[i].at[sender]
            return pltpu.make_async_remote_copy(
                src_ref=piece, dst_ref=dst[i].at[sender], send_sem=send_sems.at[i * n_peer + d - 1],
                recv_sem=recv_sems.at[i * n_peer + d - 1], device_id=_coords(sender), device_id_type=MESH_ID)

        own = []
        for i in range(n):
            piece = src[i] if kinds[i] == "gather" else src[i].at[me]
            own.append(pltpu.make_async_copy(piece, dst[i].at[me], local_sems.at[i]))
            own[-1].start()
        sent = [remote(i, d) for d in range(1, N_DEV) for i in range(n)]
        for cp in sent:
            cp.start()
        for d in range(1, N_DEV):
            for i in range(n):
                arrival(i, d).wait_recv()
        for cp in sent:
            cp.wait_send()
        for cp in own:
            cp.wait()

    def out_of(s, kind):
        shape = s.shape if kind == "scatter" else (N_DEV,) + s.shape
        return jax.ShapeDtypeStruct(shape, s.dtype)

    any_spec = pl.BlockSpec(memory_space=pl.ANY)
    return pl.pallas_call(
        body,
        in_specs=[any_spec] * n,
        out_specs=[any_spec] * n,
        out_shape=[out_of(s, k) for s, k in zip(srcs, kinds)],
        scratch_shapes=[pltpu.SemaphoreType.DMA((n * n_peer,)), pltpu.SemaphoreType.DMA((n * n_peer,)),
                        pltpu.SemaphoreType.DMA((n,))],
        compiler_params=pltpu.CompilerParams(has_side_effects=True),
        name=name,
    )(*srcs)


def _remote(src, dst, send_sem, recv_sem, to):
    return pltpu.make_async_remote_copy(src_ref=src, dst_ref=dst, send_sem=send_sem, recv_sem=recv_sem,
                                        device_id=to, device_id_type=MESH_ID)


def _gather_two_level(shards, *, name):
    n = len(shards)
    per = N_DEV - 1

    def body(*refs):
        src, dst = refs[:n], refs[n:2 * n]
        send_sems, recv_sems, local_sems = refs[2 * n:]
        x, y, c = lax.axis_index("x"), lax.axis_index("y"), lax.axis_index("c")
        me, sibling = (x, y, c), (x, y, 1 - c)
        chips = [(1 - x, y), (x, 1 - y), (1 - x, 1 - y)]

        def slot(pos):
            return 4 * pos[0] + 2 * pos[1] + pos[2]

        def copy(i, k, block, to, from_shard=False):
            source = src[i] if from_shard else dst[i].at[slot(block)]
            return _remote(source, dst[i].at[slot(block)], send_sems.at[i * per + k], recv_sems.at[i * per + k], to)

        mine = [pltpu.make_async_copy(src[i], dst[i].at[slot(me)], local_sems.at[i]) for i in range(n)]
        for cp in mine:
            cp.start()
        first = []
        for i in range(n):
            first.append(copy(i, 0, me, sibling, from_shard=True))
            first += [copy(i, 1 + j, me, (*chip, c), from_shard=True) for j, chip in enumerate(chips)]
        for cp in first:
            cp.start()
        passed = []
        for i in range(n):
            for j, chip in enumerate(chips):
                copy(i, 1 + j, (*chip, c), me).wait_recv()
                passed.append(copy(i, 4 + j, (*chip, c), sibling))
                passed[-1].start()
        for i in range(n):
            copy(i, 0, sibling, me).wait_recv()
            for j, chip in enumerate(chips):
                copy(i, 4 + j, (*chip, 1 - c), me).wait_recv()
        for cp in first + passed:
            cp.wait_send()
        for cp in mine:
            cp.wait()

    any_spec = pl.BlockSpec(memory_space=pl.ANY)
    return pl.pallas_call(
        body,
        in_specs=[any_spec] * n,
        out_specs=[any_spec] * n,
        out_shape=[jax.ShapeDtypeStruct((N_DEV,) + s.shape, s.dtype) for s in shards],
        scratch_shapes=[pltpu.SemaphoreType.DMA((n * per,)), pltpu.SemaphoreType.DMA((n * per,)),
                        pltpu.SemaphoreType.DMA((n,))],
        name=name,
    )(*shards)


CHIPS = N_DEV // 2


def _other_chips(x, y):
    return [(x, 1 - y), (1 - x, y), (1 - x, 1 - y)]


def _hosted_gather_first(shards):
    n = len(shards)
    per = CHIPS

    def plan(src, dst, send_sems, recv_sems, local_sems, first_sem):
        x, y, c = lax.axis_index("x"), lax.axis_index("y"), lax.axis_index("c")
        peers = [(x, y, 1 - c)] + [(*chip, c) for chip in _other_chips(x, y)]
        copies = []
        for i in range(n):
            own = pltpu.make_async_copy(src[i], dst[i].at[4 * x + 2 * y + c], local_sems.at[first_sem + i])
            copies.append(_Xfer(own.start, own.wait))
        for j, peer in enumerate(peers):
            for i in range(n):
                k = first_sem + i * per + j
                out = _remote(src[i], dst[i].at[4 * x + 2 * y + c], send_sems.at[k], recv_sems.at[k], peer)
                arrival = _remote(src[i], dst[i].at[4 * peer[0] + 2 * peer[1] + peer[2]], send_sems.at[k],
                                  recv_sems.at[k], peer)

                def wait(out=out, arrival=arrival):
                    arrival.wait_recv()
                    out.wait_send()

                copies.append(_Xfer(out.start, wait))
        return copies

    out_shape = tuple(jax.ShapeDtypeStruct((N_DEV,) + s.shape, s.dtype) for s in shards)
    return _Hosted(tuple(shards), out_shape, n * per, plan)


def _hosted_gather_second(landed):
    n = len(landed)
    per = CHIPS - 1

    def plan(src, dst, send_sems, recv_sems, local_sems, first_sem):
        x, y, c = lax.axis_index("x"), lax.axis_index("y"), lax.axis_index("c")
        copies = []
        for j, chip in enumerate(_other_chips(x, y)):
            mine, theirs = 4 * chip[0] + 2 * chip[1] + c, 4 * chip[0] + 2 * chip[1] + 1 - c
            for i in range(n):
                k = first_sem + i * per + j
                out = _remote(src[i].at[mine], dst[i].at[mine], send_sems.at[k], recv_sems.at[k], (x, y, 1 - c))
                arrival = _remote(src[i].at[theirs], dst[i].at[theirs], send_sems.at[k], recv_sems.at[k],
                                  (x, y, 1 - c))

                def wait(out=out, arrival=arrival):
                    arrival.wait_recv()
                    out.wait_send()

                copies.append(_Xfer(out.start, wait))
        return copies

    out_shape = tuple(jax.ShapeDtypeStruct(a.shape, a.dtype) for a in landed)
    return _Hosted(tuple(landed), out_shape, n * per, plan, tuple((i, i) for i in range(n)))


def _hosted_sibling_swap(arrays, sliced):
    n_sems = sum(CHIPS if s else 1 for s in sliced)

    def plan(src, dst, send_sems, recv_sems, local_sems, first_sem):
        x, y, c = lax.axis_index("x"), lax.axis_index("y"), lax.axis_index("c")
        sibling = (x, y, 1 - c)
        copies, k = [], first_sem
        for i, is_sliced in enumerate(sliced):
            pieces = [(src[i].at[2 * s + 1 - c], dst[i].at[s]) for s in range(CHIPS)] if is_sliced else [(src[i], dst[i])]
            for source, target in pieces:
                cp = _remote(source, target, send_sems.at[k], recv_sems.at[k], sibling)
                copies.append(_Xfer(cp.start, cp.wait))
                k += 1
        return copies

    out_shape = tuple(jax.ShapeDtypeStruct((CHIPS,) + a.shape[1:] if s else a.shape, a.dtype)
                      for a, s in zip(arrays, sliced))
    return _Hosted(tuple(arrays), out_shape, n_sems, plan)


def _hosted_chip_exchange(arrays, sliced):
    n = len(arrays)
    per = CHIPS - 1

    def plan(src, dst, send_sems, recv_sems, local_sems, first_sem):
        x, y, c = lax.axis_index("x"), lax.axis_index("y"), lax.axis_index("c")
        chip = 2 * x + y
        copies = []
        for i in range(n):
            own = pltpu.make_async_copy(src[i].at[chip] if sliced[i] else src[i], dst[i].at[chip],
                                        local_sems.at[first_sem + i])
            copies.append(_Xfer(own.start, own.wait))
        for d in range(1, CHIPS):
            other = chip ^ d
            to = ((other >> 1) & 1, other & 1, c)
            for i in range(n):
                k = first_sem + i * per + d - 1
                source = src[i].at[other] if sliced[i] else src[i]
                out = _remote(source, dst[i].at[chip], send_sems.at[k], recv_sems.at[k], to)
                arrival = _remote(source, dst[i].at[other], send_sems.at[k], recv_sems.at[k], to)

                def wait(out=out, arrival=arrival):
                    arrival.wait_recv()
                    out.wait_send()

                copies.append(_Xfer(out.start, wait))
        return copies

    out_shape = tuple(jax.ShapeDtypeStruct(a.shape if s else (CHIPS,) + a.shape, a.dtype)
                      for a, s in zip(arrays, sliced))
    return _Hosted(tuple(arrays), out_shape, n * per, plan)


def _add_sibling(parts, received, core, *, name):
    _, r, cols = parts.shape
    tr = min(256, r)

    def body(core_ref, a_ref, b_ref, o_ref):
        o_ref[...] = (a_ref[...] + b_ref[...]).astype(BF16)

    grid_spec = pltpu.PrefetchScalarGridSpec(
        num_scalar_prefetch=1,
        grid=(CHIPS, r // tr),
        in_specs=[pl.BlockSpec((None, tr, cols), lambda k, i, core_ref: (2 * k + core_ref[0], i, 0)),
                  pl.BlockSpec((None, tr, cols), lambda k, i, core_ref: (k, i, 0))],
        out_specs=pl.BlockSpec((None, tr, cols), lambda k, i, core_ref: (k, i, 0)),
    )
    return pl.pallas_call(body, grid_spec=grid_spec, out_shape=jax.ShapeDtypeStruct((CHIPS, r, cols), BF16),
                          compiler_params=_params("parallel", "parallel"), name=name)(core, parts, received)


def _add_whole(a, b, *, name):
    def body(a_ref, b_ref, o_ref):
        o_ref[...] = a_ref[...] + b_ref[...]

    return pl.pallas_call(body, out_shape=jax.ShapeDtypeStruct(a.shape, F32), name=name)(a, b)


def _adamw(parts, w, m, v, *, name):
    r, c = w.shape
    n_parts = parts.shape[0]
    tr = min(256, r)
    c1 = 1.0 - ADAM_B1 ** ADAM_STEP
    c2 = 1.0 - ADAM_B2 ** ADAM_STEP

    def body(p_ref, w_ref, m_ref, v_ref, g_ref, d_ref, nm_ref, nv_ref):
        g = p_ref[0].astype(F32)
        for s in range(1, n_parts):
            g = g + p_ref[s].astype(F32)
        nm = ADAM_B1 * m_ref[...] + (1.0 - ADAM_B1) * g
        nv = ADAM_B2 * v_ref[...] + (1.0 - ADAM_B2) * (g * g)
        g_ref[...] = g
        nm_ref[...] = nm
        nv_ref[...] = nv
        d_ref[...] = -ADAM_LR * ((nm / c1) / (jnp.sqrt(nv / c2) + ADAM_EPS) + ADAM_WD * w_ref[...])

    tile = pl.BlockSpec((tr, c), lambda i: (i, 0))
    return pl.pallas_call(
        body,
        grid=(r // tr,),
        in_specs=[pl.BlockSpec((n_parts, tr, c), lambda i: (0, i, 0)), tile, tile, tile],
        out_specs=[tile] * 4,
        out_shape=[jax.ShapeDtypeStruct((r, c), F32)] * 4,
        compiler_params=_params("parallel"),
        name=name,
    )(parts, w, m, v)


BIG = ("w_in", "w_rnn_proj", "w_attn_proj", "w_out", "w_up", "w_down", "w_ple_gate", "w_ple_proj")
LOSS_ROW = "loss"
SMALL = (("conv_b", 1), ("b_rg", 1), ("b_ig", 1), ("lru_lambda", 1), ("g_mlp", 1), ("g_ple", 1),
         ("q_gain", 1), ("k_gain", 1), ("sinks", 1), (LOSS_ROW, 1), ("w_rg", 64), ("w_ig", 64))
SMALL_ROWS = 144
ROW_SHARDED = ("w_rnn_proj", "w_attn_proj", "w_out", "w_down", "w_ple_gate")
COL_SHARDED = ("w_in", "w_up", "w_ple_proj")
BATCHES = {1: ("w_ple_proj", "w_ple_gate", "w_down", "w_up"), 2: ("w_out", "w_rnn_proj", "w_attn_proj"),
           3: ("w_in", "conv_w")}
SMALL_BATCH = 4


def _pack_small(vals):
    rows = []
    for nm, nrow in SMALL:
        flat = vals[nm].reshape(-1).astype(F32)
        rows.append(jnp.pad(flat, (0, nrow * D_MODEL - flat.shape[0])).reshape(nrow, D_MODEL))
    used = sum(nrow for _, nrow in SMALL)
    rows.append(jnp.zeros((SMALL_ROWS - used, D_MODEL), F32))
    return jnp.concatenate(rows, axis=0)


def _unpack_small(packed, shapes):
    out, at = {}, 0
    for nm, nrow in SMALL:
        size = 1
        for s in shapes[nm]:
            size *= s
        out[nm] = packed[at:at + nrow].reshape(-1)[:size].reshape(shapes[nm])
        at += nrow
    return out


def _full_weight(name, landed):
    if name in COL_SHARDED:
        return landed.transpose(1, 0, 2).reshape(landed.shape[1], N_DEV * landed.shape[2])
    return landed.reshape(N_DEV * landed.shape[1], landed.shape[2])


def _owner_slots(name, grad):
    if name == "w_in":
        return grad.reshape(D_MODEL, N_DEV, IN_TOTAL // N_DEV).transpose(1, 0, 2)
    if name == "conv_w":
        return grad.reshape(CONV_W, N_DEV, D_MODEL // N_DEV).transpose(1, 0, 2)
    if name in COL_SHARDED:
        return grad
    return grad.reshape(N_DEV, grad.shape[0] // N_DEV, grad.shape[1])


class _StepExchanges:
    FIRST, SECOND = "first", "second"
    EARLY, MID, LATE = ("w_rnn_proj", "w_attn_proj", "w_out"), ("w_up",), ("w_down", "w_ple_gate", "w_ple_proj")
    GATHERS = {"rnn_fwd": ((FIRST, EARLY), (FIRST, MID)),
               "attn_fwd": ((SECOND, EARLY), (SECOND, MID), (FIRST, LATE)), "mm_rnn_proj": ((SECOND, LATE),)}
    SWAPS = {"mm_dhm": 1, "mm_dya_in": 2}
    CHIP_EXCHANGES = {"rnn_bwd": (1,), "attn_bwd": (2,), "mm_dh": (3, SMALL_BATCH)}

    def __init__(self, shards, core):
        self.shards = shards
        self.core = core
        self.parts, self.swapped, self.summed, self.half_gathered = {}, {}, {}, {}

    def ready(self, batch, grads, extra=None):
        arrays = [_owner_slots(nm, grads[nm]) for nm in BATCHES[batch]]
        self.parts[batch] = (arrays, [True] * len(arrays))
        if batch not in self.SWAPS.values():
            self.parts[SMALL_BATCH] = ([_pack_small({**grads, **extra})], [False])
            both = [a + b for a, b in zip(self.parts[batch], self.parts[SMALL_BATCH])]
            _, swapped = _call(
                lambda: None, grid=(1,), in_specs=[], out_specs=[], out_shape=[], args=(), name="swap_last",
                semantics=("arbitrary",), hosted=_hosted_sibling_swap(*both))
            self.swapped[batch], self.swapped[SMALL_BATCH] = swapped[:-1], swapped[-1:]

    def host(self, tag):
        if tag in self.GATHERS:
            return _merge_hosted([
                _hosted_gather_first([self.shards[nm] for nm in group]) if half == self.FIRST
                else _hosted_gather_second([self.half_gathered[nm] for nm in group])
                for half, group in self.GATHERS[tag]])
        if tag in self.SWAPS:
            return _hosted_sibling_swap(*self.parts[self.SWAPS[tag]])
        if tag in self.CHIP_EXCHANGES:
            hosted = []
            for batch in self.CHIP_EXCHANGES[tag]:
                arrays, sliced = self.parts[batch]
                labels = BATCHES.get(batch, ("small",))
                sums = [_add_sibling(a, r, self.core, name="add_" + lb) if s else _add_whole(a, r, name="add_" + lb)
                        for a, r, s, lb in zip(arrays, self.swapped[batch], sliced, labels)]
                hosted.append(_hosted_chip_exchange(sums, sliced))
            return _merge_hosted(hosted)
        return None

    def landed(self, tag, landed, weights):
        if tag in self.GATHERS:
            names = [(half, nm) for half, group in self.GATHERS[tag] for nm in group]
            for (half, nm), buf in zip(names, landed):
                if half == self.FIRST:
                    self.half_gathered[nm] = buf
                else:
                    weights[nm] = _full_weight(nm, buf)
        elif tag in self.SWAPS:
            self.swapped[self.SWAPS[tag]] = landed
        else:
            at = 0
            for batch in self.CHIP_EXCHANGES[tag]:
                count = len(self.parts[batch][0])
                self.summed[batch] = landed[at:at + count]
                at += count


def kernel(x, p, g_mix, w_in, conv_w, conv_b, w_rg, b_rg, w_ig, b_ig, lru_lambda, w_rnn_proj, q_gain, k_gain, sinks, w_attn_proj, w_out, g_mlp, w_up, w_down, g_ple, w_ple_gate, w_ple_proj, loss_target, m_g_mix, m_w_in, m_conv_w, m_conv_b, m_w_rg, m_b_rg, m_w_ig, m_b_ig, m_lru_lambda, m_w_rnn_proj, m_q_gain, m_k_gain, m_sinks, m_w_attn_proj, m_w_out, m_g_mlp, m_w_up, m_w_down, m_g_ple, m_w_ple_gate, m_w_ple_proj, v_g_mix, v_w_in, v_conv_w, v_conv_b, v_w_rg, v_b_rg, v_w_ig, v_b_ig, v_lru_lambda, v_w_rnn_proj, v_q_gain, v_k_gain, v_sinks, v_w_attn_proj, v_w_out, v_g_mlp, v_w_up, v_w_down, v_g_ple, v_w_ple_gate, v_w_ple_proj):
    names = ("g_mix", "w_in", "conv_w", "conv_b", "w_rg", "b_rg", "w_ig", "b_ig", "lru_lambda", "w_rnn_proj",
             "q_gain", "k_gain", "sinks", "w_attn_proj", "w_out", "g_mlp", "w_up", "w_down", "g_ple",
             "w_ple_gate", "w_ple_proj")
    wts = dict(zip(names, (g_mix, w_in, conv_w, conv_b, w_rg, b_rg, w_ig, b_ig, lru_lambda, w_rnn_proj, q_gain,
                           k_gain, sinks, w_attn_proj, w_out, g_mlp, w_up, w_down, g_ple, w_ple_gate, w_ple_proj)))
    mom1 = dict(zip(names, (m_g_mix, m_w_in, m_conv_w, m_conv_b, m_w_rg, m_b_rg, m_w_ig, m_b_ig, m_lru_lambda,
                            m_w_rnn_proj, m_q_gain, m_k_gain, m_sinks, m_w_attn_proj, m_w_out, m_g_mlp, m_w_up,
                            m_w_down, m_g_ple, m_w_ple_gate, m_w_ple_proj)))
    mom2 = dict(zip(names, (v_g_mix, v_w_in, v_conv_w, v_conv_b, v_w_rg, v_b_rg, v_w_ig, v_b_ig, v_lru_lambda,
                            v_w_rnn_proj, v_q_gain, v_k_gain, v_sinks, v_w_attn_proj, v_w_out, v_g_mlp, v_w_up,
                            v_w_down, v_g_ple, v_w_ple_gate, v_w_ple_proj)))
    n_seq, seq, _ = x.shape
    core = lax.axis_index("c").astype(jnp.int32).reshape(1)

    shards = {nm: wts[nm][0].astype(BF16) for nm in BIG}
    w_in_all, conv_all = _gather_two_level([shards["w_in"], conv_w[0]], name="gather_w_in")
    w = {nm: wts[nm] for nm in names if nm not in BIG}
    w["w_rg"], w["w_ig"] = w_rg[0], w_ig[0]
    w["conv_w"] = conv_all.transpose(1, 0, 2).reshape(CONV_W, D_MODEL)
    w["w_in"] = _full_weight("w_in", w_in_all)
    comm = _StepExchanges(shards, core)
    loss_sum, grad_x, g = _local_step(
        x.reshape(n_seq * seq, D_MODEL), p.reshape(n_seq * seq, PLE_DIM), loss_target.reshape(n_seq * seq, D_MODEL),
        w, n_seq=n_seq, seq=seq, comm=comm)
    del loss_sum

    res = {}
    for batch, batch_names in BATCHES.items():
        for nm, summed in zip(batch_names, comm.summed[batch]):
            res[nm] = _adamw(summed, wts[nm][0], mom1[nm][0], mom2[nm][0], name="adamw_" + nm)
    g_mix_parts, = _exchange([g["g_mix"]], ["gather"], name="gather_g_mix")
    res["g_mix"] = [r[0] for r in _adamw(g_mix_parts, g_mix, m_g_mix, v_g_mix, name="adamw_g_mix")]
    small_names = [nm for nm, _ in SMALL if nm != LOSS_ROW]
    full_small = {}
    for src, key in ((wts, "w"), (mom1, "m"), (mom2, "v")):
        vals = {nm: src[nm][0] for nm in small_names}
        vals[LOSS_ROW] = jnp.zeros((1,), F32)
        full_small[key] = _pack_small(vals)
    small_res = _adamw(comm.summed[SMALL_BATCH][0],full_small["w"], full_small["m"], full_small["v"], name="adamw_small")
    shapes = {nm: wts[nm].shape[1:] for nm in small_names}
    shapes[LOSS_ROW] = (D_MODEL,)
    small_out = [_unpack_small(r, shapes) for r in small_res]
    for nm in small_names:
        res[nm] = [so[nm] for so in small_out]
    loss = jnp.sum(small_out[0][LOSS_ROW]) * (0.5 / D_MODEL)

    outs = [loss, grad_x.reshape(n_seq, seq, D_MODEL)]
    for k in range(4):
        outs.extend(res[nm][k][None] for nm in names)
    return tuple(outs)
```

```python
import functools
from typing import Callable, NamedTuple

import jax
import jax.numpy as jnp
from jax import lax
from jax.experimental import pallas as pl
from jax.experimental.pallas import tpu as pltpu

F32 = jnp.float32
BF16 = jnp.bfloat16

N_DEV = 8
D_MODEL = 1024
RNN_BLOCK_W = 64
CONV_W = 4
LRU_C = 8.0
HEAD_DIM = 64
N_Q_HEADS = 16
N_KV_HEADS = 4
KV_W = N_KV_HEADS * HEAD_DIM
WINDOW = 128
ROPE_THETA = 10000.0
D_FF = 4096
PLE_DIM = 256
NORM_EPS = 1e-6
IN_TOTAL = 5632
COL_RNN_END, COL_ATTN_END = 2048, 3584
ATTN_W = COL_ATTN_END - COL_RNN_END
ATTN_K_AT, ATTN_V_AT = 1024, 1280

ADAM_LR = 0.001
ADAM_B1 = 0.9
ADAM_B2 = 0.999
ADAM_EPS = 1e-08
ADAM_WD = 0.01
ADAM_STEP = 10

LANES = 128
SUBLANES = 8
RNN_TILE = 256
VMEM_LIMIT = 48 * 1024 * 1024
NEG_BIG = -1e30


def _params(*sem):
    return pltpu.CompilerParams(dimension_semantics=sem if sem else None, vmem_limit_bytes=VMEM_LIMIT)


def _sig(x):
    return 0.5 * jnp.tanh(0.5 * x) + 0.5


def _dot_nt(a, b):
    return lax.dot_general(a, b, (((1,), (1,)), ((), ())), preferred_element_type=F32)


def _dot_tn(a, b):
    return lax.dot_general(a, b, (((0,), (0,)), ((), ())), preferred_element_type=F32)


class _Xfer:
    def __init__(self, start, wait):
        self.start, self.wait = start, wait


class _Hosted(NamedTuple):
    srcs: tuple
    out_shape: tuple
    n_sems: int
    plan: Callable
    aliases: tuple = ()


def _merge_hosted(parts):
    parts = [p for p in parts if p is not None]
    if len(parts) <= 1:
        return parts[0] if parts else None
    src_at, dst_at, sem_at, aliases = [0], [0], [0], []
    for p in parts:
        aliases += [(i + src_at[-1], j + dst_at[-1]) for i, j in p.aliases]
        src_at.append(src_at[-1] + len(p.srcs))
        dst_at.append(dst_at[-1] + len(p.out_shape))
        sem_at.append(sem_at[-1] + p.n_sems)

    def plan(src, dst, send_sems, recv_sems, local_sems, first_sem):
        copies = []
        for k, p in enumerate(parts):
            copies += p.plan(src[src_at[k]:src_at[k + 1]], dst[dst_at[k]:dst_at[k + 1]], send_sems, recv_sems,
                             local_sems, first_sem + sem_at[k])
        return copies

    return _Hosted(tuple(a for p in parts for a in p.srcs), tuple(s for p in parts for s in p.out_shape),
                   sem_at[-1], plan, tuple(aliases))


def _call(body, *, grid, in_specs, out_specs, out_shape, args, name, semantics, scratch_shapes=(), hosted=None):
    if hosted is None:
        outs = pl.pallas_call(body, grid=grid, in_specs=list(in_specs), out_specs=list(out_specs),
                              out_shape=list(out_shape), scratch_shapes=list(scratch_shapes),
                              compiler_params=_params(*semantics), name=name)(*args)
        return list(outs), []
    counts = (len(in_specs), len(hosted.srcs), len(out_specs), len(hosted.out_shape), len(scratch_shapes), 3)

    def wrapped(*refs):
        at, groups = 0, []
        for count in counts:
            groups.append(refs[at:at + count])
            at += count
        ins, srcs, outs, dsts, scratch, sems = groups
        copies = hosted.plan(srcs, dsts, *sems, 0)
        ids = [pl.program_id(axis) for axis in range(len(grid))]
        first = functools.reduce(jnp.logical_and, [i == 0 for i in ids])
        last = functools.reduce(jnp.logical_and, [i == g - 1 for i, g in zip(ids, grid)])

        @pl.when(first)
        def _():
            for cp in copies:
                cp.start()

        body(*ins, *outs, *scratch)

        @pl.when(last)
        def _():
            for cp in copies:
                cp.wait()

    any_spec = pl.BlockSpec(memory_space=pl.ANY)
    sems = [pltpu.SemaphoreType.DMA((hosted.n_sems,))] * 3
    outs = pl.pallas_call(
        wrapped, grid=grid, in_specs=list(in_specs) + [any_spec] * counts[1],
        out_specs=list(out_specs) + [any_spec] * counts[3], out_shape=list(out_shape) + list(hosted.out_shape),
        scratch_shapes=list(scratch_shapes) + sems, compiler_params=_params(*["arbitrary"] * len(grid)),
        input_output_aliases={counts[0] + i: counts[2] + j for i, j in hosted.aliases},
        name=name)(*args, *hosted.srcs)
    return list(outs[:counts[2]]), list(outs[counts[2]:])


def _dividing_tile(n, want):
    tile = min(want, n)
    while n % tile:
        tile -= LANES
    return tile


def _matmul(a, b, *, mode, tm, tn, out_dtypes, name, epilogue=None, extras=(), hosted=None, b_cols=None,
            row_vecs=(), n_row_sums=0, extra_col_blocks=None):
    a_parts = tuple(a) if isinstance(a, (tuple, list)) else (a,)
    b_parts = tuple(b) if isinstance(b, (tuple, list)) else (b,)
    assert len(a_parts) == len(b_parts) and (mode == "nt" or len(a_parts) == 1)
    n_parts = len(a_parts)
    m = a_parts[0].shape[0]
    if b_cols is None:
        b_cols = [(0, bp.shape[1]) for bp in b_parts]
    n = b_cols[0][1] if mode == "nn" else b_parts[0].shape[0]
    tm, tn = min(tm, m), _dividing_tile(n, tn)
    n_extra = len(extras) + len(row_vecs)
    n_tiles_out = len(out_dtypes)
    assert n_row_sums == 0 or n == tn

    def body(*refs):
        a_refs, b_refs = refs[:n_parts], refs[n_parts:2 * n_parts]
        rest = refs[2 * n_parts:]
        extra_refs, out_refs = rest[:n_extra], rest[n_extra:]
        if mode == "nn":
            acc = jnp.dot(a_refs[0][...], b_refs[0][...], preferred_element_type=F32)
        else:
            acc = _dot_nt(a_refs[0][...], b_refs[0][...])
            for a_ref, b_ref in zip(a_refs[1:], b_refs[1:]):
                acc = acc + _dot_nt(a_ref[...], b_ref[...])
        res = epilogue(acc, *[e[...] for e in extra_refs]) if epilogue is not None else (acc,)
        for o_ref, r in zip(out_refs[:n_tiles_out], res):
            o_ref[...] = r.astype(o_ref.dtype)
        if n_row_sums:
            @pl.when(pl.program_id(0) == 0)
            def _():
                for o_ref in out_refs[n_tiles_out:]:
                    o_ref[...] = jnp.zeros_like(o_ref)

            for o_ref, r in zip(out_refs[n_tiles_out:], res[n_tiles_out:]):
                o_ref[...] += r

    a_specs = [pl.BlockSpec((tm, ap.shape[1]), lambda i, j: (i, 0)) for ap in a_parts]
    if mode == "nn":
        assert b_cols[0][0] % tn == 0
        first = b_cols[0][0] // tn
        b_specs = [pl.BlockSpec((b_parts[0].shape[0], tn), lambda i, j: (0, first + j))]
    else:
        assert all(at % width == 0 for at, width in b_cols)
        b_specs = [pl.BlockSpec((tn, width), functools.partial(lambda i, j, blk: (j, blk), blk=at // width))
                   for at, width in b_cols]
    tile = pl.BlockSpec((tm, tn), lambda i, j: (i, j))
    row = pl.BlockSpec((1, tn), lambda i, j: (0, j))
    extra_specs = [pl.BlockSpec((tm, tn), functools.partial(lambda i, j, first: (i, first + j), first=first))
                   for first in (extra_col_blocks or [0] * len(extras))]
    outs, landed = _call(
        body,
        grid=(m // tm, n // tn),
        in_specs=a_specs + b_specs + extra_specs + [row] * len(row_vecs),
        out_specs=[tile] * n_tiles_out + [row] * n_row_sums,
        out_shape=[jax.ShapeDtypeStruct((m, n), dt) for dt in out_dtypes]
        + [jax.ShapeDtypeStruct((1, n), F32)] * n_row_sums,
        args=(*a_parts, *b_parts, *extras, *row_vecs), name=name,
        semantics=("arbitrary" if n_row_sums else "parallel", "arbitrary"), hosted=hosted)
    if hosted is not None:
        return (*outs, landed)
    return outs[0] if len(outs) == 1 else outs


def _matmul_tn(a, b, *, tk, tn, tt, name, slot_cols=None):
    t, k = a.shape
    n = b.shape[1]
    tk, tn, tt = min(tk, k), _dividing_tile(n, tn), min(tt, t)

    def body(a_ref, b_ref, o_ref):
        @pl.when(pl.program_id(2) == 0)
        def _():
            o_ref[...] = jnp.zeros_like(o_ref)

        if slot_cols is None:
            o_ref[...] += _dot_tn(a_ref[...], b_ref[...])
        else:
            av = a_ref[...]
            for s in range(tn // slot_cols):
                o_ref[s] += _dot_tn(av, b_ref[:, s * slot_cols:(s + 1) * slot_cols])

    if slot_cols is not None:
        out_spec = pl.BlockSpec((tn // slot_cols, tk, slot_cols), lambda i, j, s: (j, i, 0))
        out_shape = jax.ShapeDtypeStruct((n // slot_cols, k, slot_cols), F32)
    else:
        out_spec = pl.BlockSpec((tk, tn), lambda i, j, s: (i, j))
        out_shape = jax.ShapeDtypeStruct((k, n), F32)
    return pl.pallas_call(
        body,
        grid=(k // tk, n // tn, t // tt),
        in_specs=[pl.BlockSpec((tt, tk), lambda i, j, s: (s, i)), pl.BlockSpec((tt, tn), lambda i, j, s: (s, j))],
        out_specs=out_spec,
        out_shape=out_shape,
        compiler_params=_params("parallel", "parallel", "arbitrary"),
        name=name,
    )(a, b)


def _matmul_tn_multi(a, bs, *, tt, name, hosted=None):
    t, k = a.shape
    tt = min(tt, t)
    n_b = len(bs)

    def body(a_ref, *refs):
        b_refs, o_refs = refs[:n_b], refs[n_b:]

        @pl.when(pl.program_id(0) == 0)
        def _():
            for o_ref in o_refs:
                o_ref[...] = jnp.zeros_like(o_ref)

        a_t = a_ref[...].T
        for b_ref, o_ref in zip(b_refs, o_refs):
            o_ref[...] += jnp.dot(a_t, b_ref[...], preferred_element_type=F32)

    outs, landed = _call(
        body,
        grid=(t // tt,),
        in_specs=[pl.BlockSpec((tt, k), lambda s: (s, 0))] + [pl.BlockSpec((tt, b.shape[1]), lambda s: (s, 0)) for b in bs],
        out_specs=[pl.BlockSpec((k, b.shape[1]), lambda s: (0, 0)) for b in bs],
        out_shape=[jax.ShapeDtypeStruct((k, b.shape[1]), F32) for b in bs],
        args=(a, *bs), name=name, semantics=("arbitrary",), hosted=hosted)
    return (*outs, landed) if hosted is not None else outs


def _rmsnorm_rows(x, g):
    return x * lax.rsqrt(jnp.mean(x * x, axis=-1, keepdims=True) + NORM_EPS) * g


def _rmsnorm_fwd(x, g, *, name):
    t, d = x.shape
    tm = min(512, t)

    def body(x_ref, g_ref, o_ref):
        o_ref[...] = _rmsnorm_rows(x_ref[...], g_ref[...]).astype(BF16)

    return pl.pallas_call(
        body,
        grid=(t // tm,),
        in_specs=[pl.BlockSpec((tm, d), lambda i: (i, 0)), pl.BlockSpec((1, d), lambda i: (0, 0))],
        out_specs=pl.BlockSpec((tm, d), lambda i: (i, 0)),
        out_shape=jax.ShapeDtypeStruct((t, d), BF16),
        compiler_params=_params("parallel"),
        name=name,
    )(x, g)


def _rmsnorm_bwd_rows(dy, x, dres, g):
    r = lax.rsqrt(jnp.mean(x * x, axis=-1, keepdims=True) + NORM_EPS)
    xr = x * r
    gy = dy * g
    dx = dres + r * (gy - xr * jnp.mean(gy * xr, axis=-1, keepdims=True))
    return dx, jnp.sum(dy * xr, axis=0, keepdims=True)


def _softplus_neg(lam):
    z = -lam
    return jnp.maximum(z, 0.0) + jnp.log1p(jnp.exp(-jnp.abs(z)))


def _neg_expm1(y, exp_half_y):
    series = -y * (1.0 + y * 0.5 * (1.0 + y * (1.0 / 3.0) * (1.0 + y * 0.25 * (1.0 + y * 0.2))))
    return jnp.where(y > -0.0625, series, 1.0 - exp_half_y * exp_half_y)


def _gelu_parts(x):
    c = 0.7978845608028654
    u = c * (x + 0.044715 * x * x * x)
    th = jnp.tanh(u)
    gel = 0.5 * x * (1.0 + th)
    dgel = 0.5 * (1.0 + th) + 0.5 * x * (1.0 - th * th) * c * (1.0 + 3.0 * 0.044715 * x * x)
    return gel, dgel


def _shift_down(v, k, rows):
    return jnp.where(rows < k, 0.0, pltpu.roll(v, k, 0))


def _shift_up(v, k, rows, n):
    return jnp.where(rows >= n - k, 0.0, pltpu.roll(v, n - k, 0))


def _scan_within_groups(a, b, rows, *, reverse):
    n = a.shape[0]
    in_group = rows & (SUBLANES - 1)
    for s in (1, 2, 4):
        if reverse:
            inside, shift = in_group < SUBLANES - s, n - s
        else:
            inside, shift = in_group >= s, s
        b = b + a * jnp.where(inside, pltpu.roll(b, shift, 0), 0.0)
        a = a * jnp.where(inside, pltpu.roll(a, shift, 0), 1.0)
    return a, b


def _rnn_gates(xc, wrg, brg, wig, big, lam):
    xcb = xc.astype(BF16)
    r = _sig(jnp.dot(xcb, wrg, preferred_element_type=F32) + brg)
    i = _sig(jnp.dot(xcb, wig, preferred_element_type=F32) + big)
    sp = _softplus_neg(lam)
    log_a = -LRU_C * r * sp
    a = jnp.exp(log_a)
    mult = jnp.sqrt(_neg_expm1(2.0 * log_a, a))
    return xcb, r, i, sp, a, mult


def _conv_fwd(xv, cw, cb, rows):
    return (cb + _shift_down(xv, 3, rows) * cw[0:1, :] + _shift_down(xv, 2, rows) * cw[1:2, :]
            + _shift_down(xv, 1, rows) * cw[2:3, :] + xv * cw[3:4, :])


def _rnn_fwd(z, conv_w, conv_b, wrg_bd, b_rg, wig_bd, b_ig, lam, *, n_seq, seq, hosted=None):
    t = n_seq * seq
    ct = RNN_TILE
    n_ct = D_MODEL // ct

    def body(x_ref, g_ref, cw_ref, cb_ref, wrg_ref, brg_ref, wig_ref, big_ref, lam_ref,
             xc_ref, hr_ref, ya_ref, a_s, b_s):
        rows = lax.broadcasted_iota(jnp.int32, (seq, ct), 0)
        xc = _conv_fwd(x_ref[...], cw_ref[...], cb_ref[...], rows)
        _, r, i, sp, a, mult = _rnn_gates(xc, wrg_ref[...], brg_ref[...], wig_ref[...], big_ref[...], lam_ref[...])
        a_s[...], b_s[...] = _scan_within_groups(a, mult * (i * xc), rows, reverse=False)

        def step(j, carry):
            r0 = pl.multiple_of(j * SUBLANES, SUBLANES)
            h = b_s[pl.ds(r0, SUBLANES), :] + a_s[pl.ds(r0, SUBLANES), :] * carry
            hr_ref[pl.ds(r0, SUBLANES), :] = h
            return h[SUBLANES - 1:SUBLANES, :]

        lax.fori_loop(0, seq // SUBLANES, step, jnp.zeros((1, ct), F32), unroll=4)
        gel, _ = _gelu_parts(g_ref[...])
        xc_ref[...] = xc
        ya_ref[...] = (hr_ref[...] * gel).astype(BF16)

    vec = pl.BlockSpec((1, ct), lambda b, c: (0, c))
    gate_w = pl.BlockSpec((None, ct, ct), lambda b, c: (c, 0, 0))
    tile = pl.BlockSpec((seq, ct), lambda b, c: (b, c))
    outs, landed = _call(
        body,
        grid=(n_seq, n_ct),
        in_specs=[
            pl.BlockSpec((seq, ct), lambda b, c: (b, c)),
            pl.BlockSpec((seq, ct), lambda b, c: (b, n_ct + c)),
            pl.BlockSpec((CONV_W, ct), lambda b, c: (0, c)), vec, gate_w, vec, gate_w, vec, vec,
        ],
        out_specs=[tile, tile, tile],
        out_shape=[jax.ShapeDtypeStruct((t, D_MODEL), F32), jax.ShapeDtypeStruct((t, D_MODEL), F32),
                   jax.ShapeDtypeStruct((t, D_MODEL), BF16)],
        scratch_shapes=[pltpu.VMEM((seq, ct), F32), pltpu.VMEM((seq, ct), F32)],
        args=(z, z, conv_w, conv_b, wrg_bd, b_rg, wig_bd, b_ig, lam), name="rnn_fwd",
        semantics=("parallel", "parallel"), hosted=hosted)
    return (*outs, landed) if hosted is not None else outs


def _rnn_bwd(dya, z, xc, hr, conv_w, wrg_bd, b_rg, wig_bd, b_ig, lam, *, n_seq, seq, hosted=None):
    t = n_seq * seq
    ct = RNN_TILE
    n_ct = D_MODEL // ct

    def body(dya_ref, x_ref, g_ref, xc_ref, hr_ref, cw_ref, wrg_ref, brg_ref, wig_ref, big_ref, lam_ref,
             dx_ref, dg_ref, dwrg_ref, dwig_ref, vec_ref, a_s, d_s, g_s):
        rows = lax.broadcasted_iota(jnp.int32, (seq, ct), 0)
        xv, xc, hr, dyv = x_ref[...], xc_ref[...], hr_ref[...], dya_ref[...]
        lamv = lam_ref[...]
        gel, dgel = _gelu_parts(g_ref[...])
        dg_ref[...] = (dyv * hr * dgel).astype(BF16)
        xcb, r, i, sp, a, mult = _rnn_gates(xc, wrg_ref[...], brg_ref[...], wig_ref[...], big_ref[...], lamv)
        a_s[...], d_s[...] = _scan_within_groups(_shift_up(a, 1, rows, seq), dyv * gel, rows, reverse=True)

        def step(k, carry):
            r0 = pl.multiple_of((seq // SUBLANES - 1 - k) * SUBLANES, SUBLANES)
            gs = d_s[pl.ds(r0, SUBLANES), :] + a_s[pl.ds(r0, SUBLANES), :] * carry
            g_s[pl.ds(r0, SUBLANES), :] = gs
            return gs[0:1, :]

        lax.fori_loop(0, seq // SUBLANES, step, jnp.zeros((1, ct), F32), unroll=4)
        gsum = g_s[...]
        gated = i * xc
        d_log_a = gsum * _shift_down(hr, 1, rows) * a - gsum * gated * (a * a / mult)
        d_gated = gsum * mult
        d_pre_r = (d_log_a * (-LRU_C) * sp) * r * (1.0 - r)
        d_pre_i = (d_gated * xc) * i * (1.0 - i)
        dprb, dpib = d_pre_r.astype(BF16), d_pre_i.astype(BF16)
        dxc = d_gated * i + _dot_nt(dprb, wrg_ref[...]) + _dot_nt(dpib, wig_ref[...])
        cw = cw_ref[...]
        dx = (dxc * cw[3:4, :] + _shift_up(dxc, 1, rows, seq) * cw[2:3, :]
              + _shift_up(dxc, 2, rows, seq) * cw[1:2, :] + _shift_up(dxc, 3, rows, seq) * cw[0:1, :])
        dx_ref[...] = dx.astype(BF16)

        @pl.when(pl.program_id(1) == 0)
        def _():
            dwrg_ref[...] = jnp.zeros_like(dwrg_ref)
            dwig_ref[...] = jnp.zeros_like(dwig_ref)
            vec_ref[...] = jnp.zeros_like(vec_ref)

        dwrg_ref[...] += _dot_tn(xcb, dprb)
        dwig_ref[...] += _dot_tn(xcb, dpib)

        def colsum(v):
            return jnp.sum(v, axis=0, keepdims=True)

        d_sp = colsum(d_log_a * (-LRU_C) * r)
        vec_ref[0:1, :] += colsum(d_pre_r)
        vec_ref[1:2, :] += colsum(d_pre_i)
        vec_ref[2:3, :] += d_sp * (-_sig(-lamv))
        vec_ref[3:4, :] += colsum(dxc)
        vec_ref[4:5, :] += colsum(dxc * _shift_down(xv, 3, rows))
        vec_ref[5:6, :] += colsum(dxc * _shift_down(xv, 2, rows))
        vec_ref[6:7, :] += colsum(dxc * _shift_down(xv, 1, rows))
        vec_ref[7:8, :] += colsum(dxc * xv)

    vec = pl.BlockSpec((1, ct), lambda c, b: (0, c))
    gate_w = pl.BlockSpec((None, ct, ct), lambda c, b: (c, 0, 0))
    tile = pl.BlockSpec((seq, ct), lambda c, b: (b, c))
    outs, landed = _call(
        body,
        grid=(n_ct, n_seq),
        in_specs=[
            tile,
            pl.BlockSpec((seq, ct), lambda c, b: (b, c)),
            pl.BlockSpec((seq, ct), lambda c, b: (b, n_ct + c)),
            tile, tile,
            pl.BlockSpec((CONV_W, ct), lambda c, b: (0, c)), gate_w, vec, gate_w, vec, vec,
        ],
        out_specs=[tile, tile, gate_w, gate_w, pl.BlockSpec((8, ct), lambda c, b: (0, c))],
        out_shape=[jax.ShapeDtypeStruct((t, D_MODEL), BF16), jax.ShapeDtypeStruct((t, D_MODEL), BF16),
                   jax.ShapeDtypeStruct((n_ct, ct, ct), F32), jax.ShapeDtypeStruct((n_ct, ct, ct), F32),
                   jax.ShapeDtypeStruct((8, D_MODEL), F32)],
        scratch_shapes=[pltpu.VMEM((seq, ct), F32)] * 3,
        args=(dya, z, z, xc, hr, conv_w, wrg_bd, b_rg, wig_bd, b_ig, lam), name="rnn_bwd",
        semantics=("parallel", "arbitrary"), hosted=hosted)
    return (*outs, landed) if hosted is not None else outs


def _split_hi_lo(x):
    hi = x.astype(BF16)
    return hi, (x - hi.astype(F32)).astype(BF16)


def _dot_split(x, m_twice):
    hi, lo = _split_hi_lo(x)
    return jnp.dot(jnp.concatenate([hi, lo], axis=1), m_twice, preferred_element_type=F32)


def _head_matrices(width):
    ec = ((lax.broadcasted_iota(jnp.int32, (2 * width, LANES), 0) & (width - 1)) // HEAD_DIM
          == lax.broadcasted_iota(jnp.int32, (2 * width, LANES), 1))
    ee = (lax.broadcasted_iota(jnp.int32, (2 * LANES, width), 1) // HEAD_DIM
          == (lax.broadcasted_iota(jnp.int32, (2 * LANES, width), 0) & (LANES - 1)))
    return jnp.where(ec, 1.0, 0.0).astype(BF16), jnp.where(ee, 1.0, 0.0).astype(BF16)


def _swap_halves(y):
    w = y.shape[1]
    first = (lax.broadcasted_iota(jnp.int32, y.shape, 1) % HEAD_DIM) < HEAD_DIM // 2
    return jnp.where(first, pltpu.roll(y, w - HEAD_DIM // 2, 1), pltpu.roll(y, HEAD_DIM // 2, 1))


def _normrope_fwd(x, gain, cos_t, sin_t, ec, ee):
    w = x.shape[1]
    rs = _dot_split(lax.rsqrt(_dot_split(x * x, ec) * (1.0 / HEAD_DIM) + NORM_EPS), ee)
    nx = x * rs
    y = nx * gain
    reps = w // LANES
    out = y * jnp.tile(cos_t, (1, reps)) + _swap_halves(y) * jnp.tile(sin_t, (1, reps))
    return out, nx, rs


def _normrope_bwd(dout, nx, rs, gain, cos_t, sin_t, ec, ee):
    w = dout.shape[1]
    reps = w // LANES
    dy = dout * jnp.tile(cos_t, (1, reps)) + _swap_halves(dout * jnp.tile(sin_t, (1, reps)))
    dgain = jnp.sum(dy * nx, axis=0, keepdims=True)
    dn = dy * gain
    seg = _dot_split(_dot_split(dn * nx, ec) * (1.0 / HEAD_DIM), ee)
    return rs * (dn - nx * seg), dgain


def _pair_operand(t, group):
    chunk = t[:, (group // 2) * LANES:(group // 2 + 1) * LANES]
    low = lax.broadcasted_iota(jnp.int32, chunk.shape, 1) < HEAD_DIM
    rolled = pltpu.roll(chunk, HEAD_DIM, 1)
    return jnp.where(low, chunk, rolled) if group % 2 == 0 else jnp.where(low, rolled, chunk)


GROUP = N_Q_HEADS // N_KV_HEADS
GROUP_W = GROUP * HEAD_DIM


def _replicate_head(t, group):
    return jnp.tile(_pair_operand(t, group), (1, 2))


def _head_blocks(t):
    seg = lax.broadcasted_iota(jnp.int32, t.shape, 1) // HEAD_DIM
    return jnp.concatenate([jnp.where(seg == h, t, 0.0) for h in range(GROUP)], axis=0)


def _stack_heads(t_t, rows):
    return jnp.concatenate([t_t[:, h * rows:(h + 1) * rows] for h in range(GROUP)], axis=0)


def _head_rows(mat_t, group):
    return jnp.concatenate([mat_t[GROUP * group + h:GROUP * group + h + 1, :] for h in range(GROUP)], axis=1)


def _window_masks(blk):
    key = lax.broadcasted_iota(jnp.int32, (blk, GROUP * blk), 0)
    query = lax.broadcasted_iota(jnp.int32, (blk, GROUP * blk), 1) & (blk - 1)
    return key > query, key <= query


def _mask_window(t, before_ok, own_ok, fill):
    blk = t.shape[0] // 2
    return jnp.concatenate([jnp.where(before_ok, t[:blk], fill), jnp.where(own_ok, t[blk:], fill)], axis=0)


def _attn_fwd(z, cos_t, sin_t, q_gain_t, k_gain_t, sinks_t, *, n_seq, seq, hosted=None):
    t = n_seq * seq
    blk = WINDOW
    nb = seq // blk

    def body(q_ref, kp_ref, kc_ref, vp_ref, vc_ref, cosc_ref, sinc_ref, cosp_ref, sinp_ref, qg_ref, kg_ref, sk_ref,
             o_ref, l_ref):
        n = pl.program_id(1)
        ecq, eeq = _head_matrices(D_MODEL)
        eck, eek = _head_matrices(KV_W)
        cosc, sinc = cosc_ref[...], sinc_ref[...]
        qh, _, _ = _normrope_fwd(q_ref[...], qg_ref[...], cosc, sinc, ecq, eeq)
        qh = qh * (HEAD_DIM ** -0.5)
        kc, _, _ = _normrope_fwd(kc_ref[...], kg_ref[...], cosc, sinc, eck, eek)
        kp, _, _ = _normrope_fwd(kp_ref[...], kg_ref[...], cosp_ref[...], sinp_ref[...], eck, eek)
        kcat = jnp.concatenate([kp, kc], axis=0)
        vcat = jnp.concatenate([vp_ref[...], vc_ref[...]], axis=0)
        above, causal = _window_masks(blk)
        above = above & (n > 0)
        head_row = lax.broadcasted_iota(jnp.int32, (blk, blk), 0)
        sk_t = jnp.broadcast_to(sk_ref[...], (blk, LANES)).T
        vcat_t = vcat.T.astype(BF16)
        lmat = jnp.zeros((blk, blk), F32)
        groups = range(N_KV_HEADS)
        cols = [slice(g * GROUP_W, (g + 1) * GROUP_W) for g in groups]
        scores = [_dot_nt(_replicate_head(kcat, g).astype(BF16), _head_blocks(qh[:, cols[g]]).astype(BF16))
                  for g in groups]
        probs = []
        for g in groups:
            s = _mask_window(scores[g], above, causal, NEG_BIG)
            sink = _head_rows(sk_t, g)
            m = jnp.maximum(jnp.max(s, axis=0, keepdims=True), sink)
            e = jnp.exp(s - m)
            den = jnp.sum(e, axis=0, keepdims=True) + jnp.exp(sink - m)
            probs.append((e * (1.0 / den)).astype(BF16))
            lse = m + jnp.log(den)
            for h in range(GROUP):
                lmat = lmat + jnp.where(head_row == GROUP * g + h, lse[:, h * blk:(h + 1) * blk], 0.0)
        for g in groups:
            out_t = jnp.dot(vcat_t[g * HEAD_DIM:(g + 1) * HEAD_DIM], probs[g], preferred_element_type=F32)
            o_ref[:, cols[g]] = _stack_heads(out_t, blk).T.astype(BF16)
        l_ref[...] = lmat

    def row(b, n):
        return b * nb + n

    def prev(b, n):
        return b * nb + jnp.maximum(n - 1, 0)

    kw = KV_W
    tab_c = pl.BlockSpec((blk, LANES), lambda b, n: (n, 0))
    tab_p = pl.BlockSpec((blk, LANES), lambda b, n: (jnp.maximum(n - 1, 0), 0))
    outs, landed = _call(
        body,
        grid=(n_seq, nb),
        in_specs=[
            pl.BlockSpec((blk, D_MODEL), lambda b, n: (row(b, n), 0)),
            pl.BlockSpec((blk, kw), lambda b, n: (prev(b, n), ATTN_K_AT // kw)),
            pl.BlockSpec((blk, kw), lambda b, n: (row(b, n), ATTN_K_AT // kw)),
            pl.BlockSpec((blk, kw), lambda b, n: (prev(b, n), ATTN_V_AT // kw)),
            pl.BlockSpec((blk, kw), lambda b, n: (row(b, n), ATTN_V_AT // kw)),
            tab_c, tab_c, tab_p, tab_p,
            pl.BlockSpec((1, D_MODEL), lambda b, n: (0, 0)),
            pl.BlockSpec((1, kw), lambda b, n: (0, 0)),
            pl.BlockSpec((1, LANES), lambda b, n: (0, 0)),
        ],
        out_specs=[pl.BlockSpec((blk, D_MODEL), lambda b, n: (row(b, n), 0)),
                   pl.BlockSpec((blk, LANES), lambda b, n: (row(b, n), 0))],
        out_shape=[jax.ShapeDtypeStruct((t, D_MODEL), BF16), jax.ShapeDtypeStruct((t, LANES), F32)],
        args=(z, z, z, z, z, cos_t, sin_t, cos_t, sin_t, q_gain_t, k_gain_t, sinks_t), name="attn_fwd",
        semantics=("parallel", "parallel"), hosted=hosted)
    return (*outs, landed) if hosted is not None else outs


def _attn_bwd(z, o, lse, do, cos_t, sin_t, q_gain_t, k_gain_t, sinks_t, *, n_seq, seq, hosted=None):
    t = n_seq * seq
    blk = WINDOW
    nb = seq // blk
    kw = KV_W
    scale = HEAD_DIM ** -0.5

    def body(qc_ref, qn_ref, kc_ref, vp_ref, vc_ref, oc_ref, on_ref, doc_ref, don_ref, lc_ref, ln_ref,
             cosc_ref, sinc_ref, cosn_ref, sinn_ref, qg_ref, kg_ref, sk_ref,
             dz_ref, vec_ref, dq_s, q_s, k_s):
        n = pl.program_id(1)
        ecq, eeq = _head_matrices(D_MODEL)
        eck, eek = _head_matrices(KV_W)
        cosc, sinc = cosc_ref[...], sinc_ref[...]
        qg, kg = qg_ref[...], kg_ref[...]
        own, other = n & 1, 1 - (n & 1)

        @pl.when(n == 0)
        def _():
            for part, value in enumerate(_normrope_fwd(qc_ref[...], qg, cosc, sinc, ecq, eeq)):
                q_s[own, part] = value
            k_s[other] = jnp.zeros((blk, kw), F32)

        for part, value in enumerate(_normrope_fwd(qn_ref[...], qg, cosn_ref[...], sinn_ref[...], ecq, eeq)):
            q_s[other, part] = value
        qhc, nqc, rsqc = q_s[own, 0], q_s[own, 1], q_s[own, 2]
        qhn = q_s[other, 0]
        khc, nkc, rskc = _normrope_fwd(kc_ref[...], kg, cosc, sinc, eck, eek)
        khp = k_s[other]
        k_s[own] = khc
        doc = doc_ref[...].astype(F32)
        don = don_ref[...].astype(F32)
        delc = _dot_split(doc * oc_ref[...].astype(F32), ecq)
        deln = _dot_split(don * on_ref[...].astype(F32), ecq)
        lc_t, ln_t, delc_t, deln_t = lc_ref[...], ln_ref[...], delc.T, deln.T
        above, causal = _window_masks(blk)
        above_c, above_n = above & (n > 0), above & (n < nb - 1)
        seg = lax.broadcasted_iota(jnp.int32, (blk, GROUP_W), 1) // HEAD_DIM
        lane = lax.broadcasted_iota(jnp.int32, (1, LANES), 1)
        sk_t = jnp.broadcast_to(sk_ref[...], (blk, LANES)).T
        dsink = jnp.zeros((1, LANES), F32)
        kcat = jnp.concatenate([khp, khc], axis=0)
        vcat = jnp.concatenate([vp_ref[...], vc_ref[...]], axis=0)
        kcat_t = kcat.T.astype(BF16)
        dkh = jnp.zeros((blk, GROUP_W), F32)
        dvh = jnp.zeros((blk, GROUP_W), F32)

        def fold_to(group, t):
            total = t + pltpu.roll(t, HEAD_DIM, 1)
            total = total + pltpu.roll(total, 2 * HEAD_DIM, 1)
            return jnp.where(seg == group, total, 0.0)

        groups = range(N_KV_HEADS)
        cols = [slice(g * GROUP_W, (g + 1) * GROUP_W) for g in groups]
        qsc, qsn = qhc * scale, qhn * scale
        qb_c = [_head_blocks(qsc[:, cols[g]]).astype(BF16) for g in groups]
        qb_n = [_head_blocks(qsn[:, cols[g]]).astype(BF16) for g in groups]
        dob_c = [_head_blocks(doc[:, cols[g]]).astype(BF16) for g in groups]
        dob_n = [_head_blocks(don[:, cols[g]]).astype(BF16) for g in groups]
        raw = []
        for g in groups:
            krep = _replicate_head(kcat, g).astype(BF16)
            vrep = _replicate_head(vcat, g).astype(BF16)
            raw.append((_dot_nt(krep, qb_c[g]), _dot_nt(vrep, dob_c[g]),
                        _dot_nt(krep[blk:], qb_n[g]), _dot_nt(vrep[blk:], dob_n[g])))
        cooked = []
        for g in groups:
            s_c, dp_c, s_n, dp_n = raw[g]
            l_row, d_row = _head_rows(lc_t, g), _head_rows(delc_t, g)
            p_c = _mask_window(jnp.exp(s_c - l_row), above_c, causal, 0.0)
            ds_c = (p_c * (dp_c - d_row)).astype(BF16)
            p_n = jnp.where(above_n, jnp.exp(s_n - _head_rows(ln_t, g)), 0.0)
            ds_n = (p_n * (dp_n - _head_rows(deln_t, g))).astype(BF16)
            cooked.append((p_c[blk:].astype(BF16), ds_c, p_n.astype(BF16), ds_n))
            p_sink = jnp.exp(_head_rows(sk_t, g) - l_row) * d_row
            for h in range(GROUP):
                dsink = dsink + jnp.where(lane == GROUP * g + h,
                                          -jnp.sum(p_sink[:, h * blk:(h + 1) * blk], axis=1, keepdims=True), 0.0)
        for g in groups:
            p_cb, ds_c, p_nb, ds_n = cooked[g]
            dq_t = jnp.dot(kcat_t[g * HEAD_DIM:(g + 1) * HEAD_DIM], ds_c, preferred_element_type=F32)
            dq_s[:, cols[g]] = _stack_heads(dq_t, blk).T * scale
            dk_rep = (jnp.dot(ds_c[blk:], qb_c[g], preferred_element_type=F32)
                      + jnp.dot(ds_n, qb_n[g], preferred_element_type=F32))
            dv_rep = (jnp.dot(p_cb, dob_c[g], preferred_element_type=F32)
                      + jnp.dot(p_nb, dob_n[g], preferred_element_type=F32))
            dkh = dkh + fold_to(g, dk_rep)
            dvh = dvh + fold_to(g, dv_rep)
        dq, dqg = _normrope_bwd(dq_s[...], nqc, rsqc, qg, cosc, sinc, ecq, eeq)
        dk, dkg = _normrope_bwd(dkh, nkc, rskc, kg, cosc, sinc, eck, eek)
        dz_ref[:, :ATTN_K_AT] = dq.astype(BF16)
        dz_ref[:, ATTN_K_AT:ATTN_V_AT] = dk.astype(BF16)
        dz_ref[:, ATTN_V_AT:] = dvh.astype(BF16)

        @pl.when(n == 0)
        def _():
            vec_ref[...] = jnp.zeros_like(vec_ref)

        vec_ref[0:1, :] += dqg
        vec_ref[1:2, 0:kw] += dkg
        vec_ref[2:3, 0:LANES] += dsink

    def row(b, n):
        return b * nb + n

    def prev(b, n):
        return b * nb + jnp.maximum(n - 1, 0)

    def nxt(b, n):
        return b * nb + jnp.minimum(n + 1, nb - 1)

    def tiles(width, col, which):
        return pl.BlockSpec((blk, width), lambda b, n: (which(b, n), col))

    def table(which):
        return pl.BlockSpec((blk, LANES), lambda b, n: (which(0, n), 0))

    outs, landed = _call(
        body,
        grid=(n_seq, nb),
        in_specs=[
            tiles(D_MODEL, 0, row), tiles(D_MODEL, 0, nxt),
            tiles(kw, ATTN_K_AT // kw, row),
            tiles(kw, ATTN_V_AT // kw, prev), tiles(kw, ATTN_V_AT // kw, row),
            tiles(D_MODEL, 0, row), tiles(D_MODEL, 0, nxt),
            tiles(D_MODEL, 0, row), tiles(D_MODEL, 0, nxt),
            tiles(LANES, 0, row), tiles(LANES, 0, nxt),
            table(row), table(row), table(nxt), table(nxt),
            pl.BlockSpec((1, D_MODEL), lambda b, n: (0, 0)),
            pl.BlockSpec((1, kw), lambda b, n: (0, 0)),
            pl.BlockSpec((1, LANES), lambda b, n: (0, 0)),
        ],
        out_specs=[tiles(ATTN_W, 0, row), pl.BlockSpec((None, 8, D_MODEL), lambda b, n: (b, 0, 0))],
        out_shape=[jax.ShapeDtypeStruct((t, ATTN_W), BF16), jax.ShapeDtypeStruct((n_seq, 8, D_MODEL), F32)],
        scratch_shapes=[pltpu.VMEM((blk, D_MODEL), F32), pltpu.VMEM((2, 3, blk, D_MODEL), F32),
                        pltpu.VMEM((2, blk, kw), F32)],
        args=(z, z, z, z, z, o, o, do, do, lse, lse, cos_t, sin_t, cos_t, sin_t,
              q_gain_t, k_gain_t, sinks_t), name="attn_bwd", semantics=("arbitrary", "arbitrary"), hosted=hosted)
    return (*outs, landed) if hosted is not None else outs


MERGE_COLS = 512


def _merge_fwd(z, ya, yb):
    t = ya.shape[0]
    tm, tc = min(512, t), MERGE_COLS

    def body(ga_ref, gb_ref, ya_ref, yb_ref, o_ref):
        o_ref[...] = (_sig(ga_ref[...]) * ya_ref[...] + _sig(gb_ref[...]) * yb_ref[...]).astype(BF16)

    tile = pl.BlockSpec((tm, tc), lambda i, j: (i, j))
    return pl.pallas_call(
        body,
        grid=(t // tm, D_MODEL // tc),
        in_specs=[pl.BlockSpec((tm, tc), lambda i, j: (i, j)),
                  pl.BlockSpec((tm, tc), lambda i, j: (i, D_MODEL // tc + j)), tile, tile],
        out_specs=tile,
        out_shape=jax.ShapeDtypeStruct((t, D_MODEL), BF16),
        compiler_params=_params("parallel", "parallel"),
        name="merge_fwd",
    )(z, z, ya, yb)


def _rope_tables(seq):
    inv = ROPE_THETA ** (-jnp.arange(0, HEAD_DIM, 2, dtype=F32) / HEAD_DIM)
    ang = jnp.arange(seq, dtype=F32)[:, None] * inv[None, :]
    cos, sin = jnp.cos(ang), jnp.sin(ang)
    return jnp.tile(jnp.concatenate([cos, cos], axis=1), (1, 2)), jnp.tile(jnp.concatenate([-sin, sin], axis=1), (1, 2))


def _block_diag_tiles(w):
    per = RNN_TILE // RNN_BLOCK_W
    w4 = w.reshape(D_MODEL // RNN_TILE, per, RNN_BLOCK_W, RNN_BLOCK_W)
    eye = jnp.eye(per, dtype=w.dtype)
    dense = jnp.einsum("tpij,pq->tpiqj", w4, eye)
    return dense.reshape(D_MODEL // RNN_TILE, RNN_TILE, RNN_TILE).astype(BF16)


def _block_diag_extract(dense):
    per = RNN_TILE // RNN_BLOCK_W
    d5 = dense.reshape(D_MODEL // RNN_TILE, per, RNN_BLOCK_W, per, RNN_BLOCK_W)
    blocks = jnp.stack([d5[:, p, :, p, :] for p in range(per)], axis=1)
    return blocks.reshape(D_MODEL // RNN_BLOCK_W, RNN_BLOCK_W, RNN_BLOCK_W)


def _local_step(x, p, target, w, *, n_seq, seq, comm=None):
    w = dict(w)

    def run(tag, fn, *args, **kwargs):
        hosted = comm.host(tag) if comm is not None else None
        if hosted is None:
            return fn(*args, **kwargs)
        *outs, landed = fn(*args, hosted=hosted, **kwargs)
        comm.landed(tag, landed, w)
        return outs[0] if len(outs) == 1 else outs

    def ready(batch, grads, extra=None):
        if comm is not None:
            comm.ready(batch, grads, extra)

    cos_t, sin_t = _rope_tables(seq)
    q_gain_t = jnp.tile(w["q_gain"], (1, N_Q_HEADS))
    k_gain_t = jnp.tile(w["k_gain"], (1, N_KV_HEADS))
    sinks_t = jnp.pad(w["sinks"], ((0, 0), (0, LANES - N_Q_HEADS)))
    wrg_bd, wig_bd = _block_diag_tiles(w["w_rg"]), _block_diag_tiles(w["w_ig"])
    dims = dict(n_seq=n_seq, seq=seq)

    h = _rmsnorm_fwd(x, w["g_mix"], name="norm_mix")
    z_rnn = _matmul(h, w["w_in"], mode="nn", tm=1024, tn=1024, out_dtypes=[F32], name="mm_in_rnn",
                    b_cols=[(0, COL_RNN_END)])
    w_in_attn, w_in_gate = w["w_in"][:, COL_RNN_END:COL_ATTN_END], w["w_in"][:, COL_ATTN_END:]
    z_attn = _matmul(h, w_in_attn, mode="nn", tm=1024, tn=1024, out_dtypes=[F32], name="mm_in_attn")
    z_gate = _matmul(h, w_in_gate, mode="nn", tm=1024, tn=1024, out_dtypes=[F32], name="mm_in_gate")
    xc, hr, ya_in = run("rnn_fwd", _rnn_fwd, z_rnn, w["conv_w"], w["conv_b"], wrg_bd, w["b_rg"], wig_bd, w["b_ig"],
                        w["lru_lambda"], **dims)
    o, lse = run("attn_fwd", _attn_fwd, z_attn, cos_t, sin_t, q_gain_t, k_gain_t, sinks_t, **dims)
    ya = run("mm_rnn_proj", _matmul, ya_in, w["w_rnn_proj"], mode="nn", tm=1024, tn=1024, out_dtypes=[F32],
             name="mm_rnn_proj")
    yb = _matmul(o, w["w_attn_proj"], mode="nn", tm=1024, tn=1024, out_dtypes=[F32], name="mm_attn_proj")
    merged = _merge_fwd(z_gate, ya, yb)
    def residual_then_norm(acc, res, gain):
        new = res + acc
        return new, _rmsnorm_rows(new, gain)

    x1, hm = _matmul(merged, w["w_out"], mode="nn", tm=512, tn=1024, out_dtypes=[F32, BF16], name="mm_out",
                     epilogue=residual_then_norm, extras=(x,), row_vecs=(w["g_mlp"],))
    act = _matmul(hm, w["w_up"], mode="nn", tm=1024, tn=1024, out_dtypes=[BF16], name="mm_up",
                  epilogue=lambda acc: (jnp.square(jnp.maximum(acc, 0.0)),))
    x2, hp = _matmul(act, w["w_down"], mode="nn", tm=512, tn=1024, out_dtypes=[F32, BF16], name="mm_down",
                     epilogue=residual_then_norm, extras=(x1,), row_vecs=(w["g_ple"],))
    p_bf = p.astype(BF16)
    e = _matmul(p_bf, w["w_ple_proj"], mode="nn", tm=1024, tn=1024, out_dtypes=[F32], name="mm_ple_proj")

    def loss_head(gt, x2v, ev, tgt):
        sg = _sig(gt)
        diff = x2v + ev * sg - tgt
        dx = diff * (1.0 / D_MODEL)
        return dx, dx * ev * sg * (1.0 - sg), dx * sg, jnp.sum(diff * diff, axis=0, keepdims=True)

    dx3, dgt, de, loss_row = _matmul(hp, w["w_ple_gate"], mode="nn", tm=512, tn=1024, out_dtypes=[F32, BF16, BF16],
                                     name="mm_ple_gate", epilogue=loss_head, extras=(x2, e, target), n_row_sums=1)

    g = {}
    g["w_ple_proj"] = _matmul_tn(p_bf, de, tk=PLE_DIM, tn=1024, tt=1024, name="mm_d_ple_proj",
                                 slot_cols=D_MODEL // N_DEV)
    g["w_ple_gate"] = _matmul_tn(hp, dgt, tk=1024, tn=1024, tt=512, name="mm_d_ple_gate")
    def through_norm(dy, xv, dres, gain):
        dx, dgain = _rmsnorm_bwd_rows(dy, xv, dres, gain)
        return dx, dx, dgain

    dx2, dx2_bf, g["g_ple"] = _matmul(
        dgt, w["w_ple_gate"], mode="nt", tm=512, tn=1024, out_dtypes=[F32, BF16], name="mm_dhp",
        epilogue=through_norm, extras=(x2, dx3), row_vecs=(w["g_ple"],), n_row_sums=1)
    g["w_down"] = _matmul_tn(act, dx2_bf, tk=1024, tn=1024, tt=512, name="mm_d_down")
    du = _matmul(dx2_bf, w["w_down"], mode="nt", tm=1024, tn=1024, out_dtypes=[BF16], name="mm_dact",
                 epilogue=lambda acc, a: (acc * (2.0 * jnp.sqrt(a.astype(F32))),), extras=(act,))
    g["w_up"] = _matmul_tn(hm, du, tk=1024, tn=1024, tt=512, name="mm_d_up", slot_cols=D_FF // N_DEV)
    ready(1, g)
    dx1, dx1_bf, g["g_mlp"] = run(
        "mm_dhm", _matmul, du, w["w_up"], mode="nt", tm=512, tn=1024, out_dtypes=[F32, BF16], name="mm_dhm",
        epilogue=through_norm, extras=(x1, dx2), row_vecs=(w["g_mlp"],), n_row_sums=1)
    g["w_out"] = _matmul_tn(merged, dx1_bf, tk=1024, tn=1024, tt=512, name="mm_d_out")
    def merge_bwd(dm, ga, gb, yav, ybv):
        sa, sb = _sig(ga), _sig(gb)
        return dm * sa, dm * sb, dm * yav * sa * (1.0 - sa), dm * ybv * sb * (1.0 - sb)

    dya, dyb, dga, dgb = _matmul(dx1_bf, w["w_out"], mode="nt", tm=512, tn=1024, out_dtypes=[BF16] * 4,
                                 name="mm_dmerged", epilogue=merge_bwd, extras=(z_gate, z_gate, ya, yb),
                                 extra_col_blocks=(0, 1, 0, 0))
    g["w_rnn_proj"] = _matmul_tn(ya_in, dya, tk=1024, tn=1024, tt=512, name="mm_d_rnn_proj")
    g["w_attn_proj"] = _matmul_tn(o, dyb, tk=1024, tn=1024, tt=512, name="mm_d_attn_proj")
    ready(2, g)
    dya_in = run("mm_dya_in", _matmul, dya, w["w_rnn_proj"], mode="nt", tm=1024, tn=1024, out_dtypes=[F32],
                 name="mm_dya_in")
    do = _matmul(dyb, w["w_attn_proj"], mode="nt", tm=1024, tn=1024, out_dtypes=[BF16], name="mm_do")
    dx_rnn, dg_rnn, dwrg_dense, dwig_dense, rnn_vec = run(
        "rnn_bwd", _rnn_bwd, dya_in, z_rnn, xc, hr, w["conv_w"], wrg_bd, w["b_rg"], wig_bd, w["b_ig"],
        w["lru_lambda"], **dims)
    dz_attn, attn_vec = run("attn_bwd", _attn_bwd, z_attn, o, lse, do, cos_t, sin_t, q_gain_t, k_gain_t, sinks_t,
                            **dims)
    dz_parts = (dx_rnn, dg_rnn, dz_attn, dga, dgb)
    g["w_rg"] = _block_diag_extract(dwrg_dense)
    g["w_ig"] = _block_diag_extract(dwig_dense)
    g["b_rg"], g["b_ig"], g["lru_lambda"], g["conv_b"] = (rnn_vec[i:i + 1] for i in range(4))
    g["conv_w"] = rnn_vec[4:8]
    attn_vec = attn_vec[0] if n_seq == 1 else functools.reduce(jnp.add, [attn_vec[b] for b in range(n_seq)])
    g["q_gain"] = attn_vec[0].reshape(N_Q_HEADS, HEAD_DIM).sum(axis=0)[None, :]
    g["k_gain"] = attn_vec[1, :KV_W].reshape(N_KV_HEADS, HEAD_DIM).sum(axis=0)[None, :]
    g["sinks"] = attn_vec[2:3, :N_Q_HEADS]
    ready(SMALL_BATCH, g, {LOSS_ROW: loss_row})
    g["w_in"] = jnp.concatenate(
        list(run("mm_d_in_rnn", _matmul_tn_multi, h, dz_parts[:2], tt=512, name="mm_d_in_rnn"))
        + list(run("mm_d_in_rest", _matmul_tn_multi, h, dz_parts[2:], tt=512, name="mm_d_in_rest")), axis=1)
    ready(3, g)
    windows = ((w["w_in"], (0, D_MODEL)), (w["w_in"], (D_MODEL, D_MODEL)), (w_in_attn, (0, ATTN_W)),
               (w_in_gate, (0, D_MODEL)), (w_in_gate, (D_MODEL, D_MODEL)))
    grad_x, g["g_mix"] = run(
        "mm_dh", _matmul, dz_parts, [wd[0] for wd in windows], mode="nt", tm=256, tn=1024, out_dtypes=[F32],
        name="mm_dh", b_cols=[wd[1] for wd in windows], epilogue=_rmsnorm_bwd_rows, extras=(x, dx1),
        row_vecs=(w["g_mix"],), n_row_sums=1)
    return jnp.sum(loss_row), grad_x, g


MESH_ID = pl.DeviceIdType.MESH


def _coords(index):
    return (index >> 2) & 1, (index >> 1) & 1, index & 1


def _exchange(srcs, kinds, *, name):
    n = len(srcs)
    n_peer = N_DEV - 1

    def body(*refs):
        src, dst = refs[:n], refs[n:2 * n]
        send_sems, recv_sems, local_sems = refs[2 * n:]
        me = 4 * lax.axis_index("x") + 2 * lax.axis_index("y") + lax.axis_index("c")

        def remote(i, d):
            peer = (me + d) & (N_DEV - 1)
            piece = src[i] if kinds[i] == "gather" else src[i].at[peer]
            return pltpu.make_async_remote_copy(
                src_ref=piece, dst_ref=dst[i].at[me], send_sem=send_sems.at[i * n_peer + d - 1],
                recv_sem=recv_sems.at[i * n_peer + d - 1], device_id=_coords(peer), device_id_type=MESH_ID)

        def arrival(i, d):
            sender = (me - d) & (N_DEV - 1)
            piece = src[i] if kinds[i] == "gather" else src[i].at[sender]
            return pltpu.make_async_remote_copy(
                src_ref=piece, dst_ref=dst[i].at[sender], send_sem=send_sems.at[i * n_peer + d - 1],
                recv_sem=recv_sems.at[i * n_peer + d - 1], device_id=_coords(sender), device_id_type=MESH_ID)

        own = []
        for i in range(n):
            piece = src[i] if kinds[i] == "gather" else src[i].at[me]
            own.append(pltpu.make_async_copy(piece, dst[i].at[me], local_sems.at[i]))
            own[-1].start()
        sent = [remote(i, d) for d in range(1, N_DEV) for i in range(n)]
        for cp in sent:
            cp.start()
        for d in range(1, N_DEV):
            for i in range(n):
                arrival(i, d).wait_recv()
        for cp in sent:
            cp.wait_send()
        for cp in own:
            cp.wait()

    def out_of(s, kind):
        shape = s.shape if kind == "scatter" else (N_DEV,) + s.shape
        return jax.ShapeDtypeStruct(shape, s.dtype)

    any_spec = pl.BlockSpec(memory_space=pl.ANY)
    return pl.pallas_call(
        body,
        in_specs=[any_spec] * n,
        out_specs=[any_spec] * n,
        out_shape=[out_of(s, k) for s, k in zip(srcs, kinds)],
        scratch_shapes=[pltpu.SemaphoreType.DMA((n * n_peer,)), pltpu.SemaphoreType.DMA((n * n_peer,)),
                        pltpu.SemaphoreType.DMA((n,))],
        compiler_params=pltpu.CompilerParams(has_side_effects=True),
        name=name,
    )(*srcs)


def _remote(src, dst, send_sem, recv_sem, to):
    return pltpu.make_async_remote_copy(src_ref=src, dst_ref=dst, send_sem=send_sem, recv_sem=recv_sem,
                                        device_id=to, device_id_type=MESH_ID)


def _gather_two_level(shards, *, name):
    n = len(shards)
    per = N_DEV - 1

    def body(*refs):
        src, dst = refs[:n], refs[n:2 * n]
        send_sems, recv_sems, local_sems = refs[2 * n:]
        x, y, c = lax.axis_index("x"), lax.axis_index("y"), lax.axis_index("c")
        me, sibling = (x, y, c), (x, y, 1 - c)
        chips = [(1 - x, y), (x, 1 - y), (1 - x, 1 - y)]

        def slot(pos):
            return 4 * pos[0] + 2 * pos[1] + pos[2]

        def copy(i, k, block, to, from_shard=False):
            source = src[i] if from_shard else dst[i].at[slot(block)]
            return _remote(source, dst[i].at[slot(block)], send_sems.at[i * per + k], recv_sems.at[i * per + k], to)

        mine = [pltpu.make_async_copy(src[i], dst[i].at[slot(me)], local_sems.at[i]) for i in range(n)]
        for cp in mine:
            cp.start()
        first = []
        for i in range(n):
            first.append(copy(i, 0, me, sibling, from_shard=True))
            first += [copy(i, 1 + j, me, (*chip, c), from_shard=True) for j, chip in enumerate(chips)]
        for cp in first:
            cp.start()
        passed = []
        for i in range(n):
            for j, chip in enumerate(chips):
                copy(i, 1 + j, (*chip, c), me).wait_recv()
                passed.append(copy(i, 4 + j, (*chip, c), sibling))
                passed[-1].start()
        for i in range(n):
            copy(i, 0, sibling, me).wait_recv()
            for j, chip in enumerate(chips):
                copy(i, 4 + j, (*chip, 1 - c), me).wait_recv()
        for cp in first + passed:
            cp.wait_send()
        for cp in mine:
            cp.wait()

    any_spec = pl.BlockSpec(memory_space=pl.ANY)
    return pl.pallas_call(
        body,
        in_specs=[any_spec] * n,
        out_specs=[any_spec] * n,
        out_shape=[jax.ShapeDtypeStruct((N_DEV,) + s.shape, s.dtype) for s in shards],
        scratch_shapes=[pltpu.SemaphoreType.DMA((n * per,)), pltpu.SemaphoreType.DMA((n * per,)),
                        pltpu.SemaphoreType.DMA((n,))],
        name=name,
    )(*shards)


CHIPS = N_DEV // 2


def _other_chips(x, y):
    return [(x, 1 - y), (1 - x, y), (1 - x, 1 - y)]


def _hosted_gather_first(shards):
    n = len(shards)
    per = CHIPS

    def plan(src, dst, send_sems, recv_sems, local_sems, first_sem):
        x, y, c = lax.axis_index("x"), lax.axis_index("y"), lax.axis_index("c")
        peers = [(x, y, 1 - c)] + [(*chip, c) for chip in _other_chips(x, y)]
        copies = []
        for i in range(n):
            own = pltpu.make_async_copy(src[i], dst[i].at[4 * x + 2 * y + c], local_sems.at[first_sem + i])
            copies.append(_Xfer(own.start, own.wait))
        for j, peer in enumerate(peers):
            for i in range(n):
                k = first_sem + i * per + j
                out = _remote(src[i], dst[i].at[4 * x + 2 * y + c], send_sems.at[k], recv_sems.at[k], peer)
                arrival = _remote(src[i], dst[i].at[4 * peer[0] + 2 * peer[1] + peer[2]], send_sems.at[k],
                                  recv_sems.at[k], peer)

                def wait(out=out, arrival=arrival):
                    arrival.wait_recv()
                    out.wait_send()

                copies.append(_Xfer(out.start, wait))
        return copies

    out_shape = tuple(jax.ShapeDtypeStruct((N_DEV,) + s.shape, s.dtype) for s in shards)
    return _Hosted(tuple(shards), out_shape, n * per, plan)


def _hosted_gather_second(landed):
    n = len(landed)
    per = CHIPS - 1

    def plan(src, dst, send_sems, recv_sems, local_sems, first_sem):
        x, y, c = lax.axis_index("x"), lax.axis_index("y"), lax.axis_index("c")
        copies = []
        for j, chip in enumerate(_other_chips(x, y)):
            mine, theirs = 4 * chip[0] + 2 * chip[1] + c, 4 * chip[0] + 2 * chip[1] + 1 - c
            for i in range(n):
                k = first_sem + i * per + j
                out = _remote(src[i].at[mine], dst[i].at[mine], send_sems.at[k], recv_sems.at[k], (x, y, 1 - c))
                arrival = _remote(src[i].at[theirs], dst[i].at[theirs], send_sems.at[k], recv_sems.at[k],
                                  (x, y, 1 - c))

                def wait(out=out, arrival=arrival):
                    arrival.wait_recv()
                    out.wait_send()

                copies.append(_Xfer(out.start, wait))
        return copies

    out_shape = tuple(jax.ShapeDtypeStruct(a.shape, a.dtype) for a in landed)
    return _Hosted(tuple(landed), out_shape, n * per, plan, tuple((i, i) for i in range(n)))


def _hosted_sibling_swap(arrays, sliced):
    n_sems = sum(CHIPS if s else 1 for s in sliced)

    def plan(src, dst, send_sems, recv_sems, local_sems, first_sem):
        x, y, c = lax.axis_index("x"), lax.axis_index("y"), lax.axis_index("c")
        sibling = (x, y, 1 - c)
        copies, k = [], first_sem
        for i, is_sliced in enumerate(sliced):
            pieces = [(src[i].at[2 * s + 1 - c], dst[i].at[s]) for s in range(CHIPS)] if is_sliced else [(src[i], dst[i])]
            for source, target in pieces:
                cp = _remote(source, target, send_sems.at[k], recv_sems.at[k], sibling)
                copies.append(_Xfer(cp.start, cp.wait))
                k += 1
        return copies

    out_shape = tuple(jax.ShapeDtypeStruct((CHIPS,) + a.shape[1:] if s else a.shape, a.dtype)
                      for a, s in zip(arrays, sliced))
    return _Hosted(tuple(arrays), out_shape, n_sems, plan)


def _hosted_chip_exchange(arrays, sliced):
    n = len(arrays)
    per = CHIPS - 1

    def plan(src, dst, send_sems, recv_sems, local_sems, first_sem):
        x, y, c = lax.axis_index("x"), lax.axis_index("y"), lax.axis_index("c")
        chip = 2 * x + y
        copies = []
        for i in range(n):
            own = pltpu.make_async_copy(src[i].at[chip] if sliced[i] else src[i], dst[i].at[chip],
                                        local_sems.at[first_sem + i])
            copies.append(_Xfer(own.start, own.wait))
        for d in range(1, CHIPS):
            other = chip ^ d
            to = ((other >> 1) & 1, other & 1, c)
            for i in range(n):
                k = first_sem + i * per + d - 1
                source = src[i].at[other] if sliced[i] else src[i]
                out = _remote(source, dst[i].at[chip], send_sems.at[k], recv_sems.at[k], to)
                arrival = _remote(source, dst[i].at[other], send_sems.at[k], recv_sems.at[k], to)

                def wait(out=out, arrival=arrival):
                    arrival.wait_recv()
                    out.wait_send()

                copies.append(_Xfer(out.start, wait))
        return copies

    out_shape = tuple(jax.ShapeDtypeStruct(a.shape if s else (CHIPS,) + a.shape, a.dtype)
                      for a, s in zip(arrays, sliced))
    return _Hosted(tuple(arrays), out_shape, n * per, plan)


def _add_sibling(parts, received, core, *, name):
    _, r, cols = parts.shape
    tr = min(256, r)

    def body(core_ref, a_ref, b_ref, o_ref):
        o_ref[...] = (a_ref[...] + b_ref[...]).astype(BF16)

    grid_spec = pltpu.PrefetchScalarGridSpec(
        num_scalar_prefetch=1,
        grid=(CHIPS, r // tr),
        in_specs=[pl.BlockSpec((None, tr, cols), lambda k, i, core_ref: (2 * k + core_ref[0], i, 0)),
                  pl.BlockSpec((None, tr, cols), lambda k, i, core_ref: (k, i, 0))],
        out_specs=pl.BlockSpec((None, tr, cols), lambda k, i, core_ref: (k, i, 0)),
    )
    return pl.pallas_call(body, grid_spec=grid_spec, out_shape=jax.ShapeDtypeStruct((CHIPS, r, cols), BF16),
                          compiler_params=_params("parallel", "parallel"), name=name)(core, parts, received)


def _add_whole(a, b, *, name):
    def body(a_ref, b_ref, o_ref):
        o_ref[...] = a_ref[...] + b_ref[...]

    return pl.pallas_call(body, out_shape=jax.ShapeDtypeStruct(a.shape, F32), name=name)(a, b)


def _adamw(parts, w, m, v, *, name):
    r, c = w.shape
    n_parts = parts.shape[0]
    tr = min(256, r)
    c1 = 1.0 - ADAM_B1 ** ADAM_STEP
    c2 = 1.0 - ADAM_B2 ** ADAM_STEP

    def body(p_ref, w_ref, m_ref, v_ref, g_ref, d_ref, nm_ref, nv_ref):
        g = p_ref[0].astype(F32)
        for s in range(1, n_parts):
            g = g + p_ref[s].astype(F32)
        nm = ADAM_B1 * m_ref[...] + (1.0 - ADAM_B1) * g
        nv = ADAM_B2 * v_ref[...] + (1.0 - ADAM_B2) * (g * g)
        g_ref[...] = g
        nm_ref[...] = nm
        nv_ref[...] = nv
        d_ref[...] = -ADAM_LR * ((nm / c1) / (jnp.sqrt(nv / c2) + ADAM_EPS) + ADAM_WD * w_ref[...])

    tile = pl.BlockSpec((tr, c), lambda i: (i, 0))
    return pl.pallas_call(
        body,
        grid=(r // tr,),
        in_specs=[pl.BlockSpec((n_parts, tr, c), lambda i: (0, i, 0)), tile, tile, tile],
        out_specs=[tile] * 4,
        out_shape=[jax.ShapeDtypeStruct((r, c), F32)] * 4,
        compiler_params=_params("parallel"),
        name=name,
    )(parts, w, m, v)


BIG = ("w_in", "w_rnn_proj", "w_attn_proj", "w_out", "w_up", "w_down", "w_ple_gate", "w_ple_proj")
LOSS_ROW = "loss"
SMALL = (("conv_b", 1), ("b_rg", 1), ("b_ig", 1), ("lru_lambda", 1), ("g_mlp", 1), ("g_ple", 1),
         ("q_gain", 1), ("k_gain", 1), ("sinks", 1), (LOSS_ROW, 1), ("w_rg", 64), ("w_ig", 64))
SMALL_ROWS = 144
ROW_SHARDED = ("w_rnn_proj", "w_attn_proj", "w_out", "w_down", "w_ple_gate")
COL_SHARDED = ("w_in", "w_up", "w_ple_proj")
BATCHES = {1: ("w_ple_proj", "w_ple_gate", "w_down", "w_up"), 2: ("w_out", "w_rnn_proj", "w_attn_proj"),
           3: ("w_in", "conv_w")}
SMALL_BATCH = 4


def _pack_small(vals):
    rows = []
    for nm, nrow in SMALL:
        flat = vals[nm].reshape(-1).astype(F32)
        rows.append(jnp.pad(flat, (0, nrow * D_MODEL - flat.shape[0])).reshape(nrow, D_MODEL))
    used = sum(nrow for _, nrow in SMALL)
    rows.append(jnp.zeros((SMALL_ROWS - used, D_MODEL), F32))
    return jnp.concatenate(rows, axis=0)


def _unpack_small(packed, shapes):
    out, at = {}, 0
    for nm, nrow in SMALL:
        size = 1
        for s in shapes[nm]:
            size *= s
        out[nm] = packed[at:at + nrow].reshape(-1)[:size].reshape(shapes[nm])
        at += nrow
    return out


def _full_weight(name, landed):
    if name in COL_SHARDED:
        return landed.transpose(1, 0, 2).reshape(landed.shape[1], N_DEV * landed.shape[2])
    return landed.reshape(N_DEV * landed.shape[1], landed.shape[2])


def _owner_slots(name, grad):
    if name == "w_in":
        return grad.reshape(D_MODEL, N_DEV, IN_TOTAL // N_DEV).transpose(1, 0, 2)
    if name == "conv_w":
        return grad.reshape(CONV_W, N_DEV, D_MODEL // N_DEV).transpose(1, 0, 2)
    if name in COL_SHARDED:
        return grad
    return grad.reshape(N_DEV, grad.shape[0] // N_DEV, grad.shape[1])


class _StepExchanges:
    FIRST, SECOND = "first", "second"
    EARLY, MID, LATE = ("w_rnn_proj", "w_attn_proj", "w_out"), ("w_up",), ("w_down", "w_ple_gate", "w_ple_proj")
    GATHERS = {"rnn_fwd": ((FIRST, EARLY), (FIRST, MID)),
               "attn_fwd": ((SECOND, EARLY), (SECOND, MID), (FIRST, LATE)), "mm_rnn_proj": ((SECOND, LATE),)}
    SWAPS = {"mm_dhm": 1, "mm_dya_in": 2, "mm_d_in_rnn": SMALL_BATCH}
    CHIP_EXCHANGES = {"rnn_bwd": (1,), "attn_bwd": (2,), "mm_d_in_rest": (SMALL_BATCH,), "mm_dh": (3,)}

    def __init__(self, shards, core):
        self.shards = shards
        self.core = core
        self.parts, self.swapped, self.summed, self.half_gathered = {}, {}, {}, {}

    def ready(self, batch, grads, extra=None):
        if batch == SMALL_BATCH:
            self.parts[batch] = ([_pack_small({**grads, **extra})], [False])
            return
        arrays = [_owner_slots(nm, grads[nm]) for nm in BATCHES[batch]]
        self.parts[batch] = (arrays, [True] * len(arrays))
        if batch not in self.SWAPS.values():
            _, self.swapped[batch] = _call(
                lambda: None, grid=(1,), in_specs=[], out_specs=[], out_shape=[], args=(), name="swap_last",
                semantics=("arbitrary",), hosted=_hosted_sibling_swap(*self.parts[batch]))

    def host(self, tag):
        if tag in self.GATHERS:
            return _merge_hosted([
                _hosted_gather_first([self.shards[nm] for nm in group]) if half == self.FIRST
                else _hosted_gather_second([self.half_gathered[nm] for nm in group])
                for half, group in self.GATHERS[tag]])
        if tag in self.SWAPS:
            return _hosted_sibling_swap(*self.parts[self.SWAPS[tag]])
        if tag in self.CHIP_EXCHANGES:
            hosted = []
            for batch in self.CHIP_EXCHANGES[tag]:
                arrays, sliced = self.parts[batch]
                labels = BATCHES.get(batch, ("small",))
                sums = [_add_sibling(a, r, self.core, name="add_" + lb) if s else _add_whole(a, r, name="add_" + lb)
                        for a, r, s, lb in zip(arrays, self.swapped[batch], sliced, labels)]
                hosted.append(_hosted_chip_exchange(sums, sliced))
            return _merge_hosted(hosted)
        return None

    def landed(self, tag, landed, weights):
        if tag in self.GATHERS:
            names = [(half, nm) for half, group in self.GATHERS[tag] for nm in group]
            for (half, nm), buf in zip(names, landed):
                if half == self.FIRST:
                    self.half_gathered[nm] = buf
                else:
                    weights[nm] = _full_weight(nm, buf)
        elif tag in self.SWAPS:
            self.swapped[self.SWAPS[tag]] = landed
        else:
            at = 0
            for batch in self.CHIP_EXCHANGES[tag]:
                count = len(self.parts[batch][0])
                self.summed[batch] = landed[at:at + count]
                at += count


def kernel(x, p, g_mix, w_in, conv_w, conv_b, w_rg, b_rg, w_ig, b_ig, lru_lambda, w_rnn_proj, q_gain, k_gain, sinks, w_attn_proj, w_out, g_mlp, w_up, w_down, g_ple, w_ple_gate, w_ple_proj, loss_target, m_g_mix, m_w_in, m_conv_w, m_conv_b, m_w_rg, m_b_rg, m_w_ig, m_b_ig, m_lru_lambda, m_w_rnn_proj, m_q_gain, m_k_gain, m_sinks, m_w_attn_proj, m_w_out, m_g_mlp, m_w_up, m_w_down, m_g_ple, m_w_ple_gate, m_w_ple_proj, v_g_mix, v_w_in, v_conv_w, v_conv_b, v_w_rg, v_b_rg, v_w_ig, v_b_ig, v_lru_lambda, v_w_rnn_proj, v_q_gain, v_k_gain, v_sinks, v_w_attn_proj, v_w_out, v_g_mlp, v_w_up, v_w_down, v_g_ple, v_w_ple_gate, v_w_ple_proj):
    names = ("g_mix", "w_in", "conv_w", "conv_b", "w_rg", "b_rg", "w_ig", "b_ig", "lru_lambda", "w_rnn_proj",
             "q_gain", "k_gain", "sinks", "w_attn_proj", "w_out", "g_mlp", "w_up", "w_down", "g_ple",
             "w_ple_gate", "w_ple_proj")
    wts = dict(zip(names, (g_mix, w_in, conv_w, conv_b, w_rg, b_rg, w_ig, b_ig, lru_lambda, w_rnn_proj, q_gain,
                           k_gain, sinks, w_attn_proj, w_out, g_mlp, w_up, w_down, g_ple, w_ple_gate, w_ple_proj)))
    mom1 = dict(zip(names, (m_g_mix, m_w_in, m_conv_w, m_conv_b, m_w_rg, m_b_rg, m_w_ig, m_b_ig, m_lru_lambda,
                            m_w_rnn_proj, m_q_gain, m_k_gain, m_sinks, m_w_attn_proj, m_w_out, m_g_mlp, m_w_up,
                            m_w_down, m_g_ple, m_w_ple_gate, m_w_ple_proj)))
    mom2 = dict(zip(names, (v_g_mix, v_w_in, v_conv_w, v_conv_b, v_w_rg, v_b_rg, v_w_ig, v_b_ig, v_lru_lambda,
                            v_w_rnn_proj, v_q_gain, v_k_gain, v_sinks, v_w_attn_proj, v_w_out, v_g_mlp, v_w_up,
                            v_w_down, v_g_ple, v_w_ple_gate, v_w_ple_proj)))
    n_seq, seq, _ = x.shape
    core = lax.axis_index("c").astype(jnp.int32).reshape(1)

    shards = {nm: wts[nm][0].astype(BF16) for nm in BIG}
    w_in_all, conv_all = _gather_two_level([shards["w_in"], conv_w[0]], name="gather_w_in")
    w = {nm: wts[nm] for nm in names if nm not in BIG}
    w["w_rg"], w["w_ig"] = w_rg[0], w_ig[0]
    w["conv_w"] = conv_all.transpose(1, 0, 2).reshape(CONV_W, D_MODEL)
    w["w_in"] = _full_weight("w_in", w_in_all)
    comm = _StepExchanges(shards, core)
    loss_sum, grad_x, g = _local_step(
        x.reshape(n_seq * seq, D_MODEL), p.reshape(n_seq * seq, PLE_DIM), loss_target.reshape(n_seq * seq, D_MODEL),
        w, n_seq=n_seq, seq=seq, comm=comm)
    del loss_sum

    res = {}
    for batch, batch_names in BATCHES.items():
        for nm, summed in zip(batch_names, comm.summed[batch]):
            res[nm] = _adamw(summed, wts[nm][0], mom1[nm][0], mom2[nm][0], name="adamw_" + nm)
    g_mix_parts, = _exchange([g["g_mix"]], ["gather"], name="gather_g_mix")
    res["g_mix"] = [r[0] for r in _adamw(g_mix_parts, g_mix, m_g_mix, v_g_mix, name="adamw_g_mix")]
    small_names = [nm for nm, _ in SMALL if nm != LOSS_ROW]
    full_small = {}
    for src, key in ((wts, "w"), (mom1, "m"), (mom2, "v")):
        vals = {nm: src[nm][0] for nm in small_names}
        vals[LOSS_ROW] = jnp.zeros((1,), F32)
        full_small[key] = _pack_small(vals)
    small_res = _adamw(comm.summed[SMALL_BATCH][0],full_small["w"], full_small["m"], full_small["v"], name="adamw_small")
    shapes = {nm: wts[nm].shape[1:] for nm in small_names}
    shapes[LOSS_ROW] = (D_MODEL,)
    small_out = [_unpack_small(r, shapes) for r in small_res]
    for nm in small_names:
        res[nm] = [so[nm] for so in small_out]
    loss = jnp.sum(small_out[0][LOSS_ROW]) * (0.5 / D_MODEL)

    outs = [loss, grad_x.reshape(n_seq, seq, D_MODEL)]
    for k in range(4):
        outs.extend(res[nm][k][None] for nm in names)
    return tuple(outs)
```

```python
import functools
from typing import Callable, NamedTuple

import jax
import jax.numpy as jnp
from jax import lax
from jax.experimental import pallas as pl
from jax.experimental.pallas import tpu as pltpu

F32 = jnp.float32
BF16 = jnp.bfloat16

N_DEV = 8
D_MODEL = 1024
RNN_BLOCK_W = 64
CONV_W = 4
LRU_C = 8.0
HEAD_DIM = 64
N_Q_HEADS = 16
N_KV_HEADS = 4
KV_W = N_KV_HEADS * HEAD_DIM
WINDOW = 128
ROPE_THETA = 10000.0
D_FF = 4096
PLE_DIM = 256
NORM_EPS = 1e-6
IN_TOTAL = 5632
COL_RNN_END, COL_ATTN_END = 2048, 3584
ATTN_W = COL_ATTN_END - COL_RNN_END
ATTN_K_AT, ATTN_V_AT = 1024, 1280

ADAM_LR = 0.001
ADAM_B1 = 0.9
ADAM_B2 = 0.999
ADAM_EPS = 1e-08
ADAM_WD = 0.01
ADAM_STEP = 10

LANES = 128
SUBLANES = 8
RNN_TILE = 256
VMEM_LIMIT = 48 * 1024 * 1024
NEG_BIG = -1e30


def _params(*sem):
    return pltpu.CompilerParams(dimension_semantics=sem if sem else None, vmem_limit_bytes=VMEM_LIMIT)


def _sig(x):
    return 0.5 * jnp.tanh(0.5 * x) + 0.5


def _dot_nt(a, b):
    return lax.dot_general(a, b, (((1,), (1,)), ((), ())), preferred_element_type=F32)


def _dot_tn(a, b):
    return lax.dot_general(a, b, (((0,), (0,)), ((), ())), preferred_element_type=F32)


class _Xfer:
    def __init__(self, start, wait):
        self.start, self.wait = start, wait


class _Hosted(NamedTuple):
    srcs: tuple
    out_shape: tuple
    n_sems: int
    plan: Callable
    aliases: tuple = ()


def _merge_hosted(parts):
    parts = [p for p in parts if p is not None]
    if len(parts) <= 1:
        return parts[0] if parts else None
    src_at, dst_at, sem_at, aliases = [0], [0], [0], []
    for p in parts:
        aliases += [(i + src_at[-1], j + dst_at[-1]) for i, j in p.aliases]
        src_at.append(src_at[-1] + len(p.srcs))
        dst_at.append(dst_at[-1] + len(p.out_shape))
        sem_at.append(sem_at[-1] + p.n_sems)

    def plan(src, dst, send_sems, recv_sems, local_sems, first_sem):
        copies = []
        for k, p in enumerate(parts):
            copies += p.plan(src[src_at[k]:src_at[k + 1]], dst[dst_at[k]:dst_at[k + 1]], send_sems, recv_sems,
                             local_sems, first_sem + sem_at[k])
        return copies

    return _Hosted(tuple(a for p in parts for a in p.srcs), tuple(s for p in parts for s in p.out_shape),
                   sem_at[-1], plan, tuple(aliases))


def _call(body, *, grid, in_specs, out_specs, out_shape, args, name, semantics, scratch_shapes=(), hosted=None):
    if hosted is None:
        outs = pl.pallas_call(body, grid=grid, in_specs=list(in_specs), out_specs=list(out_specs),
                              out_shape=list(out_shape), scratch_shapes=list(scratch_shapes),
                              compiler_params=_params(*semantics), name=name)(*args)
        return list(outs), []
    counts = (len(in_specs), len(hosted.srcs), len(out_specs), len(hosted.out_shape), len(scratch_shapes), 3)

    def wrapped(*refs):
        at, groups = 0, []
        for count in counts:
            groups.append(refs[at:at + count])
            at += count
        ins, srcs, outs, dsts, scratch, sems = groups
        copies = hosted.plan(srcs, dsts, *sems, 0)
        ids = [pl.program_id(axis) for axis in range(len(grid))]
        first = functools.reduce(jnp.logical_and, [i == 0 for i in ids])
        last = functools.reduce(jnp.logical_and, [i == g - 1 for i, g in zip(ids, grid)])

        @pl.when(first)
        def _():
            for cp in copies:
                cp.start()

        body(*ins, *outs, *scratch)

        @pl.when(last)
        def _():
            for cp in copies:
                cp.wait()

    any_spec = pl.BlockSpec(memory_space=pl.ANY)
    sems = [pltpu.SemaphoreType.DMA((hosted.n_sems,))] * 3
    outs = pl.pallas_call(
        wrapped, grid=grid, in_specs=list(in_specs) + [any_spec] * counts[1],
        out_specs=list(out_specs) + [any_spec] * counts[3], out_shape=list(out_shape) + list(hosted.out_shape),
        scratch_shapes=list(scratch_shapes) + sems, compiler_params=_params(*["arbitrary"] * len(grid)),
        input_output_aliases={counts[0] + i: counts[2] + j for i, j in hosted.aliases},
        name=name)(*args, *hosted.srcs)
    return list(outs[:counts[2]]), list(outs[counts[2]:])


def _dividing_tile(n, want):
    tile = min(want, n)
    while n % tile:
        tile -= LANES
    return tile


def _matmul(a, b, *, mode, tm, tn, out_dtypes, name, epilogue=None, extras=(), hosted=None, b_cols=None,
            row_vecs=(), n_row_sums=0, extra_col_blocks=None):
    a_parts = tuple(a) if isinstance(a, (tuple, list)) else (a,)
    b_parts = tuple(b) if isinstance(b, (tuple, list)) else (b,)
    assert len(a_parts) == len(b_parts) and (mode == "nt" or len(a_parts) == 1)
    n_parts = len(a_parts)
    m = a_parts[0].shape[0]
    if b_cols is None:
        b_cols = [(0, bp.shape[1]) for bp in b_parts]
    n = b_cols[0][1] if mode == "nn" else b_parts[0].shape[0]
    tm, tn = min(tm, m), _dividing_tile(n, tn)
    n_extra = len(extras) + len(row_vecs)
    n_tiles_out = len(out_dtypes)
    assert n_row_sums == 0 or n == tn

    def body(*refs):
        a_refs, b_refs = refs[:n_parts], refs[n_parts:2 * n_parts]
        rest = refs[2 * n_parts:]
        extra_refs, out_refs = rest[:n_extra], rest[n_extra:]
        if mode == "nn":
            acc = jnp.dot(a_refs[0][...], b_refs[0][...], preferred_element_type=F32)
        else:
            acc = _dot_nt(a_refs[0][...], b_refs[0][...])
            for a_ref, b_ref in zip(a_refs[1:], b_refs[1:]):
                acc = acc + _dot_nt(a_ref[...], b_ref[...])
        res = epilogue(acc, *[e[...] for e in extra_refs]) if epilogue is not None else (acc,)
        for o_ref, r in zip(out_refs[:n_tiles_out], res):
            o_ref[...] = r.astype(o_ref.dtype)
        if n_row_sums:
            @pl.when(pl.program_id(0) == 0)
            def _():
                for o_ref in out_refs[n_tiles_out:]:
                    o_ref[...] = jnp.zeros_like(o_ref)

            for o_ref, r in zip(out_refs[n_tiles_out:], res[n_tiles_out:]):
                o_ref[...] += r

    a_specs = [pl.BlockSpec((tm, ap.shape[1]), lambda i, j: (i, 0)) for ap in a_parts]
    if mode == "nn":
        assert b_cols[0][0] % tn == 0
        first = b_cols[0][0] // tn
        b_specs = [pl.BlockSpec((b_parts[0].shape[0], tn), lambda i, j: (0, first + j))]
    else:
        assert all(at % width == 0 for at, width in b_cols)
        b_specs = [pl.BlockSpec((tn, width), functools.partial(lambda i, j, blk: (j, blk), blk=at // width))
                   for at, width in b_cols]
    tile = pl.BlockSpec((tm, tn), lambda i, j: (i, j))
    row = pl.BlockSpec((1, tn), lambda i, j: (0, j))
    extra_specs = [pl.BlockSpec((tm, tn), functools.partial(lambda i, j, first: (i, first + j), first=first))
                   for first in (extra_col_blocks or [0] * len(extras))]
    outs, landed = _call(
        body,
        grid=(m // tm, n // tn),
        in_specs=a_specs + b_specs + extra_specs + [row] * len(row_vecs),
        out_specs=[tile] * n_tiles_out + [row] * n_row_sums,
        out_shape=[jax.ShapeDtypeStruct((m, n), dt) for dt in out_dtypes]
        + [jax.ShapeDtypeStruct((1, n), F32)] * n_row_sums,
        args=(*a_parts, *b_parts, *extras, *row_vecs), name=name,
        semantics=("arbitrary" if n_row_sums else "parallel", "arbitrary"), hosted=hosted)
    if hosted is not None:
        return (*outs, landed)
    return outs[0] if len(outs) == 1 else outs


def _matmul_tn(a, b, *, tk, tn, tt, name, slot_cols=None):
    t, k = a.shape
    n = b.shape[1]
    tk, tn, tt = min(tk, k), _dividing_tile(n, tn), min(tt, t)

    def body(a_ref, b_ref, o_ref):
        @pl.when(pl.program_id(2) == 0)
        def _():
            o_ref[...] = jnp.zeros_like(o_ref)

        if slot_cols is None:
            o_ref[...] += _dot_tn(a_ref[...], b_ref[...])
        else:
            av = a_ref[...]
            for s in range(tn // slot_cols):
                o_ref[s] += _dot_tn(av, b_ref[:, s * slot_cols:(s + 1) * slot_cols])

    if slot_cols is not None:
        out_spec = pl.BlockSpec((tn // slot_cols, tk, slot_cols), lambda i, j, s: (j, i, 0))
        out_shape = jax.ShapeDtypeStruct((n // slot_cols, k, slot_cols), F32)
    else:
        out_spec = pl.BlockSpec((tk, tn), lambda i, j, s: (i, j))
        out_shape = jax.ShapeDtypeStruct((k, n), F32)
    return pl.pallas_call(
        body,
        grid=(k // tk, n // tn, t // tt),
        in_specs=[pl.BlockSpec((tt, tk), lambda i, j, s: (s, i)), pl.BlockSpec((tt, tn), lambda i, j, s: (s, j))],
        out_specs=out_spec,
        out_shape=out_shape,
        compiler_params=_params("parallel", "parallel", "arbitrary"),
        name=name,
    )(a, b)


def _matmul_tn_multi(a, bs, *, tt, name, hosted=None):
    t, k = a.shape
    tt = min(tt, t)
    n_b = len(bs)

    def body(a_ref, *refs):
        b_refs, o_refs = refs[:n_b], refs[n_b:]

        @pl.when(pl.program_id(0) == 0)
        def _():
            for o_ref in o_refs:
                o_ref[...] = jnp.zeros_like(o_ref)

        a_t = a_ref[...].T
        for b_ref, o_ref in zip(b_refs, o_refs):
            o_ref[...] += jnp.dot(a_t, b_ref[...], preferred_element_type=F32)

    outs, landed = _call(
        body,
        grid=(t // tt,),
        in_specs=[pl.BlockSpec((tt, k), lambda s: (s, 0))] + [pl.BlockSpec((tt, b.shape[1]), lambda s: (s, 0)) for b in bs],
        out_specs=[pl.BlockSpec((k, b.shape[1]), lambda s: (0, 0)) for b in bs],
        out_shape=[jax.ShapeDtypeStruct((k, b.shape[1]), F32) for b in bs],
        args=(a, *bs), name=name, semantics=("arbitrary",), hosted=hosted)
    return (*outs, landed) if hosted is not None else outs


def _rmsnorm_rows(x, g):
    return x * lax.rsqrt(jnp.mean(x * x, axis=-1, keepdims=True) + NORM_EPS) * g


def _rmsnorm_fwd(x, g, *, name):
    t, d = x.shape
    tm = min(512, t)

    def body(x_ref, g_ref, o_ref):
        o_ref[...] = _rmsnorm_rows(x_ref[...], g_ref[...]).astype(BF16)

    return pl.pallas_call(
        body,
        grid=(t // tm,),
        in_specs=[pl.BlockSpec((tm, d), lambda i: (i, 0)), pl.BlockSpec((1, d), lambda i: (0, 0))],
        out_specs=pl.BlockSpec((tm, d), lambda i: (i, 0)),
        out_shape=jax.ShapeDtypeStruct((t, d), BF16),
        compiler_params=_params("parallel"),
        name=name,
    )(x, g)


def _rmsnorm_bwd_rows(dy, x, dres, g):
    r = lax.rsqrt(jnp.mean(x * x, axis=-1, keepdims=True) + NORM_EPS)
    xr = x * r
    gy = dy * g
    dx = dres + r * (gy - xr * jnp.mean(gy * xr, axis=-1, keepdims=True))
    return dx, jnp.sum(dy * xr, axis=0, keepdims=True)


def _softplus_neg(lam):
    z = -lam
    return jnp.maximum(z, 0.0) + jnp.log1p(jnp.exp(-jnp.abs(z)))


def _neg_expm1(y, exp_half_y):
    series = -y * (1.0 + y * 0.5 * (1.0 + y * (1.0 / 3.0) * (1.0 + y * 0.25 * (1.0 + y * 0.2))))
    return jnp.where(y > -0.0625, series, 1.0 - exp_half_y * exp_half_y)


def _gelu_parts(x):
    c = 0.7978845608028654
    u = c * (x + 0.044715 * x * x * x)
    th = jnp.tanh(u)
    gel = 0.5 * x * (1.0 + th)
    dgel = 0.5 * (1.0 + th) + 0.5 * x * (1.0 - th * th) * c * (1.0 + 3.0 * 0.044715 * x * x)
    return gel, dgel


def _shift_down(v, k, rows):
    return jnp.where(rows < k, 0.0, pltpu.roll(v, k, 0))


def _shift_up(v, k, rows, n):
    return jnp.where(rows >= n - k, 0.0, pltpu.roll(v, n - k, 0))


def _scan_within_groups(a, b, *, reverse):
    shape = a.shape
    a = a.reshape(shape[0] // SUBLANES, SUBLANES, shape[1])
    b = b.reshape(a.shape)
    in_group = lax.broadcasted_iota(jnp.int32, a.shape, 1)
    for s in (1, 2, 4):
        if reverse:
            inside, shift = in_group < SUBLANES - s, SUBLANES - s
        else:
            inside, shift = in_group >= s, s
        b = b + a * jnp.where(inside, pltpu.roll(b, shift, 1), 0.0)
        a = a * jnp.where(inside, pltpu.roll(a, shift, 1), 1.0)
    return a.reshape(shape), b.reshape(shape)


def _rnn_gates(xc, wrg, brg, wig, big, lam):
    xcb = xc.astype(BF16)
    r = _sig(jnp.dot(xcb, wrg, preferred_element_type=F32) + brg)
    i = _sig(jnp.dot(xcb, wig, preferred_element_type=F32) + big)
    sp = _softplus_neg(lam)
    log_a = -LRU_C * r * sp
    a = jnp.exp(log_a)
    mult = jnp.sqrt(_neg_expm1(2.0 * log_a, a))
    return xcb, r, i, sp, a, mult


def _conv_fwd(xv, cw, cb, rows):
    return (cb + _shift_down(xv, 3, rows) * cw[0:1, :] + _shift_down(xv, 2, rows) * cw[1:2, :]
            + _shift_down(xv, 1, rows) * cw[2:3, :] + xv * cw[3:4, :])


def _rnn_fwd(z, conv_w, conv_b, wrg_bd, b_rg, wig_bd, b_ig, lam, *, n_seq, seq, hosted=None):
    t = n_seq * seq
    ct = RNN_TILE
    n_ct = D_MODEL // ct

    def body(x_ref, g_ref, cw_ref, cb_ref, wrg_ref, brg_ref, wig_ref, big_ref, lam_ref,
             xc_ref, hr_ref, ya_ref, a_s, b_s):
        rows = lax.broadcasted_iota(jnp.int32, (seq, ct), 0)
        xc = _conv_fwd(x_ref[...], cw_ref[...], cb_ref[...], rows)
        _, r, i, sp, a, mult = _rnn_gates(xc, wrg_ref[...], brg_ref[...], wig_ref[...], big_ref[...], lam_ref[...])
        a_s[...], b_s[...] = _scan_within_groups(a, mult * (i * xc), reverse=False)

        def step(j, carry):
            r0 = pl.multiple_of(j * SUBLANES, SUBLANES)
            h = b_s[pl.ds(r0, SUBLANES), :] + a_s[pl.ds(r0, SUBLANES), :] * carry
            hr_ref[pl.ds(r0, SUBLANES), :] = h
            return h[SUBLANES - 1:SUBLANES, :]

        lax.fori_loop(0, seq // SUBLANES, step, jnp.zeros((1, ct), F32), unroll=4)
        gel, _ = _gelu_parts(g_ref[...])
        xc_ref[...] = xc
        ya_ref[...] = (hr_ref[...] * gel).astype(BF16)

    vec = pl.BlockSpec((1, ct), lambda b, c: (0, c))
    gate_w = pl.BlockSpec((None, ct, ct), lambda b, c: (c, 0, 0))
    tile = pl.BlockSpec((seq, ct), lambda b, c: (b, c))
    outs, landed = _call(
        body,
        grid=(n_seq, n_ct),
        in_specs=[
            pl.BlockSpec((seq, ct), lambda b, c: (b, c)),
            pl.BlockSpec((seq, ct), lambda b, c: (b, n_ct + c)),
            pl.BlockSpec((CONV_W, ct), lambda b, c: (0, c)), vec, gate_w, vec, gate_w, vec, vec,
        ],
        out_specs=[tile, tile, tile],
        out_shape=[jax.ShapeDtypeStruct((t, D_MODEL), F32), jax.ShapeDtypeStruct((t, D_MODEL), F32),
                   jax.ShapeDtypeStruct((t, D_MODEL), BF16)],
        scratch_shapes=[pltpu.VMEM((seq, ct), F32), pltpu.VMEM((seq, ct), F32)],
        args=(z, z, conv_w, conv_b, wrg_bd, b_rg, wig_bd, b_ig, lam), name="rnn_fwd",
        semantics=("parallel", "parallel"), hosted=hosted)
    return (*outs, landed) if hosted is not None else outs


def _rnn_bwd(dya, z, xc, hr, conv_w, wrg_bd, b_rg, wig_bd, b_ig, lam, *, n_seq, seq, hosted=None):
    t = n_seq * seq
    ct = RNN_TILE
    n_ct = D_MODEL // ct

    def body(dya_ref, x_ref, g_ref, xc_ref, hr_ref, cw_ref, wrg_ref, brg_ref, wig_ref, big_ref, lam_ref,
             dx_ref, dg_ref, dwrg_ref, dwig_ref, vec_ref, a_s, d_s, g_s):
        rows = lax.broadcasted_iota(jnp.int32, (seq, ct), 0)
        xv, xc, hr, dyv = x_ref[...], xc_ref[...], hr_ref[...], dya_ref[...]
        lamv = lam_ref[...]
        gel, dgel = _gelu_parts(g_ref[...])
        dg_ref[...] = (dyv * hr * dgel).astype(BF16)
        xcb, r, i, sp, a, mult = _rnn_gates(xc, wrg_ref[...], brg_ref[...], wig_ref[...], big_ref[...], lamv)
        a_s[...], d_s[...] = _scan_within_groups(_shift_up(a, 1, rows, seq), dyv * gel, reverse=True)

        def step(k, carry):
            r0 = pl.multiple_of((seq // SUBLANES - 1 - k) * SUBLANES, SUBLANES)
            gs = d_s[pl.ds(r0, SUBLANES), :] + a_s[pl.ds(r0, SUBLANES), :] * carry
            g_s[pl.ds(r0, SUBLANES), :] = gs
            return gs[0:1, :]

        lax.fori_loop(0, seq // SUBLANES, step, jnp.zeros((1, ct), F32), unroll=4)
        gsum = g_s[...]
        gated = i * xc
        d_log_a = gsum * _shift_down(hr, 1, rows) * a - gsum * gated * (a * a / mult)
        d_gated = gsum * mult
        d_pre_r = (d_log_a * (-LRU_C) * sp) * r * (1.0 - r)
        d_pre_i = (d_gated * xc) * i * (1.0 - i)
        dprb, dpib = d_pre_r.astype(BF16), d_pre_i.astype(BF16)
        dxc = d_gated * i + _dot_nt(dprb, wrg_ref[...]) + _dot_nt(dpib, wig_ref[...])
        cw = cw_ref[...]
        dx = (dxc * cw[3:4, :] + _shift_up(dxc, 1, rows, seq) * cw[2:3, :]
              + _shift_up(dxc, 2, rows, seq) * cw[1:2, :] + _shift_up(dxc, 3, rows, seq) * cw[0:1, :])
        dx_ref[...] = dx.astype(BF16)

        @pl.when(pl.program_id(1) == 0)
        def _():
            dwrg_ref[...] = jnp.zeros_like(dwrg_ref)
            dwig_ref[...] = jnp.zeros_like(dwig_ref)
            vec_ref[...] = jnp.zeros_like(vec_ref)

        dwrg_ref[...] += _dot_tn(xcb, dprb)
        dwig_ref[...] += _dot_tn(xcb, dpib)

        def colsum(v):
            return jnp.sum(v, axis=0, keepdims=True)

        d_sp = colsum(d_log_a * (-LRU_C) * r)
        vec_ref[0:1, :] += colsum(d_pre_r)
        vec_ref[1:2, :] += colsum(d_pre_i)
        vec_ref[2:3, :] += d_sp * (-_sig(-lamv))
        vec_ref[3:4, :] += colsum(dxc)
        vec_ref[4:5, :] += colsum(dxc * _shift_down(xv, 3, rows))
        vec_ref[5:6, :] += colsum(dxc * _shift_down(xv, 2, rows))
        vec_ref[6:7, :] += colsum(dxc * _shift_down(xv, 1, rows))
        vec_ref[7:8, :] += colsum(dxc * xv)

    vec = pl.BlockSpec((1, ct), lambda c, b: (0, c))
    gate_w = pl.BlockSpec((None, ct, ct), lambda c, b: (c, 0, 0))
    tile = pl.BlockSpec((seq, ct), lambda c, b: (b, c))
    outs, landed = _call(
        body,
        grid=(n_ct, n_seq),
        in_specs=[
            tile,
            pl.BlockSpec((seq, ct), lambda c, b: (b, c)),
            pl.BlockSpec((seq, ct), lambda c, b: (b, n_ct + c)),
            tile, tile,
            pl.BlockSpec((CONV_W, ct), lambda c, b: (0, c)), gate_w, vec, gate_w, vec, vec,
        ],
        out_specs=[tile, tile, gate_w, gate_w, pl.BlockSpec((8, ct), lambda c, b: (0, c))],
        out_shape=[jax.ShapeDtypeStruct((t, D_MODEL), BF16), jax.ShapeDtypeStruct((t, D_MODEL), BF16),
                   jax.ShapeDtypeStruct((n_ct, ct, ct), F32), jax.ShapeDtypeStruct((n_ct, ct, ct), F32),
                   jax.ShapeDtypeStruct((8, D_MODEL), F32)],
        scratch_shapes=[pltpu.VMEM((seq, ct), F32)] * 3,
        args=(dya, z, z, xc, hr, conv_w, wrg_bd, b_rg, wig_bd, b_ig, lam), name="rnn_bwd",
        semantics=("parallel", "arbitrary"), hosted=hosted)
    return (*outs, landed) if hosted is not None else outs


def _split_hi_lo(x):
    hi = x.astype(BF16)
    return hi, (x - hi.astype(F32)).astype(BF16)


def _dot_split(x, m_twice):
    hi, lo = _split_hi_lo(x)
    return jnp.dot(jnp.concatenate([hi, lo], axis=1), m_twice, preferred_element_type=F32)


def _head_matrices(width):
    ec = ((lax.broadcasted_iota(jnp.int32, (2 * width, LANES), 0) & (width - 1)) // HEAD_DIM
          == lax.broadcasted_iota(jnp.int32, (2 * width, LANES), 1))
    ee = (lax.broadcasted_iota(jnp.int32, (2 * LANES, width), 1) // HEAD_DIM
          == (lax.broadcasted_iota(jnp.int32, (2 * LANES, width), 0) & (LANES - 1)))
    return jnp.where(ec, 1.0, 0.0).astype(BF16), jnp.where(ee, 1.0, 0.0).astype(BF16)


def _swap_halves(y):
    w = y.shape[1]
    first = (lax.broadcasted_iota(jnp.int32, y.shape, 1) % HEAD_DIM) < HEAD_DIM // 2
    return jnp.where(first, pltpu.roll(y, w - HEAD_DIM // 2, 1), pltpu.roll(y, HEAD_DIM // 2, 1))


def _normrope_fwd(x, gain, cos_t, sin_t, ec, ee):
    w = x.shape[1]
    rs = _dot_split(lax.rsqrt(_dot_split(x * x, ec) * (1.0 / HEAD_DIM) + NORM_EPS), ee)
    nx = x * rs
    y = nx * gain
    reps = w // LANES
    out = y * jnp.tile(cos_t, (1, reps)) + _swap_halves(y) * jnp.tile(sin_t, (1, reps))
    return out, nx, rs


def _normrope_bwd(dout, nx, rs, gain, cos_t, sin_t, ec, ee):
    w = dout.shape[1]
    reps = w // LANES
    dy = dout * jnp.tile(cos_t, (1, reps)) + _swap_halves(dout * jnp.tile(sin_t, (1, reps)))
    dgain = jnp.sum(dy * nx, axis=0, keepdims=True)
    dn = dy * gain
    seg = _dot_split(_dot_split(dn * nx, ec) * (1.0 / HEAD_DIM), ee)
    return rs * (dn - nx * seg), dgain


def _pair_operand(t, group):
    chunk = t[:, (group // 2) * LANES:(group // 2 + 1) * LANES]
    low = lax.broadcasted_iota(jnp.int32, chunk.shape, 1) < HEAD_DIM
    rolled = pltpu.roll(chunk, HEAD_DIM, 1)
    return jnp.where(low, chunk, rolled) if group % 2 == 0 else jnp.where(low, rolled, chunk)


GROUP = N_Q_HEADS // N_KV_HEADS
GROUP_W = GROUP * HEAD_DIM


def _replicate_head(t, group):
    return jnp.tile(_pair_operand(t, group), (1, 2))


def _head_blocks(t):
    seg = lax.broadcasted_iota(jnp.int32, t.shape, 1) // HEAD_DIM
    return jnp.concatenate([jnp.where(seg == h, t, 0.0) for h in range(GROUP)], axis=0)


def _stack_heads(t_t, rows):
    return jnp.concatenate([t_t[:, h * rows:(h + 1) * rows] for h in range(GROUP)], axis=0)


def _head_rows(mat_t, group):
    return jnp.concatenate([mat_t[GROUP * group + h:GROUP * group + h + 1, :] for h in range(GROUP)], axis=1)


def _window_masks(blk):
    key = lax.broadcasted_iota(jnp.int32, (blk, GROUP * blk), 0)
    query = lax.broadcasted_iota(jnp.int32, (blk, GROUP * blk), 1) & (blk - 1)
    return key > query, key <= query


def _mask_window(t, before_ok, own_ok, fill):
    blk = t.shape[0] // 2
    return jnp.concatenate([jnp.where(before_ok, t[:blk], fill), jnp.where(own_ok, t[blk:], fill)], axis=0)


def _attn_fwd(z, cos_t, sin_t, q_gain_t, k_gain_t, sinks_t, *, n_seq, seq, hosted=None):
    t = n_seq * seq
    blk = WINDOW
    nb = seq // blk

    def body(q_ref, kp_ref, kc_ref, vp_ref, vc_ref, cosc_ref, sinc_ref, cosp_ref, sinp_ref, qg_ref, kg_ref, sk_ref,
             o_ref, l_ref):
        n = pl.program_id(1)
        ecq, eeq = _head_matrices(D_MODEL)
        eck, eek = _head_matrices(KV_W)
        cosc, sinc = cosc_ref[...], sinc_ref[...]
        qh, _, _ = _normrope_fwd(q_ref[...], qg_ref[...], cosc, sinc, ecq, eeq)
        qh = qh * (HEAD_DIM ** -0.5)
        kc, _, _ = _normrope_fwd(kc_ref[...], kg_ref[...], cosc, sinc, eck, eek)
        kp, _, _ = _normrope_fwd(kp_ref[...], kg_ref[...], cosp_ref[...], sinp_ref[...], eck, eek)
        kcat = jnp.concatenate([kp, kc], axis=0)
        vcat = jnp.concatenate([vp_ref[...], vc_ref[...]], axis=0)
        above, causal = _window_masks(blk)
        above = above & (n > 0)
        head_row = lax.broadcasted_iota(jnp.int32, (blk, blk), 0)
        sk_t = jnp.broadcast_to(sk_ref[...], (blk, LANES)).T
        vcat_t = vcat.T.astype(BF16)
        lmat = jnp.zeros((blk, blk), F32)
        groups = range(N_KV_HEADS)
        cols = [slice(g * GROUP_W, (g + 1) * GROUP_W) for g in groups]
        scores = [_dot_nt(_replicate_head(kcat, g).astype(BF16), _head_blocks(qh[:, cols[g]]).astype(BF16))
                  for g in groups]
        probs = []
        for g in groups:
            s = _mask_window(scores[g], above, causal, NEG_BIG)
            sink = _head_rows(sk_t, g)
            m = jnp.maximum(jnp.max(s, axis=0, keepdims=True), sink)
            e = jnp.exp(s - m)
            den = jnp.sum(e, axis=0, keepdims=True) + jnp.exp(sink - m)
            probs.append((e * (1.0 / den)).astype(BF16))
            lse = m + jnp.log(den)
            for h in range(GROUP):
                lmat = lmat + jnp.where(head_row == GROUP * g + h, lse[:, h * blk:(h + 1) * blk], 0.0)
        for g in groups:
            out_t = jnp.dot(vcat_t[g * HEAD_DIM:(g + 1) * HEAD_DIM], probs[g], preferred_element_type=F32)
            o_ref[:, cols[g]] = _stack_heads(out_t, blk).T.astype(BF16)
        l_ref[...] = lmat

    def row(b, n):
        return b * nb + n

    def prev(b, n):
        return b * nb + jnp.maximum(n - 1, 0)

    kw = KV_W
    tab_c = pl.BlockSpec((blk, LANES), lambda b, n: (n, 0))
    tab_p = pl.BlockSpec((blk, LANES), lambda b, n: (jnp.maximum(n - 1, 0), 0))
    outs, landed = _call(
        body,
        grid=(n_seq, nb),
        in_specs=[
            pl.BlockSpec((blk, D_MODEL), lambda b, n: (row(b, n), 0)),
            pl.BlockSpec((blk, kw), lambda b, n: (prev(b, n), ATTN_K_AT // kw)),
            pl.BlockSpec((blk, kw), lambda b, n: (row(b, n), ATTN_K_AT // kw)),
            pl.BlockSpec((blk, kw), lambda b, n: (prev(b, n), ATTN_V_AT // kw)),
            pl.BlockSpec((blk, kw), lambda b, n: (row(b, n), ATTN_V_AT // kw)),
            tab_c, tab_c, tab_p, tab_p,
            pl.BlockSpec((1, D_MODEL), lambda b, n: (0, 0)),
            pl.BlockSpec((1, kw), lambda b, n: (0, 0)),
            pl.BlockSpec((1, LANES), lambda b, n: (0, 0)),
        ],
        out_specs=[pl.BlockSpec((blk, D_MODEL), lambda b, n: (row(b, n), 0)),
                   pl.BlockSpec((blk, LANES), lambda b, n: (row(b, n), 0))],
        out_shape=[jax.ShapeDtypeStruct((t, D_MODEL), BF16), jax.ShapeDtypeStruct((t, LANES), F32)],
        args=(z, z, z, z, z, cos_t, sin_t, cos_t, sin_t, q_gain_t, k_gain_t, sinks_t), name="attn_fwd",
        semantics=("parallel", "parallel"), hosted=hosted)
    return (*outs, landed) if hosted is not None else outs


def _attn_bwd(z, o, lse, do, cos_t, sin_t, q_gain_t, k_gain_t, sinks_t, *, n_seq, seq, hosted=None):
    t = n_seq * seq
    blk = WINDOW
    nb = seq // blk
    kw = KV_W
    scale = HEAD_DIM ** -0.5

    def body(qc_ref, qn_ref, kc_ref, vp_ref, vc_ref, oc_ref, on_ref, doc_ref, don_ref, lc_ref, ln_ref,
             cosc_ref, sinc_ref, cosn_ref, sinn_ref, qg_ref, kg_ref, sk_ref,
             dz_ref, vec_ref, dq_s, q_s, k_s):
        n = pl.program_id(1)
        ecq, eeq = _head_matrices(D_MODEL)
        eck, eek = _head_matrices(KV_W)
        cosc, sinc = cosc_ref[...], sinc_ref[...]
        qg, kg = qg_ref[...], kg_ref[...]
        own, other = n & 1, 1 - (n & 1)

        @pl.when(n == 0)
        def _():
            for part, value in enumerate(_normrope_fwd(qc_ref[...], qg, cosc, sinc, ecq, eeq)):
                q_s[own, part] = value
            k_s[other] = jnp.zeros((blk, kw), F32)

        for part, value in enumerate(_normrope_fwd(qn_ref[...], qg, cosn_ref[...], sinn_ref[...], ecq, eeq)):
            q_s[other, part] = value
        qhc, nqc, rsqc = q_s[own, 0], q_s[own, 1], q_s[own, 2]
        qhn = q_s[other, 0]
        khc, nkc, rskc = _normrope_fwd(kc_ref[...], kg, cosc, sinc, eck, eek)
        khp = k_s[other]
        k_s[own] = khc
        doc = doc_ref[...].astype(F32)
        don = don_ref[...].astype(F32)
        delc = _dot_split(doc * oc_ref[...].astype(F32), ecq)
        deln = _dot_split(don * on_ref[...].astype(F32), ecq)
        lc_t, ln_t, delc_t, deln_t = lc_ref[...], ln_ref[...], delc.T, deln.T
        above, causal = _window_masks(blk)
        above_c, above_n = above & (n > 0), above & (n < nb - 1)
        seg = lax.broadcasted_iota(jnp.int32, (blk, GROUP_W), 1) // HEAD_DIM
        lane = lax.broadcasted_iota(jnp.int32, (1, LANES), 1)
        sk_t = jnp.broadcast_to(sk_ref[...], (blk, LANES)).T
        dsink = jnp.zeros((1, LANES), F32)
        kcat = jnp.concatenate([khp, khc], axis=0)
        vcat = jnp.concatenate([vp_ref[...], vc_ref[...]], axis=0)
        kcat_t = kcat.T.astype(BF16)
        dkh = jnp.zeros((blk, GROUP_W), F32)
        dvh = jnp.zeros((blk, GROUP_W), F32)

        def fold_to(group, t):
            total = t + pltpu.roll(t, HEAD_DIM, 1)
            total = total + pltpu.roll(total, 2 * HEAD_DIM, 1)
            return jnp.where(seg == group, total, 0.0)

        groups = range(N_KV_HEADS)
        cols = [slice(g * GROUP_W, (g + 1) * GROUP_W) for g in groups]
        qsc, qsn = qhc * scale, qhn * scale
        qb_c = [_head_blocks(qsc[:, cols[g]]).astype(BF16) for g in groups]
        qb_n = [_head_blocks(qsn[:, cols[g]]).astype(BF16) for g in groups]
        dob_c = [_head_blocks(doc[:, cols[g]]).astype(BF16) for g in groups]
        dob_n = [_head_blocks(don[:, cols[g]]).astype(BF16) for g in groups]
        raw = []
        for g in groups:
            krep = _replicate_head(kcat, g).astype(BF16)
            vrep = _replicate_head(vcat, g).astype(BF16)
            raw.append((_dot_nt(krep, qb_c[g]), _dot_nt(vrep, dob_c[g]),
                        _dot_nt(krep[blk:], qb_n[g]), _dot_nt(vrep[blk:], dob_n[g])))
        cooked = []
        for g in groups:
            s_c, dp_c, s_n, dp_n = raw[g]
            l_row, d_row = _head_rows(lc_t, g), _head_rows(delc_t, g)
            p_c = _mask_window(jnp.exp(s_c - l_row), above_c, causal, 0.0)
            ds_c = (p_c * (dp_c - d_row)).astype(BF16)
            p_n = jnp.where(above_n, jnp.exp(s_n - _head_rows(ln_t, g)), 0.0)
            ds_n = (p_n * (dp_n - _head_rows(deln_t, g))).astype(BF16)
            cooked.append((p_c[blk:].astype(BF16), ds_c, p_n.astype(BF16), ds_n))
            p_sink = jnp.exp(_head_rows(sk_t, g) - l_row) * d_row
            for h in range(GROUP):
                dsink = dsink + jnp.where(lane == GROUP * g + h,
                                          -jnp.sum(p_sink[:, h * blk:(h + 1) * blk], axis=1, keepdims=True), 0.0)
        for g in groups:
            p_cb, ds_c, p_nb, ds_n = cooked[g]
            dq_t = jnp.dot(kcat_t[g * HEAD_DIM:(g + 1) * HEAD_DIM], ds_c, preferred_element_type=F32)
            dq_s[:, cols[g]] = _stack_heads(dq_t, blk).T * scale
            dk_rep = (jnp.dot(ds_c[blk:], qb_c[g], preferred_element_type=F32)
                      + jnp.dot(ds_n, qb_n[g], preferred_element_type=F32))
            dv_rep = (jnp.dot(p_cb, dob_c[g], preferred_element_type=F32)
                      + jnp.dot(p_nb, dob_n[g], preferred_element_type=F32))
            dkh = dkh + fold_to(g, dk_rep)
            dvh = dvh + fold_to(g, dv_rep)
        dq, dqg = _normrope_bwd(dq_s[...], nqc, rsqc, qg, cosc, sinc, ecq, eeq)
        dk, dkg = _normrope_bwd(dkh, nkc, rskc, kg, cosc, sinc, eck, eek)
        dz_ref[:, :ATTN_K_AT] = dq.astype(BF16)
        dz_ref[:, ATTN_K_AT:ATTN_V_AT] = dk.astype(BF16)
        dz_ref[:, ATTN_V_AT:] = dvh.astype(BF16)

        @pl.when(n == 0)
        def _():
            vec_ref[...] = jnp.zeros_like(vec_ref)

        vec_ref[0:1, :] += dqg
        vec_ref[1:2, 0:kw] += dkg
        vec_ref[2:3, 0:LANES] += dsink

    def row(b, n):
        return b * nb + n

    def prev(b, n):
        return b * nb + jnp.maximum(n - 1, 0)

    def nxt(b, n):
        return b * nb + jnp.minimum(n + 1, nb - 1)

    def tiles(width, col, which):
        return pl.BlockSpec((blk, width), lambda b, n: (which(b, n), col))

    def table(which):
        return pl.BlockSpec((blk, LANES), lambda b, n: (which(0, n), 0))

    outs, landed = _call(
        body,
        grid=(n_seq, nb),
        in_specs=[
            tiles(D_MODEL, 0, row), tiles(D_MODEL, 0, nxt),
            tiles(kw, ATTN_K_AT // kw, row),
            tiles(kw, ATTN_V_AT // kw, prev), tiles(kw, ATTN_V_AT // kw, row),
            tiles(D_MODEL, 0, row), tiles(D_MODEL, 0, nxt),
            tiles(D_MODEL, 0, row), tiles(D_MODEL, 0, nxt),
            tiles(LANES, 0, row), tiles(LANES, 0, nxt),
            table(row), table(row), table(nxt), table(nxt),
            pl.BlockSpec((1, D_MODEL), lambda b, n: (0, 0)),
            pl.BlockSpec((1, kw), lambda b, n: (0, 0)),
            pl.BlockSpec((1, LANES), lambda b, n: (0, 0)),
        ],
        out_specs=[tiles(ATTN_W, 0, row), pl.BlockSpec((None, 8, D_MODEL), lambda b, n: (b, 0, 0))],
        out_shape=[jax.ShapeDtypeStruct((t, ATTN_W), BF16), jax.ShapeDtypeStruct((n_seq, 8, D_MODEL), F32)],
        scratch_shapes=[pltpu.VMEM((blk, D_MODEL), F32), pltpu.VMEM((2, 3, blk, D_MODEL), F32),
                        pltpu.VMEM((2, blk, kw), F32)],
        args=(z, z, z, z, z, o, o, do, do, lse, lse, cos_t, sin_t, cos_t, sin_t,
              q_gain_t, k_gain_t, sinks_t), name="attn_bwd", semantics=("arbitrary", "arbitrary"), hosted=hosted)
    return (*outs, landed) if hosted is not None else outs


MERGE_COLS = 512


def _merge_fwd(z, ya, yb):
    t = ya.shape[0]
    tm, tc = min(512, t), MERGE_COLS

    def body(ga_ref, gb_ref, ya_ref, yb_ref, o_ref):
        o_ref[...] = (_sig(ga_ref[...]) * ya_ref[...] + _sig(gb_ref[...]) * yb_ref[...]).astype(BF16)

    tile = pl.BlockSpec((tm, tc), lambda i, j: (i, j))
    return pl.pallas_call(
        body,
        grid=(t // tm, D_MODEL // tc),
        in_specs=[pl.BlockSpec((tm, tc), lambda i, j: (i, j)),
                  pl.BlockSpec((tm, tc), lambda i, j: (i, D_MODEL // tc + j)), tile, tile],
        out_specs=tile,
        out_shape=jax.ShapeDtypeStruct((t, D_MODEL), BF16),
        compiler_params=_params("parallel", "parallel"),
        name="merge_fwd",
    )(z, z, ya, yb)


def _rope_tables(seq):
    inv = ROPE_THETA ** (-jnp.arange(0, HEAD_DIM, 2, dtype=F32) / HEAD_DIM)
    ang = jnp.arange(seq, dtype=F32)[:, None] * inv[None, :]
    cos, sin = jnp.cos(ang), jnp.sin(ang)
    return jnp.tile(jnp.concatenate([cos, cos], axis=1), (1, 2)), jnp.tile(jnp.concatenate([-sin, sin], axis=1), (1, 2))


def _block_diag_tiles(w):
    per = RNN_TILE // RNN_BLOCK_W
    w4 = w.reshape(D_MODEL // RNN_TILE, per, RNN_BLOCK_W, RNN_BLOCK_W)
    eye = jnp.eye(per, dtype=w.dtype)
    dense = jnp.einsum("tpij,pq->tpiqj", w4, eye)
    return dense.reshape(D_MODEL // RNN_TILE, RNN_TILE, RNN_TILE).astype(BF16)


def _block_diag_extract(dense):
    per = RNN_TILE // RNN_BLOCK_W
    d5 = dense.reshape(D_MODEL // RNN_TILE, per, RNN_BLOCK_W, per, RNN_BLOCK_W)
    blocks = jnp.stack([d5[:, p, :, p, :] for p in range(per)], axis=1)
    return blocks.reshape(D_MODEL // RNN_BLOCK_W, RNN_BLOCK_W, RNN_BLOCK_W)


def _local_step(x, p, target, w, *, n_seq, seq, comm=None):
    w = dict(w)

    def run(tag, fn, *args, **kwargs):
        hosted = comm.host(tag) if comm is not None else None
        if hosted is None:
            return fn(*args, **kwargs)
        *outs, landed = fn(*args, hosted=hosted, **kwargs)
        comm.landed(tag, landed, w)
        return outs[0] if len(outs) == 1 else outs

    def ready(batch, grads, extra=None):
        if comm is not None:
            comm.ready(batch, grads, extra)

    cos_t, sin_t = _rope_tables(seq)
    q_gain_t = jnp.tile(w["q_gain"], (1, N_Q_HEADS))
    k_gain_t = jnp.tile(w["k_gain"], (1, N_KV_HEADS))
    sinks_t = jnp.pad(w["sinks"], ((0, 0), (0, LANES - N_Q_HEADS)))
    wrg_bd, wig_bd = _block_diag_tiles(w["w_rg"]), _block_diag_tiles(w["w_ig"])
    dims = dict(n_seq=n_seq, seq=seq)

    h = _rmsnorm_fwd(x, w["g_mix"], name="norm_mix")
    z_rnn = _matmul(h, w["w_in"], mode="nn", tm=1024, tn=1024, out_dtypes=[F32], name="mm_in_rnn",
                    b_cols=[(0, COL_RNN_END)])
    w_in_attn, w_in_gate = w["w_in"][:, COL_RNN_END:COL_ATTN_END], w["w_in"][:, COL_ATTN_END:]
    z_attn = _matmul(h, w_in_attn, mode="nn", tm=1024, tn=1024, out_dtypes=[F32], name="mm_in_attn")
    z_gate = _matmul(h, w_in_gate, mode="nn", tm=1024, tn=1024, out_dtypes=[F32], name="mm_in_gate")
    xc, hr, ya_in = run("rnn_fwd", _rnn_fwd, z_rnn, w["conv_w"], w["conv_b"], wrg_bd, w["b_rg"], wig_bd, w["b_ig"],
                        w["lru_lambda"], **dims)
    o, lse = run("attn_fwd", _attn_fwd, z_attn, cos_t, sin_t, q_gain_t, k_gain_t, sinks_t, **dims)
    ya = run("mm_rnn_proj", _matmul, ya_in, w["w_rnn_proj"], mode="nn", tm=1024, tn=1024, out_dtypes=[F32],
             name="mm_rnn_proj")
    yb = _matmul(o, w["w_attn_proj"], mode="nn", tm=1024, tn=1024, out_dtypes=[F32], name="mm_attn_proj")
    merged = _merge_fwd(z_gate, ya, yb)
    def residual_then_norm(acc, res, gain):
        new = res + acc
        return new, _rmsnorm_rows(new, gain)

    x1, hm = _matmul(merged, w["w_out"], mode="nn", tm=512, tn=1024, out_dtypes=[F32, BF16], name="mm_out",
                     epilogue=residual_then_norm, extras=(x,), row_vecs=(w["g_mlp"],))
    act = _matmul(hm, w["w_up"], mode="nn", tm=1024, tn=1024, out_dtypes=[BF16], name="mm_up",
                  epilogue=lambda acc: (jnp.square(jnp.maximum(acc, 0.0)),))
    x2, hp = _matmul(act, w["w_down"], mode="nn", tm=512, tn=1024, out_dtypes=[F32, BF16], name="mm_down",
                     epilogue=residual_then_norm, extras=(x1,), row_vecs=(w["g_ple"],))
    p_bf = p.astype(BF16)
    e = _matmul(p_bf, w["w_ple_proj"], mode="nn", tm=1024, tn=1024, out_dtypes=[F32], name="mm_ple_proj")

    def loss_head(gt, x2v, ev, tgt):
        sg = _sig(gt)
        diff = x2v + ev * sg - tgt
        dx = diff * (1.0 / D_MODEL)
        return dx, dx * ev * sg * (1.0 - sg), dx * sg, jnp.sum(diff * diff, axis=0, keepdims=True)

    dx3, dgt, de, loss_row = _matmul(hp, w["w_ple_gate"], mode="nn", tm=512, tn=1024, out_dtypes=[F32, BF16, BF16],
                                     name="mm_ple_gate", epilogue=loss_head, extras=(x2, e, target), n_row_sums=1)

    g = {}
    g["w_ple_proj"] = _matmul_tn(p_bf, de, tk=PLE_DIM, tn=1024, tt=1024, name="mm_d_ple_proj",
                                 slot_cols=D_MODEL // N_DEV)
    g["w_ple_gate"] = _matmul_tn(hp, dgt, tk=1024, tn=1024, tt=1024, name="mm_d_ple_gate")
    def through_norm(dy, xv, dres, gain):
        dx, dgain = _rmsnorm_bwd_rows(dy, xv, dres, gain)
        return dx, dx, dgain

    dx2, dx2_bf, g["g_ple"] = _matmul(
        dgt, w["w_ple_gate"], mode="nt", tm=512, tn=1024, out_dtypes=[F32, BF16], name="mm_dhp",
        epilogue=through_norm, extras=(x2, dx3), row_vecs=(w["g_ple"],), n_row_sums=1)
    g["w_down"] = _matmul_tn(act, dx2_bf, tk=1024, tn=1024, tt=1024, name="mm_d_down")
    du = _matmul(dx2_bf, w["w_down"], mode="nt", tm=1024, tn=1024, out_dtypes=[BF16], name="mm_dact",
                 epilogue=lambda acc, a: (acc * (2.0 * jnp.sqrt(a.astype(F32))),), extras=(act,))
    g["w_up"] = _matmul_tn(hm, du, tk=1024, tn=1024, tt=1024, name="mm_d_up", slot_cols=D_FF // N_DEV)
    ready(1, g)
    dx1, dx1_bf, g["g_mlp"] = run(
        "mm_dhm", _matmul, du, w["w_up"], mode="nt", tm=512, tn=1024, out_dtypes=[F32, BF16], name="mm_dhm",
        epilogue=through_norm, extras=(x1, dx2), row_vecs=(w["g_mlp"],), n_row_sums=1)
    g["w_out"] = _matmul_tn(merged, dx1_bf, tk=1024, tn=1024, tt=1024, name="mm_d_out")
    def merge_bwd(dm, ga, gb, yav, ybv):
        sa, sb = _sig(ga), _sig(gb)
        return dm * sa, dm * sb, dm * yav * sa * (1.0 - sa), dm * ybv * sb * (1.0 - sb)

    dya, dyb, dga, dgb = _matmul(dx1_bf, w["w_out"], mode="nt", tm=512, tn=1024, out_dtypes=[BF16] * 4,
                                 name="mm_dmerged", epilogue=merge_bwd, extras=(z_gate, z_gate, ya, yb),
                                 extra_col_blocks=(0, 1, 0, 0))
    g["w_rnn_proj"] = _matmul_tn(ya_in, dya, tk=1024, tn=1024, tt=1024, name="mm_d_rnn_proj")
    g["w_attn_proj"] = _matmul_tn(o, dyb, tk=1024, tn=1024, tt=1024, name="mm_d_attn_proj")
    ready(2, g)
    dya_in = run("mm_dya_in", _matmul, dya, w["w_rnn_proj"], mode="nt", tm=1024, tn=1024, out_dtypes=[F32],
                 name="mm_dya_in")
    do = _matmul(dyb, w["w_attn_proj"], mode="nt", tm=1024, tn=1024, out_dtypes=[BF16], name="mm_do")
    dx_rnn, dg_rnn, dwrg_dense, dwig_dense, rnn_vec = run(
        "rnn_bwd", _rnn_bwd, dya_in, z_rnn, xc, hr, w["conv_w"], wrg_bd, w["b_rg"], wig_bd, w["b_ig"],
        w["lru_lambda"], **dims)
    dz_attn, attn_vec = run("attn_bwd", _attn_bwd, z_attn, o, lse, do, cos_t, sin_t, q_gain_t, k_gain_t, sinks_t,
                            **dims)
    dz_parts = (dx_rnn, dg_rnn, dz_attn, dga, dgb)
    g["w_rg"] = _block_diag_extract(dwrg_dense)
    g["w_ig"] = _block_diag_extract(dwig_dense)
    g["b_rg"], g["b_ig"], g["lru_lambda"], g["conv_b"] = (rnn_vec[i:i + 1] for i in range(4))
    g["conv_w"] = rnn_vec[4:8]
    attn_vec = attn_vec[0] if n_seq == 1 else functools.reduce(jnp.add, [attn_vec[b] for b in range(n_seq)])
    g["q_gain"] = attn_vec[0].reshape(N_Q_HEADS, HEAD_DIM).sum(axis=0)[None, :]
    g["k_gain"] = attn_vec[1, :KV_W].reshape(N_KV_HEADS, HEAD_DIM).sum(axis=0)[None, :]
    g["sinks"] = attn_vec[2:3, :N_Q_HEADS]
    ready(SMALL_BATCH, g, {LOSS_ROW: loss_row})
    g["w_in"] = jnp.concatenate(
        list(run("mm_d_in_rnn", _matmul_tn_multi, h, dz_parts[:2], tt=1024, name="mm_d_in_rnn"))
        + list(run("mm_d_in_rest", _matmul_tn_multi, h, dz_parts[2:], tt=512, name="mm_d_in_rest")), axis=1)
    ready(3, g)
    windows = ((w["w_in"], (0, D_MODEL)), (w["w_in"], (D_MODEL, D_MODEL)), (w_in_attn, (0, ATTN_W)),
               (w_in_gate, (0, D_MODEL)), (w_in_gate, (D_MODEL, D_MODEL)))
    grad_x, g["g_mix"] = run(
        "mm_dh", _matmul, dz_parts, [wd[0] for wd in windows], mode="nt", tm=256, tn=1024, out_dtypes=[F32],
        name="mm_dh", b_cols=[wd[1] for wd in windows], epilogue=_rmsnorm_bwd_rows, extras=(x, dx1),
        row_vecs=(w["g_mix"],), n_row_sums=1)
    return jnp.sum(loss_row), grad_x, g


MESH_ID = pl.DeviceIdType.MESH


def _coords(index):
    return (index >> 2) & 1, (index >> 1) & 1, index & 1


def _exchange(srcs, kinds, *, name):
    n = len(srcs)
    n_peer = N_DEV - 1

    def body(*refs):
        src, dst = refs[:n], refs[n:2 * n]
        send_sems, recv_sems, local_sems = refs[2 * n:]
        me = 4 * lax.axis_index("x") + 2 * lax.axis_index("y") + lax.axis_index("c")

        def remote(i, d):
            peer = (me + d) & (N_DEV - 1)
            piece = src[i] if kinds[i] == "gather" else src[i].at[peer]
            return pltpu.make_async_remote_copy(
                src_ref=piece, dst_ref=dst[i].at[me], send_sem=send_sems.at[i * n_peer + d - 1],
                recv_sem=recv_sems.at[i * n_peer + d - 1], device_id=_coords(peer), device_id_type=MESH_ID)

        def arrival(i, d):
            sender = (me - d) & (N_DEV - 1)
            piece = src[i] if kinds[i] == "gather" else src[i].at[sender]
            return pltpu.make_async_remote_copy(
                src_ref=piece, dst_ref=dst[i].at[sender], send_sem=send_sems.at[i * n_peer + d - 1],
                recv_sem=recv_sems.at[i * n_peer + d - 1], device_id=_coords(sender), device_id_type=MESH_ID)

        own = []
        for i in range(n):
            piece = src[i] if kinds[i] == "gather" else src[i].at[me]
            own.append(pltpu.make_async_copy(piece, dst[i].at[me], local_sems.at[i]))
            own[-1].start()
        sent = [remote(i, d) for d in range(1, N_DEV) for i in range(n)]
        for cp in sent:
            cp.start()
        for d in range(1, N_DEV):
            for i in range(n):
                arrival(i, d).wait_recv()
        for cp in sent:
            cp.wait_send()
        for cp in own:
            cp.wait()

    def out_of(s, kind):
        shape = s.shape if kind == "scatter" else (N_DEV,) + s.shape
        return jax.ShapeDtypeStruct(shape, s.dtype)

    any_spec = pl.BlockSpec(memory_space=pl.ANY)
    return pl.pallas_call(
        body,
        in_specs=[any_spec] * n,
        out_specs=[any_spec] * n,
        out_shape=[out_of(s, k) for s, k in zip(srcs, kinds)],
        scratch_shapes=[pltpu.SemaphoreType.DMA((n * n_peer,)), pltpu.SemaphoreType.DMA((n * n_peer,)),
                        pltpu.SemaphoreType.DMA((n,))],
        compiler_params=pltpu.CompilerParams(has_side_effects=True),
        name=name,
    )(*srcs)


def _remote(src, dst, send_sem, recv_sem, to):
    return pltpu.make_async_remote_copy(src_ref=src, dst_ref=dst, send_sem=send_sem, recv_sem=recv_sem,
                                        device_id=to, device_id_type=MESH_ID)


def _gather_two_level(shards, *, name):
    n = len(shards)
    per = N_DEV - 1

    def body(*refs):
        src, dst = refs[:n], refs[n:2 * n]
        send_sems, recv_sems, local_sems = refs[2 * n:]
        x, y, c = lax.axis_index("x"), lax.axis_index("y"), lax.axis_index("c")
        me, sibling = (x, y, c), (x, y, 1 - c)
        chips = [(1 - x, y), (x, 1 - y), (1 - x, 1 - y)]

        def slot(pos):
            return 4 * pos[0] + 2 * pos[1] + pos[2]

        def copy(i, k, block, to, from_shard=False):
            source = src[i] if from_shard else dst[i].at[slot(block)]
            return _remote(source, dst[i].at[slot(block)], send_sems.at[i * per + k], recv_sems.at[i * per + k], to)

        mine = [pltpu.make_async_copy(src[i], dst[i].at[slot(me)], local_sems.at[i]) for i in range(n)]
        for cp in mine:
            cp.start()
        first = []
        for i in range(n):
            first.append(copy(i, 0, me, sibling, from_shard=True))
            first += [copy(i, 1 + j, me, (*chip, c), from_shard=True) for j, chip in enumerate(chips)]
        for cp in first:
            cp.start()
        passed = []
        for i in range(n):
            for j, chip in enumerate(chips):
                copy(i, 1 + j, (*chip, c), me).wait_recv()
                passed.append(copy(i, 4 + j, (*chip, c), sibling))
                passed[-1].start()
        for i in range(n):
            copy(i, 0, sibling, me).wait_recv()
            for j, chip in enumerate(chips):
                copy(i, 4 + j, (*chip, 1 - c), me).wait_recv()
        for cp in first + passed:
            cp.wait_send()
        for cp in mine:
            cp.wait()

    any_spec = pl.BlockSpec(memory_space=pl.ANY)
    return pl.pallas_call(
        body,
        in_specs=[any_spec] * n,
        out_specs=[any_spec] * n,
        out_shape=[jax.ShapeDtypeStruct((N_DEV,) + s.shape, s.dtype) for s in shards],
        scratch_shapes=[pltpu.SemaphoreType.DMA((n * per,)), pltpu.SemaphoreType.DMA((n * per,)),
                        pltpu.SemaphoreType.DMA((n,))],
        name=name,
    )(*shards)


CHIPS = N_DEV // 2


def _other_chips(x, y):
    return [(x, 1 - y), (1 - x, y), (1 - x, 1 - y)]


def _hosted_gather_first(shards):
    n = len(shards)
    per = CHIPS

    def plan(src, dst, send_sems, recv_sems, local_sems, first_sem):
        x, y, c = lax.axis_index("x"), lax.axis_index("y"), lax.axis_index("c")
        peers = [(x, y, 1 - c)] + [(*chip, c) for chip in _other_chips(x, y)]
        copies = []
        for i in range(n):
            own = pltpu.make_async_copy(src[i], dst[i].at[4 * x + 2 * y + c], local_sems.at[first_sem + i])
            copies.append(_Xfer(own.start, own.wait))
        for j, peer in enumerate(peers):
            for i in range(n):
                k = first_sem + i * per + j
                out = _remote(src[i], dst[i].at[4 * x + 2 * y + c], send_sems.at[k], recv_sems.at[k], peer)
                arrival = _remote(src[i], dst[i].at[4 * peer[0] + 2 * peer[1] + peer[2]], send_sems.at[k],
                                  recv_sems.at[k], peer)

                def wait(out=out, arrival=arrival):
                    arrival.wait_recv()
                    out.wait_send()

                copies.append(_Xfer(out.start, wait))
        return copies

    out_shape = tuple(jax.ShapeDtypeStruct((N_DEV,) + s.shape, s.dtype) for s in shards)
    return _Hosted(tuple(shards), out_shape, n * per, plan)


def _hosted_gather_second(landed):
    n = len(landed)
    per = CHIPS - 1

    def plan(src, dst, send_sems, recv_sems, local_sems, first_sem):
        x, y, c = lax.axis_index("x"), lax.axis_index("y"), lax.axis_index("c")
        copies = []
        for j, chip in enumerate(_other_chips(x, y)):
            mine, theirs = 4 * chip[0] + 2 * chip[1] + c, 4 * chip[0] + 2 * chip[1] + 1 - c
            for i in range(n):
                k = first_sem + i * per + j
                out = _remote(src[i].at[mine], dst[i].at[mine], send_sems.at[k], recv_sems.at[k], (x, y, 1 - c))
                arrival = _remote(src[i].at[theirs], dst[i].at[theirs], send_sems.at[k], recv_sems.at[k],
                                  (x, y, 1 - c))

                def wait(out=out, arrival=arrival):
                    arrival.wait_recv()
                    out.wait_send()

                copies.append(_Xfer(out.start, wait))
        return copies

    out_shape = tuple(jax.ShapeDtypeStruct(a.shape, a.dtype) for a in landed)
    return _Hosted(tuple(landed), out_shape, n * per, plan, tuple((i, i) for i in range(n)))


def _hosted_sibling_swap(arrays, sliced):
    n_sems = sum(CHIPS if s else 1 for s in sliced)

    def plan(src, dst, send_sems, recv_sems, local_sems, first_sem):
        x, y, c = lax.axis_index("x"), lax.axis_index("y"), lax.axis_index("c")
        sibling = (x, y, 1 - c)
        copies, k = [], first_sem
        for i, is_sliced in enumerate(sliced):
            pieces = [(src[i].at[2 * s + 1 - c], dst[i].at[s]) for s in range(CHIPS)] if is_sliced else [(src[i], dst[i])]
            for source, target in pieces:
                cp = _remote(source, target, send_sems.at[k], recv_sems.at[k], sibling)
                copies.append(_Xfer(cp.start, cp.wait))
                k += 1
        return copies

    out_shape = tuple(jax.ShapeDtypeStruct((CHIPS,) + a.shape[1:] if s else a.shape, a.dtype)
                      for a, s in zip(arrays, sliced))
    return _Hosted(tuple(arrays), out_shape, n_sems, plan)


def _hosted_chip_exchange(arrays, sliced):
    n = len(arrays)
    per = CHIPS - 1

    def plan(src, dst, send_sems, recv_sems, local_sems, first_sem):
        x, y, c = lax.axis_index("x"), lax.axis_index("y"), lax.axis_index("c")
        chip = 2 * x + y
        copies = []
        for i in range(n):
            own = pltpu.make_async_copy(src[i].at[chip] if sliced[i] else src[i], dst[i].at[chip],
                                        local_sems.at[first_sem + i])
            copies.append(_Xfer(own.start, own.wait))
        for d in range(1, CHIPS):
            other = chip ^ d
            to = ((other >> 1) & 1, other & 1, c)
            for i in range(n):
                k = first_sem + i * per + d - 1
                source = src[i].at[other] if sliced[i] else src[i]
                out = _remote(source, dst[i].at[chip], send_sems.at[k], recv_sems.at[k], to)
                arrival = _remote(source, dst[i].at[other], send_sems.at[k], recv_sems.at[k], to)

                def wait(out=out, arrival=arrival):
                    arrival.wait_recv()
                    out.wait_send()

                copies.append(_Xfer(out.start, wait))
        return copies

    out_shape = tuple(jax.ShapeDtypeStruct(a.shape if s else (CHIPS,) + a.shape, a.dtype)
                      for a, s in zip(arrays, sliced))
    return _Hosted(tuple(arrays), out_shape, n * per, plan)


def _add_sibling(parts, received, core, *, name):
    _, r, cols = parts.shape
    tr = min(256, r)

    def body(core_ref, a_ref, b_ref, o_ref):
        o_ref[...] = (a_ref[...] + b_ref[...]).astype(BF16)

    grid_spec = pltpu.PrefetchScalarGridSpec(
        num_scalar_prefetch=1,
        grid=(CHIPS, r // tr),
        in_specs=[pl.BlockSpec((None, tr, cols), lambda k, i, core_ref: (2 * k + core_ref[0], i, 0)),
                  pl.BlockSpec((None, tr, cols), lambda k, i, core_ref: (k, i, 0))],
        out_specs=pl.BlockSpec((None, tr, cols), lambda k, i, core_ref: (k, i, 0)),
    )
    return pl.pallas_call(body, grid_spec=grid_spec, out_shape=jax.ShapeDtypeStruct((CHIPS, r, cols), BF16),
                          compiler_params=_params("parallel", "parallel"), name=name)(core, parts, received)


def _add_whole(a, b, *, name):
    def body(a_ref, b_ref, o_ref):
        o_ref[...] = a_ref[...] + b_ref[...]

    return pl.pallas_call(body, out_shape=jax.ShapeDtypeStruct(a.shape, F32), name=name)(a, b)


def _adamw(parts, w, m, v, *, name):
    r, c = w.shape
    n_parts = parts.shape[0]
    tr = min(256, r)
    c1 = 1.0 - ADAM_B1 ** ADAM_STEP
    c2 = 1.0 - ADAM_B2 ** ADAM_STEP

    def body(p_ref, w_ref, m_ref, v_ref, g_ref, d_ref, nm_ref, nv_ref):
        g = p_ref[0].astype(F32)
        for s in range(1, n_parts):
            g = g + p_ref[s].astype(F32)
        nm = ADAM_B1 * m_ref[...] + (1.0 - ADAM_B1) * g
        nv = ADAM_B2 * v_ref[...] + (1.0 - ADAM_B2) * (g * g)
        g_ref[...] = g
        nm_ref[...] = nm
        nv_ref[...] = nv
        d_ref[...] = -ADAM_LR * ((nm / c1) / (jnp.sqrt(nv / c2) + ADAM_EPS) + ADAM_WD * w_ref[...])

    tile = pl.BlockSpec((tr, c), lambda i: (i, 0))
    return pl.pallas_call(
        body,
        grid=(r // tr,),
        in_specs=[pl.BlockSpec((n_parts, tr, c), lambda i: (0, i, 0)), tile, tile, tile],
        out_specs=[tile] * 4,
        out_shape=[jax.ShapeDtypeStruct((r, c), F32)] * 4,
        compiler_params=_params("parallel"),
        name=name,
    )(parts, w, m, v)


BIG = ("w_in", "w_rnn_proj", "w_attn_proj", "w_out", "w_up", "w_down", "w_ple_gate", "w_ple_proj")
LOSS_ROW = "loss"
SMALL = (("conv_b", 1), ("b_rg", 1), ("b_ig", 1), ("lru_lambda", 1), ("g_mlp", 1), ("g_ple", 1),
         ("q_gain", 1), ("k_gain", 1), ("sinks", 1), (LOSS_ROW, 1), ("w_rg", 64), ("w_ig", 64))
SMALL_ROWS = 144
ROW_SHARDED = ("w_rnn_proj", "w_attn_proj", "w_out", "w_down", "w_ple_gate")
COL_SHARDED = ("w_in", "w_up", "w_ple_proj")
BATCHES = {1: ("w_ple_proj", "w_ple_gate", "w_down", "w_up"), 2: ("w_out", "w_rnn_proj", "w_attn_proj"),
           3: ("w_in", "conv_w")}
SMALL_BATCH = 4


def _pack_small(vals):
    rows = []
    for nm, nrow in SMALL:
        flat = vals[nm].reshape(-1).astype(F32)
        rows.append(jnp.pad(flat, (0, nrow * D_MODEL - flat.shape[0])).reshape(nrow, D_MODEL))
    used = sum(nrow for _, nrow in SMALL)
    rows.append(jnp.zeros((SMALL_ROWS - used, D_MODEL), F32))
    return jnp.concatenate(rows, axis=0)


def _unpack_small(packed, shapes):
    out, at = {}, 0
    for nm, nrow in SMALL:
        size = 1
        for s in shapes[nm]:
            size *= s
        out[nm] = packed[at:at + nrow].reshape(-1)[:size].reshape(shapes[nm])
        at += nrow
    return out


def _full_weight(name, landed):
    if name in COL_SHARDED:
        return landed.transpose(1, 0, 2).reshape(landed.shape[1], N_DEV * landed.shape[2])
    return landed.reshape(N_DEV * landed.shape[1], landed.shape[2])


def _owner_slots(name, grad):
    if name == "w_in":
        return grad.reshape(D_MODEL, N_DEV, IN_TOTAL // N_DEV).transpose(1, 0, 2)
    if name == "conv_w":
        return grad.reshape(CONV_W, N_DEV, D_MODEL // N_DEV).transpose(1, 0, 2)
    if name in COL_SHARDED:
        return grad
    return grad.reshape(N_DEV, grad.shape[0] // N_DEV, grad.shape[1])


class _StepExchanges:
    FIRST, SECOND = "first", "second"
    EARLY, MID, LATE = ("w_rnn_proj", "w_attn_proj", "w_out"), ("w_up",), ("w_down", "w_ple_gate", "w_ple_proj")
    GATHERS = {"rnn_fwd": ((FIRST, EARLY), (FIRST, MID)),
               "attn_fwd": ((SECOND, EARLY), (SECOND, MID), (FIRST, LATE)), "mm_rnn_proj": ((SECOND, LATE),)}
    SWAPS = {"mm_dhm": 1, "mm_dya_in": 2, "mm_d_in_rnn": SMALL_BATCH}
    CHIP_EXCHANGES = {"rnn_bwd": (1,), "attn_bwd": (2,), "mm_d_in_rest": (SMALL_BATCH,), "mm_dh": (3,)}

    def __init__(self, shards, core):
        self.shards = shards
        self.core = core
        self.parts, self.swapped, self.summed, self.half_gathered = {}, {}, {}, {}

    def ready(self, batch, grads, extra=None):
        if batch == SMALL_BATCH:
            self.parts[batch] = ([_pack_small({**grads, **extra})], [False])
            return
        arrays = [_owner_slots(nm, grads[nm]) for nm in BATCHES[batch]]
        self.parts[batch] = (arrays, [True] * len(arrays))
        if batch not in self.SWAPS.values():
            _, self.swapped[batch] = _call(
                lambda: None, grid=(1,), in_specs=[], out_specs=[], out_shape=[], args=(), name="swap_last",
                semantics=("arbitrary",), hosted=_hosted_sibling_swap(*self.parts[batch]))

    def host(self, tag):
        if tag in self.GATHERS:
            return _merge_hosted([
                _hosted_gather_first([self.shards[nm] for nm in group]) if half == self.FIRST
                else _hosted_gather_second([self.half_gathered[nm] for nm in group])
                for half, group in self.GATHERS[tag]])
        if tag in self.SWAPS:
            return _hosted_sibling_swap(*self.parts[self.SWAPS[tag]])
        if tag in self.CHIP_EXCHANGES:
            hosted = []
            for batch in self.CHIP_EXCHANGES[tag]:
                arrays, sliced = self.parts[batch]
                labels = BATCHES.get(batch, ("small",))
                sums = [_add_sibling(a, r, self.core, name="add_" + lb) if s else _add_whole(a, r, name="add_" + lb)
                        for a, r, s, lb in zip(arrays, self.swapped[batch], sliced, labels)]
                hosted.append(_hosted_chip_exchange(sums, sliced))
            return _merge_hosted(hosted)
        return None

    def landed(self, tag, landed, weights):
        if tag in self.GATHERS:
            names = [(half, nm) for half, group in self.GATHERS[tag] for nm in group]
            for (half, nm), buf in zip(names, landed):
                if half == self.FIRST:
                    self.half_gathered[nm] = buf
                else:
                    weights[nm] = _full_weight(nm, buf)
        elif tag in self.SWAPS:
            self.swapped[self.SWAPS[tag]] = landed
        else:
            at = 0
            for batch in self.CHIP_EXCHANGES[tag]:
                count = len(self.parts[batch][0])
                self.summed[batch] = landed[at:at + count]
                at += count


def kernel(x, p, g_mix, w_in, conv_w, conv_b, w_rg, b_rg, w_ig, b_ig, lru_lambda, w_rnn_proj, q_gain, k_gain, sinks, w_attn_proj, w_out, g_mlp, w_up, w_down, g_ple, w_ple_gate, w_ple_proj, loss_target, m_g_mix, m_w_in, m_conv_w, m_conv_b, m_w_rg, m_b_rg, m_w_ig, m_b_ig, m_lru_lambda, m_w_rnn_proj, m_q_gain, m_k_gain, m_sinks, m_w_attn_proj, m_w_out, m_g_mlp, m_w_up, m_w_down, m_g_ple, m_w_ple_gate, m_w_ple_proj, v_g_mix, v_w_in, v_conv_w, v_conv_b, v_w_rg, v_b_rg, v_w_ig, v_b_ig, v_lru_lambda, v_w_rnn_proj, v_q_gain, v_k_gain, v_sinks, v_w_attn_proj, v_w_out, v_g_mlp, v_w_up, v_w_down, v_g_ple, v_w_ple_gate, v_w_ple_proj):
    names = ("g_mix", "w_in", "conv_w", "conv_b", "w_rg", "b_rg", "w_ig", "b_ig", "lru_lambda", "w_rnn_proj",
             "q_gain", "k_gain", "sinks", "w_attn_proj", "w_out", "g_mlp", "w_up", "w_down", "g_ple",
             "w_ple_gate", "w_ple_proj")
    wts = dict(zip(names, (g_mix, w_in, conv_w, conv_b, w_rg, b_rg, w_ig, b_ig, lru_lambda, w_rnn_proj, q_gain,
                           k_gain, sinks, w_attn_proj, w_out, g_mlp, w_up, w_down, g_ple, w_ple_gate, w_ple_proj)))
    mom1 = dict(zip(names, (m_g_mix, m_w_in, m_conv_w, m_conv_b, m_w_rg, m_b_rg, m_w_ig, m_b_ig, m_lru_lambda,
                            m_w_rnn_proj, m_q_gain, m_k_gain, m_sinks, m_w_attn_proj, m_w_out, m_g_mlp, m_w_up,
                            m_w_down, m_g_ple, m_w_ple_gate, m_w_ple_proj)))
    mom2 = dict(zip(names, (v_g_mix, v_w_in, v_conv_w, v_conv_b, v_w_rg, v_b_rg, v_w_ig, v_b_ig, v_lru_lambda,
                            v_w_rnn_proj, v_q_gain, v_k_gain, v_sinks, v_w_attn_proj, v_w_out, v_g_mlp, v_w_up,
                            v_w_down, v_g_ple, v_w_ple_gate, v_w_ple_proj)))
    n_seq, seq, _ = x.shape
    core = lax.axis_index("c").astype(jnp.int32).reshape(1)

    shards = {nm: wts[nm][0].astype(BF16) for nm in BIG}
    w_in_all, conv_all = _gather_two_level([shards["w_in"], conv_w[0]], name="gather_w_in")
    w = {nm: wts[nm] for nm in names if nm not in BIG}
    w["w_rg"], w["w_ig"] = w_rg[0], w_ig[0]
    w["conv_w"] = conv_all.transpose(1, 0, 2).reshape(CONV_W, D_MODEL)
    w["w_in"] = _full_weight("w_in", w_in_all)
    comm = _StepExchanges(shards, core)
    loss_sum, grad_x, g = _local_step(
        x.reshape(n_seq * seq, D_MODEL), p.reshape(n_seq * seq, PLE_DIM), loss_target.reshape(n_seq * seq, D_MODEL),
        w, n_seq=n_seq, seq=seq, comm=comm)
    del loss_sum

    res = {}
    for batch, batch_names in BATCHES.items():
        for nm, summed in zip(batch_names, comm.summed[batch]):
            res[nm] = _adamw(summed, wts[nm][0], mom1[nm][0], mom2[nm][0], name="adamw_" + nm)
    g_mix_parts, = _exchange([g["g_mix"]], ["gather"], name="gather_g_mix")
    res["g_mix"] = [r[0] for r in _adamw(g_mix_parts, g_mix, m_g_mix, v_g_mix, name="adamw_g_mix")]
    small_names = [nm for nm, _ in SMALL if nm != LOSS_ROW]
    full_small = {}
    for src, key in ((wts, "w"), (mom1, "m"), (mom2, "v")):
        vals = {nm: src[nm][0] for nm in small_names}
        vals[LOSS_ROW] = jnp.zeros((1,), F32)
        full_small[key] = _pack_small(vals)
    small_res = _adamw(comm.summed[SMALL_BATCH][0],full_small["w"], full_small["m"], full_small["v"], name="adamw_small")
    shapes = {nm: wts[nm].shape[1:] for nm in small_names}
    shapes[LOSS_ROW] = (D_MODEL,)
    small_out = [_unpack_small(r, shapes) for r in small_res]
    for nm in small_names:
        res[nm] = [so[nm] for so in small_out]
    loss = jnp.sum(small_out[0][LOSS_ROW]) * (0.5 / D_MODEL)

    outs = [loss, grad_x.reshape(n_seq, seq, D_MODEL)]
    for k in range(4):
        outs.extend(res[nm][k][None] for nm in names)
    return tuple(outs)
```

```python
import functools
from typing import Callable, NamedTuple

import jax
import jax.numpy as jnp
from jax import lax
from jax.experimental import pallas as pl
from jax.experimental.pallas import tpu as pltpu

F32 = jnp.float32
BF16 = jnp.bfloat16

N_DEV = 8
D_MODEL = 1024
RNN_BLOCK_W = 64
CONV_W = 4
LRU_C = 8.0
HEAD_DIM = 64
N_Q_HEADS = 16
N_KV_HEADS = 4
KV_W = N_KV_HEADS * HEAD_DIM
WINDOW = 128
ROPE_THETA = 10000.0
D_FF = 4096
PLE_DIM = 256
NORM_EPS = 1e-6
IN_TOTAL = 5632
COL_RNN_END, COL_ATTN_END = 2048, 3584
ATTN_W = COL_ATTN_END - COL_RNN_END
ATTN_K_AT, ATTN_V_AT = 1024, 1280

ADAM_LR = 0.001
ADAM_B1 = 0.9
ADAM_B2 = 0.999
ADAM_EPS = 1e-08
ADAM_WD = 0.01
ADAM_STEP = 10

LANES = 128
SUBLANES = 8
RNN_TILE = 256
VMEM_LIMIT = 48 * 1024 * 1024
NEG_BIG = -1e30


def _params(*sem):
    return pltpu.CompilerParams(dimension_semantics=sem if sem else None, vmem_limit_bytes=VMEM_LIMIT)


def _sig(x):
    return 0.5 * jnp.tanh(0.5 * x) + 0.5


def _dot_nt(a, b):
    return lax.dot_general(a, b, (((1,), (1,)), ((), ())), preferred_element_type=F32)


def _dot_tn(a, b):
    return lax.dot_general(a, b, (((0,), (0,)), ((), ())), preferred_element_type=F32)


class _Xfer:
    def __init__(self, start, wait):
        self.start, self.wait = start, wait


class _Hosted(NamedTuple):
    srcs: tuple
    out_shape: tuple
    n_sems: int
    plan: Callable
    aliases: tuple = ()


def _merge_hosted(parts):
    parts = [p for p in parts if p is not None]
    if len(parts) <= 1:
        return parts[0] if parts else None
    src_at, dst_at, sem_at, aliases = [0], [0], [0], []
    for p in parts:
        aliases += [(i + src_at[-1], j + dst_at[-1]) for i, j in p.aliases]
        src_at.append(src_at[-1] + len(p.srcs))
        dst_at.append(dst_at[-1] + len(p.out_shape))
        sem_at.append(sem_at[-1] + p.n_sems)

    def plan(src, dst, send_sems, recv_sems, local_sems, first_sem):
        copies = []
        for k, p in enumerate(parts):
            copies += p.plan(src[src_at[k]:src_at[k + 1]], dst[dst_at[k]:dst_at[k + 1]], send_sems, recv_sems,
                             local_sems, first_sem + sem_at[k])
        return copies

    return _Hosted(tuple(a for p in parts for a in p.srcs), tuple(s for p in parts for s in p.out_shape),
                   sem_at[-1], plan, tuple(aliases))


def _call(body, *, grid, in_specs, out_specs, out_shape, args, name, semantics, scratch_shapes=(), hosted=None):
    if hosted is None:
        outs = pl.pallas_call(body, grid=grid, in_specs=list(in_specs), out_specs=list(out_specs),
                              out_shape=list(out_shape), scratch_shapes=list(scratch_shapes),
                              compiler_params=_params(*semantics), name=name)(*args)
        return list(outs), []
    counts = (len(in_specs), len(hosted.srcs), len(out_specs), len(hosted.out_shape), len(scratch_shapes), 3)

    def wrapped(*refs):
        at, groups = 0, []
        for count in counts:
            groups.append(refs[at:at + count])
            at += count
        ins, srcs, outs, dsts, scratch, sems = groups
        copies = hosted.plan(srcs, dsts, *sems, 0)
        ids = [pl.program_id(axis) for axis in range(len(grid))]
        first = functools.reduce(jnp.logical_and, [i == 0 for i in ids])
        last = functools.reduce(jnp.logical_and, [i == g - 1 for i, g in zip(ids, grid)])

        @pl.when(first)
        def _():
            for cp in copies:
                cp.start()

        body(*ins, *outs, *scratch)

        @pl.when(last)
        def _():
            for cp in copies:
                cp.wait()

    any_spec = pl.BlockSpec(memory_space=pl.ANY)
    sems = [pltpu.SemaphoreType.DMA((hosted.n_sems,))] * 3
    outs = pl.pallas_call(
        wrapped, grid=grid, in_specs=list(in_specs) + [any_spec] * counts[1],
        out_specs=list(out_specs) + [any_spec] * counts[3], out_shape=list(out_shape) + list(hosted.out_shape),
        scratch_shapes=list(scratch_shapes) + sems, compiler_params=_params(*["arbitrary"] * len(grid)),
        input_output_aliases={counts[0] + i: counts[2] + j for i, j in hosted.aliases},
        name=name)(*args, *hosted.srcs)
    return list(outs[:counts[2]]), list(outs[counts[2]:])


def _dividing_tile(n, want):
    tile = min(want, n)
    while n % tile:
        tile -= LANES
    return tile


def _matmul(a, b, *, mode, tm, tn, out_dtypes, name, epilogue=None, extras=(), hosted=None, b_cols=None,
            row_vecs=(), n_row_sums=0, extra_col_blocks=None):
    a_parts = tuple(a) if isinstance(a, (tuple, list)) else (a,)
    b_parts = tuple(b) if isinstance(b, (tuple, list)) else (b,)
    assert len(a_parts) == len(b_parts) and (mode == "nt" or len(a_parts) == 1)
    n_parts = len(a_parts)
    m = a_parts[0].shape[0]
    if b_cols is None:
        b_cols = [(0, bp.shape[1]) for bp in b_parts]
    n = b_cols[0][1] if mode == "nn" else b_parts[0].shape[0]
    tm, tn = min(tm, m), _dividing_tile(n, tn)
    n_extra = len(extras) + len(row_vecs)
    n_tiles_out = len(out_dtypes)
    assert n_row_sums == 0 or n == tn

    def body(*refs):
        a_refs, b_refs = refs[:n_parts], refs[n_parts:2 * n_parts]
        rest = refs[2 * n_parts:]
        extra_refs, out_refs = rest[:n_extra], rest[n_extra:]
        if mode == "nn":
            acc = jnp.dot(a_refs[0][...], b_refs[0][...], preferred_element_type=F32)
        else:
            acc = _dot_nt(a_refs[0][...], b_refs[0][...])
            for a_ref, b_ref in zip(a_refs[1:], b_refs[1:]):
                acc = acc + _dot_nt(a_ref[...], b_ref[...])
        res = epilogue(acc, *[e[...] for e in extra_refs]) if epilogue is not None else (acc,)
        for o_ref, r in zip(out_refs[:n_tiles_out], res):
            o_ref[...] = r.astype(o_ref.dtype)
        if n_row_sums:
            @pl.when(pl.program_id(0) == 0)
            def _():
                for o_ref in out_refs[n_tiles_out:]:
                    o_ref[...] = jnp.zeros_like(o_ref)

            for o_ref, r in zip(out_refs[n_tiles_out:], res[n_tiles_out:]):
                o_ref[...] += r

    a_specs = [pl.BlockSpec((tm, ap.shape[1]), lambda i, j: (i, 0)) for ap in a_parts]
    if mode == "nn":
        assert b_cols[0][0] % tn == 0
        first = b_cols[0][0] // tn
        b_specs = [pl.BlockSpec((b_parts[0].shape[0], tn), lambda i, j: (0, first + j))]
    else:
        assert all(at % width == 0 for at, width in b_cols)
        b_specs = [pl.BlockSpec((tn, width), functools.partial(lambda i, j, blk: (j, blk), blk=at // width))
                   for at, width in b_cols]
    tile = pl.BlockSpec((tm, tn), lambda i, j: (i, j))
    row = pl.BlockSpec((1, tn), lambda i, j: (0, j))
    extra_specs = [pl.BlockSpec((tm, tn), functools.partial(lambda i, j, first: (i, first + j), first=first))
                   for first in (extra_col_blocks or [0] * len(extras))]
    outs, landed = _call(
        body,
        grid=(m // tm, n // tn),
        in_specs=a_specs + b_specs + extra_specs + [row] * len(row_vecs),
        out_specs=[tile] * n_tiles_out + [row] * n_row_sums,
        out_shape=[jax.ShapeDtypeStruct((m, n), dt) for dt in out_dtypes]
        + [jax.ShapeDtypeStruct((1, n), F32)] * n_row_sums,
        args=(*a_parts, *b_parts, *extras, *row_vecs), name=name,
        semantics=("arbitrary" if n_row_sums else "parallel", "arbitrary"), hosted=hosted)
    if hosted is not None:
        return (*outs, landed)
    return outs[0] if len(outs) == 1 else outs


def _matmul_tn(a, b, *, tk, tn, tt, name, slot_cols=None):
    t, k = a.shape
    n = b.shape[1]
    tk, tn, tt = min(tk, k), _dividing_tile(n, tn), min(tt, t)

    def body(a_ref, b_ref, o_ref):
        @pl.when(pl.program_id(2) == 0)
        def _():
            o_ref[...] = jnp.zeros_like(o_ref)

        if slot_cols is None:
            o_ref[...] += _dot_tn(a_ref[...], b_ref[...])
        else:
            av = a_ref[...]
            for s in range(tn // slot_cols):
                o_ref[s] += _dot_tn(av, b_ref[:, s * slot_cols:(s + 1) * slot_cols])

    if slot_cols is not None:
        out_spec = pl.BlockSpec((tn // slot_cols, tk, slot_cols), lambda i, j, s: (j, i, 0))
        out_shape = jax.ShapeDtypeStruct((n // slot_cols, k, slot_cols), F32)
    else:
        out_spec = pl.BlockSpec((tk, tn), lambda i, j, s: (i, j))
        out_shape = jax.ShapeDtypeStruct((k, n), F32)
    return pl.pallas_call(
        body,
        grid=(k // tk, n // tn, t // tt),
        in_specs=[pl.BlockSpec((tt, tk), lambda i, j, s: (s, i)), pl.BlockSpec((tt, tn), lambda i, j, s: (s, j))],
        out_specs=out_spec,
        out_shape=out_shape,
        compiler_params=_params("parallel", "parallel", "arbitrary"),
        name=name,
    )(a, b)


def _matmul_tn_multi(a, bs, *, tt, name, hosted=None):
    t, k = a.shape
    tt = min(tt, t)
    n_b = len(bs)

    def body(a_ref, *refs):
        b_refs, o_refs = refs[:n_b], refs[n_b:]

        @pl.when(pl.program_id(0) == 0)
        def _():
            for o_ref in o_refs:
                o_ref[...] = jnp.zeros_like(o_ref)

        a_t = a_ref[...].T
        for b_ref, o_ref in zip(b_refs, o_refs):
            o_ref[...] += jnp.dot(a_t, b_ref[...], preferred_element_type=F32)

    outs, landed = _call(
        body,
        grid=(t // tt,),
        in_specs=[pl.BlockSpec((tt, k), lambda s: (s, 0))] + [pl.BlockSpec((tt, b.shape[1]), lambda s: (s, 0)) for b in bs],
        out_specs=[pl.BlockSpec((k, b.shape[1]), lambda s: (0, 0)) for b in bs],
        out_shape=[jax.ShapeDtypeStruct((k, b.shape[1]), F32) for b in bs],
        args=(a, *bs), name=name, semantics=("arbitrary",), hosted=hosted)
    return (*outs, landed) if hosted is not None else outs


def _rmsnorm_rows(x, g):
    return x * lax.rsqrt(jnp.mean(x * x, axis=-1, keepdims=True) + NORM_EPS) * g


def _rmsnorm_fwd(x, g, *, name):
    t, d = x.shape
    tm = min(512, t)

    def body(x_ref, g_ref, o_ref):
        o_ref[...] = _rmsnorm_rows(x_ref[...], g_ref[...]).astype(BF16)

    return pl.pallas_call(
        body,
        grid=(t // tm,),
        in_specs=[pl.BlockSpec((tm, d), lambda i: (i, 0)), pl.BlockSpec((1, d), lambda i: (0, 0))],
        out_specs=pl.BlockSpec((tm, d), lambda i: (i, 0)),
        out_shape=jax.ShapeDtypeStruct((t, d), BF16),
        compiler_params=_params("parallel"),
        name=name,
    )(x, g)


def _rmsnorm_bwd_rows(dy, x, dres, g):
    r = lax.rsqrt(jnp.mean(x * x, axis=-1, keepdims=True) + NORM_EPS)
    xr = x * r
    gy = dy * g
    dx = dres + r * (gy - xr * jnp.mean(gy * xr, axis=-1, keepdims=True))
    return dx, jnp.sum(dy * xr, axis=0, keepdims=True)


def _softplus_neg(lam):
    z = -lam
    return jnp.maximum(z, 0.0) + jnp.log1p(jnp.exp(-jnp.abs(z)))


def _neg_expm1(y, exp_half_y):
    series = -y * (1.0 + y * 0.5 * (1.0 + y * (1.0 / 3.0) * (1.0 + y * 0.25 * (1.0 + y * 0.2))))
    return jnp.where(y > -0.0625, series, 1.0 - exp_half_y * exp_half_y)


def _gelu_parts(x):
    c = 0.7978845608028654
    u = c * (x + 0.044715 * x * x * x)
    th = jnp.tanh(u)
    gel = 0.5 * x * (1.0 + th)
    dgel = 0.5 * (1.0 + th) + 0.5 * x * (1.0 - th * th) * c * (1.0 + 3.0 * 0.044715 * x * x)
    return gel, dgel


def _shift_down(v, k, rows):
    return jnp.where(rows < k, 0.0, pltpu.roll(v, k, 0))


def _shift_up(v, k, rows, n):
    return jnp.where(rows >= n - k, 0.0, pltpu.roll(v, n - k, 0))


def _scan_within_groups(a, b, *, reverse):
    shape = a.shape
    a = a.reshape(shape[0] // SUBLANES, SUBLANES, shape[1])
    b = b.reshape(a.shape)
    in_group = lax.broadcasted_iota(jnp.int32, a.shape, 1)
    for s in (1, 2, 4):
        if reverse:
            inside, shift = in_group < SUBLANES - s, SUBLANES - s
        else:
            inside, shift = in_group >= s, s
        b = b + a * jnp.where(inside, pltpu.roll(b, shift, 1), 0.0)
        a = a * jnp.where(inside, pltpu.roll(a, shift, 1), 1.0)
    return a.reshape(shape), b.reshape(shape)


def _rnn_gates(xc, wrg, brg, wig, big, lam):
    xcb = xc.astype(BF16)
    r = _sig(jnp.dot(xcb, wrg, preferred_element_type=F32) + brg)
    i = _sig(jnp.dot(xcb, wig, preferred_element_type=F32) + big)
    sp = _softplus_neg(lam)
    log_a = -LRU_C * r * sp
    a = jnp.exp(log_a)
    mult = jnp.sqrt(_neg_expm1(2.0 * log_a, a))
    return xcb, r, i, sp, a, mult


def _conv_fwd(xv, cw, cb, rows):
    return (cb + _shift_down(xv, 3, rows) * cw[0:1, :] + _shift_down(xv, 2, rows) * cw[1:2, :]
            + _shift_down(xv, 1, rows) * cw[2:3, :] + xv * cw[3:4, :])


def _rnn_fwd(z, conv_w, conv_b, wrg_bd, b_rg, wig_bd, b_ig, lam, *, n_seq, seq, hosted=None):
    t = n_seq * seq
    ct = RNN_TILE
    n_ct = D_MODEL // ct

    def body(x_ref, g_ref, cw_ref, cb_ref, wrg_ref, brg_ref, wig_ref, big_ref, lam_ref,
             xc_ref, hr_ref, ya_ref, a_s, b_s):
        rows = lax.broadcasted_iota(jnp.int32, (seq, ct), 0)
        xc = _conv_fwd(x_ref[...], cw_ref[...], cb_ref[...], rows)
        _, r, i, sp, a, mult = _rnn_gates(xc, wrg_ref[...], brg_ref[...], wig_ref[...], big_ref[...], lam_ref[...])
        a_s[...], b_s[...] = _scan_within_groups(a, mult * (i * xc), reverse=False)

        def step(j, carry):
            r0 = pl.multiple_of(j * SUBLANES, SUBLANES)
            h = b_s[pl.ds(r0, SUBLANES), :] + a_s[pl.ds(r0, SUBLANES), :] * carry
            hr_ref[pl.ds(r0, SUBLANES), :] = h
            return h[SUBLANES - 1:SUBLANES, :]

        lax.fori_loop(0, seq // SUBLANES, step, jnp.zeros((1, ct), F32), unroll=4)
        gel, _ = _gelu_parts(g_ref[...])
        xc_ref[...] = xc
        ya_ref[...] = (hr_ref[...] * gel).astype(BF16)

    vec = pl.BlockSpec((1, ct), lambda b, c: (0, c))
    gate_w = pl.BlockSpec((None, ct, ct), lambda b, c: (c, 0, 0))
    tile = pl.BlockSpec((seq, ct), lambda b, c: (b, c))
    outs, landed = _call(
        body,
        grid=(n_seq, n_ct),
        in_specs=[
            pl.BlockSpec((seq, ct), lambda b, c: (b, c)),
            pl.BlockSpec((seq, ct), lambda b, c: (b, n_ct + c)),
            pl.BlockSpec((CONV_W, ct), lambda b, c: (0, c)), vec, gate_w, vec, gate_w, vec, vec,
        ],
        out_specs=[tile, tile, tile],
        out_shape=[jax.ShapeDtypeStruct((t, D_MODEL), F32), jax.ShapeDtypeStruct((t, D_MODEL), F32),
                   jax.ShapeDtypeStruct((t, D_MODEL), BF16)],
        scratch_shapes=[pltpu.VMEM((seq, ct), F32), pltpu.VMEM((seq, ct), F32)],
        args=(z, z, conv_w, conv_b, wrg_bd, b_rg, wig_bd, b_ig, lam), name="rnn_fwd",
        semantics=("parallel", "parallel"), hosted=hosted)
    return (*outs, landed) if hosted is not None else outs


def _rnn_bwd(dya, z, xc, hr, conv_w, wrg_bd, b_rg, wig_bd, b_ig, lam, *, n_seq, seq, hosted=None):
    t = n_seq * seq
    ct = RNN_TILE
    n_ct = D_MODEL // ct

    def body(dya_ref, x_ref, g_ref, xc_ref, hr_ref, cw_ref, wrg_ref, brg_ref, wig_ref, big_ref, lam_ref,
             dx_ref, dg_ref, dwrg_ref, dwig_ref, vec_ref, a_s, d_s, g_s):
        rows = lax.broadcasted_iota(jnp.int32, (seq, ct), 0)
        xv, xc, hr, dyv = x_ref[...], xc_ref[...], hr_ref[...], dya_ref[...]
        lamv = lam_ref[...]
        gel, dgel = _gelu_parts(g_ref[...])
        dg_ref[...] = (dyv * hr * dgel).astype(BF16)
        xcb, r, i, sp, a, mult = _rnn_gates(xc, wrg_ref[...], brg_ref[...], wig_ref[...], big_ref[...], lamv)
        a_s[...], d_s[...] = _scan_within_groups(_shift_up(a, 1, rows, seq), dyv * gel, reverse=True)

        def step(k, carry):
            r0 = pl.multiple_of((seq // SUBLANES - 1 - k) * SUBLANES, SUBLANES)
            gs = d_s[pl.ds(r0, SUBLANES), :] + a_s[pl.ds(r0, SUBLANES), :] * carry
            g_s[pl.ds(r0, SUBLANES), :] = gs
            return gs[0:1, :]

        lax.fori_loop(0, seq // SUBLANES, step, jnp.zeros((1, ct), F32), unroll=4)
        gsum = g_s[...]
        gated = i * xc
        d_log_a = gsum * _shift_down(hr, 1, rows) * a - gsum * gated * (a * a / mult)
        d_gated = gsum * mult
        d_pre_r = (d_log_a * (-LRU_C) * sp) * r * (1.0 - r)
        d_pre_i = (d_gated * xc) * i * (1.0 - i)
        dprb, dpib = d_pre_r.astype(BF16), d_pre_i.astype(BF16)
        dxc = d_gated * i + _dot_nt(dprb, wrg_ref[...]) + _dot_nt(dpib, wig_ref[...])
        cw = cw_ref[...]
        dx = (dxc * cw[3:4, :] + _shift_up(dxc, 1, rows, seq) * cw[2:3, :]
              + _shift_up(dxc, 2, rows, seq) * cw[1:2, :] + _shift_up(dxc, 3, rows, seq) * cw[0:1, :])
        dx_ref[...] = dx.astype(BF16)

        @pl.when(pl.program_id(1) == 0)
        def _():
            dwrg_ref[...] = jnp.zeros_like(dwrg_ref)
            dwig_ref[...] = jnp.zeros_like(dwig_ref)
            vec_ref[...] = jnp.zeros_like(vec_ref)

        dwrg_ref[...] += _dot_tn(xcb, dprb)
        dwig_ref[...] += _dot_tn(xcb, dpib)

        def colsum(v):
            return jnp.sum(v, axis=0, keepdims=True)

        d_sp = colsum(d_log_a * (-LRU_C) * r)
        vec_ref[0:1, :] += colsum(d_pre_r)
        vec_ref[1:2, :] += colsum(d_pre_i)
        vec_ref[2:3, :] += d_sp * (-_sig(-lamv))
        vec_ref[3:4, :] += colsum(dxc)
        vec_ref[4:5, :] += colsum(dxc * _shift_down(xv, 3, rows))
        vec_ref[5:6, :] += colsum(dxc * _shift_down(xv, 2, rows))
        vec_ref[6:7, :] += colsum(dxc * _shift_down(xv, 1, rows))
        vec_ref[7:8, :] += colsum(dxc * xv)

    vec = pl.BlockSpec((1, ct), lambda c, b: (0, c))
    gate_w = pl.BlockSpec((None, ct, ct), lambda c, b: (c, 0, 0))
    tile = pl.BlockSpec((seq, ct), lambda c, b: (b, c))
    outs, landed = _call(
        body,
        grid=(n_ct, n_seq),
        in_specs=[
            tile,
            pl.BlockSpec((seq, ct), lambda c, b: (b, c)),
            pl.BlockSpec((seq, ct), lambda c, b: (b, n_ct + c)),
            tile, tile,
            pl.BlockSpec((CONV_W, ct), lambda c, b: (0, c)), gate_w, vec, gate_w, vec, vec,
        ],
        out_specs=[tile, tile, gate_w, gate_w, pl.BlockSpec((8, ct), lambda c, b: (0, c))],
        out_shape=[jax.ShapeDtypeStruct((t, D_MODEL), BF16), jax.ShapeDtypeStruct((t, D_MODEL), BF16),
                   jax.ShapeDtypeStruct((n_ct, ct, ct), F32), jax.ShapeDtypeStruct((n_ct, ct, ct), F32),
                   jax.ShapeDtypeStruct((8, D_MODEL), F32)],
        scratch_shapes=[pltpu.VMEM((seq, ct), F32)] * 3,
        args=(dya, z, z, xc, hr, conv_w, wrg_bd, b_rg, wig_bd, b_ig, lam), name="rnn_bwd",
        semantics=("parallel", "arbitrary"), hosted=hosted)
    return (*outs, landed) if hosted is not None else outs


def _split_hi_lo(x):
    hi = x.astype(BF16)
    return hi, (x - hi.astype(F32)).astype(BF16)


def _dot_split(x, m_twice):
    hi, lo = _split_hi_lo(x)
    return jnp.dot(jnp.concatenate([hi, lo], axis=1), m_twice, preferred_element_type=F32)


def _head_matrices(width):
    ec = ((lax.broadcasted_iota(jnp.int32, (2 * width, LANES), 0) & (width - 1)) // HEAD_DIM
          == lax.broadcasted_iota(jnp.int32, (2 * width, LANES), 1))
    ee = (lax.broadcasted_iota(jnp.int32, (2 * LANES, width), 1) // HEAD_DIM
          == (lax.broadcasted_iota(jnp.int32, (2 * LANES, width), 0) & (LANES - 1)))
    return jnp.where(ec, 1.0, 0.0).astype(BF16), jnp.where(ee, 1.0, 0.0).astype(BF16)


def _swap_halves(y):
    w = y.shape[1]
    first = (lax.broadcasted_iota(jnp.int32, y.shape, 1) % HEAD_DIM) < HEAD_DIM // 2
    return jnp.where(first, pltpu.roll(y, w - HEAD_DIM // 2, 1), pltpu.roll(y, HEAD_DIM // 2, 1))


def _normrope_fwd(x, gain, cos_t, sin_t, ec, ee):
    w = x.shape[1]
    rs = _dot_split(lax.rsqrt(_dot_split(x * x, ec) * (1.0 / HEAD_DIM) + NORM_EPS), ee)
    nx = x * rs
    y = nx * gain
    reps = w // LANES
    out = y * jnp.tile(cos_t, (1, reps)) + _swap_halves(y) * jnp.tile(sin_t, (1, reps))
    return out, nx, rs


def _normrope_bwd(dout, nx, rs, gain, cos_t, sin_t, ec, ee):
    w = dout.shape[1]
    reps = w // LANES
    dy = dout * jnp.tile(cos_t, (1, reps)) + _swap_halves(dout * jnp.tile(sin_t, (1, reps)))
    dgain = jnp.sum(dy * nx, axis=0, keepdims=True)
    dn = dy * gain
    seg = _dot_split(_dot_split(dn * nx, ec) * (1.0 / HEAD_DIM), ee)
    return rs * (dn - nx * seg), dgain


def _pair_operand(t, group):
    chunk = t[:, (group // 2) * LANES:(group // 2 + 1) * LANES]
    low = lax.broadcasted_iota(jnp.int32, chunk.shape, 1) < HEAD_DIM
    rolled = pltpu.roll(chunk, HEAD_DIM, 1)
    return jnp.where(low, chunk, rolled) if group % 2 == 0 else jnp.where(low, rolled, chunk)


GROUP = N_Q_HEADS // N_KV_HEADS
GROUP_W = GROUP * HEAD_DIM


def _replicate_head(t, group):
    return jnp.tile(_pair_operand(t, group), (1, 2))


def _head_blocks(t):
    seg = lax.broadcasted_iota(jnp.int32, t.shape, 1) // HEAD_DIM
    return jnp.concatenate([jnp.where(seg == h, t, 0.0) for h in range(GROUP)], axis=0)


def _stack_heads(t_t, rows):
    return jnp.concatenate([t_t[:, h * rows:(h + 1) * rows] for h in range(GROUP)], axis=0)


def _head_rows(mat_t, group):
    return jnp.concatenate([mat_t[GROUP * group + h:GROUP * group + h + 1, :] for h in range(GROUP)], axis=1)


def _window_masks(blk):
    key = lax.broadcasted_iota(jnp.int32, (blk, GROUP * blk), 0)
    query = lax.broadcasted_iota(jnp.int32, (blk, GROUP * blk), 1) & (blk - 1)
    return key > query, key <= query


def _mask_window(t, before_ok, own_ok, fill):
    blk = t.shape[0] // 2
    return jnp.concatenate([jnp.where(before_ok, t[:blk], fill), jnp.where(own_ok, t[blk:], fill)], axis=0)


def _attn_fwd(z, cos_t, sin_t, q_gain_t, k_gain_t, sinks_t, *, n_seq, seq, hosted=None):
    t = n_seq * seq
    blk = WINDOW
    nb = seq // blk

    def body(q_ref, kp_ref, kc_ref, vp_ref, vc_ref, cosc_ref, sinc_ref, cosp_ref, sinp_ref, qg_ref, kg_ref, sk_ref,
             o_ref, l_ref):
        n = pl.program_id(1)
        ecq, eeq = _head_matrices(D_MODEL)
        eck, eek = _head_matrices(KV_W)
        cosc, sinc = cosc_ref[...], sinc_ref[...]
        qh, _, _ = _normrope_fwd(q_ref[...], qg_ref[...], cosc, sinc, ecq, eeq)
        qh = qh * (HEAD_DIM ** -0.5)
        kc, _, _ = _normrope_fwd(kc_ref[...], kg_ref[...], cosc, sinc, eck, eek)
        kp, _, _ = _normrope_fwd(kp_ref[...], kg_ref[...], cosp_ref[...], sinp_ref[...], eck, eek)
        kcat = jnp.concatenate([kp, kc], axis=0)
        vcat = jnp.concatenate([vp_ref[...], vc_ref[...]], axis=0)
        above, causal = _window_masks(blk)
        above = above & (n > 0)
        head_row = lax.broadcasted_iota(jnp.int32, (blk, blk), 0)
        sk_t = jnp.broadcast_to(sk_ref[...], (blk, LANES)).T
        vcat_t = vcat.T.astype(BF16)
        lmat = jnp.zeros((blk, blk), F32)
        groups = range(N_KV_HEADS)
        cols = [slice(g * GROUP_W, (g + 1) * GROUP_W) for g in groups]
        scores = [_dot_nt(_replicate_head(kcat, g).astype(BF16), _head_blocks(qh[:, cols[g]]).astype(BF16))
                  for g in groups]
        probs = []
        for g in groups:
            s = _mask_window(scores[g], above, causal, NEG_BIG)
            sink = _head_rows(sk_t, g)
            m = jnp.maximum(jnp.max(s, axis=0, keepdims=True), sink)
            e = jnp.exp(s - m)
            den = jnp.sum(e, axis=0, keepdims=True) + jnp.exp(sink - m)
            probs.append((e * (1.0 / den)).astype(BF16))
            lse = m + jnp.log(den)
            for h in range(GROUP):
                lmat = lmat + jnp.where(head_row == GROUP * g + h, lse[:, h * blk:(h + 1) * blk], 0.0)
        for g in groups:
            out_t = jnp.dot(vcat_t[g * HEAD_DIM:(g + 1) * HEAD_DIM], probs[g], preferred_element_type=F32)
            o_ref[:, cols[g]] = _stack_heads(out_t, blk).T.astype(BF16)
        l_ref[...] = lmat

    def row(b, n):
        return b * nb + n

    def prev(b, n):
        return b * nb + jnp.maximum(n - 1, 0)

    kw = KV_W
    tab_c = pl.BlockSpec((blk, LANES), lambda b, n: (n, 0))
    tab_p = pl.BlockSpec((blk, LANES), lambda b, n: (jnp.maximum(n - 1, 0), 0))
    outs, landed = _call(
        body,
        grid=(n_seq, nb),
        in_specs=[
            pl.BlockSpec((blk, D_MODEL), lambda b, n: (row(b, n), 0)),
            pl.BlockSpec((blk, kw), lambda b, n: (prev(b, n), ATTN_K_AT // kw)),
            pl.BlockSpec((blk, kw), lambda b, n: (row(b, n), ATTN_K_AT // kw)),
            pl.BlockSpec((blk, kw), lambda b, n: (prev(b, n), ATTN_V_AT // kw)),
            pl.BlockSpec((blk, kw), lambda b, n: (row(b, n), ATTN_V_AT // kw)),
            tab_c, tab_c, tab_p, tab_p,
            pl.BlockSpec((1, D_MODEL), lambda b, n: (0, 0)),
            pl.BlockSpec((1, kw), lambda b, n: (0, 0)),
            pl.BlockSpec((1, LANES), lambda b, n: (0, 0)),
        ],
        out_specs=[pl.BlockSpec((blk, D_MODEL), lambda b, n: (row(b, n), 0)),
                   pl.BlockSpec((blk, LANES), lambda b, n: (row(b, n), 0))],
        out_shape=[jax.ShapeDtypeStruct((t, D_MODEL), BF16), jax.ShapeDtypeStruct((t, LANES), F32)],
        args=(z, z, z, z, z, cos_t, sin_t, cos_t, sin_t, q_gain_t, k_gain_t, sinks_t), name="attn_fwd",
        semantics=("parallel", "parallel"), hosted=hosted)
    return (*outs, landed) if hosted is not None else outs


def _attn_bwd(z, o, lse, do, cos_t, sin_t, q_gain_t, k_gain_t, sinks_t, *, n_seq, seq, hosted=None):
    t = n_seq * seq
    blk = WINDOW
    nb = seq // blk
    kw = KV_W
    scale = HEAD_DIM ** -0.5

    def body(qc_ref, qn_ref, kc_ref, vp_ref, vc_ref, oc_ref, on_ref, doc_ref, don_ref, lc_ref, ln_ref,
             cosc_ref, sinc_ref, cosn_ref, sinn_ref, qg_ref, kg_ref, sk_ref,
             dz_ref, vec_ref, dq_s, q_s, k_s):
        n = pl.program_id(1)
        ecq, eeq = _head_matrices(D_MODEL)
        eck, eek = _head_matrices(KV_W)
        cosc, sinc = cosc_ref[...], sinc_ref[...]
        qg, kg = qg_ref[...], kg_ref[...]
        own, other = n & 1, 1 - (n & 1)

        @pl.when(n == 0)
        def _():
            for part, value in enumerate(_normrope_fwd(qc_ref[...], qg, cosc, sinc, ecq, eeq)):
                q_s[own, part] = value
            k_s[other] = jnp.zeros((blk, kw), F32)

        for part, value in enumerate(_normrope_fwd(qn_ref[...], qg, cosn_ref[...], sinn_ref[...], ecq, eeq)):
            q_s[other, part] = value
        qhc, nqc, rsqc = q_s[own, 0], q_s[own, 1], q_s[own, 2]
        qhn = q_s[other, 0]
        khc, nkc, rskc = _normrope_fwd(kc_ref[...], kg, cosc, sinc, eck, eek)
        khp = k_s[other]
        k_s[own] = khc
        doc = doc_ref[...].astype(F32)
        don = don_ref[...].astype(F32)
        delc = _dot_split(doc * oc_ref[...].astype(F32), ecq)
        deln = _dot_split(don * on_ref[...].astype(F32), ecq)
        lc_t, ln_t, delc_t, deln_t = lc_ref[...], ln_ref[...], delc.T, deln.T
        above, causal = _window_masks(blk)
        above_c, above_n = above & (n > 0), above & (n < nb - 1)
        seg = lax.broadcasted_iota(jnp.int32, (blk, GROUP_W), 1) // HEAD_DIM
        lane = lax.broadcasted_iota(jnp.int32, (1, LANES), 1)
        sk_t = jnp.broadcast_to(sk_ref[...], (blk, LANES)).T
        dsink = jnp.zeros((1, LANES), F32)
        kcat = jnp.concatenate([khp, khc], axis=0)
        vcat = jnp.concatenate([vp_ref[...], vc_ref[...]], axis=0)
        kcat_t = kcat.T.astype(BF16)
        dkh = jnp.zeros((blk, GROUP_W), F32)
        dvh = jnp.zeros((blk, GROUP_W), F32)

        def fold_to(group, t):
            total = t + pltpu.roll(t, HEAD_DIM, 1)
            total = total + pltpu.roll(total, 2 * HEAD_DIM, 1)
            return jnp.where(seg == group, total, 0.0)

        groups = range(N_KV_HEADS)
        cols = [slice(g * GROUP_W, (g + 1) * GROUP_W) for g in groups]
        qsc, qsn = qhc * scale, qhn * scale
        qb_c = [_head_blocks(qsc[:, cols[g]]).astype(BF16) for g in groups]
        qb_n = [_head_blocks(qsn[:, cols[g]]).astype(BF16) for g in groups]
        dob_c = [_head_blocks(doc[:, cols[g]]).astype(BF16) for g in groups]
        dob_n = [_head_blocks(don[:, cols[g]]).astype(BF16) for g in groups]
        raw = []
        for g in groups:
            krep = _replicate_head(kcat, g).astype(BF16)
            vrep = _replicate_head(vcat, g).astype(BF16)
            raw.append((_dot_nt(krep, qb_c[g]), _dot_nt(vrep, dob_c[g]),
                        _dot_nt(krep[blk:], qb_n[g]), _dot_nt(vrep[blk:], dob_n[g])))
        cooked = []
        for g in groups:
            s_c, dp_c, s_n, dp_n = raw[g]
            l_row, d_row = _head_rows(lc_t, g), _head_rows(delc_t, g)
            p_c = _mask_window(jnp.exp(s_c - l_row), above_c, causal, 0.0)
            ds_c = (p_c * (dp_c - d_row)).astype(BF16)
            p_n = jnp.where(above_n, jnp.exp(s_n - _head_rows(ln_t, g)), 0.0)
            ds_n = (p_n * (dp_n - _head_rows(deln_t, g))).astype(BF16)
            cooked.append((p_c[blk:].astype(BF16), ds_c, p_n.astype(BF16), ds_n))
            p_sink = jnp.exp(_head_rows(sk_t, g) - l_row) * d_row
            for h in range(GROUP):
                dsink = dsink + jnp.where(lane == GROUP * g + h,
                                          -jnp.sum(p_sink[:, h * blk:(h + 1) * blk], axis=1, keepdims=True), 0.0)
        for g in groups:
            p_cb, ds_c, p_nb, ds_n = cooked[g]
            dq_t = jnp.dot(kcat_t[g * HEAD_DIM:(g + 1) * HEAD_DIM], ds_c, preferred_element_type=F32)
            dq_s[:, cols[g]] = _stack_heads(dq_t, blk).T * scale
            dk_rep = (jnp.dot(ds_c[blk:], qb_c[g], preferred_element_type=F32)
                      + jnp.dot(ds_n, qb_n[g], preferred_element_type=F32))
            dv_rep = (jnp.dot(p_cb, dob_c[g], preferred_element_type=F32)
                      + jnp.dot(p_nb, dob_n[g], preferred_element_type=F32))
            dkh = dkh + fold_to(g, dk_rep)
            dvh = dvh + fold_to(g, dv_rep)
        dq, dqg = _normrope_bwd(dq_s[...], nqc, rsqc, qg, cosc, sinc, ecq, eeq)
        dk, dkg = _normrope_bwd(dkh, nkc, rskc, kg, cosc, sinc, eck, eek)
        dz_ref[:, :ATTN_K_AT] = dq.astype(BF16)
        dz_ref[:, ATTN_K_AT:ATTN_V_AT] = dk.astype(BF16)
        dz_ref[:, ATTN_V_AT:] = dvh.astype(BF16)

        @pl.when(n == 0)
        def _():
            vec_ref[...] = jnp.zeros_like(vec_ref)

        vec_ref[0:1, :] += dqg
        vec_ref[1:2, 0:kw] += dkg
        vec_ref[2:3, 0:LANES] += dsink

    def row(b, n):
        return b * nb + n

    def prev(b, n):
        return b * nb + jnp.maximum(n - 1, 0)

    def nxt(b, n):
        return b * nb + jnp.minimum(n + 1, nb - 1)

    def tiles(width, col, which):
        return pl.BlockSpec((blk, width), lambda b, n: (which(b, n), col))

    def table(which):
        return pl.BlockSpec((blk, LANES), lambda b, n: (which(0, n), 0))

    outs, landed = _call(
        body,
        grid=(n_seq, nb),
        in_specs=[
            tiles(D_MODEL, 0, row), tiles(D_MODEL, 0, nxt),
            tiles(kw, ATTN_K_AT // kw, row),
            tiles(kw, ATTN_V_AT // kw, prev), tiles(kw, ATTN_V_AT // kw, row),
            tiles(D_MODEL, 0, row), tiles(D_MODEL, 0, nxt),
            tiles(D_MODEL, 0, row), tiles(D_MODEL, 0, nxt),
            tiles(LANES, 0, row), tiles(LANES, 0, nxt),
            table(row), table(row), table(nxt), table(nxt),
            pl.BlockSpec((1, D_MODEL), lambda b, n: (0, 0)),
            pl.BlockSpec((1, kw), lambda b, n: (0, 0)),
            pl.BlockSpec((1, LANES), lambda b, n: (0, 0)),
        ],
        out_specs=[tiles(ATTN_W, 0, row), pl.BlockSpec((None, 8, D_MODEL), lambda b, n: (b, 0, 0))],
        out_shape=[jax.ShapeDtypeStruct((t, ATTN_W), BF16), jax.ShapeDtypeStruct((n_seq, 8, D_MODEL), F32)],
        scratch_shapes=[pltpu.VMEM((blk, D_MODEL), F32), pltpu.VMEM((2, 3, blk, D_MODEL), F32),
                        pltpu.VMEM((2, blk, kw), F32)],
        args=(z, z, z, z, z, o, o, do, do, lse, lse, cos_t, sin_t, cos_t, sin_t,
              q_gain_t, k_gain_t, sinks_t), name="attn_bwd", semantics=("arbitrary", "arbitrary"), hosted=hosted)
    return (*outs, landed) if hosted is not None else outs


MERGE_COLS = 512


def _merge_fwd(z, ya, yb):
    t = ya.shape[0]
    tm, tc = min(512, t), MERGE_COLS

    def body(ga_ref, gb_ref, ya_ref, yb_ref, o_ref):
        o_ref[...] = (_sig(ga_ref[...]) * ya_ref[...] + _sig(gb_ref[...]) * yb_ref[...]).astype(BF16)

    tile = pl.BlockSpec((tm, tc), lambda i, j: (i, j))
    return pl.pallas_call(
        body,
        grid=(t // tm, D_MODEL // tc),
        in_specs=[pl.BlockSpec((tm, tc), lambda i, j: (i, j)),
                  pl.BlockSpec((tm, tc), lambda i, j: (i, D_MODEL // tc + j)), tile, tile],
        out_specs=tile,
        out_shape=jax.ShapeDtypeStruct((t, D_MODEL), BF16),
        compiler_params=_params("parallel", "parallel"),
        name="merge_fwd",
    )(z, z, ya, yb)


def _rope_tables(seq):
    inv = ROPE_THETA ** (-jnp.arange(0, HEAD_DIM, 2, dtype=F32) / HEAD_DIM)
    ang = jnp.arange(seq, dtype=F32)[:, None] * inv[None, :]
    cos, sin = jnp.cos(ang), jnp.sin(ang)
    return jnp.tile(jnp.concatenate([cos, cos], axis=1), (1, 2)), jnp.tile(jnp.concatenate([-sin, sin], axis=1), (1, 2))


def _block_diag_tiles(w):
    per = RNN_TILE // RNN_BLOCK_W
    w4 = w.reshape(D_MODEL // RNN_TILE, per, RNN_BLOCK_W, RNN_BLOCK_W)
    eye = jnp.eye(per, dtype=w.dtype)
    dense = jnp.einsum("tpij,pq->tpiqj", w4, eye)
    return dense.reshape(D_MODEL // RNN_TILE, RNN_TILE, RNN_TILE).astype(BF16)


def _block_diag_extract(dense):
    per = RNN_TILE // RNN_BLOCK_W
    d5 = dense.reshape(D_MODEL // RNN_TILE, per, RNN_BLOCK_W, per, RNN_BLOCK_W)
    blocks = jnp.stack([d5[:, p, :, p, :] for p in range(per)], axis=1)
    return blocks.reshape(D_MODEL // RNN_BLOCK_W, RNN_BLOCK_W, RNN_BLOCK_W)


def _local_step(x, p, target, w, *, n_seq, seq, comm=None):
    w = dict(w)

    def run(tag, fn, *args, **kwargs):
        hosted = comm.host(tag) if comm is not None else None
        if hosted is None:
            return fn(*args, **kwargs)
        *outs, landed = fn(*args, hosted=hosted, **kwargs)
        comm.landed(tag, landed, w)
        return outs[0] if len(outs) == 1 else outs

    def ready(batch, grads, extra=None):
        if comm is not None:
            comm.ready(batch, grads, extra)

    cos_t, sin_t = _rope_tables(seq)
    q_gain_t = jnp.tile(w["q_gain"], (1, N_Q_HEADS))
    k_gain_t = jnp.tile(w["k_gain"], (1, N_KV_HEADS))
    sinks_t = jnp.pad(w["sinks"], ((0, 0), (0, LANES - N_Q_HEADS)))
    wrg_bd, wig_bd = _block_diag_tiles(w["w_rg"]), _block_diag_tiles(w["w_ig"])
    dims = dict(n_seq=n_seq, seq=seq)

    h = _rmsnorm_fwd(x, w["g_mix"], name="norm_mix")
    z_rnn = run("mm_in_rnn", _matmul, h, w["w_in"], mode="nn", tm=1024, tn=1024, out_dtypes=[F32],
                name="mm_in_rnn", b_cols=[(0, COL_RNN_END)])
    w_in_attn, w_in_gate = w["w_in"][:, COL_RNN_END:COL_ATTN_END], w["w_in"][:, COL_ATTN_END:]
    z_attn = run("mm_in_attn", _matmul, h, w_in_attn, mode="nn", tm=1024, tn=1024, out_dtypes=[F32],
                 name="mm_in_attn")
    z_gate = run("mm_in_gate", _matmul, h, w_in_gate, mode="nn", tm=1024, tn=1024, out_dtypes=[F32],
                 name="mm_in_gate")
    xc, hr, ya_in = run("rnn_fwd", _rnn_fwd, z_rnn, w["conv_w"], w["conv_b"], wrg_bd, w["b_rg"], wig_bd, w["b_ig"],
                        w["lru_lambda"], **dims)
    o, lse = run("attn_fwd", _attn_fwd, z_attn, cos_t, sin_t, q_gain_t, k_gain_t, sinks_t, **dims)
    ya = run("mm_rnn_proj", _matmul, ya_in, w["w_rnn_proj"], mode="nn", tm=1024, tn=1024, out_dtypes=[F32],
             name="mm_rnn_proj")
    yb = _matmul(o, w["w_attn_proj"], mode="nn", tm=1024, tn=1024, out_dtypes=[F32], name="mm_attn_proj")
    merged = _merge_fwd(z_gate, ya, yb)
    def residual_then_norm(acc, res, gain):
        new = res + acc
        return new, _rmsnorm_rows(new, gain)

    x1, hm = _matmul(merged, w["w_out"], mode="nn", tm=512, tn=1024, out_dtypes=[F32, BF16], name="mm_out",
                     epilogue=residual_then_norm, extras=(x,), row_vecs=(w["g_mlp"],))
    act = _matmul(hm, w["w_up"], mode="nn", tm=1024, tn=1024, out_dtypes=[BF16], name="mm_up",
                  epilogue=lambda acc: (jnp.square(jnp.maximum(acc, 0.0)),))
    x2, hp = _matmul(act, w["w_down"], mode="nn", tm=512, tn=1024, out_dtypes=[F32, BF16], name="mm_down",
                     epilogue=residual_then_norm, extras=(x1,), row_vecs=(w["g_ple"],))
    p_bf = p.astype(BF16)
    e = _matmul(p_bf, w["w_ple_proj"], mode="nn", tm=1024, tn=1024, out_dtypes=[F32], name="mm_ple_proj")

    def loss_head(gt, x2v, ev, tgt):
        sg = _sig(gt)
        diff = x2v + ev * sg - tgt
        dx = diff * (1.0 / D_MODEL)
        return dx, dx * ev * sg * (1.0 - sg), dx * sg, jnp.sum(diff * diff, axis=0, keepdims=True)

    dx3, dgt, de, loss_row = _matmul(hp, w["w_ple_gate"], mode="nn", tm=512, tn=1024, out_dtypes=[F32, BF16, BF16],
                                     name="mm_ple_gate", epilogue=loss_head, extras=(x2, e, target), n_row_sums=1)

    g = {}
    g["w_ple_proj"] = _matmul_tn(p_bf, de, tk=PLE_DIM, tn=1024, tt=1024, name="mm_d_ple_proj",
                                 slot_cols=D_MODEL // N_DEV)
    g["w_ple_gate"] = _matmul_tn(hp, dgt, tk=1024, tn=1024, tt=1024, name="mm_d_ple_gate")
    def through_norm(dy, xv, dres, gain):
        dx, dgain = _rmsnorm_bwd_rows(dy, xv, dres, gain)
        return dx, dx, dgain

    dx2, dx2_bf, g["g_ple"] = _matmul(
        dgt, w["w_ple_gate"], mode="nt", tm=512, tn=1024, out_dtypes=[F32, BF16], name="mm_dhp",
        epilogue=through_norm, extras=(x2, dx3), row_vecs=(w["g_ple"],), n_row_sums=1)
    g["w_down"] = _matmul_tn(act, dx2_bf, tk=1024, tn=1024, tt=1024, name="mm_d_down")
    du = _matmul(dx2_bf, w["w_down"], mode="nt", tm=1024, tn=1024, out_dtypes=[BF16], name="mm_dact",
                 epilogue=lambda acc, a: (acc * (2.0 * jnp.sqrt(a.astype(F32))),), extras=(act,))
    g["w_up"] = _matmul_tn(hm, du, tk=1024, tn=1024, tt=1024, name="mm_d_up", slot_cols=D_FF // N_DEV)
    ready(1, g)
    dx1, dx1_bf, g["g_mlp"] = run(
        "mm_dhm", _matmul, du, w["w_up"], mode="nt", tm=512, tn=1024, out_dtypes=[F32, BF16], name="mm_dhm",
        epilogue=through_norm, extras=(x1, dx2), row_vecs=(w["g_mlp"],), n_row_sums=1)
    g["w_out"] = _matmul_tn(merged, dx1_bf, tk=1024, tn=1024, tt=1024, name="mm_d_out")
    def merge_bwd(dm, ga, gb, yav, ybv):
        sa, sb = _sig(ga), _sig(gb)
        return dm * sa, dm * sb, dm * yav * sa * (1.0 - sa), dm * ybv * sb * (1.0 - sb)

    dya, dyb, dga, dgb = _matmul(dx1_bf, w["w_out"], mode="nt", tm=512, tn=1024, out_dtypes=[BF16] * 4,
                                 name="mm_dmerged", epilogue=merge_bwd, extras=(z_gate, z_gate, ya, yb),
                                 extra_col_blocks=(0, 1, 0, 0))
    g["w_rnn_proj"] = _matmul_tn(ya_in, dya, tk=1024, tn=1024, tt=1024, name="mm_d_rnn_proj")
    g["w_attn_proj"] = _matmul_tn(o, dyb, tk=1024, tn=1024, tt=1024, name="mm_d_attn_proj")
    ready(2, g)
    dya_in = run("mm_dya_in", _matmul, dya, w["w_rnn_proj"], mode="nt", tm=1024, tn=1024, out_dtypes=[F32],
                 name="mm_dya_in")
    do = _matmul(dyb, w["w_attn_proj"], mode="nt", tm=1024, tn=1024, out_dtypes=[BF16], name="mm_do")
    dx_rnn, dg_rnn, dwrg_dense, dwig_dense, rnn_vec = run(
        "rnn_bwd", _rnn_bwd, dya_in, z_rnn, xc, hr, w["conv_w"], wrg_bd, w["b_rg"], wig_bd, w["b_ig"],
        w["lru_lambda"], **dims)
    dz_attn, attn_vec = run("attn_bwd", _attn_bwd, z_attn, o, lse, do, cos_t, sin_t, q_gain_t, k_gain_t, sinks_t,
                            **dims)
    dz_parts = (dx_rnn, dg_rnn, dz_attn, dga, dgb)
    g["w_rg"] = _block_diag_extract(dwrg_dense)
    g["w_ig"] = _block_diag_extract(dwig_dense)
    g["b_rg"], g["b_ig"], g["lru_lambda"], g["conv_b"] = (rnn_vec[i:i + 1] for i in range(4))
    g["conv_w"] = rnn_vec[4:8]
    attn_vec = attn_vec[0] if n_seq == 1 else functools.reduce(jnp.add, [attn_vec[b] for b in range(n_seq)])
    g["q_gain"] = attn_vec[0].reshape(N_Q_HEADS, HEAD_DIM).sum(axis=0)[None, :]
    g["k_gain"] = attn_vec[1, :KV_W].reshape(N_KV_HEADS, HEAD_DIM).sum(axis=0)[None, :]
    g["sinks"] = attn_vec[2:3, :N_Q_HEADS]
    ready(SMALL_BATCH, g, {LOSS_ROW: loss_row})
    g["w_in"] = jnp.concatenate(
        list(run("mm_d_in_rnn", _matmul_tn_multi, h, dz_parts[:2], tt=1024, name="mm_d_in_rnn"))
        + list(run("mm_d_in_rest", _matmul_tn_multi, h, dz_parts[2:], tt=512, name="mm_d_in_rest")), axis=1)
    ready(3, g)
    windows = ((w["w_in"], (0, D_MODEL)), (w["w_in"], (D_MODEL, D_MODEL)), (w_in_attn, (0, ATTN_W)),
               (w_in_gate, (0, D_MODEL)), (w_in_gate, (D_MODEL, D_MODEL)))
    grad_x, g["g_mix"] = run(
        "mm_dh", _matmul, dz_parts, [wd[0] for wd in windows], mode="nt", tm=256, tn=1024, out_dtypes=[F32],
        name="mm_dh", b_cols=[wd[1] for wd in windows], epilogue=_rmsnorm_bwd_rows, extras=(x, dx1),
        row_vecs=(w["g_mix"],), n_row_sums=1)
    return jnp.sum(loss_row), grad_x, g


MESH_ID = pl.DeviceIdType.MESH


def _coords(index):
    return (index >> 2) & 1, (index >> 1) & 1, index & 1


def _exchange(srcs, kinds, *, name):
    n = len(srcs)
    n_peer = N_DEV - 1

    def body(*refs):
        src, dst = refs[:n], refs[n:2 * n]
        send_sems, recv_sems, local_sems = refs[2 * n:]
        me = 4 * lax.axis_index("x") + 2 * lax.axis_index("y") + lax.axis_index("c")

        def remote(i, d):
            peer = (me + d) & (N_DEV - 1)
            piece = src[i] if kinds[i] == "gather" else src[i].at[peer]
            return pltpu.make_async_remote_copy(
                src_ref=piece, dst_ref=dst[i].at[me], send_sem=send_sems.at[i * n_peer + d - 1],
                recv_sem=recv_sems.at[i * n_peer + d - 1], device_id=_coords(peer), device_id_type=MESH_ID)

        def arrival(i, d):
            sender = (me - d) & (N_DEV - 1)
            piece = src[i] if kinds[i] == "gather" else src[i].at[sender]
            return pltpu.make_async_remote_copy(
                src_ref=piece, dst_ref=dst[i].at[sender], send_sem=send_sems.at[i * n_peer + d - 1],
                recv_sem=recv_sems.at[i * n_peer + d - 1], device_id=_coords(sender), device_id_type=MESH_ID)

        own = []
        for i in range(n):
            piece = src[i] if kinds[i] == "gather" else src[i].at[me]
            own.append(pltpu.make_async_copy(piece, dst[i].at[me], local_sems.at[i]))
            own[-1].start()
        sent = [remote(i, d) for d in range(1, N_DEV) for i in range(n)]
        for cp in sent:
            cp.start()
        for d in range(1, N_DEV):
            for i in range(n):
                arrival(i, d).wait_recv()
        for cp in sent:
            cp.wait_send()
        for cp in own:
            cp.wait()

    def out_of(s, kind):
        shape = s.shape if kind == "scatter" else (N_DEV,) + s.shape
        return jax.ShapeDtypeStruct(shape, s.dtype)

    any_spec = pl.BlockSpec(memory_space=pl.ANY)
    return pl.pallas_call(
        body,
        in_specs=[any_spec] * n,
        out_specs=[any_spec] * n,
        out_shape=[out_of(s, k) for s, k in zip(srcs, kinds)],
        scratch_shapes=[pltpu.SemaphoreType.DMA((n * n_peer,)), pltpu.SemaphoreType.DMA((n * n_peer,)),
                        pltpu.SemaphoreType.DMA((n,))],
        compiler_params=pltpu.CompilerParams(has_side_effects=True),
        name=name,
    )(*srcs)


def _remote(src, dst, send_sem, recv_sem, to):
    return pltpu.make_async_remote_copy(src_ref=src, dst_ref=dst, send_sem=send_sem, recv_sem=recv_sem,
                                        device_id=to, device_id_type=MESH_ID)


def _gather_two_level(shards, *, name):
    n = len(shards)
    per = N_DEV - 1

    def body(*refs):
        src, dst = refs[:n], refs[n:2 * n]
        send_sems, recv_sems, local_sems = refs[2 * n:]
        x, y, c = lax.axis_index("x"), lax.axis_index("y"), lax.axis_index("c")
        me, sibling = (x, y, c), (x, y, 1 - c)
        chips = [(1 - x, y), (x, 1 - y), (1 - x, 1 - y)]

        def slot(pos):
            return 4 * pos[0] + 2 * pos[1] + pos[2]

        def copy(i, k, block, to, from_shard=False):
            source = src[i] if from_shard else dst[i].at[slot(block)]
            return _remote(source, dst[i].at[slot(block)], send_sems.at[i * per + k], recv_sems.at[i * per + k], to)

        mine = [pltpu.make_async_copy(src[i], dst[i].at[slot(me)], local_sems.at[i]) for i in range(n)]
        for cp in mine:
            cp.start()
        first = []
        for i in range(n):
            first.append(copy(i, 0, me, sibling, from_shard=True))
            first += [copy(i, 1 + j, me, (*chip, c), from_shard=True) for j, chip in enumerate(chips)]
        for cp in first:
            cp.start()
        passed = []
        for i in range(n):
            for j, chip in enumerate(chips):
                copy(i, 1 + j, (*chip, c), me).wait_recv()
                passed.append(copy(i, 4 + j, (*chip, c), sibling))
                passed[-1].start()
        for i in range(n):
            copy(i, 0, sibling, me).wait_recv()
            for j, chip in enumerate(chips):
                copy(i, 4 + j, (*chip, 1 - c), me).wait_recv()
        for cp in first + passed:
            cp.wait_send()
        for cp in mine:
            cp.wait()

    any_spec = pl.BlockSpec(memory_space=pl.ANY)
    return pl.pallas_call(
        body,
        in_specs=[any_spec] * n,
        out_specs=[any_spec] * n,
        out_shape=[jax.ShapeDtypeStruct((N_DEV,) + s.shape, s.dtype) for s in shards],
        scratch_shapes=[pltpu.SemaphoreType.DMA((n * per,)), pltpu.SemaphoreType.DMA((n * per,)),
                        pltpu.SemaphoreType.DMA((n,))],
        name=name,
    )(*shards)


CHIPS = N_DEV // 2


def _other_chips(x, y):
    return [(x, 1 - y), (1 - x, y), (1 - x, 1 - y)]


def _hosted_gather_first(shards):
    n = len(shards)
    per = CHIPS

    def plan(src, dst, send_sems, recv_sems, local_sems, first_sem):
        x, y, c = lax.axis_index("x"), lax.axis_index("y"), lax.axis_index("c")
        peers = [(x, y, 1 - c)] + [(*chip, c) for chip in _other_chips(x, y)]
        copies = []
        for i in range(n):
            own = pltpu.make_async_copy(src[i], dst[i].at[4 * x + 2 * y + c], local_sems.at[first_sem + i])
            copies.append(_Xfer(own.start, own.wait))
        for j, peer in enumerate(peers):
            for i in range(n):
                k = first_sem + i * per + j
                out = _remote(src[i], dst[i].at[4 * x + 2 * y + c], send_sems.at[k], recv_sems.at[k], peer)
                arrival = _remote(src[i], dst[i].at[4 * peer[0] + 2 * peer[1] + peer[2]], send_sems.at[k],
                                  recv_sems.at[k], peer)

                def wait(out=out, arrival=arrival):
                    arrival.wait_recv()
                    out.wait_send()

                copies.append(_Xfer(out.start, wait))
        return copies

    out_shape = tuple(jax.ShapeDtypeStruct((N_DEV,) + s.shape, s.dtype) for s in shards)
    return _Hosted(tuple(shards), out_shape, n * per, plan)


def _hosted_gather_second(landed):
    n = len(landed)
    per = CHIPS - 1

    def plan(src, dst, send_sems, recv_sems, local_sems, first_sem):
        x, y, c = lax.axis_index("x"), lax.axis_index("y"), lax.axis_index("c")
        copies = []
        for j, chip in enumerate(_other_chips(x, y)):
            mine, theirs = 4 * chip[0] + 2 * chip[1] + c, 4 * chip[0] + 2 * chip[1] + 1 - c
            for i in range(n):
                k = first_sem + i * per + j
                out = _remote(src[i].at[mine], dst[i].at[mine], send_sems.at[k], recv_sems.at[k], (x, y, 1 - c))
                arrival = _remote(src[i].at[theirs], dst[i].at[theirs], send_sems.at[k], recv_sems.at[k],
                                  (x, y, 1 - c))

                def wait(out=out, arrival=arrival):
                    arrival.wait_recv()
                    out.wait_send()

                copies.append(_Xfer(out.start, wait))
        return copies

    out_shape = tuple(jax.ShapeDtypeStruct(a.shape, a.dtype) for a in landed)
    return _Hosted(tuple(landed), out_shape, n * per, plan, tuple((i, i) for i in range(n)))


def _hosted_sibling_swap(arrays, sliced):
    n_sems = sum(CHIPS if s else 1 for s in sliced)

    def plan(src, dst, send_sems, recv_sems, local_sems, first_sem):
        x, y, c = lax.axis_index("x"), lax.axis_index("y"), lax.axis_index("c")
        sibling = (x, y, 1 - c)
        copies, k = [], first_sem
        for i, is_sliced in enumerate(sliced):
            pieces = [(src[i].at[2 * s + 1 - c], dst[i].at[s]) for s in range(CHIPS)] if is_sliced else [(src[i], dst[i])]
            for source, target in pieces:
                cp = _remote(source, target, send_sems.at[k], recv_sems.at[k], sibling)
                copies.append(_Xfer(cp.start, cp.wait))
                k += 1
        return copies

    out_shape = tuple(jax.ShapeDtypeStruct((CHIPS,) + a.shape[1:] if s else a.shape, a.dtype)
                      for a, s in zip(arrays, sliced))
    return _Hosted(tuple(arrays), out_shape, n_sems, plan)


def _hosted_chip_exchange(arrays, sliced):
    n = len(arrays)
    per = CHIPS - 1

    def plan(src, dst, send_sems, recv_sems, local_sems, first_sem):
        x, y, c = lax.axis_index("x"), lax.axis_index("y"), lax.axis_index("c")
        chip = 2 * x + y
        copies = []
        for i in range(n):
            own = pltpu.make_async_copy(src[i].at[chip] if sliced[i] else src[i], dst[i].at[chip],
                                        local_sems.at[first_sem + i])
            copies.append(_Xfer(own.start, own.wait))
        for d in range(1, CHIPS):
            other = chip ^ d
            to = ((other >> 1) & 1, other & 1, c)
            for i in range(n):
                k = first_sem + i * per + d - 1
                source = src[i].at[other] if sliced[i] else src[i]
                out = _remote(source, dst[i].at[chip], send_sems.at[k], recv_sems.at[k], to)
                arrival = _remote(source, dst[i].at[other], send_sems.at[k], recv_sems.at[k], to)

                def wait(out=out, arrival=arrival):
                    arrival.wait_recv()
                    out.wait_send()

                copies.append(_Xfer(out.start, wait))
        return copies

    out_shape = tuple(jax.ShapeDtypeStruct(a.shape if s else (CHIPS,) + a.shape, a.dtype)
                      for a, s in zip(arrays, sliced))
    return _Hosted(tuple(arrays), out_shape, n * per, plan)


def _add_sibling(parts, received, core, *, name):
    _, r, cols = parts.shape
    tr = min(256, r)

    def body(core_ref, a_ref, b_ref, o_ref):
        o_ref[...] = (a_ref[...] + b_ref[...]).astype(BF16)

    grid_spec = pltpu.PrefetchScalarGridSpec(
        num_scalar_prefetch=1,
        grid=(CHIPS, r // tr),
        in_specs=[pl.BlockSpec((None, tr, cols), lambda k, i, core_ref: (2 * k + core_ref[0], i, 0)),
                  pl.BlockSpec((None, tr, cols), lambda k, i, core_ref: (k, i, 0))],
        out_specs=pl.BlockSpec((None, tr, cols), lambda k, i, core_ref: (k, i, 0)),
    )
    return pl.pallas_call(body, grid_spec=grid_spec, out_shape=jax.ShapeDtypeStruct((CHIPS, r, cols), BF16),
                          compiler_params=_params("parallel", "parallel"), name=name)(core, parts, received)


def _add_whole(a, b, *, name):
    def body(a_ref, b_ref, o_ref):
        o_ref[...] = a_ref[...] + b_ref[...]

    return pl.pallas_call(body, out_shape=jax.ShapeDtypeStruct(a.shape, F32), name=name)(a, b)


def _adamw(parts, w, m, v, *, name):
    r, c = w.shape
    n_parts = parts.shape[0]
    tr = min(256, r)
    c1 = 1.0 - ADAM_B1 ** ADAM_STEP
    c2 = 1.0 - ADAM_B2 ** ADAM_STEP

    def body(p_ref, w_ref, m_ref, v_ref, g_ref, d_ref, nm_ref, nv_ref):
        g = p_ref[0].astype(F32)
        for s in range(1, n_parts):
            g = g + p_ref[s].astype(F32)
        nm = ADAM_B1 * m_ref[...] + (1.0 - ADAM_B1) * g
        nv = ADAM_B2 * v_ref[...] + (1.0 - ADAM_B2) * (g * g)
        g_ref[...] = g
        nm_ref[...] = nm
        nv_ref[...] = nv
        d_ref[...] = -ADAM_LR * ((nm / c1) / (jnp.sqrt(nv / c2) + ADAM_EPS) + ADAM_WD * w_ref[...])

    tile = pl.BlockSpec((tr, c), lambda i: (i, 0))
    return pl.pallas_call(
        body,
        grid=(r // tr,),
        in_specs=[pl.BlockSpec((n_parts, tr, c), lambda i: (0, i, 0)), tile, tile, tile],
        out_specs=[tile] * 4,
        out_shape=[jax.ShapeDtypeStruct((r, c), F32)] * 4,
        compiler_params=_params("parallel"),
        name=name,
    )(parts, w, m, v)


BIG = ("w_in", "w_rnn_proj", "w_attn_proj", "w_out", "w_up", "w_down", "w_ple_gate", "w_ple_proj")
LOSS_ROW = "loss"
SMALL = (("conv_b", 1), ("b_rg", 1), ("b_ig", 1), ("lru_lambda", 1), ("g_mlp", 1), ("g_ple", 1),
         ("q_gain", 1), ("k_gain", 1), ("sinks", 1), (LOSS_ROW, 1), ("w_rg", 64), ("w_ig", 64))
SMALL_ROWS = 144
ROW_SHARDED = ("w_rnn_proj", "w_attn_proj", "w_out", "w_down", "w_ple_gate")
COL_SHARDED = ("w_in", "w_up", "w_ple_proj")
BATCHES = {1: ("w_ple_proj", "w_ple_gate", "w_down", "w_up"), 2: ("w_out", "w_rnn_proj", "w_attn_proj"),
           3: ("w_in", "conv_w")}
SMALL_BATCH = 4


def _pack_small(vals):
    rows = []
    for nm, nrow in SMALL:
        flat = vals[nm].reshape(-1).astype(F32)
        rows.append(jnp.pad(flat, (0, nrow * D_MODEL - flat.shape[0])).reshape(nrow, D_MODEL))
    used = sum(nrow for _, nrow in SMALL)
    rows.append(jnp.zeros((SMALL_ROWS - used, D_MODEL), F32))
    return jnp.concatenate(rows, axis=0)


def _unpack_small(packed, shapes):
    out, at = {}, 0
    for nm, nrow in SMALL:
        size = 1
        for s in shapes[nm]:
            size *= s
        out[nm] = packed[at:at + nrow].reshape(-1)[:size].reshape(shapes[nm])
        at += nrow
    return out


def _full_weight(name, landed):
    if name in COL_SHARDED:
        return landed.transpose(1, 0, 2).reshape(landed.shape[1], N_DEV * landed.shape[2])
    return landed.reshape(N_DEV * landed.shape[1], landed.shape[2])


def _owner_slots(name, grad):
    if name == "w_in":
        return grad.reshape(D_MODEL, N_DEV, IN_TOTAL // N_DEV).transpose(1, 0, 2)
    if name == "conv_w":
        return grad.reshape(CONV_W, N_DEV, D_MODEL // N_DEV).transpose(1, 0, 2)
    if name in COL_SHARDED:
        return grad
    return grad.reshape(N_DEV, grad.shape[0] // N_DEV, grad.shape[1])


class _StepExchanges:
    FIRST, SECOND = "first", "second"
    PROJ, OUT, PLE_GATE, UP, DOWN = (("w_rnn_proj", "w_attn_proj"), ("w_out",), ("w_ple_gate",), ("w_up",),
                                     ("w_down", "w_ple_proj"))
    GATHERS = {"mm_in_rnn": ((FIRST, PROJ),), "mm_in_attn": ((FIRST, OUT),),
               "mm_in_gate": ((SECOND, PROJ), (FIRST, PLE_GATE)),
               "rnn_fwd": ((SECOND, OUT), (SECOND, PLE_GATE), (FIRST, UP)),
               "attn_fwd": ((SECOND, UP), (FIRST, DOWN)), "mm_rnn_proj": ((SECOND, DOWN),)}
    SWAPS = {"mm_dhm": 1, "mm_dya_in": 2, "mm_d_in_rnn": SMALL_BATCH}
    CHIP_EXCHANGES = {"rnn_bwd": (1,), "attn_bwd": (2,), "mm_d_in_rest": (SMALL_BATCH,), "mm_dh": (3,)}

    def __init__(self, shards, core):
        self.shards = shards
        self.core = core
        self.parts, self.swapped, self.summed, self.half_gathered = {}, {}, {}, {}

    def ready(self, batch, grads, extra=None):
        if batch == SMALL_BATCH:
            self.parts[batch] = ([_pack_small({**grads, **extra})], [False])
            return
        arrays = [_owner_slots(nm, grads[nm]) for nm in BATCHES[batch]]
        self.parts[batch] = (arrays, [True] * len(arrays))
        if batch not in self.SWAPS.values():
            _, self.swapped[batch] = _call(
                lambda: None, grid=(1,), in_specs=[], out_specs=[], out_shape=[], args=(), name="swap_last",
                semantics=("arbitrary",), hosted=_hosted_sibling_swap(*self.parts[batch]))

    def host(self, tag):
        if tag in self.GATHERS:
            return _merge_hosted([
                _hosted_gather_first([self.shards[nm] for nm in group]) if half == self.FIRST
                else _hosted_gather_second([self.half_gathered[nm] for nm in group])
                for half, group in self.GATHERS[tag]])
        if tag in self.SWAPS:
            return _hosted_sibling_swap(*self.parts[self.SWAPS[tag]])
        if tag in self.CHIP_EXCHANGES:
            hosted = []
            for batch in self.CHIP_EXCHANGES[tag]:
                arrays, sliced = self.parts[batch]
                labels = BATCHES.get(batch, ("small",))
                sums = [_add_sibling(a, r, self.core, name="add_" + lb) if s else _add_whole(a, r, name="add_" + lb)
                        for a, r, s, lb in zip(arrays, self.swapped[batch], sliced, labels)]
                hosted.append(_hosted_chip_exchange(sums, sliced))
            return _merge_hosted(hosted)
        return None

    def landed(self, tag, landed, weights):
        if tag in self.GATHERS:
            names = [(half, nm) for half, group in self.GATHERS[tag] for nm in group]
            for (half, nm), buf in zip(names, landed):
                if half == self.FIRST:
                    self.half_gathered[nm] = buf
                else:
                    weights[nm] = _full_weight(nm, buf)
        elif tag in self.SWAPS:
            self.swapped[self.SWAPS[tag]] = landed
        else:
            at = 0
            for batch in self.CHIP_EXCHANGES[tag]:
                count = len(self.parts[batch][0])
                self.summed[batch] = landed[at:at + count]
                at += count


def kernel(x, p, g_mix, w_in, conv_w, conv_b, w_rg, b_rg, w_ig, b_ig, lru_lambda, w_rnn_proj, q_gain, k_gain, sinks, w_attn_proj, w_out, g_mlp, w_up, w_down, g_ple, w_ple_gate, w_ple_proj, loss_target, m_g_mix, m_w_in, m_conv_w, m_conv_b, m_w_rg, m_b_rg, m_w_ig, m_b_ig, m_lru_lambda, m_w_rnn_proj, m_q_gain, m_k_gain, m_sinks, m_w_attn_proj, m_w_out, m_g_mlp, m_w_up, m_w_down, m_g_ple, m_w_ple_gate, m_w_ple_proj, v_g_mix, v_w_in, v_conv_w, v_conv_b, v_w_rg, v_b_rg, v_w_ig, v_b_ig, v_lru_lambda, v_w_rnn_proj, v_q_gain, v_k_gain, v_sinks, v_w_attn_proj, v_w_out, v_g_mlp, v_w_up, v_w_down, v_g_ple, v_w_ple_gate, v_w_ple_proj):
    names = ("g_mix", "w_in", "conv_w", "conv_b", "w_rg", "b_rg", "w_ig", "b_ig", "lru_lambda", "w_rnn_proj",
             "q_gain", "k_gain", "sinks", "w_attn_proj", "w_out", "g_mlp", "w_up", "w_down", "g_ple",
             "w_ple_gate", "w_ple_proj")
    wts = dict(zip(names, (g_mix, w_in, conv_w, conv_b, w_rg, b_rg, w_ig, b_ig, lru_lambda, w_rnn_proj, q_gain,
                           k_gain, sinks, w_attn_proj, w_out, g_mlp, w_up, w_down, g_ple, w_ple_gate, w_ple_proj)))
    mom1 = dict(zip(names, (m_g_mix, m_w_in, m_conv_w, m_conv_b, m_w_rg, m_b_rg, m_w_ig, m_b_ig, m_lru_lambda,
                            m_w_rnn_proj, m_q_gain, m_k_gain, m_sinks, m_w_attn_proj, m_w_out, m_g_mlp, m_w_up,
                            m_w_down, m_g_ple, m_w_ple_gate, m_w_ple_proj)))
    mom2 = dict(zip(names, (v_g_mix, v_w_in, v_conv_w, v_conv_b, v_w_rg, v_b_rg, v_w_ig, v_b_ig, v_lru_lambda,
                            v_w_rnn_proj, v_q_gain, v_k_gain, v_sinks, v_w_attn_proj, v_w_out, v_g_mlp, v_w_up,
                            v_w_down, v_g_ple, v_w_ple_gate, v_w_ple_proj)))
    n_seq, seq, _ = x.shape
    core = lax.axis_index("c").astype(jnp.int32).reshape(1)

    shards = {nm: wts[nm][0].astype(BF16) for nm in BIG}
    w_in_all, conv_all = _gather_two_level([shards["w_in"], conv_w[0]], name="gather_w_in")
    w = {nm: wts[nm] for nm in names if nm not in BIG}
    w["w_rg"], w["w_ig"] = w_rg[0], w_ig[0]
    w["conv_w"] = conv_all.transpose(1, 0, 2).reshape(CONV_W, D_MODEL)
    w["w_in"] = _full_weight("w_in", w_in_all)
    comm = _StepExchanges(shards, core)
    loss_sum, grad_x, g = _local_step(
        x.reshape(n_seq * seq, D_MODEL), p.reshape(n_seq * seq, PLE_DIM), loss_target.reshape(n_seq * seq, D_MODEL),
        w, n_seq=n_seq, seq=seq, comm=comm)
    del loss_sum

    res = {}
    for batch, batch_names in BATCHES.items():
        for nm, summed in zip(batch_names, comm.summed[batch]):
            res[nm] = _adamw(summed, wts[nm][0], mom1[nm][0], mom2[nm][0], name="adamw_" + nm)
    g_mix_parts, = _exchange([g["g_mix"]], ["gather"], name="gather_g_mix")
    res["g_mix"] = [r[0] for r in _adamw(g_mix_parts, g_mix, m_g_mix, v_g_mix, name="adamw_g_mix")]
    small_names = [nm for nm, _ in SMALL if nm != LOSS_ROW]
    full_small = {}
    for src, key in ((wts, "w"), (mom1, "m"), (mom2, "v")):
        vals = {nm: src[nm][0] for nm in small_names}
        vals[LOSS_ROW] = jnp.zeros((1,), F32)
        full_small[key] = _pack_small(vals)
    small_res = _adamw(comm.summed[SMALL_BATCH][0],full_small["w"], full_small["m"], full_small["v"], name="adamw_small")
    shapes = {nm: wts[nm].shape[1:] for nm in small_names}
    shapes[LOSS_ROW] = (D_MODEL,)
    small_out = [_unpack_small(r, shapes) for r in small_res]
    for nm in small_names:
        res[nm] = [so[nm] for so in small_out]
    loss = jnp.sum(small_out[0][LOSS_ROW]) * (0.5 / D_MODEL)

    outs = [loss, grad_x.reshape(n_seq, seq, D_MODEL)]
    for k in range(4):
        outs.extend(res[nm][k][None] for nm in names)
    return tuple(outs)
```

```python
import functools
from typing import Callable, NamedTuple

import jax
import jax.numpy as jnp
from jax import lax
from jax.experimental import pallas as pl
from jax.experimental.pallas import tpu as pltpu

F32 = jnp.float32
BF16 = jnp.bfloat16

N_DEV = 8
D_MODEL = 1024
RNN_BLOCK_W = 64
CONV_W = 4
LRU_C = 8.0
HEAD_DIM = 64
N_Q_HEADS = 16
N_KV_HEADS = 4
KV_W = N_KV_HEADS * HEAD_DIM
WINDOW = 128
ROPE_THETA = 10000.0
D_FF = 4096
PLE_DIM = 256
NORM_EPS = 1e-6
IN_TOTAL = 5632
COL_RNN_END, COL_ATTN_END = 2048, 3584
ATTN_W = COL_ATTN_END - COL_RNN_END
ATTN_K_AT, ATTN_V_AT = 1024, 1280

ADAM_LR = 0.001
ADAM_B1 = 0.9
ADAM_B2 = 0.999
ADAM_EPS = 1e-08
ADAM_WD = 0.01
ADAM_STEP = 10

LANES = 128
SUBLANES = 8
RNN_TILE = 256
VMEM_LIMIT = 48 * 1024 * 1024
NEG_BIG = -1e30


def _params(*sem):
    return pltpu.CompilerParams(dimension_semantics=sem if sem else None, vmem_limit_bytes=VMEM_LIMIT)


def _sig(x):
    return 0.5 * jnp.tanh(0.5 * x) + 0.5


def _dot_nt(a, b):
    return lax.dot_general(a, b, (((1,), (1,)), ((), ())), preferred_element_type=F32)


def _dot_tn(a, b):
    return lax.dot_general(a, b, (((0,), (0,)), ((), ())), preferred_element_type=F32)


class _Xfer:
    def __init__(self, start, wait):
        self.start, self.wait = start, wait


class _Hosted(NamedTuple):
    srcs: tuple
    out_shape: tuple
    n_sems: int
    plan: Callable
    aliases: tuple = ()


def _merge_hosted(parts):
    parts = [p for p in parts if p is not None]
    if len(parts) <= 1:
        return parts[0] if parts else None
    src_at, dst_at, sem_at, aliases = [0], [0], [0], []
    for p in parts:
        aliases += [(i + src_at[-1], j + dst_at[-1]) for i, j in p.aliases]
        src_at.append(src_at[-1] + len(p.srcs))
        dst_at.append(dst_at[-1] + len(p.out_shape))
        sem_at.append(sem_at[-1] + p.n_sems)

    def plan(src, dst, send_sems, recv_sems, local_sems, first_sem):
        copies = []
        for k, p in enumerate(parts):
            copies += p.plan(src[src_at[k]:src_at[k + 1]], dst[dst_at[k]:dst_at[k + 1]], send_sems, recv_sems,
                             local_sems, first_sem + sem_at[k])
        return copies

    return _Hosted(tuple(a for p in parts for a in p.srcs), tuple(s for p in parts for s in p.out_shape),
                   sem_at[-1], plan, tuple(aliases))


def _call(body, *, grid, in_specs, out_specs, out_shape, args, name, semantics, scratch_shapes=(), hosted=None):
    if hosted is None:
        outs = pl.pallas_call(body, grid=grid, in_specs=list(in_specs), out_specs=list(out_specs),
                              out_shape=list(out_shape), scratch_shapes=list(scratch_shapes),
                              compiler_params=_params(*semantics), name=name)(*args)
        return list(outs), []
    counts = (len(in_specs), len(hosted.srcs), len(out_specs), len(hosted.out_shape), len(scratch_shapes), 3)

    def wrapped(*refs):
        at, groups = 0, []
        for count in counts:
            groups.append(refs[at:at + count])
            at += count
        ins, srcs, outs, dsts, scratch, sems = groups
        copies = hosted.plan(srcs, dsts, *sems, 0)
        ids = [pl.program_id(axis) for axis in range(len(grid))]
        first = functools.reduce(jnp.logical_and, [i == 0 for i in ids])
        last = functools.reduce(jnp.logical_and, [i == g - 1 for i, g in zip(ids, grid)])

        @pl.when(first)
        def _():
            for cp in copies:
                cp.start()

        body(*ins, *outs, *scratch)

        @pl.when(last)
        def _():
            for cp in copies:
                cp.wait()

    any_spec = pl.BlockSpec(memory_space=pl.ANY)
    sems = [pltpu.SemaphoreType.DMA((hosted.n_sems,))] * 3
    outs = pl.pallas_call(
        wrapped, grid=grid, in_specs=list(in_specs) + [any_spec] * counts[1],
        out_specs=list(out_specs) + [any_spec] * counts[3], out_shape=list(out_shape) + list(hosted.out_shape),
        scratch_shapes=list(scratch_shapes) + sems, compiler_params=_params(*["arbitrary"] * len(grid)),
        input_output_aliases={counts[0] + i: counts[2] + j for i, j in hosted.aliases},
        name=name)(*args, *hosted.srcs)
    return list(outs[:counts[2]]), list(outs[counts[2]:])


def _dividing_tile(n, want):
    tile = min(want, n)
    while n % tile:
        tile -= LANES
    return tile


def _matmul(a, b, *, mode, tm, tn, out_dtypes, name, epilogue=None, extras=(), hosted=None, b_cols=None,
            row_vecs=(), n_row_sums=0, extra_col_blocks=None):
    a_parts = tuple(a) if isinstance(a, (tuple, list)) else (a,)
    b_parts = tuple(b) if isinstance(b, (tuple, list)) else (b,)
    assert len(a_parts) == len(b_parts) and (mode == "nt" or len(a_parts) == 1)
    n_parts = len(a_parts)
    m = a_parts[0].shape[0]
    if b_cols is None:
        b_cols = [(0, bp.shape[1]) for bp in b_parts]
    n = b_cols[0][1] if mode == "nn" else b_parts[0].shape[0]
    tm, tn = min(tm, m), _dividing_tile(n, tn)
    n_extra = len(extras) + len(row_vecs)
    n_tiles_out = len(out_dtypes)
    assert n_row_sums == 0 or n == tn

    def body(*refs):
        a_refs, b_refs = refs[:n_parts], refs[n_parts:2 * n_parts]
        rest = refs[2 * n_parts:]
        extra_refs, out_refs = rest[:n_extra], rest[n_extra:]
        if mode == "nn":
            acc = jnp.dot(a_refs[0][...], b_refs[0][...], preferred_element_type=F32)
        else:
            acc = _dot_nt(a_refs[0][...], b_refs[0][...])
            for a_ref, b_ref in zip(a_refs[1:], b_refs[1:]):
                acc = acc + _dot_nt(a_ref[...], b_ref[...])
        res = epilogue(acc, *[e[...] for e in extra_refs]) if epilogue is not None else (acc,)
        for o_ref, r in zip(out_refs[:n_tiles_out], res):
            o_ref[...] = r.astype(o_ref.dtype)
        if n_row_sums:
            @pl.when(pl.program_id(0) == 0)
            def _():
                for o_ref in out_refs[n_tiles_out:]:
                    o_ref[...] = jnp.zeros_like(o_ref)

            for o_ref, r in zip(out_refs[n_tiles_out:], res[n_tiles_out:]):
                o_ref[...] += r

    a_specs = [pl.BlockSpec((tm, ap.shape[1]), lambda i, j: (i, 0)) for ap in a_parts]
    if mode == "nn":
        assert b_cols[0][0] % tn == 0
        first = b_cols[0][0] // tn
        b_specs = [pl.BlockSpec((b_parts[0].shape[0], tn), lambda i, j: (0, first + j))]
    else:
        assert all(at % width == 0 for at, width in b_cols)
        b_specs = [pl.BlockSpec((tn, width), functools.partial(lambda i, j, blk: (j, blk), blk=at // width))
                   for at, width in b_cols]
    tile = pl.BlockSpec((tm, tn), lambda i, j: (i, j))
    row = pl.BlockSpec((1, tn), lambda i, j: (0, j))
    extra_specs = [pl.BlockSpec((tm, tn), functools.partial(lambda i, j, first: (i, first + j), first=first))
                   for first in (extra_col_blocks or [0] * len(extras))]
    outs, landed = _call(
        body,
        grid=(m // tm, n // tn),
        in_specs=a_specs + b_specs + extra_specs + [row] * len(row_vecs),
        out_specs=[tile] * n_tiles_out + [row] * n_row_sums,
        out_shape=[jax.ShapeDtypeStruct((m, n), dt) for dt in out_dtypes]
        + [jax.ShapeDtypeStruct((1, n), F32)] * n_row_sums,
        args=(*a_parts, *b_parts, *extras, *row_vecs), name=name,
        semantics=("arbitrary" if n_row_sums else "parallel", "arbitrary"), hosted=hosted)
    if hosted is not None:
        return (*outs, landed)
    return outs[0] if len(outs) == 1 else outs


def _matmul_tn(a, b, *, tk, tn, tt, name, slot_cols=None):
    t, k = a.shape
    n = b.shape[1]
    tk, tn, tt = min(tk, k), _dividing_tile(n, tn), min(tt, t)

    def body(a_ref, b_ref, o_ref):
        @pl.when(pl.program_id(2) == 0)
        def _():
            o_ref[...] = jnp.zeros_like(o_ref)

        if slot_cols is None:
            o_ref[...] += _dot_tn(a_ref[...], b_ref[...])
        else:
            av = a_ref[...]
            for s in range(tn // slot_cols):
                o_ref[s] += _dot_tn(av, b_ref[:, s * slot_cols:(s + 1) * slot_cols])

    if slot_cols is not None:
        out_spec = pl.BlockSpec((tn // slot_cols, tk, slot_cols), lambda i, j, s: (j, i, 0))
        out_shape = jax.ShapeDtypeStruct((n // slot_cols, k, slot_cols), F32)
    else:
        out_spec = pl.BlockSpec((tk, tn), lambda i, j, s: (i, j))
        out_shape = jax.ShapeDtypeStruct((k, n), F32)
    return pl.pallas_call(
        body,
        grid=(k // tk, n // tn, t // tt),
        in_specs=[pl.BlockSpec((tt, tk), lambda i, j, s: (s, i)), pl.BlockSpec((tt, tn), lambda i, j, s: (s, j))],
        out_specs=out_spec,
        out_shape=out_shape,
        compiler_params=_params("parallel", "parallel", "arbitrary"),
        name=name,
    )(a, b)


def _matmul_tn_multi(a, bs, *, tt, name, hosted=None):
    t, k = a.shape
    tt = min(tt, t)
    n_b = len(bs)

    def body(a_ref, *refs):
        b_refs, o_refs = refs[:n_b], refs[n_b:]

        @pl.when(pl.program_id(0) == 0)
        def _():
            for o_ref in o_refs:
                o_ref[...] = jnp.zeros_like(o_ref)

        a_t = a_ref[...].T
        for b_ref, o_ref in zip(b_refs, o_refs):
            o_ref[...] += jnp.dot(a_t, b_ref[...], preferred_element_type=F32)

    outs, landed = _call(
        body,
        grid=(t // tt,),
        in_specs=[pl.BlockSpec((tt, k), lambda s: (s, 0))] + [pl.BlockSpec((tt, b.shape[1]), lambda s: (s, 0)) for b in bs],
        out_specs=[pl.BlockSpec((k, b.shape[1]), lambda s: (0, 0)) for b in bs],
        out_shape=[jax.ShapeDtypeStruct((k, b.shape[1]), F32) for b in bs],
        args=(a, *bs), name=name, semantics=("arbitrary",), hosted=hosted)
    return (*outs, landed) if hosted is not None else outs


def _rmsnorm_rows(x, g):
    return x * lax.rsqrt(jnp.mean(x * x, axis=-1, keepdims=True) + NORM_EPS) * g


def _rmsnorm_fwd(x, g, *, name):
    t, d = x.shape
    tm = min(512, t)

    def body(x_ref, g_ref, o_ref):
        o_ref[...] = _rmsnorm_rows(x_ref[...], g_ref[...]).astype(BF16)

    return pl.pallas_call(
        body,
        grid=(t // tm,),
        in_specs=[pl.BlockSpec((tm, d), lambda i: (i, 0)), pl.BlockSpec((1, d), lambda i: (0, 0))],
        out_specs=pl.BlockSpec((tm, d), lambda i: (i, 0)),
        out_shape=jax.ShapeDtypeStruct((t, d), BF16),
        compiler_params=_params("parallel"),
        name=name,
    )(x, g)


def _rmsnorm_bwd_rows(dy, x, dres, g):
    r = lax.rsqrt(jnp.mean(x * x, axis=-1, keepdims=True) + NORM_EPS)
    xr = x * r
    gy = dy * g
    dx = dres + r * (gy - xr * jnp.mean(gy * xr, axis=-1, keepdims=True))
    return dx, jnp.sum(dy * xr, axis=0, keepdims=True)


def _softplus_neg(lam):
    z = -lam
    return jnp.maximum(z, 0.0) + jnp.log1p(jnp.exp(-jnp.abs(z)))


def _neg_expm1(y, exp_half_y):
    series = -y * (1.0 + y * 0.5 * (1.0 + y * (1.0 / 3.0) * (1.0 + y * 0.25 * (1.0 + y * 0.2))))
    return jnp.where(y > -0.0625, series, 1.0 - exp_half_y * exp_half_y)


def _gelu_parts(x):
    c = 0.7978845608028654
    u = c * (x + 0.044715 * x * x * x)
    th = jnp.tanh(u)
    gel = 0.5 * x * (1.0 + th)
    dgel = 0.5 * (1.0 + th) + 0.5 * x * (1.0 - th * th) * c * (1.0 + 3.0 * 0.044715 * x * x)
    return gel, dgel


def _shift_down(v, k, rows):
    return jnp.where(rows < k, 0.0, pltpu.roll(v, k, 0))


def _shift_up(v, k, rows, n):
    return jnp.where(rows >= n - k, 0.0, pltpu.roll(v, n - k, 0))


def _scan_within_groups(a, b, *, reverse):
    shape = a.shape
    a = a.reshape(shape[0] // SUBLANES, SUBLANES, shape[1])
    b = b.reshape(a.shape)
    in_group = lax.broadcasted_iota(jnp.int32, a.shape, 1)
    for s in (1, 2, 4):
        if reverse:
            inside, shift = in_group < SUBLANES - s, SUBLANES - s
        else:
            inside, shift = in_group >= s, s
        b = b + a * jnp.where(inside, pltpu.roll(b, shift, 1), 0.0)
        a = a * jnp.where(inside, pltpu.roll(a, shift, 1), 1.0)
    return a.reshape(shape), b.reshape(shape)


def _rnn_gates(xc, wrg, brg, wig, big, lam):
    xcb = xc.astype(BF16)
    r = _sig(jnp.dot(xcb, wrg, preferred_element_type=F32) + brg)
    i = _sig(jnp.dot(xcb, wig, preferred_element_type=F32) + big)
    sp = _softplus_neg(lam)
    log_a = -LRU_C * r * sp
    a = jnp.exp(log_a)
    mult = jnp.sqrt(_neg_expm1(2.0 * log_a, a))
    return xcb, r, i, sp, a, mult


def _conv_fwd(xv, cw, cb, rows):
    return (cb + _shift_down(xv, 3, rows) * cw[0:1, :] + _shift_down(xv, 2, rows) * cw[1:2, :]
            + _shift_down(xv, 1, rows) * cw[2:3, :] + xv * cw[3:4, :])


def _rnn_fwd(z, conv_w, conv_b, wrg_bd, b_rg, wig_bd, b_ig, lam, *, n_seq, seq, hosted=None):
    t = n_seq * seq
    ct = RNN_TILE
    n_ct = D_MODEL // ct

    def body(x_ref, g_ref, cw_ref, cb_ref, wrg_ref, brg_ref, wig_ref, big_ref, lam_ref,
             xc_ref, hr_ref, ya_ref, a_s, b_s):
        rows = lax.broadcasted_iota(jnp.int32, (seq, ct), 0)
        xc = _conv_fwd(x_ref[...], cw_ref[...], cb_ref[...], rows)
        _, r, i, sp, a, mult = _rnn_gates(xc, wrg_ref[...], brg_ref[...], wig_ref[...], big_ref[...], lam_ref[...])
        a_s[...], b_s[...] = _scan_within_groups(a, mult * (i * xc), reverse=False)

        def step(j, carry):
            r0 = pl.multiple_of(j * SUBLANES, SUBLANES)
            h = b_s[pl.ds(r0, SUBLANES), :] + a_s[pl.ds(r0, SUBLANES), :] * carry
            hr_ref[pl.ds(r0, SUBLANES), :] = h
            return h[SUBLANES - 1:SUBLANES, :]

        lax.fori_loop(0, seq // SUBLANES, step, jnp.zeros((1, ct), F32), unroll=4)
        gel, _ = _gelu_parts(g_ref[...])
        xc_ref[...] = xc
        ya_ref[...] = (hr_ref[...] * gel).astype(BF16)

    vec = pl.BlockSpec((1, ct), lambda b, c: (0, c))
    gate_w = pl.BlockSpec((None, ct, ct), lambda b, c: (c, 0, 0))
    tile = pl.BlockSpec((seq, ct), lambda b, c: (b, c))
    outs, landed = _call(
        body,
        grid=(n_seq, n_ct),
        in_specs=[
            pl.BlockSpec((seq, ct), lambda b, c: (b, c)),
            pl.BlockSpec((seq, ct), lambda b, c: (b, n_ct + c)),
            pl.BlockSpec((CONV_W, ct), lambda b, c: (0, c)), vec, gate_w, vec, gate_w, vec, vec,
        ],
        out_specs=[tile, tile, tile],
        out_shape=[jax.ShapeDtypeStruct((t, D_MODEL), F32), jax.ShapeDtypeStruct((t, D_MODEL), F32),
                   jax.ShapeDtypeStruct((t, D_MODEL), BF16)],
        scratch_shapes=[pltpu.VMEM((seq, ct), F32), pltpu.VMEM((seq, ct), F32)],
        args=(z, z, conv_w, conv_b, wrg_bd, b_rg, wig_bd, b_ig, lam), name="rnn_fwd",
        semantics=("parallel", "parallel"), hosted=hosted)
    return (*outs, landed) if hosted is not None else outs


def _rnn_bwd(dya, z, xc, hr, conv_w, wrg_bd, b_rg, wig_bd, b_ig, lam, *, n_seq, seq, hosted=None):
    t = n_seq * seq
    ct = RNN_TILE
    n_ct = D_MODEL // ct

    def body(dya_ref, x_ref, g_ref, xc_ref, hr_ref, cw_ref, wrg_ref, brg_ref, wig_ref, big_ref, lam_ref,
             dx_ref, dg_ref, dwrg_ref, dwig_ref, vec_ref, a_s, d_s, g_s):
        rows = lax.broadcasted_iota(jnp.int32, (seq, ct), 0)
        xv, xc, hr, dyv = x_ref[...], xc_ref[...], hr_ref[...], dya_ref[...]
        lamv = lam_ref[...]
        gel, dgel = _gelu_parts(g_ref[...])
        dg_ref[...] = (dyv * hr * dgel).astype(BF16)
        xcb, r, i, sp, a, mult = _rnn_gates(xc, wrg_ref[...], brg_ref[...], wig_ref[...], big_ref[...], lamv)
        a_s[...], d_s[...] = _scan_within_groups(_shift_up(a, 1, rows, seq), dyv * gel, reverse=True)

        def step(k, carry):
            r0 = pl.multiple_of((seq // SUBLANES - 1 - k) * SUBLANES, SUBLANES)
            gs = d_s[pl.ds(r0, SUBLANES), :] + a_s[pl.ds(r0, SUBLANES), :] * carry
            g_s[pl.ds(r0, SUBLANES), :] = gs
            return gs[0:1, :]

        lax.fori_loop(0, seq // SUBLANES, step, jnp.zeros((1, ct), F32), unroll=4)
        gsum = g_s[...]
        gated = i * xc
        d_log_a = gsum * _shift_down(hr, 1, rows) * a - gsum * gated * (a * a / mult)
        d_gated = gsum * mult
        d_pre_r = (d_log_a * (-LRU_C) * sp) * r * (1.0 - r)
        d_pre_i = (d_gated * xc) * i * (1.0 - i)
        dprb, dpib = d_pre_r.astype(BF16), d_pre_i.astype(BF16)
        dxc = d_gated * i + _dot_nt(dprb, wrg_ref[...]) + _dot_nt(dpib, wig_ref[...])
        cw = cw_ref[...]
        dx = (dxc * cw[3:4, :] + _shift_up(dxc, 1, rows, seq) * cw[2:3, :]
              + _shift_up(dxc, 2, rows, seq) * cw[1:2, :] + _shift_up(dxc, 3, rows, seq) * cw[0:1, :])
        dx_ref[...] = dx.astype(BF16)

        @pl.when(pl.program_id(1) == 0)
        def _():
            dwrg_ref[...] = jnp.zeros_like(dwrg_ref)
            dwig_ref[...] = jnp.zeros_like(dwig_ref)
            vec_ref[...] = jnp.zeros_like(vec_ref)

        dwrg_ref[...] += _dot_tn(xcb, dprb)
        dwig_ref[...] += _dot_tn(xcb, dpib)

        def colsum(v):
            return jnp.sum(v, axis=0, keepdims=True)

        d_sp = colsum(d_log_a * (-LRU_C) * r)
        vec_ref[0:1, :] += colsum(d_pre_r)
        vec_ref[1:2, :] += colsum(d_pre_i)
        vec_ref[2:3, :] += d_sp * (-_sig(-lamv))
        vec_ref[3:4, :] += colsum(dxc)
        vec_ref[4:5, :] += colsum(dxc * _shift_down(xv, 3, rows))
        vec_ref[5:6, :] += colsum(dxc * _shift_down(xv, 2, rows))
        vec_ref[6:7, :] += colsum(dxc * _shift_down(xv, 1, rows))
        vec_ref[7:8, :] += colsum(dxc * xv)

    vec = pl.BlockSpec((1, ct), lambda c, b: (0, c))
    gate_w = pl.BlockSpec((None, ct, ct), lambda c, b: (c, 0, 0))
    tile = pl.BlockSpec((seq, ct), lambda c, b: (b, c))
    outs, landed = _call(
        body,
        grid=(n_ct, n_seq),
        in_specs=[
            tile,
            pl.BlockSpec((seq, ct), lambda c, b: (b, c)),
            pl.BlockSpec((seq, ct), lambda c, b: (b, n_ct + c)),
            tile, tile,
            pl.BlockSpec((CONV_W, ct), lambda c, b: (0, c)), gate_w, vec, gate_w, vec, vec,
        ],
        out_specs=[tile, tile, gate_w, gate_w, pl.BlockSpec((8, ct), lambda c, b: (0, c))],
        out_shape=[jax.ShapeDtypeStruct((t, D_MODEL), BF16), jax.ShapeDtypeStruct((t, D_MODEL), BF16),
                   jax.ShapeDtypeStruct((n_ct, ct, ct), F32), jax.ShapeDtypeStruct((n_ct, ct, ct), F32),
                   jax.ShapeDtypeStruct((8, D_MODEL), F32)],
        scratch_shapes=[pltpu.VMEM((seq, ct), F32)] * 3,
        args=(dya, z, z, xc, hr, conv_w, wrg_bd, b_rg, wig_bd, b_ig, lam), name="rnn_bwd",
        semantics=("parallel", "arbitrary"), hosted=hosted)
    return (*outs, landed) if hosted is not None else outs


def _split_hi_lo(x):
    hi = x.astype(BF16)
    return hi, (x - hi.astype(F32)).astype(BF16)


def _dot_split(x, m_twice):
    hi, lo = _split_hi_lo(x)
    return jnp.dot(jnp.concatenate([hi, lo], axis=1), m_twice, preferred_element_type=F32)


def _head_matrices(width):
    ec = ((lax.broadcasted_iota(jnp.int32, (2 * width, LANES), 0) & (width - 1)) // HEAD_DIM
          == lax.broadcasted_iota(jnp.int32, (2 * width, LANES), 1))
    ee = (lax.broadcasted_iota(jnp.int32, (2 * LANES, width), 1) // HEAD_DIM
          == (lax.broadcasted_iota(jnp.int32, (2 * LANES, width), 0) & (LANES - 1)))
    return jnp.where(ec, 1.0, 0.0).astype(BF16), jnp.where(ee, 1.0, 0.0).astype(BF16)


def _swap_halves(y):
    w = y.shape[1]
    first = (lax.broadcasted_iota(jnp.int32, y.shape, 1) % HEAD_DIM) < HEAD_DIM // 2
    return jnp.where(first, pltpu.roll(y, w - HEAD_DIM // 2, 1), pltpu.roll(y, HEAD_DIM // 2, 1))


def _normrope_fwd(x, gain, cos_t, sin_t, ec, ee):
    w = x.shape[1]
    rs = _dot_split(lax.rsqrt(_dot_split(x * x, ec) * (1.0 / HEAD_DIM) + NORM_EPS), ee)
    nx = x * rs
    y = nx * gain
    reps = w // LANES
    out = y * jnp.tile(cos_t, (1, reps)) + _swap_halves(y) * jnp.tile(sin_t, (1, reps))
    return out, nx, rs


def _normrope_bwd(dout, nx, rs, gain, cos_t, sin_t, ec, ee):
    w = dout.shape[1]
    reps = w // LANES
    dy = dout * jnp.tile(cos_t, (1, reps)) + _swap_halves(dout * jnp.tile(sin_t, (1, reps)))
    dgain = jnp.sum(dy * nx, axis=0, keepdims=True)
    dn = dy * gain
    seg = _dot_split(_dot_split(dn * nx, ec) * (1.0 / HEAD_DIM), ee)
    return rs * (dn - nx * seg), dgain


def _pair_operand(t, group):
    chunk = t[:, (group // 2) * LANES:(group // 2 + 1) * LANES]
    low = lax.broadcasted_iota(jnp.int32, chunk.shape, 1) < HEAD_DIM
    rolled = pltpu.roll(chunk, HEAD_DIM, 1)
    return jnp.where(low, chunk, rolled) if group % 2 == 0 else jnp.where(low, rolled, chunk)


GROUP = N_Q_HEADS // N_KV_HEADS
GROUP_W = GROUP * HEAD_DIM


def _replicate_head(t, group):
    return jnp.tile(_pair_operand(t, group), (1, 2))


def _head_blocks(t):
    seg = lax.broadcasted_iota(jnp.int32, t.shape, 1) // HEAD_DIM
    return jnp.concatenate([jnp.where(seg == h, t, 0.0) for h in range(GROUP)], axis=0)


def _stack_heads(t_t, rows):
    return jnp.concatenate([t_t[:, h * rows:(h + 1) * rows] for h in range(GROUP)], axis=0)


def _head_rows(mat_t, group):
    return jnp.concatenate([mat_t[GROUP * group + h:GROUP * group + h + 1, :] for h in range(GROUP)], axis=1)


def _window_masks(blk):
    key = lax.broadcasted_iota(jnp.int32, (blk, GROUP * blk), 0)
    query = lax.broadcasted_iota(jnp.int32, (blk, GROUP * blk), 1) & (blk - 1)
    return key > query, key <= query


def _mask_window(t, before_ok, own_ok, fill):
    blk = t.shape[0] // 2
    return jnp.concatenate([jnp.where(before_ok, t[:blk], fill), jnp.where(own_ok, t[blk:], fill)], axis=0)


def _attn_fwd(z, cos_t, sin_t, q_gain_t, k_gain_t, sinks_t, *, n_seq, seq, hosted=None):
    t = n_seq * seq
    blk = WINDOW
    nb = seq // blk

    def body(q_ref, kp_ref, kc_ref, vp_ref, vc_ref, cosc_ref, sinc_ref, cosp_ref, sinp_ref, qg_ref, kg_ref, sk_ref,
             o_ref, l_ref):
        n = pl.program_id(1)
        ecq, eeq = _head_matrices(D_MODEL)
        eck, eek = _head_matrices(KV_W)
        cosc, sinc = cosc_ref[...], sinc_ref[...]
        qh, _, _ = _normrope_fwd(q_ref[...], qg_ref[...], cosc, sinc, ecq, eeq)
        qh = qh * (HEAD_DIM ** -0.5)
        kc, _, _ = _normrope_fwd(kc_ref[...], kg_ref[...], cosc, sinc, eck, eek)
        kp, _, _ = _normrope_fwd(kp_ref[...], kg_ref[...], cosp_ref[...], sinp_ref[...], eck, eek)
        kcat = jnp.concatenate([kp, kc], axis=0)
        vcat = jnp.concatenate([vp_ref[...], vc_ref[...]], axis=0)
        above, causal = _window_masks(blk)
        above = above & (n > 0)
        head_row = lax.broadcasted_iota(jnp.int32, (blk, blk), 0)
        sk_t = jnp.broadcast_to(sk_ref[...], (blk, LANES)).T
        vcat_t = vcat.T.astype(BF16)
        lmat = jnp.zeros((blk, blk), F32)
        groups = range(N_KV_HEADS)
        cols = [slice(g * GROUP_W, (g + 1) * GROUP_W) for g in groups]
        scores = [_dot_nt(_replicate_head(kcat, g).astype(BF16), _head_blocks(qh[:, cols[g]]).astype(BF16))
                  for g in groups]
        probs = []
        for g in groups:
            s = _mask_window(scores[g], above, causal, NEG_BIG)
            sink = _head_rows(sk_t, g)
            m = jnp.maximum(jnp.max(s, axis=0, keepdims=True), sink)
            e = jnp.exp(s - m)
            den = jnp.sum(e, axis=0, keepdims=True) + jnp.exp(sink - m)
            probs.append((e * (1.0 / den)).astype(BF16))
            lse = m + jnp.log(den)
            for h in range(GROUP):
                lmat = lmat + jnp.where(head_row == GROUP * g + h, lse[:, h * blk:(h + 1) * blk], 0.0)
        for g in groups:
            out_t = jnp.dot(vcat_t[g * HEAD_DIM:(g + 1) * HEAD_DIM], probs[g], preferred_element_type=F32)
            o_ref[:, cols[g]] = _stack_heads(out_t, blk).T.astype(BF16)
        l_ref[...] = lmat

    def row(b, n):
        return b * nb + n

    def prev(b, n):
        return b * nb + jnp.maximum(n - 1, 0)

    kw = KV_W
    tab_c = pl.BlockSpec((blk, LANES), lambda b, n: (n, 0))
    tab_p = pl.BlockSpec((blk, LANES), lambda b, n: (jnp.maximum(n - 1, 0), 0))
    outs, landed = _call(
        body,
        grid=(n_seq, nb),
        in_specs=[
            pl.BlockSpec((blk, D_MODEL), lambda b, n: (row(b, n), 0)),
            pl.BlockSpec((blk, kw), lambda b, n: (prev(b, n), ATTN_K_AT // kw)),
            pl.BlockSpec((blk, kw), lambda b, n: (row(b, n), ATTN_K_AT // kw)),
            pl.BlockSpec((blk, kw), lambda b, n: (prev(b, n), ATTN_V_AT // kw)),
            pl.BlockSpec((blk, kw), lambda b, n: (row(b, n), ATTN_V_AT // kw)),
            tab_c, tab_c, tab_p, tab_p,
            pl.BlockSpec((1, D_MODEL), lambda b, n: (0, 0)),
            pl.BlockSpec((1, kw), lambda b, n: (0, 0)),
            pl.BlockSpec((1, LANES), lambda b, n: (0, 0)),
        ],
        out_specs=[pl.BlockSpec((blk, D_MODEL), lambda b, n: (row(b, n), 0)),
                   pl.BlockSpec((blk, LANES), lambda b, n: (row(b, n), 0))],
        out_shape=[jax.ShapeDtypeStruct((t, D_MODEL), BF16), jax.ShapeDtypeStruct((t, LANES), F32)],
        args=(z, z, z, z, z, cos_t, sin_t, cos_t, sin_t, q_gain_t, k_gain_t, sinks_t), name="attn_fwd",
        semantics=("parallel", "parallel"), hosted=hosted)
    return (*outs, landed) if hosted is not None else outs


def _attn_bwd(z, o, lse, do, cos_t, sin_t, q_gain_t, k_gain_t, sinks_t, *, n_seq, seq, hosted=None):
    t = n_seq * seq
    blk = WINDOW
    nb = seq // blk
    kw = KV_W
    scale = HEAD_DIM ** -0.5

    def body(qc_ref, qn_ref, kc_ref, vp_ref, vc_ref, oc_ref, on_ref, doc_ref, don_ref, lc_ref, ln_ref,
             cosc_ref, sinc_ref, cosn_ref, sinn_ref, qg_ref, kg_ref, sk_ref,
             dz_ref, vec_ref, dq_s, q_s, k_s):
        n = pl.program_id(1)
        ecq, eeq = _head_matrices(D_MODEL)
        eck, eek = _head_matrices(KV_W)
        cosc, sinc = cosc_ref[...], sinc_ref[...]
        qg, kg = qg_ref[...], kg_ref[...]
        own, other = n & 1, 1 - (n & 1)

        @pl.when(n == 0)
        def _():
            for part, value in enumerate(_normrope_fwd(qc_ref[...], qg, cosc, sinc, ecq, eeq)):
                q_s[own, part] = value
            k_s[other] = jnp.zeros((blk, kw), F32)

        for part, value in enumerate(_normrope_fwd(qn_ref[...], qg, cosn_ref[...], sinn_ref[...], ecq, eeq)):
            q_s[other, part] = value
        qhc, nqc, rsqc = q_s[own, 0], q_s[own, 1], q_s[own, 2]
        qhn = q_s[other, 0]
        khc, nkc, rskc = _normrope_fwd(kc_ref[...], kg, cosc, sinc, eck, eek)
        khp = k_s[other]
        k_s[own] = khc
        doc = doc_ref[...].astype(F32)
        don = don_ref[...].astype(F32)
        delc = _dot_split(doc * oc_ref[...].astype(F32), ecq)
        deln = _dot_split(don * on_ref[...].astype(F32), ecq)
        lc_t, ln_t, delc_t, deln_t = lc_ref[...], ln_ref[...], delc.T, deln.T
        above, causal = _window_masks(blk)
        above_c, above_n = above & (n > 0), above & (n < nb - 1)
        seg = lax.broadcasted_iota(jnp.int32, (blk, GROUP_W), 1) // HEAD_DIM
        lane = lax.broadcasted_iota(jnp.int32, (1, LANES), 1)
        sk_t = jnp.broadcast_to(sk_ref[...], (blk, LANES)).T
        dsink = jnp.zeros((1, LANES), F32)
        kcat = jnp.concatenate([khp, khc], axis=0)
        vcat = jnp.concatenate([vp_ref[...], vc_ref[...]], axis=0)
        kcat_t = kcat.T.astype(BF16)
        dkh = jnp.zeros((blk, GROUP_W), F32)
        dvh = jnp.zeros((blk, GROUP_W), F32)

        def fold_to(group, t):
            total = t + pltpu.roll(t, HEAD_DIM, 1)
            total = total + pltpu.roll(total, 2 * HEAD_DIM, 1)
            return jnp.where(seg == group, total, 0.0)

        groups = range(N_KV_HEADS)
        cols = [slice(g * GROUP_W, (g + 1) * GROUP_W) for g in groups]
        qsc, qsn = qhc * scale, qhn * scale
        qb_c = [_head_blocks(qsc[:, cols[g]]).astype(BF16) for g in groups]
        qb_n = [_head_blocks(qsn[:, cols[g]]).astype(BF16) for g in groups]
        dob_c = [_head_blocks(doc[:, cols[g]]).astype(BF16) for g in groups]
        dob_n = [_head_blocks(don[:, cols[g]]).astype(BF16) for g in groups]
        raw = []
        for g in groups:
            krep = _replicate_head(kcat, g).astype(BF16)
            vrep = _replicate_head(vcat, g).astype(BF16)
            raw.append((_dot_nt(krep, qb_c[g]), _dot_nt(vrep, dob_c[g]),
                        _dot_nt(krep[blk:], qb_n[g]), _dot_nt(vrep[blk:], dob_n[g])))
        cooked = []
        for g in groups:
            s_c, dp_c, s_n, dp_n = raw[g]
            l_row, d_row = _head_rows(lc_t, g), _head_rows(delc_t, g)
            p_c = _mask_window(jnp.exp(s_c - l_row), above_c, causal, 0.0)
            ds_c = (p_c * (dp_c - d_row)).astype(BF16)
            p_n = jnp.where(above_n, jnp.exp(s_n - _head_rows(ln_t, g)), 0.0)
            ds_n = (p_n * (dp_n - _head_rows(deln_t, g))).astype(BF16)
            cooked.append((p_c[blk:].astype(BF16), ds_c, p_n.astype(BF16), ds_n))
            p_sink = jnp.exp(_head_rows(sk_t, g) - l_row) * d_row
            for h in range(GROUP):
                dsink = dsink + jnp.where(lane == GROUP * g + h,
                                          -jnp.sum(p_sink[:, h * blk:(h + 1) * blk], axis=1, keepdims=True), 0.0)
        for g in groups:
            p_cb, ds_c, p_nb, ds_n = cooked[g]
            dq_t = jnp.dot(kcat_t[g * HEAD_DIM:(g + 1) * HEAD_DIM], ds_c, preferred_element_type=F32)
            dq_s[:, cols[g]] = _stack_heads(dq_t, blk).T * scale
            dk_rep = (jnp.dot(ds_c[blk:], qb_c[g], preferred_element_type=F32)
                      + jnp.dot(ds_n, qb_n[g], preferred_element_type=F32))
            dv_rep = (jnp.dot(p_cb, dob_c[g], preferred_element_type=F32)
                      + jnp.dot(p_nb, dob_n[g], preferred_element_type=F32))
            dkh = dkh + fold_to(g, dk_rep)
            dvh = dvh + fold_to(g, dv_rep)
        dq, dqg = _normrope_bwd(dq_s[...], nqc, rsqc, qg, cosc, sinc, ecq, eeq)
        dk, dkg = _normrope_bwd(dkh, nkc, rskc, kg, cosc, sinc, eck, eek)
        dz_ref[:, :ATTN_K_AT] = dq.astype(BF16)
        dz_ref[:, ATTN_K_AT:ATTN_V_AT] = dk.astype(BF16)
        dz_ref[:, ATTN_V_AT:] = dvh.astype(BF16)

        @pl.when(n == 0)
        def _():
            vec_ref[...] = jnp.zeros_like(vec_ref)

        vec_ref[0:1, :] += dqg
        vec_ref[1:2, 0:kw] += dkg
        vec_ref[2:3, 0:LANES] += dsink

    def row(b, n):
        return b * nb + n

    def prev(b, n):
        return b * nb + jnp.maximum(n - 1, 0)

    def nxt(b, n):
        return b * nb + jnp.minimum(n + 1, nb - 1)

    def tiles(width, col, which):
        return pl.BlockSpec((blk, width), lambda b, n: (which(b, n), col))

    def table(which):
        return pl.BlockSpec((blk, LANES), lambda b, n: (which(0, n), 0))

    outs, landed = _call(
        body,
        grid=(n_seq, nb),
        in_specs=[
            tiles(D_MODEL, 0, row), tiles(D_MODEL, 0, nxt),
            tiles(kw, ATTN_K_AT // kw, row),
            tiles(kw, ATTN_V_AT // kw, prev), tiles(kw, ATTN_V_AT // kw, row),
            tiles(D_MODEL, 0, row), tiles(D_MODEL, 0, nxt),
            tiles(D_MODEL, 0, row), tiles(D_MODEL, 0, nxt),
            tiles(LANES, 0, row), tiles(LANES, 0, nxt),
            table(row), table(row), table(nxt), table(nxt),
            pl.BlockSpec((1, D_MODEL), lambda b, n: (0, 0)),
            pl.BlockSpec((1, kw), lambda b, n: (0, 0)),
            pl.BlockSpec((1, LANES), lambda b, n: (0, 0)),
        ],
        out_specs=[tiles(ATTN_W, 0, row), pl.BlockSpec((None, 8, D_MODEL), lambda b, n: (b, 0, 0))],
        out_shape=[jax.ShapeDtypeStruct((t, ATTN_W), BF16), jax.ShapeDtypeStruct((n_seq, 8, D_MODEL), F32)],
        scratch_shapes=[pltpu.VMEM((blk, D_MODEL), F32), pltpu.VMEM((2, 3, blk, D_MODEL), F32),
                        pltpu.VMEM((2, blk, kw), F32)],
        args=(z, z, z, z, z, o, o, do, do, lse, lse, cos_t, sin_t, cos_t, sin_t,
              q_gain_t, k_gain_t, sinks_t), name="attn_bwd", semantics=("arbitrary", "arbitrary"), hosted=hosted)
    return (*outs, landed) if hosted is not None else outs


MERGE_COLS = 512


def _merge_fwd(z, ya, yb):
    t = ya.shape[0]
    tm, tc = min(512, t), MERGE_COLS

    def body(ga_ref, gb_ref, ya_ref, yb_ref, o_ref):
        o_ref[...] = (_sig(ga_ref[...]) * ya_ref[...] + _sig(gb_ref[...]) * yb_ref[...]).astype(BF16)

    tile = pl.BlockSpec((tm, tc), lambda i, j: (i, j))
    return pl.pallas_call(
        body,
        grid=(t // tm, D_MODEL // tc),
        in_specs=[pl.BlockSpec((tm, tc), lambda i, j: (i, j)),
                  pl.BlockSpec((tm, tc), lambda i, j: (i, D_MODEL // tc + j)), tile, tile],
        out_specs=tile,
        out_shape=jax.ShapeDtypeStruct((t, D_MODEL), BF16),
        compiler_params=_params("parallel", "parallel"),
        name="merge_fwd",
    )(z, z, ya, yb)


def _rope_tables(seq):
    inv = ROPE_THETA ** (-jnp.arange(0, HEAD_DIM, 2, dtype=F32) / HEAD_DIM)
    ang = jnp.arange(seq, dtype=F32)[:, None] * inv[None, :]
    cos, sin = jnp.cos(ang), jnp.sin(ang)
    return jnp.tile(jnp.concatenate([cos, cos], axis=1), (1, 2)), jnp.tile(jnp.concatenate([-sin, sin], axis=1), (1, 2))


def _block_diag_tiles(w):
    per = RNN_TILE // RNN_BLOCK_W
    w4 = w.reshape(D_MODEL // RNN_TILE, per, RNN_BLOCK_W, RNN_BLOCK_W)
    eye = jnp.eye(per, dtype=w.dtype)
    dense = jnp.einsum("tpij,pq->tpiqj", w4, eye)
    return dense.reshape(D_MODEL // RNN_TILE, RNN_TILE, RNN_TILE).astype(BF16)


def _block_diag_extract(dense):
    per = RNN_TILE // RNN_BLOCK_W
    d5 = dense.reshape(D_MODEL // RNN_TILE, per, RNN_BLOCK_W, per, RNN_BLOCK_W)
    blocks = jnp.stack([d5[:, p, :, p, :] for p in range(per)], axis=1)
    return blocks.reshape(D_MODEL // RNN_BLOCK_W, RNN_BLOCK_W, RNN_BLOCK_W)


def _column_regions(w_in):
    return w_in[:, :COL_RNN_END], w_in[:, COL_RNN_END:COL_ATTN_END], w_in[:, COL_ATTN_END:]


def _column_regions_of_shards(gathered):
    shard = gathered.shape[2]
    regions = []
    for start, end in ((0, COL_RNN_END), (COL_RNN_END, COL_ATTN_END), (COL_ATTN_END, IN_TOTAL)):
        first, last = start // shard, -(-end // shard)
        cols = gathered[first:last].transpose(1, 0, 2).reshape(gathered.shape[1], (last - first) * shard)
        regions.append(cols[:, start - first * shard:end - first * shard])
    return tuple(regions)


def _local_step(x, p, target, w, *, n_seq, seq, comm=None):
    w = dict(w)

    def run(tag, fn, *args, **kwargs):
        hosted = comm.host(tag) if comm is not None else None
        if hosted is None:
            return fn(*args, **kwargs)
        *outs, landed = fn(*args, hosted=hosted, **kwargs)
        comm.landed(tag, landed, w)
        return outs[0] if len(outs) == 1 else outs

    def ready(batch, grads, extra=None):
        if comm is not None:
            comm.ready(batch, grads, extra)

    cos_t, sin_t = _rope_tables(seq)
    q_gain_t = jnp.tile(w["q_gain"], (1, N_Q_HEADS))
    k_gain_t = jnp.tile(w["k_gain"], (1, N_KV_HEADS))
    sinks_t = jnp.pad(w["sinks"], ((0, 0), (0, LANES - N_Q_HEADS)))
    wrg_bd, wig_bd = _block_diag_tiles(w["w_rg"]), _block_diag_tiles(w["w_ig"])
    dims = dict(n_seq=n_seq, seq=seq)

    h = _rmsnorm_fwd(x, w["g_mix"], name="norm_mix")
    w_in_rnn, w_in_attn, w_in_gate = w["w_in"] if isinstance(w["w_in"], tuple) else _column_regions(w["w_in"])
    z_rnn = run("mm_in_rnn", _matmul, h, w_in_rnn, mode="nn", tm=1024, tn=1024, out_dtypes=[F32], name="mm_in_rnn")
    z_attn = run("mm_in_attn", _matmul, h, w_in_attn, mode="nn", tm=1024, tn=1024, out_dtypes=[F32],
                 name="mm_in_attn")
    z_gate = run("mm_in_gate", _matmul, h, w_in_gate, mode="nn", tm=1024, tn=1024, out_dtypes=[F32],
                 name="mm_in_gate")
    xc, hr, ya_in = run("rnn_fwd", _rnn_fwd, z_rnn, w["conv_w"], w["conv_b"], wrg_bd, w["b_rg"], wig_bd, w["b_ig"],
                        w["lru_lambda"], **dims)
    o, lse = run("attn_fwd", _attn_fwd, z_attn, cos_t, sin_t, q_gain_t, k_gain_t, sinks_t, **dims)
    ya = run("mm_rnn_proj", _matmul, ya_in, w["w_rnn_proj"], mode="nn", tm=1024, tn=1024, out_dtypes=[F32],
             name="mm_rnn_proj")
    yb = _matmul(o, w["w_attn_proj"], mode="nn", tm=1024, tn=1024, out_dtypes=[F32], name="mm_attn_proj")
    merged = _merge_fwd(z_gate, ya, yb)
    def residual_then_norm(acc, res, gain):
        new = res + acc
        return new, _rmsnorm_rows(new, gain)

    x1, hm = _matmul(merged, w["w_out"], mode="nn", tm=512, tn=1024, out_dtypes=[F32, BF16], name="mm_out",
                     epilogue=residual_then_norm, extras=(x,), row_vecs=(w["g_mlp"],))
    act = _matmul(hm, w["w_up"], mode="nn", tm=1024, tn=1024, out_dtypes=[BF16], name="mm_up",
                  epilogue=lambda acc: (jnp.square(jnp.maximum(acc, 0.0)),))
    x2, hp = _matmul(act, w["w_down"], mode="nn", tm=512, tn=1024, out_dtypes=[F32, BF16], name="mm_down",
                     epilogue=residual_then_norm, extras=(x1,), row_vecs=(w["g_ple"],))
    p_bf = p.astype(BF16)
    e = _matmul(p_bf, w["w_ple_proj"], mode="nn", tm=1024, tn=1024, out_dtypes=[F32], name="mm_ple_proj")

    def loss_head(gt, x2v, ev, tgt):
        sg = _sig(gt)
        diff = x2v + ev * sg - tgt
        dx = diff * (1.0 / D_MODEL)
        return dx, dx * ev * sg * (1.0 - sg), dx * sg, jnp.sum(diff * diff, axis=0, keepdims=True)

    dx3, dgt, de, loss_row = _matmul(hp, w["w_ple_gate"], mode="nn", tm=512, tn=1024, out_dtypes=[F32, BF16, BF16],
                                     name="mm_ple_gate", epilogue=loss_head, extras=(x2, e, target), n_row_sums=1)

    g = {}
    g["w_ple_proj"] = _matmul_tn(p_bf, de, tk=PLE_DIM, tn=1024, tt=1024, name="mm_d_ple_proj",
                                 slot_cols=D_MODEL // N_DEV)
    g["w_ple_gate"] = _matmul_tn(hp, dgt, tk=1024, tn=1024, tt=1024, name="mm_d_ple_gate")
    def through_norm(dy, xv, dres, gain):
        dx, dgain = _rmsnorm_bwd_rows(dy, xv, dres, gain)
        return dx, dx, dgain

    dx2, dx2_bf, g["g_ple"] = _matmul(
        dgt, w["w_ple_gate"], mode="nt", tm=512, tn=1024, out_dtypes=[F32, BF16], name="mm_dhp",
        epilogue=through_norm, extras=(x2, dx3), row_vecs=(w["g_ple"],), n_row_sums=1)
    g["w_down"] = _matmul_tn(act, dx2_bf, tk=1024, tn=1024, tt=1024, name="mm_d_down")
    du = _matmul(dx2_bf, w["w_down"], mode="nt", tm=1024, tn=1024, out_dtypes=[BF16], name="mm_dact",
                 epilogue=lambda acc, a: (acc * (2.0 * jnp.sqrt(a.astype(F32))),), extras=(act,))
    g["w_up"] = _matmul_tn(hm, du, tk=1024, tn=1024, tt=1024, name="mm_d_up", slot_cols=D_FF // N_DEV)
    ready(1, g)
    dx1, dx1_bf, g["g_mlp"] = run(
        "mm_dhm", _matmul, du, w["w_up"], mode="nt", tm=512, tn=1024, out_dtypes=[F32, BF16], name="mm_dhm",
        epilogue=through_norm, extras=(x1, dx2), row_vecs=(w["g_mlp"],), n_row_sums=1)
    g["w_out"] = _matmul_tn(merged, dx1_bf, tk=1024, tn=1024, tt=1024, name="mm_d_out")
    def merge_bwd(dm, ga, gb, yav, ybv):
        sa, sb = _sig(ga), _sig(gb)
        return dm * sa, dm * sb, dm * yav * sa * (1.0 - sa), dm * ybv * sb * (1.0 - sb)

    dya, dyb, dga, dgb = _matmul(dx1_bf, w["w_out"], mode="nt", tm=512, tn=1024, out_dtypes=[BF16] * 4,
                                 name="mm_dmerged", epilogue=merge_bwd, extras=(z_gate, z_gate, ya, yb),
                                 extra_col_blocks=(0, 1, 0, 0))
    g["w_rnn_proj"] = _matmul_tn(ya_in, dya, tk=1024, tn=1024, tt=1024, name="mm_d_rnn_proj")
    g["w_attn_proj"] = _matmul_tn(o, dyb, tk=1024, tn=1024, tt=1024, name="mm_d_attn_proj")
    ready(2, g)
    dya_in = run("mm_dya_in", _matmul, dya, w["w_rnn_proj"], mode="nt", tm=1024, tn=1024, out_dtypes=[F32],
                 name="mm_dya_in")
    do = _matmul(dyb, w["w_attn_proj"], mode="nt", tm=1024, tn=1024, out_dtypes=[BF16], name="mm_do")
    dx_rnn, dg_rnn, dwrg_dense, dwig_dense, rnn_vec = run(
        "rnn_bwd", _rnn_bwd, dya_in, z_rnn, xc, hr, w["conv_w"], wrg_bd, w["b_rg"], wig_bd, w["b_ig"],
        w["lru_lambda"], **dims)
    dz_attn, attn_vec = run("attn_bwd", _attn_bwd, z_attn, o, lse, do, cos_t, sin_t, q_gain_t, k_gain_t, sinks_t,
                            **dims)
    dz_parts = (dx_rnn, dg_rnn, dz_attn, dga, dgb)
    g["w_rg"] = _block_diag_extract(dwrg_dense)
    g["w_ig"] = _block_diag_extract(dwig_dense)
    g["b_rg"], g["b_ig"], g["lru_lambda"], g["conv_b"] = (rnn_vec[i:i + 1] for i in range(4))
    g["conv_w"] = rnn_vec[4:8]
    attn_vec = attn_vec[0] if n_seq == 1 else functools.reduce(jnp.add, [attn_vec[b] for b in range(n_seq)])
    g["q_gain"] = attn_vec[0].reshape(N_Q_HEADS, HEAD_DIM).sum(axis=0)[None, :]
    g["k_gain"] = attn_vec[1, :KV_W].reshape(N_KV_HEADS, HEAD_DIM).sum(axis=0)[None, :]
    g["sinks"] = attn_vec[2:3, :N_Q_HEADS]
    ready(SMALL_BATCH, g, {LOSS_ROW: loss_row})
    g["w_in"] = jnp.concatenate(
        list(run("mm_d_in_rnn", _matmul_tn_multi, h, dz_parts[:2], tt=1024, name="mm_d_in_rnn"))
        + list(run("mm_d_in_rest", _matmul_tn_multi, h, dz_parts[2:], tt=512, name="mm_d_in_rest")), axis=1)
    ready(3, g)
    windows = ((w_in_rnn, (0, D_MODEL)), (w_in_rnn, (D_MODEL, D_MODEL)), (w_in_attn, (0, ATTN_W)),
               (w_in_gate, (0, D_MODEL)), (w_in_gate, (D_MODEL, D_MODEL)))
    grad_x, g["g_mix"] = run(
        "mm_dh", _matmul, dz_parts, [wd[0] for wd in windows], mode="nt", tm=256, tn=1024, out_dtypes=[F32],
        name="mm_dh", b_cols=[wd[1] for wd in windows], epilogue=_rmsnorm_bwd_rows, extras=(x, dx1),
        row_vecs=(w["g_mix"],), n_row_sums=1)
    return jnp.sum(loss_row), grad_x, g


MESH_ID = pl.DeviceIdType.MESH


def _coords(index):
    return (index >> 2) & 1, (index >> 1) & 1, index & 1


def _exchange(srcs, kinds, *, name):
    n = len(srcs)
    n_peer = N_DEV - 1

    def body(*refs):
        src, dst = refs[:n], refs[n:2 * n]
        send_sems, recv_sems, local_sems = refs[2 * n:]
        me = 4 * lax.axis_index("x") + 2 * lax.axis_index("y") + lax.axis_index("c")

        def remote(i, d):
            peer = (me + d) & (N_DEV - 1)
            piece = src[i] if kinds[i] == "gather" else src[i].at[peer]
            return pltpu.make_async_remote_copy(
                src_ref=piece, dst_ref=dst[i].at[me], send_sem=send_sems.at[i * n_peer + d - 1],
                recv_sem=recv_sems.at[i * n_peer + d - 1], device_id=_coords(peer), device_id_type=MESH_ID)

        def arrival(i, d):
            sender = (me - d) & (N_DEV - 1)
            piece = src[i] if kinds[i] == "gather" else src[i].at[sender]
            return pltpu.make_async_remote_copy(
                src_ref=piece, dst_ref=dst[i].at[sender], send_sem=send_sems.at[i * n_peer + d - 1],
                recv_sem=recv_sems.at[i * n_peer + d - 1], device_id=_coords(sender), device_id_type=MESH_ID)

        own = []
        for i in range(n):
            piece = src[i] if kinds[i] == "gather" else src[i].at[me]
            own.append(pltpu.make_async_copy(piece, dst[i].at[me], local_sems.at[i]))
            own[-1].start()
        sent = [remote(i, d) for d in range(1, N_DEV) for i in range(n)]
        for cp in sent:
            cp.start()
        for d in range(1, N_DEV):
            for i in range(n):
                arrival(i, d).wait_recv()
        for cp in sent:
            cp.wait_send()
        for cp in own:
            cp.wait()

    def out_of(s, kind):
        shape = s.shape if kind == "scatter" else (N_DEV,) + s.shape
        return jax.ShapeDtypeStruct(shape, s.dtype)

    any_spec = pl.BlockSpec(memory_space=pl.ANY)
    return pl.pallas_call(
        body,
        in_specs=[any_spec] * n,
        out_specs=[any_spec] * n,
        out_shape=[out_of(s, k) for s, k in zip(srcs, kinds)],
        scratch_shapes=[pltpu.SemaphoreType.DMA((n * n_peer,)), pltpu.SemaphoreType.DMA((n * n_peer,)),
                        pltpu.SemaphoreType.DMA((n,))],
        compiler_params=pltpu.CompilerParams(has_side_effects=True),
        name=name,
    )(*srcs)


def _remote(src, dst, send_sem, recv_sem, to):
    return pltpu.make_async_remote_copy(src_ref=src, dst_ref=dst, send_sem=send_sem, recv_sem=recv_sem,
                                        device_id=to, device_id_type=MESH_ID)


def _gather_two_level(shards, *, name):
    n = len(shards)
    per = N_DEV - 1

    def body(*refs):
        src, dst = refs[:n], refs[n:2 * n]
        send_sems, recv_sems, local_sems = refs[2 * n:]
        x, y, c = lax.axis_index("x"), lax.axis_index("y"), lax.axis_index("c")
        me, sibling = (x, y, c), (x, y, 1 - c)
        chips = [(1 - x, y), (x, 1 - y), (1 - x, 1 - y)]

        def slot(pos):
            return 4 * pos[0] + 2 * pos[1] + pos[2]

        def copy(i, k, block, to, from_shard=False):
            source = src[i] if from_shard else dst[i].at[slot(block)]
            return _remote(source, dst[i].at[slot(block)], send_sems.at[i * per + k], recv_sems.at[i * per + k], to)

        mine = [pltpu.make_async_copy(src[i], dst[i].at[slot(me)], local_sems.at[i]) for i in range(n)]
        for cp in mine:
            cp.start()
        first = []
        for i in range(n):
            first.append(copy(i, 0, me, sibling, from_shard=True))
            first += [copy(i, 1 + j, me, (*chip, c), from_shard=True) for j, chip in enumerate(chips)]
        for cp in first:
            cp.start()
        passed = []
        for i in range(n):
            for j, chip in enumerate(chips):
                copy(i, 1 + j, (*chip, c), me).wait_recv()
                passed.append(copy(i, 4 + j, (*chip, c), sibling))
                passed[-1].start()
        for i in range(n):
            copy(i, 0, sibling, me).wait_recv()
            for j, chip in enumerate(chips):
                copy(i, 4 + j, (*chip, 1 - c), me).wait_recv()
        for cp in first + passed:
            cp.wait_send()
        for cp in mine:
            cp.wait()

    any_spec = pl.BlockSpec(memory_space=pl.ANY)
    return pl.pallas_call(
        body,
        in_specs=[any_spec] * n,
        out_specs=[any_spec] * n,
        out_shape=[jax.ShapeDtypeStruct((N_DEV,) + s.shape, s.dtype) for s in shards],
        scratch_shapes=[pltpu.SemaphoreType.DMA((n * per,)), pltpu.SemaphoreType.DMA((n * per,)),
                        pltpu.SemaphoreType.DMA((n,))],
        name=name,
    )(*shards)


CHIPS = N_DEV // 2


def _other_chips(x, y):
    return [(x, 1 - y), (1 - x, y), (1 - x, 1 - y)]


def _hosted_gather_first(shards):
    n = len(shards)
    per = CHIPS

    def plan(src, dst, send_sems, recv_sems, local_sems, first_sem):
        x, y, c = lax.axis_index("x"), lax.axis_index("y"), lax.axis_index("c")
        peers = [(x, y, 1 - c)] + [(*chip, c) for chip in _other_chips(x, y)]
        copies = []
        for i in range(n):
            own = pltpu.make_async_copy(src[i], dst[i].at[4 * x + 2 * y + c], local_sems.at[first_sem + i])
            copies.append(_Xfer(own.start, own.wait))
        for j, peer in enumerate(peers):
            for i in range(n):
                k = first_sem + i * per + j
                out = _remote(src[i], dst[i].at[4 * x + 2 * y + c], send_sems.at[k], recv_sems.at[k], peer)
                arrival = _remote(src[i], dst[i].at[4 * peer[0] + 2 * peer[1] + peer[2]], send_sems.at[k],
                                  recv_sems.at[k], peer)

                def wait(out=out, arrival=arrival):
                    arrival.wait_recv()
                    out.wait_send()

                copies.append(_Xfer(out.start, wait))
        return copies

    out_shape = tuple(jax.ShapeDtypeStruct((N_DEV,) + s.shape, s.dtype) for s in shards)
    return _Hosted(tuple(shards), out_shape, n * per, plan)


def _hosted_gather_second(landed):
    n = len(landed)
    per = CHIPS - 1

    def plan(src, dst, send_sems, recv_sems, local_sems, first_sem):
        x, y, c = lax.axis_index("x"), lax.axis_index("y"), lax.axis_index("c")
        copies = []
        for j, chip in enumerate(_other_chips(x, y)):
            mine, theirs = 4 * chip[0] + 2 * chip[1] + c, 4 * chip[0] + 2 * chip[1] + 1 - c
            for i in range(n):
                k = first_sem + i * per + j
                out = _remote(src[i].at[mine], dst[i].at[mine], send_sems.at[k], recv_sems.at[k], (x, y, 1 - c))
                arrival = _remote(src[i].at[theirs], dst[i].at[theirs], send_sems.at[k], recv_sems.at[k],
                                  (x, y, 1 - c))

                def wait(out=out, arrival=arrival):
                    arrival.wait_recv()
                    out.wait_send()

                copies.append(_Xfer(out.start, wait))
        return copies

    out_shape = tuple(jax.ShapeDtypeStruct(a.shape, a.dtype) for a in landed)
    return _Hosted(tuple(landed), out_shape, n * per, plan, tuple((i, i) for i in range(n)))


def _hosted_sibling_swap(arrays, sliced):
    n_sems = sum(CHIPS if s else 1 for s in sliced)

    def plan(src, dst, send_sems, recv_sems, local_sems, first_sem):
        x, y, c = lax.axis_index("x"), lax.axis_index("y"), lax.axis_index("c")
        sibling = (x, y, 1 - c)
        copies, k = [], first_sem
        for i, is_sliced in enumerate(sliced):
            pieces = [(src[i].at[2 * s + 1 - c], dst[i].at[s]) for s in range(CHIPS)] if is_sliced else [(src[i], dst[i])]
            for source, target in pieces:
                cp = _remote(source, target, send_sems.at[k], recv_sems.at[k], sibling)
                copies.append(_Xfer(cp.start, cp.wait))
                k += 1
        return copies

    out_shape = tuple(jax.ShapeDtypeStruct((CHIPS,) + a.shape[1:] if s else a.shape, a.dtype)
                      for a, s in zip(arrays, sliced))
    return _Hosted(tuple(arrays), out_shape, n_sems, plan)


def _hosted_chip_exchange(arrays, sliced):
    n = len(arrays)
    per = CHIPS - 1

    def plan(src, dst, send_sems, recv_sems, local_sems, first_sem):
        x, y, c = lax.axis_index("x"), lax.axis_index("y"), lax.axis_index("c")
        chip = 2 * x + y
        copies = []
        for i in range(n):
            own = pltpu.make_async_copy(src[i].at[chip] if sliced[i] else src[i], dst[i].at[chip],
                                        local_sems.at[first_sem + i])
            copies.append(_Xfer(own.start, own.wait))
        for d in range(1, CHIPS):
            other = chip ^ d
            to = ((other >> 1) & 1, other & 1, c)
            for i in range(n):
                k = first_sem + i * per + d - 1
                source = src[i].at[other] if sliced[i] else src[i]
                out = _remote(source, dst[i].at[chip], send_sems.at[k], recv_sems.at[k], to)
                arrival = _remote(source, dst[i].at[other], send_sems.at[k], recv_sems.at[k], to)

                def wait(out=out, arrival=arrival):
                    arrival.wait_recv()
                    out.wait_send()

                copies.append(_Xfer(out.start, wait))
        return copies

    out_shape = tuple(jax.ShapeDtypeStruct(a.shape if s else (CHIPS,) + a.shape, a.dtype)
                      for a, s in zip(arrays, sliced))
    return _Hosted(tuple(arrays), out_shape, n * per, plan)


def _add_sibling(parts, received, core, *, name):
    _, r, cols = parts.shape
    tr = min(256, r)

    def body(core_ref, a_ref, b_ref, o_ref):
        o_ref[...] = (a_ref[...] + b_ref[...]).astype(BF16)

    grid_spec = pltpu.PrefetchScalarGridSpec(
        num_scalar_prefetch=1,
        grid=(CHIPS, r // tr),
        in_specs=[pl.BlockSpec((None, tr, cols), lambda k, i, core_ref: (2 * k + core_ref[0], i, 0)),
                  pl.BlockSpec((None, tr, cols), lambda k, i, core_ref: (k, i, 0))],
        out_specs=pl.BlockSpec((None, tr, cols), lambda k, i, core_ref: (k, i, 0)),
    )
    return pl.pallas_call(body, grid_spec=grid_spec, out_shape=jax.ShapeDtypeStruct((CHIPS, r, cols), BF16),
                          compiler_params=_params("parallel", "parallel"), name=name)(core, parts, received)


def _add_whole(a, b, *, name):
    def body(a_ref, b_ref, o_ref):
        o_ref[...] = a_ref[...] + b_ref[...]

    return pl.pallas_call(body, out_shape=jax.ShapeDtypeStruct(a.shape, F32), name=name)(a, b)


def _adamw(parts, w, m, v, *, name):
    r, c = w.shape
    n_parts = parts.shape[0]
    tr = min(256, r)
    c1 = 1.0 - ADAM_B1 ** ADAM_STEP
    c2 = 1.0 - ADAM_B2 ** ADAM_STEP

    def body(p_ref, w_ref, m_ref, v_ref, g_ref, d_ref, nm_ref, nv_ref):
        g = p_ref[0].astype(F32)
        for s in range(1, n_parts):
            g = g + p_ref[s].astype(F32)
        nm = ADAM_B1 * m_ref[...] + (1.0 - ADAM_B1) * g
        nv = ADAM_B2 * v_ref[...] + (1.0 - ADAM_B2) * (g * g)
        g_ref[...] = g
        nm_ref[...] = nm
        nv_ref[...] = nv
        d_ref[...] = -ADAM_LR * ((nm / c1) / (jnp.sqrt(nv / c2) + ADAM_EPS) + ADAM_WD * w_ref[...])

    tile = pl.BlockSpec((tr, c), lambda i: (i, 0))
    return pl.pallas_call(
        body,
        grid=(r // tr,),
        in_specs=[pl.BlockSpec((n_parts, tr, c), lambda i: (0, i, 0)), tile, tile, tile],
        out_specs=[tile] * 4,
        out_shape=[jax.ShapeDtypeStruct((r, c), F32)] * 4,
        compiler_params=_params("parallel"),
        name=name,
    )(parts, w, m, v)


BIG = ("w_in", "w_rnn_proj", "w_attn_proj", "w_out", "w_up", "w_down", "w_ple_gate", "w_ple_proj")
LOSS_ROW = "loss"
SMALL = (("conv_b", 1), ("b_rg", 1), ("b_ig", 1), ("lru_lambda", 1), ("g_mlp", 1), ("g_ple", 1),
         ("q_gain", 1), ("k_gain", 1), ("sinks", 1), (LOSS_ROW, 1), ("w_rg", 64), ("w_ig", 64))
SMALL_ROWS = 144
ROW_SHARDED = ("w_rnn_proj", "w_attn_proj", "w_out", "w_down", "w_ple_gate")
COL_SHARDED = ("w_in", "w_up", "w_ple_proj")
BATCHES = {1: ("w_down", "w_up"), 2: ("w_ple_proj", "w_ple_gate", "w_out", "w_rnn_proj", "w_attn_proj"),
           3: ("w_in", "conv_w")}
SMALL_BATCH = 4


def _pack_small(vals):
    rows = []
    for nm, nrow in SMALL:
        flat = vals[nm].reshape(-1).astype(F32)
        rows.append(jnp.pad(flat, (0, nrow * D_MODEL - flat.shape[0])).reshape(nrow, D_MODEL))
    used = sum(nrow for _, nrow in SMALL)
    rows.append(jnp.zeros((SMALL_ROWS - used, D_MODEL), F32))
    return jnp.concatenate(rows, axis=0)


def _unpack_small(packed, shapes):
    out, at = {}, 0
    for nm, nrow in SMALL:
        size = 1
        for s in shapes[nm]:
            size *= s
        out[nm] = packed[at:at + nrow].reshape(-1)[:size].reshape(shapes[nm])
        at += nrow
    return out


def _full_weight(name, landed):
    if name in COL_SHARDED:
        return landed.transpose(1, 0, 2).reshape(landed.shape[1], N_DEV * landed.shape[2])
    return landed.reshape(N_DEV * landed.shape[1], landed.shape[2])


def _owner_slots(name, grad):
    if name == "w_in":
        return grad.reshape(D_MODEL, N_DEV, IN_TOTAL // N_DEV).transpose(1, 0, 2)
    if name == "conv_w":
        return grad.reshape(CONV_W, N_DEV, D_MODEL // N_DEV).transpose(1, 0, 2)
    if name in COL_SHARDED:
        return grad
    return grad.reshape(N_DEV, grad.shape[0] // N_DEV, grad.shape[1])


class _StepExchanges:
    FIRST, SECOND = "first", "second"
    PROJ, OUT, PLE_GATE, UP, DOWN = (("w_rnn_proj", "w_attn_proj"), ("w_out",), ("w_ple_gate",), ("w_up",),
                                     ("w_down", "w_ple_proj"))
    GATHERS = {"mm_in_rnn": ((FIRST, PROJ),), "mm_in_attn": ((FIRST, OUT),),
               "mm_in_gate": ((SECOND, PROJ), (FIRST, PLE_GATE)),
               "rnn_fwd": ((SECOND, OUT), (SECOND, PLE_GATE), (FIRST, UP)),
               "attn_fwd": ((SECOND, UP), (FIRST, DOWN)), "mm_rnn_proj": ((SECOND, DOWN),)}
    SWAPS = {"mm_dhm": 1, "mm_dya_in": 2, "mm_d_in_rnn": SMALL_BATCH}
    CHIP_EXCHANGES = {"rnn_bwd": (1,), "attn_bwd": (2,), "mm_d_in_rest": (SMALL_BATCH,), "mm_dh": (3,)}

    def __init__(self, shards, core):
        self.shards = shards
        self.core = core
        self.parts, self.swapped, self.summed, self.half_gathered = {}, {}, {}, {}

    def ready(self, batch, grads, extra=None):
        if batch == SMALL_BATCH:
            self.parts[batch] = ([_pack_small({**grads, **extra})], [False])
            return
        arrays = [_owner_slots(nm, grads[nm]) for nm in BATCHES[batch]]
        self.parts[batch] = (arrays, [True] * len(arrays))
        if batch not in self.SWAPS.values():
            _, self.swapped[batch] = _call(
                lambda: None, grid=(1,), in_specs=[], out_specs=[], out_shape=[], args=(), name="swap_last",
                semantics=("arbitrary",), hosted=_hosted_sibling_swap(*self.parts[batch]))

    def host(self, tag):
        if tag in self.GATHERS:
            return _merge_hosted([
                _hosted_gather_first([self.shards[nm] for nm in group]) if half == self.FIRST
                else _hosted_gather_second([self.half_gathered[nm] for nm in group])
                for half, group in self.GATHERS[tag]])
        if tag in self.SWAPS:
            return _hosted_sibling_swap(*self.parts[self.SWAPS[tag]])
        if tag in self.CHIP_EXCHANGES:
            hosted = []
            for batch in self.CHIP_EXCHANGES[tag]:
                arrays, sliced = self.parts[batch]
                labels = BATCHES.get(batch, ("small",))
                sums = [_add_sibling(a, r, self.core, name="add_" + lb) if s else _add_whole(a, r, name="add_" + lb)
                        for a, r, s, lb in zip(arrays, self.swapped[batch], sliced, labels)]
                hosted.append(_hosted_chip_exchange(sums, sliced))
            return _merge_hosted(hosted)
        return None

    def landed(self, tag, landed, weights):
        if tag in self.GATHERS:
            names = [(half, nm) for half, group in self.GATHERS[tag] for nm in group]
            for (half, nm), buf in zip(names, landed):
                if half == self.FIRST:
                    self.half_gathered[nm] = buf
                else:
                    weights[nm] = _full_weight(nm, buf)
        elif tag in self.SWAPS:
            self.swapped[self.SWAPS[tag]] = landed
        else:
            at = 0
            for batch in self.CHIP_EXCHANGES[tag]:
                count = len(self.parts[batch][0])
                self.summed[batch] = landed[at:at + count]
                at += count


def kernel(x, p, g_mix, w_in, conv_w, conv_b, w_rg, b_rg, w_ig, b_ig, lru_lambda, w_rnn_proj, q_gain, k_gain, sinks, w_attn_proj, w_out, g_mlp, w_up, w_down, g_ple, w_ple_gate, w_ple_proj, loss_target, m_g_mix, m_w_in, m_conv_w, m_conv_b, m_w_rg, m_b_rg, m_w_ig, m_b_ig, m_lru_lambda, m_w_rnn_proj, m_q_gain, m_k_gain, m_sinks, m_w_attn_proj, m_w_out, m_g_mlp, m_w_up, m_w_down, m_g_ple, m_w_ple_gate, m_w_ple_proj, v_g_mix, v_w_in, v_conv_w, v_conv_b, v_w_rg, v_b_rg, v_w_ig, v_b_ig, v_lru_lambda, v_w_rnn_proj, v_q_gain, v_k_gain, v_sinks, v_w_attn_proj, v_w_out, v_g_mlp, v_w_up, v_w_down, v_g_ple, v_w_ple_gate, v_w_ple_proj):
    names = ("g_mix", "w_in", "conv_w", "conv_b", "w_rg", "b_rg", "w_ig", "b_ig", "lru_lambda", "w_rnn_proj",
             "q_gain", "k_gain", "sinks", "w_attn_proj", "w_out", "g_mlp", "w_up", "w_down", "g_ple",
             "w_ple_gate", "w_ple_proj")
    wts = dict(zip(names, (g_mix, w_in, conv_w, conv_b, w_rg, b_rg, w_ig, b_ig, lru_lambda, w_rnn_proj, q_gain,
                           k_gain, sinks, w_attn_proj, w_out, g_mlp, w_up, w_down, g_ple, w_ple_gate, w_ple_proj)))
    mom1 = dict(zip(names, (m_g_mix, m_w_in, m_conv_w, m_conv_b, m_w_rg, m_b_rg, m_w_ig, m_b_ig, m_lru_lambda,
                            m_w_rnn_proj, m_q_gain, m_k_gain, m_sinks, m_w_attn_proj, m_w_out, m_g_mlp, m_w_up,
                            m_w_down, m_g_ple, m_w_ple_gate, m_w_ple_proj)))
    mom2 = dict(zip(names, (v_g_mix, v_w_in, v_conv_w, v_conv_b, v_w_rg, v_b_rg, v_w_ig, v_b_ig, v_lru_lambda,
                            v_w_rnn_proj, v_q_gain, v_k_gain, v_sinks, v_w_attn_proj, v_w_out, v_g_mlp, v_w_up,
                            v_w_down, v_g_ple, v_w_ple_gate, v_w_ple_proj)))
    n_seq, seq, _ = x.shape
    core = lax.axis_index("c").astype(jnp.int32).reshape(1)

    shards = {nm: wts[nm][0].astype(BF16) for nm in BIG}
    w_in_all, conv_all = _gather_two_level([shards["w_in"], conv_w[0]], name="gather_w_in")
    w = {nm: wts[nm] for nm in names if nm not in BIG}
    w["w_rg"], w["w_ig"] = w_rg[0], w_ig[0]
    w["conv_w"] = conv_all.transpose(1, 0, 2).reshape(CONV_W, D_MODEL)
    w["w_in"] = _column_regions_of_shards(w_in_all)
    comm = _StepExchanges(shards, core)
    loss_sum, grad_x, g = _local_step(
        x.reshape(n_seq * seq, D_MODEL), p.reshape(n_seq * seq, PLE_DIM), loss_target.reshape(n_seq * seq, D_MODEL),
        w, n_seq=n_seq, seq=seq, comm=comm)
    del loss_sum

    res = {}
    for batch, batch_names in BATCHES.items():
        for nm, summed in zip(batch_names, comm.summed[batch]):
            res[nm] = _adamw(summed, wts[nm][0], mom1[nm][0], mom2[nm][0], name="adamw_" + nm)
    g_mix_parts, = _exchange([g["g_mix"]], ["gather"], name="gather_g_mix")
    res["g_mix"] = [r[0] for r in _adamw(g_mix_parts, g_mix, m_g_mix, v_g_mix, name="adamw_g_mix")]
    small_names = [nm for nm, _ in SMALL if nm != LOSS_ROW]
    full_small = {}
    for src, key in ((wts, "w"), (mom1, "m"), (mom2, "v")):
        vals = {nm: src[nm][0] for nm in small_names}
        vals[LOSS_ROW] = jnp.zeros((1,), F32)
        full_small[key] = _pack_small(vals)
    small_res = _adamw(comm.summed[SMALL_BATCH][0],full_small["w"], full_small["m"], full_small["v"], name="adamw_small")
    shapes = {nm: wts[nm].shape[1:] for nm in small_names}
    shapes[LOSS_ROW] = (D_MODEL,)
    small_out = [_unpack_small(r, shapes) for r in small_res]
    for nm in small_names:
        res[nm] = [so[nm] for so in small_out]
    loss = jnp.sum(small_out[0][LOSS_ROW]) * (0.5 / D_MODEL)

    outs = [loss, grad_x.reshape(n_seq, seq, D_MODEL)]
    for k in range(4):
        outs.extend(res[nm][k][None] for nm in names)
    return tuple(outs)
```

```python
import functools
from typing import Callable, NamedTuple

import jax
import jax.numpy as jnp
from jax import lax
from jax.experimental import pallas as pl
from jax.experimental.pallas import tpu as pltpu

F32 = jnp.float32
BF16 = jnp.bfloat16

N_DEV = 8
D_MODEL = 1024
RNN_BLOCK_W = 64
CONV_W = 4
LRU_C = 8.0
HEAD_DIM = 64
N_Q_HEADS = 16
N_KV_HEADS = 4
KV_W = N_KV_HEADS * HEAD_DIM
WINDOW = 128
ROPE_THETA = 10000.0
D_FF = 4096
PLE_DIM = 256
NORM_EPS = 1e-6
IN_TOTAL = 5632
COL_RNN_END, COL_ATTN_END = 2048, 3584
ATTN_W = COL_ATTN_END - COL_RNN_END
ATTN_K_AT, ATTN_V_AT = 1024, 1280

ADAM_LR = 0.001
ADAM_B1 = 0.9
ADAM_B2 = 0.999
ADAM_EPS = 1e-08
ADAM_WD = 0.01
ADAM_STEP = 10

LANES = 128
SUBLANES = 8
RNN_TILE = 256
VMEM_LIMIT = 48 * 1024 * 1024
NEG_BIG = -1e30


def _params(*sem):
    return pltpu.CompilerParams(dimension_semantics=sem if sem else None, vmem_limit_bytes=VMEM_LIMIT)


def _sig(x):
    return 0.5 * jnp.tanh(0.5 * x) + 0.5


def _dot_nt(a, b):
    return lax.dot_general(a, b, (((1,), (1,)), ((), ())), preferred_element_type=F32)


def _dot_tn(a, b):
    return lax.dot_general(a, b, (((0,), (0,)), ((), ())), preferred_element_type=F32)


class _Xfer:
    def __init__(self, start, wait):
        self.start, self.wait = start, wait


class _Hosted(NamedTuple):
    srcs: tuple
    out_shape: tuple
    n_sems: int
    plan: Callable
    aliases: tuple = ()


def _merge_hosted(parts):
    parts = [p for p in parts if p is not None]
    if len(parts) <= 1:
        return parts[0] if parts else None
    src_at, dst_at, sem_at, aliases = [0], [0], [0], []
    for p in parts:
        aliases += [(i + src_at[-1], j + dst_at[-1]) for i, j in p.aliases]
        src_at.append(src_at[-1] + len(p.srcs))
        dst_at.append(dst_at[-1] + len(p.out_shape))
        sem_at.append(sem_at[-1] + p.n_sems)

    def plan(src, dst, send_sems, recv_sems, local_sems, first_sem):
        copies = []
        for k, p in enumerate(parts):
            copies += p.plan(src[src_at[k]:src_at[k + 1]], dst[dst_at[k]:dst_at[k + 1]], send_sems, recv_sems,
                             local_sems, first_sem + sem_at[k])
        return copies

    return _Hosted(tuple(a for p in parts for a in p.srcs), tuple(s for p in parts for s in p.out_shape),
                   sem_at[-1], plan, tuple(aliases))


def _call(body, *, grid, in_specs, out_specs, out_shape, args, name, semantics, scratch_shapes=(), hosted=None):
    if hosted is None:
        outs = pl.pallas_call(body, grid=grid, in_specs=list(in_specs), out_specs=list(out_specs),
                              out_shape=list(out_shape), scratch_shapes=list(scratch_shapes),
                              compiler_params=_params(*semantics), name=name)(*args)
        return list(outs), []
    counts = (len(in_specs), len(hosted.srcs), len(out_specs), len(hosted.out_shape), len(scratch_shapes), 3)

    def wrapped(*refs):
        at, groups = 0, []
        for count in counts:
            groups.append(refs[at:at + count])
            at += count
        ins, srcs, outs, dsts, scratch, sems = groups
        copies = hosted.plan(srcs, dsts, *sems, 0)
        ids = [pl.program_id(axis) for axis in range(len(grid))]
        first = functools.reduce(jnp.logical_and, [i == 0 for i in ids])
        last = functools.reduce(jnp.logical_and, [i == g - 1 for i, g in zip(ids, grid)])

        @pl.when(first)
        def _():
            for cp in copies:
                cp.start()

        body(*ins, *outs, *scratch)

        @pl.when(last)
        def _():
            for cp in copies:
                cp.wait()

    any_spec = pl.BlockSpec(memory_space=pl.ANY)
    sems = [pltpu.SemaphoreType.DMA((hosted.n_sems,))] * 3
    outs = pl.pallas_call(
        wrapped, grid=grid, in_specs=list(in_specs) + [any_spec] * counts[1],
        out_specs=list(out_specs) + [any_spec] * counts[3], out_shape=list(out_shape) + list(hosted.out_shape),
        scratch_shapes=list(scratch_shapes) + sems, compiler_params=_params(*["arbitrary"] * len(grid)),
        input_output_aliases={counts[0] + i: counts[2] + j for i, j in hosted.aliases},
        name=name)(*args, *hosted.srcs)
    return list(outs[:counts[2]]), list(outs[counts[2]:])


def _dividing_tile(n, want):
    tile = min(want, n)
    while n % tile:
        tile -= LANES
    return tile


def _matmul(a, b, *, mode, tm, tn, out_dtypes, name, epilogue=None, extras=(), hosted=None, b_cols=None,
            row_vecs=(), n_row_sums=0, extra_col_blocks=None):
    a_parts = tuple(a) if isinstance(a, (tuple, list)) else (a,)
    b_parts = tuple(b) if isinstance(b, (tuple, list)) else (b,)
    assert len(a_parts) == len(b_parts) and (mode == "nt" or len(a_parts) == 1)
    n_parts = len(a_parts)
    m = a_parts[0].shape[0]
    if b_cols is None:
        b_cols = [(0, bp.shape[1]) for bp in b_parts]
    n = b_cols[0][1] if mode == "nn" else b_parts[0].shape[0]
    tm, tn = min(tm, m), _dividing_tile(n, tn)
    n_extra = len(extras) + len(row_vecs)
    n_tiles_out = len(out_dtypes)
    assert n_row_sums == 0 or n == tn

    def body(*refs):
        a_refs, b_refs = refs[:n_parts], refs[n_parts:2 * n_parts]
        rest = refs[2 * n_parts:]
        extra_refs, out_refs = rest[:n_extra], rest[n_extra:]
        if mode == "nn":
            acc = jnp.dot(a_refs[0][...], b_refs[0][...], preferred_element_type=F32)
        else:
            acc = _dot_nt(a_refs[0][...], b_refs[0][...])
            for a_ref, b_ref in zip(a_refs[1:], b_refs[1:]):
                acc = acc + _dot_nt(a_ref[...], b_ref[...])
        res = epilogue(acc, *[e[...] for e in extra_refs]) if epilogue is not None else (acc,)
        for o_ref, r in zip(out_refs[:n_tiles_out], res):
            o_ref[...] = r.astype(o_ref.dtype)
        if n_row_sums:
            @pl.when(pl.program_id(0) == 0)
            def _():
                for o_ref in out_refs[n_tiles_out:]:
                    o_ref[...] = jnp.zeros_like(o_ref)

            for o_ref, r in zip(out_refs[n_tiles_out:], res[n_tiles_out:]):
                o_ref[...] += r

    a_specs = [pl.BlockSpec((tm, ap.shape[1]), lambda i, j: (i, 0)) for ap in a_parts]
    if mode == "nn":
        assert b_cols[0][0] % tn == 0
        first = b_cols[0][0] // tn
        b_specs = [pl.BlockSpec((b_parts[0].shape[0], tn), lambda i, j: (0, first + j))]
    else:
        assert all(at % width == 0 for at, width in b_cols)
        b_specs = [pl.BlockSpec((tn, width), functools.partial(lambda i, j, blk: (j, blk), blk=at // width))
                   for at, width in b_cols]
    tile = pl.BlockSpec((tm, tn), lambda i, j: (i, j))
    row = pl.BlockSpec((1, tn), lambda i, j: (0, j))
    extra_specs = [pl.BlockSpec((tm, tn), functools.partial(lambda i, j, first: (i, first + j), first=first))
                   for first in (extra_col_blocks or [0] * len(extras))]
    outs, landed = _call(
        body,
        grid=(m // tm, n // tn),
        in_specs=a_specs + b_specs + extra_specs + [row] * len(row_vecs),
        out_specs=[tile] * n_tiles_out + [row] * n_row_sums,
        out_shape=[jax.ShapeDtypeStruct((m, n), dt) for dt in out_dtypes]
        + [jax.ShapeDtypeStruct((1, n), F32)] * n_row_sums,
        args=(*a_parts, *b_parts, *extras, *row_vecs), name=name,
        semantics=("arbitrary" if n_row_sums else "parallel", "arbitrary"), hosted=hosted)
    if hosted is not None:
        return (*outs, landed)
    return outs[0] if len(outs) == 1 else outs


def _matmul_tn(a, b, *, tk, tn, tt, name, slot_cols=None):
    t, k = a.shape
    n = b.shape[1]
    tk, tn, tt = min(tk, k), _dividing_tile(n, tn), min(tt, t)

    def body(a_ref, b_ref, o_ref):
        @pl.when(pl.program_id(2) == 0)
        def _():
            o_ref[...] = jnp.zeros_like(o_ref)

        if slot_cols is None:
            o_ref[...] += _dot_tn(a_ref[...], b_ref[...])
        else:
            av = a_ref[...]
            for s in range(tn // slot_cols):
                o_ref[s] += _dot_tn(av, b_ref[:, s * slot_cols:(s + 1) * slot_cols])

    if slot_cols is not None:
        out_spec = pl.BlockSpec((tn // slot_cols, tk, slot_cols), lambda i, j, s: (j, i, 0))
        out_shape = jax.ShapeDtypeStruct((n // slot_cols, k, slot_cols), F32)
    else:
        out_spec = pl.BlockSpec((tk, tn), lambda i, j, s: (i, j))
        out_shape = jax.ShapeDtypeStruct((k, n), F32)
    return pl.pallas_call(
        body,
        grid=(k // tk, n // tn, t // tt),
        in_specs=[pl.BlockSpec((tt, tk), lambda i, j, s: (s, i)), pl.BlockSpec((tt, tn), lambda i, j, s: (s, j))],
        out_specs=out_spec,
        out_shape=out_shape,
        compiler_params=_params("parallel", "parallel", "arbitrary"),
        name=name,
    )(a, b)


def _matmul_tn_multi(a, bs, *, tt, name, hosted=None):
    t, k = a.shape
    tt = min(tt, t)
    n_b = len(bs)

    def body(a_ref, *refs):
        b_refs, o_refs = refs[:n_b], refs[n_b:]

        @pl.when(pl.program_id(0) == 0)
        def _():
            for o_ref in o_refs:
                o_ref[...] = jnp.zeros_like(o_ref)

        a_t = a_ref[...].T
        for b_ref, o_ref in zip(b_refs, o_refs):
            o_ref[...] += jnp.dot(a_t, b_ref[...], preferred_element_type=F32)

    outs, landed = _call(
        body,
        grid=(t // tt,),
        in_specs=[pl.BlockSpec((tt, k), lambda s: (s, 0))] + [pl.BlockSpec((tt, b.shape[1]), lambda s: (s, 0)) for b in bs],
        out_specs=[pl.BlockSpec((k, b.shape[1]), lambda s: (0, 0)) for b in bs],
        out_shape=[jax.ShapeDtypeStruct((k, b.shape[1]), F32) for b in bs],
        args=(a, *bs), name=name, semantics=("arbitrary",), hosted=hosted)
    return (*outs, landed) if hosted is not None else outs


def _rmsnorm_rows(x, g):
    return x * lax.rsqrt(jnp.mean(x * x, axis=-1, keepdims=True) + NORM_EPS) * g


def _rmsnorm_fwd(x, g, *, name):
    t, d = x.shape
    tm = min(512, t)

    def body(x_ref, g_ref, o_ref):
        o_ref[...] = _rmsnorm_rows(x_ref[...], g_ref[...]).astype(BF16)

    return pl.pallas_call(
        body,
        grid=(t // tm,),
        in_specs=[pl.BlockSpec((tm, d), lambda i: (i, 0)), pl.BlockSpec((1, d), lambda i: (0, 0))],
        out_specs=pl.BlockSpec((tm, d), lambda i: (i, 0)),
        out_shape=jax.ShapeDtypeStruct((t, d), BF16),
        compiler_params=_params("parallel"),
        name=name,
    )(x, g)


def _rmsnorm_bwd_rows(dy, x, dres, g):
    r = lax.rsqrt(jnp.mean(x * x, axis=-1, keepdims=True) + NORM_EPS)
    xr = x * r
    gy = dy * g
    dx = dres + r * (gy - xr * jnp.mean(gy * xr, axis=-1, keepdims=True))
    return dx, jnp.sum(dy * xr, axis=0, keepdims=True)


def _softplus_neg(lam):
    z = -lam
    return jnp.maximum(z, 0.0) + jnp.log1p(jnp.exp(-jnp.abs(z)))


def _neg_expm1(y, exp_half_y):
    series = -y * (1.0 + y * 0.5 * (1.0 + y * (1.0 / 3.0) * (1.0 + y * 0.25 * (1.0 + y * 0.2))))
    return jnp.where(y > -0.0625, series, 1.0 - exp_half_y * exp_half_y)


def _gelu_parts(x):
    c = 0.7978845608028654
    u = c * (x + 0.044715 * x * x * x)
    th = jnp.tanh(u)
    gel = 0.5 * x * (1.0 + th)
    dgel = 0.5 * (1.0 + th) + 0.5 * x * (1.0 - th * th) * c * (1.0 + 3.0 * 0.044715 * x * x)
    return gel, dgel


def _shift_down(v, k, rows):
    return jnp.where(rows < k, 0.0, pltpu.roll(v, k, 0))


def _shift_up(v, k, rows, n):
    return jnp.where(rows >= n - k, 0.0, pltpu.roll(v, n - k, 0))


def _scan_within_groups(a, b, *, reverse):
    shape = a.shape
    a = a.reshape(shape[0] // SUBLANES, SUBLANES, shape[1])
    b = b.reshape(a.shape)
    in_group = lax.broadcasted_iota(jnp.int32, a.shape, 1)
    for s in (1, 2, 4):
        if reverse:
            inside, shift = in_group < SUBLANES - s, SUBLANES - s
        else:
            inside, shift = in_group >= s, s
        b = b + a * jnp.where(inside, pltpu.roll(b, shift, 1), 0.0)
        a = a * jnp.where(inside, pltpu.roll(a, shift, 1), 1.0)
    return a.reshape(shape), b.reshape(shape)


def _rnn_gates(xc, wrg, brg, wig, big, lam):
    xcb = xc.astype(BF16)
    r = _sig(jnp.dot(xcb, wrg, preferred_element_type=F32) + brg)
    i = _sig(jnp.dot(xcb, wig, preferred_element_type=F32) + big)
    sp = _softplus_neg(lam)
    log_a = -LRU_C * r * sp
    a = jnp.exp(log_a)
    mult = jnp.sqrt(_neg_expm1(2.0 * log_a, a))
    return xcb, r, i, sp, a, mult


def _conv_fwd(xv, cw, cb, rows):
    return (cb + _shift_down(xv, 3, rows) * cw[0:1, :] + _shift_down(xv, 2, rows) * cw[1:2, :]
            + _shift_down(xv, 1, rows) * cw[2:3, :] + xv * cw[3:4, :])


def _rnn_fwd(z, conv_w, conv_b, wrg_bd, b_rg, wig_bd, b_ig, lam, *, n_seq, seq, hosted=None):
    t = n_seq * seq
    ct = RNN_TILE
    n_ct = D_MODEL // ct

    def body(x_ref, g_ref, cw_ref, cb_ref, wrg_ref, brg_ref, wig_ref, big_ref, lam_ref,
             xc_ref, hr_ref, ya_ref, a_s, b_s):
        rows = lax.broadcasted_iota(jnp.int32, (seq, ct), 0)
        xc = _conv_fwd(x_ref[...], cw_ref[...], cb_ref[...], rows)
        _, r, i, sp, a, mult = _rnn_gates(xc, wrg_ref[...], brg_ref[...], wig_ref[...], big_ref[...], lam_ref[...])
        a_s[...], b_s[...] = _scan_within_groups(a, mult * (i * xc), reverse=False)

        def step(j, carry):
            r0 = pl.multiple_of(j * SUBLANES, SUBLANES)
            h = b_s[pl.ds(r0, SUBLANES), :] + a_s[pl.ds(r0, SUBLANES), :] * carry
            hr_ref[pl.ds(r0, SUBLANES), :] = h
            return h[SUBLANES - 1:SUBLANES, :]

        lax.fori_loop(0, seq // SUBLANES, step, jnp.zeros((1, ct), F32), unroll=4)
        gel, _ = _gelu_parts(g_ref[...])
        xc_ref[...] = xc
        ya_ref[...] = (hr_ref[...] * gel).astype(BF16)

    vec = pl.BlockSpec((1, ct), lambda b, c: (0, c))
    gate_w = pl.BlockSpec((None, ct, ct), lambda b, c: (c, 0, 0))
    tile = pl.BlockSpec((seq, ct), lambda b, c: (b, c))
    outs, landed = _call(
        body,
        grid=(n_seq, n_ct),
        in_specs=[
            pl.BlockSpec((seq, ct), lambda b, c: (b, c)),
            pl.BlockSpec((seq, ct), lambda b, c: (b, n_ct + c)),
            pl.BlockSpec((CONV_W, ct), lambda b, c: (0, c)), vec, gate_w, vec, gate_w, vec, vec,
        ],
        out_specs=[tile, tile, tile],
        out_shape=[jax.ShapeDtypeStruct((t, D_MODEL), F32), jax.ShapeDtypeStruct((t, D_MODEL), F32),
                   jax.ShapeDtypeStruct((t, D_MODEL), BF16)],
        scratch_shapes=[pltpu.VMEM((seq, ct), F32), pltpu.VMEM((seq, ct), F32)],
        args=(z, z, conv_w, conv_b, wrg_bd, b_rg, wig_bd, b_ig, lam), name="rnn_fwd",
        semantics=("parallel", "parallel"), hosted=hosted)
    return (*outs, landed) if hosted is not None else outs


def _rnn_bwd(dya, z, xc, hr, conv_w, wrg_bd, b_rg, wig_bd, b_ig, lam, *, n_seq, seq, hosted=None):
    t = n_seq * seq
    ct = RNN_TILE
    n_ct = D_MODEL // ct

    def body(dya_ref, x_ref, g_ref, xc_ref, hr_ref, cw_ref, wrg_ref, brg_ref, wig_ref, big_ref, lam_ref,
             dx_ref, dg_ref, dwrg_ref, dwig_ref, vec_ref, a_s, d_s, g_s):
        rows = lax.broadcasted_iota(jnp.int32, (seq, ct), 0)
        xv, xc, hr, dyv = x_ref[...], xc_ref[...], hr_ref[...], dya_ref[...]
        lamv = lam_ref[...]
        gel, dgel = _gelu_parts(g_ref[...])
        dg_ref[...] = (dyv * hr * dgel).astype(BF16)
        xcb, r, i, sp, a, mult = _rnn_gates(xc, wrg_ref[...], brg_ref[...], wig_ref[...], big_ref[...], lamv)
        a_s[...], d_s[...] = _scan_within_groups(_shift_up(a, 1, rows, seq), dyv * gel, reverse=True)

        def step(k, carry):
            r0 = pl.multiple_of((seq // SUBLANES - 1 - k) * SUBLANES, SUBLANES)
            gs = d_s[pl.ds(r0, SUBLANES), :] + a_s[pl.ds(r0, SUBLANES), :] * carry
            g_s[pl.ds(r0, SUBLANES), :] = gs
            return gs[0:1, :]

        lax.fori_loop(0, seq // SUBLANES, step, jnp.zeros((1, ct), F32), unroll=4)
        gsum = g_s[...]
        gated = i * xc
        d_log_a = gsum * _shift_down(hr, 1, rows) * a - gsum * gated * (a * a / mult)
        d_gated = gsum * mult
        d_pre_r = (d_log_a * (-LRU_C) * sp) * r * (1.0 - r)
        d_pre_i = (d_gated * xc) * i * (1.0 - i)
        dprb, dpib = d_pre_r.astype(BF16), d_pre_i.astype(BF16)
        dxc = d_gated * i + _dot_nt(dprb, wrg_ref[...]) + _dot_nt(dpib, wig_ref[...])
        cw = cw_ref[...]
        dx = (dxc * cw[3:4, :] + _shift_up(dxc, 1, rows, seq) * cw[2:3, :]
              + _shift_up(dxc, 2, rows, seq) * cw[1:2, :] + _shift_up(dxc, 3, rows, seq) * cw[0:1, :])
        dx_ref[...] = dx.astype(BF16)

        @pl.when(pl.program_id(1) == 0)
        def _():
            dwrg_ref[...] = jnp.zeros_like(dwrg_ref)
            dwig_ref[...] = jnp.zeros_like(dwig_ref)
            vec_ref[...] = jnp.zeros_like(vec_ref)

        dwrg_ref[...] += _dot_tn(xcb, dprb)
        dwig_ref[...] += _dot_tn(xcb, dpib)

        def colsum(v):
            return jnp.sum(v, axis=0, keepdims=True)

        d_sp = colsum(d_log_a * (-LRU_C) * r)
        vec_ref[0:1, :] += colsum(d_pre_r)
        vec_ref[1:2, :] += colsum(d_pre_i)
        vec_ref[2:3, :] += d_sp * (-_sig(-lamv))
        vec_ref[3:4, :] += colsum(dxc)
        vec_ref[4:5, :] += colsum(dxc * _shift_down(xv, 3, rows))
        vec_ref[5:6, :] += colsum(dxc * _shift_down(xv, 2, rows))
        vec_ref[6:7, :] += colsum(dxc * _shift_down(xv, 1, rows))
        vec_ref[7:8, :] += colsum(dxc * xv)

    vec = pl.BlockSpec((1, ct), lambda c, b: (0, c))
    gate_w = pl.BlockSpec((None, ct, ct), lambda c, b: (c, 0, 0))
    tile = pl.BlockSpec((seq, ct), lambda c, b: (b, c))
    outs, landed = _call(
        body,
        grid=(n_ct, n_seq),
        in_specs=[
            tile,
            pl.BlockSpec((seq, ct), lambda c, b: (b, c)),
            pl.BlockSpec((seq, ct), lambda c, b: (b, n_ct + c)),
            tile, tile,
            pl.BlockSpec((CONV_W, ct), lambda c, b: (0, c)), gate_w, vec, gate_w, vec, vec,
        ],
        out_specs=[tile, tile, gate_w, gate_w, pl.BlockSpec((8, ct), lambda c, b: (0, c))],
        out_shape=[jax.ShapeDtypeStruct((t, D_MODEL), BF16), jax.ShapeDtypeStruct((t, D_MODEL), BF16),
                   jax.ShapeDtypeStruct((n_ct, ct, ct), F32), jax.ShapeDtypeStruct((n_ct, ct, ct), F32),
                   jax.ShapeDtypeStruct((8, D_MODEL), F32)],
        scratch_shapes=[pltpu.VMEM((seq, ct), F32)] * 3,
        args=(dya, z, z, xc, hr, conv_w, wrg_bd, b_rg, wig_bd, b_ig, lam), name="rnn_bwd",
        semantics=("parallel", "arbitrary"), hosted=hosted)
    return (*outs, landed) if hosted is not None else outs


def _split_hi_lo(x):
    hi = x.astype(BF16)
    return hi, (x - hi.astype(F32)).astype(BF16)


def _dot_split(x, m_twice):
    hi, lo = _split_hi_lo(x)
    return jnp.dot(jnp.concatenate([hi, lo], axis=1), m_twice, preferred_element_type=F32)


def _head_matrices(width):
    ec = ((lax.broadcasted_iota(jnp.int32, (2 * width, LANES), 0) & (width - 1)) // HEAD_DIM
          == lax.broadcasted_iota(jnp.int32, (2 * width, LANES), 1))
    ee = (lax.broadcasted_iota(jnp.int32, (2 * LANES, width), 1) // HEAD_DIM
          == (lax.broadcasted_iota(jnp.int32, (2 * LANES, width), 0) & (LANES - 1)))
    return jnp.where(ec, 1.0, 0.0).astype(BF16), jnp.where(ee, 1.0, 0.0).astype(BF16)


def _swap_halves(y):
    w = y.shape[1]
    first = (lax.broadcasted_iota(jnp.int32, y.shape, 1) % HEAD_DIM) < HEAD_DIM // 2
    return jnp.where(first, pltpu.roll(y, w - HEAD_DIM // 2, 1), pltpu.roll(y, HEAD_DIM // 2, 1))


def _normrope_fwd(x, gain, cos_t, sin_t, ec, ee):
    w = x.shape[1]
    rs = _dot_split(lax.rsqrt(_dot_split(x * x, ec) * (1.0 / HEAD_DIM) + NORM_EPS), ee)
    nx = x * rs
    y = nx * gain
    reps = w // LANES
    out = y * jnp.tile(cos_t, (1, reps)) + _swap_halves(y) * jnp.tile(sin_t, (1, reps))
    return out, nx, rs


def _normrope_bwd(dout, nx, rs, gain, cos_t, sin_t, ec, ee):
    w = dout.shape[1]
    reps = w // LANES
    dy = dout * jnp.tile(cos_t, (1, reps)) + _swap_halves(dout * jnp.tile(sin_t, (1, reps)))
    dgain = jnp.sum(dy * nx, axis=0, keepdims=True)
    dn = dy * gain
    seg = _dot_split(_dot_split(dn * nx, ec) * (1.0 / HEAD_DIM), ee)
    return rs * (dn - nx * seg), dgain


def _pair_operand(t, group):
    chunk = t[:, (group // 2) * LANES:(group // 2 + 1) * LANES]
    low = lax.broadcasted_iota(jnp.int32, chunk.shape, 1) < HEAD_DIM
    rolled = pltpu.roll(chunk, HEAD_DIM, 1)
    return jnp.where(low, chunk, rolled) if group % 2 == 0 else jnp.where(low, rolled, chunk)


GROUP = N_Q_HEADS // N_KV_HEADS
GROUP_W = GROUP * HEAD_DIM


def _replicate_head(t, group):
    return jnp.tile(_pair_operand(t, group), (1, 2))


def _head_blocks(t):
    seg = lax.broadcasted_iota(jnp.int32, t.shape, 1) // HEAD_DIM
    return jnp.concatenate([jnp.where(seg == h, t, 0.0) for h in range(GROUP)], axis=0)


def _stack_heads(t_t, rows):
    return jnp.concatenate([t_t[:, h * rows:(h + 1) * rows] for h in range(GROUP)], axis=0)


def _head_rows(mat_t, group):
    return jnp.concatenate([mat_t[GROUP * group + h:GROUP * group + h + 1, :] for h in range(GROUP)], axis=1)


def _window_masks(blk):
    key = lax.broadcasted_iota(jnp.int32, (blk, GROUP * blk), 0)
    query = lax.broadcasted_iota(jnp.int32, (blk, GROUP * blk), 1) & (blk - 1)
    return key > query, key <= query


def _mask_window(t, before_ok, own_ok, fill):
    blk = t.shape[0] // 2
    return jnp.concatenate([jnp.where(before_ok, t[:blk], fill), jnp.where(own_ok, t[blk:], fill)], axis=0)


def _attn_fwd(z, cos_t, sin_t, q_gain_t, k_gain_t, sinks_t, *, n_seq, seq, hosted=None):
    t = n_seq * seq
    blk = WINDOW
    nb = seq // blk

    def body(q_ref, kp_ref, kc_ref, vp_ref, vc_ref, cosc_ref, sinc_ref, cosp_ref, sinp_ref, qg_ref, kg_ref, sk_ref,
             o_ref, l_ref):
        n = pl.program_id(1)
        ecq, eeq = _head_matrices(D_MODEL)
        eck, eek = _head_matrices(KV_W)
        cosc, sinc = cosc_ref[...], sinc_ref[...]
        qh, _, _ = _normrope_fwd(q_ref[...], qg_ref[...], cosc, sinc, ecq, eeq)
        qh = qh * (HEAD_DIM ** -0.5)
        kc, _, _ = _normrope_fwd(kc_ref[...], kg_ref[...], cosc, sinc, eck, eek)
        kp, _, _ = _normrope_fwd(kp_ref[...], kg_ref[...], cosp_ref[...], sinp_ref[...], eck, eek)
        kcat = jnp.concatenate([kp, kc], axis=0)
        vcat = jnp.concatenate([vp_ref[...], vc_ref[...]], axis=0)
        above, causal = _window_masks(blk)
        above = above & (n > 0)
        head_row = lax.broadcasted_iota(jnp.int32, (blk, blk), 0)
        sk_t = jnp.broadcast_to(sk_ref[...], (blk, LANES)).T
        vcat_t = vcat.T.astype(BF16)
        lmat = jnp.zeros((blk, blk), F32)
        groups = range(N_KV_HEADS)
        cols = [slice(g * GROUP_W, (g + 1) * GROUP_W) for g in groups]
        scores = [_dot_nt(_replicate_head(kcat, g).astype(BF16), _head_blocks(qh[:, cols[g]]).astype(BF16))
                  for g in groups]
        probs = []
        for g in groups:
            s = _mask_window(scores[g], above, causal, NEG_BIG)
            sink = _head_rows(sk_t, g)
            m = jnp.maximum(jnp.max(s, axis=0, keepdims=True), sink)
            e = jnp.exp(s - m)
            den = jnp.sum(e, axis=0, keepdims=True) + jnp.exp(sink - m)
            probs.append((e * (1.0 / den)).astype(BF16))
            lse = m + jnp.log(den)
            for h in range(GROUP):
                lmat = lmat + jnp.where(head_row == GROUP * g + h, lse[:, h * blk:(h + 1) * blk], 0.0)
        for g in groups:
            out_t = jnp.dot(vcat_t[g * HEAD_DIM:(g + 1) * HEAD_DIM], probs[g], preferred_element_type=F32)
            o_ref[:, cols[g]] = _stack_heads(out_t, blk).T.astype(BF16)
        l_ref[...] = lmat

    def row(b, n):
        return b * nb + n

    def prev(b, n):
        return b * nb + jnp.maximum(n - 1, 0)

    kw = KV_W
    tab_c = pl.BlockSpec((blk, LANES), lambda b, n: (n, 0))
    tab_p = pl.BlockSpec((blk, LANES), lambda b, n: (jnp.maximum(n - 1, 0), 0))
    outs, landed = _call(
        body,
        grid=(n_seq, nb),
        in_specs=[
            pl.BlockSpec((blk, D_MODEL), lambda b, n: (row(b, n), 0)),
            pl.BlockSpec((blk, kw), lambda b, n: (prev(b, n), ATTN_K_AT // kw)),
            pl.BlockSpec((blk, kw), lambda b, n: (row(b, n), ATTN_K_AT // kw)),
            pl.BlockSpec((blk, kw), lambda b, n: (prev(b, n), ATTN_V_AT // kw)),
            pl.BlockSpec((blk, kw), lambda b, n: (row(b, n), ATTN_V_AT // kw)),
            tab_c, tab_c, tab_p, tab_p,
            pl.BlockSpec((1, D_MODEL), lambda b, n: (0, 0)),
            pl.BlockSpec((1, kw), lambda b, n: (0, 0)),
            pl.BlockSpec((1, LANES), lambda b, n: (0, 0)),
        ],
        out_specs=[pl.BlockSpec((blk, D_MODEL), lambda b, n: (row(b, n), 0)),
                   pl.BlockSpec((blk, LANES), lambda b, n: (row(b, n), 0))],
        out_shape=[jax.ShapeDtypeStruct((t, D_MODEL), BF16), jax.ShapeDtypeStruct((t, LANES), F32)],
        args=(z, z, z, z, z, cos_t, sin_t, cos_t, sin_t, q_gain_t, k_gain_t, sinks_t), name="attn_fwd",
        semantics=("parallel", "parallel"), hosted=hosted)
    return (*outs, landed) if hosted is not None else outs


def _attn_bwd(z, o, lse, do, cos_t, sin_t, q_gain_t, k_gain_t, sinks_t, *, n_seq, seq, hosted=None):
    t = n_seq * seq
    blk = WINDOW
    nb = seq // blk
    kw = KV_W
    scale = HEAD_DIM ** -0.5

    def body(qc_ref, qn_ref, kc_ref, vp_ref, vc_ref, oc_ref, on_ref, doc_ref, don_ref, lc_ref, ln_ref,
             cosc_ref, sinc_ref, cosn_ref, sinn_ref, qg_ref, kg_ref, sk_ref,
             dz_ref, vec_ref, dq_s, q_s, k_s):
        n = pl.program_id(1)
        ecq, eeq = _head_matrices(D_MODEL)
        eck, eek = _head_matrices(KV_W)
        cosc, sinc = cosc_ref[...], sinc_ref[...]
        qg, kg = qg_ref[...], kg_ref[...]
        own, other = n & 1, 1 - (n & 1)

        @pl.when(n == 0)
        def _():
            for part, value in enumerate(_normrope_fwd(qc_ref[...], qg, cosc, sinc, ecq, eeq)):
                q_s[own, part] = value
            k_s[other] = jnp.zeros((blk, kw), F32)

        for part, value in enumerate(_normrope_fwd(qn_ref[...], qg, cosn_ref[...], sinn_ref[...], ecq, eeq)):
            q_s[other, part] = value
        qhc, nqc, rsqc = q_s[own, 0], q_s[own, 1], q_s[own, 2]
        qhn = q_s[other, 0]
        khc, nkc, rskc = _normrope_fwd(kc_ref[...], kg, cosc, sinc, eck, eek)
        khp = k_s[other]
        k_s[own] = khc
        doc = doc_ref[...].astype(F32)
        don = don_ref[...].astype(F32)
        delc = _dot_split(doc * oc_ref[...].astype(F32), ecq)
        deln = _dot_split(don * on_ref[...].astype(F32), ecq)
        lc_t, ln_t, delc_t, deln_t = lc_ref[...], ln_ref[...], delc.T, deln.T
        above, causal = _window_masks(blk)
        above_c, above_n = above & (n > 0), above & (n < nb - 1)
        seg = lax.broadcasted_iota(jnp.int32, (blk, GROUP_W), 1) // HEAD_DIM
        lane = lax.broadcasted_iota(jnp.int32, (1, LANES), 1)
        sk_t = jnp.broadcast_to(sk_ref[...], (blk, LANES)).T
        dsink = jnp.zeros((1, LANES), F32)
        kcat = jnp.concatenate([khp, khc], axis=0)
        vcat = jnp.concatenate([vp_ref[...], vc_ref[...]], axis=0)
        kcat_t = kcat.T.astype(BF16)
        dkh = jnp.zeros((blk, GROUP_W), F32)
        dvh = jnp.zeros((blk, GROUP_W), F32)

        def fold_to(group, t):
            total = t + pltpu.roll(t, HEAD_DIM, 1)
            total = total + pltpu.roll(total, 2 * HEAD_DIM, 1)
            return jnp.where(seg == group, total, 0.0)

        groups = range(N_KV_HEADS)
        cols = [slice(g * GROUP_W, (g + 1) * GROUP_W) for g in groups]
        qsc, qsn = qhc * scale, qhn * scale
        qb_c = [_head_blocks(qsc[:, cols[g]]).astype(BF16) for g in groups]
        qb_n = [_head_blocks(qsn[:, cols[g]]).astype(BF16) for g in groups]
        dob_c = [_head_blocks(doc[:, cols[g]]).astype(BF16) for g in groups]
        dob_n = [_head_blocks(don[:, cols[g]]).astype(BF16) for g in groups]
        raw = []
        for g in groups:
            krep = _replicate_head(kcat, g).astype(BF16)
            vrep = _replicate_head(vcat, g).astype(BF16)
            raw.append((_dot_nt(krep, qb_c[g]), _dot_nt(vrep, dob_c[g]),
                        _dot_nt(krep[blk:], qb_n[g]), _dot_nt(vrep[blk:], dob_n[g])))
        cooked = []
        for g in groups:
            s_c, dp_c, s_n, dp_n = raw[g]
            l_row, d_row = _head_rows(lc_t, g), _head_rows(delc_t, g)
            p_c = _mask_window(jnp.exp(s_c - l_row), above_c, causal, 0.0)
            ds_c = (p_c * (dp_c - d_row)).astype(BF16)
            p_n = jnp.where(above_n, jnp.exp(s_n - _head_rows(ln_t, g)), 0.0)
            ds_n = (p_n * (dp_n - _head_rows(deln_t, g))).astype(BF16)
            cooked.append((p_c[blk:].astype(BF16), ds_c, p_n.astype(BF16), ds_n))
            p_sink = jnp.exp(_head_rows(sk_t, g) - l_row) * d_row
            for h in range(GROUP):
                dsink = dsink + jnp.where(lane == GROUP * g + h,
                                          -jnp.sum(p_sink[:, h * blk:(h + 1) * blk], axis=1, keepdims=True), 0.0)
        for g in groups:
            p_cb, ds_c, p_nb, ds_n = cooked[g]
            dq_t = jnp.dot(kcat_t[g * HEAD_DIM:(g + 1) * HEAD_DIM], ds_c, preferred_element_type=F32)
            dq_s[:, cols[g]] = _stack_heads(dq_t, blk).T * scale
            dk_rep = (jnp.dot(ds_c[blk:], qb_c[g], preferred_element_type=F32)
                      + jnp.dot(ds_n, qb_n[g], preferred_element_type=F32))
            dv_rep = (jnp.dot(p_cb, dob_c[g], preferred_element_type=F32)
                      + jnp.dot(p_nb, dob_n[g], preferred_element_type=F32))
            dkh = dkh + fold_to(g, dk_rep)
            dvh = dvh + fold_to(g, dv_rep)
        dq, dqg = _normrope_bwd(dq_s[...], nqc, rsqc, qg, cosc, sinc, ecq, eeq)
        dk, dkg = _normrope_bwd(dkh, nkc, rskc, kg, cosc, sinc, eck, eek)
        dz_ref[:, :ATTN_K_AT] = dq.astype(BF16)
        dz_ref[:, ATTN_K_AT:ATTN_V_AT] = dk.astype(BF16)
        dz_ref[:, ATTN_V_AT:] = dvh.astype(BF16)

        @pl.when(n == 0)
        def _():
            vec_ref[...] = jnp.zeros_like(vec_ref)

        vec_ref[0:1, :] += dqg
        vec_ref[1:2, 0:kw] += dkg
        vec_ref[2:3, 0:LANES] += dsink

    def row(b, n):
        return b * nb + n

    def prev(b, n):
        return b * nb + jnp.maximum(n - 1, 0)

    def nxt(b, n):
        return b * nb + jnp.minimum(n + 1, nb - 1)

    def tiles(width, col, which):
        return pl.BlockSpec((blk, width), lambda b, n: (which(b, n), col))

    def table(which):
        return pl.BlockSpec((blk, LANES), lambda b, n: (which(0, n), 0))

    outs, landed = _call(
        body,
        grid=(n_seq, nb),
        in_specs=[
            tiles(D_MODEL, 0, row), tiles(D_MODEL, 0, nxt),
            tiles(kw, ATTN_K_AT // kw, row),
            tiles(kw, ATTN_V_AT // kw, prev), tiles(kw, ATTN_V_AT // kw, row),
            tiles(D_MODEL, 0, row), tiles(D_MODEL, 0, nxt),
            tiles(D_MODEL, 0, row), tiles(D_MODEL, 0, nxt),
            tiles(LANES, 0, row), tiles(LANES, 0, nxt),
            table(row), table(row), table(nxt), table(nxt),
            pl.BlockSpec((1, D_MODEL), lambda b, n: (0, 0)),
            pl.BlockSpec((1, kw), lambda b, n: (0, 0)),
            pl.BlockSpec((1, LANES), lambda b, n: (0, 0)),
        ],
        out_specs=[tiles(ATTN_W, 0, row), pl.BlockSpec((None, 8, D_MODEL), lambda b, n: (b, 0, 0))],
        out_shape=[jax.ShapeDtypeStruct((t, ATTN_W), BF16), jax.ShapeDtypeStruct((n_seq, 8, D_MODEL), F32)],
        scratch_shapes=[pltpu.VMEM((blk, D_MODEL), F32), pltpu.VMEM((2, 3, blk, D_MODEL), F32),
                        pltpu.VMEM((2, blk, kw), F32)],
        args=(z, z, z, z, z, o, o, do, do, lse, lse, cos_t, sin_t, cos_t, sin_t,
              q_gain_t, k_gain_t, sinks_t), name="attn_bwd", semantics=("arbitrary", "arbitrary"), hosted=hosted)
    return (*outs, landed) if hosted is not None else outs


def _rope_tables(seq):
    inv = ROPE_THETA ** (-jnp.arange(0, HEAD_DIM, 2, dtype=F32) / HEAD_DIM)
    ang = jnp.arange(seq, dtype=F32)[:, None] * inv[None, :]
    cos, sin = jnp.cos(ang), jnp.sin(ang)
    return jnp.tile(jnp.concatenate([cos, cos], axis=1), (1, 2)), jnp.tile(jnp.concatenate([-sin, sin], axis=1), (1, 2))


def _block_diag_tiles(w):
    per = RNN_TILE // RNN_BLOCK_W
    w4 = w.reshape(D_MODEL // RNN_TILE, per, RNN_BLOCK_W, RNN_BLOCK_W)
    eye = jnp.eye(per, dtype=w.dtype)
    dense = jnp.einsum("tpij,pq->tpiqj", w4, eye)
    return dense.reshape(D_MODEL // RNN_TILE, RNN_TILE, RNN_TILE).astype(BF16)


def _block_diag_extract(dense):
    per = RNN_TILE // RNN_BLOCK_W
    d5 = dense.reshape(D_MODEL // RNN_TILE, per, RNN_BLOCK_W, per, RNN_BLOCK_W)
    blocks = jnp.stack([d5[:, p, :, p, :] for p in range(per)], axis=1)
    return blocks.reshape(D_MODEL // RNN_BLOCK_W, RNN_BLOCK_W, RNN_BLOCK_W)


def _local_step(x, p, target, w, *, n_seq, seq, comm=None):
    w = dict(w)

    def run(tag, fn, *args, **kwargs):
        hosted = comm.host(tag) if comm is not None else None
        if hosted is None:
            return fn(*args, **kwargs)
        *outs, landed = fn(*args, hosted=hosted, **kwargs)
        comm.landed(tag, landed, w)
        return outs[0] if len(outs) == 1 else outs

    def ready(batch, grads, extra=None):
        if comm is not None:
            comm.ready(batch, grads, extra)

    cos_t, sin_t = _rope_tables(seq)
    q_gain_t = jnp.tile(w["q_gain"], (1, N_Q_HEADS))
    k_gain_t = jnp.tile(w["k_gain"], (1, N_KV_HEADS))
    sinks_t = jnp.pad(w["sinks"], ((0, 0), (0, LANES - N_Q_HEADS)))
    wrg_bd, wig_bd = _block_diag_tiles(w["w_rg"]), _block_diag_tiles(w["w_ig"])
    dims = dict(n_seq=n_seq, seq=seq)

    h = _rmsnorm_fwd(x, w["g_mix"], name="norm_mix")
    z_rnn = run("mm_in_rnn", _matmul, h, w["w_in"], mode="nn", tm=1024, tn=1024, out_dtypes=[F32],
                name="mm_in_rnn", b_cols=[(0, COL_RNN_END)])
    w_in_attn, w_in_gate = w["w_in"][:, COL_RNN_END:COL_ATTN_END], w["w_in"][:, COL_ATTN_END:]
    z_attn = run("mm_in_attn", _matmul, h, w_in_attn, mode="nn", tm=1024, tn=1024, out_dtypes=[F32],
                 name="mm_in_attn")
    z_gate = run("mm_in_gate", _matmul, h, w_in_gate, mode="nn", tm=1024, tn=1024, out_dtypes=[F32],
                 name="mm_in_gate")
    xc, hr, ya_in = run("rnn_fwd", _rnn_fwd, z_rnn, w["conv_w"], w["conv_b"], wrg_bd, w["b_rg"], wig_bd, w["b_ig"],
                        w["lru_lambda"], **dims)
    o, lse = run("attn_fwd", _attn_fwd, z_attn, cos_t, sin_t, q_gain_t, k_gain_t, sinks_t, **dims)
    ya = run("mm_rnn_proj", _matmul, ya_in, w["w_rnn_proj"], mode="nn", tm=1024, tn=1024, out_dtypes=[F32],
             name="mm_rnn_proj")
    yb, merged = _matmul(
        o, w["w_attn_proj"], mode="nn", tm=512, tn=1024, out_dtypes=[F32, BF16], name="mm_attn_proj",
        epilogue=lambda acc, ga, gb, yav: (acc, _sig(ga) * yav + _sig(gb) * acc),
        extras=(z_gate, z_gate, ya), extra_col_blocks=(0, 1, 0))
    def residual_then_norm(acc, res, gain):
        new = res + acc
        return new, _rmsnorm_rows(new, gain)

    x1, hm = _matmul(merged, w["w_out"], mode="nn", tm=512, tn=1024, out_dtypes=[F32, BF16], name="mm_out",
                     epilogue=residual_then_norm, extras=(x,), row_vecs=(w["g_mlp"],))
    act = _matmul(hm, w["w_up"], mode="nn", tm=1024, tn=1024, out_dtypes=[BF16], name="mm_up",
                  epilogue=lambda acc: (jnp.square(jnp.maximum(acc, 0.0)),))
    x2, hp = _matmul(act, w["w_down"], mode="nn", tm=512, tn=1024, out_dtypes=[F32, BF16], name="mm_down",
                     epilogue=residual_then_norm, extras=(x1,), row_vecs=(w["g_ple"],))
    p_bf = p.astype(BF16)
    e = _matmul(p_bf, w["w_ple_proj"], mode="nn", tm=1024, tn=1024, out_dtypes=[F32], name="mm_ple_proj")

    def loss_head(gt, x2v, ev, tgt):
        sg = _sig(gt)
        diff = x2v + ev * sg - tgt
        dx = diff * (1.0 / D_MODEL)
        return dx, dx * ev * sg * (1.0 - sg), dx * sg, jnp.sum(diff * diff, axis=0, keepdims=True)

    dx3, dgt, de, loss_row = _matmul(hp, w["w_ple_gate"], mode="nn", tm=512, tn=1024, out_dtypes=[F32, BF16, BF16],
                                     name="mm_ple_gate", epilogue=loss_head, extras=(x2, e, target), n_row_sums=1)

    g = {}
    g["w_ple_proj"] = _matmul_tn(p_bf, de, tk=PLE_DIM, tn=1024, tt=1024, name="mm_d_ple_proj",
                                 slot_cols=D_MODEL // N_DEV)
    g["w_ple_gate"] = _matmul_tn(hp, dgt, tk=1024, tn=1024, tt=1024, name="mm_d_ple_gate")
    def through_norm(dy, xv, dres, gain):
        dx, dgain = _rmsnorm_bwd_rows(dy, xv, dres, gain)
        return dx, dx, dgain

    dx2, dx2_bf, g["g_ple"] = _matmul(
        dgt, w["w_ple_gate"], mode="nt", tm=512, tn=1024, out_dtypes=[F32, BF16], name="mm_dhp",
        epilogue=through_norm, extras=(x2, dx3), row_vecs=(w["g_ple"],), n_row_sums=1)
    g["w_down"] = _matmul_tn(act, dx2_bf, tk=1024, tn=1024, tt=1024, name="mm_d_down")

    def relu_grad(dact, a):
        a = a.astype(F32)
        return (dact * (2.0 * jnp.where(a > 0.0, a * lax.rsqrt(a), 0.0)),)

    du = _matmul(dx2_bf, w["w_down"], mode="nt", tm=1024, tn=1024, out_dtypes=[BF16], name="mm_dact",
                 epilogue=relu_grad, extras=(act,))
    g["w_up"] = _matmul_tn(hm, du, tk=1024, tn=1024, tt=1024, name="mm_d_up", slot_cols=D_FF // N_DEV)
    ready(1, g)
    dx1, dx1_bf, g["g_mlp"] = run(
        "mm_dhm", _matmul, du, w["w_up"], mode="nt", tm=512, tn=1024, out_dtypes=[F32, BF16], name="mm_dhm",
        epilogue=through_norm, extras=(x1, dx2), row_vecs=(w["g_mlp"],), n_row_sums=1)
    g["w_out"] = _matmul_tn(merged, dx1_bf, tk=1024, tn=1024, tt=1024, name="mm_d_out")
    def merge_bwd(dm, ga, gb, yav, ybv):
        sa, sb = _sig(ga), _sig(gb)
        return dm * sa, dm * sb, dm * yav * sa * (1.0 - sa), dm * ybv * sb * (1.0 - sb)

    dya, dyb, dga, dgb = _matmul(dx1_bf, w["w_out"], mode="nt", tm=512, tn=1024, out_dtypes=[BF16] * 4,
                                 name="mm_dmerged", epilogue=merge_bwd, extras=(z_gate, z_gate, ya, yb),
                                 extra_col_blocks=(0, 1, 0, 0))
    g["w_rnn_proj"] = _matmul_tn(ya_in, dya, tk=1024, tn=1024, tt=1024, name="mm_d_rnn_proj")
    g["w_attn_proj"] = _matmul_tn(o, dyb, tk=1024, tn=1024, tt=1024, name="mm_d_attn_proj")
    ready(2, g)
    dya_in = run("mm_dya_in", _matmul, dya, w["w_rnn_proj"], mode="nt", tm=1024, tn=1024, out_dtypes=[F32],
                 name="mm_dya_in")
    do = _matmul(dyb, w["w_attn_proj"], mode="nt", tm=1024, tn=1024, out_dtypes=[BF16], name="mm_do")
    dx_rnn, dg_rnn, dwrg_dense, dwig_dense, rnn_vec = run(
        "rnn_bwd", _rnn_bwd, dya_in, z_rnn, xc, hr, w["conv_w"], wrg_bd, w["b_rg"], wig_bd, w["b_ig"],
        w["lru_lambda"], **dims)
    dz_attn, attn_vec = run("attn_bwd", _attn_bwd, z_attn, o, lse, do, cos_t, sin_t, q_gain_t, k_gain_t, sinks_t,
                            **dims)
    dz_parts = (dx_rnn, dg_rnn, dz_attn, dga, dgb)
    g["w_rg"] = _block_diag_extract(dwrg_dense)
    g["w_ig"] = _block_diag_extract(dwig_dense)
    g["b_rg"], g["b_ig"], g["lru_lambda"], g["conv_b"] = (rnn_vec[i:i + 1] for i in range(4))
    g["conv_w"] = rnn_vec[4:8]
    attn_vec = attn_vec[0] if n_seq == 1 else functools.reduce(jnp.add, [attn_vec[b] for b in range(n_seq)])
    g["q_gain"] = attn_vec[0].reshape(N_Q_HEADS, HEAD_DIM).sum(axis=0)[None, :]
    g["k_gain"] = attn_vec[1, :KV_W].reshape(N_KV_HEADS, HEAD_DIM).sum(axis=0)[None, :]
    g["sinks"] = attn_vec[2:3, :N_Q_HEADS]
    ready(SMALL_BATCH, g, {LOSS_ROW: loss_row})
    g["w_in"] = jnp.concatenate(
        list(run("mm_d_in_rnn", _matmul_tn_multi, h, dz_parts[:2], tt=1024, name="mm_d_in_rnn"))
        + list(run("mm_d_in_rest", _matmul_tn_multi, h, dz_parts[2:], tt=512, name="mm_d_in_rest")), axis=1)
    ready(3, g)
    windows = ((w["w_in"], (0, D_MODEL)), (w["w_in"], (D_MODEL, D_MODEL)), (w_in_attn, (0, ATTN_W)),
               (w_in_gate, (0, D_MODEL)), (w_in_gate, (D_MODEL, D_MODEL)))
    grad_x, g["g_mix"] = run(
        "mm_dh", _matmul, dz_parts, [wd[0] for wd in windows], mode="nt", tm=256, tn=1024, out_dtypes=[F32],
        name="mm_dh", b_cols=[wd[1] for wd in windows], epilogue=_rmsnorm_bwd_rows, extras=(x, dx1),
        row_vecs=(w["g_mix"],), n_row_sums=1)
    return jnp.sum(loss_row), grad_x, g


MESH_ID = pl.DeviceIdType.MESH


def _coords(index):
    return (index >> 2) & 1, (index >> 1) & 1, index & 1


def _exchange(srcs, kinds, *, name):
    n = len(srcs)
    n_peer = N_DEV - 1

    def body(*refs):
        src, dst = refs[:n], refs[n:2 * n]
        send_sems, recv_sems, local_sems = refs[2 * n:]
        me = 4 * lax.axis_index("x") + 2 * lax.axis_index("y") + lax.axis_index("c")

        def remote(i, d):
            peer = (me + d) & (N_DEV - 1)
            piece = src[i] if kinds[i] == "gather" else src[i].at[peer]
            return pltpu.make_async_remote_copy(
                src_ref=piece, dst_ref=dst[i].at[me], send_sem=send_sems.at[i * n_peer + d - 1],
                recv_sem=recv_sems.at[i * n_peer + d - 1], device_id=_coords(peer), device_id_type=MESH_ID)

        def arrival(i, d):
            sender = (me - d) & (N_DEV - 1)
            piece = src[i] if kinds[i] == "gather" else src[i].at[sender]
            return pltpu.make_async_remote_copy(
                src_ref=piece, dst_ref=dst[i].at[sender], send_sem=send_sems.at[i * n_peer + d - 1],
                recv_sem=recv_sems.at[i * n_peer + d - 1], device_id=_coords(sender), device_id_type=MESH_ID)

        own = []
        for i in range(n):
            piece = src[i] if kinds[i] == "gather" else src[i].at[me]
            own.append(pltpu.make_async_copy(piece, dst[i].at[me], local_sems.at[i]))
            own[-1].start()
        sent = [remote(i, d) for d in range(1, N_DEV) for i in range(n)]
        for cp in sent:
            cp.start()
        for d in range(1, N_DEV):
            for i in range(n):
                arrival(i, d).wait_recv()
        for cp in sent:
            cp.wait_send()
        for cp in own:
            cp.wait()

    def out_of(s, kind):
        shape = s.shape if kind == "scatter" else (N_DEV,) + s.shape
        return jax.ShapeDtypeStruct(shape, s.dtype)

    any_spec = pl.BlockSpec(memory_space=pl.ANY)
    return pl.pallas_call(
        body,
        in_specs=[any_spec] * n,
        out_specs=[any_spec] * n,
        out_shape=[out_of(s, k) for s, k in zip(srcs, kinds)],
        scratch_shapes=[pltpu.SemaphoreType.DMA((n * n_peer,)), pltpu.SemaphoreType.DMA((n * n_peer,)),
                        pltpu.SemaphoreType.DMA((n,))],
        compiler_params=pltpu.CompilerParams(has_side_effects=True),
        name=name,
    )(*srcs)


def _remote(src, dst, send_sem, recv_sem, to):
    return pltpu.make_async_remote_copy(src_ref=src, dst_ref=dst, send_sem=send_sem, recv_sem=recv_sem,
                                        device_id=to, device_id_type=MESH_ID)


def _gather_two_level(shards, *, name):
    n = len(shards)
    per = N_DEV - 1

    def body(*refs):
        src, dst = refs[:n], refs[n:2 * n]
        send_sems, recv_sems, local_sems = refs[2 * n:]
        x, y, c = lax.axis_index("x"), lax.axis_index("y"), lax.axis_index("c")
        me, sibling = (x, y, c), (x, y, 1 - c)
        chips = [(1 - x, y), (x, 1 - y), (1 - x, 1 - y)]

        def slot(pos):
            return 4 * pos[0] + 2 * pos[1] + pos[2]

        def copy(i, k, block, to, from_shard=False):
            source = src[i] if from_shard else dst[i].at[slot(block)]
            return _remote(source, dst[i].at[slot(block)], send_sems.at[i * per + k], recv_sems.at[i * per + k], to)

        mine = [pltpu.make_async_copy(src[i], dst[i].at[slot(me)], local_sems.at[i]) for i in range(n)]
        for cp in mine:
            cp.start()
        first = []
        for i in range(n):
            first.append(copy(i, 0, me, sibling, from_shard=True))
            first += [copy(i, 1 + j, me, (*chip, c), from_shard=True) for j, chip in enumerate(chips)]
        for cp in first:
            cp.start()
        passed = []
        for i in range(n):
            for j, chip in enumerate(chips):
                copy(i, 1 + j, (*chip, c), me).wait_recv()
                passed.append(copy(i, 4 + j, (*chip, c), sibling))
                passed[-1].start()
        for i in range(n):
            copy(i, 0, sibling, me).wait_recv()
            for j, chip in enumerate(chips):
                copy(i, 4 + j, (*chip, 1 - c), me).wait_recv()
        for cp in first + passed:
            cp.wait_send()
        for cp in mine:
            cp.wait()

    any_spec = pl.BlockSpec(memory_space=pl.ANY)
    return pl.pallas_call(
        body,
        in_specs=[any_spec] * n,
        out_specs=[any_spec] * n,
        out_shape=[jax.ShapeDtypeStruct((N_DEV,) + s.shape, s.dtype) for s in shards],
        scratch_shapes=[pltpu.SemaphoreType.DMA((n * per,)), pltpu.SemaphoreType.DMA((n * per,)),
                        pltpu.SemaphoreType.DMA((n,))],
        name=name,
    )(*shards)


CHIPS = N_DEV // 2


def _other_chips(x, y):
    return [(x, 1 - y), (1 - x, y), (1 - x, 1 - y)]


def _hosted_gather_first(shards):
    n = len(shards)
    per = CHIPS

    def plan(src, dst, send_sems, recv_sems, local_sems, first_sem):
        x, y, c = lax.axis_index("x"), lax.axis_index("y"), lax.axis_index("c")
        peers = [(x, y, 1 - c)] + [(*chip, c) for chip in _other_chips(x, y)]
        copies = []
        for i in range(n):
            own = pltpu.make_async_copy(src[i], dst[i].at[4 * x + 2 * y + c], local_sems.at[first_sem + i])
            copies.append(_Xfer(own.start, own.wait))
        for j, peer in enumerate(peers):
            for i in range(n):
                k = first_sem + i * per + j
                out = _remote(src[i], dst[i].at[4 * x + 2 * y + c], send_sems.at[k], recv_sems.at[k], peer)
                arrival = _remote(src[i], dst[i].at[4 * peer[0] + 2 * peer[1] + peer[2]], send_sems.at[k],
                                  recv_sems.at[k], peer)

                def wait(out=out, arrival=arrival):
                    arrival.wait_recv()
                    out.wait_send()

                copies.append(_Xfer(out.start, wait))
        return copies

    out_shape = tuple(jax.ShapeDtypeStruct((N_DEV,) + s.shape, s.dtype) for s in shards)
    return _Hosted(tuple(shards), out_shape, n * per, plan)


def _hosted_gather_second(landed):
    n = len(landed)
    per = CHIPS - 1

    def plan(src, dst, send_sems, recv_sems, local_sems, first_sem):
        x, y, c = lax.axis_index("x"), lax.axis_index("y"), lax.axis_index("c")
        copies = []
        for j, chip in enumerate(_other_chips(x, y)):
            mine, theirs = 4 * chip[0] + 2 * chip[1] + c, 4 * chip[0] + 2 * chip[1] + 1 - c
            for i in range(n):
                k = first_sem + i * per + j
                out = _remote(src[i].at[mine], dst[i].at[mine], send_sems.at[k], recv_sems.at[k], (x, y, 1 - c))
                arrival = _remote(src[i].at[theirs], dst[i].at[theirs], send_sems.at[k], recv_sems.at[k],
                                  (x, y, 1 - c))

                def wait(out=out, arrival=arrival):
                    arrival.wait_recv()
                    out.wait_send()

                copies.append(_Xfer(out.start, wait))
        return copies

    out_shape = tuple(jax.ShapeDtypeStruct(a.shape, a.dtype) for a in landed)
    return _Hosted(tuple(landed), out_shape, n * per, plan, tuple((i, i) for i in range(n)))


def _hosted_sibling_swap(arrays, sliced):
    n_sems = sum(CHIPS if s else 1 for s in sliced)

    def plan(src, dst, send_sems, recv_sems, local_sems, first_sem):
        x, y, c = lax.axis_index("x"), lax.axis_index("y"), lax.axis_index("c")
        sibling = (x, y, 1 - c)
        copies, k = [], first_sem
        for i, is_sliced in enumerate(sliced):
            pieces = [(src[i].at[2 * s + 1 - c], dst[i].at[s]) for s in range(CHIPS)] if is_sliced else [(src[i], dst[i])]
            for source, target in pieces:
                cp = _remote(source, target, send_sems.at[k], recv_sems.at[k], sibling)
                copies.append(_Xfer(cp.start, cp.wait))
                k += 1
        return copies

    out_shape = tuple(jax.ShapeDtypeStruct((CHIPS,) + a.shape[1:] if s else a.shape, a.dtype)
                      for a, s in zip(arrays, sliced))
    return _Hosted(tuple(arrays), out_shape, n_sems, plan)


def _hosted_chip_exchange(arrays, sliced):
    n = len(arrays)
    per = CHIPS - 1

    def plan(src, dst, send_sems, recv_sems, local_sems, first_sem):
        x, y, c = lax.axis_index("x"), lax.axis_index("y"), lax.axis_index("c")
        chip = 2 * x + y
        copies = []
        for i in range(n):
            own = pltpu.make_async_copy(src[i].at[chip] if sliced[i] else src[i], dst[i].at[chip],
                                        local_sems.at[first_sem + i])
            copies.append(_Xfer(own.start, own.wait))
        for d in range(1, CHIPS):
            other = chip ^ d
            to = ((other >> 1) & 1, other & 1, c)
            for i in range(n):
                k = first_sem + i * per + d - 1
                source = src[i].at[other] if sliced[i] else src[i]
                out = _remote(source, dst[i].at[chip], send_sems.at[k], recv_sems.at[k], to)
                arrival = _remote(source, dst[i].at[other], send_sems.at[k], recv_sems.at[k], to)

                def wait(out=out, arrival=arrival):
                    arrival.wait_recv()
                    out.wait_send()

                copies.append(_Xfer(out.start, wait))
        return copies

    out_shape = tuple(jax.ShapeDtypeStruct(a.shape if s else (CHIPS,) + a.shape, a.dtype)
                      for a, s in zip(arrays, sliced))
    return _Hosted(tuple(arrays), out_shape, n * per, plan)


def _add_sibling(parts, received, core, *, name):
    _, r, cols = parts.shape
    tr = min(256, r)

    def body(core_ref, a_ref, b_ref, o_ref):
        o_ref[...] = (a_ref[...] + b_ref[...]).astype(BF16)

    grid_spec = pltpu.PrefetchScalarGridSpec(
        num_scalar_prefetch=1,
        grid=(CHIPS, r // tr),
        in_specs=[pl.BlockSpec((None, tr, cols), lambda k, i, core_ref: (2 * k + core_ref[0], i, 0)),
                  pl.BlockSpec((None, tr, cols), lambda k, i, core_ref: (k, i, 0))],
        out_specs=pl.BlockSpec((None, tr, cols), lambda k, i, core_ref: (k, i, 0)),
    )
    return pl.pallas_call(body, grid_spec=grid_spec, out_shape=jax.ShapeDtypeStruct((CHIPS, r, cols), BF16),
                          compiler_params=_params("parallel", "parallel"), name=name)(core, parts, received)


def _add_whole(a, b, *, name):
    def body(a_ref, b_ref, o_ref):
        o_ref[...] = a_ref[...] + b_ref[...]

    return pl.pallas_call(body, out_shape=jax.ShapeDtypeStruct(a.shape, F32), name=name)(a, b)


def _adamw(parts, w, m, v, *, name):
    r, c = w.shape
    n_parts = parts.shape[0]
    tr = min(256, r)
    c1 = 1.0 - ADAM_B1 ** ADAM_STEP
    c2 = 1.0 - ADAM_B2 ** ADAM_STEP

    def body(p_ref, w_ref, m_ref, v_ref, g_ref, d_ref, nm_ref, nv_ref):
        g = p_ref[0].astype(F32)
        for s in range(1, n_parts):
            g = g + p_ref[s].astype(F32)
        nm = ADAM_B1 * m_ref[...] + (1.0 - ADAM_B1) * g
        nv = ADAM_B2 * v_ref[...] + (1.0 - ADAM_B2) * (g * g)
        g_ref[...] = g
        nm_ref[...] = nm
        nv_ref[...] = nv
        d_ref[...] = -ADAM_LR * ((nm / c1) / (jnp.sqrt(nv / c2) + ADAM_EPS) + ADAM_WD * w_ref[...])

    tile = pl.BlockSpec((tr, c), lambda i: (i, 0))
    return pl.pallas_call(
        body,
        grid=(r // tr,),
        in_specs=[pl.BlockSpec((n_parts, tr, c), lambda i: (0, i, 0)), tile, tile, tile],
        out_specs=[tile] * 4,
        out_shape=[jax.ShapeDtypeStruct((r, c), F32)] * 4,
        compiler_params=_params("parallel"),
        name=name,
    )(parts, w, m, v)


BIG = ("w_in", "w_rnn_proj", "w_attn_proj", "w_out", "w_up", "w_down", "w_ple_gate", "w_ple_proj")
LOSS_ROW = "loss"
SMALL = (("conv_b", 1), ("b_rg", 1), ("b_ig", 1), ("lru_lambda", 1), ("g_mlp", 1), ("g_ple", 1),
         ("q_gain", 1), ("k_gain", 1), ("sinks", 1), (LOSS_ROW, 1), ("w_rg", 64), ("w_ig", 64))
SMALL_ROWS = 144
ROW_SHARDED = ("w_rnn_proj", "w_attn_proj", "w_out", "w_down", "w_ple_gate")
COL_SHARDED = ("w_in", "w_up", "w_ple_proj")
BATCHES = {1: ("w_ple_proj", "w_ple_gate", "w_down", "w_up"), 2: ("w_out", "w_rnn_proj", "w_attn_proj"),
           3: ("w_in", "conv_w")}
SMALL_BATCH = 4


def _pack_small(vals):
    rows = []
    for nm, nrow in SMALL:
        flat = vals[nm].reshape(-1).astype(F32)
        rows.append(jnp.pad(flat, (0, nrow * D_MODEL - flat.shape[0])).reshape(nrow, D_MODEL))
    used = sum(nrow for _, nrow in SMALL)
    rows.append(jnp.zeros((SMALL_ROWS - used, D_MODEL), F32))
    return jnp.concatenate(rows, axis=0)


def _unpack_small(packed, shapes):
    out, at = {}, 0
    for nm, nrow in SMALL:
        size = 1
        for s in shapes[nm]:
            size *= s
        out[nm] = packed[at:at + nrow].reshape(-1)[:size].reshape(shapes[nm])
        at += nrow
    return out


def _full_weight(name, landed):
    if name in COL_SHARDED:
        return landed.transpose(1, 0, 2).reshape(landed.shape[1], N_DEV * landed.shape[2])
    return landed.reshape(N_DEV * landed.shape[1], landed.shape[2])


def _owner_slots(name, grad):
    if name == "w_in":
        return grad.reshape(D_MODEL, N_DEV, IN_TOTAL // N_DEV).transpose(1, 0, 2)
    if name == "conv_w":
        return grad.reshape(CONV_W, N_DEV, D_MODEL // N_DEV).transpose(1, 0, 2)
    if name in COL_SHARDED:
        return grad
    return grad.reshape(N_DEV, grad.shape[0] // N_DEV, grad.shape[1])


class _StepExchanges:
    FIRST, SECOND = "first", "second"
    PROJ, OUT, PLE_GATE, UP, DOWN = (("w_rnn_proj", "w_attn_proj"), ("w_out",), ("w_ple_gate",), ("w_up",),
                                     ("w_down", "w_ple_proj"))
    GATHERS = {"mm_in_rnn": ((FIRST, PROJ),), "mm_in_attn": ((FIRST, OUT),),
               "mm_in_gate": ((SECOND, PROJ), (FIRST, PLE_GATE)),
               "rnn_fwd": ((SECOND, OUT), (SECOND, PLE_GATE), (FIRST, UP)),
               "attn_fwd": ((SECOND, UP), (FIRST, DOWN)), "mm_rnn_proj": ((SECOND, DOWN),)}
    SWAPS = {"mm_dhm": 1, "mm_dya_in": 2, "mm_d_in_rnn": SMALL_BATCH}
    CHIP_EXCHANGES = {"rnn_bwd": (1,), "attn_bwd": (2,), "mm_d_in_rest": (SMALL_BATCH,), "mm_dh": (3,)}

    def __init__(self, shards, core):
        self.shards = shards
        self.core = core
        self.parts, self.swapped, self.summed, self.half_gathered = {}, {}, {}, {}

    def ready(self, batch, grads, extra=None):
        if batch == SMALL_BATCH:
            self.parts[batch] = ([_pack_small({**grads, **extra})], [False])
            return
        arrays = [_owner_slots(nm, grads[nm]) for nm in BATCHES[batch]]
        self.parts[batch] = (arrays, [True] * len(arrays))
        if batch not in self.SWAPS.values():
            _, self.swapped[batch] = _call(
                lambda: None, grid=(1,), in_specs=[], out_specs=[], out_shape=[], args=(), name="swap_last",
                semantics=("arbitrary",), hosted=_hosted_sibling_swap(*self.parts[batch]))

    def host(self, tag):
        if tag in self.GATHERS:
            return _merge_hosted([
                _hosted_gather_first([self.shards[nm] for nm in group]) if half == self.FIRST
                else _hosted_gather_second([self.half_gathered[nm] for nm in group])
                for half, group in self.GATHERS[tag]])
        if tag in self.SWAPS:
            return _hosted_sibling_swap(*self.parts[self.SWAPS[tag]])
        if tag in self.CHIP_EXCHANGES:
            hosted = []
            for batch in self.CHIP_EXCHANGES[tag]:
                arrays, sliced = self.parts[batch]
                labels = BATCHES.get(batch, ("small",))
                sums = [_add_sibling(a, r, self.core, name="add_" + lb) if s else _add_whole(a, r, name="add_" + lb)
                        for a, r, s, lb in zip(arrays, self.swapped[batch], sliced, labels)]
                hosted.append(_hosted_chip_exchange(sums, sliced))
            return _merge_hosted(hosted)
        return None

    def landed(self, tag, landed, weights):
        if tag in self.GATHERS:
            names = [(half, nm) for half, group in self.GATHERS[tag] for nm in group]
            for (half, nm), buf in zip(names, landed):
                if half == self.FIRST:
                    self.half_gathered[nm] = buf
                else:
                    weights[nm] = _full_weight(nm, buf)
        elif tag in self.SWAPS:
            self.swapped[self.SWAPS[tag]] = landed
        else:
            at = 0
            for batch in self.CHIP_EXCHANGES[tag]:
                count = len(self.parts[batch][0])
                self.summed[batch] = landed[at:at + count]
                at += count


def kernel(x, p, g_mix, w_in, conv_w, conv_b, w_rg, b_rg, w_ig, b_ig, lru_lambda, w_rnn_proj, q_gain, k_gain, sinks, w_attn_proj, w_out, g_mlp, w_up, w_down, g_ple, w_ple_gate, w_ple_proj, loss_target, m_g_mix, m_w_in, m_conv_w, m_conv_b, m_w_rg, m_b_rg, m_w_ig, m_b_ig, m_lru_lambda, m_w_rnn_proj, m_q_gain, m_k_gain, m_sinks, m_w_attn_proj, m_w_out, m_g_mlp, m_w_up, m_w_down, m_g_ple, m_w_ple_gate, m_w_ple_proj, v_g_mix, v_w_in, v_conv_w, v_conv_b, v_w_rg, v_b_rg, v_w_ig, v_b_ig, v_lru_lambda, v_w_rnn_proj, v_q_gain, v_k_gain, v_sinks, v_w_attn_proj, v_w_out, v_g_mlp, v_w_up, v_w_down, v_g_ple, v_w_ple_gate, v_w_ple_proj):
    names = ("g_mix", "w_in", "conv_w", "conv_b", "w_rg", "b_rg", "w_ig", "b_ig", "lru_lambda", "w_rnn_proj",
             "q_gain", "k_gain", "sinks", "w_attn_proj", "w_out", "g_mlp", "w_up", "w_down", "g_ple",
             "w_ple_gate", "w_ple_proj")
    wts = dict(zip(names, (g_mix, w_in, conv_w, conv_b, w_rg, b_rg, w_ig, b_ig, lru_lambda, w_rnn_proj, q_gain,
                           k_gain, sinks, w_attn_proj, w_out, g_mlp, w_up, w_down, g_ple, w_ple_gate, w_ple_proj)))
    mom1 = dict(zip(names, (m_g_mix, m_w_in, m_conv_w, m_conv_b, m_w_rg, m_b_rg, m_w_ig, m_b_ig, m_lru_lambda,
                            m_w_rnn_proj, m_q_gain, m_k_gain, m_sinks, m_w_attn_proj, m_w_out, m_g_mlp, m_w_up,
                            m_w_down, m_g_ple, m_w_ple_gate, m_w_ple_proj)))
    mom2 = dict(zip(names, (v_g_mix, v_w_in, v_conv_w, v_conv_b, v_w_rg, v_b_rg, v_w_ig, v_b_ig, v_lru_lambda,
                            v_w_rnn_proj, v_q_gain, v_k_gain, v_sinks, v_w_attn_proj, v_w_out, v_g_mlp, v_w_up,
                            v_w_down, v_g_ple, v_w_ple_gate, v_w_ple_proj)))
    n_seq, seq, _ = x.shape
    core = lax.axis_index("c").astype(jnp.int32).reshape(1)

    shards = {nm: wts[nm][0].astype(BF16) for nm in BIG}
    w_in_all, conv_all = _gather_two_level([shards["w_in"], conv_w[0]], name="gather_w_in")
    w = {nm: wts[nm] for nm in names if nm not in BIG}
    w["w_rg"], w["w_ig"] = w_rg[0], w_ig[0]
    w["conv_w"] = conv_all.transpose(1, 0, 2).reshape(CONV_W, D_MODEL)
    w["w_in"] = _full_weight("w_in", w_in_all)
    comm = _StepExchanges(shards, core)
    loss_sum, grad_x, g = _local_step(
        x.reshape(n_seq * seq, D_MODEL), p.reshape(n_seq * seq, PLE_DIM), loss_target.reshape(n_seq * seq, D_MODEL),
        w, n_seq=n_seq, seq=seq, comm=comm)
    del loss_sum

    res = {}
    for batch, batch_names in BATCHES.items():
        for nm, summed in zip(batch_names, comm.summed[batch]):
            res[nm] = _adamw(summed, wts[nm][0], mom1[nm][0], mom2[nm][0], name="adamw_" + nm)
    g_mix_parts, = _exchange([g["g_mix"]], ["gather"], name="gather_g_mix")
    res["g_mix"] = [r[0] for r in _adamw(g_mix_parts, g_mix, m_g_mix, v_g_mix, name="adamw_g_mix")]
    small_names = [nm for nm, _ in SMALL if nm != LOSS_ROW]
    full_small = {}
    for src, key in ((wts, "w"), (mom1, "m"), (mom2, "v")):
        vals = {nm: src[nm][0] for nm in small_names}
        vals[LOSS_ROW] = jnp.zeros((1,), F32)
        full_small[key] = _pack_small(vals)
    small_res = _adamw(comm.summed[SMALL_BATCH][0],full_small["w"], full_small["m"], full_small["v"], name="adamw_small")
    shapes = {nm: wts[nm].shape[1:] for nm in small_names}
    shapes[LOSS_ROW] = (D_MODEL,)
    small_out = [_unpack_small(r, shapes) for r in small_res]
    for nm in small_names:
        res[nm] = [so[nm] for so in small_out]
    loss = jnp.sum(small_out[0][LOSS_ROW]) * (0.5 / D_MODEL)

    outs = [loss, grad_x.reshape(n_seq, seq, D_MODEL)]
    for k in range(4):
        outs.extend(res[nm][k][None] for nm in names)
    return tuple(outs)
```

```python
import functools
from typing import Callable, NamedTuple

import jax
import jax.numpy as jnp
from jax import lax
from jax.experimental import pallas as pl
from jax.experimental.pallas import tpu as pltpu

F32 = jnp.float32
BF16 = jnp.bfloat16

N_DEV = 8
D_MODEL = 1024
RNN_BLOCK_W = 64
CONV_W = 4
LRU_C = 8.0
HEAD_DIM = 64
N_Q_HEADS = 16
N_KV_HEADS = 4
KV_W = N_KV_HEADS * HEAD_DIM
WINDOW = 128
ROPE_THETA = 10000.0
D_FF = 4096
PLE_DIM = 256
NORM_EPS = 1e-6
IN_TOTAL = 5632
COL_RNN_END, COL_ATTN_END = 2048, 3584
ATTN_W = COL_ATTN_END - COL_RNN_END
ATTN_K_AT, ATTN_V_AT = 1024, 1280

ADAM_LR = 0.001
ADAM_B1 = 0.9
ADAM_B2 = 0.999
ADAM_EPS = 1e-08
ADAM_WD = 0.01
ADAM_STEP = 10

LANES = 128
SUBLANES = 8
RNN_TILE = 256
VMEM_LIMIT = 48 * 1024 * 1024
NEG_BIG = -1e30


def _params(*sem):
    return pltpu.CompilerParams(dimension_semantics=sem if sem else None, vmem_limit_bytes=VMEM_LIMIT)


def _sig(x):
    return 0.5 * jnp.tanh(0.5 * x) + 0.5


def _dot_nt(a, b):
    return lax.dot_general(a, b, (((1,), (1,)), ((), ())), preferred_element_type=F32)


def _dot_tn(a, b):
    return lax.dot_general(a, b, (((0,), (0,)), ((), ())), preferred_element_type=F32)


class _Xfer:
    def __init__(self, start, wait):
        self.start, self.wait = start, wait


class _Hosted(NamedTuple):
    srcs: tuple
    out_shape: tuple
    n_sems: int
    plan: Callable
    aliases: tuple = ()


def _merge_hosted(parts):
    parts = [p for p in parts if p is not None]
    if len(parts) <= 1:
        return parts[0] if parts else None
    src_at, dst_at, sem_at, aliases = [0], [0], [0], []
    for p in parts:
        aliases += [(i + src_at[-1], j + dst_at[-1]) for i, j in p.aliases]
        src_at.append(src_at[-1] + len(p.srcs))
        dst_at.append(dst_at[-1] + len(p.out_shape))
        sem_at.append(sem_at[-1] + p.n_sems)

    def plan(src, dst, send_sems, recv_sems, local_sems, first_sem):
        copies = []
        for k, p in enumerate(parts):
            copies += p.plan(src[src_at[k]:src_at[k + 1]], dst[dst_at[k]:dst_at[k + 1]], send_sems, recv_sems,
                             local_sems, first_sem + sem_at[k])
        return copies

    return _Hosted(tuple(a for p in parts for a in p.srcs), tuple(s for p in parts for s in p.out_shape),
                   sem_at[-1], plan, tuple(aliases))


def _call(body, *, grid, in_specs, out_specs, out_shape, args, name, semantics, scratch_shapes=(), hosted=None):
    if hosted is None:
        outs = pl.pallas_call(body, grid=grid, in_specs=list(in_specs), out_specs=list(out_specs),
                              out_shape=list(out_shape), scratch_shapes=list(scratch_shapes),
                              compiler_params=_params(*semantics), name=name)(*args)
        return list(outs), []
    counts = (len(in_specs), len(hosted.srcs), len(out_specs), len(hosted.out_shape), len(scratch_shapes), 3)

    def wrapped(*refs):
        at, groups = 0, []
        for count in counts:
            groups.append(refs[at:at + count])
            at += count
        ins, srcs, outs, dsts, scratch, sems = groups
        copies = hosted.plan(srcs, dsts, *sems, 0)
        ids = [pl.program_id(axis) for axis in range(len(grid))]
        first = functools.reduce(jnp.logical_and, [i == 0 for i in ids])
        last = functools.reduce(jnp.logical_and, [i == g - 1 for i, g in zip(ids, grid)])

        @pl.when(first)
        def _():
            for cp in copies:
                cp.start()

        body(*ins, *outs, *scratch)

        @pl.when(last)
        def _():
            for cp in copies:
                cp.wait()

    any_spec = pl.BlockSpec(memory_space=pl.ANY)
    sems = [pltpu.SemaphoreType.DMA((hosted.n_sems,))] * 3
    outs = pl.pallas_call(
        wrapped, grid=grid, in_specs=list(in_specs) + [any_spec] * counts[1],
        out_specs=list(out_specs) + [any_spec] * counts[3], out_shape=list(out_shape) + list(hosted.out_shape),
        scratch_shapes=list(scratch_shapes) + sems, compiler_params=_params(*["arbitrary"] * len(grid)),
        input_output_aliases={counts[0] + i: counts[2] + j for i, j in hosted.aliases},
        name=name)(*args, *hosted.srcs)
    return list(outs[:counts[2]]), list(outs[counts[2]:])


def _dividing_tile(n, want):
    tile = min(want, n)
    while n % tile:
        tile -= LANES
    return tile


def _matmul(a, b, *, mode, tm, tn, out_dtypes, name, epilogue=None, extras=(), hosted=None, b_cols=None,
            row_vecs=(), n_row_sums=0, extra_col_blocks=None):
    a_parts = tuple(a) if isinstance(a, (tuple, list)) else (a,)
    b_parts = tuple(b) if isinstance(b, (tuple, list)) else (b,)
    assert len(a_parts) == len(b_parts) and (mode == "nt" or len(a_parts) == 1)
    n_parts = len(a_parts)
    m = a_parts[0].shape[0]
    if b_cols is None:
        b_cols = [(0, bp.shape[1]) for bp in b_parts]
    n = b_cols[0][1] if mode == "nn" else b_parts[0].shape[0]
    tm, tn = min(tm, m), _dividing_tile(n, tn)
    n_extra = len(extras) + len(row_vecs)
    n_tiles_out = len(out_dtypes)
    assert n_row_sums == 0 or n == tn

    def body(*refs):
        a_refs, b_refs = refs[:n_parts], refs[n_parts:2 * n_parts]
        rest = refs[2 * n_parts:]
        extra_refs, out_refs = rest[:n_extra], rest[n_extra:]
        if mode == "nn":
            acc = jnp.dot(a_refs[0][...], b_refs[0][...], preferred_element_type=F32)
        else:
            acc = _dot_nt(a_refs[0][...], b_refs[0][...])
            for a_ref, b_ref in zip(a_refs[1:], b_refs[1:]):
                acc = acc + _dot_nt(a_ref[...], b_ref[...])
        res = epilogue(acc, *[e[...] for e in extra_refs]) if epilogue is not None else (acc,)
        for o_ref, r in zip(out_refs[:n_tiles_out], res):
            o_ref[...] = r.astype(o_ref.dtype)
        if n_row_sums:
            @pl.when(pl.program_id(0) == 0)
            def _():
                for o_ref in out_refs[n_tiles_out:]:
                    o_ref[...] = jnp.zeros_like(o_ref)

            for o_ref, r in zip(out_refs[n_tiles_out:], res[n_tiles_out:]):
                o_ref[...] += r

    a_specs = [pl.BlockSpec((tm, ap.shape[1]), lambda i, j: (i, 0)) for ap in a_parts]
    if mode == "nn":
        assert b_cols[0][0] % tn == 0
        first = b_cols[0][0] // tn
        b_specs = [pl.BlockSpec((b_parts[0].shape[0], tn), lambda i, j: (0, first + j))]
    else:
        assert all(at % width == 0 for at, width in b_cols)
        b_specs = [pl.BlockSpec((tn, width), functools.partial(lambda i, j, blk: (j, blk), blk=at // width))
                   for at, width in b_cols]
    tile = pl.BlockSpec((tm, tn), lambda i, j: (i, j))
    row = pl.BlockSpec((1, tn), lambda i, j: (0, j))
    extra_specs = [pl.BlockSpec((tm, tn), functools.partial(lambda i, j, first: (i, first + j), first=first))
                   for first in (extra_col_blocks or [0] * len(extras))]
    outs, landed = _call(
        body,
        grid=(m // tm, n // tn),
        in_specs=a_specs + b_specs + extra_specs + [row] * len(row_vecs),
        out_specs=[tile] * n_tiles_out + [row] * n_row_sums,
        out_shape=[jax.ShapeDtypeStruct((m, n), dt) for dt in out_dtypes]
        + [jax.ShapeDtypeStruct((1, n), F32)] * n_row_sums,
        args=(*a_parts, *b_parts, *extras, *row_vecs), name=name,
        semantics=("arbitrary" if n_row_sums else "parallel", "arbitrary"), hosted=hosted)
    if hosted is not None:
        return (*outs, landed)
    return outs[0] if len(outs) == 1 else outs


def _matmul_tn(a, b, *, tk, tn, tt, name, slot_cols=None):
    t, k = a.shape
    n = b.shape[1]
    tk, tn, tt = min(tk, k), _dividing_tile(n, tn), min(tt, t)

    def body(a_ref, b_ref, o_ref):
        @pl.when(pl.program_id(2) == 0)
        def _():
            o_ref[...] = jnp.zeros_like(o_ref)

        if slot_cols is None:
            o_ref[...] += _dot_tn(a_ref[...], b_ref[...])
        else:
            av = a_ref[...]
            for s in range(tn // slot_cols):
                o_ref[s] += _dot_tn(av, b_ref[:, s * slot_cols:(s + 1) * slot_cols])

    if slot_cols is not None:
        out_spec = pl.BlockSpec((tn // slot_cols, tk, slot_cols), lambda i, j, s: (j, i, 0))
        out_shape = jax.ShapeDtypeStruct((n // slot_cols, k, slot_cols), F32)
    else:
        out_spec = pl.BlockSpec((tk, tn), lambda i, j, s: (i, j))
        out_shape = jax.ShapeDtypeStruct((k, n), F32)
    return pl.pallas_call(
        body,
        grid=(k // tk, n // tn, t // tt),
        in_specs=[pl.BlockSpec((tt, tk), lambda i, j, s: (s, i)), pl.BlockSpec((tt, tn), lambda i, j, s: (s, j))],
        out_specs=out_spec,
        out_shape=out_shape,
        compiler_params=_params("parallel", "parallel", "arbitrary"),
        name=name,
    )(a, b)


def _matmul_tn_multi(a, bs, *, tt, name, hosted=None):
    t, k = a.shape
    tt = min(tt, t)
    n_b = len(bs)

    def body(a_ref, *refs):
        b_refs, o_refs = refs[:n_b], refs[n_b:]

        @pl.when(pl.program_id(0) == 0)
        def _():
            for o_ref in o_refs:
                o_ref[...] = jnp.zeros_like(o_ref)

        a_t = a_ref[...].T
        for b_ref, o_ref in zip(b_refs, o_refs):
            o_ref[...] += jnp.dot(a_t, b_ref[...], preferred_element_type=F32)

    outs, landed = _call(
        body,
        grid=(t // tt,),
        in_specs=[pl.BlockSpec((tt, k), lambda s: (s, 0))] + [pl.BlockSpec((tt, b.shape[1]), lambda s: (s, 0)) for b in bs],
        out_specs=[pl.BlockSpec((k, b.shape[1]), lambda s: (0, 0)) for b in bs],
        out_shape=[jax.ShapeDtypeStruct((k, b.shape[1]), F32) for b in bs],
        args=(a, *bs), name=name, semantics=("arbitrary",), hosted=hosted)
    return (*outs, landed) if hosted is not None else outs


def _rmsnorm_rows(x, g):
    return x * lax.rsqrt(jnp.mean(x * x, axis=-1, keepdims=True) + NORM_EPS) * g


def _rmsnorm_fwd(x, g, *, name):
    t, d = x.shape
    tm = min(512, t)

    def body(x_ref, g_ref, o_ref):
        o_ref[...] = _rmsnorm_rows(x_ref[...], g_ref[...]).astype(BF16)

    return pl.pallas_call(
        body,
        grid=(t // tm,),
        in_specs=[pl.BlockSpec((tm, d), lambda i: (i, 0)), pl.BlockSpec((1, d), lambda i: (0, 0))],
        out_specs=pl.BlockSpec((tm, d), lambda i: (i, 0)),
        out_shape=jax.ShapeDtypeStruct((t, d), BF16),
        compiler_params=_params("parallel"),
        name=name,
    )(x, g)


def _rmsnorm_bwd_rows(dy, x, dres, g):
    r = lax.rsqrt(jnp.mean(x * x, axis=-1, keepdims=True) + NORM_EPS)
    xr = x * r
    gy = dy * g
    dx = dres + r * (gy - xr * jnp.mean(gy * xr, axis=-1, keepdims=True))
    return dx, jnp.sum(dy * xr, axis=0, keepdims=True)


def _softplus_neg(lam):
    z = -lam
    return jnp.maximum(z, 0.0) + jnp.log1p(jnp.exp(-jnp.abs(z)))


def _neg_expm1(y, exp_half_y):
    series = -y * (1.0 + y * 0.5 * (1.0 + y * (1.0 / 3.0) * (1.0 + y * 0.25 * (1.0 + y * 0.2))))
    return jnp.where(y > -0.0625, series, 1.0 - exp_half_y * exp_half_y)


def _gelu_parts(x):
    c = 0.7978845608028654
    u = c * (x + 0.044715 * x * x * x)
    th = jnp.tanh(u)
    gel = 0.5 * x * (1.0 + th)
    dgel = 0.5 * (1.0 + th) + 0.5 * x * (1.0 - th * th) * c * (1.0 + 3.0 * 0.044715 * x * x)
    return gel, dgel


def _shift_down(v, k, rows):
    return jnp.where(rows < k, 0.0, pltpu.roll(v, k, 0))


def _shift_up(v, k, rows, n):
    return jnp.where(rows >= n - k, 0.0, pltpu.roll(v, n - k, 0))


def _scan_within_groups(a, b, *, reverse):
    shape = a.shape
    a = a.reshape(shape[0] // SUBLANES, SUBLANES, shape[1])
    b = b.reshape(a.shape)
    in_group = lax.broadcasted_iota(jnp.int32, a.shape, 1)
    for s in (1, 2, 4):
        if reverse:
            inside, shift = in_group < SUBLANES - s, SUBLANES - s
        else:
            inside, shift = in_group >= s, s
        b = b + a * jnp.where(inside, pltpu.roll(b, shift, 1), 0.0)
        a = a * jnp.where(inside, pltpu.roll(a, shift, 1), 1.0)
    return a.reshape(shape), b.reshape(shape)


def _rnn_gates(xc, wrg, brg, wig, big, lam):
    xcb = xc.astype(BF16)
    r = _sig(jnp.dot(xcb, wrg, preferred_element_type=F32) + brg)
    i = _sig(jnp.dot(xcb, wig, preferred_element_type=F32) + big)
    sp = _softplus_neg(lam)
    log_a = -LRU_C * r * sp
    a = jnp.exp(log_a)
    mult = jnp.sqrt(_neg_expm1(2.0 * log_a, a))
    return xcb, r, i, sp, a, mult


def _conv_fwd(xv, cw, cb, rows):
    return (cb + _shift_down(xv, 3, rows) * cw[0:1, :] + _shift_down(xv, 2, rows) * cw[1:2, :]
            + _shift_down(xv, 1, rows) * cw[2:3, :] + xv * cw[3:4, :])


def _rnn_fwd(z, conv_w, conv_b, wrg_bd, b_rg, wig_bd, b_ig, lam, *, n_seq, seq, hosted=None):
    t = n_seq * seq
    ct = RNN_TILE
    n_ct = D_MODEL // ct

    def body(x_ref, g_ref, cw_ref, cb_ref, wrg_ref, brg_ref, wig_ref, big_ref, lam_ref,
             xc_ref, hr_ref, ya_ref, a_s, b_s):
        rows = lax.broadcasted_iota(jnp.int32, (seq, ct), 0)
        xc = _conv_fwd(x_ref[...], cw_ref[...], cb_ref[...], rows)
        _, r, i, sp, a, mult = _rnn_gates(xc, wrg_ref[...], brg_ref[...], wig_ref[...], big_ref[...], lam_ref[...])
        a_s[...], b_s[...] = _scan_within_groups(a, mult * (i * xc), reverse=False)

        def step(j, carry):
            r0 = pl.multiple_of(j * SUBLANES, SUBLANES)
            h = b_s[pl.ds(r0, SUBLANES), :] + a_s[pl.ds(r0, SUBLANES), :] * carry
            hr_ref[pl.ds(r0, SUBLANES), :] = h
            return h[SUBLANES - 1:SUBLANES, :]

        lax.fori_loop(0, seq // SUBLANES, step, jnp.zeros((1, ct), F32), unroll=4)
        gel, _ = _gelu_parts(g_ref[...])
        xc_ref[...] = xc
        ya_ref[...] = (hr_ref[...] * gel).astype(BF16)

    vec = pl.BlockSpec((1, ct), lambda b, c: (0, c))
    gate_w = pl.BlockSpec((None, ct, ct), lambda b, c: (c, 0, 0))
    tile = pl.BlockSpec((seq, ct), lambda b, c: (b, c))
    outs, landed = _call(
        body,
        grid=(n_seq, n_ct),
        in_specs=[
            pl.BlockSpec((seq, ct), lambda b, c: (b, c)),
            pl.BlockSpec((seq, ct), lambda b, c: (b, n_ct + c)),
            pl.BlockSpec((CONV_W, ct), lambda b, c: (0, c)), vec, gate_w, vec, gate_w, vec, vec,
        ],
        out_specs=[tile, tile, tile],
        out_shape=[jax.ShapeDtypeStruct((t, D_MODEL), F32), jax.ShapeDtypeStruct((t, D_MODEL), F32),
                   jax.ShapeDtypeStruct((t, D_MODEL), BF16)],
        scratch_shapes=[pltpu.VMEM((seq, ct), F32), pltpu.VMEM((seq, ct), F32)],
        args=(z, z, conv_w, conv_b, wrg_bd, b_rg, wig_bd, b_ig, lam), name="rnn_fwd",
        semantics=("parallel", "parallel"), hosted=hosted)
    return (*outs, landed) if hosted is not None else outs


def _rnn_bwd(dya, z, xc, hr, conv_w, wrg_bd, b_rg, wig_bd, b_ig, lam, *, n_seq, seq, hosted=None):
    t = n_seq * seq
    ct = RNN_TILE
    n_ct = D_MODEL // ct

    def body(dya_ref, x_ref, g_ref, xc_ref, hr_ref, cw_ref, wrg_ref, brg_ref, wig_ref, big_ref, lam_ref,
             dx_ref, dg_ref, dwrg_ref, dwig_ref, vec_ref, a_s, d_s, g_s):
        rows = lax.broadcasted_iota(jnp.int32, (seq, ct), 0)
        xv, xc, hr, dyv = x_ref[...], xc_ref[...], hr_ref[...], dya_ref[...]
        lamv = lam_ref[...]
        gel, dgel = _gelu_parts(g_ref[...])
        dg_ref[...] = (dyv * hr * dgel).astype(BF16)
        xcb, r, i, sp, a, mult = _rnn_gates(xc, wrg_ref[...], brg_ref[...], wig_ref[...], big_ref[...], lamv)
        a_s[...], d_s[...] = _scan_within_groups(_shift_up(a, 1, rows, seq), dyv * gel, reverse=True)

        def step(k, carry):
            r0 = pl.multiple_of((seq // SUBLANES - 1 - k) * SUBLANES, SUBLANES)
            gs = d_s[pl.ds(r0, SUBLANES), :] + a_s[pl.ds(r0, SUBLANES), :] * carry
            g_s[pl.ds(r0, SUBLANES), :] = gs
            return gs[0:1, :]

        lax.fori_loop(0, seq // SUBLANES, step, jnp.zeros((1, ct), F32), unroll=4)
        gsum = g_s[...]
        gated = i * xc
        d_log_a = gsum * _shift_down(hr, 1, rows) * a - gsum * gated * (a * a / mult)
        d_gated = gsum * mult
        d_pre_r = (d_log_a * (-LRU_C) * sp) * r * (1.0 - r)
        d_pre_i = (d_gated * xc) * i * (1.0 - i)
        dprb, dpib = d_pre_r.astype(BF16), d_pre_i.astype(BF16)
        dxc = d_gated * i + _dot_nt(dprb, wrg_ref[...]) + _dot_nt(dpib, wig_ref[...])
        cw = cw_ref[...]
        dx = (dxc * cw[3:4, :] + _shift_up(dxc, 1, rows, seq) * cw[2:3, :]
              + _shift_up(dxc, 2, rows, seq) * cw[1:2, :] + _shift_up(dxc, 3, rows, seq) * cw[0:1, :])
        dx_ref[...] = dx.astype(BF16)

        @pl.when(pl.program_id(1) == 0)
        def _():
            dwrg_ref[...] = jnp.zeros_like(dwrg_ref)
            dwig_ref[...] = jnp.zeros_like(dwig_ref)
            vec_ref[...] = jnp.zeros_like(vec_ref)

        dwrg_ref[...] += _dot_tn(xcb, dprb)
        dwig_ref[...] += _dot_tn(xcb, dpib)

        def colsum(v):
            return jnp.sum(v, axis=0, keepdims=True)

        d_sp = colsum(d_log_a * (-LRU_C) * r)
        vec_ref[0:1, :] += colsum(d_pre_r)
        vec_ref[1:2, :] += colsum(d_pre_i)
        vec_ref[2:3, :] += d_sp * (-_sig(-lamv))
        vec_ref[3:4, :] += colsum(dxc)
        vec_ref[4:5, :] += colsum(dxc * _shift_down(xv, 3, rows))
        vec_ref[5:6, :] += colsum(dxc * _shift_down(xv, 2, rows))
        vec_ref[6:7, :] += colsum(dxc * _shift_down(xv, 1, rows))
        vec_ref[7:8, :] += colsum(dxc * xv)

    vec = pl.BlockSpec((1, ct), lambda c, b: (0, c))
    gate_w = pl.BlockSpec((None, ct, ct), lambda c, b: (c, 0, 0))
    tile = pl.BlockSpec((seq, ct), lambda c, b: (b, c))
    outs, landed = _call(
        body,
        grid=(n_ct, n_seq),
        in_specs=[
            tile,
            pl.BlockSpec((seq, ct), lambda c, b: (b, c)),
            pl.BlockSpec((seq, ct), lambda c, b: (b, n_ct + c)),
            tile, tile,
            pl.BlockSpec((CONV_W, ct), lambda c, b: (0, c)), gate_w, vec, gate_w, vec, vec,
        ],
        out_specs=[tile, tile, gate_w, gate_w, pl.BlockSpec((8, ct), lambda c, b: (0, c))],
        out_shape=[jax.ShapeDtypeStruct((t, D_MODEL), BF16), jax.ShapeDtypeStruct((t, D_MODEL), BF16),
                   jax.ShapeDtypeStruct((n_ct, ct, ct), F32), jax.ShapeDtypeStruct((n_ct, ct, ct), F32),
                   jax.ShapeDtypeStruct((8, D_MODEL), F32)],
        scratch_shapes=[pltpu.VMEM((seq, ct), F32)] * 3,
        args=(dya, z, z, xc, hr, conv_w, wrg_bd, b_rg, wig_bd, b_ig, lam), name="rnn_bwd",
        semantics=("parallel", "arbitrary"), hosted=hosted)
    return (*outs, landed) if hosted is not None else outs


def _split_hi_lo(x):
    hi = x.astype(BF16)
    return hi, (x - hi.astype(F32)).astype(BF16)


def _dot_split(x, m_twice):
    hi, lo = _split_hi_lo(x)
    return jnp.dot(jnp.concatenate([hi, lo], axis=1), m_twice, preferred_element_type=F32)


def _head_matrices(width):
    ec = ((lax.broadcasted_iota(jnp.int32, (2 * width, LANES), 0) & (width - 1)) // HEAD_DIM
          == lax.broadcasted_iota(jnp.int32, (2 * width, LANES), 1))
    ee = (lax.broadcasted_iota(jnp.int32, (2 * LANES, width), 1) // HEAD_DIM
          == (lax.broadcasted_iota(jnp.int32, (2 * LANES, width), 0) & (LANES - 1)))
    return jnp.where(ec, 1.0, 0.0).astype(BF16), jnp.where(ee, 1.0, 0.0).astype(BF16)


def _swap_halves(y):
    w = y.shape[1]
    first = (lax.broadcasted_iota(jnp.int32, y.shape, 1) % HEAD_DIM) < HEAD_DIM // 2
    return jnp.where(first, pltpu.roll(y, w - HEAD_DIM // 2, 1), pltpu.roll(y, HEAD_DIM // 2, 1))


def _normrope_fwd(x, gain, cos_t, sin_t, ec, ee):
    w = x.shape[1]
    rs = _dot_split(lax.rsqrt(_dot_split(x * x, ec) * (1.0 / HEAD_DIM) + NORM_EPS), ee)
    nx = x * rs
    y = nx * gain
    reps = w // LANES
    out = y * jnp.tile(cos_t, (1, reps)) + _swap_halves(y) * jnp.tile(sin_t, (1, reps))
    return out, nx, rs


def _normrope_bwd(dout, nx, rs, gain, cos_t, sin_t, ec, ee):
    w = dout.shape[1]
    reps = w // LANES
    dy = dout * jnp.tile(cos_t, (1, reps)) + _swap_halves(dout * jnp.tile(sin_t, (1, reps)))
    dgain = jnp.sum(dy * nx, axis=0, keepdims=True)
    dn = dy * gain
    seg = _dot_split(_dot_split(dn * nx, ec) * (1.0 / HEAD_DIM), ee)
    return rs * (dn - nx * seg), dgain


def _pair_operand(t, group):
    chunk = t[:, (group // 2) * LANES:(group // 2 + 1) * LANES]
    low = lax.broadcasted_iota(jnp.int32, chunk.shape, 1) < HEAD_DIM
    rolled = pltpu.roll(chunk, HEAD_DIM, 1)
    return jnp.where(low, chunk, rolled) if group % 2 == 0 else jnp.where(low, rolled, chunk)


GROUP = N_Q_HEADS // N_KV_HEADS
GROUP_W = GROUP * HEAD_DIM


def _replicate_head(t, group):
    return jnp.tile(_pair_operand(t, group), (1, 2))


def _head_blocks(t):
    seg = lax.broadcasted_iota(jnp.int32, t.shape, 1) // HEAD_DIM
    return jnp.concatenate([jnp.where(seg == h, t, 0.0) for h in range(GROUP)], axis=0)


def _stack_heads(t_t, rows):
    return jnp.concatenate([t_t[:, h * rows:(h + 1) * rows] for h in range(GROUP)], axis=0)


def _head_rows(mat_t, group):
    return jnp.concatenate([mat_t[GROUP * group + h:GROUP * group + h + 1, :] for h in range(GROUP)], axis=1)


def _window_masks(blk):
    key = lax.broadcasted_iota(jnp.int32, (blk, GROUP * blk), 0)
    query = lax.broadcasted_iota(jnp.int32, (blk, GROUP * blk), 1) & (blk - 1)
    return key > query, key <= query


def _mask_window(t, before_ok, own_ok, fill):
    blk = t.shape[0] // 2
    return jnp.concatenate([jnp.where(before_ok, t[:blk], fill), jnp.where(own_ok, t[blk:], fill)], axis=0)


def _attn_fwd(z, cos_t, sin_t, q_gain_t, k_gain_t, sinks_t, *, n_seq, seq, hosted=None):
    t = n_seq * seq
    blk = WINDOW
    nb = seq // blk

    def body(q_ref, kp_ref, kc_ref, vp_ref, vc_ref, cosc_ref, sinc_ref, cosp_ref, sinp_ref, qg_ref, kg_ref, sk_ref,
             o_ref, l_ref):
        n = pl.program_id(1)
        ecq, eeq = _head_matrices(D_MODEL)
        eck, eek = _head_matrices(KV_W)
        cosc, sinc = cosc_ref[...], sinc_ref[...]
        qh, _, _ = _normrope_fwd(q_ref[...], qg_ref[...], cosc, sinc, ecq, eeq)
        qh = qh * (HEAD_DIM ** -0.5)
        kc, _, _ = _normrope_fwd(kc_ref[...], kg_ref[...], cosc, sinc, eck, eek)
        kp, _, _ = _normrope_fwd(kp_ref[...], kg_ref[...], cosp_ref[...], sinp_ref[...], eck, eek)
        kcat = jnp.concatenate([kp, kc], axis=0)
        vcat = jnp.concatenate([vp_ref[...], vc_ref[...]], axis=0)
        above, causal = _window_masks(blk)
        above = above & (n > 0)
        head_row = lax.broadcasted_iota(jnp.int32, (blk, blk), 0)
        sk_t = jnp.broadcast_to(sk_ref[...], (blk, LANES)).T
        vcat_t = vcat.T.astype(BF16)
        lmat = jnp.zeros((blk, blk), F32)
        groups = range(N_KV_HEADS)
        cols = [slice(g * GROUP_W, (g + 1) * GROUP_W) for g in groups]
        scores = [_dot_nt(_replicate_head(kcat, g).astype(BF16), _head_blocks(qh[:, cols[g]]).astype(BF16))
                  for g in groups]
        probs = []
        for g in groups:
            s = _mask_window(scores[g], above, causal, NEG_BIG)
            sink = _head_rows(sk_t, g)
            m = jnp.maximum(jnp.max(s, axis=0, keepdims=True), sink)
            e = jnp.exp(s - m)
            den = jnp.sum(e, axis=0, keepdims=True) + jnp.exp(sink - m)
            probs.append((e * (1.0 / den)).astype(BF16))
            lse = m + jnp.log(den)
            for h in range(GROUP):
                lmat = lmat + jnp.where(head_row == GROUP * g + h, lse[:, h * blk:(h + 1) * blk], 0.0)
        for g in groups:
            out_t = jnp.dot(vcat_t[g * HEAD_DIM:(g + 1) * HEAD_DIM], probs[g], preferred_element_type=F32)
            o_ref[:, cols[g]] = _stack_heads(out_t, blk).T.astype(BF16)
        l_ref[...] = lmat

    def row(b, n):
        return b * nb + n

    def prev(b, n):
        return b * nb + jnp.maximum(n - 1, 0)

    kw = KV_W
    tab_c = pl.BlockSpec((blk, LANES), lambda b, n: (n, 0))
    tab_p = pl.BlockSpec((blk, LANES), lambda b, n: (jnp.maximum(n - 1, 0), 0))
    outs, landed = _call(
        body,
        grid=(n_seq, nb),
        in_specs=[
            pl.BlockSpec((blk, D_MODEL), lambda b, n: (row(b, n), COL_RNN_END // D_MODEL)),
            pl.BlockSpec((blk, kw), lambda b, n: (prev(b, n), (COL_RNN_END + ATTN_K_AT) // kw)),
            pl.BlockSpec((blk, kw), lambda b, n: (row(b, n), (COL_RNN_END + ATTN_K_AT) // kw)),
            pl.BlockSpec((blk, kw), lambda b, n: (prev(b, n), (COL_RNN_END + ATTN_V_AT) // kw)),
            pl.BlockSpec((blk, kw), lambda b, n: (row(b, n), (COL_RNN_END + ATTN_V_AT) // kw)),
            tab_c, tab_c, tab_p, tab_p,
            pl.BlockSpec((1, D_MODEL), lambda b, n: (0, 0)),
            pl.BlockSpec((1, kw), lambda b, n: (0, 0)),
            pl.BlockSpec((1, LANES), lambda b, n: (0, 0)),
        ],
        out_specs=[pl.BlockSpec((blk, D_MODEL), lambda b, n: (row(b, n), 0)),
                   pl.BlockSpec((blk, LANES), lambda b, n: (row(b, n), 0))],
        out_shape=[jax.ShapeDtypeStruct((t, D_MODEL), BF16), jax.ShapeDtypeStruct((t, LANES), F32)],
        args=(z, z, z, z, z, cos_t, sin_t, cos_t, sin_t, q_gain_t, k_gain_t, sinks_t), name="attn_fwd",
        semantics=("parallel", "parallel"), hosted=hosted)
    return (*outs, landed) if hosted is not None else outs


def _attn_bwd(z, o, lse, do, cos_t, sin_t, q_gain_t, k_gain_t, sinks_t, *, n_seq, seq, hosted=None):
    t = n_seq * seq
    blk = WINDOW
    nb = seq // blk
    kw = KV_W
    scale = HEAD_DIM ** -0.5

    def body(qc_ref, qn_ref, kc_ref, vp_ref, vc_ref, oc_ref, on_ref, doc_ref, don_ref, lc_ref, ln_ref,
             cosc_ref, sinc_ref, cosn_ref, sinn_ref, qg_ref, kg_ref, sk_ref,
             dz_ref, vec_ref, dq_s, q_s, k_s):
        n = pl.program_id(1)
        ecq, eeq = _head_matrices(D_MODEL)
        eck, eek = _head_matrices(KV_W)
        cosc, sinc = cosc_ref[...], sinc_ref[...]
        qg, kg = qg_ref[...], kg_ref[...]
        own, other = n & 1, 1 - (n & 1)

        @pl.when(n == 0)
        def _():
            for part, value in enumerate(_normrope_fwd(qc_ref[...], qg, cosc, sinc, ecq, eeq)):
                q_s[own, part] = value
            k_s[other] = jnp.zeros((blk, kw), F32)

        for part, value in enumerate(_normrope_fwd(qn_ref[...], qg, cosn_ref[...], sinn_ref[...], ecq, eeq)):
            q_s[other, part] = value
        qhc, nqc, rsqc = q_s[own, 0], q_s[own, 1], q_s[own, 2]
        qhn = q_s[other, 0]
        khc, nkc, rskc = _normrope_fwd(kc_ref[...], kg, cosc, sinc, eck, eek)
        khp = k_s[other]
        k_s[own] = khc
        doc = doc_ref[...].astype(F32)
        don = don_ref[...].astype(F32)
        delc = _dot_split(doc * oc_ref[...].astype(F32), ecq)
        deln = _dot_split(don * on_ref[...].astype(F32), ecq)
        lc_t, ln_t, delc_t, deln_t = lc_ref[...], ln_ref[...], delc.T, deln.T
        above, causal = _window_masks(blk)
        above_c, above_n = above & (n > 0), above & (n < nb - 1)
        seg = lax.broadcasted_iota(jnp.int32, (blk, GROUP_W), 1) // HEAD_DIM
        lane = lax.broadcasted_iota(jnp.int32, (1, LANES), 1)
        sk_t = jnp.broadcast_to(sk_ref[...], (blk, LANES)).T
        dsink = jnp.zeros((1, LANES), F32)
        kcat = jnp.concatenate([khp, khc], axis=0)
        vcat = jnp.concatenate([vp_ref[...], vc_ref[...]], axis=0)
        kcat_t = kcat.T.astype(BF16)
        dkh = jnp.zeros((blk, GROUP_W), F32)
        dvh = jnp.zeros((blk, GROUP_W), F32)

        def fold_to(group, t):
            total = t + pltpu.roll(t, HEAD_DIM, 1)
            total = total + pltpu.roll(total, 2 * HEAD_DIM, 1)
            return jnp.where(seg == group, total, 0.0)

        groups = range(N_KV_HEADS)
        cols = [slice(g * GROUP_W, (g + 1) * GROUP_W) for g in groups]
        qsc, qsn = qhc * scale, qhn * scale
        qb_c = [_head_blocks(qsc[:, cols[g]]).astype(BF16) for g in groups]
        qb_n = [_head_blocks(qsn[:, cols[g]]).astype(BF16) for g in groups]
        dob_c = [_head_blocks(doc[:, cols[g]]).astype(BF16) for g in groups]
        dob_n = [_head_blocks(don[:, cols[g]]).astype(BF16) for g in groups]
        raw = []
        for g in groups:
            krep = _replicate_head(kcat, g).astype(BF16)
            vrep = _replicate_head(vcat, g).astype(BF16)
            raw.append((_dot_nt(krep, qb_c[g]), _dot_nt(vrep, dob_c[g]),
                        _dot_nt(krep[blk:], qb_n[g]), _dot_nt(vrep[blk:], dob_n[g])))
        cooked = []
        for g in groups:
            s_c, dp_c, s_n, dp_n = raw[g]
            l_row, d_row = _head_rows(lc_t, g), _head_rows(delc_t, g)
            p_c = _mask_window(jnp.exp(s_c - l_row), above_c, causal, 0.0)
            ds_c = (p_c * (dp_c - d_row)).astype(BF16)
            p_n = jnp.where(above_n, jnp.exp(s_n - _head_rows(ln_t, g)), 0.0)
            ds_n = (p_n * (dp_n - _head_rows(deln_t, g))).astype(BF16)
            cooked.append((p_c[blk:].astype(BF16), ds_c, p_n.astype(BF16), ds_n))
            p_sink = jnp.exp(_head_rows(sk_t, g) - l_row) * d_row
            for h in range(GROUP):
                dsink = dsink + jnp.where(lane == GROUP * g + h,
                                          -jnp.sum(p_sink[:, h * blk:(h + 1) * blk], axis=1, keepdims=True), 0.0)
        for g in groups:
            p_cb, ds_c, p_nb, ds_n = cooked[g]
            dq_t = jnp.dot(kcat_t[g * HEAD_DIM:(g + 1) * HEAD_DIM], ds_c, preferred_element_type=F32)
            dq_s[:, cols[g]] = _stack_heads(dq_t, blk).T * scale
            dk_rep = (jnp.dot(ds_c[blk:], qb_c[g], preferred_element_type=F32)
                      + jnp.dot(ds_n, qb_n[g], preferred_element_type=F32))
            dv_rep = (jnp.dot(p_cb, dob_c[g], preferred_element_type=F32)
                      + jnp.dot(p_nb, dob_n[g], preferred_element_type=F32))
            dkh = dkh + fold_to(g, dk_rep)
            dvh = dvh + fold_to(g, dv_rep)
        dq, dqg = _normrope_bwd(dq_s[...], nqc, rsqc, qg, cosc, sinc, ecq, eeq)
        dk, dkg = _normrope_bwd(dkh, nkc, rskc, kg, cosc, sinc, eck, eek)
        dz_ref[:, :ATTN_K_AT] = dq.astype(BF16)
        dz_ref[:, ATTN_K_AT:ATTN_V_AT] = dk.astype(BF16)
        dz_ref[:, ATTN_V_AT:] = dvh.astype(BF16)

        @pl.when(n == 0)
        def _():
            vec_ref[...] = jnp.zeros_like(vec_ref)

        vec_ref[0:1, :] += dqg
        vec_ref[1:2, 0:kw] += dkg
        vec_ref[2:3, 0:LANES] += dsink

    def row(b, n):
        return b * nb + n

    def prev(b, n):
        return b * nb + jnp.maximum(n - 1, 0)

    def nxt(b, n):
        return b * nb + jnp.minimum(n + 1, nb - 1)

    def tiles(width, col, which):
        return pl.BlockSpec((blk, width), lambda b, n: (which(b, n), col))

    def table(which):
        return pl.BlockSpec((blk, LANES), lambda b, n: (which(0, n), 0))

    outs, landed = _call(
        body,
        grid=(n_seq, nb),
        in_specs=[
            tiles(D_MODEL, COL_RNN_END // D_MODEL, row), tiles(D_MODEL, COL_RNN_END // D_MODEL, nxt),
            tiles(kw, (COL_RNN_END + ATTN_K_AT) // kw, row),
            tiles(kw, (COL_RNN_END + ATTN_V_AT) // kw, prev), tiles(kw, (COL_RNN_END + ATTN_V_AT) // kw, row),
            tiles(D_MODEL, 0, row), tiles(D_MODEL, 0, nxt),
            tiles(D_MODEL, 0, row), tiles(D_MODEL, 0, nxt),
            tiles(LANES, 0, row), tiles(LANES, 0, nxt),
            table(row), table(row), table(nxt), table(nxt),
            pl.BlockSpec((1, D_MODEL), lambda b, n: (0, 0)),
            pl.BlockSpec((1, kw), lambda b, n: (0, 0)),
            pl.BlockSpec((1, LANES), lambda b, n: (0, 0)),
        ],
        out_specs=[tiles(ATTN_W, 0, row), pl.BlockSpec((None, 8, D_MODEL), lambda b, n: (b, 0, 0))],
        out_shape=[jax.ShapeDtypeStruct((t, ATTN_W), BF16), jax.ShapeDtypeStruct((n_seq, 8, D_MODEL), F32)],
        scratch_shapes=[pltpu.VMEM((blk, D_MODEL), F32), pltpu.VMEM((2, 3, blk, D_MODEL), F32),
                        pltpu.VMEM((2, blk, kw), F32)],
        args=(z, z, z, z, z, o, o, do, do, lse, lse, cos_t, sin_t, cos_t, sin_t,
              q_gain_t, k_gain_t, sinks_t), name="attn_bwd", semantics=("arbitrary", "arbitrary"), hosted=hosted)
    return (*outs, landed) if hosted is not None else outs


def _rope_tables(seq):
    inv = ROPE_THETA ** (-jnp.arange(0, HEAD_DIM, 2, dtype=F32) / HEAD_DIM)
    ang = jnp.arange(seq, dtype=F32)[:, None] * inv[None, :]
    cos, sin = jnp.cos(ang), jnp.sin(ang)
    return jnp.tile(jnp.concatenate([cos, cos], axis=1), (1, 2)), jnp.tile(jnp.concatenate([-sin, sin], axis=1), (1, 2))


def _block_diag_tiles(w):
    per = RNN_TILE // RNN_BLOCK_W
    w4 = w.reshape(D_MODEL // RNN_TILE, per, RNN_BLOCK_W, RNN_BLOCK_W)
    eye = jnp.eye(per, dtype=w.dtype)
    dense = jnp.einsum("tpij,pq->tpiqj", w4, eye)
    return dense.reshape(D_MODEL // RNN_TILE, RNN_TILE, RNN_TILE).astype(BF16)


def _block_diag_extract(dense):
    per = RNN_TILE // RNN_BLOCK_W
    d5 = dense.reshape(D_MODEL // RNN_TILE, per, RNN_BLOCK_W, per, RNN_BLOCK_W)
    blocks = jnp.stack([d5[:, p, :, p, :] for p in range(per)], axis=1)
    return blocks.reshape(D_MODEL // RNN_BLOCK_W, RNN_BLOCK_W, RNN_BLOCK_W)


def _local_step(x, p, target, w, *, n_seq, seq, comm=None):
    w = dict(w)

    def run(tag, fn, *args, **kwargs):
        hosted = comm.host(tag) if comm is not None else None
        if hosted is None:
            return fn(*args, **kwargs)
        *outs, landed = fn(*args, hosted=hosted, **kwargs)
        comm.landed(tag, landed, w)
        return outs[0] if len(outs) == 1 else outs

    def ready(batch, grads, extra=None):
        if comm is not None:
            comm.ready(batch, grads, extra)

    cos_t, sin_t = _rope_tables(seq)
    q_gain_t = jnp.tile(w["q_gain"], (1, N_Q_HEADS))
    k_gain_t = jnp.tile(w["k_gain"], (1, N_KV_HEADS))
    sinks_t = jnp.pad(w["sinks"], ((0, 0), (0, LANES - N_Q_HEADS)))
    wrg_bd, wig_bd = _block_diag_tiles(w["w_rg"]), _block_diag_tiles(w["w_ig"])
    dims = dict(n_seq=n_seq, seq=seq)

    h = _rmsnorm_fwd(x, w["g_mix"], name="norm_mix")
    z = run("mm_in", _matmul, h, w["w_in"], mode="nn", tm=1024, tn=IN_TOTAL // 4, out_dtypes=[F32], name="mm_in")
    gate_tile = 512
    ga_at, gb_at = COL_ATTN_END // gate_tile, (COL_ATTN_END + D_MODEL) // gate_tile
    xc, hr, ya_in = run("rnn_fwd", _rnn_fwd, z, w["conv_w"], w["conv_b"], wrg_bd, w["b_rg"], wig_bd, w["b_ig"],
                        w["lru_lambda"], **dims)
    o, lse = run("attn_fwd", _attn_fwd, z, cos_t, sin_t, q_gain_t, k_gain_t, sinks_t, **dims)
    ya = run("mm_rnn_proj", _matmul, ya_in, w["w_rnn_proj"], mode="nn", tm=1024, tn=1024, out_dtypes=[F32],
             name="mm_rnn_proj")
    yb, merged = _matmul(
        o, w["w_attn_proj"], mode="nn", tm=1024, tn=gate_tile, out_dtypes=[F32, BF16], name="mm_attn_proj",
        epilogue=lambda acc, ga, gb, yav: (acc, _sig(ga) * yav + _sig(gb) * acc),
        extras=(z, z, ya), extra_col_blocks=(ga_at, gb_at, 0))
    def residual_then_norm(acc, res, gain):
        new = res + acc
        return new, _rmsnorm_rows(new, gain)

    x1, hm = _matmul(merged, w["w_out"], mode="nn", tm=512, tn=1024, out_dtypes=[F32, BF16], name="mm_out",
                     epilogue=residual_then_norm, extras=(x,), row_vecs=(w["g_mlp"],))
    act = _matmul(hm, w["w_up"], mode="nn", tm=1024, tn=1024, out_dtypes=[BF16], name="mm_up",
                  epilogue=lambda acc: (jnp.square(jnp.maximum(acc, 0.0)),))
    x2, hp = _matmul(act, w["w_down"], mode="nn", tm=512, tn=1024, out_dtypes=[F32, BF16], name="mm_down",
                     epilogue=residual_then_norm, extras=(x1,), row_vecs=(w["g_ple"],))
    p_bf = p.astype(BF16)
    e = _matmul(p_bf, w["w_ple_proj"], mode="nn", tm=1024, tn=1024, out_dtypes=[F32], name="mm_ple_proj")

    def loss_head(gt, x2v, ev, tgt):
        sg = _sig(gt)
        diff = x2v + ev * sg - tgt
        dx = diff * (1.0 / D_MODEL)
        return dx, dx * ev * sg * (1.0 - sg), dx * sg, jnp.sum(diff * diff, axis=0, keepdims=True)

    dx3, dgt, de, loss_row = _matmul(hp, w["w_ple_gate"], mode="nn", tm=512, tn=1024, out_dtypes=[F32, BF16, BF16],
                                     name="mm_ple_gate", epilogue=loss_head, extras=(x2, e, target), n_row_sums=1)

    g = {}
    g["w_ple_proj"] = _matmul_tn(p_bf, de, tk=PLE_DIM, tn=1024, tt=1024, name="mm_d_ple_proj",
                                 slot_cols=D_MODEL // N_DEV)
    g["w_ple_gate"] = _matmul_tn(hp, dgt, tk=1024, tn=1024, tt=1024, name="mm_d_ple_gate")
    def through_norm(dy, xv, dres, gain):
        dx, dgain = _rmsnorm_bwd_rows(dy, xv, dres, gain)
        return dx, dx, dgain

    dx2, dx2_bf, g["g_ple"] = _matmul(
        dgt, w["w_ple_gate"], mode="nt", tm=512, tn=1024, out_dtypes=[F32, BF16], name="mm_dhp",
        epilogue=through_norm, extras=(x2, dx3), row_vecs=(w["g_ple"],), n_row_sums=1)
    g["w_down"] = _matmul_tn(act, dx2_bf, tk=1024, tn=1024, tt=1024, name="mm_d_down")

    def relu_grad(dact, a):
        a = a.astype(F32)
        return (dact * (2.0 * jnp.where(a > 0.0, a * lax.rsqrt(a), 0.0)),)

    du = _matmul(dx2_bf, w["w_down"], mode="nt", tm=1024, tn=1024, out_dtypes=[BF16], name="mm_dact",
                 epilogue=relu_grad, extras=(act,))
    g["w_up"] = _matmul_tn(hm, du, tk=1024, tn=1024, tt=1024, name="mm_d_up", slot_cols=D_FF // N_DEV)
    ready(1, g)
    dx1, dx1_bf, g["g_mlp"] = run(
        "mm_dhm", _matmul, du, w["w_up"], mode="nt", tm=512, tn=1024, out_dtypes=[F32, BF16], name="mm_dhm",
        epilogue=through_norm, extras=(x1, dx2), row_vecs=(w["g_mlp"],), n_row_sums=1)
    g["w_out"] = _matmul_tn(merged, dx1_bf, tk=1024, tn=1024, tt=1024, name="mm_d_out")
    def merge_bwd(dm, ga, gb, yav, ybv):
        sa, sb = _sig(ga), _sig(gb)
        return dm * sa, dm * sb, dm * yav * sa * (1.0 - sa), dm * ybv * sb * (1.0 - sb)

    dya, dyb, dga, dgb = _matmul(dx1_bf, w["w_out"], mode="nt", tm=1024, tn=gate_tile, out_dtypes=[BF16] * 4,
                                 name="mm_dmerged", epilogue=merge_bwd, extras=(z, z, ya, yb),
                                 extra_col_blocks=(ga_at, gb_at, 0, 0))
    g["w_rnn_proj"] = _matmul_tn(ya_in, dya, tk=1024, tn=1024, tt=1024, name="mm_d_rnn_proj")
    g["w_attn_proj"] = _matmul_tn(o, dyb, tk=1024, tn=1024, tt=1024, name="mm_d_attn_proj")
    ready(2, g)
    dya_in = run("mm_dya_in", _matmul, dya, w["w_rnn_proj"], mode="nt", tm=1024, tn=1024, out_dtypes=[F32],
                 name="mm_dya_in")
    do = _matmul(dyb, w["w_attn_proj"], mode="nt", tm=1024, tn=1024, out_dtypes=[BF16], name="mm_do")
    dx_rnn, dg_rnn, dwrg_dense, dwig_dense, rnn_vec = run(
        "rnn_bwd", _rnn_bwd, dya_in, z, xc, hr, w["conv_w"], wrg_bd, w["b_rg"], wig_bd, w["b_ig"],
        w["lru_lambda"], **dims)
    dz_attn, attn_vec = run("attn_bwd", _attn_bwd, z, o, lse, do, cos_t, sin_t, q_gain_t, k_gain_t, sinks_t,
                            **dims)
    dz_parts = (dx_rnn, dg_rnn, dz_attn, dga, dgb)
    g["w_rg"] = _block_diag_extract(dwrg_dense)
    g["w_ig"] = _block_diag_extract(dwig_dense)
    g["b_rg"], g["b_ig"], g["lru_lambda"], g["conv_b"] = (rnn_vec[i:i + 1] for i in range(4))
    g["conv_w"] = rnn_vec[4:8]
    attn_vec = attn_vec[0] if n_seq == 1 else functools.reduce(jnp.add, [attn_vec[b] for b in range(n_seq)])
    g["q_gain"] = attn_vec[0].reshape(N_Q_HEADS, HEAD_DIM).sum(axis=0)[None, :]
    g["k_gain"] = attn_vec[1, :KV_W].reshape(N_KV_HEADS, HEAD_DIM).sum(axis=0)[None, :]
    g["sinks"] = attn_vec[2:3, :N_Q_HEADS]
    ready(SMALL_BATCH, g, {LOSS_ROW: loss_row})
    g["w_in"] = jnp.concatenate(
        list(run("mm_d_in_rnn", _matmul_tn_multi, h, dz_parts[:2], tt=1024, name="mm_d_in_rnn"))
        + list(run("mm_d_in_rest", _matmul_tn_multi, h, dz_parts[2:], tt=512, name="mm_d_in_rest")), axis=1)
    ready(3, g)
    w_in_attn, w_in_gate = w["w_in"][:, COL_RNN_END:COL_ATTN_END], w["w_in"][:, COL_ATTN_END:]
    windows = ((w["w_in"], (0, D_MODEL)), (w["w_in"], (D_MODEL, D_MODEL)), (w_in_attn, (0, ATTN_W)),
               (w_in_gate, (0, D_MODEL)), (w_in_gate, (D_MODEL, D_MODEL)))
    grad_x, g["g_mix"] = run(
        "mm_dh", _matmul, dz_parts, [wd[0] for wd in windows], mode="nt", tm=256, tn=1024, out_dtypes=[F32],
        name="mm_dh", b_cols=[wd[1] for wd in windows], epilogue=_rmsnorm_bwd_rows, extras=(x, dx1),
        row_vecs=(w["g_mix"],), n_row_sums=1)
    return jnp.sum(loss_row), grad_x, g


MESH_ID = pl.DeviceIdType.MESH


def _coords(index):
    return (index >> 2) & 1, (index >> 1) & 1, index & 1


def _exchange(srcs, kinds, *, name):
    n = len(srcs)
    n_peer = N_DEV - 1

    def body(*refs):
        src, dst = refs[:n], refs[n:2 * n]
        send_sems, recv_sems, local_sems = refs[2 * n:]
        me = 4 * lax.axis_index("x") + 2 * lax.axis_index("y") + lax.axis_index("c")

        def remote(i, d):
            peer = (me + d) & (N_DEV - 1)
            piece = src[i] if kinds[i] == "gather" else src[i].at[peer]
            return pltpu.make_async_remote_copy(
                src_ref=piece, dst_ref=dst[i].at[me], send_sem=send_sems.at[i * n_peer + d - 1],
                recv_sem=recv_sems.at[i * n_peer + d - 1], device_id=_coords(peer), device_id_type=MESH_ID)

        def arrival(i, d):
            sender = (me - d) & (N_DEV - 1)
            piece = src[i] if kinds[i] == "gather" else src[i].at[sender]
            return pltpu.make_async_remote_copy(
                src_ref=piece, dst_ref=dst[i].at[sender], send_sem=send_sems.at[i * n_peer + d - 1],
                recv_sem=recv_sems.at[i * n_peer + d - 1], device_id=_coords(sender), device_id_type=MESH_ID)

        own = []
        for i in range(n):
            piece = src[i] if kinds[i] == "gather" else src[i].at[me]
            own.append(pltpu.make_async_copy(piece, dst[i].at[me], local_sems.at[i]))
            own[-1].start()
        sent = [remote(i, d) for d in range(1, N_DEV) for i in range(n)]
        for cp in sent:
            cp.start()
        for d in range(1, N_DEV):
            for i in range(n):
                arrival(i, d).wait_recv()
        for cp in sent:
            cp.wait_send()
        for cp in own:
            cp.wait()

    def out_of(s, kind):
        shape = s.shape if kind == "scatter" else (N_DEV,) + s.shape
        return jax.ShapeDtypeStruct(shape, s.dtype)

    any_spec = pl.BlockSpec(memory_space=pl.ANY)
    return pl.pallas_call(
        body,
        in_specs=[any_spec] * n,
        out_specs=[any_spec] * n,
        out_shape=[out_of(s, k) for s, k in zip(srcs, kinds)],
        scratch_shapes=[pltpu.SemaphoreType.DMA((n * n_peer,)), pltpu.SemaphoreType.DMA((n * n_peer,)),
                        pltpu.SemaphoreType.DMA((n,))],
        compiler_params=pltpu.CompilerParams(has_side_effects=True),
        name=name,
    )(*srcs)


def _remote(src, dst, send_sem, recv_sem, to):
    return pltpu.make_async_remote_copy(src_ref=src, dst_ref=dst, send_sem=send_sem, recv_sem=recv_sem,
                                        device_id=to, device_id_type=MESH_ID)


def _gather_two_level(shards, *, name):
    n = len(shards)
    per = N_DEV - 1

    def body(*refs):
        src, dst = refs[:n], refs[n:2 * n]
        send_sems, recv_sems, local_sems = refs[2 * n:]
        x, y, c = lax.axis_index("x"), lax.axis_index("y"), lax.axis_index("c")
        me, sibling = (x, y, c), (x, y, 1 - c)
        chips = [(1 - x, y), (x, 1 - y), (1 - x, 1 - y)]

        def slot(pos):
            return 4 * pos[0] + 2 * pos[1] + pos[2]

        def copy(i, k, block, to, from_shard=False):
            source = src[i] if from_shard else dst[i].at[slot(block)]
            return _remote(source, dst[i].at[slot(block)], send_sems.at[i * per + k], recv_sems.at[i * per + k], to)

        mine = [pltpu.make_async_copy(src[i], dst[i].at[slot(me)], local_sems.at[i]) for i in range(n)]
        for cp in mine:
            cp.start()
        first = []
        for i in range(n):
            first.append(copy(i, 0, me, sibling, from_shard=True))
            first += [copy(i, 1 + j, me, (*chip, c), from_shard=True) for j, chip in enumerate(chips)]
        for cp in first:
            cp.start()
        passed = []
        for i in range(n):
            for j, chip in enumerate(chips):
                copy(i, 1 + j, (*chip, c), me).wait_recv()
                passed.append(copy(i, 4 + j, (*chip, c), sibling))
                passed[-1].start()
        for i in range(n):
            copy(i, 0, sibling, me).wait_recv()
            for j, chip in enumerate(chips):
                copy(i, 4 + j, (*chip, 1 - c), me).wait_recv()
        for cp in first + passed:
            cp.wait_send()
        for cp in mine:
            cp.wait()

    any_spec = pl.BlockSpec(memory_space=pl.ANY)
    return pl.pallas_call(
        body,
        in_specs=[any_spec] * n,
        out_specs=[any_spec] * n,
        out_shape=[jax.ShapeDtypeStruct((N_DEV,) + s.shape, s.dtype) for s in shards],
        scratch_shapes=[pltpu.SemaphoreType.DMA((n * per,)), pltpu.SemaphoreType.DMA((n * per,)),
                        pltpu.SemaphoreType.DMA((n,))],
        name=name,
    )(*shards)


CHIPS = N_DEV // 2


def _other_chips(x, y):
    return [(x, 1 - y), (1 - x, y), (1 - x, 1 - y)]


def _hosted_gather_first(shards):
    n = len(shards)
    per = CHIPS

    def plan(src, dst, send_sems, recv_sems, local_sems, first_sem):
        x, y, c = lax.axis_index("x"), lax.axis_index("y"), lax.axis_index("c")
        peers = [(x, y, 1 - c)] + [(*chip, c) for chip in _other_chips(x, y)]
        copies = []
        for i in range(n):
            own = pltpu.make_async_copy(src[i], dst[i].at[4 * x + 2 * y + c], local_sems.at[first_sem + i])
            copies.append(_Xfer(own.start, own.wait))
        for j, peer in enumerate(peers):
            for i in range(n):
                k = first_sem + i * per + j
                out = _remote(src[i], dst[i].at[4 * x + 2 * y + c], send_sems.at[k], recv_sems.at[k], peer)
                arrival = _remote(src[i], dst[i].at[4 * peer[0] + 2 * peer[1] + peer[2]], send_sems.at[k],
                                  recv_sems.at[k], peer)

                def wait(out=out, arrival=arrival):
                    arrival.wait_recv()
                    out.wait_send()

                copies.append(_Xfer(out.start, wait))
        return copies

    out_shape = tuple(jax.ShapeDtypeStruct((N_DEV,) + s.shape, s.dtype) for s in shards)
    return _Hosted(tuple(shards), out_shape, n * per, plan)


def _hosted_gather_second(landed):
    n = len(landed)
    per = CHIPS - 1

    def plan(src, dst, send_sems, recv_sems, local_sems, first_sem):
        x, y, c = lax.axis_index("x"), lax.axis_index("y"), lax.axis_index("c")
        copies = []
        for j, chip in enumerate(_other_chips(x, y)):
            mine, theirs = 4 * chip[0] + 2 * chip[1] + c, 4 * chip[0] + 2 * chip[1] + 1 - c
            for i in range(n):
                k = first_sem + i * per + j
                out = _remote(src[i].at[mine], dst[i].at[mine], send_sems.at[k], recv_sems.at[k], (x, y, 1 - c))
                arrival = _remote(src[i].at[theirs], dst[i].at[theirs], send_sems.at[k], recv_sems.at[k],
                                  (x, y, 1 - c))

                def wait(out=out, arrival=arrival):
                    arrival.wait_recv()
                    out.wait_send()

                copies.append(_Xfer(out.start, wait))
        return copies

    out_shape = tuple(jax.ShapeDtypeStruct(a.shape, a.dtype) for a in landed)
    return _Hosted(tuple(landed), out_shape, n * per, plan, tuple((i, i) for i in range(n)))


def _hosted_sibling_swap(arrays, sliced):
    n_sems = sum(CHIPS if s else 1 for s in sliced)

    def plan(src, dst, send_sems, recv_sems, local_sems, first_sem):
        x, y, c = lax.axis_index("x"), lax.axis_index("y"), lax.axis_index("c")
        sibling = (x, y, 1 - c)
        copies, k = [], first_sem
        for i, is_sliced in enumerate(sliced):
            pieces = [(src[i].at[2 * s + 1 - c], dst[i].at[s]) for s in range(CHIPS)] if is_sliced else [(src[i], dst[i])]
            for source, target in pieces:
                cp = _remote(source, target, send_sems.at[k], recv_sems.at[k], sibling)
                copies.append(_Xfer(cp.start, cp.wait))
                k += 1
        return copies

    out_shape = tuple(jax.ShapeDtypeStruct((CHIPS,) + a.shape[1:] if s else a.shape, a.dtype)
                      for a, s in zip(arrays, sliced))
    return _Hosted(tuple(arrays), out_shape, n_sems, plan)


def _hosted_chip_exchange(arrays, sliced):
    n = len(arrays)
    per = CHIPS - 1

    def plan(src, dst, send_sems, recv_sems, local_sems, first_sem):
        x, y, c = lax.axis_index("x"), lax.axis_index("y"), lax.axis_index("c")
        chip = 2 * x + y
        copies = []
        for i in range(n):
            own = pltpu.make_async_copy(src[i].at[chip] if sliced[i] else src[i], dst[i].at[chip],
                                        local_sems.at[first_sem + i])
            copies.append(_Xfer(own.start, own.wait))
        for d in range(1, CHIPS):
            other = chip ^ d
            to = ((other >> 1) & 1, other & 1, c)
            for i in range(n):
                k = first_sem + i * per + d - 1
                source = src[i].at[other] if sliced[i] else src[i]
                out = _remote(source, dst[i].at[chip], send_sems.at[k], recv_sems.at[k], to)
                arrival = _remote(source, dst[i].at[other], send_sems.at[k], recv_sems.at[k], to)

                def wait(out=out, arrival=arrival):
                    arrival.wait_recv()
                    out.wait_send()

                copies.append(_Xfer(out.start, wait))
        return copies

    out_shape = tuple(jax.ShapeDtypeStruct(a.shape if s else (CHIPS,) + a.shape, a.dtype)
                      for a, s in zip(arrays, sliced))
    return _Hosted(tuple(arrays), out_shape, n * per, plan)


def _add_sibling(parts, received, core, *, name):
    _, r, cols = parts.shape
    tr = min(256, r)

    def body(core_ref, a_ref, b_ref, o_ref):
        o_ref[...] = (a_ref[...] + b_ref[...]).astype(BF16)

    grid_spec = pltpu.PrefetchScalarGridSpec(
        num_scalar_prefetch=1,
        grid=(CHIPS, r // tr),
        in_specs=[pl.BlockSpec((None, tr, cols), lambda k, i, core_ref: (2 * k + core_ref[0], i, 0)),
                  pl.BlockSpec((None, tr, cols), lambda k, i, core_ref: (k, i, 0))],
        out_specs=pl.BlockSpec((None, tr, cols), lambda k, i, core_ref: (k, i, 0)),
    )
    return pl.pallas_call(body, grid_spec=grid_spec, out_shape=jax.ShapeDtypeStruct((CHIPS, r, cols), BF16),
                          compiler_params=_params("parallel", "parallel"), name=name)(core, parts, received)


def _add_whole(a, b, *, name):
    def body(a_ref, b_ref, o_ref):
        o_ref[...] = a_ref[...] + b_ref[...]

    return pl.pallas_call(body, out_shape=jax.ShapeDtypeStruct(a.shape, F32), name=name)(a, b)


def _adamw(parts, w, m, v, *, name):
    r, c = w.shape
    n_parts = parts.shape[0]
    tr = min(256, r)
    c1 = 1.0 - ADAM_B1 ** ADAM_STEP
    c2 = 1.0 - ADAM_B2 ** ADAM_STEP

    def body(p_ref, w_ref, m_ref, v_ref, g_ref, d_ref, nm_ref, nv_ref):
        g = p_ref[0].astype(F32)
        for s in range(1, n_parts):
            g = g + p_ref[s].astype(F32)
        nm = ADAM_B1 * m_ref[...] + (1.0 - ADAM_B1) * g
        nv = ADAM_B2 * v_ref[...] + (1.0 - ADAM_B2) * (g * g)
        g_ref[...] = g
        nm_ref[...] = nm
        nv_ref[...] = nv
        d_ref[...] = -ADAM_LR * ((nm / c1) / (jnp.sqrt(nv / c2) + ADAM_EPS) + ADAM_WD * w_ref[...])

    tile = pl.BlockSpec((tr, c), lambda i: (i, 0))
    return pl.pallas_call(
        body,
        grid=(r // tr,),
        in_specs=[pl.BlockSpec((n_parts, tr, c), lambda i: (0, i, 0)), tile, tile, tile],
        out_specs=[tile] * 4,
        out_shape=[jax.ShapeDtypeStruct((r, c), F32)] * 4,
        compiler_params=_params("parallel"),
        name=name,
    )(parts, w, m, v)


BIG = ("w_in", "w_rnn_proj", "w_attn_proj", "w_out", "w_up", "w_down", "w_ple_gate", "w_ple_proj")
LOSS_ROW = "loss"
SMALL = (("conv_b", 1), ("b_rg", 1), ("b_ig", 1), ("lru_lambda", 1), ("g_mlp", 1), ("g_ple", 1),
         ("q_gain", 1), ("k_gain", 1), ("sinks", 1), (LOSS_ROW, 1), ("w_rg", 64), ("w_ig", 64))
SMALL_ROWS = 144
ROW_SHARDED = ("w_rnn_proj", "w_attn_proj", "w_out", "w_down", "w_ple_gate")
COL_SHARDED = ("w_in", "w_up", "w_ple_proj")
BATCHES = {1: ("w_ple_proj", "w_ple_gate", "w_down", "w_up"), 2: ("w_out", "w_rnn_proj", "w_attn_proj"),
           3: ("w_in", "conv_w")}
SMALL_BATCH = 4


def _pack_small(vals):
    rows = []
    for nm, nrow in SMALL:
        flat = vals[nm].reshape(-1).astype(F32)
        rows.append(jnp.pad(flat, (0, nrow * D_MODEL - flat.shape[0])).reshape(nrow, D_MODEL))
    used = sum(nrow for _, nrow in SMALL)
    rows.append(jnp.zeros((SMALL_ROWS - used, D_MODEL), F32))
    return jnp.concatenate(rows, axis=0)


def _unpack_small(packed, shapes):
    out, at = {}, 0
    for nm, nrow in SMALL:
        size = 1
        for s in shapes[nm]:
            size *= s
        out[nm] = packed[at:at + nrow].reshape(-1)[:size].reshape(shapes[nm])
        at += nrow
    return out


def _full_weight(name, landed):
    if name in COL_SHARDED:
        return landed.transpose(1, 0, 2).reshape(landed.shape[1], N_DEV * landed.shape[2])
    return landed.reshape(N_DEV * landed.shape[1], landed.shape[2])


def _owner_slots(name, grad):
    if name == "w_in":
        return grad.reshape(D_MODEL, N_DEV, IN_TOTAL // N_DEV).transpose(1, 0, 2)
    if name == "conv_w":
        return grad.reshape(CONV_W, N_DEV, D_MODEL // N_DEV).transpose(1, 0, 2)
    if name in COL_SHARDED:
        return grad
    return grad.reshape(N_DEV, grad.shape[0] // N_DEV, grad.shape[1])


class _StepExchanges:
    FIRST, SECOND = "first", "second"
    PROJ, OUT, PLE_GATE, UP, DOWN = (("w_rnn_proj", "w_attn_proj"), ("w_out",), ("w_ple_gate",), ("w_up",),
                                     ("w_down", "w_ple_proj"))
    GATHERS = {"mm_in": ((FIRST, PROJ), (FIRST, OUT), (FIRST, PLE_GATE)),
               "rnn_fwd": ((SECOND, PROJ), (SECOND, OUT), (SECOND, PLE_GATE), (FIRST, UP)),
               "attn_fwd": ((SECOND, UP), (FIRST, DOWN)), "mm_rnn_proj": ((SECOND, DOWN),)}
    SWAPS = {"mm_dhm": 1, "mm_dya_in": 2, "mm_d_in_rnn": SMALL_BATCH}
    CHIP_EXCHANGES = {"rnn_bwd": (1,), "attn_bwd": (2,), "mm_d_in_rest": (SMALL_BATCH,), "mm_dh": (3,)}

    def __init__(self, shards, core):
        self.shards = shards
        self.core = core
        self.parts, self.swapped, self.summed, self.half_gathered = {}, {}, {}, {}

    def ready(self, batch, grads, extra=None):
        if batch == SMALL_BATCH:
            self.parts[batch] = ([_pack_small({**grads, **extra})], [False])
            return
        arrays = [_owner_slots(nm, grads[nm]) for nm in BATCHES[batch]]
        self.parts[batch] = (arrays, [True] * len(arrays))
        if batch not in self.SWAPS.values():
            _, self.swapped[batch] = _call(
                lambda: None, grid=(1,), in_specs=[], out_specs=[], out_shape=[], args=(), name="swap_last",
                semantics=("arbitrary",), hosted=_hosted_sibling_swap(*self.parts[batch]))

    def host(self, tag):
        if tag in self.GATHERS:
            return _merge_hosted([
                _hosted_gather_first([self.shards[nm] for nm in group]) if half == self.FIRST
                else _hosted_gather_second([self.half_gathered[nm] for nm in group])
                for half, group in self.GATHERS[tag]])
        if tag in self.SWAPS:
            return _hosted_sibling_swap(*self.parts[self.SWAPS[tag]])
        if tag in self.CHIP_EXCHANGES:
            hosted = []
            for batch in self.CHIP_EXCHANGES[tag]:
                arrays, sliced = self.parts[batch]
                labels = BATCHES.get(batch, ("small",))
                sums = [_add_sibling(a, r, self.core, name="add_" + lb) if s else _add_whole(a, r, name="add_" + lb)
                        for a, r, s, lb in zip(arrays, self.swapped[batch], sliced, labels)]
                hosted.append(_hosted_chip_exchange(sums, sliced))
            return _merge_hosted(hosted)
        return None

    def landed(self, tag, landed, weights):
        if tag in self.GATHERS:
            names = [(half, nm) for half, group in self.GATHERS[tag] for nm in group]
            for (half, nm), buf in zip(names, landed):
                if half == self.FIRST:
                    self.half_gathered[nm] = buf
                else:
                    weights[nm] = _full_weight(nm, buf)
        elif tag in self.SWAPS:
            self.swapped[self.SWAPS[tag]] = landed
        else:
            at = 0
            for batch in self.CHIP_EXCHANGES[tag]:
                count = len(self.parts[batch][0])
                self.summed[batch] = landed[at:at + count]
                at += count


def kernel(x, p, g_mix, w_in, conv_w, conv_b, w_rg, b_rg, w_ig, b_ig, lru_lambda, w_rnn_proj, q_gain, k_gain, sinks, w_attn_proj, w_out, g_mlp, w_up, w_down, g_ple, w_ple_gate, w_ple_proj, loss_target, m_g_mix, m_w_in, m_conv_w, m_conv_b, m_w_rg, m_b_rg, m_w_ig, m_b_ig, m_lru_lambda, m_w_rnn_proj, m_q_gain, m_k_gain, m_sinks, m_w_attn_proj, m_w_out, m_g_mlp, m_w_up, m_w_down, m_g_ple, m_w_ple_gate, m_w_ple_proj, v_g_mix, v_w_in, v_conv_w, v_conv_b, v_w_rg, v_b_rg, v_w_ig, v_b_ig, v_lru_lambda, v_w_rnn_proj, v_q_gain, v_k_gain, v_sinks, v_w_attn_proj, v_w_out, v_g_mlp, v_w_up, v_w_down, v_g_ple, v_w_ple_gate, v_w_ple_proj):
    names = ("g_mix", "w_in", "conv_w", "conv_b", "w_rg", "b_rg", "w_ig", "b_ig", "lru_lambda", "w_rnn_proj",
             "q_gain", "k_gain", "sinks", "w_attn_proj", "w_out", "g_mlp", "w_up", "w_down", "g_ple",
             "w_ple_gate", "w_ple_proj")
    wts = dict(zip(names, (g_mix, w_in, conv_w, conv_b, w_rg, b_rg, w_ig, b_ig, lru_lambda, w_rnn_proj, q_gain,
                           k_gain, sinks, w_attn_proj, w_out, g_mlp, w_up, w_down, g_ple, w_ple_gate, w_ple_proj)))
    mom1 = dict(zip(names, (m_g_mix, m_w_in, m_conv_w, m_conv_b, m_w_rg, m_b_rg, m_w_ig, m_b_ig, m_lru_lambda,
                            m_w_rnn_proj, m_q_gain, m_k_gain, m_sinks, m_w_attn_proj, m_w_out, m_g_mlp, m_w_up,
                            m_w_down, m_g_ple, m_w_ple_gate, m_w_ple_proj)))
    mom2 = dict(zip(names, (v_g_mix, v_w_in, v_conv_w, v_conv_b, v_w_rg, v_b_rg, v_w_ig, v_b_ig, v_lru_lambda,
                            v_w_rnn_proj, v_q_gain, v_k_gain, v_sinks, v_w_attn_proj, v_w_out, v_g_mlp, v_w_up,
                            v_w_down, v_g_ple, v_w_ple_gate, v_w_ple_proj)))
    n_seq, seq, _ = x.shape
    core = lax.axis_index("c").astype(jnp.int32).reshape(1)

    shards = {nm: wts[nm][0].astype(BF16) for nm in BIG}
    w_in_all, conv_all = _gather_two_level([shards["w_in"], conv_w[0]], name="gather_w_in")
    w = {nm: wts[nm] for nm in names if nm not in BIG}
    w["w_rg"], w["w_ig"] = w_rg[0], w_ig[0]
    w["conv_w"] = conv_all.transpose(1, 0, 2).reshape(CONV_W, D_MODEL)
    w["w_in"] = _full_weight("w_in", w_in_all)
    comm = _StepExchanges(shards, core)
    loss_sum, grad_x, g = _local_step(
        x.reshape(n_seq * seq, D_MODEL), p.reshape(n_seq * seq, PLE_DIM), loss_target.reshape(n_seq * seq, D_MODEL),
        w, n_seq=n_seq, seq=seq, comm=comm)
    del loss_sum

    res = {}
    for batch, batch_names in BATCHES.items():
        for nm, summed in zip(batch_names, comm.summed[batch]):
            res[nm] = _adamw(summed, wts[nm][0], mom1[nm][0], mom2[nm][0], name="adamw_" + nm)
    g_mix_parts, = _exchange([g["g_mix"]], ["gather"], name="gather_g_mix")
    res["g_mix"] = [r[0] for r in _adamw(g_mix_parts, g_mix, m_g_mix, v_g_mix, name="adamw_g_mix")]
    small_names = [nm for nm, _ in SMALL if nm != LOSS_ROW]
    full_small = {}
    for src, key in ((wts, "w"), (mom1, "m"), (mom2, "v")):
        vals = {nm: src[nm][0] for nm in small_names}
        vals[LOSS_ROW] = jnp.zeros((1,), F32)
        full_small[key] = _pack_small(vals)
    small_res = _adamw(comm.summed[SMALL_BATCH][0],full_small["w"], full_small["m"], full_small["v"], name="adamw_small")
    shapes = {nm: wts[nm].shape[1:] for nm in small_names}
    shapes[LOSS_ROW] = (D_MODEL,)
    small_out = [_unpack_small(r, shapes) for r in small_res]
    for nm in small_names:
        res[nm] = [so[nm] for so in small_out]
    loss = jnp.sum(small_out[0][LOSS_ROW]) * (0.5 / D_MODEL)

    outs = [loss, grad_x.reshape(n_seq, seq, D_MODEL)]
    for k in range(4):
        outs.extend(res[nm][k][None] for nm in names)
    return tuple(outs)
```

```python
import functools
from typing import Callable, NamedTuple

import jax
import jax.numpy as jnp
from jax import lax
from jax.experimental import pallas as pl
from jax.experimental.pallas import tpu as pltpu

F32 = jnp.float32
BF16 = jnp.bfloat16

N_DEV = 8
D_MODEL = 1024
RNN_BLOCK_W = 64
CONV_W = 4
LRU_C = 8.0
HEAD_DIM = 64
N_Q_HEADS = 16
N_KV_HEADS = 4
KV_W = N_KV_HEADS * HEAD_DIM
WINDOW = 128
ROPE_THETA = 10000.0
D_FF = 4096
PLE_DIM = 256
NORM_EPS = 1e-6
IN_TOTAL = 5632
COL_RNN_END, COL_ATTN_END = 2048, 3584
ATTN_W = COL_ATTN_END - COL_RNN_END
ATTN_K_AT, ATTN_V_AT = 1024, 1280

ADAM_LR = 0.001
ADAM_B1 = 0.9
ADAM_B2 = 0.999
ADAM_EPS = 1e-08
ADAM_WD = 0.01
ADAM_STEP = 10

LANES = 128
SUBLANES = 8
RNN_TILE = 256
VMEM_LIMIT = 48 * 1024 * 1024
NEG_BIG = -1e30


def _params(*sem):
    return pltpu.CompilerParams(dimension_semantics=sem if sem else None, vmem_limit_bytes=VMEM_LIMIT)


def _sig(x):
    return 0.5 * jnp.tanh(0.5 * x) + 0.5


def _dot_nt(a, b):
    return lax.dot_general(a, b, (((1,), (1,)), ((), ())), preferred_element_type=F32)


def _dot_tn(a, b):
    return lax.dot_general(a, b, (((0,), (0,)), ((), ())), preferred_element_type=F32)


class _Xfer:
    def __init__(self, start, wait):
        self.start, self.wait = start, wait


class _Hosted(NamedTuple):
    srcs: tuple
    out_shape: tuple
    n_sems: int
    plan: Callable
    aliases: tuple = ()


def _merge_hosted(parts):
    parts = [p for p in parts if p is not None]
    if len(parts) <= 1:
        return parts[0] if parts else None
    src_at, dst_at, sem_at, aliases = [0], [0], [0], []
    for p in parts:
        aliases += [(i + src_at[-1], j + dst_at[-1]) for i, j in p.aliases]
        src_at.append(src_at[-1] + len(p.srcs))
        dst_at.append(dst_at[-1] + len(p.out_shape))
        sem_at.append(sem_at[-1] + p.n_sems)

    def plan(src, dst, send_sems, recv_sems, local_sems, first_sem):
        copies = []
        for k, p in enumerate(parts):
            copies += p.plan(src[src_at[k]:src_at[k + 1]], dst[dst_at[k]:dst_at[k + 1]], send_sems, recv_sems,
                             local_sems, first_sem + sem_at[k])
        return copies

    return _Hosted(tuple(a for p in parts for a in p.srcs), tuple(s for p in parts for s in p.out_shape),
                   sem_at[-1], plan, tuple(aliases))


def _call(body, *, grid, in_specs, out_specs, out_shape, args, name, semantics, scratch_shapes=(), hosted=None):
    if hosted is None:
        outs = pl.pallas_call(body, grid=grid, in_specs=list(in_specs), out_specs=list(out_specs),
                              out_shape=list(out_shape), scratch_shapes=list(scratch_shapes),
                              compiler_params=_params(*semantics), name=name)(*args)
        return list(outs), []
    counts = (len(in_specs), len(hosted.srcs), len(out_specs), len(hosted.out_shape), len(scratch_shapes), 3)

    def wrapped(*refs):
        at, groups = 0, []
        for count in counts:
            groups.append(refs[at:at + count])
            at += count
        ins, srcs, outs, dsts, scratch, sems = groups
        copies = hosted.plan(srcs, dsts, *sems, 0)
        ids = [pl.program_id(axis) for axis in range(len(grid))]
        first = functools.reduce(jnp.logical_and, [i == 0 for i in ids])
        last = functools.reduce(jnp.logical_and, [i == g - 1 for i, g in zip(ids, grid)])

        @pl.when(first)
        def _():
            for cp in copies:
                cp.start()

        body(*ins, *outs, *scratch)

        @pl.when(last)
        def _():
            for cp in copies:
                cp.wait()

    any_spec = pl.BlockSpec(memory_space=pl.ANY)
    sems = [pltpu.SemaphoreType.DMA((hosted.n_sems,))] * 3
    outs = pl.pallas_call(
        wrapped, grid=grid, in_specs=list(in_specs) + [any_spec] * counts[1],
        out_specs=list(out_specs) + [any_spec] * counts[3], out_shape=list(out_shape) + list(hosted.out_shape),
        scratch_shapes=list(scratch_shapes) + sems, compiler_params=_params(*["arbitrary"] * len(grid)),
        input_output_aliases={counts[0] + i: counts[2] + j for i, j in hosted.aliases},
        name=name)(*args, *hosted.srcs)
    return list(outs[:counts[2]]), list(outs[counts[2]:])


def _dividing_tile(n, want):
    tile = min(want, n)
    while n % tile:
        tile -= LANES
    return tile


def _matmul(a, b, *, mode, tm, tn, out_dtypes, name, epilogue=None, extras=(), hosted=None, b_cols=None,
            row_vecs=(), n_row_sums=0, extra_col_blocks=None):
    a_parts = tuple(a) if isinstance(a, (tuple, list)) else (a,)
    b_parts = tuple(b) if isinstance(b, (tuple, list)) else (b,)
    assert len(a_parts) == len(b_parts) and (mode == "nt" or len(a_parts) == 1)
    n_parts = len(a_parts)
    m = a_parts[0].shape[0]
    if b_cols is None:
        b_cols = [(0, bp.shape[1]) for bp in b_parts]
    n = b_cols[0][1] if mode == "nn" else b_parts[0].shape[0]
    tm, tn = min(tm, m), _dividing_tile(n, tn)
    n_extra = len(extras) + len(row_vecs)
    n_tiles_out = len(out_dtypes)
    assert n_row_sums == 0 or n == tn

    def body(*refs):
        a_refs, b_refs = refs[:n_parts], refs[n_parts:2 * n_parts]
        rest = refs[2 * n_parts:]
        extra_refs, out_refs = rest[:n_extra], rest[n_extra:]
        if mode == "nn":
            acc = jnp.dot(a_refs[0][...], b_refs[0][...], preferred_element_type=F32)
        else:
            acc = _dot_nt(a_refs[0][...], b_refs[0][...])
            for a_ref, b_ref in zip(a_refs[1:], b_refs[1:]):
                acc = acc + _dot_nt(a_ref[...], b_ref[...])
        res = epilogue(acc, *[e[...] for e in extra_refs]) if epilogue is not None else (acc,)
        for o_ref, r in zip(out_refs[:n_tiles_out], res):
            o_ref[...] = r.astype(o_ref.dtype)
        if n_row_sums:
            @pl.when(pl.program_id(0) == 0)
            def _():
                for o_ref in out_refs[n_tiles_out:]:
                    o_ref[...] = jnp.zeros_like(o_ref)

            for o_ref, r in zip(out_refs[n_tiles_out:], res[n_tiles_out:]):
                o_ref[...] += r

    a_specs = [pl.BlockSpec((tm, ap.shape[1]), lambda i, j: (i, 0)) for ap in a_parts]
    if mode == "nn":
        assert b_cols[0][0] % tn == 0
        first = b_cols[0][0] // tn
        b_specs = [pl.BlockSpec((b_parts[0].shape[0], tn), lambda i, j: (0, first + j))]
    else:
        assert all(at % width == 0 for at, width in b_cols)
        b_specs = [pl.BlockSpec((tn, width), functools.partial(lambda i, j, blk: (j, blk), blk=at // width))
                   for at, width in b_cols]
    tile = pl.BlockSpec((tm, tn), lambda i, j: (i, j))
    row = pl.BlockSpec((1, tn), lambda i, j: (0, j))
    extra_specs = [pl.BlockSpec((tm, tn), functools.partial(lambda i, j, first: (i, first + j), first=first))
                   for first in (extra_col_blocks or [0] * len(extras))]
    outs, landed = _call(
        body,
        grid=(m // tm, n // tn),
        in_specs=a_specs + b_specs + extra_specs + [row] * len(row_vecs),
        out_specs=[tile] * n_tiles_out + [row] * n_row_sums,
        out_shape=[jax.ShapeDtypeStruct((m, n), dt) for dt in out_dtypes]
        + [jax.ShapeDtypeStruct((1, n), F32)] * n_row_sums,
        args=(*a_parts, *b_parts, *extras, *row_vecs), name=name,
        semantics=("arbitrary" if n_row_sums else "parallel", "arbitrary"), hosted=hosted)
    if hosted is not None:
        return (*outs, landed)
    return outs[0] if len(outs) == 1 else outs


def _matmul_tn(a, b, *, tk, tn, tt, name, slot_cols=None):
    t, k = a.shape
    n = b.shape[1]
    tk, tn, tt = min(tk, k), _dividing_tile(n, tn), min(tt, t)

    def body(a_ref, b_ref, o_ref):
        @pl.when(pl.program_id(2) == 0)
        def _():
            o_ref[...] = jnp.zeros_like(o_ref)

        if slot_cols is None:
            o_ref[...] += _dot_tn(a_ref[...], b_ref[...])
        else:
            av = a_ref[...]
            for s in range(tn // slot_cols):
                o_ref[s] += _dot_tn(av, b_ref[:, s * slot_cols:(s + 1) * slot_cols])

    if slot_cols is not None:
        out_spec = pl.BlockSpec((tn // slot_cols, tk, slot_cols), lambda i, j, s: (j, i, 0))
        out_shape = jax.ShapeDtypeStruct((n // slot_cols, k, slot_cols), F32)
    else:
        out_spec = pl.BlockSpec((tk, tn), lambda i, j, s: (i, j))
        out_shape = jax.ShapeDtypeStruct((k, n), F32)
    return pl.pallas_call(
        body,
        grid=(k // tk, n // tn, t // tt),
        in_specs=[pl.BlockSpec((tt, tk), lambda i, j, s: (s, i)), pl.BlockSpec((tt, tn), lambda i, j, s: (s, j))],
        out_specs=out_spec,
        out_shape=out_shape,
        compiler_params=_params("parallel", "parallel", "arbitrary"),
        name=name,
    )(a, b)


def _matmul_tn_multi(a, bs, *, tt, name, hosted=None):
    t, k = a.shape
    tt = min(tt, t)
    n_b = len(bs)

    def body(a_ref, *refs):
        b_refs, o_refs = refs[:n_b], refs[n_b:]

        @pl.when(pl.program_id(0) == 0)
        def _():
            for o_ref in o_refs:
                o_ref[...] = jnp.zeros_like(o_ref)

        a_t = a_ref[...].T
        for b_ref, o_ref in zip(b_refs, o_refs):
            o_ref[...] += jnp.dot(a_t, b_ref[...], preferred_element_type=F32)

    outs, landed = _call(
        body,
        grid=(t // tt,),
        in_specs=[pl.BlockSpec((tt, k), lambda s: (s, 0))] + [pl.BlockSpec((tt, b.shape[1]), lambda s: (s, 0)) for b in bs],
        out_specs=[pl.BlockSpec((k, b.shape[1]), lambda s: (0, 0)) for b in bs],
        out_shape=[jax.ShapeDtypeStruct((k, b.shape[1]), F32) for b in bs],
        args=(a, *bs), name=name, semantics=("arbitrary",), hosted=hosted)
    return (*outs, landed) if hosted is not None else outs


def _rmsnorm_rows(x, g):
    return x * lax.rsqrt(jnp.mean(x * x, axis=-1, keepdims=True) + NORM_EPS) * g


def _norm_matmul(x, g, b, *, tm, tn, name, hosted=None):
    m, k = x.shape
    n = b.shape[1]
    tm, tn = min(tm, m), _dividing_tile(n, tn)

    def body(x_ref, g_ref, b_ref, z_ref, h_ref, h_s):
        @pl.when(pl.program_id(1) == 0)
        def _():
            h_s[...] = _rmsnorm_rows(x_ref[...], g_ref[...]).astype(BF16)
            h_ref[...] = h_s[...]

        z_ref[...] = jnp.dot(h_s[...], b_ref[...], preferred_element_type=F32)

    rows = pl.BlockSpec((tm, k), lambda i, j: (i, 0))
    outs, landed = _call(
        body,
        grid=(m // tm, n // tn),
        in_specs=[rows, pl.BlockSpec((1, k), lambda i, j: (0, 0)), pl.BlockSpec((k, tn), lambda i, j: (0, j))],
        out_specs=[pl.BlockSpec((tm, tn), lambda i, j: (i, j)), rows],
        out_shape=[jax.ShapeDtypeStruct((m, n), F32), jax.ShapeDtypeStruct((m, k), BF16)],
        scratch_shapes=[pltpu.VMEM((tm, k), BF16)],
        args=(x, g, b), name=name, semantics=("parallel", "arbitrary"), hosted=hosted)
    return (*outs, landed) if hosted is not None else outs


def _rmsnorm_bwd_rows(dy, x, dres, g):
    r = lax.rsqrt(jnp.mean(x * x, axis=-1, keepdims=True) + NORM_EPS)
    xr = x * r
    gy = dy * g
    dx = dres + r * (gy - xr * jnp.mean(gy * xr, axis=-1, keepdims=True))
    return dx, jnp.sum(dy * xr, axis=0, keepdims=True)


def _softplus_neg(lam):
    z = -lam
    return jnp.maximum(z, 0.0) + jnp.log1p(jnp.exp(-jnp.abs(z)))


def _neg_expm1(y, exp_half_y):
    series = -y * (1.0 + y * 0.5 * (1.0 + y * (1.0 / 3.0) * (1.0 + y * 0.25 * (1.0 + y * 0.2))))
    return jnp.where(y > -0.0625, series, 1.0 - exp_half_y * exp_half_y)


def _gelu_parts(x):
    c = 0.7978845608028654
    u = c * (x + 0.044715 * x * x * x)
    th = jnp.tanh(u)
    gel = 0.5 * x * (1.0 + th)
    dgel = 0.5 * (1.0 + th) + 0.5 * x * (1.0 - th * th) * c * (1.0 + 3.0 * 0.044715 * x * x)
    return gel, dgel


def _shift_down(v, k, rows):
    return jnp.where(rows < k, 0.0, pltpu.roll(v, k, 0))


def _shift_up(v, k, rows, n):
    return jnp.where(rows >= n - k, 0.0, pltpu.roll(v, n - k, 0))


def _scan_within_groups(a, b, *, reverse):
    shape = a.shape
    a = a.reshape(shape[0] // SUBLANES, SUBLANES, shape[1])
    b = b.reshape(a.shape)
    in_group = lax.broadcasted_iota(jnp.int32, a.shape, 1)
    for s in (1, 2, 4):
        if reverse:
            inside, shift = in_group < SUBLANES - s, SUBLANES - s
        else:
            inside, shift = in_group >= s, s
        b = b + a * jnp.where(inside, pltpu.roll(b, shift, 1), 0.0)
        a = a * jnp.where(inside, pltpu.roll(a, shift, 1), 1.0)
    return a.reshape(shape), b.reshape(shape)


def _rnn_gates(xc, wrg, brg, wig, big, lam):
    xcb = xc.astype(BF16)
    r = _sig(jnp.dot(xcb, wrg, preferred_element_type=F32) + brg)
    i = _sig(jnp.dot(xcb, wig, preferred_element_type=F32) + big)
    sp = _softplus_neg(lam)
    log_a = -LRU_C * r * sp
    a = jnp.exp(log_a)
    mult = jnp.sqrt(_neg_expm1(2.0 * log_a, a))
    return xcb, r, i, sp, a, mult


def _conv_fwd(xv, cw, cb, rows):
    return (cb + _shift_down(xv, 3, rows) * cw[0:1, :] + _shift_down(xv, 2, rows) * cw[1:2, :]
            + _shift_down(xv, 1, rows) * cw[2:3, :] + xv * cw[3:4, :])


def _rnn_fwd(z, conv_w, conv_b, wrg_bd, b_rg, wig_bd, b_ig, lam, *, n_seq, seq, hosted=None):
    t = n_seq * seq
    ct = RNN_TILE
    n_ct = D_MODEL // ct

    def body(x_ref, g_ref, cw_ref, cb_ref, wrg_ref, brg_ref, wig_ref, big_ref, lam_ref,
             xc_ref, hr_ref, ya_ref, a_s, b_s):
        rows = lax.broadcasted_iota(jnp.int32, (seq, ct), 0)
        xc = _conv_fwd(x_ref[...], cw_ref[...], cb_ref[...], rows)
        _, r, i, sp, a, mult = _rnn_gates(xc, wrg_ref[...], brg_ref[...], wig_ref[...], big_ref[...], lam_ref[...])
        a_s[...], b_s[...] = _scan_within_groups(a, mult * (i * xc), reverse=False)

        def step(j, carry):
            r0 = pl.multiple_of(j * SUBLANES, SUBLANES)
            h = b_s[pl.ds(r0, SUBLANES), :] + a_s[pl.ds(r0, SUBLANES), :] * carry
            hr_ref[pl.ds(r0, SUBLANES), :] = h
            return h[SUBLANES - 1:SUBLANES, :]

        lax.fori_loop(0, seq // SUBLANES, step, jnp.zeros((1, ct), F32), unroll=4)
        gel, _ = _gelu_parts(g_ref[...])
        xc_ref[...] = xc
        ya_ref[...] = (hr_ref[...] * gel).astype(BF16)

    vec = pl.BlockSpec((1, ct), lambda b, c: (0, c))
    gate_w = pl.BlockSpec((None, ct, ct), lambda b, c: (c, 0, 0))
    tile = pl.BlockSpec((seq, ct), lambda b, c: (b, c))
    outs, landed = _call(
        body,
        grid=(n_seq, n_ct),
        in_specs=[
            pl.BlockSpec((seq, ct), lambda b, c: (b, c)),
            pl.BlockSpec((seq, ct), lambda b, c: (b, n_ct + c)),
            pl.BlockSpec((CONV_W, ct), lambda b, c: (0, c)), vec, gate_w, vec, gate_w, vec, vec,
        ],
        out_specs=[tile, tile, tile],
        out_shape=[jax.ShapeDtypeStruct((t, D_MODEL), F32), jax.ShapeDtypeStruct((t, D_MODEL), F32),
                   jax.ShapeDtypeStruct((t, D_MODEL), BF16)],
        scratch_shapes=[pltpu.VMEM((seq, ct), F32), pltpu.VMEM((seq, ct), F32)],
        args=(z, z, conv_w, conv_b, wrg_bd, b_rg, wig_bd, b_ig, lam), name="rnn_fwd",
        semantics=("parallel", "parallel"), hosted=hosted)
    return (*outs, landed) if hosted is not None else outs


def _rnn_bwd(dya, z, xc, hr, conv_w, wrg_bd, b_rg, wig_bd, b_ig, lam, *, n_seq, seq, hosted=None):
    t = n_seq * seq
    ct = RNN_TILE
    n_ct = D_MODEL // ct

    def body(dya_ref, x_ref, g_ref, xc_ref, hr_ref, cw_ref, wrg_ref, brg_ref, wig_ref, big_ref, lam_ref,
             dx_ref, dg_ref, dwrg_ref, dwig_ref, vec_ref, a_s, d_s, g_s):
        rows = lax.broadcasted_iota(jnp.int32, (seq, ct), 0)
        xv, xc, hr, dyv = x_ref[...], xc_ref[...], hr_ref[...], dya_ref[...]
        lamv = lam_ref[...]
        gel, dgel = _gelu_parts(g_ref[...])
        dg_ref[...] = (dyv * hr * dgel).astype(BF16)
        xcb, r, i, sp, a, mult = _rnn_gates(xc, wrg_ref[...], brg_ref[...], wig_ref[...], big_ref[...], lamv)
        a_s[...], d_s[...] = _scan_within_groups(_shift_up(a, 1, rows, seq), dyv * gel, reverse=True)

        def step(k, carry):
            r0 = pl.multiple_of((seq // SUBLANES - 1 - k) * SUBLANES, SUBLANES)
            gs = d_s[pl.ds(r0, SUBLANES), :] + a_s[pl.ds(r0, SUBLANES), :] * carry
            g_s[pl.ds(r0, SUBLANES), :] = gs
            return gs[0:1, :]

        lax.fori_loop(0, seq // SUBLANES, step, jnp.zeros((1, ct), F32), unroll=4)
        gsum = g_s[...]
        gated = i * xc
        d_log_a = gsum * _shift_down(hr, 1, rows) * a - gsum * gated * (a * a / mult)
        d_gated = gsum * mult
        d_pre_r = (d_log_a * (-LRU_C) * sp) * r * (1.0 - r)
        d_pre_i = (d_gated * xc) * i * (1.0 - i)
        dprb, dpib = d_pre_r.astype(BF16), d_pre_i.astype(BF16)
        dxc = d_gated * i + _dot_nt(dprb, wrg_ref[...]) + _dot_nt(dpib, wig_ref[...])
        cw = cw_ref[...]
        dx = (dxc * cw[3:4, :] + _shift_up(dxc, 1, rows, seq) * cw[2:3, :]
              + _shift_up(dxc, 2, rows, seq) * cw[1:2, :] + _shift_up(dxc, 3, rows, seq) * cw[0:1, :])
        dx_ref[...] = dx.astype(BF16)

        @pl.when(pl.program_id(1) == 0)
        def _():
            dwrg_ref[...] = jnp.zeros_like(dwrg_ref)
            dwig_ref[...] = jnp.zeros_like(dwig_ref)
            vec_ref[...] = jnp.zeros_like(vec_ref)

        dwrg_ref[...] += _dot_tn(xcb, dprb)
        dwig_ref[...] += _dot_tn(xcb, dpib)

        def colsum(v):
            return jnp.sum(v, axis=0, keepdims=True)

        d_sp = colsum(d_log_a * (-LRU_C) * r)
        vec_ref[0:1, :] += colsum(d_pre_r)
        vec_ref[1:2, :] += colsum(d_pre_i)
        vec_ref[2:3, :] += d_sp * (-_sig(-lamv))
        vec_ref[3:4, :] += colsum(dxc)
        vec_ref[4:5, :] += colsum(dxc * _shift_down(xv, 3, rows))
        vec_ref[5:6, :] += colsum(dxc * _shift_down(xv, 2, rows))
        vec_ref[6:7, :] += colsum(dxc * _shift_down(xv, 1, rows))
        vec_ref[7:8, :] += colsum(dxc * xv)

    vec = pl.BlockSpec((1, ct), lambda c, b: (0, c))
    gate_w = pl.BlockSpec((None, ct, ct), lambda c, b: (c, 0, 0))
    tile = pl.BlockSpec((seq, ct), lambda c, b: (b, c))
    outs, landed = _call(
        body,
        grid=(n_ct, n_seq),
        in_specs=[
            tile,
            pl.BlockSpec((seq, ct), lambda c, b: (b, c)),
            pl.BlockSpec((seq, ct), lambda c, b: (b, n_ct + c)),
            tile, tile,
            pl.BlockSpec((CONV_W, ct), lambda c, b: (0, c)), gate_w, vec, gate_w, vec, vec,
        ],
        out_specs=[tile, tile, gate_w, gate_w, pl.BlockSpec((8, ct), lambda c, b: (0, c))],
        out_shape=[jax.ShapeDtypeStruct((t, D_MODEL), BF16), jax.ShapeDtypeStruct((t, D_MODEL), BF16),
                   jax.ShapeDtypeStruct((n_ct, ct, ct), F32), jax.ShapeDtypeStruct((n_ct, ct, ct), F32),
                   jax.ShapeDtypeStruct((8, D_MODEL), F32)],
        scratch_shapes=[pltpu.VMEM((seq, ct), F32)] * 3,
        args=(dya, z, z, xc, hr, conv_w, wrg_bd, b_rg, wig_bd, b_ig, lam), name="rnn_bwd",
        semantics=("parallel", "arbitrary"), hosted=hosted)
    return (*outs, landed) if hosted is not None else outs


def _split_hi_lo(x):
    hi = x.astype(BF16)
    return hi, (x - hi.astype(F32)).astype(BF16)


def _dot_split(x, m_twice):
    hi, lo = _split_hi_lo(x)
    return jnp.dot(jnp.concatenate([hi, lo], axis=1), m_twice, preferred_element_type=F32)


def _head_matrices(width):
    ec = ((lax.broadcasted_iota(jnp.int32, (2 * width, LANES), 0) & (width - 1)) // HEAD_DIM
          == lax.broadcasted_iota(jnp.int32, (2 * width, LANES), 1))
    ee = (lax.broadcasted_iota(jnp.int32, (2 * LANES, width), 1) // HEAD_DIM
          == (lax.broadcasted_iota(jnp.int32, (2 * LANES, width), 0) & (LANES - 1)))
    return jnp.where(ec, 1.0, 0.0).astype(BF16), jnp.where(ee, 1.0, 0.0).astype(BF16)


def _swap_halves(y):
    w = y.shape[1]
    first = (lax.broadcasted_iota(jnp.int32, y.shape, 1) % HEAD_DIM) < HEAD_DIM // 2
    return jnp.where(first, pltpu.roll(y, w - HEAD_DIM // 2, 1), pltpu.roll(y, HEAD_DIM // 2, 1))


def _normrope_fwd(x, gain, cos_t, sin_t, ec, ee):
    w = x.shape[1]
    rs = _dot_split(lax.rsqrt(_dot_split(x * x, ec) * (1.0 / HEAD_DIM) + NORM_EPS), ee)
    nx = x * rs
    y = nx * gain
    reps = w // LANES
    out = y * jnp.tile(cos_t, (1, reps)) + _swap_halves(y) * jnp.tile(sin_t, (1, reps))
    return out, nx, rs


def _normrope_bwd(dout, nx, rs, gain, cos_t, sin_t, ec, ee):
    w = dout.shape[1]
    reps = w // LANES
    dy = dout * jnp.tile(cos_t, (1, reps)) + _swap_halves(dout * jnp.tile(sin_t, (1, reps)))
    dgain = jnp.sum(dy * nx, axis=0, keepdims=True)
    dn = dy * gain
    seg = _dot_split(_dot_split(dn * nx, ec) * (1.0 / HEAD_DIM), ee)
    return rs * (dn - nx * seg), dgain


def _pair_operand(t, group):
    chunk = t[:, (group // 2) * LANES:(group // 2 + 1) * LANES]
    low = lax.broadcasted_iota(jnp.int32, chunk.shape, 1) < HEAD_DIM
    rolled = pltpu.roll(chunk, HEAD_DIM, 1)
    return jnp.where(low, chunk, rolled) if group % 2 == 0 else jnp.where(low, rolled, chunk)


GROUP = N_Q_HEADS // N_KV_HEADS
GROUP_W = GROUP * HEAD_DIM


def _replicate_head(t, group):
    return jnp.tile(_pair_operand(t, group), (1, 2))


def _head_blocks(t):
    seg = lax.broadcasted_iota(jnp.int32, t.shape, 1) // HEAD_DIM
    return jnp.concatenate([jnp.where(seg == h, t, 0.0) for h in range(GROUP)], axis=0)


def _stack_heads(t_t, rows):
    return jnp.concatenate([t_t[:, h * rows:(h + 1) * rows] for h in range(GROUP)], axis=0)


def _head_rows(mat_t, group):
    return jnp.concatenate([mat_t[GROUP * group + h:GROUP * group + h + 1, :] for h in range(GROUP)], axis=1)


def _window_masks(blk):
    key = lax.broadcasted_iota(jnp.int32, (blk, GROUP * blk), 0)
    query = lax.broadcasted_iota(jnp.int32, (blk, GROUP * blk), 1) & (blk - 1)
    return key > query, key <= query


def _mask_window(t, before_ok, own_ok, fill):
    blk = t.shape[0] // 2
    return jnp.concatenate([jnp.where(before_ok, t[:blk], fill), jnp.where(own_ok, t[blk:], fill)], axis=0)


def _attn_fwd(z, cos_t, sin_t, q_gain_t, k_gain_t, sinks_t, *, n_seq, seq, hosted=None):
    t = n_seq * seq
    blk = WINDOW
    nb = seq // blk

    def body(q_ref, kp_ref, kc_ref, vp_ref, vc_ref, cosc_ref, sinc_ref, cosp_ref, sinp_ref, qg_ref, kg_ref, sk_ref,
             o_ref, l_ref):
        n = pl.program_id(1)
        ecq, eeq = _head_matrices(D_MODEL)
        eck, eek = _head_matrices(KV_W)
        cosc, sinc = cosc_ref[...], sinc_ref[...]
        qh, _, _ = _normrope_fwd(q_ref[...], qg_ref[...], cosc, sinc, ecq, eeq)
        qh = qh * (HEAD_DIM ** -0.5)
        kc, _, _ = _normrope_fwd(kc_ref[...], kg_ref[...], cosc, sinc, eck, eek)
        kp, _, _ = _normrope_fwd(kp_ref[...], kg_ref[...], cosp_ref[...], sinp_ref[...], eck, eek)
        kcat = jnp.concatenate([kp, kc], axis=0)
        vcat = jnp.concatenate([vp_ref[...], vc_ref[...]], axis=0)
        above, causal = _window_masks(blk)
        above = above & (n > 0)
        head_row = lax.broadcasted_iota(jnp.int32, (blk, blk), 0)
        sk_t = jnp.broadcast_to(sk_ref[...], (blk, LANES)).T
        vcat_t = vcat.T.astype(BF16)
        lmat = jnp.zeros((blk, blk), F32)
        groups = range(N_KV_HEADS)
        cols = [slice(g * GROUP_W, (g + 1) * GROUP_W) for g in groups]
        scores = [_dot_nt(_replicate_head(kcat, g).astype(BF16), _head_blocks(qh[:, cols[g]]).astype(BF16))
                  for g in groups]
        probs = []
        for g in groups:
            s = _mask_window(scores[g], above, causal, NEG_BIG)
            sink = _head_rows(sk_t, g)
            m = jnp.maximum(jnp.max(s, axis=0, keepdims=True), sink)
            e = jnp.exp(s - m)
            den = jnp.sum(e, axis=0, keepdims=True) + jnp.exp(sink - m)
            probs.append((e * (1.0 / den)).astype(BF16))
            lse = m + jnp.log(den)
            for h in range(GROUP):
                lmat = lmat + jnp.where(head_row == GROUP * g + h, lse[:, h * blk:(h + 1) * blk], 0.0)
        for g in groups:
            out_t = jnp.dot(vcat_t[g * HEAD_DIM:(g + 1) * HEAD_DIM], probs[g], preferred_element_type=F32)
            o_ref[:, cols[g]] = _stack_heads(out_t, blk).T.astype(BF16)
        l_ref[...] = lmat

    def row(b, n):
        return b * nb + n

    def prev(b, n):
        return b * nb + jnp.maximum(n - 1, 0)

    kw = KV_W
    tab_c = pl.BlockSpec((blk, LANES), lambda b, n: (n, 0))
    tab_p = pl.BlockSpec((blk, LANES), lambda b, n: (jnp.maximum(n - 1, 0), 0))
    outs, landed = _call(
        body,
        grid=(n_seq, nb),
        in_specs=[
            pl.BlockSpec((blk, D_MODEL), lambda b, n: (row(b, n), COL_RNN_END // D_MODEL)),
            pl.BlockSpec((blk, kw), lambda b, n: (prev(b, n), (COL_RNN_END + ATTN_K_AT) // kw)),
            pl.BlockSpec((blk, kw), lambda b, n: (row(b, n), (COL_RNN_END + ATTN_K_AT) // kw)),
            pl.BlockSpec((blk, kw), lambda b, n: (prev(b, n), (COL_RNN_END + ATTN_V_AT) // kw)),
            pl.BlockSpec((blk, kw), lambda b, n: (row(b, n), (COL_RNN_END + ATTN_V_AT) // kw)),
            tab_c, tab_c, tab_p, tab_p,
            pl.BlockSpec((1, D_MODEL), lambda b, n: (0, 0)),
            pl.BlockSpec((1, kw), lambda b, n: (0, 0)),
            pl.BlockSpec((1, LANES), lambda b, n: (0, 0)),
        ],
        out_specs=[pl.BlockSpec((blk, D_MODEL), lambda b, n: (row(b, n), 0)),
                   pl.BlockSpec((blk, LANES), lambda b, n: (row(b, n), 0))],
        out_shape=[jax.ShapeDtypeStruct((t, D_MODEL), BF16), jax.ShapeDtypeStruct((t, LANES), F32)],
        args=(z, z, z, z, z, cos_t, sin_t, cos_t, sin_t, q_gain_t, k_gain_t, sinks_t), name="attn_fwd",
        semantics=("parallel", "parallel"), hosted=hosted)
    return (*outs, landed) if hosted is not None else outs


def _attn_bwd(z, o, lse, do, cos_t, sin_t, q_gain_t, k_gain_t, sinks_t, *, n_seq, seq, hosted=None):
    t = n_seq * seq
    blk = WINDOW
    nb = seq // blk
    kw = KV_W
    scale = HEAD_DIM ** -0.5

    def body(qc_ref, qn_ref, kc_ref, vp_ref, vc_ref, oc_ref, on_ref, doc_ref, don_ref, lc_ref, ln_ref,
             cosc_ref, sinc_ref, cosn_ref, sinn_ref, qg_ref, kg_ref, sk_ref,
             dz_ref, vec_ref, dq_s, q_s, k_s):
        n = pl.program_id(1)
        ecq, eeq = _head_matrices(D_MODEL)
        eck, eek = _head_matrices(KV_W)
        cosc, sinc = cosc_ref[...], sinc_ref[...]
        qg, kg = qg_ref[...], kg_ref[...]
        own, other = n & 1, 1 - (n & 1)

        @pl.when(n == 0)
        def _():
            for part, value in enumerate(_normrope_fwd(qc_ref[...], qg, cosc, sinc, ecq, eeq)):
                q_s[own, part] = value
            k_s[other] = jnp.zeros((blk, kw), F32)

        for part, value in enumerate(_normrope_fwd(qn_ref[...], qg, cosn_ref[...], sinn_ref[...], ecq, eeq)):
            q_s[other, part] = value
        qhc, nqc, rsqc = q_s[own, 0], q_s[own, 1], q_s[own, 2]
        qhn = q_s[other, 0]
        khc, nkc, rskc = _normrope_fwd(kc_ref[...], kg, cosc, sinc, eck, eek)
        khp = k_s[other]
        k_s[own] = khc
        doc = doc_ref[...].astype(F32)
        don = don_ref[...].astype(F32)
        delc = _dot_split(doc * oc_ref[...].astype(F32), ecq)
        deln = _dot_split(don * on_ref[...].astype(F32), ecq)
        lc_t, ln_t, delc_t, deln_t = lc_ref[...], ln_ref[...], delc.T, deln.T
        above, causal = _window_masks(blk)
        above_c, above_n = above & (n > 0), above & (n < nb - 1)
        seg = lax.broadcasted_iota(jnp.int32, (blk, GROUP_W), 1) // HEAD_DIM
        lane = lax.broadcasted_iota(jnp.int32, (1, LANES), 1)
        sk_t = jnp.broadcast_to(sk_ref[...], (blk, LANES)).T
        dsink = jnp.zeros((1, LANES), F32)
        kcat = jnp.concatenate([khp, khc], axis=0)
        vcat = jnp.concatenate([vp_ref[...], vc_ref[...]], axis=0)
        kcat_t = kcat.T.astype(BF16)
        dkh = jnp.zeros((blk, GROUP_W), F32)
        dvh = jnp.zeros((blk, GROUP_W), F32)

        def fold_to(group, t):
            total = t + pltpu.roll(t, HEAD_DIM, 1)
            total = total + pltpu.roll(total, 2 * HEAD_DIM, 1)
            return jnp.where(seg == group, total, 0.0)

        groups = range(N_KV_HEADS)
        cols = [slice(g * GROUP_W, (g + 1) * GROUP_W) for g in groups]
        qsc, qsn = qhc * scale, qhn * scale
        qb_c = [_head_blocks(qsc[:, cols[g]]).astype(BF16) for g in groups]
        qb_n = [_head_blocks(qsn[:, cols[g]]).astype(BF16) for g in groups]
        dob_c = [_head_blocks(doc[:, cols[g]]).astype(BF16) for g in groups]
        dob_n = [_head_blocks(don[:, cols[g]]).astype(BF16) for g in groups]
        raw = []
        for g in groups:
            krep = _replicate_head(kcat, g).astype(BF16)
            vrep = _replicate_head(vcat, g).astype(BF16)
            raw.append((_dot_nt(krep, qb_c[g]), _dot_nt(vrep, dob_c[g]),
                        _dot_nt(krep[blk:], qb_n[g]), _dot_nt(vrep[blk:], dob_n[g])))
        cooked = []
        for g in groups:
            s_c, dp_c, s_n, dp_n = raw[g]
            l_row, d_row = _head_rows(lc_t, g), _head_rows(delc_t, g)
            p_c = _mask_window(jnp.exp(s_c - l_row), above_c, causal, 0.0)
            ds_c = (p_c * (dp_c - d_row)).astype(BF16)
            p_n = jnp.where(above_n, jnp.exp(s_n - _head_rows(ln_t, g)), 0.0)
            ds_n = (p_n * (dp_n - _head_rows(deln_t, g))).astype(BF16)
            cooked.append((p_c[blk:].astype(BF16), ds_c, p_n.astype(BF16), ds_n))
            p_sink = jnp.exp(_head_rows(sk_t, g) - l_row) * d_row
            for h in range(GROUP):
                dsink = dsink + jnp.where(lane == GROUP * g + h,
                                          -jnp.sum(p_sink[:, h * blk:(h + 1) * blk], axis=1, keepdims=True), 0.0)
        for g in groups:
            p_cb, ds_c, p_nb, ds_n = cooked[g]
            dq_t = jnp.dot(kcat_t[g * HEAD_DIM:(g + 1) * HEAD_DIM], ds_c, preferred_element_type=F32)
            dq_s[:, cols[g]] = _stack_heads(dq_t, blk).T * scale
            dk_rep = (jnp.dot(ds_c[blk:], qb_c[g], preferred_element_type=F32)
                      + jnp.dot(ds_n, qb_n[g], preferred_element_type=F32))
            dv_rep = (jnp.dot(p_cb, dob_c[g], preferred_element_type=F32)
                      + jnp.dot(p_nb, dob_n[g], preferred_element_type=F32))
            dkh = dkh + fold_to(g, dk_rep)
            dvh = dvh + fold_to(g, dv_rep)
        dq, dqg = _normrope_bwd(dq_s[...], nqc, rsqc, qg, cosc, sinc, ecq, eeq)
        dk, dkg = _normrope_bwd(dkh, nkc, rskc, kg, cosc, sinc, eck, eek)
        dz_ref[:, :ATTN_K_AT] = dq.astype(BF16)
        dz_ref[:, ATTN_K_AT:ATTN_V_AT] = dk.astype(BF16)
        dz_ref[:, ATTN_V_AT:] = dvh.astype(BF16)

        @pl.when(n == 0)
        def _():
            vec_ref[...] = jnp.zeros_like(vec_ref)

        vec_ref[0:1, :] += dqg
        vec_ref[1:2, 0:kw] += dkg
        vec_ref[2:3, 0:LANES] += dsink

    def row(b, n):
        return b * nb + n

    def prev(b, n):
        return b * nb + jnp.maximum(n - 1, 0)

    def nxt(b, n):
        return b * nb + jnp.minimum(n + 1, nb - 1)

    def tiles(width, col, which):
        return pl.BlockSpec((blk, width), lambda b, n: (which(b, n), col))

    def table(which):
        return pl.BlockSpec((blk, LANES), lambda b, n: (which(0, n), 0))

    outs, landed = _call(
        body,
        grid=(n_seq, nb),
        in_specs=[
            tiles(D_MODEL, COL_RNN_END // D_MODEL, row), tiles(D_MODEL, COL_RNN_END // D_MODEL, nxt),
            tiles(kw, (COL_RNN_END + ATTN_K_AT) // kw, row),
            tiles(kw, (COL_RNN_END + ATTN_V_AT) // kw, prev), tiles(kw, (COL_RNN_END + ATTN_V_AT) // kw, row),
            tiles(D_MODEL, 0, row), tiles(D_MODEL, 0, nxt),
            tiles(D_MODEL, 0, row), tiles(D_MODEL, 0, nxt),
            tiles(LANES, 0, row), tiles(LANES, 0, nxt),
            table(row), table(row), table(nxt), table(nxt),
            pl.BlockSpec((1, D_MODEL), lambda b, n: (0, 0)),
            pl.BlockSpec((1, kw), lambda b, n: (0, 0)),
            pl.BlockSpec((1, LANES), lambda b, n: (0, 0)),
        ],
        out_specs=[tiles(ATTN_W, 0, row), pl.BlockSpec((None, 8, D_MODEL), lambda b, n: (b, 0, 0))],
        out_shape=[jax.ShapeDtypeStruct((t, ATTN_W), BF16), jax.ShapeDtypeStruct((n_seq, 8, D_MODEL), F32)],
        scratch_shapes=[pltpu.VMEM((blk, D_MODEL), F32), pltpu.VMEM((2, 3, blk, D_MODEL), F32),
                        pltpu.VMEM((2, blk, kw), F32)],
        args=(z, z, z, z, z, o, o, do, do, lse, lse, cos_t, sin_t, cos_t, sin_t,
              q_gain_t, k_gain_t, sinks_t), name="attn_bwd", semantics=("arbitrary", "arbitrary"), hosted=hosted)
    return (*outs, landed) if hosted is not None else outs


def _rope_tables(seq):
    inv = ROPE_THETA ** (-jnp.arange(0, HEAD_DIM, 2, dtype=F32) / HEAD_DIM)
    ang = jnp.arange(seq, dtype=F32)[:, None] * inv[None, :]
    cos, sin = jnp.cos(ang), jnp.sin(ang)
    return jnp.tile(jnp.concatenate([cos, cos], axis=1), (1, 2)), jnp.tile(jnp.concatenate([-sin, sin], axis=1), (1, 2))


def _block_diag_tiles(w):
    per = RNN_TILE // RNN_BLOCK_W
    w4 = w.reshape(D_MODEL // RNN_TILE, per, RNN_BLOCK_W, RNN_BLOCK_W)
    eye = jnp.eye(per, dtype=w.dtype)
    dense = jnp.einsum("tpij,pq->tpiqj", w4, eye)
    return dense.reshape(D_MODEL // RNN_TILE, RNN_TILE, RNN_TILE).astype(BF16)


def _block_diag_extract(dense):
    per = RNN_TILE // RNN_BLOCK_W
    d5 = dense.reshape(D_MODEL // RNN_TILE, per, RNN_BLOCK_W, per, RNN_BLOCK_W)
    blocks = jnp.stack([d5[:, p, :, p, :] for p in range(per)], axis=1)
    return blocks.reshape(D_MODEL // RNN_BLOCK_W, RNN_BLOCK_W, RNN_BLOCK_W)


def _local_step(x, p, target, w, *, n_seq, seq, comm=None):
    w = dict(w)

    def run(tag, fn, *args, **kwargs):
        hosted = comm.host(tag) if comm is not None else None
        if hosted is None:
            return fn(*args, **kwargs)
        *outs, landed = fn(*args, hosted=hosted, **kwargs)
        comm.landed(tag, landed, w)
        return outs[0] if len(outs) == 1 else outs

    def ready(batch, grads, extra=None):
        if comm is not None:
            comm.ready(batch, grads, extra)

    cos_t, sin_t = _rope_tables(seq)
    q_gain_t = jnp.tile(w["q_gain"], (1, N_Q_HEADS))
    k_gain_t = jnp.tile(w["k_gain"], (1, N_KV_HEADS))
    sinks_t = jnp.pad(w["sinks"], ((0, 0), (0, LANES - N_Q_HEADS)))
    wrg_bd, wig_bd = _block_diag_tiles(w["w_rg"]), _block_diag_tiles(w["w_ig"])
    dims = dict(n_seq=n_seq, seq=seq)

    z, h = run("mm_in", _norm_matmul, x, w["g_mix"], w["w_in"], tm=1024, tn=IN_TOTAL // 4, name="mm_in")
    gate_tile = 512
    ga_at, gb_at = COL_ATTN_END // gate_tile, (COL_ATTN_END + D_MODEL) // gate_tile
    xc, hr, ya_in = run("rnn_fwd", _rnn_fwd, z, w["conv_w"], w["conv_b"], wrg_bd, w["b_rg"], wig_bd, w["b_ig"],
                        w["lru_lambda"], **dims)
    o, lse = run("attn_fwd", _attn_fwd, z, cos_t, sin_t, q_gain_t, k_gain_t, sinks_t, **dims)
    ya = run("mm_rnn_proj", _matmul, ya_in, w["w_rnn_proj"], mode="nn", tm=1024, tn=1024, out_dtypes=[F32],
             name="mm_rnn_proj")
    yb, merged = _matmul(
        o, w["w_attn_proj"], mode="nn", tm=1024, tn=gate_tile, out_dtypes=[F32, BF16], name="mm_attn_proj",
        epilogue=lambda acc, ga, gb, yav: (acc, _sig(ga) * yav + _sig(gb) * acc),
        extras=(z, z, ya), extra_col_blocks=(ga_at, gb_at, 0))
    def residual_then_norm(acc, res, gain):
        new = res + acc
        return new, _rmsnorm_rows(new, gain)

    x1, hm = _matmul(merged, w["w_out"], mode="nn", tm=512, tn=1024, out_dtypes=[F32, BF16], name="mm_out",
                     epilogue=residual_then_norm, extras=(x,), row_vecs=(w["g_mlp"],))
    act = _matmul(hm, w["w_up"], mode="nn", tm=1024, tn=1024, out_dtypes=[BF16], name="mm_up",
                  epilogue=lambda acc: (jnp.square(jnp.maximum(acc, 0.0)),))
    x2, hp = _matmul(act, w["w_down"], mode="nn", tm=512, tn=1024, out_dtypes=[F32, BF16], name="mm_down",
                     epilogue=residual_then_norm, extras=(x1,), row_vecs=(w["g_ple"],))
    p_bf = p.astype(BF16)
    e = _matmul(p_bf, w["w_ple_proj"], mode="nn", tm=1024, tn=1024, out_dtypes=[F32], name="mm_ple_proj")

    def loss_head(gt, x2v, ev, tgt):
        sg = _sig(gt)
        diff = x2v + ev * sg - tgt
        dx = diff * (1.0 / D_MODEL)
        return dx, dx * ev * sg * (1.0 - sg), dx * sg, jnp.sum(diff * diff, axis=0, keepdims=True)

    dx3, dgt, de, loss_row = _matmul(hp, w["w_ple_gate"], mode="nn", tm=512, tn=1024, out_dtypes=[F32, BF16, BF16],
                                     name="mm_ple_gate", epilogue=loss_head, extras=(x2, e, target), n_row_sums=1)

    g = {}
    g["w_ple_proj"] = _matmul_tn(p_bf, de, tk=PLE_DIM, tn=1024, tt=1024, name="mm_d_ple_proj",
                                 slot_cols=D_MODEL // N_DEV)
    g["w_ple_gate"] = _matmul_tn(hp, dgt, tk=1024, tn=1024, tt=1024, name="mm_d_ple_gate")
    def through_norm(dy, xv, dres, gain):
        dx, dgain = _rmsnorm_bwd_rows(dy, xv, dres, gain)
        return dx, dx, dgain

    dx2, dx2_bf, g["g_ple"] = _matmul(
        dgt, w["w_ple_gate"], mode="nt", tm=512, tn=1024, out_dtypes=[F32, BF16], name="mm_dhp",
        epilogue=through_norm, extras=(x2, dx3), row_vecs=(w["g_ple"],), n_row_sums=1)
    g["w_down"] = _matmul_tn(act, dx2_bf, tk=1024, tn=1024, tt=1024, name="mm_d_down")

    def relu_grad(dact, a):
        a = a.astype(F32)
        return (dact * (2.0 * jnp.where(a > 0.0, a * lax.rsqrt(a), 0.0)),)

    du = _matmul(dx2_bf, w["w_down"], mode="nt", tm=1024, tn=1024, out_dtypes=[BF16], name="mm_dact",
                 epilogue=relu_grad, extras=(act,))
    g["w_up"] = _matmul_tn(hm, du, tk=1024, tn=1024, tt=1024, name="mm_d_up", slot_cols=D_FF // N_DEV)
    ready(1, g)
    dx1, dx1_bf, g["g_mlp"] = run(
        "mm_dhm", _matmul, du, w["w_up"], mode="nt", tm=512, tn=1024, out_dtypes=[F32, BF16], name="mm_dhm",
        epilogue=through_norm, extras=(x1, dx2), row_vecs=(w["g_mlp"],), n_row_sums=1)
    g["w_out"] = _matmul_tn(merged, dx1_bf, tk=1024, tn=1024, tt=1024, name="mm_d_out")
    def merge_bwd(dm, ga, gb, yav, ybv):
        sa, sb = _sig(ga), _sig(gb)
        return dm * sa, dm * sb, dm * yav * sa * (1.0 - sa), dm * ybv * sb * (1.0 - sb)

    dya, dyb, dga, dgb = _matmul(dx1_bf, w["w_out"], mode="nt", tm=1024, tn=gate_tile, out_dtypes=[BF16] * 4,
                                 name="mm_dmerged", epilogue=merge_bwd, extras=(z, z, ya, yb),
                                 extra_col_blocks=(ga_at, gb_at, 0, 0))
    g["w_rnn_proj"] = _matmul_tn(ya_in, dya, tk=1024, tn=1024, tt=1024, name="mm_d_rnn_proj")
    g["w_attn_proj"] = _matmul_tn(o, dyb, tk=1024, tn=1024, tt=1024, name="mm_d_attn_proj")
    ready(2, g)
    dya_in = run("mm_dya_in", _matmul, dya, w["w_rnn_proj"], mode="nt", tm=1024, tn=1024, out_dtypes=[F32],
                 name="mm_dya_in")
    do = _matmul(dyb, w["w_attn_proj"], mode="nt", tm=1024, tn=1024, out_dtypes=[BF16], name="mm_do")
    dx_rnn, dg_rnn, dwrg_dense, dwig_dense, rnn_vec = run(
        "rnn_bwd", _rnn_bwd, dya_in, z, xc, hr, w["conv_w"], wrg_bd, w["b_rg"], wig_bd, w["b_ig"],
        w["lru_lambda"], **dims)
    dz_attn, attn_vec = run("attn_bwd", _attn_bwd, z, o, lse, do, cos_t, sin_t, q_gain_t, k_gain_t, sinks_t,
                            **dims)
    dz_parts = (dx_rnn, dg_rnn, dz_attn, dga, dgb)
    g["w_rg"] = _block_diag_extract(dwrg_dense)
    g["w_ig"] = _block_diag_extract(dwig_dense)
    g["b_rg"], g["b_ig"], g["lru_lambda"], g["conv_b"] = (rnn_vec[i:i + 1] for i in range(4))
    g["conv_w"] = rnn_vec[4:8]
    attn_vec = attn_vec[0] if n_seq == 1 else functools.reduce(jnp.add, [attn_vec[b] for b in range(n_seq)])
    g["q_gain"] = attn_vec[0].reshape(N_Q_HEADS, HEAD_DIM).sum(axis=0)[None, :]
    g["k_gain"] = attn_vec[1, :KV_W].reshape(N_KV_HEADS, HEAD_DIM).sum(axis=0)[None, :]
    g["sinks"] = attn_vec[2:3, :N_Q_HEADS]
    ready(SMALL_BATCH, g, {LOSS_ROW: loss_row})
    g["w_in"] = jnp.concatenate(
        list(run("mm_d_in_rnn", _matmul_tn_multi, h, dz_parts[:2], tt=1024, name="mm_d_in_rnn"))
        + list(run("mm_d_in_rest", _matmul_tn_multi, h, dz_parts[2:], tt=512, name="mm_d_in_rest")), axis=1)
    ready(3, g)
    w_in_attn, w_in_gate = w["w_in"][:, COL_RNN_END:COL_ATTN_END], w["w_in"][:, COL_ATTN_END:]
    windows = ((w["w_in"], (0, D_MODEL)), (w["w_in"], (D_MODEL, D_MODEL)), (w_in_attn, (0, ATTN_W)),
               (w_in_gate, (0, D_MODEL)), (w_in_gate, (D_MODEL, D_MODEL)))
    grad_x, g["g_mix"] = run(
        "mm_dh", _matmul, dz_parts, [wd[0] for wd in windows], mode="nt", tm=256, tn=1024, out_dtypes=[F32],
        name="mm_dh", b_cols=[wd[1] for wd in windows], epilogue=_rmsnorm_bwd_rows, extras=(x, dx1),
        row_vecs=(w["g_mix"],), n_row_sums=1)
    return jnp.sum(loss_row), grad_x, g


MESH_ID = pl.DeviceIdType.MESH


def _coords(index):
    return (index >> 2) & 1, (index >> 1) & 1, index & 1


def _exchange(srcs, kinds, *, name):
    n = len(srcs)
    n_peer = N_DEV - 1

    def body(*refs):
        src, dst = refs[:n], refs[n:2 * n]
        send_sems, recv_sems, local_sems = refs[2 * n:]
        me = 4 * lax.axis_index("x") + 2 * lax.axis_index("y") + lax.axis_index("c")

        def remote(i, d):
            peer = (me + d) & (N_DEV - 1)
            piece = src[i] if kinds[i] == "gather" else src[i].at[peer]
            return pltpu.make_async_remote_copy(
                src_ref=piece, dst_ref=dst[i].at[me], send_sem=send_sems.at[i * n_peer + d - 1],
                recv_sem=recv_sems.at[i * n_peer + d - 1], device_id=_coords(peer), device_id_type=MESH_ID)

        def arrival(i, d):
            sender = (me - d) & (N_DEV - 1)
            piece = src[i] if kinds[i] == "gather" else src[i].at[sender]
            return pltpu.make_async_remote_copy(
                src_ref=piece, dst_ref=dst[i].at[sender], send_sem=send_sems.at[i * n_peer + d - 1],
                recv_sem=recv_sems.at[i * n_peer + d - 1], device_id=_coords(sender), device_id_type=MESH_ID)

        own = []
        for i in range(n):
            piece = src[i] if kinds[i] == "gather" else src[i].at[me]
            own.append(pltpu.make_async_copy(piece, dst[i].at[me], local_sems.at[i]))
            own[-1].start()
        sent = [remote(i, d) for d in range(1, N_DEV) for i in range(n)]
        for cp in sent:
            cp.start()
        for d in range(1, N_DEV):
            for i in range(n):
                arrival(i, d).wait_recv()
        for cp in sent:
            cp.wait_send()
        for cp in own:
            cp.wait()

    def out_of(s, kind):
        shape = s.shape if kind == "scatter" else (N_DEV,) + s.shape
        return jax.ShapeDtypeStruct(shape, s.dtype)

    any_spec = pl.BlockSpec(memory_space=pl.ANY)
    return pl.pallas_call(
        body,
        in_specs=[any_spec] * n,
        out_specs=[any_spec] * n,
        out_shape=[out_of(s, k) for s, k in zip(srcs, kinds)],
        scratch_shapes=[pltpu.SemaphoreType.DMA((n * n_peer,)), pltpu.SemaphoreType.DMA((n * n_peer,)),
                        pltpu.SemaphoreType.DMA((n,))],
        compiler_params=pltpu.CompilerParams(has_side_effects=True),
        name=name,
    )(*srcs)


def _remote(src, dst, send_sem, recv_sem, to):
    return pltpu.make_async_remote_copy(src_ref=src, dst_ref=dst, send_sem=send_sem, recv_sem=recv_sem,
                                        device_id=to, device_id_type=MESH_ID)


def _gather_two_level(shards, *, name):
    n = len(shards)
    per = N_DEV - 1

    def body(*refs):
        src, dst = refs[:n], refs[n:2 * n]
        send_sems, recv_sems, local_sems = refs[2 * n:]
        x, y, c = lax.axis_index("x"), lax.axis_index("y"), lax.axis_index("c")
        me, sibling = (x, y, c), (x, y, 1 - c)
        chips = [(1 - x, y), (x, 1 - y), (1 - x, 1 - y)]

        def slot(pos):
            return 4 * pos[0] + 2 * pos[1] + pos[2]

        def copy(i, k, block, to, from_shard=False):
            source = src[i] if from_shard else dst[i].at[slot(block)]
            return _remote(source, dst[i].at[slot(block)], send_sems.at[i * per + k], recv_sems.at[i * per + k], to)

        mine = [pltpu.make_async_copy(src[i], dst[i].at[slot(me)], local_sems.at[i]) for i in range(n)]
        for cp in mine:
            cp.start()
        first = []
        for i in range(n):
            first.append(copy(i, 0, me, sibling, from_shard=True))
            first += [copy(i, 1 + j, me, (*chip, c), from_shard=True) for j, chip in enumerate(chips)]
        for cp in first:
            cp.start()
        passed = []
        for i in range(n):
            for j, chip in enumerate(chips):
                copy(i, 1 + j, (*chip, c), me).wait_recv()
                passed.append(copy(i, 4 + j, (*chip, c), sibling))
                passed[-1].start()
        for i in range(n):
            copy(i, 0, sibling, me).wait_recv()
            for j, chip in enumerate(chips):
                copy(i, 4 + j, (*chip, 1 - c), me).wait_recv()
        for cp in first + passed:
            cp.wait_send()
        for cp in mine:
            cp.wait()

    any_spec = pl.BlockSpec(memory_space=pl.ANY)
    return pl.pallas_call(
        body,
        in_specs=[any_spec] * n,
        out_specs=[any_spec] * n,
        out_shape=[jax.ShapeDtypeStruct((N_DEV,) + s.shape, s.dtype) for s in shards],
        scratch_shapes=[pltpu.SemaphoreType.DMA((n * per,)), pltpu.SemaphoreType.DMA((n * per,)),
                        pltpu.SemaphoreType.DMA((n,))],
        name=name,
    )(*shards)


CHIPS = N_DEV // 2


def _other_chips(x, y):
    return [(x, 1 - y), (1 - x, y), (1 - x, 1 - y)]


def _hosted_gather_first(shards):
    n = len(shards)
    per = CHIPS

    def plan(src, dst, send_sems, recv_sems, local_sems, first_sem):
        x, y, c = lax.axis_index("x"), lax.axis_index("y"), lax.axis_index("c")
        peers = [(x, y, 1 - c)] + [(*chip, c) for chip in _other_chips(x, y)]
        copies = []
        for i in range(n):
            own = pltpu.make_async_copy(src[i], dst[i].at[4 * x + 2 * y + c], local_sems.at[first_sem + i])
            copies.append(_Xfer(own.start, own.wait))
        for j, peer in enumerate(peers):
            for i in range(n):
                k = first_sem + i * per + j
                out = _remote(src[i], dst[i].at[4 * x + 2 * y + c], send_sems.at[k], recv_sems.at[k], peer)
                arrival = _remote(src[i], dst[i].at[4 * peer[0] + 2 * peer[1] + peer[2]], send_sems.at[k],
                                  recv_sems.at[k], peer)

                def wait(out=out, arrival=arrival):
                    arrival.wait_recv()
                    out.wait_send()

                copies.append(_Xfer(out.start, wait))
        return copies

    out_shape = tuple(jax.ShapeDtypeStruct((N_DEV,) + s.shape, s.dtype) for s in shards)
    return _Hosted(tuple(shards), out_shape, n * per, plan)


def _hosted_gather_second(landed):
    n = len(landed)
    per = CHIPS - 1

    def plan(src, dst, send_sems, recv_sems, local_sems, first_sem):
        x, y, c = lax.axis_index("x"), lax.axis_index("y"), lax.axis_index("c")
        copies = []
        for j, chip in enumerate(_other_chips(x, y)):
            mine, theirs = 4 * chip[0] + 2 * chip[1] + c, 4 * chip[0] + 2 * chip[1] + 1 - c
            for i in range(n):
                k = first_sem + i * per + j
                out = _remote(src[i].at[mine], dst[i].at[mine], send_sems.at[k], recv_sems.at[k], (x, y, 1 - c))
                arrival = _remote(src[i].at[theirs], dst[i].at[theirs], send_sems.at[k], recv_sems.at[k],
                                  (x, y, 1 - c))

                def wait(out=out, arrival=arrival):
                    arrival.wait_recv()
                    out.wait_send()

                copies.append(_Xfer(out.start, wait))
        return copies

    out_shape = tuple(jax.ShapeDtypeStruct(a.shape, a.dtype) for a in landed)
    return _Hosted(tuple(landed), out_shape, n * per, plan, tuple((i, i) for i in range(n)))


def _hosted_sibling_swap(arrays, sliced):
    n_sems = sum(CHIPS if s else 1 for s in sliced)

    def plan(src, dst, send_sems, recv_sems, local_sems, first_sem):
        x, y, c = lax.axis_index("x"), lax.axis_index("y"), lax.axis_index("c")
        sibling = (x, y, 1 - c)
        copies, k = [], first_sem
        for i, is_sliced in enumerate(sliced):
            pieces = [(src[i].at[2 * s + 1 - c], dst[i].at[s]) for s in range(CHIPS)] if is_sliced else [(src[i], dst[i])]
            for source, target in pieces:
                cp = _remote(source, target, send_sems.at[k], recv_sems.at[k], sibling)
                copies.append(_Xfer(cp.start, cp.wait))
                k += 1
        return copies

    out_shape = tuple(jax.ShapeDtypeStruct((CHIPS,) + a.shape[1:] if s else a.shape, a.dtype)
                      for a, s in zip(arrays, sliced))
    return _Hosted(tuple(arrays), out_shape, n_sems, plan)


def _hosted_chip_exchange(arrays, sliced):
    n = len(arrays)
    per = CHIPS - 1

    def plan(src, dst, send_sems, recv_sems, local_sems, first_sem):
        x, y, c = lax.axis_index("x"), lax.axis_index("y"), lax.axis_index("c")
        chip = 2 * x + y
        copies = []
        for i in range(n):
            own = pltpu.make_async_copy(src[i].at[chip] if sliced[i] else src[i], dst[i].at[chip],
                                        local_sems.at[first_sem + i])
            copies.append(_Xfer(own.start, own.wait))
        for d in range(1, CHIPS):
            other = chip ^ d
            to = ((other >> 1) & 1, other & 1, c)
            for i in range(n):
                k = first_sem + i * per + d - 1
                source = src[i].at[other] if sliced[i] else src[i]
                out = _remote(source, dst[i].at[chip], send_sems.at[k], recv_sems.at[k], to)
                arrival = _remote(source, dst[i].at[other], send_sems.at[k], recv_sems.at[k], to)

                def wait(out=out, arrival=arrival):
                    arrival.wait_recv()
                    out.wait_send()

                copies.append(_Xfer(out.start, wait))
        return copies

    out_shape = tuple(jax.ShapeDtypeStruct(a.shape if s else (CHIPS,) + a.shape, a.dtype)
                      for a, s in zip(arrays, sliced))
    return _Hosted(tuple(arrays), out_shape, n * per, plan)


def _add_sibling(parts, received, core, *, name):
    _, r, cols = parts.shape
    tr = min(256, r)

    def body(core_ref, a_ref, b_ref, o_ref):
        o_ref[...] = (a_ref[...] + b_ref[...]).astype(BF16)

    grid_spec = pltpu.PrefetchScalarGridSpec(
        num_scalar_prefetch=1,
        grid=(CHIPS, r // tr),
        in_specs=[pl.BlockSpec((None, tr, cols), lambda k, i, core_ref: (2 * k + core_ref[0], i, 0)),
                  pl.BlockSpec((None, tr, cols), lambda k, i, core_ref: (k, i, 0))],
        out_specs=pl.BlockSpec((None, tr, cols), lambda k, i, core_ref: (k, i, 0)),
    )
    return pl.pallas_call(body, grid_spec=grid_spec, out_shape=jax.ShapeDtypeStruct((CHIPS, r, cols), BF16),
                          compiler_params=_params("parallel", "parallel"), name=name)(core, parts, received)


def _add_whole(a, b, *, name):
    def body(a_ref, b_ref, o_ref):
        o_ref[...] = a_ref[...] + b_ref[...]

    return pl.pallas_call(body, out_shape=jax.ShapeDtypeStruct(a.shape, F32), name=name)(a, b)


def _adamw(parts, w, m, v, *, name):
    r, c = w.shape
    n_parts = parts.shape[0]
    tr = min(256, r)
    c1 = 1.0 - ADAM_B1 ** ADAM_STEP
    c2 = 1.0 - ADAM_B2 ** ADAM_STEP

    def body(p_ref, w_ref, m_ref, v_ref, g_ref, d_ref, nm_ref, nv_ref):
        g = p_ref[0].astype(F32)
        for s in range(1, n_parts):
            g = g + p_ref[s].astype(F32)
        nm = ADAM_B1 * m_ref[...] + (1.0 - ADAM_B1) * g
        nv = ADAM_B2 * v_ref[...] + (1.0 - ADAM_B2) * (g * g)
        g_ref[...] = g
        nm_ref[...] = nm
        nv_ref[...] = nv
        d_ref[...] = -ADAM_LR * ((nm / c1) / (jnp.sqrt(nv / c2) + ADAM_EPS) + ADAM_WD * w_ref[...])

    tile = pl.BlockSpec((tr, c), lambda i: (i, 0))
    return pl.pallas_call(
        body,
        grid=(r // tr,),
        in_specs=[pl.BlockSpec((n_parts, tr, c), lambda i: (0, i, 0)), tile, tile, tile],
        out_specs=[tile] * 4,
        out_shape=[jax.ShapeDtypeStruct((r, c), F32)] * 4,
        compiler_params=_params("parallel"),
        name=name,
    )(parts, w, m, v)


BIG = ("w_in", "w_rnn_proj", "w_attn_proj", "w_out", "w_up", "w_down", "w_ple_gate", "w_ple_proj")
LOSS_ROW = "loss"
SMALL = (("conv_b", 1), ("b_rg", 1), ("b_ig", 1), ("lru_lambda", 1), ("g_mlp", 1), ("g_ple", 1),
         ("q_gain", 1), ("k_gain", 1), ("sinks", 1), (LOSS_ROW, 1), ("w_rg", 64), ("w_ig", 64))
SMALL_ROWS = 144
ROW_SHARDED = ("w_rnn_proj", "w_attn_proj", "w_out", "w_down", "w_ple_gate")
COL_SHARDED = ("w_in", "w_up", "w_ple_proj")
BATCHES = {1: ("w_ple_proj", "w_ple_gate", "w_down", "w_up"), 2: ("w_out", "w_rnn_proj", "w_attn_proj"),
           3: ("w_in", "conv_w")}
SMALL_BATCH = 4


def _pack_small(vals):
    rows = []
    for nm, nrow in SMALL:
        flat = vals[nm].reshape(-1).astype(F32)
        rows.append(jnp.pad(flat, (0, nrow * D_MODEL - flat.shape[0])).reshape(nrow, D_MODEL))
    used = sum(nrow for _, nrow in SMALL)
    rows.append(jnp.zeros((SMALL_ROWS - used, D_MODEL), F32))
    return jnp.concatenate(rows, axis=0)


def _unpack_small(packed, shapes):
    out, at = {}, 0
    for nm, nrow in SMALL:
        size = 1
        for s in shapes[nm]:
            size *= s
        out[nm] = packed[at:at + nrow].reshape(-1)[:size].reshape(shapes[nm])
        at += nrow
    return out


def _full_weight(name, landed):
    if name in COL_SHARDED:
        return landed.transpose(1, 0, 2).reshape(landed.shape[1], N_DEV * landed.shape[2])
    return landed.reshape(N_DEV * landed.shape[1], landed.shape[2])


def _owner_slots(name, grad):
    if name == "w_in":
        return grad.reshape(D_MODEL, N_DEV, IN_TOTAL // N_DEV).transpose(1, 0, 2)
    if name == "conv_w":
        return grad.reshape(CONV_W, N_DEV, D_MODEL // N_DEV).transpose(1, 0, 2)
    if name in COL_SHARDED:
        return grad
    return grad.reshape(N_DEV, grad.shape[0] // N_DEV, grad.shape[1])


class _StepExchanges:
    FIRST, SECOND = "first", "second"
    PROJ, OUT, PLE_GATE, UP, DOWN = (("w_rnn_proj", "w_attn_proj"), ("w_out",), ("w_ple_gate",), ("w_up",),
                                     ("w_down", "w_ple_proj"))
    GATHERS = {"mm_in": ((FIRST, PROJ), (FIRST, OUT), (FIRST, PLE_GATE)),
               "rnn_fwd": ((SECOND, PROJ), (SECOND, OUT), (SECOND, PLE_GATE), (FIRST, UP)),
               "attn_fwd": ((SECOND, UP), (FIRST, DOWN)), "mm_rnn_proj": ((SECOND, DOWN),)}
    SWAPS = {"mm_dhm": 1, "mm_dya_in": 2, "mm_d_in_rnn": SMALL_BATCH}
    CHIP_EXCHANGES = {"rnn_bwd": (1,), "attn_bwd": (2,), "mm_d_in_rest": (SMALL_BATCH,), "mm_dh": (3,)}

    def __init__(self, shards, core):
        self.shards = shards
        self.core = core
        self.parts, self.swapped, self.summed, self.half_gathered = {}, {}, {}, {}

    def ready(self, batch, grads, extra=None):
        if batch == SMALL_BATCH:
            self.parts[batch] = ([_pack_small({**grads, **extra})], [False])
            return
        arrays = [_owner_slots(nm, grads[nm]) for nm in BATCHES[batch]]
        self.parts[batch] = (arrays, [True] * len(arrays))
        if batch not in self.SWAPS.values():
            _, self.swapped[batch] = _call(
                lambda: None, grid=(1,), in_specs=[], out_specs=[], out_shape=[], args=(), name="swap_last",
                semantics=("arbitrary",), hosted=_hosted_sibling_swap(*self.parts[batch]))

    def host(self, tag):
        if tag in self.GATHERS:
            return _merge_hosted([
                _hosted_gather_first([self.shards[nm] for nm in group]) if half == self.FIRST
                else _hosted_gather_second([self.half_gathered[nm] for nm in group])
                for half, group in self.GATHERS[tag]])
        if tag in self.SWAPS:
            return _hosted_sibling_swap(*self.parts[self.SWAPS[tag]])
        if tag in self.CHIP_EXCHANGES:
            hosted = []
            for batch in self.CHIP_EXCHANGES[tag]:
                arrays, sliced = self.parts[batch]
                labels = BATCHES.get(batch, ("small",))
                sums = [_add_sibling(a, r, self.core, name="add_" + lb) if s else _add_whole(a, r, name="add_" + lb)
                        for a, r, s, lb in zip(arrays, self.swapped[batch], sliced, labels)]
                hosted.append(_hosted_chip_exchange(sums, sliced))
            return _merge_hosted(hosted)
        return None

    def landed(self, tag, landed, weights):
        if tag in self.GATHERS:
            names = [(half, nm) for half, group in self.GATHERS[tag] for nm in group]
            for (half, nm), buf in zip(names, landed):
                if half == self.FIRST:
                    self.half_gathered[nm] = buf
                else:
                    weights[nm] = _full_weight(nm, buf)
        elif tag in self.SWAPS:
            self.swapped[self.SWAPS[tag]] = landed
        else:
            at = 0
            for batch in self.CHIP_EXCHANGES[tag]:
                count = len(self.parts[batch][0])
                self.summed[batch] = landed[at:at + count]
                at += count


def kernel(x, p, g_mix, w_in, conv_w, conv_b, w_rg, b_rg, w_ig, b_ig, lru_lambda, w_rnn_proj, q_gain, k_gain, sinks, w_attn_proj, w_out, g_mlp, w_up, w_down, g_ple, w_ple_gate, w_ple_proj, loss_target, m_g_mix, m_w_in, m_conv_w, m_conv_b, m_w_rg, m_b_rg, m_w_ig, m_b_ig, m_lru_lambda, m_w_rnn_proj, m_q_gain, m_k_gain, m_sinks, m_w_attn_proj, m_w_out, m_g_mlp, m_w_up, m_w_down, m_g_ple, m_w_ple_gate, m_w_ple_proj, v_g_mix, v_w_in, v_conv_w, v_conv_b, v_w_rg, v_b_rg, v_w_ig, v_b_ig, v_lru_lambda, v_w_rnn_proj, v_q_gain, v_k_gain, v_sinks, v_w_attn_proj, v_w_out, v_g_mlp, v_w_up, v_w_down, v_g_ple, v_w_ple_gate, v_w_ple_proj):
    names = ("g_mix", "w_in", "conv_w", "conv_b", "w_rg", "b_rg", "w_ig", "b_ig", "lru_lambda", "w_rnn_proj",
             "q_gain", "k_gain", "sinks", "w_attn_proj", "w_out", "g_mlp", "w_up", "w_down", "g_ple",
             "w_ple_gate", "w_ple_proj")
    wts = dict(zip(names, (g_mix, w_in, conv_w, conv_b, w_rg, b_rg, w_ig, b_ig, lru_lambda, w_rnn_proj, q_gain,
                           k_gain, sinks, w_attn_proj, w_out, g_mlp, w_up, w_down, g_ple, w_ple_gate, w_ple_proj)))
    mom1 = dict(zip(names, (m_g_mix, m_w_in, m_conv_w, m_conv_b, m_w_rg, m_b_rg, m_w_ig, m_b_ig, m_lru_lambda,
                            m_w_rnn_proj, m_q_gain, m_k_gain, m_sinks, m_w_attn_proj, m_w_out, m_g_mlp, m_w_up,
                            m_w_down, m_g_ple, m_w_ple_gate, m_w_ple_proj)))
    mom2 = dict(zip(names, (v_g_mix, v_w_in, v_conv_w, v_conv_b, v_w_rg, v_b_rg, v_w_ig, v_b_ig, v_lru_lambda,
                            v_w_rnn_proj, v_q_gain, v_k_gain, v_sinks, v_w_attn_proj, v_w_out, v_g_mlp, v_w_up,
                            v_w_down, v_g_ple, v_w_ple_gate, v_w_ple_proj)))
    n_seq, seq, _ = x.shape
    core = lax.axis_index("c").astype(jnp.int32).reshape(1)

    shards = {nm: wts[nm][0].astype(BF16) for nm in BIG}
    w_in_all, conv_all = _gather_two_level([shards["w_in"], conv_w[0]], name="gather_w_in")
    w = {nm: wts[nm] for nm in names if nm not in BIG}
    w["w_rg"], w["w_ig"] = w_rg[0], w_ig[0]
    w["conv_w"] = conv_all.transpose(1, 0, 2).reshape(CONV_W, D_MODEL)
    w["w_in"] = _full_weight("w_in", w_in_all)
    comm = _StepExchanges(shards, core)
    loss_sum, grad_x, g = _local_step(
        x.reshape(n_seq * seq, D_MODEL), p.reshape(n_seq * seq, PLE_DIM), loss_target.reshape(n_seq * seq, D_MODEL),
        w, n_seq=n_seq, seq=seq, comm=comm)
    del loss_sum

    res = {}
    for batch, batch_names in BATCHES.items():
        for nm, summed in zip(batch_names, comm.summed[batch]):
            res[nm] = _adamw(summed, wts[nm][0], mom1[nm][0], mom2[nm][0], name="adamw_" + nm)
    g_mix_parts, = _exchange([g["g_mix"]], ["gather"], name="gather_g_mix")
    res["g_mix"] = [r[0] for r in _adamw(g_mix_parts, g_mix, m_g_mix, v_g_mix, name="adamw_g_mix")]
    small_names = [nm for nm, _ in SMALL if nm != LOSS_ROW]
    full_small = {}
    for src, key in ((wts, "w"), (mom1, "m"), (mom2, "v")):
        vals = {nm: src[nm][0] for nm in small_names}
        vals[LOSS_ROW] = jnp.zeros((1,), F32)
        full_small[key] = _pack_small(vals)
    small_res = _adamw(comm.summed[SMALL_BATCH][0],full_small["w"], full_small["m"], full_small["v"], name="adamw_small")
    shapes = {nm: wts[nm].shape[1:] for nm in small_names}
    shapes[LOSS_ROW] = (D_MODEL,)
    small_out = [_unpack_small(r, shapes) for r in small_res]
    for nm in small_names:
        res[nm] = [so[nm] for so in small_out]
    loss = jnp.sum(small_out[0][LOSS_ROW]) * (0.5 / D_MODEL)

    outs = [loss, grad_x.reshape(n_seq, seq, D_MODEL)]
    for k in range(4):
        outs.extend(res[nm][k][None] for nm in names)
    return tuple(outs)
```

```python
import functools
from typing import Callable, NamedTuple

import jax
import jax.numpy as jnp
from jax import lax
from jax.experimental import pallas as pl
from jax.experimental.pallas import tpu as pltpu

F32 = jnp.float32
BF16 = jnp.bfloat16

N_DEV = 8
D_MODEL = 1024
RNN_BLOCK_W = 64
CONV_W = 4
LRU_C = 8.0
HEAD_DIM = 64
N_Q_HEADS = 16
N_KV_HEADS = 4
KV_W = N_KV_HEADS * HEAD_DIM
WINDOW = 128
ROPE_THETA = 10000.0
D_FF = 4096
PLE_DIM = 256
NORM_EPS = 1e-6
IN_TOTAL = 5632
COL_RNN_END, COL_ATTN_END = 2048, 3584
ATTN_W = COL_ATTN_END - COL_RNN_END
ATTN_K_AT, ATTN_V_AT = 1024, 1280

ADAM_LR = 0.001
ADAM_B1 = 0.9
ADAM_B2 = 0.999
ADAM_EPS = 1e-08
ADAM_WD = 0.01
ADAM_STEP = 10

LANES = 128
SUBLANES = 8
RNN_TILE = 256
VMEM_LIMIT = 48 * 1024 * 1024
NEG_BIG = -1e30


def _params(*sem):
    return pltpu.CompilerParams(dimension_semantics=sem if sem else None, vmem_limit_bytes=VMEM_LIMIT)


def _sig(x):
    return 0.5 * jnp.tanh(0.5 * x) + 0.5


def _dot_nt(a, b):
    return lax.dot_general(a, b, (((1,), (1,)), ((), ())), preferred_element_type=F32)


def _dot_tn(a, b):
    return lax.dot_general(a, b, (((0,), (0,)), ((), ())), preferred_element_type=F32)


class _Xfer:
    def __init__(self, start, wait):
        self.start, self.wait = start, wait


class _Hosted(NamedTuple):
    srcs: tuple
    out_shape: tuple
    n_sems: int
    plan: Callable
    aliases: tuple = ()


def _merge_hosted(parts):
    parts = [p for p in parts if p is not None]
    if len(parts) <= 1:
        return parts[0] if parts else None
    src_at, dst_at, sem_at, aliases = [0], [0], [0], []
    for p in parts:
        aliases += [(i + src_at[-1], j + dst_at[-1]) for i, j in p.aliases]
        src_at.append(src_at[-1] + len(p.srcs))
        dst_at.append(dst_at[-1] + len(p.out_shape))
        sem_at.append(sem_at[-1] + p.n_sems)

    def plan(src, dst, send_sems, recv_sems, local_sems, first_sem):
        copies = []
        for k, p in enumerate(parts):
            copies += p.plan(src[src_at[k]:src_at[k + 1]], dst[dst_at[k]:dst_at[k + 1]], send_sems, recv_sems,
                             local_sems, first_sem + sem_at[k])
        return copies

    return _Hosted(tuple(a for p in parts for a in p.srcs), tuple(s for p in parts for s in p.out_shape),
                   sem_at[-1], plan, tuple(aliases))


def _call(body, *, grid, in_specs, out_specs, out_shape, args, name, semantics, scratch_shapes=(), hosted=None):
    if hosted is None:
        outs = pl.pallas_call(body, grid=grid, in_specs=list(in_specs), out_specs=list(out_specs),
                              out_shape=list(out_shape), scratch_shapes=list(scratch_shapes),
                              compiler_params=_params(*semantics), name=name)(*args)
        return list(outs), []
    counts = (len(in_specs), len(hosted.srcs), len(out_specs), len(hosted.out_shape), len(scratch_shapes), 3)

    def wrapped(*refs):
        at, groups = 0, []
        for count in counts:
            groups.append(refs[at:at + count])
            at += count
        ins, srcs, outs, dsts, scratch, sems = groups
        copies = hosted.plan(srcs, dsts, *sems, 0)
        ids = [pl.program_id(axis) for axis in range(len(grid))]
        first = functools.reduce(jnp.logical_and, [i == 0 for i in ids])
        last = functools.reduce(jnp.logical_and, [i == g - 1 for i, g in zip(ids, grid)])

        @pl.when(first)
        def _():
            for cp in copies:
                cp.start()

        body(*ins, *outs, *scratch)

        @pl.when(last)
        def _():
            for cp in copies:
                cp.wait()

    any_spec = pl.BlockSpec(memory_space=pl.ANY)
    sems = [pltpu.SemaphoreType.DMA((hosted.n_sems,))] * 3
    outs = pl.pallas_call(
        wrapped, grid=grid, in_specs=list(in_specs) + [any_spec] * counts[1],
        out_specs=list(out_specs) + [any_spec] * counts[3], out_shape=list(out_shape) + list(hosted.out_shape),
        scratch_shapes=list(scratch_shapes) + sems, compiler_params=_params(*["arbitrary"] * len(grid)),
        input_output_aliases={counts[0] + i: counts[2] + j for i, j in hosted.aliases},
        name=name)(*args, *hosted.srcs)
    return list(outs[:counts[2]]), list(outs[counts[2]:])


def _dividing_tile(n, want):
    tile = min(want, n)
    while n % tile:
        tile -= LANES
    return tile


def _matmul(a, b, *, mode, tm, tn, out_dtypes, name, epilogue=None, extras=(), hosted=None, b_cols=None,
            row_vecs=(), n_row_sums=0, extra_col_blocks=None):
    a_parts = tuple(a) if isinstance(a, (tuple, list)) else (a,)
    b_parts = tuple(b) if isinstance(b, (tuple, list)) else (b,)
    assert len(a_parts) == len(b_parts) and (mode == "nt" or len(a_parts) == 1)
    n_parts = len(a_parts)
    m = a_parts[0].shape[0]
    if b_cols is None:
        b_cols = [(0, bp.shape[1]) for bp in b_parts]
    n = b_cols[0][1] if mode == "nn" else b_parts[0].shape[0]
    tm, tn = min(tm, m), _dividing_tile(n, tn)
    n_extra = len(extras) + len(row_vecs)
    n_tiles_out = len(out_dtypes)
    assert n_row_sums == 0 or n == tn

    def body(*refs):
        a_refs, b_refs = refs[:n_parts], refs[n_parts:2 * n_parts]
        rest = refs[2 * n_parts:]
        extra_refs, out_refs = rest[:n_extra], rest[n_extra:]
        if mode == "nn":
            acc = jnp.dot(a_refs[0][...], b_refs[0][...], preferred_element_type=F32)
        else:
            acc = _dot_nt(a_refs[0][...], b_refs[0][...])
            for a_ref, b_ref in zip(a_refs[1:], b_refs[1:]):
                acc = acc + _dot_nt(a_ref[...], b_ref[...])
        res = epilogue(acc, *[e[...] for e in extra_refs]) if epilogue is not None else (acc,)
        for o_ref, r in zip(out_refs[:n_tiles_out], res):
            o_ref[...] = r.astype(o_ref.dtype)
        if n_row_sums:
            @pl.when(pl.program_id(0) == 0)
            def _():
                for o_ref in out_refs[n_tiles_out:]:
                    o_ref[...] = jnp.zeros_like(o_ref)

            for o_ref, r in zip(out_refs[n_tiles_out:], res[n_tiles_out:]):
                o_ref[...] += r

    a_specs = [pl.BlockSpec((tm, ap.shape[1]), lambda i, j: (i, 0)) for ap in a_parts]
    if mode == "nn":
        assert b_cols[0][0] % tn == 0
        first = b_cols[0][0] // tn
        b_specs = [pl.BlockSpec((b_parts[0].shape[0], tn), lambda i, j: (0, first + j))]
    else:
        assert all(at % width == 0 for at, width in b_cols)
        b_specs = [pl.BlockSpec((tn, width), functools.partial(lambda i, j, blk: (j, blk), blk=at // width))
                   for at, width in b_cols]
    tile = pl.BlockSpec((tm, tn), lambda i, j: (i, j))
    row = pl.BlockSpec((1, tn), lambda i, j: (0, j))
    extra_specs = [pl.BlockSpec((tm, tn), functools.partial(lambda i, j, first: (i, first + j), first=first))
                   for first in (extra_col_blocks or [0] * len(extras))]
    outs, landed = _call(
        body,
        grid=(m // tm, n // tn),
        in_specs=a_specs + b_specs + extra_specs + [row] * len(row_vecs),
        out_specs=[tile] * n_tiles_out + [row] * n_row_sums,
        out_shape=[jax.ShapeDtypeStruct((m, n), dt) for dt in out_dtypes]
        + [jax.ShapeDtypeStruct((1, n), F32)] * n_row_sums,
        args=(*a_parts, *b_parts, *extras, *row_vecs), name=name,
        semantics=("arbitrary" if n_row_sums else "parallel", "arbitrary"), hosted=hosted)
    if hosted is not None:
        return (*outs, landed)
    return outs[0] if len(outs) == 1 else outs


def _matmul_tn(a, b, *, tk, tn, tt, name, slot_cols=None):
    t, k = a.shape
    n = b.shape[1]
    tk, tn, tt = min(tk, k), _dividing_tile(n, tn), min(tt, t)

    def body(a_ref, b_ref, o_ref):
        @pl.when(pl.program_id(2) == 0)
        def _():
            o_ref[...] = jnp.zeros_like(o_ref)

        if slot_cols is None:
            o_ref[...] += _dot_tn(a_ref[...], b_ref[...])
        else:
            av = a_ref[...]
            for s in range(tn // slot_cols):
                o_ref[s] += _dot_tn(av, b_ref[:, s * slot_cols:(s + 1) * slot_cols])

    if slot_cols is not None:
        out_spec = pl.BlockSpec((tn // slot_cols, tk, slot_cols), lambda i, j, s: (j, i, 0))
        out_shape = jax.ShapeDtypeStruct((n // slot_cols, k, slot_cols), F32)
    else:
        out_spec = pl.BlockSpec((tk, tn), lambda i, j, s: (i, j))
        out_shape = jax.ShapeDtypeStruct((k, n), F32)
    return pl.pallas_call(
        body,
        grid=(k // tk, n // tn, t // tt),
        in_specs=[pl.BlockSpec((tt, tk), lambda i, j, s: (s, i)), pl.BlockSpec((tt, tn), lambda i, j, s: (s, j))],
        out_specs=out_spec,
        out_shape=out_shape,
        compiler_params=_params("parallel", "parallel", "arbitrary"),
        name=name,
    )(a, b)


def _matmul_tn_multi(a, bs, *, tt, name, hosted=None):
    t, k = a.shape
    tt = min(tt, t)
    n_b = len(bs)

    def body(a_ref, *refs):
        b_refs, o_refs = refs[:n_b], refs[n_b:]

        @pl.when(pl.program_id(0) == 0)
        def _():
            for o_ref in o_refs:
                o_ref[...] = jnp.zeros_like(o_ref)

        a_t = a_ref[...].T
        for b_ref, o_ref in zip(b_refs, o_refs):
            o_ref[...] += jnp.dot(a_t, b_ref[...], preferred_element_type=F32)

    outs, landed = _call(
        body,
        grid=(t // tt,),
        in_specs=[pl.BlockSpec((tt, k), lambda s: (s, 0))] + [pl.BlockSpec((tt, b.shape[1]), lambda s: (s, 0)) for b in bs],
        out_specs=[pl.BlockSpec((k, b.shape[1]), lambda s: (0, 0)) for b in bs],
        out_shape=[jax.ShapeDtypeStruct((k, b.shape[1]), F32) for b in bs],
        args=(a, *bs), name=name, semantics=("arbitrary",), hosted=hosted)
    return (*outs, landed) if hosted is not None else outs


def _rmsnorm_rows(x, g):
    return x * lax.rsqrt(jnp.mean(x * x, axis=-1, keepdims=True) + NORM_EPS) * g


def _norm_matmul(x, g, b, *, tm, tn, name, hosted=None):
    m, k = x.shape
    n = b.shape[1]
    tm, tn = min(tm, m), _dividing_tile(n, tn)

    def body(x_ref, g_ref, b_ref, z_ref, h_ref, h_s):
        @pl.when(pl.program_id(1) == 0)
        def _():
            h_s[...] = _rmsnorm_rows(x_ref[...], g_ref[...]).astype(BF16)
            h_ref[...] = h_s[...]

        z_ref[...] = jnp.dot(h_s[...], b_ref[...], preferred_element_type=F32)

    rows = pl.BlockSpec((tm, k), lambda i, j: (i, 0))
    outs, landed = _call(
        body,
        grid=(m // tm, n // tn),
        in_specs=[rows, pl.BlockSpec((1, k), lambda i, j: (0, 0)), pl.BlockSpec((k, tn), lambda i, j: (0, j))],
        out_specs=[pl.BlockSpec((tm, tn), lambda i, j: (i, j)), rows],
        out_shape=[jax.ShapeDtypeStruct((m, n), F32), jax.ShapeDtypeStruct((m, k), BF16)],
        scratch_shapes=[pltpu.VMEM((tm, k), BF16)],
        args=(x, g, b), name=name, semantics=("parallel", "arbitrary"), hosted=hosted)
    return (*outs, landed) if hosted is not None else outs


def _rmsnorm_bwd_rows(dy, x, dres, g):
    r = lax.rsqrt(jnp.mean(x * x, axis=-1, keepdims=True) + NORM_EPS)
    xr = x * r
    gy = dy * g
    dx = dres + r * (gy - xr * jnp.mean(gy * xr, axis=-1, keepdims=True))
    return dx, jnp.sum(dy * xr, axis=0, keepdims=True)


def _softplus_neg(lam):
    z = -lam
    return jnp.maximum(z, 0.0) + jnp.log1p(jnp.exp(-jnp.abs(z)))


def _neg_expm1(y, exp_half_y):
    series = -y * (1.0 + y * 0.5 * (1.0 + y * (1.0 / 3.0) * (1.0 + y * 0.25 * (1.0 + y * 0.2))))
    return jnp.where(y > -0.0625, series, 1.0 - exp_half_y * exp_half_y)


def _gelu_parts(x):
    c = 0.7978845608028654
    u = c * (x + 0.044715 * x * x * x)
    th = jnp.tanh(u)
    gel = 0.5 * x * (1.0 + th)
    dgel = 0.5 * (1.0 + th) + 0.5 * x * (1.0 - th * th) * c * (1.0 + 3.0 * 0.044715 * x * x)
    return gel, dgel


def _shift_down(v, k, rows):
    return jnp.where(rows < k, 0.0, pltpu.roll(v, k, 0))


def _shift_up(v, k, rows, n):
    return jnp.where(rows >= n - k, 0.0, pltpu.roll(v, n - k, 0))


def _scan_within_groups(a, b, *, reverse):
    shape = a.shape
    a = a.reshape(shape[0] // SUBLANES, SUBLANES, shape[1])
    b = b.reshape(a.shape)
    in_group = lax.broadcasted_iota(jnp.int32, a.shape, 1)
    for s in (1, 2, 4):
        if reverse:
            inside, shift = in_group < SUBLANES - s, SUBLANES - s
        else:
            inside, shift = in_group >= s, s
        b = b + a * jnp.where(inside, pltpu.roll(b, shift, 1), 0.0)
        a = a * jnp.where(inside, pltpu.roll(a, shift, 1), 1.0)
    return a.reshape(shape), b.reshape(shape)


def _rnn_gates(xc, wrg, brg, wig, big, lam):
    xcb = xc.astype(BF16)
    r = _sig(jnp.dot(xcb, wrg, preferred_element_type=F32) + brg)
    i = _sig(jnp.dot(xcb, wig, preferred_element_type=F32) + big)
    sp = _softplus_neg(lam)
    log_a = -LRU_C * r * sp
    a = jnp.exp(log_a)
    mult = jnp.sqrt(_neg_expm1(2.0 * log_a, a))
    return xcb, r, i, sp, a, mult


def _conv_fwd(xv, cw, cb, rows):
    return (cb + _shift_down(xv, 3, rows) * cw[0:1, :] + _shift_down(xv, 2, rows) * cw[1:2, :]
            + _shift_down(xv, 1, rows) * cw[2:3, :] + xv * cw[3:4, :])


def _rnn_fwd(z, conv_w, conv_b, wrg_bd, b_rg, wig_bd, b_ig, lam, *, n_seq, seq, hosted=None):
    t = n_seq * seq
    ct = RNN_TILE
    n_ct = D_MODEL // ct

    def body(x_ref, g_ref, cw_ref, cb_ref, wrg_ref, brg_ref, wig_ref, big_ref, lam_ref,
             xc_ref, hr_ref, ya_ref, a_s, b_s):
        rows = lax.broadcasted_iota(jnp.int32, (seq, ct), 0)
        xc = _conv_fwd(x_ref[...], cw_ref[...], cb_ref[...], rows)
        _, r, i, sp, a, mult = _rnn_gates(xc, wrg_ref[...], brg_ref[...], wig_ref[...], big_ref[...], lam_ref[...])
        a_s[...], b_s[...] = _scan_within_groups(a, mult * (i * xc), reverse=False)

        def step(j, carry):
            r0 = pl.multiple_of(j * SUBLANES, SUBLANES)
            h = b_s[pl.ds(r0, SUBLANES), :] + a_s[pl.ds(r0, SUBLANES), :] * carry
            hr_ref[pl.ds(r0, SUBLANES), :] = h
            return h[SUBLANES - 1:SUBLANES, :]

        lax.fori_loop(0, seq // SUBLANES, step, jnp.zeros((1, ct), F32), unroll=4)
        gel, _ = _gelu_parts(g_ref[...])
        xc_ref[...] = xc
        ya_ref[...] = (hr_ref[...] * gel).astype(BF16)

    vec = pl.BlockSpec((1, ct), lambda b, c: (0, c))
    gate_w = pl.BlockSpec((None, ct, ct), lambda b, c: (c, 0, 0))
    tile = pl.BlockSpec((seq, ct), lambda b, c: (b, c))
    outs, landed = _call(
        body,
        grid=(n_seq, n_ct),
        in_specs=[
            pl.BlockSpec((seq, ct), lambda b, c: (b, c)),
            pl.BlockSpec((seq, ct), lambda b, c: (b, n_ct + c)),
            pl.BlockSpec((CONV_W, ct), lambda b, c: (0, c)), vec, gate_w, vec, gate_w, vec, vec,
        ],
        out_specs=[tile, tile, tile],
        out_shape=[jax.ShapeDtypeStruct((t, D_MODEL), F32), jax.ShapeDtypeStruct((t, D_MODEL), F32),
                   jax.ShapeDtypeStruct((t, D_MODEL), BF16)],
        scratch_shapes=[pltpu.VMEM((seq, ct), F32), pltpu.VMEM((seq, ct), F32)],
        args=(z, z, conv_w, conv_b, wrg_bd, b_rg, wig_bd, b_ig, lam), name="rnn_fwd",
        semantics=("parallel", "parallel"), hosted=hosted)
    return (*outs, landed) if hosted is not None else outs


def _rnn_bwd(dya, z, xc, hr, conv_w, wrg_bd, b_rg, wig_bd, b_ig, lam, *, n_seq, seq, hosted=None):
    t = n_seq * seq
    ct = RNN_TILE
    n_ct = D_MODEL // ct

    def body(dya_ref, x_ref, g_ref, xc_ref, hr_ref, cw_ref, wrg_ref, brg_ref, wig_ref, big_ref, lam_ref,
             dx_ref, dg_ref, dwrg_ref, dwig_ref, vec_ref, a_s, d_s, g_s):
        rows = lax.broadcasted_iota(jnp.int32, (seq, ct), 0)
        xv, xc, hr, dyv = x_ref[...], xc_ref[...], hr_ref[...], dya_ref[...]
        lamv = lam_ref[...]
        gel, dgel = _gelu_parts(g_ref[...])
        dg_ref[...] = (dyv * hr * dgel).astype(BF16)
        xcb, r, i, sp, a, mult = _rnn_gates(xc, wrg_ref[...], brg_ref[...], wig_ref[...], big_ref[...], lamv)
        a_s[...], d_s[...] = _scan_within_groups(_shift_up(a, 1, rows, seq), dyv * gel, reverse=True)

        def step(k, carry):
            r0 = pl.multiple_of((seq // SUBLANES - 1 - k) * SUBLANES, SUBLANES)
            gs = d_s[pl.ds(r0, SUBLANES), :] + a_s[pl.ds(r0, SUBLANES), :] * carry
            g_s[pl.ds(r0, SUBLANES), :] = gs
            return gs[0:1, :]

        lax.fori_loop(0, seq // SUBLANES, step, jnp.zeros((1, ct), F32), unroll=4)
        gsum = g_s[...]
        gated = i * xc
        d_log_a = gsum * _shift_down(hr, 1, rows) * a - gsum * gated * (a * a / mult)
        d_gated = gsum * mult
        d_pre_r = (d_log_a * (-LRU_C) * sp) * r * (1.0 - r)
        d_pre_i = (d_gated * xc) * i * (1.0 - i)
        dprb, dpib = d_pre_r.astype(BF16), d_pre_i.astype(BF16)
        dxc = d_gated * i + _dot_nt(dprb, wrg_ref[...]) + _dot_nt(dpib, wig_ref[...])
        cw = cw_ref[...]
        dx = (dxc * cw[3:4, :] + _shift_up(dxc, 1, rows, seq) * cw[2:3, :]
              + _shift_up(dxc, 2, rows, seq) * cw[1:2, :] + _shift_up(dxc, 3, rows, seq) * cw[0:1, :])
        dx_ref[...] = dx.astype(BF16)

        @pl.when(pl.program_id(1) == 0)
        def _():
            dwrg_ref[...] = jnp.zeros_like(dwrg_ref)
            dwig_ref[...] = jnp.zeros_like(dwig_ref)
            vec_ref[...] = jnp.zeros_like(vec_ref)

        dwrg_ref[...] += _dot_tn(xcb, dprb)
        dwig_ref[...] += _dot_tn(xcb, dpib)

        def colsum(v):
            return jnp.sum(v, axis=0, keepdims=True)

        d_sp = colsum(d_log_a * (-LRU_C) * r)
        vec_ref[0:1, :] += colsum(d_pre_r)
        vec_ref[1:2, :] += colsum(d_pre_i)
        vec_ref[2:3, :] += d_sp * (-_sig(-lamv))
        vec_ref[3:4, :] += colsum(dxc)
        vec_ref[4:5, :] += colsum(dxc * _shift_down(xv, 3, rows))
        vec_ref[5:6, :] += colsum(dxc * _shift_down(xv, 2, rows))
        vec_ref[6:7, :] += colsum(dxc * _shift_down(xv, 1, rows))
        vec_ref[7:8, :] += colsum(dxc * xv)

    vec = pl.BlockSpec((1, ct), lambda c, b: (0, c))
    gate_w = pl.BlockSpec((None, ct, ct), lambda c, b: (c, 0, 0))
    tile = pl.BlockSpec((seq, ct), lambda c, b: (b, c))
    outs, landed = _call(
        body,
        grid=(n_ct, n_seq),
        in_specs=[
            tile,
            pl.BlockSpec((seq, ct), lambda c, b: (b, c)),
            pl.BlockSpec((seq, ct), lambda c, b: (b, n_ct + c)),
            tile, tile,
            pl.BlockSpec((CONV_W, ct), lambda c, b: (0, c)), gate_w, vec, gate_w, vec, vec,
        ],
        out_specs=[tile, tile, gate_w, gate_w, pl.BlockSpec((8, ct), lambda c, b: (0, c))],
        out_shape=[jax.ShapeDtypeStruct((t, D_MODEL), BF16), jax.ShapeDtypeStruct((t, D_MODEL), BF16),
                   jax.ShapeDtypeStruct((n_ct, ct, ct), F32), jax.ShapeDtypeStruct((n_ct, ct, ct), F32),
                   jax.ShapeDtypeStruct((8, D_MODEL), F32)],
        scratch_shapes=[pltpu.VMEM((seq, ct), F32)] * 3,
        args=(dya, z, z, xc, hr, conv_w, wrg_bd, b_rg, wig_bd, b_ig, lam), name="rnn_bwd",
        semantics=("parallel", "arbitrary"), hosted=hosted)
    return (*outs, landed) if hosted is not None else outs


def _split_hi_lo(x):
    hi = x.astype(BF16)
    return hi, (x - hi.astype(F32)).astype(BF16)


def _dot_split(x, m_twice):
    hi, lo = _split_hi_lo(x)
    return jnp.dot(jnp.concatenate([hi, lo], axis=1), m_twice, preferred_element_type=F32)


def _head_matrices(width):
    ec = ((lax.broadcasted_iota(jnp.int32, (2 * width, LANES), 0) & (width - 1)) // HEAD_DIM
          == lax.broadcasted_iota(jnp.int32, (2 * width, LANES), 1))
    ee = (lax.broadcasted_iota(jnp.int32, (2 * LANES, width), 1) // HEAD_DIM
          == (lax.broadcasted_iota(jnp.int32, (2 * LANES, width), 0) & (LANES - 1)))
    return jnp.where(ec, 1.0, 0.0).astype(BF16), jnp.where(ee, 1.0, 0.0).astype(BF16)


def _swap_halves(y):
    w = y.shape[1]
    first = (lax.broadcasted_iota(jnp.int32, y.shape, 1) % HEAD_DIM) < HEAD_DIM // 2
    return jnp.where(first, pltpu.roll(y, w - HEAD_DIM // 2, 1), pltpu.roll(y, HEAD_DIM // 2, 1))


def _normrope_fwd(x, gain, cos_t, sin_t, ec, ee):
    w = x.shape[1]
    rs = _dot_split(lax.rsqrt(_dot_split(x * x, ec) * (1.0 / HEAD_DIM) + NORM_EPS), ee)
    nx = x * rs
    y = nx * gain
    reps = w // LANES
    out = y * jnp.tile(cos_t, (1, reps)) + _swap_halves(y) * jnp.tile(sin_t, (1, reps))
    return out, nx, rs


def _normrope_bwd(dout, nx, rs, gain, cos_t, sin_t, ec, ee):
    w = dout.shape[1]
    reps = w // LANES
    dy = dout * jnp.tile(cos_t, (1, reps)) + _swap_halves(dout * jnp.tile(sin_t, (1, reps)))
    dgain = jnp.sum(dy * nx, axis=0, keepdims=True)
    dn = dy * gain
    seg = _dot_split(_dot_split(dn * nx, ec) * (1.0 / HEAD_DIM), ee)
    return rs * (dn - nx * seg), dgain


def _pair_operand(t, group):
    chunk = t[:, (group // 2) * LANES:(group // 2 + 1) * LANES]
    low = lax.broadcasted_iota(jnp.int32, chunk.shape, 1) < HEAD_DIM
    rolled = pltpu.roll(chunk, HEAD_DIM, 1)
    return jnp.where(low, chunk, rolled) if group % 2 == 0 else jnp.where(low, rolled, chunk)


GROUP = N_Q_HEADS // N_KV_HEADS
GROUP_W = GROUP * HEAD_DIM


def _replicate_head(t, group):
    return jnp.tile(_pair_operand(t, group), (1, 2))


def _head_blocks(t):
    seg = lax.broadcasted_iota(jnp.int32, t.shape, 1) // HEAD_DIM
    return jnp.concatenate([jnp.where(seg == h, t, 0.0) for h in range(GROUP)], axis=0)


def _stack_heads(t_t, rows):
    return jnp.concatenate([t_t[:, h * rows:(h + 1) * rows] for h in range(GROUP)], axis=0)


def _head_rows(mat_t, group):
    return jnp.concatenate([mat_t[GROUP * group + h:GROUP * group + h + 1, :] for h in range(GROUP)], axis=1)


def _window_masks(blk):
    key = lax.broadcasted_iota(jnp.int32, (blk, GROUP * blk), 0)
    query = lax.broadcasted_iota(jnp.int32, (blk, GROUP * blk), 1) & (blk - 1)
    return key > query, key <= query


def _mask_window(t, before_ok, own_ok, fill):
    blk = t.shape[0] // 2
    return jnp.concatenate([jnp.where(before_ok, t[:blk], fill), jnp.where(own_ok, t[blk:], fill)], axis=0)


def _attn_fwd(z, cos_t, sin_t, q_gain_t, k_gain_t, sinks_t, *, n_seq, seq, hosted=None):
    t = n_seq * seq
    blk = WINDOW
    nb = seq // blk

    def body(q_ref, kc_ref, vp_ref, vc_ref, cosc_ref, sinc_ref, qg_ref, kg_ref, sk_ref, o_ref, l_ref, k_s):
        n = pl.program_id(1)
        ecq, eeq = _head_matrices(D_MODEL)
        eck, eek = _head_matrices(KV_W)
        cosc, sinc = cosc_ref[...], sinc_ref[...]
        qh, _, _ = _normrope_fwd(q_ref[...], qg_ref[...], cosc, sinc, ecq, eeq)
        qh = qh * (HEAD_DIM ** -0.5)
        own, other = n & 1, 1 - (n & 1)

        @pl.when(n == 0)
        def _():
            k_s[other] = jnp.zeros((blk, KV_W), F32)

        kc, _, _ = _normrope_fwd(kc_ref[...], kg_ref[...], cosc, sinc, eck, eek)
        kp = k_s[other]
        k_s[own] = kc
        kcat = jnp.concatenate([kp, kc], axis=0)
        vcat = jnp.concatenate([vp_ref[...], vc_ref[...]], axis=0)
        above, causal = _window_masks(blk)
        above = above & (n > 0)
        head_row = lax.broadcasted_iota(jnp.int32, (blk, blk), 0)
        sk_t = jnp.broadcast_to(sk_ref[...], (blk, LANES)).T
        vcat_t = vcat.T.astype(BF16)
        lmat = jnp.zeros((blk, blk), F32)
        groups = range(N_KV_HEADS)
        cols = [slice(g * GROUP_W, (g + 1) * GROUP_W) for g in groups]
        scores = [_dot_nt(_replicate_head(kcat, g).astype(BF16), _head_blocks(qh[:, cols[g]]).astype(BF16))
                  for g in groups]
        probs = []
        for g in groups:
            s = _mask_window(scores[g], above, causal, NEG_BIG)
            sink = _head_rows(sk_t, g)
            m = jnp.maximum(jnp.max(s, axis=0, keepdims=True), sink)
            e = jnp.exp(s - m)
            den = jnp.sum(e, axis=0, keepdims=True) + jnp.exp(sink - m)
            probs.append((e * (1.0 / den)).astype(BF16))
            lse = m + jnp.log(den)
            for h in range(GROUP):
                lmat = lmat + jnp.where(head_row == GROUP * g + h, lse[:, h * blk:(h + 1) * blk], 0.0)
        for g in groups:
            out_t = jnp.dot(vcat_t[g * HEAD_DIM:(g + 1) * HEAD_DIM], probs[g], preferred_element_type=F32)
            o_ref[:, cols[g]] = _stack_heads(out_t, blk).T.astype(BF16)
        l_ref[...] = lmat

    def row(b, n):
        return b * nb + n

    def prev(b, n):
        return b * nb + jnp.maximum(n - 1, 0)

    kw = KV_W
    tab_c = pl.BlockSpec((blk, LANES), lambda b, n: (n, 0))
    outs, landed = _call(
        body,
        grid=(n_seq, nb),
        in_specs=[
            pl.BlockSpec((blk, D_MODEL), lambda b, n: (row(b, n), COL_RNN_END // D_MODEL)),
            pl.BlockSpec((blk, kw), lambda b, n: (row(b, n), (COL_RNN_END + ATTN_K_AT) // kw)),
            pl.BlockSpec((blk, kw), lambda b, n: (prev(b, n), (COL_RNN_END + ATTN_V_AT) // kw)),
            pl.BlockSpec((blk, kw), lambda b, n: (row(b, n), (COL_RNN_END + ATTN_V_AT) // kw)),
            tab_c, tab_c,
            pl.BlockSpec((1, D_MODEL), lambda b, n: (0, 0)),
            pl.BlockSpec((1, kw), lambda b, n: (0, 0)),
            pl.BlockSpec((1, LANES), lambda b, n: (0, 0)),
        ],
        out_specs=[pl.BlockSpec((blk, D_MODEL), lambda b, n: (row(b, n), 0)),
                   pl.BlockSpec((blk, LANES), lambda b, n: (row(b, n), 0))],
        out_shape=[jax.ShapeDtypeStruct((t, D_MODEL), BF16), jax.ShapeDtypeStruct((t, LANES), F32)],
        scratch_shapes=[pltpu.VMEM((2, blk, kw), F32)],
        args=(z, z, z, z, cos_t, sin_t, q_gain_t, k_gain_t, sinks_t), name="attn_fwd",
        semantics=("arbitrary", "arbitrary"), hosted=hosted)
    return (*outs, landed) if hosted is not None else outs


def _attn_bwd(z, o, lse, do, cos_t, sin_t, q_gain_t, k_gain_t, sinks_t, *, n_seq, seq, hosted=None):
    t = n_seq * seq
    blk = WINDOW
    nb = seq // blk
    kw = KV_W
    scale = HEAD_DIM ** -0.5

    def body(qc_ref, qn_ref, kc_ref, vp_ref, vc_ref, oc_ref, on_ref, doc_ref, don_ref, lc_ref, ln_ref,
             cosc_ref, sinc_ref, cosn_ref, sinn_ref, qg_ref, kg_ref, sk_ref,
             dz_ref, vec_ref, dq_s, q_s, k_s):
        n = pl.program_id(1)
        ecq, eeq = _head_matrices(D_MODEL)
        eck, eek = _head_matrices(KV_W)
        cosc, sinc = cosc_ref[...], sinc_ref[...]
        qg, kg = qg_ref[...], kg_ref[...]
        own, other = n & 1, 1 - (n & 1)

        @pl.when(n == 0)
        def _():
            for part, value in enumerate(_normrope_fwd(qc_ref[...], qg, cosc, sinc, ecq, eeq)):
                q_s[own, part] = value
            k_s[other] = jnp.zeros((blk, kw), F32)

        for part, value in enumerate(_normrope_fwd(qn_ref[...], qg, cosn_ref[...], sinn_ref[...], ecq, eeq)):
            q_s[other, part] = value
        qhc, nqc, rsqc = q_s[own, 0], q_s[own, 1], q_s[own, 2]
        qhn = q_s[other, 0]
        khc, nkc, rskc = _normrope_fwd(kc_ref[...], kg, cosc, sinc, eck, eek)
        khp = k_s[other]
        k_s[own] = khc
        doc = doc_ref[...].astype(F32)
        don = don_ref[...].astype(F32)
        delc = _dot_split(doc * oc_ref[...].astype(F32), ecq)
        deln = _dot_split(don * on_ref[...].astype(F32), ecq)
        lc_t, ln_t, delc_t, deln_t = lc_ref[...], ln_ref[...], delc.T, deln.T
        above, causal = _window_masks(blk)
        above_c, above_n = above & (n > 0), above & (n < nb - 1)
        seg = lax.broadcasted_iota(jnp.int32, (blk, GROUP_W), 1) // HEAD_DIM
        lane = lax.broadcasted_iota(jnp.int32, (1, LANES), 1)
        sk_t = jnp.broadcast_to(sk_ref[...], (blk, LANES)).T
        dsink = jnp.zeros((1, LANES), F32)
        kcat = jnp.concatenate([khp, khc], axis=0)
        vcat = jnp.concatenate([vp_ref[...], vc_ref[...]], axis=0)
        kcat_t = kcat.T.astype(BF16)
        dkh = jnp.zeros((blk, GROUP_W), F32)
        dvh = jnp.zeros((blk, GROUP_W), F32)

        def fold_to(group, t):
            total = t + pltpu.roll(t, HEAD_DIM, 1)
            total = total + pltpu.roll(total, 2 * HEAD_DIM, 1)
            return jnp.where(seg == group, total, 0.0)

        groups = range(N_KV_HEADS)
        cols = [slice(g * GROUP_W, (g + 1) * GROUP_W) for g in groups]
        qsc, qsn = qhc * scale, qhn * scale
        qb_c = [_head_blocks(qsc[:, cols[g]]).astype(BF16) for g in groups]
        qb_n = [_head_blocks(qsn[:, cols[g]]).astype(BF16) for g in groups]
        dob_c = [_head_blocks(doc[:, cols[g]]).astype(BF16) for g in groups]
        dob_n = [_head_blocks(don[:, cols[g]]).astype(BF16) for g in groups]
        raw = []
        for g in groups:
            krep = _replicate_head(kcat, g).astype(BF16)
            vrep = _replicate_head(vcat, g).astype(BF16)
            raw.append((_dot_nt(krep, qb_c[g]), _dot_nt(vrep, dob_c[g]),
                        _dot_nt(krep[blk:], qb_n[g]), _dot_nt(vrep[blk:], dob_n[g])))
        cooked = []
        for g in groups:
            s_c, dp_c, s_n, dp_n = raw[g]
            l_row, d_row = _head_rows(lc_t, g), _head_rows(delc_t, g)
            p_c = _mask_window(jnp.exp(s_c - l_row), above_c, causal, 0.0)
            ds_c = (p_c * (dp_c - d_row)).astype(BF16)
            p_n = jnp.where(above_n, jnp.exp(s_n - _head_rows(ln_t, g)), 0.0)
            ds_n = (p_n * (dp_n - _head_rows(deln_t, g))).astype(BF16)
            cooked.append((p_c[blk:].astype(BF16), ds_c, p_n.astype(BF16), ds_n))
            p_sink = jnp.exp(_head_rows(sk_t, g) - l_row) * d_row
            for h in range(GROUP):
                dsink = dsink + jnp.where(lane == GROUP * g + h,
                                          -jnp.sum(p_sink[:, h * blk:(h + 1) * blk], axis=1, keepdims=True), 0.0)
        for g in groups:
            p_cb, ds_c, p_nb, ds_n = cooked[g]
            dq_t = jnp.dot(kcat_t[g * HEAD_DIM:(g + 1) * HEAD_DIM], ds_c, preferred_element_type=F32)
            dq_s[:, cols[g]] = _stack_heads(dq_t, blk).T * scale
            dk_rep = (jnp.dot(ds_c[blk:], qb_c[g], preferred_element_type=F32)
                      + jnp.dot(ds_n, qb_n[g], preferred_element_type=F32))
            dv_rep = (jnp.dot(p_cb, dob_c[g], preferred_element_type=F32)
                      + jnp.dot(p_nb, dob_n[g], preferred_element_type=F32))
            dkh = dkh + fold_to(g, dk_rep)
            dvh = dvh + fold_to(g, dv_rep)
        dq, dqg = _normrope_bwd(dq_s[...], nqc, rsqc, qg, cosc, sinc, ecq, eeq)
        dk, dkg = _normrope_bwd(dkh, nkc, rskc, kg, cosc, sinc, eck, eek)
        dz_ref[:, :ATTN_K_AT] = dq.astype(BF16)
        dz_ref[:, ATTN_K_AT:ATTN_V_AT] = dk.astype(BF16)
        dz_ref[:, ATTN_V_AT:] = dvh.astype(BF16)

        @pl.when(n == 0)
        def _():
            vec_ref[...] = jnp.zeros_like(vec_ref)

        vec_ref[0:1, :] += dqg
        vec_ref[1:2, 0:kw] += dkg
        vec_ref[2:3, 0:LANES] += dsink

    def row(b, n):
        return b * nb + n

    def prev(b, n):
        return b * nb + jnp.maximum(n - 1, 0)

    def nxt(b, n):
        return b * nb + jnp.minimum(n + 1, nb - 1)

    def tiles(width, col, which):
        return pl.BlockSpec((blk, width), lambda b, n: (which(b, n), col))

    def table(which):
        return pl.BlockSpec((blk, LANES), lambda b, n: (which(0, n), 0))

    outs, landed = _call(
        body,
        grid=(n_seq, nb),
        in_specs=[
            tiles(D_MODEL, COL_RNN_END // D_MODEL, row), tiles(D_MODEL, COL_RNN_END // D_MODEL, nxt),
            tiles(kw, (COL_RNN_END + ATTN_K_AT) // kw, row),
            tiles(kw, (COL_RNN_END + ATTN_V_AT) // kw, prev), tiles(kw, (COL_RNN_END + ATTN_V_AT) // kw, row),
            tiles(D_MODEL, 0, row), tiles(D_MODEL, 0, nxt),
            tiles(D_MODEL, 0, row), tiles(D_MODEL, 0, nxt),
            tiles(LANES, 0, row), tiles(LANES, 0, nxt),
            table(row), table(row), table(nxt), table(nxt),
            pl.BlockSpec((1, D_MODEL), lambda b, n: (0, 0)),
            pl.BlockSpec((1, kw), lambda b, n: (0, 0)),
            pl.BlockSpec((1, LANES), lambda b, n: (0, 0)),
        ],
        out_specs=[tiles(ATTN_W, 0, row), pl.BlockSpec((None, 8, D_MODEL), lambda b, n: (b, 0, 0))],
        out_shape=[jax.ShapeDtypeStruct((t, ATTN_W), BF16), jax.ShapeDtypeStruct((n_seq, 8, D_MODEL), F32)],
        scratch_shapes=[pltpu.VMEM((blk, D_MODEL), F32), pltpu.VMEM((2, 3, blk, D_MODEL), F32),
                        pltpu.VMEM((2, blk, kw), F32)],
        args=(z, z, z, z, z, o, o, do, do, lse, lse, cos_t, sin_t, cos_t, sin_t,
              q_gain_t, k_gain_t, sinks_t), name="attn_bwd", semantics=("arbitrary", "arbitrary"), hosted=hosted)
    return (*outs, landed) if hosted is not None else outs


def _rope_tables(seq):
    inv = ROPE_THETA ** (-jnp.arange(0, HEAD_DIM, 2, dtype=F32) / HEAD_DIM)
    ang = jnp.arange(seq, dtype=F32)[:, None] * inv[None, :]
    cos, sin = jnp.cos(ang), jnp.sin(ang)
    return jnp.tile(jnp.concatenate([cos, cos], axis=1), (1, 2)), jnp.tile(jnp.concatenate([-sin, sin], axis=1), (1, 2))


def _block_diag_tiles(w):
    per = RNN_TILE // RNN_BLOCK_W
    w4 = w.reshape(D_MODEL // RNN_TILE, per, RNN_BLOCK_W, RNN_BLOCK_W)
    eye = jnp.eye(per, dtype=w.dtype)
    dense = jnp.einsum("tpij,pq->tpiqj", w4, eye)
    return dense.reshape(D_MODEL // RNN_TILE, RNN_TILE, RNN_TILE).astype(BF16)


def _block_diag_extract(dense):
    per = RNN_TILE // RNN_BLOCK_W
    d5 = dense.reshape(D_MODEL // RNN_TILE, per, RNN_BLOCK_W, per, RNN_BLOCK_W)
    blocks = jnp.stack([d5[:, p, :, p, :] for p in range(per)], axis=1)
    return blocks.reshape(D_MODEL // RNN_BLOCK_W, RNN_BLOCK_W, RNN_BLOCK_W)


def _local_step(x, p, target, w, *, n_seq, seq, comm=None):
    w = dict(w)

    def run(tag, fn, *args, **kwargs):
        hosted = comm.host(tag) if comm is not None else None
        if hosted is None:
            return fn(*args, **kwargs)
        *outs, landed = fn(*args, hosted=hosted, **kwargs)
        comm.landed(tag, landed, w)
        return outs[0] if len(outs) == 1 else outs

    def ready(batch, grads, extra=None):
        if comm is not None:
            comm.ready(batch, grads, extra)

    cos_t, sin_t = _rope_tables(seq)
    q_gain_t = jnp.tile(w["q_gain"], (1, N_Q_HEADS))
    k_gain_t = jnp.tile(w["k_gain"], (1, N_KV_HEADS))
    sinks_t = jnp.pad(w["sinks"], ((0, 0), (0, LANES - N_Q_HEADS)))
    wrg_bd, wig_bd = _block_diag_tiles(w["w_rg"]), _block_diag_tiles(w["w_ig"])
    dims = dict(n_seq=n_seq, seq=seq)

    z, h = run("mm_in", _norm_matmul, x, w["g_mix"], w["w_in"], tm=1024, tn=IN_TOTAL // 4, name="mm_in")
    gate_tile = 512
    ga_at, gb_at = COL_ATTN_END // gate_tile, (COL_ATTN_END + D_MODEL) // gate_tile
    xc, hr, ya_in = run("rnn_fwd", _rnn_fwd, z, w["conv_w"], w["conv_b"], wrg_bd, w["b_rg"], wig_bd, w["b_ig"],
                        w["lru_lambda"], **dims)
    o, lse = run("attn_fwd", _attn_fwd, z, cos_t, sin_t, q_gain_t, k_gain_t, sinks_t, **dims)
    ya = run("mm_rnn_proj", _matmul, ya_in, w["w_rnn_proj"], mode="nn", tm=1024, tn=1024, out_dtypes=[F32],
             name="mm_rnn_proj")
    yb, merged = _matmul(
        o, w["w_attn_proj"], mode="nn", tm=1024, tn=gate_tile, out_dtypes=[F32, BF16], name="mm_attn_proj",
        epilogue=lambda acc, ga, gb, yav: (acc, _sig(ga) * yav + _sig(gb) * acc),
        extras=(z, z, ya), extra_col_blocks=(ga_at, gb_at, 0))
    def residual_then_norm(acc, res, gain):
        new = res + acc
        return new, _rmsnorm_rows(new, gain)

    x1, hm = _matmul(merged, w["w_out"], mode="nn", tm=512, tn=1024, out_dtypes=[F32, BF16], name="mm_out",
                     epilogue=residual_then_norm, extras=(x,), row_vecs=(w["g_mlp"],))
    act = _matmul(hm, w["w_up"], mode="nn", tm=1024, tn=1024, out_dtypes=[BF16], name="mm_up",
                  epilogue=lambda acc: (jnp.square(jnp.maximum(acc, 0.0)),))
    x2, hp = _matmul(act, w["w_down"], mode="nn", tm=512, tn=1024, out_dtypes=[F32, BF16], name="mm_down",
                     epilogue=residual_then_norm, extras=(x1,), row_vecs=(w["g_ple"],))
    p_bf = p.astype(BF16)
    e = _matmul(p_bf, w["w_ple_proj"], mode="nn", tm=1024, tn=1024, out_dtypes=[F32], name="mm_ple_proj")

    def loss_head(gt, x2v, ev, tgt):
        sg = _sig(gt)
        diff = x2v + ev * sg - tgt
        dx = diff * (1.0 / D_MODEL)
        return dx, dx * ev * sg * (1.0 - sg), dx * sg, jnp.sum(diff * diff, axis=0, keepdims=True)

    dx3, dgt, de, loss_row = _matmul(hp, w["w_ple_gate"], mode="nn", tm=512, tn=1024, out_dtypes=[F32, BF16, BF16],
                                     name="mm_ple_gate", epilogue=loss_head, extras=(x2, e, target), n_row_sums=1)

    g = {}
    g["w_ple_proj"] = _matmul_tn(p_bf, de, tk=PLE_DIM, tn=1024, tt=1024, name="mm_d_ple_proj",
                                 slot_cols=D_MODEL // N_DEV)
    g["w_ple_gate"] = _matmul_tn(hp, dgt, tk=1024, tn=1024, tt=1024, name="mm_d_ple_gate")
    def through_norm(dy, xv, dres, gain):
        dx, dgain = _rmsnorm_bwd_rows(dy, xv, dres, gain)
        return dx, dx, dgain

    dx2, dx2_bf, g["g_ple"] = _matmul(
        dgt, w["w_ple_gate"], mode="nt", tm=512, tn=1024, out_dtypes=[F32, BF16], name="mm_dhp",
        epilogue=through_norm, extras=(x2, dx3), row_vecs=(w["g_ple"],), n_row_sums=1)
    g["w_down"] = _matmul_tn(act, dx2_bf, tk=1024, tn=1024, tt=1024, name="mm_d_down")

    def relu_grad(dact, a):
        a = a.astype(F32)
        return (dact * (2.0 * jnp.where(a > 0.0, a * lax.rsqrt(a), 0.0)),)

    du = _matmul(dx2_bf, w["w_down"], mode="nt", tm=1024, tn=1024, out_dtypes=[BF16], name="mm_dact",
                 epilogue=relu_grad, extras=(act,))
    g["w_up"] = _matmul_tn(hm, du, tk=1024, tn=1024, tt=1024, name="mm_d_up", slot_cols=D_FF // N_DEV)
    ready(1, g)
    dx1, dx1_bf, g["g_mlp"] = run(
        "mm_dhm", _matmul, du, w["w_up"], mode="nt", tm=512, tn=1024, out_dtypes=[F32, BF16], name="mm_dhm",
        epilogue=through_norm, extras=(x1, dx2), row_vecs=(w["g_mlp"],), n_row_sums=1)
    g["w_out"] = _matmul_tn(merged, dx1_bf, tk=1024, tn=1024, tt=1024, name="mm_d_out")
    def merge_bwd(dm, ga, gb, yav, ybv):
        sa, sb = _sig(ga), _sig(gb)
        return dm * sa, dm * sb, dm * yav * sa * (1.0 - sa), dm * ybv * sb * (1.0 - sb)

    dya, dyb, dga, dgb = _matmul(dx1_bf, w["w_out"], mode="nt", tm=1024, tn=gate_tile, out_dtypes=[BF16] * 4,
                                 name="mm_dmerged", epilogue=merge_bwd, extras=(z, z, ya, yb),
                                 extra_col_blocks=(ga_at, gb_at, 0, 0))
    g["w_rnn_proj"] = _matmul_tn(ya_in, dya, tk=1024, tn=1024, tt=1024, name="mm_d_rnn_proj")
    g["w_attn_proj"] = _matmul_tn(o, dyb, tk=1024, tn=1024, tt=1024, name="mm_d_attn_proj")
    ready(2, g)
    dya_in = run("mm_dya_in", _matmul, dya, w["w_rnn_proj"], mode="nt", tm=1024, tn=1024, out_dtypes=[F32],
                 name="mm_dya_in")
    do = _matmul(dyb, w["w_attn_proj"], mode="nt", tm=1024, tn=1024, out_dtypes=[BF16], name="mm_do")
    dx_rnn, dg_rnn, dwrg_dense, dwig_dense, rnn_vec = run(
        "rnn_bwd", _rnn_bwd, dya_in, z, xc, hr, w["conv_w"], wrg_bd, w["b_rg"], wig_bd, w["b_ig"],
        w["lru_lambda"], **dims)
    dz_attn, attn_vec = run("attn_bwd", _attn_bwd, z, o, lse, do, cos_t, sin_t, q_gain_t, k_gain_t, sinks_t,
                            **dims)
    dz_parts = (dx_rnn, dg_rnn, dz_attn, dga, dgb)
    g["w_rg"] = _block_diag_extract(dwrg_dense)
    g["w_ig"] = _block_diag_extract(dwig_dense)
    g["b_rg"], g["b_ig"], g["lru_lambda"], g["conv_b"] = (rnn_vec[i:i + 1] for i in range(4))
    g["conv_w"] = rnn_vec[4:8]
    attn_vec = attn_vec[0] if n_seq == 1 else functools.reduce(jnp.add, [attn_vec[b] for b in range(n_seq)])
    g["q_gain"] = attn_vec[0].reshape(N_Q_HEADS, HEAD_DIM).sum(axis=0)[None, :]
    g["k_gain"] = attn_vec[1, :KV_W].reshape(N_KV_HEADS, HEAD_DIM).sum(axis=0)[None, :]
    g["sinks"] = attn_vec[2:3, :N_Q_HEADS]
    ready(SMALL_BATCH, g, {LOSS_ROW: loss_row})
    g["w_in"] = jnp.concatenate(
        list(run("mm_d_in_rnn", _matmul_tn_multi, h, dz_parts[:2], tt=1024, name="mm_d_in_rnn"))
        + list(run("mm_d_in_rest", _matmul_tn_multi, h, dz_parts[2:], tt=512, name="mm_d_in_rest")), axis=1)
    ready(3, g)
    w_in_attn, w_in_gate = w["w_in"][:, COL_RNN_END:COL_ATTN_END], w["w_in"][:, COL_ATTN_END:]
    windows = ((w["w_in"], (0, D_MODEL)), (w["w_in"], (D_MODEL, D_MODEL)), (w_in_attn, (0, ATTN_W)),
               (w_in_gate, (0, D_MODEL)), (w_in_gate, (D_MODEL, D_MODEL)))
    grad_x, g["g_mix"] = run(
        "mm_dh", _matmul, dz_parts, [wd[0] for wd in windows], mode="nt", tm=256, tn=1024, out_dtypes=[F32],
        name="mm_dh", b_cols=[wd[1] for wd in windows], epilogue=_rmsnorm_bwd_rows, extras=(x, dx1),
        row_vecs=(w["g_mix"],), n_row_sums=1)
    return jnp.sum(loss_row), grad_x, g


MESH_ID = pl.DeviceIdType.MESH


def _coords(index):
    return (index >> 2) & 1, (index >> 1) & 1, index & 1


def _exchange(srcs, kinds, *, name):
    n = len(srcs)
    n_peer = N_DEV - 1

    def body(*refs):
        src, dst = refs[:n], refs[n:2 * n]
        send_sems, recv_sems, local_sems = refs[2 * n:]
        me = 4 * lax.axis_index("x") + 2 * lax.axis_index("y") + lax.axis_index("c")

        def remote(i, d):
            peer = (me + d) & (N_DEV - 1)
            piece = src[i] if kinds[i] == "gather" else src[i].at[peer]
            return pltpu.make_async_remote_copy(
                src_ref=piece, dst_ref=dst[i].at[me], send_sem=send_sems.at[i * n_peer + d - 1],
                recv_sem=recv_sems.at[i * n_peer + d - 1], device_id=_coords(peer), device_id_type=MESH_ID)

        def arrival(i, d):
            sender = (me - d) & (N_DEV - 1)
            piece = src[i] if kinds[i] == "gather" else src[i].at[sender]
            return pltpu.make_async_remote_copy(
                src_ref=piece, dst_ref=dst[i].at[sender], send_sem=send_sems.at[i * n_peer + d - 1],
                recv_sem=recv_sems.at[i * n_peer + d - 1], device_id=_coords(sender), device_id_type=MESH_ID)

        own = []
        for i in range(n):
            piece = src[i] if kinds[i] == "gather" else src[i].at[me]
            own.append(pltpu.make_async_copy(piece, dst[i].at[me], local_sems.at[i]))
            own[-1].start()
        sent = [remote(i, d) for d in range(1, N_DEV) for i in range(n)]
        for cp in sent:
            cp.start()
        for d in range(1, N_DEV):
            for i in range(n):
                arrival(i, d).wait_recv()
        for cp in sent:
            cp.wait_send()
        for cp in own:
            cp.wait()

    def out_of(s, kind):
        shape = s.shape if kind == "scatter" else (N_DEV,) + s.shape
        return jax.ShapeDtypeStruct(shape, s.dtype)

    any_spec = pl.BlockSpec(memory_space=pl.ANY)
    return pl.pallas_call(
        body,
        in_specs=[any_spec] * n,
        out_specs=[any_spec] * n,
        out_shape=[out_of(s, k) for s, k in zip(srcs, kinds)],
        scratch_shapes=[pltpu.SemaphoreType.DMA((n * n_peer,)), pltpu.SemaphoreType.DMA((n * n_peer,)),
                        pltpu.SemaphoreType.DMA((n,))],
        compiler_params=pltpu.CompilerParams(has_side_effects=True),
        name=name,
    )(*srcs)


def _remote(src, dst, send_sem, recv_sem, to):
    return pltpu.make_async_remote_copy(src_ref=src, dst_ref=dst, send_sem=send_sem, recv_sem=recv_sem,
                                        device_id=to, device_id_type=MESH_ID)


def _gather_two_level(shards, *, name):
    n = len(shards)
    per = N_DEV - 1

    def body(*refs):
        src, dst = refs[:n], refs[n:2 * n]
        send_sems, recv_sems, local_sems = refs[2 * n:]
        x, y, c = lax.axis_index("x"), lax.axis_index("y"), lax.axis_index("c")
        me, sibling = (x, y, c), (x, y, 1 - c)
        chips = [(1 - x, y), (x, 1 - y), (1 - x, 1 - y)]

        def slot(pos):
            return 4 * pos[0] + 2 * pos[1] + pos[2]

        def copy(i, k, block, to, from_shard=False):
            source = src[i] if from_shard else dst[i].at[slot(block)]
            return _remote(source, dst[i].at[slot(block)], send_sems.at[i * per + k], recv_sems.at[i * per + k], to)

        mine = [pltpu.make_async_copy(src[i], dst[i].at[slot(me)], local_sems.at[i]) for i in range(n)]
        for cp in mine:
            cp.start()
        first = []
        for i in range(n):
            first.append(copy(i, 0, me, sibling, from_shard=True))
            first += [copy(i, 1 + j, me, (*chip, c), from_shard=True) for j, chip in enumerate(chips)]
        for cp in first:
            cp.start()
        passed = []
        for i in range(n):
            for j, chip in enumerate(chips):
                copy(i, 1 + j, (*chip, c), me).wait_recv()
                passed.append(copy(i, 4 + j, (*chip, c), sibling))
                passed[-1].start()
        for i in range(n):
            copy(i, 0, sibling, me).wait_recv()
            for j, chip in enumerate(chips):
                copy(i, 4 + j, (*chip, 1 - c), me).wait_recv()
        for cp in first + passed:
            cp.wait_send()
        for cp in mine:
            cp.wait()

    any_spec = pl.BlockSpec(memory_space=pl.ANY)
    return pl.pallas_call(
        body,
        in_specs=[any_spec] * n,
        out_specs=[any_spec] * n,
        out_shape=[jax.ShapeDtypeStruct((N_DEV,) + s.shape, s.dtype) for s in shards],
        scratch_shapes=[pltpu.SemaphoreType.DMA((n * per,)), pltpu.SemaphoreType.DMA((n * per,)),
                        pltpu.SemaphoreType.DMA((n,))],
        name=name,
    )(*shards)


CHIPS = N_DEV // 2


def _other_chips(x, y):
    return [(x, 1 - y), (1 - x, y), (1 - x, 1 - y)]


def _hosted_gather_first(shards):
    n = len(shards)
    per = CHIPS

    def plan(src, dst, send_sems, recv_sems, local_sems, first_sem):
        x, y, c = lax.axis_index("x"), lax.axis_index("y"), lax.axis_index("c")
        peers = [(x, y, 1 - c)] + [(*chip, c) for chip in _other_chips(x, y)]
        copies = []
        for i in range(n):
            own = pltpu.make_async_copy(src[i], dst[i].at[4 * x + 2 * y + c], local_sems.at[first_sem + i])
            copies.append(_Xfer(own.start, own.wait))
        for j, peer in enumerate(peers):
            for i in range(n):
                k = first_sem + i * per + j
                out = _remote(src[i], dst[i].at[4 * x + 2 * y + c], send_sems.at[k], recv_sems.at[k], peer)
                arrival = _remote(src[i], dst[i].at[4 * peer[0] + 2 * peer[1] + peer[2]], send_sems.at[k],
                                  recv_sems.at[k], peer)

                def wait(out=out, arrival=arrival):
                    arrival.wait_recv()
                    out.wait_send()

                copies.append(_Xfer(out.start, wait))
        return copies

    out_shape = tuple(jax.ShapeDtypeStruct((N_DEV,) + s.shape, s.dtype) for s in shards)
    return _Hosted(tuple(shards), out_shape, n * per, plan)


def _hosted_gather_second(landed):
    n = len(landed)
    per = CHIPS - 1

    def plan(src, dst, send_sems, recv_sems, local_sems, first_sem):
        x, y, c = lax.axis_index("x"), lax.axis_index("y"), lax.axis_index("c")
        copies = []
        for j, chip in enumerate(_other_chips(x, y)):
            mine, theirs = 4 * chip[0] + 2 * chip[1] + c, 4 * chip[0] + 2 * chip[1] + 1 - c
            for i in range(n):
                k = first_sem + i * per + j
                out = _remote(src[i].at[mine], dst[i].at[mine], send_sems.at[k], recv_sems.at[k], (x, y, 1 - c))
                arrival = _remote(src[i].at[theirs], dst[i].at[theirs], send_sems.at[k], recv_sems.at[k],
                                  (x, y, 1 - c))

                def wait(out=out, arrival=arrival):
                    arrival.wait_recv()
                    out.wait_send()

                copies.append(_Xfer(out.start, wait))
        return copies

    out_shape = tuple(jax.ShapeDtypeStruct(a.shape, a.dtype) for a in landed)
    return _Hosted(tuple(landed), out_shape, n * per, plan, tuple((i, i) for i in range(n)))


def _hosted_sibling_swap(arrays, sliced):
    n_sems = sum(CHIPS if s else 1 for s in sliced)

    def plan(src, dst, send_sems, recv_sems, local_sems, first_sem):
        x, y, c = lax.axis_index("x"), lax.axis_index("y"), lax.axis_index("c")
        sibling = (x, y, 1 - c)
        copies, k = [], first_sem
        for i, is_sliced in enumerate(sliced):
            pieces = [(src[i].at[2 * s + 1 - c], dst[i].at[s]) for s in range(CHIPS)] if is_sliced else [(src[i], dst[i])]
            for source, target in pieces:
                cp = _remote(source, target, send_sems.at[k], recv_sems.at[k], sibling)
                copies.append(_Xfer(cp.start, cp.wait))
                k += 1
        return copies

    out_shape = tuple(jax.ShapeDtypeStruct((CHIPS,) + a.shape[1:] if s else a.shape, a.dtype)
                      for a, s in zip(arrays, sliced))
    return _Hosted(tuple(arrays), out_shape, n_sems, plan)


def _hosted_chip_exchange(arrays, sliced):
    n = len(arrays)
    per = CHIPS - 1

    def plan(src, dst, send_sems, recv_sems, local_sems, first_sem):
        x, y, c = lax.axis_index("x"), lax.axis_index("y"), lax.axis_index("c")
        chip = 2 * x + y
        copies = []
        for i in range(n):
            own = pltpu.make_async_copy(src[i].at[chip] if sliced[i] else src[i], dst[i].at[chip],
                                        local_sems.at[first_sem + i])
            copies.append(_Xfer(own.start, own.wait))
        for d in range(1, CHIPS):
            other = chip ^ d
            to = ((other >> 1) & 1, other & 1, c)
            for i in range(n):
                k = first_sem + i * per + d - 1
                source = src[i].at[other] if sliced[i] else src[i]
                out = _remote(source, dst[i].at[chip], send_sems.at[k], recv_sems.at[k], to)
                arrival = _remote(source, dst[i].at[other], send_sems.at[k], recv_sems.at[k], to)

                def wait(out=out, arrival=arrival):
                    arrival.wait_recv()
                    out.wait_send()

                copies.append(_Xfer(out.start, wait))
        return copies

    out_shape = tuple(jax.ShapeDtypeStruct(a.shape if s else (CHIPS,) + a.shape, a.dtype)
                      for a, s in zip(arrays, sliced))
    return _Hosted(tuple(arrays), out_shape, n * per, plan)


def _add_sibling(parts, received, core, *, name):
    _, r, cols = parts.shape
    tr = min(256, r)

    def body(core_ref, a_ref, b_ref, o_ref):
        o_ref[...] = (a_ref[...] + b_ref[...]).astype(BF16)

    grid_spec = pltpu.PrefetchScalarGridSpec(
        num_scalar_prefetch=1,
        grid=(CHIPS, r // tr),
        in_specs=[pl.BlockSpec((None, tr, cols), lambda k, i, core_ref: (2 * k + core_ref[0], i, 0)),
                  pl.BlockSpec((None, tr, cols), lambda k, i, core_ref: (k, i, 0))],
        out_specs=pl.BlockSpec((None, tr, cols), lambda k, i, core_ref: (k, i, 0)),
    )
    return pl.pallas_call(body, grid_spec=grid_spec, out_shape=jax.ShapeDtypeStruct((CHIPS, r, cols), BF16),
                          compiler_params=_params("parallel", "parallel"), name=name)(core, parts, received)


def _add_whole(a, b, *, name):
    def body(a_ref, b_ref, o_ref):
        o_ref[...] = a_ref[...] + b_ref[...]

    return pl.pallas_call(body, out_shape=jax.ShapeDtypeStruct(a.shape, F32), name=name)(a, b)


def _adamw(parts, w, m, v, *, name):
    r, c = w.shape
    n_parts = parts.shape[0]
    tr = min(256, r)
    c1 = 1.0 - ADAM_B1 ** ADAM_STEP
    c2 = 1.0 - ADAM_B2 ** ADAM_STEP

    def body(p_ref, w_ref, m_ref, v_ref, g_ref, d_ref, nm_ref, nv_ref):
        g = p_ref[0].astype(F32)
        for s in range(1, n_parts):
            g = g + p_ref[s].astype(F32)
        nm = ADAM_B1 * m_ref[...] + (1.0 - ADAM_B1) * g
        nv = ADAM_B2 * v_ref[...] + (1.0 - ADAM_B2) * (g * g)
        g_ref[...] = g
        nm_ref[...] = nm
        nv_ref[...] = nv
        d_ref[...] = -ADAM_LR * ((nm / c1) / (jnp.sqrt(nv / c2) + ADAM_EPS) + ADAM_WD * w_ref[...])

    tile = pl.BlockSpec((tr, c), lambda i: (i, 0))
    return pl.pallas_call(
        body,
        grid=(r // tr,),
        in_specs=[pl.BlockSpec((n_parts, tr, c), lambda i: (0, i, 0)), tile, tile, tile],
        out_specs=[tile] * 4,
        out_shape=[jax.ShapeDtypeStruct((r, c), F32)] * 4,
        compiler_params=_params("parallel"),
        name=name,
    )(parts, w, m, v)


BIG = ("w_in", "w_rnn_proj", "w_attn_proj", "w_out", "w_up", "w_down", "w_ple_gate", "w_ple_proj")
LOSS_ROW = "loss"
SMALL = (("conv_b", 1), ("b_rg", 1), ("b_ig", 1), ("lru_lambda", 1), ("g_mlp", 1), ("g_ple", 1),
         ("q_gain", 1), ("k_gain", 1), ("sinks", 1), (LOSS_ROW, 1), ("w_rg", 64), ("w_ig", 64))
SMALL_ROWS = 144
COL_SHARDED = ("w_in", "w_up", "w_ple_proj")
BATCHES = {1: ("w_ple_proj", "w_ple_gate", "w_down", "w_up"), 2: ("w_out", "w_rnn_proj", "w_attn_proj"),
           3: ("w_in", "conv_w")}
SMALL_BATCH = 4


def _pack_small(vals):
    rows = []
    for nm, nrow in SMALL:
        flat = vals[nm].reshape(-1).astype(F32)
        rows.append(jnp.pad(flat, (0, nrow * D_MODEL - flat.shape[0])).reshape(nrow, D_MODEL))
    used = sum(nrow for _, nrow in SMALL)
    rows.append(jnp.zeros((SMALL_ROWS - used, D_MODEL), F32))
    return jnp.concatenate(rows, axis=0)


def _unpack_small(packed, shapes):
    out, at = {}, 0
    for nm, nrow in SMALL:
        size = 1
        for s in shapes[nm]:
            size *= s
        out[nm] = packed[at:at + nrow].reshape(-1)[:size].reshape(shapes[nm])
        at += nrow
    return out


def _full_weight(name, landed):
    if name in COL_SHARDED:
        return landed.transpose(1, 0, 2).reshape(landed.shape[1], N_DEV * landed.shape[2])
    return landed.reshape(N_DEV * landed.shape[1], landed.shape[2])


def _owner_slots(name, grad):
    if name == "w_in":
        return grad.reshape(D_MODEL, N_DEV, IN_TOTAL // N_DEV).transpose(1, 0, 2)
    if name == "conv_w":
        return grad.reshape(CONV_W, N_DEV, D_MODEL // N_DEV).transpose(1, 0, 2)
    if name in COL_SHARDED:
        return grad
    return grad.reshape(N_DEV, grad.shape[0] // N_DEV, grad.shape[1])


class _StepExchanges:
    FIRST, SECOND = "first", "second"
    PROJ, OUT, PLE_GATE, UP, DOWN = (("w_rnn_proj", "w_attn_proj"), ("w_out",), ("w_ple_gate",), ("w_up",),
                                     ("w_down", "w_ple_proj"))
    GATHERS = {"mm_in": ((FIRST, PROJ), (FIRST, OUT), (FIRST, PLE_GATE)),
               "rnn_fwd": ((SECOND, PROJ), (SECOND, OUT), (SECOND, PLE_GATE), (FIRST, UP)),
               "attn_fwd": ((SECOND, UP), (FIRST, DOWN)), "mm_rnn_proj": ((SECOND, DOWN),)}
    SWAPS = {"mm_dhm": 1, "mm_dya_in": 2, "mm_d_in_rnn": SMALL_BATCH}
    CHIP_EXCHANGES = {"rnn_bwd": (1,), "attn_bwd": (2,), "mm_d_in_rest": (SMALL_BATCH,), "mm_dh": (3,)}

    def __init__(self, shards, core):
        self.shards = shards
        self.core = core
        self.parts, self.swapped, self.summed, self.half_gathered = {}, {}, {}, {}

    def ready(self, batch, grads, extra=None):
        if batch == SMALL_BATCH:
            self.parts[batch] = ([_pack_small({**grads, **extra})], [False])
            return
        arrays = [_owner_slots(nm, grads[nm]) for nm in BATCHES[batch]]
        self.parts[batch] = (arrays, [True] * len(arrays))
        if batch not in self.SWAPS.values():
            _, self.swapped[batch] = _call(
                lambda: None, grid=(1,), in_specs=[], out_specs=[], out_shape=[], args=(), name="swap_last",
                semantics=("arbitrary",), hosted=_hosted_sibling_swap(*self.parts[batch]))

    def host(self, tag):
        if tag in self.GATHERS:
            return _merge_hosted([
                _hosted_gather_first([self.shards[nm] for nm in group]) if half == self.FIRST
                else _hosted_gather_second([self.half_gathered[nm] for nm in group])
                for half, group in self.GATHERS[tag]])
        if tag in self.SWAPS:
            return _hosted_sibling_swap(*self.parts[self.SWAPS[tag]])
        if tag in self.CHIP_EXCHANGES:
            hosted = []
            for batch in self.CHIP_EXCHANGES[tag]:
                arrays, sliced = self.parts[batch]
                labels = BATCHES.get(batch, ("small",))
                sums = [_add_sibling(a, r, self.core, name="add_" + lb) if s else _add_whole(a, r, name="add_" + lb)
                        for a, r, s, lb in zip(arrays, self.swapped[batch], sliced, labels)]
                hosted.append(_hosted_chip_exchange(sums, sliced))
            return _merge_hosted(hosted)
        return None

    def landed(self, tag, landed, weights):
        if tag in self.GATHERS:
            names = [(half, nm) for half, group in self.GATHERS[tag] for nm in group]
            for (half, nm), buf in zip(names, landed):
                if half == self.FIRST:
                    self.half_gathered[nm] = buf
                else:
                    weights[nm] = _full_weight(nm, buf)
        elif tag in self.SWAPS:
            self.swapped[self.SWAPS[tag]] = landed
        else:
            at = 0
            for batch in self.CHIP_EXCHANGES[tag]:
                count = len(self.parts[batch][0])
                self.summed[batch] = landed[at:at + count]
                at += count


def kernel(x, p, g_mix, w_in, conv_w, conv_b, w_rg, b_rg, w_ig, b_ig, lru_lambda, w_rnn_proj, q_gain, k_gain, sinks, w_attn_proj, w_out, g_mlp, w_up, w_down, g_ple, w_ple_gate, w_ple_proj, loss_target, m_g_mix, m_w_in, m_conv_w, m_conv_b, m_w_rg, m_b_rg, m_w_ig, m_b_ig, m_lru_lambda, m_w_rnn_proj, m_q_gain, m_k_gain, m_sinks, m_w_attn_proj, m_w_out, m_g_mlp, m_w_up, m_w_down, m_g_ple, m_w_ple_gate, m_w_ple_proj, v_g_mix, v_w_in, v_conv_w, v_conv_b, v_w_rg, v_b_rg, v_w_ig, v_b_ig, v_lru_lambda, v_w_rnn_proj, v_q_gain, v_k_gain, v_sinks, v_w_attn_proj, v_w_out, v_g_mlp, v_w_up, v_w_down, v_g_ple, v_w_ple_gate, v_w_ple_proj):
    names = ("g_mix", "w_in", "conv_w", "conv_b", "w_rg", "b_rg", "w_ig", "b_ig", "lru_lambda", "w_rnn_proj",
             "q_gain", "k_gain", "sinks", "w_attn_proj", "w_out", "g_mlp", "w_up", "w_down", "g_ple",
             "w_ple_gate", "w_ple_proj")
    wts = dict(zip(names, (g_mix, w_in, conv_w, conv_b, w_rg, b_rg, w_ig, b_ig, lru_lambda, w_rnn_proj, q_gain,
                           k_gain, sinks, w_attn_proj, w_out, g_mlp, w_up, w_down, g_ple, w_ple_gate, w_ple_proj)))
    mom1 = dict(zip(names, (m_g_mix, m_w_in, m_conv_w, m_conv_b, m_w_rg, m_b_rg, m_w_ig, m_b_ig, m_lru_lambda,
                            m_w_rnn_proj, m_q_gain, m_k_gain, m_sinks, m_w_attn_proj, m_w_out, m_g_mlp, m_w_up,
                            m_w_down, m_g_ple, m_w_ple_gate, m_w_ple_proj)))
    mom2 = dict(zip(names, (v_g_mix, v_w_in, v_conv_w, v_conv_b, v_w_rg, v_b_rg, v_w_ig, v_b_ig, v_lru_lambda,
                            v_w_rnn_proj, v_q_gain, v_k_gain, v_sinks, v_w_attn_proj, v_w_out, v_g_mlp, v_w_up,
                            v_w_down, v_g_ple, v_w_ple_gate, v_w_ple_proj)))
    n_seq, seq, _ = x.shape
    core = lax.axis_index("c").astype(jnp.int32).reshape(1)

    shards = {nm: wts[nm][0].astype(BF16) for nm in BIG}
    w_in_all, conv_all = _gather_two_level([shards["w_in"], conv_w[0]], name="gather_w_in")
    w = {nm: wts[nm] for nm in names if nm not in BIG}
    w["w_rg"], w["w_ig"] = w_rg[0], w_ig[0]
    w["conv_w"] = conv_all.transpose(1, 0, 2).reshape(CONV_W, D_MODEL)
    w["w_in"] = _full_weight("w_in", w_in_all)
    comm = _StepExchanges(shards, core)
    loss_sum, grad_x, g = _local_step(
        x.reshape(n_seq * seq, D_MODEL), p.reshape(n_seq * seq, PLE_DIM), loss_target.reshape(n_seq * seq, D_MODEL),
        w, n_seq=n_seq, seq=seq, comm=comm)
    del loss_sum

    res = {}
    for batch, batch_names in BATCHES.items():
        for nm, summed in zip(batch_names, comm.summed[batch]):
            res[nm] = _adamw(summed, wts[nm][0], mom1[nm][0], mom2[nm][0], name="adamw_" + nm)
    g_mix_parts, = _exchange([g["g_mix"]], ["gather"], name="gather_g_mix")
    res["g_mix"] = [r[0] for r in _adamw(g_mix_parts, g_mix, m_g_mix, v_g_mix, name="adamw_g_mix")]
    small_names = [nm for nm, _ in SMALL if nm != LOSS_ROW]
    full_small = {}
    for src, key in ((wts, "w"), (mom1, "m"), (mom2, "v")):
        vals = {nm: src[nm][0] for nm in small_names}
        vals[LOSS_ROW] = jnp.zeros((1,), F32)
        full_small[key] = _pack_small(vals)
    small_res = _adamw(comm.summed[SMALL_BATCH][0],full_small["w"], full_small["m"], full_small["v"], name="adamw_small")
    shapes = {nm: wts[nm].shape[1:] for nm in small_names}
    shapes[LOSS_ROW] = (D_MODEL,)
    small_out = [_unpack_small(r, shapes) for r in small_res]
    for nm in small_names:
        res[nm] = [so[nm] for so in small_out]
    loss = jnp.sum(small_out[0][LOSS_ROW]) * (0.5 / D_MODEL)

    outs = [loss, grad_x.reshape(n_seq, seq, D_MODEL)]
    for k in range(4):
        outs.extend(res[nm][k][None] for nm in names)
    return tuple(outs)
```

```python
import functools
from typing import Callable, NamedTuple

import jax
import jax.numpy as jnp
from jax import lax
from jax.experimental import pallas as pl
from jax.experimental.pallas import tpu as pltpu

F32 = jnp.float32
BF16 = jnp.bfloat16

N_DEV = 8
D_MODEL = 1024
RNN_BLOCK_W = 64
CONV_W = 4
LRU_C = 8.0
HEAD_DIM = 64
N_Q_HEADS = 16
N_KV_HEADS = 4
KV_W = N_KV_HEADS * HEAD_DIM
WINDOW = 128
ROPE_THETA = 10000.0
D_FF = 4096
PLE_DIM = 256
NORM_EPS = 1e-6
IN_TOTAL = 5632
COL_RNN_END, COL_ATTN_END = 2048, 3584
ATTN_W = COL_ATTN_END - COL_RNN_END
ATTN_K_AT, ATTN_V_AT = 1024, 1280

ADAM_LR = 0.001
ADAM_B1 = 0.9
ADAM_B2 = 0.999
ADAM_EPS = 1e-08
ADAM_WD = 0.01
ADAM_STEP = 10

LANES = 128
SUBLANES = 8
RNN_TILE = 256
VMEM_LIMIT = 48 * 1024 * 1024
NEG_BIG = -1e30


def _params(*sem):
    return pltpu.CompilerParams(dimension_semantics=sem if sem else None, vmem_limit_bytes=VMEM_LIMIT)


def _sig(x):
    return 0.5 * jnp.tanh(0.5 * x) + 0.5


def _dot_nt(a, b):
    return lax.dot_general(a, b, (((1,), (1,)), ((), ())), preferred_element_type=F32)


def _dot_tn(a, b):
    return lax.dot_general(a, b, (((0,), (0,)), ((), ())), preferred_element_type=F32)


class _Xfer:
    def __init__(self, start, wait):
        self.start, self.wait = start, wait


class _Hosted(NamedTuple):
    srcs: tuple
    out_shape: tuple
    n_sems: int
    plan: Callable
    aliases: tuple = ()


def _merge_hosted(parts):
    parts = [p for p in parts if p is not None]
    if len(parts) <= 1:
        return parts[0] if parts else None
    src_at, dst_at, sem_at, aliases = [0], [0], [0], []
    for p in parts:
        aliases += [(i + src_at[-1], j + dst_at[-1]) for i, j in p.aliases]
        src_at.append(src_at[-1] + len(p.srcs))
        dst_at.append(dst_at[-1] + len(p.out_shape))
        sem_at.append(sem_at[-1] + p.n_sems)

    def plan(src, dst, send_sems, recv_sems, local_sems, first_sem):
        copies = []
        for k, p in enumerate(parts):
            copies += p.plan(src[src_at[k]:src_at[k + 1]], dst[dst_at[k]:dst_at[k + 1]], send_sems, recv_sems,
                             local_sems, first_sem + sem_at[k])
        return copies

    return _Hosted(tuple(a for p in parts for a in p.srcs), tuple(s for p in parts for s in p.out_shape),
                   sem_at[-1], plan, tuple(aliases))


def _call(body, *, grid, in_specs, out_specs, out_shape, args, name, semantics, scratch_shapes=(), hosted=None):
    if hosted is None:
        outs = pl.pallas_call(body, grid=grid, in_specs=list(in_specs), out_specs=list(out_specs),
                              out_shape=list(out_shape), scratch_shapes=list(scratch_shapes),
                              compiler_params=_params(*semantics), name=name)(*args)
        return list(outs), []
    counts = (len(in_specs), len(hosted.srcs), len(out_specs), len(hosted.out_shape), len(scratch_shapes), 3)

    def wrapped(*refs):
        at, groups = 0, []
        for count in counts:
            groups.append(refs[at:at + count])
            at += count
        ins, srcs, outs, dsts, scratch, sems = groups
        copies = hosted.plan(srcs, dsts, *sems, 0)
        ids = [pl.program_id(axis) for axis in range(len(grid))]
        first = functools.reduce(jnp.logical_and, [i == 0 for i in ids])
        last = functools.reduce(jnp.logical_and, [i == g - 1 for i, g in zip(ids, grid)])

        @pl.when(first)
        def _():
            for cp in copies:
                cp.start()

        body(*ins, *outs, *scratch)

        @pl.when(last)
        def _():
            for cp in copies:
                cp.wait()

    any_spec = pl.BlockSpec(memory_space=pl.ANY)
    sems = [pltpu.SemaphoreType.DMA((hosted.n_sems,))] * 3
    outs = pl.pallas_call(
        wrapped, grid=grid, in_specs=list(in_specs) + [any_spec] * counts[1],
        out_specs=list(out_specs) + [any_spec] * counts[3], out_shape=list(out_shape) + list(hosted.out_shape),
        scratch_shapes=list(scratch_shapes) + sems, compiler_params=_params(*["arbitrary"] * len(grid)),
        input_output_aliases={counts[0] + i: counts[2] + j for i, j in hosted.aliases},
        name=name)(*args, *hosted.srcs)
    return list(outs[:counts[2]]), list(outs[counts[2]:])


def _dividing_tile(n, want):
    tile = min(want, n)
    while n % tile:
        tile -= LANES
    return tile


def _matmul(a, b, *, mode, tm, tn, out_dtypes, name, epilogue=None, extras=(), hosted=None, b_cols=None,
            row_vecs=(), n_row_sums=0, extra_col_blocks=None):
    a_parts = tuple(a) if isinstance(a, (tuple, list)) else (a,)
    b_parts = tuple(b) if isinstance(b, (tuple, list)) else (b,)
    assert len(a_parts) == len(b_parts) and (mode == "nt" or len(a_parts) == 1)
    n_parts = len(a_parts)
    m = a_parts[0].shape[0]
    if b_cols is None:
        b_cols = [(0, bp.shape[1]) for bp in b_parts]
    n = b_cols[0][1] if mode == "nn" else b_parts[0].shape[0]
    tm, tn = min(tm, m), _dividing_tile(n, tn)
    n_extra = len(extras) + len(row_vecs)
    n_tiles_out = len(out_dtypes)
    assert n_row_sums == 0 or n == tn

    def body(*refs):
        a_refs, b_refs = refs[:n_parts], refs[n_parts:2 * n_parts]
        rest = refs[2 * n_parts:]
        extra_refs, out_refs = rest[:n_extra], rest[n_extra:]
        if mode == "nn":
            acc = jnp.dot(a_refs[0][...], b_refs[0][...], preferred_element_type=F32)
        else:
            acc = _dot_nt(a_refs[0][...], b_refs[0][...])
            for a_ref, b_ref in zip(a_refs[1:], b_refs[1:]):
                acc = acc + _dot_nt(a_ref[...], b_ref[...])
        res = epilogue(acc, *[e[...] for e in extra_refs]) if epilogue is not None else (acc,)
        for o_ref, r in zip(out_refs[:n_tiles_out], res):
            o_ref[...] = r.astype(o_ref.dtype)
        if n_row_sums:
            @pl.when(pl.program_id(0) == 0)
            def _():
                for o_ref in out_refs[n_tiles_out:]:
                    o_ref[...] = jnp.zeros_like(o_ref)

            for o_ref, r in zip(out_refs[n_tiles_out:], res[n_tiles_out:]):
                o_ref[...] += r

    a_specs = [pl.BlockSpec((tm, ap.shape[1]), lambda i, j: (i, 0)) for ap in a_parts]
    if mode == "nn":
        assert b_cols[0][0] % tn == 0
        first = b_cols[0][0] // tn
        b_specs = [pl.BlockSpec((b_parts[0].shape[0], tn), lambda i, j: (0, first + j))]
    else:
        assert all(at % width == 0 for at, width in b_cols)
        b_specs = [pl.BlockSpec((tn, width), functools.partial(lambda i, j, blk: (j, blk), blk=at // width))
                   for at, width in b_cols]
    tile = pl.BlockSpec((tm, tn), lambda i, j: (i, j))
    row = pl.BlockSpec((1, tn), lambda i, j: (0, j))
    extra_specs = [pl.BlockSpec((tm, tn), functools.partial(lambda i, j, first: (i, first + j), first=first))
                   for first in (extra_col_blocks or [0] * len(extras))]
    outs, landed = _call(
        body,
        grid=(m // tm, n // tn),
        in_specs=a_specs + b_specs + extra_specs + [row] * len(row_vecs),
        out_specs=[tile] * n_tiles_out + [row] * n_row_sums,
        out_shape=[jax.ShapeDtypeStruct((m, n), dt) for dt in out_dtypes]
        + [jax.ShapeDtypeStruct((1, n), F32)] * n_row_sums,
        args=(*a_parts, *b_parts, *extras, *row_vecs), name=name,
        semantics=("arbitrary" if n_row_sums else "parallel", "arbitrary"), hosted=hosted)
    if hosted is not None:
        return (*outs, landed)
    return outs[0] if len(outs) == 1 else outs


def _matmul_tn(a, b, *, tk, tn, tt, name, slot_cols=None):
    t, k = a.shape
    n = b.shape[1]
    tk, tn, tt = min(tk, k), _dividing_tile(n, tn), min(tt, t)

    def body(a_ref, b_ref, o_ref):
        @pl.when(pl.program_id(2) == 0)
        def _():
            o_ref[...] = jnp.zeros_like(o_ref)

        if slot_cols is None:
            o_ref[...] += _dot_tn(a_ref[...], b_ref[...])
        else:
            av = a_ref[...]
            for s in range(tn // slot_cols):
                o_ref[s] += _dot_tn(av, b_ref[:, s * slot_cols:(s + 1) * slot_cols])

    if slot_cols is not None:
        out_spec = pl.BlockSpec((tn // slot_cols, tk, slot_cols), lambda i, j, s: (j, i, 0))
        out_shape = jax.ShapeDtypeStruct((n // slot_cols, k, slot_cols), F32)
    else:
        out_spec = pl.BlockSpec((tk, tn), lambda i, j, s: (i, j))
        out_shape = jax.ShapeDtypeStruct((k, n), F32)
    return pl.pallas_call(
        body,
        grid=(k // tk, n // tn, t // tt),
        in_specs=[pl.BlockSpec((tt, tk), lambda i, j, s: (s, i)), pl.BlockSpec((tt, tn), lambda i, j, s: (s, j))],
        out_specs=out_spec,
        out_shape=out_shape,
        compiler_params=_params("parallel", "parallel", "arbitrary"),
        name=name,
    )(a, b)


def _matmul_tn_multi(a, bs, *, tt, name, hosted=None):
    t, k = a.shape
    tt = min(tt, t)
    n_b = len(bs)

    def body(a_ref, *refs):
        b_refs, o_refs = refs[:n_b], refs[n_b:]

        @pl.when(pl.program_id(0) == 0)
        def _():
            for o_ref in o_refs:
                o_ref[...] = jnp.zeros_like(o_ref)

        a_t = a_ref[...].T
        for b_ref, o_ref in zip(b_refs, o_refs):
            o_ref[...] += jnp.dot(a_t, b_ref[...], preferred_element_type=F32)

    outs, landed = _call(
        body,
        grid=(t // tt,),
        in_specs=[pl.BlockSpec((tt, k), lambda s: (s, 0))] + [pl.BlockSpec((tt, b.shape[1]), lambda s: (s, 0)) for b in bs],
        out_specs=[pl.BlockSpec((k, b.shape[1]), lambda s: (0, 0)) for b in bs],
        out_shape=[jax.ShapeDtypeStruct((k, b.shape[1]), F32) for b in bs],
        args=(a, *bs), name=name, semantics=("arbitrary",), hosted=hosted)
    return (*outs, landed) if hosted is not None else outs


def _rmsnorm_rows(x, g):
    return x * lax.rsqrt(jnp.mean(x * x, axis=-1, keepdims=True) + NORM_EPS) * g


def _norm_matmul(x, g, b, *, tm, tn, name, hosted=None):
    m, k = x.shape
    n = b.shape[1]
    tm, tn = min(tm, m), _dividing_tile(n, tn)

    def body(x_ref, g_ref, b_ref, z_ref, h_ref, h_s):
        @pl.when(pl.program_id(1) == 0)
        def _():
            h_s[...] = _rmsnorm_rows(x_ref[...], g_ref[...]).astype(BF16)
            h_ref[...] = h_s[...]

        z_ref[...] = jnp.dot(h_s[...], b_ref[...], preferred_element_type=F32)

    rows = pl.BlockSpec((tm, k), lambda i, j: (i, 0))
    outs, landed = _call(
        body,
        grid=(m // tm, n // tn),
        in_specs=[rows, pl.BlockSpec((1, k), lambda i, j: (0, 0)), pl.BlockSpec((k, tn), lambda i, j: (0, j))],
        out_specs=[pl.BlockSpec((tm, tn), lambda i, j: (i, j)), rows],
        out_shape=[jax.ShapeDtypeStruct((m, n), F32), jax.ShapeDtypeStruct((m, k), BF16)],
        scratch_shapes=[pltpu.VMEM((tm, k), BF16)],
        args=(x, g, b), name=name, semantics=("parallel", "arbitrary"), hosted=hosted)
    return (*outs, landed) if hosted is not None else outs


def _rmsnorm_bwd_rows(dy, x, dres, g):
    r = lax.rsqrt(jnp.mean(x * x, axis=-1, keepdims=True) + NORM_EPS)
    xr = x * r
    gy = dy * g
    dx = dres + r * (gy - xr * jnp.mean(gy * xr, axis=-1, keepdims=True))
    return dx, jnp.sum(dy * xr, axis=0, keepdims=True)


def _softplus_neg(lam):
    z = -lam
    return jnp.maximum(z, 0.0) + jnp.log1p(jnp.exp(-jnp.abs(z)))


def _neg_expm1(y, exp_half_y):
    series = -y * (1.0 + y * 0.5 * (1.0 + y * (1.0 / 3.0) * (1.0 + y * 0.25 * (1.0 + y * 0.2))))
    return jnp.where(y > -0.0625, series, 1.0 - exp_half_y * exp_half_y)


def _gelu_parts(x):
    c = 0.7978845608028654
    u = c * (x + 0.044715 * x * x * x)
    th = jnp.tanh(u)
    gel = 0.5 * x * (1.0 + th)
    dgel = 0.5 * (1.0 + th) + 0.5 * x * (1.0 - th * th) * c * (1.0 + 3.0 * 0.044715 * x * x)
    return gel, dgel


def _shift_down(v, k, rows):
    return jnp.where(rows < k, 0.0, pltpu.roll(v, k, 0))


def _shift_up(v, k, rows, n):
    return jnp.where(rows >= n - k, 0.0, pltpu.roll(v, n - k, 0))


def _scan_within_groups(a, b, *, reverse):
    shape = a.shape
    a = a.reshape(shape[0] // SUBLANES, SUBLANES, shape[1])
    b = b.reshape(a.shape)
    in_group = lax.broadcasted_iota(jnp.int32, a.shape, 1)
    for s in (1, 2, 4):
        if reverse:
            inside, shift = in_group < SUBLANES - s, SUBLANES - s
        else:
            inside, shift = in_group >= s, s
        b = b + a * jnp.where(inside, pltpu.roll(b, shift, 1), 0.0)
        a = a * jnp.where(inside, pltpu.roll(a, shift, 1), 1.0)
    return a.reshape(shape), b.reshape(shape)


def _rnn_gates(xc, wrg, brg, wig, big, lam):
    xcb = xc.astype(BF16)
    r = _sig(jnp.dot(xcb, wrg, preferred_element_type=F32) + brg)
    i = _sig(jnp.dot(xcb, wig, preferred_element_type=F32) + big)
    sp = _softplus_neg(lam)
    log_a = -LRU_C * r * sp
    a = jnp.exp(log_a)
    mult = jnp.sqrt(_neg_expm1(2.0 * log_a, a))
    return xcb, r, i, sp, a, mult


def _conv_fwd(xv, cw, cb, rows):
    return (cb + _shift_down(xv, 3, rows) * cw[0:1, :] + _shift_down(xv, 2, rows) * cw[1:2, :]
            + _shift_down(xv, 1, rows) * cw[2:3, :] + xv * cw[3:4, :])


def _rnn_fwd(z, conv_w, conv_b, wrg_bd, b_rg, wig_bd, b_ig, lam, *, n_seq, seq, hosted=None):
    t = n_seq * seq
    ct = RNN_TILE
    n_ct = D_MODEL // ct

    def body(x_ref, g_ref, cw_ref, cb_ref, wrg_ref, brg_ref, wig_ref, big_ref, lam_ref,
             xc_ref, hr_ref, ya_ref, a_s, b_s):
        rows = lax.broadcasted_iota(jnp.int32, (seq, ct), 0)
        xc = _conv_fwd(x_ref[...], cw_ref[...], cb_ref[...], rows)
        _, r, i, sp, a, mult = _rnn_gates(xc, wrg_ref[...], brg_ref[...], wig_ref[...], big_ref[...], lam_ref[...])
        a_s[...], b_s[...] = _scan_within_groups(a, mult * (i * xc), reverse=False)

        def step(j, carry):
            r0 = pl.multiple_of(j * SUBLANES, SUBLANES)
            h = b_s[pl.ds(r0, SUBLANES), :] + a_s[pl.ds(r0, SUBLANES), :] * carry
            hr_ref[pl.ds(r0, SUBLANES), :] = h
            return h[SUBLANES - 1:SUBLANES, :]

        lax.fori_loop(0, seq // SUBLANES, step, jnp.zeros((1, ct), F32), unroll=4)
        gel, _ = _gelu_parts(g_ref[...])
        xc_ref[...] = xc
        ya_ref[...] = (hr_ref[...] * gel).astype(BF16)

    vec = pl.BlockSpec((1, ct), lambda b, c: (0, c))
    gate_w = pl.BlockSpec((None, ct, ct), lambda b, c: (c, 0, 0))
    tile = pl.BlockSpec((seq, ct), lambda b, c: (b, c))
    outs, landed = _call(
        body,
        grid=(n_seq, n_ct),
        in_specs=[
            pl.BlockSpec((seq, ct), lambda b, c: (b, c)),
            pl.BlockSpec((seq, ct), lambda b, c: (b, n_ct + c)),
            pl.BlockSpec((CONV_W, ct), lambda b, c: (0, c)), vec, gate_w, vec, gate_w, vec, vec,
        ],
        out_specs=[tile, tile, tile],
        out_shape=[jax.ShapeDtypeStruct((t, D_MODEL), F32), jax.ShapeDtypeStruct((t, D_MODEL), F32),
                   jax.ShapeDtypeStruct((t, D_MODEL), BF16)],
        scratch_shapes=[pltpu.VMEM((seq, ct), F32), pltpu.VMEM((seq, ct), F32)],
        args=(z, z, conv_w, conv_b, wrg_bd, b_rg, wig_bd, b_ig, lam), name="rnn_fwd",
        semantics=("parallel", "parallel"), hosted=hosted)
    return (*outs, landed) if hosted is not None else outs


def _rnn_bwd(dya, z, xc, hr, conv_w, wrg_bd, b_rg, wig_bd, b_ig, lam, *, n_seq, seq, hosted=None):
    t = n_seq * seq
    ct = RNN_TILE
    n_ct = D_MODEL // ct

    def body(dya_ref, x_ref, g_ref, xc_ref, hr_ref, cw_ref, wrg_ref, brg_ref, wig_ref, big_ref, lam_ref,
             dx_ref, dg_ref, dwrg_ref, dwig_ref, vec_ref, a_s, d_s, g_s):
        rows = lax.broadcasted_iota(jnp.int32, (seq, ct), 0)
        xv, xc, hr, dyv = x_ref[...], xc_ref[...], hr_ref[...], dya_ref[...]
        lamv = lam_ref[...]
        gel, dgel = _gelu_parts(g_ref[...])
        dg_ref[...] = (dyv * hr * dgel).astype(BF16)
        xcb, r, i, sp, a, mult = _rnn_gates(xc, wrg_ref[...], brg_ref[...], wig_ref[...], big_ref[...], lamv)
        a_s[...], d_s[...] = _scan_within_groups(_shift_up(a, 1, rows, seq), dyv * gel, reverse=True)

        def step(k, carry):
            r0 = pl.multiple_of((seq // SUBLANES - 1 - k) * SUBLANES, SUBLANES)
            gs = d_s[pl.ds(r0, SUBLANES), :] + a_s[pl.ds(r0, SUBLANES), :] * carry
            g_s[pl.ds(r0, SUBLANES), :] = gs
            return gs[0:1, :]

        lax.fori_loop(0, seq // SUBLANES, step, jnp.zeros((1, ct), F32), unroll=4)
        gsum = g_s[...]
        gated = i * xc
        d_log_a = gsum * _shift_down(hr, 1, rows) * a - gsum * gated * (a * a / mult)
        d_gated = gsum * mult
        d_pre_r = (d_log_a * (-LRU_C) * sp) * r * (1.0 - r)
        d_pre_i = (d_gated * xc) * i * (1.0 - i)
        dprb, dpib = d_pre_r.astype(BF16), d_pre_i.astype(BF16)
        dxc = d_gated * i + _dot_nt(dprb, wrg_ref[...]) + _dot_nt(dpib, wig_ref[...])
        cw = cw_ref[...]
        dx = (dxc * cw[3:4, :] + _shift_up(dxc, 1, rows, seq) * cw[2:3, :]
              + _shift_up(dxc, 2, rows, seq) * cw[1:2, :] + _shift_up(dxc, 3, rows, seq) * cw[0:1, :])
        dx_ref[...] = dx.astype(BF16)

        @pl.when(pl.program_id(1) == 0)
        def _():
            dwrg_ref[...] = jnp.zeros_like(dwrg_ref)
            dwig_ref[...] = jnp.zeros_like(dwig_ref)
            vec_ref[...] = jnp.zeros_like(vec_ref)

        dwrg_ref[...] += _dot_tn(xcb, dprb)
        dwig_ref[...] += _dot_tn(xcb, dpib)

        def colsum(v):
            return jnp.sum(v, axis=0, keepdims=True)

        d_sp = colsum(d_log_a * (-LRU_C) * r)
        vec_ref[0:1, :] += colsum(d_pre_r)
        vec_ref[1:2, :] += colsum(d_pre_i)
        vec_ref[2:3, :] += d_sp * (-_sig(-lamv))
        vec_ref[3:4, :] += colsum(dxc)
        vec_ref[4:5, :] += colsum(dxc * _shift_down(xv, 3, rows))
        vec_ref[5:6, :] += colsum(dxc * _shift_down(xv, 2, rows))
        vec_ref[6:7, :] += colsum(dxc * _shift_down(xv, 1, rows))
        vec_ref[7:8, :] += colsum(dxc * xv)

    vec = pl.BlockSpec((1, ct), lambda c, b: (0, c))
    gate_w = pl.BlockSpec((None, ct, ct), lambda c, b: (c, 0, 0))
    tile = pl.BlockSpec((seq, ct), lambda c, b: (b, c))
    outs, landed = _call(
        body,
        grid=(n_ct, n_seq),
        in_specs=[
            tile,
            pl.BlockSpec((seq, ct), lambda c, b: (b, c)),
            pl.BlockSpec((seq, ct), lambda c, b: (b, n_ct + c)),
            tile, tile,
            pl.BlockSpec((CONV_W, ct), lambda c, b: (0, c)), gate_w, vec, gate_w, vec, vec,
        ],
        out_specs=[tile, tile, gate_w, gate_w, pl.BlockSpec((8, ct), lambda c, b: (0, c))],
        out_shape=[jax.ShapeDtypeStruct((t, D_MODEL), BF16), jax.ShapeDtypeStruct((t, D_MODEL), BF16),
                   jax.ShapeDtypeStruct((n_ct, ct, ct), F32), jax.ShapeDtypeStruct((n_ct, ct, ct), F32),
                   jax.ShapeDtypeStruct((8, D_MODEL), F32)],
        scratch_shapes=[pltpu.VMEM((seq, ct), F32)] * 3,
        args=(dya, z, z, xc, hr, conv_w, wrg_bd, b_rg, wig_bd, b_ig, lam), name="rnn_bwd",
        semantics=("parallel", "arbitrary"), hosted=hosted)
    return (*outs, landed) if hosted is not None else outs


def _split_hi_lo(x):
    hi = x.astype(BF16)
    return hi, (x - hi.astype(F32)).astype(BF16)


def _dot_split(x, m_twice):
    hi, lo = _split_hi_lo(x)
    return jnp.dot(jnp.concatenate([hi, lo], axis=1), m_twice, preferred_element_type=F32)


def _head_matrices(width):
    ec = ((lax.broadcasted_iota(jnp.int32, (2 * width, LANES), 0) & (width - 1)) // HEAD_DIM
          == lax.broadcasted_iota(jnp.int32, (2 * width, LANES), 1))
    ee = (lax.broadcasted_iota(jnp.int32, (2 * LANES, width), 1) // HEAD_DIM
          == (lax.broadcasted_iota(jnp.int32, (2 * LANES, width), 0) & (LANES - 1)))
    return jnp.where(ec, 1.0, 0.0).astype(BF16), jnp.where(ee, 1.0, 0.0).astype(BF16)


def _swap_halves(y):
    w = y.shape[1]
    first = (lax.broadcasted_iota(jnp.int32, y.shape, 1) % HEAD_DIM) < HEAD_DIM // 2
    return jnp.where(first, pltpu.roll(y, w - HEAD_DIM // 2, 1), pltpu.roll(y, HEAD_DIM // 2, 1))


def _normrope_fwd(x, gain, cos_t, sin_t, ec, ee):
    w = x.shape[1]
    rs = _dot_split(lax.rsqrt(_dot_split(x * x, ec) * (1.0 / HEAD_DIM) + NORM_EPS), ee)
    nx = x * rs
    y = nx * gain
    reps = w // LANES
    out = y * jnp.tile(cos_t, (1, reps)) + _swap_halves(y) * jnp.tile(sin_t, (1, reps))
    return out, nx, rs


def _normrope_bwd(dout, nx, rs, gain, cos_t, sin_t, ec, ee):
    w = dout.shape[1]
    reps = w // LANES
    dy = dout * jnp.tile(cos_t, (1, reps)) + _swap_halves(dout * jnp.tile(sin_t, (1, reps)))
    dgain = jnp.sum(dy * nx, axis=0, keepdims=True)
    dn = dy * gain
    seg = _dot_split(_dot_split(dn * nx, ec) * (1.0 / HEAD_DIM), ee)
    return rs * (dn - nx * seg), dgain


def _pair_operand(t, group):
    chunk = t[:, (group // 2) * LANES:(group // 2 + 1) * LANES]
    low = lax.broadcasted_iota(jnp.int32, chunk.shape, 1) < HEAD_DIM
    rolled = pltpu.roll(chunk, HEAD_DIM, 1)
    return jnp.where(low, chunk, rolled) if group % 2 == 0 else jnp.where(low, rolled, chunk)


GROUP = N_Q_HEADS // N_KV_HEADS
GROUP_W = GROUP * HEAD_DIM


def _replicate_head(t, group):
    return jnp.tile(_pair_operand(t, group), (1, 2))


def _head_blocks(t):
    seg = lax.broadcasted_iota(jnp.int32, t.shape, 1) // HEAD_DIM
    return jnp.concatenate([jnp.where(seg == h, t, 0.0) for h in range(GROUP)], axis=0)


def _stack_heads(t_t, rows):
    return jnp.concatenate([t_t[:, h * rows:(h + 1) * rows] for h in range(GROUP)], axis=0)


def _head_rows(mat_t, group):
    return jnp.concatenate([mat_t[GROUP * group + h:GROUP * group + h + 1, :] for h in range(GROUP)], axis=1)


def _window_masks(blk):
    key = lax.broadcasted_iota(jnp.int32, (blk, GROUP * blk), 0)
    query = lax.broadcasted_iota(jnp.int32, (blk, GROUP * blk), 1) & (blk - 1)
    return key > query, key <= query


def _mask_window(t, before_ok, own_ok, fill):
    blk = t.shape[0] // 2
    return jnp.concatenate([jnp.where(before_ok, t[:blk], fill), jnp.where(own_ok, t[blk:], fill)], axis=0)


def _attn_fwd(z, cos_t, sin_t, q_gain_t, k_gain_t, sinks_t, *, n_seq, seq, hosted=None):
    t = n_seq * seq
    blk = WINDOW
    nb = seq // blk

    def body(q_ref, kp_ref, kc_ref, vp_ref, vc_ref, cosc_ref, sinc_ref, cosp_ref, sinp_ref, qg_ref, kg_ref, sk_ref,
             o_ref, l_ref):
        n = pl.program_id(1)
        ecq, eeq = _head_matrices(D_MODEL)
        eck, eek = _head_matrices(KV_W)
        cosc, sinc = cosc_ref[...], sinc_ref[...]
        qh, _, _ = _normrope_fwd(q_ref[...], qg_ref[...], cosc, sinc, ecq, eeq)
        qh = qh * (HEAD_DIM ** -0.5)
        kc, _, _ = _normrope_fwd(kc_ref[...], kg_ref[...], cosc, sinc, eck, eek)
        kp, _, _ = _normrope_fwd(kp_ref[...], kg_ref[...], cosp_ref[...], sinp_ref[...], eck, eek)
        kcat = jnp.concatenate([kp, kc], axis=0)
        vcat = jnp.concatenate([vp_ref[...], vc_ref[...]], axis=0)
        above, causal = _window_masks(blk)
        above = above & (n > 0)
        head_row = lax.broadcasted_iota(jnp.int32, (blk, blk), 0)
        sk_t = jnp.broadcast_to(sk_ref[...], (blk, LANES)).T
        vcat_t = vcat.T.astype(BF16)
        lmat = jnp.zeros((blk, blk), F32)
        groups = range(N_KV_HEADS)
        cols = [slice(g * GROUP_W, (g + 1) * GROUP_W) for g in groups]
        scores = [_dot_nt(_replicate_head(kcat, g).astype(BF16), _head_blocks(qh[:, cols[g]]).astype(BF16))
                  for g in groups]
        probs = []
        for g in groups:
            s = _mask_window(scores[g], above, causal, NEG_BIG)
            sink = _head_rows(sk_t, g)
            m = jnp.maximum(jnp.max(s, axis=0, keepdims=True), sink)
            e = jnp.exp(s - m)
            den = jnp.sum(e, axis=0, keepdims=True) + jnp.exp(sink - m)
            probs.append((e * (1.0 / den)).astype(BF16))
            lse = m + jnp.log(den)
            for h in range(GROUP):
                lmat = lmat + jnp.where(head_row == GROUP * g + h, lse[:, h * blk:(h + 1) * blk], 0.0)
        for g in groups:
            out_t = jnp.dot(vcat_t[g * HEAD_DIM:(g + 1) * HEAD_DIM], probs[g], preferred_element_type=F32)
            o_ref[:, cols[g]] = _stack_heads(out_t, blk).T.astype(BF16)
        l_ref[...] = lmat

    def row(b, n):
        return b * nb + n

    def prev(b, n):
        return b * nb + jnp.maximum(n - 1, 0)

    kw = KV_W
    tab_c = pl.BlockSpec((blk, LANES), lambda b, n: (n, 0))
    tab_p = pl.BlockSpec((blk, LANES), lambda b, n: (jnp.maximum(n - 1, 0), 0))
    outs, landed = _call(
        body,
        grid=(n_seq, nb),
        in_specs=[
            pl.BlockSpec((blk, D_MODEL), lambda b, n: (row(b, n), COL_RNN_END // D_MODEL)),
            pl.BlockSpec((blk, kw), lambda b, n: (prev(b, n), (COL_RNN_END + ATTN_K_AT) // kw)),
            pl.BlockSpec((blk, kw), lambda b, n: (row(b, n), (COL_RNN_END + ATTN_K_AT) // kw)),
            pl.BlockSpec((blk, kw), lambda b, n: (prev(b, n), (COL_RNN_END + ATTN_V_AT) // kw)),
            pl.BlockSpec((blk, kw), lambda b, n: (row(b, n), (COL_RNN_END + ATTN_V_AT) // kw)),
            tab_c, tab_c, tab_p, tab_p,
            pl.BlockSpec((1, D_MODEL), lambda b, n: (0, 0)),
            pl.BlockSpec((1, kw), lambda b, n: (0, 0)),
            pl.BlockSpec((1, LANES), lambda b, n: (0, 0)),
        ],
        out_specs=[pl.BlockSpec((blk, D_MODEL), lambda b, n: (row(b, n), 0)),
                   pl.BlockSpec((blk, LANES), lambda b, n: (row(b, n), 0))],
        out_shape=[jax.ShapeDtypeStruct((t, D_MODEL), BF16), jax.ShapeDtypeStruct((t, LANES), F32)],
        args=(z, z, z, z, z, cos_t, sin_t, cos_t, sin_t, q_gain_t, k_gain_t, sinks_t), name="attn_fwd",
        semantics=("parallel", "parallel"), hosted=hosted)
    return (*outs, landed) if hosted is not None else outs


def _attn_bwd(z, o, lse, do, cos_t, sin_t, q_gain_t, k_gain_t, sinks_t, *, n_seq, seq, hosted=None):
    t = n_seq * seq
    blk = WINDOW
    nb = seq // blk
    kw = KV_W
    scale = HEAD_DIM ** -0.5

    def body(qc_ref, qn_ref, kc_ref, vp_ref, vc_ref, oc_ref, on_ref, doc_ref, don_ref, lc_ref, ln_ref,
             cosc_ref, sinc_ref, cosn_ref, sinn_ref, qg_ref, kg_ref, sk_ref,
             dz_ref, vec_ref, dq_s, q_s, k_s):
        n = pl.program_id(1)
        ecq, eeq = _head_matrices(D_MODEL)
        eck, eek = _head_matrices(KV_W)
        cosc, sinc = cosc_ref[...], sinc_ref[...]
        qg, kg = qg_ref[...], kg_ref[...]
        own, other = n & 1, 1 - (n & 1)

        @pl.when(n == 0)
        def _():
            for part, value in enumerate(_normrope_fwd(qc_ref[...], qg, cosc, sinc, ecq, eeq)):
                q_s[own, part] = value
            k_s[other] = jnp.zeros((blk, kw), F32)

        for part, value in enumerate(_normrope_fwd(qn_ref[...], qg, cosn_ref[...], sinn_ref[...], ecq, eeq)):
            q_s[other, part] = value
        qhc, nqc, rsqc = q_s[own, 0], q_s[own, 1], q_s[own, 2]
        qhn = q_s[other, 0]
        khc, nkc, rskc = _normrope_fwd(kc_ref[...], kg, cosc, sinc, eck, eek)
        khp = k_s[other]
        k_s[own] = khc
        doc = doc_ref[...].astype(F32)
        don = don_ref[...].astype(F32)
        delc = _dot_split(doc * oc_ref[...].astype(F32), ecq)
        deln = _dot_split(don * on_ref[...].astype(F32), ecq)
        lc_t, ln_t, delc_t, deln_t = lc_ref[...], ln_ref[...], delc.T, deln.T
        above, causal = _window_masks(blk)
        above_c, above_n = above & (n > 0), above & (n < nb - 1)
        seg = lax.broadcasted_iota(jnp.int32, (blk, GROUP_W), 1) // HEAD_DIM
        lane = lax.broadcasted_iota(jnp.int32, (1, LANES), 1)
        sk_t = jnp.broadcast_to(sk_ref[...], (blk, LANES)).T
        dsink = jnp.zeros((1, LANES), F32)
        kcat = jnp.concatenate([khp, khc], axis=0)
        vcat = jnp.concatenate([vp_ref[...], vc_ref[...]], axis=0)
        kcat_t = kcat.T.astype(BF16)
        dkh = jnp.zeros((blk, GROUP_W), F32)
        dvh = jnp.zeros((blk, GROUP_W), F32)

        def fold_to(group, t):
            total = t + pltpu.roll(t, HEAD_DIM, 1)
            total = total + pltpu.roll(total, 2 * HEAD_DIM, 1)
            return jnp.where(seg == group, total, 0.0)

        groups = range(N_KV_HEADS)
        cols = [slice(g * GROUP_W, (g + 1) * GROUP_W) for g in groups]
        qsc, qsn = qhc * scale, qhn * scale
        qb_c = [_head_blocks(qsc[:, cols[g]]).astype(BF16) for g in groups]
        qb_n = [_head_blocks(qsn[:, cols[g]]).astype(BF16) for g in groups]
        dob_c = [_head_blocks(doc[:, cols[g]]).astype(BF16) for g in groups]
        dob_n = [_head_blocks(don[:, cols[g]]).astype(BF16) for g in groups]
        raw = []
        for g in groups:
            krep = _replicate_head(kcat, g).astype(BF16)
            vrep = _replicate_head(vcat, g).astype(BF16)
            raw.append((_dot_nt(krep, qb_c[g]), _dot_nt(vrep, dob_c[g]),
                        _dot_nt(krep[blk:], qb_n[g]), _dot_nt(vrep[blk:], dob_n[g])))
        cooked = []
        for g in groups:
            s_c, dp_c, s_n, dp_n = raw[g]
            l_row, d_row = _head_rows(lc_t, g), _head_rows(delc_t, g)
            p_c = _mask_window(jnp.exp(s_c - l_row), above_c, causal, 0.0)
            ds_c = (p_c * (dp_c - d_row)).astype(BF16)
            p_n = jnp.where(above_n, jnp.exp(s_n - _head_rows(ln_t, g)), 0.0)
            ds_n = (p_n * (dp_n - _head_rows(deln_t, g))).astype(BF16)
            cooked.append((p_c[blk:].astype(BF16), ds_c, p_n.astype(BF16), ds_n))
            p_sink = jnp.exp(_head_rows(sk_t, g) - l_row) * d_row
            for h in range(GROUP):
                dsink = dsink + jnp.where(lane == GROUP * g + h,
                                          -jnp.sum(p_sink[:, h * blk:(h + 1) * blk], axis=1, keepdims=True), 0.0)
        for g in groups:
            p_cb, ds_c, p_nb, ds_n = cooked[g]
            dq_t = jnp.dot(kcat_t[g * HEAD_DIM:(g + 1) * HEAD_DIM], ds_c, preferred_element_type=F32)
            dq_s[:, cols[g]] = _stack_heads(dq_t, blk).T * scale
            dk_rep = (jnp.dot(ds_c[blk:], qb_c[g], preferred_element_type=F32)
                      + jnp.dot(ds_n, qb_n[g], preferred_element_type=F32))
            dv_rep = (jnp.dot(p_cb, dob_c[g], preferred_element_type=F32)
                      + jnp.dot(p_nb, dob_n[g], preferred_element_type=F32))
            dkh = dkh + fold_to(g, dk_rep)
            dvh = dvh + fold_to(g, dv_rep)
        dq, dqg = _normrope_bwd(dq_s[...], nqc, rsqc, qg, cosc, sinc, ecq, eeq)
        dk, dkg = _normrope_bwd(dkh, nkc, rskc, kg, cosc, sinc, eck, eek)
        dz_ref[:, :ATTN_K_AT] = dq.astype(BF16)
        dz_ref[:, ATTN_K_AT:ATTN_V_AT] = dk.astype(BF16)
        dz_ref[:, ATTN_V_AT:] = dvh.astype(BF16)

        @pl.when(n == 0)
        def _():
            vec_ref[...] = jnp.zeros_like(vec_ref)

        vec_ref[0:1, :] += dqg
        vec_ref[1:2, 0:kw] += dkg
        vec_ref[2:3, 0:LANES] += dsink

    def row(b, n):
        return b * nb + n

    def prev(b, n):
        return b * nb + jnp.maximum(n - 1, 0)

    def nxt(b, n):
        return b * nb + jnp.minimum(n + 1, nb - 1)

    def tiles(width, col, which):
        return pl.BlockSpec((blk, width), lambda b, n: (which(b, n), col))

    def table(which):
        return pl.BlockSpec((blk, LANES), lambda b, n: (which(0, n), 0))

    outs, landed = _call(
        body,
        grid=(n_seq, nb),
        in_specs=[
            tiles(D_MODEL, COL_RNN_END // D_MODEL, row), tiles(D_MODEL, COL_RNN_END // D_MODEL, nxt),
            tiles(kw, (COL_RNN_END + ATTN_K_AT) // kw, row),
            tiles(kw, (COL_RNN_END + ATTN_V_AT) // kw, prev), tiles(kw, (COL_RNN_END + ATTN_V_AT) // kw, row),
            tiles(D_MODEL, 0, row), tiles(D_MODEL, 0, nxt),
            tiles(D_MODEL, 0, row), tiles(D_MODEL, 0, nxt),
            tiles(LANES, 0, row), tiles(LANES, 0, nxt),
            table(row), table(row), table(nxt), table(nxt),
            pl.BlockSpec((1, D_MODEL), lambda b, n: (0, 0)),
            pl.BlockSpec((1, kw), lambda b, n: (0, 0)),
            pl.BlockSpec((1, LANES), lambda b, n: (0, 0)),
        ],
        out_specs=[tiles(ATTN_W, 0, row), pl.BlockSpec((None, 8, D_MODEL), lambda b, n: (b, 0, 0))],
        out_shape=[jax.ShapeDtypeStruct((t, ATTN_W), BF16), jax.ShapeDtypeStruct((n_seq, 8, D_MODEL), F32)],
        scratch_shapes=[pltpu.VMEM((blk, D_MODEL), F32), pltpu.VMEM((2, 3, blk, D_MODEL), F32),
                        pltpu.VMEM((2, blk, kw), F32)],
        args=(z, z, z, z, z, o, o, do, do, lse, lse, cos_t, sin_t, cos_t, sin_t,
              q_gain_t, k_gain_t, sinks_t), name="attn_bwd", semantics=("arbitrary", "arbitrary"), hosted=hosted)
    return (*outs, landed) if hosted is not None else outs


def _rope_tables(seq):
    inv = ROPE_THETA ** (-jnp.arange(0, HEAD_DIM, 2, dtype=F32) / HEAD_DIM)
    ang = jnp.arange(seq, dtype=F32)[:, None] * inv[None, :]
    cos, sin = jnp.cos(ang), jnp.sin(ang)
    return jnp.tile(jnp.concatenate([cos, cos], axis=1), (1, 2)), jnp.tile(jnp.concatenate([-sin, sin], axis=1), (1, 2))


def _block_diag_tiles(w):
    per = RNN_TILE // RNN_BLOCK_W
    w4 = w.reshape(D_MODEL // RNN_TILE, per, RNN_BLOCK_W, RNN_BLOCK_W)
    eye = jnp.eye(per, dtype=w.dtype)
    dense = jnp.einsum("tpij,pq->tpiqj", w4, eye)
    return dense.reshape(D_MODEL // RNN_TILE, RNN_TILE, RNN_TILE).astype(BF16)


def _block_diag_extract(dense):
    per = RNN_TILE // RNN_BLOCK_W
    d5 = dense.reshape(D_MODEL // RNN_TILE, per, RNN_BLOCK_W, per, RNN_BLOCK_W)
    blocks = jnp.stack([d5[:, p, :, p, :] for p in range(per)], axis=1)
    return blocks.reshape(D_MODEL // RNN_BLOCK_W, RNN_BLOCK_W, RNN_BLOCK_W)


def _local_step(x, p, target, w, *, n_seq, seq, comm=None):
    w = dict(w)

    def run(tag, fn, *args, **kwargs):
        hosted = comm.host(tag) if comm is not None else None
        if hosted is None:
            return fn(*args, **kwargs)
        *outs, landed = fn(*args, hosted=hosted, **kwargs)
        comm.landed(tag, landed, w)
        return outs[0] if len(outs) == 1 else outs

    def ready(batch, grads, extra=None):
        if comm is not None:
            comm.ready(batch, grads, extra)

    cos_t, sin_t = _rope_tables(seq)
    q_gain_t = jnp.tile(w["q_gain"], (1, N_Q_HEADS))
    k_gain_t = jnp.tile(w["k_gain"], (1, N_KV_HEADS))
    sinks_t = jnp.pad(w["sinks"], ((0, 0), (0, LANES - N_Q_HEADS)))
    wrg_bd, wig_bd = _block_diag_tiles(w["w_rg"]), _block_diag_tiles(w["w_ig"])
    dims = dict(n_seq=n_seq, seq=seq)

    z, h = run("mm_in", _norm_matmul, x, w["g_mix"], w["w_in"], tm=1024, tn=IN_TOTAL // 4, name="mm_in")
    gate_tile = 512
    ga_at, gb_at = COL_ATTN_END // gate_tile, (COL_ATTN_END + D_MODEL) // gate_tile
    xc, hr, ya_in = run("rnn_fwd", _rnn_fwd, z, w["conv_w"], w["conv_b"], wrg_bd, w["b_rg"], wig_bd, w["b_ig"],
                        w["lru_lambda"], **dims)
    o, lse = run("attn_fwd", _attn_fwd, z, cos_t, sin_t, q_gain_t, k_gain_t, sinks_t, **dims)
    ya = run("mm_rnn_proj", _matmul, ya_in, w["w_rnn_proj"], mode="nn", tm=1024, tn=1024, out_dtypes=[F32],
             name="mm_rnn_proj")
    yb, merged = _matmul(
        o, w["w_attn_proj"], mode="nn", tm=1024, tn=gate_tile, out_dtypes=[F32, BF16], name="mm_attn_proj",
        epilogue=lambda acc, ga, gb, yav: (acc, _sig(ga) * yav + _sig(gb) * acc),
        extras=(z, z, ya), extra_col_blocks=(ga_at, gb_at, 0))
    def residual_then_norm(acc, res, gain):
        new = res + acc
        return new, _rmsnorm_rows(new, gain)

    x1, hm = _matmul(merged, w["w_out"], mode="nn", tm=512, tn=1024, out_dtypes=[F32, BF16], name="mm_out",
                     epilogue=residual_then_norm, extras=(x,), row_vecs=(w["g_mlp"],))
    act = _matmul(hm, w["w_up"], mode="nn", tm=1024, tn=1024, out_dtypes=[BF16], name="mm_up",
                  epilogue=lambda acc: (jnp.square(jnp.maximum(acc, 0.0)),))
    x2, hp = _matmul(act, w["w_down"], mode="nn", tm=512, tn=1024, out_dtypes=[F32, BF16], name="mm_down",
                     epilogue=residual_then_norm, extras=(x1,), row_vecs=(w["g_ple"],))
    p_bf = p.astype(BF16)
    e = _matmul(p_bf, w["w_ple_proj"], mode="nn", tm=1024, tn=1024, out_dtypes=[F32], name="mm_ple_proj")

    def loss_head(gt, x2v, ev, tgt):
        sg = _sig(gt)
        diff = x2v + ev * sg - tgt
        dx = diff * (1.0 / D_MODEL)
        return dx, dx * ev * sg * (1.0 - sg), dx * sg, jnp.sum(diff * diff, axis=0, keepdims=True)

    dx3, dgt, de, loss_row = _matmul(hp, w["w_ple_gate"], mode="nn", tm=512, tn=1024, out_dtypes=[F32, BF16, BF16],
                                     name="mm_ple_gate", epilogue=loss_head, extras=(x2, e, target), n_row_sums=1)

    g = {}
    g["w_ple_proj"] = _matmul_tn(p_bf, de, tk=PLE_DIM, tn=1024, tt=1024, name="mm_d_ple_proj",
                                 slot_cols=D_MODEL // N_DEV)
    g["w_ple_gate"] = _matmul_tn(hp, dgt, tk=1024, tn=1024, tt=1024, name="mm_d_ple_gate")
    def through_norm(dy, xv, dres, gain):
        dx, dgain = _rmsnorm_bwd_rows(dy, xv, dres, gain)
        return dx, dx, dgain

    dx2, dx2_bf, g["g_ple"] = _matmul(
        dgt, w["w_ple_gate"], mode="nt", tm=512, tn=1024, out_dtypes=[F32, BF16], name="mm_dhp",
        epilogue=through_norm, extras=(x2, dx3), row_vecs=(w["g_ple"],), n_row_sums=1)
    g["w_down"] = _matmul_tn(act, dx2_bf, tk=1024, tn=1024, tt=1024, name="mm_d_down")

    def relu_grad(dact, a):
        a = a.astype(F32)
        return (dact * (2.0 * jnp.where(a > 0.0, a * lax.rsqrt(a), 0.0)),)

    du = _matmul(dx2_bf, w["w_down"], mode="nt", tm=1024, tn=1024, out_dtypes=[BF16], name="mm_dact",
                 epilogue=relu_grad, extras=(act,))
    g["w_up"] = _matmul_tn(hm, du, tk=1024, tn=1024, tt=1024, name="mm_d_up", slot_cols=D_FF // N_DEV)
    ready(1, g)
    dx1, dx1_bf, g["g_mlp"] = run(
        "mm_dhm", _matmul, du, w["w_up"], mode="nt", tm=512, tn=1024, out_dtypes=[F32, BF16], name="mm_dhm",
        epilogue=through_norm, extras=(x1, dx2), row_vecs=(w["g_mlp"],), n_row_sums=1)
    g["w_out"] = _matmul_tn(merged, dx1_bf, tk=1024, tn=1024, tt=1024, name="mm_d_out")
    def merge_bwd(dm, ga, gb, yav, ybv):
        sa, sb = _sig(ga), _sig(gb)
        return dm * sa, dm * sb, dm * yav * sa * (1.0 - sa), dm * ybv * sb * (1.0 - sb)

    dya, dyb, dga, dgb = _matmul(dx1_bf, w["w_out"], mode="nt", tm=1024, tn=gate_tile, out_dtypes=[BF16] * 4,
                                 name="mm_dmerged", epilogue=merge_bwd, extras=(z, z, ya, yb),
                                 extra_col_blocks=(ga_at, gb_at, 0, 0))
    g["w_rnn_proj"] = _matmul_tn(ya_in, dya, tk=1024, tn=1024, tt=1024, name="mm_d_rnn_proj")
    g["w_attn_proj"] = _matmul_tn(o, dyb, tk=1024, tn=1024, tt=1024, name="mm_d_attn_proj")
    ready(2, g)
    dya_in = run("mm_dya_in", _matmul, dya, w["w_rnn_proj"], mode="nt", tm=1024, tn=1024, out_dtypes=[F32],
                 name="mm_dya_in")
    do = _matmul(dyb, w["w_attn_proj"], mode="nt", tm=1024, tn=1024, out_dtypes=[BF16], name="mm_do")
    dx_rnn, dg_rnn, dwrg_dense, dwig_dense, rnn_vec = run(
        "rnn_bwd", _rnn_bwd, dya_in, z, xc, hr, w["conv_w"], wrg_bd, w["b_rg"], wig_bd, w["b_ig"],
        w["lru_lambda"], **dims)
    dz_attn, attn_vec = run("attn_bwd", _attn_bwd, z, o, lse, do, cos_t, sin_t, q_gain_t, k_gain_t, sinks_t,
                            **dims)
    dz_parts = (dx_rnn, dg_rnn, dz_attn, dga, dgb)
    g["w_rg"] = _block_diag_extract(dwrg_dense)
    g["w_ig"] = _block_diag_extract(dwig_dense)
    g["b_rg"], g["b_ig"], g["lru_lambda"], g["conv_b"] = (rnn_vec[i:i + 1] for i in range(4))
    g["conv_w"] = rnn_vec[4:8]
    attn_vec = attn_vec[0] if n_seq == 1 else functools.reduce(jnp.add, [attn_vec[b] for b in range(n_seq)])
    g["q_gain"] = attn_vec[0].reshape(N_Q_HEADS, HEAD_DIM).sum(axis=0)[None, :]
    g["k_gain"] = attn_vec[1, :KV_W].reshape(N_KV_HEADS, HEAD_DIM).sum(axis=0)[None, :]
    g["sinks"] = attn_vec[2:3, :N_Q_HEADS]
    ready(SMALL_BATCH, g, {LOSS_ROW: loss_row})
    g["w_in"] = jnp.concatenate(
        list(run("mm_d_in_rnn", _matmul_tn_multi, h, dz_parts[:2], tt=1024, name="mm_d_in_rnn"))
        + list(run("mm_d_in_rest", _matmul_tn_multi, h, dz_parts[2:], tt=512, name="mm_d_in_rest")), axis=1)
    ready(3, g)
    w_in_attn, w_in_gate = w["w_in"][:, COL_RNN_END:COL_ATTN_END], w["w_in"][:, COL_ATTN_END:]
    windows = ((w["w_in"], (0, D_MODEL)), (w["w_in"], (D_MODEL, D_MODEL)), (w_in_attn, (0, ATTN_W)),
               (w_in_gate, (0, D_MODEL)), (w_in_gate, (D_MODEL, D_MODEL)))
    grad_x, g["g_mix"] = run(
        "mm_dh", _matmul, dz_parts, [wd[0] for wd in windows], mode="nt", tm=256, tn=1024, out_dtypes=[F32],
        name="mm_dh", b_cols=[wd[1] for wd in windows], epilogue=_rmsnorm_bwd_rows, extras=(x, dx1),
        row_vecs=(w["g_mix"],), n_row_sums=1)
    return jnp.sum(loss_row), grad_x, g


MESH_ID = pl.DeviceIdType.MESH


def _coords(index):
    return (index >> 2) & 1, (index >> 1) & 1, index & 1


def _exchange(srcs, kinds, *, name):
    n = len(srcs)
    n_peer = N_DEV - 1

    def body(*refs):
        src, dst = refs[:n], refs[n:2 * n]
        send_sems, recv_sems, local_sems = refs[2 * n:]
        me = 4 * lax.axis_index("x") + 2 * lax.axis_index("y") + lax.axis_index("c")

        def remote(i, d):
            peer = (me + d) & (N_DEV - 1)
            piece = src[i] if kinds[i] == "gather" else src[i].at[peer]
            return pltpu.make_async_remote_copy(
                src_ref=piece, dst_ref=dst[i].at[me], send_sem=send_sems.at[i * n_peer + d - 1],
                recv_sem=recv_sems.at[i * n_peer + d - 1], device_id=_coords(peer), device_id_type=MESH_ID)

        def arrival(i, d):
            sender = (me - d) & (N_DEV - 1)
            piece = src[i] if kinds[i] == "gather" else src[i].at[sender]
            return pltpu.make_async_remote_copy(
                src_ref=piece, dst_ref=dst[i].at[sender], send_sem=send_sems.at[i * n_peer + d - 1],
                recv_sem=recv_sems.at[i * n_peer + d - 1], device_id=_coords(sender), device_id_type=MESH_ID)

        own = []
        for i in range(n):
            piece = src[i] if kinds[i] == "gather" else src[i].at[me]
            own.append(pltpu.make_async_copy(piece, dst[i].at[me], local_sems.at[i]))
            own[-1].start()
        sent = [remote(i, d) for d in range(1, N_DEV) for i in range(n)]
        for cp in sent:
            cp.start()
        for d in range(1, N_DEV):
            for i in range(n):
                arrival(i, d).wait_recv()
        for cp in sent:
            cp.wait_send()
        for cp in own:
            cp.wait()

    def out_of(s, kind):
        shape = s.shape if kind == "scatter" else (N_DEV,) + s.shape
        return jax.ShapeDtypeStruct(shape, s.dtype)

    any_spec = pl.BlockSpec(memory_space=pl.ANY)
    return pl.pallas_call(
        body,
        in_specs=[any_spec] * n,
        out_specs=[any_spec] * n,
        out_shape=[out_of(s, k) for s, k in zip(srcs, kinds)],
        scratch_shapes=[pltpu.SemaphoreType.DMA((n * n_peer,)), pltpu.SemaphoreType.DMA((n * n_peer,)),
                        pltpu.SemaphoreType.DMA((n,))],
        compiler_params=pltpu.CompilerParams(has_side_effects=True),
        name=name,
    )(*srcs)


def _remote(src, dst, send_sem, recv_sem, to):
    return pltpu.make_async_remote_copy(src_ref=src, dst_ref=dst, send_sem=send_sem, recv_sem=recv_sem,
                                        device_id=to, device_id_type=MESH_ID)


def _gather_two_level(shards, *, name):
    n = len(shards)
    per = N_DEV - 1

    def body(*refs):
        src, dst = refs[:n], refs[n:2 * n]
        send_sems, recv_sems, local_sems = refs[2 * n:]
        x, y, c = lax.axis_index("x"), lax.axis_index("y"), lax.axis_index("c")
        me, sibling = (x, y, c), (x, y, 1 - c)
        chips = [(1 - x, y), (x, 1 - y), (1 - x, 1 - y)]

        def slot(pos):
            return 4 * pos[0] + 2 * pos[1] + pos[2]

        def copy(i, k, block, to, from_shard=False):
            source = src[i] if from_shard else dst[i].at[slot(block)]
            return _remote(source, dst[i].at[slot(block)], send_sems.at[i * per + k], recv_sems.at[i * per + k], to)

        mine = [pltpu.make_async_copy(src[i], dst[i].at[slot(me)], local_sems.at[i]) for i in range(n)]
        for cp in mine:
            cp.start()
        first = []
        for i in range(n):
            first.append(copy(i, 0, me, sibling, from_shard=True))
            first += [copy(i, 1 + j, me, (*chip, c), from_shard=True) for j, chip in enumerate(chips)]
        for cp in first:
            cp.start()
        passed = []
        for i in range(n):
            for j, chip in enumerate(chips):
                copy(i, 1 + j, (*chip, c), me).wait_recv()
                passed.append(copy(i, 4 + j, (*chip, c), sibling))
                passed[-1].start()
        for i in range(n):
            copy(i, 0, sibling, me).wait_recv()
            for j, chip in enumerate(chips):
                copy(i, 4 + j, (*chip, 1 - c), me).wait_recv()
        for cp in first + passed:
            cp.wait_send()
        for cp in mine:
            cp.wait()

    any_spec = pl.BlockSpec(memory_space=pl.ANY)
    return pl.pallas_call(
        body,
        in_specs=[any_spec] * n,
        out_specs=[any_spec] * n,
        out_shape=[jax.ShapeDtypeStruct((N_DEV,) + s.shape, s.dtype) for s in shards],
        scratch_shapes=[pltpu.SemaphoreType.DMA((n * per,)), pltpu.SemaphoreType.DMA((n * per,)),
                        pltpu.SemaphoreType.DMA((n,))],
        name=name,
    )(*shards)


CHIPS = N_DEV // 2


def _other_chips(x, y):
    return [(x, 1 - y), (1 - x, y), (1 - x, 1 - y)]


def _hosted_gather_first(shards):
    n = len(shards)
    per = CHIPS

    def plan(src, dst, send_sems, recv_sems, local_sems, first_sem):
        x, y, c = lax.axis_index("x"), lax.axis_index("y"), lax.axis_index("c")
        peers = [(x, y, 1 - c)] + [(*chip, c) for chip in _other_chips(x, y)]
        copies = []
        for i in range(n):
            own = pltpu.make_async_copy(src[i], dst[i].at[4 * x + 2 * y + c], local_sems.at[first_sem + i])
            copies.append(_Xfer(own.start, own.wait))
        for j, peer in enumerate(peers):
            for i in range(n):
                k = first_sem + i * per + j
                out = _remote(src[i], dst[i].at[4 * x + 2 * y + c], send_sems.at[k], recv_sems.at[k], peer)
                arrival = _remote(src[i], dst[i].at[4 * peer[0] + 2 * peer[1] + peer[2]], send_sems.at[k],
                                  recv_sems.at[k], peer)

                def wait(out=out, arrival=arrival):
                    arrival.wait_recv()
                    out.wait_send()

                copies.append(_Xfer(out.start, wait))
        return copies

    out_shape = tuple(jax.ShapeDtypeStruct((N_DEV,) + s.shape, s.dtype) for s in shards)
    return _Hosted(tuple(shards), out_shape, n * per, plan)


def _hosted_gather_second(landed):
    n = len(landed)
    per = CHIPS - 1

    def plan(src, dst, send_sems, recv_sems, local_sems, first_sem):
        x, y, c = lax.axis_index("x"), lax.axis_index("y"), lax.axis_index("c")
        copies = []
        for j, chip in enumerate(_other_chips(x, y)):
            mine, theirs = 4 * chip[0] + 2 * chip[1] + c, 4 * chip[0] + 2 * chip[1] + 1 - c
            for i in range(n):
                k = first_sem + i * per + j
                out = _remote(src[i].at[mine], dst[i].at[mine], send_sems.at[k], recv_sems.at[k], (x, y, 1 - c))
                arrival = _remote(src[i].at[theirs], dst[i].at[theirs], send_sems.at[k], recv_sems.at[k],
                                  (x, y, 1 - c))

                def wait(out=out, arrival=arrival):
                    arrival.wait_recv()
                    out.wait_send()

                copies.append(_Xfer(out.start, wait))
        return copies

    out_shape = tuple(jax.ShapeDtypeStruct(a.shape, a.dtype) for a in landed)
    return _Hosted(tuple(landed), out_shape, n * per, plan, tuple((i, i) for i in range(n)))


def _hosted_sibling_swap(arrays, sliced):
    n_sems = sum(CHIPS if s else 1 for s in sliced)

    def plan(src, dst, send_sems, recv_sems, local_sems, first_sem):
        x, y, c = lax.axis_index("x"), lax.axis_index("y"), lax.axis_index("c")
        sibling = (x, y, 1 - c)
        copies, k = [], first_sem
        for i, is_sliced in enumerate(sliced):
            pieces = [(src[i].at[2 * s + 1 - c], dst[i].at[s]) for s in range(CHIPS)] if is_sliced else [(src[i], dst[i])]
            for source, target in pieces:
                cp = _remote(source, target, send_sems.at[k], recv_sems.at[k], sibling)
                copies.append(_Xfer(cp.start, cp.wait))
                k += 1
        return copies

    out_shape = tuple(jax.ShapeDtypeStruct((CHIPS,) + a.shape[1:] if s else a.shape, a.dtype)
                      for a, s in zip(arrays, sliced))
    return _Hosted(tuple(arrays), out_shape, n_sems, plan)


def _hosted_chip_exchange(arrays, sliced):
    n = len(arrays)
    per = CHIPS - 1

    def plan(src, dst, send_sems, recv_sems, local_sems, first_sem):
        x, y, c = lax.axis_index("x"), lax.axis_index("y"), lax.axis_index("c")
        chip = 2 * x + y
        copies = []
        for i in range(n):
            own = pltpu.make_async_copy(src[i].at[chip] if sliced[i] else src[i], dst[i].at[chip],
                                        local_sems.at[first_sem + i])
            copies.append(_Xfer(own.start, own.wait))
        for d in range(1, CHIPS):
            other = chip ^ d
            to = ((other >> 1) & 1, other & 1, c)
            for i in range(n):
                k = first_sem + i * per + d - 1
                source = src[i].at[other] if sliced[i] else src[i]
                out = _remote(source, dst[i].at[chip], send_sems.at[k], recv_sems.at[k], to)
                arrival = _remote(source, dst[i].at[other], send_sems.at[k], recv_sems.at[k], to)

                def wait(out=out, arrival=arrival):
                    arrival.wait_recv()
                    out.wait_send()

                copies.append(_Xfer(out.start, wait))
        return copies

    out_shape = tuple(jax.ShapeDtypeStruct(a.shape if s else (CHIPS,) + a.shape, a.dtype)
                      for a, s in zip(arrays, sliced))
    return _Hosted(tuple(arrays), out_shape, n * per, plan)


def _add_sibling(parts, received, core, *, name):
    _, r, cols = parts.shape
    tr = min(1024, r)

    def body(core_ref, a_ref, b_ref, o_ref):
        o_ref[...] = (a_ref[...] + b_ref[...]).astype(BF16)

    grid_spec = pltpu.PrefetchScalarGridSpec(
        num_scalar_prefetch=1,
        grid=(CHIPS, r // tr),
        in_specs=[pl.BlockSpec((None, tr, cols), lambda k, i, core_ref: (2 * k + core_ref[0], i, 0)),
                  pl.BlockSpec((None, tr, cols), lambda k, i, core_ref: (k, i, 0))],
        out_specs=pl.BlockSpec((None, tr, cols), lambda k, i, core_ref: (k, i, 0)),
    )
    return pl.pallas_call(body, grid_spec=grid_spec, out_shape=jax.ShapeDtypeStruct((CHIPS, r, cols), BF16),
                          compiler_params=_params("parallel", "parallel"), name=name)(core, parts, received)


def _add_whole(a, b, *, name):
    def body(a_ref, b_ref, o_ref):
        o_ref[...] = a_ref[...] + b_ref[...]

    return pl.pallas_call(body, out_shape=jax.ShapeDtypeStruct(a.shape, F32), name=name)(a, b)


def _adamw(parts, w, m, v, *, name):
    r, c = w.shape
    n_parts = parts.shape[0]
    tr = min(512, r)
    c1 = 1.0 - ADAM_B1 ** ADAM_STEP
    c2 = 1.0 - ADAM_B2 ** ADAM_STEP

    def body(p_ref, w_ref, m_ref, v_ref, g_ref, d_ref, nm_ref, nv_ref):
        g = p_ref[0].astype(F32)
        for s in range(1, n_parts):
            g = g + p_ref[s].astype(F32)
        nm = ADAM_B1 * m_ref[...] + (1.0 - ADAM_B1) * g
        nv = ADAM_B2 * v_ref[...] + (1.0 - ADAM_B2) * (g * g)
        g_ref[...] = g
        nm_ref[...] = nm
        nv_ref[...] = nv
        d_ref[...] = -ADAM_LR * ((nm / c1) / (jnp.sqrt(nv / c2) + ADAM_EPS) + ADAM_WD * w_ref[...])

    tile = pl.BlockSpec((tr, c), lambda i: (i, 0))
    return pl.pallas_call(
        body,
        grid=(r // tr,),
        in_specs=[pl.BlockSpec((n_parts, tr, c), lambda i: (0, i, 0)), tile, tile, tile],
        out_specs=[tile] * 4,
        out_shape=[jax.ShapeDtypeStruct((r, c), F32)] * 4,
        compiler_params=_params("parallel"),
        name=name,
    )(parts, w, m, v)


BIG = ("w_in", "w_rnn_proj", "w_attn_proj", "w_out", "w_up", "w_down", "w_ple_gate", "w_ple_proj")
LOSS_ROW = "loss"
SMALL = (("conv_b", 1), ("b_rg", 1), ("b_ig", 1), ("lru_lambda", 1), ("g_mlp", 1), ("g_ple", 1),
         ("q_gain", 1), ("k_gain", 1), ("sinks", 1), (LOSS_ROW, 1), ("w_rg", 64), ("w_ig", 64))
SMALL_ROWS = 144
COL_SHARDED = ("w_in", "w_up", "w_ple_proj")
BATCHES = {1: ("w_ple_proj", "w_ple_gate", "w_down", "w_up"), 2: ("w_out", "w_rnn_proj", "w_attn_proj"),
           3: ("w_in", "conv_w")}
SMALL_BATCH = 4


def _pack_small(vals):
    rows = []
    for nm, nrow in SMALL:
        flat = vals[nm].reshape(-1).astype(F32)
        rows.append(jnp.pad(flat, (0, nrow * D_MODEL - flat.shape[0])).reshape(nrow, D_MODEL))
    used = sum(nrow for _, nrow in SMALL)
    rows.append(jnp.zeros((SMALL_ROWS - used, D_MODEL), F32))
    return jnp.concatenate(rows, axis=0)


def _unpack_small(packed, shapes):
    out, at = {}, 0
    for nm, nrow in SMALL:
        size = 1
        for s in shapes[nm]:
            size *= s
        out[nm] = packed[at:at + nrow].reshape(-1)[:size].reshape(shapes[nm])
        at += nrow
    return out


def _full_weight(name, landed):
    if name in COL_SHARDED:
        return landed.transpose(1, 0, 2).reshape(landed.shape[1], N_DEV * landed.shape[2])
    return landed.reshape(N_DEV * landed.shape[1], landed.shape[2])


def _owner_slots(name, grad):
    if name == "w_in":
        return grad.reshape(D_MODEL, N_DEV, IN_TOTAL // N_DEV).transpose(1, 0, 2)
    if name == "conv_w":
        return grad.reshape(CONV_W, N_DEV, D_MODEL // N_DEV).transpose(1, 0, 2)
    if name in COL_SHARDED:
        return grad
    return grad.reshape(N_DEV, grad.shape[0] // N_DEV, grad.shape[1])


class _StepExchanges:
    FIRST, SECOND = "first", "second"
    PROJ, OUT, PLE_GATE, UP, DOWN = (("w_rnn_proj", "w_attn_proj"), ("w_out",), ("w_ple_gate",), ("w_up",),
                                     ("w_down", "w_ple_proj"))
    GATHERS = {"mm_in": ((FIRST, PROJ), (FIRST, OUT), (FIRST, PLE_GATE)),
               "rnn_fwd": ((SECOND, PROJ), (SECOND, OUT), (SECOND, PLE_GATE), (FIRST, UP)),
               "attn_fwd": ((SECOND, UP), (FIRST, DOWN)), "mm_rnn_proj": ((SECOND, DOWN),)}
    SWAPS = {"mm_dhm": 1, "mm_dya_in": 2, "mm_d_in_rnn": SMALL_BATCH}
    CHIP_EXCHANGES = {"rnn_bwd": (1,), "attn_bwd": (2,), "mm_d_in_rest": (SMALL_BATCH,), "mm_dh": (3,)}

    def __init__(self, shards, core):
        self.shards = shards
        self.core = core
        self.parts, self.swapped, self.summed, self.half_gathered = {}, {}, {}, {}

    def ready(self, batch, grads, extra=None):
        if batch == SMALL_BATCH:
            self.parts[batch] = ([_pack_small({**grads, **extra})], [False])
            return
        arrays = [_owner_slots(nm, grads[nm]) for nm in BATCHES[batch]]
        self.parts[batch] = (arrays, [True] * len(arrays))
        if batch not in self.SWAPS.values():
            _, self.swapped[batch] = _call(
                lambda: None, grid=(1,), in_specs=[], out_specs=[], out_shape=[], args=(), name="swap_last",
                semantics=("arbitrary",), hosted=_hosted_sibling_swap(*self.parts[batch]))

    def host(self, tag):
        if tag in self.GATHERS:
            return _merge_hosted([
                _hosted_gather_first([self.shards[nm] for nm in group]) if half == self.FIRST
                else _hosted_gather_second([self.half_gathered[nm] for nm in group])
                for half, group in self.GATHERS[tag]])
        if tag in self.SWAPS:
            return _hosted_sibling_swap(*self.parts[self.SWAPS[tag]])
        if tag in self.CHIP_EXCHANGES:
            hosted = []
            for batch in self.CHIP_EXCHANGES[tag]:
                arrays, sliced = self.parts[batch]
                labels = BATCHES.get(batch, ("small",))
                sums = [_add_sibling(a, r, self.core, name="add_" + lb) if s else _add_whole(a, r, name="add_" + lb)
                        for a, r, s, lb in zip(arrays, self.swapped[batch], sliced, labels)]
                hosted.append(_hosted_chip_exchange(sums, sliced))
            return _merge_hosted(hosted)
        return None

    def landed(self, tag, landed, weights):
        if tag in self.GATHERS:
            names = [(half, nm) for half, group in self.GATHERS[tag] for nm in group]
            for (half, nm), buf in zip(names, landed):
                if half == self.FIRST:
                    self.half_gathered[nm] = buf
                else:
                    weights[nm] = _full_weight(nm, buf)
        elif tag in self.SWAPS:
            self.swapped[self.SWAPS[tag]] = landed
        else:
            at = 0
            for batch in self.CHIP_EXCHANGES[tag]:
                count = len(self.parts[batch][0])
                self.summed[batch] = landed[at:at + count]
                at += count


def kernel(x, p, g_mix, w_in, conv_w, conv_b, w_rg, b_rg, w_ig, b_ig, lru_lambda, w_rnn_proj, q_gain, k_gain, sinks, w_attn_proj, w_out, g_mlp, w_up, w_down, g_ple, w_ple_gate, w_ple_proj, loss_target, m_g_mix, m_w_in, m_conv_w, m_conv_b, m_w_rg, m_b_rg, m_w_ig, m_b_ig, m_lru_lambda, m_w_rnn_proj, m_q_gain, m_k_gain, m_sinks, m_w_attn_proj, m_w_out, m_g_mlp, m_w_up, m_w_down, m_g_ple, m_w_ple_gate, m_w_ple_proj, v_g_mix, v_w_in, v_conv_w, v_conv_b, v_w_rg, v_b_rg, v_w_ig, v_b_ig, v_lru_lambda, v_w_rnn_proj, v_q_gain, v_k_gain, v_sinks, v_w_attn_proj, v_w_out, v_g_mlp, v_w_up, v_w_down, v_g_ple, v_w_ple_gate, v_w_ple_proj):
    names = ("g_mix", "w_in", "conv_w", "conv_b", "w_rg", "b_rg", "w_ig", "b_ig", "lru_lambda", "w_rnn_proj",
             "q_gain", "k_gain", "sinks", "w_attn_proj", "w_out", "g_mlp", "w_up", "w_down", "g_ple",
             "w_ple_gate", "w_ple_proj")
    wts = dict(zip(names, (g_mix, w_in, conv_w, conv_b, w_rg, b_rg, w_ig, b_ig, lru_lambda, w_rnn_proj, q_gain,
                           k_gain, sinks, w_attn_proj, w_out, g_mlp, w_up, w_down, g_ple, w_ple_gate, w_ple_proj)))
    mom1 = dict(zip(names, (m_g_mix, m_w_in, m_conv_w, m_conv_b, m_w_rg, m_b_rg, m_w_ig, m_b_ig, m_lru_lambda,
                            m_w_rnn_proj, m_q_gain, m_k_gain, m_sinks, m_w_attn_proj, m_w_out, m_g_mlp, m_w_up,
                            m_w_down, m_g_ple, m_w_ple_gate, m_w_ple_proj)))
    mom2 = dict(zip(names, (v_g_mix, v_w_in, v_conv_w, v_conv_b, v_w_rg, v_b_rg, v_w_ig, v_b_ig, v_lru_lambda,
                            v_w_rnn_proj, v_q_gain, v_k_gain, v_sinks, v_w_attn_proj, v_w_out, v_g_mlp, v_w_up,
                            v_w_down, v_g_ple, v_w_ple_gate, v_w_ple_proj)))
    n_seq, seq, _ = x.shape
    core = lax.axis_index("c").astype(jnp.int32).reshape(1)

    shards = {nm: wts[nm][0].astype(BF16) for nm in BIG}
    w_in_all, conv_all = _gather_two_level([shards["w_in"], conv_w[0]], name="gather_w_in")
    w = {nm: wts[nm] for nm in names if nm not in BIG}
    w["w_rg"], w["w_ig"] = w_rg[0], w_ig[0]
    w["conv_w"] = conv_all.transpose(1, 0, 2).reshape(CONV_W, D_MODEL)
    w["w_in"] = _full_weight("w_in", w_in_all)
    comm = _StepExchanges(shards, core)
    loss_sum, grad_x, g = _local_step(
        x.reshape(n_seq * seq, D_MODEL), p.reshape(n_seq * seq, PLE_DIM), loss_target.reshape(n_seq * seq, D_MODEL),
        w, n_seq=n_seq, seq=seq, comm=comm)
    del loss_sum

    res = {}
    for batch, batch_names in BATCHES.items():
        for nm, summed in zip(batch_names, comm.summed[batch]):
            res[nm] = _adamw(summed, wts[nm][0], mom1[nm][0], mom2[nm][0], name="adamw_" + nm)
    g_mix_parts, = _exchange([g["g_mix"]], ["gather"], name="gather_g_mix")
    res["g_mix"] = [r[0] for r in _adamw(g_mix_parts, g_mix, m_g_mix, v_g_mix, name="adamw_g_mix")]
    small_names = [nm for nm, _ in SMALL if nm != LOSS_ROW]
    full_small = {}
    for src, key in ((wts, "w"), (mom1, "m"), (mom2, "v")):
        vals = {nm: src[nm][0] for nm in small_names}
        vals[LOSS_ROW] = jnp.zeros((1,), F32)
        full_small[key] = _pack_small(vals)
    small_res = _adamw(comm.summed[SMALL_BATCH][0],full_small["w"], full_small["m"], full_small["v"], name="adamw_small")
    shapes = {nm: wts[nm].shape[1:] for nm in small_names}
    shapes[LOSS_ROW] = (D_MODEL,)
    small_out = [_unpack_small(r, shapes) for r in small_res]
    for nm in small_names:
        res[nm] = [so[nm] for so in small_out]
    loss = jnp.sum(small_out[0][LOSS_ROW]) * (0.5 / D_MODEL)

    outs = [loss, grad_x.reshape(n_seq, seq, D_MODEL)]
    for k in range(4):
        outs.extend(res[nm][k][None] for nm in names)
    return tuple(outs)
```

```python
import functools
from typing import Callable, NamedTuple

import jax
import jax.numpy as jnp
from jax import lax
from jax.experimental import pallas as pl
from jax.experimental.pallas import tpu as pltpu

F32 = jnp.float32
BF16 = jnp.bfloat16

N_DEV = 8
D_MODEL = 1024
RNN_BLOCK_W = 64
CONV_W = 4
LRU_C = 8.0
HEAD_DIM = 64
N_Q_HEADS = 16
N_KV_HEADS = 4
KV_W = N_KV_HEADS * HEAD_DIM
WINDOW = 128
ROPE_THETA = 10000.0
D_FF = 4096
PLE_DIM = 256
NORM_EPS = 1e-6
IN_TOTAL = 5632
COL_RNN_END, COL_ATTN_END = 2048, 3584
ATTN_W = COL_ATTN_END - COL_RNN_END
ATTN_K_AT, ATTN_V_AT = 1024, 1280

ADAM_LR = 0.001
ADAM_B1 = 0.9
ADAM_B2 = 0.999
ADAM_EPS = 1e-08
ADAM_WD = 0.01
ADAM_STEP = 10

LANES = 128
SUBLANES = 8
RNN_TILE = 256
VMEM_LIMIT = 48 * 1024 * 1024
NEG_BIG = -1e30


def _params(*sem):
    return pltpu.CompilerParams(dimension_semantics=sem if sem else None, vmem_limit_bytes=VMEM_LIMIT)


def _sig(x):
    return 0.5 * jnp.tanh(0.5 * x) + 0.5


def _dot_nt(a, b):
    return lax.dot_general(a, b, (((1,), (1,)), ((), ())), preferred_element_type=F32)


def _dot_tn(a, b):
    return lax.dot_general(a, b, (((0,), (0,)), ((), ())), preferred_element_type=F32)


class _Xfer:
    def __init__(self, start, wait):
        self.start, self.wait = start, wait


class _Hosted(NamedTuple):
    srcs: tuple
    out_shape: tuple
    n_sems: int
    plan: Callable
    aliases: tuple = ()


def _merge_hosted(parts):
    parts = [p for p in parts if p is not None]
    if len(parts) <= 1:
        return parts[0] if parts else None
    src_at, dst_at, sem_at, aliases = [0], [0], [0], []
    for p in parts:
        aliases += [(i + src_at[-1], j + dst_at[-1]) for i, j in p.aliases]
        src_at.append(src_at[-1] + len(p.srcs))
        dst_at.append(dst_at[-1] + len(p.out_shape))
        sem_at.append(sem_at[-1] + p.n_sems)

    def plan(src, dst, send_sems, recv_sems, local_sems, first_sem):
        copies = []
        for k, p in enumerate(parts):
            copies += p.plan(src[src_at[k]:src_at[k + 1]], dst[dst_at[k]:dst_at[k + 1]], send_sems, recv_sems,
                             local_sems, first_sem + sem_at[k])
        return copies

    return _Hosted(tuple(a for p in parts for a in p.srcs), tuple(s for p in parts for s in p.out_shape),
                   sem_at[-1], plan, tuple(aliases))


def _call(body, *, grid, in_specs, out_specs, out_shape, args, name, semantics, scratch_shapes=(), hosted=None):
    if hosted is None:
        outs = pl.pallas_call(body, grid=grid, in_specs=list(in_specs), out_specs=list(out_specs),
                              out_shape=list(out_shape), scratch_shapes=list(scratch_shapes),
                              compiler_params=_params(*semantics), name=name)(*args)
        return list(outs), []
    counts = (len(in_specs), len(hosted.srcs), len(out_specs), len(hosted.out_shape), len(scratch_shapes), 3)

    def wrapped(*refs):
        at, groups = 0, []
        for count in counts:
            groups.append(refs[at:at + count])
            at += count
        ins, srcs, outs, dsts, scratch, sems = groups
        copies = hosted.plan(srcs, dsts, *sems, 0)
        ids = [pl.program_id(axis) for axis in range(len(grid))]
        first = functools.reduce(jnp.logical_and, [i == 0 for i in ids])
        last = functools.reduce(jnp.logical_and, [i == g - 1 for i, g in zip(ids, grid)])

        @pl.when(first)
        def _():
            for cp in copies:
                cp.start()

        body(*ins, *outs, *scratch)

        @pl.when(last)
        def _():
            for cp in copies:
                cp.wait()

    any_spec = pl.BlockSpec(memory_space=pl.ANY)
    sems = [pltpu.SemaphoreType.DMA((hosted.n_sems,))] * 3
    outs = pl.pallas_call(
        wrapped, grid=grid, in_specs=list(in_specs) + [any_spec] * counts[1],
        out_specs=list(out_specs) + [any_spec] * counts[3], out_shape=list(out_shape) + list(hosted.out_shape),
        scratch_shapes=list(scratch_shapes) + sems, compiler_params=_params(*["arbitrary"] * len(grid)),
        input_output_aliases={counts[0] + i: counts[2] + j for i, j in hosted.aliases},
        name=name)(*args, *hosted.srcs)
    return list(outs[:counts[2]]), list(outs[counts[2]:])


def _dividing_tile(n, want):
    tile = min(want, n)
    while n % tile:
        tile -= LANES
    return tile


def _matmul(a, b, *, mode, tm, tn, out_dtypes, name, epilogue=None, extras=(), hosted=None, b_cols=None,
            row_vecs=(), n_row_sums=0, extra_col_blocks=None):
    a_parts = tuple(a) if isinstance(a, (tuple, list)) else (a,)
    b_parts = tuple(b) if isinstance(b, (tuple, list)) else (b,)
    assert len(a_parts) == len(b_parts) and (mode == "nt" or len(a_parts) == 1)
    n_parts = len(a_parts)
    m = a_parts[0].shape[0]
    if b_cols is None:
        b_cols = [(0, bp.shape[1]) for bp in b_parts]
    n = b_cols[0][1] if mode == "nn" else b_parts[0].shape[0]
    tm, tn = min(tm, m), _dividing_tile(n, tn)
    n_extra = len(extras) + len(row_vecs)
    n_tiles_out = len(out_dtypes)
    assert n_row_sums == 0 or n == tn

    def body(*refs):
        a_refs, b_refs = refs[:n_parts], refs[n_parts:2 * n_parts]
        rest = refs[2 * n_parts:]
        extra_refs, out_refs = rest[:n_extra], rest[n_extra:]
        if mode == "nn":
            acc = jnp.dot(a_refs[0][...], b_refs[0][...], preferred_element_type=F32)
        else:
            acc = _dot_nt(a_refs[0][...], b_refs[0][...])
            for a_ref, b_ref in zip(a_refs[1:], b_refs[1:]):
                acc = acc + _dot_nt(a_ref[...], b_ref[...])
        res = epilogue(acc, *[e[...] for e in extra_refs]) if epilogue is not None else (acc,)
        for o_ref, r in zip(out_refs[:n_tiles_out], res):
            o_ref[...] = r.astype(o_ref.dtype)
        if n_row_sums:
            @pl.when(pl.program_id(0) == 0)
            def _():
                for o_ref in out_refs[n_tiles_out:]:
                    o_ref[...] = jnp.zeros_like(o_ref)

            for o_ref, r in zip(out_refs[n_tiles_out:], res[n_tiles_out:]):
                o_ref[...] += r

    a_specs = [pl.BlockSpec((tm, ap.shape[1]), lambda i, j: (i, 0)) for ap in a_parts]
    if mode == "nn":
        assert b_cols[0][0] % tn == 0
        first = b_cols[0][0] // tn
        b_specs = [pl.BlockSpec((b_parts[0].shape[0], tn), lambda i, j: (0, first + j))]
    else:
        assert all(at % width == 0 for at, width in b_cols)
        b_specs = [pl.BlockSpec((tn, width), functools.partial(lambda i, j, blk: (j, blk), blk=at // width))
                   for at, width in b_cols]
    tile = pl.BlockSpec((tm, tn), lambda i, j: (i, j))
    row = pl.BlockSpec((1, tn), lambda i, j: (0, j))
    extra_specs = [pl.BlockSpec((tm, tn), functools.partial(lambda i, j, first: (i, first + j), first=first))
                   for first in (extra_col_blocks or [0] * len(extras))]
    outs, landed = _call(
        body,
        grid=(m // tm, n // tn),
        in_specs=a_specs + b_specs + extra_specs + [row] * len(row_vecs),
        out_specs=[tile] * n_tiles_out + [row] * n_row_sums,
        out_shape=[jax.ShapeDtypeStruct((m, n), dt) for dt in out_dtypes]
        + [jax.ShapeDtypeStruct((1, n), F32)] * n_row_sums,
        args=(*a_parts, *b_parts, *extras, *row_vecs), name=name,
        semantics=("arbitrary" if n_row_sums else "parallel", "arbitrary"), hosted=hosted)
    if hosted is not None:
        return (*outs, landed)
    return outs[0] if len(outs) == 1 else outs


def _matmul_tn(a, b, *, tk, tn, tt, name, slot_cols=None):
    t, k = a.shape
    n = b.shape[1]
    tk, tn, tt = min(tk, k), _dividing_tile(n, tn), min(tt, t)

    def body(a_ref, b_ref, o_ref):
        @pl.when(pl.program_id(2) == 0)
        def _():
            o_ref[...] = jnp.zeros_like(o_ref)

        if slot_cols is None:
            o_ref[...] += _dot_tn(a_ref[...], b_ref[...])
        else:
            av = a_ref[...]
            for s in range(tn // slot_cols):
                o_ref[s] += _dot_tn(av, b_ref[:, s * slot_cols:(s + 1) * slot_cols])

    if slot_cols is not None:
        out_spec = pl.BlockSpec((tn // slot_cols, tk, slot_cols), lambda i, j, s: (j, i, 0))
        out_shape = jax.ShapeDtypeStruct((n // slot_cols, k, slot_cols), F32)
    else:
        out_spec = pl.BlockSpec((tk, tn), lambda i, j, s: (i, j))
        out_shape = jax.ShapeDtypeStruct((k, n), F32)
    return pl.pallas_call(
        body,
        grid=(k // tk, n // tn, t // tt),
        in_specs=[pl.BlockSpec((tt, tk), lambda i, j, s: (s, i)), pl.BlockSpec((tt, tn), lambda i, j, s: (s, j))],
        out_specs=out_spec,
        out_shape=out_shape,
        compiler_params=_params("parallel", "parallel", "arbitrary"),
        name=name,
    )(a, b)


def _matmul_tn_multi(a, bs, *, tt, name, hosted=None):
    t, k = a.shape
    tt = min(tt, t)
    n_b = len(bs)

    def body(a_ref, *refs):
        b_refs, o_refs = refs[:n_b], refs[n_b:]

        @pl.when(pl.program_id(0) == 0)
        def _():
            for o_ref in o_refs:
                o_ref[...] = jnp.zeros_like(o_ref)

        a_t = a_ref[...].T
        for b_ref, o_ref in zip(b_refs, o_refs):
            o_ref[...] += jnp.dot(a_t, b_ref[...], preferred_element_type=F32)

    outs, landed = _call(
        body,
        grid=(t // tt,),
        in_specs=[pl.BlockSpec((tt, k), lambda s: (s, 0))] + [pl.BlockSpec((tt, b.shape[1]), lambda s: (s, 0)) for b in bs],
        out_specs=[pl.BlockSpec((k, b.shape[1]), lambda s: (0, 0)) for b in bs],
        out_shape=[jax.ShapeDtypeStruct((k, b.shape[1]), F32) for b in bs],
        args=(a, *bs), name=name, semantics=("arbitrary",), hosted=hosted)
    return (*outs, landed) if hosted is not None else outs


def _rmsnorm_rows(x, g):
    return x * lax.rsqrt(jnp.mean(x * x, axis=-1, keepdims=True) + NORM_EPS) * g


def _norm_matmul(x, g, b, *, tm, tn, name, hosted=None):
    m, k = x.shape
    n = b.shape[1]
    tm, tn = min(tm, m), _dividing_tile(n, tn)

    def body(x_ref, g_ref, b_ref, z_ref, h_ref, h_s):
        @pl.when(pl.program_id(1) == 0)
        def _():
            h_s[...] = _rmsnorm_rows(x_ref[...], g_ref[...]).astype(BF16)
            h_ref[...] = h_s[...]

        z_ref[...] = jnp.dot(h_s[...], b_ref[...], preferred_element_type=F32)

    rows = pl.BlockSpec((tm, k), lambda i, j: (i, 0))
    outs, landed = _call(
        body,
        grid=(m // tm, n // tn),
        in_specs=[rows, pl.BlockSpec((1, k), lambda i, j: (0, 0)), pl.BlockSpec((k, tn), lambda i, j: (0, j))],
        out_specs=[pl.BlockSpec((tm, tn), lambda i, j: (i, j)), rows],
        out_shape=[jax.ShapeDtypeStruct((m, n), F32), jax.ShapeDtypeStruct((m, k), BF16)],
        scratch_shapes=[pltpu.VMEM((tm, k), BF16)],
        args=(x, g, b), name=name, semantics=("parallel", "arbitrary"), hosted=hosted)
    return (*outs, landed) if hosted is not None else outs


def _rmsnorm_bwd_rows(dy, x, dres, g):
    r = lax.rsqrt(jnp.mean(x * x, axis=-1, keepdims=True) + NORM_EPS)
    xr = x * r
    gy = dy * g
    dx = dres + r * (gy - xr * jnp.mean(gy * xr, axis=-1, keepdims=True))
    return dx, jnp.sum(dy * xr, axis=0, keepdims=True)


def _softplus_neg(lam):
    z = -lam
    return jnp.maximum(z, 0.0) + jnp.log1p(jnp.exp(-jnp.abs(z)))


def _neg_expm1(y, exp_half_y):
    series = -y * (1.0 + y * 0.5 * (1.0 + y * (1.0 / 3.0) * (1.0 + y * 0.25 * (1.0 + y * 0.2))))
    return jnp.where(y > -0.0625, series, 1.0 - exp_half_y * exp_half_y)


def _gelu_parts(x):
    c = 0.7978845608028654
    u = c * (x + 0.044715 * x * x * x)
    th = jnp.tanh(u)
    gel = 0.5 * x * (1.0 + th)
    dgel = 0.5 * (1.0 + th) + 0.5 * x * (1.0 - th * th) * c * (1.0 + 3.0 * 0.044715 * x * x)
    return gel, dgel


def _shift_down(v, k, rows):
    return jnp.where(rows < k, 0.0, pltpu.roll(v, k, 0))


def _shift_up(v, k, rows, n):
    return jnp.where(rows >= n - k, 0.0, pltpu.roll(v, n - k, 0))


def _scan_within_groups(a, b, *, reverse):
    shape = a.shape
    a = a.reshape(shape[0] // SUBLANES, SUBLANES, shape[1])
    b = b.reshape(a.shape)
    in_group = lax.broadcasted_iota(jnp.int32, a.shape, 1)
    for s in (1, 2, 4):
        if reverse:
            inside, shift = in_group < SUBLANES - s, SUBLANES - s
        else:
            inside, shift = in_group >= s, s
        b = b + a * jnp.where(inside, pltpu.roll(b, shift, 1), 0.0)
        a = a * jnp.where(inside, pltpu.roll(a, shift, 1), 1.0)
    return a.reshape(shape), b.reshape(shape)


def _rnn_gates(xc, wrg, brg, wig, big, lam):
    xcb = xc.astype(BF16)
    r = _sig(jnp.dot(xcb, wrg, preferred_element_type=F32) + brg)
    i = _sig(jnp.dot(xcb, wig, preferred_element_type=F32) + big)
    sp = _softplus_neg(lam)
    log_a = -LRU_C * r * sp
    a = jnp.exp(log_a)
    mult = jnp.sqrt(_neg_expm1(2.0 * log_a, a))
    return xcb, r, i, sp, a, mult


def _conv_fwd(xv, cw, cb, rows):
    return (cb + _shift_down(xv, 3, rows) * cw[0:1, :] + _shift_down(xv, 2, rows) * cw[1:2, :]
            + _shift_down(xv, 1, rows) * cw[2:3, :] + xv * cw[3:4, :])


def _rnn_fwd(z, conv_w, conv_b, wrg_bd, b_rg, wig_bd, b_ig, lam, *, n_seq, seq, hosted=None):
    t = n_seq * seq
    ct = RNN_TILE
    n_ct = D_MODEL // ct

    def body(x_ref, g_ref, cw_ref, cb_ref, wrg_ref, brg_ref, wig_ref, big_ref, lam_ref,
             xc_ref, hr_ref, ya_ref, a_s, b_s):
        rows = lax.broadcasted_iota(jnp.int32, (seq, ct), 0)
        xc = _conv_fwd(x_ref[...], cw_ref[...], cb_ref[...], rows)
        _, r, i, sp, a, mult = _rnn_gates(xc, wrg_ref[...], brg_ref[...], wig_ref[...], big_ref[...], lam_ref[...])
        a_s[...], b_s[...] = _scan_within_groups(a, mult * (i * xc), reverse=False)

        def step(j, carry):
            r0 = pl.multiple_of(j * SUBLANES, SUBLANES)
            h = b_s[pl.ds(r0, SUBLANES), :] + a_s[pl.ds(r0, SUBLANES), :] * carry
            hr_ref[pl.ds(r0, SUBLANES), :] = h
            return h[SUBLANES - 1:SUBLANES, :]

        lax.fori_loop(0, seq // SUBLANES, step, jnp.zeros((1, ct), F32), unroll=4)
        gel, _ = _gelu_parts(g_ref[...])
        xc_ref[...] = xc
        ya_ref[...] = (hr_ref[...] * gel).astype(BF16)

    vec = pl.BlockSpec((1, ct), lambda b, c: (0, c))
    gate_w = pl.BlockSpec((None, ct, ct), lambda b, c: (c, 0, 0))
    tile = pl.BlockSpec((seq, ct), lambda b, c: (b, c))
    outs, landed = _call(
        body,
        grid=(n_seq, n_ct),
        in_specs=[
            pl.BlockSpec((seq, ct), lambda b, c: (b, c)),
            pl.BlockSpec((seq, ct), lambda b, c: (b, n_ct + c)),
            pl.BlockSpec((CONV_W, ct), lambda b, c: (0, c)), vec, gate_w, vec, gate_w, vec, vec,
        ],
        out_specs=[tile, tile, tile],
        out_shape=[jax.ShapeDtypeStruct((t, D_MODEL), F32), jax.ShapeDtypeStruct((t, D_MODEL), F32),
                   jax.ShapeDtypeStruct((t, D_MODEL), BF16)],
        scratch_shapes=[pltpu.VMEM((seq, ct), F32), pltpu.VMEM((seq, ct), F32)],
        args=(z, z, conv_w, conv_b, wrg_bd, b_rg, wig_bd, b_ig, lam), name="rnn_fwd",
        semantics=("parallel", "parallel"), hosted=hosted)
    return (*outs, landed) if hosted is not None else outs


def _rnn_bwd(dya, z, xc, hr, conv_w, wrg_bd, b_rg, wig_bd, b_ig, lam, *, n_seq, seq, hosted=None):
    t = n_seq * seq
    ct = RNN_TILE
    n_ct = D_MODEL // ct

    def body(dya_ref, x_ref, g_ref, xc_ref, hr_ref, cw_ref, wrg_ref, brg_ref, wig_ref, big_ref, lam_ref,
             dx_ref, dg_ref, dwrg_ref, dwig_ref, vec_ref, a_s, d_s, g_s):
        rows = lax.broadcasted_iota(jnp.int32, (seq, ct), 0)
        xv, xc, hr, dyv = x_ref[...], xc_ref[...], hr_ref[...], dya_ref[...]
        lamv = lam_ref[...]
        gel, dgel = _gelu_parts(g_ref[...])
        dg_ref[...] = (dyv * hr * dgel).astype(BF16)
        xcb, r, i, sp, a, mult = _rnn_gates(xc, wrg_ref[...], brg_ref[...], wig_ref[...], big_ref[...], lamv)
        a_s[...], d_s[...] = _scan_within_groups(_shift_up(a, 1, rows, seq), dyv * gel, reverse=True)

        def step(k, carry):
            r0 = pl.multiple_of((seq // SUBLANES - 1 - k) * SUBLANES, SUBLANES)
            gs = d_s[pl.ds(r0, SUBLANES), :] + a_s[pl.ds(r0, SUBLANES), :] * carry
            g_s[pl.ds(r0, SUBLANES), :] = gs
            return gs[0:1, :]

        lax.fori_loop(0, seq // SUBLANES, step, jnp.zeros((1, ct), F32), unroll=4)
        gsum = g_s[...]
        gated = i * xc
        d_log_a = gsum * _shift_down(hr, 1, rows) * a - gsum * gated * (a * a / mult)
        d_gated = gsum * mult
        d_pre_r = (d_log_a * (-LRU_C) * sp) * r * (1.0 - r)
        d_pre_i = (d_gated * xc) * i * (1.0 - i)
        dprb, dpib = d_pre_r.astype(BF16), d_pre_i.astype(BF16)
        dxc = d_gated * i + _dot_nt(dprb, wrg_ref[...]) + _dot_nt(dpib, wig_ref[...])
        cw = cw_ref[...]
        dx = (dxc * cw[3:4, :] + _shift_up(dxc, 1, rows, seq) * cw[2:3, :]
              + _shift_up(dxc, 2, rows, seq) * cw[1:2, :] + _shift_up(dxc, 3, rows, seq) * cw[0:1, :])
        dx_ref[...] = dx.astype(BF16)

        @pl.when(pl.program_id(1) == 0)
        def _():
            dwrg_ref[...] = jnp.zeros_like(dwrg_ref)
            dwig_ref[...] = jnp.zeros_like(dwig_ref)
            vec_ref[...] = jnp.zeros_like(vec_ref)

        dwrg_ref[...] += _dot_tn(xcb, dprb)
        dwig_ref[...] += _dot_tn(xcb, dpib)

        def colsum(v):
            return jnp.sum(v, axis=0, keepdims=True)

        d_sp = colsum(d_log_a * (-LRU_C) * r)
        vec_ref[0:1, :] += colsum(d_pre_r)
        vec_ref[1:2, :] += colsum(d_pre_i)
        vec_ref[2:3, :] += d_sp * (-_sig(-lamv))
        vec_ref[3:4, :] += colsum(dxc)
        vec_ref[4:5, :] += colsum(dxc * _shift_down(xv, 3, rows))
        vec_ref[5:6, :] += colsum(dxc * _shift_down(xv, 2, rows))
        vec_ref[6:7, :] += colsum(dxc * _shift_down(xv, 1, rows))
        vec_ref[7:8, :] += colsum(dxc * xv)

    vec = pl.BlockSpec((1, ct), lambda c, b: (0, c))
    gate_w = pl.BlockSpec((None, ct, ct), lambda c, b: (c, 0, 0))
    tile = pl.BlockSpec((seq, ct), lambda c, b: (b, c))
    outs, landed = _call(
        body,
        grid=(n_ct, n_seq),
        in_specs=[
            tile,
            pl.BlockSpec((seq, ct), lambda c, b: (b, c)),
            pl.BlockSpec((seq, ct), lambda c, b: (b, n_ct + c)),
            tile, tile,
            pl.BlockSpec((CONV_W, ct), lambda c, b: (0, c)), gate_w, vec, gate_w, vec, vec,
        ],
        out_specs=[tile, tile, gate_w, gate_w, pl.BlockSpec((8, ct), lambda c, b: (0, c))],
        out_shape=[jax.ShapeDtypeStruct((t, D_MODEL), BF16), jax.ShapeDtypeStruct((t, D_MODEL), BF16),
                   jax.ShapeDtypeStruct((n_ct, ct, ct), F32), jax.ShapeDtypeStruct((n_ct, ct, ct), F32),
                   jax.ShapeDtypeStruct((8, D_MODEL), F32)],
        scratch_shapes=[pltpu.VMEM((seq, ct), F32)] * 3,
        args=(dya, z, z, xc, hr, conv_w, wrg_bd, b_rg, wig_bd, b_ig, lam), name="rnn_bwd",
        semantics=("parallel", "arbitrary"), hosted=hosted)
    return (*outs, landed) if hosted is not None else outs


def _split_hi_lo(x):
    hi = x.astype(BF16)
    return hi, (x - hi.astype(F32)).astype(BF16)


def _dot_split(x, m_twice):
    hi, lo = _split_hi_lo(x)
    return jnp.dot(jnp.concatenate([hi, lo], axis=1), m_twice, preferred_element_type=F32)


def _head_matrices(width):
    ec = ((lax.broadcasted_iota(jnp.int32, (2 * width, LANES), 0) & (width - 1)) // HEAD_DIM
          == lax.broadcasted_iota(jnp.int32, (2 * width, LANES), 1))
    ee = (lax.broadcasted_iota(jnp.int32, (2 * LANES, width), 1) // HEAD_DIM
          == (lax.broadcasted_iota(jnp.int32, (2 * LANES, width), 0) & (LANES - 1)))
    return jnp.where(ec, 1.0, 0.0).astype(BF16), jnp.where(ee, 1.0, 0.0).astype(BF16)


def _swap_halves(y):
    w = y.shape[1]
    first = (lax.broadcasted_iota(jnp.int32, y.shape, 1) % HEAD_DIM) < HEAD_DIM // 2
    return jnp.where(first, pltpu.roll(y, w - HEAD_DIM // 2, 1), pltpu.roll(y, HEAD_DIM // 2, 1))


def _normrope_fwd(x, gain, cos_t, sin_t, ec, ee):
    w = x.shape[1]
    rs = _dot_split(lax.rsqrt(_dot_split(x * x, ec) * (1.0 / HEAD_DIM) + NORM_EPS), ee)
    nx = x * rs
    y = nx * gain
    reps = w // LANES
    out = y * jnp.tile(cos_t, (1, reps)) + _swap_halves(y) * jnp.tile(sin_t, (1, reps))
    return out, nx, rs


def _normrope_bwd(dout, nx, rs, gain, cos_t, sin_t, ec, ee):
    w = dout.shape[1]
    reps = w // LANES
    dy = dout * jnp.tile(cos_t, (1, reps)) + _swap_halves(dout * jnp.tile(sin_t, (1, reps)))
    dgain = jnp.sum(dy * nx, axis=0, keepdims=True)
    dn = dy * gain
    seg = _dot_split(_dot_split(dn * nx, ec) * (1.0 / HEAD_DIM), ee)
    return rs * (dn - nx * seg), dgain


def _pair_operand(t, group):
    chunk = t[:, (group // 2) * LANES:(group // 2 + 1) * LANES]
    low = lax.broadcasted_iota(jnp.int32, chunk.shape, 1) < HEAD_DIM
    rolled = pltpu.roll(chunk, HEAD_DIM, 1)
    return jnp.where(low, chunk, rolled) if group % 2 == 0 else jnp.where(low, rolled, chunk)


GROUP = N_Q_HEADS // N_KV_HEADS
GROUP_W = GROUP * HEAD_DIM


def _replicate_head(t, group):
    return jnp.tile(_pair_operand(t, group), (1, 2))


def _head_blocks(t):
    seg = lax.broadcasted_iota(jnp.int32, t.shape, 1) // HEAD_DIM
    return jnp.concatenate([jnp.where(seg == h, t, 0.0) for h in range(GROUP)], axis=0)


def _stack_heads(t_t, rows):
    return jnp.concatenate([t_t[:, h * rows:(h + 1) * rows] for h in range(GROUP)], axis=0)


def _head_rows(mat_t, group):
    return jnp.concatenate([mat_t[GROUP * group + h:GROUP * group + h + 1, :] for h in range(GROUP)], axis=1)


def _window_masks(blk):
    key = lax.broadcasted_iota(jnp.int32, (blk, GROUP * blk), 0)
    query = lax.broadcasted_iota(jnp.int32, (blk, GROUP * blk), 1) & (blk - 1)
    return key > query, key <= query


def _mask_window(t, before_ok, own_ok, fill):
    blk = t.shape[0] // 2
    return jnp.concatenate([jnp.where(before_ok, t[:blk], fill), jnp.where(own_ok, t[blk:], fill)], axis=0)


def _attn_fwd(z, cos_t, sin_t, q_gain_t, k_gain_t, sinks_t, *, n_seq, seq, hosted=None):
    t = n_seq * seq
    blk = WINDOW
    nb = seq // blk

    def body(q_ref, kp_ref, kc_ref, vp_ref, vc_ref, cosc_ref, sinc_ref, cosp_ref, sinp_ref, qg_ref, kg_ref, sk_ref,
             o_ref, l_ref):
        n = pl.program_id(1)
        ecq, eeq = _head_matrices(D_MODEL)
        eck, eek = _head_matrices(KV_W)
        cosc, sinc = cosc_ref[...], sinc_ref[...]
        qh, _, _ = _normrope_fwd(q_ref[...], qg_ref[...], cosc, sinc, ecq, eeq)
        qh = qh * (HEAD_DIM ** -0.5)
        kc, _, _ = _normrope_fwd(kc_ref[...], kg_ref[...], cosc, sinc, eck, eek)
        kp, _, _ = _normrope_fwd(kp_ref[...], kg_ref[...], cosp_ref[...], sinp_ref[...], eck, eek)
        kcat = jnp.concatenate([kp, kc], axis=0)
        vcat = jnp.concatenate([vp_ref[...], vc_ref[...]], axis=0)
        above, causal = _window_masks(blk)
        above = above & (n > 0)
        head_row = lax.broadcasted_iota(jnp.int32, (blk, blk), 0)
        sk_t = jnp.broadcast_to(sk_ref[...], (blk, LANES)).T
        vcat_t = vcat.T.astype(BF16)
        lmat = jnp.zeros((blk, blk), F32)
        groups = range(N_KV_HEADS)
        cols = [slice(g * GROUP_W, (g + 1) * GROUP_W) for g in groups]
        scores = [_dot_nt(_replicate_head(kcat, g).astype(BF16), _head_blocks(qh[:, cols[g]]).astype(BF16))
                  for g in groups]
        probs = []
        for g in groups:
            s = _mask_window(scores[g], above, causal, NEG_BIG)
            sink = _head_rows(sk_t, g)
            m = jnp.maximum(jnp.max(s, axis=0, keepdims=True), sink)
            e = jnp.exp(s - m)
            den = jnp.sum(e, axis=0, keepdims=True) + jnp.exp(sink - m)
            probs.append((e * (1.0 / den)).astype(BF16))
            lse = m + jnp.log(den)
            for h in range(GROUP):
                lmat = lmat + jnp.where(head_row == GROUP * g + h, lse[:, h * blk:(h + 1) * blk], 0.0)
        for g in groups:
            out_t = jnp.dot(vcat_t[g * HEAD_DIM:(g + 1) * HEAD_DIM], probs[g], preferred_element_type=F32)
            o_ref[:, cols[g]] = _stack_heads(out_t, blk).T.astype(BF16)
        l_ref[...] = lmat

    def row(b, n):
        return b * nb + n

    def prev(b, n):
        return b * nb + jnp.maximum(n - 1, 0)

    kw = KV_W
    tab_c = pl.BlockSpec((blk, LANES), lambda b, n: (n, 0))
    tab_p = pl.BlockSpec((blk, LANES), lambda b, n: (jnp.maximum(n - 1, 0), 0))
    outs, landed = _call(
        body,
        grid=(n_seq, nb),
        in_specs=[
            pl.BlockSpec((blk, D_MODEL), lambda b, n: (row(b, n), COL_RNN_END // D_MODEL)),
            pl.BlockSpec((blk, kw), lambda b, n: (prev(b, n), (COL_RNN_END + ATTN_K_AT) // kw)),
            pl.BlockSpec((blk, kw), lambda b, n: (row(b, n), (COL_RNN_END + ATTN_K_AT) // kw)),
            pl.BlockSpec((blk, kw), lambda b, n: (prev(b, n), (COL_RNN_END + ATTN_V_AT) // kw)),
            pl.BlockSpec((blk, kw), lambda b, n: (row(b, n), (COL_RNN_END + ATTN_V_AT) // kw)),
            tab_c, tab_c, tab_p, tab_p,
            pl.BlockSpec((1, D_MODEL), lambda b, n: (0, 0)),
            pl.BlockSpec((1, kw), lambda b, n: (0, 0)),
            pl.BlockSpec((1, LANES), lambda b, n: (0, 0)),
        ],
        out_specs=[pl.BlockSpec((blk, D_MODEL), lambda b, n: (row(b, n), 0)),
                   pl.BlockSpec((blk, LANES), lambda b, n: (row(b, n), 0))],
        out_shape=[jax.ShapeDtypeStruct((t, D_MODEL), BF16), jax.ShapeDtypeStruct((t, LANES), F32)],
        args=(z, z, z, z, z, cos_t, sin_t, cos_t, sin_t, q_gain_t, k_gain_t, sinks_t), name="attn_fwd",
        semantics=("parallel", "parallel"), hosted=hosted)
    return (*outs, landed) if hosted is not None else outs


def _attn_bwd(z, o, lse, do, cos_t, sin_t, q_gain_t, k_gain_t, sinks_t, *, n_seq, seq, hosted=None):
    t = n_seq * seq
    blk = WINDOW
    nb = seq // blk
    kw = KV_W
    scale = HEAD_DIM ** -0.5

    def body(qc_ref, qn_ref, kc_ref, vp_ref, vc_ref, oc_ref, on_ref, doc_ref, don_ref, lc_ref, ln_ref,
             cosc_ref, sinc_ref, cosn_ref, sinn_ref, qg_ref, kg_ref, sk_ref,
             dz_ref, vec_ref, dq_s, q_s, k_s):
        n = pl.program_id(1)
        ecq, eeq = _head_matrices(D_MODEL)
        eck, eek = _head_matrices(KV_W)
        cosc, sinc = cosc_ref[...], sinc_ref[...]
        qg, kg = qg_ref[...], kg_ref[...]
        own, other = n & 1, 1 - (n & 1)

        @pl.when(n == 0)
        def _():
            for part, value in enumerate(_normrope_fwd(qc_ref[...], qg, cosc, sinc, ecq, eeq)):
                q_s[own, part] = value
            k_s[other] = jnp.zeros((blk, kw), F32)

        for part, value in enumerate(_normrope_fwd(qn_ref[...], qg, cosn_ref[...], sinn_ref[...], ecq, eeq)):
            q_s[other, part] = value
        qhc, nqc, rsqc = q_s[own, 0], q_s[own, 1], q_s[own, 2]
        qhn = q_s[other, 0]
        khc, nkc, rskc = _normrope_fwd(kc_ref[...], kg, cosc, sinc, eck, eek)
        khp = k_s[other]
        k_s[own] = khc
        doc = doc_ref[...].astype(F32)
        don = don_ref[...].astype(F32)
        delc = _dot_split(doc * oc_ref[...].astype(F32), ecq)
        deln = _dot_split(don * on_ref[...].astype(F32), ecq)
        lc_t, ln_t, delc_t, deln_t = lc_ref[...], ln_ref[...], delc.T, deln.T
        above, causal = _window_masks(blk)
        above_c, above_n = above & (n > 0), above & (n < nb - 1)
        seg = lax.broadcasted_iota(jnp.int32, (blk, GROUP_W), 1) // HEAD_DIM
        lane = lax.broadcasted_iota(jnp.int32, (1, LANES), 1)
        sk_t = jnp.broadcast_to(sk_ref[...], (blk, LANES)).T
        dsink = jnp.zeros((1, LANES), F32)
        kcat = jnp.concatenate([khp, khc], axis=0)
        vcat = jnp.concatenate([vp_ref[...], vc_ref[...]], axis=0)
        kcat_t = kcat.T.astype(BF16)
        dkh = jnp.zeros((blk, GROUP_W), F32)
        dvh = jnp.zeros((blk, GROUP_W), F32)

        def fold_to(group, t):
            total = t + pltpu.roll(t, HEAD_DIM, 1)
            total = total + pltpu.roll(total, 2 * HEAD_DIM, 1)
            return jnp.where(seg == group, total, 0.0)

        groups = range(N_KV_HEADS)
        cols = [slice(g * GROUP_W, (g + 1) * GROUP_W) for g in groups]
        qsc, qsn = qhc * scale, qhn * scale
        qb_c = [_head_blocks(qsc[:, cols[g]]).astype(BF16) for g in groups]
        qb_n = [_head_blocks(qsn[:, cols[g]]).astype(BF16) for g in groups]
        dob_c = [_head_blocks(doc[:, cols[g]]).astype(BF16) for g in groups]
        dob_n = [_head_blocks(don[:, cols[g]]).astype(BF16) for g in groups]
        raw = []
        for g in groups:
            krep = _replicate_head(kcat, g).astype(BF16)
            vrep = _replicate_head(vcat, g).astype(BF16)
            raw.append((_dot_nt(krep, qb_c[g]), _dot_nt(vrep, dob_c[g]),
                        _dot_nt(krep[blk:], qb_n[g]), _dot_nt(vrep[blk:], dob_n[g])))
        cooked = []
        for g in groups:
            s_c, dp_c, s_n, dp_n = raw[g]
            l_row, d_row = _head_rows(lc_t, g), _head_rows(delc_t, g)
            p_c = _mask_window(jnp.exp(s_c - l_row), above_c, causal, 0.0)
            ds_c = (p_c * (dp_c - d_row)).astype(BF16)
            p_n = jnp.where(above_n, jnp.exp(s_n - _head_rows(ln_t, g)), 0.0)
            ds_n = (p_n * (dp_n - _head_rows(deln_t, g))).astype(BF16)
            cooked.append((p_c[blk:].astype(BF16), ds_c, p_n.astype(BF16), ds_n))
            p_sink = jnp.exp(_head_rows(sk_t, g) - l_row) * d_row
            for h in range(GROUP):
                dsink = dsink + jnp.where(lane == GROUP * g + h,
                                          -jnp.sum(p_sink[:, h * blk:(h + 1) * blk], axis=1, keepdims=True), 0.0)
        for g in groups:
            p_cb, ds_c, p_nb, ds_n = cooked[g]
            dq_t = jnp.dot(kcat_t[g * HEAD_DIM:(g + 1) * HEAD_DIM], ds_c, preferred_element_type=F32)
            dq_s[:, cols[g]] = _stack_heads(dq_t, blk).T * scale
            dk_rep = (jnp.dot(ds_c[blk:], qb_c[g], preferred_element_type=F32)
                      + jnp.dot(ds_n, qb_n[g], preferred_element_type=F32))
            dv_rep = (jnp.dot(p_cb, dob_c[g], preferred_element_type=F32)
                      + jnp.dot(p_nb, dob_n[g], preferred_element_type=F32))
            dkh = dkh + fold_to(g, dk_rep)
            dvh = dvh + fold_to(g, dv_rep)
        dq, dqg = _normrope_bwd(dq_s[...], nqc, rsqc, qg, cosc, sinc, ecq, eeq)
        dk, dkg = _normrope_bwd(dkh, nkc, rskc, kg, cosc, sinc, eck, eek)
        dz_ref[:, :ATTN_K_AT] = dq.astype(BF16)
        dz_ref[:, ATTN_K_AT:ATTN_V_AT] = dk.astype(BF16)
        dz_ref[:, ATTN_V_AT:] = dvh.astype(BF16)

        @pl.when(n == 0)
        def _():
            vec_ref[...] = jnp.zeros_like(vec_ref)

        vec_ref[0:1, :] += dqg
        vec_ref[1:2, 0:kw] += dkg
        vec_ref[2:3, 0:LANES] += dsink

    def row(b, n):
        return b * nb + n

    def prev(b, n):
        return b * nb + jnp.maximum(n - 1, 0)

    def nxt(b, n):
        return b * nb + jnp.minimum(n + 1, nb - 1)

    def tiles(width, col, which):
        return pl.BlockSpec((blk, width), lambda b, n: (which(b, n), col))

    def table(which):
        return pl.BlockSpec((blk, LANES), lambda b, n: (which(0, n), 0))

    outs, landed = _call(
        body,
        grid=(n_seq, nb),
        in_specs=[
            tiles(D_MODEL, COL_RNN_END // D_MODEL, row), tiles(D_MODEL, COL_RNN_END // D_MODEL, nxt),
            tiles(kw, (COL_RNN_END + ATTN_K_AT) // kw, row),
            tiles(kw, (COL_RNN_END + ATTN_V_AT) // kw, prev), tiles(kw, (COL_RNN_END + ATTN_V_AT) // kw, row),
            tiles(D_MODEL, 0, row), tiles(D_MODEL, 0, nxt),
            tiles(D_MODEL, 0, row), tiles(D_MODEL, 0, nxt),
            tiles(LANES, 0, row), tiles(LANES, 0, nxt),
            table(row), table(row), table(nxt), table(nxt),
            pl.BlockSpec((1, D_MODEL), lambda b, n: (0, 0)),
            pl.BlockSpec((1, kw), lambda b, n: (0, 0)),
            pl.BlockSpec((1, LANES), lambda b, n: (0, 0)),
        ],
        out_specs=[tiles(ATTN_W, 0, row), pl.BlockSpec((None, 8, D_MODEL), lambda b, n: (b, 0, 0))],
        out_shape=[jax.ShapeDtypeStruct((t, ATTN_W), BF16), jax.ShapeDtypeStruct((n_seq, 8, D_MODEL), F32)],
        scratch_shapes=[pltpu.VMEM((blk, D_MODEL), F32), pltpu.VMEM((2, 3, blk, D_MODEL), F32),
                        pltpu.VMEM((2, blk, kw), F32)],
        args=(z, z, z, z, z, o, o, do, do, lse, lse, cos_t, sin_t, cos_t, sin_t,
              q_gain_t, k_gain_t, sinks_t), name="attn_bwd", semantics=("arbitrary", "arbitrary"), hosted=hosted)
    return (*outs, landed) if hosted is not None else outs


def _rope_tables(seq):
    inv = ROPE_THETA ** (-jnp.arange(0, HEAD_DIM, 2, dtype=F32) / HEAD_DIM)
    ang = jnp.arange(seq, dtype=F32)[:, None] * inv[None, :]
    cos, sin = jnp.cos(ang), jnp.sin(ang)
    return jnp.tile(jnp.concatenate([cos, cos], axis=1), (1, 2)), jnp.tile(jnp.concatenate([-sin, sin], axis=1), (1, 2))


def _block_diag_tiles(w):
    per = RNN_TILE // RNN_BLOCK_W
    w4 = w.reshape(D_MODEL // RNN_TILE, per, RNN_BLOCK_W, RNN_BLOCK_W)
    eye = jnp.eye(per, dtype=w.dtype)
    dense = jnp.einsum("tpij,pq->tpiqj", w4, eye)
    return dense.reshape(D_MODEL // RNN_TILE, RNN_TILE, RNN_TILE).astype(BF16)


def _block_diag_extract(dense):
    per = RNN_TILE // RNN_BLOCK_W
    d5 = dense.reshape(D_MODEL // RNN_TILE, per, RNN_BLOCK_W, per, RNN_BLOCK_W)
    blocks = jnp.stack([d5[:, p, :, p, :] for p in range(per)], axis=1)
    return blocks.reshape(D_MODEL // RNN_BLOCK_W, RNN_BLOCK_W, RNN_BLOCK_W)


def _local_step(x, p, target, w, *, n_seq, seq, comm=None):
    w = dict(w)

    def run(tag, fn, *args, **kwargs):
        hosted = comm.host(tag) if comm is not None else None
        if hosted is None:
            return fn(*args, **kwargs)
        *outs, landed = fn(*args, hosted=hosted, **kwargs)
        comm.landed(tag, landed, w)
        return outs[0] if len(outs) == 1 else outs

    def ready(batch, grads, extra=None):
        if comm is not None:
            comm.ready(batch, grads, extra)

    cos_t, sin_t = _rope_tables(seq)
    q_gain_t = jnp.tile(w["q_gain"], (1, N_Q_HEADS))
    k_gain_t = jnp.tile(w["k_gain"], (1, N_KV_HEADS))
    sinks_t = jnp.pad(w["sinks"], ((0, 0), (0, LANES - N_Q_HEADS)))
    wrg_bd, wig_bd = _block_diag_tiles(w["w_rg"]), _block_diag_tiles(w["w_ig"])
    dims = dict(n_seq=n_seq, seq=seq)

    z, h = run("mm_in", _norm_matmul, x, w["g_mix"], w["w_in"], tm=1024, tn=IN_TOTAL // 4, name="mm_in")
    gate_tile = 512
    ga_at, gb_at = COL_ATTN_END // gate_tile, (COL_ATTN_END + D_MODEL) // gate_tile
    xc, hr, ya_in = run("rnn_fwd", _rnn_fwd, z, w["conv_w"], w["conv_b"], wrg_bd, w["b_rg"], wig_bd, w["b_ig"],
                        w["lru_lambda"], **dims)
    o, lse = run("attn_fwd", _attn_fwd, z, cos_t, sin_t, q_gain_t, k_gain_t, sinks_t, **dims)
    ya = run("mm_rnn_proj", _matmul, ya_in, w["w_rnn_proj"], mode="nn", tm=1024, tn=1024, out_dtypes=[F32],
             name="mm_rnn_proj")
    yb, merged = _matmul(
        o, w["w_attn_proj"], mode="nn", tm=1024, tn=gate_tile, out_dtypes=[F32, BF16], name="mm_attn_proj",
        epilogue=lambda acc, ga, gb, yav: (acc, _sig(ga) * yav + _sig(gb) * acc),
        extras=(z, z, ya), extra_col_blocks=(ga_at, gb_at, 0))
    def residual_then_norm(acc, res, gain):
        new = res + acc
        return new, _rmsnorm_rows(new, gain)

    x1, hm = _matmul(merged, w["w_out"], mode="nn", tm=512, tn=1024, out_dtypes=[F32, BF16], name="mm_out",
                     epilogue=residual_then_norm, extras=(x,), row_vecs=(w["g_mlp"],))
    act = _matmul(hm, w["w_up"], mode="nn", tm=1024, tn=1024, out_dtypes=[BF16], name="mm_up",
                  epilogue=lambda acc: (jnp.square(jnp.maximum(acc, 0.0)),))
    x2, hp = _matmul(act, w["w_down"], mode="nn", tm=512, tn=1024, out_dtypes=[F32, BF16], name="mm_down",
                     epilogue=residual_then_norm, extras=(x1,), row_vecs=(w["g_ple"],))
    p_bf = p.astype(BF16)
    e = _matmul(p_bf, w["w_ple_proj"], mode="nn", tm=1024, tn=1024, out_dtypes=[F32], name="mm_ple_proj")

    def loss_head(gt, x2v, ev, tgt):
        sg = _sig(gt)
        diff = x2v + ev * sg - tgt
        dx = diff * (1.0 / D_MODEL)
        return dx, dx * ev * sg * (1.0 - sg), dx * sg, jnp.sum(diff * diff, axis=0, keepdims=True)

    dx3, dgt, de, loss_row = _matmul(hp, w["w_ple_gate"], mode="nn", tm=512, tn=1024, out_dtypes=[F32, BF16, BF16],
                                     name="mm_ple_gate", epilogue=loss_head, extras=(x2, e, target), n_row_sums=1)

    g = {}
    g["w_ple_proj"] = _matmul_tn(p_bf, de, tk=PLE_DIM, tn=1024, tt=1024, name="mm_d_ple_proj",
                                 slot_cols=D_MODEL // N_DEV)
    g["w_ple_gate"] = _matmul_tn(hp, dgt, tk=1024, tn=1024, tt=1024, name="mm_d_ple_gate")
    def through_norm(dy, xv, dres, gain):
        dx, dgain = _rmsnorm_bwd_rows(dy, xv, dres, gain)
        return dx, dx, dgain

    dx2, dx2_bf, g["g_ple"] = _matmul(
        dgt, w["w_ple_gate"], mode="nt", tm=512, tn=1024, out_dtypes=[F32, BF16], name="mm_dhp",
        epilogue=through_norm, extras=(x2, dx3), row_vecs=(w["g_ple"],), n_row_sums=1)
    g["w_down"] = _matmul_tn(act, dx2_bf, tk=1024, tn=1024, tt=1024, name="mm_d_down")

    def relu_grad(dact, a):
        a = a.astype(F32)
        return (dact * (2.0 * jnp.where(a > 0.0, a * lax.rsqrt(a), 0.0)),)

    du = _matmul(dx2_bf, w["w_down"], mode="nt", tm=1024, tn=1024, out_dtypes=[BF16], name="mm_dact",
                 epilogue=relu_grad, extras=(act,))
    g["w_up"] = _matmul_tn(hm, du, tk=1024, tn=1024, tt=1024, name="mm_d_up", slot_cols=D_FF // N_DEV)
    ready(1, g)
    dx1, dx1_bf, g["g_mlp"] = run(
        "mm_dhm", _matmul, du, w["w_up"], mode="nt", tm=512, tn=1024, out_dtypes=[F32, BF16], name="mm_dhm",
        epilogue=through_norm, extras=(x1, dx2), row_vecs=(w["g_mlp"],), n_row_sums=1)
    g["w_out"] = _matmul_tn(merged, dx1_bf, tk=1024, tn=1024, tt=1024, name="mm_d_out")
    def merge_bwd(dm, ga, gb, yav, ybv):
        sa, sb = _sig(ga), _sig(gb)
        return dm * sa, dm * sb, dm * yav * sa * (1.0 - sa), dm * ybv * sb * (1.0 - sb)

    dya, dyb, dga, dgb = _matmul(dx1_bf, w["w_out"], mode="nt", tm=1024, tn=gate_tile, out_dtypes=[BF16] * 4,
                                 name="mm_dmerged", epilogue=merge_bwd, extras=(z, z, ya, yb),
                                 extra_col_blocks=(ga_at, gb_at, 0, 0))
    g["w_rnn_proj"] = _matmul_tn(ya_in, dya, tk=1024, tn=1024, tt=1024, name="mm_d_rnn_proj")
    g["w_attn_proj"] = _matmul_tn(o, dyb, tk=1024, tn=1024, tt=1024, name="mm_d_attn_proj")
    ready(2, g)
    dya_in = run("mm_dya_in", _matmul, dya, w["w_rnn_proj"], mode="nt", tm=1024, tn=1024, out_dtypes=[F32],
                 name="mm_dya_in")
    do = _matmul(dyb, w["w_attn_proj"], mode="nt", tm=1024, tn=1024, out_dtypes=[BF16], name="mm_do")
    dx_rnn, dg_rnn, dwrg_dense, dwig_dense, rnn_vec = run(
        "rnn_bwd", _rnn_bwd, dya_in, z, xc, hr, w["conv_w"], wrg_bd, w["b_rg"], wig_bd, w["b_ig"],
        w["lru_lambda"], **dims)
    dz_attn, attn_vec = run("attn_bwd", _attn_bwd, z, o, lse, do, cos_t, sin_t, q_gain_t, k_gain_t, sinks_t,
                            **dims)
    dz_parts = (dx_rnn, dg_rnn, dz_attn, dga, dgb)
    g["w_rg"] = _block_diag_extract(dwrg_dense)
    g["w_ig"] = _block_diag_extract(dwig_dense)
    g["b_rg"], g["b_ig"], g["lru_lambda"], g["conv_b"] = (rnn_vec[i:i + 1] for i in range(4))
    g["conv_w"] = rnn_vec[4:8]
    attn_vec = attn_vec[0] if n_seq == 1 else functools.reduce(jnp.add, [attn_vec[b] for b in range(n_seq)])
    g["q_gain"] = attn_vec[0].reshape(N_Q_HEADS, HEAD_DIM).sum(axis=0)[None, :]
    g["k_gain"] = attn_vec[1, :KV_W].reshape(N_KV_HEADS, HEAD_DIM).sum(axis=0)[None, :]
    g["sinks"] = attn_vec[2:3, :N_Q_HEADS]
    ready(SMALL_BATCH, g, {LOSS_ROW: loss_row})
    g["w_in"] = jnp.concatenate(
        list(run("mm_d_in_rnn", _matmul_tn_multi, h, dz_parts[:2], tt=1024, name="mm_d_in_rnn"))
        + list(run("mm_d_in_rest", _matmul_tn_multi, h, dz_parts[2:], tt=512, name="mm_d_in_rest")), axis=1)
    ready(3, g)
    w_in_attn, w_in_gate = w["w_in"][:, COL_RNN_END:COL_ATTN_END], w["w_in"][:, COL_ATTN_END:]
    windows = ((w["w_in"], (0, D_MODEL)), (w["w_in"], (D_MODEL, D_MODEL)), (w_in_attn, (0, ATTN_W)),
               (w_in_gate, (0, D_MODEL)), (w_in_gate, (D_MODEL, D_MODEL)))
    grad_x, g["g_mix"] = run(
        "mm_dh", _matmul, dz_parts, [wd[0] for wd in windows], mode="nt", tm=256, tn=1024, out_dtypes=[F32],
        name="mm_dh", b_cols=[wd[1] for wd in windows], epilogue=_rmsnorm_bwd_rows, extras=(x, dx1),
        row_vecs=(w["g_mix"],), n_row_sums=1)
    return jnp.sum(loss_row), grad_x, g


MESH_ID = pl.DeviceIdType.MESH


def _coords(index):
    return (index >> 2) & 1, (index >> 1) & 1, index & 1


def _exchange(srcs, kinds, *, name):
    n = len(srcs)
    n_peer = N_DEV - 1

    def body(*refs):
        src, dst = refs[:n], refs[n:2 * n]
        send_sems, recv_sems, local_sems = refs[2 * n:]
        me = 4 * lax.axis_index("x") + 2 * lax.axis_index("y") + lax.axis_index("c")

        def remote(i, d):
            peer = (me + d) & (N_DEV - 1)
            piece = src[i] if kinds[i] == "gather" else src[i].at[peer]
            return pltpu.make_async_remote_copy(
                src_ref=piece, dst_ref=dst[i].at[me], send_sem=send_sems.at[i * n_peer + d - 1],
                recv_sem=recv_sems.at[i * n_peer + d - 1], device_id=_coords(peer), device_id_type=MESH_ID)

        def arrival(i, d):
            sender = (me - d) & (N_DEV - 1)
            piece = src[i] if kinds[i] == "gather" else src[i].at[sender]
            return pltpu.make_async_remote_copy(
                src_ref=piece, dst_ref=dst[i].at[sender], send_sem=send_sems.at[i * n_peer + d - 1],
                recv_sem=recv_sems.at[i * n_peer + d - 1], device_id=_coords(sender), device_id_type=MESH_ID)

        own = []
        for i in range(n):
            piece = src[i] if kinds[i] == "gather" else src[i].at[me]
            own.append(pltpu.make_async_copy(piece, dst[i].at[me], local_sems.at[i]))
            own[-1].start()
        sent = [remote(i, d) for d in range(1, N_DEV) for i in range(n)]
        for cp in sent:
            cp.start()
        for d in range(1, N_DEV):
            for i in range(n):
                arrival(i, d).wait_recv()
        for cp in sent:
            cp.wait_send()
        for cp in own:
            cp.wait()

    def out_of(s, kind):
        shape = s.shape if kind == "scatter" else (N_DEV,) + s.shape
        return jax.ShapeDtypeStruct(shape, s.dtype)

    any_spec = pl.BlockSpec(memory_space=pl.ANY)
    return pl.pallas_call(
        body,
        in_specs=[any_spec] * n,
        out_specs=[any_spec] * n,
        out_shape=[out_of(s, k) for s, k in zip(srcs, kinds)],
        scratch_shapes=[pltpu.SemaphoreType.DMA((n * n_peer,)), pltpu.SemaphoreType.DMA((n * n_peer,)),
                        pltpu.SemaphoreType.DMA((n,))],
        compiler_params=pltpu.CompilerParams(has_side_effects=True),
        name=name,
    )(*srcs)


def _remote(src, dst, send_sem, recv_sem, to):
    return pltpu.make_async_remote_copy(src_ref=src, dst_ref=dst, send_sem=send_sem, recv_sem=recv_sem,
                                        device_id=to, device_id_type=MESH_ID)


def _gather_two_level(shards, *, name):
    n = len(shards)
    per = N_DEV - 1

    def body(*refs):
        src, dst = refs[:n], refs[n:2 * n]
        send_sems, recv_sems, local_sems = refs[2 * n:]
        x, y, c = lax.axis_index("x"), lax.axis_index("y"), lax.axis_index("c")
        me, sibling = (x, y, c), (x, y, 1 - c)
        chips = [(1 - x, y), (x, 1 - y), (1 - x, 1 - y)]

        def slot(pos):
            return 4 * pos[0] + 2 * pos[1] + pos[2]

        def copy(i, k, block, to, from_shard=False):
            source = src[i] if from_shard else dst[i].at[slot(block)]
            return _remote(source, dst[i].at[slot(block)], send_sems.at[i * per + k], recv_sems.at[i * per + k], to)

        mine = [pltpu.make_async_copy(src[i], dst[i].at[slot(me)], local_sems.at[i]) for i in range(n)]
        for cp in mine:
            cp.start()
        first = []
        for i in range(n):
            first.append(copy(i, 0, me, sibling, from_shard=True))
            first += [copy(i, 1 + j, me, (*chip, c), from_shard=True) for j, chip in enumerate(chips)]
        for cp in first:
            cp.start()
        passed = []
        for i in range(n):
            for j, chip in enumerate(chips):
                copy(i, 1 + j, (*chip, c), me).wait_recv()
                passed.append(copy(i, 4 + j, (*chip, c), sibling))
                passed[-1].start()
        for i in range(n):
            copy(i, 0, sibling, me).wait_recv()
            for j, chip in enumerate(chips):
                copy(i, 4 + j, (*chip, 1 - c), me).wait_recv()
        for cp in first + passed:
            cp.wait_send()
        for cp in mine:
            cp.wait()

    any_spec = pl.BlockSpec(memory_space=pl.ANY)
    return pl.pallas_call(
        body,
        in_specs=[any_spec] * n,
        out_specs=[any_spec] * n,
        out_shape=[jax.ShapeDtypeStruct((N_DEV,) + s.shape, s.dtype) for s in shards],
        scratch_shapes=[pltpu.SemaphoreType.DMA((n * per,)), pltpu.SemaphoreType.DMA((n * per,)),
                        pltpu.SemaphoreType.DMA((n,))],
        name=name,
    )(*shards)


CHIPS = N_DEV // 2


def _other_chips(x, y):
    return [(x, 1 - y), (1 - x, y), (1 - x, 1 - y)]


def _hosted_gather_first(shards):
    n = len(shards)
    per = CHIPS

    def plan(src, dst, send_sems, recv_sems, local_sems, first_sem):
        x, y, c = lax.axis_index("x"), lax.axis_index("y"), lax.axis_index("c")
        peers = [(x, y, 1 - c)] + [(*chip, c) for chip in _other_chips(x, y)]
        copies = []
        for i in range(n):
            own = pltpu.make_async_copy(src[i], dst[i].at[4 * x + 2 * y + c], local_sems.at[first_sem + i])
            copies.append(_Xfer(own.start, own.wait))
        for j, peer in enumerate(peers):
            for i in range(n):
                k = first_sem + i * per + j
                out = _remote(src[i], dst[i].at[4 * x + 2 * y + c], send_sems.at[k], recv_sems.at[k], peer)
                arrival = _remote(src[i], dst[i].at[4 * peer[0] + 2 * peer[1] + peer[2]], send_sems.at[k],
                                  recv_sems.at[k], peer)

                def wait(out=out, arrival=arrival):
                    arrival.wait_recv()
                    out.wait_send()

                copies.append(_Xfer(out.start, wait))
        return copies

    out_shape = tuple(jax.ShapeDtypeStruct((N_DEV,) + s.shape, s.dtype) for s in shards)
    return _Hosted(tuple(shards), out_shape, n * per, plan)


def _hosted_gather_second(landed):
    n = len(landed)
    per = CHIPS - 1

    def plan(src, dst, send_sems, recv_sems, local_sems, first_sem):
        x, y, c = lax.axis_index("x"), lax.axis_index("y"), lax.axis_index("c")
        copies = []
        for j, chip in enumerate(_other_chips(x, y)):
            mine, theirs = 4 * chip[0] + 2 * chip[1] + c, 4 * chip[0] + 2 * chip[1] + 1 - c
            for i in range(n):
                k = first_sem + i * per + j
                out = _remote(src[i].at[mine], dst[i].at[mine], send_sems.at[k], recv_sems.at[k], (x, y, 1 - c))
                arrival = _remote(src[i].at[theirs], dst[i].at[theirs], send_sems.at[k], recv_sems.at[k],
                                  (x, y, 1 - c))

                def wait(out=out, arrival=arrival):
                    arrival.wait_recv()
                    out.wait_send()

                copies.append(_Xfer(out.start, wait))
        return copies

    out_shape = tuple(jax.ShapeDtypeStruct(a.shape, a.dtype) for a in landed)
    return _Hosted(tuple(landed), out_shape, n * per, plan, tuple((i, i) for i in range(n)))


def _hosted_sibling_swap(arrays, sliced):
    n_sems = sum(CHIPS if s else 1 for s in sliced)

    def plan(src, dst, send_sems, recv_sems, local_sems, first_sem):
        x, y, c = lax.axis_index("x"), lax.axis_index("y"), lax.axis_index("c")
        sibling = (x, y, 1 - c)
        copies, k = [], first_sem
        for i, is_sliced in enumerate(sliced):
            pieces = [(src[i].at[2 * s + 1 - c], dst[i].at[s]) for s in range(CHIPS)] if is_sliced else [(src[i], dst[i])]
            for source, target in pieces:
                cp = _remote(source, target, send_sems.at[k], recv_sems.at[k], sibling)
                copies.append(_Xfer(cp.start, cp.wait))
                k += 1
        return copies

    out_shape = tuple(jax.ShapeDtypeStruct((CHIPS,) + a.shape[1:] if s else a.shape, a.dtype)
                      for a, s in zip(arrays, sliced))
    return _Hosted(tuple(arrays), out_shape, n_sems, plan)


def _hosted_chip_exchange(arrays, sliced):
    n = len(arrays)
    per = CHIPS - 1

    def plan(src, dst, send_sems, recv_sems, local_sems, first_sem):
        x, y, c = lax.axis_index("x"), lax.axis_index("y"), lax.axis_index("c")
        chip = 2 * x + y
        copies = []
        for i in range(n):
            own = pltpu.make_async_copy(src[i].at[chip] if sliced[i] else src[i], dst[i].at[chip],
                                        local_sems.at[first_sem + i])
            copies.append(_Xfer(own.start, own.wait))
        for d in range(1, CHIPS):
            other = chip ^ d
            to = ((other >> 1) & 1, other & 1, c)
            for i in range(n):
                k = first_sem + i * per + d - 1
                source = src[i].at[other] if sliced[i] else src[i]
                out = _remote(source, dst[i].at[chip], send_sems.at[k], recv_sems.at[k], to)
                arrival = _remote(source, dst[i].at[other], send_sems.at[k], recv_sems.at[k], to)

                def wait(out=out, arrival=arrival):
                    arrival.wait_recv()
                    out.wait_send()

                copies.append(_Xfer(out.start, wait))
        return copies

    out_shape = tuple(jax.ShapeDtypeStruct(a.shape if s else (CHIPS,) + a.shape, a.dtype)
                      for a, s in zip(arrays, sliced))
    return _Hosted(tuple(arrays), out_shape, n * per, plan)


def _add_sibling(parts, received, core, *, name):
    _, r, cols = parts.shape
    tr = min(1024, r)

    def body(core_ref, a_ref, b_ref, o_ref):
        o_ref[...] = (a_ref[...] + b_ref[...]).astype(BF16)

    grid_spec = pltpu.PrefetchScalarGridSpec(
        num_scalar_prefetch=1,
        grid=(CHIPS, r // tr),
        in_specs=[pl.BlockSpec((None, tr, cols), lambda k, i, core_ref: (2 * k + core_ref[0], i, 0)),
                  pl.BlockSpec((None, tr, cols), lambda k, i, core_ref: (k, i, 0))],
        out_specs=pl.BlockSpec((None, tr, cols), lambda k, i, core_ref: (k, i, 0)),
    )
    return pl.pallas_call(body, grid_spec=grid_spec, out_shape=jax.ShapeDtypeStruct((CHIPS, r, cols), BF16),
                          compiler_params=_params("parallel", "parallel"), name=name)(core, parts, received)


def _add_whole(a, b, *, name):
    def body(a_ref, b_ref, o_ref):
        o_ref[...] = a_ref[...] + b_ref[...]

    return pl.pallas_call(body, out_shape=jax.ShapeDtypeStruct(a.shape, F32), name=name)(a, b)


def _adamw(parts, w, m, v, *, name):
    r, c = w.shape
    n_parts = parts.shape[0]
    tr = min(512, r)
    c1 = 1.0 - ADAM_B1 ** ADAM_STEP
    c2 = 1.0 - ADAM_B2 ** ADAM_STEP

    def body(p_ref, w_ref, m_ref, v_ref, g_ref, d_ref, nm_ref, nv_ref):
        g = p_ref[0].astype(F32)
        for s in range(1, n_parts):
            g = g + p_ref[s].astype(F32)
        nm = ADAM_B1 * m_ref[...] + (1.0 - ADAM_B1) * g
        nv = ADAM_B2 * v_ref[...] + (1.0 - ADAM_B2) * (g * g)
        g_ref[...] = g
        nm_ref[...] = nm
        nv_ref[...] = nv
        d_ref[...] = -ADAM_LR * ((nm / c1) / (jnp.sqrt(nv / c2) + ADAM_EPS) + ADAM_WD * w_ref[...])

    tile = pl.BlockSpec((tr, c), lambda i: (i, 0))
    return pl.pallas_call(
        body,
        grid=(r // tr,),
        in_specs=[pl.BlockSpec((n_parts, tr, c), lambda i: (0, i, 0)), tile, tile, tile],
        out_specs=[tile] * 4,
        out_shape=[jax.ShapeDtypeStruct((r, c), F32)] * 4,
        compiler_params=_params("parallel"),
        name=name,
    )(parts, w, m, v)


BIG = ("w_in", "w_rnn_proj", "w_attn_proj", "w_out", "w_up", "w_down", "w_ple_gate", "w_ple_proj")
LOSS_ROW = "loss"
SMALL = (("conv_b", 1), ("b_rg", 1), ("b_ig", 1), ("lru_lambda", 1), ("g_mlp", 1), ("g_ple", 1),
         ("q_gain", 1), ("k_gain", 1), ("sinks", 1), (LOSS_ROW, 1), ("w_rg", 64), ("w_ig", 64))
SMALL_ROWS = 144
COL_SHARDED = ("w_in", "w_up", "w_ple_proj")
BATCHES = {1: ("w_ple_proj", "w_ple_gate", "w_down", "w_up"), 2: ("w_out", "w_rnn_proj", "w_attn_proj"),
           3: ("w_in", "conv_w")}
SMALL_BATCH = 4


def _pack_small(vals):
    rows = []
    for nm, nrow in SMALL:
        flat = vals[nm].reshape(-1).astype(F32)
        rows.append(jnp.pad(flat, (0, nrow * D_MODEL - flat.shape[0])).reshape(nrow, D_MODEL))
    used = sum(nrow for _, nrow in SMALL)
    rows.append(jnp.zeros((SMALL_ROWS - used, D_MODEL), F32))
    return jnp.concatenate(rows, axis=0)


def _unpack_small(packed, shapes):
    out, at = {}, 0
    for nm, nrow in SMALL:
        size = 1
        for s in shapes[nm]:
            size *= s
        out[nm] = packed[at:at + nrow].reshape(-1)[:size].reshape(shapes[nm])
        at += nrow
    return out


def _full_weight(name, landed):
    if name in COL_SHARDED:
        return landed.transpose(1, 0, 2).reshape(landed.shape[1], N_DEV * landed.shape[2])
    return landed.reshape(N_DEV * landed.shape[1], landed.shape[2])


def _owner_slots(name, grad):
    if name == "w_in":
        return grad.reshape(D_MODEL, N_DEV, IN_TOTAL // N_DEV).transpose(1, 0, 2)
    if name == "conv_w":
        return grad.reshape(CONV_W, N_DEV, D_MODEL // N_DEV).transpose(1, 0, 2)
    if name in COL_SHARDED:
        return grad
    return grad.reshape(N_DEV, grad.shape[0] // N_DEV, grad.shape[1])


class _StepExchanges:
    FIRST, SECOND = "first", "second"
    PROJ, OUT, PLE_GATE, UP, DOWN = (("w_rnn_proj", "w_attn_proj"), ("w_out",), ("w_ple_gate",), ("w_up",),
                                     ("w_down", "w_ple_proj"))
    GATHERS = {"mm_in": ((FIRST, PROJ), (FIRST, OUT), (FIRST, PLE_GATE)),
               "rnn_fwd": ((SECOND, PROJ), (SECOND, OUT), (SECOND, PLE_GATE), (FIRST, UP)),
               "attn_fwd": ((SECOND, UP), (FIRST, DOWN)), "mm_rnn_proj": ((SECOND, DOWN),)}
    SWAPS = {"mm_dhm": 1, "mm_dya_in": 2, "mm_d_in_rnn": SMALL_BATCH}
    CHIP_EXCHANGES = {"rnn_bwd": ((1, (0, 1, 2)),), "attn_bwd": ((1, (3,)), (2, None)),
                      "mm_d_in_rest": ((SMALL_BATCH, None),), "mm_dh": ((3, None),)}

    def __init__(self, shards, core):
        self.shards = shards
        self.core = core
        self.parts, self.swapped, self.summed, self.half_gathered = {}, {}, {}, {}

    def ready(self, batch, grads, extra=None):
        if batch == SMALL_BATCH:
            self.parts[batch] = ([_pack_small({**grads, **extra})], [False])
            return
        arrays = [_owner_slots(nm, grads[nm]) for nm in BATCHES[batch]]
        self.parts[batch] = (arrays, [True] * len(arrays))
        if batch not in self.SWAPS.values():
            _, self.swapped[batch] = _call(
                lambda: None, grid=(1,), in_specs=[], out_specs=[], out_shape=[], args=(), name="swap_last",
                semantics=("arbitrary",), hosted=_hosted_sibling_swap(*self.parts[batch]))

    def host(self, tag):
        if tag in self.GATHERS:
            return _merge_hosted([
                _hosted_gather_first([self.shards[nm] for nm in group]) if half == self.FIRST
                else _hosted_gather_second([self.half_gathered[nm] for nm in group])
                for half, group in self.GATHERS[tag]])
        if tag in self.SWAPS:
            return _hosted_sibling_swap(*self.parts[self.SWAPS[tag]])
        if tag in self.CHIP_EXCHANGES:
            hosted = []
            for batch, members in self._exchange_members(tag):
                arrays, sliced = self.parts[batch]
                labels = BATCHES.get(batch, ("small",))
                sums = [_add_sibling(arrays[i], self.swapped[batch][i], self.core, name="add_" + labels[i])
                        if sliced[i] else _add_whole(arrays[i], self.swapped[batch][i], name="add_" + labels[i])
                        for i in members]
                hosted.append(_hosted_chip_exchange(sums, [sliced[i] for i in members]))
            return _merge_hosted(hosted)
        return None

    def _exchange_members(self, tag):
        return [(batch, members if members is not None else tuple(range(len(self.parts[batch][0]))))
                for batch, members in self.CHIP_EXCHANGES[tag]]

    def landed(self, tag, landed, weights):
        if tag in self.GATHERS:
            names = [(half, nm) for half, group in self.GATHERS[tag] for nm in group]
            for (half, nm), buf in zip(names, landed):
                if half == self.FIRST:
                    self.half_gathered[nm] = buf
                else:
                    weights[nm] = _full_weight(nm, buf)
        elif tag in self.SWAPS:
            self.swapped[self.SWAPS[tag]] = landed
        else:
            at = 0
            for batch, members in self._exchange_members(tag):
                for i in members:
                    self.summed.setdefault(batch, {})[i] = landed[at]
                    at += 1


def kernel(x, p, g_mix, w_in, conv_w, conv_b, w_rg, b_rg, w_ig, b_ig, lru_lambda, w_rnn_proj, q_gain, k_gain, sinks, w_attn_proj, w_out, g_mlp, w_up, w_down, g_ple, w_ple_gate, w_ple_proj, loss_target, m_g_mix, m_w_in, m_conv_w, m_conv_b, m_w_rg, m_b_rg, m_w_ig, m_b_ig, m_lru_lambda, m_w_rnn_proj, m_q_gain, m_k_gain, m_sinks, m_w_attn_proj, m_w_out, m_g_mlp, m_w_up, m_w_down, m_g_ple, m_w_ple_gate, m_w_ple_proj, v_g_mix, v_w_in, v_conv_w, v_conv_b, v_w_rg, v_b_rg, v_w_ig, v_b_ig, v_lru_lambda, v_w_rnn_proj, v_q_gain, v_k_gain, v_sinks, v_w_attn_proj, v_w_out, v_g_mlp, v_w_up, v_w_down, v_g_ple, v_w_ple_gate, v_w_ple_proj):
    names = ("g_mix", "w_in", "conv_w", "conv_b", "w_rg", "b_rg", "w_ig", "b_ig", "lru_lambda", "w_rnn_proj",
             "q_gain", "k_gain", "sinks", "w_attn_proj", "w_out", "g_mlp", "w_up", "w_down", "g_ple",
             "w_ple_gate", "w_ple_proj")
    wts = dict(zip(names, (g_mix, w_in, conv_w, conv_b, w_rg, b_rg, w_ig, b_ig, lru_lambda, w_rnn_proj, q_gain,
                           k_gain, sinks, w_attn_proj, w_out, g_mlp, w_up, w_down, g_ple, w_ple_gate, w_ple_proj)))
    mom1 = dict(zip(names, (m_g_mix, m_w_in, m_conv_w, m_conv_b, m_w_rg, m_b_rg, m_w_ig, m_b_ig, m_lru_lambda,
                            m_w_rnn_proj, m_q_gain, m_k_gain, m_sinks, m_w_attn_proj, m_w_out, m_g_mlp, m_w_up,
                            m_w_down, m_g_ple, m_w_ple_gate, m_w_ple_proj)))
    mom2 = dict(zip(names, (v_g_mix, v_w_in, v_conv_w, v_conv_b, v_w_rg, v_b_rg, v_w_ig, v_b_ig, v_lru_lambda,
                            v_w_rnn_proj, v_q_gain, v_k_gain, v_sinks, v_w_attn_proj, v_w_out, v_g_mlp, v_w_up,
                            v_w_down, v_g_ple, v_w_ple_gate, v_w_ple_proj)))
    n_seq, seq, _ = x.shape
    core = lax.axis_index("c").astype(jnp.int32).reshape(1)

    shards = {nm: wts[nm][0].astype(BF16) for nm in BIG}
    w_in_all, conv_all = _gather_two_level([shards["w_in"], conv_w[0]], name="gather_w_in")
    w = {nm: wts[nm] for nm in names if nm not in BIG}
    w["w_rg"], w["w_ig"] = w_rg[0], w_ig[0]
    w["conv_w"] = conv_all.transpose(1, 0, 2).reshape(CONV_W, D_MODEL)
    w["w_in"] = _full_weight("w_in", w_in_all)
    comm = _StepExchanges(shards, core)
    loss_sum, grad_x, g = _local_step(
        x.reshape(n_seq * seq, D_MODEL), p.reshape(n_seq * seq, PLE_DIM), loss_target.reshape(n_seq * seq, D_MODEL),
        w, n_seq=n_seq, seq=seq, comm=comm)
    del loss_sum

    res = {}
    for batch, batch_names in BATCHES.items():
        for i, nm in enumerate(batch_names):
            res[nm] = _adamw(comm.summed[batch][i], wts[nm][0], mom1[nm][0], mom2[nm][0], name="adamw_" + nm)
    g_mix_parts, = _exchange([g["g_mix"]], ["gather"], name="gather_g_mix")
    res["g_mix"] = [r[0] for r in _adamw(g_mix_parts, g_mix, m_g_mix, v_g_mix, name="adamw_g_mix")]
    small_names = [nm for nm, _ in SMALL if nm != LOSS_ROW]
    full_small = {}
    for src, key in ((wts, "w"), (mom1, "m"), (mom2, "v")):
        vals = {nm: src[nm][0] for nm in small_names}
        vals[LOSS_ROW] = jnp.zeros((1,), F32)
        full_small[key] = _pack_small(vals)
    small_res = _adamw(comm.summed[SMALL_BATCH][0],full_small["w"], full_small["m"], full_small["v"], name="adamw_small")
    shapes = {nm: wts[nm].shape[1:] for nm in small_names}
    shapes[LOSS_ROW] = (D_MODEL,)
    small_out = [_unpack_small(r, shapes) for r in small_res]
    for nm in small_names:
        res[nm] = [so[nm] for so in small_out]
    loss = jnp.sum(small_out[0][LOSS_ROW]) * (0.5 / D_MODEL)

    outs = [loss, grad_x.reshape(n_seq, seq, D_MODEL)]
    for k in range(4):
        outs.extend(res[nm][k][None] for nm in names)
    return tuple(outs)
```

```python
import functools
from typing import Callable, NamedTuple

import jax
import jax.numpy as jnp
from jax import lax
from jax.experimental import pallas as pl
from jax.experimental.pallas import tpu as pltpu

F32 = jnp.float32
BF16 = jnp.bfloat16

N_DEV = 8
D_MODEL = 1024
RNN_BLOCK_W = 64
CONV_W = 4
LRU_C = 8.0
HEAD_DIM = 64
N_Q_HEADS = 16
N_KV_HEADS = 4
KV_W = N_KV_HEADS * HEAD_DIM
WINDOW = 128
ROPE_THETA = 10000.0
D_FF = 4096
PLE_DIM = 256
NORM_EPS = 1e-6
IN_TOTAL = 5632
COL_RNN_END, COL_ATTN_END = 2048, 3584
ATTN_W = COL_ATTN_END - COL_RNN_END
ATTN_K_AT, ATTN_V_AT = 1024, 1280

ADAM_LR = 0.001
ADAM_B1 = 0.9
ADAM_B2 = 0.999
ADAM_EPS = 1e-08
ADAM_WD = 0.01
ADAM_STEP = 10

LANES = 128
SUBLANES = 8
RNN_TILE = 256
VMEM_LIMIT = 48 * 1024 * 1024
NEG_BIG = -1e30


def _params(*sem):
    return pltpu.CompilerParams(dimension_semantics=sem if sem else None, vmem_limit_bytes=VMEM_LIMIT)


def _sig(x):
    return 0.5 * jnp.tanh(0.5 * x) + 0.5


def _dot_nt(a, b):
    return lax.dot_general(a, b, (((1,), (1,)), ((), ())), preferred_element_type=F32)


def _dot_tn(a, b):
    return lax.dot_general(a, b, (((0,), (0,)), ((), ())), preferred_element_type=F32)


class _Xfer:
    def __init__(self, start, wait):
        self.start, self.wait = start, wait


class _Hosted(NamedTuple):
    srcs: tuple
    out_shape: tuple
    n_sems: int
    plan: Callable
    aliases: tuple = ()


def _merge_hosted(parts):
    parts = [p for p in parts if p is not None]
    if len(parts) <= 1:
        return parts[0] if parts else None
    src_at, dst_at, sem_at, aliases = [0], [0], [0], []
    for p in parts:
        aliases += [(i + src_at[-1], j + dst_at[-1]) for i, j in p.aliases]
        src_at.append(src_at[-1] + len(p.srcs))
        dst_at.append(dst_at[-1] + len(p.out_shape))
        sem_at.append(sem_at[-1] + p.n_sems)

    def plan(src, dst, send_sems, recv_sems, local_sems, first_sem):
        copies = []
        for k, p in enumerate(parts):
            copies += p.plan(src[src_at[k]:src_at[k + 1]], dst[dst_at[k]:dst_at[k + 1]], send_sems, recv_sems,
                             local_sems, first_sem + sem_at[k])
        return copies

    return _Hosted(tuple(a for p in parts for a in p.srcs), tuple(s for p in parts for s in p.out_shape),
                   sem_at[-1], plan, tuple(aliases))


def _call(body, *, grid, in_specs, out_specs, out_shape, args, name, semantics, scratch_shapes=(), hosted=None):
    if hosted is None:
        outs = pl.pallas_call(body, grid=grid, in_specs=list(in_specs), out_specs=list(out_specs),
                              out_shape=list(out_shape), scratch_shapes=list(scratch_shapes),
                              compiler_params=_params(*semantics), name=name)(*args)
        return list(outs), []
    counts = (len(in_specs), len(hosted.srcs), len(out_specs), len(hosted.out_shape), len(scratch_shapes), 3)

    def wrapped(*refs):
        at, groups = 0, []
        for count in counts:
            groups.append(refs[at:at + count])
            at += count
        ins, srcs, outs, dsts, scratch, sems = groups
        copies = hosted.plan(srcs, dsts, *sems, 0)
        ids = [pl.program_id(axis) for axis in range(len(grid))]
        first = functools.reduce(jnp.logical_and, [i == 0 for i in ids])
        last = functools.reduce(jnp.logical_and, [i == g - 1 for i, g in zip(ids, grid)])

        @pl.when(first)
        def _():
            for cp in copies:
                cp.start()

        body(*ins, *outs, *scratch)

        @pl.when(last)
        def _():
            for cp in copies:
                cp.wait()

    any_spec = pl.BlockSpec(memory_space=pl.ANY)
    sems = [pltpu.SemaphoreType.DMA((hosted.n_sems,))] * 3
    outs = pl.pallas_call(
        wrapped, grid=grid, in_specs=list(in_specs) + [any_spec] * counts[1],
        out_specs=list(out_specs) + [any_spec] * counts[3], out_shape=list(out_shape) + list(hosted.out_shape),
        scratch_shapes=list(scratch_shapes) + sems, compiler_params=_params(*["arbitrary"] * len(grid)),
        input_output_aliases={counts[0] + i: counts[2] + j for i, j in hosted.aliases},
        name=name)(*args, *hosted.srcs)
    return list(outs[:counts[2]]), list(outs[counts[2]:])


def _dividing_tile(n, want):
    tile = min(want, n)
    while n % tile:
        tile -= LANES
    return tile


def _matmul(a, b, *, mode, tm, tn, out_dtypes, name, epilogue=None, extras=(), hosted=None, b_cols=None,
            row_vecs=(), n_row_sums=0, extra_col_blocks=None):
    a_parts = tuple(a) if isinstance(a, (tuple, list)) else (a,)
    b_parts = tuple(b) if isinstance(b, (tuple, list)) else (b,)
    assert len(a_parts) == len(b_parts) and (mode == "nt" or len(a_parts) == 1)
    n_parts = len(a_parts)
    m = a_parts[0].shape[0]
    if b_cols is None:
        b_cols = [(0, bp.shape[1]) for bp in b_parts]
    n = b_cols[0][1] if mode == "nn" else b_parts[0].shape[0]
    tm, tn = min(tm, m), _dividing_tile(n, tn)
    n_extra = len(extras) + len(row_vecs)
    n_tiles_out = len(out_dtypes)
    assert n_row_sums == 0 or n == tn

    def body(*refs):
        a_refs, b_refs = refs[:n_parts], refs[n_parts:2 * n_parts]
        rest = refs[2 * n_parts:]
        extra_refs, out_refs = rest[:n_extra], rest[n_extra:]
        if mode == "nn":
            acc = jnp.dot(a_refs[0][...], b_refs[0][...], preferred_element_type=F32)
        else:
            acc = _dot_nt(a_refs[0][...], b_refs[0][...])
            for a_ref, b_ref in zip(a_refs[1:], b_refs[1:]):
                acc = acc + _dot_nt(a_ref[...], b_ref[...])
        res = epilogue(acc, *[e[...] for e in extra_refs]) if epilogue is not None else (acc,)
        for o_ref, r in zip(out_refs[:n_tiles_out], res):
            o_ref[...] = r.astype(o_ref.dtype)
        if n_row_sums:
            @pl.when(pl.program_id(0) == 0)
            def _():
                for o_ref in out_refs[n_tiles_out:]:
                    o_ref[...] = jnp.zeros_like(o_ref)

            for o_ref, r in zip(out_refs[n_tiles_out:], res[n_tiles_out:]):
                o_ref[...] += r

    a_specs = [pl.BlockSpec((tm, ap.shape[1]), lambda i, j: (i, 0)) for ap in a_parts]
    if mode == "nn":
        assert b_cols[0][0] % tn == 0
        first = b_cols[0][0] // tn
        b_specs = [pl.BlockSpec((b_parts[0].shape[0], tn), lambda i, j: (0, first + j))]
    else:
        assert all(at % width == 0 for at, width in b_cols)
        b_specs = [pl.BlockSpec((tn, width), functools.partial(lambda i, j, blk: (j, blk), blk=at // width))
                   for at, width in b_cols]
    tile = pl.BlockSpec((tm, tn), lambda i, j: (i, j))
    row = pl.BlockSpec((1, tn), lambda i, j: (0, j))
    extra_specs = [pl.BlockSpec((tm, tn), functools.partial(lambda i, j, first: (i, first + j), first=first))
                   for first in (extra_col_blocks or [0] * len(extras))]
    outs, landed = _call(
        body,
        grid=(m // tm, n // tn),
        in_specs=a_specs + b_specs + extra_specs + [row] * len(row_vecs),
        out_specs=[tile] * n_tiles_out + [row] * n_row_sums,
        out_shape=[jax.ShapeDtypeStruct((m, n), dt) for dt in out_dtypes]
        + [jax.ShapeDtypeStruct((1, n), F32)] * n_row_sums,
        args=(*a_parts, *b_parts, *extras, *row_vecs), name=name,
        semantics=("arbitrary" if n_row_sums else "parallel", "arbitrary"), hosted=hosted)
    if hosted is not None:
        return (*outs, landed)
    return outs[0] if len(outs) == 1 else outs


def _matmul_tn(a, b, *, tk, tn, tt, name, slot_cols=None):
    t, k = a.shape
    n = b.shape[1]
    tk, tn, tt = min(tk, k), _dividing_tile(n, tn), min(tt, t)

    def body(a_ref, b_ref, o_ref):
        @pl.when(pl.program_id(2) == 0)
        def _():
            o_ref[...] = jnp.zeros_like(o_ref)

        if slot_cols is None:
            o_ref[...] += _dot_tn(a_ref[...], b_ref[...])
        else:
            av = a_ref[...]
            for s in range(tn // slot_cols):
                o_ref[s] += _dot_tn(av, b_ref[:, s * slot_cols:(s + 1) * slot_cols])

    if slot_cols is not None:
        out_spec = pl.BlockSpec((tn // slot_cols, tk, slot_cols), lambda i, j, s: (j, i, 0))
        out_shape = jax.ShapeDtypeStruct((n // slot_cols, k, slot_cols), F32)
    else:
        out_spec = pl.BlockSpec((tk, tn), lambda i, j, s: (i, j))
        out_shape = jax.ShapeDtypeStruct((k, n), F32)
    return pl.pallas_call(
        body,
        grid=(k // tk, n // tn, t // tt),
        in_specs=[pl.BlockSpec((tt, tk), lambda i, j, s: (s, i)), pl.BlockSpec((tt, tn), lambda i, j, s: (s, j))],
        out_specs=out_spec,
        out_shape=out_shape,
        compiler_params=_params("parallel", "parallel", "arbitrary"),
        name=name,
    )(a, b)


def _matmul_tn_multi(a, bs, *, tt, name, hosted=None):
    t, k = a.shape
    tt = min(tt, t)
    n_b = len(bs)

    def body(a_ref, *refs):
        b_refs, o_refs = refs[:n_b], refs[n_b:]

        @pl.when(pl.program_id(0) == 0)
        def _():
            for o_ref in o_refs:
                o_ref[...] = jnp.zeros_like(o_ref)

        a_t = a_ref[...].T
        for b_ref, o_ref in zip(b_refs, o_refs):
            o_ref[...] += jnp.dot(a_t, b_ref[...], preferred_element_type=F32)

    outs, landed = _call(
        body,
        grid=(t // tt,),
        in_specs=[pl.BlockSpec((tt, k), lambda s: (s, 0))] + [pl.BlockSpec((tt, b.shape[1]), lambda s: (s, 0)) for b in bs],
        out_specs=[pl.BlockSpec((k, b.shape[1]), lambda s: (0, 0)) for b in bs],
        out_shape=[jax.ShapeDtypeStruct((k, b.shape[1]), F32) for b in bs],
        args=(a, *bs), name=name, semantics=("arbitrary",), hosted=hosted)
    return (*outs, landed) if hosted is not None else outs


def _rmsnorm_rows(x, g):
    return x * lax.rsqrt(jnp.mean(x * x, axis=-1, keepdims=True) + NORM_EPS) * g


def _norm_matmul(x, g, b, *, tm, tn, name, hosted=None):
    m, k = x.shape
    n = b.shape[1]
    tm, tn = min(tm, m), _dividing_tile(n, tn)

    def body(x_ref, g_ref, b_ref, z_ref, h_ref, h_s):
        @pl.when(pl.program_id(1) == 0)
        def _():
            h_s[...] = _rmsnorm_rows(x_ref[...], g_ref[...]).astype(BF16)
            h_ref[...] = h_s[...]

        z_ref[...] = jnp.dot(h_s[...], b_ref[...], preferred_element_type=F32)

    rows = pl.BlockSpec((tm, k), lambda i, j: (i, 0))
    outs, landed = _call(
        body,
        grid=(m // tm, n // tn),
        in_specs=[rows, pl.BlockSpec((1, k), lambda i, j: (0, 0)), pl.BlockSpec((k, tn), lambda i, j: (0, j))],
        out_specs=[pl.BlockSpec((tm, tn), lambda i, j: (i, j)), rows],
        out_shape=[jax.ShapeDtypeStruct((m, n), F32), jax.ShapeDtypeStruct((m, k), BF16)],
        scratch_shapes=[pltpu.VMEM((tm, k), BF16)],
        args=(x, g, b), name=name, semantics=("parallel", "arbitrary"), hosted=hosted)
    return (*outs, landed) if hosted is not None else outs


def _rmsnorm_bwd_rows(dy, x, dres, g):
    r = lax.rsqrt(jnp.mean(x * x, axis=-1, keepdims=True) + NORM_EPS)
    xr = x * r
    gy = dy * g
    dx = dres + r * (gy - xr * jnp.mean(gy * xr, axis=-1, keepdims=True))
    return dx, jnp.sum(dy * xr, axis=0, keepdims=True)


def _softplus_neg(lam):
    z = -lam
    return jnp.maximum(z, 0.0) + jnp.log1p(jnp.exp(-jnp.abs(z)))


def _neg_expm1(y, exp_half_y):
    series = -y * (1.0 + y * 0.5 * (1.0 + y * (1.0 / 3.0) * (1.0 + y * 0.25 * (1.0 + y * 0.2))))
    return jnp.where(y > -0.0625, series, 1.0 - exp_half_y * exp_half_y)


def _gelu_parts(x):
    c = 0.7978845608028654
    u = c * (x + 0.044715 * x * x * x)
    th = jnp.tanh(u)
    gel = 0.5 * x * (1.0 + th)
    dgel = 0.5 * (1.0 + th) + 0.5 * x * (1.0 - th * th) * c * (1.0 + 3.0 * 0.044715 * x * x)
    return gel, dgel


def _shift_down(v, k, rows):
    return jnp.where(rows < k, 0.0, pltpu.roll(v, k, 0))


def _shift_up(v, k, rows, n):
    return jnp.where(rows >= n - k, 0.0, pltpu.roll(v, n - k, 0))


def _scan_within_groups(a, b, *, reverse):
    shape = a.shape
    a = a.reshape(shape[0] // SUBLANES, SUBLANES, shape[1])
    b = b.reshape(a.shape)
    in_group = lax.broadcasted_iota(jnp.int32, a.shape, 1)
    for s in (1, 2, 4):
        if reverse:
            inside, shift = in_group < SUBLANES - s, SUBLANES - s
        else:
            inside, shift = in_group >= s, s
        b = b + a * jnp.where(inside, pltpu.roll(b, shift, 1), 0.0)
        a = a * jnp.where(inside, pltpu.roll(a, shift, 1), 1.0)
    return a.reshape(shape), b.reshape(shape)


def _rnn_gates(xc, wrg, brg, wig, big, lam):
    xcb = xc.astype(BF16)
    r = _sig(jnp.dot(xcb, wrg, preferred_element_type=F32) + brg)
    i = _sig(jnp.dot(xcb, wig, preferred_element_type=F32) + big)
    sp = _softplus_neg(lam)
    log_a = -LRU_C * r * sp
    a = jnp.exp(log_a)
    mult = jnp.sqrt(_neg_expm1(2.0 * log_a, a))
    return xcb, r, i, sp, a, mult


def _conv_fwd(xv, cw, cb, rows):
    return (cb + _shift_down(xv, 3, rows) * cw[0:1, :] + _shift_down(xv, 2, rows) * cw[1:2, :]
            + _shift_down(xv, 1, rows) * cw[2:3, :] + xv * cw[3:4, :])


def _rnn_fwd(z, conv_w, conv_b, wrg_bd, b_rg, wig_bd, b_ig, lam, *, n_seq, seq, hosted=None):
    t = n_seq * seq
    ct = RNN_TILE
    n_ct = D_MODEL // ct

    def body(x_ref, g_ref, cw_ref, cb_ref, wrg_ref, brg_ref, wig_ref, big_ref, lam_ref,
             xc_ref, hr_ref, ya_ref, a_s, b_s):
        rows = lax.broadcasted_iota(jnp.int32, (seq, ct), 0)
        xc = _conv_fwd(x_ref[...], cw_ref[...], cb_ref[...], rows)
        _, r, i, sp, a, mult = _rnn_gates(xc, wrg_ref[...], brg_ref[...], wig_ref[...], big_ref[...], lam_ref[...])
        a_s[...], b_s[...] = _scan_within_groups(a, mult * (i * xc), reverse=False)

        def step(j, carry):
            r0 = pl.multiple_of(j * SUBLANES, SUBLANES)
            h = b_s[pl.ds(r0, SUBLANES), :] + a_s[pl.ds(r0, SUBLANES), :] * carry
            hr_ref[pl.ds(r0, SUBLANES), :] = h
            return h[SUBLANES - 1:SUBLANES, :]

        lax.fori_loop(0, seq // SUBLANES, step, jnp.zeros((1, ct), F32), unroll=4)
        gel, _ = _gelu_parts(g_ref[...])
        xc_ref[...] = xc
        ya_ref[...] = (hr_ref[...] * gel).astype(BF16)

    vec = pl.BlockSpec((1, ct), lambda b, c: (0, c))
    gate_w = pl.BlockSpec((None, ct, ct), lambda b, c: (c, 0, 0))
    tile = pl.BlockSpec((seq, ct), lambda b, c: (b, c))
    outs, landed = _call(
        body,
        grid=(n_seq, n_ct),
        in_specs=[
            pl.BlockSpec((seq, ct), lambda b, c: (b, c)),
            pl.BlockSpec((seq, ct), lambda b, c: (b, n_ct + c)),
            pl.BlockSpec((CONV_W, ct), lambda b, c: (0, c)), vec, gate_w, vec, gate_w, vec, vec,
        ],
        out_specs=[tile, tile, tile],
        out_shape=[jax.ShapeDtypeStruct((t, D_MODEL), F32), jax.ShapeDtypeStruct((t, D_MODEL), F32),
                   jax.ShapeDtypeStruct((t, D_MODEL), BF16)],
        scratch_shapes=[pltpu.VMEM((seq, ct), F32), pltpu.VMEM((seq, ct), F32)],
        args=(z, z, conv_w, conv_b, wrg_bd, b_rg, wig_bd, b_ig, lam), name="rnn_fwd",
        semantics=("parallel", "parallel"), hosted=hosted)
    return (*outs, landed) if hosted is not None else outs


def _rnn_bwd(dya, z, xc, hr, conv_w, wrg_bd, b_rg, wig_bd, b_ig, lam, *, n_seq, seq, hosted=None):
    t = n_seq * seq
    ct = RNN_TILE
    n_ct = D_MODEL // ct

    def body(dya_ref, x_ref, g_ref, xc_ref, hr_ref, cw_ref, wrg_ref, brg_ref, wig_ref, big_ref, lam_ref,
             dx_ref, dg_ref, dwrg_ref, dwig_ref, vec_ref, a_s, d_s, g_s):
        rows = lax.broadcasted_iota(jnp.int32, (seq, ct), 0)
        xv, xc, hr, dyv = x_ref[...], xc_ref[...], hr_ref[...], dya_ref[...]
        lamv = lam_ref[...]
        gel, dgel = _gelu_parts(g_ref[...])
        dg_ref[...] = (dyv * hr * dgel).astype(BF16)
        xcb, r, i, sp, a, mult = _rnn_gates(xc, wrg_ref[...], brg_ref[...], wig_ref[...], big_ref[...], lamv)
        a_s[...], d_s[...] = _scan_within_groups(_shift_up(a, 1, rows, seq), dyv * gel, reverse=True)

        def step(k, carry):
            r0 = pl.multiple_of((seq // SUBLANES - 1 - k) * SUBLANES, SUBLANES)
            gs = d_s[pl.ds(r0, SUBLANES), :] + a_s[pl.ds(r0, SUBLANES), :] * carry
            g_s[pl.ds(r0, SUBLANES), :] = gs
            return gs[0:1, :]

        lax.fori_loop(0, seq // SUBLANES, step, jnp.zeros((1, ct), F32), unroll=4)
        gsum = g_s[...]
        gated = i * xc
        d_log_a = gsum * _shift_down(hr, 1, rows) * a - gsum * gated * (a * a / mult)
        d_gated = gsum * mult
        d_pre_r = (d_log_a * (-LRU_C) * sp) * r * (1.0 - r)
        d_pre_i = (d_gated * xc) * i * (1.0 - i)
        dprb, dpib = d_pre_r.astype(BF16), d_pre_i.astype(BF16)
        dxc = d_gated * i + _dot_nt(dprb, wrg_ref[...]) + _dot_nt(dpib, wig_ref[...])
        cw = cw_ref[...]
        dx = dxc * cw[CONV_W - 1:CONV_W, :]
        d_taps = [jnp.sum(dxc * xv, axis=0, keepdims=True)]
        for k in range(1, CONV_W):
            up = _shift_up(dxc, k, rows, seq)
            dx = dx + up * cw[CONV_W - 1 - k:CONV_W - k, :]
            d_taps.append(jnp.sum(up * xv, axis=0, keepdims=True))
        dx_ref[...] = dx.astype(BF16)

        @pl.when(pl.program_id(1) == 0)
        def _():
            dwrg_ref[...] = jnp.zeros_like(dwrg_ref)
            dwig_ref[...] = jnp.zeros_like(dwig_ref)
            vec_ref[...] = jnp.zeros_like(vec_ref)

        dwrg_ref[...] += _dot_tn(xcb, dprb)
        dwig_ref[...] += _dot_tn(xcb, dpib)

        def colsum(v):
            return jnp.sum(v, axis=0, keepdims=True)

        d_sp = colsum(d_log_a * (-LRU_C) * r)
        vec_ref[0:1, :] += colsum(d_pre_r)
        vec_ref[1:2, :] += colsum(d_pre_i)
        vec_ref[2:3, :] += d_sp * (-_sig(-lamv))
        vec_ref[3:4, :] += colsum(dxc)
        for tap in range(CONV_W):
            vec_ref[4 + tap:5 + tap, :] += d_taps[CONV_W - 1 - tap]

    vec = pl.BlockSpec((1, ct), lambda c, b: (0, c))
    gate_w = pl.BlockSpec((None, ct, ct), lambda c, b: (c, 0, 0))
    tile = pl.BlockSpec((seq, ct), lambda c, b: (b, c))
    outs, landed = _call(
        body,
        grid=(n_ct, n_seq),
        in_specs=[
            tile,
            pl.BlockSpec((seq, ct), lambda c, b: (b, c)),
            pl.BlockSpec((seq, ct), lambda c, b: (b, n_ct + c)),
            tile, tile,
            pl.BlockSpec((CONV_W, ct), lambda c, b: (0, c)), gate_w, vec, gate_w, vec, vec,
        ],
        out_specs=[tile, tile, gate_w, gate_w, pl.BlockSpec((8, ct), lambda c, b: (0, c))],
        out_shape=[jax.ShapeDtypeStruct((t, D_MODEL), BF16), jax.ShapeDtypeStruct((t, D_MODEL), BF16),
                   jax.ShapeDtypeStruct((n_ct, ct, ct), F32), jax.ShapeDtypeStruct((n_ct, ct, ct), F32),
                   jax.ShapeDtypeStruct((8, D_MODEL), F32)],
        scratch_shapes=[pltpu.VMEM((seq, ct), F32)] * 3,
        args=(dya, z, z, xc, hr, conv_w, wrg_bd, b_rg, wig_bd, b_ig, lam), name="rnn_bwd",
        semantics=("parallel", "arbitrary"), hosted=hosted)
    return (*outs, landed) if hosted is not None else outs


def _split_hi_lo(x):
    hi = x.astype(BF16)
    return hi, (x - hi.astype(F32)).astype(BF16)


def _dot_split(x, m_twice):
    hi, lo = _split_hi_lo(x)
    return jnp.dot(jnp.concatenate([hi, lo], axis=1), m_twice, preferred_element_type=F32)


def _head_matrices(width):
    ec = ((lax.broadcasted_iota(jnp.int32, (2 * width, LANES), 0) & (width - 1)) // HEAD_DIM
          == lax.broadcasted_iota(jnp.int32, (2 * width, LANES), 1))
    ee = (lax.broadcasted_iota(jnp.int32, (2 * LANES, width), 1) // HEAD_DIM
          == (lax.broadcasted_iota(jnp.int32, (2 * LANES, width), 0) & (LANES - 1)))
    return jnp.where(ec, 1.0, 0.0).astype(BF16), jnp.where(ee, 1.0, 0.0).astype(BF16)


def _swap_halves(y):
    w = y.shape[1]
    first = (lax.broadcasted_iota(jnp.int32, y.shape, 1) % HEAD_DIM) < HEAD_DIM // 2
    return jnp.where(first, pltpu.roll(y, w - HEAD_DIM // 2, 1), pltpu.roll(y, HEAD_DIM // 2, 1))


def _normrope_fwd(x, gain, cos_t, sin_t, ec, ee):
    w = x.shape[1]
    rs = _dot_split(lax.rsqrt(_dot_split(x * x, ec) * (1.0 / HEAD_DIM) + NORM_EPS), ee)
    nx = x * rs
    y = nx * gain
    reps = w // LANES
    out = y * jnp.tile(cos_t, (1, reps)) + _swap_halves(y) * jnp.tile(sin_t, (1, reps))
    return out, nx, rs


def _normrope_bwd(dout, nx, rs, gain, cos_t, sin_t, ec, ee):
    w = dout.shape[1]
    reps = w // LANES
    dy = dout * jnp.tile(cos_t, (1, reps)) + _swap_halves(dout * jnp.tile(sin_t, (1, reps)))
    dgain = jnp.sum(dy * nx, axis=0, keepdims=True)
    dn = dy * gain
    seg = _dot_split(_dot_split(dn * nx, ec) * (1.0 / HEAD_DIM), ee)
    return rs * (dn - nx * seg), dgain


def _pair_operand(t, group):
    chunk = t[:, (group // 2) * LANES:(group // 2 + 1) * LANES]
    low = lax.broadcasted_iota(jnp.int32, chunk.shape, 1) < HEAD_DIM
    rolled = pltpu.roll(chunk, HEAD_DIM, 1)
    return jnp.where(low, chunk, rolled) if group % 2 == 0 else jnp.where(low, rolled, chunk)


GROUP = N_Q_HEADS // N_KV_HEADS
GROUP_W = GROUP * HEAD_DIM


def _replicate_head(t, group):
    return jnp.tile(_pair_operand(t, group), (1, 2))


def _head_blocks(t):
    seg = lax.broadcasted_iota(jnp.int32, t.shape, 1) // HEAD_DIM
    return jnp.concatenate([jnp.where(seg == h, t, 0.0) for h in range(GROUP)], axis=0)


def _stack_heads(t_t, rows):
    return jnp.concatenate([t_t[:, h * rows:(h + 1) * rows] for h in range(GROUP)], axis=0)


def _head_rows(mat_t, group):
    return jnp.concatenate([mat_t[GROUP * group + h:GROUP * group + h + 1, :] for h in range(GROUP)], axis=1)


def _window_masks(blk):
    key = lax.broadcasted_iota(jnp.int32, (blk, GROUP * blk), 0)
    query = lax.broadcasted_iota(jnp.int32, (blk, GROUP * blk), 1) & (blk - 1)
    return key > query, key <= query


def _mask_window(t, before_ok, own_ok, fill):
    blk = t.shape[0] // 2
    return jnp.concatenate([jnp.where(before_ok, t[:blk], fill), jnp.where(own_ok, t[blk:], fill)], axis=0)


def _attn_fwd(z, cos_t, sin_t, q_gain_t, k_gain_t, sinks_t, *, n_seq, seq, hosted=None):
    t = n_seq * seq
    blk = WINDOW
    nb = seq // blk

    def body(q_ref, kp_ref, kc_ref, vp_ref, vc_ref, cosc_ref, sinc_ref, cosp_ref, sinp_ref, qg_ref, kg_ref, sk_ref,
             o_ref, l_ref):
        n = pl.program_id(1)
        ecq, eeq = _head_matrices(D_MODEL)
        eck, eek = _head_matrices(KV_W)
        cosc, sinc = cosc_ref[...], sinc_ref[...]
        qh, _, _ = _normrope_fwd(q_ref[...], qg_ref[...], cosc, sinc, ecq, eeq)
        qh = qh * (HEAD_DIM ** -0.5)
        kc, _, _ = _normrope_fwd(kc_ref[...], kg_ref[...], cosc, sinc, eck, eek)
        kp, _, _ = _normrope_fwd(kp_ref[...], kg_ref[...], cosp_ref[...], sinp_ref[...], eck, eek)
        kcat = jnp.concatenate([kp, kc], axis=0)
        vcat = jnp.concatenate([vp_ref[...], vc_ref[...]], axis=0)
        above, causal = _window_masks(blk)
        above = above & (n > 0)
        head_row = lax.broadcasted_iota(jnp.int32, (blk, blk), 0)
        sk_t = jnp.broadcast_to(sk_ref[...], (blk, LANES)).T
        vcat_t = vcat.T.astype(BF16)
        lmat = jnp.zeros((blk, blk), F32)
        groups = range(N_KV_HEADS)
        cols = [slice(g * GROUP_W, (g + 1) * GROUP_W) for g in groups]
        qh = qh.astype(BF16)
        scores = [_dot_nt(_replicate_head(kcat, g).astype(BF16), _head_blocks(qh[:, cols[g]]))
                  for g in groups]
        probs = []
        for g in groups:
            s = _mask_window(scores[g], above, causal, NEG_BIG)
            sink = _head_rows(sk_t, g)
            m = jnp.maximum(jnp.max(s, axis=0, keepdims=True), sink)
            e = jnp.exp(s - m)
            den = jnp.sum(e, axis=0, keepdims=True) + jnp.exp(sink - m)
            probs.append((e * (1.0 / den)).astype(BF16))
            lse = m + jnp.log(den)
            for h in range(GROUP):
                lmat = lmat + jnp.where(head_row == GROUP * g + h, lse[:, h * blk:(h + 1) * blk], 0.0)
        for g in groups:
            out_t = jnp.dot(vcat_t[g * HEAD_DIM:(g + 1) * HEAD_DIM], probs[g], preferred_element_type=F32)
            o_ref[:, cols[g]] = _stack_heads(out_t, blk).T.astype(BF16)
        l_ref[...] = lmat

    def row(b, n):
        return b * nb + n

    def prev(b, n):
        return b * nb + jnp.maximum(n - 1, 0)

    kw = KV_W
    tab_c = pl.BlockSpec((blk, LANES), lambda b, n: (n, 0))
    tab_p = pl.BlockSpec((blk, LANES), lambda b, n: (jnp.maximum(n - 1, 0), 0))
    outs, landed = _call(
        body,
        grid=(n_seq, nb),
        in_specs=[
            pl.BlockSpec((blk, D_MODEL), lambda b, n: (row(b, n), COL_RNN_END // D_MODEL)),
            pl.BlockSpec((blk, kw), lambda b, n: (prev(b, n), (COL_RNN_END + ATTN_K_AT) // kw)),
            pl.BlockSpec((blk, kw), lambda b, n: (row(b, n), (COL_RNN_END + ATTN_K_AT) // kw)),
            pl.BlockSpec((blk, kw), lambda b, n: (prev(b, n), (COL_RNN_END + ATTN_V_AT) // kw)),
            pl.BlockSpec((blk, kw), lambda b, n: (row(b, n), (COL_RNN_END + ATTN_V_AT) // kw)),
            tab_c, tab_c, tab_p, tab_p,
            pl.BlockSpec((1, D_MODEL), lambda b, n: (0, 0)),
            pl.BlockSpec((1, kw), lambda b, n: (0, 0)),
            pl.BlockSpec((1, LANES), lambda b, n: (0, 0)),
        ],
        out_specs=[pl.BlockSpec((blk, D_MODEL), lambda b, n: (row(b, n), 0)),
                   pl.BlockSpec((blk, LANES), lambda b, n: (row(b, n), 0))],
        out_shape=[jax.ShapeDtypeStruct((t, D_MODEL), BF16), jax.ShapeDtypeStruct((t, LANES), F32)],
        args=(z, z, z, z, z, cos_t, sin_t, cos_t, sin_t, q_gain_t, k_gain_t, sinks_t), name="attn_fwd",
        semantics=("parallel", "parallel"), hosted=hosted)
    return (*outs, landed) if hosted is not None else outs


def _attn_bwd(z, o, lse, do, cos_t, sin_t, q_gain_t, k_gain_t, sinks_t, *, n_seq, seq, hosted=None):
    t = n_seq * seq
    blk = WINDOW
    nb = seq // blk
    kw = KV_W
    scale = HEAD_DIM ** -0.5

    def body(qc_ref, qn_ref, kc_ref, vp_ref, vc_ref, oc_ref, on_ref, doc_ref, don_ref, lc_ref, ln_ref,
             cosc_ref, sinc_ref, cosn_ref, sinn_ref, qg_ref, kg_ref, sk_ref,
             dz_ref, vec_ref, dq_s, q_s, k_s):
        n = pl.program_id(1)
        ecq, eeq = _head_matrices(D_MODEL)
        eck, eek = _head_matrices(KV_W)
        cosc, sinc = cosc_ref[...], sinc_ref[...]
        qg, kg = qg_ref[...], kg_ref[...]
        own, other = n & 1, 1 - (n & 1)

        @pl.when(n == 0)
        def _():
            for part, value in enumerate(_normrope_fwd(qc_ref[...], qg, cosc, sinc, ecq, eeq)):
                q_s[own, part] = value
            k_s[other] = jnp.zeros((blk, kw), F32)

        for part, value in enumerate(_normrope_fwd(qn_ref[...], qg, cosn_ref[...], sinn_ref[...], ecq, eeq)):
            q_s[other, part] = value
        qhc, nqc, rsqc = q_s[own, 0], q_s[own, 1], q_s[own, 2]
        qhn = q_s[other, 0]
        khc, nkc, rskc = _normrope_fwd(kc_ref[...], kg, cosc, sinc, eck, eek)
        khp = k_s[other]
        k_s[own] = khc
        doc = doc_ref[...].astype(F32)
        don = don_ref[...].astype(F32)
        delc = _dot_split(doc * oc_ref[...].astype(F32), ecq)
        deln = _dot_split(don * on_ref[...].astype(F32), ecq)
        lc_t, ln_t, delc_t, deln_t = lc_ref[...], ln_ref[...], delc.T, deln.T
        above, causal = _window_masks(blk)
        above_c, above_n = above & (n > 0), above & (n < nb - 1)
        seg = lax.broadcasted_iota(jnp.int32, (blk, GROUP_W), 1) // HEAD_DIM
        lane = lax.broadcasted_iota(jnp.int32, (1, LANES), 1)
        sk_t = jnp.broadcast_to(sk_ref[...], (blk, LANES)).T
        dsink = jnp.zeros((1, LANES), F32)
        kcat = jnp.concatenate([khp, khc], axis=0)
        vcat = jnp.concatenate([vp_ref[...], vc_ref[...]], axis=0)
        kcat_t = kcat.T.astype(BF16)
        dkh = jnp.zeros((blk, GROUP_W), F32)
        dvh = jnp.zeros((blk, GROUP_W), F32)

        def fold_to(group, t):
            total = t + pltpu.roll(t, HEAD_DIM, 1)
            total = total + pltpu.roll(total, 2 * HEAD_DIM, 1)
            return jnp.where(seg == group, total, 0.0)

        groups = range(N_KV_HEADS)
        cols = [slice(g * GROUP_W, (g + 1) * GROUP_W) for g in groups]
        qsc, qsn = qhc * scale, qhn * scale
        qb_c = [_head_blocks(qsc[:, cols[g]]).astype(BF16) for g in groups]
        qb_n = [_head_blocks(qsn[:, cols[g]]).astype(BF16) for g in groups]
        dob_c = [_head_blocks(doc[:, cols[g]]).astype(BF16) for g in groups]
        dob_n = [_head_blocks(don[:, cols[g]]).astype(BF16) for g in groups]
        raw = []
        for g in groups:
            krep = _replicate_head(kcat, g).astype(BF16)
            vrep = _replicate_head(vcat, g).astype(BF16)
            raw.append((_dot_nt(krep, qb_c[g]), _dot_nt(vrep, dob_c[g]),
                        _dot_nt(krep[blk:], qb_n[g]), _dot_nt(vrep[blk:], dob_n[g])))
        cooked = []
        for g in groups:
            s_c, dp_c, s_n, dp_n = raw[g]
            l_row, d_row = _head_rows(lc_t, g), _head_rows(delc_t, g)
            p_c = _mask_window(jnp.exp(s_c - l_row), above_c, causal, 0.0)
            ds_c = (p_c * (dp_c - d_row)).astype(BF16)
            p_n = jnp.where(above_n, jnp.exp(s_n - _head_rows(ln_t, g)), 0.0)
            ds_n = (p_n * (dp_n - _head_rows(deln_t, g))).astype(BF16)
            cooked.append((p_c[blk:].astype(BF16), ds_c, p_n.astype(BF16), ds_n))
            p_sink = jnp.exp(_head_rows(sk_t, g) - l_row) * d_row
            for h in range(GROUP):
                dsink = dsink + jnp.where(lane == GROUP * g + h,
                                          -jnp.sum(p_sink[:, h * blk:(h + 1) * blk], axis=1, keepdims=True), 0.0)
        for g in groups:
            p_cb, ds_c, p_nb, ds_n = cooked[g]
            dq_t = jnp.dot(kcat_t[g * HEAD_DIM:(g + 1) * HEAD_DIM], ds_c, preferred_element_type=F32)
            dq_s[:, cols[g]] = _stack_heads(dq_t, blk).T * scale
            dk_rep = (jnp.dot(ds_c[blk:], qb_c[g], preferred_element_type=F32)
                      + jnp.dot(ds_n, qb_n[g], preferred_element_type=F32))
            dv_rep = (jnp.dot(p_cb, dob_c[g], preferred_element_type=F32)
                      + jnp.dot(p_nb, dob_n[g], preferred_element_type=F32))
            dkh = dkh + fold_to(g, dk_rep)
            dvh = dvh + fold_to(g, dv_rep)
        dq, dqg = _normrope_bwd(dq_s[...], nqc, rsqc, qg, cosc, sinc, ecq, eeq)
        dk, dkg = _normrope_bwd(dkh, nkc, rskc, kg, cosc, sinc, eck, eek)
        dz_ref[:, :ATTN_K_AT] = dq.astype(BF16)
        dz_ref[:, ATTN_K_AT:ATTN_V_AT] = dk.astype(BF16)
        dz_ref[:, ATTN_V_AT:] = dvh.astype(BF16)

        @pl.when(n == 0)
        def _():
            vec_ref[...] = jnp.zeros_like(vec_ref)

        vec_ref[0:1, :] += dqg
        vec_ref[1:2, 0:kw] += dkg
        vec_ref[2:3, 0:LANES] += dsink

    def row(b, n):
        return b * nb + n

    def prev(b, n):
        return b * nb + jnp.maximum(n - 1, 0)

    def nxt(b, n):
        return b * nb + jnp.minimum(n + 1, nb - 1)

    def tiles(width, col, which):
        return pl.BlockSpec((blk, width), lambda b, n: (which(b, n), col))

    def table(which):
        return pl.BlockSpec((blk, LANES), lambda b, n: (which(0, n), 0))

    outs, landed = _call(
        body,
        grid=(n_seq, nb),
        in_specs=[
            tiles(D_MODEL, COL_RNN_END // D_MODEL, row), tiles(D_MODEL, COL_RNN_END // D_MODEL, nxt),
            tiles(kw, (COL_RNN_END + ATTN_K_AT) // kw, row),
            tiles(kw, (COL_RNN_END + ATTN_V_AT) // kw, prev), tiles(kw, (COL_RNN_END + ATTN_V_AT) // kw, row),
            tiles(D_MODEL, 0, row), tiles(D_MODEL, 0, nxt),
            tiles(D_MODEL, 0, row), tiles(D_MODEL, 0, nxt),
            tiles(LANES, 0, row), tiles(LANES, 0, nxt),
            table(row), table(row), table(nxt), table(nxt),
            pl.BlockSpec((1, D_MODEL), lambda b, n: (0, 0)),
            pl.BlockSpec((1, kw), lambda b, n: (0, 0)),
            pl.BlockSpec((1, LANES), lambda b, n: (0, 0)),
        ],
        out_specs=[tiles(ATTN_W, 0, row), pl.BlockSpec((None, 8, D_MODEL), lambda b, n: (b, 0, 0))],
        out_shape=[jax.ShapeDtypeStruct((t, ATTN_W), BF16), jax.ShapeDtypeStruct((n_seq, 8, D_MODEL), F32)],
        scratch_shapes=[pltpu.VMEM((blk, D_MODEL), F32), pltpu.VMEM((2, 3, blk, D_MODEL), F32),
                        pltpu.VMEM((2, blk, kw), F32)],
        args=(z, z, z, z, z, o, o, do, do, lse, lse, cos_t, sin_t, cos_t, sin_t,
              q_gain_t, k_gain_t, sinks_t), name="attn_bwd", semantics=("arbitrary", "arbitrary"), hosted=hosted)
    return (*outs, landed) if hosted is not None else outs


def _rope_tables(seq):
    inv = ROPE_THETA ** (-jnp.arange(0, HEAD_DIM, 2, dtype=F32) / HEAD_DIM)
    ang = jnp.arange(seq, dtype=F32)[:, None] * inv[None, :]
    cos, sin = jnp.cos(ang), jnp.sin(ang)
    return jnp.tile(jnp.concatenate([cos, cos], axis=1), (1, 2)), jnp.tile(jnp.concatenate([-sin, sin], axis=1), (1, 2))


def _block_diag_tiles(w):
    per = RNN_TILE // RNN_BLOCK_W
    w4 = w.reshape(D_MODEL // RNN_TILE, per, RNN_BLOCK_W, RNN_BLOCK_W)
    eye = jnp.eye(per, dtype=w.dtype)
    dense = jnp.einsum("tpij,pq->tpiqj", w4, eye)
    return dense.reshape(D_MODEL // RNN_TILE, RNN_TILE, RNN_TILE).astype(BF16)


def _block_diag_extract(dense):
    per = RNN_TILE // RNN_BLOCK_W
    d5 = dense.reshape(D_MODEL // RNN_TILE, per, RNN_BLOCK_W, per, RNN_BLOCK_W)
    blocks = jnp.stack([d5[:, p, :, p, :] for p in range(per)], axis=1)
    return blocks.reshape(D_MODEL // RNN_BLOCK_W, RNN_BLOCK_W, RNN_BLOCK_W)


def _local_step(x, p, target, w, *, n_seq, seq, comm=None):
    w = dict(w)

    def run(tag, fn, *args, **kwargs):
        hosted = comm.host(tag) if comm is not None else None
        if hosted is None:
            return fn(*args, **kwargs)
        *outs, landed = fn(*args, hosted=hosted, **kwargs)
        comm.landed(tag, landed, w)
        return outs[0] if len(outs) == 1 else outs

    def ready(batch, grads, extra=None):
        if comm is not None:
            comm.ready(batch, grads, extra)

    cos_t, sin_t = _rope_tables(seq)
    q_gain_t = jnp.tile(w["q_gain"], (1, N_Q_HEADS))
    k_gain_t = jnp.tile(w["k_gain"], (1, N_KV_HEADS))
    sinks_t = jnp.pad(w["sinks"], ((0, 0), (0, LANES - N_Q_HEADS)))
    wrg_bd, wig_bd = _block_diag_tiles(w["w_rg"]), _block_diag_tiles(w["w_ig"])
    dims = dict(n_seq=n_seq, seq=seq)

    z, h = run("mm_in", _norm_matmul, x, w["g_mix"], w["w_in"], tm=1024, tn=IN_TOTAL // 4, name="mm_in")
    gate_tile = 512
    ga_at, gb_at = COL_ATTN_END // gate_tile, (COL_ATTN_END + D_MODEL) // gate_tile
    xc, hr, ya_in = run("rnn_fwd", _rnn_fwd, z, w["conv_w"], w["conv_b"], wrg_bd, w["b_rg"], wig_bd, w["b_ig"],
                        w["lru_lambda"], **dims)
    o, lse = run("attn_fwd", _attn_fwd, z, cos_t, sin_t, q_gain_t, k_gain_t, sinks_t, **dims)
    ya = run("mm_rnn_proj", _matmul, ya_in, w["w_rnn_proj"], mode="nn", tm=1024, tn=1024, out_dtypes=[F32],
             name="mm_rnn_proj")
    yb, merged = _matmul(
        o, w["w_attn_proj"], mode="nn", tm=1024, tn=gate_tile, out_dtypes=[F32, BF16], name="mm_attn_proj",
        epilogue=lambda acc, ga, gb, yav: (acc, _sig(ga) * yav + _sig(gb) * acc),
        extras=(z, z, ya), extra_col_blocks=(ga_at, gb_at, 0))
    def residual_then_norm(acc, res, gain):
        new = res + acc
        return new, _rmsnorm_rows(new, gain)

    x1, hm = _matmul(merged, w["w_out"], mode="nn", tm=512, tn=1024, out_dtypes=[F32, BF16], name="mm_out",
                     epilogue=residual_then_norm, extras=(x,), row_vecs=(w["g_mlp"],))
    act = _matmul(hm, w["w_up"], mode="nn", tm=1024, tn=1024, out_dtypes=[BF16], name="mm_up",
                  epilogue=lambda acc: (jnp.square(jnp.maximum(acc, 0.0)),))
    x2, hp = _matmul(act, w["w_down"], mode="nn", tm=512, tn=1024, out_dtypes=[F32, BF16], name="mm_down",
                     epilogue=residual_then_norm, extras=(x1,), row_vecs=(w["g_ple"],))
    p_bf = p.astype(BF16)
    e = _matmul(p_bf, w["w_ple_proj"], mode="nn", tm=1024, tn=1024, out_dtypes=[F32], name="mm_ple_proj")

    def loss_head(gt, x2v, ev, tgt):
        sg = _sig(gt)
        diff = x2v + ev * sg - tgt
        dx = diff * (1.0 / D_MODEL)
        return dx, dx * ev * sg * (1.0 - sg), dx * sg, jnp.sum(diff * diff, axis=0, keepdims=True)

    dx3, dgt, de, loss_row = _matmul(hp, w["w_ple_gate"], mode="nn", tm=512, tn=1024, out_dtypes=[F32, BF16, BF16],
                                     name="mm_ple_gate", epilogue=loss_head, extras=(x2, e, target), n_row_sums=1)

    g = {}
    g["w_ple_proj"] = _matmul_tn(p_bf, de, tk=PLE_DIM, tn=1024, tt=1024, name="mm_d_ple_proj",
                                 slot_cols=D_MODEL // N_DEV)
    g["w_ple_gate"] = _matmul_tn(hp, dgt, tk=1024, tn=1024, tt=1024, name="mm_d_ple_gate")
    def through_norm(dy, xv, dres, gain):
        dx, dgain = _rmsnorm_bwd_rows(dy, xv, dres, gain)
        return dx, dx, dgain

    dx2, dx2_bf, g["g_ple"] = _matmul(
        dgt, w["w_ple_gate"], mode="nt", tm=512, tn=1024, out_dtypes=[F32, BF16], name="mm_dhp",
        epilogue=through_norm, extras=(x2, dx3), row_vecs=(w["g_ple"],), n_row_sums=1)
    g["w_down"] = _matmul_tn(act, dx2_bf, tk=1024, tn=1024, tt=1024, name="mm_d_down")

    def relu_grad(dact, a):
        a = a.astype(F32)
        return (dact * (2.0 * jnp.where(a > 0.0, a * lax.rsqrt(a), 0.0)),)

    du = _matmul(dx2_bf, w["w_down"], mode="nt", tm=1024, tn=1024, out_dtypes=[BF16], name="mm_dact",
                 epilogue=relu_grad, extras=(act,))
    g["w_up"] = _matmul_tn(hm, du, tk=1024, tn=1024, tt=1024, name="mm_d_up", slot_cols=D_FF // N_DEV)
    ready(1, g)
    dx1, dx1_bf, g["g_mlp"] = run(
        "mm_dhm", _matmul, du, w["w_up"], mode="nt", tm=512, tn=1024, out_dtypes=[F32, BF16], name="mm_dhm",
        epilogue=through_norm, extras=(x1, dx2), row_vecs=(w["g_mlp"],), n_row_sums=1)
    g["w_out"] = _matmul_tn(merged, dx1_bf, tk=1024, tn=1024, tt=1024, name="mm_d_out")
    def merge_bwd(dm, ga, gb, yav, ybv):
        sa, sb = _sig(ga), _sig(gb)
        return dm * sa, dm * sb, dm * yav * sa * (1.0 - sa), dm * ybv * sb * (1.0 - sb)

    dya, dyb, dga, dgb = _matmul(dx1_bf, w["w_out"], mode="nt", tm=1024, tn=gate_tile, out_dtypes=[BF16] * 4,
                                 name="mm_dmerged", epilogue=merge_bwd, extras=(z, z, ya, yb),
                                 extra_col_blocks=(ga_at, gb_at, 0, 0))
    g["w_rnn_proj"] = _matmul_tn(ya_in, dya, tk=1024, tn=1024, tt=1024, name="mm_d_rnn_proj")
    g["w_attn_proj"] = _matmul_tn(o, dyb, tk=1024, tn=1024, tt=1024, name="mm_d_attn_proj")
    ready(2, g)
    dya_in = run("mm_dya_in", _matmul, dya, w["w_rnn_proj"], mode="nt", tm=1024, tn=1024, out_dtypes=[F32],
                 name="mm_dya_in")
    do = _matmul(dyb, w["w_attn_proj"], mode="nt", tm=1024, tn=1024, out_dtypes=[BF16], name="mm_do")
    dx_rnn, dg_rnn, dwrg_dense, dwig_dense, rnn_vec = run(
        "rnn_bwd", _rnn_bwd, dya_in, z, xc, hr, w["conv_w"], wrg_bd, w["b_rg"], wig_bd, w["b_ig"],
        w["lru_lambda"], **dims)
    dz_attn, attn_vec = run("attn_bwd", _attn_bwd, z, o, lse, do, cos_t, sin_t, q_gain_t, k_gain_t, sinks_t,
                            **dims)
    dz_parts = (dx_rnn, dg_rnn, dz_attn, dga, dgb)
    g["w_rg"] = _block_diag_extract(dwrg_dense)
    g["w_ig"] = _block_diag_extract(dwig_dense)
    g["b_rg"], g["b_ig"], g["lru_lambda"], g["conv_b"] = (rnn_vec[i:i + 1] for i in range(4))
    g["conv_w"] = rnn_vec[4:8]
    attn_vec = attn_vec[0] if n_seq == 1 else functools.reduce(jnp.add, [attn_vec[b] for b in range(n_seq)])
    g["q_gain"] = attn_vec[0].reshape(N_Q_HEADS, HEAD_DIM).sum(axis=0)[None, :]
    g["k_gain"] = attn_vec[1, :KV_W].reshape(N_KV_HEADS, HEAD_DIM).sum(axis=0)[None, :]
    g["sinks"] = attn_vec[2:3, :N_Q_HEADS]
    ready(SMALL_BATCH, g, {LOSS_ROW: loss_row})
    g["w_in"] = jnp.concatenate(
        list(run("mm_d_in_rnn", _matmul_tn_multi, h, dz_parts[:2], tt=1024, name="mm_d_in_rnn"))
        + list(run("mm_d_in_rest", _matmul_tn_multi, h, dz_parts[2:], tt=512, name="mm_d_in_rest")), axis=1)
    ready(3, g)
    w_in_attn, w_in_gate = w["w_in"][:, COL_RNN_END:COL_ATTN_END], w["w_in"][:, COL_ATTN_END:]
    windows = ((w["w_in"], (0, D_MODEL)), (w["w_in"], (D_MODEL, D_MODEL)), (w_in_attn, (0, ATTN_W)),
               (w_in_gate, (0, D_MODEL)), (w_in_gate, (D_MODEL, D_MODEL)))
    grad_x, g["g_mix"] = run(
        "mm_dh", _matmul, dz_parts, [wd[0] for wd in windows], mode="nt", tm=256, tn=1024, out_dtypes=[F32],
        name="mm_dh", b_cols=[wd[1] for wd in windows], epilogue=_rmsnorm_bwd_rows, extras=(x, dx1),
        row_vecs=(w["g_mix"],), n_row_sums=1)
    return jnp.sum(loss_row), grad_x, g


MESH_ID = pl.DeviceIdType.MESH


def _coords(index):
    return (index >> 2) & 1, (index >> 1) & 1, index & 1


def _exchange(srcs, kinds, *, name):
    n = len(srcs)
    n_peer = N_DEV - 1

    def body(*refs):
        src, dst = refs[:n], refs[n:2 * n]
        send_sems, recv_sems, local_sems = refs[2 * n:]
        me = 4 * lax.axis_index("x") + 2 * lax.axis_index("y") + lax.axis_index("c")

        def remote(i, d):
            peer = (me + d) & (N_DEV - 1)
            piece = src[i] if kinds[i] == "gather" else src[i].at[peer]
            return pltpu.make_async_remote_copy(
                src_ref=piece, dst_ref=dst[i].at[me], send_sem=send_sems.at[i * n_peer + d - 1],
                recv_sem=recv_sems.at[i * n_peer + d - 1], device_id=_coords(peer), device_id_type=MESH_ID)

        def arrival(i, d):
            sender = (me - d) & (N_DEV - 1)
            piece = src[i] if kinds[i] == "gather" else src[i].at[sender]
            return pltpu.make_async_remote_copy(
                src_ref=piece, dst_ref=dst[i].at[sender], send_sem=send_sems.at[i * n_peer + d - 1],
                recv_sem=recv_sems.at[i * n_peer + d - 1], device_id=_coords(sender), device_id_type=MESH_ID)

        own = []
        for i in range(n):
            piece = src[i] if kinds[i] == "gather" else src[i].at[me]
            own.append(pltpu.make_async_copy(piece, dst[i].at[me], local_sems.at[i]))
            own[-1].start()
        sent = [remote(i, d) for d in range(1, N_DEV) for i in range(n)]
        for cp in sent:
            cp.start()
        for d in range(1, N_DEV):
            for i in range(n):
                arrival(i, d).wait_recv()
        for cp in sent:
            cp.wait_send()
        for cp in own:
            cp.wait()

    def out_of(s, kind):
        shape = s.shape if kind == "scatter" else (N_DEV,) + s.shape
        return jax.ShapeDtypeStruct(shape, s.dtype)

    any_spec = pl.BlockSpec(memory_space=pl.ANY)
    return pl.pallas_call(
        body,
        in_specs=[any_spec] * n,
        out_specs=[any_spec] * n,
        out_shape=[out_of(s, k) for s, k in zip(srcs, kinds)],
        scratch_shapes=[pltpu.SemaphoreType.DMA((n * n_peer,)), pltpu.SemaphoreType.DMA((n * n_peer,)),
                        pltpu.SemaphoreType.DMA((n,))],
        compiler_params=pltpu.CompilerParams(has_side_effects=True),
        name=name,
    )(*srcs)


def _remote(src, dst, send_sem, recv_sem, to):
    return pltpu.make_async_remote_copy(src_ref=src, dst_ref=dst, send_sem=send_sem, recv_sem=recv_sem,
                                        device_id=to, device_id_type=MESH_ID)


def _gather_two_level(shards, *, name):
    n = len(shards)
    per = N_DEV - 1

    def body(*refs):
        src, dst = refs[:n], refs[n:2 * n]
        send_sems, recv_sems, local_sems = refs[2 * n:]
        x, y, c = lax.axis_index("x"), lax.axis_index("y"), lax.axis_index("c")
        me, sibling = (x, y, c), (x, y, 1 - c)
        chips = [(1 - x, y), (x, 1 - y), (1 - x, 1 - y)]

        def slot(pos):
            return 4 * pos[0] + 2 * pos[1] + pos[2]

        def copy(i, k, block, to, from_shard=False):
            source = src[i] if from_shard else dst[i].at[slot(block)]
            return _remote(source, dst[i].at[slot(block)], send_sems.at[i * per + k], recv_sems.at[i * per + k], to)

        mine = [pltpu.make_async_copy(src[i], dst[i].at[slot(me)], local_sems.at[i]) for i in range(n)]
        for cp in mine:
            cp.start()
        first = []
        for i in range(n):
            first.append(copy(i, 0, me, sibling, from_shard=True))
            first += [copy(i, 1 + j, me, (*chip, c), from_shard=True) for j, chip in enumerate(chips)]
        for cp in first:
            cp.start()
        passed = []
        for i in range(n):
            for j, chip in enumerate(chips):
                copy(i, 1 + j, (*chip, c), me).wait_recv()
                passed.append(copy(i, 4 + j, (*chip, c), sibling))
                passed[-1].start()
        for i in range(n):
            copy(i, 0, sibling, me).wait_recv()
            for j, chip in enumerate(chips):
                copy(i, 4 + j, (*chip, 1 - c), me).wait_recv()
        for cp in first + passed:
            cp.wait_send()
        for cp in mine:
            cp.wait()

    any_spec = pl.BlockSpec(memory_space=pl.ANY)
    return pl.pallas_call(
        body,
        in_specs=[any_spec] * n,
        out_specs=[any_spec] * n,
        out_shape=[jax.ShapeDtypeStruct((N_DEV,) + s.shape, s.dtype) for s in shards],
        scratch_shapes=[pltpu.SemaphoreType.DMA((n * per,)), pltpu.SemaphoreType.DMA((n * per,)),
                        pltpu.SemaphoreType.DMA((n,))],
        name=name,
    )(*shards)


CHIPS = N_DEV // 2


def _other_chips(x, y):
    return [(x, 1 - y), (1 - x, y), (1 - x, 1 - y)]


def _hosted_gather_first(shards):
    n = len(shards)
    per = CHIPS

    def plan(src, dst, send_sems, recv_sems, local_sems, first_sem):
        x, y, c = lax.axis_index("x"), lax.axis_index("y"), lax.axis_index("c")
        peers = [(x, y, 1 - c)] + [(*chip, c) for chip in _other_chips(x, y)]
        copies = []
        for i in range(n):
            own = pltpu.make_async_copy(src[i], dst[i].at[4 * x + 2 * y + c], local_sems.at[first_sem + i])
            copies.append(_Xfer(own.start, own.wait))
        for j, peer in enumerate(peers):
            for i in range(n):
                k = first_sem + i * per + j
                out = _remote(src[i], dst[i].at[4 * x + 2 * y + c], send_sems.at[k], recv_sems.at[k], peer)
                arrival = _remote(src[i], dst[i].at[4 * peer[0] + 2 * peer[1] + peer[2]], send_sems.at[k],
                                  recv_sems.at[k], peer)

                def wait(out=out, arrival=arrival):
                    arrival.wait_recv()
                    out.wait_send()

                copies.append(_Xfer(out.start, wait))
        return copies

    out_shape = tuple(jax.ShapeDtypeStruct((N_DEV,) + s.shape, s.dtype) for s in shards)
    return _Hosted(tuple(shards), out_shape, n * per, plan)


def _hosted_gather_second(landed):
    n = len(landed)
    per = CHIPS - 1

    def plan(src, dst, send_sems, recv_sems, local_sems, first_sem):
        x, y, c = lax.axis_index("x"), lax.axis_index("y"), lax.axis_index("c")
        copies = []
        for j, chip in enumerate(_other_chips(x, y)):
            mine, theirs = 4 * chip[0] + 2 * chip[1] + c, 4 * chip[0] + 2 * chip[1] + 1 - c
            for i in range(n):
                k = first_sem + i * per + j
                out = _remote(src[i].at[mine], dst[i].at[mine], send_sems.at[k], recv_sems.at[k], (x, y, 1 - c))
                arrival = _remote(src[i].at[theirs], dst[i].at[theirs], send_sems.at[k], recv_sems.at[k],
                                  (x, y, 1 - c))

                def wait(out=out, arrival=arrival):
                    arrival.wait_recv()
                    out.wait_send()

                copies.append(_Xfer(out.start, wait))
        return copies

    out_shape = tuple(jax.ShapeDtypeStruct(a.shape, a.dtype) for a in landed)
    return _Hosted(tuple(landed), out_shape, n * per, plan, tuple((i, i) for i in range(n)))


def _hosted_sibling_swap(arrays, sliced):
    n_sems = sum(CHIPS if s else 1 for s in sliced)

    def plan(src, dst, send_sems, recv_sems, local_sems, first_sem):
        x, y, c = lax.axis_index("x"), lax.axis_index("y"), lax.axis_index("c")
        sibling = (x, y, 1 - c)
        copies, k = [], first_sem
        for i, is_sliced in enumerate(sliced):
            pieces = [(src[i].at[2 * s + 1 - c], dst[i].at[s]) for s in range(CHIPS)] if is_sliced else [(src[i], dst[i])]
            for source, target in pieces:
                cp = _remote(source, target, send_sems.at[k], recv_sems.at[k], sibling)
                copies.append(_Xfer(cp.start, cp.wait))
                k += 1
        return copies

    out_shape = tuple(jax.ShapeDtypeStruct((CHIPS,) + a.shape[1:] if s else a.shape, a.dtype)
                      for a, s in zip(arrays, sliced))
    return _Hosted(tuple(arrays), out_shape, n_sems, plan)


def _hosted_chip_exchange(arrays, sliced):
    n = len(arrays)
    per = CHIPS - 1

    def plan(src, dst, send_sems, recv_sems, local_sems, first_sem):
        x, y, c = lax.axis_index("x"), lax.axis_index("y"), lax.axis_index("c")
        chip = 2 * x + y
        copies = []
        for i in range(n):
            own = pltpu.make_async_copy(src[i].at[chip] if sliced[i] else src[i], dst[i].at[chip],
                                        local_sems.at[first_sem + i])
            copies.append(_Xfer(own.start, own.wait))
        for d in range(1, CHIPS):
            other = chip ^ d
            to = ((other >> 1) & 1, other & 1, c)
            for i in range(n):
                k = first_sem + i * per + d - 1
                source = src[i].at[other] if sliced[i] else src[i]
                out = _remote(source, dst[i].at[chip], send_sems.at[k], recv_sems.at[k], to)
                arrival = _remote(source, dst[i].at[other], send_sems.at[k], recv_sems.at[k], to)

                def wait(out=out, arrival=arrival):
                    arrival.wait_recv()
                    out.wait_send()

                copies.append(_Xfer(out.start, wait))
        return copies

    out_shape = tuple(jax.ShapeDtypeStruct(a.shape if s else (CHIPS,) + a.shape, a.dtype)
                      for a, s in zip(arrays, sliced))
    return _Hosted(tuple(arrays), out_shape, n * per, plan)


def _add_sibling(parts, received, core, *, name):
    _, r, cols = parts.shape
    tr = min(1024, r)

    def body(core_ref, a_ref, b_ref, o_ref):
        o_ref[...] = (a_ref[...] + b_ref[...]).astype(BF16)

    grid_spec = pltpu.PrefetchScalarGridSpec(
        num_scalar_prefetch=1,
        grid=(CHIPS, r // tr),
        in_specs=[pl.BlockSpec((None, tr, cols), lambda k, i, core_ref: (2 * k + core_ref[0], i, 0)),
                  pl.BlockSpec((None, tr, cols), lambda k, i, core_ref: (k, i, 0))],
        out_specs=pl.BlockSpec((None, tr, cols), lambda k, i, core_ref: (k, i, 0)),
    )
    return pl.pallas_call(body, grid_spec=grid_spec, out_shape=jax.ShapeDtypeStruct((CHIPS, r, cols), BF16),
                          compiler_params=_params("parallel", "parallel"), name=name)(core, parts, received)


def _add_whole(a, b, *, name):
    def body(a_ref, b_ref, o_ref):
        o_ref[...] = a_ref[...] + b_ref[...]

    return pl.pallas_call(body, out_shape=jax.ShapeDtypeStruct(a.shape, F32), name=name)(a, b)


def _adamw(parts, w, m, v, *, name):
    r, c = w.shape
    n_parts = parts.shape[0]
    tr = min(512, r)
    c1 = 1.0 - ADAM_B1 ** ADAM_STEP
    c2 = 1.0 - ADAM_B2 ** ADAM_STEP

    def body(p_ref, w_ref, m_ref, v_ref, g_ref, d_ref, nm_ref, nv_ref):
        g = p_ref[0].astype(F32)
        for s in range(1, n_parts):
            g = g + p_ref[s].astype(F32)
        nm = ADAM_B1 * m_ref[...] + (1.0 - ADAM_B1) * g
        nv = ADAM_B2 * v_ref[...] + (1.0 - ADAM_B2) * (g * g)
        g_ref[...] = g
        nm_ref[...] = nm
        nv_ref[...] = nv
        d_ref[...] = -ADAM_LR * ((nm / c1) / (jnp.sqrt(nv / c2) + ADAM_EPS) + ADAM_WD * w_ref[...])

    tile = pl.BlockSpec((tr, c), lambda i: (i, 0))
    return pl.pallas_call(
        body,
        grid=(r // tr,),
        in_specs=[pl.BlockSpec((n_parts, tr, c), lambda i: (0, i, 0)), tile, tile, tile],
        out_specs=[tile] * 4,
        out_shape=[jax.ShapeDtypeStruct((r, c), F32)] * 4,
        compiler_params=_params("parallel"),
        name=name,
    )(parts, w, m, v)


BIG = ("w_in", "w_rnn_proj", "w_attn_proj", "w_out", "w_up", "w_down", "w_ple_gate", "w_ple_proj")
LOSS_ROW = "loss"
SMALL = (("conv_b", 1), ("b_rg", 1), ("b_ig", 1), ("lru_lambda", 1), ("g_mlp", 1), ("g_ple", 1),
         ("q_gain", 1), ("k_gain", 1), ("sinks", 1), (LOSS_ROW, 1), ("w_rg", 64), ("w_ig", 64))
SMALL_ROWS = 144
COL_SHARDED = ("w_in", "w_up", "w_ple_proj")
BATCHES = {1: ("w_ple_proj", "w_ple_gate", "w_down", "w_up"), 2: ("w_out", "w_rnn_proj", "w_attn_proj"),
           3: ("w_in", "conv_w")}
SMALL_BATCH = 4


def _pack_small(vals):
    rows = []
    for nm, nrow in SMALL:
        flat = vals[nm].reshape(-1).astype(F32)
        rows.append(jnp.pad(flat, (0, nrow * D_MODEL - flat.shape[0])).reshape(nrow, D_MODEL))
    used = sum(nrow for _, nrow in SMALL)
    rows.append(jnp.zeros((SMALL_ROWS - used, D_MODEL), F32))
    return jnp.concatenate(rows, axis=0)


def _unpack_small(packed, shapes):
    out, at = {}, 0
    for nm, nrow in SMALL:
        size = 1
        for s in shapes[nm]:
            size *= s
        out[nm] = packed[at:at + nrow].reshape(-1)[:size].reshape(shapes[nm])
        at += nrow
    return out


def _full_weight(name, landed):
    if name in COL_SHARDED:
        return landed.transpose(1, 0, 2).reshape(landed.shape[1], N_DEV * landed.shape[2])
    return landed.reshape(N_DEV * landed.shape[1], landed.shape[2])


def _owner_slots(name, grad):
    if name == "w_in":
        return grad.reshape(D_MODEL, N_DEV, IN_TOTAL // N_DEV).transpose(1, 0, 2)
    if name == "conv_w":
        return grad.reshape(CONV_W, N_DEV, D_MODEL // N_DEV).transpose(1, 0, 2)
    if name in COL_SHARDED:
        return grad
    return grad.reshape(N_DEV, grad.shape[0] // N_DEV, grad.shape[1])


class _StepExchanges:
    FIRST, SECOND = "first", "second"
    PROJ, OUT, PLE_GATE, UP, DOWN = (("w_rnn_proj", "w_attn_proj"), ("w_out",), ("w_ple_gate",), ("w_up",),
                                     ("w_down", "w_ple_proj"))
    GATHERS = {"mm_in": ((FIRST, PROJ), (FIRST, OUT), (FIRST, PLE_GATE)),
               "rnn_fwd": ((SECOND, PROJ), (SECOND, OUT), (SECOND, PLE_GATE), (FIRST, UP)),
               "attn_fwd": ((SECOND, UP), (FIRST, DOWN)), "mm_rnn_proj": ((SECOND, DOWN),)}
    SWAPS = {"mm_dhm": 1, "mm_dya_in": 2, "mm_d_in_rnn": SMALL_BATCH}
    CHIP_EXCHANGES = {"rnn_bwd": ((1, (0, 1, 2)),), "attn_bwd": ((1, (3,)), (2, None)),
                      "mm_d_in_rest": ((SMALL_BATCH, None),), "mm_dh": ((3, None),)}

    def __init__(self, shards, core):
        self.shards = shards
        self.core = core
        self.parts, self.swapped, self.summed, self.half_gathered = {}, {}, {}, {}

    def ready(self, batch, grads, extra=None):
        if batch == SMALL_BATCH:
            self.parts[batch] = ([_pack_small({**grads, **extra})], [False])
            return
        arrays = [_owner_slots(nm, grads[nm]) for nm in BATCHES[batch]]
        self.parts[batch] = (arrays, [True] * len(arrays))
        if batch not in self.SWAPS.values():
            _, self.swapped[batch] = _call(
                lambda: None, grid=(1,), in_specs=[], out_specs=[], out_shape=[], args=(), name="swap_last",
                semantics=("arbitrary",), hosted=_hosted_sibling_swap(*self.parts[batch]))

    def host(self, tag):
        if tag in self.GATHERS:
            return _merge_hosted([
                _hosted_gather_first([self.shards[nm] for nm in group]) if half == self.FIRST
                else _hosted_gather_second([self.half_gathered[nm] for nm in group])
                for half, group in self.GATHERS[tag]])
        if tag in self.SWAPS:
            return _hosted_sibling_swap(*self.parts[self.SWAPS[tag]])
        if tag in self.CHIP_EXCHANGES:
            hosted = []
            for batch, members in self._exchange_members(tag):
                arrays, sliced = self.parts[batch]
                labels = BATCHES.get(batch, ("small",))
                sums = [_add_sibling(arrays[i], self.swapped[batch][i], self.core, name="add_" + labels[i])
                        if sliced[i] else _add_whole(arrays[i], self.swapped[batch][i], name="add_" + labels[i])
                        for i in members]
                hosted.append(_hosted_chip_exchange(sums, [sliced[i] for i in members]))
            return _merge_hosted(hosted)
        return None

    def _exchange_members(self, tag):
        return [(batch, members if members is not None else tuple(range(len(self.parts[batch][0]))))
                for batch, members in self.CHIP_EXCHANGES[tag]]

    def landed(self, tag, landed, weights):
        if tag in self.GATHERS:
            names = [(half, nm) for half, group in self.GATHERS[tag] for nm in group]
            for (half, nm), buf in zip(names, landed):
                if half == self.FIRST:
                    self.half_gathered[nm] = buf
                else:
                    weights[nm] = _full_weight(nm, buf)
        elif tag in self.SWAPS:
            self.swapped[self.SWAPS[tag]] = landed
        else:
            at = 0
            for batch, members in self._exchange_members(tag):
                for i in members:
                    self.summed.setdefault(batch, {})[i] = landed[at]
                    at += 1


def kernel(x, p, g_mix, w_in, conv_w, conv_b, w_rg, b_rg, w_ig, b_ig, lru_lambda, w_rnn_proj, q_gain, k_gain, sinks, w_attn_proj, w_out, g_mlp, w_up, w_down, g_ple, w_ple_gate, w_ple_proj, loss_target, m_g_mix, m_w_in, m_conv_w, m_conv_b, m_w_rg, m_b_rg, m_w_ig, m_b_ig, m_lru_lambda, m_w_rnn_proj, m_q_gain, m_k_gain, m_sinks, m_w_attn_proj, m_w_out, m_g_mlp, m_w_up, m_w_down, m_g_ple, m_w_ple_gate, m_w_ple_proj, v_g_mix, v_w_in, v_conv_w, v_conv_b, v_w_rg, v_b_rg, v_w_ig, v_b_ig, v_lru_lambda, v_w_rnn_proj, v_q_gain, v_k_gain, v_sinks, v_w_attn_proj, v_w_out, v_g_mlp, v_w_up, v_w_down, v_g_ple, v_w_ple_gate, v_w_ple_proj):
    names = ("g_mix", "w_in", "conv_w", "conv_b", "w_rg", "b_rg", "w_ig", "b_ig", "lru_lambda", "w_rnn_proj",
             "q_gain", "k_gain", "sinks", "w_attn_proj", "w_out", "g_mlp", "w_up", "w_down", "g_ple",
             "w_ple_gate", "w_ple_proj")
    wts = dict(zip(names, (g_mix, w_in, conv_w, conv_b, w_rg, b_rg, w_ig, b_ig, lru_lambda, w_rnn_proj, q_gain,
                           k_gain, sinks, w_attn_proj, w_out, g_mlp, w_up, w_down, g_ple, w_ple_gate, w_ple_proj)))
    mom1 = dict(zip(names, (m_g_mix, m_w_in, m_conv_w, m_conv_b, m_w_rg, m_b_rg, m_w_ig, m_b_ig, m_lru_lambda,
                            m_w_rnn_proj, m_q_gain, m_k_gain, m_sinks, m_w_attn_proj, m_w_out, m_g_mlp, m_w_up,
                            m_w_down, m_g_ple, m_w_ple_gate, m_w_ple_proj)))
    mom2 = dict(zip(names, (v_g_mix, v_w_in, v_conv_w, v_conv_b, v_w_rg, v_b_rg, v_w_ig, v_b_ig, v_lru_lambda,
                            v_w_rnn_proj, v_q_gain, v_k_gain, v_sinks, v_w_attn_proj, v_w_out, v_g_mlp, v_w_up,
                            v_w_down, v_g_ple, v_w_ple_gate, v_w_ple_proj)))
    n_seq, seq, _ = x.shape
    core = lax.axis_index("c").astype(jnp.int32).reshape(1)

    shards = {nm: wts[nm][0].astype(BF16) for nm in BIG}
    w_in_all, conv_all = _gather_two_level([shards["w_in"], conv_w[0]], name="gather_w_in")
    w = {nm: wts[nm] for nm in names if nm not in BIG}
    w["w_rg"], w["w_ig"] = w_rg[0], w_ig[0]
    w["conv_w"] = conv_all.transpose(1, 0, 2).reshape(CONV_W, D_MODEL)
    w["w_in"] = _full_weight("w_in", w_in_all)
    comm = _StepExchanges(shards, core)
    loss_sum, grad_x, g = _local_step(
        x.reshape(n_seq * seq, D_MODEL), p.reshape(n_seq * seq, PLE_DIM), loss_target.reshape(n_seq * seq, D_MODEL),
        w, n_seq=n_seq, seq=seq, comm=comm)
    del loss_sum

    res = {}
    for batch, batch_names in BATCHES.items():
        for i, nm in enumerate(batch_names):
            res[nm] = _adamw(comm.summed[batch][i], wts[nm][0], mom1[nm][0], mom2[nm][0], name="adamw_" + nm)
    g_mix_parts, = _exchange([g["g_mix"]], ["gather"], name="gather_g_mix")
    res["g_mix"] = [r[0] for r in _adamw(g_mix_parts, g_mix, m_g_mix, v_g_mix, name="adamw_g_mix")]
    small_names = [nm for nm, _ in SMALL if nm != LOSS_ROW]
    full_small = {}
    for src, key in ((wts, "w"), (mom1, "m"), (mom2, "v")):
        vals = {nm: src[nm][0] for nm in small_names}
        vals[LOSS_ROW] = jnp.zeros((1,), F32)
        full_small[key] = _pack_small(vals)
    small_res = _adamw(comm.summed[SMALL_BATCH][0],full_small["w"], full_small["m"], full_small["v"], name="adamw_small")
    shapes = {nm: wts[nm].shape[1:] for nm in small_names}
    shapes[LOSS_ROW] = (D_MODEL,)
    small_out = [_unpack_small(r, shapes) for r in small_res]
    for nm in small_names:
        res[nm] = [so[nm] for so in small_out]
    loss = jnp.sum(small_out[0][LOSS_ROW]) * (0.5 / D_MODEL)

    outs = [loss, grad_x.reshape(n_seq, seq, D_MODEL)]
    for k in range(4):
        outs.extend(res[nm][k][None] for nm in names)
    return tuple(outs)
```

```python
import functools
from typing import Callable, NamedTuple

import jax
import jax.numpy as jnp
from jax import lax
from jax.experimental import pallas as pl
from jax.experimental.pallas import tpu as pltpu

F32 = jnp.float32
BF16 = jnp.bfloat16

N_DEV = 8
D_MODEL = 1024
RNN_BLOCK_W = 64
CONV_W = 4
LRU_C = 8.0
HEAD_DIM = 64
N_Q_HEADS = 16
N_KV_HEADS = 4
KV_W = N_KV_HEADS * HEAD_DIM
WINDOW = 128
ROPE_THETA = 10000.0
D_FF = 4096
PLE_DIM = 256
NORM_EPS = 1e-6
IN_TOTAL = 5632
COL_RNN_END, COL_ATTN_END = 2048, 3584
ATTN_W = COL_ATTN_END - COL_RNN_END
ATTN_K_AT, ATTN_V_AT = 1024, 1280

ADAM_LR = 0.001
ADAM_B1 = 0.9
ADAM_B2 = 0.999
ADAM_EPS = 1e-08
ADAM_WD = 0.01
ADAM_STEP = 10

LANES = 128
SUBLANES = 8
RNN_TILE = 256
VMEM_LIMIT = 48 * 1024 * 1024
NEG_BIG = -1e30


def _params(*sem):
    return pltpu.CompilerParams(dimension_semantics=sem if sem else None, vmem_limit_bytes=VMEM_LIMIT)


def _sig(x):
    return 0.5 * jnp.tanh(0.5 * x) + 0.5


def _dot_nt(a, b):
    return lax.dot_general(a, b, (((1,), (1,)), ((), ())), preferred_element_type=F32)


def _dot_tn(a, b):
    return lax.dot_general(a, b, (((0,), (0,)), ((), ())), preferred_element_type=F32)


class _Xfer:
    def __init__(self, start, wait):
        self.start, self.wait = start, wait


class _Hosted(NamedTuple):
    srcs: tuple
    out_shape: tuple
    n_sems: int
    plan: Callable
    aliases: tuple = ()


def _merge_hosted(parts):
    parts = [p for p in parts if p is not None]
    if len(parts) <= 1:
        return parts[0] if parts else None
    src_at, dst_at, sem_at, aliases = [0], [0], [0], []
    for p in parts:
        aliases += [(i + src_at[-1], j + dst_at[-1]) for i, j in p.aliases]
        src_at.append(src_at[-1] + len(p.srcs))
        dst_at.append(dst_at[-1] + len(p.out_shape))
        sem_at.append(sem_at[-1] + p.n_sems)

    def plan(src, dst, send_sems, recv_sems, local_sems, first_sem):
        copies = []
        for k, p in enumerate(parts):
            copies += p.plan(src[src_at[k]:src_at[k + 1]], dst[dst_at[k]:dst_at[k + 1]], send_sems, recv_sems,
                             local_sems, first_sem + sem_at[k])
        return copies

    return _Hosted(tuple(a for p in parts for a in p.srcs), tuple(s for p in parts for s in p.out_shape),
                   sem_at[-1], plan, tuple(aliases))


def _call(body, *, grid, in_specs, out_specs, out_shape, args, name, semantics, scratch_shapes=(), hosted=None):
    if hosted is None:
        outs = pl.pallas_call(body, grid=grid, in_specs=list(in_specs), out_specs=list(out_specs),
                              out_shape=list(out_shape), scratch_shapes=list(scratch_shapes),
                              compiler_params=_params(*semantics), name=name)(*args)
        return list(outs), []
    counts = (len(in_specs), len(hosted.srcs), len(out_specs), len(hosted.out_shape), len(scratch_shapes), 3)

    def wrapped(*refs):
        at, groups = 0, []
        for count in counts:
            groups.append(refs[at:at + count])
            at += count
        ins, srcs, outs, dsts, scratch, sems = groups
        copies = hosted.plan(srcs, dsts, *sems, 0)
        ids = [pl.program_id(axis) for axis in range(len(grid))]
        first = functools.reduce(jnp.logical_and, [i == 0 for i in ids])
        last = functools.reduce(jnp.logical_and, [i == g - 1 for i, g in zip(ids, grid)])

        @pl.when(first)
        def _():
            for cp in copies:
                cp.start()

        body(*ins, *outs, *scratch)

        @pl.when(last)
        def _():
            for cp in copies:
                cp.wait()

    any_spec = pl.BlockSpec(memory_space=pl.ANY)
    sems = [pltpu.SemaphoreType.DMA((hosted.n_sems,))] * 3
    outs = pl.pallas_call(
        wrapped, grid=grid, in_specs=list(in_specs) + [any_spec] * counts[1],
        out_specs=list(out_specs) + [any_spec] * counts[3], out_shape=list(out_shape) + list(hosted.out_shape),
        scratch_shapes=list(scratch_shapes) + sems, compiler_params=_params(*["arbitrary"] * len(grid)),
        input_output_aliases={counts[0] + i: counts[2] + j for i, j in hosted.aliases},
        name=name)(*args, *hosted.srcs)
    return list(outs[:counts[2]]), list(outs[counts[2]:])


def _dividing_tile(n, want):
    tile = min(want, n)
    while n % tile:
        tile -= LANES
    return tile


def _matmul(a, b, *, mode, tm, tn, out_dtypes, name, epilogue=None, extras=(), hosted=None, b_cols=None,
            row_vecs=(), n_row_sums=0, extra_col_blocks=None):
    a_parts = tuple(a) if isinstance(a, (tuple, list)) else (a,)
    b_parts = tuple(b) if isinstance(b, (tuple, list)) else (b,)
    assert len(a_parts) == len(b_parts) and (mode == "nt" or len(a_parts) == 1)
    n_parts = len(a_parts)
    m = a_parts[0].shape[0]
    if b_cols is None:
        b_cols = [(0, bp.shape[1]) for bp in b_parts]
    n = b_cols[0][1] if mode == "nn" else b_parts[0].shape[0]
    tm, tn = min(tm, m), _dividing_tile(n, tn)
    n_extra = len(extras) + len(row_vecs)
    n_tiles_out = len(out_dtypes)
    assert n_row_sums == 0 or n == tn

    def body(*refs):
        a_refs, b_refs = refs[:n_parts], refs[n_parts:2 * n_parts]
        rest = refs[2 * n_parts:]
        extra_refs, out_refs = rest[:n_extra], rest[n_extra:]
        if mode == "nn":
            acc = jnp.dot(a_refs[0][...], b_refs[0][...], preferred_element_type=F32)
        else:
            acc = _dot_nt(a_refs[0][...], b_refs[0][...])
            for a_ref, b_ref in zip(a_refs[1:], b_refs[1:]):
                acc = acc + _dot_nt(a_ref[...], b_ref[...])
        res = epilogue(acc, *[e[...] for e in extra_refs]) if epilogue is not None else (acc,)
        for o_ref, r in zip(out_refs[:n_tiles_out], res):
            o_ref[...] = r.astype(o_ref.dtype)
        if n_row_sums:
            @pl.when(pl.program_id(0) == 0)
            def _():
                for o_ref in out_refs[n_tiles_out:]:
                    o_ref[...] = jnp.zeros_like(o_ref)

            for o_ref, r in zip(out_refs[n_tiles_out:], res[n_tiles_out:]):
                o_ref[...] += r

    a_specs = [pl.BlockSpec((tm, ap.shape[1]), lambda i, j: (i, 0)) for ap in a_parts]
    if mode == "nn":
        assert b_cols[0][0] % tn == 0
        first = b_cols[0][0] // tn
        b_specs = [pl.BlockSpec((b_parts[0].shape[0], tn), lambda i, j: (0, first + j))]
    else:
        assert all(at % width == 0 for at, width in b_cols)
        b_specs = [pl.BlockSpec((tn, width), functools.partial(lambda i, j, blk: (j, blk), blk=at // width))
                   for at, width in b_cols]
    tile = pl.BlockSpec((tm, tn), lambda i, j: (i, j))
    row = pl.BlockSpec((1, tn), lambda i, j: (0, j))
    extra_specs = [pl.BlockSpec((tm, tn), functools.partial(lambda i, j, first: (i, first + j), first=first))
                   for first in (extra_col_blocks or [0] * len(extras))]
    outs, landed = _call(
        body,
        grid=(m // tm, n // tn),
        in_specs=a_specs + b_specs + extra_specs + [row] * len(row_vecs),
        out_specs=[tile] * n_tiles_out + [row] * n_row_sums,
        out_shape=[jax.ShapeDtypeStruct((m, n), dt) for dt in out_dtypes]
        + [jax.ShapeDtypeStruct((1, n), F32)] * n_row_sums,
        args=(*a_parts, *b_parts, *extras, *row_vecs), name=name,
        semantics=("arbitrary" if n_row_sums else "parallel", "arbitrary"), hosted=hosted)
    if hosted is not None:
        return (*outs, landed)
    return outs[0] if len(outs) == 1 else outs


def _matmul_tn(a, b, *, tk, tn, tt, name, slot_cols=None):
    t, k = a.shape
    n = b.shape[1]
    tk, tn, tt = min(tk, k), _dividing_tile(n, tn), min(tt, t)

    def body(a_ref, b_ref, o_ref):
        @pl.when(pl.program_id(2) == 0)
        def _():
            o_ref[...] = jnp.zeros_like(o_ref)

        if slot_cols is None:
            o_ref[...] += _dot_tn(a_ref[...], b_ref[...])
        else:
            av = a_ref[...]
            for s in range(tn // slot_cols):
                o_ref[s] += _dot_tn(av, b_ref[:, s * slot_cols:(s + 1) * slot_cols])

    if slot_cols is not None:
        out_spec = pl.BlockSpec((tn // slot_cols, tk, slot_cols), lambda i, j, s: (j, i, 0))
        out_shape = jax.ShapeDtypeStruct((n // slot_cols, k, slot_cols), F32)
    else:
        out_spec = pl.BlockSpec((tk, tn), lambda i, j, s: (i, j))
        out_shape = jax.ShapeDtypeStruct((k, n), F32)
    return pl.pallas_call(
        body,
        grid=(k // tk, n // tn, t // tt),
        in_specs=[pl.BlockSpec((tt, tk), lambda i, j, s: (s, i)), pl.BlockSpec((tt, tn), lambda i, j, s: (s, j))],
        out_specs=out_spec,
        out_shape=out_shape,
        compiler_params=_params("parallel", "parallel", "arbitrary"),
        name=name,
    )(a, b)


def _matmul_tn_multi(a, bs, *, tt, name, hosted=None):
    t, k = a.shape
    tt = min(tt, t)
    n_b = len(bs)

    def body(a_ref, *refs):
        b_refs, o_refs = refs[:n_b], refs[n_b:]

        @pl.when(pl.program_id(0) == 0)
        def _():
            for o_ref in o_refs:
                o_ref[...] = jnp.zeros_like(o_ref)

        a_t = a_ref[...].T
        for b_ref, o_ref in zip(b_refs, o_refs):
            o_ref[...] += jnp.dot(a_t, b_ref[...], preferred_element_type=F32)

    outs, landed = _call(
        body,
        grid=(t // tt,),
        in_specs=[pl.BlockSpec((tt, k), lambda s: (s, 0))] + [pl.BlockSpec((tt, b.shape[1]), lambda s: (s, 0)) for b in bs],
        out_specs=[pl.BlockSpec((k, b.shape[1]), lambda s: (0, 0)) for b in bs],
        out_shape=[jax.ShapeDtypeStruct((k, b.shape[1]), F32) for b in bs],
        args=(a, *bs), name=name, semantics=("arbitrary",), hosted=hosted)
    return (*outs, landed) if hosted is not None else outs


def _rmsnorm_rows(x, g):
    return x * lax.rsqrt(jnp.mean(x * x, axis=-1, keepdims=True) + NORM_EPS) * g


def _norm_matmul(x, g, b, *, tm, tn, name, hosted=None):
    m, k = x.shape
    n = b.shape[1]
    tm, tn = min(tm, m), _dividing_tile(n, tn)

    def body(x_ref, g_ref, b_ref, z_ref, h_ref, h_s):
        @pl.when(pl.program_id(1) == 0)
        def _():
            h_s[...] = _rmsnorm_rows(x_ref[...], g_ref[...]).astype(BF16)
            h_ref[...] = h_s[...]

        z_ref[...] = jnp.dot(h_s[...], b_ref[...], preferred_element_type=F32)

    rows = pl.BlockSpec((tm, k), lambda i, j: (i, 0))
    outs, landed = _call(
        body,
        grid=(m // tm, n // tn),
        in_specs=[rows, pl.BlockSpec((1, k), lambda i, j: (0, 0)), pl.BlockSpec((k, tn), lambda i, j: (0, j))],
        out_specs=[pl.BlockSpec((tm, tn), lambda i, j: (i, j)), rows],
        out_shape=[jax.ShapeDtypeStruct((m, n), F32), jax.ShapeDtypeStruct((m, k), BF16)],
        scratch_shapes=[pltpu.VMEM((tm, k), BF16)],
        args=(x, g, b), name=name, semantics=("parallel", "arbitrary"), hosted=hosted)
    return (*outs, landed) if hosted is not None else outs


def _rmsnorm_bwd_rows(dy, x, dres, g):
    r = lax.rsqrt(jnp.mean(x * x, axis=-1, keepdims=True) + NORM_EPS)
    xr = x * r
    gy = dy * g
    dx = dres + r * (gy - xr * jnp.mean(gy * xr, axis=-1, keepdims=True))
    return dx, jnp.sum(dy * xr, axis=0, keepdims=True)


def _softplus_neg(lam):
    z = -lam
    return jnp.maximum(z, 0.0) + jnp.log1p(jnp.exp(-jnp.abs(z)))


def _neg_expm1(y, exp_half_y):
    series = -y * (1.0 + y * 0.5 * (1.0 + y * (1.0 / 3.0) * (1.0 + y * 0.25 * (1.0 + y * 0.2))))
    return jnp.where(y > -0.0625, series, 1.0 - exp_half_y * exp_half_y)


def _gelu_parts(x):
    c = 0.7978845608028654
    u = c * (x + 0.044715 * x * x * x)
    th = jnp.tanh(u)
    gel = 0.5 * x * (1.0 + th)
    dgel = 0.5 * (1.0 + th) + 0.5 * x * (1.0 - th * th) * c * (1.0 + 3.0 * 0.044715 * x * x)
    return gel, dgel


def _shift_down(v, k, rows):
    return jnp.where(rows < k, 0.0, pltpu.roll(v, k, 0))


def _shift_up(v, k, rows, n):
    return jnp.where(rows >= n - k, 0.0, pltpu.roll(v, n - k, 0))


def _scan_within_groups(a, b, *, reverse):
    shape = a.shape
    a = a.reshape(shape[0] // SUBLANES, SUBLANES, shape[1])
    b = b.reshape(a.shape)
    in_group = lax.broadcasted_iota(jnp.int32, a.shape, 1)
    for s in (1, 2, 4):
        if reverse:
            inside, shift = in_group < SUBLANES - s, SUBLANES - s
        else:
            inside, shift = in_group >= s, s
        b = b + a * jnp.where(inside, pltpu.roll(b, shift, 1), 0.0)
        a = a * jnp.where(inside, pltpu.roll(a, shift, 1), 1.0)
    return a.reshape(shape), b.reshape(shape)


def _rnn_gates(xc, wrg, brg, wig, big, lam):
    xcb = xc.astype(BF16)
    r = _sig(jnp.dot(xcb, wrg, preferred_element_type=F32) + brg)
    i = _sig(jnp.dot(xcb, wig, preferred_element_type=F32) + big)
    sp = _softplus_neg(lam)
    log_a = -LRU_C * r * sp
    a = jnp.exp(log_a)
    mult = jnp.sqrt(_neg_expm1(2.0 * log_a, a))
    return xcb, r, i, sp, a, mult


def _conv_fwd(xv, cw, cb, rows):
    return (cb + _shift_down(xv, 3, rows) * cw[0:1, :] + _shift_down(xv, 2, rows) * cw[1:2, :]
            + _shift_down(xv, 1, rows) * cw[2:3, :] + xv * cw[3:4, :])


def _rnn_fwd(z, conv_w, conv_b, wrg_bd, b_rg, wig_bd, b_ig, lam, *, n_seq, seq, hosted=None):
    t = n_seq * seq
    ct = RNN_TILE
    n_ct = D_MODEL // ct

    def body(x_ref, g_ref, cw_ref, cb_ref, wrg_ref, brg_ref, wig_ref, big_ref, lam_ref,
             xc_ref, hr_ref, ya_ref, a_s, b_s):
        rows = lax.broadcasted_iota(jnp.int32, (seq, ct), 0)
        xc = _conv_fwd(x_ref[...], cw_ref[...], cb_ref[...], rows)
        _, r, i, sp, a, mult = _rnn_gates(xc, wrg_ref[...], brg_ref[...], wig_ref[...], big_ref[...], lam_ref[...])
        a_s[...], b_s[...] = _scan_within_groups(a, mult * (i * xc), reverse=False)

        def step(j, carry):
            r0 = pl.multiple_of(j * SUBLANES, SUBLANES)
            h = b_s[pl.ds(r0, SUBLANES), :] + a_s[pl.ds(r0, SUBLANES), :] * carry
            hr_ref[pl.ds(r0, SUBLANES), :] = h
            return h[SUBLANES - 1:SUBLANES, :]

        lax.fori_loop(0, seq // SUBLANES, step, jnp.zeros((1, ct), F32), unroll=4)
        gel, _ = _gelu_parts(g_ref[...])
        xc_ref[...] = xc
        ya_ref[...] = (hr_ref[...] * gel).astype(BF16)

    vec = pl.BlockSpec((1, ct), lambda b, c: (0, c))
    gate_w = pl.BlockSpec((None, ct, ct), lambda b, c: (c, 0, 0))
    tile = pl.BlockSpec((seq, ct), lambda b, c: (b, c))
    outs, landed = _call(
        body,
        grid=(n_seq, n_ct),
        in_specs=[
            pl.BlockSpec((seq, ct), lambda b, c: (b, c)),
            pl.BlockSpec((seq, ct), lambda b, c: (b, n_ct + c)),
            pl.BlockSpec((CONV_W, ct), lambda b, c: (0, c)), vec, gate_w, vec, gate_w, vec, vec,
        ],
        out_specs=[tile, tile, tile],
        out_shape=[jax.ShapeDtypeStruct((t, D_MODEL), F32), jax.ShapeDtypeStruct((t, D_MODEL), F32),
                   jax.ShapeDtypeStruct((t, D_MODEL), BF16)],
        scratch_shapes=[pltpu.VMEM((seq, ct), F32), pltpu.VMEM((seq, ct), F32)],
        args=(z, z, conv_w, conv_b, wrg_bd, b_rg, wig_bd, b_ig, lam), name="rnn_fwd",
        semantics=("parallel", "parallel"), hosted=hosted)
    return (*outs, landed) if hosted is not None else outs


def _rnn_bwd(dya, z, xc, hr, conv_w, wrg_bd, b_rg, wig_bd, b_ig, lam, *, n_seq, seq, hosted=None):
    t = n_seq * seq
    ct = RNN_TILE
    n_ct = D_MODEL // ct

    def body(dya_ref, x_ref, g_ref, xc_ref, hr_ref, cw_ref, wrg_ref, brg_ref, wig_ref, big_ref, lam_ref,
             dx_ref, dg_ref, dwrg_ref, dwig_ref, vec_ref, a_s, d_s, g_s):
        rows = lax.broadcasted_iota(jnp.int32, (seq, ct), 0)
        xv, xc, hr, dyv = x_ref[...], xc_ref[...], hr_ref[...], dya_ref[...]
        lamv = lam_ref[...]
        gel, dgel = _gelu_parts(g_ref[...])
        dg_ref[...] = (dyv * hr * dgel).astype(BF16)
        xcb, r, i, sp, a, mult = _rnn_gates(xc, wrg_ref[...], brg_ref[...], wig_ref[...], big_ref[...], lamv)
        a_s[...], d_s[...] = _scan_within_groups(_shift_up(a, 1, rows, seq), dyv * gel, reverse=True)

        def step(k, carry):
            r0 = pl.multiple_of((seq // SUBLANES - 1 - k) * SUBLANES, SUBLANES)
            gs = d_s[pl.ds(r0, SUBLANES), :] + a_s[pl.ds(r0, SUBLANES), :] * carry
            g_s[pl.ds(r0, SUBLANES), :] = gs
            return gs[0:1, :]

        lax.fori_loop(0, seq // SUBLANES, step, jnp.zeros((1, ct), F32), unroll=4)
        gsum = g_s[...]
        gated = i * xc
        d_log_a = gsum * _shift_down(hr, 1, rows) * a - gsum * gated * (a * a / mult)
        d_gated = gsum * mult
        d_pre_r = (d_log_a * (-LRU_C) * sp) * r * (1.0 - r)
        d_pre_i = (d_gated * xc) * i * (1.0 - i)
        dprb, dpib = d_pre_r.astype(BF16), d_pre_i.astype(BF16)
        dxc = d_gated * i + _dot_nt(dprb, wrg_ref[...]) + _dot_nt(dpib, wig_ref[...])
        cw = cw_ref[...]
        dx = dxc * cw[CONV_W - 1:CONV_W, :]
        d_taps = [jnp.sum(dxc * xv, axis=0, keepdims=True)]
        for k in range(1, CONV_W):
            up = _shift_up(dxc, k, rows, seq)
            dx = dx + up * cw[CONV_W - 1 - k:CONV_W - k, :]
            d_taps.append(jnp.sum(up * xv, axis=0, keepdims=True))
        dx_ref[...] = dx.astype(BF16)

        @pl.when(pl.program_id(1) == 0)
        def _():
            dwrg_ref[...] = jnp.zeros_like(dwrg_ref)
            dwig_ref[...] = jnp.zeros_like(dwig_ref)
            vec_ref[...] = jnp.zeros_like(vec_ref)

        dwrg_ref[...] += _dot_tn(xcb, dprb)
        dwig_ref[...] += _dot_tn(xcb, dpib)

        def colsum(v):
            return jnp.sum(v, axis=0, keepdims=True)

        d_sp = colsum(d_log_a * (-LRU_C) * r)
        vec_ref[0:1, :] += colsum(d_pre_r)
        vec_ref[1:2, :] += colsum(d_pre_i)
        vec_ref[2:3, :] += d_sp * (-_sig(-lamv))
        vec_ref[3:4, :] += colsum(dxc)
        for tap in range(CONV_W):
            vec_ref[4 + tap:5 + tap, :] += d_taps[CONV_W - 1 - tap]

    vec = pl.BlockSpec((1, ct), lambda c, b: (0, c))
    gate_w = pl.BlockSpec((None, ct, ct), lambda c, b: (c, 0, 0))
    tile = pl.BlockSpec((seq, ct), lambda c, b: (b, c))
    outs, landed = _call(
        body,
        grid=(n_ct, n_seq),
        in_specs=[
            tile,
            pl.BlockSpec((seq, ct), lambda c, b: (b, c)),
            pl.BlockSpec((seq, ct), lambda c, b: (b, n_ct + c)),
            tile, tile,
            pl.BlockSpec((CONV_W, ct), lambda c, b: (0, c)), gate_w, vec, gate_w, vec, vec,
        ],
        out_specs=[tile, tile, gate_w, gate_w, pl.BlockSpec((8, ct), lambda c, b: (0, c))],
        out_shape=[jax.ShapeDtypeStruct((t, D_MODEL), BF16), jax.ShapeDtypeStruct((t, D_MODEL), BF16),
                   jax.ShapeDtypeStruct((n_ct, ct, ct), F32), jax.ShapeDtypeStruct((n_ct, ct, ct), F32),
                   jax.ShapeDtypeStruct((8, D_MODEL), F32)],
        scratch_shapes=[pltpu.VMEM((seq, ct), F32)] * 3,
        args=(dya, z, z, xc, hr, conv_w, wrg_bd, b_rg, wig_bd, b_ig, lam), name="rnn_bwd",
        semantics=("parallel", "arbitrary"), hosted=hosted)
    return (*outs, landed) if hosted is not None else outs


def _split_hi_lo(x):
    hi = x.astype(BF16)
    return hi, (x - hi.astype(F32)).astype(BF16)


def _dot_split(x, m_twice):
    hi, lo = _split_hi_lo(x)
    return jnp.dot(jnp.concatenate([hi, lo], axis=1), m_twice, preferred_element_type=F32)


def _head_matrices(width):
    ec = ((lax.broadcasted_iota(jnp.int32, (2 * width, LANES), 0) & (width - 1)) // HEAD_DIM
          == lax.broadcasted_iota(jnp.int32, (2 * width, LANES), 1))
    ee = (lax.broadcasted_iota(jnp.int32, (2 * LANES, width), 1) // HEAD_DIM
          == (lax.broadcasted_iota(jnp.int32, (2 * LANES, width), 0) & (LANES - 1)))
    return jnp.where(ec, 1.0, 0.0).astype(BF16), jnp.where(ee, 1.0, 0.0).astype(BF16)


def _swap_halves(y):
    w = y.shape[1]
    first = (lax.broadcasted_iota(jnp.int32, y.shape, 1) % HEAD_DIM) < HEAD_DIM // 2
    return jnp.where(first, pltpu.roll(y, w - HEAD_DIM // 2, 1), pltpu.roll(y, HEAD_DIM // 2, 1))


def _normrope_fwd(x, gain, cos_t, sin_t, ec, ee):
    w = x.shape[1]
    rs = _dot_split(lax.rsqrt(_dot_split(x * x, ec) * (1.0 / HEAD_DIM) + NORM_EPS), ee)
    nx = x * rs
    y = nx * gain
    reps = w // LANES
    out = y * jnp.tile(cos_t, (1, reps)) + _swap_halves(y) * jnp.tile(sin_t, (1, reps))
    return out, nx, rs


def _normrope_bwd(dout, nx, rs, gain, cos_t, sin_t, ec, ee):
    w = dout.shape[1]
    reps = w // LANES
    dy = dout * jnp.tile(cos_t, (1, reps)) + _swap_halves(dout * jnp.tile(sin_t, (1, reps)))
    dgain = jnp.sum(dy * nx, axis=0, keepdims=True)
    dn = dy * gain
    seg = _dot_split(_dot_split(dn * nx, ec) * (1.0 / HEAD_DIM), ee)
    return rs * (dn - nx * seg), dgain


def _pair_operand(t, group):
    chunk = t[:, (group // 2) * LANES:(group // 2 + 1) * LANES]
    low = lax.broadcasted_iota(jnp.int32, chunk.shape, 1) < HEAD_DIM
    rolled = pltpu.roll(chunk, HEAD_DIM, 1)
    return jnp.where(low, chunk, rolled) if group % 2 == 0 else jnp.where(low, rolled, chunk)


GROUP = N_Q_HEADS // N_KV_HEADS
GROUP_W = GROUP * HEAD_DIM


def _replicate_head(t, group):
    return jnp.tile(_pair_operand(t, group), (1, 2))


def _head_blocks(t):
    seg = lax.broadcasted_iota(jnp.int32, t.shape, 1) // HEAD_DIM
    return jnp.concatenate([jnp.where(seg == h, t, 0.0) for h in range(GROUP)], axis=0)


def _stack_heads(t_t, rows):
    return jnp.concatenate([t_t[:, h * rows:(h + 1) * rows] for h in range(GROUP)], axis=0)


def _head_rows(mat_t, group):
    return jnp.concatenate([mat_t[GROUP * group + h:GROUP * group + h + 1, :] for h in range(GROUP)], axis=1)


def _window_masks(blk):
    key = lax.broadcasted_iota(jnp.int32, (blk, GROUP * blk), 0)
    query = lax.broadcasted_iota(jnp.int32, (blk, GROUP * blk), 1) & (blk - 1)
    return key > query, key <= query


def _mask_window(t, before_ok, own_ok, fill):
    blk = t.shape[0] // 2
    return jnp.concatenate([jnp.where(before_ok, t[:blk], fill), jnp.where(own_ok, t[blk:], fill)], axis=0)


def _attn_fwd(z, cos_t, sin_t, q_gain_t, k_gain_t, sinks_t, *, n_seq, seq, hosted=None):
    t = n_seq * seq
    blk = WINDOW
    nb = seq // blk

    def body(q_ref, kp_ref, kc_ref, vp_ref, vc_ref, cosc_ref, sinc_ref, cosp_ref, sinp_ref, qg_ref, kg_ref, sk_ref,
             o_ref, l_ref):
        n = pl.program_id(1)
        ecq, eeq = _head_matrices(D_MODEL)
        eck, eek = _head_matrices(KV_W)
        cosc, sinc = cosc_ref[...], sinc_ref[...]
        qh, _, _ = _normrope_fwd(q_ref[...], qg_ref[...], cosc, sinc, ecq, eeq)
        qh = qh * (HEAD_DIM ** -0.5)
        kc, _, _ = _normrope_fwd(kc_ref[...], kg_ref[...], cosc, sinc, eck, eek)
        kp, _, _ = _normrope_fwd(kp_ref[...], kg_ref[...], cosp_ref[...], sinp_ref[...], eck, eek)
        kcat = jnp.concatenate([kp, kc], axis=0)
        vcat = jnp.concatenate([vp_ref[...], vc_ref[...]], axis=0)
        above, causal = _window_masks(blk)
        above = above & (n > 0)
        head_row = lax.broadcasted_iota(jnp.int32, (blk, blk), 0)
        sk_t = jnp.broadcast_to(sk_ref[...], (blk, LANES)).T
        vcat_t = vcat.T.astype(BF16)
        lmat = jnp.zeros((blk, blk), F32)
        groups = range(N_KV_HEADS)
        cols = [slice(g * GROUP_W, (g + 1) * GROUP_W) for g in groups]
        qh = qh.astype(BF16)
        scores = [_dot_nt(_replicate_head(kcat, g).astype(BF16), _head_blocks(qh[:, cols[g]]))
                  for g in groups]
        probs = []
        for g in groups:
            s = _mask_window(scores[g], above, causal, NEG_BIG)
            sink = _head_rows(sk_t, g)
            m = jnp.maximum(jnp.max(s, axis=0, keepdims=True), sink)
            e = jnp.exp(s - m)
            den = jnp.sum(e, axis=0, keepdims=True) + jnp.exp(sink - m)
            probs.append((e * (1.0 / den)).astype(BF16))
            lse = m + jnp.log(den)
            for h in range(GROUP):
                lmat = lmat + jnp.where(head_row == GROUP * g + h, lse[:, h * blk:(h + 1) * blk], 0.0)
        for g in groups:
            out_t = jnp.dot(vcat_t[g * HEAD_DIM:(g + 1) * HEAD_DIM], probs[g], preferred_element_type=F32)
            o_ref[:, cols[g]] = _stack_heads(out_t, blk).T.astype(BF16)
        l_ref[...] = lmat

    def row(b, n):
        return b * nb + n

    def prev(b, n):
        return b * nb + jnp.maximum(n - 1, 0)

    kw = KV_W
    tab_c = pl.BlockSpec((blk, LANES), lambda b, n: (n, 0))
    tab_p = pl.BlockSpec((blk, LANES), lambda b, n: (jnp.maximum(n - 1, 0), 0))
    outs, landed = _call(
        body,
        grid=(n_seq, nb),
        in_specs=[
            pl.BlockSpec((blk, D_MODEL), lambda b, n: (row(b, n), COL_RNN_END // D_MODEL)),
            pl.BlockSpec((blk, kw), lambda b, n: (prev(b, n), (COL_RNN_END + ATTN_K_AT) // kw)),
            pl.BlockSpec((blk, kw), lambda b, n: (row(b, n), (COL_RNN_END + ATTN_K_AT) // kw)),
            pl.BlockSpec((blk, kw), lambda b, n: (prev(b, n), (COL_RNN_END + ATTN_V_AT) // kw)),
            pl.BlockSpec((blk, kw), lambda b, n: (row(b, n), (COL_RNN_END + ATTN_V_AT) // kw)),
            tab_c, tab_c, tab_p, tab_p,
            pl.BlockSpec((1, D_MODEL), lambda b, n: (0, 0)),
            pl.BlockSpec((1, kw), lambda b, n: (0, 0)),
            pl.BlockSpec((1, LANES), lambda b, n: (0, 0)),
        ],
        out_specs=[pl.BlockSpec((blk, D_MODEL), lambda b, n: (row(b, n), 0)),
                   pl.BlockSpec((blk, LANES), lambda b, n: (row(b, n), 0))],
        out_shape=[jax.ShapeDtypeStruct((t, D_MODEL), BF16), jax.ShapeDtypeStruct((t, LANES), F32)],
        args=(z, z, z, z, z, cos_t, sin_t, cos_t, sin_t, q_gain_t, k_gain_t, sinks_t), name="attn_fwd",
        semantics=("parallel", "parallel"), hosted=hosted)
    return (*outs, landed) if hosted is not None else outs


def _attn_bwd(z, o, lse, do, cos_t, sin_t, q_gain_t, k_gain_t, sinks_t, *, n_seq, seq, hosted=None):
    t = n_seq * seq
    blk = WINDOW
    nb = seq // blk
    kw = KV_W
    scale = HEAD_DIM ** -0.5

    def body(qc_ref, qn_ref, kc_ref, vp_ref, vc_ref, oc_ref, on_ref, doc_ref, don_ref, lc_ref, ln_ref,
             cosc_ref, sinc_ref, cosn_ref, sinn_ref, qg_ref, kg_ref, sk_ref,
             dz_ref, vec_ref, dq_s, q_s, k_s):
        n = pl.program_id(1)
        ecq, eeq = _head_matrices(D_MODEL)
        eck, eek = _head_matrices(KV_W)
        cosc, sinc = cosc_ref[...], sinc_ref[...]
        qg, kg = qg_ref[...], kg_ref[...]
        own, other = n & 1, 1 - (n & 1)

        @pl.when(n == 0)
        def _():
            for part, value in enumerate(_normrope_fwd(qc_ref[...], qg, cosc, sinc, ecq, eeq)):
                q_s[own, part] = value
            k_s[other] = jnp.zeros((blk, kw), F32)

        for part, value in enumerate(_normrope_fwd(qn_ref[...], qg, cosn_ref[...], sinn_ref[...], ecq, eeq)):
            q_s[other, part] = value
        qhc, nqc, rsqc = q_s[own, 0], q_s[own, 1], q_s[own, 2]
        qhn = q_s[other, 0]
        khc, nkc, rskc = _normrope_fwd(kc_ref[...], kg, cosc, sinc, eck, eek)
        khp = k_s[other]
        k_s[own] = khc
        doc = doc_ref[...].astype(F32)
        don = don_ref[...].astype(F32)
        delc = _dot_split(doc * oc_ref[...].astype(F32), ecq)
        deln = _dot_split(don * on_ref[...].astype(F32), ecq)
        lc_t, ln_t, delc_t, deln_t = lc_ref[...], ln_ref[...], delc.T, deln.T
        above, causal = _window_masks(blk)
        above_c, above_n = above & (n > 0), above & (n < nb - 1)
        seg = lax.broadcasted_iota(jnp.int32, (blk, GROUP_W), 1) // HEAD_DIM
        lane = lax.broadcasted_iota(jnp.int32, (1, LANES), 1)
        sk_t = jnp.broadcast_to(sk_ref[...], (blk, LANES)).T
        dsink = jnp.zeros((1, LANES), F32)
        kcat = jnp.concatenate([khp, khc], axis=0)
        vcat = jnp.concatenate([vp_ref[...], vc_ref[...]], axis=0)
        kcat_t = kcat.T.astype(BF16)
        dkh = jnp.zeros((blk, GROUP_W), F32)
        dvh = jnp.zeros((blk, GROUP_W), F32)

        def fold_to(group, t):
            total = t + pltpu.roll(t, HEAD_DIM, 1)
            total = total + pltpu.roll(total, 2 * HEAD_DIM, 1)
            return jnp.where(seg == group, total, 0.0)

        groups = range(N_KV_HEADS)
        cols = [slice(g * GROUP_W, (g + 1) * GROUP_W) for g in groups]
        qsc, qsn = qhc * scale, qhn * scale
        qb_c = [_head_blocks(qsc[:, cols[g]]).astype(BF16) for g in groups]
        qb_n = [_head_blocks(qsn[:, cols[g]]).astype(BF16) for g in groups]
        dob_c = [_head_blocks(doc[:, cols[g]]).astype(BF16) for g in groups]
        dob_n = [_head_blocks(don[:, cols[g]]).astype(BF16) for g in groups]
        raw = []
        for g in groups:
            krep = _replicate_head(kcat, g).astype(BF16)
            vrep = _replicate_head(vcat, g).astype(BF16)
            raw.append((_dot_nt(krep, qb_c[g]), _dot_nt(vrep, dob_c[g]),
                        _dot_nt(krep[blk:], qb_n[g]), _dot_nt(vrep[blk:], dob_n[g])))
        cooked = []
        for g in groups:
            s_c, dp_c, s_n, dp_n = raw[g]
            l_row, d_row = _head_rows(lc_t, g), _head_rows(delc_t, g)
            p_c = _mask_window(jnp.exp(s_c - l_row), above_c, causal, 0.0)
            ds_c = (p_c * (dp_c - d_row)).astype(BF16)
            p_n = jnp.where(above_n, jnp.exp(s_n - _head_rows(ln_t, g)), 0.0)
            ds_n = (p_n * (dp_n - _head_rows(deln_t, g))).astype(BF16)
            cooked.append((p_c[blk:].astype(BF16), ds_c, p_n.astype(BF16), ds_n))
            p_sink = jnp.exp(_head_rows(sk_t, g) - l_row) * d_row
            for h in range(GROUP):
                dsink = dsink + jnp.where(lane == GROUP * g + h,
                                          -jnp.sum(p_sink[:, h * blk:(h + 1) * blk], axis=1, keepdims=True), 0.0)
        for g in groups:
            p_cb, ds_c, p_nb, ds_n = cooked[g]
            dq_t = jnp.dot(kcat_t[g * HEAD_DIM:(g + 1) * HEAD_DIM], ds_c, preferred_element_type=F32)
            dq_s[:, cols[g]] = _stack_heads(dq_t, blk).T * scale
            dk_rep = (jnp.dot(ds_c[blk:], qb_c[g], preferred_element_type=F32)
                      + jnp.dot(ds_n, qb_n[g], preferred_element_type=F32))
            dv_rep = (jnp.dot(p_cb, dob_c[g], preferred_element_type=F32)
                      + jnp.dot(p_nb, dob_n[g], preferred_element_type=F32))
            dkh = dkh + fold_to(g, dk_rep)
            dvh = dvh + fold_to(g, dv_rep)
        dq, dqg = _normrope_bwd(dq_s[...], nqc, rsqc, qg, cosc, sinc, ecq, eeq)
        dk, dkg = _normrope_bwd(dkh, nkc, rskc, kg, cosc, sinc, eck, eek)
        dz_ref[:, :ATTN_K_AT] = dq.astype(BF16)
        dz_ref[:, ATTN_K_AT:ATTN_V_AT] = dk.astype(BF16)
        dz_ref[:, ATTN_V_AT:] = dvh.astype(BF16)

        @pl.when(n == 0)
        def _():
            vec_ref[...] = jnp.zeros_like(vec_ref)

        vec_ref[0:1, :] += dqg
        vec_ref[1:2, 0:kw] += dkg
        vec_ref[2:3, 0:LANES] += dsink

    def row(b, n):
        return b * nb + n

    def prev(b, n):
        return b * nb + jnp.maximum(n - 1, 0)

    def nxt(b, n):
        return b * nb + jnp.minimum(n + 1, nb - 1)

    def tiles(width, col, which):
        return pl.BlockSpec((blk, width), lambda b, n: (which(b, n), col))

    def table(which):
        return pl.BlockSpec((blk, LANES), lambda b, n: (which(0, n), 0))

    outs, landed = _call(
        body,
        grid=(n_seq, nb),
        in_specs=[
            tiles(D_MODEL, COL_RNN_END // D_MODEL, row), tiles(D_MODEL, COL_RNN_END // D_MODEL, nxt),
            tiles(kw, (COL_RNN_END + ATTN_K_AT) // kw, row),
            tiles(kw, (COL_RNN_END + ATTN_V_AT) // kw, prev), tiles(kw, (COL_RNN_END + ATTN_V_AT) // kw, row),
            tiles(D_MODEL, 0, row), tiles(D_MODEL, 0, nxt),
            tiles(D_MODEL, 0, row), tiles(D_MODEL, 0, nxt),
            tiles(LANES, 0, row), tiles(LANES, 0, nxt),
            table(row), table(row), table(nxt), table(nxt),
            pl.BlockSpec((1, D_MODEL), lambda b, n: (0, 0)),
            pl.BlockSpec((1, kw), lambda b, n: (0, 0)),
            pl.BlockSpec((1, LANES), lambda b, n: (0, 0)),
        ],
        out_specs=[tiles(ATTN_W, 0, row), pl.BlockSpec((None, 8, D_MODEL), lambda b, n: (b, 0, 0))],
        out_shape=[jax.ShapeDtypeStruct((t, ATTN_W), BF16), jax.ShapeDtypeStruct((n_seq, 8, D_MODEL), F32)],
        scratch_shapes=[pltpu.VMEM((blk, D_MODEL), F32), pltpu.VMEM((2, 3, blk, D_MODEL), F32),
                        pltpu.VMEM((2, blk, kw), F32)],
        args=(z, z, z, z, z, o, o, do, do, lse, lse, cos_t, sin_t, cos_t, sin_t,
              q_gain_t, k_gain_t, sinks_t), name="attn_bwd", semantics=("arbitrary", "arbitrary"), hosted=hosted)
    return (*outs, landed) if hosted is not None else outs


def _rope_tables(seq):
    inv = ROPE_THETA ** (-jnp.arange(0, HEAD_DIM, 2, dtype=F32) / HEAD_DIM)
    ang = jnp.arange(seq, dtype=F32)[:, None] * inv[None, :]
    cos, sin = jnp.cos(ang), jnp.sin(ang)
    return jnp.tile(jnp.concatenate([cos, cos], axis=1), (1, 2)), jnp.tile(jnp.concatenate([-sin, sin], axis=1), (1, 2))


def _block_diag_tiles(w):
    per = RNN_TILE // RNN_BLOCK_W
    w4 = w.reshape(D_MODEL // RNN_TILE, per, RNN_BLOCK_W, RNN_BLOCK_W)
    eye = jnp.eye(per, dtype=w.dtype)
    dense = jnp.einsum("tpij,pq->tpiqj", w4, eye)
    return dense.reshape(D_MODEL // RNN_TILE, RNN_TILE, RNN_TILE).astype(BF16)


def _block_diag_extract(dense):
    per = RNN_TILE // RNN_BLOCK_W
    d5 = dense.reshape(D_MODEL // RNN_TILE, per, RNN_BLOCK_W, per, RNN_BLOCK_W)
    blocks = jnp.stack([d5[:, p, :, p, :] for p in range(per)], axis=1)
    return blocks.reshape(D_MODEL // RNN_BLOCK_W, RNN_BLOCK_W, RNN_BLOCK_W)


def _local_step(x, p, target, w, *, n_seq, seq, comm=None):
    w = dict(w)

    def run(tag, fn, *args, **kwargs):
        hosted = comm.host(tag) if comm is not None else None
        if hosted is None:
            return fn(*args, **kwargs)
        *outs, landed = fn(*args, hosted=hosted, **kwargs)
        comm.landed(tag, landed, w)
        return outs[0] if len(outs) == 1 else outs

    def ready(batch, grads, extra=None):
        if comm is not None:
            comm.ready(batch, grads, extra)

    cos_t, sin_t = _rope_tables(seq)
    q_gain_t = jnp.tile(w["q_gain"], (1, N_Q_HEADS))
    k_gain_t = jnp.tile(w["k_gain"], (1, N_KV_HEADS))
    sinks_t = jnp.pad(w["sinks"], ((0, 0), (0, LANES - N_Q_HEADS)))
    wrg_bd, wig_bd = _block_diag_tiles(w["w_rg"]), _block_diag_tiles(w["w_ig"])
    dims = dict(n_seq=n_seq, seq=seq)

    z, h = run("mm_in", _norm_matmul, x, w["g_mix"], w["w_in"], tm=1024, tn=IN_TOTAL // 4, name="mm_in")
    gate_tile = 512
    ga_at, gb_at = COL_ATTN_END // gate_tile, (COL_ATTN_END + D_MODEL) // gate_tile
    xc, hr, ya_in = run("rnn_fwd", _rnn_fwd, z, w["conv_w"], w["conv_b"], wrg_bd, w["b_rg"], wig_bd, w["b_ig"],
                        w["lru_lambda"], **dims)
    o, lse = run("attn_fwd", _attn_fwd, z, cos_t, sin_t, q_gain_t, k_gain_t, sinks_t, **dims)
    ya = run("mm_rnn_proj", _matmul, ya_in, w["w_rnn_proj"], mode="nn", tm=1024, tn=1024, out_dtypes=[F32],
             name="mm_rnn_proj")
    yb, merged = _matmul(
        o, w["w_attn_proj"], mode="nn", tm=1024, tn=gate_tile, out_dtypes=[F32, BF16], name="mm_attn_proj",
        epilogue=lambda acc, ga, gb, yav: (acc, _sig(ga) * yav + _sig(gb) * acc),
        extras=(z, z, ya), extra_col_blocks=(ga_at, gb_at, 0))
    def residual_then_norm(acc, res, gain):
        new = res + acc
        return new, _rmsnorm_rows(new, gain)

    x1, hm = _matmul(merged, w["w_out"], mode="nn", tm=512, tn=1024, out_dtypes=[F32, BF16], name="mm_out",
                     epilogue=residual_then_norm, extras=(x,), row_vecs=(w["g_mlp"],))
    act = _matmul(hm, w["w_up"], mode="nn", tm=1024, tn=1024, out_dtypes=[BF16], name="mm_up",
                  epilogue=lambda acc: (jnp.square(jnp.maximum(acc, 0.0)),))
    x2, hp = _matmul(act, w["w_down"], mode="nn", tm=512, tn=1024, out_dtypes=[F32, BF16], name="mm_down",
                     epilogue=residual_then_norm, extras=(x1,), row_vecs=(w["g_ple"],))
    p_bf = p.astype(BF16)
    e = _matmul(p_bf, w["w_ple_proj"], mode="nn", tm=1024, tn=1024, out_dtypes=[F32], name="mm_ple_proj")

    def loss_head(gt, x2v, ev, tgt):
        sg = _sig(gt)
        diff = x2v + ev * sg - tgt
        dx = diff * (1.0 / D_MODEL)
        return dx, dx * ev * sg * (1.0 - sg), dx * sg, jnp.sum(diff * diff, axis=0, keepdims=True)

    dx3, dgt, de, loss_row = _matmul(hp, w["w_ple_gate"], mode="nn", tm=512, tn=1024, out_dtypes=[F32, BF16, BF16],
                                     name="mm_ple_gate", epilogue=loss_head, extras=(x2, e, target), n_row_sums=1)

    g = {}
    g["w_ple_proj"] = _matmul_tn(p_bf, de, tk=PLE_DIM, tn=1024, tt=1024, name="mm_d_ple_proj",
                                 slot_cols=D_MODEL // N_DEV)
    g["w_ple_gate"] = _matmul_tn(hp, dgt, tk=1024, tn=1024, tt=1024, name="mm_d_ple_gate")
    def through_norm(dy, xv, dres, gain):
        dx, dgain = _rmsnorm_bwd_rows(dy, xv, dres, gain)
        return dx, dx, dgain

    dx2, dx2_bf, g["g_ple"] = _matmul(
        dgt, w["w_ple_gate"], mode="nt", tm=512, tn=1024, out_dtypes=[F32, BF16], name="mm_dhp",
        epilogue=through_norm, extras=(x2, dx3), row_vecs=(w["g_ple"],), n_row_sums=1)
    g["w_down"] = _matmul_tn(act, dx2_bf, tk=1024, tn=1024, tt=1024, name="mm_d_down")

    def relu_grad(dact, a):
        a = a.astype(F32)
        return (dact * (2.0 * jnp.where(a > 0.0, a * lax.rsqrt(a), 0.0)),)

    du = _matmul(dx2_bf, w["w_down"], mode="nt", tm=1024, tn=1024, out_dtypes=[BF16], name="mm_dact",
                 epilogue=relu_grad, extras=(act,))
    g["w_up"] = _matmul_tn(hm, du, tk=1024, tn=1024, tt=1024, name="mm_d_up", slot_cols=D_FF // N_DEV)
    ready(1, g)
    dx1, dx1_bf, g["g_mlp"] = run(
        "mm_dhm", _matmul, du, w["w_up"], mode="nt", tm=512, tn=1024, out_dtypes=[F32, BF16], name="mm_dhm",
        epilogue=through_norm, extras=(x1, dx2), row_vecs=(w["g_mlp"],), n_row_sums=1)
    g["w_out"] = _matmul_tn(merged, dx1_bf, tk=1024, tn=1024, tt=1024, name="mm_d_out")
    def merge_bwd(dm, ga, gb, yav, ybv):
        sa, sb = _sig(ga), _sig(gb)
        return dm * sa, dm * sb, dm * yav * sa * (1.0 - sa), dm * ybv * sb * (1.0 - sb)

    dya, dyb, dga, dgb = _matmul(dx1_bf, w["w_out"], mode="nt", tm=1024, tn=gate_tile, out_dtypes=[BF16] * 4,
                                 name="mm_dmerged", epilogue=merge_bwd, extras=(z, z, ya, yb),
                                 extra_col_blocks=(ga_at, gb_at, 0, 0))
    g["w_rnn_proj"] = _matmul_tn(ya_in, dya, tk=1024, tn=1024, tt=1024, name="mm_d_rnn_proj")
    g["w_attn_proj"] = _matmul_tn(o, dyb, tk=1024, tn=1024, tt=1024, name="mm_d_attn_proj")
    ready(2, g)
    dya_in = run("mm_dya_in", _matmul, dya, w["w_rnn_proj"], mode="nt", tm=1024, tn=1024, out_dtypes=[F32],
                 name="mm_dya_in")
    do = _matmul(dyb, w["w_attn_proj"], mode="nt", tm=1024, tn=1024, out_dtypes=[BF16], name="mm_do")
    dx_rnn, dg_rnn, dwrg_dense, dwig_dense, rnn_vec = run(
        "rnn_bwd", _rnn_bwd, dya_in, z, xc, hr, w["conv_w"], wrg_bd, w["b_rg"], wig_bd, w["b_ig"],
        w["lru_lambda"], **dims)
    dz_attn, attn_vec = run("attn_bwd", _attn_bwd, z, o, lse, do, cos_t, sin_t, q_gain_t, k_gain_t, sinks_t,
                            **dims)
    dz_parts = (dx_rnn, dg_rnn, dz_attn, dga, dgb)
    g["w_rg"] = _block_diag_extract(dwrg_dense)
    g["w_ig"] = _block_diag_extract(dwig_dense)
    g["b_rg"], g["b_ig"], g["lru_lambda"], g["conv_b"] = (rnn_vec[i:i + 1] for i in range(4))
    g["conv_w"] = rnn_vec[4:8]
    attn_vec = attn_vec[0] if n_seq == 1 else functools.reduce(jnp.add, [attn_vec[b] for b in range(n_seq)])
    g["q_gain"] = attn_vec[0].reshape(N_Q_HEADS, HEAD_DIM).sum(axis=0)[None, :]
    g["k_gain"] = attn_vec[1, :KV_W].reshape(N_KV_HEADS, HEAD_DIM).sum(axis=0)[None, :]
    g["sinks"] = attn_vec[2:3, :N_Q_HEADS]
    ready(SMALL_BATCH, g, {LOSS_ROW: loss_row})
    g["w_in"] = jnp.concatenate(
        list(run("mm_d_in_rnn", _matmul_tn_multi, h, dz_parts[:2], tt=1024, name="mm_d_in_rnn"))
        + list(run("mm_d_in_rest", _matmul_tn_multi, h, dz_parts[2:], tt=512, name="mm_d_in_rest")), axis=1)
    ready(3, g)
    w_in_attn, w_in_gate = w["w_in"][:, COL_RNN_END:COL_ATTN_END], w["w_in"][:, COL_ATTN_END:]
    windows = ((w["w_in"], (0, D_MODEL)), (w["w_in"], (D_MODEL, D_MODEL)), (w_in_attn, (0, ATTN_W)),
               (w_in_gate, (0, D_MODEL)), (w_in_gate, (D_MODEL, D_MODEL)))
    grad_x, g["g_mix"] = run(
        "mm_dh", _matmul, dz_parts, [wd[0] for wd in windows], mode="nt", tm=256, tn=1024, out_dtypes=[F32],
        name="mm_dh", b_cols=[wd[1] for wd in windows], epilogue=_rmsnorm_bwd_rows, extras=(x, dx1),
        row_vecs=(w["g_mix"],), n_row_sums=1)
    return jnp.sum(loss_row), grad_x, g


MESH_ID = pl.DeviceIdType.MESH


def _coords(index):
    return (index >> 2) & 1, (index >> 1) & 1, index & 1


def _exchange(srcs, kinds, *, name):
    n = len(srcs)
    n_peer = N_DEV - 1

    def body(*refs):
        src, dst = refs[:n], refs[n:2 * n]
        send_sems, recv_sems, local_sems = refs[2 * n:]
        me = 4 * lax.axis_index("x") + 2 * lax.axis_index("y") + lax.axis_index("c")

        def remote(i, d):
            peer = (me + d) & (N_DEV - 1)
            piece = src[i] if kinds[i] == "gather" else src[i].at[peer]
            return pltpu.make_async_remote_copy(
                src_ref=piece, dst_ref=dst[i].at[me], send_sem=send_sems.at[i * n_peer + d - 1],
                recv_sem=recv_sems.at[i * n_peer + d - 1], device_id=_coords(peer), device_id_type=MESH_ID)

        def arrival(i, d):
            sender = (me - d) & (N_DEV - 1)
            piece = src[i] if kinds[i] == "gather" else src[i].at[sender]
            return pltpu.make_async_remote_copy(
                src_ref=piece, dst_ref=dst[i].at[sender], send_sem=send_sems.at[i * n_peer + d - 1],
                recv_sem=recv_sems.at[i * n_peer + d - 1], device_id=_coords(sender), device_id_type=MESH_ID)

        own = []
        for i in range(n):
            piece = src[i] if kinds[i] == "gather" else src[i].at[me]
            own.append(pltpu.make_async_copy(piece, dst[i].at[me], local_sems.at[i]))
            own[-1].start()
        sent = [remote(i, d) for d in range(1, N_DEV) for i in range(n)]
        for cp in sent:
            cp.start()
        for d in range(1, N_DEV):
            for i in range(n):
                arrival(i, d).wait_recv()
        for cp in sent:
            cp.wait_send()
        for cp in own:
            cp.wait()

    def out_of(s, kind):
        shape = s.shape if kind == "scatter" else (N_DEV,) + s.shape
        return jax.ShapeDtypeStruct(shape, s.dtype)

    any_spec = pl.BlockSpec(memory_space=pl.ANY)
    return pl.pallas_call(
        body,
        in_specs=[any_spec] * n,
        out_specs=[any_spec] * n,
        out_shape=[out_of(s, k) for s, k in zip(srcs, kinds)],
        scratch_shapes=[pltpu.SemaphoreType.DMA((n * n_peer,)), pltpu.SemaphoreType.DMA((n * n_peer,)),
                        pltpu.SemaphoreType.DMA((n,))],
        compiler_params=pltpu.CompilerParams(has_side_effects=True),
        name=name,
    )(*srcs)


def _remote(src, dst, send_sem, recv_sem, to):
    return pltpu.make_async_remote_copy(src_ref=src, dst_ref=dst, send_sem=send_sem, recv_sem=recv_sem,
                                        device_id=to, device_id_type=MESH_ID)


GATHER_PIECES = 4


def _gather_two_level(shards, *, name):
    n = len(shards)
    per = N_DEV - 1
    pieces = []
    for i, s in enumerate(shards):
        n_rows = s.shape[0]
        count = GATHER_PIECES if n_rows % (GATHER_PIECES * LANES) == 0 else 1
        pieces += [(i, r * (n_rows // count), n_rows // count) for r in range(count)]

    def body(*refs):
        src, dst = refs[:n], refs[n:2 * n]
        send_sems, recv_sems, local_sems = refs[2 * n:]
        x, y, c = lax.axis_index("x"), lax.axis_index("y"), lax.axis_index("c")
        me, sibling = (x, y, c), (x, y, 1 - c)
        chips = [(1 - x, y), (x, 1 - y), (1 - x, 1 - y)]

        def slot(pos):
            return 4 * pos[0] + 2 * pos[1] + pos[2]

        def copy(p, k, block, to, from_shard=False):
            i, first_row, rows = pieces[p]
            landed = dst[i].at[slot(block), pl.ds(first_row, rows)]
            source = src[i].at[pl.ds(first_row, rows)] if from_shard else landed
            return _remote(source, landed, send_sems.at[p * per + k], recv_sems.at[p * per + k], to)

        mine = [pltpu.make_async_copy(src[i], dst[i].at[slot(me)], local_sems.at[i]) for i in range(n)]
        for cp in mine:
            cp.start()
        first = []
        for p in range(len(pieces)):
            first.append(copy(p, 0, me, sibling, from_shard=True))
            first += [copy(p, 1 + j, me, (*chip, c), from_shard=True) for j, chip in enumerate(chips)]
        for cp in first:
            cp.start()
        passed = []
        for p in range(len(pieces)):
            for j, chip in enumerate(chips):
                copy(p, 1 + j, (*chip, c), me).wait_recv()
                passed.append(copy(p, 4 + j, (*chip, c), sibling))
                passed[-1].start()
        for p in range(len(pieces)):
            copy(p, 0, sibling, me).wait_recv()
            for j, chip in enumerate(chips):
                copy(p, 4 + j, (*chip, 1 - c), me).wait_recv()
        for cp in first + passed:
            cp.wait_send()
        for cp in mine:
            cp.wait()

    any_spec = pl.BlockSpec(memory_space=pl.ANY)
    return pl.pallas_call(
        body,
        in_specs=[any_spec] * n,
        out_specs=[any_spec] * n,
        out_shape=[jax.ShapeDtypeStruct((N_DEV,) + s.shape, s.dtype) for s in shards],
        scratch_shapes=[pltpu.SemaphoreType.DMA((len(pieces) * per,)), pltpu.SemaphoreType.DMA((len(pieces) * per,)),
                        pltpu.SemaphoreType.DMA((n,))],
        name=name,
    )(*shards)


CHIPS = N_DEV // 2


def _other_chips(x, y):
    return [(x, 1 - y), (1 - x, y), (1 - x, 1 - y)]


def _hosted_gather_first(shards):
    n = len(shards)
    per = CHIPS

    def plan(src, dst, send_sems, recv_sems, local_sems, first_sem):
        x, y, c = lax.axis_index("x"), lax.axis_index("y"), lax.axis_index("c")
        peers = [(x, y, 1 - c)] + [(*chip, c) for chip in _other_chips(x, y)]
        copies = []
        for i in range(n):
            own = pltpu.make_async_copy(src[i], dst[i].at[4 * x + 2 * y + c], local_sems.at[first_sem + i])
            copies.append(_Xfer(own.start, own.wait))
        for j, peer in enumerate(peers):
            for i in range(n):
                k = first_sem + i * per + j
                out = _remote(src[i], dst[i].at[4 * x + 2 * y + c], send_sems.at[k], recv_sems.at[k], peer)
                arrival = _remote(src[i], dst[i].at[4 * peer[0] + 2 * peer[1] + peer[2]], send_sems.at[k],
                                  recv_sems.at[k], peer)

                def wait(out=out, arrival=arrival):
                    arrival.wait_recv()
                    out.wait_send()

                copies.append(_Xfer(out.start, wait))
        return copies

    out_shape = tuple(jax.ShapeDtypeStruct((N_DEV,) + s.shape, s.dtype) for s in shards)
    return _Hosted(tuple(shards), out_shape, n * per, plan)


def _hosted_gather_second(landed):
    n = len(landed)
    per = CHIPS - 1

    def plan(src, dst, send_sems, recv_sems, local_sems, first_sem):
        x, y, c = lax.axis_index("x"), lax.axis_index("y"), lax.axis_index("c")
        copies = []
        for j, chip in enumerate(_other_chips(x, y)):
            mine, theirs = 4 * chip[0] + 2 * chip[1] + c, 4 * chip[0] + 2 * chip[1] + 1 - c
            for i in range(n):
                k = first_sem + i * per + j
                out = _remote(src[i].at[mine], dst[i].at[mine], send_sems.at[k], recv_sems.at[k], (x, y, 1 - c))
                arrival = _remote(src[i].at[theirs], dst[i].at[theirs], send_sems.at[k], recv_sems.at[k],
                                  (x, y, 1 - c))

                def wait(out=out, arrival=arrival):
                    arrival.wait_recv()
                    out.wait_send()

                copies.append(_Xfer(out.start, wait))
        return copies

    out_shape = tuple(jax.ShapeDtypeStruct(a.shape, a.dtype) for a in landed)
    return _Hosted(tuple(landed), out_shape, n * per, plan, tuple((i, i) for i in range(n)))


def _hosted_sibling_swap(arrays, sliced):
    n_sems = sum(CHIPS if s else 1 for s in sliced)

    def plan(src, dst, send_sems, recv_sems, local_sems, first_sem):
        x, y, c = lax.axis_index("x"), lax.axis_index("y"), lax.axis_index("c")
        sibling = (x, y, 1 - c)
        copies, k = [], first_sem
        for i, is_sliced in enumerate(sliced):
            pieces = [(src[i].at[2 * s + 1 - c], dst[i].at[s]) for s in range(CHIPS)] if is_sliced else [(src[i], dst[i])]
            for source, target in pieces:
                cp = _remote(source, target, send_sems.at[k], recv_sems.at[k], sibling)
                copies.append(_Xfer(cp.start, cp.wait))
                k += 1
        return copies

    out_shape = tuple(jax.ShapeDtypeStruct((CHIPS,) + a.shape[1:] if s else a.shape, a.dtype)
                      for a, s in zip(arrays, sliced))
    return _Hosted(tuple(arrays), out_shape, n_sems, plan)


def _hosted_chip_exchange(arrays, sliced):
    n = len(arrays)
    per = CHIPS - 1

    def plan(src, dst, send_sems, recv_sems, local_sems, first_sem):
        x, y, c = lax.axis_index("x"), lax.axis_index("y"), lax.axis_index("c")
        chip = 2 * x + y
        copies = []
        for i in range(n):
            own = pltpu.make_async_copy(src[i].at[chip] if sliced[i] else src[i], dst[i].at[chip],
                                        local_sems.at[first_sem + i])
            copies.append(_Xfer(own.start, own.wait))
        for d in range(1, CHIPS):
            other = chip ^ d
            to = ((other >> 1) & 1, other & 1, c)
            for i in range(n):
                k = first_sem + i * per + d - 1
                source = src[i].at[other] if sliced[i] else src[i]
                out = _remote(source, dst[i].at[chip], send_sems.at[k], recv_sems.at[k], to)
                arrival = _remote(source, dst[i].at[other], send_sems.at[k], recv_sems.at[k], to)

                def wait(out=out, arrival=arrival):
                    arrival.wait_recv()
                    out.wait_send()

                copies.append(_Xfer(out.start, wait))
        return copies

    out_shape = tuple(jax.ShapeDtypeStruct(a.shape if s else (CHIPS,) + a.shape, a.dtype)
                      for a, s in zip(arrays, sliced))
    return _Hosted(tuple(arrays), out_shape, n * per, plan)


def _add_sibling(parts, received, core, *, name):
    _, r, cols = parts.shape
    tr = min(1024, r)

    def body(core_ref, a_ref, b_ref, o_ref):
        o_ref[...] = (a_ref[...] + b_ref[...]).astype(BF16)

    grid_spec = pltpu.PrefetchScalarGridSpec(
        num_scalar_prefetch=1,
        grid=(CHIPS, r // tr),
        in_specs=[pl.BlockSpec((None, tr, cols), lambda k, i, core_ref: (2 * k + core_ref[0], i, 0)),
                  pl.BlockSpec((None, tr, cols), lambda k, i, core_ref: (k, i, 0))],
        out_specs=pl.BlockSpec((None, tr, cols), lambda k, i, core_ref: (k, i, 0)),
    )
    return pl.pallas_call(body, grid_spec=grid_spec, out_shape=jax.ShapeDtypeStruct((CHIPS, r, cols), BF16),
                          compiler_params=_params("parallel", "parallel"), name=name)(core, parts, received)


def _add_whole(a, b, *, name):
    def body(a_ref, b_ref, o_ref):
        o_ref[...] = a_ref[...] + b_ref[...]

    return pl.pallas_call(body, out_shape=jax.ShapeDtypeStruct(a.shape, F32), name=name)(a, b)


def _adamw(parts, w, m, v, *, name):
    r, c = w.shape
    n_parts = parts.shape[0]
    tr = min(512, r)
    c1 = 1.0 - ADAM_B1 ** ADAM_STEP
    c2 = 1.0 - ADAM_B2 ** ADAM_STEP

    def body(p_ref, w_ref, m_ref, v_ref, g_ref, d_ref, nm_ref, nv_ref):
        g = p_ref[0].astype(F32)
        for s in range(1, n_parts):
            g = g + p_ref[s].astype(F32)
        nm = ADAM_B1 * m_ref[...] + (1.0 - ADAM_B1) * g
        nv = ADAM_B2 * v_ref[...] + (1.0 - ADAM_B2) * (g * g)
        g_ref[...] = g
        nm_ref[...] = nm
        nv_ref[...] = nv
        d_ref[...] = -ADAM_LR * ((nm / c1) / (jnp.sqrt(nv / c2) + ADAM_EPS) + ADAM_WD * w_ref[...])

    tile = pl.BlockSpec((tr, c), lambda i: (i, 0))
    return pl.pallas_call(
        body,
        grid=(r // tr,),
        in_specs=[pl.BlockSpec((n_parts, tr, c), lambda i: (0, i, 0)), tile, tile, tile],
        out_specs=[tile] * 4,
        out_shape=[jax.ShapeDtypeStruct((r, c), F32)] * 4,
        compiler_params=_params("parallel"),
        name=name,
    )(parts, w, m, v)


BIG = ("w_in", "w_rnn_proj", "w_attn_proj", "w_out", "w_up", "w_down", "w_ple_gate", "w_ple_proj")
LOSS_ROW = "loss"
SMALL = (("conv_b", 1), ("b_rg", 1), ("b_ig", 1), ("lru_lambda", 1), ("g_mlp", 1), ("g_ple", 1),
         ("q_gain", 1), ("k_gain", 1), ("sinks", 1), (LOSS_ROW, 1), ("w_rg", 64), ("w_ig", 64))
SMALL_ROWS = 144
COL_SHARDED = ("w_in", "w_up", "w_ple_proj")
BATCHES = {1: ("w_ple_proj", "w_ple_gate", "w_down", "w_up"), 2: ("w_out", "w_rnn_proj", "w_attn_proj"),
           3: ("w_in", "conv_w")}
SMALL_BATCH = 4


def _pack_small(vals):
    rows = []
    for nm, nrow in SMALL:
        flat = vals[nm].reshape(-1).astype(F32)
        rows.append(jnp.pad(flat, (0, nrow * D_MODEL - flat.shape[0])).reshape(nrow, D_MODEL))
    used = sum(nrow for _, nrow in SMALL)
    rows.append(jnp.zeros((SMALL_ROWS - used, D_MODEL), F32))
    return jnp.concatenate(rows, axis=0)


def _unpack_small(packed, shapes):
    out, at = {}, 0
    for nm, nrow in SMALL:
        size = 1
        for s in shapes[nm]:
            size *= s
        out[nm] = packed[at:at + nrow].reshape(-1)[:size].reshape(shapes[nm])
        at += nrow
    return out


def _full_weight(name, landed):
    if name in COL_SHARDED:
        return landed.transpose(1, 0, 2).reshape(landed.shape[1], N_DEV * landed.shape[2])
    return landed.reshape(N_DEV * landed.shape[1], landed.shape[2])


def _owner_slots(name, grad):
    if name == "w_in":
        return grad.reshape(D_MODEL, N_DEV, IN_TOTAL // N_DEV).transpose(1, 0, 2)
    if name == "conv_w":
        return grad.reshape(CONV_W, N_DEV, D_MODEL // N_DEV).transpose(1, 0, 2)
    if name in COL_SHARDED:
        return grad
    return grad.reshape(N_DEV, grad.shape[0] // N_DEV, grad.shape[1])


class _StepExchanges:
    FIRST, SECOND = "first", "second"
    PROJ, OUT, PLE_GATE, UP, DOWN = (("w_rnn_proj", "w_attn_proj"), ("w_out",), ("w_ple_gate",), ("w_up",),
                                     ("w_down", "w_ple_proj"))
    GATHERS = {"mm_in": ((FIRST, PROJ), (FIRST, OUT), (FIRST, PLE_GATE)),
               "rnn_fwd": ((SECOND, PROJ), (SECOND, OUT), (SECOND, PLE_GATE), (FIRST, UP)),
               "attn_fwd": ((SECOND, UP), (FIRST, DOWN)), "mm_rnn_proj": ((SECOND, DOWN),)}
    SWAPS = {"mm_dhm": 1, "mm_dya_in": 2, "mm_d_in_rnn": SMALL_BATCH}
    CHIP_EXCHANGES = {"rnn_bwd": ((1, (0, 1, 2)),), "attn_bwd": ((1, (3,)), (2, None)),
                      "mm_d_in_rest": ((SMALL_BATCH, None),), "mm_dh": ((3, None),)}

    def __init__(self, shards, core):
        self.shards = shards
        self.core = core
        self.parts, self.swapped, self.summed, self.half_gathered = {}, {}, {}, {}

    def ready(self, batch, grads, extra=None):
        if batch == SMALL_BATCH:
            self.parts[batch] = ([_pack_small({**grads, **extra})], [False])
            return
        arrays = [_owner_slots(nm, grads[nm]) for nm in BATCHES[batch]]
        self.parts[batch] = (arrays, [True] * len(arrays))
        if batch not in self.SWAPS.values():
            _, self.swapped[batch] = _call(
                lambda: None, grid=(1,), in_specs=[], out_specs=[], out_shape=[], args=(), name="swap_last",
                semantics=("arbitrary",), hosted=_hosted_sibling_swap(*self.parts[batch]))

    def host(self, tag):
        if tag in self.GATHERS:
            return _merge_hosted([
                _hosted_gather_first([self.shards[nm] for nm in group]) if half == self.FIRST
                else _hosted_gather_second([self.half_gathered[nm] for nm in group])
                for half, group in self.GATHERS[tag]])
        if tag in self.SWAPS:
            return _hosted_sibling_swap(*self.parts[self.SWAPS[tag]])
        if tag in self.CHIP_EXCHANGES:
            hosted = []
            for batch, members in self._exchange_members(tag):
                arrays, sliced = self.parts[batch]
                labels = BATCHES.get(batch, ("small",))
                sums = [_add_sibling(arrays[i], self.swapped[batch][i], self.core, name="add_" + labels[i])
                        if sliced[i] else _add_whole(arrays[i], self.swapped[batch][i], name="add_" + labels[i])
                        for i in members]
                hosted.append(_hosted_chip_exchange(sums, [sliced[i] for i in members]))
            return _merge_hosted(hosted)
        return None

    def _exchange_members(self, tag):
        return [(batch, members if members is not None else tuple(range(len(self.parts[batch][0]))))
                for batch, members in self.CHIP_EXCHANGES[tag]]

    def landed(self, tag, landed, weights):
        if tag in self.GATHERS:
            names = [(half, nm) for half, group in self.GATHERS[tag] for nm in group]
            for (half, nm), buf in zip(names, landed):
                if half == self.FIRST:
                    self.half_gathered[nm] = buf
                else:
                    weights[nm] = _full_weight(nm, buf)
        elif tag in self.SWAPS:
            self.swapped[self.SWAPS[tag]] = landed
        else:
            at = 0
            for batch, members in self._exchange_members(tag):
                for i in members:
                    self.summed.setdefault(batch, {})[i] = landed[at]
                    at += 1


def kernel(x, p, g_mix, w_in, conv_w, conv_b, w_rg, b_rg, w_ig, b_ig, lru_lambda, w_rnn_proj, q_gain, k_gain, sinks, w_attn_proj, w_out, g_mlp, w_up, w_down, g_ple, w_ple_gate, w_ple_proj, loss_target, m_g_mix, m_w_in, m_conv_w, m_conv_b, m_w_rg, m_b_rg, m_w_ig, m_b_ig, m_lru_lambda, m_w_rnn_proj, m_q_gain, m_k_gain, m_sinks, m_w_attn_proj, m_w_out, m_g_mlp, m_w_up, m_w_down, m_g_ple, m_w_ple_gate, m_w_ple_proj, v_g_mix, v_w_in, v_conv_w, v_conv_b, v_w_rg, v_b_rg, v_w_ig, v_b_ig, v_lru_lambda, v_w_rnn_proj, v_q_gain, v_k_gain, v_sinks, v_w_attn_proj, v_w_out, v_g_mlp, v_w_up, v_w_down, v_g_ple, v_w_ple_gate, v_w_ple_proj):
    names = ("g_mix", "w_in", "conv_w", "conv_b", "w_rg", "b_rg", "w_ig", "b_ig", "lru_lambda", "w_rnn_proj",
             "q_gain", "k_gain", "sinks", "w_attn_proj", "w_out", "g_mlp", "w_up", "w_down", "g_ple",
             "w_ple_gate", "w_ple_proj")
    wts = dict(zip(names, (g_mix, w_in, conv_w, conv_b, w_rg, b_rg, w_ig, b_ig, lru_lambda, w_rnn_proj, q_gain,
                           k_gain, sinks, w_attn_proj, w_out, g_mlp, w_up, w_down, g_ple, w_ple_gate, w_ple_proj)))
    mom1 = dict(zip(names, (m_g_mix, m_w_in, m_conv_w, m_conv_b, m_w_rg, m_b_rg, m_w_ig, m_b_ig, m_lru_lambda,
                            m_w_rnn_proj, m_q_gain, m_k_gain, m_sinks, m_w_attn_proj, m_w_out, m_g_mlp, m_w_up,
                            m_w_down, m_g_ple, m_w_ple_gate, m_w_ple_proj)))
    mom2 = dict(zip(names, (v_g_mix, v_w_in, v_conv_w, v_conv_b, v_w_rg, v_b_rg, v_w_ig, v_b_ig, v_lru_lambda,
                            v_w_rnn_proj, v_q_gain, v_k_gain, v_sinks, v_w_attn_proj, v_w_out, v_g_mlp, v_w_up,
                            v_w_down, v_g_ple, v_w_ple_gate, v_w_ple_proj)))
    n_seq, seq, _ = x.shape
    core = lax.axis_index("c").astype(jnp.int32).reshape(1)

    shards = {nm: wts[nm][0].astype(BF16) for nm in BIG}
    w_in_all, conv_all = _gather_two_level([shards["w_in"], conv_w[0]], name="gather_w_in")
    w = {nm: wts[nm] for nm in names if nm not in BIG}
    w["w_rg"], w["w_ig"] = w_rg[0], w_ig[0]
    w["conv_w"] = conv_all.transpose(1, 0, 2).reshape(CONV_W, D_MODEL)
    w["w_in"] = _full_weight("w_in", w_in_all)
    comm = _StepExchanges(shards, core)
    loss_sum, grad_x, g = _local_step(
        x.reshape(n_seq * seq, D_MODEL), p.reshape(n_seq * seq, PLE_DIM), loss_target.reshape(n_seq * seq, D_MODEL),
        w, n_seq=n_seq, seq=seq, comm=comm)
    del loss_sum

    res = {}
    for batch, batch_names in BATCHES.items():
        for i, nm in enumerate(batch_names):
            res[nm] = _adamw(comm.summed[batch][i], wts[nm][0], mom1[nm][0], mom2[nm][0], name="adamw_" + nm)
    g_mix_parts, = _exchange([g["g_mix"]], ["gather"], name="gather_g_mix")
    res["g_mix"] = [r[0] for r in _adamw(g_mix_parts, g_mix, m_g_mix, v_g_mix, name="adamw_g_mix")]
    small_names = [nm for nm, _ in SMALL if nm != LOSS_ROW]
    full_small = {}
    for src, key in ((wts, "w"), (mom1, "m"), (mom2, "v")):
        vals = {nm: src[nm][0] for nm in small_names}
        vals[LOSS_ROW] = jnp.zeros((1,), F32)
        full_small[key] = _pack_small(vals)
    small_res = _adamw(comm.summed[SMALL_BATCH][0],full_small["w"], full_small["m"], full_small["v"], name="adamw_small")
    shapes = {nm: wts[nm].shape[1:] for nm in small_names}
    shapes[LOSS_ROW] = (D_MODEL,)
    small_out = [_unpack_small(r, shapes) for r in small_res]
    for nm in small_names:
        res[nm] = [so[nm] for so in small_out]
    loss = jnp.sum(small_out[0][LOSS_ROW]) * (0.5 / D_MODEL)

    outs = [loss, grad_x.reshape(n_seq, seq, D_MODEL)]
    for k in range(4):
        outs.extend(res[nm][k][None] for nm in names)
    return tuple(outs)
```

```python
import functools
from typing import Callable, NamedTuple

import jax
import jax.numpy as jnp
from jax import lax
from jax.experimental import pallas as pl
from jax.experimental.pallas import tpu as pltpu

F32 = jnp.float32
BF16 = jnp.bfloat16

N_DEV = 8
D_MODEL = 1024
RNN_BLOCK_W = 64
CONV_W = 4
LRU_C = 8.0
HEAD_DIM = 64
N_Q_HEADS = 16
N_KV_HEADS = 4
KV_W = N_KV_HEADS * HEAD_DIM
WINDOW = 128
ROPE_THETA = 10000.0
D_FF = 4096
PLE_DIM = 256
NORM_EPS = 1e-6
IN_TOTAL = 5632
COL_RNN_END, COL_ATTN_END = 2048, 3584
ATTN_W = COL_ATTN_END - COL_RNN_END
ATTN_K_AT, ATTN_V_AT = 1024, 1280

ADAM_LR = 0.001
ADAM_B1 = 0.9
ADAM_B2 = 0.999
ADAM_EPS = 1e-08
ADAM_WD = 0.01
ADAM_STEP = 10

LANES = 128
SUBLANES = 8
RNN_TILE = 256
VMEM_LIMIT = 56 * 1024 * 1024
NEG_BIG = -1e30


def _params(*sem):
    return pltpu.CompilerParams(dimension_semantics=sem if sem else None, vmem_limit_bytes=VMEM_LIMIT)


def _sig(x):
    return 0.5 * jnp.tanh(0.5 * x) + 0.5


def _dot_nt(a, b):
    return lax.dot_general(a, b, (((1,), (1,)), ((), ())), preferred_element_type=F32)


def _dot_tn(a, b):
    return lax.dot_general(a, b, (((0,), (0,)), ((), ())), preferred_element_type=F32)


class _Xfer:
    def __init__(self, start, wait):
        self.start, self.wait = start, wait


class _Hosted(NamedTuple):
    srcs: tuple
    out_shape: tuple
    n_sems: int
    plan: Callable
    aliases: tuple = ()


def _merge_hosted(parts):
    parts = [p for p in parts if p is not None]
    if len(parts) <= 1:
        return parts[0] if parts else None
    src_at, dst_at, sem_at, aliases = [0], [0], [0], []
    for p in parts:
        aliases += [(i + src_at[-1], j + dst_at[-1]) for i, j in p.aliases]
        src_at.append(src_at[-1] + len(p.srcs))
        dst_at.append(dst_at[-1] + len(p.out_shape))
        sem_at.append(sem_at[-1] + p.n_sems)

    def plan(src, dst, send_sems, recv_sems, local_sems, first_sem):
        copies = []
        for k, p in enumerate(parts):
            copies += p.plan(src[src_at[k]:src_at[k + 1]], dst[dst_at[k]:dst_at[k + 1]], send_sems, recv_sems,
                             local_sems, first_sem + sem_at[k])
        return copies

    return _Hosted(tuple(a for p in parts for a in p.srcs), tuple(s for p in parts for s in p.out_shape),
                   sem_at[-1], plan, tuple(aliases))


def _call(body, *, grid, in_specs, out_specs, out_shape, args, name, semantics, scratch_shapes=(), hosted=None):
    if hosted is None:
        outs = pl.pallas_call(body, grid=grid, in_specs=list(in_specs), out_specs=list(out_specs),
                              out_shape=list(out_shape), scratch_shapes=list(scratch_shapes),
                              compiler_params=_params(*semantics), name=name)(*args)
        return list(outs), []
    counts = (len(in_specs), len(hosted.srcs), len(out_specs), len(hosted.out_shape), len(scratch_shapes), 3)

    def wrapped(*refs):
        at, groups = 0, []
        for count in counts:
            groups.append(refs[at:at + count])
            at += count
        ins, srcs, outs, dsts, scratch, sems = groups
        copies = hosted.plan(srcs, dsts, *sems, 0)
        ids = [pl.program_id(axis) for axis in range(len(grid))]
        first = functools.reduce(jnp.logical_and, [i == 0 for i in ids])
        last = functools.reduce(jnp.logical_and, [i == g - 1 for i, g in zip(ids, grid)])

        @pl.when(first)
        def _():
            for cp in copies:
                cp.start()

        body(*ins, *outs, *scratch)

        @pl.when(last)
        def _():
            for cp in copies:
                cp.wait()

    any_spec = pl.BlockSpec(memory_space=pl.ANY)
    sems = [pltpu.SemaphoreType.DMA((hosted.n_sems,))] * 3
    outs = pl.pallas_call(
        wrapped, grid=grid, in_specs=list(in_specs) + [any_spec] * counts[1],
        out_specs=list(out_specs) + [any_spec] * counts[3], out_shape=list(out_shape) + list(hosted.out_shape),
        scratch_shapes=list(scratch_shapes) + sems, compiler_params=_params(*["arbitrary"] * len(grid)),
        input_output_aliases={counts[0] + i: counts[2] + j for i, j in hosted.aliases},
        name=name)(*args, *hosted.srcs)
    return list(outs[:counts[2]]), list(outs[counts[2]:])


def _dividing_tile(n, want):
    tile = min(want, n)
    while n % tile:
        tile -= LANES
    return tile


def _matmul(a, b, *, mode, tm, tn, out_dtypes, name, epilogue=None, extras=(), hosted=None, b_cols=None,
            row_vecs=(), n_row_sums=0, extra_col_blocks=None):
    a_parts = tuple(a) if isinstance(a, (tuple, list)) else (a,)
    b_parts = tuple(b) if isinstance(b, (tuple, list)) else (b,)
    assert len(a_parts) == len(b_parts) and (mode == "nt" or len(a_parts) == 1)
    n_parts = len(a_parts)
    m = a_parts[0].shape[0]
    if b_cols is None:
        b_cols = [(0, bp.shape[1]) for bp in b_parts]
    n = b_cols[0][1] if mode == "nn" else b_parts[0].shape[0]
    tm, tn = min(tm, m), _dividing_tile(n, tn)
    n_extra = len(extras) + len(row_vecs)
    n_tiles_out = len(out_dtypes)
    assert n_row_sums == 0 or n == tn

    def body(*refs):
        a_refs, b_refs = refs[:n_parts], refs[n_parts:2 * n_parts]
        rest = refs[2 * n_parts:]
        extra_refs, out_refs = rest[:n_extra], rest[n_extra:]
        if mode == "nn":
            acc = jnp.dot(a_refs[0][...], b_refs[0][...], preferred_element_type=F32)
        else:
            acc = _dot_nt(a_refs[0][...], b_refs[0][...])
            for a_ref, b_ref in zip(a_refs[1:], b_refs[1:]):
                acc = acc + _dot_nt(a_ref[...], b_ref[...])
        res = epilogue(acc, *[e[...] for e in extra_refs]) if epilogue is not None else (acc,)
        for o_ref, r in zip(out_refs[:n_tiles_out], res):
            o_ref[...] = r.astype(o_ref.dtype)
        if n_row_sums:
            @pl.when(pl.program_id(0) == 0)
            def _():
                for o_ref in out_refs[n_tiles_out:]:
                    o_ref[...] = jnp.zeros_like(o_ref)

            for o_ref, r in zip(out_refs[n_tiles_out:], res[n_tiles_out:]):
                o_ref[...] += r

    a_specs = [pl.BlockSpec((tm, ap.shape[1]), lambda i, j: (i, 0)) for ap in a_parts]
    if mode == "nn":
        assert b_cols[0][0] % tn == 0
        first = b_cols[0][0] // tn
        b_specs = [pl.BlockSpec((b_parts[0].shape[0], tn), lambda i, j: (0, first + j))]
    else:
        assert all(at % width == 0 for at, width in b_cols)
        b_specs = [pl.BlockSpec((tn, width), functools.partial(lambda i, j, blk: (j, blk), blk=at // width))
                   for at, width in b_cols]
    tile = pl.BlockSpec((tm, tn), lambda i, j: (i, j))
    row = pl.BlockSpec((1, tn), lambda i, j: (0, j))
    extra_specs = [pl.BlockSpec((tm, tn), functools.partial(lambda i, j, first: (i, first + j), first=first))
                   for first in (extra_col_blocks or [0] * len(extras))]
    outs, landed = _call(
        body,
        grid=(m // tm, n // tn),
        in_specs=a_specs + b_specs + extra_specs + [row] * len(row_vecs),
        out_specs=[tile] * n_tiles_out + [row] * n_row_sums,
        out_shape=[jax.ShapeDtypeStruct((m, n), dt) for dt in out_dtypes]
        + [jax.ShapeDtypeStruct((1, n), F32)] * n_row_sums,
        args=(*a_parts, *b_parts, *extras, *row_vecs), name=name,
        semantics=("arbitrary" if n_row_sums else "parallel", "arbitrary"), hosted=hosted)
    if hosted is not None:
        return (*outs, landed)
    return outs[0] if len(outs) == 1 else outs


def _matmul_tn(a, b, *, tk, tn, tt, name, slot_cols=None):
    t, k = a.shape
    n = b.shape[1]
    tk, tn, tt = min(tk, k), _dividing_tile(n, tn), min(tt, t)

    def body(a_ref, b_ref, o_ref):
        @pl.when(pl.program_id(2) == 0)
        def _():
            o_ref[...] = jnp.zeros_like(o_ref)

        if slot_cols is None:
            o_ref[...] += _dot_tn(a_ref[...], b_ref[...])
        else:
            av = a_ref[...]
            for s in range(tn // slot_cols):
                o_ref[s] += _dot_tn(av, b_ref[:, s * slot_cols:(s + 1) * slot_cols])

    if slot_cols is not None:
        out_spec = pl.BlockSpec((tn // slot_cols, tk, slot_cols), lambda i, j, s: (j, i, 0))
        out_shape = jax.ShapeDtypeStruct((n // slot_cols, k, slot_cols), F32)
    else:
        out_spec = pl.BlockSpec((tk, tn), lambda i, j, s: (i, j))
        out_shape = jax.ShapeDtypeStruct((k, n), F32)
    return pl.pallas_call(
        body,
        grid=(k // tk, n // tn, t // tt),
        in_specs=[pl.BlockSpec((tt, tk), lambda i, j, s: (s, i)), pl.BlockSpec((tt, tn), lambda i, j, s: (s, j))],
        out_specs=out_spec,
        out_shape=out_shape,
        compiler_params=_params("parallel", "parallel", "arbitrary"),
        name=name,
    )(a, b)


def _matmul_tn_multi(a, bs, *, tt, name, hosted=None):
    t, k = a.shape
    tt = min(tt, t)
    n_b = len(bs)

    def body(a_ref, *refs):
        b_refs, o_refs = refs[:n_b], refs[n_b:]

        @pl.when(pl.program_id(0) == 0)
        def _():
            for o_ref in o_refs:
                o_ref[...] = jnp.zeros_like(o_ref)

        a_t = a_ref[...].T
        for b_ref, o_ref in zip(b_refs, o_refs):
            o_ref[...] += jnp.dot(a_t, b_ref[...], preferred_element_type=F32)

    outs, landed = _call(
        body,
        grid=(t // tt,),
        in_specs=[pl.BlockSpec((tt, k), lambda s: (s, 0))] + [pl.BlockSpec((tt, b.shape[1]), lambda s: (s, 0)) for b in bs],
        out_specs=[pl.BlockSpec((k, b.shape[1]), lambda s: (0, 0)) for b in bs],
        out_shape=[jax.ShapeDtypeStruct((k, b.shape[1]), F32) for b in bs],
        args=(a, *bs), name=name, semantics=("arbitrary",), hosted=hosted)
    return (*outs, landed) if hosted is not None else outs


def _rmsnorm_rows(x, g):
    return x * lax.rsqrt(jnp.mean(x * x, axis=-1, keepdims=True) + NORM_EPS) * g


def _norm_matmul(x, g, b, *, tm, tn, name, hosted=None):
    m, k = x.shape
    n = b.shape[1]
    tm, tn = min(tm, m), _dividing_tile(n, tn)

    def body(x_ref, g_ref, b_ref, z_ref, h_ref, h_s):
        @pl.when(pl.program_id(1) == 0)
        def _():
            h_s[...] = _rmsnorm_rows(x_ref[...], g_ref[...]).astype(BF16)
            h_ref[...] = h_s[...]

        z_ref[...] = jnp.dot(h_s[...], b_ref[...], preferred_element_type=F32)

    rows = pl.BlockSpec((tm, k), lambda i, j: (i, 0))
    outs, landed = _call(
        body,
        grid=(m // tm, n // tn),
        in_specs=[rows, pl.BlockSpec((1, k), lambda i, j: (0, 0)), pl.BlockSpec((k, tn), lambda i, j: (0, j))],
        out_specs=[pl.BlockSpec((tm, tn), lambda i, j: (i, j)), rows],
        out_shape=[jax.ShapeDtypeStruct((m, n), F32), jax.ShapeDtypeStruct((m, k), BF16)],
        scratch_shapes=[pltpu.VMEM((tm, k), BF16)],
        args=(x, g, b), name=name, semantics=("parallel", "arbitrary"), hosted=hosted)
    return (*outs, landed) if hosted is not None else outs


def _rmsnorm_bwd_rows(dy, x, dres, g):
    r = lax.rsqrt(jnp.mean(x * x, axis=-1, keepdims=True) + NORM_EPS)
    xr = x * r
    gy = dy * g
    dx = dres + r * (gy - xr * jnp.mean(gy * xr, axis=-1, keepdims=True))
    return dx, jnp.sum(dy * xr, axis=0, keepdims=True)


def _softplus_neg(lam):
    z = -lam
    return jnp.maximum(z, 0.0) + jnp.log1p(jnp.exp(-jnp.abs(z)))


def _neg_expm1(y, exp_half_y):
    series = -y * (1.0 + y * 0.5 * (1.0 + y * (1.0 / 3.0) * (1.0 + y * 0.25 * (1.0 + y * 0.2))))
    return jnp.where(y > -0.0625, series, 1.0 - exp_half_y * exp_half_y)


def _gelu_parts(x):
    c = 0.7978845608028654
    u = c * (x + 0.044715 * x * x * x)
    th = jnp.tanh(u)
    gel = 0.5 * x * (1.0 + th)
    dgel = 0.5 * (1.0 + th) + 0.5 * x * (1.0 - th * th) * c * (1.0 + 3.0 * 0.044715 * x * x)
    return gel, dgel


def _shift_down(v, k, rows):
    return jnp.where(rows < k, 0.0, pltpu.roll(v, k, 0))


def _shift_up(v, k, rows, n):
    return jnp.where(rows >= n - k, 0.0, pltpu.roll(v, n - k, 0))


def _scan_within_groups(a, b, *, reverse):
    shape = a.shape
    a = a.reshape(shape[0] // SUBLANES, SUBLANES, shape[1])
    b = b.reshape(a.shape)
    in_group = lax.broadcasted_iota(jnp.int32, a.shape, 1)
    for s in (1, 2, 4):
        if reverse:
            inside, shift = in_group < SUBLANES - s, SUBLANES - s
        else:
            inside, shift = in_group >= s, s
        b = b + a * jnp.where(inside, pltpu.roll(b, shift, 1), 0.0)
        a = a * jnp.where(inside, pltpu.roll(a, shift, 1), 1.0)
    return a.reshape(shape), b.reshape(shape)


def _rnn_gates(xc, wrg, brg, wig, big, lam):
    xcb = xc.astype(BF16)
    r = _sig(jnp.dot(xcb, wrg, preferred_element_type=F32) + brg)
    i = _sig(jnp.dot(xcb, wig, preferred_element_type=F32) + big)
    sp = _softplus_neg(lam)
    log_a = -LRU_C * r * sp
    a = jnp.exp(log_a)
    mult = jnp.sqrt(_neg_expm1(2.0 * log_a, a))
    return xcb, r, i, sp, a, mult


def _conv_fwd(xv, cw, cb, rows):
    return (cb + _shift_down(xv, 3, rows) * cw[0:1, :] + _shift_down(xv, 2, rows) * cw[1:2, :]
            + _shift_down(xv, 1, rows) * cw[2:3, :] + xv * cw[3:4, :])


def _rnn_fwd(z, conv_w, conv_b, wrg_bd, b_rg, wig_bd, b_ig, lam, *, n_seq, seq, hosted=None):
    t = n_seq * seq
    ct = RNN_TILE
    n_ct = D_MODEL // ct

    def body(x_ref, g_ref, cw_ref, cb_ref, wrg_ref, brg_ref, wig_ref, big_ref, lam_ref,
             xc_ref, hr_ref, ya_ref, a_s, b_s):
        rows = lax.broadcasted_iota(jnp.int32, (seq, ct), 0)
        xc = _conv_fwd(x_ref[...], cw_ref[...], cb_ref[...], rows)
        _, r, i, sp, a, mult = _rnn_gates(xc, wrg_ref[...], brg_ref[...], wig_ref[...], big_ref[...], lam_ref[...])
        a_s[...], b_s[...] = _scan_within_groups(a, mult * (i * xc), reverse=False)

        def step(j, carry):
            r0 = pl.multiple_of(j * SUBLANES, SUBLANES)
            h = b_s[pl.ds(r0, SUBLANES), :] + a_s[pl.ds(r0, SUBLANES), :] * carry
            hr_ref[pl.ds(r0, SUBLANES), :] = h
            return h[SUBLANES - 1:SUBLANES, :]

        lax.fori_loop(0, seq // SUBLANES, step, jnp.zeros((1, ct), F32), unroll=4)
        gel, _ = _gelu_parts(g_ref[...])
        xc_ref[...] = xc
        ya_ref[...] = (hr_ref[...] * gel).astype(BF16)

    vec = pl.BlockSpec((1, ct), lambda b, c: (0, c))
    gate_w = pl.BlockSpec((None, ct, ct), lambda b, c: (c, 0, 0))
    tile = pl.BlockSpec((seq, ct), lambda b, c: (b, c))
    outs, landed = _call(
        body,
        grid=(n_seq, n_ct),
        in_specs=[
            pl.BlockSpec((seq, ct), lambda b, c: (b, c)),
            pl.BlockSpec((seq, ct), lambda b, c: (b, n_ct + c)),
            pl.BlockSpec((CONV_W, ct), lambda b, c: (0, c)), vec, gate_w, vec, gate_w, vec, vec,
        ],
        out_specs=[tile, tile, tile],
        out_shape=[jax.ShapeDtypeStruct((t, D_MODEL), F32), jax.ShapeDtypeStruct((t, D_MODEL), F32),
                   jax.ShapeDtypeStruct((t, D_MODEL), BF16)],
        scratch_shapes=[pltpu.VMEM((seq, ct), F32), pltpu.VMEM((seq, ct), F32)],
        args=(z, z, conv_w, conv_b, wrg_bd, b_rg, wig_bd, b_ig, lam), name="rnn_fwd",
        semantics=("parallel", "parallel"), hosted=hosted)
    return (*outs, landed) if hosted is not None else outs


def _rnn_bwd(dya, z, xc, hr, conv_w, wrg_bd, b_rg, wig_bd, b_ig, lam, *, n_seq, seq, hosted=None):
    t = n_seq * seq
    ct = RNN_TILE
    n_ct = D_MODEL // ct

    def body(dya_ref, x_ref, g_ref, xc_ref, hr_ref, cw_ref, wrg_ref, brg_ref, wig_ref, big_ref, lam_ref,
             dx_ref, dg_ref, dwrg_ref, dwig_ref, vec_ref, a_s, d_s, g_s):
        rows = lax.broadcasted_iota(jnp.int32, (seq, ct), 0)
        xv, xc, hr, dyv = x_ref[...], xc_ref[...], hr_ref[...], dya_ref[...]
        lamv = lam_ref[...]
        gel, dgel = _gelu_parts(g_ref[...])
        dg_ref[...] = (dyv * hr * dgel).astype(BF16)
        xcb, r, i, sp, a, mult = _rnn_gates(xc, wrg_ref[...], brg_ref[...], wig_ref[...], big_ref[...], lamv)
        a_s[...], d_s[...] = _scan_within_groups(_shift_up(a, 1, rows, seq), dyv * gel, reverse=True)

        def step(k, carry):
            r0 = pl.multiple_of((seq // SUBLANES - 1 - k) * SUBLANES, SUBLANES)
            gs = d_s[pl.ds(r0, SUBLANES), :] + a_s[pl.ds(r0, SUBLANES), :] * carry
            g_s[pl.ds(r0, SUBLANES), :] = gs
            return gs[0:1, :]

        lax.fori_loop(0, seq // SUBLANES, step, jnp.zeros((1, ct), F32), unroll=4)
        gsum = g_s[...]
        gated = i * xc
        d_log_a = gsum * _shift_down(hr, 1, rows) * a - gsum * gated * (a * a / mult)
        d_gated = gsum * mult
        d_pre_r = (d_log_a * (-LRU_C) * sp) * r * (1.0 - r)
        d_pre_i = (d_gated * xc) * i * (1.0 - i)
        dprb, dpib = d_pre_r.astype(BF16), d_pre_i.astype(BF16)
        dxc = d_gated * i + _dot_nt(dprb, wrg_ref[...]) + _dot_nt(dpib, wig_ref[...])
        cw = cw_ref[...]
        dx = dxc * cw[CONV_W - 1:CONV_W, :]
        d_taps = [jnp.sum(dxc * xv, axis=0, keepdims=True)]
        for k in range(1, CONV_W):
            up = _shift_up(dxc, k, rows, seq)
            dx = dx + up * cw[CONV_W - 1 - k:CONV_W - k, :]
            d_taps.append(jnp.sum(up * xv, axis=0, keepdims=True))
        dx_ref[...] = dx.astype(BF16)

        @pl.when(pl.program_id(1) == 0)
        def _():
            dwrg_ref[...] = jnp.zeros_like(dwrg_ref)
            dwig_ref[...] = jnp.zeros_like(dwig_ref)
            vec_ref[...] = jnp.zeros_like(vec_ref)

        dwrg_ref[...] += _dot_tn(xcb, dprb)
        dwig_ref[...] += _dot_tn(xcb, dpib)

        def colsum(v):
            return jnp.sum(v, axis=0, keepdims=True)

        d_sp = colsum(d_log_a * (-LRU_C) * r)
        vec_ref[0:1, :] += colsum(d_pre_r)
        vec_ref[1:2, :] += colsum(d_pre_i)
        vec_ref[2:3, :] += d_sp * (-_sig(-lamv))
        vec_ref[3:4, :] += colsum(dxc)
        for tap in range(CONV_W):
            vec_ref[4 + tap:5 + tap, :] += d_taps[CONV_W - 1 - tap]

    vec = pl.BlockSpec((1, ct), lambda c, b: (0, c))
    gate_w = pl.BlockSpec((None, ct, ct), lambda c, b: (c, 0, 0))
    tile = pl.BlockSpec((seq, ct), lambda c, b: (b, c))
    outs, landed = _call(
        body,
        grid=(n_ct, n_seq),
        in_specs=[
            tile,
            pl.BlockSpec((seq, ct), lambda c, b: (b, c)),
            pl.BlockSpec((seq, ct), lambda c, b: (b, n_ct + c)),
            tile, tile,
            pl.BlockSpec((CONV_W, ct), lambda c, b: (0, c)), gate_w, vec, gate_w, vec, vec,
        ],
        out_specs=[tile, tile, gate_w, gate_w, pl.BlockSpec((8, ct), lambda c, b: (0, c))],
        out_shape=[jax.ShapeDtypeStruct((t, D_MODEL), BF16), jax.ShapeDtypeStruct((t, D_MODEL), BF16),
                   jax.ShapeDtypeStruct((n_ct, ct, ct), F32), jax.ShapeDtypeStruct((n_ct, ct, ct), F32),
                   jax.ShapeDtypeStruct((8, D_MODEL), F32)],
        scratch_shapes=[pltpu.VMEM((seq, ct), F32)] * 3,
        args=(dya, z, z, xc, hr, conv_w, wrg_bd, b_rg, wig_bd, b_ig, lam), name="rnn_bwd",
        semantics=("parallel", "arbitrary"), hosted=hosted)
    return (*outs, landed) if hosted is not None else outs


def _split_hi_lo(x):
    hi = x.astype(BF16)
    return hi, (x - hi.astype(F32)).astype(BF16)


def _dot_split(x, m_twice):
    hi, lo = _split_hi_lo(x)
    return jnp.dot(jnp.concatenate([hi, lo], axis=1), m_twice, preferred_element_type=F32)


def _head_matrices(width):
    ec = ((lax.broadcasted_iota(jnp.int32, (2 * width, LANES), 0) & (width - 1)) // HEAD_DIM
          == lax.broadcasted_iota(jnp.int32, (2 * width, LANES), 1))
    ee = (lax.broadcasted_iota(jnp.int32, (2 * LANES, width), 1) // HEAD_DIM
          == (lax.broadcasted_iota(jnp.int32, (2 * LANES, width), 0) & (LANES - 1)))
    return jnp.where(ec, 1.0, 0.0).astype(BF16), jnp.where(ee, 1.0, 0.0).astype(BF16)


def _swap_halves(y):
    w = y.shape[1]
    first = (lax.broadcasted_iota(jnp.int32, y.shape, 1) % HEAD_DIM) < HEAD_DIM // 2
    return jnp.where(first, pltpu.roll(y, w - HEAD_DIM // 2, 1), pltpu.roll(y, HEAD_DIM // 2, 1))


def _normrope_fwd(x, gain, cos_t, sin_t, ec, ee):
    w = x.shape[1]
    rs = _dot_split(lax.rsqrt(_dot_split(x * x, ec) * (1.0 / HEAD_DIM) + NORM_EPS), ee)
    nx = x * rs
    y = nx * gain
    reps = w // LANES
    out = y * jnp.tile(cos_t, (1, reps)) + _swap_halves(y) * jnp.tile(sin_t, (1, reps))
    return out, nx, rs


def _normrope_bwd(dout, nx, rs, gain, cos_t, sin_t, ec, ee):
    w = dout.shape[1]
    reps = w // LANES
    dy = dout * jnp.tile(cos_t, (1, reps)) + _swap_halves(dout * jnp.tile(sin_t, (1, reps)))
    dgain = jnp.sum(dy * nx, axis=0, keepdims=True)
    dn = dy * gain
    seg = _dot_split(_dot_split(dn * nx, ec) * (1.0 / HEAD_DIM), ee)
    return rs * (dn - nx * seg), dgain


def _pair_operand(t, group):
    chunk = t[:, (group // 2) * LANES:(group // 2 + 1) * LANES]
    low = lax.broadcasted_iota(jnp.int32, chunk.shape, 1) < HEAD_DIM
    rolled = pltpu.roll(chunk, HEAD_DIM, 1)
    return jnp.where(low, chunk, rolled) if group % 2 == 0 else jnp.where(low, rolled, chunk)


GROUP = N_Q_HEADS // N_KV_HEADS
GROUP_W = GROUP * HEAD_DIM


def _replicate_head(t, group):
    return jnp.tile(_pair_operand(t, group), (1, 2))


def _head_blocks(t):
    seg = lax.broadcasted_iota(jnp.int32, t.shape, 1) // HEAD_DIM
    return jnp.concatenate([jnp.where(seg == h, t, 0.0) for h in range(GROUP)], axis=0)


def _stack_heads(t_t, rows):
    return jnp.concatenate([t_t[:, h * rows:(h + 1) * rows] for h in range(GROUP)], axis=0)


def _head_rows(mat_t, group):
    return jnp.concatenate([mat_t[GROUP * group + h:GROUP * group + h + 1, :] for h in range(GROUP)], axis=1)


def _window_masks(blk):
    key = lax.broadcasted_iota(jnp.int32, (blk, GROUP * blk), 0)
    query = lax.broadcasted_iota(jnp.int32, (blk, GROUP * blk), 1) & (blk - 1)
    return key > query, key <= query


def _mask_window(t, before_ok, own_ok, fill):
    blk = t.shape[0] // 2
    return jnp.concatenate([jnp.where(before_ok, t[:blk], fill), jnp.where(own_ok, t[blk:], fill)], axis=0)


def _attn_fwd(z, cos_t, sin_t, q_gain_t, k_gain_t, sinks_t, *, n_seq, seq, hosted=None):
    t = n_seq * seq
    blk = WINDOW
    nb = seq // blk

    def body(q_ref, kp_ref, kc_ref, vp_ref, vc_ref, cosc_ref, sinc_ref, cosp_ref, sinp_ref, qg_ref, kg_ref, sk_ref,
             o_ref, l_ref):
        n = pl.program_id(1)
        ecq, eeq = _head_matrices(D_MODEL)
        eck, eek = _head_matrices(KV_W)
        cosc, sinc = cosc_ref[...], sinc_ref[...]
        qh, _, _ = _normrope_fwd(q_ref[...], qg_ref[...], cosc, sinc, ecq, eeq)
        qh = qh * (HEAD_DIM ** -0.5)
        kc, _, _ = _normrope_fwd(kc_ref[...], kg_ref[...], cosc, sinc, eck, eek)
        kp, _, _ = _normrope_fwd(kp_ref[...], kg_ref[...], cosp_ref[...], sinp_ref[...], eck, eek)
        kcat = jnp.concatenate([kp, kc], axis=0)
        vcat = jnp.concatenate([vp_ref[...], vc_ref[...]], axis=0)
        above, causal = _window_masks(blk)
        above = above & (n > 0)
        head_row = lax.broadcasted_iota(jnp.int32, (blk, blk), 0)
        sk_t = jnp.broadcast_to(sk_ref[...], (blk, LANES)).T
        vcat_t = vcat.T.astype(BF16)
        lmat = jnp.zeros((blk, blk), F32)
        groups = range(N_KV_HEADS)
        cols = [slice(g * GROUP_W, (g + 1) * GROUP_W) for g in groups]
        qh = qh.astype(BF16)
        scores = [_dot_nt(_replicate_head(kcat, g).astype(BF16), _head_blocks(qh[:, cols[g]]))
                  for g in groups]
        probs = []
        for g in groups:
            s = _mask_window(scores[g], above, causal, NEG_BIG)
            sink = _head_rows(sk_t, g)
            m = jnp.maximum(jnp.max(s, axis=0, keepdims=True), sink)
            e = jnp.exp(s - m)
            den = jnp.sum(e, axis=0, keepdims=True) + jnp.exp(sink - m)
            probs.append((e * (1.0 / den)).astype(BF16))
            lse = m + jnp.log(den)
            for h in range(GROUP):
                lmat = lmat + jnp.where(head_row == GROUP * g + h, lse[:, h * blk:(h + 1) * blk], 0.0)
        for g in groups:
            out_t = jnp.dot(vcat_t[g * HEAD_DIM:(g + 1) * HEAD_DIM], probs[g], preferred_element_type=F32)
            o_ref[:, cols[g]] = _stack_heads(out_t, blk).T.astype(BF16)
        l_ref[...] = lmat

    def row(b, n):
        return b * nb + n

    def prev(b, n):
        return b * nb + jnp.maximum(n - 1, 0)

    kw = KV_W
    tab_c = pl.BlockSpec((blk, LANES), lambda b, n: (n, 0))
    tab_p = pl.BlockSpec((blk, LANES), lambda b, n: (jnp.maximum(n - 1, 0), 0))
    outs, landed = _call(
        body,
        grid=(n_seq, nb),
        in_specs=[
            pl.BlockSpec((blk, D_MODEL), lambda b, n: (row(b, n), COL_RNN_END // D_MODEL)),
            pl.BlockSpec((blk, kw), lambda b, n: (prev(b, n), (COL_RNN_END + ATTN_K_AT) // kw)),
            pl.BlockSpec((blk, kw), lambda b, n: (row(b, n), (COL_RNN_END + ATTN_K_AT) // kw)),
            pl.BlockSpec((blk, kw), lambda b, n: (prev(b, n), (COL_RNN_END + ATTN_V_AT) // kw)),
            pl.BlockSpec((blk, kw), lambda b, n: (row(b, n), (COL_RNN_END + ATTN_V_AT) // kw)),
            tab_c, tab_c, tab_p, tab_p,
            pl.BlockSpec((1, D_MODEL), lambda b, n: (0, 0)),
            pl.BlockSpec((1, kw), lambda b, n: (0, 0)),
            pl.BlockSpec((1, LANES), lambda b, n: (0, 0)),
        ],
        out_specs=[pl.BlockSpec((blk, D_MODEL), lambda b, n: (row(b, n), 0)),
                   pl.BlockSpec((blk, LANES), lambda b, n: (row(b, n), 0))],
        out_shape=[jax.ShapeDtypeStruct((t, D_MODEL), BF16), jax.ShapeDtypeStruct((t, LANES), F32)],
        args=(z, z, z, z, z, cos_t, sin_t, cos_t, sin_t, q_gain_t, k_gain_t, sinks_t), name="attn_fwd",
        semantics=("parallel", "parallel"), hosted=hosted)
    return (*outs, landed) if hosted is not None else outs


def _attn_bwd(z, o, lse, do, cos_t, sin_t, q_gain_t, k_gain_t, sinks_t, *, n_seq, seq, hosted=None):
    t = n_seq * seq
    blk = WINDOW
    nb = seq // blk
    kw = KV_W
    scale = HEAD_DIM ** -0.5

    def body(qc_ref, qn_ref, kc_ref, vp_ref, vc_ref, oc_ref, on_ref, doc_ref, don_ref, lc_ref, ln_ref,
             cosc_ref, sinc_ref, cosn_ref, sinn_ref, qg_ref, kg_ref, sk_ref,
             dz_ref, vec_ref, dq_s, q_s, k_s):
        n = pl.program_id(1)
        ecq, eeq = _head_matrices(D_MODEL)
        eck, eek = _head_matrices(KV_W)
        cosc, sinc = cosc_ref[...], sinc_ref[...]
        qg, kg = qg_ref[...], kg_ref[...]
        own, other = n & 1, 1 - (n & 1)

        @pl.when(n == 0)
        def _():
            for part, value in enumerate(_normrope_fwd(qc_ref[...], qg, cosc, sinc, ecq, eeq)):
                q_s[own, part] = value
            k_s[other] = jnp.zeros((blk, kw), F32)

        for part, value in enumerate(_normrope_fwd(qn_ref[...], qg, cosn_ref[...], sinn_ref[...], ecq, eeq)):
            q_s[other, part] = value
        qhc, nqc, rsqc = q_s[own, 0], q_s[own, 1], q_s[own, 2]
        qhn = q_s[other, 0]
        khc, nkc, rskc = _normrope_fwd(kc_ref[...], kg, cosc, sinc, eck, eek)
        khp = k_s[other]
        k_s[own] = khc
        doc = doc_ref[...].astype(F32)
        don = don_ref[...].astype(F32)
        delc = _dot_split(doc * oc_ref[...].astype(F32), ecq)
        deln = _dot_split(don * on_ref[...].astype(F32), ecq)
        lc_t, ln_t, delc_t, deln_t = lc_ref[...], ln_ref[...], delc.T, deln.T
        above, causal = _window_masks(blk)
        above_c, above_n = above & (n > 0), above & (n < nb - 1)
        seg = lax.broadcasted_iota(jnp.int32, (blk, GROUP_W), 1) // HEAD_DIM
        lane = lax.broadcasted_iota(jnp.int32, (1, LANES), 1)
        sk_t = jnp.broadcast_to(sk_ref[...], (blk, LANES)).T
        dsink = jnp.zeros((1, LANES), F32)
        kcat = jnp.concatenate([khp, khc], axis=0)
        vcat = jnp.concatenate([vp_ref[...], vc_ref[...]], axis=0)
        kcat_t = kcat.T.astype(BF16)
        dkh = jnp.zeros((blk, GROUP_W), F32)
        dvh = jnp.zeros((blk, GROUP_W), F32)

        def fold_to(group, t):
            total = t + pltpu.roll(t, HEAD_DIM, 1)
            total = total + pltpu.roll(total, 2 * HEAD_DIM, 1)
            return jnp.where(seg == group, total, 0.0)

        groups = range(N_KV_HEADS)
        cols = [slice(g * GROUP_W, (g + 1) * GROUP_W) for g in groups]
        qsc, qsn = qhc * scale, qhn * scale
        qb_c = [_head_blocks(qsc[:, cols[g]]).astype(BF16) for g in groups]
        qb_n = [_head_blocks(qsn[:, cols[g]]).astype(BF16) for g in groups]
        dob_c = [_head_blocks(doc[:, cols[g]]).astype(BF16) for g in groups]
        dob_n = [_head_blocks(don[:, cols[g]]).astype(BF16) for g in groups]
        raw = []
        for g in groups:
            krep = _replicate_head(kcat, g).astype(BF16)
            vrep = _replicate_head(vcat, g).astype(BF16)
            raw.append((_dot_nt(krep, qb_c[g]), _dot_nt(vrep, dob_c[g]),
                        _dot_nt(krep[blk:], qb_n[g]), _dot_nt(vrep[blk:], dob_n[g])))
        cooked = []
        for g in groups:
            s_c, dp_c, s_n, dp_n = raw[g]
            l_row, d_row = _head_rows(lc_t, g), _head_rows(delc_t, g)
            p_c = _mask_window(jnp.exp(s_c - l_row), above_c, causal, 0.0)
            ds_c = (p_c * (dp_c - d_row)).astype(BF16)
            p_n = jnp.where(above_n, jnp.exp(s_n - _head_rows(ln_t, g)), 0.0)
            ds_n = (p_n * (dp_n - _head_rows(deln_t, g))).astype(BF16)
            cooked.append((p_c[blk:].astype(BF16), ds_c, p_n.astype(BF16), ds_n))
            p_sink = jnp.exp(_head_rows(sk_t, g) - l_row) * d_row
            for h in range(GROUP):
                dsink = dsink + jnp.where(lane == GROUP * g + h,
                                          -jnp.sum(p_sink[:, h * blk:(h + 1) * blk], axis=1, keepdims=True), 0.0)
        for g in groups:
            p_cb, ds_c, p_nb, ds_n = cooked[g]
            dq_t = jnp.dot(kcat_t[g * HEAD_DIM:(g + 1) * HEAD_DIM], ds_c, preferred_element_type=F32)
            dq_s[:, cols[g]] = _stack_heads(dq_t, blk).T * scale
            dk_rep = (jnp.dot(ds_c[blk:], qb_c[g], preferred_element_type=F32)
                      + jnp.dot(ds_n, qb_n[g], preferred_element_type=F32))
            dv_rep = (jnp.dot(p_cb, dob_c[g], preferred_element_type=F32)
                      + jnp.dot(p_nb, dob_n[g], preferred_element_type=F32))
            dkh = dkh + fold_to(g, dk_rep)
            dvh = dvh + fold_to(g, dv_rep)
        dq, dqg = _normrope_bwd(dq_s[...], nqc, rsqc, qg, cosc, sinc, ecq, eeq)
        dk, dkg = _normrope_bwd(dkh, nkc, rskc, kg, cosc, sinc, eck, eek)
        dz_ref[:, :ATTN_K_AT] = dq.astype(BF16)
        dz_ref[:, ATTN_K_AT:ATTN_V_AT] = dk.astype(BF16)
        dz_ref[:, ATTN_V_AT:] = dvh.astype(BF16)

        @pl.when(n == 0)
        def _():
            vec_ref[...] = jnp.zeros_like(vec_ref)

        vec_ref[0:1, :] += dqg
        vec_ref[1:2, 0:kw] += dkg
        vec_ref[2:3, 0:LANES] += dsink

    def row(b, n):
        return b * nb + n

    def prev(b, n):
        return b * nb + jnp.maximum(n - 1, 0)

    def nxt(b, n):
        return b * nb + jnp.minimum(n + 1, nb - 1)

    def tiles(width, col, which):
        return pl.BlockSpec((blk, width), lambda b, n: (which(b, n), col))

    def table(which):
        return pl.BlockSpec((blk, LANES), lambda b, n: (which(0, n), 0))

    outs, landed = _call(
        body,
        grid=(n_seq, nb),
        in_specs=[
            tiles(D_MODEL, COL_RNN_END // D_MODEL, row), tiles(D_MODEL, COL_RNN_END // D_MODEL, nxt),
            tiles(kw, (COL_RNN_END + ATTN_K_AT) // kw, row),
            tiles(kw, (COL_RNN_END + ATTN_V_AT) // kw, prev), tiles(kw, (COL_RNN_END + ATTN_V_AT) // kw, row),
            tiles(D_MODEL, 0, row), tiles(D_MODEL, 0, nxt),
            tiles(D_MODEL, 0, row), tiles(D_MODEL, 0, nxt),
            tiles(LANES, 0, row), tiles(LANES, 0, nxt),
            table(row), table(row), table(nxt), table(nxt),
            pl.BlockSpec((1, D_MODEL), lambda b, n: (0, 0)),
            pl.BlockSpec((1, kw), lambda b, n: (0, 0)),
            pl.BlockSpec((1, LANES), lambda b, n: (0, 0)),
        ],
        out_specs=[tiles(ATTN_W, 0, row), pl.BlockSpec((None, 8, D_MODEL), lambda b, n: (b, 0, 0))],
        out_shape=[jax.ShapeDtypeStruct((t, ATTN_W), BF16), jax.ShapeDtypeStruct((n_seq, 8, D_MODEL), F32)],
        scratch_shapes=[pltpu.VMEM((blk, D_MODEL), F32), pltpu.VMEM((2, 3, blk, D_MODEL), F32),
                        pltpu.VMEM((2, blk, kw), F32)],
        args=(z, z, z, z, z, o, o, do, do, lse, lse, cos_t, sin_t, cos_t, sin_t,
              q_gain_t, k_gain_t, sinks_t), name="attn_bwd", semantics=("arbitrary", "arbitrary"), hosted=hosted)
    return (*outs, landed) if hosted is not None else outs


def _rope_tables(seq):
    inv = ROPE_THETA ** (-jnp.arange(0, HEAD_DIM, 2, dtype=F32) / HEAD_DIM)
    ang = jnp.arange(seq, dtype=F32)[:, None] * inv[None, :]
    cos, sin = jnp.cos(ang), jnp.sin(ang)
    return jnp.tile(jnp.concatenate([cos, cos], axis=1), (1, 2)), jnp.tile(jnp.concatenate([-sin, sin], axis=1), (1, 2))


def _block_diag_tiles(w):
    per = RNN_TILE // RNN_BLOCK_W
    w4 = w.reshape(D_MODEL // RNN_TILE, per, RNN_BLOCK_W, RNN_BLOCK_W)
    eye = jnp.eye(per, dtype=w.dtype)
    dense = jnp.einsum("tpij,pq->tpiqj", w4, eye)
    return dense.reshape(D_MODEL // RNN_TILE, RNN_TILE, RNN_TILE).astype(BF16)


def _block_diag_extract(dense):
    per = RNN_TILE // RNN_BLOCK_W
    d5 = dense.reshape(D_MODEL // RNN_TILE, per, RNN_BLOCK_W, per, RNN_BLOCK_W)
    blocks = jnp.stack([d5[:, p, :, p, :] for p in range(per)], axis=1)
    return blocks.reshape(D_MODEL // RNN_BLOCK_W, RNN_BLOCK_W, RNN_BLOCK_W)


def _local_step(x, p, target, w, *, n_seq, seq, comm=None):
    w = dict(w)

    def run(tag, fn, *args, **kwargs):
        hosted = comm.host(tag) if comm is not None else None
        if hosted is None:
            return fn(*args, **kwargs)
        *outs, landed = fn(*args, hosted=hosted, **kwargs)
        comm.landed(tag, landed, w)
        return outs[0] if len(outs) == 1 else outs

    def ready(batch, grads, extra=None):
        if comm is not None:
            comm.ready(batch, grads, extra)

    cos_t, sin_t = _rope_tables(seq)
    q_gain_t = jnp.tile(w["q_gain"], (1, N_Q_HEADS))
    k_gain_t = jnp.tile(w["k_gain"], (1, N_KV_HEADS))
    sinks_t = jnp.pad(w["sinks"], ((0, 0), (0, LANES - N_Q_HEADS)))
    wrg_bd, wig_bd = _block_diag_tiles(w["w_rg"]), _block_diag_tiles(w["w_ig"])
    dims = dict(n_seq=n_seq, seq=seq)

    z, h = run("mm_in", _norm_matmul, x, w["g_mix"], w["w_in"], tm=1024, tn=IN_TOTAL // 4, name="mm_in")
    gate_tile = 512
    ga_at, gb_at = COL_ATTN_END // gate_tile, (COL_ATTN_END + D_MODEL) // gate_tile
    xc, hr, ya_in = run("rnn_fwd", _rnn_fwd, z, w["conv_w"], w["conv_b"], wrg_bd, w["b_rg"], wig_bd, w["b_ig"],
                        w["lru_lambda"], **dims)
    o, lse = run("attn_fwd", _attn_fwd, z, cos_t, sin_t, q_gain_t, k_gain_t, sinks_t, **dims)
    ya = run("mm_rnn_proj", _matmul, ya_in, w["w_rnn_proj"], mode="nn", tm=1024, tn=1024, out_dtypes=[F32],
             name="mm_rnn_proj")
    yb, merged = _matmul(
        o, w["w_attn_proj"], mode="nn", tm=1024, tn=gate_tile, out_dtypes=[F32, BF16], name="mm_attn_proj",
        epilogue=lambda acc, ga, gb, yav: (acc, _sig(ga) * yav + _sig(gb) * acc),
        extras=(z, z, ya), extra_col_blocks=(ga_at, gb_at, 0))
    def residual_then_norm(acc, res, gain):
        new = res + acc
        return new, _rmsnorm_rows(new, gain)

    x1, hm = _matmul(merged, w["w_out"], mode="nn", tm=512, tn=1024, out_dtypes=[F32, BF16], name="mm_out",
                     epilogue=residual_then_norm, extras=(x,), row_vecs=(w["g_mlp"],))
    act = _matmul(hm, w["w_up"], mode="nn", tm=1024, tn=1024, out_dtypes=[BF16], name="mm_up",
                  epilogue=lambda acc: (jnp.square(jnp.maximum(acc, 0.0)),))
    x2, hp = _matmul(act, w["w_down"], mode="nn", tm=512, tn=1024, out_dtypes=[F32, BF16], name="mm_down",
                     epilogue=residual_then_norm, extras=(x1,), row_vecs=(w["g_ple"],))
    p_bf = p.astype(BF16)
    e = _matmul(p_bf, w["w_ple_proj"], mode="nn", tm=1024, tn=1024, out_dtypes=[F32], name="mm_ple_proj")

    def loss_head(gt, x2v, ev, tgt):
        sg = _sig(gt)
        diff = x2v + ev * sg - tgt
        dx = diff * (1.0 / D_MODEL)
        return dx, dx * ev * sg * (1.0 - sg), dx * sg, jnp.sum(diff * diff, axis=0, keepdims=True)

    dx3, dgt, de, loss_row = _matmul(hp, w["w_ple_gate"], mode="nn", tm=512, tn=1024, out_dtypes=[F32, BF16, BF16],
                                     name="mm_ple_gate", epilogue=loss_head, extras=(x2, e, target), n_row_sums=1)

    g = {}
    g["w_ple_proj"] = _matmul_tn(p_bf, de, tk=PLE_DIM, tn=1024, tt=1024, name="mm_d_ple_proj",
                                 slot_cols=D_MODEL // N_DEV)
    g["w_ple_gate"] = _matmul_tn(hp, dgt, tk=1024, tn=1024, tt=1024, name="mm_d_ple_gate")
    def through_norm(dy, xv, dres, gain):
        dx, dgain = _rmsnorm_bwd_rows(dy, xv, dres, gain)
        return dx, dx, dgain

    dx2, dx2_bf, g["g_ple"] = _matmul(
        dgt, w["w_ple_gate"], mode="nt", tm=512, tn=1024, out_dtypes=[F32, BF16], name="mm_dhp",
        epilogue=through_norm, extras=(x2, dx3), row_vecs=(w["g_ple"],), n_row_sums=1)
    g["w_down"] = _matmul_tn(act, dx2_bf, tk=1024, tn=1024, tt=1024, name="mm_d_down")

    def relu_grad(dact, a):
        a = a.astype(F32)
        return (dact * (2.0 * jnp.where(a > 0.0, a * lax.rsqrt(a), 0.0)),)

    du = _matmul(dx2_bf, w["w_down"], mode="nt", tm=1024, tn=1024, out_dtypes=[BF16], name="mm_dact",
                 epilogue=relu_grad, extras=(act,))
    g["w_up"] = _matmul_tn(hm, du, tk=1024, tn=1024, tt=1024, name="mm_d_up", slot_cols=D_FF // N_DEV)
    ready(1, g)
    dx1, dx1_bf, g["g_mlp"] = run(
        "mm_dhm", _matmul, du, w["w_up"], mode="nt", tm=512, tn=1024, out_dtypes=[F32, BF16], name="mm_dhm",
        epilogue=through_norm, extras=(x1, dx2), row_vecs=(w["g_mlp"],), n_row_sums=1)
    g["w_out"] = _matmul_tn(merged, dx1_bf, tk=1024, tn=1024, tt=1024, name="mm_d_out")
    def merge_bwd(dm, ga, gb, yav, ybv):
        sa, sb = _sig(ga), _sig(gb)
        return dm * sa, dm * sb, dm * yav * sa * (1.0 - sa), dm * ybv * sb * (1.0 - sb)

    dya, dyb, dga, dgb = _matmul(dx1_bf, w["w_out"], mode="nt", tm=1024, tn=gate_tile, out_dtypes=[BF16] * 4,
                                 name="mm_dmerged", epilogue=merge_bwd, extras=(z, z, ya, yb),
                                 extra_col_blocks=(ga_at, gb_at, 0, 0))
    g["w_rnn_proj"] = _matmul_tn(ya_in, dya, tk=1024, tn=1024, tt=1024, name="mm_d_rnn_proj")
    g["w_attn_proj"] = _matmul_tn(o, dyb, tk=1024, tn=1024, tt=1024, name="mm_d_attn_proj")
    ready(2, g)
    dya_in = run("mm_dya_in", _matmul, dya, w["w_rnn_proj"], mode="nt", tm=1024, tn=1024, out_dtypes=[F32],
                 name="mm_dya_in")
    do = _matmul(dyb, w["w_attn_proj"], mode="nt", tm=1024, tn=1024, out_dtypes=[BF16], name="mm_do")
    dx_rnn, dg_rnn, dwrg_dense, dwig_dense, rnn_vec = run(
        "rnn_bwd", _rnn_bwd, dya_in, z, xc, hr, w["conv_w"], wrg_bd, w["b_rg"], wig_bd, w["b_ig"],
        w["lru_lambda"], **dims)
    dz_attn, attn_vec = run("attn_bwd", _attn_bwd, z, o, lse, do, cos_t, sin_t, q_gain_t, k_gain_t, sinks_t,
                            **dims)
    dz_parts = (dx_rnn, dg_rnn, dz_attn, dga, dgb)
    g["w_rg"] = _block_diag_extract(dwrg_dense)
    g["w_ig"] = _block_diag_extract(dwig_dense)
    g["b_rg"], g["b_ig"], g["lru_lambda"], g["conv_b"] = (rnn_vec[i:i + 1] for i in range(4))
    g["conv_w"] = rnn_vec[4:8]
    attn_vec = attn_vec[0] if n_seq == 1 else functools.reduce(jnp.add, [attn_vec[b] for b in range(n_seq)])
    g["q_gain"] = attn_vec[0].reshape(N_Q_HEADS, HEAD_DIM).sum(axis=0)[None, :]
    g["k_gain"] = attn_vec[1, :KV_W].reshape(N_KV_HEADS, HEAD_DIM).sum(axis=0)[None, :]
    g["sinks"] = attn_vec[2:3, :N_Q_HEADS]
    ready(SMALL_BATCH, g, {LOSS_ROW: loss_row})
    g["w_in"] = jnp.concatenate(
        list(run("mm_d_in_rnn", _matmul_tn_multi, h, dz_parts[:2], tt=1024, name="mm_d_in_rnn"))
        + list(run("mm_d_in_rest", _matmul_tn_multi, h, dz_parts[2:], tt=1024, name="mm_d_in_rest")), axis=1)
    ready(3, g)
    w_in_attn, w_in_gate = w["w_in"][:, COL_RNN_END:COL_ATTN_END], w["w_in"][:, COL_ATTN_END:]
    windows = ((w["w_in"], (0, D_MODEL)), (w["w_in"], (D_MODEL, D_MODEL)), (w_in_attn, (0, ATTN_W)),
               (w_in_gate, (0, D_MODEL)), (w_in_gate, (D_MODEL, D_MODEL)))
    grad_x, g["g_mix"] = run(
        "mm_dh", _matmul, dz_parts, [wd[0] for wd in windows], mode="nt", tm=256, tn=1024, out_dtypes=[F32],
        name="mm_dh", b_cols=[wd[1] for wd in windows], epilogue=_rmsnorm_bwd_rows, extras=(x, dx1),
        row_vecs=(w["g_mix"],), n_row_sums=1)
    return jnp.sum(loss_row), grad_x, g


MESH_ID = pl.DeviceIdType.MESH


def _coords(index):
    return (index >> 2) & 1, (index >> 1) & 1, index & 1


def _exchange(srcs, kinds, *, name):
    n = len(srcs)
    n_peer = N_DEV - 1

    def body(*refs):
        src, dst = refs[:n], refs[n:2 * n]
        send_sems, recv_sems, local_sems = refs[2 * n:]
        me = 4 * lax.axis_index("x") + 2 * lax.axis_index("y") + lax.axis_index("c")

        def remote(i, d):
            peer = (me + d) & (N_DEV - 1)
            piece = src[i] if kinds[i] == "gather" else src[i].at[peer]
            return pltpu.make_async_remote_copy(
                src_ref=piece, dst_ref=dst[i].at[me], send_sem=send_sems.at[i * n_peer + d - 1],
                recv_sem=recv_sems.at[i * n_peer + d - 1], device_id=_coords(peer), device_id_type=MESH_ID)

        def arrival(i, d):
            sender = (me - d) & (N_DEV - 1)
            piece = src[i] if kinds[i] == "gather" else src[i].at[sender]
            return pltpu.make_async_remote_copy(
                src_ref=piece, dst_ref=dst[i].at[sender], send_sem=send_sems.at[i * n_peer + d - 1],
                recv_sem=recv_sems.at[i * n_peer + d - 1], device_id=_coords(sender), device_id_type=MESH_ID)

        own = []
        for i in range(n):
            piece = src[i] if kinds[i] == "gather" else src[i].at[me]
            own.append(pltpu.make_async_copy(piece, dst[i].at[me], local_sems.at[i]))
            own[-1].start()
        sent = [remote(i, d) for d in range(1, N_DEV) for i in range(n)]
        for cp in sent:
            cp.start()
        for d in range(1, N_DEV):
            for i in range(n):
                arrival(i, d).wait_recv()
        for cp in sent:
            cp.wait_send()
        for cp in own:
            cp.wait()

    def out_of(s, kind):
        shape = s.shape if kind == "scatter" else (N_DEV,) + s.shape
        return jax.ShapeDtypeStruct(shape, s.dtype)

    any_spec = pl.BlockSpec(memory_space=pl.ANY)
    return pl.pallas_call(
        body,
        in_specs=[any_spec] * n,
        out_specs=[any_spec] * n,
        out_shape=[out_of(s, k) for s, k in zip(srcs, kinds)],
        scratch_shapes=[pltpu.SemaphoreType.DMA((n * n_peer,)), pltpu.SemaphoreType.DMA((n * n_peer,)),
                        pltpu.SemaphoreType.DMA((n,))],
        compiler_params=pltpu.CompilerParams(has_side_effects=True),
        name=name,
    )(*srcs)


def _remote(src, dst, send_sem, recv_sem, to):
    return pltpu.make_async_remote_copy(src_ref=src, dst_ref=dst, send_sem=send_sem, recv_sem=recv_sem,
                                        device_id=to, device_id_type=MESH_ID)


GATHER_PIECES = 4


def _gather_two_level(shards, *, name):
    n = len(shards)
    per = N_DEV - 1
    pieces = []
    for i, s in enumerate(shards):
        n_rows = s.shape[0]
        count = GATHER_PIECES if n_rows % (GATHER_PIECES * LANES) == 0 else 1
        pieces += [(i, r * (n_rows // count), n_rows // count) for r in range(count)]

    def body(*refs):
        src, dst = refs[:n], refs[n:2 * n]
        send_sems, recv_sems, local_sems = refs[2 * n:]
        x, y, c = lax.axis_index("x"), lax.axis_index("y"), lax.axis_index("c")
        me, sibling = (x, y, c), (x, y, 1 - c)
        chips = [(1 - x, y), (x, 1 - y), (1 - x, 1 - y)]

        def slot(pos):
            return 4 * pos[0] + 2 * pos[1] + pos[2]

        def copy(p, k, block, to, from_shard=False):
            i, first_row, rows = pieces[p]
            landed = dst[i].at[slot(block), pl.ds(first_row, rows)]
            source = src[i].at[pl.ds(first_row, rows)] if from_shard else landed
            return _remote(source, landed, send_sems.at[p * per + k], recv_sems.at[p * per + k], to)

        mine = [pltpu.make_async_copy(src[i], dst[i].at[slot(me)], local_sems.at[i]) for i in range(n)]
        for cp in mine:
            cp.start()
        first = []
        for p in range(len(pieces)):
            first.append(copy(p, 0, me, sibling, from_shard=True))
            first += [copy(p, 1 + j, me, (*chip, c), from_shard=True) for j, chip in enumerate(chips)]
        for cp in first:
            cp.start()
        passed = []
        for p in range(len(pieces)):
            for j, chip in enumerate(chips):
                copy(p, 1 + j, (*chip, c), me).wait_recv()
                passed.append(copy(p, 4 + j, (*chip, c), sibling))
                passed[-1].start()
        for p in range(len(pieces)):
            copy(p, 0, sibling, me).wait_recv()
            for j, chip in enumerate(chips):
                copy(p, 4 + j, (*chip, 1 - c), me).wait_recv()
        for cp in first + passed:
            cp.wait_send()
        for cp in mine:
            cp.wait()

    any_spec = pl.BlockSpec(memory_space=pl.ANY)
    return pl.pallas_call(
        body,
        in_specs=[any_spec] * n,
        out_specs=[any_spec] * n,
        out_shape=[jax.ShapeDtypeStruct((N_DEV,) + s.shape, s.dtype) for s in shards],
        scratch_shapes=[pltpu.SemaphoreType.DMA((len(pieces) * per,)), pltpu.SemaphoreType.DMA((len(pieces) * per,)),
                        pltpu.SemaphoreType.DMA((n,))],
        name=name,
    )(*shards)


CHIPS = N_DEV // 2


def _other_chips(x, y):
    return [(x, 1 - y), (1 - x, y), (1 - x, 1 - y)]


def _hosted_gather_first(shards):
    n = len(shards)
    per = CHIPS

    def plan(src, dst, send_sems, recv_sems, local_sems, first_sem):
        x, y, c = lax.axis_index("x"), lax.axis_index("y"), lax.axis_index("c")
        peers = [(x, y, 1 - c)] + [(*chip, c) for chip in _other_chips(x, y)]
        copies = []
        for i in range(n):
            own = pltpu.make_async_copy(src[i], dst[i].at[4 * x + 2 * y + c], local_sems.at[first_sem + i])
            copies.append(_Xfer(own.start, own.wait))
        for j, peer in enumerate(peers):
            for i in range(n):
                k = first_sem + i * per + j
                out = _remote(src[i], dst[i].at[4 * x + 2 * y + c], send_sems.at[k], recv_sems.at[k], peer)
                arrival = _remote(src[i], dst[i].at[4 * peer[0] + 2 * peer[1] + peer[2]], send_sems.at[k],
                                  recv_sems.at[k], peer)

                def wait(out=out, arrival=arrival):
                    arrival.wait_recv()
                    out.wait_send()

                copies.append(_Xfer(out.start, wait))
        return copies

    out_shape = tuple(jax.ShapeDtypeStruct((N_DEV,) + s.shape, s.dtype) for s in shards)
    return _Hosted(tuple(shards), out_shape, n * per, plan)


def _hosted_gather_second(landed):
    n = len(landed)
    per = CHIPS - 1

    def plan(src, dst, send_sems, recv_sems, local_sems, first_sem):
        x, y, c = lax.axis_index("x"), lax.axis_index("y"), lax.axis_index("c")
        copies = []
        for j, chip in enumerate(_other_chips(x, y)):
            mine, theirs = 4 * chip[0] + 2 * chip[1] + c, 4 * chip[0] + 2 * chip[1] + 1 - c
            for i in range(n):
                k = first_sem + i * per + j
                out = _remote(src[i].at[mine], dst[i].at[mine], send_sems.at[k], recv_sems.at[k], (x, y, 1 - c))
                arrival = _remote(src[i].at[theirs], dst[i].at[theirs], send_sems.at[k], recv_sems.at[k],
                                  (x, y, 1 - c))

                def wait(out=out, arrival=arrival):
                    arrival.wait_recv()
                    out.wait_send()

                copies.append(_Xfer(out.start, wait))
        return copies

    out_shape = tuple(jax.ShapeDtypeStruct(a.shape, a.dtype) for a in landed)
    return _Hosted(tuple(landed), out_shape, n * per, plan, tuple((i, i) for i in range(n)))


def _hosted_sibling_swap(arrays, sliced):
    n_sems = sum(CHIPS if s else 1 for s in sliced)

    def plan(src, dst, send_sems, recv_sems, local_sems, first_sem):
        x, y, c = lax.axis_index("x"), lax.axis_index("y"), lax.axis_index("c")
        sibling = (x, y, 1 - c)
        copies, k = [], first_sem
        for i, is_sliced in enumerate(sliced):
            pieces = [(src[i].at[2 * s + 1 - c], dst[i].at[s]) for s in range(CHIPS)] if is_sliced else [(src[i], dst[i])]
            for source, target in pieces:
                cp = _remote(source, target, send_sems.at[k], recv_sems.at[k], sibling)
                copies.append(_Xfer(cp.start, cp.wait))
                k += 1
        return copies

    out_shape = tuple(jax.ShapeDtypeStruct((CHIPS,) + a.shape[1:] if s else a.shape, a.dtype)
                      for a, s in zip(arrays, sliced))
    return _Hosted(tuple(arrays), out_shape, n_sems, plan)


def _hosted_chip_exchange(arrays, sliced):
    n = len(arrays)
    per = CHIPS - 1

    def plan(src, dst, send_sems, recv_sems, local_sems, first_sem):
        x, y, c = lax.axis_index("x"), lax.axis_index("y"), lax.axis_index("c")
        chip = 2 * x + y
        copies = []
        for i in range(n):
            own = pltpu.make_async_copy(src[i].at[chip] if sliced[i] else src[i], dst[i].at[chip],
                                        local_sems.at[first_sem + i])
            copies.append(_Xfer(own.start, own.wait))
        for d in range(1, CHIPS):
            other = chip ^ d
            to = ((other >> 1) & 1, other & 1, c)
            for i in range(n):
                k = first_sem + i * per + d - 1
                source = src[i].at[other] if sliced[i] else src[i]
                out = _remote(source, dst[i].at[chip], send_sems.at[k], recv_sems.at[k], to)
                arrival = _remote(source, dst[i].at[other], send_sems.at[k], recv_sems.at[k], to)

                def wait(out=out, arrival=arrival):
                    arrival.wait_recv()
                    out.wait_send()

                copies.append(_Xfer(out.start, wait))
        return copies

    out_shape = tuple(jax.ShapeDtypeStruct(a.shape if s else (CHIPS,) + a.shape, a.dtype)
                      for a, s in zip(arrays, sliced))
    return _Hosted(tuple(arrays), out_shape, n * per, plan)


def _add_sibling(parts, received, core, *, name):
    _, r, cols = parts.shape
    tr = min(1024, r)

    def body(core_ref, a_ref, b_ref, o_ref):
        o_ref[...] = (a_ref[...] + b_ref[...]).astype(BF16)

    grid_spec = pltpu.PrefetchScalarGridSpec(
        num_scalar_prefetch=1,
        grid=(CHIPS, r // tr),
        in_specs=[pl.BlockSpec((None, tr, cols), lambda k, i, core_ref: (2 * k + core_ref[0], i, 0)),
                  pl.BlockSpec((None, tr, cols), lambda k, i, core_ref: (k, i, 0))],
        out_specs=pl.BlockSpec((None, tr, cols), lambda k, i, core_ref: (k, i, 0)),
    )
    return pl.pallas_call(body, grid_spec=grid_spec, out_shape=jax.ShapeDtypeStruct((CHIPS, r, cols), BF16),
                          compiler_params=_params("parallel", "parallel"), name=name)(core, parts, received)


def _add_whole(a, b, *, name):
    def body(a_ref, b_ref, o_ref):
        o_ref[...] = a_ref[...] + b_ref[...]

    return pl.pallas_call(body, out_shape=jax.ShapeDtypeStruct(a.shape, F32), name=name)(a, b)


def _adamw(parts, w, m, v, *, name):
    r, c = w.shape
    n_parts = parts.shape[0]
    tr = min(512, r)
    c1 = 1.0 - ADAM_B1 ** ADAM_STEP
    c2 = 1.0 - ADAM_B2 ** ADAM_STEP

    def body(p_ref, w_ref, m_ref, v_ref, g_ref, d_ref, nm_ref, nv_ref):
        g = p_ref[0].astype(F32)
        for s in range(1, n_parts):
            g = g + p_ref[s].astype(F32)
        nm = ADAM_B1 * m_ref[...] + (1.0 - ADAM_B1) * g
        nv = ADAM_B2 * v_ref[...] + (1.0 - ADAM_B2) * (g * g)
        g_ref[...] = g
        nm_ref[...] = nm
        nv_ref[...] = nv
        d_ref[...] = -ADAM_LR * ((nm / c1) / (jnp.sqrt(nv / c2) + ADAM_EPS) + ADAM_WD * w_ref[...])

    tile = pl.BlockSpec((tr, c), lambda i: (i, 0))
    return pl.pallas_call(
        body,
        grid=(r // tr,),
        in_specs=[pl.BlockSpec((n_parts, tr, c), lambda i: (0, i, 0)), tile, tile, tile],
        out_specs=[tile] * 4,
        out_shape=[jax.ShapeDtypeStruct((r, c), F32)] * 4,
        compiler_params=_params("parallel"),
        name=name,
    )(parts, w, m, v)


BIG = ("w_in", "w_rnn_proj", "w_attn_proj", "w_out", "w_up", "w_down", "w_ple_gate", "w_ple_proj")
LOSS_ROW = "loss"
SMALL = (("conv_b", 1), ("b_rg", 1), ("b_ig", 1), ("lru_lambda", 1), ("g_mlp", 1), ("g_ple", 1),
         ("q_gain", 1), ("k_gain", 1), ("sinks", 1), (LOSS_ROW, 1), ("w_rg", 64), ("w_ig", 64))
SMALL_ROWS = 144
COL_SHARDED = ("w_in", "w_up", "w_ple_proj")
BATCHES = {1: ("w_ple_proj", "w_ple_gate", "w_down", "w_up"), 2: ("w_out", "w_rnn_proj", "w_attn_proj"),
           3: ("w_in", "conv_w")}
SMALL_BATCH = 4


def _pack_small(vals):
    rows = []
    for nm, nrow in SMALL:
        flat = vals[nm].reshape(-1).astype(F32)
        rows.append(jnp.pad(flat, (0, nrow * D_MODEL - flat.shape[0])).reshape(nrow, D_MODEL))
    used = sum(nrow for _, nrow in SMALL)
    rows.append(jnp.zeros((SMALL_ROWS - used, D_MODEL), F32))
    return jnp.concatenate(rows, axis=0)


def _unpack_small(packed, shapes):
    out, at = {}, 0
    for nm, nrow in SMALL:
        size = 1
        for s in shapes[nm]:
            size *= s
        out[nm] = packed[at:at + nrow].reshape(-1)[:size].reshape(shapes[nm])
        at += nrow
    return out


def _full_weight(name, landed):
    if name in COL_SHARDED:
        return landed.transpose(1, 0, 2).reshape(landed.shape[1], N_DEV * landed.shape[2])
    return landed.reshape(N_DEV * landed.shape[1], landed.shape[2])


def _owner_slots(name, grad):
    if name == "w_in":
        return grad.reshape(D_MODEL, N_DEV, IN_TOTAL // N_DEV).transpose(1, 0, 2)
    if name == "conv_w":
        return grad.reshape(CONV_W, N_DEV, D_MODEL // N_DEV).transpose(1, 0, 2)
    if name in COL_SHARDED:
        return grad
    return grad.reshape(N_DEV, grad.shape[0] // N_DEV, grad.shape[1])


class _StepExchanges:
    FIRST, SECOND = "first", "second"
    PROJ, OUT, PLE_GATE, UP, DOWN = (("w_rnn_proj", "w_attn_proj"), ("w_out",), ("w_ple_gate",), ("w_up",),
                                     ("w_down", "w_ple_proj"))
    GATHERS = {"mm_in": ((FIRST, PROJ), (FIRST, OUT), (FIRST, PLE_GATE)),
               "rnn_fwd": ((SECOND, PROJ), (SECOND, OUT), (SECOND, PLE_GATE), (FIRST, UP)),
               "attn_fwd": ((SECOND, UP), (FIRST, DOWN)), "mm_rnn_proj": ((SECOND, DOWN),)}
    SWAPS = {"mm_dhm": 1, "mm_dya_in": 2, "mm_d_in_rnn": SMALL_BATCH}
    CHIP_EXCHANGES = {"rnn_bwd": ((1, (0, 1, 2)),), "attn_bwd": ((1, (3,)), (2, None)),
                      "mm_d_in_rest": ((SMALL_BATCH, None),), "mm_dh": ((3, None),)}

    def __init__(self, shards, core):
        self.shards = shards
        self.core = core
        self.parts, self.swapped, self.summed, self.half_gathered = {}, {}, {}, {}

    def ready(self, batch, grads, extra=None):
        if batch == SMALL_BATCH:
            self.parts[batch] = ([_pack_small({**grads, **extra})], [False])
            return
        arrays = [_owner_slots(nm, grads[nm]) for nm in BATCHES[batch]]
        self.parts[batch] = (arrays, [True] * len(arrays))
        if batch not in self.SWAPS.values():
            _, self.swapped[batch] = _call(
                lambda: None, grid=(1,), in_specs=[], out_specs=[], out_shape=[], args=(), name="swap_last",
                semantics=("arbitrary",), hosted=_hosted_sibling_swap(*self.parts[batch]))

    def host(self, tag):
        if tag in self.GATHERS:
            return _merge_hosted([
                _hosted_gather_first([self.shards[nm] for nm in group]) if half == self.FIRST
                else _hosted_gather_second([self.half_gathered[nm] for nm in group])
                for half, group in self.GATHERS[tag]])
        if tag in self.SWAPS:
            return _hosted_sibling_swap(*self.parts[self.SWAPS[tag]])
        if tag in self.CHIP_EXCHANGES:
            hosted = []
            for batch, members in self._exchange_members(tag):
                arrays, sliced = self.parts[batch]
                labels = BATCHES.get(batch, ("small",))
                sums = [_add_sibling(arrays[i], self.swapped[batch][i], self.core, name="add_" + labels[i])
                        if sliced[i] else _add_whole(arrays[i], self.swapped[batch][i], name="add_" + labels[i])
                        for i in members]
                hosted.append(_hosted_chip_exchange(sums, [sliced[i] for i in members]))
            return _merge_hosted(hosted)
        return None

    def _exchange_members(self, tag):
        return [(batch, members if members is not None else tuple(range(len(self.parts[batch][0]))))
                for batch, members in self.CHIP_EXCHANGES[tag]]

    def landed(self, tag, landed, weights):
        if tag in self.GATHERS:
            names = [(half, nm) for half, group in self.GATHERS[tag] for nm in group]
            for (half, nm), buf in zip(names, landed):
                if half == self.FIRST:
                    self.half_gathered[nm] = buf
                else:
                    weights[nm] = _full_weight(nm, buf)
        elif tag in self.SWAPS:
            self.swapped[self.SWAPS[tag]] = landed
        else:
            at = 0
            for batch, members in self._exchange_members(tag):
                for i in members:
                    self.summed.setdefault(batch, {})[i] = landed[at]
                    at += 1


def kernel(x, p, g_mix, w_in, conv_w, conv_b, w_rg, b_rg, w_ig, b_ig, lru_lambda, w_rnn_proj, q_gain, k_gain, sinks, w_attn_proj, w_out, g_mlp, w_up, w_down, g_ple, w_ple_gate, w_ple_proj, loss_target, m_g_mix, m_w_in, m_conv_w, m_conv_b, m_w_rg, m_b_rg, m_w_ig, m_b_ig, m_lru_lambda, m_w_rnn_proj, m_q_gain, m_k_gain, m_sinks, m_w_attn_proj, m_w_out, m_g_mlp, m_w_up, m_w_down, m_g_ple, m_w_ple_gate, m_w_ple_proj, v_g_mix, v_w_in, v_conv_w, v_conv_b, v_w_rg, v_b_rg, v_w_ig, v_b_ig, v_lru_lambda, v_w_rnn_proj, v_q_gain, v_k_gain, v_sinks, v_w_attn_proj, v_w_out, v_g_mlp, v_w_up, v_w_down, v_g_ple, v_w_ple_gate, v_w_ple_proj):
    names = ("g_mix", "w_in", "conv_w", "conv_b", "w_rg", "b_rg", "w_ig", "b_ig", "lru_lambda", "w_rnn_proj",
             "q_gain", "k_gain", "sinks", "w_attn_proj", "w_out", "g_mlp", "w_up", "w_down", "g_ple",
             "w_ple_gate", "w_ple_proj")
    wts = dict(zip(names, (g_mix, w_in, conv_w, conv_b, w_rg, b_rg, w_ig, b_ig, lru_lambda, w_rnn_proj, q_gain,
                           k_gain, sinks, w_attn_proj, w_out, g_mlp, w_up, w_down, g_ple, w_ple_gate, w_ple_proj)))
    mom1 = dict(zip(names, (m_g_mix, m_w_in, m_conv_w, m_conv_b, m_w_rg, m_b_rg, m_w_ig, m_b_ig, m_lru_lambda,
                            m_w_rnn_proj, m_q_gain, m_k_gain, m_sinks, m_w_attn_proj, m_w_out, m_g_mlp, m_w_up,
                            m_w_down, m_g_ple, m_w_ple_gate, m_w_ple_proj)))
    mom2 = dict(zip(names, (v_g_mix, v_w_in, v_conv_w, v_conv_b, v_w_rg, v_b_rg, v_w_ig, v_b_ig, v_lru_lambda,
                            v_w_rnn_proj, v_q_gain, v_k_gain, v_sinks, v_w_attn_proj, v_w_out, v_g_mlp, v_w_up,
                            v_w_down, v_g_ple, v_w_ple_gate, v_w_ple_proj)))
    n_seq, seq, _ = x.shape
    core = lax.axis_index("c").astype(jnp.int32).reshape(1)

    shards = {nm: wts[nm][0].astype(BF16) for nm in BIG}
    w_in_all, conv_all = _gather_two_level([shards["w_in"], conv_w[0]], name="gather_w_in")
    w = {nm: wts[nm] for nm in names if nm not in BIG}
    w["w_rg"], w["w_ig"] = w_rg[0], w_ig[0]
    w["conv_w"] = conv_all.transpose(1, 0, 2).reshape(CONV_W, D_MODEL)
    w["w_in"] = _full_weight("w_in", w_in_all)
    comm = _StepExchanges(shards, core)
    loss_sum, grad_x, g = _local_step(
        x.reshape(n_seq * seq, D_MODEL), p.reshape(n_seq * seq, PLE_DIM), loss_target.reshape(n_seq * seq, D_MODEL),
        w, n_seq=n_seq, seq=seq, comm=comm)
    del loss_sum

    res = {}
    for batch, batch_names in BATCHES.items():
        for i, nm in enumerate(batch_names):
            res[nm] = _adamw(comm.summed[batch][i], wts[nm][0], mom1[nm][0], mom2[nm][0], name="adamw_" + nm)
    g_mix_parts, = _exchange([g["g_mix"]], ["gather"], name="gather_g_mix")
    res["g_mix"] = [r[0] for r in _adamw(g_mix_parts, g_mix, m_g_mix, v_g_mix, name="adamw_g_mix")]
    small_names = [nm for nm, _ in SMALL if nm != LOSS_ROW]
    full_small = {}
    for src, key in ((wts, "w"), (mom1, "m"), (mom2, "v")):
        vals = {nm: src[nm][0] for nm in small_names}
        vals[LOSS_ROW] = jnp.zeros((1,), F32)
        full_small[key] = _pack_small(vals)
    small_res = _adamw(comm.summed[SMALL_BATCH][0],full_small["w"], full_small["m"], full_small["v"], name="adamw_small")
    shapes = {nm: wts[nm].shape[1:] for nm in small_names}
    shapes[LOSS_ROW] = (D_MODEL,)
    small_out = [_unpack_small(r, shapes) for r in small_res]
    for nm in small_names:
        res[nm] = [so[nm] for so in small_out]
    loss = jnp.sum(small_out[0][LOSS_ROW]) * (0.5 / D_MODEL)

    outs = [loss, grad_x.reshape(n_seq, seq, D_MODEL)]
    for k in range(4):
        outs.extend(res[nm][k][None] for nm in names)
    return tuple(outs)
```

```python
import functools
from typing import Callable, NamedTuple

import jax
import jax.numpy as jnp
from jax import lax
from jax.experimental import pallas as pl
from jax.experimental.pallas import tpu as pltpu

F32 = jnp.float32
BF16 = jnp.bfloat16

N_DEV = 8
D_MODEL = 1024
RNN_BLOCK_W = 64
CONV_W = 4
LRU_C = 8.0
HEAD_DIM = 64
N_Q_HEADS = 16
N_KV_HEADS = 4
KV_W = N_KV_HEADS * HEAD_DIM
WINDOW = 128
ROPE_THETA = 10000.0
D_FF = 4096
PLE_DIM = 256
NORM_EPS = 1e-6
IN_TOTAL = 5632
COL_RNN_END, COL_ATTN_END = 2048, 3584
ATTN_W = COL_ATTN_END - COL_RNN_END
ATTN_K_AT, ATTN_V_AT = 1024, 1280

ADAM_LR = 0.001
ADAM_B1 = 0.9
ADAM_B2 = 0.999
ADAM_EPS = 1e-08
ADAM_WD = 0.01
ADAM_STEP = 10

LANES = 128
SUBLANES = 8
RNN_TILE = 128
VMEM_LIMIT = 48 * 1024 * 1024
NEG_BIG = -1e30


def _params(*sem):
    return pltpu.CompilerParams(dimension_semantics=sem if sem else None, vmem_limit_bytes=VMEM_LIMIT)


def _sig(x):
    return 0.5 * jnp.tanh(0.5 * x) + 0.5


def _dot_nt(a, b):
    return lax.dot_general(a, b, (((1,), (1,)), ((), ())), preferred_element_type=F32)


def _dot_tn(a, b):
    return lax.dot_general(a, b, (((0,), (0,)), ((), ())), preferred_element_type=F32)


class _Xfer:
    def __init__(self, start, wait):
        self.start, self.wait = start, wait


class _Hosted(NamedTuple):
    srcs: tuple
    out_shape: tuple
    n_sems: int
    plan: Callable
    aliases: tuple = ()


def _merge_hosted(parts):
    parts = [p for p in parts if p is not None]
    if len(parts) <= 1:
        return parts[0] if parts else None
    src_at, dst_at, sem_at, aliases = [0], [0], [0], []
    for p in parts:
        aliases += [(i + src_at[-1], j + dst_at[-1]) for i, j in p.aliases]
        src_at.append(src_at[-1] + len(p.srcs))
        dst_at.append(dst_at[-1] + len(p.out_shape))
        sem_at.append(sem_at[-1] + p.n_sems)

    def plan(src, dst, send_sems, recv_sems, local_sems, first_sem):
        copies = []
        for k, p in enumerate(parts):
            copies += p.plan(src[src_at[k]:src_at[k + 1]], dst[dst_at[k]:dst_at[k + 1]], send_sems, recv_sems,
                             local_sems, first_sem + sem_at[k])
        return copies

    return _Hosted(tuple(a for p in parts for a in p.srcs), tuple(s for p in parts for s in p.out_shape),
                   sem_at[-1], plan, tuple(aliases))


def _call(body, *, grid, in_specs, out_specs, out_shape, args, name, semantics, scratch_shapes=(), hosted=None):
    if hosted is None:
        outs = pl.pallas_call(body, grid=grid, in_specs=list(in_specs), out_specs=list(out_specs),
                              out_shape=list(out_shape), scratch_shapes=list(scratch_shapes),
                              compiler_params=_params(*semantics), name=name)(*args)
        return list(outs), []
    counts = (len(in_specs), len(hosted.srcs), len(out_specs), len(hosted.out_shape), len(scratch_shapes), 3)

    def wrapped(*refs):
        at, groups = 0, []
        for count in counts:
            groups.append(refs[at:at + count])
            at += count
        ins, srcs, outs, dsts, scratch, sems = groups
        copies = hosted.plan(srcs, dsts, *sems, 0)
        ids = [pl.program_id(axis) for axis in range(len(grid))]
        first = functools.reduce(jnp.logical_and, [i == 0 for i in ids])
        last = functools.reduce(jnp.logical_and, [i == g - 1 for i, g in zip(ids, grid)])

        @pl.when(first)
        def _():
            for cp in copies:
                cp.start()

        body(*ins, *outs, *scratch)

        @pl.when(last)
        def _():
            for cp in copies:
                cp.wait()

    any_spec = pl.BlockSpec(memory_space=pl.ANY)
    sems = [pltpu.SemaphoreType.DMA((hosted.n_sems,))] * 3
    outs = pl.pallas_call(
        wrapped, grid=grid, in_specs=list(in_specs) + [any_spec] * counts[1],
        out_specs=list(out_specs) + [any_spec] * counts[3], out_shape=list(out_shape) + list(hosted.out_shape),
        scratch_shapes=list(scratch_shapes) + sems, compiler_params=_params(*["arbitrary"] * len(grid)),
        input_output_aliases={counts[0] + i: counts[2] + j for i, j in hosted.aliases},
        name=name)(*args, *hosted.srcs)
    return list(outs[:counts[2]]), list(outs[counts[2]:])


def _dividing_tile(n, want):
    tile = min(want, n)
    while n % tile:
        tile -= LANES
    return tile


def _matmul(a, b, *, mode, tm, tn, out_dtypes, name, epilogue=None, extras=(), hosted=None, b_cols=None,
            row_vecs=(), n_row_sums=0, extra_col_blocks=None):
    a_parts = tuple(a) if isinstance(a, (tuple, list)) else (a,)
    b_parts = tuple(b) if isinstance(b, (tuple, list)) else (b,)
    assert len(a_parts) == len(b_parts) and (mode == "nt" or len(a_parts) == 1)
    n_parts = len(a_parts)
    m = a_parts[0].shape[0]
    if b_cols is None:
        b_cols = [(0, bp.shape[1]) for bp in b_parts]
    n = b_cols[0][1] if mode == "nn" else b_parts[0].shape[0]
    tm, tn = min(tm, m), _dividing_tile(n, tn)
    n_extra = len(extras) + len(row_vecs)
    n_tiles_out = len(out_dtypes)
    assert n_row_sums == 0 or n == tn

    def body(*refs):
        a_refs, b_refs = refs[:n_parts], refs[n_parts:2 * n_parts]
        rest = refs[2 * n_parts:]
        extra_refs, out_refs = rest[:n_extra], rest[n_extra:]
        if mode == "nn":
            acc = jnp.dot(a_refs[0][...], b_refs[0][...], preferred_element_type=F32)
        else:
            acc = _dot_nt(a_refs[0][...], b_refs[0][...])
            for a_ref, b_ref in zip(a_refs[1:], b_refs[1:]):
                acc = acc + _dot_nt(a_ref[...], b_ref[...])
        res = epilogue(acc, *[e[...] for e in extra_refs]) if epilogue is not None else (acc,)
        for o_ref, r in zip(out_refs[:n_tiles_out], res):
            o_ref[...] = r.astype(o_ref.dtype)
        if n_row_sums:
            @pl.when(pl.program_id(0) == 0)
            def _():
                for o_ref in out_refs[n_tiles_out:]:
                    o_ref[...] = jnp.zeros_like(o_ref)

            for o_ref, r in zip(out_refs[n_tiles_out:], res[n_tiles_out:]):
                o_ref[...] += r

    a_specs = [pl.BlockSpec((tm, ap.shape[1]), lambda i, j: (i, 0)) for ap in a_parts]
    if mode == "nn":
        assert b_cols[0][0] % tn == 0
        first = b_cols[0][0] // tn
        b_specs = [pl.BlockSpec((b_parts[0].shape[0], tn), lambda i, j: (0, first + j))]
    else:
        assert all(at % width == 0 for at, width in b_cols)
        b_specs = [pl.BlockSpec((tn, width), functools.partial(lambda i, j, blk: (j, blk), blk=at // width))
                   for at, width in b_cols]
    tile = pl.BlockSpec((tm, tn), lambda i, j: (i, j))
    row = pl.BlockSpec((1, tn), lambda i, j: (0, j))
    extra_specs = [pl.BlockSpec((tm, tn), functools.partial(lambda i, j, first: (i, first + j), first=first))
                   for first in (extra_col_blocks or [0] * len(extras))]
    outs, landed = _call(
        body,
        grid=(m // tm, n // tn),
        in_specs=a_specs + b_specs + extra_specs + [row] * len(row_vecs),
        out_specs=[tile] * n_tiles_out + [row] * n_row_sums,
        out_shape=[jax.ShapeDtypeStruct((m, n), dt) for dt in out_dtypes]
        + [jax.ShapeDtypeStruct((1, n), F32)] * n_row_sums,
        args=(*a_parts, *b_parts, *extras, *row_vecs), name=name,
        semantics=("arbitrary" if n_row_sums else "parallel", "arbitrary"), hosted=hosted)
    if hosted is not None:
        return (*outs, landed)
    return outs[0] if len(outs) == 1 else outs


def _matmul_tn(a, b, *, tk, tn, tt, name, slot_cols=None):
    t, k = a.shape
    n = b.shape[1]
    tk, tn, tt = min(tk, k), _dividing_tile(n, tn), min(tt, t)

    def body(a_ref, b_ref, o_ref):
        @pl.when(pl.program_id(2) == 0)
        def _():
            o_ref[...] = jnp.zeros_like(o_ref)

        if slot_cols is None:
            o_ref[...] += _dot_tn(a_ref[...], b_ref[...])
        else:
            av = a_ref[...]
            for s in range(tn // slot_cols):
                o_ref[s] += _dot_tn(av, b_ref[:, s * slot_cols:(s + 1) * slot_cols])

    if slot_cols is not None:
        out_spec = pl.BlockSpec((tn // slot_cols, tk, slot_cols), lambda i, j, s: (j, i, 0))
        out_shape = jax.ShapeDtypeStruct((n // slot_cols, k, slot_cols), F32)
    else:
        out_spec = pl.BlockSpec((tk, tn), lambda i, j, s: (i, j))
        out_shape = jax.ShapeDtypeStruct((k, n), F32)
    return pl.pallas_call(
        body,
        grid=(k // tk, n // tn, t // tt),
        in_specs=[pl.BlockSpec((tt, tk), lambda i, j, s: (s, i)), pl.BlockSpec((tt, tn), lambda i, j, s: (s, j))],
        out_specs=out_spec,
        out_shape=out_shape,
        compiler_params=_params("parallel", "parallel", "arbitrary"),
        name=name,
    )(a, b)


def _matmul_tn_multi(a, bs, *, tt, name, hosted=None):
    t, k = a.shape
    tt = min(tt, t)
    n_b = len(bs)

    def body(a_ref, *refs):
        b_refs, o_refs = refs[:n_b], refs[n_b:]

        @pl.when(pl.program_id(0) == 0)
        def _():
            for o_ref in o_refs:
                o_ref[...] = jnp.zeros_like(o_ref)

        a_t = a_ref[...].T
        for b_ref, o_ref in zip(b_refs, o_refs):
            o_ref[...] += jnp.dot(a_t, b_ref[...], preferred_element_type=F32)

    outs, landed = _call(
        body,
        grid=(t // tt,),
        in_specs=[pl.BlockSpec((tt, k), lambda s: (s, 0))] + [pl.BlockSpec((tt, b.shape[1]), lambda s: (s, 0)) for b in bs],
        out_specs=[pl.BlockSpec((k, b.shape[1]), lambda s: (0, 0)) for b in bs],
        out_shape=[jax.ShapeDtypeStruct((k, b.shape[1]), F32) for b in bs],
        args=(a, *bs), name=name, semantics=("arbitrary",), hosted=hosted)
    return (*outs, landed) if hosted is not None else outs


def _rmsnorm_rows(x, g):
    return x * lax.rsqrt(jnp.mean(x * x, axis=-1, keepdims=True) + NORM_EPS) * g


def _norm_matmul(x, g, b, *, tm, tn, name, hosted=None):
    m, k = x.shape
    n = b.shape[1]
    tm, tn = min(tm, m), _dividing_tile(n, tn)

    def body(x_ref, g_ref, b_ref, z_ref, h_ref, h_s):
        @pl.when(pl.program_id(1) == 0)
        def _():
            h_s[...] = _rmsnorm_rows(x_ref[...], g_ref[...]).astype(BF16)
            h_ref[...] = h_s[...]

        z_ref[...] = jnp.dot(h_s[...], b_ref[...], preferred_element_type=F32)

    rows = pl.BlockSpec((tm, k), lambda i, j: (i, 0))
    outs, landed = _call(
        body,
        grid=(m // tm, n // tn),
        in_specs=[rows, pl.BlockSpec((1, k), lambda i, j: (0, 0)), pl.BlockSpec((k, tn), lambda i, j: (0, j))],
        out_specs=[pl.BlockSpec((tm, tn), lambda i, j: (i, j)), rows],
        out_shape=[jax.ShapeDtypeStruct((m, n), F32), jax.ShapeDtypeStruct((m, k), BF16)],
        scratch_shapes=[pltpu.VMEM((tm, k), BF16)],
        args=(x, g, b), name=name, semantics=("parallel", "arbitrary"), hosted=hosted)
    return (*outs, landed) if hosted is not None else outs


def _rmsnorm_bwd_rows(dy, x, dres, g):
    r = lax.rsqrt(jnp.mean(x * x, axis=-1, keepdims=True) + NORM_EPS)
    xr = x * r
    gy = dy * g
    dx = dres + r * (gy - xr * jnp.mean(gy * xr, axis=-1, keepdims=True))
    return dx, jnp.sum(dy * xr, axis=0, keepdims=True)


def _softplus_neg(lam):
    z = -lam
    return jnp.maximum(z, 0.0) + jnp.log1p(jnp.exp(-jnp.abs(z)))


def _neg_expm1(y, exp_half_y):
    series = -y * (1.0 + y * 0.5 * (1.0 + y * (1.0 / 3.0) * (1.0 + y * 0.25 * (1.0 + y * 0.2))))
    return jnp.where(y > -0.0625, series, 1.0 - exp_half_y * exp_half_y)


def _gelu_parts(x):
    c = 0.7978845608028654
    u = c * (x + 0.044715 * x * x * x)
    th = jnp.tanh(u)
    gel = 0.5 * x * (1.0 + th)
    dgel = 0.5 * (1.0 + th) + 0.5 * x * (1.0 - th * th) * c * (1.0 + 3.0 * 0.044715 * x * x)
    return gel, dgel


def _shift_down(v, k, rows):
    return jnp.where(rows < k, 0.0, pltpu.roll(v, k, 0))


def _shift_up(v, k, rows, n):
    return jnp.where(rows >= n - k, 0.0, pltpu.roll(v, n - k, 0))


def _scan_within_groups(a, b, *, reverse):
    shape = a.shape
    a = a.reshape(shape[0] // SUBLANES, SUBLANES, shape[1])
    b = b.reshape(a.shape)
    in_group = lax.broadcasted_iota(jnp.int32, a.shape, 1)
    for s in (1, 2, 4):
        if reverse:
            inside, shift = in_group < SUBLANES - s, SUBLANES - s
        else:
            inside, shift = in_group >= s, s
        b = b + a * jnp.where(inside, pltpu.roll(b, shift, 1), 0.0)
        a = a * jnp.where(inside, pltpu.roll(a, shift, 1), 1.0)
    return a.reshape(shape), b.reshape(shape)


def _rnn_gates(xc, wrg, brg, wig, big, lam):
    xcb = xc.astype(BF16)
    r = _sig(jnp.dot(xcb, wrg, preferred_element_type=F32) + brg)
    i = _sig(jnp.dot(xcb, wig, preferred_element_type=F32) + big)
    sp = _softplus_neg(lam)
    log_a = -LRU_C * r * sp
    a = jnp.exp(log_a)
    mult = jnp.sqrt(_neg_expm1(2.0 * log_a, a))
    return xcb, r, i, sp, a, mult


def _conv_fwd(xv, cw, cb, rows):
    return (cb + _shift_down(xv, 3, rows) * cw[0:1, :] + _shift_down(xv, 2, rows) * cw[1:2, :]
            + _shift_down(xv, 1, rows) * cw[2:3, :] + xv * cw[3:4, :])


def _rnn_fwd(z, conv_w, conv_b, wrg_bd, b_rg, wig_bd, b_ig, lam, *, n_seq, seq, hosted=None):
    t = n_seq * seq
    ct = RNN_TILE
    n_ct = D_MODEL // ct

    def body(x_ref, g_ref, cw_ref, cb_ref, wrg_ref, brg_ref, wig_ref, big_ref, lam_ref,
             xc_ref, hr_ref, ya_ref, a_s, b_s):
        rows = lax.broadcasted_iota(jnp.int32, (seq, ct), 0)
        xc = _conv_fwd(x_ref[...], cw_ref[...], cb_ref[...], rows)
        _, r, i, sp, a, mult = _rnn_gates(xc, wrg_ref[...], brg_ref[...], wig_ref[...], big_ref[...], lam_ref[...])
        a_s[...], b_s[...] = _scan_within_groups(a, mult * (i * xc), reverse=False)

        def step(j, carry):
            r0 = pl.multiple_of(j * SUBLANES, SUBLANES)
            h = b_s[pl.ds(r0, SUBLANES), :] + a_s[pl.ds(r0, SUBLANES), :] * carry
            hr_ref[pl.ds(r0, SUBLANES), :] = h
            return h[SUBLANES - 1:SUBLANES, :]

        lax.fori_loop(0, seq // SUBLANES, step, jnp.zeros((1, ct), F32), unroll=4)
        gel, _ = _gelu_parts(g_ref[...])
        xc_ref[...] = xc
        ya_ref[...] = (hr_ref[...] * gel).astype(BF16)

    vec = pl.BlockSpec((1, ct), lambda b, c: (0, c))
    gate_w = pl.BlockSpec((None, ct, ct), lambda b, c: (c, 0, 0))
    tile = pl.BlockSpec((seq, ct), lambda b, c: (b, c))
    outs, landed = _call(
        body,
        grid=(n_seq, n_ct),
        in_specs=[
            pl.BlockSpec((seq, ct), lambda b, c: (b, c)),
            pl.BlockSpec((seq, ct), lambda b, c: (b, n_ct + c)),
            pl.BlockSpec((CONV_W, ct), lambda b, c: (0, c)), vec, gate_w, vec, gate_w, vec, vec,
        ],
        out_specs=[tile, tile, tile],
        out_shape=[jax.ShapeDtypeStruct((t, D_MODEL), F32), jax.ShapeDtypeStruct((t, D_MODEL), F32),
                   jax.ShapeDtypeStruct((t, D_MODEL), BF16)],
        scratch_shapes=[pltpu.VMEM((seq, ct), F32), pltpu.VMEM((seq, ct), F32)],
        args=(z, z, conv_w, conv_b, wrg_bd, b_rg, wig_bd, b_ig, lam), name="rnn_fwd",
        semantics=("parallel", "parallel"), hosted=hosted)
    return (*outs, landed) if hosted is not None else outs


def _rnn_bwd(dya, z, xc, hr, conv_w, wrg_bd, b_rg, wig_bd, b_ig, lam, *, n_seq, seq, hosted=None):
    t = n_seq * seq
    ct = RNN_TILE
    n_ct = D_MODEL // ct

    def body(dya_ref, x_ref, g_ref, xc_ref, hr_ref, cw_ref, wrg_ref, brg_ref, wig_ref, big_ref, lam_ref,
             dx_ref, dg_ref, dwrg_ref, dwig_ref, vec_ref, a_s, d_s, g_s):
        rows = lax.broadcasted_iota(jnp.int32, (seq, ct), 0)
        xv, xc, hr, dyv = x_ref[...], xc_ref[...], hr_ref[...], dya_ref[...]
        lamv = lam_ref[...]
        gel, dgel = _gelu_parts(g_ref[...])
        dg_ref[...] = (dyv * hr * dgel).astype(BF16)
        xcb, r, i, sp, a, mult = _rnn_gates(xc, wrg_ref[...], brg_ref[...], wig_ref[...], big_ref[...], lamv)
        a_s[...], d_s[...] = _scan_within_groups(_shift_up(a, 1, rows, seq), dyv * gel, reverse=True)

        def step(k, carry):
            r0 = pl.multiple_of((seq // SUBLANES - 1 - k) * SUBLANES, SUBLANES)
            gs = d_s[pl.ds(r0, SUBLANES), :] + a_s[pl.ds(r0, SUBLANES), :] * carry
            g_s[pl.ds(r0, SUBLANES), :] = gs
            return gs[0:1, :]

        lax.fori_loop(0, seq // SUBLANES, step, jnp.zeros((1, ct), F32), unroll=4)
        gsum = g_s[...]
        gated = i * xc
        d_log_a = gsum * _shift_down(hr, 1, rows) * a - gsum * gated * (a * a / mult)
        d_gated = gsum * mult
        d_pre_r = (d_log_a * (-LRU_C) * sp) * r * (1.0 - r)
        d_pre_i = (d_gated * xc) * i * (1.0 - i)
        dprb, dpib = d_pre_r.astype(BF16), d_pre_i.astype(BF16)
        dxc = d_gated * i + _dot_nt(dprb, wrg_ref[...]) + _dot_nt(dpib, wig_ref[...])
        cw = cw_ref[...]
        dx = dxc * cw[CONV_W - 1:CONV_W, :]
        d_taps = [jnp.sum(dxc * xv, axis=0, keepdims=True)]
        for k in range(1, CONV_W):
            up = _shift_up(dxc, k, rows, seq)
            dx = dx + up * cw[CONV_W - 1 - k:CONV_W - k, :]
            d_taps.append(jnp.sum(up * xv, axis=0, keepdims=True))
        dx_ref[...] = dx.astype(BF16)

        @pl.when(pl.program_id(1) == 0)
        def _():
            dwrg_ref[...] = jnp.zeros_like(dwrg_ref)
            dwig_ref[...] = jnp.zeros_like(dwig_ref)
            vec_ref[...] = jnp.zeros_like(vec_ref)

        dwrg_ref[...] += _dot_tn(xcb, dprb)
        dwig_ref[...] += _dot_tn(xcb, dpib)

        def colsum(v):
            return jnp.sum(v, axis=0, keepdims=True)

        d_sp = colsum(d_log_a * (-LRU_C) * r)
        vec_ref[0:1, :] += colsum(d_pre_r)
        vec_ref[1:2, :] += colsum(d_pre_i)
        vec_ref[2:3, :] += d_sp * (-_sig(-lamv))
        vec_ref[3:4, :] += colsum(dxc)
        for tap in range(CONV_W):
            vec_ref[4 + tap:5 + tap, :] += d_taps[CONV_W - 1 - tap]

    vec = pl.BlockSpec((1, ct), lambda c, b: (0, c))
    gate_w = pl.BlockSpec((None, ct, ct), lambda c, b: (c, 0, 0))
    tile = pl.BlockSpec((seq, ct), lambda c, b: (b, c))
    outs, landed = _call(
        body,
        grid=(n_ct, n_seq),
        in_specs=[
            tile,
            pl.BlockSpec((seq, ct), lambda c, b: (b, c)),
            pl.BlockSpec((seq, ct), lambda c, b: (b, n_ct + c)),
            tile, tile,
            pl.BlockSpec((CONV_W, ct), lambda c, b: (0, c)), gate_w, vec, gate_w, vec, vec,
        ],
        out_specs=[tile, tile, gate_w, gate_w, pl.BlockSpec((8, ct), lambda c, b: (0, c))],
        out_shape=[jax.ShapeDtypeStruct((t, D_MODEL), BF16), jax.ShapeDtypeStruct((t, D_MODEL), BF16),
                   jax.ShapeDtypeStruct((n_ct, ct, ct), F32), jax.ShapeDtypeStruct((n_ct, ct, ct), F32),
                   jax.ShapeDtypeStruct((8, D_MODEL), F32)],
        scratch_shapes=[pltpu.VMEM((seq, ct), F32)] * 3,
        args=(dya, z, z, xc, hr, conv_w, wrg_bd, b_rg, wig_bd, b_ig, lam), name="rnn_bwd",
        semantics=("parallel", "arbitrary"), hosted=hosted)
    return (*outs, landed) if hosted is not None else outs


def _split_hi_lo(x):
    hi = x.astype(BF16)
    return hi, (x - hi.astype(F32)).astype(BF16)


def _dot_split(x, m_twice):
    hi, lo = _split_hi_lo(x)
    return jnp.dot(jnp.concatenate([hi, lo], axis=1), m_twice, preferred_element_type=F32)


def _head_matrices(width):
    ec = ((lax.broadcasted_iota(jnp.int32, (2 * width, LANES), 0) & (width - 1)) // HEAD_DIM
          == lax.broadcasted_iota(jnp.int32, (2 * width, LANES), 1))
    ee = (lax.broadcasted_iota(jnp.int32, (2 * LANES, width), 1) // HEAD_DIM
          == (lax.broadcasted_iota(jnp.int32, (2 * LANES, width), 0) & (LANES - 1)))
    return jnp.where(ec, 1.0, 0.0).astype(BF16), jnp.where(ee, 1.0, 0.0).astype(BF16)


def _swap_halves(y):
    w = y.shape[1]
    first = (lax.broadcasted_iota(jnp.int32, y.shape, 1) % HEAD_DIM) < HEAD_DIM // 2
    return jnp.where(first, pltpu.roll(y, w - HEAD_DIM // 2, 1), pltpu.roll(y, HEAD_DIM // 2, 1))


def _normrope_fwd(x, gain, cos_t, sin_t, ec, ee):
    w = x.shape[1]
    rs = _dot_split(lax.rsqrt(_dot_split(x * x, ec) * (1.0 / HEAD_DIM) + NORM_EPS), ee)
    nx = x * rs
    y = nx * gain
    reps = w // LANES
    out = y * jnp.tile(cos_t, (1, reps)) + _swap_halves(y) * jnp.tile(sin_t, (1, reps))
    return out, nx, rs


def _normrope_bwd(dout, nx, rs, gain, cos_t, sin_t, ec, ee):
    w = dout.shape[1]
    reps = w // LANES
    dy = dout * jnp.tile(cos_t, (1, reps)) + _swap_halves(dout * jnp.tile(sin_t, (1, reps)))
    dgain = jnp.sum(dy * nx, axis=0, keepdims=True)
    dn = dy * gain
    seg = _dot_split(_dot_split(dn * nx, ec) * (1.0 / HEAD_DIM), ee)
    return rs * (dn - nx * seg), dgain


def _pair_operand(t, group):
    chunk = t[:, (group // 2) * LANES:(group // 2 + 1) * LANES]
    low = lax.broadcasted_iota(jnp.int32, chunk.shape, 1) < HEAD_DIM
    rolled = pltpu.roll(chunk, HEAD_DIM, 1)
    return jnp.where(low, chunk, rolled) if group % 2 == 0 else jnp.where(low, rolled, chunk)


GROUP = N_Q_HEADS // N_KV_HEADS
GROUP_W = GROUP * HEAD_DIM


def _replicate_head(t, group):
    return jnp.tile(_pair_operand(t, group), (1, 2))


def _head_blocks(t):
    seg = lax.broadcasted_iota(jnp.int32, t.shape, 1) // HEAD_DIM
    return jnp.concatenate([jnp.where(seg == h, t, 0.0) for h in range(GROUP)], axis=0)


def _stack_heads(t_t, rows):
    return jnp.concatenate([t_t[:, h * rows:(h + 1) * rows] for h in range(GROUP)], axis=0)


def _head_rows(mat_t, group):
    return jnp.concatenate([mat_t[GROUP * group + h:GROUP * group + h + 1, :] for h in range(GROUP)], axis=1)


def _window_masks(blk):
    key = lax.broadcasted_iota(jnp.int32, (blk, GROUP * blk), 0)
    query = lax.broadcasted_iota(jnp.int32, (blk, GROUP * blk), 1) & (blk - 1)
    return key > query, key <= query


def _mask_window(t, before_ok, own_ok, fill):
    blk = t.shape[0] // 2
    return jnp.concatenate([jnp.where(before_ok, t[:blk], fill), jnp.where(own_ok, t[blk:], fill)], axis=0)


def _attn_fwd(z, cos_t, sin_t, q_gain_t, k_gain_t, sinks_t, *, n_seq, seq, hosted=None):
    t = n_seq * seq
    blk = WINDOW
    nb = seq // blk

    def body(q_ref, kp_ref, kc_ref, vp_ref, vc_ref, cosc_ref, sinc_ref, cosp_ref, sinp_ref, qg_ref, kg_ref, sk_ref,
             o_ref, l_ref):
        n = pl.program_id(1)
        ecq, eeq = _head_matrices(D_MODEL)
        eck, eek = _head_matrices(KV_W)
        cosc, sinc = cosc_ref[...], sinc_ref[...]
        qh, _, _ = _normrope_fwd(q_ref[...], qg_ref[...], cosc, sinc, ecq, eeq)
        qh = qh * (HEAD_DIM ** -0.5)
        kc, _, _ = _normrope_fwd(kc_ref[...], kg_ref[...], cosc, sinc, eck, eek)
        kp, _, _ = _normrope_fwd(kp_ref[...], kg_ref[...], cosp_ref[...], sinp_ref[...], eck, eek)
        kcat = jnp.concatenate([kp, kc], axis=0)
        vcat = jnp.concatenate([vp_ref[...], vc_ref[...]], axis=0)
        above, causal = _window_masks(blk)
        above = above & (n > 0)
        head_row = lax.broadcasted_iota(jnp.int32, (blk, blk), 0)
        sk_t = jnp.broadcast_to(sk_ref[...], (blk, LANES)).T
        vcat_t = vcat.T.astype(BF16)
        lmat = jnp.zeros((blk, blk), F32)
        groups = range(N_KV_HEADS)
        cols = [slice(g * GROUP_W, (g + 1) * GROUP_W) for g in groups]
        qh = qh.astype(BF16)
        scores = [_dot_nt(_replicate_head(kcat, g).astype(BF16), _head_blocks(qh[:, cols[g]]))
                  for g in groups]
        probs = []
        for g in groups:
            s = _mask_window(scores[g], above, causal, NEG_BIG)
            sink = _head_rows(sk_t, g)
            m = jnp.maximum(jnp.max(s, axis=0, keepdims=True), sink)
            e = jnp.exp(s - m)
            den = jnp.sum(e, axis=0, keepdims=True) + jnp.exp(sink - m)
            probs.append((e * (1.0 / den)).astype(BF16))
            lse = m + jnp.log(den)
            for h in range(GROUP):
                lmat = lmat + jnp.where(head_row == GROUP * g + h, lse[:, h * blk:(h + 1) * blk], 0.0)
        for g in groups:
            out_t = jnp.dot(vcat_t[g * HEAD_DIM:(g + 1) * HEAD_DIM], probs[g], preferred_element_type=F32)
            o_ref[:, cols[g]] = _stack_heads(out_t, blk).T.astype(BF16)
        l_ref[...] = lmat

    def row(b, n):
        return b * nb + n

    def prev(b, n):
        return b * nb + jnp.maximum(n - 1, 0)

    kw = KV_W
    tab_c = pl.BlockSpec((blk, LANES), lambda b, n: (n, 0))
    tab_p = pl.BlockSpec((blk, LANES), lambda b, n: (jnp.maximum(n - 1, 0), 0))
    outs, landed = _call(
        body,
        grid=(n_seq, nb),
        in_specs=[
            pl.BlockSpec((blk, D_MODEL), lambda b, n: (row(b, n), COL_RNN_END // D_MODEL)),
            pl.BlockSpec((blk, kw), lambda b, n: (prev(b, n), (COL_RNN_END + ATTN_K_AT) // kw)),
            pl.BlockSpec((blk, kw), lambda b, n: (row(b, n), (COL_RNN_END + ATTN_K_AT) // kw)),
            pl.BlockSpec((blk, kw), lambda b, n: (prev(b, n), (COL_RNN_END + ATTN_V_AT) // kw)),
            pl.BlockSpec((blk, kw), lambda b, n: (row(b, n), (COL_RNN_END + ATTN_V_AT) // kw)),
            tab_c, tab_c, tab_p, tab_p,
            pl.BlockSpec((1, D_MODEL), lambda b, n: (0, 0)),
            pl.BlockSpec((1, kw), lambda b, n: (0, 0)),
            pl.BlockSpec((1, LANES), lambda b, n: (0, 0)),
        ],
        out_specs=[pl.BlockSpec((blk, D_MODEL), lambda b, n: (row(b, n), 0)),
                   pl.BlockSpec((blk, LANES), lambda b, n: (row(b, n), 0))],
        out_shape=[jax.ShapeDtypeStruct((t, D_MODEL), BF16), jax.ShapeDtypeStruct((t, LANES), F32)],
        args=(z, z, z, z, z, cos_t, sin_t, cos_t, sin_t, q_gain_t, k_gain_t, sinks_t), name="attn_fwd",
        semantics=("parallel", "parallel"), hosted=hosted)
    return (*outs, landed) if hosted is not None else outs


def _attn_bwd(z, o, lse, do, cos_t, sin_t, q_gain_t, k_gain_t, sinks_t, *, n_seq, seq, hosted=None):
    t = n_seq * seq
    blk = WINDOW
    nb = seq // blk
    kw = KV_W
    scale = HEAD_DIM ** -0.5

    def body(qc_ref, qn_ref, kc_ref, vp_ref, vc_ref, oc_ref, on_ref, doc_ref, don_ref, lc_ref, ln_ref,
             cosc_ref, sinc_ref, cosn_ref, sinn_ref, qg_ref, kg_ref, sk_ref,
             dz_ref, vec_ref, dq_s, q_s, k_s):
        n = pl.program_id(1)
        ecq, eeq = _head_matrices(D_MODEL)
        eck, eek = _head_matrices(KV_W)
        cosc, sinc = cosc_ref[...], sinc_ref[...]
        qg, kg = qg_ref[...], kg_ref[...]
        own, other = n & 1, 1 - (n & 1)

        @pl.when(n == 0)
        def _():
            for part, value in enumerate(_normrope_fwd(qc_ref[...], qg, cosc, sinc, ecq, eeq)):
                q_s[own, part] = value
            k_s[other] = jnp.zeros((blk, kw), F32)

        for part, value in enumerate(_normrope_fwd(qn_ref[...], qg, cosn_ref[...], sinn_ref[...], ecq, eeq)):
            q_s[other, part] = value
        qhc, nqc, rsqc = q_s[own, 0], q_s[own, 1], q_s[own, 2]
        qhn = q_s[other, 0]
        khc, nkc, rskc = _normrope_fwd(kc_ref[...], kg, cosc, sinc, eck, eek)
        khp = k_s[other]
        k_s[own] = khc
        doc = doc_ref[...].astype(F32)
        don = don_ref[...].astype(F32)
        delc = _dot_split(doc * oc_ref[...].astype(F32), ecq)
        deln = _dot_split(don * on_ref[...].astype(F32), ecq)
        lc_t, ln_t, delc_t, deln_t = lc_ref[...], ln_ref[...], delc.T, deln.T
        above, causal = _window_masks(blk)
        above_c, above_n = above & (n > 0), above & (n < nb - 1)
        seg = lax.broadcasted_iota(jnp.int32, (blk, GROUP_W), 1) // HEAD_DIM
        lane = lax.broadcasted_iota(jnp.int32, (1, LANES), 1)
        sk_t = jnp.broadcast_to(sk_ref[...], (blk, LANES)).T
        dsink = jnp.zeros((1, LANES), F32)
        kcat = jnp.concatenate([khp, khc], axis=0)
        vcat = jnp.concatenate([vp_ref[...], vc_ref[...]], axis=0)
        kcat_t = kcat.T.astype(BF16)
        dkh = jnp.zeros((blk, GROUP_W), F32)
        dvh = jnp.zeros((blk, GROUP_W), F32)

        def fold_to(group, t):
            total = t + pltpu.roll(t, HEAD_DIM, 1)
            total = total + pltpu.roll(total, 2 * HEAD_DIM, 1)
            return jnp.where(seg == group, total, 0.0)

        groups = range(N_KV_HEADS)
        cols = [slice(g * GROUP_W, (g + 1) * GROUP_W) for g in groups]
        qsc, qsn = qhc * scale, qhn * scale
        qb_c = [_head_blocks(qsc[:, cols[g]]).astype(BF16) for g in groups]
        qb_n = [_head_blocks(qsn[:, cols[g]]).astype(BF16) for g in groups]
        dob_c = [_head_blocks(doc[:, cols[g]]).astype(BF16) for g in groups]
        dob_n = [_head_blocks(don[:, cols[g]]).astype(BF16) for g in groups]
        raw = []
        for g in groups:
            krep = _replicate_head(kcat, g).astype(BF16)
            vrep = _replicate_head(vcat, g).astype(BF16)
            raw.append((_dot_nt(krep, qb_c[g]), _dot_nt(vrep, dob_c[g]),
                        _dot_nt(krep[blk:], qb_n[g]), _dot_nt(vrep[blk:], dob_n[g])))
        cooked = []
        for g in groups:
            s_c, dp_c, s_n, dp_n = raw[g]
            l_row, d_row = _head_rows(lc_t, g), _head_rows(delc_t, g)
            p_c = _mask_window(jnp.exp(s_c - l_row), above_c, causal, 0.0)
            ds_c = (p_c * (dp_c - d_row)).astype(BF16)
            p_n = jnp.where(above_n, jnp.exp(s_n - _head_rows(ln_t, g)), 0.0)
            ds_n = (p_n * (dp_n - _head_rows(deln_t, g))).astype(BF16)
            cooked.append((p_c[blk:].astype(BF16), ds_c, p_n.astype(BF16), ds_n))
            p_sink = jnp.exp(_head_rows(sk_t, g) - l_row) * d_row
            for h in range(GROUP):
                dsink = dsink + jnp.where(lane == GROUP * g + h,
                                          -jnp.sum(p_sink[:, h * blk:(h + 1) * blk], axis=1, keepdims=True), 0.0)
        for g in groups:
            p_cb, ds_c, p_nb, ds_n = cooked[g]
            dq_t = jnp.dot(kcat_t[g * HEAD_DIM:(g + 1) * HEAD_DIM], ds_c, preferred_element_type=F32)
            dq_s[:, cols[g]] = _stack_heads(dq_t, blk).T * scale
            dk_rep = (jnp.dot(ds_c[blk:], qb_c[g], preferred_element_type=F32)
                      + jnp.dot(ds_n, qb_n[g], preferred_element_type=F32))
            dv_rep = (jnp.dot(p_cb, dob_c[g], preferred_element_type=F32)
                      + jnp.dot(p_nb, dob_n[g], preferred_element_type=F32))
            dkh = dkh + fold_to(g, dk_rep)
            dvh = dvh + fold_to(g, dv_rep)
        dq, dqg = _normrope_bwd(dq_s[...], nqc, rsqc, qg, cosc, sinc, ecq, eeq)
        dk, dkg = _normrope_bwd(dkh, nkc, rskc, kg, cosc, sinc, eck, eek)
        dz_ref[:, :ATTN_K_AT] = dq.astype(BF16)
        dz_ref[:, ATTN_K_AT:ATTN_V_AT] = dk.astype(BF16)
        dz_ref[:, ATTN_V_AT:] = dvh.astype(BF16)

        @pl.when(n == 0)
        def _():
            vec_ref[...] = jnp.zeros_like(vec_ref)

        vec_ref[0:1, :] += dqg
        vec_ref[1:2, 0:kw] += dkg
        vec_ref[2:3, 0:LANES] += dsink

    def row(b, n):
        return b * nb + n

    def prev(b, n):
        return b * nb + jnp.maximum(n - 1, 0)

    def nxt(b, n):
        return b * nb + jnp.minimum(n + 1, nb - 1)

    def tiles(width, col, which):
        return pl.BlockSpec((blk, width), lambda b, n: (which(b, n), col))

    def table(which):
        return pl.BlockSpec((blk, LANES), lambda b, n: (which(0, n), 0))

    outs, landed = _call(
        body,
        grid=(n_seq, nb),
        in_specs=[
            tiles(D_MODEL, COL_RNN_END // D_MODEL, row), tiles(D_MODEL, COL_RNN_END // D_MODEL, nxt),
            tiles(kw, (COL_RNN_END + ATTN_K_AT) // kw, row),
            tiles(kw, (COL_RNN_END + ATTN_V_AT) // kw, prev), tiles(kw, (COL_RNN_END + ATTN_V_AT) // kw, row),
            tiles(D_MODEL, 0, row), tiles(D_MODEL, 0, nxt),
            tiles(D_MODEL, 0, row), tiles(D_MODEL, 0, nxt),
            tiles(LANES, 0, row), tiles(LANES, 0, nxt),
            table(row), table(row), table(nxt), table(nxt),
            pl.BlockSpec((1, D_MODEL), lambda b, n: (0, 0)),
            pl.BlockSpec((1, kw), lambda b, n: (0, 0)),
            pl.BlockSpec((1, LANES), lambda b, n: (0, 0)),
        ],
        out_specs=[tiles(ATTN_W, 0, row), pl.BlockSpec((None, 8, D_MODEL), lambda b, n: (b, 0, 0))],
        out_shape=[jax.ShapeDtypeStruct((t, ATTN_W), BF16), jax.ShapeDtypeStruct((n_seq, 8, D_MODEL), F32)],
        scratch_shapes=[pltpu.VMEM((blk, D_MODEL), F32), pltpu.VMEM((2, 3, blk, D_MODEL), F32),
                        pltpu.VMEM((2, blk, kw), F32)],
        args=(z, z, z, z, z, o, o, do, do, lse, lse, cos_t, sin_t, cos_t, sin_t,
              q_gain_t, k_gain_t, sinks_t), name="attn_bwd", semantics=("arbitrary", "arbitrary"), hosted=hosted)
    return (*outs, landed) if hosted is not None else outs


def _rope_tables(seq):
    inv = ROPE_THETA ** (-jnp.arange(0, HEAD_DIM, 2, dtype=F32) / HEAD_DIM)
    ang = jnp.arange(seq, dtype=F32)[:, None] * inv[None, :]
    cos, sin = jnp.cos(ang), jnp.sin(ang)
    return jnp.tile(jnp.concatenate([cos, cos], axis=1), (1, 2)), jnp.tile(jnp.concatenate([-sin, sin], axis=1), (1, 2))


def _block_diag_tiles(w):
    per = RNN_TILE // RNN_BLOCK_W
    w4 = w.reshape(D_MODEL // RNN_TILE, per, RNN_BLOCK_W, RNN_BLOCK_W)
    eye = jnp.eye(per, dtype=w.dtype)
    dense = jnp.einsum("tpij,pq->tpiqj", w4, eye)
    return dense.reshape(D_MODEL // RNN_TILE, RNN_TILE, RNN_TILE).astype(BF16)


def _block_diag_extract(dense):
    per = RNN_TILE // RNN_BLOCK_W
    d5 = dense.reshape(D_MODEL // RNN_TILE, per, RNN_BLOCK_W, per, RNN_BLOCK_W)
    blocks = jnp.stack([d5[:, p, :, p, :] for p in range(per)], axis=1)
    return blocks.reshape(D_MODEL // RNN_BLOCK_W, RNN_BLOCK_W, RNN_BLOCK_W)


def _local_step(x, p, target, w, *, n_seq, seq, comm=None):
    w = dict(w)

    def run(tag, fn, *args, **kwargs):
        hosted = comm.host(tag) if comm is not None else None
        if hosted is None:
            return fn(*args, **kwargs)
        *outs, landed = fn(*args, hosted=hosted, **kwargs)
        comm.landed(tag, landed, w)
        return outs[0] if len(outs) == 1 else outs

    def ready(batch, grads, extra=None):
        if comm is not None:
            comm.ready(batch, grads, extra)

    cos_t, sin_t = _rope_tables(seq)
    q_gain_t = jnp.tile(w["q_gain"], (1, N_Q_HEADS))
    k_gain_t = jnp.tile(w["k_gain"], (1, N_KV_HEADS))
    sinks_t = jnp.pad(w["sinks"], ((0, 0), (0, LANES - N_Q_HEADS)))
    wrg_bd, wig_bd = _block_diag_tiles(w["w_rg"]), _block_diag_tiles(w["w_ig"])
    dims = dict(n_seq=n_seq, seq=seq)

    z, h = run("mm_in", _norm_matmul, x, w["g_mix"], w["w_in"], tm=1024, tn=IN_TOTAL // 4, name="mm_in")
    gate_tile = 512
    ga_at, gb_at = COL_ATTN_END // gate_tile, (COL_ATTN_END + D_MODEL) // gate_tile
    xc, hr, ya_in = run("rnn_fwd", _rnn_fwd, z, w["conv_w"], w["conv_b"], wrg_bd, w["b_rg"], wig_bd, w["b_ig"],
                        w["lru_lambda"], **dims)
    o, lse = run("attn_fwd", _attn_fwd, z, cos_t, sin_t, q_gain_t, k_gain_t, sinks_t, **dims)
    ya = run("mm_rnn_proj", _matmul, ya_in, w["w_rnn_proj"], mode="nn", tm=1024, tn=1024, out_dtypes=[F32],
             name="mm_rnn_proj")
    yb, merged = _matmul(
        o, w["w_attn_proj"], mode="nn", tm=1024, tn=gate_tile, out_dtypes=[F32, BF16], name="mm_attn_proj",
        epilogue=lambda acc, ga, gb, yav: (acc, _sig(ga) * yav + _sig(gb) * acc),
        extras=(z, z, ya), extra_col_blocks=(ga_at, gb_at, 0))
    def residual_then_norm(acc, res, gain):
        new = res + acc
        return new, _rmsnorm_rows(new, gain)

    x1, hm = _matmul(merged, w["w_out"], mode="nn", tm=512, tn=1024, out_dtypes=[F32, BF16], name="mm_out",
                     epilogue=residual_then_norm, extras=(x,), row_vecs=(w["g_mlp"],))
    act = _matmul(hm, w["w_up"], mode="nn", tm=1024, tn=1024, out_dtypes=[BF16], name="mm_up",
                  epilogue=lambda acc: (jnp.square(jnp.maximum(acc, 0.0)),))
    x2, hp = _matmul(act, w["w_down"], mode="nn", tm=512, tn=1024, out_dtypes=[F32, BF16], name="mm_down",
                     epilogue=residual_then_norm, extras=(x1,), row_vecs=(w["g_ple"],))
    p_bf = p.astype(BF16)
    e = _matmul(p_bf, w["w_ple_proj"], mode="nn", tm=1024, tn=1024, out_dtypes=[F32], name="mm_ple_proj")

    def loss_head(gt, x2v, ev, tgt):
        sg = _sig(gt)
        diff = x2v + ev * sg - tgt
        dx = diff * (1.0 / D_MODEL)
        return dx, dx * ev * sg * (1.0 - sg), dx * sg, jnp.sum(diff * diff, axis=0, keepdims=True)

    dx3, dgt, de, loss_row = _matmul(hp, w["w_ple_gate"], mode="nn", tm=512, tn=1024, out_dtypes=[F32, BF16, BF16],
                                     name="mm_ple_gate", epilogue=loss_head, extras=(x2, e, target), n_row_sums=1)

    g = {}
    g["w_ple_proj"] = _matmul_tn(p_bf, de, tk=PLE_DIM, tn=1024, tt=1024, name="mm_d_ple_proj",
                                 slot_cols=D_MODEL // N_DEV)
    g["w_ple_gate"] = _matmul_tn(hp, dgt, tk=1024, tn=1024, tt=1024, name="mm_d_ple_gate")
    def through_norm(dy, xv, dres, gain):
        dx, dgain = _rmsnorm_bwd_rows(dy, xv, dres, gain)
        return dx, dx, dgain

    dx2, dx2_bf, g["g_ple"] = _matmul(
        dgt, w["w_ple_gate"], mode="nt", tm=512, tn=1024, out_dtypes=[F32, BF16], name="mm_dhp",
        epilogue=through_norm, extras=(x2, dx3), row_vecs=(w["g_ple"],), n_row_sums=1)
    g["w_down"] = _matmul_tn(act, dx2_bf, tk=1024, tn=1024, tt=1024, name="mm_d_down")

    def relu_grad(dact, a):
        a = a.astype(F32)
        return (dact * (2.0 * jnp.where(a > 0.0, a * lax.rsqrt(a), 0.0)),)

    du = _matmul(dx2_bf, w["w_down"], mode="nt", tm=1024, tn=1024, out_dtypes=[BF16], name="mm_dact",
                 epilogue=relu_grad, extras=(act,))
    g["w_up"] = _matmul_tn(hm, du, tk=1024, tn=1024, tt=1024, name="mm_d_up", slot_cols=D_FF // N_DEV)
    ready(1, g)
    dx1, dx1_bf, g["g_mlp"] = run(
        "mm_dhm", _matmul, du, w["w_up"], mode="nt", tm=512, tn=1024, out_dtypes=[F32, BF16], name="mm_dhm",
        epilogue=through_norm, extras=(x1, dx2), row_vecs=(w["g_mlp"],), n_row_sums=1)
    g["w_out"] = _matmul_tn(merged, dx1_bf, tk=1024, tn=1024, tt=1024, name="mm_d_out")
    def merge_bwd(dm, ga, gb, yav, ybv):
        sa, sb = _sig(ga), _sig(gb)
        return dm * sa, dm * sb, dm * yav * sa * (1.0 - sa), dm * ybv * sb * (1.0 - sb)

    dya, dyb, dga, dgb = _matmul(dx1_bf, w["w_out"], mode="nt", tm=1024, tn=gate_tile, out_dtypes=[BF16] * 4,
                                 name="mm_dmerged", epilogue=merge_bwd, extras=(z, z, ya, yb),
                                 extra_col_blocks=(ga_at, gb_at, 0, 0))
    g["w_rnn_proj"] = _matmul_tn(ya_in, dya, tk=1024, tn=1024, tt=1024, name="mm_d_rnn_proj")
    g["w_attn_proj"] = _matmul_tn(o, dyb, tk=1024, tn=1024, tt=1024, name="mm_d_attn_proj")
    ready(2, g)
    dya_in = run("mm_dya_in", _matmul, dya, w["w_rnn_proj"], mode="nt", tm=1024, tn=1024, out_dtypes=[F32],
                 name="mm_dya_in")
    do = _matmul(dyb, w["w_attn_proj"], mode="nt", tm=1024, tn=1024, out_dtypes=[BF16], name="mm_do")
    dx_rnn, dg_rnn, dwrg_dense, dwig_dense, rnn_vec = run(
        "rnn_bwd", _rnn_bwd, dya_in, z, xc, hr, w["conv_w"], wrg_bd, w["b_rg"], wig_bd, w["b_ig"],
        w["lru_lambda"], **dims)
    dz_attn, attn_vec = run("attn_bwd", _attn_bwd, z, o, lse, do, cos_t, sin_t, q_gain_t, k_gain_t, sinks_t,
                            **dims)
    dz_parts = (dx_rnn, dg_rnn, dz_attn, dga, dgb)
    g["w_rg"] = _block_diag_extract(dwrg_dense)
    g["w_ig"] = _block_diag_extract(dwig_dense)
    g["b_rg"], g["b_ig"], g["lru_lambda"], g["conv_b"] = (rnn_vec[i:i + 1] for i in range(4))
    g["conv_w"] = rnn_vec[4:8]
    attn_vec = attn_vec[0] if n_seq == 1 else functools.reduce(jnp.add, [attn_vec[b] for b in range(n_seq)])
    g["q_gain"] = attn_vec[0].reshape(N_Q_HEADS, HEAD_DIM).sum(axis=0)[None, :]
    g["k_gain"] = attn_vec[1, :KV_W].reshape(N_KV_HEADS, HEAD_DIM).sum(axis=0)[None, :]
    g["sinks"] = attn_vec[2:3, :N_Q_HEADS]
    ready(SMALL_BATCH, g, {LOSS_ROW: loss_row})
    g["w_in"] = jnp.concatenate(
        list(run("mm_d_in_rnn", _matmul_tn_multi, h, dz_parts[:2], tt=1024, name="mm_d_in_rnn"))
        + list(run("mm_d_in_rest", _matmul_tn_multi, h, dz_parts[2:], tt=512, name="mm_d_in_rest")), axis=1)
    ready(3, g)
    w_in_attn, w_in_gate = w["w_in"][:, COL_RNN_END:COL_ATTN_END], w["w_in"][:, COL_ATTN_END:]
    windows = ((w["w_in"], (0, D_MODEL)), (w["w_in"], (D_MODEL, D_MODEL)), (w_in_attn, (0, ATTN_W)),
               (w_in_gate, (0, D_MODEL)), (w_in_gate, (D_MODEL, D_MODEL)))
    grad_x, g["g_mix"] = run(
        "mm_dh", _matmul, dz_parts, [wd[0] for wd in windows], mode="nt", tm=256, tn=1024, out_dtypes=[F32],
        name="mm_dh", b_cols=[wd[1] for wd in windows], epilogue=_rmsnorm_bwd_rows, extras=(x, dx1),
        row_vecs=(w["g_mix"],), n_row_sums=1)
    return jnp.sum(loss_row), grad_x, g


MESH_ID = pl.DeviceIdType.MESH


def _coords(index):
    return (index >> 2) & 1, (index >> 1) & 1, index & 1


def _exchange(srcs, kinds, *, name):
    n = len(srcs)
    n_peer = N_DEV - 1

    def body(*refs):
        src, dst = refs[:n], refs[n:2 * n]
        send_sems, recv_sems, local_sems = refs[2 * n:]
        me = 4 * lax.axis_index("x") + 2 * lax.axis_index("y") + lax.axis_index("c")

        def remote(i, d):
            peer = (me + d) & (N_DEV - 1)
            piece = src[i] if kinds[i] == "gather" else src[i].at[peer]
            return pltpu.make_async_remote_copy(
                src_ref=piece, dst_ref=dst[i].at[me], send_sem=send_sems.at[i * n_peer + d - 1],
                recv_sem=recv_sems.at[i * n_peer + d - 1], device_id=_coords(peer), device_id_type=MESH_ID)

        def arrival(i, d):
            sender = (me - d) & (N_DEV - 1)
            piece = src[i] if kinds[i] == "gather" else src[i].at[sender]
            return pltpu.make_async_remote_copy(
                src_ref=piece, dst_ref=dst[i].at[sender], send_sem=send_sems.at[i * n_peer + d - 1],
                recv_sem=recv_sems.at[i * n_peer + d - 1], device_id=_coords(sender), device_id_type=MESH_ID)

        own = []
        for i in range(n):
            piece = src[i] if kinds[i] == "gather" else src[i].at[me]
            own.append(pltpu.make_async_copy(piece, dst[i].at[me], local_sems.at[i]))
            own[-1].start()
        sent = [remote(i, d) for d in range(1, N_DEV) for i in range(n)]
        for cp in sent:
            cp.start()
        for d in range(1, N_DEV):
            for i in range(n):
                arrival(i, d).wait_recv()
        for cp in sent:
            cp.wait_send()
        for cp in own:
            cp.wait()

    def out_of(s, kind):
        shape = s.shape if kind == "scatter" else (N_DEV,) + s.shape
        return jax.ShapeDtypeStruct(shape, s.dtype)

    any_spec = pl.BlockSpec(memory_space=pl.ANY)
    return pl.pallas_call(
        body,
        in_specs=[any_spec] * n,
        out_specs=[any_spec] * n,
        out_shape=[out_of(s, k) for s, k in zip(srcs, kinds)],
        scratch_shapes=[pltpu.SemaphoreType.DMA((n * n_peer,)), pltpu.SemaphoreType.DMA((n * n_peer,)),
                        pltpu.SemaphoreType.DMA((n,))],
        compiler_params=pltpu.CompilerParams(has_side_effects=True),
        name=name,
    )(*srcs)


def _remote(src, dst, send_sem, recv_sem, to):
    return pltpu.make_async_remote_copy(src_ref=src, dst_ref=dst, send_sem=send_sem, recv_sem=recv_sem,
                                        device_id=to, device_id_type=MESH_ID)


GATHER_PIECES = 4


def _gather_two_level(shards, *, name):
    n = len(shards)
    per = N_DEV - 1
    pieces = []
    for i, s in enumerate(shards):
        n_rows = s.shape[0]
        count = GATHER_PIECES if n_rows % (GATHER_PIECES * LANES) == 0 else 1
        pieces += [(i, r * (n_rows // count), n_rows // count) for r in range(count)]

    def body(*refs):
        src, dst = refs[:n], refs[n:2 * n]
        send_sems, recv_sems, local_sems = refs[2 * n:]
        x, y, c = lax.axis_index("x"), lax.axis_index("y"), lax.axis_index("c")
        me, sibling = (x, y, c), (x, y, 1 - c)
        chips = [(1 - x, y), (x, 1 - y), (1 - x, 1 - y)]

        def slot(pos):
            return 4 * pos[0] + 2 * pos[1] + pos[2]

        def copy(p, k, block, to, from_shard=False):
            i, first_row, rows = pieces[p]
            landed = dst[i].at[slot(block), pl.ds(first_row, rows)]
            source = src[i].at[pl.ds(first_row, rows)] if from_shard else landed
            return _remote(source, landed, send_sems.at[p * per + k], recv_sems.at[p * per + k], to)

        mine = [pltpu.make_async_copy(src[i], dst[i].at[slot(me)], local_sems.at[i]) for i in range(n)]
        for cp in mine:
            cp.start()
        first = []
        for p in range(len(pieces)):
            first.append(copy(p, 0, me, sibling, from_shard=True))
            first += [copy(p, 1 + j, me, (*chip, c), from_shard=True) for j, chip in enumerate(chips)]
        for cp in first:
            cp.start()
        passed = []
        for p in range(len(pieces)):
            for j, chip in enumerate(chips):
                copy(p, 1 + j, (*chip, c), me).wait_recv()
                passed.append(copy(p, 4 + j, (*chip, c), sibling))
                passed[-1].start()
        for p in range(len(pieces)):
            copy(p, 0, sibling, me).wait_recv()
            for j, chip in enumerate(chips):
                copy(p, 4 + j, (*chip, 1 - c), me).wait_recv()
        for cp in first + passed:
            cp.wait_send()
        for cp in mine:
            cp.wait()

    any_spec = pl.BlockSpec(memory_space=pl.ANY)
    return pl.pallas_call(
        body,
        in_specs=[any_spec] * n,
        out_specs=[any_spec] * n,
        out_shape=[jax.ShapeDtypeStruct((N_DEV,) + s.shape, s.dtype) for s in shards],
        scratch_shapes=[pltpu.SemaphoreType.DMA((len(pieces) * per,)), pltpu.SemaphoreType.DMA((len(pieces) * per,)),
                        pltpu.SemaphoreType.DMA((n,))],
        name=name,
    )(*shards)


CHIPS = N_DEV // 2


def _other_chips(x, y):
    return [(x, 1 - y), (1 - x, y), (1 - x, 1 - y)]


def _hosted_gather_first(shards):
    n = len(shards)
    per = CHIPS

    def plan(src, dst, send_sems, recv_sems, local_sems, first_sem):
        x, y, c = lax.axis_index("x"), lax.axis_index("y"), lax.axis_index("c")
        peers = [(x, y, 1 - c)] + [(*chip, c) for chip in _other_chips(x, y)]
        copies = []
        for i in range(n):
            own = pltpu.make_async_copy(src[i], dst[i].at[4 * x + 2 * y + c], local_sems.at[first_sem + i])
            copies.append(_Xfer(own.start, own.wait))
        for j, peer in enumerate(peers):
            for i in range(n):
                k = first_sem + i * per + j
                out = _remote(src[i], dst[i].at[4 * x + 2 * y + c], send_sems.at[k], recv_sems.at[k], peer)
                arrival = _remote(src[i], dst[i].at[4 * peer[0] + 2 * peer[1] + peer[2]], send_sems.at[k],
                                  recv_sems.at[k], peer)

                def wait(out=out, arrival=arrival):
                    arrival.wait_recv()
                    out.wait_send()

                copies.append(_Xfer(out.start, wait))
        return copies

    out_shape = tuple(jax.ShapeDtypeStruct((N_DEV,) + s.shape, s.dtype) for s in shards)
    return _Hosted(tuple(shards), out_shape, n * per, plan)


def _hosted_gather_second(landed):
    n = len(landed)
    per = CHIPS - 1

    def plan(src, dst, send_sems, recv_sems, local_sems, first_sem):
        x, y, c = lax.axis_index("x"), lax.axis_index("y"), lax.axis_index("c")
        copies = []
        for j, chip in enumerate(_other_chips(x, y)):
            mine, theirs = 4 * chip[0] + 2 * chip[1] + c, 4 * chip[0] + 2 * chip[1] + 1 - c
            for i in range(n):
                k = first_sem + i * per + j
                out = _remote(src[i].at[mine], dst[i].at[mine], send_sems.at[k], recv_sems.at[k], (x, y, 1 - c))
                arrival = _remote(src[i].at[theirs], dst[i].at[theirs], send_sems.at[k], recv_sems.at[k],
                                  (x, y, 1 - c))

                def wait(out=out, arrival=arrival):
                    arrival.wait_recv()
                    out.wait_send()

                copies.append(_Xfer(out.start, wait))
        return copies

    out_shape = tuple(jax.ShapeDtypeStruct(a.shape, a.dtype) for a in landed)
    return _Hosted(tuple(landed), out_shape, n * per, plan, tuple((i, i) for i in range(n)))


def _hosted_sibling_swap(arrays, sliced):
    n_sems = sum(CHIPS if s else 1 for s in sliced)

    def plan(src, dst, send_sems, recv_sems, local_sems, first_sem):
        x, y, c = lax.axis_index("x"), lax.axis_index("y"), lax.axis_index("c")
        sibling = (x, y, 1 - c)
        copies, k = [], first_sem
        for i, is_sliced in enumerate(sliced):
            pieces = [(src[i].at[2 * s + 1 - c], dst[i].at[s]) for s in range(CHIPS)] if is_sliced else [(src[i], dst[i])]
            for source, target in pieces:
                cp = _remote(source, target, send_sems.at[k], recv_sems.at[k], sibling)
                copies.append(_Xfer(cp.start, cp.wait))
                k += 1
        return copies

    out_shape = tuple(jax.ShapeDtypeStruct((CHIPS,) + a.shape[1:] if s else a.shape, a.dtype)
                      for a, s in zip(arrays, sliced))
    return _Hosted(tuple(arrays), out_shape, n_sems, plan)


def _hosted_chip_exchange(arrays, sliced):
    n = len(arrays)
    per = CHIPS - 1

    def plan(src, dst, send_sems, recv_sems, local_sems, first_sem):
        x, y, c = lax.axis_index("x"), lax.axis_index("y"), lax.axis_index("c")
        chip = 2 * x + y
        copies = []
        for i in range(n):
            own = pltpu.make_async_copy(src[i].at[chip] if sliced[i] else src[i], dst[i].at[chip],
                                        local_sems.at[first_sem + i])
            copies.append(_Xfer(own.start, own.wait))
        for d in range(1, CHIPS):
            other = chip ^ d
            to = ((other >> 1) & 1, other & 1, c)
            for i in range(n):
                k = first_sem + i * per + d - 1
                source = src[i].at[other] if sliced[i] else src[i]
                out = _remote(source, dst[i].at[chip], send_sems.at[k], recv_sems.at[k], to)
                arrival = _remote(source, dst[i].at[other], send_sems.at[k], recv_sems.at[k], to)

                def wait(out=out, arrival=arrival):
                    arrival.wait_recv()
                    out.wait_send()

                copies.append(_Xfer(out.start, wait))
        return copies

    out_shape = tuple(jax.ShapeDtypeStruct(a.shape if s else (CHIPS,) + a.shape, a.dtype)
                      for a, s in zip(arrays, sliced))
    return _Hosted(tuple(arrays), out_shape, n * per, plan)


def _add_sibling(parts, received, core, *, name):
    _, r, cols = parts.shape
    tr = min(1024, r)

    def body(core_ref, a_ref, b_ref, o_ref):
        o_ref[...] = (a_ref[...] + b_ref[...]).astype(BF16)

    grid_spec = pltpu.PrefetchScalarGridSpec(
        num_scalar_prefetch=1,
        grid=(CHIPS, r // tr),
        in_specs=[pl.BlockSpec((None, tr, cols), lambda k, i, core_ref: (2 * k + core_ref[0], i, 0)),
                  pl.BlockSpec((None, tr, cols), lambda k, i, core_ref: (k, i, 0))],
        out_specs=pl.BlockSpec((None, tr, cols), lambda k, i, core_ref: (k, i, 0)),
    )
    return pl.pallas_call(body, grid_spec=grid_spec, out_shape=jax.ShapeDtypeStruct((CHIPS, r, cols), BF16),
                          compiler_params=_params("parallel", "parallel"), name=name)(core, parts, received)


def _add_whole(a, b, *, name):
    def body(a_ref, b_ref, o_ref):
        o_ref[...] = a_ref[...] + b_ref[...]

    return pl.pallas_call(body, out_shape=jax.ShapeDtypeStruct(a.shape, F32), name=name)(a, b)


def _adamw(parts, w, m, v, *, name):
    r, c = w.shape
    n_parts = parts.shape[0]
    tr = min(512, r)
    c1 = 1.0 - ADAM_B1 ** ADAM_STEP
    c2 = 1.0 - ADAM_B2 ** ADAM_STEP

    def body(p_ref, w_ref, m_ref, v_ref, g_ref, d_ref, nm_ref, nv_ref):
        g = p_ref[0].astype(F32)
        for s in range(1, n_parts):
            g = g + p_ref[s].astype(F32)
        nm = ADAM_B1 * m_ref[...] + (1.0 - ADAM_B1) * g
        nv = ADAM_B2 * v_ref[...] + (1.0 - ADAM_B2) * (g * g)
        g_ref[...] = g
        nm_ref[...] = nm
        nv_ref[...] = nv
        d_ref[...] = -ADAM_LR * ((nm / c1) / (jnp.sqrt(nv / c2) + ADAM_EPS) + ADAM_WD * w_ref[...])

    tile = pl.BlockSpec((tr, c), lambda i: (i, 0))
    return pl.pallas_call(
        body,
        grid=(r // tr,),
        in_specs=[pl.BlockSpec((n_parts, tr, c), lambda i: (0, i, 0)), tile, tile, tile],
        out_specs=[tile] * 4,
        out_shape=[jax.ShapeDtypeStruct((r, c), F32)] * 4,
        compiler_params=_params("parallel"),
        name=name,
    )(parts, w, m, v)


BIG = ("w_in", "w_rnn_proj", "w_attn_proj", "w_out", "w_up", "w_down", "w_ple_gate", "w_ple_proj")
LOSS_ROW = "loss"
SMALL = (("conv_b", 1), ("b_rg", 1), ("b_ig", 1), ("lru_lambda", 1), ("g_mlp", 1), ("g_ple", 1),
         ("q_gain", 1), ("k_gain", 1), ("sinks", 1), (LOSS_ROW, 1), ("w_rg", 64), ("w_ig", 64))
SMALL_ROWS = 144
COL_SHARDED = ("w_in", "w_up", "w_ple_proj")
BATCHES = {1: ("w_ple_proj", "w_ple_gate", "w_down", "w_up"), 2: ("w_out", "w_rnn_proj", "w_attn_proj"),
           3: ("w_in", "conv_w")}
SMALL_BATCH = 4


def _pack_small(vals):
    rows = []
    for nm, nrow in SMALL:
        flat = vals[nm].reshape(-1).astype(F32)
        rows.append(jnp.pad(flat, (0, nrow * D_MODEL - flat.shape[0])).reshape(nrow, D_MODEL))
    used = sum(nrow for _, nrow in SMALL)
    rows.append(jnp.zeros((SMALL_ROWS - used, D_MODEL), F32))
    return jnp.concatenate(rows, axis=0)


def _unpack_small(packed, shapes):
    out, at = {}, 0
    for nm, nrow in SMALL:
        size = 1
        for s in shapes[nm]:
            size *= s
        out[nm] = packed[at:at + nrow].reshape(-1)[:size].reshape(shapes[nm])
        at += nrow
    return out


def _full_weight(name, landed):
    if name in COL_SHARDED:
        return landed.transpose(1, 0, 2).reshape(landed.shape[1], N_DEV * landed.shape[2])
    return landed.reshape(N_DEV * landed.shape[1], landed.shape[2])


def _owner_slots(name, grad):
    if name == "w_in":
        return grad.reshape(D_MODEL, N_DEV, IN_TOTAL // N_DEV).transpose(1, 0, 2)
    if name == "conv_w":
        return grad.reshape(CONV_W, N_DEV, D_MODEL // N_DEV).transpose(1, 0, 2)
    if name in COL_SHARDED:
        return grad
    return grad.reshape(N_DEV, grad.shape[0] // N_DEV, grad.shape[1])


class _StepExchanges:
    FIRST, SECOND = "first", "second"
    PROJ, OUT, PLE_GATE, UP, DOWN = (("w_rnn_proj", "w_attn_proj"), ("w_out",), ("w_ple_gate",), ("w_up",),
                                     ("w_down", "w_ple_proj"))
    GATHERS = {"mm_in": ((FIRST, PROJ), (FIRST, OUT), (FIRST, PLE_GATE)),
               "rnn_fwd": ((SECOND, PROJ), (SECOND, OUT), (SECOND, PLE_GATE), (FIRST, UP)),
               "attn_fwd": ((SECOND, UP), (FIRST, DOWN)), "mm_rnn_proj": ((SECOND, DOWN),)}
    SWAPS = {"mm_dhm": 1, "mm_dya_in": 2, "mm_d_in_rnn": SMALL_BATCH}
    CHIP_EXCHANGES = {"rnn_bwd": ((1, (0, 1, 2)),), "attn_bwd": ((1, (3,)), (2, None)),
                      "mm_d_in_rest": ((SMALL_BATCH, None),), "mm_dh": ((3, None),)}

    def __init__(self, shards, core):
        self.shards = shards
        self.core = core
        self.parts, self.swapped, self.summed, self.half_gathered = {}, {}, {}, {}

    def ready(self, batch, grads, extra=None):
        if batch == SMALL_BATCH:
            self.parts[batch] = ([_pack_small({**grads, **extra})], [False])
            return
        arrays = [_owner_slots(nm, grads[nm]) for nm in BATCHES[batch]]
        self.parts[batch] = (arrays, [True] * len(arrays))
        if batch not in self.SWAPS.values():
            _, self.swapped[batch] = _call(
                lambda: None, grid=(1,), in_specs=[], out_specs=[], out_shape=[], args=(), name="swap_last",
                semantics=("arbitrary",), hosted=_hosted_sibling_swap(*self.parts[batch]))

    def host(self, tag):
        if tag in self.GATHERS:
            return _merge_hosted([
                _hosted_gather_first([self.shards[nm] for nm in group]) if half == self.FIRST
                else _hosted_gather_second([self.half_gathered[nm] for nm in group])
                for half, group in self.GATHERS[tag]])
        if tag in self.SWAPS:
            return _hosted_sibling_swap(*self.parts[self.SWAPS[tag]])
        if tag in self.CHIP_EXCHANGES:
            hosted = []
            for batch, members in self._exchange_members(tag):
                arrays, sliced = self.parts[batch]
                labels = BATCHES.get(batch, ("small",))
                sums = [_add_sibling(arrays[i], self.swapped[batch][i], self.core, name="add_" + labels[i])
                        if sliced[i] else _add_whole(arrays[i], self.swapped[batch][i], name="add_" + labels[i])
                        for i in members]
                hosted.append(_hosted_chip_exchange(sums, [sliced[i] for i in members]))
            return _merge_hosted(hosted)
        return None

    def _exchange_members(self, tag):
        return [(batch, members if members is not None else tuple(range(len(self.parts[batch][0]))))
                for batch, members in self.CHIP_EXCHANGES[tag]]

    def landed(self, tag, landed, weights):
        if tag in self.GATHERS:
            names = [(half, nm) for half, group in self.GATHERS[tag] for nm in group]
            for (half, nm), buf in zip(names, landed):
                if half == self.FIRST:
                    self.half_gathered[nm] = buf
                else:
                    weights[nm] = _full_weight(nm, buf)
        elif tag in self.SWAPS:
            self.swapped[self.SWAPS[tag]] = landed
        else:
            at = 0
            for batch, members in self._exchange_members(tag):
                for i in members:
                    self.summed.setdefault(batch, {})[i] = landed[at]
                    at += 1


def kernel(x, p, g_mix, w_in, conv_w, conv_b, w_rg, b_rg, w_ig, b_ig, lru_lambda, w_rnn_proj, q_gain, k_gain, sinks, w_attn_proj, w_out, g_mlp, w_up, w_down, g_ple, w_ple_gate, w_ple_proj, loss_target, m_g_mix, m_w_in, m_conv_w, m_conv_b, m_w_rg, m_b_rg, m_w_ig, m_b_ig, m_lru_lambda, m_w_rnn_proj, m_q_gain, m_k_gain, m_sinks, m_w_attn_proj, m_w_out, m_g_mlp, m_w_up, m_w_down, m_g_ple, m_w_ple_gate, m_w_ple_proj, v_g_mix, v_w_in, v_conv_w, v_conv_b, v_w_rg, v_b_rg, v_w_ig, v_b_ig, v_lru_lambda, v_w_rnn_proj, v_q_gain, v_k_gain, v_sinks, v_w_attn_proj, v_w_out, v_g_mlp, v_w_up, v_w_down, v_g_ple, v_w_ple_gate, v_w_ple_proj):
    names = ("g_mix", "w_in", "conv_w", "conv_b", "w_rg", "b_rg", "w_ig", "b_ig", "lru_lambda", "w_rnn_proj",
             "q_gain", "k_gain", "sinks", "w_attn_proj", "w_out", "g_mlp", "w_up", "w_down", "g_ple",
             "w_ple_gate", "w_ple_proj")
    wts = dict(zip(names, (g_mix, w_in, conv_w, conv_b, w_rg, b_rg, w_ig, b_ig, lru_lambda, w_rnn_proj, q_gain,
                           k_gain, sinks, w_attn_proj, w_out, g_mlp, w_up, w_down, g_ple, w_ple_gate, w_ple_proj)))
    mom1 = dict(zip(names, (m_g_mix, m_w_in, m_conv_w, m_conv_b, m_w_rg, m_b_rg, m_w_ig, m_b_ig, m_lru_lambda,
                            m_w_rnn_proj, m_q_gain, m_k_gain, m_sinks, m_w_attn_proj, m_w_out, m_g_mlp, m_w_up,
                            m_w_down, m_g_ple, m_w_ple_gate, m_w_ple_proj)))
    mom2 = dict(zip(names, (v_g_mix, v_w_in, v_conv_w, v_conv_b, v_w_rg, v_b_rg, v_w_ig, v_b_ig, v_lru_lambda,
                            v_w_rnn_proj, v_q_gain, v_k_gain, v_sinks, v_w_attn_proj, v_w_out, v_g_mlp, v_w_up,
                            v_w_down, v_g_ple, v_w_ple_gate, v_w_ple_proj)))
    n_seq, seq, _ = x.shape
    core = lax.axis_index("c").astype(jnp.int32).reshape(1)

    shards = {nm: wts[nm][0].astype(BF16) for nm in BIG}
    w_in_all, conv_all = _gather_two_level([shards["w_in"], conv_w[0]], name="gather_w_in")
    w = {nm: wts[nm] for nm in names if nm not in BIG}
    w["w_rg"], w["w_ig"] = w_rg[0], w_ig[0]
    w["conv_w"] = conv_all.transpose(1, 0, 2).reshape(CONV_W, D_MODEL)
    w["w_in"] = _full_weight("w_in", w_in_all)
    comm = _StepExchanges(shards, core)
    loss_sum, grad_x, g = _local_step(
        x.reshape(n_seq * seq, D_MODEL), p.reshape(n_seq * seq, PLE_DIM), loss_target.reshape(n_seq * seq, D_MODEL),
        w, n_seq=n_seq, seq=seq, comm=comm)
    del loss_sum

    res = {}
    for batch, batch_names in BATCHES.items():
        for i, nm in enumerate(batch_names):
            res[nm] = _adamw(comm.summed[batch][i], wts[nm][0], mom1[nm][0], mom2[nm][0], name="adamw_" + nm)
    g_mix_parts, = _exchange([g["g_mix"]], ["gather"], name="gather_g_mix")
    res["g_mix"] = [r[0] for r in _adamw(g_mix_parts, g_mix, m_g_mix, v_g_mix, name="adamw_g_mix")]
    small_names = [nm for nm, _ in SMALL if nm != LOSS_ROW]
    full_small = {}
    for src, key in ((wts, "w"), (mom1, "m"), (mom2, "v")):
        vals = {nm: src[nm][0] for nm in small_names}
        vals[LOSS_ROW] = jnp.zeros((1,), F32)
        full_small[key] = _pack_small(vals)
    small_res = _adamw(comm.summed[SMALL_BATCH][0],full_small["w"], full_small["m"], full_small["v"], name="adamw_small")
    shapes = {nm: wts[nm].shape[1:] for nm in small_names}
    shapes[LOSS_ROW] = (D_MODEL,)
    small_out = [_unpack_small(r, shapes) for r in small_res]
    for nm in small_names:
        res[nm] = [so[nm] for so in small_out]
    loss = jnp.sum(small_out[0][LOSS_ROW]) * (0.5 / D_MODEL)

    outs = [loss, grad_x.reshape(n_seq, seq, D_MODEL)]
    for k in range(4):
        outs.extend(res[nm][k][None] for nm in names)
    return tuple(outs)
```

```python
import functools
from typing import Callable, NamedTuple

import jax
import jax.numpy as jnp
from jax import lax
from jax.experimental import pallas as pl
from jax.experimental.pallas import tpu as pltpu

F32 = jnp.float32
BF16 = jnp.bfloat16

N_DEV = 8
D_MODEL = 1024
RNN_BLOCK_W = 64
CONV_W = 4
LRU_C = 8.0
HEAD_DIM = 64
N_Q_HEADS = 16
N_KV_HEADS = 4
KV_W = N_KV_HEADS * HEAD_DIM
WINDOW = 128
ROPE_THETA = 10000.0
D_FF = 4096
PLE_DIM = 256
NORM_EPS = 1e-6
IN_TOTAL = 5632
COL_RNN_END, COL_ATTN_END = 2048, 3584
ATTN_W = COL_ATTN_END - COL_RNN_END
ATTN_K_AT, ATTN_V_AT = 1024, 1280

ADAM_LR = 0.001
ADAM_B1 = 0.9
ADAM_B2 = 0.999
ADAM_EPS = 1e-08
ADAM_WD = 0.01
ADAM_STEP = 10

LANES = 128
SUBLANES = 8
RNN_TILE = 256
VMEM_LIMIT = 48 * 1024 * 1024
NEG_BIG = -1e30


def _params(*sem):
    return pltpu.CompilerParams(dimension_semantics=sem if sem else None, vmem_limit_bytes=VMEM_LIMIT)


def _sig(x):
    return 0.5 * jnp.tanh(0.5 * x) + 0.5


def _dot_nt(a, b):
    return lax.dot_general(a, b, (((1,), (1,)), ((), ())), preferred_element_type=F32)


def _dot_tn(a, b):
    return lax.dot_general(a, b, (((0,), (0,)), ((), ())), preferred_element_type=F32)


class _Xfer:
    def __init__(self, start, wait):
        self.start, self.wait = start, wait


class _Hosted(NamedTuple):
    srcs: tuple
    out_shape: tuple
    n_sems: int
    plan: Callable
    aliases: tuple = ()


def _merge_hosted(parts):
    parts = [p for p in parts if p is not None]
    if len(parts) <= 1:
        return parts[0] if parts else None
    src_at, dst_at, sem_at, aliases = [0], [0], [0], []
    for p in parts:
        aliases += [(i + src_at[-1], j + dst_at[-1]) for i, j in p.aliases]
        src_at.append(src_at[-1] + len(p.srcs))
        dst_at.append(dst_at[-1] + len(p.out_shape))
        sem_at.append(sem_at[-1] + p.n_sems)

    def plan(src, dst, send_sems, recv_sems, local_sems, first_sem):
        copies = []
        for k, p in enumerate(parts):
            copies += p.plan(src[src_at[k]:src_at[k + 1]], dst[dst_at[k]:dst_at[k + 1]], send_sems, recv_sems,
                             local_sems, first_sem + sem_at[k])
        return copies

    return _Hosted(tuple(a for p in parts for a in p.srcs), tuple(s for p in parts for s in p.out_shape),
                   sem_at[-1], plan, tuple(aliases))


def _call(body, *, grid, in_specs, out_specs, out_shape, args, name, semantics, scratch_shapes=(), hosted=None):
    if hosted is None:
        outs = pl.pallas_call(body, grid=grid, in_specs=list(in_specs), out_specs=list(out_specs),
                              out_shape=list(out_shape), scratch_shapes=list(scratch_shapes),
                              compiler_params=_params(*semantics), name=name)(*args)
        return list(outs), []
    counts = (len(in_specs), len(hosted.srcs), len(out_specs), len(hosted.out_shape), len(scratch_shapes), 3)

    def wrapped(*refs):
        at, groups = 0, []
        for count in counts:
            groups.append(refs[at:at + count])
            at += count
        ins, srcs, outs, dsts, scratch, sems = groups
        copies = hosted.plan(srcs, dsts, *sems, 0)
        ids = [pl.program_id(axis) for axis in range(len(grid))]
        first = functools.reduce(jnp.logical_and, [i == 0 for i in ids])
        last = functools.reduce(jnp.logical_and, [i == g - 1 for i, g in zip(ids, grid)])

        @pl.when(first)
        def _():
            for cp in copies:
                cp.start()

        body(*ins, *outs, *scratch)

        @pl.when(last)
        def _():
            for cp in copies:
                cp.wait()

    any_spec = pl.BlockSpec(memory_space=pl.ANY)
    sems = [pltpu.SemaphoreType.DMA((hosted.n_sems,))] * 3
    outs = pl.pallas_call(
        wrapped, grid=grid, in_specs=list(in_specs) + [any_spec] * counts[1],
        out_specs=list(out_specs) + [any_spec] * counts[3], out_shape=list(out_shape) + list(hosted.out_shape),
        scratch_shapes=list(scratch_shapes) + sems, compiler_params=_params(*["arbitrary"] * len(grid)),
        input_output_aliases={counts[0] + i: counts[2] + j for i, j in hosted.aliases},
        name=name)(*args, *hosted.srcs)
    return list(outs[:counts[2]]), list(outs[counts[2]:])


def _dividing_tile(n, want):
    tile = min(want, n)
    while n % tile:
        tile -= LANES
    return tile


def _matmul(a, b, *, mode, tm, tn, out_dtypes, name, epilogue=None, extras=(), hosted=None, b_cols=None,
            row_vecs=(), n_row_sums=0, extra_col_blocks=None):
    a_parts = tuple(a) if isinstance(a, (tuple, list)) else (a,)
    b_parts = tuple(b) if isinstance(b, (tuple, list)) else (b,)
    assert len(a_parts) == len(b_parts) and (mode == "nt" or len(a_parts) == 1)
    n_parts = len(a_parts)
    m = a_parts[0].shape[0]
    if b_cols is None:
        b_cols = [(0, bp.shape[1]) for bp in b_parts]
    n = b_cols[0][1] if mode == "nn" else b_parts[0].shape[0]
    tm, tn = min(tm, m), _dividing_tile(n, tn)
    n_extra = len(extras) + len(row_vecs)
    n_tiles_out = len(out_dtypes)
    assert n_row_sums == 0 or n == tn

    def body(*refs):
        a_refs, b_refs = refs[:n_parts], refs[n_parts:2 * n_parts]
        rest = refs[2 * n_parts:]
        extra_refs, out_refs = rest[:n_extra], rest[n_extra:]
        if mode == "nn":
            acc = jnp.dot(a_refs[0][...], b_refs[0][...], preferred_element_type=F32)
        else:
            acc = _dot_nt(a_refs[0][...], b_refs[0][...])
            for a_ref, b_ref in zip(a_refs[1:], b_refs[1:]):
                acc = acc + _dot_nt(a_ref[...], b_ref[...])
        res = epilogue(acc, *[e[...] for e in extra_refs]) if epilogue is not None else (acc,)
        for o_ref, r in zip(out_refs[:n_tiles_out], res):
            o_ref[...] = r.astype(o_ref.dtype)
        if n_row_sums:
            @pl.when(pl.program_id(0) == 0)
            def _():
                for o_ref in out_refs[n_tiles_out:]:
                    o_ref[...] = jnp.zeros_like(o_ref)

            for o_ref, r in zip(out_refs[n_tiles_out:], res[n_tiles_out:]):
                o_ref[...] += r

    a_specs = [pl.BlockSpec((tm, ap.shape[1]), lambda i, j: (i, 0)) for ap in a_parts]
    if mode == "nn":
        assert b_cols[0][0] % tn == 0
        first = b_cols[0][0] // tn
        b_specs = [pl.BlockSpec((b_parts[0].shape[0], tn), lambda i, j: (0, first + j))]
    else:
        assert all(at % width == 0 for at, width in b_cols)
        b_specs = [pl.BlockSpec((tn, width), functools.partial(lambda i, j, blk: (j, blk), blk=at // width))
                   for at, width in b_cols]
    tile = pl.BlockSpec((tm, tn), lambda i, j: (i, j))
    row = pl.BlockSpec((1, tn), lambda i, j: (0, j))
    extra_specs = [pl.BlockSpec((tm, tn), functools.partial(lambda i, j, first: (i, first + j), first=first))
                   for first in (extra_col_blocks or [0] * len(extras))]
    outs, landed = _call(
        body,
        grid=(m // tm, n // tn),
        in_specs=a_specs + b_specs + extra_specs + [row] * len(row_vecs),
        out_specs=[tile] * n_tiles_out + [row] * n_row_sums,
        out_shape=[jax.ShapeDtypeStruct((m, n), dt) for dt in out_dtypes]
        + [jax.ShapeDtypeStruct((1, n), F32)] * n_row_sums,
        args=(*a_parts, *b_parts, *extras, *row_vecs), name=name,
        semantics=("arbitrary" if n_row_sums else "parallel", "arbitrary"), hosted=hosted)
    if hosted is not None:
        return (*outs, landed)
    return outs[0] if len(outs) == 1 else outs


def _matmul_tn(a, b, *, tk, tn, tt, name, slot_cols=None):
    t, k = a.shape
    n = b.shape[1]
    tk, tn, tt = min(tk, k), _dividing_tile(n, tn), min(tt, t)

    def body(a_ref, b_ref, o_ref):
        @pl.when(pl.program_id(2) == 0)
        def _():
            o_ref[...] = jnp.zeros_like(o_ref)

        if slot_cols is None:
            o_ref[...] += _dot_tn(a_ref[...], b_ref[...])
        else:
            av = a_ref[...]
            for s in range(tn // slot_cols):
                o_ref[s] += _dot_tn(av, b_ref[:, s * slot_cols:(s + 1) * slot_cols])

    if slot_cols is not None:
        out_spec = pl.BlockSpec((tn // slot_cols, tk, slot_cols), lambda i, j, s: (j, i, 0))
        out_shape = jax.ShapeDtypeStruct((n // slot_cols, k, slot_cols), F32)
    else:
        out_spec = pl.BlockSpec((tk, tn), lambda i, j, s: (i, j))
        out_shape = jax.ShapeDtypeStruct((k, n), F32)
    return pl.pallas_call(
        body,
        grid=(k // tk, n // tn, t // tt),
        in_specs=[pl.BlockSpec((tt, tk), lambda i, j, s: (s, i)), pl.BlockSpec((tt, tn), lambda i, j, s: (s, j))],
        out_specs=out_spec,
        out_shape=out_shape,
        compiler_params=_params("parallel", "parallel", "arbitrary"),
        name=name,
    )(a, b)


def _matmul_tn_multi(a, bs, *, tt, name, hosted=None):
    t, k = a.shape
    tt = min(tt, t)
    n_b = len(bs)

    def body(a_ref, *refs):
        b_refs, o_refs = refs[:n_b], refs[n_b:]

        @pl.when(pl.program_id(0) == 0)
        def _():
            for o_ref in o_refs:
                o_ref[...] = jnp.zeros_like(o_ref)

        a_t = a_ref[...].T
        for b_ref, o_ref in zip(b_refs, o_refs):
            o_ref[...] += jnp.dot(a_t, b_ref[...], preferred_element_type=F32)

    outs, landed = _call(
        body,
        grid=(t // tt,),
        in_specs=[pl.BlockSpec((tt, k), lambda s: (s, 0))] + [pl.BlockSpec((tt, b.shape[1]), lambda s: (s, 0)) for b in bs],
        out_specs=[pl.BlockSpec((k, b.shape[1]), lambda s: (0, 0)) for b in bs],
        out_shape=[jax.ShapeDtypeStruct((k, b.shape[1]), F32) for b in bs],
        args=(a, *bs), name=name, semantics=("arbitrary",), hosted=hosted)
    return (*outs, landed) if hosted is not None else outs


def _rmsnorm_rows(x, g):
    return x * lax.rsqrt(jnp.mean(x * x, axis=-1, keepdims=True) + NORM_EPS) * g


def _norm_matmul(x, g, b, *, tm, tn, name, hosted=None):
    m, k = x.shape
    n = b.shape[1]
    tm, tn = min(tm, m), _dividing_tile(n, tn)

    def body(x_ref, g_ref, b_ref, z_ref, h_ref, h_s):
        @pl.when(pl.program_id(1) == 0)
        def _():
            h_s[...] = _rmsnorm_rows(x_ref[...], g_ref[...]).astype(BF16)
            h_ref[...] = h_s[...]

        z_ref[...] = jnp.dot(h_s[...], b_ref[...], preferred_element_type=F32)

    rows = pl.BlockSpec((tm, k), lambda i, j: (i, 0))
    outs, landed = _call(
        body,
        grid=(m // tm, n // tn),
        in_specs=[rows, pl.BlockSpec((1, k), lambda i, j: (0, 0)), pl.BlockSpec((k, tn), lambda i, j: (0, j))],
        out_specs=[pl.BlockSpec((tm, tn), lambda i, j: (i, j)), rows],
        out_shape=[jax.ShapeDtypeStruct((m, n), F32), jax.ShapeDtypeStruct((m, k), BF16)],
        scratch_shapes=[pltpu.VMEM((tm, k), BF16)],
        args=(x, g, b), name=name, semantics=("parallel", "arbitrary"), hosted=hosted)
    return (*outs, landed) if hosted is not None else outs


def _rmsnorm_bwd_rows(dy, x, dres, g):
    r = lax.rsqrt(jnp.mean(x * x, axis=-1, keepdims=True) + NORM_EPS)
    xr = x * r
    gy = dy * g
    dx = dres + r * (gy - xr * jnp.mean(gy * xr, axis=-1, keepdims=True))
    return dx, jnp.sum(dy * xr, axis=0, keepdims=True)


def _softplus_neg(lam):
    z = -lam
    return jnp.maximum(z, 0.0) + jnp.log1p(jnp.exp(-jnp.abs(z)))


def _neg_expm1(y, exp_half_y):
    series = -y * (1.0 + y * 0.5 * (1.0 + y * (1.0 / 3.0) * (1.0 + y * 0.25 * (1.0 + y * 0.2))))
    return jnp.where(y > -0.0625, series, 1.0 - exp_half_y * exp_half_y)


def _gelu_parts(x):
    c = 0.7978845608028654
    u = c * (x + 0.044715 * x * x * x)
    th = jnp.tanh(u)
    gel = 0.5 * x * (1.0 + th)
    dgel = 0.5 * (1.0 + th) + 0.5 * x * (1.0 - th * th) * c * (1.0 + 3.0 * 0.044715 * x * x)
    return gel, dgel


def _shift_down(v, k, rows):
    return jnp.where(rows < k, 0.0, pltpu.roll(v, k, 0))


def _shift_up(v, k, rows, n):
    return jnp.where(rows >= n - k, 0.0, pltpu.roll(v, n - k, 0))


def _scan_within_groups(a, b, *, reverse):
    shape = a.shape
    a = a.reshape(shape[0] // SUBLANES, SUBLANES, shape[1])
    b = b.reshape(a.shape)
    in_group = lax.broadcasted_iota(jnp.int32, a.shape, 1)
    for s in (1, 2, 4):
        if reverse:
            inside, shift = in_group < SUBLANES - s, SUBLANES - s
        else:
            inside, shift = in_group >= s, s
        b = b + a * jnp.where(inside, pltpu.roll(b, shift, 1), 0.0)
        a = a * jnp.where(inside, pltpu.roll(a, shift, 1), 1.0)
    return a.reshape(shape), b.reshape(shape)


def _rnn_gates(xc, wrg, brg, wig, big, lam):
    xcb = xc.astype(BF16)
    r = _sig(jnp.dot(xcb, wrg, preferred_element_type=F32) + brg)
    i = _sig(jnp.dot(xcb, wig, preferred_element_type=F32) + big)
    sp = _softplus_neg(lam)
    log_a = -LRU_C * r * sp
    a = jnp.exp(log_a)
    mult = jnp.sqrt(_neg_expm1(2.0 * log_a, a))
    return xcb, r, i, sp, a, mult


def _conv_fwd(xv, cw, cb, rows):
    return (cb + _shift_down(xv, 3, rows) * cw[0:1, :] + _shift_down(xv, 2, rows) * cw[1:2, :]
            + _shift_down(xv, 1, rows) * cw[2:3, :] + xv * cw[3:4, :])


def _rnn_fwd(z, conv_w, conv_b, wrg_bd, b_rg, wig_bd, b_ig, lam, *, n_seq, seq, hosted=None):
    t = n_seq * seq
    ct = RNN_TILE
    n_ct = D_MODEL // ct

    def body(x_ref, g_ref, cw_ref, cb_ref, wrg_ref, brg_ref, wig_ref, big_ref, lam_ref,
             xc_ref, hr_ref, ya_ref, a_s, b_s):
        rows = lax.broadcasted_iota(jnp.int32, (seq, ct), 0)
        xc = _conv_fwd(x_ref[...], cw_ref[...], cb_ref[...], rows)
        _, r, i, sp, a, mult = _rnn_gates(xc, wrg_ref[...], brg_ref[...], wig_ref[...], big_ref[...], lam_ref[...])
        a_s[...], b_s[...] = _scan_within_groups(a, mult * (i * xc), reverse=False)

        def step(j, carry):
            r0 = pl.multiple_of(j * SUBLANES, SUBLANES)
            h = b_s[pl.ds(r0, SUBLANES), :] + a_s[pl.ds(r0, SUBLANES), :] * carry
            hr_ref[pl.ds(r0, SUBLANES), :] = h
            return h[SUBLANES - 1:SUBLANES, :]

        lax.fori_loop(0, seq // SUBLANES, step, jnp.zeros((1, ct), F32), unroll=4)
        gel, _ = _gelu_parts(g_ref[...])
        xc_ref[...] = xc
        ya_ref[...] = (hr_ref[...] * gel).astype(BF16)

    vec = pl.BlockSpec((1, ct), lambda b, c: (0, c))
    gate_w = pl.BlockSpec((None, ct, ct), lambda b, c: (c, 0, 0))
    tile = pl.BlockSpec((seq, ct), lambda b, c: (b, c))
    outs, landed = _call(
        body,
        grid=(n_seq, n_ct),
        in_specs=[
            pl.BlockSpec((seq, ct), lambda b, c: (b, c)),
            pl.BlockSpec((seq, ct), lambda b, c: (b, n_ct + c)),
            pl.BlockSpec((CONV_W, ct), lambda b, c: (0, c)), vec, gate_w, vec, gate_w, vec, vec,
        ],
        out_specs=[tile, tile, tile],
        out_shape=[jax.ShapeDtypeStruct((t, D_MODEL), F32), jax.ShapeDtypeStruct((t, D_MODEL), F32),
                   jax.ShapeDtypeStruct((t, D_MODEL), BF16)],
        scratch_shapes=[pltpu.VMEM((seq, ct), F32), pltpu.VMEM((seq, ct), F32)],
        args=(z, z, conv_w, conv_b, wrg_bd, b_rg, wig_bd, b_ig, lam), name="rnn_fwd",
        semantics=("parallel", "parallel"), hosted=hosted)
    return (*outs, landed) if hosted is not None else outs


def _rnn_bwd(dya, z, xc, hr, conv_w, wrg_bd, b_rg, wig_bd, b_ig, lam, *, n_seq, seq, hosted=None):
    t = n_seq * seq
    ct = RNN_TILE
    n_ct = D_MODEL // ct

    def body(dya_ref, x_ref, g_ref, xc_ref, hr_ref, cw_ref, wrg_ref, brg_ref, wig_ref, big_ref, lam_ref,
             dx_ref, dg_ref, dwrg_ref, dwig_ref, vec_ref, a_s, d_s, g_s):
        rows = lax.broadcasted_iota(jnp.int32, (seq, ct), 0)
        xv, xc, hr, dyv = x_ref[...], xc_ref[...], hr_ref[...], dya_ref[...]
        lamv = lam_ref[...]
        gel, dgel = _gelu_parts(g_ref[...])
        dg_ref[...] = (dyv * hr * dgel).astype(BF16)
        xcb, r, i, sp, a, mult = _rnn_gates(xc, wrg_ref[...], brg_ref[...], wig_ref[...], big_ref[...], lamv)
        a_s[...], d_s[...] = _scan_within_groups(_shift_up(a, 1, rows, seq), dyv * gel, reverse=True)

        def step(k, carry):
            r0 = pl.multiple_of((seq // SUBLANES - 1 - k) * SUBLANES, SUBLANES)
            gs = d_s[pl.ds(r0, SUBLANES), :] + a_s[pl.ds(r0, SUBLANES), :] * carry
            g_s[pl.ds(r0, SUBLANES), :] = gs
            return gs[0:1, :]

        lax.fori_loop(0, seq // SUBLANES, step, jnp.zeros((1, ct), F32), unroll=4)
        gsum = g_s[...]
        gated = i * xc
        d_log_a = gsum * _shift_down(hr, 1, rows) * a - gsum * gated * (a * a / mult)
        d_gated = gsum * mult
        d_pre_r = (d_log_a * (-LRU_C) * sp) * r * (1.0 - r)
        d_pre_i = (d_gated * xc) * i * (1.0 - i)
        dprb, dpib = d_pre_r.astype(BF16), d_pre_i.astype(BF16)
        dxc = d_gated * i + _dot_nt(dprb, wrg_ref[...]) + _dot_nt(dpib, wig_ref[...])
        cw = cw_ref[...]
        dx = dxc * cw[CONV_W - 1:CONV_W, :]
        d_taps = [jnp.sum(dxc * xv, axis=0, keepdims=True)]
        for k in range(1, CONV_W):
            up = _shift_up(dxc, k, rows, seq)
            dx = dx + up * cw[CONV_W - 1 - k:CONV_W - k, :]
            d_taps.append(jnp.sum(up * xv, axis=0, keepdims=True))
        dx_ref[...] = dx.astype(BF16)

        @pl.when(pl.program_id(1) == 0)
        def _():
            dwrg_ref[...] = jnp.zeros_like(dwrg_ref)
            dwig_ref[...] = jnp.zeros_like(dwig_ref)
            vec_ref[...] = jnp.zeros_like(vec_ref)

        dwrg_ref[...] += _dot_tn(xcb, dprb)
        dwig_ref[...] += _dot_tn(xcb, dpib)

        def colsum(v):
            return jnp.sum(v, axis=0, keepdims=True)

        d_sp = colsum(d_log_a * (-LRU_C) * r)
        vec_ref[0:1, :] += colsum(d_pre_r)
        vec_ref[1:2, :] += colsum(d_pre_i)
        vec_ref[2:3, :] += d_sp * (-_sig(-lamv))
        vec_ref[3:4, :] += colsum(dxc)
        for tap in range(CONV_W):
            vec_ref[4 + tap:5 + tap, :] += d_taps[CONV_W - 1 - tap]

    vec = pl.BlockSpec((1, ct), lambda c, b: (0, c))
    gate_w = pl.BlockSpec((None, ct, ct), lambda c, b: (c, 0, 0))
    tile = pl.BlockSpec((seq, ct), lambda c, b: (b, c))
    outs, landed = _call(
        body,
        grid=(n_ct, n_seq),
        in_specs=[
            tile,
            pl.BlockSpec((seq, ct), lambda c, b: (b, c)),
            pl.BlockSpec((seq, ct), lambda c, b: (b, n_ct + c)),
            tile, tile,
            pl.BlockSpec((CONV_W, ct), lambda c, b: (0, c)), gate_w, vec, gate_w, vec, vec,
        ],
        out_specs=[tile, tile, gate_w, gate_w, pl.BlockSpec((8, ct), lambda c, b: (0, c))],
        out_shape=[jax.ShapeDtypeStruct((t, D_MODEL), BF16), jax.ShapeDtypeStruct((t, D_MODEL), BF16),
                   jax.ShapeDtypeStruct((n_ct, ct, ct), F32), jax.ShapeDtypeStruct((n_ct, ct, ct), F32),
                   jax.ShapeDtypeStruct((8, D_MODEL), F32)],
        scratch_shapes=[pltpu.VMEM((seq, ct), F32)] * 3,
        args=(dya, z, z, xc, hr, conv_w, wrg_bd, b_rg, wig_bd, b_ig, lam), name="rnn_bwd",
        semantics=("parallel", "arbitrary"), hosted=hosted)
    return (*outs, landed) if hosted is not None else outs


def _split_hi_lo(x):
    hi = x.astype(BF16)
    return hi, (x - hi.astype(F32)).astype(BF16)


def _dot_split(x, m_twice):
    hi, lo = _split_hi_lo(x)
    return jnp.dot(jnp.concatenate([hi, lo], axis=1), m_twice, preferred_element_type=F32)


def _head_matrices(width):
    ec = ((lax.broadcasted_iota(jnp.int32, (2 * width, LANES), 0) & (width - 1)) // HEAD_DIM
          == lax.broadcasted_iota(jnp.int32, (2 * width, LANES), 1))
    ee = (lax.broadcasted_iota(jnp.int32, (2 * LANES, width), 1) // HEAD_DIM
          == (lax.broadcasted_iota(jnp.int32, (2 * LANES, width), 0) & (LANES - 1)))
    return jnp.where(ec, 1.0, 0.0).astype(BF16), jnp.where(ee, 1.0, 0.0).astype(BF16)


def _swap_halves(y):
    w = y.shape[1]
    first = (lax.broadcasted_iota(jnp.int32, y.shape, 1) % HEAD_DIM) < HEAD_DIM // 2
    return jnp.where(first, pltpu.roll(y, w - HEAD_DIM // 2, 1), pltpu.roll(y, HEAD_DIM // 2, 1))


def _normrope_fwd(x, gain, cos_t, sin_t, ec, ee):
    w = x.shape[1]
    rs = _dot_split(lax.rsqrt(_dot_split(x * x, ec) * (1.0 / HEAD_DIM) + NORM_EPS), ee)
    nx = x * rs
    y = nx * gain
    reps = w // LANES
    out = y * jnp.tile(cos_t, (1, reps)) + _swap_halves(y) * jnp.tile(sin_t, (1, reps))
    return out, nx, rs


def _normrope_bwd(dout, nx, rs, gain, cos_t, sin_t, ec, ee):
    w = dout.shape[1]
    reps = w // LANES
    dy = dout * jnp.tile(cos_t, (1, reps)) + _swap_halves(dout * jnp.tile(sin_t, (1, reps)))
    dgain = jnp.sum(dy * nx, axis=0, keepdims=True)
    dn = dy * gain
    seg = _dot_split(_dot_split(dn * nx, ec) * (1.0 / HEAD_DIM), ee)
    return rs * (dn - nx * seg), dgain


def _pair_operand(t, group):
    chunk = t[:, (group // 2) * LANES:(group // 2 + 1) * LANES]
    low = lax.broadcasted_iota(jnp.int32, chunk.shape, 1) < HEAD_DIM
    rolled = pltpu.roll(chunk, HEAD_DIM, 1)
    return jnp.where(low, chunk, rolled) if group % 2 == 0 else jnp.where(low, rolled, chunk)


GROUP = N_Q_HEADS // N_KV_HEADS
GROUP_W = GROUP * HEAD_DIM


def _replicate_head(t, group):
    return jnp.tile(_pair_operand(t, group), (1, 2))


def _head_blocks(t):
    seg = lax.broadcasted_iota(jnp.int32, t.shape, 1) // HEAD_DIM
    return jnp.concatenate([jnp.where(seg == h, t, 0.0) for h in range(GROUP)], axis=0)


def _stack_heads(t_t, rows):
    return jnp.concatenate([t_t[:, h * rows:(h + 1) * rows] for h in range(GROUP)], axis=0)


def _head_rows(mat_t, group):
    return jnp.concatenate([mat_t[GROUP * group + h:GROUP * group + h + 1, :] for h in range(GROUP)], axis=1)


def _window_masks(blk):
    key = lax.broadcasted_iota(jnp.int32, (blk, GROUP * blk), 0)
    query = lax.broadcasted_iota(jnp.int32, (blk, GROUP * blk), 1) & (blk - 1)
    return key > query, key <= query


def _mask_window(t, before_ok, own_ok, fill):
    blk = t.shape[0] // 2
    return jnp.concatenate([jnp.where(before_ok, t[:blk], fill), jnp.where(own_ok, t[blk:], fill)], axis=0)


def _attn_fwd(z, cos_t, sin_t, q_gain_t, k_gain_t, sinks_t, *, n_seq, seq, hosted=None):
    t = n_seq * seq
    blk = WINDOW
    nb = seq // blk

    def body(q_ref, kp_ref, kc_ref, vp_ref, vc_ref, cosc_ref, sinc_ref, cosp_ref, sinp_ref, qg_ref, kg_ref, sk_ref,
             o_ref, l_ref):
        n = pl.program_id(1)
        ecq, eeq = _head_matrices(D_MODEL)
        eck, eek = _head_matrices(KV_W)
        cosc, sinc = cosc_ref[...], sinc_ref[...]
        qh, _, _ = _normrope_fwd(q_ref[...], qg_ref[...], cosc, sinc, ecq, eeq)
        qh = qh * (HEAD_DIM ** -0.5)
        kc, _, _ = _normrope_fwd(kc_ref[...], kg_ref[...], cosc, sinc, eck, eek)
        kp, _, _ = _normrope_fwd(kp_ref[...], kg_ref[...], cosp_ref[...], sinp_ref[...], eck, eek)
        kcat = jnp.concatenate([kp, kc], axis=0)
        vcat = jnp.concatenate([vp_ref[...], vc_ref[...]], axis=0)
        above, causal = _window_masks(blk)
        above = above & (n > 0)
        head_row = lax.broadcasted_iota(jnp.int32, (blk, blk), 0)
        sk_t = jnp.broadcast_to(sk_ref[...], (blk, LANES)).T
        vcat_t = vcat.T.astype(BF16)
        lmat = jnp.zeros((blk, blk), F32)
        groups = range(N_KV_HEADS)
        cols = [slice(g * GROUP_W, (g + 1) * GROUP_W) for g in groups]
        qh = qh.astype(BF16)
        scores = [_dot_nt(_replicate_head(kcat, g).astype(BF16), _head_blocks(qh[:, cols[g]]))
                  for g in groups]
        probs = []
        for g in groups:
            s = _mask_window(scores[g], above, causal, NEG_BIG)
            sink = _head_rows(sk_t, g)
            m = jnp.maximum(jnp.max(s, axis=0, keepdims=True), sink)
            e = jnp.exp(s - m)
            den = jnp.sum(e, axis=0, keepdims=True) + jnp.exp(sink - m)
            probs.append((e * (1.0 / den)).astype(BF16))
            lse = m + jnp.log(den)
            for h in range(GROUP):
                lmat = lmat + jnp.where(head_row == GROUP * g + h, lse[:, h * blk:(h + 1) * blk], 0.0)
        for g in groups:
            out_t = jnp.dot(vcat_t[g * HEAD_DIM:(g + 1) * HEAD_DIM], probs[g], preferred_element_type=F32)
            o_ref[:, cols[g]] = _stack_heads(out_t, blk).T.astype(BF16)
        l_ref[...] = lmat

    def row(b, n):
        return b * nb + n

    def prev(b, n):
        return b * nb + jnp.maximum(n - 1, 0)

    kw = KV_W
    tab_c = pl.BlockSpec((blk, LANES), lambda b, n: (n, 0))
    tab_p = pl.BlockSpec((blk, LANES), lambda b, n: (jnp.maximum(n - 1, 0), 0))
    outs, landed = _call(
        body,
        grid=(n_seq, nb),
        in_specs=[
            pl.BlockSpec((blk, D_MODEL), lambda b, n: (row(b, n), COL_RNN_END // D_MODEL)),
            pl.BlockSpec((blk, kw), lambda b, n: (prev(b, n), (COL_RNN_END + ATTN_K_AT) // kw)),
            pl.BlockSpec((blk, kw), lambda b, n: (row(b, n), (COL_RNN_END + ATTN_K_AT) // kw)),
            pl.BlockSpec((blk, kw), lambda b, n: (prev(b, n), (COL_RNN_END + ATTN_V_AT) // kw)),
            pl.BlockSpec((blk, kw), lambda b, n: (row(b, n), (COL_RNN_END + ATTN_V_AT) // kw)),
            tab_c, tab_c, tab_p, tab_p,
            pl.BlockSpec((1, D_MODEL), lambda b, n: (0, 0)),
            pl.BlockSpec((1, kw), lambda b, n: (0, 0)),
            pl.BlockSpec((1, LANES), lambda b, n: (0, 0)),
        ],
        out_specs=[pl.BlockSpec((blk, D_MODEL), lambda b, n: (row(b, n), 0)),
                   pl.BlockSpec((blk, LANES), lambda b, n: (row(b, n), 0))],
        out_shape=[jax.ShapeDtypeStruct((t, D_MODEL), BF16), jax.ShapeDtypeStruct((t, LANES), F32)],
        args=(z, z, z, z, z, cos_t, sin_t, cos_t, sin_t, q_gain_t, k_gain_t, sinks_t), name="attn_fwd",
        semantics=("parallel", "parallel"), hosted=hosted)
    return (*outs, landed) if hosted is not None else outs


def _attn_bwd(z, o, lse, do, cos_t, sin_t, q_gain_t, k_gain_t, sinks_t, *, n_seq, seq, hosted=None):
    t = n_seq * seq
    blk = WINDOW
    nb = seq // blk
    kw = KV_W
    scale = HEAD_DIM ** -0.5

    def body(qc_ref, qn_ref, kc_ref, vp_ref, vc_ref, oc_ref, on_ref, doc_ref, don_ref, lc_ref, ln_ref,
             cosc_ref, sinc_ref, cosn_ref, sinn_ref, qg_ref, kg_ref, sk_ref,
             dz_ref, vec_ref, dq_s, q_s, k_s):
        n = pl.program_id(1)
        ecq, eeq = _head_matrices(D_MODEL)
        eck, eek = _head_matrices(KV_W)
        cosc, sinc = cosc_ref[...], sinc_ref[...]
        qg, kg = qg_ref[...], kg_ref[...]
        own, other = n & 1, 1 - (n & 1)

        @pl.when(n == 0)
        def _():
            for part, value in enumerate(_normrope_fwd(qc_ref[...], qg, cosc, sinc, ecq, eeq)):
                q_s[own, part] = value
            k_s[other] = jnp.zeros((blk, kw), F32)

        for part, value in enumerate(_normrope_fwd(qn_ref[...], qg, cosn_ref[...], sinn_ref[...], ecq, eeq)):
            q_s[other, part] = value
        qhc, nqc, rsqc = q_s[own, 0], q_s[own, 1], q_s[own, 2]
        qhn = q_s[other, 0]
        khc, nkc, rskc = _normrope_fwd(kc_ref[...], kg, cosc, sinc, eck, eek)
        khp = k_s[other]
        k_s[own] = khc
        doc = doc_ref[...].astype(F32)
        don = don_ref[...].astype(F32)
        delc = _dot_split(doc * oc_ref[...].astype(F32), ecq)
        deln = _dot_split(don * on_ref[...].astype(F32), ecq)
        lc_t, ln_t, delc_t, deln_t = lc_ref[...], ln_ref[...], delc.T, deln.T
        above, causal = _window_masks(blk)
        above_c, above_n = above & (n > 0), above & (n < nb - 1)
        seg = lax.broadcasted_iota(jnp.int32, (blk, GROUP_W), 1) // HEAD_DIM
        lane = lax.broadcasted_iota(jnp.int32, (1, LANES), 1)
        sk_t = jnp.broadcast_to(sk_ref[...], (blk, LANES)).T
        dsink = jnp.zeros((1, LANES), F32)
        kcat = jnp.concatenate([khp, khc], axis=0)
        vcat = jnp.concatenate([vp_ref[...], vc_ref[...]], axis=0)
        kcat_t = kcat.T.astype(BF16)
        dkh = jnp.zeros((blk, GROUP_W), F32)
        dvh = jnp.zeros((blk, GROUP_W), F32)

        def fold_to(group, t):
            total = t + pltpu.roll(t, HEAD_DIM, 1)
            total = total + pltpu.roll(total, 2 * HEAD_DIM, 1)
            return jnp.where(seg == group, total, 0.0)

        groups = range(N_KV_HEADS)
        cols = [slice(g * GROUP_W, (g + 1) * GROUP_W) for g in groups]
        qsc, qsn = qhc * scale, qhn * scale
        qb_c = [_head_blocks(qsc[:, cols[g]]).astype(BF16) for g in groups]
        qb_n = [_head_blocks(qsn[:, cols[g]]).astype(BF16) for g in groups]
        dob_c = [_head_blocks(doc[:, cols[g]]).astype(BF16) for g in groups]
        dob_n = [_head_blocks(don[:, cols[g]]).astype(BF16) for g in groups]
        raw = []
        for g in groups:
            krep = _replicate_head(kcat, g).astype(BF16)
            vrep = _replicate_head(vcat, g).astype(BF16)
            raw.append((_dot_nt(krep, qb_c[g]), _dot_nt(vrep, dob_c[g]),
                        _dot_nt(krep[blk:], qb_n[g]), _dot_nt(vrep[blk:], dob_n[g])))
        cooked = []
        for g in groups:
            s_c, dp_c, s_n, dp_n = raw[g]
            l_row, d_row = _head_rows(lc_t, g), _head_rows(delc_t, g)
            p_c = _mask_window(jnp.exp(s_c - l_row), above_c, causal, 0.0)
            ds_c = (p_c * (dp_c - d_row)).astype(BF16)
            p_n = jnp.where(above_n, jnp.exp(s_n - _head_rows(ln_t, g)), 0.0)
            ds_n = (p_n * (dp_n - _head_rows(deln_t, g))).astype(BF16)
            cooked.append((p_c[blk:].astype(BF16), ds_c, p_n.astype(BF16), ds_n))
            p_sink = jnp.exp(_head_rows(sk_t, g) - l_row) * d_row
            for h in range(GROUP):
                dsink = dsink + jnp.where(lane == GROUP * g + h,
                                          -jnp.sum(p_sink[:, h * blk:(h + 1) * blk], axis=1, keepdims=True), 0.0)
        for g in groups:
            p_cb, ds_c, p_nb, ds_n = cooked[g]
            dq_t = jnp.dot(kcat_t[g * HEAD_DIM:(g + 1) * HEAD_DIM], ds_c, preferred_element_type=F32)
            dq_s[:, cols[g]] = _stack_heads(dq_t, blk).T * scale
            dk_rep = (jnp.dot(ds_c[blk:], qb_c[g], preferred_element_type=F32)
                      + jnp.dot(ds_n, qb_n[g], preferred_element_type=F32))
            dv_rep = (jnp.dot(p_cb, dob_c[g], preferred_element_type=F32)
                      + jnp.dot(p_nb, dob_n[g], preferred_element_type=F32))
            dkh = dkh + fold_to(g, dk_rep)
            dvh = dvh + fold_to(g, dv_rep)
        dq, dqg = _normrope_bwd(dq_s[...], nqc, rsqc, qg, cosc, sinc, ecq, eeq)
        dk, dkg = _normrope_bwd(dkh, nkc, rskc, kg, cosc, sinc, eck, eek)
        dz_ref[:, :ATTN_K_AT] = dq.astype(BF16)
        dz_ref[:, ATTN_K_AT:ATTN_V_AT] = dk.astype(BF16)
        dz_ref[:, ATTN_V_AT:] = dvh.astype(BF16)

        @pl.when(n == 0)
        def _():
            vec_ref[...] = jnp.zeros_like(vec_ref)

        vec_ref[0:1, :] += dqg
        vec_ref[1:2, 0:kw] += dkg
        vec_ref[2:3, 0:LANES] += dsink

    def row(b, n):
        return b * nb + n

    def prev(b, n):
        return b * nb + jnp.maximum(n - 1, 0)

    def nxt(b, n):
        return b * nb + jnp.minimum(n + 1, nb - 1)

    def tiles(width, col, which):
        return pl.BlockSpec((blk, width), lambda b, n: (which(b, n), col))

    def table(which):
        return pl.BlockSpec((blk, LANES), lambda b, n: (which(0, n), 0))

    outs, landed = _call(
        body,
        grid=(n_seq, nb),
        in_specs=[
            tiles(D_MODEL, COL_RNN_END // D_MODEL, row), tiles(D_MODEL, COL_RNN_END // D_MODEL, nxt),
            tiles(kw, (COL_RNN_END + ATTN_K_AT) // kw, row),
            tiles(kw, (COL_RNN_END + ATTN_V_AT) // kw, prev), tiles(kw, (COL_RNN_END + ATTN_V_AT) // kw, row),
            tiles(D_MODEL, 0, row), tiles(D_MODEL, 0, nxt),
            tiles(D_MODEL, 0, row), tiles(D_MODEL, 0, nxt),
            tiles(LANES, 0, row), tiles(LANES, 0, nxt),
            table(row), table(row), table(nxt), table(nxt),
            pl.BlockSpec((1, D_MODEL), lambda b, n: (0, 0)),
            pl.BlockSpec((1, kw), lambda b, n: (0, 0)),
            pl.BlockSpec((1, LANES), lambda b, n: (0, 0)),
        ],
        out_specs=[tiles(ATTN_W, 0, row), pl.BlockSpec((None, 8, D_MODEL), lambda b, n: (b, 0, 0))],
        out_shape=[jax.ShapeDtypeStruct((t, ATTN_W), BF16), jax.ShapeDtypeStruct((n_seq, 8, D_MODEL), F32)],
        scratch_shapes=[pltpu.VMEM((blk, D_MODEL), F32), pltpu.VMEM((2, 3, blk, D_MODEL), F32),
                        pltpu.VMEM((2, blk, kw), F32)],
        args=(z, z, z, z, z, o, o, do, do, lse, lse, cos_t, sin_t, cos_t, sin_t,
              q_gain_t, k_gain_t, sinks_t), name="attn_bwd", semantics=("arbitrary", "arbitrary"), hosted=hosted)
    return (*outs, landed) if hosted is not None else outs


def _rope_tables(seq):
    inv = ROPE_THETA ** (-jnp.arange(0, HEAD_DIM, 2, dtype=F32) / HEAD_DIM)
    ang = jnp.arange(seq, dtype=F32)[:, None] * inv[None, :]
    cos, sin = jnp.cos(ang), jnp.sin(ang)
    return jnp.tile(jnp.concatenate([cos, cos], axis=1), (1, 2)), jnp.tile(jnp.concatenate([-sin, sin], axis=1), (1, 2))


def _block_diag_tiles(w):
    per = RNN_TILE // RNN_BLOCK_W
    w4 = w.reshape(D_MODEL // RNN_TILE, per, RNN_BLOCK_W, RNN_BLOCK_W)
    eye = jnp.eye(per, dtype=w.dtype)
    dense = jnp.einsum("tpij,pq->tpiqj", w4, eye)
    return dense.reshape(D_MODEL // RNN_TILE, RNN_TILE, RNN_TILE).astype(BF16)


def _block_diag_extract(dense):
    per = RNN_TILE // RNN_BLOCK_W
    d5 = dense.reshape(D_MODEL // RNN_TILE, per, RNN_BLOCK_W, per, RNN_BLOCK_W)
    blocks = jnp.stack([d5[:, p, :, p, :] for p in range(per)], axis=1)
    return blocks.reshape(D_MODEL // RNN_BLOCK_W, RNN_BLOCK_W, RNN_BLOCK_W)


def _local_step(x, p, target, w, *, n_seq, seq, comm=None):
    w = dict(w)

    def run(tag, fn, *args, **kwargs):
        hosted = comm.host(tag) if comm is not None else None
        if hosted is None:
            return fn(*args, **kwargs)
        *outs, landed = fn(*args, hosted=hosted, **kwargs)
        comm.landed(tag, landed, w)
        return outs[0] if len(outs) == 1 else outs

    def ready(batch, grads, extra=None):
        if comm is not None:
            comm.ready(batch, grads, extra)

    cos_t, sin_t = _rope_tables(seq)
    q_gain_t = jnp.tile(w["q_gain"], (1, N_Q_HEADS))
    k_gain_t = jnp.tile(w["k_gain"], (1, N_KV_HEADS))
    sinks_t = jnp.pad(w["sinks"], ((0, 0), (0, LANES - N_Q_HEADS)))
    wrg_bd, wig_bd = _block_diag_tiles(w["w_rg"]), _block_diag_tiles(w["w_ig"])
    dims = dict(n_seq=n_seq, seq=seq)

    z, h = run("mm_in", _norm_matmul, x, w["g_mix"], w["w_in"], tm=1024, tn=IN_TOTAL // 4, name="mm_in")
    gate_tile = 512
    ga_at, gb_at = COL_ATTN_END // gate_tile, (COL_ATTN_END + D_MODEL) // gate_tile
    xc, hr, ya_in = run("rnn_fwd", _rnn_fwd, z, w["conv_w"], w["conv_b"], wrg_bd, w["b_rg"], wig_bd, w["b_ig"],
                        w["lru_lambda"], **dims)
    o, lse = run("attn_fwd", _attn_fwd, z, cos_t, sin_t, q_gain_t, k_gain_t, sinks_t, **dims)
    ya = run("mm_rnn_proj", _matmul, ya_in, w["w_rnn_proj"], mode="nn", tm=1024, tn=1024, out_dtypes=[F32],
             name="mm_rnn_proj")
    yb, merged = _matmul(
        o, w["w_attn_proj"], mode="nn", tm=1024, tn=gate_tile, out_dtypes=[F32, BF16], name="mm_attn_proj",
        epilogue=lambda acc, ga, gb, yav: (acc, _sig(ga) * yav + _sig(gb) * acc),
        extras=(z, z, ya), extra_col_blocks=(ga_at, gb_at, 0))
    def residual_then_norm(acc, res, gain):
        new = res + acc
        return new, _rmsnorm_rows(new, gain)

    x1, hm = _matmul(merged, w["w_out"], mode="nn", tm=512, tn=1024, out_dtypes=[F32, BF16], name="mm_out",
                     epilogue=residual_then_norm, extras=(x,), row_vecs=(w["g_mlp"],))
    act = _matmul(hm, w["w_up"], mode="nn", tm=1024, tn=1024, out_dtypes=[BF16], name="mm_up",
                  epilogue=lambda acc: (jnp.square(jnp.maximum(acc, 0.0)),))
    x2, hp = _matmul(act, w["w_down"], mode="nn", tm=512, tn=1024, out_dtypes=[F32, BF16], name="mm_down",
                     epilogue=residual_then_norm, extras=(x1,), row_vecs=(w["g_ple"],))
    p_bf = p.astype(BF16)
    e = _matmul(p_bf, w["w_ple_proj"], mode="nn", tm=1024, tn=1024, out_dtypes=[F32], name="mm_ple_proj")

    def loss_head(gt, x2v, ev, tgt):
        sg = _sig(gt)
        diff = x2v + ev * sg - tgt
        dx = diff * (1.0 / D_MODEL)
        return dx, dx * ev * sg * (1.0 - sg), dx * sg, jnp.sum(diff * diff, axis=0, keepdims=True)

    dx3, dgt, de, loss_row = _matmul(hp, w["w_ple_gate"], mode="nn", tm=512, tn=1024, out_dtypes=[F32, BF16, BF16],
                                     name="mm_ple_gate", epilogue=loss_head, extras=(x2, e, target), n_row_sums=1)

    g = {}
    g["w_ple_proj"] = _matmul_tn(p_bf, de, tk=PLE_DIM, tn=1024, tt=1024, name="mm_d_ple_proj",
                                 slot_cols=D_MODEL // N_DEV)
    g["w_ple_gate"] = _matmul_tn(hp, dgt, tk=1024, tn=1024, tt=1024, name="mm_d_ple_gate")
    def through_norm(dy, xv, dres, gain):
        dx, dgain = _rmsnorm_bwd_rows(dy, xv, dres, gain)
        return dx, dx, dgain

    dx2, dx2_bf, g["g_ple"] = _matmul(
        dgt, w["w_ple_gate"], mode="nt", tm=512, tn=1024, out_dtypes=[F32, BF16], name="mm_dhp",
        epilogue=through_norm, extras=(x2, dx3), row_vecs=(w["g_ple"],), n_row_sums=1)
    g["w_down"] = _matmul_tn(act, dx2_bf, tk=1024, tn=1024, tt=1024, name="mm_d_down")

    def relu_grad(dact, a):
        a = a.astype(F32)
        return (dact * (2.0 * jnp.where(a > 0.0, a * lax.rsqrt(a), 0.0)),)

    du = _matmul(dx2_bf, w["w_down"], mode="nt", tm=1024, tn=1024, out_dtypes=[BF16], name="mm_dact",
                 epilogue=relu_grad, extras=(act,))
    g["w_up"] = _matmul_tn(hm, du, tk=1024, tn=1024, tt=1024, name="mm_d_up", slot_cols=D_FF // N_DEV)
    ready(1, g)
    dx1, dx1_bf, g["g_mlp"] = run(
        "mm_dhm", _matmul, du, w["w_up"], mode="nt", tm=512, tn=1024, out_dtypes=[F32, BF16], name="mm_dhm",
        epilogue=through_norm, extras=(x1, dx2), row_vecs=(w["g_mlp"],), n_row_sums=1)
    g["w_out"] = _matmul_tn(merged, dx1_bf, tk=1024, tn=1024, tt=1024, name="mm_d_out")
    def merge_bwd(dm, ga, gb, yav, ybv):
        sa, sb = _sig(ga), _sig(gb)
        return dm * sa, dm * sb, dm * yav * sa * (1.0 - sa), dm * ybv * sb * (1.0 - sb)

    dya, dyb, dga, dgb = _matmul(dx1_bf, w["w_out"], mode="nt", tm=1024, tn=gate_tile, out_dtypes=[BF16] * 4,
                                 name="mm_dmerged", epilogue=merge_bwd, extras=(z, z, ya, yb),
                                 extra_col_blocks=(ga_at, gb_at, 0, 0))
    g["w_rnn_proj"] = _matmul_tn(ya_in, dya, tk=1024, tn=1024, tt=1024, name="mm_d_rnn_proj")
    g["w_attn_proj"] = _matmul_tn(o, dyb, tk=1024, tn=1024, tt=1024, name="mm_d_attn_proj")
    ready(2, g)
    dya_in = run("mm_dya_in", _matmul, dya, w["w_rnn_proj"], mode="nt", tm=1024, tn=1024, out_dtypes=[F32],
                 name="mm_dya_in")
    do = _matmul(dyb, w["w_attn_proj"], mode="nt", tm=1024, tn=1024, out_dtypes=[BF16], name="mm_do")
    dx_rnn, dg_rnn, dwrg_dense, dwig_dense, rnn_vec = run(
        "rnn_bwd", _rnn_bwd, dya_in, z, xc, hr, w["conv_w"], wrg_bd, w["b_rg"], wig_bd, w["b_ig"],
        w["lru_lambda"], **dims)
    dz_attn, attn_vec = run("attn_bwd", _attn_bwd, z, o, lse, do, cos_t, sin_t, q_gain_t, k_gain_t, sinks_t,
                            **dims)
    dz_parts = (dx_rnn, dg_rnn, dz_attn, dga, dgb)
    g["w_rg"] = _block_diag_extract(dwrg_dense)
    g["w_ig"] = _block_diag_extract(dwig_dense)
    g["b_rg"], g["b_ig"], g["lru_lambda"], g["conv_b"] = (rnn_vec[i:i + 1] for i in range(4))
    g["conv_w"] = rnn_vec[4:8]
    attn_vec = attn_vec[0] if n_seq == 1 else functools.reduce(jnp.add, [attn_vec[b] for b in range(n_seq)])
    g["q_gain"] = attn_vec[0].reshape(N_Q_HEADS, HEAD_DIM).sum(axis=0)[None, :]
    g["k_gain"] = attn_vec[1, :KV_W].reshape(N_KV_HEADS, HEAD_DIM).sum(axis=0)[None, :]
    g["sinks"] = attn_vec[2:3, :N_Q_HEADS]
    ready(SMALL_BATCH, g, {LOSS_ROW: loss_row})
    g["w_in"] = jnp.concatenate(
        list(run("mm_d_in_rnn", _matmul_tn_multi, h, dz_parts[:2], tt=1024, name="mm_d_in_rnn"))
        + list(run("mm_d_in_rest", _matmul_tn_multi, h, dz_parts[2:], tt=512, name="mm_d_in_rest")), axis=1)
    ready(3, g)
    w_in_attn, w_in_gate = w["w_in"][:, COL_RNN_END:COL_ATTN_END], w["w_in"][:, COL_ATTN_END:]
    windows = ((w["w_in"], (0, D_MODEL)), (w["w_in"], (D_MODEL, D_MODEL)), (w_in_attn, (0, ATTN_W)),
               (w_in_gate, (0, D_MODEL)), (w_in_gate, (D_MODEL, D_MODEL)))
    grad_x, g["g_mix"] = run(
        "mm_dh", _matmul, dz_parts, [wd[0] for wd in windows], mode="nt", tm=256, tn=1024, out_dtypes=[F32],
        name="mm_dh", b_cols=[wd[1] for wd in windows], epilogue=_rmsnorm_bwd_rows, extras=(x, dx1),
        row_vecs=(w["g_mix"],), n_row_sums=1)
    return jnp.sum(loss_row), grad_x, g


MESH_ID = pl.DeviceIdType.MESH


def _coords(index):
    return (index >> 2) & 1, (index >> 1) & 1, index & 1


def _exchange(srcs, kinds, *, name):
    n = len(srcs)
    n_peer = N_DEV - 1

    def body(*refs):
        src, dst = refs[:n], refs[n:2 * n]
        send_sems, recv_sems, local_sems = refs[2 * n:]
        me = 4 * lax.axis_index("x") + 2 * lax.axis_index("y") + lax.axis_index("c")

        def remote(i, d):
            peer = (me + d) & (N_DEV - 1)
            piece = src[i] if kinds[i] == "gather" else src[i].at[peer]
            return pltpu.make_async_remote_copy(
                src_ref=piece, dst_ref=dst[i].at[me], send_sem=send_sems.at[i * n_peer + d - 1],
                recv_sem=recv_sems.at[i * n_peer + d - 1], device_id=_coords(peer), device_id_type=MESH_ID)

        def arrival(i, d):
            sender = (me - d) & (N_DEV - 1)
            piece = src[i] if kinds[i] == "gather" else src[i].at[sender]
            return pltpu.make_async_remote_copy(
                src_ref=piece, dst_ref=dst[i].at[sender], send_sem=send_sems.at[i * n_peer + d - 1],
                recv_sem=recv_sems.at[i * n_peer + d - 1], device_id=_coords(sender), device_id_type=MESH_ID)

        own = []
        for i in range(n):
            piece = src[i] if kinds[i] == "gather" else src[i].at[me]
            own.append(pltpu.make_async_copy(piece, dst[i].at[me], local_sems.at[i]))
            own[-1].start()
        sent = [remote(i, d) for d in range(1, N_DEV) for i in range(n)]
        for cp in sent:
            cp.start()
        for d in range(1, N_DEV):
            for i in range(n):
                arrival(i, d).wait_recv()
        for cp in sent:
            cp.wait_send()
        for cp in own:
            cp.wait()

    def out_of(s, kind):
        shape = s.shape if kind == "scatter" else (N_DEV,) + s.shape
        return jax.ShapeDtypeStruct(shape, s.dtype)

    any_spec = pl.BlockSpec(memory_space=pl.ANY)
    return pl.pallas_call(
        body,
        in_specs=[any_spec] * n,
        out_specs=[any_spec] * n,
        out_shape=[out_of(s, k) for s, k in zip(srcs, kinds)],
        scratch_shapes=[pltpu.SemaphoreType.DMA((n * n_peer,)), pltpu.SemaphoreType.DMA((n * n_peer,)),
                        pltpu.SemaphoreType.DMA((n,))],
        compiler_params=pltpu.CompilerParams(has_side_effects=True),
        name=name,
    )(*srcs)


def _remote(src, dst, send_sem, recv_sem, to):
    return pltpu.make_async_remote_copy(src_ref=src, dst_ref=dst, send_sem=send_sem, recv_sem=recv_sem,
                                        device_id=to, device_id_type=MESH_ID)


GATHER_PIECES = 4


def _gather_two_level(shards, *, name):
    n = len(shards)
    per = N_DEV - 1
    pieces = []
    for i, s in enumerate(shards):
        n_rows = s.shape[0]
        count = GATHER_PIECES if n_rows % (GATHER_PIECES * LANES) == 0 else 1
        pieces += [(i, r * (n_rows // count), n_rows // count) for r in range(count)]

    def body(*refs):
        src, dst = refs[:n], refs[n:2 * n]
        send_sems, recv_sems, local_sems = refs[2 * n:]
        x, y, c = lax.axis_index("x"), lax.axis_index("y"), lax.axis_index("c")
        me, sibling = (x, y, c), (x, y, 1 - c)
        chips = [(1 - x, y), (x, 1 - y), (1 - x, 1 - y)]

        def slot(pos):
            return 4 * pos[0] + 2 * pos[1] + pos[2]

        def copy(p, k, block, to, from_shard=False):
            i, first_row, rows = pieces[p]
            landed = dst[i].at[slot(block), pl.ds(first_row, rows)]
            source = src[i].at[pl.ds(first_row, rows)] if from_shard else landed
            return _remote(source, landed, send_sems.at[p * per + k], recv_sems.at[p * per + k], to)

        mine = [pltpu.make_async_copy(src[i], dst[i].at[slot(me)], local_sems.at[i]) for i in range(n)]
        for cp in mine:
            cp.start()
        first = []
        for p in range(len(pieces)):
            first.append(copy(p, 0, me, sibling, from_shard=True))
            first += [copy(p, 1 + j, me, (*chip, c), from_shard=True) for j, chip in enumerate(chips)]
        for cp in first:
            cp.start()
        passed = []
        for p in range(len(pieces)):
            for j, chip in enumerate(chips):
                copy(p, 1 + j, (*chip, c), me).wait_recv()
                passed.append(copy(p, 4 + j, (*chip, c), sibling))
                passed[-1].start()
        for p in range(len(pieces)):
            copy(p, 0, sibling, me).wait_recv()
            for j, chip in enumerate(chips):
                copy(p, 4 + j, (*chip, 1 - c), me).wait_recv()
        for cp in first + passed:
            cp.wait_send()
        for cp in mine:
            cp.wait()

    any_spec = pl.BlockSpec(memory_space=pl.ANY)
    return pl.pallas_call(
        body,
        in_specs=[any_spec] * n,
        out_specs=[any_spec] * n,
        out_shape=[jax.ShapeDtypeStruct((N_DEV,) + s.shape, s.dtype) for s in shards],
        scratch_shapes=[pltpu.SemaphoreType.DMA((len(pieces) * per,)), pltpu.SemaphoreType.DMA((len(pieces) * per,)),
                        pltpu.SemaphoreType.DMA((n,))],
        name=name,
    )(*shards)


CHIPS = N_DEV // 2


def _other_chips(x, y):
    return [(x, 1 - y), (1 - x, y), (1 - x, 1 - y)]


def _hosted_gather_first(shards):
    n = len(shards)
    per = CHIPS

    def plan(src, dst, send_sems, recv_sems, local_sems, first_sem):
        x, y, c = lax.axis_index("x"), lax.axis_index("y"), lax.axis_index("c")
        peers = [(x, y, 1 - c)] + [(*chip, c) for chip in _other_chips(x, y)]
        copies = []
        for i in range(n):
            own = pltpu.make_async_copy(src[i], dst[i].at[4 * x + 2 * y + c], local_sems.at[first_sem + i])
            copies.append(_Xfer(own.start, own.wait))
        for j, peer in enumerate(peers):
            for i in range(n):
                k = first_sem + i * per + j
                out = _remote(src[i], dst[i].at[4 * x + 2 * y + c], send_sems.at[k], recv_sems.at[k], peer)
                arrival = _remote(src[i], dst[i].at[4 * peer[0] + 2 * peer[1] + peer[2]], send_sems.at[k],
                                  recv_sems.at[k], peer)

                def wait(out=out, arrival=arrival):
                    arrival.wait_recv()
                    out.wait_send()

                copies.append(_Xfer(out.start, wait))
        return copies

    out_shape = tuple(jax.ShapeDtypeStruct((N_DEV,) + s.shape, s.dtype) for s in shards)
    return _Hosted(tuple(shards), out_shape, n * per, plan)


def _hosted_gather_second(landed):
    n = len(landed)
    per = CHIPS - 1

    def plan(src, dst, send_sems, recv_sems, local_sems, first_sem):
        x, y, c = lax.axis_index("x"), lax.axis_index("y"), lax.axis_index("c")
        copies = []
        for j, chip in enumerate(_other_chips(x, y)):
            mine, theirs = 4 * chip[0] + 2 * chip[1] + c, 4 * chip[0] + 2 * chip[1] + 1 - c
            for i in range(n):
                k = first_sem + i * per + j
                out = _remote(src[i].at[mine], dst[i].at[mine], send_sems.at[k], recv_sems.at[k], (x, y, 1 - c))
                arrival = _remote(src[i].at[theirs], dst[i].at[theirs], send_sems.at[k], recv_sems.at[k],
                                  (x, y, 1 - c))

                def wait(out=out, arrival=arrival):
                    arrival.wait_recv()
                    out.wait_send()

                copies.append(_Xfer(out.start, wait))
        return copies

    out_shape = tuple(jax.ShapeDtypeStruct(a.shape, a.dtype) for a in landed)
    return _Hosted(tuple(landed), out_shape, n * per, plan, tuple((i, i) for i in range(n)))


def _hosted_sibling_swap(arrays, sliced):
    n_sems = sum(CHIPS if s else 1 for s in sliced)

    def plan(src, dst, send_sems, recv_sems, local_sems, first_sem):
        x, y, c = lax.axis_index("x"), lax.axis_index("y"), lax.axis_index("c")
        sibling = (x, y, 1 - c)
        copies, k = [], first_sem
        for i, is_sliced in enumerate(sliced):
            pieces = [(src[i].at[2 * s + 1 - c], dst[i].at[s]) for s in range(CHIPS)] if is_sliced else [(src[i], dst[i])]
            for source, target in pieces:
                cp = _remote(source, target, send_sems.at[k], recv_sems.at[k], sibling)
                copies.append(_Xfer(cp.start, cp.wait))
                k += 1
        return copies

    out_shape = tuple(jax.ShapeDtypeStruct((CHIPS,) + a.shape[1:] if s else a.shape, a.dtype)
                      for a, s in zip(arrays, sliced))
    return _Hosted(tuple(arrays), out_shape, n_sems, plan)


def _hosted_chip_exchange(arrays, sliced):
    n = len(arrays)
    per = CHIPS - 1

    def plan(src, dst, send_sems, recv_sems, local_sems, first_sem):
        x, y, c = lax.axis_index("x"), lax.axis_index("y"), lax.axis_index("c")
        chip = 2 * x + y
        copies = []
        for d in (3, 1, 2):
            other = chip ^ d
            to = ((other >> 1) & 1, other & 1, c)
            for i in range(n):
                k = first_sem + i * per + d - 1
                source = src[i].at[other] if sliced[i] else src[i]
                out = _remote(source, dst[i].at[chip], send_sems.at[k], recv_sems.at[k], to)
                arrival = _remote(source, dst[i].at[other], send_sems.at[k], recv_sems.at[k], to)

                def wait(out=out, arrival=arrival):
                    arrival.wait_recv()
                    out.wait_send()

                copies.append(_Xfer(out.start, wait))
        for i in range(n):
            own = pltpu.make_async_copy(src[i].at[chip] if sliced[i] else src[i], dst[i].at[chip],
                                        local_sems.at[first_sem + i])
            copies.append(_Xfer(own.start, own.wait))
        return copies

    out_shape = tuple(jax.ShapeDtypeStruct(a.shape if s else (CHIPS,) + a.shape, a.dtype)
                      for a, s in zip(arrays, sliced))
    return _Hosted(tuple(arrays), out_shape, n * per, plan)


def _add_sibling(parts, received, core, *, name):
    _, r, cols = parts.shape
    tr = min(1024, r)

    def body(core_ref, a_ref, b_ref, o_ref):
        o_ref[...] = (a_ref[...] + b_ref[...]).astype(BF16)

    grid_spec = pltpu.PrefetchScalarGridSpec(
        num_scalar_prefetch=1,
        grid=(CHIPS, r // tr),
        in_specs=[pl.BlockSpec((None, tr, cols), lambda k, i, core_ref: (2 * k + core_ref[0], i, 0)),
                  pl.BlockSpec((None, tr, cols), lambda k, i, core_ref: (k, i, 0))],
        out_specs=pl.BlockSpec((None, tr, cols), lambda k, i, core_ref: (k, i, 0)),
    )
    return pl.pallas_call(body, grid_spec=grid_spec, out_shape=jax.ShapeDtypeStruct((CHIPS, r, cols), BF16),
                          compiler_params=_params("parallel", "parallel"), name=name)(core, parts, received)


def _add_whole(a, b, *, name):
    def body(a_ref, b_ref, o_ref):
        o_ref[...] = a_ref[...] + b_ref[...]

    return pl.pallas_call(body, out_shape=jax.ShapeDtypeStruct(a.shape, F32), name=name)(a, b)


def _adamw(parts, w, m, v, *, name):
    r, c = w.shape
    n_parts = parts.shape[0]
    tr = min(512, r)
    c1 = 1.0 - ADAM_B1 ** ADAM_STEP
    c2 = 1.0 - ADAM_B2 ** ADAM_STEP

    def body(p_ref, w_ref, m_ref, v_ref, g_ref, d_ref, nm_ref, nv_ref):
        g = p_ref[0].astype(F32)
        for s in range(1, n_parts):
            g = g + p_ref[s].astype(F32)
        nm = ADAM_B1 * m_ref[...] + (1.0 - ADAM_B1) * g
        nv = ADAM_B2 * v_ref[...] + (1.0 - ADAM_B2) * (g * g)
        g_ref[...] = g
        nm_ref[...] = nm
        nv_ref[...] = nv
        d_ref[...] = -ADAM_LR * ((nm / c1) / (jnp.sqrt(nv / c2) + ADAM_EPS) + ADAM_WD * w_ref[...])

    tile = pl.BlockSpec((tr, c), lambda i: (i, 0))
    return pl.pallas_call(
        body,
        grid=(r // tr,),
        in_specs=[pl.BlockSpec((n_parts, tr, c), lambda i: (0, i, 0)), tile, tile, tile],
        out_specs=[tile] * 4,
        out_shape=[jax.ShapeDtypeStruct((r, c), F32)] * 4,
        compiler_params=_params("parallel"),
        name=name,
    )(parts, w, m, v)


BIG = ("w_in", "w_rnn_proj", "w_attn_proj", "w_out", "w_up", "w_down", "w_ple_gate", "w_ple_proj")
LOSS_ROW = "loss"
SMALL = (("conv_b", 1), ("b_rg", 1), ("b_ig", 1), ("lru_lambda", 1), ("g_mlp", 1), ("g_ple", 1),
         ("q_gain", 1), ("k_gain", 1), ("sinks", 1), (LOSS_ROW, 1), ("w_rg", 64), ("w_ig", 64))
SMALL_ROWS = 144
COL_SHARDED = ("w_in", "w_up", "w_ple_proj")
BATCHES = {1: ("w_ple_proj", "w_ple_gate", "w_down", "w_up"), 2: ("w_out", "w_rnn_proj", "w_attn_proj"),
           3: ("w_in", "conv_w")}
SMALL_BATCH = 4


def _pack_small(vals):
    rows = []
    for nm, nrow in SMALL:
        flat = vals[nm].reshape(-1).astype(F32)
        rows.append(jnp.pad(flat, (0, nrow * D_MODEL - flat.shape[0])).reshape(nrow, D_MODEL))
    used = sum(nrow for _, nrow in SMALL)
    rows.append(jnp.zeros((SMALL_ROWS - used, D_MODEL), F32))
    return jnp.concatenate(rows, axis=0)


def _unpack_small(packed, shapes):
    out, at = {}, 0
    for nm, nrow in SMALL:
        size = 1
        for s in shapes[nm]:
            size *= s
        out[nm] = packed[at:at + nrow].reshape(-1)[:size].reshape(shapes[nm])
        at += nrow
    return out


def _full_weight(name, landed):
    if name in COL_SHARDED:
        return landed.transpose(1, 0, 2).reshape(landed.shape[1], N_DEV * landed.shape[2])
    return landed.reshape(N_DEV * landed.shape[1], landed.shape[2])


def _owner_slots(name, grad):
    if name == "w_in":
        return grad.reshape(D_MODEL, N_DEV, IN_TOTAL // N_DEV).transpose(1, 0, 2)
    if name == "conv_w":
        return grad.reshape(CONV_W, N_DEV, D_MODEL // N_DEV).transpose(1, 0, 2)
    if name in COL_SHARDED:
        return grad
    return grad.reshape(N_DEV, grad.shape[0] // N_DEV, grad.shape[1])


class _StepExchanges:
    FIRST, SECOND = "first", "second"
    PROJ, OUT, PLE_GATE, UP, DOWN = (("w_rnn_proj", "w_attn_proj"), ("w_out",), ("w_ple_gate",), ("w_up",),
                                     ("w_down", "w_ple_proj"))
    GATHERS = {"mm_in": ((FIRST, PROJ), (FIRST, OUT), (FIRST, PLE_GATE)),
               "rnn_fwd": ((SECOND, PROJ), (SECOND, OUT), (SECOND, PLE_GATE), (FIRST, UP)),
               "attn_fwd": ((SECOND, UP), (FIRST, DOWN)), "mm_rnn_proj": ((SECOND, DOWN),)}
    SWAPS = {"mm_dhm": 1, "mm_dya_in": 2, "mm_d_in_rnn": SMALL_BATCH}
    CHIP_EXCHANGES = {"rnn_bwd": ((1, (0, 1, 2)),), "attn_bwd": ((1, (3,)), (2, None)),
                      "mm_d_in_rest": ((SMALL_BATCH, None),), "mm_dh": ((3, None),)}

    def __init__(self, shards, core):
        self.shards = shards
        self.core = core
        self.parts, self.swapped, self.summed, self.half_gathered = {}, {}, {}, {}

    def ready(self, batch, grads, extra=None):
        if batch == SMALL_BATCH:
            self.parts[batch] = ([_pack_small({**grads, **extra})], [False])
            return
        arrays = [_owner_slots(nm, grads[nm]) for nm in BATCHES[batch]]
        self.parts[batch] = (arrays, [True] * len(arrays))
        if batch not in self.SWAPS.values():
            _, self.swapped[batch] = _call(
                lambda: None, grid=(1,), in_specs=[], out_specs=[], out_shape=[], args=(), name="swap_last",
                semantics=("arbitrary",), hosted=_hosted_sibling_swap(*self.parts[batch]))

    def host(self, tag):
        if tag in self.GATHERS:
            return _merge_hosted([
                _hosted_gather_first([self.shards[nm] for nm in group]) if half == self.FIRST
                else _hosted_gather_second([self.half_gathered[nm] for nm in group])
                for half, group in self.GATHERS[tag]])
        if tag in self.SWAPS:
            return _hosted_sibling_swap(*self.parts[self.SWAPS[tag]])
        if tag in self.CHIP_EXCHANGES:
            hosted = []
            for batch, members in self._exchange_members(tag):
                arrays, sliced = self.parts[batch]
                labels = BATCHES.get(batch, ("small",))
                sums = [_add_sibling(arrays[i], self.swapped[batch][i], self.core, name="add_" + labels[i])
                        if sliced[i] else _add_whole(arrays[i], self.swapped[batch][i], name="add_" + labels[i])
                        for i in members]
                hosted.append(_hosted_chip_exchange(sums, [sliced[i] for i in members]))
            return _merge_hosted(hosted)
        return None

    def _exchange_members(self, tag):
        return [(batch, members if members is not None else tuple(range(len(self.parts[batch][0]))))
                for batch, members in self.CHIP_EXCHANGES[tag]]

    def landed(self, tag, landed, weights):
        if tag in self.GATHERS:
            names = [(half, nm) for half, group in self.GATHERS[tag] for nm in group]
            for (half, nm), buf in zip(names, landed):
                if half == self.FIRST:
                    self.half_gathered[nm] = buf
                else:
                    weights[nm] = _full_weight(nm, buf)
        elif tag in self.SWAPS:
            self.swapped[self.SWAPS[tag]] = landed
        else:
            at = 0
            for batch, members in self._exchange_members(tag):
                for i in members:
                    self.summed.setdefault(batch, {})[i] = landed[at]
                    at += 1


def kernel(x, p, g_mix, w_in, conv_w, conv_b, w_rg, b_rg, w_ig, b_ig, lru_lambda, w_rnn_proj, q_gain, k_gain, sinks, w_attn_proj, w_out, g_mlp, w_up, w_down, g_ple, w_ple_gate, w_ple_proj, loss_target, m_g_mix, m_w_in, m_conv_w, m_conv_b, m_w_rg, m_b_rg, m_w_ig, m_b_ig, m_lru_lambda, m_w_rnn_proj, m_q_gain, m_k_gain, m_sinks, m_w_attn_proj, m_w_out, m_g_mlp, m_w_up, m_w_down, m_g_ple, m_w_ple_gate, m_w_ple_proj, v_g_mix, v_w_in, v_conv_w, v_conv_b, v_w_rg, v_b_rg, v_w_ig, v_b_ig, v_lru_lambda, v_w_rnn_proj, v_q_gain, v_k_gain, v_sinks, v_w_attn_proj, v_w_out, v_g_mlp, v_w_up, v_w_down, v_g_ple, v_w_ple_gate, v_w_ple_proj):
    names = ("g_mix", "w_in", "conv_w", "conv_b", "w_rg", "b_rg", "w_ig", "b_ig", "lru_lambda", "w_rnn_proj",
             "q_gain", "k_gain", "sinks", "w_attn_proj", "w_out", "g_mlp", "w_up", "w_down", "g_ple",
             "w_ple_gate", "w_ple_proj")
    wts = dict(zip(names, (g_mix, w_in, conv_w, conv_b, w_rg, b_rg, w_ig, b_ig, lru_lambda, w_rnn_proj, q_gain,
                           k_gain, sinks, w_attn_proj, w_out, g_mlp, w_up, w_down, g_ple, w_ple_gate, w_ple_proj)))
    mom1 = dict(zip(names, (m_g_mix, m_w_in, m_conv_w, m_conv_b, m_w_rg, m_b_rg, m_w_ig, m_b_ig, m_lru_lambda,
                            m_w_rnn_proj, m_q_gain, m_k_gain, m_sinks, m_w_attn_proj, m_w_out, m_g_mlp, m_w_up,
                            m_w_down, m_g_ple, m_w_ple_gate, m_w_ple_proj)))
    mom2 = dict(zip(names, (v_g_mix, v_w_in, v_conv_w, v_conv_b, v_w_rg, v_b_rg, v_w_ig, v_b_ig, v_lru_lambda,
                            v_w_rnn_proj, v_q_gain, v_k_gain, v_sinks, v_w_attn_proj, v_w_out, v_g_mlp, v_w_up,
                            v_w_down, v_g_ple, v_w_ple_gate, v_w_ple_proj)))
    n_seq, seq, _ = x.shape
    core = lax.axis_index("c").astype(jnp.int32).reshape(1)

    shards = {nm: wts[nm][0].astype(BF16) for nm in BIG}
    w_in_all, conv_all = _gather_two_level([shards["w_in"], conv_w[0]], name="gather_w_in")
    w = {nm: wts[nm] for nm in names if nm not in BIG}
    w["w_rg"], w["w_ig"] = w_rg[0], w_ig[0]
    w["conv_w"] = conv_all.transpose(1, 0, 2).reshape(CONV_W, D_MODEL)
    w["w_in"] = _full_weight("w_in", w_in_all)
    comm = _StepExchanges(shards, core)
    loss_sum, grad_x, g = _local_step(
        x.reshape(n_seq * seq, D_MODEL), p.reshape(n_seq * seq, PLE_DIM), loss_target.reshape(n_seq * seq, D_MODEL),
        w, n_seq=n_seq, seq=seq, comm=comm)
    del loss_sum

    res = {}
    for batch, batch_names in BATCHES.items():
        for i, nm in enumerate(batch_names):
            res[nm] = _adamw(comm.summed[batch][i], wts[nm][0], mom1[nm][0], mom2[nm][0], name="adamw_" + nm)
    g_mix_parts, = _exchange([g["g_mix"]], ["gather"], name="gather_g_mix")
    res["g_mix"] = [r[0] for r in _adamw(g_mix_parts, g_mix, m_g_mix, v_g_mix, name="adamw_g_mix")]
    small_names = [nm for nm, _ in SMALL if nm != LOSS_ROW]
    full_small = {}
    for src, key in ((wts, "w"), (mom1, "m"), (mom2, "v")):
        vals = {nm: src[nm][0] for nm in small_names}
        vals[LOSS_ROW] = jnp.zeros((1,), F32)
        full_small[key] = _pack_small(vals)
    small_res = _adamw(comm.summed[SMALL_BATCH][0],full_small["w"], full_small["m"], full_small["v"], name="adamw_small")
    shapes = {nm: wts[nm].shape[1:] for nm in small_names}
    shapes[LOSS_ROW] = (D_MODEL,)
    small_out = [_unpack_small(r, shapes) for r in small_res]
    for nm in small_names:
        res[nm] = [so[nm] for so in small_out]
    loss = jnp.sum(small_out[0][LOSS_ROW]) * (0.5 / D_MODEL)

    outs = [loss, grad_x.reshape(n_seq, seq, D_MODEL)]
    for k in range(4):
        outs.extend(res[nm][k][None] for nm in names)
    return tuple(outs)
```
